```python
import jax, jax.numpy as jnp
from jax import lax
import numpy as np

D_MODEL = 2048
BATCH = 8
SEQ = 2048
DEPTH = 1

MEM_LEN = 256
MIX_WIDTH = D_MODEL
A_WIDTH = MIX_WIDTH // 2
A_HEAD_DIM = 128
A_HEADS = A_WIDTH // A_HEAD_DIM
CONV_K = 5
CHUNK = 64
B_WIDTH = MIX_WIDTH // 4
B_HEAD_DIM = 64
B_Q_HEADS = B_WIDTH // B_HEAD_DIM
B_KV_HEADS = B_Q_HEADS // 4
WINDOW = 128
C_WIDTH = MIX_WIDTH // 4
C_HEADS = 4
C_HEAD_DIM = C_WIDTH // C_HEADS

ROPE_THETA = 10000.0
EPS = 1e-6
IN_SIZES = (A_WIDTH, A_WIDTH, A_WIDTH, A_WIDTH, A_HEADS, A_HEADS, A_HEADS, A_HEADS,
            B_WIDTH, B_KV_HEADS * B_HEAD_DIM, B_KV_HEADS * B_HEAD_DIM, B_WIDTH,
            C_WIDTH, C_WIDTH)
IN_WIDTH = sum(IN_SIZES)

kernel_name = "hybrid_deltanet_swa_sink_memxattn_layer"

F32 = jnp.float32


def rms_norm(t, w):
    tf = t.astype(F32)
    y = tf * lax.rsqrt(jnp.mean(tf * tf, axis=-1, keepdims=True) + EPS)
    return (y * w.astype(F32)).astype(t.dtype)


def l2_norm(t):
    tf = t.astype(F32)
    return tf * lax.rsqrt(jnp.sum(tf * tf, axis=-1, keepdims=True) + EPS)


def rope(t):
    seq, d = t.shape[1], t.shape[-1]
    inv = ROPE_THETA ** (-jnp.arange(0, d, 2, dtype=F32) / d)
    ang = jnp.arange(seq, dtype=F32)[:, None] * inv[None, :]
    cos = jnp.cos(ang)[None, :, None, :]
    sin = jnp.sin(ang)[None, :, None, :]
    tf = t.astype(F32)
    t1, t2 = tf[..., : d // 2], tf[..., d // 2:]
    return jnp.concatenate([t1 * cos - t2 * sin, t2 * cos + t1 * sin], axis=-1).astype(t.dtype)


def centred_depthwise_conv(t, w):
    c = t.shape[-1]
    pad = CONV_K // 2
    return lax.conv_general_dilated(
        t, w.astype(t.dtype)[:, None, :], window_strides=(1,), padding=[(pad, pad)],
        dimension_numbers=("NWC", "WIO", "NWC"), feature_group_count=c)


def gated_delta_rule_chunked(q, k, v, g, beta):
    bsz, seq, nh, dk = q.shape
    dv = v.shape[-1]
    n = seq // CHUNK

    def to_chunks(t):
        t = t.astype(F32).reshape((bsz, n, CHUNK, nh) + t.shape[3:])
        return jnp.moveaxis(t, 3, 1)

    q = to_chunks(q) * (dk ** -0.5)
    k, v, g, beta = to_chunks(k), to_chunks(v), to_chunks(g), to_chunks(beta)
    gc = jnp.cumsum(g, axis=-1)
    idx = jnp.arange(CHUNK)
    incl = idx[:, None] >= idx[None, :]
    strict = idx[:, None] > idx[None, :]
    diff = gc[..., :, None] - gc[..., None, :]
    decay = jnp.where(incl, jnp.exp(jnp.where(incl, diff, 0.0)), 0.0)
    kb = k * beta[..., None]
    vb = v * beta[..., None]
    a = -jnp.where(strict, jnp.einsum('bhncd,bhnkd->bhnck', kb, k) * decay, 0.0)

    def fwd_sub(i, tm):
        row = tm[..., i, :]
        return tm.at[..., i, :].set(row + jnp.einsum('...j,...jk->...k', row, tm))

    tinv = lax.fori_loop(1, CHUNK, fwd_sub, a) + jnp.eye(CHUNK, dtype=F32)
    u = jnp.einsum('bhnck,bhnkd->bhncd', tinv, vb)
    w = jnp.einsum('bhnck,bhnkd->bhncd', tinv, kb * jnp.exp(gc)[..., None])
    qk = jnp.einsum('bhncd,bhnkd->bhnck', q, k) * decay
    q_dec = q * jnp.exp(gc)[..., None]
    k_dec = k * jnp.exp(gc[..., -1:] - gc)[..., None]
    chunk_decay = jnp.exp(gc[..., -1])
    xs = tuple(jnp.moveaxis(t_, 2, 0) for t_ in (u, w, qk, q_dec, k_dec, chunk_decay))

    def step(state, xs_i):
        u_i, w_i, qk_i, qd_i, kd_i, cd_i = xs_i
        v_new = u_i - jnp.einsum('bhcd,bhde->bhce', w_i, state)
        o = (jnp.einsum('bhcd,bhde->bhce', qd_i, state)
             + jnp.einsum('bhck,bhke->bhce', qk_i, v_new))
        state = state * cd_i[..., None, None] + jnp.einsum('bhcd,bhce->bhde', kd_i, v_new)
        return state, o

    s0 = jnp.zeros((bsz, nh, dk, dv), F32)
    _, o = lax.scan(step, s0, xs)
    return jnp.transpose(o, (1, 0, 3, 2, 4)).reshape(bsz, seq, nh, dv)


def windowed_gqa_with_sink(q, k, v, sink):
    bsz, seq, hq, d = q.shape
    hkv = k.shape[2]
    grp = hq // hkv
    nb = seq // WINDOW
    qb = q.reshape(bsz, nb, WINDOW, hkv, grp, d)

    def band(t):
        tp = jnp.pad(t, ((0, 0), (WINDOW, WINDOW), (0, 0), (0, 0)))
        tp = tp.reshape(bsz, nb + 2, WINDOW, hkv, d)
        return jnp.concatenate([tp[:, :-2], tp[:, 1:-1], tp[:, 2:]], axis=2)

    kw, vw = band(k), band(v)
    s = jnp.einsum('bnqhgd,bnkhd->bnhgqk', qb, kw).astype(F32) * (d ** -0.5)
    i = jnp.arange(WINDOW)
    j = jnp.arange(3 * WINDOW)
    blk = jnp.arange(nb)
    rel = j[None, :] - WINDOW - i[:, None]
    kpos = blk[:, None] * WINDOW - WINDOW + j[None, :]
    mask = (jnp.abs(rel) <= WINDOW)[None] & ((kpos >= 0) & (kpos < seq))[:, None, :]
    mask = mask[None, :, None, None]
    s = jnp.where(mask, s, -jnp.inf)
    sink_l = sink.astype(F32).reshape(1, 1, hkv, grp, 1, 1)
    m = jnp.maximum(jnp.max(s, axis=-1, keepdims=True), sink_l)
    p = jnp.exp(s - m)
    p = p / (jnp.sum(p, axis=-1, keepdims=True) + jnp.exp(sink_l - m))
    o = jnp.einsum('bnhgqk,bnkhd->bnqhgd', p.astype(v.dtype), vw)
    return o.reshape(bsz, seq, hq, d)


def memory_cross_attention(q, k, v):
    d = q.shape[-1]
    s = jnp.einsum('bshd,bmhd->bhsm', q, k).astype(F32) * (d ** -0.5)
    p = jax.nn.softmax(s, axis=-1)
    return jnp.einsum('bhsm,bmhd->bshd', p.astype(v.dtype), v)


def hybrid_layer(h, mem, norm_w, w_in, conv_w_a, a_log_fwd, a_log_bwd, dt_bias_fwd,
                 dt_bias_bwd, o_norm_a, q_norm_b, k_norm_b, sink_b, mem_norm_w, w_mem_kv,
                 q_norm_c, k_norm_c, w_out):
    bsz, seq, _ = h.shape
    hn = rms_norm(h, norm_w)
    proj = hn @ w_in.astype(hn.dtype)
    (qa, ka, va, za, alpha_f, alpha_b, beta_f, beta_b,
     qb, kb, vb, zb, qc, zc) = jnp.split(proj, np.cumsum(IN_SIZES)[:-1].tolist(), axis=-1)

    qkv = jax.nn.silu(centred_depthwise_conv(jnp.concatenate([qa, ka, va], axis=-1), conv_w_a))
    qa, ka, va = jnp.split(qkv, 3, axis=-1)
    qa = l2_norm(qa.reshape(bsz, seq, A_HEADS, A_HEAD_DIM))
    ka = l2_norm(ka.reshape(bsz, seq, A_HEADS, A_HEAD_DIM))
    va = va.reshape(bsz, seq, A_HEADS, A_HEAD_DIM).astype(F32)
    g_f = -jnp.exp(a_log_fwd.astype(F32)) * jax.nn.softplus(alpha_f.astype(F32) + dt_bias_fwd.astype(F32))
    g_b = -jnp.exp(a_log_bwd.astype(F32)) * jax.nn.softplus(alpha_b.astype(F32) + dt_bias_bwd.astype(F32))
    bt_f = jax.nn.sigmoid(beta_f.astype(F32))
    bt_b = jax.nn.sigmoid(beta_b.astype(F32))
    flip = lambda t: jnp.flip(t, axis=1)
    o_fwd = gated_delta_rule_chunked(qa, ka, va, g_f, bt_f)
    o_bwd = flip(gated_delta_rule_chunked(flip(qa), flip(ka), flip(va), flip(g_b), flip(bt_b)))
    o_a = rms_norm((o_fwd + o_bwd).astype(h.dtype), o_norm_a).reshape(bsz, seq, A_WIDTH)
    o_a = o_a * jax.nn.silu(za)

    qb = rope(rms_norm(qb.reshape(bsz, seq, B_Q_HEADS, B_HEAD_DIM), q_norm_b))
    kb = rope(rms_norm(kb.reshape(bsz, seq, B_KV_HEADS, B_HEAD_DIM), k_norm_b))
    vb = vb.reshape(bsz, seq, B_KV_HEADS, B_HEAD_DIM)
    o_b = windowed_gqa_with_sink(qb, kb, vb, sink_b).reshape(bsz, seq, B_WIDTH)
    o_b = o_b * jax.nn.silu(zb)

    mn = rms_norm(mem, mem_norm_w)
    kv_m = mn @ w_mem_kv.astype(mn.dtype)
    km, vm = jnp.split(kv_m, 2, axis=-1)
    mlen = mem.shape[1]
    km = rms_norm(km.reshape(bsz, mlen, C_HEADS, C_HEAD_DIM), k_norm_c)
    vm = vm.reshape(bsz, mlen, C_HEADS, C_HEAD_DIM)
    qc = rms_norm(qc.reshape(bsz, seq, C_HEADS, C_HEAD_DIM), q_norm_c)
    o_c = memory_cross_attention(qc, km, vm).reshape(bsz, seq, C_WIDTH)
    o_c = o_c * jax.nn.silu(zc)

    mixed = jnp.concatenate([o_a.astype(h.dtype), o_b.astype(h.dtype), o_c.astype(h.dtype)], axis=-1)
    return h + mixed @ w_out.astype(h.dtype)


def _fwd_setup_inputs(seed: int = 0) -> dict:
    key = jax.random.key(seed)
    ks = jax.random.split(key, 20)
    nrm = lambda k_, shape: jax.random.normal(k_, shape, F32)
    x = nrm(ks[0], (BATCH, SEQ, D_MODEL))
    mem = nrm(ks[1], (BATCH, MEM_LEN, D_MODEL))
    norm_w = 1.0 + 0.02 * nrm(ks[2], (DEPTH, D_MODEL))
    w_in = nrm(ks[3], (DEPTH, D_MODEL, IN_WIDTH)) * D_MODEL ** -0.5
    conv_w_a = nrm(ks[4], (DEPTH, CONV_K, 3 * A_WIDTH)) * CONV_K ** -0.5
    a_log_fwd = jnp.log(jax.random.uniform(ks[5], (DEPTH, A_HEADS), F32, 1.0, 16.0))
    a_log_bwd = jnp.log(jax.random.uniform(ks[6], (DEPTH, A_HEADS), F32, 1.0, 16.0))
    dt_f = jnp.exp(jax.random.uniform(ks[7], (DEPTH, A_HEADS), F32, jnp.log(1e-3), jnp.log(1e-1)))
    dt_b = jnp.exp(jax.random.uniform(ks[8], (DEPTH, A_HEADS), F32, jnp.log(1e-3), jnp.log(1e-1)))
    dt_bias_fwd = dt_f + jnp.log(-jnp.expm1(-dt_f))
    dt_bias_bwd = dt_b + jnp.log(-jnp.expm1(-dt_b))
    o_norm_a = 1.0 + 0.02 * nrm(ks[9], (DEPTH, A_HEAD_DIM))
    q_norm_b = 1.0 + 0.02 * nrm(ks[10], (DEPTH, B_HEAD_DIM))
    k_norm_b = 1.0 + 0.02 * nrm(ks[11], (DEPTH, B_HEAD_DIM))
    sink_b = 0.5 * nrm(ks[12], (DEPTH, B_Q_HEADS))
    mem_norm_w = 1.0 + 0.02 * nrm(ks[13], (DEPTH, D_MODEL))
    w_mem_kv = nrm(ks[14], (DEPTH, D_MODEL, 2 * C_WIDTH)) * D_MODEL ** -0.5
    q_norm_c = 1.0 + 0.02 * nrm(ks[15], (DEPTH, C_HEAD_DIM))
    k_norm_c = 1.0 + 0.02 * nrm(ks[16], (DEPTH, C_HEAD_DIM))
    w_out = nrm(ks[17], (DEPTH, MIX_WIDTH, D_MODEL)) * MIX_WIDTH ** -0.5
    return {"x": x, "mem": mem, "norm_w": norm_w, "w_in": w_in, "conv_w_a": conv_w_a,
            "a_log_fwd": a_log_fwd, "a_log_bwd": a_log_bwd, "dt_bias_fwd": dt_bias_fwd,
            "dt_bias_bwd": dt_bias_bwd, "o_norm_a": o_norm_a, "q_norm_b": q_norm_b,
            "k_norm_b": k_norm_b, "sink_b": sink_b, "mem_norm_w": mem_norm_w,
            "w_mem_kv": w_mem_kv, "q_norm_c": q_norm_c, "k_norm_c": k_norm_c, "w_out": w_out}


def _fwd_reference(x, mem, norm_w, w_in, conv_w_a, a_log_fwd, a_log_bwd, dt_bias_fwd, dt_bias_bwd,
              o_norm_a, q_norm_b, k_norm_b, sink_b, mem_norm_w, w_mem_kv, q_norm_c, k_norm_c,
              w_out):
    h = x
    for l in range(DEPTH):
        h = hybrid_layer(h, mem, norm_w[l], w_in[l], conv_w_a[l], a_log_fwd[l], a_log_bwd[l],
                         dt_bias_fwd[l], dt_bias_bwd[l], o_norm_a[l], q_norm_b[l], k_norm_b[l],
                         sink_b[l], mem_norm_w[l], w_mem_kv[l], q_norm_c[l], k_norm_c[l], w_out[l])
    return h


import jax as _jax
import jax.numpy as _jnp

TWIN_FORMAT = 'train_step'
FWD_PARAMS = ['x', 'mem', 'norm_w', 'w_in', 'conv_w_a', 'a_log_fwd', 'a_log_bwd', 'dt_bias_fwd', 'dt_bias_bwd', 'o_norm_a', 'q_norm_b', 'k_norm_b', 'sink_b', 'mem_norm_w', 'w_mem_kv', 'q_norm_c', 'k_norm_c', 'w_out']
TWIN_WEIGHTS = ['norm_w', 'w_in', 'conv_w_a', 'a_log_fwd', 'a_log_bwd', 'dt_bias_fwd', 'dt_bias_bwd', 'o_norm_a', 'q_norm_b', 'k_norm_b', 'sink_b', 'mem_norm_w', 'w_mem_kv', 'q_norm_c', 'k_norm_c', 'w_out']
TWIN_DIFF_INPUT = 'x'
TWIN_INPUTS = ['x', 'mem', 'norm_w', 'w_in', 'conv_w_a', 'a_log_fwd', 'a_log_bwd', 'dt_bias_fwd', 'dt_bias_bwd', 'o_norm_a', 'q_norm_b', 'k_norm_b', 'sink_b', 'mem_norm_w', 'w_mem_kv', 'q_norm_c', 'k_norm_c', 'w_out', 'loss_target', 'm_norm_w', 'm_w_in', 'm_conv_w_a', 'm_a_log_fwd', 'm_a_log_bwd', 'm_dt_bias_fwd', 'm_dt_bias_bwd', 'm_o_norm_a', 'm_q_norm_b', 'm_k_norm_b', 'm_sink_b', 'm_mem_norm_w', 'm_w_mem_kv', 'm_q_norm_c', 'm_k_norm_c', 'm_w_out', 'v_norm_w', 'v_w_in', 'v_conv_w_a', 'v_a_log_fwd', 'v_a_log_bwd', 'v_dt_bias_fwd', 'v_dt_bias_bwd', 'v_o_norm_a', 'v_q_norm_b', 'v_k_norm_b', 'v_sink_b', 'v_mem_norm_w', 'v_w_mem_kv', 'v_q_norm_c', 'v_k_norm_c', 'v_w_out']
TWIN_OUTPUTS = ['loss', 'grad_x', 'grad_norm_w', 'grad_w_in', 'grad_conv_w_a', 'grad_a_log_fwd', 'grad_a_log_bwd', 'grad_dt_bias_fwd', 'grad_dt_bias_bwd', 'grad_o_norm_a', 'grad_q_norm_b', 'grad_k_norm_b', 'grad_sink_b', 'grad_mem_norm_w', 'grad_w_mem_kv', 'grad_q_norm_c', 'grad_k_norm_c', 'grad_w_out', 'delta_norm_w', 'delta_w_in', 'delta_conv_w_a', 'delta_a_log_fwd', 'delta_a_log_bwd', 'delta_dt_bias_fwd', 'delta_dt_bias_bwd', 'delta_o_norm_a', 'delta_q_norm_b', 'delta_k_norm_b', 'delta_sink_b', 'delta_mem_norm_w', 'delta_w_mem_kv', 'delta_q_norm_c', 'delta_k_norm_c', 'delta_w_out', 'new_m_norm_w', 'new_m_w_in', 'new_m_conv_w_a', 'new_m_a_log_fwd', 'new_m_a_log_bwd', 'new_m_dt_bias_fwd', 'new_m_dt_bias_bwd', 'new_m_o_norm_a', 'new_m_q_norm_b', 'new_m_k_norm_b', 'new_m_sink_b', 'new_m_mem_norm_w', 'new_m_w_mem_kv', 'new_m_q_norm_c', 'new_m_k_norm_c', 'new_m_w_out', 'new_v_norm_w', 'new_v_w_in', 'new_v_conv_w_a', 'new_v_a_log_fwd', 'new_v_a_log_bwd', 'new_v_dt_bias_fwd', 'new_v_dt_bias_bwd', 'new_v_o_norm_a', 'new_v_q_norm_b', 'new_v_k_norm_b', 'new_v_sink_b', 'new_v_mem_norm_w', 'new_v_w_mem_kv', 'new_v_q_norm_c', 'new_v_k_norm_c', 'new_v_w_out']
TWIN_LEAF_KINDS = {'loss': 'loss', 'grad_x': 'grad_x', 'grad_norm_w': 'grad_w', 'grad_w_in': 'grad_w', 'grad_conv_w_a': 'grad_w', 'grad_a_log_fwd': 'grad_w', 'grad_a_log_bwd': 'grad_w', 'grad_dt_bias_fwd': 'grad_w', 'grad_dt_bias_bwd': 'grad_w', 'grad_o_norm_a': 'grad_w', 'grad_q_norm_b': 'grad_w', 'grad_k_norm_b': 'grad_w', 'grad_sink_b': 'grad_w', 'grad_mem_norm_w': 'grad_w', 'grad_w_mem_kv': 'grad_w', 'grad_q_norm_c': 'grad_w', 'grad_k_norm_c': 'grad_w', 'grad_w_out': 'grad_w', 'delta_norm_w': 'delta_w', 'delta_w_in': 'delta_w', 'delta_conv_w_a': 'delta_w', 'delta_a_log_fwd': 'delta_w', 'delta_a_log_bwd': 'delta_w', 'delta_dt_bias_fwd': 'delta_w', 'delta_dt_bias_bwd': 'delta_w', 'delta_o_norm_a': 'delta_w', 'delta_q_norm_b': 'delta_w', 'delta_k_norm_b': 'delta_w', 'delta_sink_b': 'delta_w', 'delta_mem_norm_w': 'delta_w', 'delta_w_mem_kv': 'delta_w', 'delta_q_norm_c': 'delta_w', 'delta_k_norm_c': 'delta_w', 'delta_w_out': 'delta_w', 'new_m_norm_w': 'new_m', 'new_m_w_in': 'new_m', 'new_m_conv_w_a': 'new_m', 'new_m_a_log_fwd': 'new_m', 'new_m_a_log_bwd': 'new_m', 'new_m_dt_bias_fwd': 'new_m', 'new_m_dt_bias_bwd': 'new_m', 'new_m_o_norm_a': 'new_m', 'new_m_q_norm_b': 'new_m', 'new_m_k_norm_b': 'new_m', 'new_m_sink_b': 'new_m', 'new_m_mem_norm_w': 'new_m', 'new_m_w_mem_kv': 'new_m', 'new_m_q_norm_c': 'new_m', 'new_m_k_norm_c': 'new_m', 'new_m_w_out': 'new_m', 'new_v_norm_w': 'new_v', 'new_v_w_in': 'new_v', 'new_v_conv_w_a': 'new_v', 'new_v_a_log_fwd': 'new_v', 'new_v_a_log_bwd': 'new_v', 'new_v_dt_bias_fwd': 'new_v', 'new_v_dt_bias_bwd': 'new_v', 'new_v_o_norm_a': 'new_v', 'new_v_q_norm_b': 'new_v', 'new_v_k_norm_b': 'new_v', 'new_v_sink_b': 'new_v', 'new_v_mem_norm_w': 'new_v', 'new_v_w_mem_kv': 'new_v', 'new_v_q_norm_c': 'new_v', 'new_v_k_norm_c': 'new_v', 'new_v_w_out': 'new_v'}


def _forward(args):
    return _fwd_reference(*[args[k] for k in FWD_PARAMS])


def _output_shape():
    out = _jax.eval_shape(lambda: _forward(_fwd_setup_inputs(0)))
    return out.shape, out.dtype

N_MICROBATCH = 1
ADAM_LR = 0.001
ADAM_B1 = 0.9
ADAM_B2 = 0.999
ADAM_EPS = 1e-08
ADAM_WD = 0.01
ADAM_STEP = 10
PER_EXAMPLE_BATCH_AXIS = {'x': 0, 'mem': 0, 'loss_target': 0}
SHARED_INPUTS = []
_WEIGHT_DTYPES = {'norm_w': _jnp.float32, 'w_in': _jnp.float32, 'conv_w_a': _jnp.float32, 'a_log_fwd': _jnp.float32, 'a_log_bwd': _jnp.float32, 'dt_bias_fwd': _jnp.float32, 'dt_bias_bwd': _jnp.float32, 'o_norm_a': _jnp.float32, 'q_norm_b': _jnp.float32, 'k_norm_b': _jnp.float32, 'sink_b': _jnp.float32, 'mem_norm_w': _jnp.float32, 'w_mem_kv': _jnp.float32, 'q_norm_c': _jnp.float32, 'k_norm_c': _jnp.float32, 'w_out': _jnp.float32}
MOMENT_SCALE = {'norm_w': 1.668276e+00, 'w_in': 6.901994e-02, 'conv_w_a': 9.029206e-02, 'a_log_fwd': 2.258787e-01, 'a_log_bwd': 2.651857e-01, 'dt_bias_fwd': 2.236122e-01, 'dt_bias_bwd': 2.668474e-01, 'o_norm_a': 2.357564e+01, 'q_norm_b': 1.741800e-01, 'k_norm_b': 1.747631e-01, 'sink_b': 3.836314e-03, 'mem_norm_w': 8.287945e-03, 'w_mem_kv': 7.283012e-03, 'q_norm_c': 1.014861e-01, 'k_norm_c': 1.016846e-01, 'w_out': 1.043766e-01}


def _to_microbatches(a, axis):
    t = _jnp.moveaxis(a, axis, 0)
    t = t.reshape((N_MICROBATCH, t.shape[0] // N_MICROBATCH) + t.shape[1:])
    return _jnp.moveaxis(t, 1, axis + 1)


def setup_inputs(seed: int = 0) -> dict:
    inp = _fwd_setup_inputs(seed)
    key = _jax.random.fold_in(_jax.random.key(seed), 7919)
    shape, _ = _output_shape()
    out = dict(inp)
    out["loss_target"] = _jax.random.normal(_jax.random.fold_in(key, 0), shape, _jnp.float32)
    for i, name in enumerate(TWIN_WEIGHTS):
        w = inp[name].astype(_jnp.float32)
        if MOMENT_SCALE is None:
            s = _jnp.sqrt(_jnp.mean(_jnp.square(w)) + 1e-30)
        else:
            s = MOMENT_SCALE[name]
        km, kv = _jax.random.split(_jax.random.fold_in(key, i + 1))
        out[name] = w
        out["m_" + name] = s * _jax.random.normal(km, w.shape, _jnp.float32)
        out["v_" + name] = (s * s) * _jax.random.uniform(kv, w.shape, _jnp.float32, 0.5, 1.5)
    if N_MICROBATCH > 1:
        for name, axis in PER_EXAMPLE_BATCH_AXIS.items():
            out[name] = _to_microbatches(out[name], axis)
    return {'x': out['x'], 'mem': out['mem'], 'norm_w': out['norm_w'], 'w_in': out['w_in'], 'conv_w_a': out['conv_w_a'], 'a_log_fwd': out['a_log_fwd'], 'a_log_bwd': out['a_log_bwd'], 'dt_bias_fwd': out['dt_bias_fwd'], 'dt_bias_bwd': out['dt_bias_bwd'], 'o_norm_a': out['o_norm_a'], 'q_norm_b': out['q_norm_b'], 'k_norm_b': out['k_norm_b'], 'sink_b': out['sink_b'], 'mem_norm_w': out['mem_norm_w'], 'w_mem_kv': out['w_mem_kv'], 'q_norm_c': out['q_norm_c'], 'k_norm_c': out['k_norm_c'], 'w_out': out['w_out'], 'loss_target': out['loss_target'], 'm_norm_w': out['m_norm_w'], 'm_w_in': out['m_w_in'], 'm_conv_w_a': out['m_conv_w_a'], 'm_a_log_fwd': out['m_a_log_fwd'], 'm_a_log_bwd': out['m_a_log_bwd'], 'm_dt_bias_fwd': out['m_dt_bias_fwd'], 'm_dt_bias_bwd': out['m_dt_bias_bwd'], 'm_o_norm_a': out['m_o_norm_a'], 'm_q_norm_b': out['m_q_norm_b'], 'm_k_norm_b': out['m_k_norm_b'], 'm_sink_b': out['m_sink_b'], 'm_mem_norm_w': out['m_mem_norm_w'], 'm_w_mem_kv': out['m_w_mem_kv'], 'm_q_norm_c': out['m_q_norm_c'], 'm_k_norm_c': out['m_k_norm_c'], 'm_w_out': out['m_w_out'], 'v_norm_w': out['v_norm_w'], 'v_w_in': out['v_w_in'], 'v_conv_w_a': out['v_conv_w_a'], 'v_a_log_fwd': out['v_a_log_fwd'], 'v_a_log_bwd': out['v_a_log_bwd'], 'v_dt_bias_fwd': out['v_dt_bias_fwd'], 'v_dt_bias_bwd': out['v_dt_bias_bwd'], 'v_o_norm_a': out['v_o_norm_a'], 'v_q_norm_b': out['v_q_norm_b'], 'v_k_norm_b': out['v_k_norm_b'], 'v_sink_b': out['v_sink_b'], 'v_mem_norm_w': out['v_mem_norm_w'], 'v_w_mem_kv': out['v_w_mem_kv'], 'v_q_norm_c': out['v_q_norm_c'], 'v_k_norm_c': out['v_k_norm_c'], 'v_w_out': out['v_w_out']}


def _loss(weights, diff, rest, loss_target):
    with _jax.named_scope("forward"):
        args = {**rest, TWIN_DIFF_INPUT: diff, **{k: w.astype(_WEIGHT_DTYPES[k]) for k, w in weights.items()}}
        y = _forward(args)
    with _jax.named_scope("loss_head"):
        err = _jnp.square(y.astype(_jnp.float32) - loss_target)
        return 0.5 * _jnp.sum(_jnp.mean(err, axis=-1)) if err.ndim else 0.5 * err


def _adamw(w, g, m, v):
    m = ADAM_B1 * m + (1.0 - ADAM_B1) * g
    v = ADAM_B2 * v + (1.0 - ADAM_B2) * _jnp.square(g)
    m_hat = m / (1.0 - ADAM_B1 ** ADAM_STEP)
    v_hat = v / (1.0 - ADAM_B2 ** ADAM_STEP)
    delta = -ADAM_LR * (m_hat / (_jnp.sqrt(v_hat) + ADAM_EPS) + ADAM_WD * w)
    return delta, m, v


def reference(x, mem, norm_w, w_in, conv_w_a, a_log_fwd, a_log_bwd, dt_bias_fwd, dt_bias_bwd, o_norm_a, q_norm_b, k_norm_b, sink_b, mem_norm_w, w_mem_kv, q_norm_c, k_norm_c, w_out, loss_target, m_norm_w, m_w_in, m_conv_w_a, m_a_log_fwd, m_a_log_bwd, m_dt_bias_fwd, m_dt_bias_bwd, m_o_norm_a, m_q_norm_b, m_k_norm_b, m_sink_b, m_mem_norm_w, m_w_mem_kv, m_q_norm_c, m_k_norm_c, m_w_out, v_norm_w, v_w_in, v_conv_w_a, v_a_log_fwd, v_a_log_bwd, v_dt_bias_fwd, v_dt_bias_bwd, v_o_norm_a, v_q_norm_b, v_k_norm_b, v_sink_b, v_mem_norm_w, v_w_mem_kv, v_q_norm_c, v_k_norm_c, v_w_out):
    given = dict(x=x, mem=mem, norm_w=norm_w, w_in=w_in, conv_w_a=conv_w_a, a_log_fwd=a_log_fwd, a_log_bwd=a_log_bwd, dt_bias_fwd=dt_bias_fwd, dt_bias_bwd=dt_bias_bwd, o_norm_a=o_norm_a, q_norm_b=q_norm_b, k_norm_b=k_norm_b, sink_b=sink_b, mem_norm_w=mem_norm_w, w_mem_kv=w_mem_kv, q_norm_c=q_norm_c, k_norm_c=k_norm_c, w_out=w_out, loss_target=loss_target, m_norm_w=m_norm_w, m_w_in=m_w_in, m_conv_w_a=m_conv_w_a, m_a_log_fwd=m_a_log_fwd, m_a_log_bwd=m_a_log_bwd, m_dt_bias_fwd=m_dt_bias_fwd, m_dt_bias_bwd=m_dt_bias_bwd, m_o_norm_a=m_o_norm_a, m_q_norm_b=m_q_norm_b, m_k_norm_b=m_k_norm_b, m_sink_b=m_sink_b, m_mem_norm_w=m_mem_norm_w, m_w_mem_kv=m_w_mem_kv, m_q_norm_c=m_q_norm_c, m_k_norm_c=m_k_norm_c, m_w_out=m_w_out, v_norm_w=v_norm_w, v_w_in=v_w_in, v_conv_w_a=v_conv_w_a, v_a_log_fwd=v_a_log_fwd, v_a_log_bwd=v_a_log_bwd, v_dt_bias_fwd=v_dt_bias_fwd, v_dt_bias_bwd=v_dt_bias_bwd, v_o_norm_a=v_o_norm_a, v_q_norm_b=v_q_norm_b, v_k_norm_b=v_k_norm_b, v_sink_b=v_sink_b, v_mem_norm_w=v_mem_norm_w, v_w_mem_kv=v_w_mem_kv, v_q_norm_c=v_q_norm_c, v_k_norm_c=v_k_norm_c, v_w_out=v_w_out)
    weights = {n: given[n] for n in TWIN_WEIGHTS}
    shared = {n: given[n] for n in SHARED_INPUTS}
    per_example = {n: given[n] for n in ['x', 'mem']}
    grad_fn = _jax.value_and_grad(_loss, argnums=(0, 1))

    def one_microbatch(ex, loss_target):
        ex = dict(ex)
        diff = ex.pop(TWIN_DIFF_INPUT)
        return grad_fn(weights, diff, {**shared, **ex}, loss_target)

    if N_MICROBATCH == 1:
        loss, (grad_w, grad_x) = one_microbatch(per_example, given["loss_target"])
    else:
        def body(carry, xs):
            loss_sum, grad_sum = carry
            l_k, (gw_k, gx_k) = one_microbatch(xs[0], xs[1])
            with _jax.named_scope("update"):
                return (loss_sum + l_k, _jax.tree.map(_jnp.add, grad_sum, gw_k)), gx_k

        init = (_jnp.zeros((), _jnp.float32), _jax.tree.map(_jnp.zeros_like, weights))
        (loss, grad_w), grad_x = _jax.lax.scan(body, init, (per_example, given["loss_target"]))
    with _jax.named_scope("update"):
        delta_w, new_m, new_v = {}, {}, {}
        for n in TWIN_WEIGHTS:
            delta_w[n], new_m[n], new_v[n] = _adamw(weights[n], grad_w[n], given["m_" + n], given["v_" + n])
    return (loss, grad_x, *[grad_w[n] for n in TWIN_WEIGHTS], *[delta_w[n] for n in TWIN_WEIGHTS],
            *[new_m[n] for n in TWIN_WEIGHTS], *[new_v[n] for n in TWIN_WEIGHTS])
```

```python
import functools

import jax
import jax.numpy as jnp
from jax import lax
from jax.experimental import pallas as pl
from jax.experimental.pallas import tpu as pltpu

F32 = jnp.float32
BF16 = jnp.bfloat16
HI = lax.Precision.HIGHEST
MESH = pl.DeviceIdType.MESH

D_MODEL = 2048
A_WIDTH = 1024
A_HEADS = 8
A_DIM = 128
CONV_K = 5
CHUNK = 64
B_HEADS = 8
B_KV = 2
B_DIM = 64
WINDOW = 128
C_HEADS = 4
C_DIM = 128
MEM_LEN = 256
ROPE_THETA = 10000.0
EPS = 1e-6
IN_WIDTH = 6432
N_CHIPS = 4
W_IN_BLOCK = IN_WIDTH // N_CHIPS

LANE = 128
P_QA, P_KA, P_VA, P_ZA = 0, 1024, 2048, 3072
P_QB, P_ZB, P_QC, P_ZC = 4096, 4608, 5120, 5632
P_KB, P_VB, P_GT = 6144, 6272, 6400
P_WIDTH = 6656
O_GT, O_QB, O_KB, O_VB, O_ZB, O_QC, O_ZC = 4096, 4128, 4640, 4768, 4896, 5408, 5920

ADAM_LR, ADAM_B1, ADAM_B2, ADAM_EPS, ADAM_WD, ADAM_STEP = 0.001, 0.9, 0.999, 1e-08, 0.01, 10

VMEM_LIMIT = 56 * 1024 * 1024


def _params(sem=None):
    return pltpu.CompilerParams(dimension_semantics=sem, vmem_limit_bytes=VMEM_LIMIT)


def _dot(a, b, dims=(((1,), (0,)), ((), ())), precision=HI):
    return lax.dot_general(a, b, dims, precision=precision, preferred_element_type=F32)


def _dot_nt(a, b, precision=HI):
    return _dot(a, b, (((1,), (1,)), ((), ())), precision)


def _dot_tn(a, b, precision=HI):
    return _dot(a, b, (((0,), (0,)), ((), ())), precision)


def _rms(t, w):
    return t * lax.rsqrt(jnp.mean(t * t, axis=-1, keepdims=True) + EPS) * w


def _l2(t):
    return t * lax.rsqrt(jnp.sum(t * t, axis=-1, keepdims=True) + EPS)


def _silu(t):
    return t * jax.nn.sigmoid(t)


def _softplus(t):
    return jnp.maximum(t, 0.0) + jnp.log1p(jnp.exp(-jnp.abs(t)))


def _matmul(a, b, mode, out_dtype, name, tm=512, tn=512, tk=512):
    (m, k) = a.shape[::-1] if mode == "tn" else a.shape
    n = b.shape[0] if mode == "nt" else b.shape[1]
    tm, tn, tk = min(tm, m), min(tn, n), min(tk, k)
    assert m % tm == 0 and n % tn == 0 and k % tk == 0, (m, n, k, tm, tn, tk)
    if mode == "nn":
        a_spec = pl.BlockSpec((tm, tk), lambda i, j, kk: (i, kk))
        b_spec = pl.BlockSpec((tk, tn), lambda i, j, kk: (kk, j))
        dims = (((1,), (0,)), ((), ()))
    elif mode == "nt":
        a_spec = pl.BlockSpec((tm, tk), lambda i, j, kk: (i, kk))
        b_spec = pl.BlockSpec((tn, tk), lambda i, j, kk: (j, kk))
        dims = (((1,), (1,)), ((), ()))
    else:
        a_spec = pl.BlockSpec((tk, tm), lambda i, j, kk: (kk, i))
        b_spec = pl.BlockSpec((tk, tn), lambda i, j, kk: (kk, j))
        dims = (((0,), (0,)), ((), ()))
    nk = k // tk

    def body(a_ref, b_ref, o_ref, acc_ref):
        kk = pl.program_id(2)

        @pl.when(kk == 0)
        def _():
            acc_ref[...] = jnp.zeros_like(acc_ref)

        acc_ref[...] += lax.dot_general(a_ref[...].astype(BF16), b_ref[...].astype(BF16), dims,
                                        preferred_element_type=F32)

        @pl.when(kk == nk - 1)
        def _():
            o_ref[...] = acc_ref[...].astype(out_dtype)

    return pl.pallas_call(
        body, name=name, grid=(m // tm, n // tn, nk),
        in_specs=[a_spec, b_spec], out_specs=pl.BlockSpec((tm, tn), lambda i, j, kk: (i, j)),
        out_shape=jax.ShapeDtypeStruct((m, n), out_dtype),
        scratch_shapes=[pltpu.VMEM((tm, tn), F32)],
        compiler_params=_params(("parallel", "parallel", "arbitrary")),
    )(a, b)


def _rms_fwd(x, w, tr=256):
    s, d = x.shape

    def body(x_ref, w_ref, o_ref):
        o_ref[...] = _rms(x_ref[...], w_ref[...]).astype(BF16)

    return pl.pallas_call(
        body, name="rms_fwd", grid=(s // tr,),
        in_specs=[pl.BlockSpec((tr, d), lambda i: (i, 0)), pl.BlockSpec((1, d), lambda i: (0, 0))],
        out_specs=pl.BlockSpec((tr, d), lambda i: (i, 0)),
        out_shape=jax.ShapeDtypeStruct((s, d), BF16), compiler_params=_params(("parallel",)),
    )(x, w)


def _rms_bwd(x, w, d_hn, dy, tr=256):
    s, d = x.shape

    def body(x_ref, w_ref, g_ref, dy_ref, gx_ref, gw_ref):
        _, vjp = jax.vjp(_rms, x_ref[...], w_ref[...])
        dx, dw = vjp(g_ref[...])
        gx_ref[...] = dy_ref[...] + dx

        @pl.when(pl.program_id(0) == 0)
        def _():
            gw_ref[...] = jnp.zeros_like(gw_ref)

        gw_ref[...] += dw

    row = pl.BlockSpec((tr, d), lambda i: (i, 0))
    vec = pl.BlockSpec((1, d), lambda i: (0, 0))
    return pl.pallas_call(
        body, name="rms_bwd", grid=(s // tr,), in_specs=[row, vec, row, row], out_specs=[row, vec],
        out_shape=[jax.ShapeDtypeStruct((s, d), F32), jax.ShapeDtypeStruct((1, d), F32)],
        compiler_params=_params(("arbitrary",)),
    )(x, w, d_hn, dy)


def _loss_dy(x, mo, target, tr=256):
    s, d = x.shape
    nt = s // tr

    def body(x_ref, mo_ref, t_ref, dy_ref, dyb_ref, l_ref):
        err = x_ref[...] + mo_ref[...] - t_ref[...]
        dy = err * (1.0 / d)
        dy_ref[...] = dy
        dyb_ref[...] = dy.astype(BF16)
        l_ref[...] = jnp.full(l_ref.shape, 0.5 * jnp.sum(jnp.sum(err * err, axis=1, keepdims=True) * (1.0 / d)), F32)

    row = pl.BlockSpec((tr, d), lambda i: (i, 0))
    return pl.pallas_call(
        body, name="loss_dy", grid=(nt,), in_specs=[row, row, row],
        out_specs=[row, row, pl.BlockSpec((1, 8, LANE), lambda i: (i, 0, 0))],
        out_shape=[jax.ShapeDtypeStruct((s, d), F32), jax.ShapeDtypeStruct((s, d), BF16),
                   jax.ShapeDtypeStruct((nt, 8, LANE), F32)],
        compiler_params=_params(("parallel",)),
    )(x, mo, target)


def _shift_rows(t, s):
    if s == 0:
        return t
    n = t.shape[0]
    rolled = pltpu.roll(t, (-s) % n, axis=0)
    idx = lax.broadcasted_iota(jnp.int32, t.shape, 0) + s
    return jnp.where((idx >= 0) & (idx < n), rolled, 0.0)


def _conv_fwd(proj, conv_w):
    s = proj.shape[0]
    nblk = 3 * A_WIDTH // LANE

    def body(x_ref, w_ref, o_ref):
        x = x_ref[...]
        acc = jnp.zeros_like(x)
        for j in range(CONV_K):
            acc = acc + w_ref[j:j + 1, :] * _shift_rows(x, j - CONV_K // 2)
        o_ref[...] = acc

    return pl.pallas_call(
        body, name="conv_fwd", grid=(nblk,),
        in_specs=[pl.BlockSpec((s, LANE), lambda i: (0, i)), pl.BlockSpec((CONV_K, LANE), lambda i: (0, i))],
        out_specs=pl.BlockSpec((None, s, LANE), lambda i: (i // A_HEADS, 0, i % A_HEADS)),
        out_shape=jax.ShapeDtypeStruct((3, s, A_WIDTH), F32), compiler_params=_params(("parallel",)),
    )(proj, conv_w)


def _conv_bwd(proj, conv_w, d_c):
    s = proj.shape[0]
    nblk = 3 * A_WIDTH // LANE

    def body(x_ref, w_ref, g_ref, dx_ref, dw_ref):
        x, g = x_ref[...], g_ref[...]
        acc = jnp.zeros_like(x)
        for j in range(CONV_K):
            off = j - CONV_K // 2
            acc = acc + w_ref[j:j + 1, :] * _shift_rows(g, -off)
            dw_ref[j:j + 1, :] = jnp.sum(_shift_rows(x, off) * g, axis=0, keepdims=True)
        dx_ref[...] = acc

    col = pl.BlockSpec((s, LANE), lambda i: (0, i))
    wsp = pl.BlockSpec((CONV_K, LANE), lambda i: (0, i))
    dsp = pl.BlockSpec((None, s, LANE), lambda i: (i // A_HEADS, 0, i % A_HEADS))
    return pl.pallas_call(
        body, name="conv_bwd", grid=(nblk,), in_specs=[col, wsp, dsp], out_specs=[col, wsp],
        out_shape=[jax.ShapeDtypeStruct((s, 3 * A_WIDTH), F32), jax.ShapeDtypeStruct((CONV_K, 3 * A_WIDTH), F32)],
        compiler_params=_params(("parallel",)),
    )(proj, conv_w, d_c)


def _a_chunk(st, cq, ck, cv, gt, pa, h, reverse):
    c = CHUNK
    ii = lax.broadcasted_iota(jnp.int32, (c, c), 0)
    jj = lax.broadcasted_iota(jnp.int32, (c, c), 1)
    incl = (ii <= jj) if reverse else (ii >= jj)
    strict = (ii < jj) if reverse else (ii > jj)
    tri = incl.astype(F32)
    eye = (ii == jj).astype(F32)
    lane = lax.broadcasted_iota(jnp.int32, (1, LANE), 1)
    sel_row = lax.broadcasted_iota(jnp.int32, (LANE, LANE), 0)
    r_alog, r_dt = (1, 3) if reverse else (0, 2)
    o_alpha, o_beta = (8, 24) if reverse else (0, 16)
    a_log = jnp.sum(jnp.where(lane == h, pa[r_alog:r_alog + 1, :], 0.0), axis=1, keepdims=True)
    dt_b = jnp.sum(jnp.where(lane == h, pa[r_dt:r_dt + 1, :], 0.0), axis=1, keepdims=True)
    alpha = _dot(gt, (sel_row == h + o_alpha).astype(F32))
    beta_raw = _dot(gt, (sel_row == h + o_beta).astype(F32))
    gb = -jnp.exp(a_log) * _softplus(alpha + dt_b)
    bb = jax.nn.sigmoid(beta_raw)
    q = _l2(_silu(cq)) * (A_DIM ** -0.5)
    k = _l2(_silu(ck))
    v = _silu(cv)

    gc = _dot(tri, gb)
    tot = _dot(jnp.ones((c, c), F32), gb)
    m1 = gc[:, :c]
    decay = jnp.where(incl, jnp.exp(jnp.where(incl, m1 - m1.T, 0.0)), 0.0)
    kb = k * bb
    vb = v * bb
    a = -jnp.where(strict, _dot_nt(kb, k) * decay, 0.0)
    tinv = eye + a
    p = a
    for _ in range(5):
        p = _dot(p, p)
        tinv = tinv + _dot(tinv, p)
    eg = jnp.exp(gc)
    u = _dot(tinv, vb)
    w = _dot(tinv, kb * eg)
    qk = _dot_nt(q, k) * decay
    v_new = u - _dot(w, st)
    o = _dot(q * eg, st) + _dot(qk, v_new)
    st_new = st * jnp.exp(tot[:1, :]) + _dot_tn(k * jnp.exp(tot - gc), v_new)
    return st_new, o


def _a_final(o, za, pa):
    return _rms(o, pa[4:5, :]) * _silu(za)


def _a_scan(reverse, h, nchunk, c_ref, gt_ref, pa, o_ref, s_ref, accumulate):
    def step(n, st):
        i = (nchunk - 1 - n) if reverse else n
        sl = pl.ds(pl.multiple_of(i * CHUNK, CHUNK), CHUNK)
        st_new, o = _a_chunk(st, c_ref[0, sl, :], c_ref[1, sl, :], c_ref[2, sl, :], gt_ref[sl, :], pa, h, reverse)
        if s_ref is not None:
            s_ref[i] = st
        if accumulate:
            o_ref[sl, :] += o
        else:
            o_ref[sl, :] = o
        return st_new

    lax.fori_loop(0, nchunk, step, jnp.zeros((A_DIM, A_DIM), F32))


def _delta_fwd(cqkv, proj, pa):
    s = cqkv.shape[1]
    nchunk = s // CHUNK

    def body(c_ref, gt_ref, za_ref, pa_ref, out_ref, o_acc):
        h = pl.program_id(0)
        pa_v = pa_ref[...]
        _a_scan(False, h, nchunk, c_ref, gt_ref, pa_v, o_acc, None, False)
        _a_scan(True, h, nchunk, c_ref, gt_ref, pa_v, o_acc, None, True)
        out_ref[...] = _a_final(o_acc[...], za_ref[...], pa_v).astype(BF16)

    def col(base):
        return pl.BlockSpec((s, LANE), lambda h: (0, base + h))

    return pl.pallas_call(
        body, name="delta_fwd", grid=(A_HEADS,),
        in_specs=[pl.BlockSpec((3, s, LANE), lambda h: (0, 0, h)), pl.BlockSpec((s, LANE), lambda h: (0, P_GT // LANE)),
                  col(P_ZA // LANE), pl.BlockSpec((8, LANE), lambda h: (0, 0))],
        out_specs=col(0), out_shape=jax.ShapeDtypeStruct((s, A_WIDTH), BF16),
        scratch_shapes=[pltpu.VMEM((s, A_DIM), F32)], compiler_params=_params(("parallel",)),
    )(cqkv, proj, proj, pa)


def _delta_bwd(cqkv, proj, pa, d_mixed):
    s = cqkv.shape[1]
    nchunk = s // CHUNK

    def body(c_ref, gt_ref, za_ref, pa_ref, dm_ref,
             dc_ref, dza_ref, dgt_ref, dpa_ref, sf_ref, sb_ref, o_acc, do_ref):
        h = pl.program_id(0)
        pa_v = pa_ref[...]

        @pl.when(h == 0)
        def _():
            dgt_ref[...] = jnp.zeros_like(dgt_ref)
            dpa_ref[...] = jnp.zeros_like(dpa_ref)

        _a_scan(False, h, nchunk, c_ref, gt_ref, pa_v, o_acc, sf_ref, False)
        _a_scan(True, h, nchunk, c_ref, gt_ref, pa_v, o_acc, sb_ref, True)
        _, vjp = jax.vjp(_a_final, o_acc[...], za_ref[...], pa_v)
        d_o, d_za, dpa0 = vjp(dm_ref[...])
        do_ref[...] = d_o
        dza_ref[...] = d_za

        def scan_bwd(reverse, s_ref, accumulate, dpa_in):
            def step(n, carry):
                d_st, dpa = carry
                i = n if reverse else (nchunk - 1 - n)
                sl = pl.ds(pl.multiple_of(i * CHUNK, CHUNK), CHUNK)

                def f(st, cq, ck, cv, gt, pa_):
                    return _a_chunk(st, cq, ck, cv, gt, pa_, h, reverse)

                _, vjp_c = jax.vjp(f, s_ref[i], c_ref[0, sl, :], c_ref[1, sl, :], c_ref[2, sl, :], gt_ref[sl, :], pa_v)
                d_prev, dcq, dck, dcv, dgt, dpa_i = vjp_c((d_st, do_ref[sl, :]))
                for r, dc in enumerate((dcq, dck, dcv)):
                    if accumulate:
                        dc_ref[r, sl, :] += dc
                    else:
                        dc_ref[r, sl, :] = dc
                dgt_ref[sl, :] += dgt
                return d_prev, dpa + dpa_i

            _, dpa_out = lax.fori_loop(0, nchunk, step, (jnp.zeros((A_DIM, A_DIM), F32), dpa_in))
            return dpa_out

        dpa1 = scan_bwd(False, sf_ref, False, dpa0)
        dpa2 = scan_bwd(True, sb_ref, True, dpa1)
        dpa_ref[...] += dpa2

    def col(base):
        return pl.BlockSpec((s, LANE), lambda h: (0, base + h))

    fixed = pl.BlockSpec((s, LANE), lambda h: (0, 0))
    small = pl.BlockSpec((8, LANE), lambda h: (0, 0))
    trio = pl.BlockSpec((3, s, LANE), lambda h: (0, 0, h))
    return pl.pallas_call(
        body, name="delta_bwd", grid=(A_HEADS,),
        in_specs=[trio, pl.BlockSpec((s, LANE), lambda h: (0, P_GT // LANE)), col(P_ZA // LANE), small, col(0)],
        out_specs=[trio, col(0), fixed, small],
        out_shape=[jax.ShapeDtypeStruct((3, s, A_WIDTH), F32), jax.ShapeDtypeStruct((s, A_WIDTH), F32),
                   jax.ShapeDtypeStruct((s, LANE), F32), jax.ShapeDtypeStruct((8, LANE), F32)],
        scratch_shapes=[pltpu.VMEM((nchunk, A_DIM, A_DIM), F32), pltpu.VMEM((nchunk, A_DIM, A_DIM), F32),
                        pltpu.VMEM((s, A_DIM), F32), pltpu.VMEM((s, A_DIM), F32)],
        compiler_params=_params(("arbitrary",)),
    )(cqkv, proj, proj, pa, d_mixed)


def _rope_tables(s):
    inv = ROPE_THETA ** (-jnp.arange(0, B_DIM, 2, dtype=F32) / B_DIM)
    ang = jnp.arange(s, dtype=F32)[:, None] * inv[None, :]
    cos, sin = jnp.cos(ang), jnp.sin(ang)
    return jnp.concatenate([cos, cos], axis=1), jnp.concatenate([-sin, sin], axis=1)


def _b_block(q_t, z_t, k3, v3, cos_q, sin_q, cos_k, sin_k, pb, n, nb):
    w = WINDOW
    r = lax.broadcasted_iota(jnp.int32, (B_DIM, B_DIM), 0)
    c = lax.broadcasted_iota(jnp.int32, (B_DIM, B_DIM), 1)
    swap = (r == (c + B_DIM // 2) % B_DIM).astype(F32)
    qi = lax.broadcasted_iota(jnp.int32, (w, 3 * w), 0)
    kj = lax.broadcasted_iota(jnp.int32, (w, 3 * w), 1)
    kpos = kj + (n - 1) * w
    mask = (jnp.abs(kj - w - qi) <= w) & (kpos >= 0) & (kpos < nb * w)
    lane = lax.broadcasted_iota(jnp.int32, (1, LANE), 1)
    qn, kn = pb[0:1, :B_DIM], pb[1:2, :B_DIM]
    outs = []
    for hk in range(B_KV):
        k = _rms(k3[:, hk * B_DIM:(hk + 1) * B_DIM], kn)
        k = k * cos_k + _dot(k, swap) * sin_k
        v = v3[:, hk * B_DIM:(hk + 1) * B_DIM]
        for g in range(B_HEADS // B_KV):
            hq = hk * (B_HEADS // B_KV) + g
            q = _rms(q_t[:, hq * B_DIM:(hq + 1) * B_DIM], qn)
            q = q * cos_q + _dot(q, swap) * sin_q
            sink = jnp.sum(jnp.where(lane == hq, pb[2:3, :], 0.0), axis=1, keepdims=True)
            s = _dot_nt(q, k) * (B_DIM ** -0.5)
            s = jnp.where(mask, s, -jnp.inf)
            m = jnp.maximum(jnp.max(s, axis=1, keepdims=True), sink)
            p = jnp.exp(s - m)
            p = p / (jnp.sum(p, axis=1, keepdims=True) + jnp.exp(sink - m))
            outs.append(_dot(p, v))
    return jnp.concatenate(outs, axis=1) * _silu(z_t)


def _b_specs(s):
    nb = s // WINDOW
    qsp = pl.BlockSpec((WINDOW, 512), lambda n: (n, P_QB // 512))
    zsp = pl.BlockSpec((WINDOW, 512), lambda n: (n, P_ZB // 512))

    def three(col, width):
        return [pl.BlockSpec((WINDOW, width), lambda n: (jnp.maximum(n - 1, 0), col)),
                pl.BlockSpec((WINDOW, width), lambda n: (n, col)),
                pl.BlockSpec((WINDOW, width), lambda n: (jnp.minimum(n + 1, nb - 1), col))]

    tab = pl.BlockSpec((WINDOW, B_DIM), lambda n: (n, 0))
    small = pl.BlockSpec((8, LANE), lambda n: (0, 0))
    specs = [qsp, zsp] + three(P_KB // LANE, LANE) + three(P_VB // LANE, LANE) + [tab, tab] + three(0, B_DIM) + three(0, B_DIM) + [small]
    return nb, specs


def _b_args(proj, cos2, sin2, pb):
    return (proj, proj, proj, proj, proj, proj, proj, proj, cos2, sin2, cos2, cos2, cos2, sin2, sin2, sin2, pb)


def _b_load(refs):
    (q_ref, z_ref, kp, kc, kx, vp, vc, vx, cq, sq, ckp, ckc, ckx, skp, skc, skx, pb_ref) = refs
    cat = lambda *r: jnp.concatenate([t[...] for t in r], axis=0)
    return (q_ref[...], z_ref[...], cat(kp, kc, kx), cat(vp, vc, vx), cq[...], sq[...], cat(ckp, ckc, ckx),
            cat(skp, skc, skx), pb_ref[...])


def _attn_b_fwd(proj, cos2, sin2, pb):
    s = proj.shape[0]
    nb, specs = _b_specs(s)

    def body(*refs):
        o_ref = refs[-1]
        args = _b_load(refs[:-1])
        o_ref[...] = _b_block(*args, pl.program_id(0), nb).astype(BF16)

    return pl.pallas_call(
        body, name="attn_b_fwd", grid=(nb,), in_specs=specs,
        out_specs=pl.BlockSpec((WINDOW, 512), lambda n: (n, 0)),
        out_shape=jax.ShapeDtypeStruct((s, 512), BF16), compiler_params=_params(("parallel",)),
    )(*_b_args(proj, cos2, sin2, pb))


def _attn_b_bwd(proj, cos2, sin2, pb, d_mixed):
    s = proj.shape[0]
    nb, specs = _b_specs(s)
    w = WINDOW

    def body(*refs):
        dm_ref, dq_ref, dz_ref, dk_ref, dv_ref, dpb_ref = refs[-6:]
        n = pl.program_id(0)
        q_t, z_t, k3, v3, cq, sq, ck, sk, pb_v = _b_load(refs[:-6])

        @pl.when(n == 0)
        def _():
            dk_ref[...] = jnp.zeros_like(dk_ref)
            dv_ref[...] = jnp.zeros_like(dv_ref)
            dpb_ref[...] = jnp.zeros_like(dpb_ref)

        def f(q_, z_, k_, v_, pb_):
            return _b_block(q_, z_, k_, v_, cq, sq, ck, sk, pb_, n, nb)

        _, vjp = jax.vjp(f, q_t, z_t, k3, v3, pb_v)
        dq, dz, dk3, dv3, dpb = vjp(dm_ref[...])
        dq_ref[...] = dq
        dz_ref[...] = dz
        dpb_ref[...] += dpb

        def add(j, cond):
            @pl.when(cond)
            def _():
                rows = pl.ds(pl.multiple_of((n - 1 + j) * w, w), w)
                dk_ref[rows, :] += dk3[j * w:(j + 1) * w, :]
                dv_ref[rows, :] += dv3[j * w:(j + 1) * w, :]

        add(0, n > 0)
        add(1, n >= 0)
        add(2, n < nb - 1)

    blk = pl.BlockSpec((w, 512), lambda n: (n, 0))
    whole = pl.BlockSpec((s, LANE), lambda n: (0, 0))
    small = pl.BlockSpec((8, LANE), lambda n: (0, 0))
    return pl.pallas_call(
        body, name="attn_b_bwd", grid=(nb,),
        in_specs=specs + [pl.BlockSpec((w, 512), lambda n: (n, 2))],
        out_specs=[blk, blk, whole, whole, small],
        out_shape=[jax.ShapeDtypeStruct((s, 512), F32), jax.ShapeDtypeStruct((s, 512), F32),
                   jax.ShapeDtypeStruct((s, LANE), F32), jax.ShapeDtypeStruct((s, LANE), F32),
                   jax.ShapeDtypeStruct((8, LANE), F32)],
        compiler_params=_params(("arbitrary",)),
    )(*_b_args(proj, cos2, sin2, pb), d_mixed)


def _mem_kv_fwd(mem, mem_norm_w, w_kv):
    def body(mem_ref, nw_ref, w_ref, kv_ref):
        mn = _rms(mem_ref[...], nw_ref[...]).astype(BF16)
        kv_ref[...] = jnp.dot(mn, w_ref[...], preferred_element_type=F32)

    return pl.pallas_call(
        body, name="mem_kv_fwd", out_shape=jax.ShapeDtypeStruct((MEM_LEN, 2 * C_HEADS * C_DIM), F32),
        compiler_params=_params(),
    )(mem, mem_norm_w, w_kv)


def _mem_kv_bwd(mem, mem_norm_w, w_kv, d_kv):
    def body(mem_ref, nw_ref, w_ref, g_ref, gw_ref, gn_ref):
        mn, vjp = jax.vjp(_rms, mem_ref[...], nw_ref[...])
        g = g_ref[...].astype(BF16)
        gw_ref[...] = lax.dot_general(mn.astype(BF16), g, (((0,), (0,)), ((), ())), preferred_element_type=F32)
        d_mn = lax.dot_general(g, w_ref[...], (((1,), (1,)), ((), ())), preferred_element_type=F32)
        gn_ref[...] = vjp(d_mn)[1]

    return pl.pallas_call(
        body, name="mem_kv_bwd",
        out_shape=[jax.ShapeDtypeStruct((D_MODEL, 2 * C_HEADS * C_DIM), F32), jax.ShapeDtypeStruct((1, D_MODEL), F32)],
        compiler_params=_params(),
    )(mem, mem_norm_w, w_kv, d_kv)


def _c_tile(q_t, z_t, kvm, pc):
    width = C_HEADS * C_DIM
    outs = []
    for h in range(C_HEADS):
        q = _rms(q_t[:, h * C_DIM:(h + 1) * C_DIM], pc[0:1, :])
        k = _rms(kvm[:, h * C_DIM:(h + 1) * C_DIM], pc[1:2, :])
        v = kvm[:, width + h * C_DIM:width + (h + 1) * C_DIM]
        s = _dot_nt(q, k) * (C_DIM ** -0.5)
        p = jnp.exp(s - jnp.max(s, axis=1, keepdims=True))
        p = p / jnp.sum(p, axis=1, keepdims=True)
        outs.append(_dot(p, v))
    return jnp.concatenate(outs, axis=1) * _silu(z_t)


def _attn_c_fwd(proj, kvm, pc, tq=256):
    s = proj.shape[0]

    def body(q_ref, z_ref, kv_ref, pc_ref, o_ref):
        o_ref[...] = _c_tile(q_ref[...], z_ref[...], kv_ref[...], pc_ref[...]).astype(BF16)

    return pl.pallas_call(
        body, name="attn_c_fwd", grid=(s // tq,),
        in_specs=[pl.BlockSpec((tq, 512), lambda i: (i, P_QC // 512)), pl.BlockSpec((tq, 512), lambda i: (i, P_ZC // 512)),
                  pl.BlockSpec(kvm.shape, lambda i: (0, 0)), pl.BlockSpec((8, LANE), lambda i: (0, 0))],
        out_specs=pl.BlockSpec((tq, 512), lambda i: (i, 0)),
        out_shape=jax.ShapeDtypeStruct((s, 512), BF16), compiler_params=_params(("parallel",)),
    )(proj, proj, kvm, pc)


def _attn_c_bwd(proj, kvm, pc, d_mixed, tq=256):
    s = proj.shape[0]

    def body(q_ref, z_ref, kv_ref, pc_ref, dm_ref, dq_ref, dz_ref, dkv_ref, dpc_ref):
        @pl.when(pl.program_id(0) == 0)
        def _():
            dkv_ref[...] = jnp.zeros_like(dkv_ref)
            dpc_ref[...] = jnp.zeros_like(dpc_ref)

        _, vjp = jax.vjp(_c_tile, q_ref[...], z_ref[...], kv_ref[...], pc_ref[...])
        dq, dz, dkv, dpc = vjp(dm_ref[...])
        dq_ref[...] = dq
        dz_ref[...] = dz
        dkv_ref[...] += dkv
        dpc_ref[...] += dpc

    blk = pl.BlockSpec((tq, 512), lambda i: (i, 0))
    kvs = pl.BlockSpec(kvm.shape, lambda i: (0, 0))
    small = pl.BlockSpec((8, LANE), lambda i: (0, 0))
    return pl.pallas_call(
        body, name="attn_c_bwd", grid=(s // tq,),
        in_specs=[pl.BlockSpec((tq, 512), lambda i: (i, P_QC // 512)), pl.BlockSpec((tq, 512), lambda i: (i, P_ZC // 512)),
                  kvs, small, pl.BlockSpec((tq, 512), lambda i: (i, 3))],
        out_specs=[blk, blk, kvs, small],
        out_shape=[jax.ShapeDtypeStruct((s, 512), F32), jax.ShapeDtypeStruct((s, 512), F32),
                   jax.ShapeDtypeStruct(kvm.shape, F32), jax.ShapeDtypeStruct((8, LANE), F32)],
        compiler_params=_params(("arbitrary",)),
    )(proj, proj, kvm, pc, d_mixed)


def _pad_row(v, width=LANE):
    v = v.reshape(1, -1)
    return jnp.pad(v, ((0, 0), (0, width - v.shape[1])))


def _local_step(x, mem, target, norm_w, w_perm, conv_w, pa, pb, pc, mem_norm_w, w_kv, w_out):
    s = x.shape[0]
    cos2, sin2 = _rope_tables(s)
    hn = _rms_fwd(x, norm_w)
    proj = _matmul(hn, w_perm, "nn", F32, "mm_proj")
    cqkv = _conv_fwd(proj, conv_w)
    mixed_a = _delta_fwd(cqkv, proj, pa)
    mixed_b = _attn_b_fwd(proj, cos2, sin2, pb)
    kvm = _mem_kv_fwd(mem, mem_norm_w, w_kv)
    mixed_c = _attn_c_fwd(proj, kvm, pc)
    mixed = jnp.concatenate([mixed_a, mixed_b, mixed_c], axis=1)
    mo = _matmul(mixed, w_out, "nn", F32, "mm_out")
    dy, dyb, loss_parts = _loss_dy(x, mo, target)

    d_mixed = _matmul(dyb, w_out, "nt", F32, "mm_dmixed")
    g_w_out = _matmul(mixed, dyb, "tn", F32, "mm_gwout")
    d_qc, d_zc, d_kvm, d_pc = _attn_c_bwd(proj, kvm, pc, d_mixed)
    g_w_kv, g_mem_norm = _mem_kv_bwd(mem, mem_norm_w, w_kv, d_kvm)
    d_qb, d_zb, d_kb, d_vb, d_pb = _attn_b_bwd(proj, cos2, sin2, pb, d_mixed)
    d_c, d_za, d_gt, d_pa = _delta_bwd(cqkv, proj, pa, d_mixed)
    d_qkv, g_conv = _conv_bwd(proj, conv_w, d_c)
    d_proj = jnp.concatenate([d_qkv, d_za, d_qb, d_zb, d_qc, d_zc, d_kb, d_vb, d_gt,
                              jnp.zeros((s, P_WIDTH - P_GT - LANE), F32)], axis=1)
    d_hn = _matmul(d_proj, w_perm, "nt", F32, "mm_dhn")
    g_w_perm = _matmul(hn, d_proj, "tn", F32, "mm_gwin")
    g_x, g_norm = _rms_bwd(x, norm_w, d_hn, dy)
    return dict(loss_parts=loss_parts, g_x=g_x, g_norm=g_norm, g_w_perm=g_w_perm, g_conv=g_conv, d_pa=d_pa,
                d_pb=d_pb, d_pc=d_pc, g_mem_norm=g_mem_norm, g_w_kv=g_w_kv, g_w_out=g_w_out)


def _permute_cols(w):
    pad = jnp.zeros((w.shape[0], P_WIDTH - IN_WIDTH), w.dtype)
    return jnp.concatenate([w[:, :O_GT], w[:, O_QB:O_KB], w[:, O_ZB:O_QC], w[:, O_QC:O_ZC], w[:, O_ZC:IN_WIDTH],
                            w[:, O_KB:O_VB], w[:, O_VB:O_ZB], w[:, O_GT:O_QB], pad], axis=1)


def _unpermute_cols(g):
    return jnp.concatenate([g[:, :P_QB], g[:, P_GT:P_GT + 32], g[:, P_QB:P_ZB], g[:, P_KB:P_VB], g[:, P_VB:P_GT],
                            g[:, P_ZB:P_QC], g[:, P_QC:P_ZC], g[:, P_ZC:P_KB]], axis=1)


HBM = pl.BlockSpec(memory_space=pltpu.HBM)


def _place():
    x, y, c = lax.axis_index("x"), lax.axis_index("y"), lax.axis_index("c")
    chips = [(1 - x, y), (x, 1 - y), (1 - x, 1 - y)]
    return x, y, c, 2 * x + y, chips, [2 * cx + cy for cx, cy in chips]


def _remote(src, dst, send_sems, recv_sems, k, to):
    return pltpu.make_async_remote_copy(src_ref=src, dst_ref=dst, send_sem=send_sems.at[k], recv_sem=recv_sems.at[k],
                                        device_id=to, device_id_type=MESH)


def _half_rows(ref, c):
    half = ref.shape[-2] // 2
    return pl.ds(pl.multiple_of(c * half, 8), half)


def _all_gather_weights(w_in_b, w_out_b, w_kv_b, conv_b):
    bigs = (w_in_b, w_out_b, w_kv_b)
    n_big = len(bigs)

    def body(*refs):
        srcs, conv_src = refs[:n_big], refs[n_big]
        dsts, conv_dst = refs[n_big + 1:2 * n_big + 1], refs[2 * n_big + 1]
        send_sems, recv_sems, local_sems = refs[2 * n_big + 2:]
        x, y, c, me, chips, chip_ids = _place()
        sibling = (x, y, 1 - c)
        local = [pltpu.make_async_copy(src, dst.at[me], local_sems.at[a]) for a, (src, dst) in enumerate(zip(srcs, dsts))]
        local.append(pltpu.make_async_copy(conv_src, conv_dst.at[me], local_sems.at[n_big]))
        for cp in local:
            cp.start()
        sends = []
        for a, (src, dst) in enumerate(zip(srcs, dsts)):
            mine = _half_rows(src, c)
            for j, chip in enumerate(chips):
                sends.append(_remote(src.at[mine, :], dst.at[me, mine, :], send_sems, recv_sems, 6 * a + j, (*chip, c)))
        for j, chip in enumerate(chips):
            sends.append(_remote(conv_src, conv_dst.at[me], send_sems, recv_sems, 6 * n_big + j, (*chip, c)))
        for cp in sends:
            cp.start()
        passed = []
        for a, (src, dst) in enumerate(zip(srcs, dsts)):
            mine = _half_rows(src, c)
            for j, cid in enumerate(chip_ids):
                landed = dst.at[cid, mine, :]
                _remote(landed, landed, send_sems, recv_sems, 6 * a + j, sibling).wait_recv()
                cp = _remote(landed, landed, send_sems, recv_sems, 6 * a + 3 + j, sibling)
                cp.start()
                passed.append(cp)
        for a, (src, dst) in enumerate(zip(srcs, dsts)):
            other = _half_rows(src, 1 - c)
            for j, cid in enumerate(chip_ids):
                landed = dst.at[cid, other, :]
                _remote(landed, landed, send_sems, recv_sems, 6 * a + 3 + j, sibling).wait_recv()
        for j, cid in enumerate(chip_ids):
            _remote(conv_src, conv_dst.at[cid], send_sems, recv_sems, 6 * n_big + j, sibling).wait_recv()
        for cp in sends + passed:
            cp.wait_send()
        for cp in local:
            cp.wait()

    n_sem = 6 * n_big + 3
    return pl.pallas_call(
        body, name="all_gather_weights",
        out_shape=[jax.ShapeDtypeStruct((N_CHIPS,) + w.shape, w.dtype) for w in bigs + (conv_b,)],
        in_specs=[HBM] * (n_big + 1), out_specs=[HBM] * (n_big + 1),
        scratch_shapes=[pltpu.SemaphoreType.DMA((n_sem,)), pltpu.SemaphoreType.DMA((n_sem,)),
                        pltpu.SemaphoreType.DMA((n_big + 1,))],
    )(*bigs, conv_b)


def _pair_exchange(grads):
    n = len(grads)

    def body(*refs):
        srcs, owns, gots = refs[:n], refs[n:2 * n], refs[2 * n:3 * n]
        send_sems, recv_sems, local_sems = refs[3 * n:]
        x, y, c, _, _, _ = _place()
        copies = []
        for a in range(n):
            mine, other = _half_rows(srcs[a], c), _half_rows(srcs[a], 1 - c)
            keep = pltpu.make_async_copy(srcs[a].at[:, mine, :], owns[a], local_sems.at[a])
            keep.start()
            give = _remote(srcs[a].at[:, other, :], gots[a], send_sems, recv_sems, a, (x, y, 1 - c))
            give.start()
            copies += [keep, give]
        for cp in copies:
            cp.wait()

    halves = [jax.ShapeDtypeStruct((g.shape[0], g.shape[1] // 2, g.shape[2]), g.dtype) for g in grads]
    out = pl.pallas_call(
        body, name="grad_pair_exchange", out_shape=halves + halves,
        in_specs=[HBM] * n, out_specs=[HBM] * (2 * n),
        scratch_shapes=[pltpu.SemaphoreType.DMA((n,)), pltpu.SemaphoreType.DMA((n,)), pltpu.SemaphoreType.DMA((n,))],
    )(*grads)
    return out[:n], out[n:]


def _chip_exchange(halves):
    n = len(halves)

    def body(*refs):
        srcs, lands = refs[:n], refs[n:2 * n]
        send_sems, recv_sems, local_sems = refs[2 * n:]
        x, y, c, me, chips, chip_ids = _place()
        copies = []
        for a in range(n):
            keep = pltpu.make_async_copy(srcs[a].at[me], lands[a].at[me], local_sems.at[a])
            keep.start()
            copies.append(keep)
            for j, (chip, cid) in enumerate(zip(chips, chip_ids)):
                give = _remote(srcs[a].at[cid], lands[a].at[me], send_sems, recv_sems, 3 * a + j, (*chip, c))
                give.start()
                copies.append(give)
        for a in range(n):
            for j, cid in enumerate(chip_ids):
                _remote(srcs[a].at[cid], lands[a].at[cid], send_sems, recv_sems, 3 * a + j, (x, y, c)).wait_recv()
        for a in range(n):
            copies[4 * a].wait()
            for j in range(3):
                copies[4 * a + 1 + j].wait_send()

    return pl.pallas_call(
        body, name="grad_chip_exchange", out_shape=[jax.ShapeDtypeStruct(h.shape, h.dtype) for h in halves],
        in_specs=[HBM] * n, out_specs=[HBM] * n,
        scratch_shapes=[pltpu.SemaphoreType.DMA((3 * n,)), pltpu.SemaphoreType.DMA((3 * n,)),
                        pltpu.SemaphoreType.DMA((n,))],
    )(*halves)


def _pair_gather(halves):
    n = len(halves)

    def body(*refs):
        srcs, fulls = refs[:n], refs[n:2 * n]
        send_sems, recv_sems, local_sems = refs[2 * n:]
        x, y, c, _, _, _ = _place()
        copies = []
        for a in range(n):
            mine = _half_rows(fulls[a], c)
            keep = pltpu.make_async_copy(srcs[a], fulls[a].at[mine, :], local_sems.at[a])
            keep.start()
            give = _remote(srcs[a], fulls[a].at[mine, :], send_sems, recv_sems, a, (x, y, 1 - c))
            give.start()
            copies += [keep, give]
        for a in range(n):
            other = _half_rows(fulls[a], 1 - c)
            copies[2 * a].wait()
            copies[2 * a + 1].wait_send()
            _remote(srcs[a], fulls[a].at[other, :], send_sems, recv_sems, a, (x, y, 1 - c)).wait_recv()

    return pl.pallas_call(
        body, name="grad_pair_gather",
        out_shape=[jax.ShapeDtypeStruct((2 * h.shape[0], h.shape[1]), h.dtype) for h in halves],
        in_specs=[HBM] * n, out_specs=[HBM] * n,
        scratch_shapes=[pltpu.SemaphoreType.DMA((n,)), pltpu.SemaphoreType.DMA((n,)), pltpu.SemaphoreType.DMA((n,))],
    )(*halves)


def _all_reduce_small(p):
    n_dev = 8

    def body(p_ref, o_ref, land, send_sems, recv_sems):
        x, y, c = lax.axis_index("x"), lax.axis_index("y"), lax.axis_index("c")
        me = 4 * x + 2 * y + c
        land[me] = p_ref[...]
        sends = []
        for k in range(1, n_dev):
            fx, fy, fc = (k >> 2) & 1, (k >> 1) & 1, k & 1
            to = (x ^ fx, y ^ fy, c ^ fc)
            cp = _remote(p_ref, land.at[me], send_sems, recv_sems, k - 1, to)
            cp.start()
            sends.append(cp)
        for k in range(1, n_dev):
            _remote(p_ref, land.at[me ^ k], send_sems, recv_sems, k - 1, (x, y, c)).wait_recv()
        total = land[0]
        for d in range(1, n_dev):
            total = total + land[d]
        o_ref[...] = total
        for cp in sends:
            cp.wait_send()

    vm = pl.BlockSpec(memory_space=pltpu.VMEM)
    return pl.pallas_call(
        body, name="all_reduce_small", out_shape=jax.ShapeDtypeStruct(p.shape, p.dtype), in_specs=[vm], out_specs=vm,
        scratch_shapes=[pltpu.VMEM((n_dev,) + p.shape, p.dtype), pltpu.SemaphoreType.DMA((n_dev - 1,)),
                        pltpu.SemaphoreType.DMA((n_dev - 1,))],
    )(p)


def _row_tile(rows, cap=256):
    return cap if rows % cap == 0 else rows


def _add2(a, b, name):
    n, r, c = a.shape
    tr = _row_tile(r)

    def body(a_ref, b_ref, o_ref):
        o_ref[...] = a_ref[...] + b_ref[...]

    blk = pl.BlockSpec((None, tr, c), lambda i, j: (i, j, 0))
    return pl.pallas_call(body, name=name, grid=(n, r // tr), in_specs=[blk, blk], out_specs=blk,
                          out_shape=jax.ShapeDtypeStruct(a.shape, a.dtype),
                          compiler_params=_params(("parallel", "parallel")))(a, b)


def _sum_slots(land, name):
    n, r, c = land.shape
    tr = _row_tile(r)

    def body(l_ref, o_ref):
        total = l_ref[0]
        for j in range(1, n):
            total = total + l_ref[j]
        o_ref[...] = total

    return pl.pallas_call(body, name=name, grid=(r // tr,), in_specs=[pl.BlockSpec((n, tr, c), lambda i: (0, i, 0))],
                          out_specs=pl.BlockSpec((tr, c), lambda i: (i, 0)),
                          out_shape=jax.ShapeDtypeStruct((r, c), land.dtype), compiler_params=_params(("parallel",)))(land)


def _adamw(w, g, m, v, name):
    r, c = w.shape
    tr = _row_tile(r)

    def body(w_ref, g_ref, m_ref, v_ref, d_ref, mo_ref, vo_ref):
        g_ = g_ref[...]
        m2 = ADAM_B1 * m_ref[...] + (1.0 - ADAM_B1) * g_
        v2 = ADAM_B2 * v_ref[...] + (1.0 - ADAM_B2) * jnp.square(g_)
        m_hat = m2 / (1.0 - ADAM_B1 ** ADAM_STEP)
        v_hat = v2 / (1.0 - ADAM_B2 ** ADAM_STEP)
        d_ref[...] = -ADAM_LR * (m_hat / (jnp.sqrt(v_hat) + ADAM_EPS) + ADAM_WD * w_ref[...])
        mo_ref[...] = m2
        vo_ref[...] = v2

    blk = pl.BlockSpec((tr, c), lambda i: (i, 0))
    return pl.pallas_call(body, name=name, grid=(r // tr,), in_specs=[blk] * 4, out_specs=[blk] * 3,
                          out_shape=[jax.ShapeDtypeStruct(w.shape, F32)] * 3, compiler_params=_params(("parallel",)))(w, g, m, v)


SMALL_NAMES = ("norm_w", "mem_norm_w", "o_norm_a", "q_norm_c", "k_norm_c", "q_norm_b", "k_norm_b",
               "a_log_fwd", "a_log_bwd", "dt_bias_fwd", "dt_bias_bwd", "sink_b")
SMALL_SIZES = (2048, 2048, 128, 128, 128, 64, 64, 8, 8, 8, 8, 8)
SMALL_LOSS = sum(SMALL_SIZES)
SMALL_CONV = 5120
SMALL_TOTAL = SMALL_CONV + CONV_K * 3 * A_WIDTH
SMALL_ROWS = SMALL_TOTAL // LANE


def _pack_small(parts, extra=None, conv=None):
    vec = [parts[n].reshape(-1) for n in SMALL_NAMES]
    vec.append(jnp.zeros((1,), F32) if extra is None else extra.reshape(1))
    vec.append(jnp.zeros((SMALL_CONV - SMALL_LOSS - 1,), F32))
    vec.append(jnp.zeros((SMALL_TOTAL - SMALL_CONV,), F32) if conv is None else conv.reshape(-1))
    return jnp.concatenate(vec).reshape(SMALL_ROWS, LANE)


def _unpack_small(packed):
    flat = packed.reshape(-1)
    out, off = {}, 0
    for n, size in zip(SMALL_NAMES, SMALL_SIZES):
        out[n] = flat[off:off + size].reshape(1, size)
        off += size
    return out


WEIGHT_ORDER = ("norm_w", "w_in", "conv_w_a", "a_log_fwd", "a_log_bwd", "dt_bias_fwd", "dt_bias_bwd", "o_norm_a",
                "q_norm_b", "k_norm_b", "sink_b", "mem_norm_w", "w_mem_kv", "q_norm_c", "k_norm_c", "w_out")


def kernel(x, mem, norm_w, w_in, conv_w_a, a_log_fwd, a_log_bwd, dt_bias_fwd, dt_bias_bwd, o_norm_a, q_norm_b, k_norm_b, sink_b, mem_norm_w, w_mem_kv, q_norm_c, k_norm_c, w_out, loss_target, m_norm_w, m_w_in, m_conv_w_a, m_a_log_fwd, m_a_log_bwd, m_dt_bias_fwd, m_dt_bias_bwd, m_o_norm_a, m_q_norm_b, m_k_norm_b, m_sink_b, m_mem_norm_w, m_w_mem_kv, m_q_norm_c, m_k_norm_c, m_w_out, v_norm_w, v_w_in, v_conv_w_a, v_a_log_fwd, v_a_log_bwd, v_dt_bias_fwd, v_dt_bias_bwd, v_o_norm_a, v_q_norm_b, v_k_norm_b, v_sink_b, v_mem_norm_w, v_w_mem_kv, v_q_norm_c, v_k_norm_c, v_w_out):
    weights = dict(norm_w=norm_w, w_in=w_in, conv_w_a=conv_w_a, a_log_fwd=a_log_fwd, a_log_bwd=a_log_bwd,
                   dt_bias_fwd=dt_bias_fwd, dt_bias_bwd=dt_bias_bwd, o_norm_a=o_norm_a, q_norm_b=q_norm_b,
                   k_norm_b=k_norm_b, sink_b=sink_b, mem_norm_w=mem_norm_w, w_mem_kv=w_mem_kv, q_norm_c=q_norm_c,
                   k_norm_c=k_norm_c, w_out=w_out)
    mom1 = dict(norm_w=m_norm_w, w_in=m_w_in, conv_w_a=m_conv_w_a, a_log_fwd=m_a_log_fwd, a_log_bwd=m_a_log_bwd,
                dt_bias_fwd=m_dt_bias_fwd, dt_bias_bwd=m_dt_bias_bwd, o_norm_a=m_o_norm_a, q_norm_b=m_q_norm_b,
                k_norm_b=m_k_norm_b, sink_b=m_sink_b, mem_norm_w=m_mem_norm_w, w_mem_kv=m_w_mem_kv,
                q_norm_c=m_q_norm_c, k_norm_c=m_k_norm_c, w_out=m_w_out)
    mom2 = dict(norm_w=v_norm_w, w_in=v_w_in, conv_w_a=v_conv_w_a, a_log_fwd=v_a_log_fwd, a_log_bwd=v_a_log_bwd,
                dt_bias_fwd=v_dt_bias_fwd, dt_bias_bwd=v_dt_bias_bwd, o_norm_a=v_o_norm_a, q_norm_b=v_q_norm_b,
                k_norm_b=v_k_norm_b, sink_b=v_sink_b, mem_norm_w=v_mem_norm_w, w_mem_kv=v_w_mem_kv,
                q_norm_c=v_q_norm_c, k_norm_c=v_k_norm_c, w_out=v_w_out)
    chip = 2 * lax.axis_index("x") + lax.axis_index("y")

    w_in4, w_out4, w_kv4, conv4 = _all_gather_weights(w_in[0].astype(BF16), w_out[0].astype(BF16),
                                                      w_mem_kv[0].astype(BF16), conv_w_a[0])
    w_perm = _permute_cols(jnp.transpose(w_in4, (1, 0, 2)).reshape(D_MODEL, IN_WIDTH))
    w_out_full = w_out4.reshape(D_MODEL, D_MODEL)
    w_kv_full = w_kv4.reshape(D_MODEL, 2 * C_HEADS * C_DIM)
    conv_full = jnp.transpose(conv4, (1, 0, 2)).reshape(CONV_K, 3 * A_WIDTH)
    pa = jnp.concatenate([_pad_row(a_log_fwd), _pad_row(a_log_bwd), _pad_row(dt_bias_fwd), _pad_row(dt_bias_bwd),
                          _pad_row(o_norm_a), jnp.zeros((3, LANE), F32)], axis=0)
    pb = jnp.concatenate([_pad_row(q_norm_b), _pad_row(k_norm_b), _pad_row(sink_b), jnp.zeros((5, LANE), F32)], axis=0)
    pc = jnp.concatenate([_pad_row(q_norm_c), _pad_row(k_norm_c), jnp.zeros((6, LANE), F32)], axis=0)

    r = _local_step(x[0], mem[0], loss_target[0], norm_w, w_perm, conv_full, pa, pb, pc, mem_norm_w, w_kv_full,
                    w_out_full)

    g_in4 = jnp.transpose(_unpermute_cols(r["g_w_perm"]).reshape(D_MODEL, N_CHIPS, W_IN_BLOCK), (1, 0, 2))
    g_out4 = r["g_w_out"].reshape(N_CHIPS, D_MODEL // N_CHIPS, D_MODEL)
    g_kv4 = r["g_w_kv"].reshape(N_CHIPS, D_MODEL // N_CHIPS, 2 * C_HEADS * C_DIM)
    own, got = _pair_exchange([g_in4, g_out4, g_kv4])
    pair = [_add2(a, b, "grad_pair_sum_%d" % i) for i, (a, b) in enumerate(zip(own, got))]
    lands = _chip_exchange(pair)
    reduced = [_sum_slots(l, "grad_chip_sum_%d" % i) for i, l in enumerate(lands)]
    g_w_in, g_w_out, g_w_kv = _pair_gather(reduced)

    d_pa, d_pb, d_pc = r["d_pa"], r["d_pb"], r["d_pc"]
    small_g = dict(norm_w=r["g_norm"], mem_norm_w=r["g_mem_norm"], o_norm_a=d_pa[4], q_norm_c=d_pc[0], k_norm_c=d_pc[1],
                   q_norm_b=d_pb[0, :B_DIM], k_norm_b=d_pb[1, :B_DIM], a_log_fwd=d_pa[0, :A_HEADS],
                   a_log_bwd=d_pa[1, :A_HEADS], dt_bias_fwd=d_pa[2, :A_HEADS], dt_bias_bwd=d_pa[3, :A_HEADS],
                   sink_b=d_pb[2, :B_HEADS])
    packed = _all_reduce_small(_pack_small(small_g, jnp.sum(r["loss_parts"][:, 0, 0]), r["g_conv"]))
    flat = packed.reshape(-1)
    loss = flat[SMALL_LOSS]
    conv_sum = flat[SMALL_CONV:].reshape(CONV_K, 3 * A_WIDTH)
    conv_cols = 3 * A_WIDTH // N_CHIPS
    g_conv = lax.dynamic_slice(conv_sum, (0, chip * conv_cols), (CONV_K, conv_cols))

    grads = _unpack_small(packed)
    grads.update(w_in=g_w_in, w_mem_kv=g_w_kv, w_out=g_w_out, conv_w_a=g_conv)
    delta, new_m, new_v = {}, {}, {}
    for n in ("w_in", "w_mem_kv", "w_out", "conv_w_a"):
        delta[n], new_m[n], new_v[n] = _adamw(weights[n][0], grads[n], mom1[n][0], mom2[n][0], "adamw_" + n)
    d_s, m_s, v_s = _adamw(_pack_small(weights), packed, _pack_small(mom1), _pack_small(mom2), "adamw_small")
    d_s, m_s, v_s = _unpack_small(d_s), _unpack_small(m_s), _unpack_small(v_s)
    for n in SMALL_NAMES:
        delta[n], new_m[n], new_v[n] = d_s[n], m_s[n], v_s[n]

    def shaped(tree):
        return [tree[n].reshape(weights[n].shape) for n in WEIGHT_ORDER]

    return (loss, r["g_x"].reshape(x.shape), *shaped(grads), *shaped(delta), *shaped(new_m), *shaped(new_v))
```

```python
import functools

import jax
import jax.numpy as jnp
from jax import lax
from jax.experimental import pallas as pl
from jax.experimental.pallas import tpu as pltpu

F32 = jnp.float32
BF16 = jnp.bfloat16
HI = lax.Precision.HIGHEST
MESH = pl.DeviceIdType.MESH

D_MODEL = 2048
A_WIDTH = 1024
A_HEADS = 8
A_DIM = 128
CONV_K = 5
CHUNK = 64
B_HEADS = 8
B_KV = 2
B_DIM = 64
WINDOW = 128
C_HEADS = 4
C_DIM = 128
MEM_LEN = 256
ROPE_THETA = 10000.0
EPS = 1e-6
IN_WIDTH = 6432
N_CHIPS = 4
W_IN_BLOCK = IN_WIDTH // N_CHIPS

LANE = 128
P_QA, P_KA, P_VA, P_ZA = 0, 1024, 2048, 3072
P_QB, P_ZB, P_QC, P_ZC = 4096, 4608, 5120, 5632
P_KB, P_VB, P_GT = 6144, 6272, 6400
P_WIDTH = 6656
O_GT, O_QB, O_KB, O_VB, O_ZB, O_QC, O_ZC = 4096, 4128, 4640, 4768, 4896, 5408, 5920

ADAM_LR, ADAM_B1, ADAM_B2, ADAM_EPS, ADAM_WD, ADAM_STEP = 0.001, 0.9, 0.999, 1e-08, 0.01, 10

VMEM_LIMIT = 56 * 1024 * 1024


def _params(sem=None):
    return pltpu.CompilerParams(dimension_semantics=sem, vmem_limit_bytes=VMEM_LIMIT)


def _dot(a, b, dims=(((1,), (0,)), ((), ())), precision=HI):
    return lax.dot_general(a, b, dims, precision=precision, preferred_element_type=F32)


def _dot_nt(a, b, precision=HI):
    return _dot(a, b, (((1,), (1,)), ((), ())), precision)


def _dot_tn(a, b, precision=HI):
    return _dot(a, b, (((0,), (0,)), ((), ())), precision)


_NN = (((1,), (0,)), ((), ()))
_NT = (((1,), (1,)), ((), ()))
_TN = (((0,), (0,)), ((), ()))


def _bdot(a, b, dims):
    return lax.dot_general(a.astype(BF16), b.astype(BF16), dims, preferred_element_type=F32)


@jax.custom_vjp
def _mm(a, b):
    return _bdot(a, b, _NN)


_mm.defvjp(lambda a, b: (_bdot(a, b, _NN), (a, b)),
           lambda res, ct: (_bdot(ct, res[1], _NT), _bdot(res[0], ct, _TN)))


@jax.custom_vjp
def _mm_nt(a, b):
    return _bdot(a, b, _NT)


_mm_nt.defvjp(lambda a, b: (_bdot(a, b, _NT), (a, b)),
              lambda res, ct: (_bdot(ct, res[1], _NN), _bdot(ct, res[0], _TN)))


@jax.custom_vjp
def _mm_tn(a, b):
    return _bdot(a, b, _TN)


_mm_tn.defvjp(lambda a, b: (_bdot(a, b, _TN), (a, b)),
              lambda res, ct: (_bdot(res[1], ct, _NT), _bdot(res[0], ct, _NN)))


def _rms(t, w):
    return t * lax.rsqrt(jnp.mean(t * t, axis=-1, keepdims=True) + EPS) * w


def _l2(t):
    return t * lax.rsqrt(jnp.sum(t * t, axis=-1, keepdims=True) + EPS)


def _silu(t):
    return t * jax.nn.sigmoid(t)


def _softplus(t):
    return jnp.maximum(t, 0.0) + jnp.log1p(jnp.exp(-jnp.abs(t)))


def _matmul(a, b, mode, out_dtype, name, tm=512, tn=512, tk=512):
    (m, k) = a.shape[::-1] if mode == "tn" else a.shape
    n = b.shape[0] if mode == "nt" else b.shape[1]
    tm, tn, tk = min(tm, m), min(tn, n), min(tk, k)
    assert m % tm == 0 and n % tn == 0 and k % tk == 0, (m, n, k, tm, tn, tk)
    if mode == "nn":
        a_spec = pl.BlockSpec((tm, tk), lambda i, j, kk: (i, kk))
        b_spec = pl.BlockSpec((tk, tn), lambda i, j, kk: (kk, j))
        dims = (((1,), (0,)), ((), ()))
    elif mode == "nt":
        a_spec = pl.BlockSpec((tm, tk), lambda i, j, kk: (i, kk))
        b_spec = pl.BlockSpec((tn, tk), lambda i, j, kk: (j, kk))
        dims = (((1,), (1,)), ((), ()))
    else:
        a_spec = pl.BlockSpec((tk, tm), lambda i, j, kk: (kk, i))
        b_spec = pl.BlockSpec((tk, tn), lambda i, j, kk: (kk, j))
        dims = (((0,), (0,)), ((), ()))
    nk = k // tk

    def body_one(a_ref, b_ref, o_ref):
        o_ref[...] = _bdot(a_ref[...], b_ref[...], dims).astype(out_dtype)

    def body_acc(a_ref, b_ref, o_ref, acc_ref):
        kk = pl.program_id(2)

        @pl.when(kk == 0)
        def _():
            acc_ref[...] = jnp.zeros_like(acc_ref)

        acc_ref[...] += _bdot(a_ref[...], b_ref[...], dims)

        @pl.when(kk == nk - 1)
        def _():
            o_ref[...] = acc_ref[...].astype(out_dtype)

    return pl.pallas_call(
        body_one if nk == 1 else body_acc, name=name, grid=(m // tm, n // tn, nk),
        in_specs=[a_spec, b_spec], out_specs=pl.BlockSpec((tm, tn), lambda i, j, kk: (i, j)),
        out_shape=jax.ShapeDtypeStruct((m, n), out_dtype),
        scratch_shapes=[] if nk == 1 else [pltpu.VMEM((tm, tn), F32)],
        compiler_params=_params(("parallel", "parallel", "arbitrary")),
    )(a, b)


def _rms_fwd(x, w, tr=256):
    s, d = x.shape

    def body(x_ref, w_ref, o_ref):
        o_ref[...] = _rms(x_ref[...], w_ref[...]).astype(BF16)

    return pl.pallas_call(
        body, name="rms_fwd", grid=(s // tr,),
        in_specs=[pl.BlockSpec((tr, d), lambda i: (i, 0)), pl.BlockSpec((1, d), lambda i: (0, 0))],
        out_specs=pl.BlockSpec((tr, d), lambda i: (i, 0)),
        out_shape=jax.ShapeDtypeStruct((s, d), BF16), compiler_params=_params(("parallel",)),
    )(x, w)


def _rms_bwd(x, w, d_hn, dy, tr=256):
    s, d = x.shape

    def body(x_ref, w_ref, g_ref, dy_ref, gx_ref, gw_ref):
        _, vjp = jax.vjp(_rms, x_ref[...], w_ref[...])
        dx, dw = vjp(g_ref[...])
        gx_ref[...] = dy_ref[...] + dx

        @pl.when(pl.program_id(0) == 0)
        def _():
            gw_ref[...] = jnp.zeros_like(gw_ref)

        gw_ref[...] += dw

    row = pl.BlockSpec((tr, d), lambda i: (i, 0))
    vec = pl.BlockSpec((1, d), lambda i: (0, 0))
    return pl.pallas_call(
        body, name="rms_bwd", grid=(s // tr,), in_specs=[row, vec, row, row], out_specs=[row, vec],
        out_shape=[jax.ShapeDtypeStruct((s, d), F32), jax.ShapeDtypeStruct((1, d), F32)],
        compiler_params=_params(("arbitrary",)),
    )(x, w, d_hn, dy)


def _loss_dy(x, mo, target, tr=256):
    s, d = x.shape
    nt = s // tr

    def body(x_ref, mo_ref, t_ref, dy_ref, dyb_ref, l_ref):
        err = x_ref[...] + mo_ref[...] - t_ref[...]
        dy = err * (1.0 / d)
        dy_ref[...] = dy
        dyb_ref[...] = dy.astype(BF16)
        l_ref[...] = jnp.full(l_ref.shape, 0.5 * jnp.sum(jnp.sum(err * err, axis=1, keepdims=True) * (1.0 / d)), F32)

    row = pl.BlockSpec((tr, d), lambda i: (i, 0))
    return pl.pallas_call(
        body, name="loss_dy", grid=(nt,), in_specs=[row, row, row],
        out_specs=[row, row, pl.BlockSpec((1, 8, LANE), lambda i: (i, 0, 0))],
        out_shape=[jax.ShapeDtypeStruct((s, d), F32), jax.ShapeDtypeStruct((s, d), BF16),
                   jax.ShapeDtypeStruct((nt, 8, LANE), F32)],
        compiler_params=_params(("parallel",)),
    )(x, mo, target)


def _shift_rows(t, s):
    if s == 0:
        return t
    n = t.shape[0]
    rolled = pltpu.roll(t, (-s) % n, axis=0)
    idx = lax.broadcasted_iota(jnp.int32, t.shape, 0) + s
    return jnp.where((idx >= 0) & (idx < n), rolled, 0.0)


def _conv_fwd(proj, conv_w):
    s = proj.shape[0]
    nblk = 3 * A_WIDTH // LANE

    def body(x_ref, w_ref, o_ref):
        x = x_ref[...]
        acc = jnp.zeros_like(x)
        for j in range(CONV_K):
            acc = acc + w_ref[j:j + 1, :] * _shift_rows(x, j - CONV_K // 2)
        o_ref[...] = acc

    return pl.pallas_call(
        body, name="conv_fwd", grid=(nblk,),
        in_specs=[pl.BlockSpec((s, LANE), lambda i: (0, i)), pl.BlockSpec((CONV_K, LANE), lambda i: (0, i))],
        out_specs=pl.BlockSpec((None, s, LANE), lambda i: (i // A_HEADS, 0, i % A_HEADS)),
        out_shape=jax.ShapeDtypeStruct((3, s, A_WIDTH), F32), compiler_params=_params(("parallel",)),
    )(proj, conv_w)


def _conv_bwd(proj, conv_w, d_c):
    s = proj.shape[0]
    nblk = 3 * A_WIDTH // LANE

    def body(x_ref, w_ref, g_ref, dx_ref, dw_ref):
        x, g = x_ref[...], g_ref[...]
        acc = jnp.zeros_like(x)
        for j in range(CONV_K):
            off = j - CONV_K // 2
            acc = acc + w_ref[j:j + 1, :] * _shift_rows(g, -off)
            dw_ref[j:j + 1, :] = jnp.sum(_shift_rows(x, off) * g, axis=0, keepdims=True)
        dx_ref[...] = acc

    col = pl.BlockSpec((s, LANE), lambda i: (0, i))
    wsp = pl.BlockSpec((CONV_K, LANE), lambda i: (0, i))
    dsp = pl.BlockSpec((None, s, LANE), lambda i: (i // A_HEADS, 0, i % A_HEADS))
    return pl.pallas_call(
        body, name="conv_bwd", grid=(nblk,), in_specs=[col, wsp, dsp], out_specs=[col, wsp],
        out_shape=[jax.ShapeDtypeStruct((s, 3 * A_WIDTH), F32), jax.ShapeDtypeStruct((CONV_K, 3 * A_WIDTH), F32)],
        compiler_params=_params(("parallel",)),
    )(proj, conv_w, d_c)


def _a_chunk(st, cq, ck, cv, gt, pa, h, reverse):
    c = CHUNK
    ii = lax.broadcasted_iota(jnp.int32, (c, c), 0)
    jj = lax.broadcasted_iota(jnp.int32, (c, c), 1)
    incl = (ii <= jj) if reverse else (ii >= jj)
    strict = (ii < jj) if reverse else (ii > jj)
    tri = incl.astype(F32)
    eye = (ii == jj).astype(F32)
    lane = lax.broadcasted_iota(jnp.int32, (1, LANE), 1)
    sel_row = lax.broadcasted_iota(jnp.int32, (LANE, LANE), 0)
    r_alog, r_dt = (1, 3) if reverse else (0, 2)
    o_alpha, o_beta = (8, 24) if reverse else (0, 16)
    a_log = jnp.sum(jnp.where(lane == h, pa[r_alog:r_alog + 1, :], 0.0), axis=1, keepdims=True)
    dt_b = jnp.sum(jnp.where(lane == h, pa[r_dt:r_dt + 1, :], 0.0), axis=1, keepdims=True)
    alpha = _dot(gt, (sel_row == h + o_alpha).astype(F32))
    beta_raw = _dot(gt, (sel_row == h + o_beta).astype(F32))
    gb = -jnp.exp(a_log) * _softplus(alpha + dt_b)
    bb = jax.nn.sigmoid(beta_raw)
    q = _l2(_silu(cq)) * (A_DIM ** -0.5)
    k = _l2(_silu(ck))
    v = _silu(cv)

    gc = _dot(tri, gb)
    tot = _dot(jnp.ones((c, c), F32), gb)
    m1 = gc[:, :c]
    decay = jnp.where(incl, jnp.exp(jnp.where(incl, m1 - m1.T, 0.0)), 0.0)
    kb = k * bb
    vb = v * bb
    a = -jnp.where(strict, _mm_nt(kb, k) * decay, 0.0)
    tinv = eye + a
    p = a
    for _ in range(5):
        p = _mm(p, p)
        tinv = tinv + _mm(tinv, p)
    eg = jnp.exp(gc)
    u = _mm(tinv, vb)
    w = _mm(tinv, kb * eg)
    qk = _mm_nt(q, k) * decay
    v_new = u - _mm(w, st)
    o = _mm(q * eg, st) + _mm(qk, v_new)
    st_new = st * jnp.exp(tot[:1, :]) + _mm_tn(k * jnp.exp(tot - gc), v_new)
    return st_new, o


def _a_final(o, za, pa):
    return _rms(o, pa[4:5, :]) * _silu(za)


def _a_scan(reverse, h, nchunk, c_ref, gt_ref, pa, o_ref, s_ref, accumulate):
    def step(n, st):
        i = (nchunk - 1 - n) if reverse else n
        sl = pl.ds(pl.multiple_of(i * CHUNK, CHUNK), CHUNK)
        st_new, o = _a_chunk(st, c_ref[0, sl, :], c_ref[1, sl, :], c_ref[2, sl, :], gt_ref[sl, :], pa, h, reverse)
        if s_ref is not None:
            s_ref[i] = st
        if accumulate:
            o_ref[sl, :] += o
        else:
            o_ref[sl, :] = o
        return st_new

    lax.fori_loop(0, nchunk, step, jnp.zeros((A_DIM, A_DIM), F32))


def _delta_fwd(cqkv, proj, pa):
    s = cqkv.shape[1]
    nchunk = s // CHUNK

    def body(c_ref, gt_ref, za_ref, pa_ref, out_ref, o_acc):
        h = pl.program_id(0)
        pa_v = pa_ref[...]
        _a_scan(False, h, nchunk, c_ref, gt_ref, pa_v, o_acc, None, False)
        _a_scan(True, h, nchunk, c_ref, gt_ref, pa_v, o_acc, None, True)
        out_ref[...] = _a_final(o_acc[...], za_ref[...], pa_v).astype(BF16)

    def col(base):
        return pl.BlockSpec((s, LANE), lambda h: (0, base + h))

    return pl.pallas_call(
        body, name="delta_fwd", grid=(A_HEADS,),
        in_specs=[pl.BlockSpec((3, s, LANE), lambda h: (0, 0, h)), pl.BlockSpec((s, LANE), lambda h: (0, P_GT // LANE)),
                  col(P_ZA // LANE), pl.BlockSpec((8, LANE), lambda h: (0, 0))],
        out_specs=col(0), out_shape=jax.ShapeDtypeStruct((s, A_WIDTH), BF16),
        scratch_shapes=[pltpu.VMEM((s, A_DIM), F32)], compiler_params=_params(("parallel",)),
    )(cqkv, proj, proj, pa)


def _delta_bwd(cqkv, proj, pa, d_mixed):
    s = cqkv.shape[1]
    nchunk = s // CHUNK

    def body(c_ref, gt_ref, za_ref, pa_ref, dm_ref,
             dc_ref, dza_ref, dgt_ref, dpa_ref, sf_ref, sb_ref, o_acc, do_ref):
        h = pl.program_id(0)
        pa_v = pa_ref[...]

        @pl.when(h == 0)
        def _():
            dgt_ref[...] = jnp.zeros_like(dgt_ref)
            dpa_ref[...] = jnp.zeros_like(dpa_ref)

        _a_scan(False, h, nchunk, c_ref, gt_ref, pa_v, o_acc, sf_ref, False)
        _a_scan(True, h, nchunk, c_ref, gt_ref, pa_v, o_acc, sb_ref, True)
        _, vjp = jax.vjp(_a_final, o_acc[...], za_ref[...], pa_v)
        d_o, d_za, dpa0 = vjp(dm_ref[...])
        do_ref[...] = d_o
        dza_ref[...] = d_za

        def scan_bwd(reverse, s_ref, accumulate, dpa_in):
            def step(n, carry):
                d_st, dpa = carry
                i = n if reverse else (nchunk - 1 - n)
                sl = pl.ds(pl.multiple_of(i * CHUNK, CHUNK), CHUNK)

                def f(st, cq, ck, cv, gt, pa_):
                    return _a_chunk(st, cq, ck, cv, gt, pa_, h, reverse)

                _, vjp_c = jax.vjp(f, s_ref[i], c_ref[0, sl, :], c_ref[1, sl, :], c_ref[2, sl, :], gt_ref[sl, :], pa_v)
                d_prev, dcq, dck, dcv, dgt, dpa_i = vjp_c((d_st, do_ref[sl, :]))
                for r, dc in enumerate((dcq, dck, dcv)):
                    if accumulate:
                        dc_ref[r, sl, :] += dc
                    else:
                        dc_ref[r, sl, :] = dc
                dgt_ref[sl, :] += dgt
                return d_prev, dpa + dpa_i

            _, dpa_out = lax.fori_loop(0, nchunk, step, (jnp.zeros((A_DIM, A_DIM), F32), dpa_in))
            return dpa_out

        dpa1 = scan_bwd(False, sf_ref, False, dpa0)
        dpa2 = scan_bwd(True, sb_ref, True, dpa1)
        dpa_ref[...] += dpa2

    def col(base):
        return pl.BlockSpec((s, LANE), lambda h: (0, base + h))

    fixed = pl.BlockSpec((s, LANE), lambda h: (0, 0))
    small = pl.BlockSpec((8, LANE), lambda h: (0, 0))
    trio = pl.BlockSpec((3, s, LANE), lambda h: (0, 0, h))
    return pl.pallas_call(
        body, name="delta_bwd", grid=(A_HEADS,),
        in_specs=[trio, pl.BlockSpec((s, LANE), lambda h: (0, P_GT // LANE)), col(P_ZA // LANE), small, col(0)],
        out_specs=[trio, col(0), fixed, small],
        out_shape=[jax.ShapeDtypeStruct((3, s, A_WIDTH), F32), jax.ShapeDtypeStruct((s, A_WIDTH), F32),
                   jax.ShapeDtypeStruct((s, LANE), F32), jax.ShapeDtypeStruct((8, LANE), F32)],
        scratch_shapes=[pltpu.VMEM((nchunk, A_DIM, A_DIM), F32), pltpu.VMEM((nchunk, A_DIM, A_DIM), F32),
                        pltpu.VMEM((s, A_DIM), F32), pltpu.VMEM((s, A_DIM), F32)],
        compiler_params=_params(("arbitrary",)),
    )(cqkv, proj, proj, pa, d_mixed)


def _rope_tables(s):
    inv = ROPE_THETA ** (-jnp.arange(0, B_DIM, 2, dtype=F32) / B_DIM)
    ang = jnp.arange(s, dtype=F32)[:, None] * inv[None, :]
    cos, sin = jnp.cos(ang), jnp.sin(ang)
    return jnp.concatenate([cos, cos], axis=1), jnp.concatenate([-sin, sin], axis=1)


def _b_block(q_t, z_t, k3, v3, cos_q, sin_q, cos_k, sin_k, pb, n, nb):
    w = WINDOW
    r = lax.broadcasted_iota(jnp.int32, (B_DIM, B_DIM), 0)
    c = lax.broadcasted_iota(jnp.int32, (B_DIM, B_DIM), 1)
    swap = (r == (c + B_DIM // 2) % B_DIM).astype(F32)
    qi = lax.broadcasted_iota(jnp.int32, (w, 3 * w), 0)
    kj = lax.broadcasted_iota(jnp.int32, (w, 3 * w), 1)
    kpos = kj + (n - 1) * w
    mask = (jnp.abs(kj - w - qi) <= w) & (kpos >= 0) & (kpos < nb * w)
    lane = lax.broadcasted_iota(jnp.int32, (1, LANE), 1)
    qn, kn = pb[0:1, :B_DIM], pb[1:2, :B_DIM]
    outs = []
    for hk in range(B_KV):
        k = _rms(k3[:, hk * B_DIM:(hk + 1) * B_DIM], kn)
        k = k * cos_k + _dot(k, swap) * sin_k
        v = v3[:, hk * B_DIM:(hk + 1) * B_DIM]
        for g in range(B_HEADS // B_KV):
            hq = hk * (B_HEADS // B_KV) + g
            q = _rms(q_t[:, hq * B_DIM:(hq + 1) * B_DIM], qn)
            q = q * cos_q + _dot(q, swap) * sin_q
            sink = jnp.sum(jnp.where(lane == hq, pb[2:3, :], 0.0), axis=1, keepdims=True)
            s = _mm_nt(q, k) * (B_DIM ** -0.5)
            s = jnp.where(mask, s, -jnp.inf)
            m = jnp.maximum(jnp.max(s, axis=1, keepdims=True), sink)
            p = jnp.exp(s - m)
            p = p / (jnp.sum(p, axis=1, keepdims=True) + jnp.exp(sink - m))
            outs.append(_mm(p, v))
    return jnp.concatenate(outs, axis=1) * _silu(z_t)


def _b_specs(s):
    nb = s // WINDOW
    qsp = pl.BlockSpec((WINDOW, 512), lambda n: (n, P_QB // 512))
    zsp = pl.BlockSpec((WINDOW, 512), lambda n: (n, P_ZB // 512))

    def three(col, width):
        return [pl.BlockSpec((WINDOW, width), lambda n: (jnp.maximum(n - 1, 0), col)),
                pl.BlockSpec((WINDOW, width), lambda n: (n, col)),
                pl.BlockSpec((WINDOW, width), lambda n: (jnp.minimum(n + 1, nb - 1), col))]

    tab = pl.BlockSpec((WINDOW, B_DIM), lambda n: (n, 0))
    small = pl.BlockSpec((8, LANE), lambda n: (0, 0))
    specs = [qsp, zsp] + three(P_KB // LANE, LANE) + three(P_VB // LANE, LANE) + [tab, tab] + three(0, B_DIM) + three(0, B_DIM) + [small]
    return nb, specs


def _b_args(proj, cos2, sin2, pb):
    return (proj, proj, proj, proj, proj, proj, proj, proj, cos2, sin2, cos2, cos2, cos2, sin2, sin2, sin2, pb)


def _b_load(refs):
    (q_ref, z_ref, kp, kc, kx, vp, vc, vx, cq, sq, ckp, ckc, ckx, skp, skc, skx, pb_ref) = refs
    cat = lambda *r: jnp.concatenate([t[...] for t in r], axis=0)
    return (q_ref[...], z_ref[...], cat(kp, kc, kx), cat(vp, vc, vx), cq[...], sq[...], cat(ckp, ckc, ckx),
            cat(skp, skc, skx), pb_ref[...])


def _attn_b_fwd(proj, cos2, sin2, pb):
    s = proj.shape[0]
    nb, specs = _b_specs(s)

    def body(*refs):
        o_ref = refs[-1]
        args = _b_load(refs[:-1])
        o_ref[...] = _b_block(*args, pl.program_id(0), nb).astype(BF16)

    return pl.pallas_call(
        body, name="attn_b_fwd", grid=(nb,), in_specs=specs,
        out_specs=pl.BlockSpec((WINDOW, 512), lambda n: (n, 0)),
        out_shape=jax.ShapeDtypeStruct((s, 512), BF16), compiler_params=_params(("parallel",)),
    )(*_b_args(proj, cos2, sin2, pb))


def _attn_b_bwd(proj, cos2, sin2, pb, d_mixed):
    s = proj.shape[0]
    nb, specs = _b_specs(s)
    w = WINDOW

    def body(*refs):
        dm_ref, dq_ref, dz_ref, dk_ref, dv_ref, dpb_ref = refs[-6:]
        n = pl.program_id(0)
        q_t, z_t, k3, v3, cq, sq, ck, sk, pb_v = _b_load(refs[:-6])

        @pl.when(n == 0)
        def _():
            dk_ref[...] = jnp.zeros_like(dk_ref)
            dv_ref[...] = jnp.zeros_like(dv_ref)
            dpb_ref[...] = jnp.zeros_like(dpb_ref)

        def f(q_, z_, k_, v_, pb_):
            return _b_block(q_, z_, k_, v_, cq, sq, ck, sk, pb_, n, nb)

        _, vjp = jax.vjp(f, q_t, z_t, k3, v3, pb_v)
        dq, dz, dk3, dv3, dpb = vjp(dm_ref[...])
        dq_ref[...] = dq
        dz_ref[...] = dz
        dpb_ref[...] += dpb

        def add(j, cond):
            @pl.when(cond)
            def _():
                rows = pl.ds(pl.multiple_of((n - 1 + j) * w, w), w)
                dk_ref[rows, :] += dk3[j * w:(j + 1) * w, :]
                dv_ref[rows, :] += dv3[j * w:(j + 1) * w, :]

        add(0, n > 0)
        add(1, n >= 0)
        add(2, n < nb - 1)

    blk = pl.BlockSpec((w, 512), lambda n: (n, 0))
    whole = pl.BlockSpec((s, LANE), lambda n: (0, 0))
    small = pl.BlockSpec((8, LANE), lambda n: (0, 0))
    return pl.pallas_call(
        body, name="attn_b_bwd", grid=(nb,),
        in_specs=specs + [pl.BlockSpec((w, 512), lambda n: (n, 2))],
        out_specs=[blk, blk, whole, whole, small],
        out_shape=[jax.ShapeDtypeStruct((s, 512), F32), jax.ShapeDtypeStruct((s, 512), F32),
                   jax.ShapeDtypeStruct((s, LANE), F32), jax.ShapeDtypeStruct((s, LANE), F32),
                   jax.ShapeDtypeStruct((8, LANE), F32)],
        compiler_params=_params(("arbitrary",)),
    )(*_b_args(proj, cos2, sin2, pb), d_mixed)


def _mem_kv_fwd(mem, mem_norm_w, w_kv):
    def body(mem_ref, nw_ref, w_ref, kv_ref):
        mn = _rms(mem_ref[...], nw_ref[...]).astype(BF16)
        kv_ref[...] = jnp.dot(mn, w_ref[...], preferred_element_type=F32)

    return pl.pallas_call(
        body, name="mem_kv_fwd", out_shape=jax.ShapeDtypeStruct((MEM_LEN, 2 * C_HEADS * C_DIM), F32),
        compiler_params=_params(),
    )(mem, mem_norm_w, w_kv)


def _mem_kv_bwd(mem, mem_norm_w, w_kv, d_kv):
    def body(mem_ref, nw_ref, w_ref, g_ref, gw_ref, gn_ref):
        mn, vjp = jax.vjp(_rms, mem_ref[...], nw_ref[...])
        g = g_ref[...].astype(BF16)
        gw_ref[...] = lax.dot_general(mn.astype(BF16), g, (((0,), (0,)), ((), ())), preferred_element_type=F32)
        d_mn = lax.dot_general(g, w_ref[...], (((1,), (1,)), ((), ())), preferred_element_type=F32)
        gn_ref[...] = vjp(d_mn)[1]

    return pl.pallas_call(
        body, name="mem_kv_bwd",
        out_shape=[jax.ShapeDtypeStruct((D_MODEL, 2 * C_HEADS * C_DIM), F32), jax.ShapeDtypeStruct((1, D_MODEL), F32)],
        compiler_params=_params(),
    )(mem, mem_norm_w, w_kv, d_kv)


def _c_tile(q_t, z_t, kvm, pc):
    width = C_HEADS * C_DIM
    outs = []
    for h in range(C_HEADS):
        q = _rms(q_t[:, h * C_DIM:(h + 1) * C_DIM], pc[0:1, :])
        k = _rms(kvm[:, h * C_DIM:(h + 1) * C_DIM], pc[1:2, :])
        v = kvm[:, width + h * C_DIM:width + (h + 1) * C_DIM]
        s = _mm_nt(q, k) * (C_DIM ** -0.5)
        p = jnp.exp(s - jnp.max(s, axis=1, keepdims=True))
        p = p / jnp.sum(p, axis=1, keepdims=True)
        outs.append(_mm(p, v))
    return jnp.concatenate(outs, axis=1) * _silu(z_t)


def _attn_c_fwd(proj, kvm, pc, tq=256):
    s = proj.shape[0]

    def body(q_ref, z_ref, kv_ref, pc_ref, o_ref):
        o_ref[...] = _c_tile(q_ref[...], z_ref[...], kv_ref[...], pc_ref[...]).astype(BF16)

    return pl.pallas_call(
        body, name="attn_c_fwd", grid=(s // tq,),
        in_specs=[pl.BlockSpec((tq, 512), lambda i: (i, P_QC // 512)), pl.BlockSpec((tq, 512), lambda i: (i, P_ZC // 512)),
                  pl.BlockSpec(kvm.shape, lambda i: (0, 0)), pl.BlockSpec((8, LANE), lambda i: (0, 0))],
        out_specs=pl.BlockSpec((tq, 512), lambda i: (i, 0)),
        out_shape=jax.ShapeDtypeStruct((s, 512), BF16), compiler_params=_params(("parallel",)),
    )(proj, proj, kvm, pc)


def _attn_c_bwd(proj, kvm, pc, d_mixed, tq=256):
    s = proj.shape[0]

    def body(q_ref, z_ref, kv_ref, pc_ref, dm_ref, dq_ref, dz_ref, dkv_ref, dpc_ref):
        @pl.when(pl.program_id(0) == 0)
        def _():
            dkv_ref[...] = jnp.zeros_like(dkv_ref)
            dpc_ref[...] = jnp.zeros_like(dpc_ref)

        _, vjp = jax.vjp(_c_tile, q_ref[...], z_ref[...], kv_ref[...], pc_ref[...])
        dq, dz, dkv, dpc = vjp(dm_ref[...])
        dq_ref[...] = dq
        dz_ref[...] = dz
        dkv_ref[...] += dkv
        dpc_ref[...] += dpc

    blk = pl.BlockSpec((tq, 512), lambda i: (i, 0))
    kvs = pl.BlockSpec(kvm.shape, lambda i: (0, 0))
    small = pl.BlockSpec((8, LANE), lambda i: (0, 0))
    return pl.pallas_call(
        body, name="attn_c_bwd", grid=(s // tq,),
        in_specs=[pl.BlockSpec((tq, 512), lambda i: (i, P_QC // 512)), pl.BlockSpec((tq, 512), lambda i: (i, P_ZC // 512)),
                  kvs, small, pl.BlockSpec((tq, 512), lambda i: (i, 3))],
        out_specs=[blk, blk, kvs, small],
        out_shape=[jax.ShapeDtypeStruct((s, 512), F32), jax.ShapeDtypeStruct((s, 512), F32),
                   jax.ShapeDtypeStruct(kvm.shape, F32), jax.ShapeDtypeStruct((8, LANE), F32)],
        compiler_params=_params(("arbitrary",)),
    )(proj, proj, kvm, pc, d_mixed)


def _pad_row(v, width=LANE):
    v = v.reshape(1, -1)
    return jnp.pad(v, ((0, 0), (0, width - v.shape[1])))


def _local_step(x, mem, target, norm_w, w_perm, conv_w, pa, pb, pc, mem_norm_w, w_kv, w_out):
    s = x.shape[0]
    cos2, sin2 = _rope_tables(s)
    hn = _rms_fwd(x, norm_w)
    wide = dict(tm=1024, tn=512, tk=2048)
    proj = _matmul(hn, w_perm, "nn", F32, "mm_proj", **wide)
    cqkv = _conv_fwd(proj, conv_w)
    mixed_a = _delta_fwd(cqkv, proj, pa)
    mixed_b = _attn_b_fwd(proj, cos2, sin2, pb)
    kvm = _mem_kv_fwd(mem, mem_norm_w, w_kv)
    mixed_c = _attn_c_fwd(proj, kvm, pc)
    mixed = jnp.concatenate([mixed_a, mixed_b, mixed_c], axis=1)
    mo = _matmul(mixed, w_out, "nn", F32, "mm_out", **wide)
    dy, dyb, loss_parts = _loss_dy(x, mo, target)

    d_mixed = _matmul(dyb, w_out, "nt", F32, "mm_dmixed", **wide)
    g_w_out = _matmul(mixed, dyb, "tn", F32, "mm_gwout", **wide)
    d_qc, d_zc, d_kvm, d_pc = _attn_c_bwd(proj, kvm, pc, d_mixed)
    g_w_kv, g_mem_norm = _mem_kv_bwd(mem, mem_norm_w, w_kv, d_kvm)
    d_qb, d_zb, d_kb, d_vb, d_pb = _attn_b_bwd(proj, cos2, sin2, pb, d_mixed)
    d_c, d_za, d_gt, d_pa = _delta_bwd(cqkv, proj, pa, d_mixed)
    d_qkv, g_conv = _conv_bwd(proj, conv_w, d_c)
    d_proj = jnp.concatenate([d_qkv, d_za, d_qb, d_zb, d_qc, d_zc, d_kb, d_vb, d_gt,
                              jnp.zeros((s, P_WIDTH - P_GT - LANE), F32)], axis=1).astype(BF16)
    d_hn = _matmul(d_proj, w_perm, "nt", F32, "mm_dhn", tm=1024, tn=1024, tk=512)
    g_w_perm = _matmul(hn, d_proj, "tn", F32, "mm_gwin", **wide)
    g_x, g_norm = _rms_bwd(x, norm_w, d_hn, dy)
    return dict(loss_parts=loss_parts, g_x=g_x, g_norm=g_norm, g_w_perm=g_w_perm, g_conv=g_conv, d_pa=d_pa,
                d_pb=d_pb, d_pc=d_pc, g_mem_norm=g_mem_norm, g_w_kv=g_w_kv, g_w_out=g_w_out)


def _permute_cols(w):
    pad = jnp.zeros((w.shape[0], P_WIDTH - IN_WIDTH), w.dtype)
    return jnp.concatenate([w[:, :O_GT], w[:, O_QB:O_KB], w[:, O_ZB:O_QC], w[:, O_QC:O_ZC], w[:, O_ZC:IN_WIDTH],
                            w[:, O_KB:O_VB], w[:, O_VB:O_ZB], w[:, O_GT:O_QB], pad], axis=1)


def _unpermute_cols(g):
    return jnp.concatenate([g[:, :P_QB], g[:, P_GT:P_GT + 32], g[:, P_QB:P_ZB], g[:, P_KB:P_VB], g[:, P_VB:P_GT],
                            g[:, P_ZB:P_QC], g[:, P_QC:P_ZC], g[:, P_ZC:P_KB]], axis=1)


HBM = pl.BlockSpec(memory_space=pltpu.HBM)


def _place():
    x, y, c = lax.axis_index("x"), lax.axis_index("y"), lax.axis_index("c")
    chips = [(1 - x, y), (x, 1 - y), (1 - x, 1 - y)]
    return x, y, c, 2 * x + y, chips, [2 * cx + cy for cx, cy in chips]


COPY_CHUNK_ROWS = 128
COPY_MAX_CHUNKS = 8


class _Copies:
    def __init__(self, make, src, dst):
        rows = src.shape[-2]
        n = max(1, min(COPY_MAX_CHUNKS, rows // COPY_CHUNK_ROWS))
        assert rows % n == 0
        step = rows // n
        lead = (slice(None),) * (len(src.shape) - 2)
        self.whole = make(src, dst)
        self.parts = [self.whole] if n == 1 else [
            make(src.at[lead + (pl.ds(i * step, step), slice(None))], dst.at[lead + (pl.ds(i * step, step), slice(None))])
            for i in range(n)]

    def start(self):
        for p in self.parts:
            p.start()

    def wait(self):
        self.whole.wait()

    def wait_send(self):
        self.whole.wait_send()

    def wait_recv(self):
        self.whole.wait_recv()


def _remote(src, dst, send_sems, recv_sems, k, to):
    def make(s, d):
        return pltpu.make_async_remote_copy(src_ref=s, dst_ref=d, send_sem=send_sems.at[k], recv_sem=recv_sems.at[k],
                                            device_id=to, device_id_type=MESH)
    return _Copies(make, src, dst)


def _local(src, dst, sem):
    return _Copies(lambda s, d: pltpu.make_async_copy(s, d, sem), src, dst)


def _half_rows(ref, c):
    half = ref.shape[-2] // 2
    return pl.ds(pl.multiple_of(c * half, 8), half)


def _all_gather_weights(w_in_b, w_out_b, w_kv_b, conv_b):
    bigs = (w_in_b, w_out_b, w_kv_b)
    n_big = len(bigs)

    def body(*refs):
        srcs, conv_src = refs[:n_big], refs[n_big]
        dsts, conv_dst = refs[n_big + 1:2 * n_big + 1], refs[2 * n_big + 1]
        send_sems, recv_sems, local_sems = refs[2 * n_big + 2:]
        x, y, c, me, chips, chip_ids = _place()
        sibling = (x, y, 1 - c)
        local = [_local(src, dst.at[me], local_sems.at[a]) for a, (src, dst) in enumerate(zip(srcs, dsts))]
        local.append(_local(conv_src, conv_dst.at[me], local_sems.at[n_big]))
        for cp in local:
            cp.start()
        sends = []
        for a, (src, dst) in enumerate(zip(srcs, dsts)):
            mine = _half_rows(src, c)
            for j, chip in enumerate(chips):
                sends.append(_remote(src.at[mine, :], dst.at[me, mine, :], send_sems, recv_sems, 6 * a + j, (*chip, c)))
        for j, chip in enumerate(chips):
            sends.append(_remote(conv_src, conv_dst.at[me], send_sems, recv_sems, 6 * n_big + j, (*chip, c)))
        for cp in sends:
            cp.start()
        passed = []
        for a, (src, dst) in enumerate(zip(srcs, dsts)):
            mine = _half_rows(src, c)
            for j, cid in enumerate(chip_ids):
                landed = dst.at[cid, mine, :]
                _remote(landed, landed, send_sems, recv_sems, 6 * a + j, sibling).wait_recv()
                cp = _remote(landed, landed, send_sems, recv_sems, 6 * a + 3 + j, sibling)
                cp.start()
                passed.append(cp)
        for a, (src, dst) in enumerate(zip(srcs, dsts)):
            other = _half_rows(src, 1 - c)
            for j, cid in enumerate(chip_ids):
                landed = dst.at[cid, other, :]
                _remote(landed, landed, send_sems, recv_sems, 6 * a + 3 + j, sibling).wait_recv()
        for j, cid in enumerate(chip_ids):
            _remote(conv_src, conv_dst.at[cid], send_sems, recv_sems, 6 * n_big + j, sibling).wait_recv()
        for cp in sends + passed:
            cp.wait_send()
        for cp in local:
            cp.wait()

    n_sem = 6 * n_big + 3
    return pl.pallas_call(
        body, name="all_gather_weights",
        out_shape=[jax.ShapeDtypeStruct((N_CHIPS,) + w.shape, w.dtype) for w in bigs + (conv_b,)],
        in_specs=[HBM] * (n_big + 1), out_specs=[HBM] * (n_big + 1),
        scratch_shapes=[pltpu.SemaphoreType.DMA((n_sem,)), pltpu.SemaphoreType.DMA((n_sem,)),
                        pltpu.SemaphoreType.DMA((n_big + 1,))],
    )(*bigs, conv_b)


def _pair_exchange(grads):
    n = len(grads)

    def body(*refs):
        srcs, owns, gots = refs[:n], refs[n:2 * n], refs[2 * n:3 * n]
        send_sems, recv_sems, local_sems = refs[3 * n:]
        x, y, c, _, _, _ = _place()
        copies = []
        for a in range(n):
            mine, other = _half_rows(srcs[a], c), _half_rows(srcs[a], 1 - c)
            keep = _local(srcs[a].at[:, mine, :], owns[a], local_sems.at[a])
            keep.start()
            give = _remote(srcs[a].at[:, other, :], gots[a], send_sems, recv_sems, a, (x, y, 1 - c))
            give.start()
            copies += [keep, give]
        for cp in copies:
            cp.wait()

    halves = [jax.ShapeDtypeStruct((g.shape[0], g.shape[1] // 2, g.shape[2]), g.dtype) for g in grads]
    out = pl.pallas_call(
        body, name="grad_pair_exchange", out_shape=halves + halves,
        in_specs=[HBM] * n, out_specs=[HBM] * (2 * n),
        scratch_shapes=[pltpu.SemaphoreType.DMA((n,)), pltpu.SemaphoreType.DMA((n,)), pltpu.SemaphoreType.DMA((n,))],
    )(*grads)
    return out[:n], out[n:]


def _chip_exchange(halves):
    n = len(halves)

    def body(*refs):
        srcs, lands = refs[:n], refs[n:2 * n]
        send_sems, recv_sems, local_sems = refs[2 * n:]
        x, y, c, me, chips, chip_ids = _place()
        copies = []
        for a in range(n):
            keep = _local(srcs[a].at[me], lands[a].at[me], local_sems.at[a])
            keep.start()
            copies.append(keep)
            for j, (chip, cid) in enumerate(zip(chips, chip_ids)):
                give = _remote(srcs[a].at[cid], lands[a].at[me], send_sems, recv_sems, 3 * a + j, (*chip, c))
                give.start()
                copies.append(give)
        for a in range(n):
            for j, cid in enumerate(chip_ids):
                _remote(srcs[a].at[cid], lands[a].at[cid], send_sems, recv_sems, 3 * a + j, (x, y, c)).wait_recv()
        for a in range(n):
            copies[4 * a].wait()
            for j in range(3):
                copies[4 * a + 1 + j].wait_send()

    return pl.pallas_call(
        body, name="grad_chip_exchange", out_shape=[jax.ShapeDtypeStruct(h.shape, h.dtype) for h in halves],
        in_specs=[HBM] * n, out_specs=[HBM] * n,
        scratch_shapes=[pltpu.SemaphoreType.DMA((3 * n,)), pltpu.SemaphoreType.DMA((3 * n,)),
                        pltpu.SemaphoreType.DMA((n,))],
    )(*halves)


def _pair_gather(halves):
    n = len(halves)

    def body(*refs):
        srcs, fulls = refs[:n], refs[n:2 * n]
        send_sems, recv_sems, local_sems = refs[2 * n:]
        x, y, c, _, _, _ = _place()
        copies = []
        for a in range(n):
            mine = _half_rows(fulls[a], c)
            keep = _local(srcs[a], fulls[a].at[mine, :], local_sems.at[a])
            keep.start()
            give = _remote(srcs[a], fulls[a].at[mine, :], send_sems, recv_sems, a, (x, y, 1 - c))
            give.start()
            copies += [keep, give]
        for a in range(n):
            other = _half_rows(fulls[a], 1 - c)
            copies[2 * a].wait()
            copies[2 * a + 1].wait_send()
            _remote(srcs[a], fulls[a].at[other, :], send_sems, recv_sems, a, (x, y, 1 - c)).wait_recv()

    return pl.pallas_call(
        body, name="grad_pair_gather",
        out_shape=[jax.ShapeDtypeStruct((2 * h.shape[0], h.shape[1]), h.dtype) for h in halves],
        in_specs=[HBM] * n, out_specs=[HBM] * n,
        scratch_shapes=[pltpu.SemaphoreType.DMA((n,)), pltpu.SemaphoreType.DMA((n,)), pltpu.SemaphoreType.DMA((n,))],
    )(*halves)


def _all_reduce_small(p):
    n_dev = 8

    def body(p_ref, o_ref, land, send_sems, recv_sems):
        x, y, c = lax.axis_index("x"), lax.axis_index("y"), lax.axis_index("c")
        me = 4 * x + 2 * y + c
        land[me] = p_ref[...]
        sends = []
        for k in range(1, n_dev):
            fx, fy, fc = (k >> 2) & 1, (k >> 1) & 1, k & 1
            to = (x ^ fx, y ^ fy, c ^ fc)
            cp = _remote(p_ref, land.at[me], send_sems, recv_sems, k - 1, to)
            cp.start()
            sends.append(cp)
        for k in range(1, n_dev):
            _remote(p_ref, land.at[me ^ k], send_sems, recv_sems, k - 1, (x, y, c)).wait_recv()
        total = land[0]
        for d in range(1, n_dev):
            total = total + land[d]
        o_ref[...] = total
        for cp in sends:
            cp.wait_send()

    vm = pl.BlockSpec(memory_space=pltpu.VMEM)
    return pl.pallas_call(
        body, name="all_reduce_small", out_shape=jax.ShapeDtypeStruct(p.shape, p.dtype), in_specs=[vm], out_specs=vm,
        scratch_shapes=[pltpu.VMEM((n_dev,) + p.shape, p.dtype), pltpu.SemaphoreType.DMA((n_dev - 1,)),
                        pltpu.SemaphoreType.DMA((n_dev - 1,))],
    )(p)


def _row_tile(rows, cap=256):
    return cap if rows % cap == 0 else rows


def _add2(a, b, name):
    n, r, c = a.shape
    tr = _row_tile(r)

    def body(a_ref, b_ref, o_ref):
        o_ref[...] = a_ref[...] + b_ref[...]

    blk = pl.BlockSpec((None, tr, c), lambda i, j: (i, j, 0))
    return pl.pallas_call(body, name=name, grid=(n, r // tr), in_specs=[blk, blk], out_specs=blk,
                          out_shape=jax.ShapeDtypeStruct(a.shape, a.dtype),
                          compiler_params=_params(("parallel", "parallel")))(a, b)


def _sum_slots(land, name):
    n, r, c = land.shape
    tr = _row_tile(r)

    def body(l_ref, o_ref):
        total = l_ref[0]
        for j in range(1, n):
            total = total + l_ref[j]
        o_ref[...] = total

    return pl.pallas_call(body, name=name, grid=(r // tr,), in_specs=[pl.BlockSpec((n, tr, c), lambda i: (0, i, 0))],
                          out_specs=pl.BlockSpec((tr, c), lambda i: (i, 0)),
                          out_shape=jax.ShapeDtypeStruct((r, c), land.dtype), compiler_params=_params(("parallel",)))(land)


def _adamw(w, g, m, v, name):
    r, c = w.shape
    tr = _row_tile(r)

    def body(w_ref, g_ref, m_ref, v_ref, d_ref, mo_ref, vo_ref):
        g_ = g_ref[...]
        m2 = ADAM_B1 * m_ref[...] + (1.0 - ADAM_B1) * g_
        v2 = ADAM_B2 * v_ref[...] + (1.0 - ADAM_B2) * jnp.square(g_)
        m_hat = m2 / (1.0 - ADAM_B1 ** ADAM_STEP)
        v_hat = v2 / (1.0 - ADAM_B2 ** ADAM_STEP)
        d_ref[...] = -ADAM_LR * (m_hat / (jnp.sqrt(v_hat) + ADAM_EPS) + ADAM_WD * w_ref[...])
        mo_ref[...] = m2
        vo_ref[...] = v2

    blk = pl.BlockSpec((tr, c), lambda i: (i, 0))
    return pl.pallas_call(body, name=name, grid=(r // tr,), in_specs=[blk] * 4, out_specs=[blk] * 3,
                          out_shape=[jax.ShapeDtypeStruct(w.shape, F32)] * 3, compiler_params=_params(("parallel",)))(w, g, m, v)


SMALL_NAMES = ("norm_w", "mem_norm_w", "o_norm_a", "q_norm_c", "k_norm_c", "q_norm_b", "k_norm_b",
               "a_log_fwd", "a_log_bwd", "dt_bias_fwd", "dt_bias_bwd", "sink_b")
SMALL_SIZES = (2048, 2048, 128, 128, 128, 64, 64, 8, 8, 8, 8, 8)
SMALL_LOSS = sum(SMALL_SIZES)
SMALL_CONV = 5120
SMALL_TOTAL = SMALL_CONV + CONV_K * 3 * A_WIDTH
SMALL_ROWS = SMALL_TOTAL // LANE


def _pack_small(parts, extra=None, conv=None):
    vec = [parts[n].reshape(-1) for n in SMALL_NAMES]
    vec.append(jnp.zeros((1,), F32) if extra is None else extra.reshape(1))
    vec.append(jnp.zeros((SMALL_CONV - SMALL_LOSS - 1,), F32))
    vec.append(jnp.zeros((SMALL_TOTAL - SMALL_CONV,), F32) if conv is None else conv.reshape(-1))
    return jnp.concatenate(vec).reshape(SMALL_ROWS, LANE)


def _unpack_small(packed):
    flat = packed.reshape(-1)
    out, off = {}, 0
    for n, size in zip(SMALL_NAMES, SMALL_SIZES):
        out[n] = flat[off:off + size].reshape(1, size)
        off += size
    return out


WEIGHT_ORDER = ("norm_w", "w_in", "conv_w_a", "a_log_fwd", "a_log_bwd", "dt_bias_fwd", "dt_bias_bwd", "o_norm_a",
                "q_norm_b", "k_norm_b", "sink_b", "mem_norm_w", "w_mem_kv", "q_norm_c", "k_norm_c", "w_out")


def kernel(x, mem, norm_w, w_in, conv_w_a, a_log_fwd, a_log_bwd, dt_bias_fwd, dt_bias_bwd, o_norm_a, q_norm_b, k_norm_b, sink_b, mem_norm_w, w_mem_kv, q_norm_c, k_norm_c, w_out, loss_target, m_norm_w, m_w_in, m_conv_w_a, m_a_log_fwd, m_a_log_bwd, m_dt_bias_fwd, m_dt_bias_bwd, m_o_norm_a, m_q_norm_b, m_k_norm_b, m_sink_b, m_mem_norm_w, m_w_mem_kv, m_q_norm_c, m_k_norm_c, m_w_out, v_norm_w, v_w_in, v_conv_w_a, v_a_log_fwd, v_a_log_bwd, v_dt_bias_fwd, v_dt_bias_bwd, v_o_norm_a, v_q_norm_b, v_k_norm_b, v_sink_b, v_mem_norm_w, v_w_mem_kv, v_q_norm_c, v_k_norm_c, v_w_out):
    weights = dict(norm_w=norm_w, w_in=w_in, conv_w_a=conv_w_a, a_log_fwd=a_log_fwd, a_log_bwd=a_log_bwd,
                   dt_bias_fwd=dt_bias_fwd, dt_bias_bwd=dt_bias_bwd, o_norm_a=o_norm_a, q_norm_b=q_norm_b,
                   k_norm_b=k_norm_b, sink_b=sink_b, mem_norm_w=mem_norm_w, w_mem_kv=w_mem_kv, q_norm_c=q_norm_c,
                   k_norm_c=k_norm_c, w_out=w_out)
    mom1 = dict(norm_w=m_norm_w, w_in=m_w_in, conv_w_a=m_conv_w_a, a_log_fwd=m_a_log_fwd, a_log_bwd=m_a_log_bwd,
                dt_bias_fwd=m_dt_bias_fwd, dt_bias_bwd=m_dt_bias_bwd, o_norm_a=m_o_norm_a, q_norm_b=m_q_norm_b,
                k_norm_b=m_k_norm_b, sink_b=m_sink_b, mem_norm_w=m_mem_norm_w, w_mem_kv=m_w_mem_kv,
                q_norm_c=m_q_norm_c, k_norm_c=m_k_norm_c, w_out=m_w_out)
    mom2 = dict(norm_w=v_norm_w, w_in=v_w_in, conv_w_a=v_conv_w_a, a_log_fwd=v_a_log_fwd, a_log_bwd=v_a_log_bwd,
                dt_bias_fwd=v_dt_bias_fwd, dt_bias_bwd=v_dt_bias_bwd, o_norm_a=v_o_norm_a, q_norm_b=v_q_norm_b,
                k_norm_b=v_k_norm_b, sink_b=v_sink_b, mem_norm_w=v_mem_norm_w, w_mem_kv=v_w_mem_kv,
                q_norm_c=v_q_norm_c, k_norm_c=v_k_norm_c, w_out=v_w_out)
    chip = 2 * lax.axis_index("x") + lax.axis_index("y")

    w_in4, w_out4, w_kv4, conv4 = _all_gather_weights(w_in[0].astype(BF16), w_out[0].astype(BF16),
                                                      w_mem_kv[0].astype(BF16), conv_w_a[0])
    w_perm = _permute_cols(jnp.transpose(w_in4, (1, 0, 2)).reshape(D_MODEL, IN_WIDTH))
    w_out_full = w_out4.reshape(D_MODEL, D_MODEL)
    w_kv_full = w_kv4.reshape(D_MODEL, 2 * C_HEADS * C_DIM)
    conv_full = jnp.transpose(conv4, (1, 0, 2)).reshape(CONV_K, 3 * A_WIDTH)
    pa = jnp.concatenate([_pad_row(a_log_fwd), _pad_row(a_log_bwd), _pad_row(dt_bias_fwd), _pad_row(dt_bias_bwd),
                          _pad_row(o_norm_a), jnp.zeros((3, LANE), F32)], axis=0)
    pb = jnp.concatenate([_pad_row(q_norm_b), _pad_row(k_norm_b), _pad_row(sink_b), jnp.zeros((5, LANE), F32)], axis=0)
    pc = jnp.concatenate([_pad_row(q_norm_c), _pad_row(k_norm_c), jnp.zeros((6, LANE), F32)], axis=0)

    r = _local_step(x[0], mem[0], loss_target[0], norm_w, w_perm, conv_full, pa, pb, pc, mem_norm_w, w_kv_full,
                    w_out_full)

    g_in4 = jnp.transpose(_unpermute_cols(r["g_w_perm"]).reshape(D_MODEL, N_CHIPS, W_IN_BLOCK), (1, 0, 2))
    g_out4 = r["g_w_out"].reshape(N_CHIPS, D_MODEL // N_CHIPS, D_MODEL)
    g_kv4 = r["g_w_kv"].reshape(N_CHIPS, D_MODEL // N_CHIPS, 2 * C_HEADS * C_DIM)
    own, got = _pair_exchange([g_in4, g_out4, g_kv4])
    pair = [_add2(a, b, "grad_pair_sum_%d" % i) for i, (a, b) in enumerate(zip(own, got))]
    lands = _chip_exchange(pair)
    reduced = [_sum_slots(l, "grad_chip_sum_%d" % i) for i, l in enumerate(lands)]
    g_w_in, g_w_out, g_w_kv = _pair_gather(reduced)

    d_pa, d_pb, d_pc = r["d_pa"], r["d_pb"], r["d_pc"]
    small_g = dict(norm_w=r["g_norm"], mem_norm_w=r["g_mem_norm"], o_norm_a=d_pa[4], q_norm_c=d_pc[0], k_norm_c=d_pc[1],
                   q_norm_b=d_pb[0, :B_DIM], k_norm_b=d_pb[1, :B_DIM], a_log_fwd=d_pa[0, :A_HEADS],
                   a_log_bwd=d_pa[1, :A_HEADS], dt_bias_fwd=d_pa[2, :A_HEADS], dt_bias_bwd=d_pa[3, :A_HEADS],
                   sink_b=d_pb[2, :B_HEADS])
    packed = _all_reduce_small(_pack_small(small_g, jnp.sum(r["loss_parts"][:, 0, 0]), r["g_conv"]))
    flat = packed.reshape(-1)
    loss = flat[SMALL_LOSS]
    conv_sum = flat[SMALL_CONV:].reshape(CONV_K, 3 * A_WIDTH)
    conv_cols = 3 * A_WIDTH // N_CHIPS
    g_conv = lax.dynamic_slice(conv_sum, (0, chip * conv_cols), (CONV_K, conv_cols))

    grads = _unpack_small(packed)
    grads.update(w_in=g_w_in, w_mem_kv=g_w_kv, w_out=g_w_out, conv_w_a=g_conv)
    delta, new_m, new_v = {}, {}, {}
    for n in ("w_in", "w_mem_kv", "w_out", "conv_w_a"):
        delta[n], new_m[n], new_v[n] = _adamw(weights[n][0], grads[n], mom1[n][0], mom2[n][0], "adamw_" + n)
    d_s, m_s, v_s = _adamw(_pack_small(weights), packed, _pack_small(mom1), _pack_small(mom2), "adamw_small")
    d_s, m_s, v_s = _unpack_small(d_s), _unpack_small(m_s), _unpack_small(v_s)
    for n in SMALL_NAMES:
        delta[n], new_m[n], new_v[n] = d_s[n], m_s[n], v_s[n]

    def shaped(tree):
        return [tree[n].reshape(weights[n].shape) for n in WEIGHT_ORDER]

    return (loss, r["g_x"].reshape(x.shape), *shaped(grads), *shaped(delta), *shaped(new_m), *shaped(new_v))
```

```python
import functools

import jax
import jax.numpy as jnp
from jax import lax
from jax.experimental import pallas as pl
from jax.experimental.pallas import tpu as pltpu

F32 = jnp.float32
BF16 = jnp.bfloat16
HI = lax.Precision.HIGHEST
MESH = pl.DeviceIdType.MESH

D_MODEL = 2048
A_WIDTH = 1024
A_HEADS = 8
A_DIM = 128
CONV_K = 5
CHUNK = 64
B_HEADS = 8
B_KV = 2
B_DIM = 64
WINDOW = 128
C_HEADS = 4
C_DIM = 128
MEM_LEN = 256
ROPE_THETA = 10000.0
EPS = 1e-6
IN_WIDTH = 6432
N_CHIPS = 4
W_IN_BLOCK = IN_WIDTH // N_CHIPS

LANE = 128
P_QA, P_KA, P_VA, P_ZA = 0, 1024, 2048, 3072
P_QB, P_ZB, P_QC, P_ZC = 4096, 4608, 5120, 5632
P_KB, P_VB, P_GT = 6144, 6272, 6400
P_WIDTH = 6656
O_GT, O_QB, O_KB, O_VB, O_ZB, O_QC, O_ZC = 4096, 4128, 4640, 4768, 4896, 5408, 5920

ADAM_LR, ADAM_B1, ADAM_B2, ADAM_EPS, ADAM_WD, ADAM_STEP = 0.001, 0.9, 0.999, 1e-08, 0.01, 10

VMEM_LIMIT = 56 * 1024 * 1024


def _params(sem=None):
    return pltpu.CompilerParams(dimension_semantics=sem, vmem_limit_bytes=VMEM_LIMIT)


def _dot(a, b, dims=(((1,), (0,)), ((), ())), precision=HI):
    return lax.dot_general(a, b, dims, precision=precision, preferred_element_type=F32)


def _dot_nt(a, b, precision=HI):
    return _dot(a, b, (((1,), (1,)), ((), ())), precision)


def _dot_tn(a, b, precision=HI):
    return _dot(a, b, (((0,), (0,)), ((), ())), precision)


_NN = (((1,), (0,)), ((), ()))
_NT = (((1,), (1,)), ((), ()))
_TN = (((0,), (0,)), ((), ()))


def _bdot(a, b, dims):
    return lax.dot_general(a.astype(BF16), b.astype(BF16), dims, preferred_element_type=F32)


@jax.custom_vjp
def _mm(a, b):
    return _bdot(a, b, _NN)


_mm.defvjp(lambda a, b: (_bdot(a, b, _NN), (a, b)),
           lambda res, ct: (_bdot(ct, res[1], _NT), _bdot(res[0], ct, _TN)))


@jax.custom_vjp
def _mm_nt(a, b):
    return _bdot(a, b, _NT)


_mm_nt.defvjp(lambda a, b: (_bdot(a, b, _NT), (a, b)),
              lambda res, ct: (_bdot(ct, res[1], _NN), _bdot(ct, res[0], _TN)))


@jax.custom_vjp
def _mm_tn(a, b):
    return _bdot(a, b, _TN)


_mm_tn.defvjp(lambda a, b: (_bdot(a, b, _TN), (a, b)),
              lambda res, ct: (_bdot(res[1], ct, _NT), _bdot(res[0], ct, _NN)))


def _rms(t, w):
    return t * lax.rsqrt(jnp.mean(t * t, axis=-1, keepdims=True) + EPS) * w


def _l2(t):
    return t * lax.rsqrt(jnp.sum(t * t, axis=-1, keepdims=True) + EPS)


def _silu(t):
    return t * jax.nn.sigmoid(t)


def _softplus(t):
    return jnp.maximum(t, 0.0) + jnp.log1p(jnp.exp(-jnp.abs(t)))


def _matmul(a, b, mode, out_dtype, name, tm=512, tn=512, tk=512):
    (m, k) = a.shape[::-1] if mode == "tn" else a.shape
    n = b.shape[0] if mode == "nt" else b.shape[1]
    tm, tn, tk = min(tm, m), min(tn, n), min(tk, k)
    assert m % tm == 0 and n % tn == 0 and k % tk == 0, (m, n, k, tm, tn, tk)
    if mode == "nn":
        a_spec = pl.BlockSpec((tm, tk), lambda i, j, kk: (i, kk))
        b_spec = pl.BlockSpec((tk, tn), lambda i, j, kk: (kk, j))
        dims = (((1,), (0,)), ((), ()))
    elif mode == "nt":
        a_spec = pl.BlockSpec((tm, tk), lambda i, j, kk: (i, kk))
        b_spec = pl.BlockSpec((tn, tk), lambda i, j, kk: (j, kk))
        dims = (((1,), (1,)), ((), ()))
    else:
        a_spec = pl.BlockSpec((tk, tm), lambda i, j, kk: (kk, i))
        b_spec = pl.BlockSpec((tk, tn), lambda i, j, kk: (kk, j))
        dims = (((0,), (0,)), ((), ()))
    nk = k // tk

    def body_one(a_ref, b_ref, o_ref):
        o_ref[...] = _bdot(a_ref[...], b_ref[...], dims).astype(out_dtype)

    def body_acc(a_ref, b_ref, o_ref, acc_ref):
        kk = pl.program_id(2)

        @pl.when(kk == 0)
        def _():
            acc_ref[...] = jnp.zeros_like(acc_ref)

        acc_ref[...] += _bdot(a_ref[...], b_ref[...], dims)

        @pl.when(kk == nk - 1)
        def _():
            o_ref[...] = acc_ref[...].astype(out_dtype)

    return pl.pallas_call(
        body_one if nk == 1 else body_acc, name=name, grid=(m // tm, n // tn, nk),
        in_specs=[a_spec, b_spec], out_specs=pl.BlockSpec((tm, tn), lambda i, j, kk: (i, j)),
        out_shape=jax.ShapeDtypeStruct((m, n), out_dtype),
        scratch_shapes=[] if nk == 1 else [pltpu.VMEM((tm, tn), F32)],
        compiler_params=_params(("parallel", "parallel", "arbitrary")),
    )(a, b)


def _rms_fwd(x, w, tr=256):
    s, d = x.shape

    def body(x_ref, w_ref, o_ref):
        o_ref[...] = _rms(x_ref[...], w_ref[...]).astype(BF16)

    return pl.pallas_call(
        body, name="rms_fwd", grid=(s // tr,),
        in_specs=[pl.BlockSpec((tr, d), lambda i: (i, 0)), pl.BlockSpec((1, d), lambda i: (0, 0))],
        out_specs=pl.BlockSpec((tr, d), lambda i: (i, 0)),
        out_shape=jax.ShapeDtypeStruct((s, d), BF16), compiler_params=_params(("parallel",)),
    )(x, w)


def _rms_bwd(x, w, d_hn, dy, tr=256):
    s, d = x.shape

    def body(x_ref, w_ref, g_ref, dy_ref, gx_ref, gw_ref):
        _, vjp = jax.vjp(_rms, x_ref[...], w_ref[...])
        dx, dw = vjp(g_ref[...])
        gx_ref[...] = dy_ref[...] + dx

        @pl.when(pl.program_id(0) == 0)
        def _():
            gw_ref[...] = jnp.zeros_like(gw_ref)

        gw_ref[...] += dw

    row = pl.BlockSpec((tr, d), lambda i: (i, 0))
    vec = pl.BlockSpec((1, d), lambda i: (0, 0))
    return pl.pallas_call(
        body, name="rms_bwd", grid=(s // tr,), in_specs=[row, vec, row, row], out_specs=[row, vec],
        out_shape=[jax.ShapeDtypeStruct((s, d), F32), jax.ShapeDtypeStruct((1, d), F32)],
        compiler_params=_params(("arbitrary",)),
    )(x, w, d_hn, dy)


def _loss_dy(x, mo, target, tr=256):
    s, d = x.shape
    nt = s // tr

    def body(x_ref, mo_ref, t_ref, dy_ref, dyb_ref, l_ref):
        err = x_ref[...] + mo_ref[...] - t_ref[...]
        dy = err * (1.0 / d)
        dy_ref[...] = dy
        dyb_ref[...] = dy.astype(BF16)
        l_ref[...] = jnp.full(l_ref.shape, 0.5 * jnp.sum(jnp.sum(err * err, axis=1, keepdims=True) * (1.0 / d)), F32)

    row = pl.BlockSpec((tr, d), lambda i: (i, 0))
    return pl.pallas_call(
        body, name="loss_dy", grid=(nt,), in_specs=[row, row, row],
        out_specs=[row, row, pl.BlockSpec((1, 8, LANE), lambda i: (i, 0, 0))],
        out_shape=[jax.ShapeDtypeStruct((s, d), F32), jax.ShapeDtypeStruct((s, d), BF16),
                   jax.ShapeDtypeStruct((nt, 8, LANE), F32)],
        compiler_params=_params(("parallel",)),
    )(x, mo, target)


def _shift_rows(t, s):
    if s == 0:
        return t
    n = t.shape[0]
    rolled = pltpu.roll(t, (-s) % n, axis=0)
    idx = lax.broadcasted_iota(jnp.int32, t.shape, 0) + s
    return jnp.where((idx >= 0) & (idx < n), rolled, 0.0)


def _conv_fwd(proj, conv_w):
    s = proj.shape[0]
    nblk = 3 * A_WIDTH // LANE

    def body(x_ref, w_ref, o_ref):
        x = x_ref[...]
        acc = jnp.zeros_like(x)
        for j in range(CONV_K):
            acc = acc + w_ref[j:j + 1, :] * _shift_rows(x, j - CONV_K // 2)
        o_ref[...] = acc

    return pl.pallas_call(
        body, name="conv_fwd", grid=(nblk,),
        in_specs=[pl.BlockSpec((s, LANE), lambda i: (0, i)), pl.BlockSpec((CONV_K, LANE), lambda i: (0, i))],
        out_specs=pl.BlockSpec((None, s, LANE), lambda i: (i // A_HEADS, 0, i % A_HEADS)),
        out_shape=jax.ShapeDtypeStruct((3, s, A_WIDTH), F32), compiler_params=_params(("parallel",)),
    )(proj, conv_w)


def _conv_bwd(proj, conv_w, d_c):
    s = proj.shape[0]
    nblk = 3 * A_WIDTH // LANE

    def body(x_ref, w_ref, g_ref, dx_ref, dw_ref):
        x, g = x_ref[...], g_ref[...]
        acc = jnp.zeros_like(x)
        for j in range(CONV_K):
            off = j - CONV_K // 2
            acc = acc + w_ref[j:j + 1, :] * _shift_rows(g, -off)
            dw_ref[j:j + 1, :] = jnp.sum(_shift_rows(x, off) * g, axis=0, keepdims=True)
        dx_ref[...] = acc

    col = pl.BlockSpec((s, LANE), lambda i: (0, i))
    wsp = pl.BlockSpec((CONV_K, LANE), lambda i: (0, i))
    dsp = pl.BlockSpec((None, s, LANE), lambda i: (i // A_HEADS, 0, i % A_HEADS))
    return pl.pallas_call(
        body, name="conv_bwd", grid=(nblk,), in_specs=[col, wsp, dsp], out_specs=[col, wsp],
        out_shape=[jax.ShapeDtypeStruct((s, 3 * A_WIDTH), F32), jax.ShapeDtypeStruct((CONV_K, 3 * A_WIDTH), F32)],
        compiler_params=_params(("parallel",)),
    )(proj, conv_w, d_c)


A_STEP_HEADS = 2
A_CHAINS = 2 * A_STEP_HEADS


def _a_chain(st, cq, ck, cv, alpha, beta_raw, a_log, dt_b, incl, strict, last):
    c = CHUNK
    eye = (lax.broadcasted_iota(jnp.int32, (c, c), 0) == lax.broadcasted_iota(jnp.int32, (c, c), 1)).astype(F32)
    gb = -jnp.exp(a_log) * _softplus(alpha + dt_b)
    bb = jax.nn.sigmoid(beta_raw)
    q = _l2(_silu(cq)) * (A_DIM ** -0.5)
    k = _l2(_silu(ck))
    v = _silu(cv)

    gc = _dot(incl, jnp.broadcast_to(gb, (c, LANE)))
    tot = jnp.sum(gc * last, axis=0, keepdims=True)
    m1 = gc[:, :c]
    decay = incl * jnp.exp(incl * (m1 - m1.T))
    kb = k * bb
    vb = v * bb
    a = -(strict * decay * _mm_nt(kb, k))
    tinv = eye + a
    p = a
    for _ in range(5):
        p = _mm(p, p)
        tinv = tinv + _mm(tinv, p)
    eg = jnp.exp(gc)
    u = _mm(tinv, vb)
    w = _mm(tinv, kb * eg)
    qk = _mm_nt(q, k) * decay
    v_new = u - _mm(w, st)
    o = _mm(q * eg, st) + _mm(qk, v_new)
    st_new = st * jnp.exp(tot) + _mm_tn(k * jnp.exp(tot - gc), v_new)
    return st_new, o


def _a_step(sts, cq, ck, cv, gts, pa, h0):
    c = CHUNK
    lane = lax.broadcasted_iota(jnp.int32, (1, LANE), 1)
    ii = lax.broadcasted_iota(jnp.int32, (c, c), 0)
    jj = lax.broadcasted_iota(jnp.int32, (c, c), 1)
    row = lax.broadcasted_iota(jnp.int32, (c, 1), 0)

    def pick(t, col):
        return jnp.sum(jnp.where(lane == col, t, 0.0), axis=1, keepdims=True)

    alpha, beta_raw, a_log, dt_b, incl, strict, last = [], [], [], [], [], [], []
    for b in range(A_CHAINS):
        h, rev = h0 + b // 2, b % 2
        alpha.append(pick(gts[b], h + 8 * rev))
        beta_raw.append(pick(gts[b], h + 16 + 8 * rev))
        a_log.append(pick(pa[rev:rev + 1, :], h))
        dt_b.append(pick(pa[2 + rev:3 + rev, :], h))
        incl.append(((ii <= jj) if rev else (ii >= jj)).astype(F32))
        strict.append(((ii < jj) if rev else (ii > jj)).astype(F32))
        last.append((row == (0 if rev else c - 1)).astype(F32))
    stack = lambda ts: jnp.concatenate([t[None] for t in ts], axis=0)
    return jax.vmap(_a_chain)(sts, cq, ck, cv, stack(alpha), stack(beta_raw), stack(a_log), stack(dt_b),
                              stack(incl), stack(strict), stack(last))


def _a_final(o, za, pa):
    outs = []
    for j in range(o.shape[1] // A_DIM):
        ln = slice(j * A_DIM, (j + 1) * A_DIM)
        outs.append(_rms(o[:, ln], pa[4:5, :]) * _silu(za[:, ln]))
    return jnp.concatenate(outs, axis=1)


def _a_tiles(n, nchunk):
    tiles = []
    for b in range(A_CHAINS):
        i = (nchunk - 1 - n) if b % 2 else n
        tiles.append((i, pl.ds(pl.multiple_of(i * CHUNK, CHUNK), CHUNK), slice((b // 2) * A_DIM, (b // 2 + 1) * A_DIM)))
    return tiles


def _a_load(tiles, c_ref, gt_ref):
    cq, ck, cv = (jnp.stack([c_ref[r, sl, ln] for _, sl, ln in tiles], axis=0) for r in range(3))
    return cq, ck, cv, jnp.stack([gt_ref[sl, :] for _, sl, _ in tiles], axis=0)


def _a_scan(h0, nchunk, c_ref, gt_ref, pa, of_ref, ob_ref, s_ref):
    def step(n, sts):
        tiles = _a_tiles(n, nchunk)
        sts_new, o = _a_step(sts, *_a_load(tiles, c_ref, gt_ref), pa, h0)
        for b, (i, sl, ln) in enumerate(tiles):
            if s_ref is not None:
                s_ref[b, i] = sts[b]
            (ob_ref if b % 2 else of_ref)[sl, ln] = o[b]
        return sts_new

    lax.fori_loop(0, nchunk, step, jnp.zeros((A_CHAINS, A_DIM, A_DIM), F32))


def _a_specs(s):
    wide = A_STEP_HEADS * A_DIM
    once = pl.Buffered(1)
    trio = pl.BlockSpec((3, s, wide), lambda g: (0, 0, g), pipeline_mode=once)
    gates = pl.BlockSpec((s, LANE), lambda g: (0, P_GT // LANE))
    small = pl.BlockSpec((8, LANE), lambda g: (0, 0))

    def cols(base):
        return pl.BlockSpec((s, wide), lambda g: (0, base // wide + g), pipeline_mode=once)

    return wide, trio, gates, small, cols


def _delta_fwd(cqkv, proj, pa):
    s = cqkv.shape[1]
    nchunk = s // CHUNK
    wide, trio, gates, small, cols = _a_specs(s)

    def body(c_ref, gt_ref, za_ref, pa_ref, out_ref, of_ref, ob_ref):
        h0 = pl.program_id(0) * A_STEP_HEADS
        pa_v = pa_ref[...]
        _a_scan(h0, nchunk, c_ref, gt_ref, pa_v, of_ref, ob_ref, None)
        out_ref[...] = _a_final(of_ref[...] + ob_ref[...], za_ref[...], pa_v).astype(BF16)

    return pl.pallas_call(
        body, name="delta_fwd", grid=(A_HEADS // A_STEP_HEADS,),
        in_specs=[trio, gates, cols(P_ZA), small], out_specs=cols(0),
        out_shape=jax.ShapeDtypeStruct((s, A_WIDTH), BF16),
        scratch_shapes=[pltpu.VMEM((s, wide), F32)] * 2, compiler_params=_params(("parallel",)),
    )(cqkv, proj, proj, pa)


def _delta_bwd(cqkv, proj, pa, d_mixed):
    s = cqkv.shape[1]
    nchunk = s // CHUNK

    wide, trio, gates, small, cols = _a_specs(s)

    def body(c_ref, gt_ref, za_ref, pa_ref, dm_ref, dc_ref, dza_ref, dgt_ref, dpa_ref, s_ref, of_ref, ob_ref):
        h0 = pl.program_id(0) * A_STEP_HEADS
        pa_v = pa_ref[...]

        @pl.when(h0 == 0)
        def _():
            dgt_ref[...] = jnp.zeros_like(dgt_ref)
            dpa_ref[...] = jnp.zeros_like(dpa_ref)

        _a_scan(h0, nchunk, c_ref, gt_ref, pa_v, of_ref, ob_ref, s_ref)
        _, vjp = jax.vjp(_a_final, of_ref[...] + ob_ref[...], za_ref[...], pa_v)
        d_o, d_za, dpa0 = vjp(dm_ref[...])
        of_ref[...] = d_o
        dza_ref[...] = d_za
        dc_ref[...] = jnp.zeros_like(dc_ref)

        def step(n, carry):
            d_sts, dpa = carry
            tiles = _a_tiles(nchunk - 1 - n, nchunk)
            sts = jnp.stack([s_ref[b, i] for b, (i, _, _) in enumerate(tiles)], axis=0)
            d_o_t = jnp.stack([of_ref[sl, ln] for _, sl, ln in tiles], axis=0)
            _, vjp_c = jax.vjp(lambda *a: _a_step(*a, h0), sts, *_a_load(tiles, c_ref, gt_ref), pa_v)
            d_prev, dcq, dck, dcv, dgts, dpa_i = vjp_c((d_sts, d_o_t))
            for b, (_, sl, ln) in enumerate(tiles):
                for r, dc in enumerate((dcq, dck, dcv)):
                    dc_ref[r, sl, ln] += dc[b]
                dgt_ref[sl, :] += dgts[b]
            return d_prev, dpa + dpa_i

        _, dpa_out = lax.fori_loop(0, nchunk, step, (jnp.zeros((A_CHAINS, A_DIM, A_DIM), F32), dpa0))
        dpa_ref[...] += dpa_out

    fixed = pl.BlockSpec((s, LANE), lambda g: (0, 0))
    return pl.pallas_call(
        body, name="delta_bwd", grid=(A_HEADS // A_STEP_HEADS,),
        in_specs=[trio, gates, cols(P_ZA), small, cols(0)], out_specs=[trio, cols(0), fixed, small],
        out_shape=[jax.ShapeDtypeStruct((3, s, A_WIDTH), F32), jax.ShapeDtypeStruct((s, A_WIDTH), F32),
                   jax.ShapeDtypeStruct((s, LANE), F32), jax.ShapeDtypeStruct((8, LANE), F32)],
        scratch_shapes=[pltpu.VMEM((A_CHAINS, nchunk, A_DIM, A_DIM), F32), pltpu.VMEM((s, wide), F32),
                        pltpu.VMEM((s, wide), F32)],
        compiler_params=_params(("arbitrary",)),
    )(cqkv, proj, proj, pa, d_mixed)


def _rope_tables(s):
    inv = ROPE_THETA ** (-jnp.arange(0, B_DIM, 2, dtype=F32) / B_DIM)
    ang = jnp.arange(s, dtype=F32)[:, None] * inv[None, :]
    cos, sin = jnp.cos(ang), jnp.sin(ang)
    return jnp.concatenate([cos, cos], axis=1), jnp.concatenate([-sin, sin], axis=1)


def _b_block(q_t, z_t, k3, v3, cos_q, sin_q, cos_k, sin_k, pb, n, nb):
    w = WINDOW
    r = lax.broadcasted_iota(jnp.int32, (B_DIM, B_DIM), 0)
    c = lax.broadcasted_iota(jnp.int32, (B_DIM, B_DIM), 1)
    swap = (r == (c + B_DIM // 2) % B_DIM).astype(F32)
    qi = lax.broadcasted_iota(jnp.int32, (w, 3 * w), 0)
    kj = lax.broadcasted_iota(jnp.int32, (w, 3 * w), 1)
    kpos = kj + (n - 1) * w
    mask = (jnp.abs(kj - w - qi) <= w) & (kpos >= 0) & (kpos < nb * w)
    lane = lax.broadcasted_iota(jnp.int32, (1, LANE), 1)
    qn, kn = pb[0:1, :B_DIM], pb[1:2, :B_DIM]
    outs = []
    for hk in range(B_KV):
        k = _rms(k3[:, hk * B_DIM:(hk + 1) * B_DIM], kn)
        k = k * cos_k + _dot(k, swap) * sin_k
        v = v3[:, hk * B_DIM:(hk + 1) * B_DIM]
        for g in range(B_HEADS // B_KV):
            hq = hk * (B_HEADS // B_KV) + g
            q = _rms(q_t[:, hq * B_DIM:(hq + 1) * B_DIM], qn)
            q = q * cos_q + _dot(q, swap) * sin_q
            sink = jnp.sum(jnp.where(lane == hq, pb[2:3, :], 0.0), axis=1, keepdims=True)
            s = _mm_nt(q, k) * (B_DIM ** -0.5)
            s = jnp.where(mask, s, -jnp.inf)
            m = jnp.maximum(jnp.max(s, axis=1, keepdims=True), sink)
            p = jnp.exp(s - m)
            p = p / (jnp.sum(p, axis=1, keepdims=True) + jnp.exp(sink - m))
            outs.append(_mm(p, v))
    return jnp.concatenate(outs, axis=1) * _silu(z_t)


def _b_specs(s):
    nb = s // WINDOW
    qsp = pl.BlockSpec((WINDOW, 512), lambda n: (n, P_QB // 512))
    zsp = pl.BlockSpec((WINDOW, 512), lambda n: (n, P_ZB // 512))

    def three(col, width):
        return [pl.BlockSpec((WINDOW, width), lambda n: (jnp.maximum(n - 1, 0), col)),
                pl.BlockSpec((WINDOW, width), lambda n: (n, col)),
                pl.BlockSpec((WINDOW, width), lambda n: (jnp.minimum(n + 1, nb - 1), col))]

    tab = pl.BlockSpec((WINDOW, B_DIM), lambda n: (n, 0))
    small = pl.BlockSpec((8, LANE), lambda n: (0, 0))
    specs = [qsp, zsp] + three(P_KB // LANE, LANE) + three(P_VB // LANE, LANE) + [tab, tab] + three(0, B_DIM) + three(0, B_DIM) + [small]
    return nb, specs


def _b_args(proj, cos2, sin2, pb):
    return (proj, proj, proj, proj, proj, proj, proj, proj, cos2, sin2, cos2, cos2, cos2, sin2, sin2, sin2, pb)


def _b_load(refs):
    (q_ref, z_ref, kp, kc, kx, vp, vc, vx, cq, sq, ckp, ckc, ckx, skp, skc, skx, pb_ref) = refs
    cat = lambda *r: jnp.concatenate([t[...] for t in r], axis=0)
    return (q_ref[...], z_ref[...], cat(kp, kc, kx), cat(vp, vc, vx), cq[...], sq[...], cat(ckp, ckc, ckx),
            cat(skp, skc, skx), pb_ref[...])


def _attn_b_fwd(proj, cos2, sin2, pb):
    s = proj.shape[0]
    nb, specs = _b_specs(s)

    def body(*refs):
        o_ref = refs[-1]
        args = _b_load(refs[:-1])
        o_ref[...] = _b_block(*args, pl.program_id(0), nb).astype(BF16)

    return pl.pallas_call(
        body, name="attn_b_fwd", grid=(nb,), in_specs=specs,
        out_specs=pl.BlockSpec((WINDOW, 512), lambda n: (n, 0)),
        out_shape=jax.ShapeDtypeStruct((s, 512), BF16), compiler_params=_params(("parallel",)),
    )(*_b_args(proj, cos2, sin2, pb))


def _attn_b_bwd(proj, cos2, sin2, pb, d_mixed):
    s = proj.shape[0]
    nb, specs = _b_specs(s)
    w = WINDOW

    def body(*refs):
        dm_ref, dq_ref, dz_ref, dk_ref, dv_ref, dpb_ref = refs[-6:]
        n = pl.program_id(0)
        q_t, z_t, k3, v3, cq, sq, ck, sk, pb_v = _b_load(refs[:-6])

        @pl.when(n == 0)
        def _():
            dk_ref[...] = jnp.zeros_like(dk_ref)
            dv_ref[...] = jnp.zeros_like(dv_ref)
            dpb_ref[...] = jnp.zeros_like(dpb_ref)

        def f(q_, z_, k_, v_, pb_):
            return _b_block(q_, z_, k_, v_, cq, sq, ck, sk, pb_, n, nb)

        _, vjp = jax.vjp(f, q_t, z_t, k3, v3, pb_v)
        dq, dz, dk3, dv3, dpb = vjp(dm_ref[...])
        dq_ref[...] = dq
        dz_ref[...] = dz
        dpb_ref[...] += dpb

        def add(j, cond):
            @pl.when(cond)
            def _():
                rows = pl.ds(pl.multiple_of((n - 1 + j) * w, w), w)
                dk_ref[rows, :] += dk3[j * w:(j + 1) * w, :]
                dv_ref[rows, :] += dv3[j * w:(j + 1) * w, :]

        add(0, n > 0)
        add(1, n >= 0)
        add(2, n < nb - 1)

    blk = pl.BlockSpec((w, 512), lambda n: (n, 0))
    whole = pl.BlockSpec((s, LANE), lambda n: (0, 0))
    small = pl.BlockSpec((8, LANE), lambda n: (0, 0))
    return pl.pallas_call(
        body, name="attn_b_bwd", grid=(nb,),
        in_specs=specs + [pl.BlockSpec((w, 512), lambda n: (n, 2))],
        out_specs=[blk, blk, whole, whole, small],
        out_shape=[jax.ShapeDtypeStruct((s, 512), F32), jax.ShapeDtypeStruct((s, 512), F32),
                   jax.ShapeDtypeStruct((s, LANE), F32), jax.ShapeDtypeStruct((s, LANE), F32),
                   jax.ShapeDtypeStruct((8, LANE), F32)],
        compiler_params=_params(("arbitrary",)),
    )(*_b_args(proj, cos2, sin2, pb), d_mixed)


def _mem_kv_fwd(mem, mem_norm_w, w_kv):
    def body(mem_ref, nw_ref, w_ref, kv_ref):
        mn = _rms(mem_ref[...], nw_ref[...]).astype(BF16)
        kv_ref[...] = jnp.dot(mn, w_ref[...], preferred_element_type=F32)

    return pl.pallas_call(
        body, name="mem_kv_fwd", out_shape=jax.ShapeDtypeStruct((MEM_LEN, 2 * C_HEADS * C_DIM), F32),
        compiler_params=_params(),
    )(mem, mem_norm_w, w_kv)


def _mem_kv_bwd(mem, mem_norm_w, w_kv, d_kv):
    def body(mem_ref, nw_ref, w_ref, g_ref, gw_ref, gn_ref):
        mn, vjp = jax.vjp(_rms, mem_ref[...], nw_ref[...])
        g = g_ref[...].astype(BF16)
        gw_ref[...] = lax.dot_general(mn.astype(BF16), g, (((0,), (0,)), ((), ())), preferred_element_type=F32)
        d_mn = lax.dot_general(g, w_ref[...], (((1,), (1,)), ((), ())), preferred_element_type=F32)
        gn_ref[...] = vjp(d_mn)[1]

    return pl.pallas_call(
        body, name="mem_kv_bwd",
        out_shape=[jax.ShapeDtypeStruct((D_MODEL, 2 * C_HEADS * C_DIM), F32), jax.ShapeDtypeStruct((1, D_MODEL), F32)],
        compiler_params=_params(),
    )(mem, mem_norm_w, w_kv, d_kv)


def _c_tile(q_t, z_t, kvm, pc):
    width = C_HEADS * C_DIM
    outs = []
    for h in range(C_HEADS):
        q = _rms(q_t[:, h * C_DIM:(h + 1) * C_DIM], pc[0:1, :])
        k = _rms(kvm[:, h * C_DIM:(h + 1) * C_DIM], pc[1:2, :])
        v = kvm[:, width + h * C_DIM:width + (h + 1) * C_DIM]
        s = _mm_nt(q, k) * (C_DIM ** -0.5)
        p = jnp.exp(s - jnp.max(s, axis=1, keepdims=True))
        p = p / jnp.sum(p, axis=1, keepdims=True)
        outs.append(_mm(p, v))
    return jnp.concatenate(outs, axis=1) * _silu(z_t)


def _attn_c_fwd(proj, kvm, pc, tq=256):
    s = proj.shape[0]

    def body(q_ref, z_ref, kv_ref, pc_ref, o_ref):
        o_ref[...] = _c_tile(q_ref[...], z_ref[...], kv_ref[...], pc_ref[...]).astype(BF16)

    return pl.pallas_call(
        body, name="attn_c_fwd", grid=(s // tq,),
        in_specs=[pl.BlockSpec((tq, 512), lambda i: (i, P_QC // 512)), pl.BlockSpec((tq, 512), lambda i: (i, P_ZC // 512)),
                  pl.BlockSpec(kvm.shape, lambda i: (0, 0)), pl.BlockSpec((8, LANE), lambda i: (0, 0))],
        out_specs=pl.BlockSpec((tq, 512), lambda i: (i, 0)),
        out_shape=jax.ShapeDtypeStruct((s, 512), BF16), compiler_params=_params(("parallel",)),
    )(proj, proj, kvm, pc)


def _attn_c_bwd(proj, kvm, pc, d_mixed, tq=256):
    s = proj.shape[0]

    def body(q_ref, z_ref, kv_ref, pc_ref, dm_ref, dq_ref, dz_ref, dkv_ref, dpc_ref):
        @pl.when(pl.program_id(0) == 0)
        def _():
            dkv_ref[...] = jnp.zeros_like(dkv_ref)
            dpc_ref[...] = jnp.zeros_like(dpc_ref)

        _, vjp = jax.vjp(_c_tile, q_ref[...], z_ref[...], kv_ref[...], pc_ref[...])
        dq, dz, dkv, dpc = vjp(dm_ref[...])
        dq_ref[...] = dq
        dz_ref[...] = dz
        dkv_ref[...] += dkv
        dpc_ref[...] += dpc

    blk = pl.BlockSpec((tq, 512), lambda i: (i, 0))
    kvs = pl.BlockSpec(kvm.shape, lambda i: (0, 0))
    small = pl.BlockSpec((8, LANE), lambda i: (0, 0))
    return pl.pallas_call(
        body, name="attn_c_bwd", grid=(s // tq,),
        in_specs=[pl.BlockSpec((tq, 512), lambda i: (i, P_QC // 512)), pl.BlockSpec((tq, 512), lambda i: (i, P_ZC // 512)),
                  kvs, small, pl.BlockSpec((tq, 512), lambda i: (i, 3))],
        out_specs=[blk, blk, kvs, small],
        out_shape=[jax.ShapeDtypeStruct((s, 512), F32), jax.ShapeDtypeStruct((s, 512), F32),
                   jax.ShapeDtypeStruct(kvm.shape, F32), jax.ShapeDtypeStruct((8, LANE), F32)],
        compiler_params=_params(("arbitrary",)),
    )(proj, proj, kvm, pc, d_mixed)


def _pad_row(v, width=LANE):
    v = v.reshape(1, -1)
    return jnp.pad(v, ((0, 0), (0, width - v.shape[1])))


def _local_step(x, mem, target, norm_w, w_perm, conv_w, pa, pb, pc, mem_norm_w, w_kv, w_out):
    s = x.shape[0]
    cos2, sin2 = _rope_tables(s)
    hn = _rms_fwd(x, norm_w)
    wide = dict(tm=1024, tn=512, tk=2048)
    proj = _matmul(hn, w_perm, "nn", F32, "mm_proj", **wide)
    cqkv = _conv_fwd(proj, conv_w)
    mixed_a = _delta_fwd(cqkv, proj, pa)
    mixed_b = _attn_b_fwd(proj, cos2, sin2, pb)
    kvm = _mem_kv_fwd(mem, mem_norm_w, w_kv)
    mixed_c = _attn_c_fwd(proj, kvm, pc)
    mixed = jnp.concatenate([mixed_a, mixed_b, mixed_c], axis=1)
    mo = _matmul(mixed, w_out, "nn", F32, "mm_out", **wide)
    dy, dyb, loss_parts = _loss_dy(x, mo, target)

    d_mixed = _matmul(dyb, w_out, "nt", F32, "mm_dmixed", **wide)
    g_w_out = _matmul(mixed, dyb, "tn", F32, "mm_gwout", **wide)
    d_qc, d_zc, d_kvm, d_pc = _attn_c_bwd(proj, kvm, pc, d_mixed)
    g_w_kv, g_mem_norm = _mem_kv_bwd(mem, mem_norm_w, w_kv, d_kvm)
    d_qb, d_zb, d_kb, d_vb, d_pb = _attn_b_bwd(proj, cos2, sin2, pb, d_mixed)
    d_c, d_za, d_gt, d_pa = _delta_bwd(cqkv, proj, pa, d_mixed)
    d_qkv, g_conv = _conv_bwd(proj, conv_w, d_c)
    d_proj = jnp.concatenate([d_qkv, d_za, d_qb, d_zb, d_qc, d_zc, d_kb, d_vb, d_gt,
                              jnp.zeros((s, P_WIDTH - P_GT - LANE), F32)], axis=1).astype(BF16)
    d_hn = _matmul(d_proj, w_perm, "nt", F32, "mm_dhn", tm=1024, tn=1024, tk=512)
    g_w_perm = _matmul(hn, d_proj, "tn", F32, "mm_gwin", **wide)
    g_x, g_norm = _rms_bwd(x, norm_w, d_hn, dy)
    return dict(loss_parts=loss_parts, g_x=g_x, g_norm=g_norm, g_w_perm=g_w_perm, g_conv=g_conv, d_pa=d_pa,
                d_pb=d_pb, d_pc=d_pc, g_mem_norm=g_mem_norm, g_w_kv=g_w_kv, g_w_out=g_w_out)


def _permute_cols(w):
    pad = jnp.zeros((w.shape[0], P_WIDTH - IN_WIDTH), w.dtype)
    return jnp.concatenate([w[:, :O_GT], w[:, O_QB:O_KB], w[:, O_ZB:O_QC], w[:, O_QC:O_ZC], w[:, O_ZC:IN_WIDTH],
                            w[:, O_KB:O_VB], w[:, O_VB:O_ZB], w[:, O_GT:O_QB], pad], axis=1)


def _unpermute_cols(g):
    return jnp.concatenate([g[:, :P_QB], g[:, P_GT:P_GT + 32], g[:, P_QB:P_ZB], g[:, P_KB:P_VB], g[:, P_VB:P_GT],
                            g[:, P_ZB:P_QC], g[:, P_QC:P_ZC], g[:, P_ZC:P_KB]], axis=1)


HBM = pl.BlockSpec(memory_space=pltpu.HBM)


def _place():
    x, y, c = lax.axis_index("x"), lax.axis_index("y"), lax.axis_index("c")
    chips = [(1 - x, y), (x, 1 - y), (1 - x, 1 - y)]
    return x, y, c, 2 * x + y, chips, [2 * cx + cy for cx, cy in chips]


COPY_CHUNK_ROWS = 128
COPY_MAX_CHUNKS = 8


class _Copies:
    def __init__(self, make, src, dst):
        rows = src.shape[-2]
        n = max(1, min(COPY_MAX_CHUNKS, rows // COPY_CHUNK_ROWS))
        assert rows % n == 0
        step = rows // n
        lead = (slice(None),) * (len(src.shape) - 2)
        self.whole = make(src, dst)
        self.parts = [self.whole] if n == 1 else [
            make(src.at[lead + (pl.ds(i * step, step), slice(None))], dst.at[lead + (pl.ds(i * step, step), slice(None))])
            for i in range(n)]

    def start(self):
        for p in self.parts:
            p.start()

    def wait(self):
        self.whole.wait()

    def wait_send(self):
        self.whole.wait_send()

    def wait_recv(self):
        self.whole.wait_recv()


def _remote(src, dst, send_sems, recv_sems, k, to):
    def make(s, d):
        return pltpu.make_async_remote_copy(src_ref=s, dst_ref=d, send_sem=send_sems.at[k], recv_sem=recv_sems.at[k],
                                            device_id=to, device_id_type=MESH)
    return _Copies(make, src, dst)


def _local(src, dst, sem):
    return _Copies(lambda s, d: pltpu.make_async_copy(s, d, sem), src, dst)


def _half_rows(ref, c):
    half = ref.shape[-2] // 2
    return pl.ds(pl.multiple_of(c * half, 8), half)


def _all_gather_weights(w_in_b, w_out_b, w_kv_b, conv_b):
    bigs = (w_in_b, w_out_b, w_kv_b)
    n_big = len(bigs)

    def body(*refs):
        srcs, conv_src = refs[:n_big], refs[n_big]
        dsts, conv_dst = refs[n_big + 1:2 * n_big + 1], refs[2 * n_big + 1]
        send_sems, recv_sems, local_sems = refs[2 * n_big + 2:]
        x, y, c, me, chips, chip_ids = _place()
        sibling = (x, y, 1 - c)
        local = [_local(src, dst.at[me], local_sems.at[a]) for a, (src, dst) in enumerate(zip(srcs, dsts))]
        local.append(_local(conv_src, conv_dst.at[me], local_sems.at[n_big]))
        for cp in local:
            cp.start()
        sends = []
        for a, (src, dst) in enumerate(zip(srcs, dsts)):
            mine = _half_rows(src, c)
            for j, chip in enumerate(chips):
                sends.append(_remote(src.at[mine, :], dst.at[me, mine, :], send_sems, recv_sems, 6 * a + j, (*chip, c)))
        for j, chip in enumerate(chips):
            sends.append(_remote(conv_src, conv_dst.at[me], send_sems, recv_sems, 6 * n_big + j, (*chip, c)))
        for cp in sends:
            cp.start()
        passed = []
        for a, (src, dst) in enumerate(zip(srcs, dsts)):
            mine = _half_rows(src, c)
            for j, cid in enumerate(chip_ids):
                landed = dst.at[cid, mine, :]
                _remote(landed, landed, send_sems, recv_sems, 6 * a + j, sibling).wait_recv()
                cp = _remote(landed, landed, send_sems, recv_sems, 6 * a + 3 + j, sibling)
                cp.start()
                passed.append(cp)
        for a, (src, dst) in enumerate(zip(srcs, dsts)):
            other = _half_rows(src, 1 - c)
            for j, cid in enumerate(chip_ids):
                landed = dst.at[cid, other, :]
                _remote(landed, landed, send_sems, recv_sems, 6 * a + 3 + j, sibling).wait_recv()
        for j, cid in enumerate(chip_ids):
            _remote(conv_src, conv_dst.at[cid], send_sems, recv_sems, 6 * n_big + j, sibling).wait_recv()
        for cp in sends + passed:
            cp.wait_send()
        for cp in local:
            cp.wait()

    n_sem = 6 * n_big + 3
    return pl.pallas_call(
        body, name="all_gather_weights",
        out_shape=[jax.ShapeDtypeStruct((N_CHIPS,) + w.shape, w.dtype) for w in bigs + (conv_b,)],
        in_specs=[HBM] * (n_big + 1), out_specs=[HBM] * (n_big + 1),
        scratch_shapes=[pltpu.SemaphoreType.DMA((n_sem,)), pltpu.SemaphoreType.DMA((n_sem,)),
                        pltpu.SemaphoreType.DMA((n_big + 1,))],
    )(*bigs, conv_b)


def _pair_exchange(grads):
    n = len(grads)

    def body(*refs):
        srcs, owns, gots = refs[:n], refs[n:2 * n], refs[2 * n:3 * n]
        send_sems, recv_sems, local_sems = refs[3 * n:]
        x, y, c, _, _, _ = _place()
        copies = []
        for a in range(n):
            mine, other = _half_rows(srcs[a], c), _half_rows(srcs[a], 1 - c)
            keep = _local(srcs[a].at[:, mine, :], owns[a], local_sems.at[a])
            keep.start()
            give = _remote(srcs[a].at[:, other, :], gots[a], send_sems, recv_sems, a, (x, y, 1 - c))
            give.start()
            copies += [keep, give]
        for cp in copies:
            cp.wait()

    halves = [jax.ShapeDtypeStruct((g.shape[0], g.shape[1] // 2, g.shape[2]), g.dtype) for g in grads]
    out = pl.pallas_call(
        body, name="grad_pair_exchange", out_shape=halves + halves,
        in_specs=[HBM] * n, out_specs=[HBM] * (2 * n),
        scratch_shapes=[pltpu.SemaphoreType.DMA((n,)), pltpu.SemaphoreType.DMA((n,)), pltpu.SemaphoreType.DMA((n,))],
    )(*grads)
    return out[:n], out[n:]


def _chip_exchange(halves):
    n = len(halves)

    def body(*refs):
        srcs, lands = refs[:n], refs[n:2 * n]
        send_sems, recv_sems, local_sems = refs[2 * n:]
        x, y, c, me, chips, chip_ids = _place()
        copies = []
        for a in range(n):
            keep = _local(srcs[a].at[me], lands[a].at[me], local_sems.at[a])
            keep.start()
            copies.append(keep)
            for j, (chip, cid) in enumerate(zip(chips, chip_ids)):
                give = _remote(srcs[a].at[cid], lands[a].at[me], send_sems, recv_sems, 3 * a + j, (*chip, c))
                give.start()
                copies.append(give)
        for a in range(n):
            for j, cid in enumerate(chip_ids):
                _remote(srcs[a].at[cid], lands[a].at[cid], send_sems, recv_sems, 3 * a + j, (x, y, c)).wait_recv()
        for a in range(n):
            copies[4 * a].wait()
            for j in range(3):
                copies[4 * a + 1 + j].wait_send()

    return pl.pallas_call(
        body, name="grad_chip_exchange", out_shape=[jax.ShapeDtypeStruct(h.shape, h.dtype) for h in halves],
        in_specs=[HBM] * n, out_specs=[HBM] * n,
        scratch_shapes=[pltpu.SemaphoreType.DMA((3 * n,)), pltpu.SemaphoreType.DMA((3 * n,)),
                        pltpu.SemaphoreType.DMA((n,))],
    )(*halves)


def _pair_gather(halves):
    n = len(halves)

    def body(*refs):
        srcs, fulls = refs[:n], refs[n:2 * n]
        send_sems, recv_sems, local_sems = refs[2 * n:]
        x, y, c, _, _, _ = _place()
        copies = []
        for a in range(n):
            mine = _half_rows(fulls[a], c)
            keep = _local(srcs[a], fulls[a].at[mine, :], local_sems.at[a])
            keep.start()
            give = _remote(srcs[a], fulls[a].at[mine, :], send_sems, recv_sems, a, (x, y, 1 - c))
            give.start()
            copies += [keep, give]
        for a in range(n):
            other = _half_rows(fulls[a], 1 - c)
            copies[2 * a].wait()
            copies[2 * a + 1].wait_send()
            _remote(srcs[a], fulls[a].at[other, :], send_sems, recv_sems, a, (x, y, 1 - c)).wait_recv()

    return pl.pallas_call(
        body, name="grad_pair_gather",
        out_shape=[jax.ShapeDtypeStruct((2 * h.shape[0], h.shape[1]), h.dtype) for h in halves],
        in_specs=[pl.BlockSpec(memory_space=pltpu.VMEM)] * n, out_specs=[HBM] * n,
        scratch_shapes=[pltpu.SemaphoreType.DMA((n,)), pltpu.SemaphoreType.DMA((n,)), pltpu.SemaphoreType.DMA((n,))],
    )(*halves)


def _all_reduce_small(p):
    n_dev = 8

    def body(p_ref, o_ref, land, send_sems, recv_sems):
        x, y, c = lax.axis_index("x"), lax.axis_index("y"), lax.axis_index("c")
        me = 4 * x + 2 * y + c
        land[me] = p_ref[...]
        sends = []
        for k in range(1, n_dev):
            fx, fy, fc = (k >> 2) & 1, (k >> 1) & 1, k & 1
            to = (x ^ fx, y ^ fy, c ^ fc)
            cp = _remote(p_ref, land.at[me], send_sems, recv_sems, k - 1, to)
            cp.start()
            sends.append(cp)
        for k in range(1, n_dev):
            _remote(p_ref, land.at[me ^ k], send_sems, recv_sems, k - 1, (x, y, c)).wait_recv()
        total = land[0]
        for d in range(1, n_dev):
            total = total + land[d]
        o_ref[...] = total
        for cp in sends:
            cp.wait_send()

    vm = pl.BlockSpec(memory_space=pltpu.VMEM)
    return pl.pallas_call(
        body, name="all_reduce_small", out_shape=jax.ShapeDtypeStruct(p.shape, p.dtype), in_specs=[vm], out_specs=vm,
        scratch_shapes=[pltpu.VMEM((n_dev,) + p.shape, p.dtype), pltpu.SemaphoreType.DMA((n_dev - 1,)),
                        pltpu.SemaphoreType.DMA((n_dev - 1,))],
    )(p)


def _row_tile(rows, cap=256):
    return cap if rows % cap == 0 else rows


def _add2(a, b, name):
    n, r, c = a.shape
    tr = _row_tile(r)

    def body(a_ref, b_ref, o_ref):
        o_ref[...] = a_ref[...] + b_ref[...]

    blk = pl.BlockSpec((None, tr, c), lambda i, j: (i, j, 0))
    return pl.pallas_call(body, name=name, grid=(n, r // tr), in_specs=[blk, blk], out_specs=blk,
                          out_shape=jax.ShapeDtypeStruct(a.shape, a.dtype),
                          compiler_params=_params(("parallel", "parallel")))(a, b)


def _sum_slots(land, name):
    n, r, c = land.shape
    tr = _row_tile(r)

    def body(l_ref, o_ref):
        total = l_ref[0]
        for j in range(1, n):
            total = total + l_ref[j]
        o_ref[...] = total

    return pl.pallas_call(body, name=name, grid=(r // tr,), in_specs=[pl.BlockSpec((n, tr, c), lambda i: (0, i, 0))],
                          out_specs=pl.BlockSpec((tr, c), lambda i: (i, 0)),
                          out_shape=jax.ShapeDtypeStruct((r, c), land.dtype), compiler_params=_params(("parallel",)))(land)


def _adamw(w, g, m, v, name):
    r, c = w.shape
    tr = _row_tile(r)

    def body(w_ref, g_ref, m_ref, v_ref, d_ref, mo_ref, vo_ref):
        g_ = g_ref[...]
        m2 = ADAM_B1 * m_ref[...] + (1.0 - ADAM_B1) * g_
        v2 = ADAM_B2 * v_ref[...] + (1.0 - ADAM_B2) * jnp.square(g_)
        m_hat = m2 / (1.0 - ADAM_B1 ** ADAM_STEP)
        v_hat = v2 / (1.0 - ADAM_B2 ** ADAM_STEP)
        d_ref[...] = -ADAM_LR * (m_hat / (jnp.sqrt(v_hat) + ADAM_EPS) + ADAM_WD * w_ref[...])
        mo_ref[...] = m2
        vo_ref[...] = v2

    blk = pl.BlockSpec((tr, c), lambda i: (i, 0))
    return pl.pallas_call(body, name=name, grid=(r // tr,), in_specs=[blk] * 4, out_specs=[blk] * 3,
                          out_shape=[jax.ShapeDtypeStruct(w.shape, F32)] * 3, compiler_params=_params(("parallel",)))(w, g, m, v)


SMALL_NAMES = ("norm_w", "mem_norm_w", "o_norm_a", "q_norm_c", "k_norm_c", "q_norm_b", "k_norm_b",
               "a_log_fwd", "a_log_bwd", "dt_bias_fwd", "dt_bias_bwd", "sink_b")
SMALL_SIZES = (2048, 2048, 128, 128, 128, 64, 64, 8, 8, 8, 8, 8)
SMALL_LOSS = sum(SMALL_SIZES)
SMALL_CONV = 5120
SMALL_TOTAL = SMALL_CONV + CONV_K * 3 * A_WIDTH
SMALL_ROWS = SMALL_TOTAL // LANE


def _pack_small(parts, extra=None, conv=None):
    vec = [parts[n].reshape(-1) for n in SMALL_NAMES]
    vec.append(jnp.zeros((1,), F32) if extra is None else extra.reshape(1))
    vec.append(jnp.zeros((SMALL_CONV - SMALL_LOSS - 1,), F32))
    vec.append(jnp.zeros((SMALL_TOTAL - SMALL_CONV,), F32) if conv is None else conv.reshape(-1))
    return jnp.concatenate(vec).reshape(SMALL_ROWS, LANE)


def _unpack_small(packed):
    flat = packed.reshape(-1)
    out, off = {}, 0
    for n, size in zip(SMALL_NAMES, SMALL_SIZES):
        out[n] = flat[off:off + size].reshape(1, size)
        off += size
    return out


WEIGHT_ORDER = ("norm_w", "w_in", "conv_w_a", "a_log_fwd", "a_log_bwd", "dt_bias_fwd", "dt_bias_bwd", "o_norm_a",
                "q_norm_b", "k_norm_b", "sink_b", "mem_norm_w", "w_mem_kv", "q_norm_c", "k_norm_c", "w_out")


def kernel(x, mem, norm_w, w_in, conv_w_a, a_log_fwd, a_log_bwd, dt_bias_fwd, dt_bias_bwd, o_norm_a, q_norm_b, k_norm_b, sink_b, mem_norm_w, w_mem_kv, q_norm_c, k_norm_c, w_out, loss_target, m_norm_w, m_w_in, m_conv_w_a, m_a_log_fwd, m_a_log_bwd, m_dt_bias_fwd, m_dt_bias_bwd, m_o_norm_a, m_q_norm_b, m_k_norm_b, m_sink_b, m_mem_norm_w, m_w_mem_kv, m_q_norm_c, m_k_norm_c, m_w_out, v_norm_w, v_w_in, v_conv_w_a, v_a_log_fwd, v_a_log_bwd, v_dt_bias_fwd, v_dt_bias_bwd, v_o_norm_a, v_q_norm_b, v_k_norm_b, v_sink_b, v_mem_norm_w, v_w_mem_kv, v_q_norm_c, v_k_norm_c, v_w_out):
    weights = dict(norm_w=norm_w, w_in=w_in, conv_w_a=conv_w_a, a_log_fwd=a_log_fwd, a_log_bwd=a_log_bwd,
                   dt_bias_fwd=dt_bias_fwd, dt_bias_bwd=dt_bias_bwd, o_norm_a=o_norm_a, q_norm_b=q_norm_b,
                   k_norm_b=k_norm_b, sink_b=sink_b, mem_norm_w=mem_norm_w, w_mem_kv=w_mem_kv, q_norm_c=q_norm_c,
                   k_norm_c=k_norm_c, w_out=w_out)
    mom1 = dict(norm_w=m_norm_w, w_in=m_w_in, conv_w_a=m_conv_w_a, a_log_fwd=m_a_log_fwd, a_log_bwd=m_a_log_bwd,
                dt_bias_fwd=m_dt_bias_fwd, dt_bias_bwd=m_dt_bias_bwd, o_norm_a=m_o_norm_a, q_norm_b=m_q_norm_b,
                k_norm_b=m_k_norm_b, sink_b=m_sink_b, mem_norm_w=m_mem_norm_w, w_mem_kv=m_w_mem_kv,
                q_norm_c=m_q_norm_c, k_norm_c=m_k_norm_c, w_out=m_w_out)
    mom2 = dict(norm_w=v_norm_w, w_in=v_w_in, conv_w_a=v_conv_w_a, a_log_fwd=v_a_log_fwd, a_log_bwd=v_a_log_bwd,
                dt_bias_fwd=v_dt_bias_fwd, dt_bias_bwd=v_dt_bias_bwd, o_norm_a=v_o_norm_a, q_norm_b=v_q_norm_b,
                k_norm_b=v_k_norm_b, sink_b=v_sink_b, mem_norm_w=v_mem_norm_w, w_mem_kv=v_w_mem_kv,
                q_norm_c=v_q_norm_c, k_norm_c=v_k_norm_c, w_out=v_w_out)
    chip = 2 * lax.axis_index("x") + lax.axis_index("y")

    w_in4, w_out4, w_kv4, conv4 = _all_gather_weights(w_in[0].astype(BF16), w_out[0].astype(BF16),
                                                      w_mem_kv[0].astype(BF16), conv_w_a[0])
    w_perm = _permute_cols(jnp.transpose(w_in4, (1, 0, 2)).reshape(D_MODEL, IN_WIDTH))
    w_out_full = w_out4.reshape(D_MODEL, D_MODEL)
    w_kv_full = w_kv4.reshape(D_MODEL, 2 * C_HEADS * C_DIM)
    conv_full = jnp.transpose(conv4, (1, 0, 2)).reshape(CONV_K, 3 * A_WIDTH)
    pa = jnp.concatenate([_pad_row(a_log_fwd), _pad_row(a_log_bwd), _pad_row(dt_bias_fwd), _pad_row(dt_bias_bwd),
                          _pad_row(o_norm_a), jnp.zeros((3, LANE), F32)], axis=0)
    pb = jnp.concatenate([_pad_row(q_norm_b), _pad_row(k_norm_b), _pad_row(sink_b), jnp.zeros((5, LANE), F32)], axis=0)
    pc = jnp.concatenate([_pad_row(q_norm_c), _pad_row(k_norm_c), jnp.zeros((6, LANE), F32)], axis=0)

    r = _local_step(x[0], mem[0], loss_target[0], norm_w, w_perm, conv_full, pa, pb, pc, mem_norm_w, w_kv_full,
                    w_out_full)

    g_in4 = jnp.transpose(_unpermute_cols(r["g_w_perm"]).reshape(D_MODEL, N_CHIPS, W_IN_BLOCK), (1, 0, 2))
    g_out4 = r["g_w_out"].reshape(N_CHIPS, D_MODEL // N_CHIPS, D_MODEL)
    g_kv4 = r["g_w_kv"].reshape(N_CHIPS, D_MODEL // N_CHIPS, 2 * C_HEADS * C_DIM)
    own, got = _pair_exchange([g_in4, g_out4, g_kv4])
    pair = [_add2(a, b, "grad_pair_sum_%d" % i) for i, (a, b) in enumerate(zip(own, got))]
    lands = _chip_exchange(pair)
    reduced = [_sum_slots(l, "grad_chip_sum_%d" % i) for i, l in enumerate(lands)]
    g_w_in, g_w_out, g_w_kv = _pair_gather(reduced)

    d_pa, d_pb, d_pc = r["d_pa"], r["d_pb"], r["d_pc"]
    small_g = dict(norm_w=r["g_norm"], mem_norm_w=r["g_mem_norm"], o_norm_a=d_pa[4], q_norm_c=d_pc[0], k_norm_c=d_pc[1],
                   q_norm_b=d_pb[0, :B_DIM], k_norm_b=d_pb[1, :B_DIM], a_log_fwd=d_pa[0, :A_HEADS],
                   a_log_bwd=d_pa[1, :A_HEADS], dt_bias_fwd=d_pa[2, :A_HEADS], dt_bias_bwd=d_pa[3, :A_HEADS],
                   sink_b=d_pb[2, :B_HEADS])
    packed = _all_reduce_small(_pack_small(small_g, jnp.sum(r["loss_parts"][:, 0, 0]), r["g_conv"]))
    flat = packed.reshape(-1)
    loss = flat[SMALL_LOSS]
    conv_sum = flat[SMALL_CONV:].reshape(CONV_K, 3 * A_WIDTH)
    conv_cols = 3 * A_WIDTH // N_CHIPS
    g_conv = lax.dynamic_slice(conv_sum, (0, chip * conv_cols), (CONV_K, conv_cols))

    grads = _unpack_small(packed)
    grads.update(w_in=g_w_in, w_mem_kv=g_w_kv, w_out=g_w_out, conv_w_a=g_conv)
    delta, new_m, new_v = {}, {}, {}
    for n in ("w_in", "w_mem_kv", "w_out", "conv_w_a"):
        delta[n], new_m[n], new_v[n] = _adamw(weights[n][0], grads[n], mom1[n][0], mom2[n][0], "adamw_" + n)
    d_s, m_s, v_s = _adamw(_pack_small(weights), packed, _pack_small(mom1), _pack_small(mom2), "adamw_small")
    d_s, m_s, v_s = _unpack_small(d_s), _unpack_small(m_s), _unpack_small(v_s)
    for n in SMALL_NAMES:
        delta[n], new_m[n], new_v[n] = d_s[n], m_s[n], v_s[n]

    def shaped(tree):
        return [tree[n].reshape(weights[n].shape) for n in WEIGHT_ORDER]

    return (loss, r["g_x"].reshape(x.shape), *shaped(grads), *shaped(delta), *shaped(new_m), *shaped(new_v))
```

```python
import functools

import jax
import jax.numpy as jnp
from jax import lax
from jax.experimental import pallas as pl
from jax.experimental.pallas import tpu as pltpu

F32 = jnp.float32
BF16 = jnp.bfloat16
HI = lax.Precision.HIGHEST
MESH = pl.DeviceIdType.MESH

D_MODEL = 2048
A_WIDTH = 1024
A_HEADS = 8
A_DIM = 128
CONV_K = 5
CHUNK = 64
B_HEADS = 8
B_KV = 2
B_DIM = 64
WINDOW = 128
C_HEADS = 4
C_DIM = 128
MEM_LEN = 256
ROPE_THETA = 10000.0
EPS = 1e-6
IN_WIDTH = 6432
N_CHIPS = 4
W_IN_BLOCK = IN_WIDTH // N_CHIPS

LANE = 128
P_QA, P_KA, P_VA, P_ZA = 0, 1024, 2048, 3072
P_QB, P_ZB, P_QC, P_ZC = 4096, 4608, 5120, 5632
P_KB, P_VB, P_GT = 6144, 6272, 6400
P_WIDTH = 6656
O_GT, O_QB, O_KB, O_VB, O_ZB, O_QC, O_ZC = 4096, 4128, 4640, 4768, 4896, 5408, 5920

ADAM_LR, ADAM_B1, ADAM_B2, ADAM_EPS, ADAM_WD, ADAM_STEP = 0.001, 0.9, 0.999, 1e-08, 0.01, 10

VMEM_LIMIT = 56 * 1024 * 1024


def _params(sem=None):
    return pltpu.CompilerParams(dimension_semantics=sem, vmem_limit_bytes=VMEM_LIMIT)


def _dot(a, b, dims=(((1,), (0,)), ((), ())), precision=HI):
    return lax.dot_general(a, b, dims, precision=precision, preferred_element_type=F32)


def _dot_nt(a, b, precision=HI):
    return _dot(a, b, (((1,), (1,)), ((), ())), precision)


def _dot_tn(a, b, precision=HI):
    return _dot(a, b, (((0,), (0,)), ((), ())), precision)


_NN = (((1,), (0,)), ((), ()))
_NT = (((1,), (1,)), ((), ()))
_TN = (((0,), (0,)), ((), ()))


def _bdot(a, b, dims):
    return lax.dot_general(a.astype(BF16), b.astype(BF16), dims, preferred_element_type=F32)


@jax.custom_vjp
def _mm(a, b):
    return _bdot(a, b, _NN)


_mm.defvjp(lambda a, b: (_bdot(a, b, _NN), (a, b)),
           lambda res, ct: (_bdot(ct, res[1], _NT), _bdot(res[0], ct, _TN)))


@jax.custom_vjp
def _mm_nt(a, b):
    return _bdot(a, b, _NT)


_mm_nt.defvjp(lambda a, b: (_bdot(a, b, _NT), (a, b)),
              lambda res, ct: (_bdot(ct, res[1], _NN), _bdot(ct, res[0], _TN)))


@jax.custom_vjp
def _mm_tn(a, b):
    return _bdot(a, b, _TN)


_mm_tn.defvjp(lambda a, b: (_bdot(a, b, _TN), (a, b)),
              lambda res, ct: (_bdot(res[1], ct, _NT), _bdot(res[0], ct, _NN)))


def _rms(t, w):
    return t * lax.rsqrt(jnp.mean(t * t, axis=-1, keepdims=True) + EPS) * w


def _l2(t):
    return t * lax.rsqrt(jnp.sum(t * t, axis=-1, keepdims=True) + EPS)


def _silu(t):
    return t * jax.nn.sigmoid(t)


def _softplus(t):
    return jnp.maximum(t, 0.0) + jnp.log1p(jnp.exp(-jnp.abs(t)))


def _matmul(a, b, mode, out_dtype, name, tm=512, tn=512, tk=512):
    (m, k) = a.shape[::-1] if mode == "tn" else a.shape
    n = b.shape[0] if mode == "nt" else b.shape[1]
    tm, tn, tk = min(tm, m), min(tn, n), min(tk, k)
    assert m % tm == 0 and n % tn == 0 and k % tk == 0, (m, n, k, tm, tn, tk)
    if mode == "nn":
        a_spec = pl.BlockSpec((tm, tk), lambda i, j, kk: (i, kk))
        b_spec = pl.BlockSpec((tk, tn), lambda i, j, kk: (kk, j))
        dims = (((1,), (0,)), ((), ()))
    elif mode == "nt":
        a_spec = pl.BlockSpec((tm, tk), lambda i, j, kk: (i, kk))
        b_spec = pl.BlockSpec((tn, tk), lambda i, j, kk: (j, kk))
        dims = (((1,), (1,)), ((), ()))
    else:
        a_spec = pl.BlockSpec((tk, tm), lambda i, j, kk: (kk, i))
        b_spec = pl.BlockSpec((tk, tn), lambda i, j, kk: (kk, j))
        dims = (((0,), (0,)), ((), ()))
    nk = k // tk

    def body_one(a_ref, b_ref, o_ref):
        o_ref[...] = _bdot(a_ref[...], b_ref[...], dims).astype(out_dtype)

    def body_acc(a_ref, b_ref, o_ref, acc_ref):
        kk = pl.program_id(2)

        @pl.when(kk == 0)
        def _():
            acc_ref[...] = jnp.zeros_like(acc_ref)

        acc_ref[...] += _bdot(a_ref[...], b_ref[...], dims)

        @pl.when(kk == nk - 1)
        def _():
            o_ref[...] = acc_ref[...].astype(out_dtype)

    return pl.pallas_call(
        body_one if nk == 1 else body_acc, name=name, grid=(m // tm, n // tn, nk),
        in_specs=[a_spec, b_spec], out_specs=pl.BlockSpec((tm, tn), lambda i, j, kk: (i, j)),
        out_shape=jax.ShapeDtypeStruct((m, n), out_dtype),
        scratch_shapes=[] if nk == 1 else [pltpu.VMEM((tm, tn), F32)],
        compiler_params=_params(("parallel", "parallel", "arbitrary")),
    )(a, b)


def _rms_fwd(x, w, tr=256):
    s, d = x.shape

    def body(x_ref, w_ref, o_ref):
        o_ref[...] = _rms(x_ref[...], w_ref[...]).astype(BF16)

    return pl.pallas_call(
        body, name="rms_fwd", grid=(s // tr,),
        in_specs=[pl.BlockSpec((tr, d), lambda i: (i, 0)), pl.BlockSpec((1, d), lambda i: (0, 0))],
        out_specs=pl.BlockSpec((tr, d), lambda i: (i, 0)),
        out_shape=jax.ShapeDtypeStruct((s, d), BF16), compiler_params=_params(("parallel",)),
    )(x, w)


def _rms_bwd(x, w, d_hn, dy, tr=256):
    s, d = x.shape

    def body(x_ref, w_ref, g_ref, dy_ref, gx_ref, gw_ref):
        _, vjp = jax.vjp(_rms, x_ref[...], w_ref[...])
        dx, dw = vjp(g_ref[...])
        gx_ref[...] = dy_ref[...] + dx

        @pl.when(pl.program_id(0) == 0)
        def _():
            gw_ref[...] = jnp.zeros_like(gw_ref)

        gw_ref[...] += dw

    row = pl.BlockSpec((tr, d), lambda i: (i, 0))
    vec = pl.BlockSpec((1, d), lambda i: (0, 0))
    return pl.pallas_call(
        body, name="rms_bwd", grid=(s // tr,), in_specs=[row, vec, row, row], out_specs=[row, vec],
        out_shape=[jax.ShapeDtypeStruct((s, d), F32), jax.ShapeDtypeStruct((1, d), F32)],
        compiler_params=_params(("arbitrary",)),
    )(x, w, d_hn, dy)


def _loss_dy(x, mo, target, tr=256):
    s, d = x.shape
    nt = s // tr

    def body(x_ref, mo_ref, t_ref, dy_ref, dyb_ref, l_ref):
        err = x_ref[...] + mo_ref[...] - t_ref[...]
        dy = err * (1.0 / d)
        dy_ref[...] = dy
        dyb_ref[...] = dy.astype(BF16)
        l_ref[...] = jnp.full(l_ref.shape, 0.5 * jnp.sum(jnp.sum(err * err, axis=1, keepdims=True) * (1.0 / d)), F32)

    row = pl.BlockSpec((tr, d), lambda i: (i, 0))
    return pl.pallas_call(
        body, name="loss_dy", grid=(nt,), in_specs=[row, row, row],
        out_specs=[row, row, pl.BlockSpec((1, 8, LANE), lambda i: (i, 0, 0))],
        out_shape=[jax.ShapeDtypeStruct((s, d), F32), jax.ShapeDtypeStruct((s, d), BF16),
                   jax.ShapeDtypeStruct((nt, 8, LANE), F32)],
        compiler_params=_params(("parallel",)),
    )(x, mo, target)


def _shift_rows(t, s):
    if s == 0:
        return t
    n = t.shape[0]
    rolled = pltpu.roll(t, (-s) % n, axis=0)
    idx = lax.broadcasted_iota(jnp.int32, t.shape, 0) + s
    return jnp.where((idx >= 0) & (idx < n), rolled, 0.0)


def _conv_fwd(proj, conv_w):
    s = proj.shape[0]
    nblk = 3 * A_WIDTH // LANE

    def body(x_ref, w_ref, o_ref):
        x = x_ref[...]
        acc = jnp.zeros_like(x)
        for j in range(CONV_K):
            acc = acc + w_ref[j:j + 1, :] * _shift_rows(x, j - CONV_K // 2)
        o_ref[...] = acc

    return pl.pallas_call(
        body, name="conv_fwd", grid=(nblk,),
        in_specs=[pl.BlockSpec((s, LANE), lambda i: (0, i)), pl.BlockSpec((CONV_K, LANE), lambda i: (0, i))],
        out_specs=pl.BlockSpec((None, s, LANE), lambda i: (i // A_HEADS, 0, i % A_HEADS)),
        out_shape=jax.ShapeDtypeStruct((3, s, A_WIDTH), F32), compiler_params=_params(("parallel",)),
    )(proj, conv_w)


def _conv_bwd(proj, conv_w, d_c):
    s = proj.shape[0]
    nblk = 3 * A_WIDTH // LANE

    def body(x_ref, w_ref, g_ref, dx_ref, dw_ref):
        x, g = x_ref[...], g_ref[...]
        acc = jnp.zeros_like(x)
        for j in range(CONV_K):
            off = j - CONV_K // 2
            acc = acc + w_ref[j:j + 1, :] * _shift_rows(g, -off)
            dw_ref[j:j + 1, :] = jnp.sum(_shift_rows(x, off) * g, axis=0, keepdims=True)
        dx_ref[...] = acc

    col = pl.BlockSpec((s, LANE), lambda i: (0, i))
    wsp = pl.BlockSpec((CONV_K, LANE), lambda i: (0, i))
    dsp = pl.BlockSpec((None, s, LANE), lambda i: (i // A_HEADS, 0, i % A_HEADS))
    return pl.pallas_call(
        body, name="conv_bwd", grid=(nblk,), in_specs=[col, wsp, dsp], out_specs=[col, wsp],
        out_shape=[jax.ShapeDtypeStruct((s, 3 * A_WIDTH), F32), jax.ShapeDtypeStruct((CONV_K, 3 * A_WIDTH), F32)],
        compiler_params=_params(("parallel",)),
    )(proj, conv_w, d_c)


A_STEP_HEADS = 2
A_CHAINS = 2 * A_STEP_HEADS


def _a_chain(st, cq, ck, cv, alpha, beta_raw, a_log, dt_b, incl, strict, last):
    c = CHUNK
    eye = (lax.broadcasted_iota(jnp.int32, (c, c), 0) == lax.broadcasted_iota(jnp.int32, (c, c), 1)).astype(F32)
    gb = -jnp.exp(a_log) * _softplus(alpha + dt_b)
    bb = jax.nn.sigmoid(beta_raw)
    q = _l2(_silu(cq)) * (A_DIM ** -0.5)
    k = _l2(_silu(ck))
    v = _silu(cv)

    gc = _dot(incl, jnp.broadcast_to(gb, (c, LANE)))
    tot = jnp.sum(gc * last, axis=0, keepdims=True)
    m1 = gc[:, :c]
    decay = incl * jnp.exp(incl * (m1 - m1.T))
    kb = k * bb
    vb = v * bb
    a = -(strict * decay * _mm_nt(kb, k))
    tinv = eye + a
    p = a
    for _ in range(5):
        p = _mm(p, p)
        tinv = tinv + _mm(tinv, p)
    eg = jnp.exp(gc)
    u = _mm(tinv, vb)
    w = _mm(tinv, kb * eg)
    qk = _mm_nt(q, k) * decay
    v_new = u - _mm(w, st)
    o = _mm(q * eg, st) + _mm(qk, v_new)
    st_new = st * jnp.exp(tot) + _mm_tn(k * jnp.exp(tot - gc), v_new)
    return st_new, o


def _a_step(sts, cq, ck, cv, gts, pa, h0):
    c = CHUNK
    lane = lax.broadcasted_iota(jnp.int32, (1, LANE), 1)
    ii = lax.broadcasted_iota(jnp.int32, (c, c), 0)
    jj = lax.broadcasted_iota(jnp.int32, (c, c), 1)
    row = lax.broadcasted_iota(jnp.int32, (c, 1), 0)

    def pick(t, col):
        return jnp.sum(jnp.where(lane == col, t, 0.0), axis=1, keepdims=True)

    alpha, beta_raw, a_log, dt_b, incl, strict, last = [], [], [], [], [], [], []
    for b in range(A_CHAINS):
        h, rev = h0 + b // 2, b % 2
        alpha.append(pick(gts[b], h + 8 * rev))
        beta_raw.append(pick(gts[b], h + 16 + 8 * rev))
        a_log.append(pick(pa[rev:rev + 1, :], h))
        dt_b.append(pick(pa[2 + rev:3 + rev, :], h))
        incl.append(((ii <= jj) if rev else (ii >= jj)).astype(F32))
        strict.append(((ii < jj) if rev else (ii > jj)).astype(F32))
        last.append((row == (0 if rev else c - 1)).astype(F32))
    stack = lambda ts: jnp.concatenate([t[None] for t in ts], axis=0)
    return jax.vmap(_a_chain)(sts, cq, ck, cv, stack(alpha), stack(beta_raw), stack(a_log), stack(dt_b),
                              stack(incl), stack(strict), stack(last))


def _a_final(o, za, pa):
    outs = []
    for j in range(o.shape[1] // A_DIM):
        ln = slice(j * A_DIM, (j + 1) * A_DIM)
        outs.append(_rms(o[:, ln], pa[4:5, :]) * _silu(za[:, ln]))
    return jnp.concatenate(outs, axis=1)


def _a_tiles(n, nchunk):
    tiles = []
    for b in range(A_CHAINS):
        i = (nchunk - 1 - n) if b % 2 else n
        tiles.append((i, pl.ds(pl.multiple_of(i * CHUNK, CHUNK), CHUNK), slice((b // 2) * A_DIM, (b // 2 + 1) * A_DIM)))
    return tiles


def _a_load(tiles, c_ref, gt_ref):
    cq, ck, cv = (jnp.stack([c_ref[r, sl, ln] for _, sl, ln in tiles], axis=0) for r in range(3))
    return cq, ck, cv, jnp.stack([gt_ref[sl, :] for _, sl, _ in tiles], axis=0)


def _a_scan(h0, nchunk, c_ref, gt_ref, pa, of_ref, ob_ref, s_ref):
    def step(n, sts):
        tiles = _a_tiles(n, nchunk)
        sts_new, o = _a_step(sts, *_a_load(tiles, c_ref, gt_ref), pa, h0)
        for b, (i, sl, ln) in enumerate(tiles):
            if s_ref is not None:
                s_ref[b, i] = sts[b]
            (ob_ref if b % 2 else of_ref)[sl, ln] = o[b]
        return sts_new

    lax.fori_loop(0, nchunk, step, jnp.zeros((A_CHAINS, A_DIM, A_DIM), F32))


def _a_specs(s):
    wide = A_STEP_HEADS * A_DIM
    once = pl.Buffered(1)
    trio = pl.BlockSpec((3, s, wide), lambda g: (0, 0, g), pipeline_mode=once)
    gates = pl.BlockSpec((s, LANE), lambda g: (0, P_GT // LANE))
    small = pl.BlockSpec((8, LANE), lambda g: (0, 0))

    def cols(base):
        return pl.BlockSpec((s, wide), lambda g: (0, base // wide + g), pipeline_mode=once)

    return wide, trio, gates, small, cols


def _delta_fwd(cqkv, proj, pa):
    s = cqkv.shape[1]
    nchunk = s // CHUNK
    wide, trio, gates, small, cols = _a_specs(s)

    def body(c_ref, gt_ref, za_ref, pa_ref, out_ref, of_ref, ob_ref):
        h0 = pl.program_id(0) * A_STEP_HEADS
        pa_v = pa_ref[...]
        _a_scan(h0, nchunk, c_ref, gt_ref, pa_v, of_ref, ob_ref, None)
        out_ref[...] = _a_final(of_ref[...] + ob_ref[...], za_ref[...], pa_v).astype(BF16)

    return pl.pallas_call(
        body, name="delta_fwd", grid=(A_HEADS // A_STEP_HEADS,),
        in_specs=[trio, gates, cols(P_ZA), small], out_specs=cols(0),
        out_shape=jax.ShapeDtypeStruct((s, A_WIDTH), BF16),
        scratch_shapes=[pltpu.VMEM((s, wide), F32)] * 2, compiler_params=_params(("parallel",)),
    )(cqkv, proj, proj, pa)


def _delta_bwd(cqkv, proj, pa, d_mixed):
    s = cqkv.shape[1]
    nchunk = s // CHUNK

    wide, trio, gates, small, cols = _a_specs(s)

    def body(c_ref, gt_ref, za_ref, pa_ref, dm_ref, dc_ref, dza_ref, dgt_ref, dpa_ref, s_ref, of_ref, ob_ref):
        h0 = pl.program_id(0) * A_STEP_HEADS
        pa_v = pa_ref[...]

        @pl.when(h0 == 0)
        def _():
            dgt_ref[...] = jnp.zeros_like(dgt_ref)
            dpa_ref[...] = jnp.zeros_like(dpa_ref)

        _a_scan(h0, nchunk, c_ref, gt_ref, pa_v, of_ref, ob_ref, s_ref)
        _, vjp = jax.vjp(_a_final, of_ref[...] + ob_ref[...], za_ref[...], pa_v)
        d_o, d_za, dpa0 = vjp(dm_ref[...])
        of_ref[...] = d_o
        dza_ref[...] = d_za
        dc_ref[...] = jnp.zeros_like(dc_ref)

        def step(n, carry):
            d_sts, dpa = carry
            tiles = _a_tiles(nchunk - 1 - n, nchunk)
            sts = jnp.stack([s_ref[b, i] for b, (i, _, _) in enumerate(tiles)], axis=0)
            d_o_t = jnp.stack([of_ref[sl, ln] for _, sl, ln in tiles], axis=0)
            _, vjp_c = jax.vjp(lambda *a: _a_step(*a, h0), sts, *_a_load(tiles, c_ref, gt_ref), pa_v)
            d_prev, dcq, dck, dcv, dgts, dpa_i = vjp_c((d_sts, d_o_t))
            for b, (_, sl, ln) in enumerate(tiles):
                for r, dc in enumerate((dcq, dck, dcv)):
                    dc_ref[r, sl, ln] += dc[b]
                dgt_ref[sl, :] += dgts[b]
            return d_prev, dpa + dpa_i

        _, dpa_out = lax.fori_loop(0, nchunk, step, (jnp.zeros((A_CHAINS, A_DIM, A_DIM), F32), dpa0))
        dpa_ref[...] += dpa_out

    fixed = pl.BlockSpec((s, LANE), lambda g: (0, 0))
    return pl.pallas_call(
        body, name="delta_bwd", grid=(A_HEADS // A_STEP_HEADS,),
        in_specs=[trio, gates, cols(P_ZA), small, cols(0)], out_specs=[trio, cols(0), fixed, small],
        out_shape=[jax.ShapeDtypeStruct((3, s, A_WIDTH), F32), jax.ShapeDtypeStruct((s, A_WIDTH), F32),
                   jax.ShapeDtypeStruct((s, LANE), F32), jax.ShapeDtypeStruct((8, LANE), F32)],
        scratch_shapes=[pltpu.VMEM((A_CHAINS, nchunk, A_DIM, A_DIM), F32), pltpu.VMEM((s, wide), F32),
                        pltpu.VMEM((s, wide), F32)],
        compiler_params=_params(("arbitrary",)),
    )(cqkv, proj, proj, pa, d_mixed)


def _rope_tables(s):
    inv = ROPE_THETA ** (-jnp.arange(0, B_DIM, 2, dtype=F32) / B_DIM)
    ang = jnp.arange(s, dtype=F32)[:, None] * inv[None, :]
    cos, sin = jnp.cos(ang), jnp.sin(ang)
    return jnp.concatenate([cos, cos], axis=1), jnp.concatenate([-sin, sin], axis=1)


def _b_block(q_t, z_t, k3, v3, cos_q, sin_q, cos_k, sin_k, pb, n, nb):
    w = WINDOW
    r = lax.broadcasted_iota(jnp.int32, (B_DIM, B_DIM), 0)
    c = lax.broadcasted_iota(jnp.int32, (B_DIM, B_DIM), 1)
    swap = (r == (c + B_DIM // 2) % B_DIM).astype(F32)
    qi = lax.broadcasted_iota(jnp.int32, (w, 3 * w), 0)
    kj = lax.broadcasted_iota(jnp.int32, (w, 3 * w), 1)
    kpos = kj + (n - 1) * w
    mask = (jnp.abs(kj - w - qi) <= w) & (kpos >= 0) & (kpos < nb * w)
    lane = lax.broadcasted_iota(jnp.int32, (1, LANE), 1)
    qn, kn = pb[0:1, :B_DIM], pb[1:2, :B_DIM]
    outs = []
    for hk in range(B_KV):
        k = _rms(k3[:, hk * B_DIM:(hk + 1) * B_DIM], kn)
        k = k * cos_k + _dot(k, swap) * sin_k
        v = v3[:, hk * B_DIM:(hk + 1) * B_DIM]
        for g in range(B_HEADS // B_KV):
            hq = hk * (B_HEADS // B_KV) + g
            q = _rms(q_t[:, hq * B_DIM:(hq + 1) * B_DIM], qn)
            q = q * cos_q + _dot(q, swap) * sin_q
            sink = jnp.sum(jnp.where(lane == hq, pb[2:3, :], 0.0), axis=1, keepdims=True)
            s = _mm_nt(q, k) * (B_DIM ** -0.5)
            s = jnp.where(mask, s, -jnp.inf)
            m = jnp.maximum(jnp.max(s, axis=1, keepdims=True), sink)
            p = jnp.exp(s - m)
            p = p / (jnp.sum(p, axis=1, keepdims=True) + jnp.exp(sink - m))
            outs.append(_mm(p, v))
    return jnp.concatenate(outs, axis=1) * _silu(z_t)


def _b_specs(s):
    nb = s // WINDOW
    qsp = pl.BlockSpec((WINDOW, 512), lambda n: (n, P_QB // 512))
    zsp = pl.BlockSpec((WINDOW, 512), lambda n: (n, P_ZB // 512))

    def three(col, width):
        return [pl.BlockSpec((WINDOW, width), lambda n: (jnp.maximum(n - 1, 0), col)),
                pl.BlockSpec((WINDOW, width), lambda n: (n, col)),
                pl.BlockSpec((WINDOW, width), lambda n: (jnp.minimum(n + 1, nb - 1), col))]

    tab = pl.BlockSpec((WINDOW, B_DIM), lambda n: (n, 0))
    small = pl.BlockSpec((8, LANE), lambda n: (0, 0))
    specs = [qsp, zsp] + three(P_KB // LANE, LANE) + three(P_VB // LANE, LANE) + [tab, tab] + three(0, B_DIM) + three(0, B_DIM) + [small]
    return nb, specs


def _b_args(proj, cos2, sin2, pb):
    return (proj, proj, proj, proj, proj, proj, proj, proj, cos2, sin2, cos2, cos2, cos2, sin2, sin2, sin2, pb)


def _b_load(refs):
    (q_ref, z_ref, kp, kc, kx, vp, vc, vx, cq, sq, ckp, ckc, ckx, skp, skc, skx, pb_ref) = refs
    cat = lambda *r: jnp.concatenate([t[...] for t in r], axis=0)
    return (q_ref[...], z_ref[...], cat(kp, kc, kx), cat(vp, vc, vx), cq[...], sq[...], cat(ckp, ckc, ckx),
            cat(skp, skc, skx), pb_ref[...])


def _attn_b_fwd(proj, cos2, sin2, pb):
    s = proj.shape[0]
    nb, specs = _b_specs(s)

    def body(*refs):
        o_ref = refs[-1]
        args = _b_load(refs[:-1])
        o_ref[...] = _b_block(*args, pl.program_id(0), nb).astype(BF16)

    return pl.pallas_call(
        body, name="attn_b_fwd", grid=(nb,), in_specs=specs,
        out_specs=pl.BlockSpec((WINDOW, 512), lambda n: (n, 0)),
        out_shape=jax.ShapeDtypeStruct((s, 512), BF16), compiler_params=_params(("parallel",)),
    )(*_b_args(proj, cos2, sin2, pb))


def _attn_b_bwd(proj, cos2, sin2, pb, d_mixed):
    s = proj.shape[0]
    nb, specs = _b_specs(s)
    w = WINDOW

    def body(*refs):
        dm_ref, dq_ref, dz_ref, dk_ref, dv_ref, dpb_ref = refs[-6:]
        n = pl.program_id(0)
        q_t, z_t, k3, v3, cq, sq, ck, sk, pb_v = _b_load(refs[:-6])

        @pl.when(n == 0)
        def _():
            dk_ref[...] = jnp.zeros_like(dk_ref)
            dv_ref[...] = jnp.zeros_like(dv_ref)
            dpb_ref[...] = jnp.zeros_like(dpb_ref)

        def f(q_, z_, k_, v_, pb_):
            return _b_block(q_, z_, k_, v_, cq, sq, ck, sk, pb_, n, nb)

        _, vjp = jax.vjp(f, q_t, z_t, k3, v3, pb_v)
        dq, dz, dk3, dv3, dpb = vjp(dm_ref[...])
        dq_ref[...] = dq
        dz_ref[...] = dz
        dpb_ref[...] += dpb

        def add(j, cond):
            @pl.when(cond)
            def _():
                rows = pl.ds(pl.multiple_of((n - 1 + j) * w, w), w)
                dk_ref[rows, :] += dk3[j * w:(j + 1) * w, :]
                dv_ref[rows, :] += dv3[j * w:(j + 1) * w, :]

        add(0, n > 0)
        add(1, n >= 0)
        add(2, n < nb - 1)

    blk = pl.BlockSpec((w, 512), lambda n: (n, 0))
    whole = pl.BlockSpec((s, LANE), lambda n: (0, 0))
    small = pl.BlockSpec((8, LANE), lambda n: (0, 0))
    return pl.pallas_call(
        body, name="attn_b_bwd", grid=(nb,),
        in_specs=specs + [pl.BlockSpec((w, 512), lambda n: (n, 2))],
        out_specs=[blk, blk, whole, whole, small],
        out_shape=[jax.ShapeDtypeStruct((s, 512), F32), jax.ShapeDtypeStruct((s, 512), F32),
                   jax.ShapeDtypeStruct((s, LANE), F32), jax.ShapeDtypeStruct((s, LANE), F32),
                   jax.ShapeDtypeStruct((8, LANE), F32)],
        compiler_params=_params(("arbitrary",)),
    )(*_b_args(proj, cos2, sin2, pb), d_mixed)


def _mem_kv_fwd(mem, mem_norm_w, w_kv):
    def body(mem_ref, nw_ref, w_ref, kv_ref):
        mn = _rms(mem_ref[...], nw_ref[...]).astype(BF16)
        kv_ref[...] = jnp.dot(mn, w_ref[...], preferred_element_type=F32)

    return pl.pallas_call(
        body, name="mem_kv_fwd", out_shape=jax.ShapeDtypeStruct((MEM_LEN, 2 * C_HEADS * C_DIM), F32),
        compiler_params=_params(),
    )(mem, mem_norm_w, w_kv)


def _mem_kv_bwd(mem, mem_norm_w, w_kv, d_kv):
    def body(mem_ref, nw_ref, w_ref, g_ref, gw_ref, gn_ref):
        mn, vjp = jax.vjp(_rms, mem_ref[...], nw_ref[...])
        g = g_ref[...].astype(BF16)
        gw_ref[...] = lax.dot_general(mn.astype(BF16), g, (((0,), (0,)), ((), ())), preferred_element_type=F32)
        d_mn = lax.dot_general(g, w_ref[...], (((1,), (1,)), ((), ())), preferred_element_type=F32)
        gn_ref[...] = vjp(d_mn)[1]

    return pl.pallas_call(
        body, name="mem_kv_bwd",
        out_shape=[jax.ShapeDtypeStruct((D_MODEL, 2 * C_HEADS * C_DIM), F32), jax.ShapeDtypeStruct((1, D_MODEL), F32)],
        compiler_params=_params(),
    )(mem, mem_norm_w, w_kv, d_kv)


def _c_tile(q_t, z_t, kvm, pc):
    width = C_HEADS * C_DIM
    outs = []
    for h in range(C_HEADS):
        q = _rms(q_t[:, h * C_DIM:(h + 1) * C_DIM], pc[0:1, :])
        k = _rms(kvm[:, h * C_DIM:(h + 1) * C_DIM], pc[1:2, :])
        v = kvm[:, width + h * C_DIM:width + (h + 1) * C_DIM]
        s = _mm_nt(q, k) * (C_DIM ** -0.5)
        p = jnp.exp(s - jnp.max(s, axis=1, keepdims=True))
        p = p / jnp.sum(p, axis=1, keepdims=True)
        outs.append(_mm(p, v))
    return jnp.concatenate(outs, axis=1) * _silu(z_t)


def _attn_c_fwd(proj, kvm, pc, tq=256):
    s = proj.shape[0]

    def body(q_ref, z_ref, kv_ref, pc_ref, o_ref):
        o_ref[...] = _c_tile(q_ref[...], z_ref[...], kv_ref[...], pc_ref[...]).astype(BF16)

    return pl.pallas_call(
        body, name="attn_c_fwd", grid=(s // tq,),
        in_specs=[pl.BlockSpec((tq, 512), lambda i: (i, P_QC // 512)), pl.BlockSpec((tq, 512), lambda i: (i, P_ZC // 512)),
                  pl.BlockSpec(kvm.shape, lambda i: (0, 0)), pl.BlockSpec((8, LANE), lambda i: (0, 0))],
        out_specs=pl.BlockSpec((tq, 512), lambda i: (i, 0)),
        out_shape=jax.ShapeDtypeStruct((s, 512), BF16), compiler_params=_params(("parallel",)),
    )(proj, proj, kvm, pc)


def _attn_c_bwd(proj, kvm, pc, d_mixed, tq=256):
    s = proj.shape[0]

    def body(q_ref, z_ref, kv_ref, pc_ref, dm_ref, dq_ref, dz_ref, dkv_ref, dpc_ref):
        @pl.when(pl.program_id(0) == 0)
        def _():
            dkv_ref[...] = jnp.zeros_like(dkv_ref)
            dpc_ref[...] = jnp.zeros_like(dpc_ref)

        _, vjp = jax.vjp(_c_tile, q_ref[...], z_ref[...], kv_ref[...], pc_ref[...])
        dq, dz, dkv, dpc = vjp(dm_ref[...])
        dq_ref[...] = dq
        dz_ref[...] = dz
        dkv_ref[...] += dkv
        dpc_ref[...] += dpc

    blk = pl.BlockSpec((tq, 512), lambda i: (i, 0))
    kvs = pl.BlockSpec(kvm.shape, lambda i: (0, 0))
    small = pl.BlockSpec((8, LANE), lambda i: (0, 0))
    return pl.pallas_call(
        body, name="attn_c_bwd", grid=(s // tq,),
        in_specs=[pl.BlockSpec((tq, 512), lambda i: (i, P_QC // 512)), pl.BlockSpec((tq, 512), lambda i: (i, P_ZC // 512)),
                  kvs, small, pl.BlockSpec((tq, 512), lambda i: (i, 3))],
        out_specs=[blk, blk, kvs, small],
        out_shape=[jax.ShapeDtypeStruct((s, 512), F32), jax.ShapeDtypeStruct((s, 512), F32),
                   jax.ShapeDtypeStruct(kvm.shape, F32), jax.ShapeDtypeStruct((8, LANE), F32)],
        compiler_params=_params(("arbitrary",)),
    )(proj, proj, kvm, pc, d_mixed)


def _pad_row(v, width=LANE):
    v = v.reshape(1, -1)
    return jnp.pad(v, ((0, 0), (0, width - v.shape[1])))


def _local_step(x, mem, target, norm_w, w_perm, conv_w, pa, pb, pc, mem_norm_w, w_kv, w_out):
    s = x.shape[0]
    cos2, sin2 = _rope_tables(s)
    hn = _rms_fwd(x, norm_w)
    wide = dict(tm=1024, tn=512, tk=2048)
    proj = _matmul(hn, w_perm, "nn", F32, "mm_proj", **wide)
    cqkv = _conv_fwd(proj, conv_w)
    mixed_a = _delta_fwd(cqkv, proj, pa)
    mixed_b = _attn_b_fwd(proj, cos2, sin2, pb)
    kvm = _mem_kv_fwd(mem, mem_norm_w, w_kv)
    mixed_c = _attn_c_fwd(proj, kvm, pc)
    mixed = jnp.concatenate([mixed_a, mixed_b, mixed_c], axis=1)
    mo = _matmul(mixed, w_out, "nn", F32, "mm_out", **wide)
    dy, dyb, loss_parts = _loss_dy(x, mo, target)

    d_mixed = _matmul(dyb, w_out, "nt", F32, "mm_dmixed", **wide)
    g_w_out = _matmul(mixed, dyb, "tn", F32, "mm_gwout", **wide)
    d_qc, d_zc, d_kvm, d_pc = _attn_c_bwd(proj, kvm, pc, d_mixed)
    g_w_kv, g_mem_norm = _mem_kv_bwd(mem, mem_norm_w, w_kv, d_kvm)
    d_qb, d_zb, d_kb, d_vb, d_pb = _attn_b_bwd(proj, cos2, sin2, pb, d_mixed)
    d_c, d_za, d_gt, d_pa = _delta_bwd(cqkv, proj, pa, d_mixed)
    d_qkv, g_conv = _conv_bwd(proj, conv_w, d_c)
    d_proj = jnp.concatenate([d_qkv, d_za, d_qb, d_zb, d_qc, d_zc, d_kb, d_vb, d_gt,
                              jnp.zeros((s, P_WIDTH - P_GT - LANE), F32)], axis=1).astype(BF16)
    d_hn = _matmul(d_proj, w_perm, "nt", F32, "mm_dhn", tm=1024, tn=1024, tk=512)
    g_w_perm = _matmul(hn, d_proj, "tn", F32, "mm_gwin", **wide)
    g_x, g_norm = _rms_bwd(x, norm_w, d_hn, dy)
    return dict(loss_parts=loss_parts, g_x=g_x, g_norm=g_norm, g_w_perm=g_w_perm, g_conv=g_conv, d_pa=d_pa,
                d_pb=d_pb, d_pc=d_pc, g_mem_norm=g_mem_norm, g_w_kv=g_w_kv, g_w_out=g_w_out)


def _permute_cols(w):
    pad = jnp.zeros((w.shape[0], P_WIDTH - IN_WIDTH), w.dtype)
    return jnp.concatenate([w[:, :O_GT], w[:, O_QB:O_KB], w[:, O_ZB:O_QC], w[:, O_QC:O_ZC], w[:, O_ZC:IN_WIDTH],
                            w[:, O_KB:O_VB], w[:, O_VB:O_ZB], w[:, O_GT:O_QB], pad], axis=1)


def _unpermute_cols(g):
    return jnp.concatenate([g[:, :P_QB], g[:, P_GT:P_GT + 32], g[:, P_QB:P_ZB], g[:, P_KB:P_VB], g[:, P_VB:P_GT],
                            g[:, P_ZB:P_QC], g[:, P_QC:P_ZC], g[:, P_ZC:P_KB]], axis=1)


HBM = pl.BlockSpec(memory_space=pltpu.HBM)


def _place():
    x, y, c = lax.axis_index("x"), lax.axis_index("y"), lax.axis_index("c")
    chips = [(1 - x, y), (x, 1 - y), (1 - x, 1 - y)]
    return x, y, c, 2 * x + y, chips, [2 * cx + cy for cx, cy in chips]


COPY_CHUNK_ROWS = 128
COPY_MAX_CHUNKS = 8
PAIR_PIECE_ROWS = 256


class _Copies:
    def __init__(self, make, src, dst):
        rows = src.shape[-2]
        n = max(1, min(COPY_MAX_CHUNKS, rows // COPY_CHUNK_ROWS))
        assert rows % n == 0
        step = rows // n
        lead = (slice(None),) * (len(src.shape) - 2)
        self.whole = make(src, dst)
        self.parts = [self.whole] if n == 1 else [
            make(src.at[lead + (pl.ds(i * step, step), slice(None))], dst.at[lead + (pl.ds(i * step, step), slice(None))])
            for i in range(n)]

    def start(self):
        for p in self.parts:
            p.start()

    def wait(self):
        self.whole.wait()

    def wait_send(self):
        self.whole.wait_send()

    def wait_recv(self):
        self.whole.wait_recv()


def _remote(src, dst, send_sems, recv_sems, k, to):
    def make(s, d):
        return pltpu.make_async_remote_copy(src_ref=s, dst_ref=d, send_sem=send_sems.at[k], recv_sem=recv_sems.at[k],
                                            device_id=to, device_id_type=MESH)
    return _Copies(make, src, dst)


def _local(src, dst, sem):
    return _Copies(lambda s, d: pltpu.make_async_copy(s, d, sem), src, dst)


def _half_rows(ref, c):
    half = ref.shape[-2] // 2
    return pl.ds(pl.multiple_of(c * half, 8), half)


def _all_gather_weights(w_in_b, w_out_b, w_kv_b, conv_b):
    bigs = (w_in_b, w_out_b, w_kv_b)
    n_big = len(bigs)

    def body(*refs):
        srcs, conv_src = refs[:n_big], refs[n_big]
        dsts, conv_dst = refs[n_big + 1:2 * n_big + 1], refs[2 * n_big + 1]
        send_sems, recv_sems, local_sems = refs[2 * n_big + 2:]
        x, y, c, me, chips, chip_ids = _place()
        sibling = (x, y, 1 - c)
        local = [_local(src, dst.at[me], local_sems.at[a]) for a, (src, dst) in enumerate(zip(srcs, dsts))]
        local.append(_local(conv_src, conv_dst.at[me], local_sems.at[n_big]))
        for cp in local:
            cp.start()
        sends = []
        for a, (src, dst) in enumerate(zip(srcs, dsts)):
            mine = _half_rows(src, c)
            for j, chip in enumerate(chips):
                sends.append(_remote(src.at[mine, :], dst.at[me, mine, :], send_sems, recv_sems, 6 * a + j, (*chip, c)))
        for j, chip in enumerate(chips):
            sends.append(_remote(conv_src, conv_dst.at[me], send_sems, recv_sems, 6 * n_big + j, (*chip, c)))
        for cp in sends:
            cp.start()
        passed = []
        for a, (src, dst) in enumerate(zip(srcs, dsts)):
            mine = _half_rows(src, c)
            for j, cid in enumerate(chip_ids):
                landed = dst.at[cid, mine, :]
                _remote(landed, landed, send_sems, recv_sems, 6 * a + j, sibling).wait_recv()
                cp = _remote(landed, landed, send_sems, recv_sems, 6 * a + 3 + j, sibling)
                cp.start()
                passed.append(cp)
        for a, (src, dst) in enumerate(zip(srcs, dsts)):
            other = _half_rows(src, 1 - c)
            for j, cid in enumerate(chip_ids):
                landed = dst.at[cid, other, :]
                _remote(landed, landed, send_sems, recv_sems, 6 * a + 3 + j, sibling).wait_recv()
        for j, cid in enumerate(chip_ids):
            _remote(conv_src, conv_dst.at[cid], send_sems, recv_sems, 6 * n_big + j, sibling).wait_recv()
        for cp in sends + passed:
            cp.wait_send()
        for cp in local:
            cp.wait()

    n_sem = 6 * n_big + 3
    return pl.pallas_call(
        body, name="all_gather_weights",
        out_shape=[jax.ShapeDtypeStruct((N_CHIPS,) + w.shape, w.dtype) for w in bigs + (conv_b,)],
        in_specs=[HBM] * (n_big + 1), out_specs=[HBM] * (n_big + 1),
        scratch_shapes=[pltpu.SemaphoreType.DMA((n_sem,)), pltpu.SemaphoreType.DMA((n_sem,)),
                        pltpu.SemaphoreType.DMA((n_big + 1,))],
    )(*bigs, conv_b)


def _pair_exchange(grads):
    n = len(grads)
    piece = PAIR_PIECE_ROWS

    def body(*refs):
        srcs, owns, gots = refs[:n], refs[n:2 * n], refs[2 * n:3 * n]
        stages = refs[3 * n:4 * n]
        send_sems, recv_sems, keep_sems, load_sems = refs[4 * n:]
        x, y, c, _, _, _ = _place()
        sibling = (x, y, 1 - c)
        keeps = []
        for a in range(n):
            keep = _local(srcs[a].at[:, _half_rows(srcs[a], c), :], owns[a], keep_sems.at[a])
            keep.start()
            keeps.append(keep)
        for a in range(n):
            slabs, half, _ = gots[a].shape
            per_slab = half // piece
            first = (1 - c) * half
            loads, sends = [], []
            for i in range(slabs * per_slab):
                k, r, slot = i // per_slab, i % per_slab, i % 2
                rows = pl.ds(pl.multiple_of(first + r * piece, 8), piece)
                loads.append(pltpu.make_async_copy(srcs[a].at[k, rows, :], stages[a].at[slot], load_sems.at[2 * a + slot]))
                sends.append(pltpu.make_async_remote_copy(
                    src_ref=stages[a].at[slot], dst_ref=gots[a].at[k, pl.ds(r * piece, piece), :],
                    send_sem=send_sems.at[2 * a + slot], recv_sem=recv_sems.at[a], device_id=sibling, device_id_type=MESH))
            loads[0].start()
            for i in range(len(loads)):
                loads[i].wait()
                sends[i].start()
                if i + 1 < len(loads):
                    if i >= 1:
                        sends[i - 1].wait_send()
                    loads[i + 1].start()
            for cp in sends[-2:]:
                cp.wait_send()
        for a in range(n):
            whole = srcs[a].at[:, _half_rows(srcs[a], c), :]
            pltpu.make_async_remote_copy(src_ref=whole, dst_ref=gots[a], send_sem=send_sems.at[2 * a],
                                         recv_sem=recv_sems.at[a], device_id=sibling, device_id_type=MESH).wait_recv()
            keeps[a].wait()

    halves = [jax.ShapeDtypeStruct((g.shape[0], g.shape[1] // 2, g.shape[2]), g.dtype) for g in grads]
    assert all(h.shape[1] % piece == 0 and (h.shape[0] * h.shape[1] // piece) >= 2 for h in halves)
    out = pl.pallas_call(
        body, name="grad_pair_exchange", out_shape=halves + halves,
        in_specs=[HBM] * n, out_specs=[HBM] * (2 * n),
        scratch_shapes=[pltpu.VMEM((2, piece, g.shape[2]), g.dtype) for g in grads]
        + [pltpu.SemaphoreType.DMA((2 * n,)), pltpu.SemaphoreType.DMA((n,)), pltpu.SemaphoreType.DMA((n,)),
           pltpu.SemaphoreType.DMA((2 * n,))],
        compiler_params=_params(),
    )(*grads)
    return out[:n], out[n:]


def _chip_exchange(halves):
    n = len(halves)

    def body(*refs):
        srcs, lands = refs[:n], refs[n:2 * n]
        send_sems, recv_sems, local_sems = refs[2 * n:]
        x, y, c, me, chips, chip_ids = _place()
        copies = []
        for a in range(n):
            keep = _local(srcs[a].at[me], lands[a].at[me], local_sems.at[a])
            keep.start()
            copies.append(keep)
            for j, (chip, cid) in enumerate(zip(chips, chip_ids)):
                give = _remote(srcs[a].at[cid], lands[a].at[me], send_sems, recv_sems, 3 * a + j, (*chip, c))
                give.start()
                copies.append(give)
        for a in range(n):
            for j, cid in enumerate(chip_ids):
                _remote(srcs[a].at[cid], lands[a].at[cid], send_sems, recv_sems, 3 * a + j, (x, y, c)).wait_recv()
        for a in range(n):
            copies[4 * a].wait()
            for j in range(3):
                copies[4 * a + 1 + j].wait_send()

    return pl.pallas_call(
        body, name="grad_chip_exchange", out_shape=[jax.ShapeDtypeStruct(h.shape, h.dtype) for h in halves],
        in_specs=[HBM] * n, out_specs=[HBM] * n,
        scratch_shapes=[pltpu.SemaphoreType.DMA((3 * n,)), pltpu.SemaphoreType.DMA((3 * n,)),
                        pltpu.SemaphoreType.DMA((n,))],
    )(*halves)


def _pair_gather(halves):
    n = len(halves)

    def body(*refs):
        srcs, fulls = refs[:n], refs[n:2 * n]
        send_sems, recv_sems, local_sems = refs[2 * n:]
        x, y, c, _, _, _ = _place()
        copies = []
        for a in range(n):
            mine = _half_rows(fulls[a], c)
            keep = _local(srcs[a], fulls[a].at[mine, :], local_sems.at[a])
            keep.start()
            give = _remote(srcs[a], fulls[a].at[mine, :], send_sems, recv_sems, a, (x, y, 1 - c))
            give.start()
            copies += [keep, give]
        for a in range(n):
            other = _half_rows(fulls[a], 1 - c)
            copies[2 * a].wait()
            copies[2 * a + 1].wait_send()
            _remote(srcs[a], fulls[a].at[other, :], send_sems, recv_sems, a, (x, y, 1 - c)).wait_recv()

    return pl.pallas_call(
        body, name="grad_pair_gather",
        out_shape=[jax.ShapeDtypeStruct((2 * h.shape[0], h.shape[1]), h.dtype) for h in halves],
        in_specs=[pl.BlockSpec(memory_space=pltpu.VMEM)] * n, out_specs=[HBM] * n,
        scratch_shapes=[pltpu.SemaphoreType.DMA((n,)), pltpu.SemaphoreType.DMA((n,)), pltpu.SemaphoreType.DMA((n,))],
    )(*halves)


def _all_reduce_small(p):
    n_dev = 8

    def body(p_ref, o_ref, land, send_sems, recv_sems):
        x, y, c = lax.axis_index("x"), lax.axis_index("y"), lax.axis_index("c")
        me = 4 * x + 2 * y + c
        land[me] = p_ref[...]
        sends = []
        for k in range(1, n_dev):
            fx, fy, fc = (k >> 2) & 1, (k >> 1) & 1, k & 1
            to = (x ^ fx, y ^ fy, c ^ fc)
            cp = _remote(p_ref, land.at[me], send_sems, recv_sems, k - 1, to)
            cp.start()
            sends.append(cp)
        for k in range(1, n_dev):
            _remote(p_ref, land.at[me ^ k], send_sems, recv_sems, k - 1, (x, y, c)).wait_recv()
        total = land[0]
        for d in range(1, n_dev):
            total = total + land[d]
        o_ref[...] = total
        for cp in sends:
            cp.wait_send()

    vm = pl.BlockSpec(memory_space=pltpu.VMEM)
    return pl.pallas_call(
        body, name="all_reduce_small", out_shape=jax.ShapeDtypeStruct(p.shape, p.dtype), in_specs=[vm], out_specs=vm,
        scratch_shapes=[pltpu.VMEM((n_dev,) + p.shape, p.dtype), pltpu.SemaphoreType.DMA((n_dev - 1,)),
                        pltpu.SemaphoreType.DMA((n_dev - 1,))],
    )(p)


def _row_tile(rows, cap=256):
    return cap if rows % cap == 0 else rows


def _add2(a, b, name):
    n, r, c = a.shape
    tr = _row_tile(r)

    def body(a_ref, b_ref, o_ref):
        o_ref[...] = a_ref[...] + b_ref[...]

    blk = pl.BlockSpec((None, tr, c), lambda i, j: (i, j, 0))
    return pl.pallas_call(body, name=name, grid=(n, r // tr), in_specs=[blk, blk], out_specs=blk,
                          out_shape=jax.ShapeDtypeStruct(a.shape, a.dtype),
                          compiler_params=_params(("parallel", "parallel")))(a, b)


def _sum_slots(land, name):
    n, r, c = land.shape
    tr = _row_tile(r)

    def body(l_ref, o_ref):
        total = l_ref[0]
        for j in range(1, n):
            total = total + l_ref[j]
        o_ref[...] = total

    return pl.pallas_call(body, name=name, grid=(r // tr,), in_specs=[pl.BlockSpec((n, tr, c), lambda i: (0, i, 0))],
                          out_specs=pl.BlockSpec((tr, c), lambda i: (i, 0)),
                          out_shape=jax.ShapeDtypeStruct((r, c), land.dtype), compiler_params=_params(("parallel",)))(land)


def _adamw(w, g, m, v, name):
    r, c = w.shape
    tr = _row_tile(r)

    def body(w_ref, g_ref, m_ref, v_ref, d_ref, mo_ref, vo_ref):
        g_ = g_ref[...]
        m2 = ADAM_B1 * m_ref[...] + (1.0 - ADAM_B1) * g_
        v2 = ADAM_B2 * v_ref[...] + (1.0 - ADAM_B2) * jnp.square(g_)
        m_hat = m2 / (1.0 - ADAM_B1 ** ADAM_STEP)
        v_hat = v2 / (1.0 - ADAM_B2 ** ADAM_STEP)
        d_ref[...] = -ADAM_LR * (m_hat / (jnp.sqrt(v_hat) + ADAM_EPS) + ADAM_WD * w_ref[...])
        mo_ref[...] = m2
        vo_ref[...] = v2

    blk = pl.BlockSpec((tr, c), lambda i: (i, 0))
    return pl.pallas_call(body, name=name, grid=(r // tr,), in_specs=[blk] * 4, out_specs=[blk] * 3,
                          out_shape=[jax.ShapeDtypeStruct(w.shape, F32)] * 3, compiler_params=_params(("parallel",)))(w, g, m, v)


SMALL_NAMES = ("norm_w", "mem_norm_w", "o_norm_a", "q_norm_c", "k_norm_c", "q_norm_b", "k_norm_b",
               "a_log_fwd", "a_log_bwd", "dt_bias_fwd", "dt_bias_bwd", "sink_b")
SMALL_SIZES = (2048, 2048, 128, 128, 128, 64, 64, 8, 8, 8, 8, 8)
SMALL_LOSS = sum(SMALL_SIZES)
SMALL_CONV = 5120
SMALL_TOTAL = SMALL_CONV + CONV_K * 3 * A_WIDTH
SMALL_ROWS = SMALL_TOTAL // LANE


def _pack_small(parts, extra=None, conv=None):
    vec = [parts[n].reshape(-1) for n in SMALL_NAMES]
    vec.append(jnp.zeros((1,), F32) if extra is None else extra.reshape(1))
    vec.append(jnp.zeros((SMALL_CONV - SMALL_LOSS - 1,), F32))
    vec.append(jnp.zeros((SMALL_TOTAL - SMALL_CONV,), F32) if conv is None else conv.reshape(-1))
    return jnp.concatenate(vec).reshape(SMALL_ROWS, LANE)


def _unpack_small(packed):
    flat = packed.reshape(-1)
    out, off = {}, 0
    for n, size in zip(SMALL_NAMES, SMALL_SIZES):
        out[n] = flat[off:off + size].reshape(1, size)
        off += size
    return out


WEIGHT_ORDER = ("norm_w", "w_in", "conv_w_a", "a_log_fwd", "a_log_bwd", "dt_bias_fwd", "dt_bias_bwd", "o_norm_a",
                "q_norm_b", "k_norm_b", "sink_b", "mem_norm_w", "w_mem_kv", "q_norm_c", "k_norm_c", "w_out")


def kernel(x, mem, norm_w, w_in, conv_w_a, a_log_fwd, a_log_bwd, dt_bias_fwd, dt_bias_bwd, o_norm_a, q_norm_b, k_norm_b, sink_b, mem_norm_w, w_mem_kv, q_norm_c, k_norm_c, w_out, loss_target, m_norm_w, m_w_in, m_conv_w_a, m_a_log_fwd, m_a_log_bwd, m_dt_bias_fwd, m_dt_bias_bwd, m_o_norm_a, m_q_norm_b, m_k_norm_b, m_sink_b, m_mem_norm_w, m_w_mem_kv, m_q_norm_c, m_k_norm_c, m_w_out, v_norm_w, v_w_in, v_conv_w_a, v_a_log_fwd, v_a_log_bwd, v_dt_bias_fwd, v_dt_bias_bwd, v_o_norm_a, v_q_norm_b, v_k_norm_b, v_sink_b, v_mem_norm_w, v_w_mem_kv, v_q_norm_c, v_k_norm_c, v_w_out):
    weights = dict(norm_w=norm_w, w_in=w_in, conv_w_a=conv_w_a, a_log_fwd=a_log_fwd, a_log_bwd=a_log_bwd,
                   dt_bias_fwd=dt_bias_fwd, dt_bias_bwd=dt_bias_bwd, o_norm_a=o_norm_a, q_norm_b=q_norm_b,
                   k_norm_b=k_norm_b, sink_b=sink_b, mem_norm_w=mem_norm_w, w_mem_kv=w_mem_kv, q_norm_c=q_norm_c,
                   k_norm_c=k_norm_c, w_out=w_out)
    mom1 = dict(norm_w=m_norm_w, w_in=m_w_in, conv_w_a=m_conv_w_a, a_log_fwd=m_a_log_fwd, a_log_bwd=m_a_log_bwd,
                dt_bias_fwd=m_dt_bias_fwd, dt_bias_bwd=m_dt_bias_bwd, o_norm_a=m_o_norm_a, q_norm_b=m_q_norm_b,
                k_norm_b=m_k_norm_b, sink_b=m_sink_b, mem_norm_w=m_mem_norm_w, w_mem_kv=m_w_mem_kv,
                q_norm_c=m_q_norm_c, k_norm_c=m_k_norm_c, w_out=m_w_out)
    mom2 = dict(norm_w=v_norm_w, w_in=v_w_in, conv_w_a=v_conv_w_a, a_log_fwd=v_a_log_fwd, a_log_bwd=v_a_log_bwd,
                dt_bias_fwd=v_dt_bias_fwd, dt_bias_bwd=v_dt_bias_bwd, o_norm_a=v_o_norm_a, q_norm_b=v_q_norm_b,
                k_norm_b=v_k_norm_b, sink_b=v_sink_b, mem_norm_w=v_mem_norm_w, w_mem_kv=v_w_mem_kv,
                q_norm_c=v_q_norm_c, k_norm_c=v_k_norm_c, w_out=v_w_out)
    chip = 2 * lax.axis_index("x") + lax.axis_index("y")

    w_in4, w_out4, w_kv4, conv4 = _all_gather_weights(w_in[0].astype(BF16), w_out[0].astype(BF16),
                                                      w_mem_kv[0].astype(BF16), conv_w_a[0])
    w_perm = _permute_cols(jnp.transpose(w_in4, (1, 0, 2)).reshape(D_MODEL, IN_WIDTH))
    w_out_full = w_out4.reshape(D_MODEL, D_MODEL)
    w_kv_full = w_kv4.reshape(D_MODEL, 2 * C_HEADS * C_DIM)
    conv_full = jnp.transpose(conv4, (1, 0, 2)).reshape(CONV_K, 3 * A_WIDTH)
    pa = jnp.concatenate([_pad_row(a_log_fwd), _pad_row(a_log_bwd), _pad_row(dt_bias_fwd), _pad_row(dt_bias_bwd),
                          _pad_row(o_norm_a), jnp.zeros((3, LANE), F32)], axis=0)
    pb = jnp.concatenate([_pad_row(q_norm_b), _pad_row(k_norm_b), _pad_row(sink_b), jnp.zeros((5, LANE), F32)], axis=0)
    pc = jnp.concatenate([_pad_row(q_norm_c), _pad_row(k_norm_c), jnp.zeros((6, LANE), F32)], axis=0)

    r = _local_step(x[0], mem[0], loss_target[0], norm_w, w_perm, conv_full, pa, pb, pc, mem_norm_w, w_kv_full,
                    w_out_full)

    g_in4 = jnp.transpose(_unpermute_cols(r["g_w_perm"]).reshape(D_MODEL, N_CHIPS, W_IN_BLOCK), (1, 0, 2))
    g_out4 = r["g_w_out"].reshape(N_CHIPS, D_MODEL // N_CHIPS, D_MODEL)
    g_kv4 = r["g_w_kv"].reshape(N_CHIPS, D_MODEL // N_CHIPS, 2 * C_HEADS * C_DIM)
    own, got = _pair_exchange([g_in4, g_out4, g_kv4])
    pair = [_add2(a, b, "grad_pair_sum_%d" % i) for i, (a, b) in enumerate(zip(own, got))]
    lands = _chip_exchange(pair)
    reduced = [_sum_slots(l, "grad_chip_sum_%d" % i) for i, l in enumerate(lands)]
    g_w_in, g_w_out, g_w_kv = _pair_gather(reduced)

    d_pa, d_pb, d_pc = r["d_pa"], r["d_pb"], r["d_pc"]
    small_g = dict(norm_w=r["g_norm"], mem_norm_w=r["g_mem_norm"], o_norm_a=d_pa[4], q_norm_c=d_pc[0], k_norm_c=d_pc[1],
                   q_norm_b=d_pb[0, :B_DIM], k_norm_b=d_pb[1, :B_DIM], a_log_fwd=d_pa[0, :A_HEADS],
                   a_log_bwd=d_pa[1, :A_HEADS], dt_bias_fwd=d_pa[2, :A_HEADS], dt_bias_bwd=d_pa[3, :A_HEADS],
                   sink_b=d_pb[2, :B_HEADS])
    packed = _all_reduce_small(_pack_small(small_g, jnp.sum(r["loss_parts"][:, 0, 0]), r["g_conv"]))
    flat = packed.reshape(-1)
    loss = flat[SMALL_LOSS]
    conv_sum = flat[SMALL_CONV:].reshape(CONV_K, 3 * A_WIDTH)
    conv_cols = 3 * A_WIDTH // N_CHIPS
    g_conv = lax.dynamic_slice(conv_sum, (0, chip * conv_cols), (CONV_K, conv_cols))

    grads = _unpack_small(packed)
    grads.update(w_in=g_w_in, w_mem_kv=g_w_kv, w_out=g_w_out, conv_w_a=g_conv)
    delta, new_m, new_v = {}, {}, {}
    for n in ("w_in", "w_mem_kv", "w_out", "conv_w_a"):
        delta[n], new_m[n], new_v[n] = _adamw(weights[n][0], grads[n], mom1[n][0], mom2[n][0], "adamw_" + n)
    d_s, m_s, v_s = _adamw(_pack_small(weights), packed, _pack_small(mom1), _pack_small(mom2), "adamw_small")
    d_s, m_s, v_s = _unpack_small(d_s), _unpack_small(m_s), _unpack_small(v_s)
    for n in SMALL_NAMES:
        delta[n], new_m[n], new_v[n] = d_s[n], m_s[n], v_s[n]

    def shaped(tree):
        return [tree[n].reshape(weights[n].shape) for n in WEIGHT_ORDER]

    return (loss, r["g_x"].reshape(x.shape), *shaped(grads), *shaped(delta), *shaped(new_m), *shaped(new_v))
```

```python
import functools

import jax
import jax.numpy as jnp
from jax import lax
from jax.experimental import pallas as pl
from jax.experimental.pallas import tpu as pltpu

F32 = jnp.float32
BF16 = jnp.bfloat16
HI = lax.Precision.HIGHEST
MESH = pl.DeviceIdType.MESH

D_MODEL = 2048
A_WIDTH = 1024
A_HEADS = 8
A_DIM = 128
CONV_K = 5
CHUNK = 64
B_HEADS = 8
B_KV = 2
B_DIM = 64
WINDOW = 128
C_HEADS = 4
C_DIM = 128
MEM_LEN = 256
ROPE_THETA = 10000.0
EPS = 1e-6
IN_WIDTH = 6432
N_CHIPS = 4
W_IN_BLOCK = IN_WIDTH // N_CHIPS

LANE = 128
P_QA, P_KA, P_VA, P_ZA = 0, 1024, 2048, 3072
P_QB, P_ZB, P_QC, P_ZC = 4096, 4608, 5120, 5632
P_KB, P_VB, P_GT = 6144, 6272, 6400
P_WIDTH = 6656
O_GT, O_QB, O_KB, O_VB, O_ZB, O_QC, O_ZC = 4096, 4128, 4640, 4768, 4896, 5408, 5920

ADAM_LR, ADAM_B1, ADAM_B2, ADAM_EPS, ADAM_WD, ADAM_STEP = 0.001, 0.9, 0.999, 1e-08, 0.01, 10

VMEM_LIMIT = 56 * 1024 * 1024


def _params(sem=None):
    return pltpu.CompilerParams(dimension_semantics=sem, vmem_limit_bytes=VMEM_LIMIT)


def _dot(a, b, dims=(((1,), (0,)), ((), ())), precision=HI):
    return lax.dot_general(a, b, dims, precision=precision, preferred_element_type=F32)


def _dot_nt(a, b, precision=HI):
    return _dot(a, b, (((1,), (1,)), ((), ())), precision)


def _dot_tn(a, b, precision=HI):
    return _dot(a, b, (((0,), (0,)), ((), ())), precision)


_NN = (((1,), (0,)), ((), ()))
_NT = (((1,), (1,)), ((), ()))
_TN = (((0,), (0,)), ((), ()))


def _bdot(a, b, dims):
    return lax.dot_general(a.astype(BF16), b.astype(BF16), dims, preferred_element_type=F32)


@jax.custom_vjp
def _mm(a, b):
    return _bdot(a, b, _NN)


_mm.defvjp(lambda a, b: (_bdot(a, b, _NN), (a, b)),
           lambda res, ct: (_bdot(ct, res[1], _NT), _bdot(res[0], ct, _TN)))


@jax.custom_vjp
def _mm_nt(a, b):
    return _bdot(a, b, _NT)


_mm_nt.defvjp(lambda a, b: (_bdot(a, b, _NT), (a, b)),
              lambda res, ct: (_bdot(ct, res[1], _NN), _bdot(ct, res[0], _TN)))


@jax.custom_vjp
def _mm_tn(a, b):
    return _bdot(a, b, _TN)


_mm_tn.defvjp(lambda a, b: (_bdot(a, b, _TN), (a, b)),
              lambda res, ct: (_bdot(res[1], ct, _NT), _bdot(res[0], ct, _NN)))


def _rms(t, w):
    return t * lax.rsqrt(jnp.mean(t * t, axis=-1, keepdims=True) + EPS) * w


def _l2(t):
    return t * lax.rsqrt(jnp.sum(t * t, axis=-1, keepdims=True) + EPS)


def _silu(t):
    return t * jax.nn.sigmoid(t)


def _softplus(t):
    return jnp.maximum(t, 0.0) + jnp.log1p(jnp.exp(-jnp.abs(t)))


def _matmul(a, b, mode, out_dtype, name, tm=512, tn=512, tk=512):
    (m, k) = a.shape[::-1] if mode == "tn" else a.shape
    n = b.shape[0] if mode == "nt" else b.shape[1]
    tm, tn, tk = min(tm, m), min(tn, n), min(tk, k)
    assert m % tm == 0 and n % tn == 0 and k % tk == 0, (m, n, k, tm, tn, tk)
    if mode == "nn":
        a_spec = pl.BlockSpec((tm, tk), lambda i, j, kk: (i, kk))
        b_spec = pl.BlockSpec((tk, tn), lambda i, j, kk: (kk, j))
        dims = (((1,), (0,)), ((), ()))
    elif mode == "nt":
        a_spec = pl.BlockSpec((tm, tk), lambda i, j, kk: (i, kk))
        b_spec = pl.BlockSpec((tn, tk), lambda i, j, kk: (j, kk))
        dims = (((1,), (1,)), ((), ()))
    else:
        a_spec = pl.BlockSpec((tk, tm), lambda i, j, kk: (kk, i))
        b_spec = pl.BlockSpec((tk, tn), lambda i, j, kk: (kk, j))
        dims = (((0,), (0,)), ((), ()))
    nk = k // tk

    def body_one(a_ref, b_ref, o_ref):
        o_ref[...] = _bdot(a_ref[...], b_ref[...], dims).astype(out_dtype)

    def body_acc(a_ref, b_ref, o_ref, acc_ref):
        kk = pl.program_id(2)

        @pl.when(kk == 0)
        def _():
            acc_ref[...] = jnp.zeros_like(acc_ref)

        acc_ref[...] += _bdot(a_ref[...], b_ref[...], dims)

        @pl.when(kk == nk - 1)
        def _():
            o_ref[...] = acc_ref[...].astype(out_dtype)

    return pl.pallas_call(
        body_one if nk == 1 else body_acc, name=name, grid=(m // tm, n // tn, nk),
        in_specs=[a_spec, b_spec], out_specs=pl.BlockSpec((tm, tn), lambda i, j, kk: (i, j)),
        out_shape=jax.ShapeDtypeStruct((m, n), out_dtype),
        scratch_shapes=[] if nk == 1 else [pltpu.VMEM((tm, tn), F32)],
        compiler_params=_params(("parallel", "parallel", "arbitrary")),
    )(a, b)


def _rms_fwd(x, w, tr=256):
    s, d = x.shape

    def body(x_ref, w_ref, o_ref):
        o_ref[...] = _rms(x_ref[...], w_ref[...]).astype(BF16)

    return pl.pallas_call(
        body, name="rms_fwd", grid=(s // tr,),
        in_specs=[pl.BlockSpec((tr, d), lambda i: (i, 0)), pl.BlockSpec((1, d), lambda i: (0, 0))],
        out_specs=pl.BlockSpec((tr, d), lambda i: (i, 0)),
        out_shape=jax.ShapeDtypeStruct((s, d), BF16), compiler_params=_params(("parallel",)),
    )(x, w)


def _rms_bwd(x, w, d_hn, dy, tr=256):
    s, d = x.shape

    def body(x_ref, w_ref, g_ref, dy_ref, gx_ref, gw_ref):
        _, vjp = jax.vjp(_rms, x_ref[...], w_ref[...])
        dx, dw = vjp(g_ref[...])
        gx_ref[...] = dy_ref[...] + dx

        @pl.when(pl.program_id(0) == 0)
        def _():
            gw_ref[...] = jnp.zeros_like(gw_ref)

        gw_ref[...] += dw

    row = pl.BlockSpec((tr, d), lambda i: (i, 0))
    vec = pl.BlockSpec((1, d), lambda i: (0, 0))
    return pl.pallas_call(
        body, name="rms_bwd", grid=(s // tr,), in_specs=[row, vec, row, row], out_specs=[row, vec],
        out_shape=[jax.ShapeDtypeStruct((s, d), F32), jax.ShapeDtypeStruct((1, d), F32)],
        compiler_params=_params(("arbitrary",)),
    )(x, w, d_hn, dy)


def _loss_dy(x, mo, target, tr=256):
    s, d = x.shape
    nt = s // tr

    def body(x_ref, mo_ref, t_ref, dy_ref, dyb_ref, l_ref):
        err = x_ref[...] + mo_ref[...] - t_ref[...]
        dy = err * (1.0 / d)
        dy_ref[...] = dy
        dyb_ref[...] = dy.astype(BF16)
        l_ref[...] = jnp.full(l_ref.shape, 0.5 * jnp.sum(jnp.sum(err * err, axis=1, keepdims=True) * (1.0 / d)), F32)

    row = pl.BlockSpec((tr, d), lambda i: (i, 0))
    return pl.pallas_call(
        body, name="loss_dy", grid=(nt,), in_specs=[row, row, row],
        out_specs=[row, row, pl.BlockSpec((1, 8, LANE), lambda i: (i, 0, 0))],
        out_shape=[jax.ShapeDtypeStruct((s, d), F32), jax.ShapeDtypeStruct((s, d), BF16),
                   jax.ShapeDtypeStruct((nt, 8, LANE), F32)],
        compiler_params=_params(("parallel",)),
    )(x, mo, target)


def _shift_rows(t, s):
    if s == 0:
        return t
    n = t.shape[0]
    rolled = pltpu.roll(t, (-s) % n, axis=0)
    idx = lax.broadcasted_iota(jnp.int32, t.shape, 0) + s
    return jnp.where((idx >= 0) & (idx < n), rolled, 0.0)


def _conv_fwd(proj, conv_w):
    s = proj.shape[0]
    nblk = 3 * A_WIDTH // LANE

    def body(x_ref, w_ref, o_ref):
        x = x_ref[...]
        acc = jnp.zeros_like(x)
        for j in range(CONV_K):
            acc = acc + w_ref[j:j + 1, :] * _shift_rows(x, j - CONV_K // 2)
        o_ref[...] = acc

    return pl.pallas_call(
        body, name="conv_fwd", grid=(nblk,),
        in_specs=[pl.BlockSpec((s, LANE), lambda i: (0, i)), pl.BlockSpec((CONV_K, LANE), lambda i: (0, i))],
        out_specs=pl.BlockSpec((None, s, LANE), lambda i: (i // A_HEADS, 0, i % A_HEADS)),
        out_shape=jax.ShapeDtypeStruct((3, s, A_WIDTH), F32), compiler_params=_params(("parallel",)),
    )(proj, conv_w)


def _conv_bwd(proj, conv_w, d_c):
    s = proj.shape[0]
    nblk = 3 * A_WIDTH // LANE

    def body(x_ref, w_ref, g_ref, dx_ref, dw_ref):
        x, g = x_ref[...], g_ref[...]
        acc = jnp.zeros_like(x)
        for j in range(CONV_K):
            off = j - CONV_K // 2
            acc = acc + w_ref[j:j + 1, :] * _shift_rows(g, -off)
            dw_ref[j:j + 1, :] = jnp.sum(_shift_rows(x, off) * g, axis=0, keepdims=True)
        dx_ref[...] = acc

    col = pl.BlockSpec((s, LANE), lambda i: (0, i))
    wsp = pl.BlockSpec((CONV_K, LANE), lambda i: (0, i))
    dsp = pl.BlockSpec((None, s, LANE), lambda i: (i // A_HEADS, 0, i % A_HEADS))
    return pl.pallas_call(
        body, name="conv_bwd", grid=(nblk,), in_specs=[col, wsp, dsp], out_specs=[col, wsp],
        out_shape=[jax.ShapeDtypeStruct((s, 3 * A_WIDTH), F32), jax.ShapeDtypeStruct((CONV_K, 3 * A_WIDTH), F32)],
        compiler_params=_params(("parallel",)),
    )(proj, conv_w, d_c)


A_STEP_HEADS = 2
A_CHAINS = 2 * A_STEP_HEADS


def _a_chain(st, cq, ck, cv, alpha, beta_raw, a_log, dt_b, incl, strict, last):
    c = CHUNK
    eye = (lax.broadcasted_iota(jnp.int32, (c, c), 0) == lax.broadcasted_iota(jnp.int32, (c, c), 1)).astype(F32)
    gb = -jnp.exp(a_log) * _softplus(alpha + dt_b)
    bb = jax.nn.sigmoid(beta_raw)
    q = _l2(_silu(cq)) * (A_DIM ** -0.5)
    k = _l2(_silu(ck))
    v = _silu(cv)

    gc = _dot(incl, jnp.broadcast_to(gb, (c, LANE)))
    tot = jnp.sum(gc * last, axis=0, keepdims=True)
    m1 = gc[:, :c]
    decay = incl * jnp.exp(incl * (m1 - m1.T))
    kb = k * bb
    vb = v * bb
    a = -(strict * decay * _mm_nt(kb, k))
    tinv = eye + a
    p = a
    for _ in range(5):
        p = _mm(p, p)
        tinv = tinv + _mm(tinv, p)
    eg = jnp.exp(gc)
    u = _mm(tinv, vb)
    w = _mm(tinv, kb * eg)
    qk = _mm_nt(q, k) * decay
    v_new = u - _mm(w, st)
    o = _mm(q * eg, st) + _mm(qk, v_new)
    st_new = st * jnp.exp(tot) + _mm_tn(k * jnp.exp(tot - gc), v_new)
    return st_new, o


def _a_step(sts, cq, ck, cv, gts, pa, h0):
    c = CHUNK
    lane = lax.broadcasted_iota(jnp.int32, (1, LANE), 1)
    ii = lax.broadcasted_iota(jnp.int32, (c, c), 0)
    jj = lax.broadcasted_iota(jnp.int32, (c, c), 1)
    row = lax.broadcasted_iota(jnp.int32, (c, 1), 0)

    def pick(t, col):
        return jnp.sum(jnp.where(lane == col, t, 0.0), axis=1, keepdims=True)

    alpha, beta_raw, a_log, dt_b, incl, strict, last = [], [], [], [], [], [], []
    for b in range(A_CHAINS):
        h, rev = h0 + b // 2, b % 2
        alpha.append(pick(gts[b], h + 8 * rev))
        beta_raw.append(pick(gts[b], h + 16 + 8 * rev))
        a_log.append(pick(pa[rev:rev + 1, :], h))
        dt_b.append(pick(pa[2 + rev:3 + rev, :], h))
        incl.append(((ii <= jj) if rev else (ii >= jj)).astype(F32))
        strict.append(((ii < jj) if rev else (ii > jj)).astype(F32))
        last.append((row == (0 if rev else c - 1)).astype(F32))
    stack = lambda ts: jnp.concatenate([t[None] for t in ts], axis=0)
    return jax.vmap(_a_chain)(sts, cq, ck, cv, stack(alpha), stack(beta_raw), stack(a_log), stack(dt_b),
                              stack(incl), stack(strict), stack(last))


def _a_final(o, za, pa):
    outs = []
    for j in range(o.shape[1] // A_DIM):
        ln = slice(j * A_DIM, (j + 1) * A_DIM)
        outs.append(_rms(o[:, ln], pa[4:5, :]) * _silu(za[:, ln]))
    return jnp.concatenate(outs, axis=1)


def _a_tiles(n, nchunk):
    tiles = []
    for b in range(A_CHAINS):
        i = (nchunk - 1 - n) if b % 2 else n
        tiles.append((i, pl.ds(pl.multiple_of(i * CHUNK, CHUNK), CHUNK), slice((b // 2) * A_DIM, (b // 2 + 1) * A_DIM)))
    return tiles


def _a_load(tiles, c_ref, gt_ref):
    cq, ck, cv = (jnp.stack([c_ref[r, sl, ln] for _, sl, ln in tiles], axis=0) for r in range(3))
    return cq, ck, cv, jnp.stack([gt_ref[sl, :] for _, sl, _ in tiles], axis=0)


def _a_scan(h0, nchunk, c_ref, gt_ref, pa, of_ref, ob_ref, s_ref):
    def step(n, sts):
        tiles = _a_tiles(n, nchunk)
        sts_new, o = _a_step(sts, *_a_load(tiles, c_ref, gt_ref), pa, h0)
        for b, (i, sl, ln) in enumerate(tiles):
            if s_ref is not None:
                s_ref[b, i] = sts[b]
            (ob_ref if b % 2 else of_ref)[sl, ln] = o[b]
        return sts_new

    lax.fori_loop(0, nchunk, step, jnp.zeros((A_CHAINS, A_DIM, A_DIM), F32))


def _a_specs(s):
    wide = A_STEP_HEADS * A_DIM
    once = pl.Buffered(1)
    trio = pl.BlockSpec((3, s, wide), lambda g: (0, 0, g), pipeline_mode=once)
    gates = pl.BlockSpec((s, LANE), lambda g: (0, P_GT // LANE))
    small = pl.BlockSpec((8, LANE), lambda g: (0, 0))

    def cols(base):
        return pl.BlockSpec((s, wide), lambda g: (0, base // wide + g), pipeline_mode=once)

    return wide, trio, gates, small, cols


def _delta_fwd(cqkv, proj, pa):
    s = cqkv.shape[1]
    nchunk = s // CHUNK
    wide, trio, gates, small, cols = _a_specs(s)

    def body(c_ref, gt_ref, za_ref, pa_ref, out_ref, of_ref, ob_ref):
        h0 = pl.program_id(0) * A_STEP_HEADS
        pa_v = pa_ref[...]
        _a_scan(h0, nchunk, c_ref, gt_ref, pa_v, of_ref, ob_ref, None)
        out_ref[...] = _a_final(of_ref[...] + ob_ref[...], za_ref[...], pa_v).astype(BF16)

    return pl.pallas_call(
        body, name="delta_fwd", grid=(A_HEADS // A_STEP_HEADS,),
        in_specs=[trio, gates, cols(P_ZA), small], out_specs=cols(0),
        out_shape=jax.ShapeDtypeStruct((s, A_WIDTH), BF16),
        scratch_shapes=[pltpu.VMEM((s, wide), F32)] * 2, compiler_params=_params(("parallel",)),
    )(cqkv, proj, proj, pa)


def _delta_bwd(cqkv, proj, pa, d_mixed):
    s = cqkv.shape[1]
    nchunk = s // CHUNK

    wide, trio, gates, small, cols = _a_specs(s)

    def body(c_ref, gt_ref, za_ref, pa_ref, dm_ref, dc_ref, dza_ref, dgt_ref, dpa_ref, s_ref, of_ref, ob_ref):
        h0 = pl.program_id(0) * A_STEP_HEADS
        pa_v = pa_ref[...]

        @pl.when(h0 == 0)
        def _():
            dgt_ref[...] = jnp.zeros_like(dgt_ref)
            dpa_ref[...] = jnp.zeros_like(dpa_ref)

        _a_scan(h0, nchunk, c_ref, gt_ref, pa_v, of_ref, ob_ref, s_ref)
        _, vjp = jax.vjp(_a_final, of_ref[...] + ob_ref[...], za_ref[...], pa_v)
        d_o, d_za, dpa0 = vjp(dm_ref[...])
        of_ref[...] = d_o
        dza_ref[...] = d_za
        dc_ref[...] = jnp.zeros_like(dc_ref)

        def step(n, carry):
            d_sts, dpa = carry
            tiles = _a_tiles(nchunk - 1 - n, nchunk)
            sts = jnp.stack([s_ref[b, i] for b, (i, _, _) in enumerate(tiles)], axis=0)
            d_o_t = jnp.stack([of_ref[sl, ln] for _, sl, ln in tiles], axis=0)
            _, vjp_c = jax.vjp(lambda *a: _a_step(*a, h0), sts, *_a_load(tiles, c_ref, gt_ref), pa_v)
            d_prev, dcq, dck, dcv, dgts, dpa_i = vjp_c((d_sts, d_o_t))
            for b, (_, sl, ln) in enumerate(tiles):
                for r, dc in enumerate((dcq, dck, dcv)):
                    dc_ref[r, sl, ln] += dc[b]
                dgt_ref[sl, :] += dgts[b]
            return d_prev, dpa + dpa_i

        _, dpa_out = lax.fori_loop(0, nchunk, step, (jnp.zeros((A_CHAINS, A_DIM, A_DIM), F32), dpa0))
        dpa_ref[...] += dpa_out

    fixed = pl.BlockSpec((s, LANE), lambda g: (0, 0))
    return pl.pallas_call(
        body, name="delta_bwd", grid=(A_HEADS // A_STEP_HEADS,),
        in_specs=[trio, gates, cols(P_ZA), small, cols(0)], out_specs=[trio, cols(0), fixed, small],
        out_shape=[jax.ShapeDtypeStruct((3, s, A_WIDTH), F32), jax.ShapeDtypeStruct((s, A_WIDTH), F32),
                   jax.ShapeDtypeStruct((s, LANE), F32), jax.ShapeDtypeStruct((8, LANE), F32)],
        scratch_shapes=[pltpu.VMEM((A_CHAINS, nchunk, A_DIM, A_DIM), F32), pltpu.VMEM((s, wide), F32),
                        pltpu.VMEM((s, wide), F32)],
        compiler_params=_params(("arbitrary",)),
    )(cqkv, proj, proj, pa, d_mixed)


def _rope_tables(s):
    inv = ROPE_THETA ** (-jnp.arange(0, B_DIM, 2, dtype=F32) / B_DIM)
    ang = jnp.arange(s, dtype=F32)[:, None] * inv[None, :]
    cos, sin = jnp.cos(ang), jnp.sin(ang)
    return jnp.concatenate([cos, cos], axis=1), jnp.concatenate([-sin, sin], axis=1)


def _b_block(q_t, z_t, k3, v3, cos_q, sin_q, cos_k, sin_k, pb, n, nb):
    w = WINDOW
    r = lax.broadcasted_iota(jnp.int32, (B_DIM, B_DIM), 0)
    c = lax.broadcasted_iota(jnp.int32, (B_DIM, B_DIM), 1)
    swap = (r == (c + B_DIM // 2) % B_DIM).astype(F32)
    qi = lax.broadcasted_iota(jnp.int32, (w, 3 * w), 0)
    kj = lax.broadcasted_iota(jnp.int32, (w, 3 * w), 1)
    kpos = kj + (n - 1) * w
    mask = (jnp.abs(kj - w - qi) <= w) & (kpos >= 0) & (kpos < nb * w)
    lane = lax.broadcasted_iota(jnp.int32, (1, LANE), 1)
    qn, kn = pb[0:1, :B_DIM], pb[1:2, :B_DIM]
    outs = []
    for hk in range(B_KV):
        k = _rms(k3[:, hk * B_DIM:(hk + 1) * B_DIM], kn)
        k = k * cos_k + _dot(k, swap) * sin_k
        v = v3[:, hk * B_DIM:(hk + 1) * B_DIM]
        for g in range(B_HEADS // B_KV):
            hq = hk * (B_HEADS // B_KV) + g
            q = _rms(q_t[:, hq * B_DIM:(hq + 1) * B_DIM], qn)
            q = q * cos_q + _dot(q, swap) * sin_q
            sink = jnp.sum(jnp.where(lane == hq, pb[2:3, :], 0.0), axis=1, keepdims=True)
            s = _mm_nt(q, k) * (B_DIM ** -0.5)
            s = jnp.where(mask, s, -jnp.inf)
            m = jnp.maximum(jnp.max(s, axis=1, keepdims=True), sink)
            p = jnp.exp(s - m)
            p = p / (jnp.sum(p, axis=1, keepdims=True) + jnp.exp(sink - m))
            outs.append(_mm(p, v))
    return jnp.concatenate(outs, axis=1) * _silu(z_t)


def _b_specs(s):
    nb = s // WINDOW
    qsp = pl.BlockSpec((WINDOW, 512), lambda n: (n, P_QB // 512))
    zsp = pl.BlockSpec((WINDOW, 512), lambda n: (n, P_ZB // 512))

    def three(col, width):
        return [pl.BlockSpec((WINDOW, width), lambda n: (jnp.maximum(n - 1, 0), col)),
                pl.BlockSpec((WINDOW, width), lambda n: (n, col)),
                pl.BlockSpec((WINDOW, width), lambda n: (jnp.minimum(n + 1, nb - 1), col))]

    tab = pl.BlockSpec((WINDOW, B_DIM), lambda n: (n, 0))
    small = pl.BlockSpec((8, LANE), lambda n: (0, 0))
    specs = [qsp, zsp] + three(P_KB // LANE, LANE) + three(P_VB // LANE, LANE) + [tab, tab] + three(0, B_DIM) + three(0, B_DIM) + [small]
    return nb, specs


def _b_args(proj, cos2, sin2, pb):
    return (proj, proj, proj, proj, proj, proj, proj, proj, cos2, sin2, cos2, cos2, cos2, sin2, sin2, sin2, pb)


def _b_load(refs):
    (q_ref, z_ref, kp, kc, kx, vp, vc, vx, cq, sq, ckp, ckc, ckx, skp, skc, skx, pb_ref) = refs
    cat = lambda *r: jnp.concatenate([t[...] for t in r], axis=0)
    return (q_ref[...], z_ref[...], cat(kp, kc, kx), cat(vp, vc, vx), cq[...], sq[...], cat(ckp, ckc, ckx),
            cat(skp, skc, skx), pb_ref[...])


def _attn_b_fwd(proj, cos2, sin2, pb):
    s = proj.shape[0]
    nb, specs = _b_specs(s)

    def body(*refs):
        o_ref = refs[-1]
        args = _b_load(refs[:-1])
        o_ref[...] = _b_block(*args, pl.program_id(0), nb).astype(BF16)

    return pl.pallas_call(
        body, name="attn_b_fwd", grid=(nb,), in_specs=specs,
        out_specs=pl.BlockSpec((WINDOW, 512), lambda n: (n, 0)),
        out_shape=jax.ShapeDtypeStruct((s, 512), BF16), compiler_params=_params(("parallel",)),
    )(*_b_args(proj, cos2, sin2, pb))


def _attn_b_bwd(proj, cos2, sin2, pb, d_mixed):
    s = proj.shape[0]
    nb, specs = _b_specs(s)
    w = WINDOW

    def body(*refs):
        dm_ref, dq_ref, dz_ref, dk_ref, dv_ref, dpb_ref = refs[-6:]
        n = pl.program_id(0)
        q_t, z_t, k3, v3, cq, sq, ck, sk, pb_v = _b_load(refs[:-6])

        @pl.when(n == 0)
        def _():
            dk_ref[...] = jnp.zeros_like(dk_ref)
            dv_ref[...] = jnp.zeros_like(dv_ref)
            dpb_ref[...] = jnp.zeros_like(dpb_ref)

        def f(q_, z_, k_, v_, pb_):
            return _b_block(q_, z_, k_, v_, cq, sq, ck, sk, pb_, n, nb)

        _, vjp = jax.vjp(f, q_t, z_t, k3, v3, pb_v)
        dq, dz, dk3, dv3, dpb = vjp(dm_ref[...])
        dq_ref[...] = dq
        dz_ref[...] = dz
        dpb_ref[...] += dpb

        def add(j, cond):
            @pl.when(cond)
            def _():
                rows = pl.ds(pl.multiple_of((n - 1 + j) * w, w), w)
                dk_ref[rows, :] += dk3[j * w:(j + 1) * w, :]
                dv_ref[rows, :] += dv3[j * w:(j + 1) * w, :]

        add(0, n > 0)
        add(1, n >= 0)
        add(2, n < nb - 1)

    blk = pl.BlockSpec((w, 512), lambda n: (n, 0))
    whole = pl.BlockSpec((s, LANE), lambda n: (0, 0))
    small = pl.BlockSpec((8, LANE), lambda n: (0, 0))
    return pl.pallas_call(
        body, name="attn_b_bwd", grid=(nb,),
        in_specs=specs + [pl.BlockSpec((w, 512), lambda n: (n, 2))],
        out_specs=[blk, blk, whole, whole, small],
        out_shape=[jax.ShapeDtypeStruct((s, 512), F32), jax.ShapeDtypeStruct((s, 512), F32),
                   jax.ShapeDtypeStruct((s, LANE), F32), jax.ShapeDtypeStruct((s, LANE), F32),
                   jax.ShapeDtypeStruct((8, LANE), F32)],
        compiler_params=_params(("arbitrary",)),
    )(*_b_args(proj, cos2, sin2, pb), d_mixed)


def _mem_kv_fwd(mem, mem_norm_w, w_kv):
    def body(mem_ref, nw_ref, w_ref, kv_ref):
        mn = _rms(mem_ref[...], nw_ref[...]).astype(BF16)
        kv_ref[...] = jnp.dot(mn, w_ref[...], preferred_element_type=F32)

    return pl.pallas_call(
        body, name="mem_kv_fwd", out_shape=jax.ShapeDtypeStruct((MEM_LEN, 2 * C_HEADS * C_DIM), F32),
        compiler_params=_params(),
    )(mem, mem_norm_w, w_kv)


def _mem_kv_bwd(mem, mem_norm_w, w_kv, d_kv):
    def body(mem_ref, nw_ref, w_ref, g_ref, gw_ref, gn_ref):
        mn, vjp = jax.vjp(_rms, mem_ref[...], nw_ref[...])
        g = g_ref[...].astype(BF16)
        gw_ref[...] = lax.dot_general(mn.astype(BF16), g, (((0,), (0,)), ((), ())), preferred_element_type=F32)
        d_mn = lax.dot_general(g, w_ref[...], (((1,), (1,)), ((), ())), preferred_element_type=F32)
        gn_ref[...] = vjp(d_mn)[1]

    return pl.pallas_call(
        body, name="mem_kv_bwd",
        out_shape=[jax.ShapeDtypeStruct((D_MODEL, 2 * C_HEADS * C_DIM), F32), jax.ShapeDtypeStruct((1, D_MODEL), F32)],
        compiler_params=_params(),
    )(mem, mem_norm_w, w_kv, d_kv)


def _c_tile(q_t, z_t, kvm, pc):
    width = C_HEADS * C_DIM
    outs = []
    for h in range(C_HEADS):
        q = _rms(q_t[:, h * C_DIM:(h + 1) * C_DIM], pc[0:1, :])
        k = _rms(kvm[:, h * C_DIM:(h + 1) * C_DIM], pc[1:2, :])
        v = kvm[:, width + h * C_DIM:width + (h + 1) * C_DIM]
        s = _mm_nt(q, k) * (C_DIM ** -0.5)
        p = jnp.exp(s - jnp.max(s, axis=1, keepdims=True))
        p = p / jnp.sum(p, axis=1, keepdims=True)
        outs.append(_mm(p, v))
    return jnp.concatenate(outs, axis=1) * _silu(z_t)


def _attn_c_fwd(proj, kvm, pc, tq=256):
    s = proj.shape[0]

    def body(q_ref, z_ref, kv_ref, pc_ref, o_ref):
        o_ref[...] = _c_tile(q_ref[...], z_ref[...], kv_ref[...], pc_ref[...]).astype(BF16)

    return pl.pallas_call(
        body, name="attn_c_fwd", grid=(s // tq,),
        in_specs=[pl.BlockSpec((tq, 512), lambda i: (i, P_QC // 512)), pl.BlockSpec((tq, 512), lambda i: (i, P_ZC // 512)),
                  pl.BlockSpec(kvm.shape, lambda i: (0, 0)), pl.BlockSpec((8, LANE), lambda i: (0, 0))],
        out_specs=pl.BlockSpec((tq, 512), lambda i: (i, 0)),
        out_shape=jax.ShapeDtypeStruct((s, 512), BF16), compiler_params=_params(("parallel",)),
    )(proj, proj, kvm, pc)


def _attn_c_bwd(proj, kvm, pc, d_mixed, tq=256):
    s = proj.shape[0]

    def body(q_ref, z_ref, kv_ref, pc_ref, dm_ref, dq_ref, dz_ref, dkv_ref, dpc_ref):
        @pl.when(pl.program_id(0) == 0)
        def _():
            dkv_ref[...] = jnp.zeros_like(dkv_ref)
            dpc_ref[...] = jnp.zeros_like(dpc_ref)

        _, vjp = jax.vjp(_c_tile, q_ref[...], z_ref[...], kv_ref[...], pc_ref[...])
        dq, dz, dkv, dpc = vjp(dm_ref[...])
        dq_ref[...] = dq
        dz_ref[...] = dz
        dkv_ref[...] += dkv
        dpc_ref[...] += dpc

    blk = pl.BlockSpec((tq, 512), lambda i: (i, 0))
    kvs = pl.BlockSpec(kvm.shape, lambda i: (0, 0))
    small = pl.BlockSpec((8, LANE), lambda i: (0, 0))
    return pl.pallas_call(
        body, name="attn_c_bwd", grid=(s // tq,),
        in_specs=[pl.BlockSpec((tq, 512), lambda i: (i, P_QC // 512)), pl.BlockSpec((tq, 512), lambda i: (i, P_ZC // 512)),
                  kvs, small, pl.BlockSpec((tq, 512), lambda i: (i, 3))],
        out_specs=[blk, blk, kvs, small],
        out_shape=[jax.ShapeDtypeStruct((s, 512), F32), jax.ShapeDtypeStruct((s, 512), F32),
                   jax.ShapeDtypeStruct(kvm.shape, F32), jax.ShapeDtypeStruct((8, LANE), F32)],
        compiler_params=_params(("arbitrary",)),
    )(proj, proj, kvm, pc, d_mixed)


def _pad_row(v, width=LANE):
    v = v.reshape(1, -1)
    return jnp.pad(v, ((0, 0), (0, width - v.shape[1])))


def _local_step(x, mem, target, norm_w, w_perm, conv_w, pa, pb, pc, mem_norm_w, w_kv, w_out):
    s = x.shape[0]
    cos2, sin2 = _rope_tables(s)
    hn = _rms_fwd(x, norm_w)
    wide = dict(tm=1024, tn=512, tk=2048)
    proj = _matmul(hn, w_perm, "nn", F32, "mm_proj", **wide)
    cqkv = _conv_fwd(proj, conv_w)
    mixed_a = _delta_fwd(cqkv, proj, pa)
    mixed_b = _attn_b_fwd(proj, cos2, sin2, pb)
    kvm = _mem_kv_fwd(mem, mem_norm_w, w_kv)
    mixed_c = _attn_c_fwd(proj, kvm, pc)
    mixed = jnp.concatenate([mixed_a, mixed_b, mixed_c], axis=1)
    mo = _matmul(mixed, w_out, "nn", F32, "mm_out", **wide)
    dy, dyb, loss_parts = _loss_dy(x, mo, target)

    d_mixed = _matmul(dyb, w_out, "nt", F32, "mm_dmixed", **wide)
    g_w_out = _matmul(mixed, dyb, "tn", F32, "mm_gwout", **wide)
    d_qc, d_zc, d_kvm, d_pc = _attn_c_bwd(proj, kvm, pc, d_mixed)
    g_w_kv, g_mem_norm = _mem_kv_bwd(mem, mem_norm_w, w_kv, d_kvm)
    d_qb, d_zb, d_kb, d_vb, d_pb = _attn_b_bwd(proj, cos2, sin2, pb, d_mixed)
    d_c, d_za, d_gt, d_pa = _delta_bwd(cqkv, proj, pa, d_mixed)
    d_qkv, g_conv = _conv_bwd(proj, conv_w, d_c)
    d_proj = jnp.concatenate([d_qkv, d_za, d_qb, d_zb, d_qc, d_zc, d_kb, d_vb, d_gt,
                              jnp.zeros((s, P_WIDTH - P_GT - LANE), F32)], axis=1).astype(BF16)
    d_hn = _matmul(d_proj, w_perm, "nt", F32, "mm_dhn", tm=1024, tn=1024, tk=512)
    g_w_perm = _matmul(hn, d_proj, "tn", F32, "mm_gwin", **wide)
    g_x, g_norm = _rms_bwd(x, norm_w, d_hn, dy)
    return dict(loss_parts=loss_parts, g_x=g_x, g_norm=g_norm, g_w_perm=g_w_perm, g_conv=g_conv, d_pa=d_pa,
                d_pb=d_pb, d_pc=d_pc, g_mem_norm=g_mem_norm, g_w_kv=g_w_kv, g_w_out=g_w_out)


def _permute_cols(w):
    pad = jnp.zeros((w.shape[0], P_WIDTH - IN_WIDTH), w.dtype)
    return jnp.concatenate([w[:, :O_GT], w[:, O_QB:O_KB], w[:, O_ZB:O_QC], w[:, O_QC:O_ZC], w[:, O_ZC:IN_WIDTH],
                            w[:, O_KB:O_VB], w[:, O_VB:O_ZB], w[:, O_GT:O_QB], pad], axis=1)


def _unpermute_cols(g):
    return jnp.concatenate([g[:, :P_QB], g[:, P_GT:P_GT + 32], g[:, P_QB:P_ZB], g[:, P_KB:P_VB], g[:, P_VB:P_GT],
                            g[:, P_ZB:P_QC], g[:, P_QC:P_ZC], g[:, P_ZC:P_KB]], axis=1)


HBM = pl.BlockSpec(memory_space=pltpu.HBM)


def _place():
    x, y, c = lax.axis_index("x"), lax.axis_index("y"), lax.axis_index("c")
    chips = [(1 - x, y), (x, 1 - y), (1 - x, 1 - y)]
    return x, y, c, 2 * x + y, chips, [2 * cx + cy for cx, cy in chips]


COPY_CHUNK_ROWS = 128
COPY_MAX_CHUNKS = 8
PAIR_PIECE_ROWS = 256


class _Copies:
    def __init__(self, make, src, dst):
        rows = src.shape[-2]
        n = max(1, min(COPY_MAX_CHUNKS, rows // COPY_CHUNK_ROWS))
        assert rows % n == 0
        step = rows // n
        lead = (slice(None),) * (len(src.shape) - 2)
        self.whole = make(src, dst)
        self.parts = [self.whole] if n == 1 else [
            make(src.at[lead + (pl.ds(i * step, step), slice(None))], dst.at[lead + (pl.ds(i * step, step), slice(None))])
            for i in range(n)]

    def start(self):
        for p in self.parts:
            p.start()

    def wait(self):
        self.whole.wait()

    def wait_send(self):
        self.whole.wait_send()

    def wait_recv(self):
        self.whole.wait_recv()


def _remote(src, dst, send_sems, recv_sems, k, to):
    def make(s, d):
        return pltpu.make_async_remote_copy(src_ref=s, dst_ref=d, send_sem=send_sems.at[k], recv_sem=recv_sems.at[k],
                                            device_id=to, device_id_type=MESH)
    return _Copies(make, src, dst)


def _local(src, dst, sem):
    return _Copies(lambda s, d: pltpu.make_async_copy(s, d, sem), src, dst)


def _half_rows(ref, c):
    half = ref.shape[-2] // 2
    return pl.ds(pl.multiple_of(c * half, 8), half)


def _all_gather_weights(w_in_b, w_out_b, w_kv_b, conv_b):
    bigs = (w_in_b, w_out_b, w_kv_b)
    n_big = len(bigs)

    def body(*refs):
        srcs, conv_src = refs[:n_big], refs[n_big]
        dsts, conv_dst = refs[n_big + 1:2 * n_big + 1], refs[2 * n_big + 1]
        send_sems, recv_sems, local_sems = refs[2 * n_big + 2:]
        x, y, c, me, chips, chip_ids = _place()
        sibling = (x, y, 1 - c)
        local = [_local(src, dst.at[me], local_sems.at[a]) for a, (src, dst) in enumerate(zip(srcs, dsts))]
        local.append(_local(conv_src, conv_dst.at[me], local_sems.at[n_big]))
        for cp in local:
            cp.start()
        sends = []
        for a, (src, dst) in enumerate(zip(srcs, dsts)):
            mine = _half_rows(src, c)
            for j, chip in enumerate(chips):
                sends.append(_remote(src.at[mine, :], dst.at[me, mine, :], send_sems, recv_sems, 6 * a + j, (*chip, c)))
        for j, chip in enumerate(chips):
            sends.append(_remote(conv_src, conv_dst.at[me], send_sems, recv_sems, 6 * n_big + j, (*chip, c)))
        for cp in sends:
            cp.start()
        passed = []
        for a, (src, dst) in enumerate(zip(srcs, dsts)):
            mine = _half_rows(src, c)
            for j, cid in enumerate(chip_ids):
                landed = dst.at[cid, mine, :]
                _remote(landed, landed, send_sems, recv_sems, 6 * a + j, sibling).wait_recv()
                cp = _remote(landed, landed, send_sems, recv_sems, 6 * a + 3 + j, sibling)
                cp.start()
                passed.append(cp)
        for a, (src, dst) in enumerate(zip(srcs, dsts)):
            other = _half_rows(src, 1 - c)
            for j, cid in enumerate(chip_ids):
                landed = dst.at[cid, other, :]
                _remote(landed, landed, send_sems, recv_sems, 6 * a + 3 + j, sibling).wait_recv()
        for j, cid in enumerate(chip_ids):
            _remote(conv_src, conv_dst.at[cid], send_sems, recv_sems, 6 * n_big + j, sibling).wait_recv()
        for cp in sends + passed:
            cp.wait_send()
        for cp in local:
            cp.wait()

    n_sem = 6 * n_big + 3
    return pl.pallas_call(
        body, name="all_gather_weights",
        out_shape=[jax.ShapeDtypeStruct((N_CHIPS,) + w.shape, w.dtype) for w in bigs + (conv_b,)],
        in_specs=[pl.BlockSpec(memory_space=pltpu.VMEM)] * (n_big + 1), out_specs=[HBM] * (n_big + 1),
        scratch_shapes=[pltpu.SemaphoreType.DMA((n_sem,)), pltpu.SemaphoreType.DMA((n_sem,)),
                        pltpu.SemaphoreType.DMA((n_big + 1,))],
        compiler_params=_params(),
    )(*bigs, conv_b)


def _pair_exchange(grads):
    n = len(grads)
    piece = PAIR_PIECE_ROWS

    def body(*refs):
        srcs, gots = refs[:n], refs[n:2 * n]
        stages = refs[2 * n:3 * n]
        send_sems, recv_sems, load_sems = refs[3 * n:]
        x, y, c, _, _, _ = _place()
        sibling = (x, y, 1 - c)
        for a in range(n):
            slabs, half, _ = gots[a].shape
            per_slab = half // piece
            first = (1 - c) * half
            loads, sends = [], []
            for i in range(slabs * per_slab):
                k, r, slot = i // per_slab, i % per_slab, i % 2
                rows = pl.ds(pl.multiple_of(first + r * piece, 8), piece)
                loads.append(pltpu.make_async_copy(srcs[a].at[k, rows, :], stages[a].at[slot], load_sems.at[2 * a + slot]))
                sends.append(pltpu.make_async_remote_copy(
                    src_ref=stages[a].at[slot], dst_ref=gots[a].at[k, pl.ds(r * piece, piece), :],
                    send_sem=send_sems.at[2 * a + slot], recv_sem=recv_sems.at[a], device_id=sibling, device_id_type=MESH))
            loads[0].start()
            for i in range(len(loads)):
                loads[i].wait()
                sends[i].start()
                if i + 1 < len(loads):
                    if i >= 1:
                        sends[i - 1].wait_send()
                    loads[i + 1].start()
            for cp in sends[-2:]:
                cp.wait_send()
        for a in range(n):
            whole = srcs[a].at[:, _half_rows(srcs[a], c), :]
            pltpu.make_async_remote_copy(src_ref=whole, dst_ref=gots[a], send_sem=send_sems.at[2 * a],
                                         recv_sem=recv_sems.at[a], device_id=sibling, device_id_type=MESH).wait_recv()

    halves = [jax.ShapeDtypeStruct((g.shape[0], g.shape[1] // 2, g.shape[2]), g.dtype) for g in grads]
    assert all(h.shape[1] % piece == 0 and (h.shape[0] * h.shape[1] // piece) >= 2 for h in halves)
    return pl.pallas_call(
        body, name="grad_pair_exchange", out_shape=halves, in_specs=[HBM] * n, out_specs=[HBM] * n,
        scratch_shapes=[pltpu.VMEM((2, piece, g.shape[2]), g.dtype) for g in grads]
        + [pltpu.SemaphoreType.DMA((2 * n,)), pltpu.SemaphoreType.DMA((n,)), pltpu.SemaphoreType.DMA((2 * n,))],
        compiler_params=_params(),
    )(*grads)


def _chip_exchange(halves):
    n = len(halves)

    def body(*refs):
        srcs, lands = refs[:n], refs[n:2 * n]
        send_sems, recv_sems = refs[2 * n:]
        x, y, c, me, chips, chip_ids = _place()
        gives = []
        for a in range(n):
            for j, (chip, cid) in enumerate(zip(chips, chip_ids)):
                give = _remote(srcs[a].at[cid], lands[a].at[j], send_sems, recv_sems, 3 * a + j, (*chip, c))
                give.start()
                gives.append(give)
        for a in range(n):
            for j, cid in enumerate(chip_ids):
                _remote(srcs[a].at[cid], lands[a].at[j], send_sems, recv_sems, 3 * a + j, (x, y, c)).wait_recv()
        for give in gives:
            give.wait_send()

    return pl.pallas_call(
        body, name="grad_chip_exchange",
        out_shape=[jax.ShapeDtypeStruct((N_CHIPS - 1,) + h.shape[1:], h.dtype) for h in halves],
        in_specs=[HBM] * n, out_specs=[HBM] * n,
        scratch_shapes=[pltpu.SemaphoreType.DMA((3 * n,)), pltpu.SemaphoreType.DMA((3 * n,))],
    )(*halves)


def _pair_gather(halves):
    n = len(halves)

    def body(*refs):
        srcs, fulls = refs[:n], refs[n:2 * n]
        send_sems, recv_sems, local_sems = refs[2 * n:]
        x, y, c, _, _, _ = _place()
        copies = []
        for a in range(n):
            mine = _half_rows(fulls[a], c)
            keep = _local(srcs[a], fulls[a].at[mine, :], local_sems.at[a])
            keep.start()
            give = _remote(srcs[a], fulls[a].at[mine, :], send_sems, recv_sems, a, (x, y, 1 - c))
            give.start()
            copies += [keep, give]
        for a in range(n):
            other = _half_rows(fulls[a], 1 - c)
            copies[2 * a].wait()
            copies[2 * a + 1].wait_send()
            _remote(srcs[a], fulls[a].at[other, :], send_sems, recv_sems, a, (x, y, 1 - c)).wait_recv()

    return pl.pallas_call(
        body, name="grad_pair_gather",
        out_shape=[jax.ShapeDtypeStruct((2 * h.shape[0], h.shape[1]), h.dtype) for h in halves],
        in_specs=[pl.BlockSpec(memory_space=pltpu.VMEM)] * n, out_specs=[HBM] * n,
        scratch_shapes=[pltpu.SemaphoreType.DMA((n,)), pltpu.SemaphoreType.DMA((n,)), pltpu.SemaphoreType.DMA((n,))],
    )(*halves)


def _all_reduce_small(p):
    n_dev = 8

    def body(p_ref, o_ref, land, send_sems, recv_sems):
        x, y, c = lax.axis_index("x"), lax.axis_index("y"), lax.axis_index("c")
        me = 4 * x + 2 * y + c
        land[me] = p_ref[...]
        sends = []
        for k in range(1, n_dev):
            fx, fy, fc = (k >> 2) & 1, (k >> 1) & 1, k & 1
            to = (x ^ fx, y ^ fy, c ^ fc)
            cp = _remote(p_ref, land.at[me], send_sems, recv_sems, k - 1, to)
            cp.start()
            sends.append(cp)
        for k in range(1, n_dev):
            _remote(p_ref, land.at[me ^ k], send_sems, recv_sems, k - 1, (x, y, c)).wait_recv()
        total = land[0]
        for d in range(1, n_dev):
            total = total + land[d]
        o_ref[...] = total
        for cp in sends:
            cp.wait_send()

    vm = pl.BlockSpec(memory_space=pltpu.VMEM)
    return pl.pallas_call(
        body, name="all_reduce_small", out_shape=jax.ShapeDtypeStruct(p.shape, p.dtype), in_specs=[vm], out_specs=vm,
        scratch_shapes=[pltpu.VMEM((n_dev,) + p.shape, p.dtype), pltpu.SemaphoreType.DMA((n_dev - 1,)),
                        pltpu.SemaphoreType.DMA((n_dev - 1,))],
    )(p)


def _row_tile(rows, cap=256):
    return cap if rows % cap == 0 else rows


def _pair_sum(full, got, core, name):
    n, r, c = got.shape
    tr = _row_tile(r)
    nt = r // tr

    def body(core_ref, a_ref, b_ref, o_ref):
        o_ref[...] = a_ref[...] + b_ref[...]

    blk = pl.BlockSpec((None, tr, c), lambda i, j, core_ref: (i, j, 0))
    grid_spec = pltpu.PrefetchScalarGridSpec(
        num_scalar_prefetch=1, grid=(n, nt),
        in_specs=[pl.BlockSpec((None, tr, c), lambda i, j, core_ref: (i, core_ref[0] * nt + j, 0)), blk], out_specs=blk)
    return pl.pallas_call(body, name=name, grid_spec=grid_spec, out_shape=jax.ShapeDtypeStruct(got.shape, got.dtype),
                          compiler_params=_params(("parallel", "parallel")))(core, full, got)


def _chip_sum(pair, land, chip, name):
    n, r, c = land.shape
    tr = _row_tile(r)

    def body(chip_ref, own_ref, l_ref, o_ref):
        total = own_ref[...]
        for j in range(n):
            total = total + l_ref[j]
        o_ref[...] = total

    grid_spec = pltpu.PrefetchScalarGridSpec(
        num_scalar_prefetch=1, grid=(r // tr,),
        in_specs=[pl.BlockSpec((None, tr, c), lambda i, chip_ref: (chip_ref[0], i, 0)),
                  pl.BlockSpec((n, tr, c), lambda i, chip_ref: (0, i, 0))],
        out_specs=pl.BlockSpec((tr, c), lambda i, chip_ref: (i, 0)))
    return pl.pallas_call(body, name=name, grid_spec=grid_spec, out_shape=jax.ShapeDtypeStruct((r, c), land.dtype),
                          compiler_params=_params(("parallel",)))(chip, pair, land)


def _adamw(w, g, m, v, name):
    r, c = w.shape
    tr = _row_tile(r)

    def body(w_ref, g_ref, m_ref, v_ref, d_ref, mo_ref, vo_ref):
        g_ = g_ref[...]
        m2 = ADAM_B1 * m_ref[...] + (1.0 - ADAM_B1) * g_
        v2 = ADAM_B2 * v_ref[...] + (1.0 - ADAM_B2) * jnp.square(g_)
        m_hat = m2 / (1.0 - ADAM_B1 ** ADAM_STEP)
        v_hat = v2 / (1.0 - ADAM_B2 ** ADAM_STEP)
        d_ref[...] = -ADAM_LR * (m_hat / (jnp.sqrt(v_hat) + ADAM_EPS) + ADAM_WD * w_ref[...])
        mo_ref[...] = m2
        vo_ref[...] = v2

    blk = pl.BlockSpec((tr, c), lambda i: (i, 0))
    return pl.pallas_call(body, name=name, grid=(r // tr,), in_specs=[blk] * 4, out_specs=[blk] * 3,
                          out_shape=[jax.ShapeDtypeStruct(w.shape, F32)] * 3, compiler_params=_params(("parallel",)))(w, g, m, v)


SMALL_NAMES = ("norm_w", "mem_norm_w", "o_norm_a", "q_norm_c", "k_norm_c", "q_norm_b", "k_norm_b",
               "a_log_fwd", "a_log_bwd", "dt_bias_fwd", "dt_bias_bwd", "sink_b")
SMALL_SIZES = (2048, 2048, 128, 128, 128, 64, 64, 8, 8, 8, 8, 8)
SMALL_LOSS = sum(SMALL_SIZES)
SMALL_CONV = 5120
SMALL_TOTAL = SMALL_CONV + CONV_K * 3 * A_WIDTH
SMALL_ROWS = SMALL_TOTAL // LANE


def _pack_small(parts, extra=None, conv=None):
    vec = [parts[n].reshape(-1) for n in SMALL_NAMES]
    vec.append(jnp.zeros((1,), F32) if extra is None else extra.reshape(1))
    vec.append(jnp.zeros((SMALL_CONV - SMALL_LOSS - 1,), F32))
    vec.append(jnp.zeros((SMALL_TOTAL - SMALL_CONV,), F32) if conv is None else conv.reshape(-1))
    return jnp.concatenate(vec).reshape(SMALL_ROWS, LANE)


def _unpack_small(packed):
    flat = packed.reshape(-1)
    out, off = {}, 0
    for n, size in zip(SMALL_NAMES, SMALL_SIZES):
        out[n] = flat[off:off + size].reshape(1, size)
        off += size
    return out


WEIGHT_ORDER = ("norm_w", "w_in", "conv_w_a", "a_log_fwd", "a_log_bwd", "dt_bias_fwd", "dt_bias_bwd", "o_norm_a",
                "q_norm_b", "k_norm_b", "sink_b", "mem_norm_w", "w_mem_kv", "q_norm_c", "k_norm_c", "w_out")


def kernel(x, mem, norm_w, w_in, conv_w_a, a_log_fwd, a_log_bwd, dt_bias_fwd, dt_bias_bwd, o_norm_a, q_norm_b, k_norm_b, sink_b, mem_norm_w, w_mem_kv, q_norm_c, k_norm_c, w_out, loss_target, m_norm_w, m_w_in, m_conv_w_a, m_a_log_fwd, m_a_log_bwd, m_dt_bias_fwd, m_dt_bias_bwd, m_o_norm_a, m_q_norm_b, m_k_norm_b, m_sink_b, m_mem_norm_w, m_w_mem_kv, m_q_norm_c, m_k_norm_c, m_w_out, v_norm_w, v_w_in, v_conv_w_a, v_a_log_fwd, v_a_log_bwd, v_dt_bias_fwd, v_dt_bias_bwd, v_o_norm_a, v_q_norm_b, v_k_norm_b, v_sink_b, v_mem_norm_w, v_w_mem_kv, v_q_norm_c, v_k_norm_c, v_w_out):
    weights = dict(norm_w=norm_w, w_in=w_in, conv_w_a=conv_w_a, a_log_fwd=a_log_fwd, a_log_bwd=a_log_bwd,
                   dt_bias_fwd=dt_bias_fwd, dt_bias_bwd=dt_bias_bwd, o_norm_a=o_norm_a, q_norm_b=q_norm_b,
                   k_norm_b=k_norm_b, sink_b=sink_b, mem_norm_w=mem_norm_w, w_mem_kv=w_mem_kv, q_norm_c=q_norm_c,
                   k_norm_c=k_norm_c, w_out=w_out)
    mom1 = dict(norm_w=m_norm_w, w_in=m_w_in, conv_w_a=m_conv_w_a, a_log_fwd=m_a_log_fwd, a_log_bwd=m_a_log_bwd,
                dt_bias_fwd=m_dt_bias_fwd, dt_bias_bwd=m_dt_bias_bwd, o_norm_a=m_o_norm_a, q_norm_b=m_q_norm_b,
                k_norm_b=m_k_norm_b, sink_b=m_sink_b, mem_norm_w=m_mem_norm_w, w_mem_kv=m_w_mem_kv,
                q_norm_c=m_q_norm_c, k_norm_c=m_k_norm_c, w_out=m_w_out)
    mom2 = dict(norm_w=v_norm_w, w_in=v_w_in, conv_w_a=v_conv_w_a, a_log_fwd=v_a_log_fwd, a_log_bwd=v_a_log_bwd,
                dt_bias_fwd=v_dt_bias_fwd, dt_bias_bwd=v_dt_bias_bwd, o_norm_a=v_o_norm_a, q_norm_b=v_q_norm_b,
                k_norm_b=v_k_norm_b, sink_b=v_sink_b, mem_norm_w=v_mem_norm_w, w_mem_kv=v_w_mem_kv,
                q_norm_c=v_q_norm_c, k_norm_c=v_k_norm_c, w_out=v_w_out)
    chip = 2 * lax.axis_index("x") + lax.axis_index("y")

    w_in4, w_out4, w_kv4, conv4 = _all_gather_weights(w_in[0].astype(BF16), w_out[0].astype(BF16),
                                                      w_mem_kv[0].astype(BF16), conv_w_a[0])
    w_perm = _permute_cols(jnp.transpose(w_in4, (1, 0, 2)).reshape(D_MODEL, IN_WIDTH))
    w_out_full = w_out4.reshape(D_MODEL, D_MODEL)
    w_kv_full = w_kv4.reshape(D_MODEL, 2 * C_HEADS * C_DIM)
    conv_full = jnp.transpose(conv4, (1, 0, 2)).reshape(CONV_K, 3 * A_WIDTH)
    pa = jnp.concatenate([_pad_row(a_log_fwd), _pad_row(a_log_bwd), _pad_row(dt_bias_fwd), _pad_row(dt_bias_bwd),
                          _pad_row(o_norm_a), jnp.zeros((3, LANE), F32)], axis=0)
    pb = jnp.concatenate([_pad_row(q_norm_b), _pad_row(k_norm_b), _pad_row(sink_b), jnp.zeros((5, LANE), F32)], axis=0)
    pc = jnp.concatenate([_pad_row(q_norm_c), _pad_row(k_norm_c), jnp.zeros((6, LANE), F32)], axis=0)

    r = _local_step(x[0], mem[0], loss_target[0], norm_w, w_perm, conv_full, pa, pb, pc, mem_norm_w, w_kv_full,
                    w_out_full)

    g_in4 = jnp.transpose(_unpermute_cols(r["g_w_perm"]).reshape(D_MODEL, N_CHIPS, W_IN_BLOCK), (1, 0, 2))
    g_out4 = r["g_w_out"].reshape(N_CHIPS, D_MODEL // N_CHIPS, D_MODEL)
    g_kv4 = r["g_w_kv"].reshape(N_CHIPS, D_MODEL // N_CHIPS, 2 * C_HEADS * C_DIM)
    full = [g_in4, g_out4, g_kv4]
    core = lax.axis_index("c").astype(jnp.int32).reshape(1)
    got = _pair_exchange(full)
    pair = [_pair_sum(a, b, core, "grad_pair_sum_%d" % i) for i, (a, b) in enumerate(zip(full, got))]
    lands = _chip_exchange(pair)
    reduced = [_chip_sum(p, l, chip.astype(jnp.int32).reshape(1), "grad_chip_sum_%d" % i)
               for i, (p, l) in enumerate(zip(pair, lands))]
    g_w_in, g_w_out, g_w_kv = _pair_gather(reduced)

    d_pa, d_pb, d_pc = r["d_pa"], r["d_pb"], r["d_pc"]
    small_g = dict(norm_w=r["g_norm"], mem_norm_w=r["g_mem_norm"], o_norm_a=d_pa[4], q_norm_c=d_pc[0], k_norm_c=d_pc[1],
                   q_norm_b=d_pb[0, :B_DIM], k_norm_b=d_pb[1, :B_DIM], a_log_fwd=d_pa[0, :A_HEADS],
                   a_log_bwd=d_pa[1, :A_HEADS], dt_bias_fwd=d_pa[2, :A_HEADS], dt_bias_bwd=d_pa[3, :A_HEADS],
                   sink_b=d_pb[2, :B_HEADS])
    packed = _all_reduce_small(_pack_small(small_g, jnp.sum(r["loss_parts"][:, 0, 0]), r["g_conv"]))
    flat = packed.reshape(-1)
    loss = flat[SMALL_LOSS]
    conv_sum = flat[SMALL_CONV:].reshape(CONV_K, 3 * A_WIDTH)
    conv_cols = 3 * A_WIDTH // N_CHIPS
    g_conv = lax.dynamic_slice(conv_sum, (0, chip * conv_cols), (CONV_K, conv_cols))

    grads = _unpack_small(packed)
    grads.update(w_in=g_w_in, w_mem_kv=g_w_kv, w_out=g_w_out, conv_w_a=g_conv)
    delta, new_m, new_v = {}, {}, {}
    for n in ("w_in", "w_mem_kv", "w_out", "conv_w_a"):
        delta[n], new_m[n], new_v[n] = _adamw(weights[n][0], grads[n], mom1[n][0], mom2[n][0], "adamw_" + n)
    d_s, m_s, v_s = _adamw(_pack_small(weights), packed, _pack_small(mom1), _pack_small(mom2), "adamw_small")
    d_s, m_s, v_s = _unpack_small(d_s), _unpack_small(m_s), _unpack_small(v_s)
    for n in SMALL_NAMES:
        delta[n], new_m[n], new_v[n] = d_s[n], m_s[n], v_s[n]

    def shaped(tree):
        return [tree[n].reshape(weights[n].shape) for n in WEIGHT_ORDER]

    return (loss, r["g_x"].reshape(x.shape), *shaped(grads), *shaped(delta), *shaped(new_m), *shaped(new_v))
```

```python
import functools

import jax
import jax.numpy as jnp
from jax import lax
from jax.experimental import pallas as pl
from jax.experimental.pallas import tpu as pltpu

F32 = jnp.float32
BF16 = jnp.bfloat16
HI = lax.Precision.HIGHEST
MESH = pl.DeviceIdType.MESH

D_MODEL = 2048
A_WIDTH = 1024
A_HEADS = 8
A_DIM = 128
CONV_K = 5
CHUNK = 64
B_HEADS = 8
B_KV = 2
B_DIM = 64
WINDOW = 128
C_HEADS = 4
C_DIM = 128
MEM_LEN = 256
ROPE_THETA = 10000.0
EPS = 1e-6
IN_WIDTH = 6432
N_CHIPS = 4
W_IN_BLOCK = IN_WIDTH // N_CHIPS

LANE = 128
P_QA, P_KA, P_VA, P_ZA = 0, 1024, 2048, 3072
P_QB, P_ZB, P_QC, P_ZC = 4096, 4608, 5120, 5632
P_KB, P_VB, P_GT = 6144, 6272, 6400
P_WIDTH = 6656
O_GT, O_QB, O_KB, O_VB, O_ZB, O_QC, O_ZC = 4096, 4128, 4640, 4768, 4896, 5408, 5920

ADAM_LR, ADAM_B1, ADAM_B2, ADAM_EPS, ADAM_WD, ADAM_STEP = 0.001, 0.9, 0.999, 1e-08, 0.01, 10

VMEM_LIMIT = 56 * 1024 * 1024


def _params(sem=None):
    return pltpu.CompilerParams(dimension_semantics=sem, vmem_limit_bytes=VMEM_LIMIT)


def _dot(a, b, dims=(((1,), (0,)), ((), ())), precision=HI):
    return lax.dot_general(a, b, dims, precision=precision, preferred_element_type=F32)


def _dot_nt(a, b, precision=HI):
    return _dot(a, b, (((1,), (1,)), ((), ())), precision)


def _dot_tn(a, b, precision=HI):
    return _dot(a, b, (((0,), (0,)), ((), ())), precision)


_NN = (((1,), (0,)), ((), ()))
_NT = (((1,), (1,)), ((), ()))
_TN = (((0,), (0,)), ((), ()))


def _bdot(a, b, dims):
    return lax.dot_general(a.astype(BF16), b.astype(BF16), dims, preferred_element_type=F32)


@jax.custom_vjp
def _mm(a, b):
    return _bdot(a, b, _NN)


_mm.defvjp(lambda a, b: (_bdot(a, b, _NN), (a, b)),
           lambda res, ct: (_bdot(ct, res[1], _NT), _bdot(res[0], ct, _TN)))


@jax.custom_vjp
def _mm_nt(a, b):
    return _bdot(a, b, _NT)


_mm_nt.defvjp(lambda a, b: (_bdot(a, b, _NT), (a, b)),
              lambda res, ct: (_bdot(ct, res[1], _NN), _bdot(ct, res[0], _TN)))


@jax.custom_vjp
def _mm_tn(a, b):
    return _bdot(a, b, _TN)


_mm_tn.defvjp(lambda a, b: (_bdot(a, b, _TN), (a, b)),
              lambda res, ct: (_bdot(res[1], ct, _NT), _bdot(res[0], ct, _NN)))


def _rms(t, w):
    return t * lax.rsqrt(jnp.mean(t * t, axis=-1, keepdims=True) + EPS) * w


def _l2(t):
    return t * lax.rsqrt(jnp.sum(t * t, axis=-1, keepdims=True) + EPS)


def _silu(t):
    return t * jax.nn.sigmoid(t)


def _softplus(t):
    return jnp.maximum(t, 0.0) + jnp.log1p(jnp.exp(-jnp.abs(t)))


def _matmul(a, b, mode, out_dtype, name, tm=512, tn=512, tk=512):
    (m, k) = a.shape[::-1] if mode == "tn" else a.shape
    n = b.shape[0] if mode == "nt" else b.shape[1]
    tm, tn, tk = min(tm, m), min(tn, n), min(tk, k)
    assert m % tm == 0 and n % tn == 0 and k % tk == 0, (m, n, k, tm, tn, tk)
    if mode == "nn":
        a_spec = pl.BlockSpec((tm, tk), lambda i, j, kk: (i, kk))
        b_spec = pl.BlockSpec((tk, tn), lambda i, j, kk: (kk, j))
        dims = (((1,), (0,)), ((), ()))
    elif mode == "nt":
        a_spec = pl.BlockSpec((tm, tk), lambda i, j, kk: (i, kk))
        b_spec = pl.BlockSpec((tn, tk), lambda i, j, kk: (j, kk))
        dims = (((1,), (1,)), ((), ()))
    else:
        a_spec = pl.BlockSpec((tk, tm), lambda i, j, kk: (kk, i))
        b_spec = pl.BlockSpec((tk, tn), lambda i, j, kk: (kk, j))
        dims = (((0,), (0,)), ((), ()))
    nk = k // tk

    def body_one(a_ref, b_ref, o_ref):
        o_ref[...] = _bdot(a_ref[...], b_ref[...], dims).astype(out_dtype)

    def body_acc(a_ref, b_ref, o_ref, acc_ref):
        kk = pl.program_id(2)

        @pl.when(kk == 0)
        def _():
            acc_ref[...] = jnp.zeros_like(acc_ref)

        acc_ref[...] += _bdot(a_ref[...], b_ref[...], dims)

        @pl.when(kk == nk - 1)
        def _():
            o_ref[...] = acc_ref[...].astype(out_dtype)

    return pl.pallas_call(
        body_one if nk == 1 else body_acc, name=name, grid=(m // tm, n // tn, nk),
        in_specs=[a_spec, b_spec], out_specs=pl.BlockSpec((tm, tn), lambda i, j, kk: (i, j)),
        out_shape=jax.ShapeDtypeStruct((m, n), out_dtype),
        scratch_shapes=[] if nk == 1 else [pltpu.VMEM((tm, tn), F32)],
        compiler_params=_params(("parallel", "parallel", "arbitrary")),
    )(a, b)


def _rms_fwd(x, w, tr=256):
    s, d = x.shape

    def body(x_ref, w_ref, o_ref):
        o_ref[...] = _rms(x_ref[...], w_ref[...]).astype(BF16)

    return pl.pallas_call(
        body, name="rms_fwd", grid=(s // tr,),
        in_specs=[pl.BlockSpec((tr, d), lambda i: (i, 0)), pl.BlockSpec((1, d), lambda i: (0, 0))],
        out_specs=pl.BlockSpec((tr, d), lambda i: (i, 0)),
        out_shape=jax.ShapeDtypeStruct((s, d), BF16), compiler_params=_params(("parallel",)),
    )(x, w)


def _rms_bwd(x, w, d_hn, dy, tr=256):
    s, d = x.shape

    def body(x_ref, w_ref, g_ref, dy_ref, gx_ref, gw_ref):
        _, vjp = jax.vjp(_rms, x_ref[...], w_ref[...])
        dx, dw = vjp(g_ref[...])
        gx_ref[...] = dy_ref[...] + dx

        @pl.when(pl.program_id(0) == 0)
        def _():
            gw_ref[...] = jnp.zeros_like(gw_ref)

        gw_ref[...] += dw

    row = pl.BlockSpec((tr, d), lambda i: (i, 0))
    vec = pl.BlockSpec((1, d), lambda i: (0, 0))
    return pl.pallas_call(
        body, name="rms_bwd", grid=(s // tr,), in_specs=[row, vec, row, row], out_specs=[row, vec],
        out_shape=[jax.ShapeDtypeStruct((s, d), F32), jax.ShapeDtypeStruct((1, d), F32)],
        compiler_params=_params(("arbitrary",)),
    )(x, w, d_hn, dy)


def _loss_dy(x, mo, target, tr=256):
    s, d = x.shape
    nt = s // tr

    def body(x_ref, mo_ref, t_ref, dy_ref, dyb_ref, l_ref):
        err = x_ref[...] + mo_ref[...] - t_ref[...]
        dy = err * (1.0 / d)
        dy_ref[...] = dy
        dyb_ref[...] = dy.astype(BF16)
        l_ref[...] = jnp.full(l_ref.shape, 0.5 * jnp.sum(jnp.sum(err * err, axis=1, keepdims=True) * (1.0 / d)), F32)

    row = pl.BlockSpec((tr, d), lambda i: (i, 0))
    return pl.pallas_call(
        body, name="loss_dy", grid=(nt,), in_specs=[row, row, row],
        out_specs=[row, row, pl.BlockSpec((1, 8, LANE), lambda i: (i, 0, 0))],
        out_shape=[jax.ShapeDtypeStruct((s, d), F32), jax.ShapeDtypeStruct((s, d), BF16),
                   jax.ShapeDtypeStruct((nt, 8, LANE), F32)],
        compiler_params=_params(("parallel",)),
    )(x, mo, target)


def _shift_rows(t, s):
    if s == 0:
        return t
    n = t.shape[0]
    rolled = pltpu.roll(t, (-s) % n, axis=0)
    idx = lax.broadcasted_iota(jnp.int32, t.shape, 0) + s
    return jnp.where((idx >= 0) & (idx < n), rolled, 0.0)


def _conv_fwd(proj, conv_w):
    s = proj.shape[0]
    nblk = 3 * A_WIDTH // LANE

    def body(x_ref, w_ref, o_ref):
        x = x_ref[...]
        acc = jnp.zeros_like(x)
        for j in range(CONV_K):
            acc = acc + w_ref[j:j + 1, :] * _shift_rows(x, j - CONV_K // 2)
        o_ref[...] = acc

    return pl.pallas_call(
        body, name="conv_fwd", grid=(nblk,),
        in_specs=[pl.BlockSpec((s, LANE), lambda i: (0, i)), pl.BlockSpec((CONV_K, LANE), lambda i: (0, i))],
        out_specs=pl.BlockSpec((None, s, LANE), lambda i: (i // A_HEADS, 0, i % A_HEADS)),
        out_shape=jax.ShapeDtypeStruct((3, s, A_WIDTH), F32), compiler_params=_params(("parallel",)),
    )(proj, conv_w)


def _conv_bwd(proj, conv_w, d_c):
    s = proj.shape[0]
    nblk = 3 * A_WIDTH // LANE

    def body(x_ref, w_ref, g_ref, dx_ref, dw_ref):
        x, g = x_ref[...], g_ref[...]
        acc = jnp.zeros_like(x)
        for j in range(CONV_K):
            off = j - CONV_K // 2
            acc = acc + w_ref[j:j + 1, :] * _shift_rows(g, -off)
            dw_ref[j:j + 1, :] = jnp.sum(_shift_rows(x, off) * g, axis=0, keepdims=True)
        dx_ref[...] = acc

    col = pl.BlockSpec((s, LANE), lambda i: (0, i))
    wsp = pl.BlockSpec((CONV_K, LANE), lambda i: (0, i))
    dsp = pl.BlockSpec((None, s, LANE), lambda i: (i // A_HEADS, 0, i % A_HEADS))
    return pl.pallas_call(
        body, name="conv_bwd", grid=(nblk,), in_specs=[col, wsp, dsp], out_specs=[col, wsp],
        out_shape=[jax.ShapeDtypeStruct((s, 3 * A_WIDTH), F32), jax.ShapeDtypeStruct((CONV_K, 3 * A_WIDTH), F32)],
        compiler_params=_params(("parallel",)),
    )(proj, conv_w, d_c)


A_STEP_HEADS = 2
A_CHAINS = 2 * A_STEP_HEADS


def _a_chain(st, cq, ck, cv, alpha, beta_raw, a_log, dt_b, incl, strict, last):
    c = CHUNK
    eye = (lax.broadcasted_iota(jnp.int32, (c, c), 0) == lax.broadcasted_iota(jnp.int32, (c, c), 1)).astype(F32)
    gb = -jnp.exp(a_log) * _softplus(alpha + dt_b)
    bb = jax.nn.sigmoid(beta_raw)
    q = _l2(_silu(cq)) * (A_DIM ** -0.5)
    k = _l2(_silu(ck))
    v = _silu(cv)

    gc = _dot(incl, jnp.broadcast_to(gb, (c, LANE)))
    tot = jnp.sum(gc * last, axis=0, keepdims=True)
    m1 = gc[:, :c]
    decay = incl * jnp.exp(incl * (m1 - m1.T))
    kb = k * bb
    vb = v * bb
    a = -(strict * decay * _mm_nt(kb, k))
    tinv = eye + a
    p = a
    for _ in range(5):
        p = _mm(p, p)
        tinv = tinv + _mm(tinv, p)
    eg = jnp.exp(gc)
    u = _mm(tinv, vb)
    w = _mm(tinv, kb * eg)
    qk = _mm_nt(q, k) * decay
    v_new = u - _mm(w, st)
    o = _mm(q * eg, st) + _mm(qk, v_new)
    st_new = st * jnp.exp(tot) + _mm_tn(k * jnp.exp(tot - gc), v_new)
    return st_new, o


def _a_step(sts, cq, ck, cv, gts, pa, h0):
    c = CHUNK
    lane = lax.broadcasted_iota(jnp.int32, (1, LANE), 1)
    ii = lax.broadcasted_iota(jnp.int32, (c, c), 0)
    jj = lax.broadcasted_iota(jnp.int32, (c, c), 1)
    row = lax.broadcasted_iota(jnp.int32, (c, 1), 0)

    def pick(t, col):
        return jnp.sum(jnp.where(lane == col, t, 0.0), axis=1, keepdims=True)

    alpha, beta_raw, a_log, dt_b, incl, strict, last = [], [], [], [], [], [], []
    for b in range(A_CHAINS):
        h, rev = h0 + b // 2, b % 2
        alpha.append(pick(gts[b], h + 8 * rev))
        beta_raw.append(pick(gts[b], h + 16 + 8 * rev))
        a_log.append(pick(pa[rev:rev + 1, :], h))
        dt_b.append(pick(pa[2 + rev:3 + rev, :], h))
        incl.append(((ii <= jj) if rev else (ii >= jj)).astype(F32))
        strict.append(((ii < jj) if rev else (ii > jj)).astype(F32))
        last.append((row == (0 if rev else c - 1)).astype(F32))
    stack = lambda ts: jnp.concatenate([t[None] for t in ts], axis=0)
    return jax.vmap(_a_chain)(sts, cq, ck, cv, stack(alpha), stack(beta_raw), stack(a_log), stack(dt_b),
                              stack(incl), stack(strict), stack(last))


def _a_final(o, za, pa):
    outs = []
    for j in range(o.shape[1] // A_DIM):
        ln = slice(j * A_DIM, (j + 1) * A_DIM)
        outs.append(_rms(o[:, ln], pa[4:5, :]) * _silu(za[:, ln]))
    return jnp.concatenate(outs, axis=1)


def _a_tiles(n, nchunk):
    tiles = []
    for b in range(A_CHAINS):
        i = (nchunk - 1 - n) if b % 2 else n
        tiles.append((i, pl.ds(pl.multiple_of(i * CHUNK, CHUNK), CHUNK), slice((b // 2) * A_DIM, (b // 2 + 1) * A_DIM)))
    return tiles


def _a_load(tiles, c_ref, gt_ref):
    cq, ck, cv = (jnp.stack([c_ref[r, sl, ln] for _, sl, ln in tiles], axis=0) for r in range(3))
    return cq, ck, cv, jnp.stack([gt_ref[sl, :] for _, sl, _ in tiles], axis=0)


def _a_scan(h0, nchunk, c_ref, gt_ref, pa, of_ref, ob_ref, s_ref):
    def step(n, sts):
        tiles = _a_tiles(n, nchunk)
        sts_new, o = _a_step(sts, *_a_load(tiles, c_ref, gt_ref), pa, h0)
        for b, (i, sl, ln) in enumerate(tiles):
            if s_ref is not None:
                s_ref[b, i] = sts[b]
            (ob_ref if b % 2 else of_ref)[sl, ln] = o[b]
        return sts_new

    lax.fori_loop(0, nchunk, step, jnp.zeros((A_CHAINS, A_DIM, A_DIM), F32))


def _a_specs(s):
    wide = A_STEP_HEADS * A_DIM
    once = pl.Buffered(1)
    trio = pl.BlockSpec((3, s, wide), lambda g: (0, 0, g), pipeline_mode=once)
    gates = pl.BlockSpec((s, LANE), lambda g: (0, P_GT // LANE))
    small = pl.BlockSpec((8, LANE), lambda g: (0, 0))

    def cols(base):
        return pl.BlockSpec((s, wide), lambda g: (0, base // wide + g), pipeline_mode=once)

    return wide, trio, gates, small, cols


def _delta_fwd(cqkv, proj, pa):
    s = cqkv.shape[1]
    nchunk = s // CHUNK
    wide, trio, gates, small, cols = _a_specs(s)

    def body(c_ref, gt_ref, za_ref, pa_ref, out_ref, of_ref, ob_ref):
        h0 = pl.program_id(0) * A_STEP_HEADS
        pa_v = pa_ref[...]
        _a_scan(h0, nchunk, c_ref, gt_ref, pa_v, of_ref, ob_ref, None)
        out_ref[...] = _a_final(of_ref[...] + ob_ref[...], za_ref[...], pa_v).astype(BF16)

    return pl.pallas_call(
        body, name="delta_fwd", grid=(A_HEADS // A_STEP_HEADS,),
        in_specs=[trio, gates, cols(P_ZA), small], out_specs=cols(0),
        out_shape=jax.ShapeDtypeStruct((s, D_MODEL), BF16),
        scratch_shapes=[pltpu.VMEM((s, wide), F32)] * 2, compiler_params=_params(("parallel",)),
    )(cqkv, proj, proj, pa)


def _delta_bwd(cqkv, proj, pa, d_mixed):
    s = cqkv.shape[1]
    nchunk = s // CHUNK

    wide, trio, gates, small, cols = _a_specs(s)

    def body(c_ref, gt_ref, za_ref, pa_ref, dm_ref, dc_ref, dza_ref, dgt_ref, dpa_ref, s_ref, of_ref, ob_ref):
        h0 = pl.program_id(0) * A_STEP_HEADS
        pa_v = pa_ref[...]

        @pl.when(h0 == 0)
        def _():
            dgt_ref[...] = jnp.zeros_like(dgt_ref)
            dpa_ref[...] = jnp.zeros_like(dpa_ref)

        _a_scan(h0, nchunk, c_ref, gt_ref, pa_v, of_ref, ob_ref, s_ref)
        _, vjp = jax.vjp(_a_final, of_ref[...] + ob_ref[...], za_ref[...], pa_v)
        d_o, d_za, dpa0 = vjp(dm_ref[...])
        of_ref[...] = d_o
        dza_ref[...] = d_za
        dc_ref[...] = jnp.zeros_like(dc_ref)

        def step(n, carry):
            d_sts, dpa = carry
            tiles = _a_tiles(nchunk - 1 - n, nchunk)
            sts = jnp.stack([s_ref[b, i] for b, (i, _, _) in enumerate(tiles)], axis=0)
            d_o_t = jnp.stack([of_ref[sl, ln] for _, sl, ln in tiles], axis=0)
            _, vjp_c = jax.vjp(lambda *a: _a_step(*a, h0), sts, *_a_load(tiles, c_ref, gt_ref), pa_v)
            d_prev, dcq, dck, dcv, dgts, dpa_i = vjp_c((d_sts, d_o_t))
            for b, (_, sl, ln) in enumerate(tiles):
                for r, dc in enumerate((dcq, dck, dcv)):
                    dc_ref[r, sl, ln] += dc[b]
                dgt_ref[sl, :] += dgts[b]
            return d_prev, dpa + dpa_i

        _, dpa_out = lax.fori_loop(0, nchunk, step, (jnp.zeros((A_CHAINS, A_DIM, A_DIM), F32), dpa0))
        dpa_ref[...] += dpa_out

    fixed = pl.BlockSpec((s, LANE), lambda g: (0, 0))
    return pl.pallas_call(
        body, name="delta_bwd", grid=(A_HEADS // A_STEP_HEADS,),
        in_specs=[trio, gates, cols(P_ZA), small, cols(0)], out_specs=[trio, cols(0), fixed, small],
        out_shape=[jax.ShapeDtypeStruct((3, s, A_WIDTH), F32), jax.ShapeDtypeStruct((s, A_WIDTH), F32),
                   jax.ShapeDtypeStruct((s, LANE), F32), jax.ShapeDtypeStruct((8, LANE), F32)],
        scratch_shapes=[pltpu.VMEM((A_CHAINS, nchunk, A_DIM, A_DIM), F32), pltpu.VMEM((s, wide), F32),
                        pltpu.VMEM((s, wide), F32)],
        compiler_params=_params(("arbitrary",)),
    )(cqkv, proj, proj, pa, d_mixed)


def _rope_tables(s):
    inv = ROPE_THETA ** (-jnp.arange(0, B_DIM, 2, dtype=F32) / B_DIM)
    ang = jnp.arange(s, dtype=F32)[:, None] * inv[None, :]
    cos, sin = jnp.cos(ang), jnp.sin(ang)
    return jnp.concatenate([cos, cos], axis=1), jnp.concatenate([-sin, sin], axis=1)


def _b_block(q_t, z_t, k3, v3, cos_q, sin_q, cos_k, sin_k, pb, n, nb):
    w = WINDOW
    def swap(t):
        return jnp.concatenate([t[:, B_DIM // 2:], t[:, :B_DIM // 2]], axis=1)

    grp = B_HEADS // B_KV
    qi = lax.broadcasted_iota(jnp.int32, (grp * w, 3 * w), 0) & (w - 1)
    kj = lax.broadcasted_iota(jnp.int32, (grp * w, 3 * w), 1)
    kpos = kj + (n - 1) * w
    mask = (jnp.abs(kj - w - qi) <= w) & (kpos >= 0) & (kpos < nb * w)
    lane = lax.broadcasted_iota(jnp.int32, (1, LANE), 1)
    qn, kn = pb[0:1, :B_DIM], pb[1:2, :B_DIM]
    cos_g = jnp.concatenate([cos_q] * grp, axis=0)
    sin_g = jnp.concatenate([sin_q] * grp, axis=0)
    outs = []
    for hk in range(B_KV):
        k = _rms(k3[:, hk * B_DIM:(hk + 1) * B_DIM], kn)
        k = k * cos_k + swap(k) * sin_k
        v = v3[:, hk * B_DIM:(hk + 1) * B_DIM]
        heads = [hk * grp + g for g in range(grp)]
        q = _rms(jnp.concatenate([q_t[:, hq * B_DIM:(hq + 1) * B_DIM] for hq in heads], axis=0), qn)
        q = q * cos_g + swap(q) * sin_g
        sink = jnp.concatenate(
            [jnp.broadcast_to(jnp.sum(jnp.where(lane == hq, pb[2:3, :], 0.0), axis=1, keepdims=True), (w, 1))
             for hq in heads], axis=0)
        s = _mm_nt(q, k) * (B_DIM ** -0.5)
        s = jnp.where(mask, s, -jnp.inf)
        m = jnp.maximum(jnp.max(s, axis=1, keepdims=True), sink)
        p = jnp.exp(s - m)
        p = p / (jnp.sum(p, axis=1, keepdims=True) + jnp.exp(sink - m))
        o = _mm(p, v)
        outs += [o[g * w:(g + 1) * w, :] for g in range(grp)]
    return jnp.concatenate(outs, axis=1) * _silu(z_t)


def _b_specs(s):
    nb = s // WINDOW
    qsp = pl.BlockSpec((WINDOW, 512), lambda n: (n, P_QB // 512))
    zsp = pl.BlockSpec((WINDOW, 512), lambda n: (n, P_ZB // 512))

    def three(col, width):
        return [pl.BlockSpec((WINDOW, width), lambda n: (jnp.maximum(n - 1, 0), col)),
                pl.BlockSpec((WINDOW, width), lambda n: (n, col)),
                pl.BlockSpec((WINDOW, width), lambda n: (jnp.minimum(n + 1, nb - 1), col))]

    tab = pl.BlockSpec((WINDOW, B_DIM), lambda n: (n, 0))
    small = pl.BlockSpec((8, LANE), lambda n: (0, 0))
    specs = [qsp, zsp] + three(P_KB // LANE, LANE) + three(P_VB // LANE, LANE) + [tab, tab] + three(0, B_DIM) + three(0, B_DIM) + [small]
    return nb, specs


def _b_args(proj, cos2, sin2, pb):
    return (proj, proj, proj, proj, proj, proj, proj, proj, cos2, sin2, cos2, cos2, cos2, sin2, sin2, sin2, pb)


def _b_load(refs):
    (q_ref, z_ref, kp, kc, kx, vp, vc, vx, cq, sq, ckp, ckc, ckx, skp, skc, skx, pb_ref) = refs
    cat = lambda *r: jnp.concatenate([t[...] for t in r], axis=0)
    return (q_ref[...], z_ref[...], cat(kp, kc, kx), cat(vp, vc, vx), cq[...], sq[...], cat(ckp, ckc, ckx),
            cat(skp, skc, skx), pb_ref[...])


def _attn_b_fwd(proj, cos2, sin2, pb, mixed):
    s = proj.shape[0]
    nb, specs = _b_specs(s)

    def body(*refs):
        o_ref = refs[-1]
        args = _b_load(refs[:-2])
        o_ref[...] = _b_block(*args, pl.program_id(0), nb).astype(BF16)

    return pl.pallas_call(
        body, name="attn_b_fwd", grid=(nb,), in_specs=specs + [pl.BlockSpec(memory_space=pl.ANY)],
        out_specs=pl.BlockSpec((WINDOW, 512), lambda n: (n, A_WIDTH // 512)),
        out_shape=jax.ShapeDtypeStruct(mixed.shape, mixed.dtype), input_output_aliases={len(specs): 0},
        compiler_params=_params(("parallel",)),
    )(*_b_args(proj, cos2, sin2, pb), mixed)


def _attn_b_bwd(proj, cos2, sin2, pb, d_mixed):
    s = proj.shape[0]
    nb, specs = _b_specs(s)
    w = WINDOW

    def body(*refs):
        dm_ref, dq_ref, dz_ref, dk_ref, dv_ref, dpb_ref = refs[-6:]
        n = pl.program_id(0)
        q_t, z_t, k3, v3, cq, sq, ck, sk, pb_v = _b_load(refs[:-6])

        @pl.when(n == 0)
        def _():
            dk_ref[...] = jnp.zeros_like(dk_ref)
            dv_ref[...] = jnp.zeros_like(dv_ref)
            dpb_ref[...] = jnp.zeros_like(dpb_ref)

        def f(q_, z_, k_, v_, pb_):
            return _b_block(q_, z_, k_, v_, cq, sq, ck, sk, pb_, n, nb)

        _, vjp = jax.vjp(f, q_t, z_t, k3, v3, pb_v)
        dq, dz, dk3, dv3, dpb = vjp(dm_ref[...])
        dq_ref[...] = dq
        dz_ref[...] = dz
        dpb_ref[...] += dpb

        def add(j, cond):
            @pl.when(cond)
            def _():
                rows = pl.ds(pl.multiple_of((n - 1 + j) * w, w), w)
                dk_ref[rows, :] += dk3[j * w:(j + 1) * w, :]
                dv_ref[rows, :] += dv3[j * w:(j + 1) * w, :]

        add(0, n > 0)
        add(1, n >= 0)
        add(2, n < nb - 1)

    blk = pl.BlockSpec((w, 512), lambda n: (n, 0))
    whole = pl.BlockSpec((s, LANE), lambda n: (0, 0))
    small = pl.BlockSpec((8, LANE), lambda n: (0, 0))
    return pl.pallas_call(
        body, name="attn_b_bwd", grid=(nb,),
        in_specs=specs + [pl.BlockSpec((w, 512), lambda n: (n, 2))],
        out_specs=[blk, blk, whole, whole, small],
        out_shape=[jax.ShapeDtypeStruct((s, 512), F32), jax.ShapeDtypeStruct((s, 512), F32),
                   jax.ShapeDtypeStruct((s, LANE), F32), jax.ShapeDtypeStruct((s, LANE), F32),
                   jax.ShapeDtypeStruct((8, LANE), F32)],
        compiler_params=_params(("arbitrary",)),
    )(*_b_args(proj, cos2, sin2, pb), d_mixed)


def _mem_kv_fwd(mem, mem_norm_w, w_kv):
    def body(mem_ref, nw_ref, w_ref, kv_ref):
        mn = _rms(mem_ref[...], nw_ref[...]).astype(BF16)
        kv_ref[...] = jnp.dot(mn, w_ref[...], preferred_element_type=F32)

    return pl.pallas_call(
        body, name="mem_kv_fwd", out_shape=jax.ShapeDtypeStruct((MEM_LEN, 2 * C_HEADS * C_DIM), F32),
        compiler_params=_params(),
    )(mem, mem_norm_w, w_kv)


def _mem_kv_bwd(mem, mem_norm_w, w_kv, d_kv):
    def body(mem_ref, nw_ref, w_ref, g_ref, gw_ref, gn_ref):
        mn, vjp = jax.vjp(_rms, mem_ref[...], nw_ref[...])
        g = g_ref[...].astype(BF16)
        gw_ref[...] = lax.dot_general(mn.astype(BF16), g, (((0,), (0,)), ((), ())), preferred_element_type=F32)
        d_mn = lax.dot_general(g, w_ref[...], (((1,), (1,)), ((), ())), preferred_element_type=F32)
        gn_ref[...] = vjp(d_mn)[1]

    return pl.pallas_call(
        body, name="mem_kv_bwd",
        out_shape=[jax.ShapeDtypeStruct((D_MODEL, 2 * C_HEADS * C_DIM), F32), jax.ShapeDtypeStruct((1, D_MODEL), F32)],
        compiler_params=_params(),
    )(mem, mem_norm_w, w_kv, d_kv)


def _c_tile(q_t, z_t, kvm, pc):
    width = C_HEADS * C_DIM
    outs = []
    for h in range(C_HEADS):
        q = _rms(q_t[:, h * C_DIM:(h + 1) * C_DIM], pc[0:1, :])
        k = _rms(kvm[:, h * C_DIM:(h + 1) * C_DIM], pc[1:2, :])
        v = kvm[:, width + h * C_DIM:width + (h + 1) * C_DIM]
        s = _mm_nt(q, k) * (C_DIM ** -0.5)
        p = jnp.exp(s - jnp.max(s, axis=1, keepdims=True))
        p = p / jnp.sum(p, axis=1, keepdims=True)
        outs.append(_mm(p, v))
    return jnp.concatenate(outs, axis=1) * _silu(z_t)


def _attn_c_fwd(proj, kvm, pc, mixed, tq=256):
    s = proj.shape[0]

    def body(q_ref, z_ref, kv_ref, pc_ref, mixed_ref, o_ref):
        o_ref[...] = _c_tile(q_ref[...], z_ref[...], kv_ref[...], pc_ref[...]).astype(BF16)

    return pl.pallas_call(
        body, name="attn_c_fwd", grid=(s // tq,),
        in_specs=[pl.BlockSpec((tq, 512), lambda i: (i, P_QC // 512)), pl.BlockSpec((tq, 512), lambda i: (i, P_ZC // 512)),
                  pl.BlockSpec(kvm.shape, lambda i: (0, 0)), pl.BlockSpec((8, LANE), lambda i: (0, 0)),
                  pl.BlockSpec(memory_space=pl.ANY)],
        out_specs=pl.BlockSpec((tq, 512), lambda i: (i, (A_WIDTH + 512) // 512)),
        out_shape=jax.ShapeDtypeStruct(mixed.shape, mixed.dtype), input_output_aliases={4: 0},
        compiler_params=_params(("parallel",)),
    )(proj, proj, kvm, pc, mixed)


def _attn_c_bwd(proj, kvm, pc, d_mixed, tq=256):
    s = proj.shape[0]

    def body(q_ref, z_ref, kv_ref, pc_ref, dm_ref, dq_ref, dz_ref, dkv_ref, dpc_ref):
        @pl.when(pl.program_id(0) == 0)
        def _():
            dkv_ref[...] = jnp.zeros_like(dkv_ref)
            dpc_ref[...] = jnp.zeros_like(dpc_ref)

        _, vjp = jax.vjp(_c_tile, q_ref[...], z_ref[...], kv_ref[...], pc_ref[...])
        dq, dz, dkv, dpc = vjp(dm_ref[...])
        dq_ref[...] = dq
        dz_ref[...] = dz
        dkv_ref[...] += dkv
        dpc_ref[...] += dpc

    blk = pl.BlockSpec((tq, 512), lambda i: (i, 0))
    kvs = pl.BlockSpec(kvm.shape, lambda i: (0, 0))
    small = pl.BlockSpec((8, LANE), lambda i: (0, 0))
    return pl.pallas_call(
        body, name="attn_c_bwd", grid=(s // tq,),
        in_specs=[pl.BlockSpec((tq, 512), lambda i: (i, P_QC // 512)), pl.BlockSpec((tq, 512), lambda i: (i, P_ZC // 512)),
                  kvs, small, pl.BlockSpec((tq, 512), lambda i: (i, 3))],
        out_specs=[blk, blk, kvs, small],
        out_shape=[jax.ShapeDtypeStruct((s, 512), F32), jax.ShapeDtypeStruct((s, 512), F32),
                   jax.ShapeDtypeStruct(kvm.shape, F32), jax.ShapeDtypeStruct((8, LANE), F32)],
        compiler_params=_params(("arbitrary",)),
    )(proj, proj, kvm, pc, d_mixed)


def _pad_row(v, width=LANE):
    v = v.reshape(1, -1)
    return jnp.pad(v, ((0, 0), (0, width - v.shape[1])))


def _local_step(x, mem, target, norm_w, w_perm, conv_w, pa, pb, pc, mem_norm_w, w_kv, w_out):
    s = x.shape[0]
    cos2, sin2 = _rope_tables(s)
    hn = _rms_fwd(x, norm_w)
    wide = dict(tm=1024, tn=512, tk=2048)
    proj = _matmul(hn, w_perm, "nn", F32, "mm_proj", **wide)
    cqkv = _conv_fwd(proj, conv_w)
    mixed = _delta_fwd(cqkv, proj, pa)
    mixed = _attn_b_fwd(proj, cos2, sin2, pb, mixed)
    kvm = _mem_kv_fwd(mem, mem_norm_w, w_kv)
    mixed = _attn_c_fwd(proj, kvm, pc, mixed)
    mo = _matmul(mixed, w_out, "nn", F32, "mm_out", **wide)
    dy, dyb, loss_parts = _loss_dy(x, mo, target)

    d_mixed = _matmul(dyb, w_out, "nt", F32, "mm_dmixed", **wide)
    g_w_out = _matmul(mixed, dyb, "tn", F32, "mm_gwout", **wide)
    d_qc, d_zc, d_kvm, d_pc = _attn_c_bwd(proj, kvm, pc, d_mixed)
    g_w_kv, g_mem_norm = _mem_kv_bwd(mem, mem_norm_w, w_kv, d_kvm)
    d_qb, d_zb, d_kb, d_vb, d_pb = _attn_b_bwd(proj, cos2, sin2, pb, d_mixed)
    d_c, d_za, d_gt, d_pa = _delta_bwd(cqkv, proj, pa, d_mixed)
    d_qkv, g_conv = _conv_bwd(proj, conv_w, d_c)
    d_proj = jnp.concatenate([d_qkv, d_za, d_qb, d_zb, d_qc, d_zc, d_kb, d_vb, d_gt,
                              jnp.zeros((s, P_WIDTH - P_GT - LANE), F32)], axis=1).astype(BF16)
    d_hn = _matmul(d_proj, w_perm, "nt", F32, "mm_dhn", tm=1024, tn=1024, tk=512)
    g_w_perm = _matmul(hn, d_proj, "tn", F32, "mm_gwin", **wide)
    g_x, g_norm = _rms_bwd(x, norm_w, d_hn, dy)
    return dict(loss_parts=loss_parts, g_x=g_x, g_norm=g_norm, g_w_perm=g_w_perm, g_conv=g_conv, d_pa=d_pa,
                d_pb=d_pb, d_pc=d_pc, g_mem_norm=g_mem_norm, g_w_kv=g_w_kv, g_w_out=g_w_out)


def _permute_cols(w):
    pad = jnp.zeros((w.shape[0], P_WIDTH - IN_WIDTH), w.dtype)
    return jnp.concatenate([w[:, :O_GT], w[:, O_QB:O_KB], w[:, O_ZB:O_QC], w[:, O_QC:O_ZC], w[:, O_ZC:IN_WIDTH],
                            w[:, O_KB:O_VB], w[:, O_VB:O_ZB], w[:, O_GT:O_QB], pad], axis=1)


def _unpermute_cols(g):
    return jnp.concatenate([g[:, :P_QB], g[:, P_GT:P_GT + 32], g[:, P_QB:P_ZB], g[:, P_KB:P_VB], g[:, P_VB:P_GT],
                            g[:, P_ZB:P_QC], g[:, P_QC:P_ZC], g[:, P_ZC:P_KB]], axis=1)


HBM = pl.BlockSpec(memory_space=pltpu.HBM)


def _place():
    x, y, c = lax.axis_index("x"), lax.axis_index("y"), lax.axis_index("c")
    chips = [(1 - x, y), (x, 1 - y), (1 - x, 1 - y)]
    return x, y, c, 2 * x + y, chips, [2 * cx + cy for cx, cy in chips]


COPY_CHUNK_ROWS = 128
COPY_MAX_CHUNKS = 8
PAIR_PIECE_ROWS = 256


class _Copies:
    def __init__(self, make, src, dst):
        rows = src.shape[-2]
        n = max(1, min(COPY_MAX_CHUNKS, rows // COPY_CHUNK_ROWS))
        assert rows % n == 0
        step = rows // n
        lead = (slice(None),) * (len(src.shape) - 2)
        self.whole = make(src, dst)
        self.parts = [self.whole] if n == 1 else [
            make(src.at[lead + (pl.ds(i * step, step), slice(None))], dst.at[lead + (pl.ds(i * step, step), slice(None))])
            for i in range(n)]

    def start(self):
        for p in self.parts:
            p.start()

    def wait(self):
        self.whole.wait()

    def wait_send(self):
        self.whole.wait_send()

    def wait_recv(self):
        self.whole.wait_recv()


def _remote(src, dst, send_sems, recv_sems, k, to):
    def make(s, d):
        return pltpu.make_async_remote_copy(src_ref=s, dst_ref=d, send_sem=send_sems.at[k], recv_sem=recv_sems.at[k],
                                            device_id=to, device_id_type=MESH)
    return _Copies(make, src, dst)


def _local(src, dst, sem):
    return _Copies(lambda s, d: pltpu.make_async_copy(s, d, sem), src, dst)


def _half_rows(ref, c):
    half = ref.shape[-2] // 2
    return pl.ds(pl.multiple_of(c * half, 8), half)


def _all_gather_weights(w_in_b, w_out_b, w_kv_b, conv_b):
    bigs = (w_in_b, w_out_b, w_kv_b)
    n_big = len(bigs)

    def body(*refs):
        srcs, conv_src = refs[:n_big], refs[n_big]
        dsts, conv_dst = refs[n_big + 1:2 * n_big + 1], refs[2 * n_big + 1]
        send_sems, recv_sems, local_sems = refs[2 * n_big + 2:]
        x, y, c, me, chips, chip_ids = _place()
        sibling = (x, y, 1 - c)
        local = [_local(src, dst.at[me], local_sems.at[a]) for a, (src, dst) in enumerate(zip(srcs, dsts))]
        local.append(_local(conv_src, conv_dst.at[me], local_sems.at[n_big]))
        for cp in local:
            cp.start()
        sends = []
        for a, (src, dst) in enumerate(zip(srcs, dsts)):
            mine = _half_rows(src, c)
            for j, chip in enumerate(chips):
                sends.append(_remote(src.at[mine, :], dst.at[me, mine, :], send_sems, recv_sems, 6 * a + j, (*chip, c)))
        for j, chip in enumerate(chips):
            sends.append(_remote(conv_src, conv_dst.at[me], send_sems, recv_sems, 6 * n_big + j, (*chip, c)))
        for cp in sends:
            cp.start()
        passed = []
        for a, (src, dst) in enumerate(zip(srcs, dsts)):
            mine = _half_rows(src, c)
            for j, cid in enumerate(chip_ids):
                landed = dst.at[cid, mine, :]
                _remote(landed, landed, send_sems, recv_sems, 6 * a + j, sibling).wait_recv()
                cp = _remote(landed, landed, send_sems, recv_sems, 6 * a + 3 + j, sibling)
                cp.start()
                passed.append(cp)
        for a, (src, dst) in enumerate(zip(srcs, dsts)):
            other = _half_rows(src, 1 - c)
            for j, cid in enumerate(chip_ids):
                landed = dst.at[cid, other, :]
                _remote(landed, landed, send_sems, recv_sems, 6 * a + 3 + j, sibling).wait_recv()
        for j, cid in enumerate(chip_ids):
            _remote(conv_src, conv_dst.at[cid], send_sems, recv_sems, 6 * n_big + j, sibling).wait_recv()
        for cp in sends + passed:
            cp.wait_send()
        for cp in local:
            cp.wait()

    n_sem = 6 * n_big + 3
    return pl.pallas_call(
        body, name="all_gather_weights",
        out_shape=[jax.ShapeDtypeStruct((N_CHIPS,) + w.shape, w.dtype) for w in bigs + (conv_b,)],
        in_specs=[pl.BlockSpec(memory_space=pltpu.VMEM)] * (n_big + 1), out_specs=[HBM] * (n_big + 1),
        scratch_shapes=[pltpu.SemaphoreType.DMA((n_sem,)), pltpu.SemaphoreType.DMA((n_sem,)),
                        pltpu.SemaphoreType.DMA((n_big + 1,))],
        compiler_params=_params(),
    )(*bigs, conv_b)


def _pair_exchange(grads):
    n = len(grads)
    piece = PAIR_PIECE_ROWS

    def body(*refs):
        srcs, gots = refs[:n], refs[n:2 * n]
        stages = refs[2 * n:3 * n]
        send_sems, recv_sems, load_sems = refs[3 * n:]
        x, y, c, _, _, _ = _place()
        sibling = (x, y, 1 - c)
        for a in range(n):
            slabs, half, _ = gots[a].shape
            per_slab = half // piece
            first = (1 - c) * half
            loads, sends = [], []
            for i in range(slabs * per_slab):
                k, r, slot = i // per_slab, i % per_slab, i % 2
                rows = pl.ds(pl.multiple_of(first + r * piece, 8), piece)
                loads.append(pltpu.make_async_copy(srcs[a].at[k, rows, :], stages[a].at[slot], load_sems.at[2 * a + slot]))
                sends.append(pltpu.make_async_remote_copy(
                    src_ref=stages[a].at[slot], dst_ref=gots[a].at[k, pl.ds(r * piece, piece), :],
                    send_sem=send_sems.at[2 * a + slot], recv_sem=recv_sems.at[a], device_id=sibling, device_id_type=MESH))
            loads[0].start()
            for i in range(len(loads)):
                loads[i].wait()
                sends[i].start()
                if i + 1 < len(loads):
                    if i >= 1:
                        sends[i - 1].wait_send()
                    loads[i + 1].start()
            for cp in sends[-2:]:
                cp.wait_send()
        for a in range(n):
            whole = srcs[a].at[:, _half_rows(srcs[a], c), :]
            pltpu.make_async_remote_copy(src_ref=whole, dst_ref=gots[a], send_sem=send_sems.at[2 * a],
                                         recv_sem=recv_sems.at[a], device_id=sibling, device_id_type=MESH).wait_recv()

    halves = [jax.ShapeDtypeStruct((g.shape[0], g.shape[1] // 2, g.shape[2]), g.dtype) for g in grads]
    assert all(h.shape[1] % piece == 0 and (h.shape[0] * h.shape[1] // piece) >= 2 for h in halves)
    return pl.pallas_call(
        body, name="grad_pair_exchange", out_shape=halves, in_specs=[HBM] * n, out_specs=[HBM] * n,
        scratch_shapes=[pltpu.VMEM((2, piece, g.shape[2]), g.dtype) for g in grads]
        + [pltpu.SemaphoreType.DMA((2 * n,)), pltpu.SemaphoreType.DMA((n,)), pltpu.SemaphoreType.DMA((2 * n,))],
        compiler_params=_params(),
    )(*grads)


def _chip_exchange(halves):
    n = len(halves)

    def body(*refs):
        srcs, lands = refs[:n], refs[n:2 * n]
        send_sems, recv_sems = refs[2 * n:]
        x, y, c, me, chips, chip_ids = _place()
        gives = []
        for a in range(n):
            for j, (chip, cid) in enumerate(zip(chips, chip_ids)):
                give = _remote(srcs[a].at[cid], lands[a].at[j], send_sems, recv_sems, 3 * a + j, (*chip, c))
                give.start()
                gives.append(give)
        for a in range(n):
            for j, cid in enumerate(chip_ids):
                _remote(srcs[a].at[cid], lands[a].at[j], send_sems, recv_sems, 3 * a + j, (x, y, c)).wait_recv()
        for give in gives:
            give.wait_send()

    return pl.pallas_call(
        body, name="grad_chip_exchange",
        out_shape=[jax.ShapeDtypeStruct((N_CHIPS - 1,) + h.shape[1:], h.dtype) for h in halves],
        in_specs=[HBM] * n, out_specs=[HBM] * n,
        scratch_shapes=[pltpu.SemaphoreType.DMA((3 * n,)), pltpu.SemaphoreType.DMA((3 * n,))],
    )(*halves)


def _pair_gather(halves):
    n = len(halves)

    def body(*refs):
        srcs, fulls = refs[:n], refs[n:2 * n]
        send_sems, recv_sems, local_sems = refs[2 * n:]
        x, y, c, _, _, _ = _place()
        copies = []
        for a in range(n):
            mine = _half_rows(fulls[a], c)
            keep = _local(srcs[a], fulls[a].at[mine, :], local_sems.at[a])
            keep.start()
            give = _remote(srcs[a], fulls[a].at[mine, :], send_sems, recv_sems, a, (x, y, 1 - c))
            give.start()
            copies += [keep, give]
        for a in range(n):
            other = _half_rows(fulls[a], 1 - c)
            copies[2 * a].wait()
            copies[2 * a + 1].wait_send()
            _remote(srcs[a], fulls[a].at[other, :], send_sems, recv_sems, a, (x, y, 1 - c)).wait_recv()

    return pl.pallas_call(
        body, name="grad_pair_gather",
        out_shape=[jax.ShapeDtypeStruct((2 * h.shape[0], h.shape[1]), h.dtype) for h in halves],
        in_specs=[pl.BlockSpec(memory_space=pltpu.VMEM)] * n, out_specs=[HBM] * n,
        scratch_shapes=[pltpu.SemaphoreType.DMA((n,)), pltpu.SemaphoreType.DMA((n,)), pltpu.SemaphoreType.DMA((n,))],
    )(*halves)


def _all_reduce_small(p):
    n_dev = 8

    def body(p_ref, o_ref, land, send_sems, recv_sems):
        x, y, c = lax.axis_index("x"), lax.axis_index("y"), lax.axis_index("c")
        me = 4 * x + 2 * y + c
        land[me] = p_ref[...]
        sends = []
        for k in range(1, n_dev):
            fx, fy, fc = (k >> 2) & 1, (k >> 1) & 1, k & 1
            to = (x ^ fx, y ^ fy, c ^ fc)
            cp = _remote(p_ref, land.at[me], send_sems, recv_sems, k - 1, to)
            cp.start()
            sends.append(cp)
        for k in range(1, n_dev):
            _remote(p_ref, land.at[me ^ k], send_sems, recv_sems, k - 1, (x, y, c)).wait_recv()
        total = land[0]
        for d in range(1, n_dev):
            total = total + land[d]
        o_ref[...] = total
        for cp in sends:
            cp.wait_send()

    vm = pl.BlockSpec(memory_space=pltpu.VMEM)
    return pl.pallas_call(
        body, name="all_reduce_small", out_shape=jax.ShapeDtypeStruct(p.shape, p.dtype), in_specs=[vm], out_specs=vm,
        scratch_shapes=[pltpu.VMEM((n_dev,) + p.shape, p.dtype), pltpu.SemaphoreType.DMA((n_dev - 1,)),
                        pltpu.SemaphoreType.DMA((n_dev - 1,))],
    )(p)


def _row_tile(rows, cap=256):
    return cap if rows % cap == 0 else rows


def _pair_sum(full, got, core, name):
    n, r, c = got.shape
    tr = _row_tile(r)
    nt = r // tr

    def body(core_ref, a_ref, b_ref, o_ref):
        o_ref[...] = (a_ref[...] + b_ref[...]).astype(BF16)

    blk = pl.BlockSpec((None, tr, c), lambda i, j, core_ref: (i, j, 0))
    grid_spec = pltpu.PrefetchScalarGridSpec(
        num_scalar_prefetch=1, grid=(n, nt),
        in_specs=[pl.BlockSpec((None, tr, c), lambda i, j, core_ref: (i, core_ref[0] * nt + j, 0)), blk], out_specs=blk)
    return pl.pallas_call(body, name=name, grid_spec=grid_spec, out_shape=jax.ShapeDtypeStruct(got.shape, BF16),
                          compiler_params=_params(("parallel", "parallel")))(core, full, got)


def _chip_sum(full, got, land, place, name):
    n, r, c = land.shape
    tr = _row_tile(r)
    nt = r // tr

    def body(place_ref, a_ref, b_ref, l_ref, o_ref):
        total = a_ref[...] + b_ref[...]
        for j in range(n):
            total = total + l_ref[j].astype(F32)
        o_ref[...] = total

    grid_spec = pltpu.PrefetchScalarGridSpec(
        num_scalar_prefetch=1, grid=(nt,),
        in_specs=[pl.BlockSpec((None, tr, c), lambda i, p: (p[0], p[1] * nt + i, 0)),
                  pl.BlockSpec((None, tr, c), lambda i, p: (p[0], i, 0)),
                  pl.BlockSpec((n, tr, c), lambda i, p: (0, i, 0))],
        out_specs=pl.BlockSpec((tr, c), lambda i, p: (i, 0)))
    return pl.pallas_call(body, name=name, grid_spec=grid_spec, out_shape=jax.ShapeDtypeStruct((r, c), F32),
                          compiler_params=_params(("parallel",)))(place, full, got, land)


def _adamw(w, g, m, v, name):
    r, c = w.shape
    tr = _row_tile(r)

    def body(w_ref, g_ref, m_ref, v_ref, d_ref, mo_ref, vo_ref):
        g_ = g_ref[...]
        m2 = ADAM_B1 * m_ref[...] + (1.0 - ADAM_B1) * g_
        v2 = ADAM_B2 * v_ref[...] + (1.0 - ADAM_B2) * jnp.square(g_)
        m_hat = m2 / (1.0 - ADAM_B1 ** ADAM_STEP)
        v_hat = v2 / (1.0 - ADAM_B2 ** ADAM_STEP)
        d_ref[...] = -ADAM_LR * (m_hat / (jnp.sqrt(v_hat) + ADAM_EPS) + ADAM_WD * w_ref[...])
        mo_ref[...] = m2
        vo_ref[...] = v2

    blk = pl.BlockSpec((tr, c), lambda i: (i, 0))
    return pl.pallas_call(body, name=name, grid=(r // tr,), in_specs=[blk] * 4, out_specs=[blk] * 3,
                          out_shape=[jax.ShapeDtypeStruct(w.shape, F32)] * 3, compiler_params=_params(("parallel",)))(w, g, m, v)


SMALL_NAMES = ("norm_w", "mem_norm_w", "o_norm_a", "q_norm_c", "k_norm_c", "q_norm_b", "k_norm_b",
               "a_log_fwd", "a_log_bwd", "dt_bias_fwd", "dt_bias_bwd", "sink_b")
SMALL_SIZES = (2048, 2048, 128, 128, 128, 64, 64, 8, 8, 8, 8, 8)
SMALL_LOSS = sum(SMALL_SIZES)
SMALL_CONV = 5120
SMALL_TOTAL = SMALL_CONV + CONV_K * 3 * A_WIDTH
SMALL_ROWS = SMALL_TOTAL // LANE


def _pack_small(parts, extra=None, conv=None):
    vec = [parts[n].reshape(-1) for n in SMALL_NAMES]
    vec.append(jnp.zeros((1,), F32) if extra is None else extra.reshape(1))
    vec.append(jnp.zeros((SMALL_CONV - SMALL_LOSS - 1,), F32))
    vec.append(jnp.zeros((SMALL_TOTAL - SMALL_CONV,), F32) if conv is None else conv.reshape(-1))
    return jnp.concatenate(vec).reshape(SMALL_ROWS, LANE)


def _unpack_small(packed):
    flat = packed.reshape(-1)
    out, off = {}, 0
    for n, size in zip(SMALL_NAMES, SMALL_SIZES):
        out[n] = flat[off:off + size].reshape(1, size)
        off += size
    return out


WEIGHT_ORDER = ("norm_w", "w_in", "conv_w_a", "a_log_fwd", "a_log_bwd", "dt_bias_fwd", "dt_bias_bwd", "o_norm_a",
                "q_norm_b", "k_norm_b", "sink_b", "mem_norm_w", "w_mem_kv", "q_norm_c", "k_norm_c", "w_out")


def kernel(x, mem, norm_w, w_in, conv_w_a, a_log_fwd, a_log_bwd, dt_bias_fwd, dt_bias_bwd, o_norm_a, q_norm_b, k_norm_b, sink_b, mem_norm_w, w_mem_kv, q_norm_c, k_norm_c, w_out, loss_target, m_norm_w, m_w_in, m_conv_w_a, m_a_log_fwd, m_a_log_bwd, m_dt_bias_fwd, m_dt_bias_bwd, m_o_norm_a, m_q_norm_b, m_k_norm_b, m_sink_b, m_mem_norm_w, m_w_mem_kv, m_q_norm_c, m_k_norm_c, m_w_out, v_norm_w, v_w_in, v_conv_w_a, v_a_log_fwd, v_a_log_bwd, v_dt_bias_fwd, v_dt_bias_bwd, v_o_norm_a, v_q_norm_b, v_k_norm_b, v_sink_b, v_mem_norm_w, v_w_mem_kv, v_q_norm_c, v_k_norm_c, v_w_out):
    weights = dict(norm_w=norm_w, w_in=w_in, conv_w_a=conv_w_a, a_log_fwd=a_log_fwd, a_log_bwd=a_log_bwd,
                   dt_bias_fwd=dt_bias_fwd, dt_bias_bwd=dt_bias_bwd, o_norm_a=o_norm_a, q_norm_b=q_norm_b,
                   k_norm_b=k_norm_b, sink_b=sink_b, mem_norm_w=mem_norm_w, w_mem_kv=w_mem_kv, q_norm_c=q_norm_c,
                   k_norm_c=k_norm_c, w_out=w_out)
    mom1 = dict(norm_w=m_norm_w, w_in=m_w_in, conv_w_a=m_conv_w_a, a_log_fwd=m_a_log_fwd, a_log_bwd=m_a_log_bwd,
                dt_bias_fwd=m_dt_bias_fwd, dt_bias_bwd=m_dt_bias_bwd, o_norm_a=m_o_norm_a, q_norm_b=m_q_norm_b,
                k_norm_b=m_k_norm_b, sink_b=m_sink_b, mem_norm_w=m_mem_norm_w, w_mem_kv=m_w_mem_kv,
                q_norm_c=m_q_norm_c, k_norm_c=m_k_norm_c, w_out=m_w_out)
    mom2 = dict(norm_w=v_norm_w, w_in=v_w_in, conv_w_a=v_conv_w_a, a_log_fwd=v_a_log_fwd, a_log_bwd=v_a_log_bwd,
                dt_bias_fwd=v_dt_bias_fwd, dt_bias_bwd=v_dt_bias_bwd, o_norm_a=v_o_norm_a, q_norm_b=v_q_norm_b,
                k_norm_b=v_k_norm_b, sink_b=v_sink_b, mem_norm_w=v_mem_norm_w, w_mem_kv=v_w_mem_kv,
                q_norm_c=v_q_norm_c, k_norm_c=v_k_norm_c, w_out=v_w_out)
    chip = 2 * lax.axis_index("x") + lax.axis_index("y")

    w_in4, w_out4, w_kv4, conv4 = _all_gather_weights(w_in[0].astype(BF16), w_out[0].astype(BF16),
                                                      w_mem_kv[0].astype(BF16), conv_w_a[0])
    w_perm = _permute_cols(jnp.transpose(w_in4, (1, 0, 2)).reshape(D_MODEL, IN_WIDTH))
    w_out_full = w_out4.reshape(D_MODEL, D_MODEL)
    w_kv_full = w_kv4.reshape(D_MODEL, 2 * C_HEADS * C_DIM)
    conv_full = jnp.transpose(conv4, (1, 0, 2)).reshape(CONV_K, 3 * A_WIDTH)
    pa = jnp.concatenate([_pad_row(a_log_fwd), _pad_row(a_log_bwd), _pad_row(dt_bias_fwd), _pad_row(dt_bias_bwd),
                          _pad_row(o_norm_a), jnp.zeros((3, LANE), F32)], axis=0)
    pb = jnp.concatenate([_pad_row(q_norm_b), _pad_row(k_norm_b), _pad_row(sink_b), jnp.zeros((5, LANE), F32)], axis=0)
    pc = jnp.concatenate([_pad_row(q_norm_c), _pad_row(k_norm_c), jnp.zeros((6, LANE), F32)], axis=0)

    r = _local_step(x[0], mem[0], loss_target[0], norm_w, w_perm, conv_full, pa, pb, pc, mem_norm_w, w_kv_full,
                    w_out_full)

    g_in4 = jnp.transpose(_unpermute_cols(r["g_w_perm"]).reshape(D_MODEL, N_CHIPS, W_IN_BLOCK), (1, 0, 2))
    g_out4 = r["g_w_out"].reshape(N_CHIPS, D_MODEL // N_CHIPS, D_MODEL)
    g_kv4 = r["g_w_kv"].reshape(N_CHIPS, D_MODEL // N_CHIPS, 2 * C_HEADS * C_DIM)
    full = [g_in4, g_out4, g_kv4]
    core = lax.axis_index("c").astype(jnp.int32).reshape(1)
    got = _pair_exchange(full)
    pair = [_pair_sum(a, b, core, "grad_pair_sum_%d" % i) for i, (a, b) in enumerate(zip(full, got))]
    lands = _chip_exchange(pair)
    place = jnp.stack([chip, lax.axis_index("c")]).astype(jnp.int32)
    reduced = [_chip_sum(a, b, l, place, "grad_chip_sum_%d" % i) for i, (a, b, l) in enumerate(zip(full, got, lands))]
    g_w_in, g_w_out, g_w_kv = _pair_gather(reduced)

    d_pa, d_pb, d_pc = r["d_pa"], r["d_pb"], r["d_pc"]
    small_g = dict(norm_w=r["g_norm"], mem_norm_w=r["g_mem_norm"], o_norm_a=d_pa[4], q_norm_c=d_pc[0], k_norm_c=d_pc[1],
                   q_norm_b=d_pb[0, :B_DIM], k_norm_b=d_pb[1, :B_DIM], a_log_fwd=d_pa[0, :A_HEADS],
                   a_log_bwd=d_pa[1, :A_HEADS], dt_bias_fwd=d_pa[2, :A_HEADS], dt_bias_bwd=d_pa[3, :A_HEADS],
                   sink_b=d_pb[2, :B_HEADS])
    packed = _all_reduce_small(_pack_small(small_g, jnp.sum(r["loss_parts"][:, 0, 0]), r["g_conv"]))
    flat = packed.reshape(-1)
    loss = flat[SMALL_LOSS]
    conv_sum = flat[SMALL_CONV:].reshape(CONV_K, 3 * A_WIDTH)
    conv_cols = 3 * A_WIDTH // N_CHIPS
    g_conv = lax.dynamic_slice(conv_sum, (0, chip * conv_cols), (CONV_K, conv_cols))

    grads = _unpack_small(packed)
    grads.update(w_in=g_w_in, w_mem_kv=g_w_kv, w_out=g_w_out, conv_w_a=g_conv)
    delta, new_m, new_v = {}, {}, {}
    for n in ("w_in", "w_mem_kv", "w_out", "conv_w_a"):
        delta[n], new_m[n], new_v[n] = _adamw(weights[n][0], grads[n], mom1[n][0], mom2[n][0], "adamw_" + n)
    d_s, m_s, v_s = _adamw(_pack_small(weights), packed, _pack_small(mom1), _pack_small(mom2), "adamw_small")
    d_s, m_s, v_s = _unpack_small(d_s), _unpack_small(m_s), _unpack_small(v_s)
    for n in SMALL_NAMES:
        delta[n], new_m[n], new_v[n] = d_s[n], m_s[n], v_s[n]

    def shaped(tree):
        return [tree[n].reshape(weights[n].shape) for n in WEIGHT_ORDER]

    return (loss, r["g_x"].reshape(x.shape), *shaped(grads), *shaped(delta), *shaped(new_m), *shaped(new_v))
```

```python
import functools

import jax
import jax.numpy as jnp
from jax import lax
from jax.experimental import pallas as pl
from jax.experimental.pallas import tpu as pltpu

F32 = jnp.float32
BF16 = jnp.bfloat16
HI = lax.Precision.HIGHEST
MESH = pl.DeviceIdType.MESH

D_MODEL = 2048
A_WIDTH = 1024
A_HEADS = 8
A_DIM = 128
CONV_K = 5
CHUNK = 64
B_HEADS = 8
B_KV = 2
B_DIM = 64
WINDOW = 128
C_HEADS = 4
C_DIM = 128
MEM_LEN = 256
ROPE_THETA = 10000.0
EPS = 1e-6
IN_WIDTH = 6432
N_CHIPS = 4
W_IN_BLOCK = IN_WIDTH // N_CHIPS

LANE = 128
P_QA, P_KA, P_VA, P_ZA = 0, 1024, 2048, 3072
P_QB, P_ZB, P_QC, P_ZC = 4096, 4608, 5120, 5632
P_KB, P_VB, P_GT = 6144, 6272, 6400
P_WIDTH = 6656
O_GT, O_QB, O_KB, O_VB, O_ZB, O_QC, O_ZC = 4096, 4128, 4640, 4768, 4896, 5408, 5920

ADAM_LR, ADAM_B1, ADAM_B2, ADAM_EPS, ADAM_WD, ADAM_STEP = 0.001, 0.9, 0.999, 1e-08, 0.01, 10

VMEM_LIMIT = 56 * 1024 * 1024


def _params(sem=None):
    return pltpu.CompilerParams(dimension_semantics=sem, vmem_limit_bytes=VMEM_LIMIT)


def _dot(a, b, dims=(((1,), (0,)), ((), ())), precision=HI):
    return lax.dot_general(a, b, dims, precision=precision, preferred_element_type=F32)


def _dot_nt(a, b, precision=HI):
    return _dot(a, b, (((1,), (1,)), ((), ())), precision)


def _dot_tn(a, b, precision=HI):
    return _dot(a, b, (((0,), (0,)), ((), ())), precision)


_NN = (((1,), (0,)), ((), ()))
_NT = (((1,), (1,)), ((), ()))
_TN = (((0,), (0,)), ((), ()))


def _bdot(a, b, dims):
    return lax.dot_general(a.astype(BF16), b.astype(BF16), dims, preferred_element_type=F32)


@jax.custom_vjp
def _mm(a, b):
    return _bdot(a, b, _NN)


_mm.defvjp(lambda a, b: (_bdot(a, b, _NN), (a, b)),
           lambda res, ct: (_bdot(ct, res[1], _NT), _bdot(res[0], ct, _TN)))


@jax.custom_vjp
def _mm_nt(a, b):
    return _bdot(a, b, _NT)


_mm_nt.defvjp(lambda a, b: (_bdot(a, b, _NT), (a, b)),
              lambda res, ct: (_bdot(ct, res[1], _NN), _bdot(ct, res[0], _TN)))


@jax.custom_vjp
def _mm_tn(a, b):
    return _bdot(a, b, _TN)


_mm_tn.defvjp(lambda a, b: (_bdot(a, b, _TN), (a, b)),
              lambda res, ct: (_bdot(res[1], ct, _NT), _bdot(res[0], ct, _NN)))


def _rms(t, w):
    return t * lax.rsqrt(jnp.mean(t * t, axis=-1, keepdims=True) + EPS) * w


def _l2(t):
    return t * lax.rsqrt(jnp.sum(t * t, axis=-1, keepdims=True) + EPS)


def _silu(t):
    return t * jax.nn.sigmoid(t)


def _softplus(t):
    return jnp.maximum(t, 0.0) + jnp.log1p(jnp.exp(-jnp.abs(t)))


def _matmul(a, b, mode, out_dtype, name, tm=512, tn=512, tk=512):
    (m, k) = a.shape[::-1] if mode == "tn" else a.shape
    n = b.shape[0] if mode == "nt" else b.shape[1]
    tm, tn, tk = min(tm, m), min(tn, n), min(tk, k)
    assert m % tm == 0 and n % tn == 0 and k % tk == 0, (m, n, k, tm, tn, tk)
    if mode == "nn":
        a_spec = pl.BlockSpec((tm, tk), lambda i, j, kk: (i, kk))
        b_spec = pl.BlockSpec((tk, tn), lambda i, j, kk: (kk, j))
        dims = (((1,), (0,)), ((), ()))
    elif mode == "nt":
        a_spec = pl.BlockSpec((tm, tk), lambda i, j, kk: (i, kk))
        b_spec = pl.BlockSpec((tn, tk), lambda i, j, kk: (j, kk))
        dims = (((1,), (1,)), ((), ()))
    else:
        a_spec = pl.BlockSpec((tk, tm), lambda i, j, kk: (kk, i))
        b_spec = pl.BlockSpec((tk, tn), lambda i, j, kk: (kk, j))
        dims = (((0,), (0,)), ((), ()))
    nk = k // tk

    def body_one(a_ref, b_ref, o_ref):
        o_ref[...] = _bdot(a_ref[...], b_ref[...], dims).astype(out_dtype)

    def body_acc(a_ref, b_ref, o_ref, acc_ref):
        kk = pl.program_id(2)

        @pl.when(kk == 0)
        def _():
            acc_ref[...] = jnp.zeros_like(acc_ref)

        acc_ref[...] += _bdot(a_ref[...], b_ref[...], dims)

        @pl.when(kk == nk - 1)
        def _():
            o_ref[...] = acc_ref[...].astype(out_dtype)

    return pl.pallas_call(
        body_one if nk == 1 else body_acc, name=name, grid=(m // tm, n // tn, nk),
        in_specs=[a_spec, b_spec], out_specs=pl.BlockSpec((tm, tn), lambda i, j, kk: (i, j)),
        out_shape=jax.ShapeDtypeStruct((m, n), out_dtype),
        scratch_shapes=[] if nk == 1 else [pltpu.VMEM((tm, tn), F32)],
        compiler_params=_params(("parallel", "parallel", "arbitrary")),
    )(a, b)


def _rms_fwd(x, w, tr=256):
    s, d = x.shape

    def body(x_ref, w_ref, o_ref):
        o_ref[...] = _rms(x_ref[...], w_ref[...]).astype(BF16)

    return pl.pallas_call(
        body, name="rms_fwd", grid=(s // tr,),
        in_specs=[pl.BlockSpec((tr, d), lambda i: (i, 0)), pl.BlockSpec((1, d), lambda i: (0, 0))],
        out_specs=pl.BlockSpec((tr, d), lambda i: (i, 0)),
        out_shape=jax.ShapeDtypeStruct((s, d), BF16), compiler_params=_params(("parallel",)),
    )(x, w)


def _rms_bwd(x, w, d_hn, dy, tr=256):
    s, d = x.shape

    def body(x_ref, w_ref, g_ref, dy_ref, gx_ref, gw_ref):
        _, vjp = jax.vjp(_rms, x_ref[...], w_ref[...])
        dx, dw = vjp(g_ref[...])
        gx_ref[...] = dy_ref[...] + dx

        @pl.when(pl.program_id(0) == 0)
        def _():
            gw_ref[...] = jnp.zeros_like(gw_ref)

        gw_ref[...] += dw

    row = pl.BlockSpec((tr, d), lambda i: (i, 0))
    vec = pl.BlockSpec((1, d), lambda i: (0, 0))
    return pl.pallas_call(
        body, name="rms_bwd", grid=(s // tr,), in_specs=[row, vec, row, row], out_specs=[row, vec],
        out_shape=[jax.ShapeDtypeStruct((s, d), F32), jax.ShapeDtypeStruct((1, d), F32)],
        compiler_params=_params(("arbitrary",)),
    )(x, w, d_hn, dy)


def _loss_dy(x, mo, target, tr=256):
    s, d = x.shape
    nt = s // tr

    def body(x_ref, mo_ref, t_ref, dy_ref, dyb_ref, l_ref):
        err = x_ref[...] + mo_ref[...] - t_ref[...]
        dy = err * (1.0 / d)
        dy_ref[...] = dy
        dyb_ref[...] = dy.astype(BF16)
        l_ref[...] = jnp.full(l_ref.shape, 0.5 * jnp.sum(jnp.sum(err * err, axis=1, keepdims=True) * (1.0 / d)), F32)

    row = pl.BlockSpec((tr, d), lambda i: (i, 0))
    return pl.pallas_call(
        body, name="loss_dy", grid=(nt,), in_specs=[row, row, row],
        out_specs=[row, row, pl.BlockSpec((1, 8, LANE), lambda i: (i, 0, 0))],
        out_shape=[jax.ShapeDtypeStruct((s, d), F32), jax.ShapeDtypeStruct((s, d), BF16),
                   jax.ShapeDtypeStruct((nt, 8, LANE), F32)],
        compiler_params=_params(("parallel",)),
    )(x, mo, target)


def _shift_rows(t, s):
    if s == 0:
        return t
    n = t.shape[0]
    rolled = pltpu.roll(t, (-s) % n, axis=0)
    idx = lax.broadcasted_iota(jnp.int32, t.shape, 0) + s
    return jnp.where((idx >= 0) & (idx < n), rolled, 0.0)


def _conv_fwd(proj, conv_w):
    s = proj.shape[0]
    nblk = 3 * A_WIDTH // LANE

    def body(x_ref, w_ref, o_ref):
        x = x_ref[...]
        acc = jnp.zeros_like(x)
        for j in range(CONV_K):
            acc = acc + w_ref[j:j + 1, :] * _shift_rows(x, j - CONV_K // 2)
        o_ref[...] = acc

    return pl.pallas_call(
        body, name="conv_fwd", grid=(nblk,),
        in_specs=[pl.BlockSpec((s, LANE), lambda i: (0, i)), pl.BlockSpec((CONV_K, LANE), lambda i: (0, i))],
        out_specs=pl.BlockSpec((None, s, LANE), lambda i: (i // A_HEADS, 0, i % A_HEADS)),
        out_shape=jax.ShapeDtypeStruct((3, s, A_WIDTH), F32), compiler_params=_params(("parallel",)),
    )(proj, conv_w)


def _conv_bwd(proj, conv_w, d_c):
    s = proj.shape[0]
    nblk = 3 * A_WIDTH // LANE

    def body(x_ref, w_ref, g_ref, dx_ref, dw_ref):
        x, g = x_ref[...], g_ref[...]
        acc = jnp.zeros_like(x)
        for j in range(CONV_K):
            off = j - CONV_K // 2
            acc = acc + w_ref[j:j + 1, :] * _shift_rows(g, -off)
            dw_ref[j:j + 1, :] = jnp.sum(_shift_rows(x, off) * g, axis=0, keepdims=True)
        dx_ref[...] = acc

    col = pl.BlockSpec((s, LANE), lambda i: (0, i))
    wsp = pl.BlockSpec((CONV_K, LANE), lambda i: (0, i))
    dsp = pl.BlockSpec((None, s, LANE), lambda i: (i // A_HEADS, 0, i % A_HEADS))
    return pl.pallas_call(
        body, name="conv_bwd", grid=(nblk,), in_specs=[col, wsp, dsp], out_specs=[col, wsp],
        out_shape=[jax.ShapeDtypeStruct((s, 3 * A_WIDTH), F32), jax.ShapeDtypeStruct((CONV_K, 3 * A_WIDTH), F32)],
        compiler_params=_params(("parallel",)),
    )(proj, conv_w, d_c)


A_STEP_HEADS = 2
A_CHAINS = 2 * A_STEP_HEADS


def _a_chain(st, cq, ck, cv, alpha, beta_raw, a_log, dt_b, incl, strict, last):
    c = CHUNK
    eye = (lax.broadcasted_iota(jnp.int32, (c, c), 0) == lax.broadcasted_iota(jnp.int32, (c, c), 1)).astype(F32)
    gb = -jnp.exp(a_log) * _softplus(alpha + dt_b)
    bb = jax.nn.sigmoid(beta_raw)
    q = _l2(_silu(cq)) * (A_DIM ** -0.5)
    k = _l2(_silu(ck))
    v = _silu(cv)

    gc = _dot(incl, jnp.broadcast_to(gb, (c, LANE)))
    tot = jnp.sum(gc * last, axis=0, keepdims=True)
    m1 = gc[:, :c]
    decay = incl * jnp.exp(incl * (m1 - m1.T))
    kb = k * bb
    vb = v * bb
    a = -(strict * decay * _mm_nt(kb, k))
    tinv = eye + a
    p = a
    for _ in range(5):
        p = _mm(p, p)
        tinv = tinv + _mm(tinv, p)
    eg = jnp.exp(gc)
    u = _mm(tinv, vb)
    w = _mm(tinv, kb * eg)
    qk = _mm_nt(q, k) * decay
    v_new = u - _mm(w, st)
    o = _mm(q * eg, st) + _mm(qk, v_new)
    st_new = st * jnp.exp(tot) + _mm_tn(k * jnp.exp(tot - gc), v_new)
    return st_new, o


def _a_step(sts, cq, ck, cv, gts, pa, h0):
    c = CHUNK
    lane = lax.broadcasted_iota(jnp.int32, (1, LANE), 1)
    ii = lax.broadcasted_iota(jnp.int32, (c, c), 0)
    jj = lax.broadcasted_iota(jnp.int32, (c, c), 1)
    row = lax.broadcasted_iota(jnp.int32, (c, 1), 0)

    def pick(t, col):
        return jnp.sum(jnp.where(lane == col, t, 0.0), axis=1, keepdims=True)

    alpha, beta_raw, a_log, dt_b, incl, strict, last = [], [], [], [], [], [], []
    for b in range(A_CHAINS):
        h, rev = h0 + b // 2, b % 2
        alpha.append(pick(gts[b], h + 8 * rev))
        beta_raw.append(pick(gts[b], h + 16 + 8 * rev))
        a_log.append(pick(pa[rev:rev + 1, :], h))
        dt_b.append(pick(pa[2 + rev:3 + rev, :], h))
        incl.append(((ii <= jj) if rev else (ii >= jj)).astype(F32))
        strict.append(((ii < jj) if rev else (ii > jj)).astype(F32))
        last.append((row == (0 if rev else c - 1)).astype(F32))
    stack = lambda ts: jnp.concatenate([t[None] for t in ts], axis=0)
    return jax.vmap(_a_chain)(sts, cq, ck, cv, stack(alpha), stack(beta_raw), stack(a_log), stack(dt_b),
                              stack(incl), stack(strict), stack(last))


def _a_final(o, za, pa):
    outs = []
    for j in range(o.shape[1] // A_DIM):
        ln = slice(j * A_DIM, (j + 1) * A_DIM)
        outs.append(_rms(o[:, ln], pa[4:5, :]) * _silu(za[:, ln]))
    return jnp.concatenate(outs, axis=1)


def _a_tiles(n, nchunk):
    tiles = []
    for b in range(A_CHAINS):
        i = (nchunk - 1 - n) if b % 2 else n
        tiles.append((i, pl.ds(pl.multiple_of(i * CHUNK, CHUNK), CHUNK), slice((b // 2) * A_DIM, (b // 2 + 1) * A_DIM)))
    return tiles


def _a_load(tiles, c_ref, gt_ref):
    cq, ck, cv = (jnp.stack([c_ref[r, sl, ln] for _, sl, ln in tiles], axis=0) for r in range(3))
    return cq, ck, cv, jnp.stack([gt_ref[sl, :] for _, sl, _ in tiles], axis=0)


def _loop_by_two(n, step, init):
    assert n % 2 == 0
    return lax.fori_loop(0, n // 2, lambda m, carry: step(2 * m + 1, step(2 * m, carry)), init)


def _a_scan(h0, nchunk, c_ref, gt_ref, pa, of_ref, ob_ref, s_ref):
    def step(n, sts):
        tiles = _a_tiles(n, nchunk)
        sts_new, o = _a_step(sts, *_a_load(tiles, c_ref, gt_ref), pa, h0)
        for b, (i, sl, ln) in enumerate(tiles):
            if s_ref is not None:
                s_ref[b, i] = sts[b]
            (ob_ref if b % 2 else of_ref)[sl, ln] = o[b]
        return sts_new

    _loop_by_two(nchunk, step, jnp.zeros((A_CHAINS, A_DIM, A_DIM), F32))


def _a_specs(s):
    wide = A_STEP_HEADS * A_DIM
    once = pl.Buffered(1)
    trio = pl.BlockSpec((3, s, wide), lambda g: (0, 0, g), pipeline_mode=once)
    gates = pl.BlockSpec((s, LANE), lambda g: (0, P_GT // LANE))
    small = pl.BlockSpec((8, LANE), lambda g: (0, 0))

    def cols(base):
        return pl.BlockSpec((s, wide), lambda g: (0, base // wide + g), pipeline_mode=once)

    return wide, trio, gates, small, cols


def _a_state_spec(nchunk):
    return pl.BlockSpec((A_CHAINS, nchunk, A_DIM, A_DIM), lambda g: (g, 0, 0, 0), pipeline_mode=pl.Buffered(1))


def _delta_fwd(cqkv, proj, pa):
    s = cqkv.shape[1]
    nchunk = s // CHUNK
    wide, trio, gates, small, cols = _a_specs(s)

    def body(c_ref, gt_ref, za_ref, pa_ref, out_ref, o_ref, s_ref, ob_ref):
        h0 = pl.program_id(0) * A_STEP_HEADS
        pa_v = pa_ref[...]
        _a_scan(h0, nchunk, c_ref, gt_ref, pa_v, o_ref, ob_ref, s_ref)
        o_ref[...] += ob_ref[...]
        out_ref[...] = _a_final(o_ref[...], za_ref[...], pa_v).astype(BF16)

    steps = A_HEADS // A_STEP_HEADS
    return pl.pallas_call(
        body, name="delta_fwd", grid=(steps,),
        in_specs=[trio, gates, cols(P_ZA), small], out_specs=[cols(0), cols(0), _a_state_spec(nchunk)],
        out_shape=[jax.ShapeDtypeStruct((s, D_MODEL), BF16),
                   jax.ShapeDtypeStruct((s, A_WIDTH), F32),
                   jax.ShapeDtypeStruct((steps * A_CHAINS, nchunk, A_DIM, A_DIM), F32)],
        scratch_shapes=[pltpu.VMEM((s, wide), F32)], compiler_params=_params(("parallel",)),
    )(cqkv, proj, proj, pa)


def _delta_bwd(cqkv, proj, pa, d_mixed, o_sum, states):
    s = cqkv.shape[1]
    nchunk = s // CHUNK

    wide, trio, gates, small, cols = _a_specs(s)

    def body(c_ref, gt_ref, za_ref, pa_ref, dm_ref, o_ref, s_ref, dc_ref, dza_ref, dgt_ref, dpa_ref, of_ref):
        h0 = pl.program_id(0) * A_STEP_HEADS
        pa_v = pa_ref[...]

        @pl.when(h0 == 0)
        def _():
            dgt_ref[...] = jnp.zeros_like(dgt_ref)
            dpa_ref[...] = jnp.zeros_like(dpa_ref)

        _, vjp = jax.vjp(_a_final, o_ref[...], za_ref[...], pa_v)
        d_o, d_za, dpa0 = vjp(dm_ref[...])
        of_ref[...] = d_o
        dza_ref[...] = d_za
        dc_ref[...] = jnp.zeros_like(dc_ref)

        def step(n, carry):
            d_sts, dpa = carry
            tiles = _a_tiles(nchunk - 1 - n, nchunk)
            sts = jnp.stack([s_ref[b, i] for b, (i, _, _) in enumerate(tiles)], axis=0)
            d_o_t = jnp.stack([of_ref[sl, ln] for _, sl, ln in tiles], axis=0)
            _, vjp_c = jax.vjp(lambda *a: _a_step(*a, h0), sts, *_a_load(tiles, c_ref, gt_ref), pa_v)
            d_prev, dcq, dck, dcv, dgts, dpa_i = vjp_c((d_sts, d_o_t))
            for b, (_, sl, ln) in enumerate(tiles):
                for r, dc in enumerate((dcq, dck, dcv)):
                    dc_ref[r, sl, ln] += dc[b]
                dgt_ref[sl, :] += dgts[b]
            return d_prev, dpa + dpa_i

        _, dpa_out = _loop_by_two(nchunk, step, (jnp.zeros((A_CHAINS, A_DIM, A_DIM), F32), dpa0))
        dpa_ref[...] += dpa_out

    fixed = pl.BlockSpec((s, LANE), lambda g: (0, 0))
    return pl.pallas_call(
        body, name="delta_bwd", grid=(A_HEADS // A_STEP_HEADS,),
        in_specs=[trio, gates, cols(P_ZA), small, cols(0), cols(0), _a_state_spec(nchunk)],
        out_specs=[trio, cols(0), fixed, small],
        out_shape=[jax.ShapeDtypeStruct((3, s, A_WIDTH), F32), jax.ShapeDtypeStruct((s, A_WIDTH), F32),
                   jax.ShapeDtypeStruct((s, LANE), F32), jax.ShapeDtypeStruct((8, LANE), F32)],
        scratch_shapes=[pltpu.VMEM((s, wide), F32)], compiler_params=_params(("arbitrary",)),
    )(cqkv, proj, proj, pa, d_mixed, o_sum, states)


def _rope_tables(s):
    inv = ROPE_THETA ** (-jnp.arange(0, B_DIM, 2, dtype=F32) / B_DIM)
    ang = jnp.arange(s, dtype=F32)[:, None] * inv[None, :]
    cos, sin = jnp.cos(ang), jnp.sin(ang)
    return jnp.concatenate([cos, cos], axis=1), jnp.concatenate([-sin, sin], axis=1)


def _b_block(q_t, z_t, k3, v3, cos_q, sin_q, cos_k, sin_k, pb, n, nb):
    w = WINDOW
    def swap(t):
        return jnp.concatenate([t[:, B_DIM // 2:], t[:, :B_DIM // 2]], axis=1)

    grp = B_HEADS // B_KV
    qi = lax.broadcasted_iota(jnp.int32, (grp * w, 3 * w), 0) & (w - 1)
    kj = lax.broadcasted_iota(jnp.int32, (grp * w, 3 * w), 1)
    kpos = kj + (n - 1) * w
    mask = (jnp.abs(kj - w - qi) <= w) & (kpos >= 0) & (kpos < nb * w)
    lane = lax.broadcasted_iota(jnp.int32, (1, LANE), 1)
    qn, kn = pb[0:1, :B_DIM], pb[1:2, :B_DIM]
    cos_g = jnp.concatenate([cos_q] * grp, axis=0)
    sin_g = jnp.concatenate([sin_q] * grp, axis=0)
    outs = []
    for hk in range(B_KV):
        k = _rms(k3[:, hk * B_DIM:(hk + 1) * B_DIM], kn)
        k = k * cos_k + swap(k) * sin_k
        v = v3[:, hk * B_DIM:(hk + 1) * B_DIM]
        heads = [hk * grp + g for g in range(grp)]
        q = _rms(jnp.concatenate([q_t[:, hq * B_DIM:(hq + 1) * B_DIM] for hq in heads], axis=0), qn)
        q = q * cos_g + swap(q) * sin_g
        sink = jnp.concatenate(
            [jnp.broadcast_to(jnp.sum(jnp.where(lane == hq, pb[2:3, :], 0.0), axis=1, keepdims=True), (w, 1))
             for hq in heads], axis=0)
        s = _mm_nt(q, k) * (B_DIM ** -0.5)
        s = jnp.where(mask, s, -jnp.inf)
        m = jnp.maximum(jnp.max(s, axis=1, keepdims=True), sink)
        p = jnp.exp(s - m)
        p = p / (jnp.sum(p, axis=1, keepdims=True) + jnp.exp(sink - m))
        o = _mm(p, v)
        outs += [o[g * w:(g + 1) * w, :] for g in range(grp)]
    return jnp.concatenate(outs, axis=1) * _silu(z_t)


def _b_specs(s):
    nb = s // WINDOW
    qsp = pl.BlockSpec((WINDOW, 512), lambda n: (n, P_QB // 512))
    zsp = pl.BlockSpec((WINDOW, 512), lambda n: (n, P_ZB // 512))

    def three(col, width):
        return [pl.BlockSpec((WINDOW, width), lambda n: (jnp.maximum(n - 1, 0), col)),
                pl.BlockSpec((WINDOW, width), lambda n: (n, col)),
                pl.BlockSpec((WINDOW, width), lambda n: (jnp.minimum(n + 1, nb - 1), col))]

    tab = pl.BlockSpec((WINDOW, B_DIM), lambda n: (n, 0))
    small = pl.BlockSpec((8, LANE), lambda n: (0, 0))
    specs = [qsp, zsp] + three(P_KB // LANE, LANE) + three(P_VB // LANE, LANE) + [tab, tab] + three(0, B_DIM) + three(0, B_DIM) + [small]
    return nb, specs


def _b_args(proj, cos2, sin2, pb):
    return (proj, proj, proj, proj, proj, proj, proj, proj, cos2, sin2, cos2, cos2, cos2, sin2, sin2, sin2, pb)


def _b_load(refs):
    (q_ref, z_ref, kp, kc, kx, vp, vc, vx, cq, sq, ckp, ckc, ckx, skp, skc, skx, pb_ref) = refs
    cat = lambda *r: jnp.concatenate([t[...] for t in r], axis=0)
    return (q_ref[...], z_ref[...], cat(kp, kc, kx), cat(vp, vc, vx), cq[...], sq[...], cat(ckp, ckc, ckx),
            cat(skp, skc, skx), pb_ref[...])


def _attn_b_fwd(proj, cos2, sin2, pb, mixed):
    s = proj.shape[0]
    nb, specs = _b_specs(s)

    def body(*refs):
        o_ref = refs[-1]
        args = _b_load(refs[:-2])
        o_ref[...] = _b_block(*args, pl.program_id(0), nb).astype(BF16)

    return pl.pallas_call(
        body, name="attn_b_fwd", grid=(nb,), in_specs=specs + [pl.BlockSpec(memory_space=pl.ANY)],
        out_specs=pl.BlockSpec((WINDOW, 512), lambda n: (n, A_WIDTH // 512)),
        out_shape=jax.ShapeDtypeStruct(mixed.shape, mixed.dtype), input_output_aliases={len(specs): 0},
        compiler_params=_params(("parallel",)),
    )(*_b_args(proj, cos2, sin2, pb), mixed)


def _attn_b_bwd(proj, cos2, sin2, pb, d_mixed):
    s = proj.shape[0]
    nb, specs = _b_specs(s)
    w = WINDOW

    def body(*refs):
        dm_ref, dq_ref, dz_ref, dk_ref, dv_ref, dpb_ref = refs[-6:]
        n = pl.program_id(0)
        q_t, z_t, k3, v3, cq, sq, ck, sk, pb_v = _b_load(refs[:-6])

        @pl.when(n == 0)
        def _():
            dk_ref[...] = jnp.zeros_like(dk_ref)
            dv_ref[...] = jnp.zeros_like(dv_ref)
            dpb_ref[...] = jnp.zeros_like(dpb_ref)

        def f(q_, z_, k_, v_, pb_):
            return _b_block(q_, z_, k_, v_, cq, sq, ck, sk, pb_, n, nb)

        _, vjp = jax.vjp(f, q_t, z_t, k3, v3, pb_v)
        dq, dz, dk3, dv3, dpb = vjp(dm_ref[...])
        dq_ref[...] = dq
        dz_ref[...] = dz
        dpb_ref[...] += dpb

        def add(j, cond):
            @pl.when(cond)
            def _():
                rows = pl.ds(pl.multiple_of((n - 1 + j) * w, w), w)
                dk_ref[rows, :] += dk3[j * w:(j + 1) * w, :]
                dv_ref[rows, :] += dv3[j * w:(j + 1) * w, :]

        add(0, n > 0)
        add(1, n >= 0)
        add(2, n < nb - 1)

    blk = pl.BlockSpec((w, 512), lambda n: (n, 0))
    whole = pl.BlockSpec((s, LANE), lambda n: (0, 0))
    small = pl.BlockSpec((8, LANE), lambda n: (0, 0))
    return pl.pallas_call(
        body, name="attn_b_bwd", grid=(nb,),
        in_specs=specs + [pl.BlockSpec((w, 512), lambda n: (n, 2))],
        out_specs=[blk, blk, whole, whole, small],
        out_shape=[jax.ShapeDtypeStruct((s, 512), F32), jax.ShapeDtypeStruct((s, 512), F32),
                   jax.ShapeDtypeStruct((s, LANE), F32), jax.ShapeDtypeStruct((s, LANE), F32),
                   jax.ShapeDtypeStruct((8, LANE), F32)],
        compiler_params=_params(("arbitrary",)),
    )(*_b_args(proj, cos2, sin2, pb), d_mixed)


def _mem_kv_fwd(mem, mem_norm_w, w_kv):
    def body(mem_ref, nw_ref, w_ref, kv_ref):
        mn = _rms(mem_ref[...], nw_ref[...]).astype(BF16)
        kv_ref[...] = jnp.dot(mn, w_ref[...], preferred_element_type=F32)

    return pl.pallas_call(
        body, name="mem_kv_fwd", out_shape=jax.ShapeDtypeStruct((MEM_LEN, 2 * C_HEADS * C_DIM), F32),
        compiler_params=_params(),
    )(mem, mem_norm_w, w_kv)


def _mem_kv_bwd(mem, mem_norm_w, w_kv, d_kv):
    def body(mem_ref, nw_ref, w_ref, g_ref, gw_ref, gn_ref):
        mn, vjp = jax.vjp(_rms, mem_ref[...], nw_ref[...])
        g = g_ref[...].astype(BF16)
        gw_ref[...] = lax.dot_general(mn.astype(BF16), g, (((0,), (0,)), ((), ())), preferred_element_type=F32)
        d_mn = lax.dot_general(g, w_ref[...], (((1,), (1,)), ((), ())), preferred_element_type=F32)
        gn_ref[...] = vjp(d_mn)[1]

    return pl.pallas_call(
        body, name="mem_kv_bwd",
        out_shape=[jax.ShapeDtypeStruct((D_MODEL, 2 * C_HEADS * C_DIM), F32), jax.ShapeDtypeStruct((1, D_MODEL), F32)],
        compiler_params=_params(),
    )(mem, mem_norm_w, w_kv, d_kv)


def _c_tile(q_t, z_t, kvm, pc):
    width = C_HEADS * C_DIM
    outs = []
    for h in range(C_HEADS):
        q = _rms(q_t[:, h * C_DIM:(h + 1) * C_DIM], pc[0:1, :])
        k = _rms(kvm[:, h * C_DIM:(h + 1) * C_DIM], pc[1:2, :])
        v = kvm[:, width + h * C_DIM:width + (h + 1) * C_DIM]
        s = _mm_nt(q, k) * (C_DIM ** -0.5)
        p = jnp.exp(s - jnp.max(s, axis=1, keepdims=True))
        p = p / jnp.sum(p, axis=1, keepdims=True)
        outs.append(_mm(p, v))
    return jnp.concatenate(outs, axis=1) * _silu(z_t)


def _attn_c_fwd(proj, kvm, pc, mixed, tq=256):
    s = proj.shape[0]

    def body(q_ref, z_ref, kv_ref, pc_ref, mixed_ref, o_ref):
        o_ref[...] = _c_tile(q_ref[...], z_ref[...], kv_ref[...], pc_ref[...]).astype(BF16)

    return pl.pallas_call(
        body, name="attn_c_fwd", grid=(s // tq,),
        in_specs=[pl.BlockSpec((tq, 512), lambda i: (i, P_QC // 512)), pl.BlockSpec((tq, 512), lambda i: (i, P_ZC // 512)),
                  pl.BlockSpec(kvm.shape, lambda i: (0, 0)), pl.BlockSpec((8, LANE), lambda i: (0, 0)),
                  pl.BlockSpec(memory_space=pl.ANY)],
        out_specs=pl.BlockSpec((tq, 512), lambda i: (i, (A_WIDTH + 512) // 512)),
        out_shape=jax.ShapeDtypeStruct(mixed.shape, mixed.dtype), input_output_aliases={4: 0},
        compiler_params=_params(("parallel",)),
    )(proj, proj, kvm, pc, mixed)


def _attn_c_bwd(proj, kvm, pc, d_mixed, tq=256):
    s = proj.shape[0]

    def body(q_ref, z_ref, kv_ref, pc_ref, dm_ref, dq_ref, dz_ref, dkv_ref, dpc_ref):
        @pl.when(pl.program_id(0) == 0)
        def _():
            dkv_ref[...] = jnp.zeros_like(dkv_ref)
            dpc_ref[...] = jnp.zeros_like(dpc_ref)

        _, vjp = jax.vjp(_c_tile, q_ref[...], z_ref[...], kv_ref[...], pc_ref[...])
        dq, dz, dkv, dpc = vjp(dm_ref[...])
        dq_ref[...] = dq
        dz_ref[...] = dz
        dkv_ref[...] += dkv
        dpc_ref[...] += dpc

    blk = pl.BlockSpec((tq, 512), lambda i: (i, 0))
    kvs = pl.BlockSpec(kvm.shape, lambda i: (0, 0))
    small = pl.BlockSpec((8, LANE), lambda i: (0, 0))
    return pl.pallas_call(
        body, name="attn_c_bwd", grid=(s // tq,),
        in_specs=[pl.BlockSpec((tq, 512), lambda i: (i, P_QC // 512)), pl.BlockSpec((tq, 512), lambda i: (i, P_ZC // 512)),
                  kvs, small, pl.BlockSpec((tq, 512), lambda i: (i, 3))],
        out_specs=[blk, blk, kvs, small],
        out_shape=[jax.ShapeDtypeStruct((s, 512), F32), jax.ShapeDtypeStruct((s, 512), F32),
                   jax.ShapeDtypeStruct(kvm.shape, F32), jax.ShapeDtypeStruct((8, LANE), F32)],
        compiler_params=_params(("arbitrary",)),
    )(proj, proj, kvm, pc, d_mixed)


def _pad_row(v, width=LANE):
    v = v.reshape(1, -1)
    return jnp.pad(v, ((0, 0), (0, width - v.shape[1])))


def _local_step(x, mem, target, norm_w, w_perm, conv_w, pa, pb, pc, mem_norm_w, w_kv, w_out):
    s = x.shape[0]
    cos2, sin2 = _rope_tables(s)
    hn = _rms_fwd(x, norm_w)
    wide = dict(tm=1024, tn=512, tk=2048)
    proj = _matmul(hn, w_perm, "nn", F32, "mm_proj", **wide)
    cqkv = _conv_fwd(proj, conv_w)
    mixed, o_sum, states = _delta_fwd(cqkv, proj, pa)
    mixed = _attn_b_fwd(proj, cos2, sin2, pb, mixed)
    kvm = _mem_kv_fwd(mem, mem_norm_w, w_kv)
    mixed = _attn_c_fwd(proj, kvm, pc, mixed)
    mo = _matmul(mixed, w_out, "nn", F32, "mm_out", **wide)
    dy, dyb, loss_parts = _loss_dy(x, mo, target)

    d_mixed = _matmul(dyb, w_out, "nt", F32, "mm_dmixed", **wide)
    g_w_out = _matmul(mixed, dyb, "tn", F32, "mm_gwout", **wide)
    d_qc, d_zc, d_kvm, d_pc = _attn_c_bwd(proj, kvm, pc, d_mixed)
    g_w_kv, g_mem_norm = _mem_kv_bwd(mem, mem_norm_w, w_kv, d_kvm)
    d_qb, d_zb, d_kb, d_vb, d_pb = _attn_b_bwd(proj, cos2, sin2, pb, d_mixed)
    d_c, d_za, d_gt, d_pa = _delta_bwd(cqkv, proj, pa, d_mixed, o_sum, states)
    d_qkv, g_conv = _conv_bwd(proj, conv_w, d_c)
    d_proj = jnp.concatenate([d_qkv, d_za, d_qb, d_zb, d_qc, d_zc, d_kb, d_vb, d_gt,
                              jnp.zeros((s, P_WIDTH - P_GT - LANE), F32)], axis=1).astype(BF16)
    d_hn = _matmul(d_proj, w_perm, "nt", F32, "mm_dhn", tm=1024, tn=2048, tk=512)
    g_w_perm = _matmul(hn, d_proj, "tn", F32, "mm_gwin", **wide)
    g_x, g_norm = _rms_bwd(x, norm_w, d_hn, dy)
    return dict(loss_parts=loss_parts, g_x=g_x, g_norm=g_norm, g_w_perm=g_w_perm, g_conv=g_conv, d_pa=d_pa,
                d_pb=d_pb, d_pc=d_pc, g_mem_norm=g_mem_norm, g_w_kv=g_w_kv, g_w_out=g_w_out)


def _permute_cols(w):
    pad = jnp.zeros((w.shape[0], P_WIDTH - IN_WIDTH), w.dtype)
    return jnp.concatenate([w[:, :O_GT], w[:, O_QB:O_KB], w[:, O_ZB:O_QC], w[:, O_QC:O_ZC], w[:, O_ZC:IN_WIDTH],
                            w[:, O_KB:O_VB], w[:, O_VB:O_ZB], w[:, O_GT:O_QB], pad], axis=1)


def _unpermute_cols(g):
    return jnp.concatenate([g[:, :P_QB], g[:, P_GT:P_GT + 32], g[:, P_QB:P_ZB], g[:, P_KB:P_VB], g[:, P_VB:P_GT],
                            g[:, P_ZB:P_QC], g[:, P_QC:P_ZC], g[:, P_ZC:P_KB]], axis=1)


_SEGMENTS = ((0, O_GT, 0), (O_GT, O_QB, P_GT), (O_QB, O_KB, P_QB), (O_KB, O_VB, P_KB), (O_VB, O_ZB, P_VB),
             (O_ZB, O_QC, P_ZB), (O_QC, O_ZC, P_QC), (O_ZC, IN_WIDTH, P_ZC))


def _permute_blocks(w4):
    parts = []
    for first, end, _ in sorted(_SEGMENTS, key=lambda seg: seg[2]):
        col = first
        while col < end:
            k = col // W_IN_BLOCK
            stop = min(end, (k + 1) * W_IN_BLOCK)
            parts.append(w4[k][:, col - k * W_IN_BLOCK:stop - k * W_IN_BLOCK])
            col = stop
    parts.append(jnp.zeros((w4.shape[1], P_WIDTH - IN_WIDTH), w4.dtype))
    return jnp.concatenate(parts, axis=1)


def _unpermute_blocks(g):
    blocks = []
    for k in range(N_CHIPS):
        lo, hi = k * W_IN_BLOCK, (k + 1) * W_IN_BLOCK
        parts = [g[:, p + max(first, lo) - first:p + min(end, hi) - first]
                 for first, end, p in _SEGMENTS if max(first, lo) < min(end, hi)]
        blocks.append(jnp.concatenate(parts, axis=1))
    return jnp.stack(blocks, axis=0)


HBM = pl.BlockSpec(memory_space=pltpu.HBM)


def _place():
    x, y, c = lax.axis_index("x"), lax.axis_index("y"), lax.axis_index("c")
    chips = [(1 - x, y), (x, 1 - y), (1 - x, 1 - y)]
    return x, y, c, 2 * x + y, chips, [2 * cx + cy for cx, cy in chips]


COPY_CHUNK_ROWS = 128
COPY_MAX_CHUNKS = 8
PAIR_PIECE_ROWS = 256


class _Copies:
    def __init__(self, make, src, dst):
        rows = src.shape[-2]
        n = max(1, min(COPY_MAX_CHUNKS, rows // COPY_CHUNK_ROWS))
        assert rows % n == 0
        step = rows // n
        lead = (slice(None),) * (len(src.shape) - 2)
        self.whole = make(src, dst)
        self.parts = [self.whole] if n == 1 else [
            make(src.at[lead + (pl.ds(i * step, step), slice(None))], dst.at[lead + (pl.ds(i * step, step), slice(None))])
            for i in range(n)]

    def start(self):
        for p in self.parts:
            p.start()

    def wait(self):
        self.whole.wait()

    def wait_send(self):
        self.whole.wait_send()

    def wait_recv(self):
        self.whole.wait_recv()


def _remote(src, dst, send_sems, recv_sems, k, to):
    def make(s, d):
        return pltpu.make_async_remote_copy(src_ref=s, dst_ref=d, send_sem=send_sems.at[k], recv_sem=recv_sems.at[k],
                                            device_id=to, device_id_type=MESH)
    return _Copies(make, src, dst)


def _local(src, dst, sem):
    return _Copies(lambda s, d: pltpu.make_async_copy(s, d, sem), src, dst)


def _half_rows(ref, c):
    half = ref.shape[-2] // 2
    return pl.ds(pl.multiple_of(c * half, 8), half)


def _all_gather_weights(w_in_b, w_out_b, w_kv_b, conv_b):
    bigs = (w_in_b, w_out_b, w_kv_b)
    n_big = len(bigs)

    def body(*refs):
        srcs, conv_src = refs[:n_big], refs[n_big]
        dsts, conv_dst = refs[n_big + 1:2 * n_big + 1], refs[2 * n_big + 1]
        send_sems, recv_sems, local_sems = refs[2 * n_big + 2:]
        x, y, c, me, chips, chip_ids = _place()
        sibling = (x, y, 1 - c)
        local = [_local(src, dst.at[me], local_sems.at[a]) for a, (src, dst) in enumerate(zip(srcs, dsts))]
        local.append(_local(conv_src, conv_dst.at[me], local_sems.at[n_big]))
        for cp in local:
            cp.start()
        sends = []
        for a, (src, dst) in enumerate(zip(srcs, dsts)):
            mine = _half_rows(src, c)
            for j, chip in enumerate(chips):
                sends.append(_remote(src.at[mine, :], dst.at[me, mine, :], send_sems, recv_sems, 6 * a + j, (*chip, c)))
        for j, chip in enumerate(chips):
            sends.append(_remote(conv_src, conv_dst.at[me], send_sems, recv_sems, 6 * n_big + j, (*chip, c)))
        for cp in sends:
            cp.start()
        passed = []
        for a, (src, dst) in enumerate(zip(srcs, dsts)):
            mine = _half_rows(src, c)
            for j, cid in enumerate(chip_ids):
                landed = dst.at[cid, mine, :]
                _remote(landed, landed, send_sems, recv_sems, 6 * a + j, sibling).wait_recv()
                cp = _remote(landed, landed, send_sems, recv_sems, 6 * a + 3 + j, sibling)
                cp.start()
                passed.append(cp)
        for a, (src, dst) in enumerate(zip(srcs, dsts)):
            other = _half_rows(src, 1 - c)
            for j, cid in enumerate(chip_ids):
                landed = dst.at[cid, other, :]
                _remote(landed, landed, send_sems, recv_sems, 6 * a + 3 + j, sibling).wait_recv()
        for j, cid in enumerate(chip_ids):
            _remote(conv_src, conv_dst.at[cid], send_sems, recv_sems, 6 * n_big + j, sibling).wait_recv()
        for cp in sends + passed:
            cp.wait_send()
        for cp in local:
            cp.wait()

    n_sem = 6 * n_big + 3
    return pl.pallas_call(
        body, name="all_gather_weights",
        out_shape=[jax.ShapeDtypeStruct((N_CHIPS,) + w.shape, w.dtype) for w in bigs + (conv_b,)],
        in_specs=[pl.BlockSpec(memory_space=pltpu.VMEM)] * (n_big + 1), out_specs=[HBM] * (n_big + 1),
        scratch_shapes=[pltpu.SemaphoreType.DMA((n_sem,)), pltpu.SemaphoreType.DMA((n_sem,)),
                        pltpu.SemaphoreType.DMA((n_big + 1,))],
        compiler_params=_params(),
    )(*bigs, conv_b)


def _pair_exchange(grads):
    n = len(grads)
    piece = PAIR_PIECE_ROWS

    def body(*refs):
        srcs, gots = refs[:n], refs[n:2 * n]
        stages = refs[2 * n:3 * n]
        send_sems, recv_sems, load_sems = refs[3 * n:]
        x, y, c, _, _, _ = _place()
        sibling = (x, y, 1 - c)
        for a in range(n):
            slabs, half, _ = gots[a].shape
            per_slab = half // piece
            first = (1 - c) * half
            loads, sends = [], []
            for i in range(slabs * per_slab):
                k, r, slot = i // per_slab, i % per_slab, i % 2
                rows = pl.ds(pl.multiple_of(first + r * piece, 8), piece)
                loads.append(pltpu.make_async_copy(srcs[a].at[k, rows, :], stages[a].at[slot], load_sems.at[2 * a + slot]))
                sends.append(pltpu.make_async_remote_copy(
                    src_ref=stages[a].at[slot], dst_ref=gots[a].at[k, pl.ds(r * piece, piece), :],
                    send_sem=send_sems.at[2 * a + slot], recv_sem=recv_sems.at[a], device_id=sibling, device_id_type=MESH))
            loads[0].start()
            for i in range(len(loads)):
                loads[i].wait()
                sends[i].start()
                if i + 1 < len(loads):
                    if i >= 1:
                        sends[i - 1].wait_send()
                    loads[i + 1].start()
            for cp in sends[-2:]:
                cp.wait_send()
        for a in range(n):
            whole = srcs[a].at[:, _half_rows(srcs[a], c), :]
            pltpu.make_async_remote_copy(src_ref=whole, dst_ref=gots[a], send_sem=send_sems.at[2 * a],
                                         recv_sem=recv_sems.at[a], device_id=sibling, device_id_type=MESH).wait_recv()

    halves = [jax.ShapeDtypeStruct((g.shape[0], g.shape[1] // 2, g.shape[2]), g.dtype) for g in grads]
    assert all(h.shape[1] % piece == 0 and (h.shape[0] * h.shape[1] // piece) >= 2 for h in halves)
    return pl.pallas_call(
        body, name="grad_pair_exchange", out_shape=halves, in_specs=[HBM] * n, out_specs=[HBM] * n,
        scratch_shapes=[pltpu.VMEM((2, piece, g.shape[2]), g.dtype) for g in grads]
        + [pltpu.SemaphoreType.DMA((2 * n,)), pltpu.SemaphoreType.DMA((n,)), pltpu.SemaphoreType.DMA((2 * n,))],
        compiler_params=_params(),
    )(*grads)


def _chip_exchange(halves):
    n = len(halves)

    def body(*refs):
        srcs, lands = refs[:n], refs[n:2 * n]
        send_sems, recv_sems = refs[2 * n:]
        x, y, c, me, chips, chip_ids = _place()
        gives = []
        for a in range(n):
            for j, (chip, cid) in enumerate(zip(chips, chip_ids)):
                give = _remote(srcs[a].at[cid], lands[a].at[j], send_sems, recv_sems, 3 * a + j, (*chip, c))
                give.start()
                gives.append(give)
        for a in range(n):
            for j, cid in enumerate(chip_ids):
                _remote(srcs[a].at[cid], lands[a].at[j], send_sems, recv_sems, 3 * a + j, (x, y, c)).wait_recv()
        for give in gives:
            give.wait_send()

    return pl.pallas_call(
        body, name="grad_chip_exchange",
        out_shape=[jax.ShapeDtypeStruct((N_CHIPS - 1,) + h.shape[1:], h.dtype) for h in halves],
        in_specs=[HBM] * n, out_specs=[HBM] * n,
        scratch_shapes=[pltpu.SemaphoreType.DMA((3 * n,)), pltpu.SemaphoreType.DMA((3 * n,))],
    )(*halves)


def _pair_gather(halves):
    n = len(halves)

    def body(*refs):
        srcs, fulls = refs[:n], refs[n:2 * n]
        send_sems, recv_sems, local_sems = refs[2 * n:]
        x, y, c, _, _, _ = _place()
        copies = []
        for a in range(n):
            mine = _half_rows(fulls[a], c)
            keep = _local(srcs[a], fulls[a].at[mine, :], local_sems.at[a])
            keep.start()
            give = _remote(srcs[a], fulls[a].at[mine, :], send_sems, recv_sems, a, (x, y, 1 - c))
            give.start()
            copies += [keep, give]
        for a in range(n):
            other = _half_rows(fulls[a], 1 - c)
            copies[2 * a].wait()
            copies[2 * a + 1].wait_send()
            _remote(srcs[a], fulls[a].at[other, :], send_sems, recv_sems, a, (x, y, 1 - c)).wait_recv()

    return pl.pallas_call(
        body, name="grad_pair_gather",
        out_shape=[jax.ShapeDtypeStruct((2 * h.shape[0], h.shape[1]), h.dtype) for h in halves],
        in_specs=[pl.BlockSpec(memory_space=pltpu.VMEM)] * n, out_specs=[HBM] * n,
        scratch_shapes=[pltpu.SemaphoreType.DMA((n,)), pltpu.SemaphoreType.DMA((n,)), pltpu.SemaphoreType.DMA((n,))],
    )(*halves)


def _all_reduce_small(p):
    n_dev = 8

    def body(p_ref, o_ref, land, send_sems, recv_sems):
        x, y, c = lax.axis_index("x"), lax.axis_index("y"), lax.axis_index("c")
        me = 4 * x + 2 * y + c
        land[me] = p_ref[...]
        sends = []
        for k in range(1, n_dev):
            fx, fy, fc = (k >> 2) & 1, (k >> 1) & 1, k & 1
            to = (x ^ fx, y ^ fy, c ^ fc)
            cp = _remote(p_ref, land.at[me], send_sems, recv_sems, k - 1, to)
            cp.start()
            sends.append(cp)
        for k in range(1, n_dev):
            _remote(p_ref, land.at[me ^ k], send_sems, recv_sems, k - 1, (x, y, c)).wait_recv()
        total = land[0]
        for d in range(1, n_dev):
            total = total + land[d]
        o_ref[...] = total
        for cp in sends:
            cp.wait_send()

    vm = pl.BlockSpec(memory_space=pltpu.VMEM)
    return pl.pallas_call(
        body, name="all_reduce_small", out_shape=jax.ShapeDtypeStruct(p.shape, p.dtype), in_specs=[vm], out_specs=vm,
        scratch_shapes=[pltpu.VMEM((n_dev,) + p.shape, p.dtype), pltpu.SemaphoreType.DMA((n_dev - 1,)),
                        pltpu.SemaphoreType.DMA((n_dev - 1,))],
    )(p)


def _row_tile(rows, cap=256):
    return cap if rows % cap == 0 else rows


def _pair_sum(full, got, core, name):
    n, r, c = got.shape
    tr = _row_tile(r)
    nt = r // tr

    def body(core_ref, a_ref, b_ref, o_ref):
        o_ref[...] = (a_ref[...] + b_ref[...]).astype(BF16)

    blk = pl.BlockSpec((None, tr, c), lambda i, j, core_ref: (i, j, 0))
    grid_spec = pltpu.PrefetchScalarGridSpec(
        num_scalar_prefetch=1, grid=(n, nt),
        in_specs=[pl.BlockSpec((None, tr, c), lambda i, j, core_ref: (i, core_ref[0] * nt + j, 0)), blk], out_specs=blk)
    return pl.pallas_call(body, name=name, grid_spec=grid_spec, out_shape=jax.ShapeDtypeStruct(got.shape, BF16),
                          compiler_params=_params(("parallel", "parallel")))(core, full, got)


def _chip_sum(full, got, land, place, name):
    n, r, c = land.shape
    tr = _row_tile(r)
    nt = r // tr

    def body(place_ref, a_ref, b_ref, l_ref, o_ref):
        total = a_ref[...] + b_ref[...]
        for j in range(n):
            total = total + l_ref[j].astype(F32)
        o_ref[...] = total

    grid_spec = pltpu.PrefetchScalarGridSpec(
        num_scalar_prefetch=1, grid=(nt,),
        in_specs=[pl.BlockSpec((None, tr, c), lambda i, p: (p[0], p[1] * nt + i, 0)),
                  pl.BlockSpec((None, tr, c), lambda i, p: (p[0], i, 0)),
                  pl.BlockSpec((n, tr, c), lambda i, p: (0, i, 0))],
        out_specs=pl.BlockSpec((tr, c), lambda i, p: (i, 0)))
    return pl.pallas_call(body, name=name, grid_spec=grid_spec, out_shape=jax.ShapeDtypeStruct((r, c), F32),
                          compiler_params=_params(("parallel",)))(place, full, got, land)


def _adamw(w, g, m, v, name):
    r, c = w.shape
    tr = _row_tile(r)

    def body(w_ref, g_ref, m_ref, v_ref, d_ref, mo_ref, vo_ref):
        g_ = g_ref[...]
        m2 = ADAM_B1 * m_ref[...] + (1.0 - ADAM_B1) * g_
        v2 = ADAM_B2 * v_ref[...] + (1.0 - ADAM_B2) * jnp.square(g_)
        m_hat = m2 / (1.0 - ADAM_B1 ** ADAM_STEP)
        v_hat = v2 / (1.0 - ADAM_B2 ** ADAM_STEP)
        d_ref[...] = -ADAM_LR * (m_hat / (jnp.sqrt(v_hat) + ADAM_EPS) + ADAM_WD * w_ref[...])
        mo_ref[...] = m2
        vo_ref[...] = v2

    blk = pl.BlockSpec((tr, c), lambda i: (i, 0))
    return pl.pallas_call(body, name=name, grid=(r // tr,), in_specs=[blk] * 4, out_specs=[blk] * 3,
                          out_shape=[jax.ShapeDtypeStruct(w.shape, F32)] * 3, compiler_params=_params(("parallel",)))(w, g, m, v)


SMALL_NAMES = ("norm_w", "mem_norm_w", "o_norm_a", "q_norm_c", "k_norm_c", "q_norm_b", "k_norm_b",
               "a_log_fwd", "a_log_bwd", "dt_bias_fwd", "dt_bias_bwd", "sink_b")
SMALL_SIZES = (2048, 2048, 128, 128, 128, 64, 64, 8, 8, 8, 8, 8)
SMALL_LOSS = sum(SMALL_SIZES)
SMALL_CONV = 5120
SMALL_TOTAL = SMALL_CONV + CONV_K * 3 * A_WIDTH
SMALL_ROWS = SMALL_TOTAL // LANE


def _pack_small(parts, extra=None, conv=None):
    vec = [parts[n].reshape(-1) for n in SMALL_NAMES]
    vec.append(jnp.zeros((1,), F32) if extra is None else extra.reshape(1))
    vec.append(jnp.zeros((SMALL_CONV - SMALL_LOSS - 1,), F32))
    vec.append(jnp.zeros((SMALL_TOTAL - SMALL_CONV,), F32) if conv is None else conv.reshape(-1))
    return jnp.concatenate(vec).reshape(SMALL_ROWS, LANE)


def _unpack_small(packed):
    flat = packed.reshape(-1)
    out, off = {}, 0
    for n, size in zip(SMALL_NAMES, SMALL_SIZES):
        out[n] = flat[off:off + size].reshape(1, size)
        off += size
    return out


WEIGHT_ORDER = ("norm_w", "w_in", "conv_w_a", "a_log_fwd", "a_log_bwd", "dt_bias_fwd", "dt_bias_bwd", "o_norm_a",
                "q_norm_b", "k_norm_b", "sink_b", "mem_norm_w", "w_mem_kv", "q_norm_c", "k_norm_c", "w_out")


def kernel(x, mem, norm_w, w_in, conv_w_a, a_log_fwd, a_log_bwd, dt_bias_fwd, dt_bias_bwd, o_norm_a, q_norm_b, k_norm_b, sink_b, mem_norm_w, w_mem_kv, q_norm_c, k_norm_c, w_out, loss_target, m_norm_w, m_w_in, m_conv_w_a, m_a_log_fwd, m_a_log_bwd, m_dt_bias_fwd, m_dt_bias_bwd, m_o_norm_a, m_q_norm_b, m_k_norm_b, m_sink_b, m_mem_norm_w, m_w_mem_kv, m_q_norm_c, m_k_norm_c, m_w_out, v_norm_w, v_w_in, v_conv_w_a, v_a_log_fwd, v_a_log_bwd, v_dt_bias_fwd, v_dt_bias_bwd, v_o_norm_a, v_q_norm_b, v_k_norm_b, v_sink_b, v_mem_norm_w, v_w_mem_kv, v_q_norm_c, v_k_norm_c, v_w_out):
    weights = dict(norm_w=norm_w, w_in=w_in, conv_w_a=conv_w_a, a_log_fwd=a_log_fwd, a_log_bwd=a_log_bwd,
                   dt_bias_fwd=dt_bias_fwd, dt_bias_bwd=dt_bias_bwd, o_norm_a=o_norm_a, q_norm_b=q_norm_b,
                   k_norm_b=k_norm_b, sink_b=sink_b, mem_norm_w=mem_norm_w, w_mem_kv=w_mem_kv, q_norm_c=q_norm_c,
                   k_norm_c=k_norm_c, w_out=w_out)
    mom1 = dict(norm_w=m_norm_w, w_in=m_w_in, conv_w_a=m_conv_w_a, a_log_fwd=m_a_log_fwd, a_log_bwd=m_a_log_bwd,
                dt_bias_fwd=m_dt_bias_fwd, dt_bias_bwd=m_dt_bias_bwd, o_norm_a=m_o_norm_a, q_norm_b=m_q_norm_b,
                k_norm_b=m_k_norm_b, sink_b=m_sink_b, mem_norm_w=m_mem_norm_w, w_mem_kv=m_w_mem_kv,
                q_norm_c=m_q_norm_c, k_norm_c=m_k_norm_c, w_out=m_w_out)
    mom2 = dict(norm_w=v_norm_w, w_in=v_w_in, conv_w_a=v_conv_w_a, a_log_fwd=v_a_log_fwd, a_log_bwd=v_a_log_bwd,
                dt_bias_fwd=v_dt_bias_fwd, dt_bias_bwd=v_dt_bias_bwd, o_norm_a=v_o_norm_a, q_norm_b=v_q_norm_b,
                k_norm_b=v_k_norm_b, sink_b=v_sink_b, mem_norm_w=v_mem_norm_w, w_mem_kv=v_w_mem_kv,
                q_norm_c=v_q_norm_c, k_norm_c=v_k_norm_c, w_out=v_w_out)
    chip = 2 * lax.axis_index("x") + lax.axis_index("y")

    w_in4, w_out4, w_kv4, conv4 = _all_gather_weights(w_in[0].astype(BF16), w_out[0].astype(BF16),
                                                      w_mem_kv[0].astype(BF16), conv_w_a[0])
    w_perm = _permute_blocks(w_in4)
    w_out_full = w_out4.reshape(D_MODEL, D_MODEL)
    w_kv_full = w_kv4.reshape(D_MODEL, 2 * C_HEADS * C_DIM)
    conv_full = jnp.transpose(conv4, (1, 0, 2)).reshape(CONV_K, 3 * A_WIDTH)
    pa = jnp.concatenate([_pad_row(a_log_fwd), _pad_row(a_log_bwd), _pad_row(dt_bias_fwd), _pad_row(dt_bias_bwd),
                          _pad_row(o_norm_a), jnp.zeros((3, LANE), F32)], axis=0)
    pb = jnp.concatenate([_pad_row(q_norm_b), _pad_row(k_norm_b), _pad_row(sink_b), jnp.zeros((5, LANE), F32)], axis=0)
    pc = jnp.concatenate([_pad_row(q_norm_c), _pad_row(k_norm_c), jnp.zeros((6, LANE), F32)], axis=0)

    r = _local_step(x[0], mem[0], loss_target[0], norm_w, w_perm, conv_full, pa, pb, pc, mem_norm_w, w_kv_full,
                    w_out_full)

    g_in4 = _unpermute_blocks(r["g_w_perm"])
    g_out4 = r["g_w_out"].reshape(N_CHIPS, D_MODEL // N_CHIPS, D_MODEL)
    g_kv4 = r["g_w_kv"].reshape(N_CHIPS, D_MODEL // N_CHIPS, 2 * C_HEADS * C_DIM)
    full = [g_in4, g_out4, g_kv4]
    core = lax.axis_index("c").astype(jnp.int32).reshape(1)
    got = _pair_exchange(full)
    pair = [_pair_sum(a, b, core, "grad_pair_sum_%d" % i) for i, (a, b) in enumerate(zip(full, got))]
    lands = _chip_exchange(pair)
    place = jnp.stack([chip, lax.axis_index("c")]).astype(jnp.int32)
    reduced = [_chip_sum(a, b, l, place, "grad_chip_sum_%d" % i) for i, (a, b, l) in enumerate(zip(full, got, lands))]
    g_w_in, g_w_out, g_w_kv = _pair_gather(reduced)

    d_pa, d_pb, d_pc = r["d_pa"], r["d_pb"], r["d_pc"]
    small_g = dict(norm_w=r["g_norm"], mem_norm_w=r["g_mem_norm"], o_norm_a=d_pa[4], q_norm_c=d_pc[0], k_norm_c=d_pc[1],
                   q_norm_b=d_pb[0, :B_DIM], k_norm_b=d_pb[1, :B_DIM], a_log_fwd=d_pa[0, :A_HEADS],
                   a_log_bwd=d_pa[1, :A_HEADS], dt_bias_fwd=d_pa[2, :A_HEADS], dt_bias_bwd=d_pa[3, :A_HEADS],
                   sink_b=d_pb[2, :B_HEADS])
    packed = _all_reduce_small(_pack_small(small_g, jnp.sum(r["loss_parts"][:, 0, 0]), r["g_conv"]))
    flat = packed.reshape(-1)
    loss = flat[SMALL_LOSS]
    conv_sum = flat[SMALL_CONV:].reshape(CONV_K, 3 * A_WIDTH)
    conv_cols = 3 * A_WIDTH // N_CHIPS
    g_conv = lax.dynamic_slice(conv_sum, (0, chip * conv_cols), (CONV_K, conv_cols))

    grads = _unpack_small(packed)
    grads.update(w_in=g_w_in, w_mem_kv=g_w_kv, w_out=g_w_out, conv_w_a=g_conv)
    delta, new_m, new_v = {}, {}, {}
    for n in ("w_in", "w_mem_kv", "w_out", "conv_w_a"):
        delta[n], new_m[n], new_v[n] = _adamw(weights[n][0], grads[n], mom1[n][0], mom2[n][0], "adamw_" + n)
    d_s, m_s, v_s = _adamw(_pack_small(weights), packed, _pack_small(mom1), _pack_small(mom2), "adamw_small")
    d_s, m_s, v_s = _unpack_small(d_s), _unpack_small(m_s), _unpack_small(v_s)
    for n in SMALL_NAMES:
        delta[n], new_m[n], new_v[n] = d_s[n], m_s[n], v_s[n]

    def shaped(tree):
        return [tree[n].reshape(weights[n].shape) for n in WEIGHT_ORDER]

    return (loss, r["g_x"].reshape(x.shape), *shaped(grads), *shaped(delta), *shaped(new_m), *shaped(new_v))
```

```python
import functools

import jax
import jax.numpy as jnp
from jax import lax
from jax.experimental import pallas as pl
from jax.experimental.pallas import tpu as pltpu

F32 = jnp.float32
BF16 = jnp.bfloat16
HI = lax.Precision.HIGHEST
MESH = pl.DeviceIdType.MESH

D_MODEL = 2048
A_WIDTH = 1024
A_HEADS = 8
A_DIM = 128
CONV_K = 5
CHUNK = 64
B_HEADS = 8
B_KV = 2
B_DIM = 64
WINDOW = 128
C_HEADS = 4
C_DIM = 128
MEM_LEN = 256
ROPE_THETA = 10000.0
EPS = 1e-6
IN_WIDTH = 6432
N_CHIPS = 4
W_IN_BLOCK = IN_WIDTH // N_CHIPS

LANE = 128
P_QA, P_KA, P_VA, P_ZA = 0, 1024, 2048, 3072
P_QB, P_ZB, P_QC, P_ZC = 4096, 4608, 5120, 5632
P_KB, P_VB, P_GT = 6144, 6272, 6400
P_WIDTH = 6656
O_GT, O_QB, O_KB, O_VB, O_ZB, O_QC, O_ZC = 4096, 4128, 4640, 4768, 4896, 5408, 5920

ADAM_LR, ADAM_B1, ADAM_B2, ADAM_EPS, ADAM_WD, ADAM_STEP = 0.001, 0.9, 0.999, 1e-08, 0.01, 10

VMEM_LIMIT = 56 * 1024 * 1024


def _params(sem=None):
    return pltpu.CompilerParams(dimension_semantics=sem, vmem_limit_bytes=VMEM_LIMIT)


def _dot(a, b, dims=(((1,), (0,)), ((), ())), precision=HI):
    return lax.dot_general(a, b, dims, precision=precision, preferred_element_type=F32)


def _dot_nt(a, b, precision=HI):
    return _dot(a, b, (((1,), (1,)), ((), ())), precision)


def _dot_tn(a, b, precision=HI):
    return _dot(a, b, (((0,), (0,)), ((), ())), precision)


_NN = (((1,), (0,)), ((), ()))
_NT = (((1,), (1,)), ((), ()))
_TN = (((0,), (0,)), ((), ()))


def _bdot(a, b, dims):
    return lax.dot_general(a.astype(BF16), b.astype(BF16), dims, preferred_element_type=F32)


@jax.custom_vjp
def _mm(a, b):
    return _bdot(a, b, _NN)


_mm.defvjp(lambda a, b: (_bdot(a, b, _NN), (a, b)),
           lambda res, ct: (_bdot(ct, res[1], _NT), _bdot(res[0], ct, _TN)))


@jax.custom_vjp
def _mm_nt(a, b):
    return _bdot(a, b, _NT)


_mm_nt.defvjp(lambda a, b: (_bdot(a, b, _NT), (a, b)),
              lambda res, ct: (_bdot(ct, res[1], _NN), _bdot(ct, res[0], _TN)))


@jax.custom_vjp
def _mm_tn(a, b):
    return _bdot(a, b, _TN)


_mm_tn.defvjp(lambda a, b: (_bdot(a, b, _TN), (a, b)),
              lambda res, ct: (_bdot(res[1], ct, _NT), _bdot(res[0], ct, _NN)))


def _rms(t, w):
    return t * lax.rsqrt(jnp.mean(t * t, axis=-1, keepdims=True) + EPS) * w


def _l2(t):
    return t * lax.rsqrt(jnp.sum(t * t, axis=-1, keepdims=True) + EPS)


def _silu(t):
    return t * jax.nn.sigmoid(t)


def _softplus(t):
    return jnp.maximum(t, 0.0) + jnp.log1p(jnp.exp(-jnp.abs(t)))


def _matmul(a, b, mode, out_dtype, name, tm=512, tn=512, tk=512):
    (m, k) = a.shape[::-1] if mode == "tn" else a.shape
    n = b.shape[0] if mode == "nt" else b.shape[1]
    tm, tn, tk = min(tm, m), min(tn, n), min(tk, k)
    assert m % tm == 0 and n % tn == 0 and k % tk == 0, (m, n, k, tm, tn, tk)
    if mode == "nn":
        a_spec = pl.BlockSpec((tm, tk), lambda i, j, kk: (i, kk))
        b_spec = pl.BlockSpec((tk, tn), lambda i, j, kk: (kk, j))
        dims = (((1,), (0,)), ((), ()))
    elif mode == "nt":
        a_spec = pl.BlockSpec((tm, tk), lambda i, j, kk: (i, kk))
        b_spec = pl.BlockSpec((tn, tk), lambda i, j, kk: (j, kk))
        dims = (((1,), (1,)), ((), ()))
    else:
        a_spec = pl.BlockSpec((tk, tm), lambda i, j, kk: (kk, i))
        b_spec = pl.BlockSpec((tk, tn), lambda i, j, kk: (kk, j))
        dims = (((0,), (0,)), ((), ()))
    nk = k // tk

    def body_one(a_ref, b_ref, o_ref):
        o_ref[...] = _bdot(a_ref[...], b_ref[...], dims).astype(out_dtype)

    def body_acc(a_ref, b_ref, o_ref, acc_ref):
        kk = pl.program_id(2)

        @pl.when(kk == 0)
        def _():
            acc_ref[...] = jnp.zeros_like(acc_ref)

        acc_ref[...] += _bdot(a_ref[...], b_ref[...], dims)

        @pl.when(kk == nk - 1)
        def _():
            o_ref[...] = acc_ref[...].astype(out_dtype)

    return pl.pallas_call(
        body_one if nk == 1 else body_acc, name=name, grid=(m // tm, n // tn, nk),
        in_specs=[a_spec, b_spec], out_specs=pl.BlockSpec((tm, tn), lambda i, j, kk: (i, j)),
        out_shape=jax.ShapeDtypeStruct((m, n), out_dtype),
        scratch_shapes=[] if nk == 1 else [pltpu.VMEM((tm, tn), F32)],
        compiler_params=_params(("parallel", "parallel", "arbitrary")),
    )(a, b)


def _rms_fwd(x, w, tr=256):
    s, d = x.shape

    def body(x_ref, w_ref, o_ref):
        o_ref[...] = _rms(x_ref[...], w_ref[...]).astype(BF16)

    return pl.pallas_call(
        body, name="rms_fwd", grid=(s // tr,),
        in_specs=[pl.BlockSpec((tr, d), lambda i: (i, 0)), pl.BlockSpec((1, d), lambda i: (0, 0))],
        out_specs=pl.BlockSpec((tr, d), lambda i: (i, 0)),
        out_shape=jax.ShapeDtypeStruct((s, d), BF16), compiler_params=_params(("parallel",)),
    )(x, w)


def _rms_bwd(x, w, d_hn, dy, tr=256):
    s, d = x.shape

    def body(x_ref, w_ref, g_ref, dy_ref, gx_ref, gw_ref):
        _, vjp = jax.vjp(_rms, x_ref[...], w_ref[...])
        dx, dw = vjp(g_ref[...])
        gx_ref[...] = dy_ref[...] + dx

        @pl.when(pl.program_id(0) == 0)
        def _():
            gw_ref[...] = jnp.zeros_like(gw_ref)

        gw_ref[...] += dw

    row = pl.BlockSpec((tr, d), lambda i: (i, 0))
    vec = pl.BlockSpec((1, d), lambda i: (0, 0))
    return pl.pallas_call(
        body, name="rms_bwd", grid=(s // tr,), in_specs=[row, vec, row, row], out_specs=[row, vec],
        out_shape=[jax.ShapeDtypeStruct((s, d), F32), jax.ShapeDtypeStruct((1, d), F32)],
        compiler_params=_params(("arbitrary",)),
    )(x, w, d_hn, dy)


def _loss_dy(x, mo, target, tr=256):
    s, d = x.shape
    nt = s // tr

    def body(x_ref, mo_ref, t_ref, dy_ref, dyb_ref, l_ref):
        err = x_ref[...] + mo_ref[...] - t_ref[...]
        dy = err * (1.0 / d)
        dy_ref[...] = dy
        dyb_ref[...] = dy.astype(BF16)
        l_ref[...] = jnp.full(l_ref.shape, 0.5 * jnp.sum(jnp.sum(err * err, axis=1, keepdims=True) * (1.0 / d)), F32)

    row = pl.BlockSpec((tr, d), lambda i: (i, 0))
    return pl.pallas_call(
        body, name="loss_dy", grid=(nt,), in_specs=[row, row, row],
        out_specs=[row, row, pl.BlockSpec((1, 8, LANE), lambda i: (i, 0, 0))],
        out_shape=[jax.ShapeDtypeStruct((s, d), F32), jax.ShapeDtypeStruct((s, d), BF16),
                   jax.ShapeDtypeStruct((nt, 8, LANE), F32)],
        compiler_params=_params(("parallel",)),
    )(x, mo, target)


def _shift_rows(t, s):
    if s == 0:
        return t
    n = t.shape[0]
    rolled = pltpu.roll(t, (-s) % n, axis=0)
    idx = lax.broadcasted_iota(jnp.int32, t.shape, 0) + s
    return jnp.where((idx >= 0) & (idx < n), rolled, 0.0)


def _conv_fwd(proj, conv_w):
    s = proj.shape[0]
    nblk = 3 * A_WIDTH // LANE

    def body(x_ref, w_ref, o_ref):
        x = x_ref[...]
        acc = jnp.zeros_like(x)
        for j in range(CONV_K):
            acc = acc + w_ref[j:j + 1, :] * _shift_rows(x, j - CONV_K // 2)
        o_ref[...] = acc

    return pl.pallas_call(
        body, name="conv_fwd", grid=(nblk,),
        in_specs=[pl.BlockSpec((s, LANE), lambda i: (0, i)), pl.BlockSpec((CONV_K, LANE), lambda i: (0, i))],
        out_specs=pl.BlockSpec((None, s, LANE), lambda i: (i // A_HEADS, 0, i % A_HEADS)),
        out_shape=jax.ShapeDtypeStruct((3, s, A_WIDTH), F32), compiler_params=_params(("parallel",)),
    )(proj, conv_w)


def _conv_bwd(proj, conv_w, d_c):
    s = proj.shape[0]
    nblk = 3 * A_WIDTH // LANE

    def body(x_ref, w_ref, g_ref, dx_ref, dw_ref):
        x, g = x_ref[...], g_ref[...]
        acc = jnp.zeros_like(x)
        for j in range(CONV_K):
            off = j - CONV_K // 2
            acc = acc + w_ref[j:j + 1, :] * _shift_rows(g, -off)
            dw_ref[j:j + 1, :] = jnp.sum(_shift_rows(x, off) * g, axis=0, keepdims=True)
        dx_ref[...] = acc

    col = pl.BlockSpec((s, LANE), lambda i: (0, i))
    wsp = pl.BlockSpec((CONV_K, LANE), lambda i: (0, i))
    dsp = pl.BlockSpec((None, s, LANE), lambda i: (i // A_HEADS, 0, i % A_HEADS))
    return pl.pallas_call(
        body, name="conv_bwd", grid=(nblk,), in_specs=[col, wsp, dsp], out_specs=[col, wsp],
        out_shape=[jax.ShapeDtypeStruct((s, 3 * A_WIDTH), F32), jax.ShapeDtypeStruct((CONV_K, 3 * A_WIDTH), F32)],
        compiler_params=_params(("parallel",)),
    )(proj, conv_w, d_c)


A_FWD_HEADS = 4
A_BWD_HEADS = 4


def _a_chain(st, cq, ck, cv, alpha, beta_raw, a_log, dt_b, incl, strict, last):
    c = CHUNK
    eye = (lax.broadcasted_iota(jnp.int32, (c, c), 0) == lax.broadcasted_iota(jnp.int32, (c, c), 1)).astype(F32)
    gb = -jnp.exp(a_log) * _softplus(alpha + dt_b)
    bb = jax.nn.sigmoid(beta_raw)
    q = _l2(_silu(cq)) * (A_DIM ** -0.5)
    k = _l2(_silu(ck))
    v = _silu(cv)

    gc = _dot(incl, jnp.broadcast_to(gb, (c, LANE)))
    tot = jnp.sum(gc * last, axis=0, keepdims=True)
    m1 = gc[:, :c]
    decay = incl * jnp.exp(incl * (m1 - m1.T))
    kb = k * bb
    vb = v * bb
    a = -(strict * decay * _mm_nt(kb, k))
    tinv = eye + a
    p = a
    for _ in range(5):
        p = _mm(p, p)
        tinv = tinv + _mm(tinv, p)
    eg = jnp.exp(gc)
    u = _mm(tinv, vb)
    w = _mm(tinv, kb * eg)
    qk = _mm_nt(q, k) * decay
    v_new = u - _mm(w, st)
    o = _mm(q * eg, st) + _mm(qk, v_new)
    st_new = st * jnp.exp(tot) + _mm_tn(k * jnp.exp(tot - gc), v_new)
    return st_new, o


def _a_step(sts, cq, ck, cv, gts, pa, h0):
    c = CHUNK
    lane = lax.broadcasted_iota(jnp.int32, (1, LANE), 1)
    ii = lax.broadcasted_iota(jnp.int32, (c, c), 0)
    jj = lax.broadcasted_iota(jnp.int32, (c, c), 1)
    row = lax.broadcasted_iota(jnp.int32, (c, 1), 0)

    def pick(t, col):
        return jnp.sum(jnp.where(lane == col, t, 0.0), axis=1, keepdims=True)

    alpha, beta_raw, a_log, dt_b, incl, strict, last = [], [], [], [], [], [], []
    for b in range(sts.shape[0]):
        h, rev = h0 + b // 2, b % 2
        alpha.append(pick(gts[b], h + 8 * rev))
        beta_raw.append(pick(gts[b], h + 16 + 8 * rev))
        a_log.append(pick(pa[rev:rev + 1, :], h))
        dt_b.append(pick(pa[2 + rev:3 + rev, :], h))
        incl.append(((ii <= jj) if rev else (ii >= jj)).astype(F32))
        strict.append(((ii < jj) if rev else (ii > jj)).astype(F32))
        last.append((row == (0 if rev else c - 1)).astype(F32))
    stack = lambda ts: jnp.concatenate([t[None] for t in ts], axis=0)
    return jax.vmap(_a_chain)(sts, cq, ck, cv, stack(alpha), stack(beta_raw), stack(a_log), stack(dt_b),
                              stack(incl), stack(strict), stack(last))


def _a_final(o, za, pa):
    outs = []
    for j in range(o.shape[1] // A_DIM):
        ln = slice(j * A_DIM, (j + 1) * A_DIM)
        outs.append(_rms(o[:, ln], pa[4:5, :]) * _silu(za[:, ln]))
    return jnp.concatenate(outs, axis=1)


def _a_tiles(n, nchunk, heads):
    tiles = []
    for b in range(2 * heads):
        i = (nchunk - 1 - n) if b % 2 else n
        tiles.append((i, pl.ds(pl.multiple_of(i * CHUNK, CHUNK), CHUNK), slice((b // 2) * A_DIM, (b // 2 + 1) * A_DIM)))
    return tiles


def _a_load(tiles, c_ref, gt_ref):
    cq, ck, cv = (jnp.stack([c_ref[r, sl, ln] for _, sl, ln in tiles], axis=0) for r in range(3))
    return cq, ck, cv, jnp.stack([gt_ref[sl, :] for _, sl, _ in tiles], axis=0)


def _loop_by_two(n, step, init):
    assert n % 2 == 0
    return lax.fori_loop(0, n // 2, lambda m, carry: step(2 * m + 1, step(2 * m, carry, 0), 1), init)


def _a_scan(h0, heads, nchunk, c_ref, gt_ref, pa, of_ref, ob_ref, s_ref):
    def step(n, sts, parity):
        tiles = _a_tiles(n, nchunk, heads)
        sts_new, o = _a_step(sts, *_a_load(tiles, c_ref, gt_ref), pa, h0)
        for b, (i, sl, ln) in enumerate(tiles):
            s_ref[b, i] = sts[b]
            (ob_ref if b % 2 else of_ref)[sl, ln] = o[b]
        return sts_new

    _loop_by_two(nchunk, step, jnp.zeros((2 * heads, A_DIM, A_DIM), F32))


def _a_specs(s, heads):
    wide = heads * A_DIM
    once = pl.Buffered(1)
    trio = pl.BlockSpec((3, s, wide), lambda g: (0, 0, g), pipeline_mode=once)
    gates = pl.BlockSpec((s, LANE), lambda g: (0, P_GT // LANE))
    small = pl.BlockSpec((8, LANE), lambda g: (0, 0))

    def cols(base):
        return pl.BlockSpec((s, wide), lambda g: (0, base // wide + g), pipeline_mode=once)

    state = pl.BlockSpec((2 * heads, s // CHUNK, A_DIM, A_DIM), lambda g: (g, 0, 0, 0), pipeline_mode=once)
    return wide, trio, gates, small, cols, state


def _delta_fwd(cqkv, proj, pa):
    s = cqkv.shape[1]
    nchunk = s // CHUNK
    heads = A_FWD_HEADS
    wide, trio, gates, small, cols, state = _a_specs(s, heads)

    def body(c_ref, gt_ref, za_ref, pa_ref, out_ref, o_ref, s_ref, ob_ref):
        h0 = pl.program_id(0) * heads
        pa_v = pa_ref[...]
        _a_scan(h0, heads, nchunk, c_ref, gt_ref, pa_v, o_ref, ob_ref, s_ref)
        o_ref[...] += ob_ref[...]
        out_ref[...] = _a_final(o_ref[...], za_ref[...], pa_v).astype(BF16)

    return pl.pallas_call(
        body, name="delta_fwd", grid=(A_HEADS // heads,),
        in_specs=[trio, gates, cols(P_ZA), small], out_specs=[cols(0), cols(0), state],
        out_shape=[jax.ShapeDtypeStruct((s, D_MODEL), BF16),
                   jax.ShapeDtypeStruct((s, A_WIDTH), F32),
                   jax.ShapeDtypeStruct((2 * A_HEADS, nchunk, A_DIM, A_DIM), F32)],
        scratch_shapes=[pltpu.VMEM((s, wide), F32)], compiler_params=_params(("parallel",)),
    )(cqkv, proj, proj, pa)


def _delta_out_bwd(o_sum, proj, pa, d_mixed, tr=256):
    s = o_sum.shape[0]

    def body(o_ref, za_ref, pa_ref, dm_ref, do_ref, dza_ref, dpa_ref):
        @pl.when(pl.program_id(0) == 0)
        def _():
            dpa_ref[...] = jnp.zeros_like(dpa_ref)

        _, vjp = jax.vjp(_a_final, o_ref[...], za_ref[...], pa_ref[...])
        d_o, d_za, dpa = vjp(dm_ref[...].astype(F32))
        do_ref[...] = d_o
        dza_ref[...] = d_za
        dpa_ref[...] += dpa

    def rows(col):
        return pl.BlockSpec((tr, A_WIDTH), lambda i: (i, col))

    small = pl.BlockSpec((8, LANE), lambda i: (0, 0))
    return pl.pallas_call(
        body, name="delta_out_bwd", grid=(s // tr,), in_specs=[rows(0), rows(P_ZA // A_WIDTH), small, rows(0)],
        out_specs=[rows(0), rows(0), small],
        out_shape=[jax.ShapeDtypeStruct((s, A_WIDTH), F32), jax.ShapeDtypeStruct((s, A_WIDTH), F32),
                   jax.ShapeDtypeStruct((8, LANE), F32)],
        compiler_params=_params(("arbitrary",)),
    )(o_sum, proj, pa, d_mixed)


def _delta_bwd(cqkv, proj, pa, d_o, states):
    s = cqkv.shape[1]
    nchunk = s // CHUNK
    heads = A_BWD_HEADS
    wide, trio, gates, small, cols, state = _a_specs(s, heads)

    def body(c_ref, gt_ref, pa_ref, do_ref, s_hbm, dc_ref, dgt_ref, dpa_ref, s_buf, s_sems):
        h0 = pl.program_id(0) * heads
        pa_v = pa_ref[...]

        @pl.when(h0 == 0)
        def _():
            dgt_ref[...] = jnp.zeros_like(dgt_ref)
            dpa_ref[...] = jnp.zeros_like(dpa_ref)

        dc_ref[...] = jnp.zeros_like(dc_ref)

        def state_copies(n, slot):
            return [pltpu.make_async_copy(s_hbm.at[2 * h0 + b, i], s_buf.at[slot, b], s_sems.at[slot, b])
                    for b, (i, _, _) in enumerate(_a_tiles(nchunk - 1 - n, nchunk, heads))]

        for cp in state_copies(0, 0):
            cp.start()

        def step(n, carry, parity):
            d_sts, dpa = carry
            tiles = _a_tiles(nchunk - 1 - n, nchunk, heads)
            for cp in state_copies(n, parity):
                cp.wait()

            @pl.when(n + 1 < nchunk)
            def _():
                for cp in state_copies(n + 1, 1 - parity):
                    cp.start()

            sts = s_buf[parity]
            d_o_t = jnp.stack([do_ref[sl, ln] for _, sl, ln in tiles], axis=0)
            _, vjp_c = jax.vjp(lambda *a: _a_step(*a, h0), sts, *_a_load(tiles, c_ref, gt_ref), pa_v)
            d_prev, dcq, dck, dcv, dgts, dpa_i = vjp_c((d_sts, d_o_t))
            for b, (_, sl, ln) in enumerate(tiles):
                for r, dc in enumerate((dcq, dck, dcv)):
                    dc_ref[r, sl, ln] += dc[b]
                dgt_ref[sl, :] += dgts[b]
            return d_prev, dpa + dpa_i

        init = (jnp.zeros((2 * heads, A_DIM, A_DIM), F32), jnp.zeros((8, LANE), F32))
        _, dpa_out = _loop_by_two(nchunk, step, init)
        dpa_ref[...] += dpa_out

    fixed = pl.BlockSpec((s, LANE), lambda g: (0, 0))
    return pl.pallas_call(
        body, name="delta_bwd", grid=(A_HEADS // heads,),
        in_specs=[trio, gates, small, cols(0), pl.BlockSpec(memory_space=pl.ANY)], out_specs=[trio, fixed, small],
        out_shape=[jax.ShapeDtypeStruct((3, s, A_WIDTH), F32), jax.ShapeDtypeStruct((s, LANE), F32),
                   jax.ShapeDtypeStruct((8, LANE), F32)],
        scratch_shapes=[pltpu.VMEM((2, 2 * heads, A_DIM, A_DIM), F32), pltpu.SemaphoreType.DMA((2, 2 * heads))],
        compiler_params=_params(("arbitrary",)),
    )(cqkv, proj, pa, d_o, states)


def _rope_tables(s):
    inv = ROPE_THETA ** (-jnp.arange(0, B_DIM, 2, dtype=F32) / B_DIM)
    ang = jnp.arange(s, dtype=F32)[:, None] * inv[None, :]
    cos, sin = jnp.cos(ang), jnp.sin(ang)
    return jnp.concatenate([cos, cos], axis=1), jnp.concatenate([-sin, sin], axis=1)


def _b_block(q_t, z_t, k3, v3, cos_q, sin_q, cos_k, sin_k, pb, n, nb):
    w = WINDOW
    def swap(t):
        return jnp.concatenate([t[:, B_DIM // 2:], t[:, :B_DIM // 2]], axis=1)

    grp = B_HEADS // B_KV
    qi = lax.broadcasted_iota(jnp.int32, (grp * w, 3 * w), 0) & (w - 1)
    kj = lax.broadcasted_iota(jnp.int32, (grp * w, 3 * w), 1)
    kpos = kj + (n - 1) * w
    mask = (jnp.abs(kj - w - qi) <= w) & (kpos >= 0) & (kpos < nb * w)
    lane = lax.broadcasted_iota(jnp.int32, (1, LANE), 1)
    qn, kn = pb[0:1, :B_DIM], pb[1:2, :B_DIM]
    cos_g = jnp.concatenate([cos_q] * grp, axis=0)
    sin_g = jnp.concatenate([sin_q] * grp, axis=0)
    outs = []
    for hk in range(B_KV):
        k = _rms(k3[:, hk * B_DIM:(hk + 1) * B_DIM], kn)
        k = k * cos_k + swap(k) * sin_k
        v = v3[:, hk * B_DIM:(hk + 1) * B_DIM]
        heads = [hk * grp + g for g in range(grp)]
        q = _rms(jnp.concatenate([q_t[:, hq * B_DIM:(hq + 1) * B_DIM] for hq in heads], axis=0), qn)
        q = q * cos_g + swap(q) * sin_g
        sink = jnp.concatenate(
            [jnp.broadcast_to(jnp.sum(jnp.where(lane == hq, pb[2:3, :], 0.0), axis=1, keepdims=True), (w, 1))
             for hq in heads], axis=0)
        s = _mm_nt(q, k) * (B_DIM ** -0.5)
        s = jnp.where(mask, s, -jnp.inf)
        m = jnp.maximum(jnp.max(s, axis=1, keepdims=True), sink)
        p = jnp.exp(s - m)
        p = p / (jnp.sum(p, axis=1, keepdims=True) + jnp.exp(sink - m))
        o = _mm(p, v)
        outs += [o[g * w:(g + 1) * w, :] for g in range(grp)]
    return jnp.concatenate(outs, axis=1) * _silu(z_t)


def _b_specs(s):
    nb = s // WINDOW
    qsp = pl.BlockSpec((WINDOW, 512), lambda n: (n, P_QB // 512))
    zsp = pl.BlockSpec((WINDOW, 512), lambda n: (n, P_ZB // 512))

    def three(col, width):
        return [pl.BlockSpec((WINDOW, width), lambda n: (jnp.maximum(n - 1, 0), col)),
                pl.BlockSpec((WINDOW, width), lambda n: (n, col)),
                pl.BlockSpec((WINDOW, width), lambda n: (jnp.minimum(n + 1, nb - 1), col))]

    tab = pl.BlockSpec((WINDOW, B_DIM), lambda n: (n, 0))
    small = pl.BlockSpec((8, LANE), lambda n: (0, 0))
    specs = [qsp, zsp] + three(P_KB // LANE, LANE) + three(P_VB // LANE, LANE) + [tab, tab] + three(0, B_DIM) + three(0, B_DIM) + [small]
    return nb, specs


def _b_args(proj, cos2, sin2, pb):
    return (proj, proj, proj, proj, proj, proj, proj, proj, cos2, sin2, cos2, cos2, cos2, sin2, sin2, sin2, pb)


def _b_load(refs):
    (q_ref, z_ref, kp, kc, kx, vp, vc, vx, cq, sq, ckp, ckc, ckx, skp, skc, skx, pb_ref) = refs
    cat = lambda *r: jnp.concatenate([t[...] for t in r], axis=0)
    return (q_ref[...], z_ref[...], cat(kp, kc, kx), cat(vp, vc, vx), cq[...], sq[...], cat(ckp, ckc, ckx),
            cat(skp, skc, skx), pb_ref[...])


def _attn_b_fwd(proj, cos2, sin2, pb, mixed):
    s = proj.shape[0]
    nb, specs = _b_specs(s)

    def body(*refs):
        o_ref = refs[-1]
        args = _b_load(refs[:-2])
        o_ref[...] = _b_block(*args, pl.program_id(0), nb).astype(BF16)

    return pl.pallas_call(
        body, name="attn_b_fwd", grid=(nb,), in_specs=specs + [pl.BlockSpec(memory_space=pl.ANY)],
        out_specs=pl.BlockSpec((WINDOW, 512), lambda n: (n, A_WIDTH // 512)),
        out_shape=jax.ShapeDtypeStruct(mixed.shape, mixed.dtype), input_output_aliases={len(specs): 0},
        compiler_params=_params(("parallel",)),
    )(*_b_args(proj, cos2, sin2, pb), mixed)


def _attn_b_bwd(proj, cos2, sin2, pb, d_mixed):
    s = proj.shape[0]
    nb, specs = _b_specs(s)
    w = WINDOW

    def body(*refs):
        dm_ref, dq_ref, dz_ref, dk_ref, dv_ref, dpb_ref = refs[-6:]
        n = pl.program_id(0)
        q_t, z_t, k3, v3, cq, sq, ck, sk, pb_v = _b_load(refs[:-6])

        @pl.when(n == 0)
        def _():
            dk_ref[...] = jnp.zeros_like(dk_ref)
            dv_ref[...] = jnp.zeros_like(dv_ref)
            dpb_ref[...] = jnp.zeros_like(dpb_ref)

        def f(q_, z_, k_, v_, pb_):
            return _b_block(q_, z_, k_, v_, cq, sq, ck, sk, pb_, n, nb)

        _, vjp = jax.vjp(f, q_t, z_t, k3, v3, pb_v)
        dq, dz, dk3, dv3, dpb = vjp(dm_ref[...])
        dq_ref[...] = dq
        dz_ref[...] = dz
        dpb_ref[...] += dpb

        def add(j, cond):
            @pl.when(cond)
            def _():
                rows = pl.ds(pl.multiple_of((n - 1 + j) * w, w), w)
                dk_ref[rows, :] += dk3[j * w:(j + 1) * w, :]
                dv_ref[rows, :] += dv3[j * w:(j + 1) * w, :]

        add(0, n > 0)
        add(1, n >= 0)
        add(2, n < nb - 1)

    blk = pl.BlockSpec((w, 512), lambda n: (n, 0))
    whole = pl.BlockSpec((s, LANE), lambda n: (0, 0))
    small = pl.BlockSpec((8, LANE), lambda n: (0, 0))
    return pl.pallas_call(
        body, name="attn_b_bwd", grid=(nb,),
        in_specs=specs + [pl.BlockSpec((w, 512), lambda n: (n, 2))],
        out_specs=[blk, blk, whole, whole, small],
        out_shape=[jax.ShapeDtypeStruct((s, 512), F32), jax.ShapeDtypeStruct((s, 512), F32),
                   jax.ShapeDtypeStruct((s, LANE), F32), jax.ShapeDtypeStruct((s, LANE), F32),
                   jax.ShapeDtypeStruct((8, LANE), F32)],
        compiler_params=_params(("arbitrary",)),
    )(*_b_args(proj, cos2, sin2, pb), d_mixed)


def _mem_kv_fwd(mem, mem_norm_w, w_kv):
    def body(mem_ref, nw_ref, w_ref, kv_ref):
        mn = _rms(mem_ref[...], nw_ref[...]).astype(BF16)
        kv_ref[...] = jnp.dot(mn, w_ref[...], preferred_element_type=F32)

    return pl.pallas_call(
        body, name="mem_kv_fwd", out_shape=jax.ShapeDtypeStruct((MEM_LEN, 2 * C_HEADS * C_DIM), F32),
        compiler_params=_params(),
    )(mem, mem_norm_w, w_kv)


def _mem_kv_bwd(mem, mem_norm_w, w_kv, d_kv):
    def body(mem_ref, nw_ref, w_ref, g_ref, gw_ref, gn_ref):
        mn, vjp = jax.vjp(_rms, mem_ref[...], nw_ref[...])
        g = g_ref[...].astype(BF16)
        gw_ref[...] = lax.dot_general(mn.astype(BF16), g, (((0,), (0,)), ((), ())), preferred_element_type=F32)
        d_mn = lax.dot_general(g, w_ref[...], (((1,), (1,)), ((), ())), preferred_element_type=F32)
        gn_ref[...] = vjp(d_mn)[1]

    return pl.pallas_call(
        body, name="mem_kv_bwd",
        out_shape=[jax.ShapeDtypeStruct((D_MODEL, 2 * C_HEADS * C_DIM), F32), jax.ShapeDtypeStruct((1, D_MODEL), F32)],
        compiler_params=_params(),
    )(mem, mem_norm_w, w_kv, d_kv)


def _c_tile(q_t, z_t, kvm, pc):
    width = C_HEADS * C_DIM
    outs = []
    for h in range(C_HEADS):
        q = _rms(q_t[:, h * C_DIM:(h + 1) * C_DIM], pc[0:1, :])
        k = _rms(kvm[:, h * C_DIM:(h + 1) * C_DIM], pc[1:2, :])
        v = kvm[:, width + h * C_DIM:width + (h + 1) * C_DIM]
        s = _mm_nt(q, k) * (C_DIM ** -0.5)
        p = jnp.exp(s - jnp.max(s, axis=1, keepdims=True))
        p = p / jnp.sum(p, axis=1, keepdims=True)
        outs.append(_mm(p, v))
    return jnp.concatenate(outs, axis=1) * _silu(z_t)


def _attn_c_fwd(proj, kvm, pc, mixed, tq=256):
    s = proj.shape[0]

    def body(q_ref, z_ref, kv_ref, pc_ref, mixed_ref, o_ref):
        o_ref[...] = _c_tile(q_ref[...], z_ref[...], kv_ref[...], pc_ref[...]).astype(BF16)

    return pl.pallas_call(
        body, name="attn_c_fwd", grid=(s // tq,),
        in_specs=[pl.BlockSpec((tq, 512), lambda i: (i, P_QC // 512)), pl.BlockSpec((tq, 512), lambda i: (i, P_ZC // 512)),
                  pl.BlockSpec(kvm.shape, lambda i: (0, 0)), pl.BlockSpec((8, LANE), lambda i: (0, 0)),
                  pl.BlockSpec(memory_space=pl.ANY)],
        out_specs=pl.BlockSpec((tq, 512), lambda i: (i, (A_WIDTH + 512) // 512)),
        out_shape=jax.ShapeDtypeStruct(mixed.shape, mixed.dtype), input_output_aliases={4: 0},
        compiler_params=_params(("parallel",)),
    )(proj, proj, kvm, pc, mixed)


def _attn_c_bwd(proj, kvm, pc, d_mixed, tq=256):
    s = proj.shape[0]

    def body(q_ref, z_ref, kv_ref, pc_ref, dm_ref, dq_ref, dz_ref, dkv_ref, dpc_ref):
        @pl.when(pl.program_id(0) == 0)
        def _():
            dkv_ref[...] = jnp.zeros_like(dkv_ref)
            dpc_ref[...] = jnp.zeros_like(dpc_ref)

        _, vjp = jax.vjp(_c_tile, q_ref[...], z_ref[...], kv_ref[...], pc_ref[...])
        dq, dz, dkv, dpc = vjp(dm_ref[...])
        dq_ref[...] = dq
        dz_ref[...] = dz
        dkv_ref[...] += dkv
        dpc_ref[...] += dpc

    blk = pl.BlockSpec((tq, 512), lambda i: (i, 0))
    kvs = pl.BlockSpec(kvm.shape, lambda i: (0, 0))
    small = pl.BlockSpec((8, LANE), lambda i: (0, 0))
    return pl.pallas_call(
        body, name="attn_c_bwd", grid=(s // tq,),
        in_specs=[pl.BlockSpec((tq, 512), lambda i: (i, P_QC // 512)), pl.BlockSpec((tq, 512), lambda i: (i, P_ZC // 512)),
                  kvs, small, pl.BlockSpec((tq, 512), lambda i: (i, 3))],
        out_specs=[blk, blk, kvs, small],
        out_shape=[jax.ShapeDtypeStruct((s, 512), F32), jax.ShapeDtypeStruct((s, 512), F32),
                   jax.ShapeDtypeStruct(kvm.shape, F32), jax.ShapeDtypeStruct((8, LANE), F32)],
        compiler_params=_params(("arbitrary",)),
    )(proj, proj, kvm, pc, d_mixed)


def _pad_row(v, width=LANE):
    v = v.reshape(1, -1)
    return jnp.pad(v, ((0, 0), (0, width - v.shape[1])))


def _local_step(x, mem, target, norm_w, w_perm, conv_w, pa, pb, pc, mem_norm_w, w_kv, w_out):
    s = x.shape[0]
    cos2, sin2 = _rope_tables(s)
    hn = _rms_fwd(x, norm_w)
    wide = dict(tm=1024, tn=512, tk=2048)
    proj = _matmul(hn, w_perm, "nn", F32, "mm_proj", **wide)
    cqkv = _conv_fwd(proj, conv_w)
    mixed, o_sum, states = _delta_fwd(cqkv, proj, pa)
    mixed = _attn_b_fwd(proj, cos2, sin2, pb, mixed)
    kvm = _mem_kv_fwd(mem, mem_norm_w, w_kv)
    mixed = _attn_c_fwd(proj, kvm, pc, mixed)
    mo = _matmul(mixed, w_out, "nn", F32, "mm_out", **wide)
    dy, dyb, loss_parts = _loss_dy(x, mo, target)

    d_mixed = _matmul(dyb, w_out, "nt", F32, "mm_dmixed", **wide)
    g_w_out = _matmul(mixed, dyb, "tn", F32, "mm_gwout", **wide)
    d_qc, d_zc, d_kvm, d_pc = _attn_c_bwd(proj, kvm, pc, d_mixed)
    g_w_kv, g_mem_norm = _mem_kv_bwd(mem, mem_norm_w, w_kv, d_kvm)
    d_qb, d_zb, d_kb, d_vb, d_pb = _attn_b_bwd(proj, cos2, sin2, pb, d_mixed)
    d_o, d_za, d_pa_out = _delta_out_bwd(o_sum, proj, pa, d_mixed)
    d_c, d_gt, d_pa_scan = _delta_bwd(cqkv, proj, pa, d_o, states)
    d_pa = d_pa_out + d_pa_scan
    d_qkv, g_conv = _conv_bwd(proj, conv_w, d_c)
    d_proj = jnp.concatenate([d_qkv, d_za, d_qb, d_zb, d_qc, d_zc, d_kb, d_vb, d_gt,
                              jnp.zeros((s, P_WIDTH - P_GT - LANE), F32)], axis=1).astype(BF16)
    d_hn = _matmul(d_proj, w_perm, "nt", F32, "mm_dhn", tm=1024, tn=2048, tk=512)
    g_w_perm = _matmul(hn, d_proj, "tn", F32, "mm_gwin", **wide)
    g_x, g_norm = _rms_bwd(x, norm_w, d_hn, dy)
    return dict(loss_parts=loss_parts, g_x=g_x, g_norm=g_norm, g_w_perm=g_w_perm, g_conv=g_conv, d_pa=d_pa,
                d_pb=d_pb, d_pc=d_pc, g_mem_norm=g_mem_norm, g_w_kv=g_w_kv, g_w_out=g_w_out)


def _permute_cols(w):
    pad = jnp.zeros((w.shape[0], P_WIDTH - IN_WIDTH), w.dtype)
    return jnp.concatenate([w[:, :O_GT], w[:, O_QB:O_KB], w[:, O_ZB:O_QC], w[:, O_QC:O_ZC], w[:, O_ZC:IN_WIDTH],
                            w[:, O_KB:O_VB], w[:, O_VB:O_ZB], w[:, O_GT:O_QB], pad], axis=1)


def _unpermute_cols(g):
    return jnp.concatenate([g[:, :P_QB], g[:, P_GT:P_GT + 32], g[:, P_QB:P_ZB], g[:, P_KB:P_VB], g[:, P_VB:P_GT],
                            g[:, P_ZB:P_QC], g[:, P_QC:P_ZC], g[:, P_ZC:P_KB]], axis=1)


_SEGMENTS = ((0, O_GT, 0), (O_GT, O_QB, P_GT), (O_QB, O_KB, P_QB), (O_KB, O_VB, P_KB), (O_VB, O_ZB, P_VB),
             (O_ZB, O_QC, P_ZB), (O_QC, O_ZC, P_QC), (O_ZC, IN_WIDTH, P_ZC))


def _permute_blocks(w4):
    parts = []
    for first, end, _ in sorted(_SEGMENTS, key=lambda seg: seg[2]):
        col = first
        while col < end:
            k = col // W_IN_BLOCK
            stop = min(end, (k + 1) * W_IN_BLOCK)
            parts.append(w4[k][:, col - k * W_IN_BLOCK:stop - k * W_IN_BLOCK])
            col = stop
    parts.append(jnp.zeros((w4.shape[1], P_WIDTH - IN_WIDTH), w4.dtype))
    return jnp.concatenate(parts, axis=1)


def _unpermute_blocks(g):
    blocks = []
    for k in range(N_CHIPS):
        lo, hi = k * W_IN_BLOCK, (k + 1) * W_IN_BLOCK
        parts = [g[:, p + max(first, lo) - first:p + min(end, hi) - first]
                 for first, end, p in _SEGMENTS if max(first, lo) < min(end, hi)]
        blocks.append(jnp.concatenate(parts, axis=1))
    return jnp.stack(blocks, axis=0)


HBM = pl.BlockSpec(memory_space=pltpu.HBM)


def _place():
    x, y, c = lax.axis_index("x"), lax.axis_index("y"), lax.axis_index("c")
    chips = [(1 - x, y), (x, 1 - y), (1 - x, 1 - y)]
    return x, y, c, 2 * x + y, chips, [2 * cx + cy for cx, cy in chips]


COPY_CHUNK_ROWS = 128
COPY_MAX_CHUNKS = 8
PAIR_PIECE_ROWS = 256


class _Copies:
    def __init__(self, make, src, dst):
        rows = src.shape[-2]
        n = max(1, min(COPY_MAX_CHUNKS, rows // COPY_CHUNK_ROWS))
        assert rows % n == 0
        step = rows // n
        lead = (slice(None),) * (len(src.shape) - 2)
        self.whole = make(src, dst)
        self.parts = [self.whole] if n == 1 else [
            make(src.at[lead + (pl.ds(i * step, step), slice(None))], dst.at[lead + (pl.ds(i * step, step), slice(None))])
            for i in range(n)]

    def start(self):
        for p in self.parts:
            p.start()

    def wait(self):
        self.whole.wait()

    def wait_send(self):
        self.whole.wait_send()

    def wait_recv(self):
        self.whole.wait_recv()


def _remote(src, dst, send_sems, recv_sems, k, to):
    def make(s, d):
        return pltpu.make_async_remote_copy(src_ref=s, dst_ref=d, send_sem=send_sems.at[k], recv_sem=recv_sems.at[k],
                                            device_id=to, device_id_type=MESH)
    return _Copies(make, src, dst)


def _local(src, dst, sem):
    return _Copies(lambda s, d: pltpu.make_async_copy(s, d, sem), src, dst)


def _half_rows(ref, c):
    half = ref.shape[-2] // 2
    return pl.ds(pl.multiple_of(c * half, 8), half)


def _all_gather_weights(w_in_b, w_out_b, w_kv_b, conv_b):
    bigs = (w_in_b, w_out_b, w_kv_b)
    n_big = len(bigs)

    def body(*refs):
        srcs, conv_src = refs[:n_big], refs[n_big]
        dsts, conv_dst = refs[n_big + 1:2 * n_big + 1], refs[2 * n_big + 1]
        send_sems, recv_sems, local_sems = refs[2 * n_big + 2:]
        x, y, c, me, chips, chip_ids = _place()
        sibling = (x, y, 1 - c)
        local = [_local(src, dst.at[me], local_sems.at[a]) for a, (src, dst) in enumerate(zip(srcs, dsts))]
        local.append(_local(conv_src, conv_dst.at[me], local_sems.at[n_big]))
        for cp in local:
            cp.start()
        sends = []
        for a, (src, dst) in enumerate(zip(srcs, dsts)):
            mine = _half_rows(src, c)
            for j, chip in enumerate(chips):
                sends.append(_remote(src.at[mine, :], dst.at[me, mine, :], send_sems, recv_sems, 6 * a + j, (*chip, c)))
        for j, chip in enumerate(chips):
            sends.append(_remote(conv_src, conv_dst.at[me], send_sems, recv_sems, 6 * n_big + j, (*chip, c)))
        for cp in sends:
            cp.start()
        passed = []
        for a, (src, dst) in enumerate(zip(srcs, dsts)):
            mine = _half_rows(src, c)
            for j, cid in enumerate(chip_ids):
                landed = dst.at[cid, mine, :]
                _remote(landed, landed, send_sems, recv_sems, 6 * a + j, sibling).wait_recv()
                cp = _remote(landed, landed, send_sems, recv_sems, 6 * a + 3 + j, sibling)
                cp.start()
                passed.append(cp)
        for a, (src, dst) in enumerate(zip(srcs, dsts)):
            other = _half_rows(src, 1 - c)
            for j, cid in enumerate(chip_ids):
                landed = dst.at[cid, other, :]
                _remote(landed, landed, send_sems, recv_sems, 6 * a + 3 + j, sibling).wait_recv()
        for j, cid in enumerate(chip_ids):
            _remote(conv_src, conv_dst.at[cid], send_sems, recv_sems, 6 * n_big + j, sibling).wait_recv()
        for cp in sends + passed:
            cp.wait_send()
        for cp in local:
            cp.wait()

    n_sem = 6 * n_big + 3
    return pl.pallas_call(
        body, name="all_gather_weights",
        out_shape=[jax.ShapeDtypeStruct((N_CHIPS,) + w.shape, w.dtype) for w in bigs + (conv_b,)],
        in_specs=[pl.BlockSpec(memory_space=pltpu.VMEM)] * (n_big + 1), out_specs=[HBM] * (n_big + 1),
        scratch_shapes=[pltpu.SemaphoreType.DMA((n_sem,)), pltpu.SemaphoreType.DMA((n_sem,)),
                        pltpu.SemaphoreType.DMA((n_big + 1,))],
        compiler_params=_params(),
    )(*bigs, conv_b)


def _pair_exchange(grads):
    n = len(grads)
    piece = PAIR_PIECE_ROWS

    def body(*refs):
        srcs, gots = refs[:n], refs[n:2 * n]
        stages = refs[2 * n:3 * n]
        send_sems, recv_sems, load_sems = refs[3 * n:]
        x, y, c, _, _, _ = _place()
        sibling = (x, y, 1 - c)
        for a in range(n):
            slabs, half, _ = gots[a].shape
            per_slab = half // piece
            first = (1 - c) * half
            loads, sends = [], []
            for i in range(slabs * per_slab):
                k, r, slot = i // per_slab, i % per_slab, i % 2
                rows = pl.ds(pl.multiple_of(first + r * piece, 8), piece)
                loads.append(pltpu.make_async_copy(srcs[a].at[k, rows, :], stages[a].at[slot], load_sems.at[2 * a + slot]))
                sends.append(pltpu.make_async_remote_copy(
                    src_ref=stages[a].at[slot], dst_ref=gots[a].at[k, pl.ds(r * piece, piece), :],
                    send_sem=send_sems.at[2 * a + slot], recv_sem=recv_sems.at[a], device_id=sibling, device_id_type=MESH))
            loads[0].start()
            for i in range(len(loads)):
                loads[i].wait()
                sends[i].start()
                if i + 1 < len(loads):
                    if i >= 1:
                        sends[i - 1].wait_send()
                    loads[i + 1].start()
            for cp in sends[-2:]:
                cp.wait_send()
        for a in range(n):
            whole = srcs[a].at[:, _half_rows(srcs[a], c), :]
            pltpu.make_async_remote_copy(src_ref=whole, dst_ref=gots[a], send_sem=send_sems.at[2 * a],
                                         recv_sem=recv_sems.at[a], device_id=sibling, device_id_type=MESH).wait_recv()

    halves = [jax.ShapeDtypeStruct((g.shape[0], g.shape[1] // 2, g.shape[2]), g.dtype) for g in grads]
    assert all(h.shape[1] % piece == 0 and (h.shape[0] * h.shape[1] // piece) >= 2 for h in halves)
    return pl.pallas_call(
        body, name="grad_pair_exchange", out_shape=halves, in_specs=[HBM] * n, out_specs=[HBM] * n,
        scratch_shapes=[pltpu.VMEM((2, piece, g.shape[2]), g.dtype) for g in grads]
        + [pltpu.SemaphoreType.DMA((2 * n,)), pltpu.SemaphoreType.DMA((n,)), pltpu.SemaphoreType.DMA((2 * n,))],
        compiler_params=_params(),
    )(*grads)


def _chip_exchange(halves):
    n = len(halves)

    def body(*refs):
        srcs, lands = refs[:n], refs[n:2 * n]
        send_sems, recv_sems = refs[2 * n:]
        x, y, c, me, chips, chip_ids = _place()
        gives = []
        for a in range(n):
            for j, (chip, cid) in enumerate(zip(chips, chip_ids)):
                give = _remote(srcs[a].at[cid], lands[a].at[j], send_sems, recv_sems, 3 * a + j, (*chip, c))
                give.start()
                gives.append(give)
        for a in range(n):
            for j, cid in enumerate(chip_ids):
                _remote(srcs[a].at[cid], lands[a].at[j], send_sems, recv_sems, 3 * a + j, (x, y, c)).wait_recv()
        for give in gives:
            give.wait_send()

    return pl.pallas_call(
        body, name="grad_chip_exchange",
        out_shape=[jax.ShapeDtypeStruct((N_CHIPS - 1,) + h.shape[1:], h.dtype) for h in halves],
        in_specs=[HBM] * n, out_specs=[HBM] * n,
        scratch_shapes=[pltpu.SemaphoreType.DMA((3 * n,)), pltpu.SemaphoreType.DMA((3 * n,))],
    )(*halves)


def _pair_gather(halves):
    n = len(halves)

    def body(*refs):
        srcs, fulls = refs[:n], refs[n:2 * n]
        send_sems, recv_sems, local_sems = refs[2 * n:]
        x, y, c, _, _, _ = _place()
        copies = []
        for a in range(n):
            mine = _half_rows(fulls[a], c)
            keep = _local(srcs[a], fulls[a].at[mine, :], local_sems.at[a])
            keep.start()
            give = _remote(srcs[a], fulls[a].at[mine, :], send_sems, recv_sems, a, (x, y, 1 - c))
            give.start()
            copies += [keep, give]
        for a in range(n):
            other = _half_rows(fulls[a], 1 - c)
            copies[2 * a].wait()
            copies[2 * a + 1].wait_send()
            _remote(srcs[a], fulls[a].at[other, :], send_sems, recv_sems, a, (x, y, 1 - c)).wait_recv()

    return pl.pallas_call(
        body, name="grad_pair_gather",
        out_shape=[jax.ShapeDtypeStruct((2 * h.shape[0], h.shape[1]), h.dtype) for h in halves],
        in_specs=[pl.BlockSpec(memory_space=pltpu.VMEM)] * n, out_specs=[HBM] * n,
        scratch_shapes=[pltpu.SemaphoreType.DMA((n,)), pltpu.SemaphoreType.DMA((n,)), pltpu.SemaphoreType.DMA((n,))],
    )(*halves)


def _all_reduce_small(p):
    n_dev = 8

    def body(p_ref, o_ref, land, send_sems, recv_sems):
        x, y, c = lax.axis_index("x"), lax.axis_index("y"), lax.axis_index("c")
        me = 4 * x + 2 * y + c
        land[me] = p_ref[...]
        sends = []
        for k in range(1, n_dev):
            fx, fy, fc = (k >> 2) & 1, (k >> 1) & 1, k & 1
            to = (x ^ fx, y ^ fy, c ^ fc)
            cp = _remote(p_ref, land.at[me], send_sems, recv_sems, k - 1, to)
            cp.start()
            sends.append(cp)
        for k in range(1, n_dev):
            _remote(p_ref, land.at[me ^ k], send_sems, recv_sems, k - 1, (x, y, c)).wait_recv()
        total = land[0]
        for d in range(1, n_dev):
            total = total + land[d]
        o_ref[...] = total
        for cp in sends:
            cp.wait_send()

    vm = pl.BlockSpec(memory_space=pltpu.VMEM)
    return pl.pallas_call(
        body, name="all_reduce_small", out_shape=jax.ShapeDtypeStruct(p.shape, p.dtype), in_specs=[vm], out_specs=vm,
        scratch_shapes=[pltpu.VMEM((n_dev,) + p.shape, p.dtype), pltpu.SemaphoreType.DMA((n_dev - 1,)),
                        pltpu.SemaphoreType.DMA((n_dev - 1,))],
    )(p)


def _row_tile(rows, cap=256):
    return cap if rows % cap == 0 else rows


def _pair_sum(full, got, core, name):
    n, r, c = got.shape
    tr = _row_tile(r)
    nt = r // tr

    def body(core_ref, a_ref, b_ref, o_ref):
        o_ref[...] = (a_ref[...] + b_ref[...]).astype(BF16)

    blk = pl.BlockSpec((None, tr, c), lambda i, j, core_ref: (i, j, 0))
    grid_spec = pltpu.PrefetchScalarGridSpec(
        num_scalar_prefetch=1, grid=(n, nt),
        in_specs=[pl.BlockSpec((None, tr, c), lambda i, j, core_ref: (i, core_ref[0] * nt + j, 0)), blk], out_specs=blk)
    return pl.pallas_call(body, name=name, grid_spec=grid_spec, out_shape=jax.ShapeDtypeStruct(got.shape, BF16),
                          compiler_params=_params(("parallel", "parallel")))(core, full, got)


def _chip_sum(full, got, land, place, name):
    n, r, c = land.shape
    tr = _row_tile(r)
    nt = r // tr

    def body(place_ref, a_ref, b_ref, l_ref, o_ref):
        total = a_ref[...] + b_ref[...]
        for j in range(n):
            total = total + l_ref[j].astype(F32)
        o_ref[...] = total

    grid_spec = pltpu.PrefetchScalarGridSpec(
        num_scalar_prefetch=1, grid=(nt,),
        in_specs=[pl.BlockSpec((None, tr, c), lambda i, p: (p[0], p[1] * nt + i, 0)),
                  pl.BlockSpec((None, tr, c), lambda i, p: (p[0], i, 0)),
                  pl.BlockSpec((n, tr, c), lambda i, p: (0, i, 0))],
        out_specs=pl.BlockSpec((tr, c), lambda i, p: (i, 0)))
    return pl.pallas_call(body, name=name, grid_spec=grid_spec, out_shape=jax.ShapeDtypeStruct((r, c), F32),
                          compiler_params=_params(("parallel",)))(place, full, got, land)


def _adamw(w, g, m, v, name):
    r, c = w.shape
    tr = _row_tile(r)

    def body(w_ref, g_ref, m_ref, v_ref, d_ref, mo_ref, vo_ref):
        g_ = g_ref[...]
        m2 = ADAM_B1 * m_ref[...] + (1.0 - ADAM_B1) * g_
        v2 = ADAM_B2 * v_ref[...] + (1.0 - ADAM_B2) * jnp.square(g_)
        m_hat = m2 / (1.0 - ADAM_B1 ** ADAM_STEP)
        v_hat = v2 / (1.0 - ADAM_B2 ** ADAM_STEP)
        d_ref[...] = -ADAM_LR * (m_hat / (jnp.sqrt(v_hat) + ADAM_EPS) + ADAM_WD * w_ref[...])
        mo_ref[...] = m2
        vo_ref[...] = v2

    blk = pl.BlockSpec((tr, c), lambda i: (i, 0))
    return pl.pallas_call(body, name=name, grid=(r // tr,), in_specs=[blk] * 4, out_specs=[blk] * 3,
                          out_shape=[jax.ShapeDtypeStruct(w.shape, F32)] * 3, compiler_params=_params(("parallel",)))(w, g, m, v)


SMALL_NAMES = ("norm_w", "mem_norm_w", "o_norm_a", "q_norm_c", "k_norm_c", "q_norm_b", "k_norm_b",
               "a_log_fwd", "a_log_bwd", "dt_bias_fwd", "dt_bias_bwd", "sink_b")
SMALL_SIZES = (2048, 2048, 128, 128, 128, 64, 64, 8, 8, 8, 8, 8)
SMALL_LOSS = sum(SMALL_SIZES)
SMALL_CONV = 5120
SMALL_TOTAL = SMALL_CONV + CONV_K * 3 * A_WIDTH
SMALL_ROWS = SMALL_TOTAL // LANE


def _pack_small(parts, extra=None, conv=None):
    vec = [parts[n].reshape(-1) for n in SMALL_NAMES]
    vec.append(jnp.zeros((1,), F32) if extra is None else extra.reshape(1))
    vec.append(jnp.zeros((SMALL_CONV - SMALL_LOSS - 1,), F32))
    vec.append(jnp.zeros((SMALL_TOTAL - SMALL_CONV,), F32) if conv is None else conv.reshape(-1))
    return jnp.concatenate(vec).reshape(SMALL_ROWS, LANE)


def _unpack_small(packed):
    flat = packed.reshape(-1)
    out, off = {}, 0
    for n, size in zip(SMALL_NAMES, SMALL_SIZES):
        out[n] = flat[off:off + size].reshape(1, size)
        off += size
    return out


WEIGHT_ORDER = ("norm_w", "w_in", "conv_w_a", "a_log_fwd", "a_log_bwd", "dt_bias_fwd", "dt_bias_bwd", "o_norm_a",
                "q_norm_b", "k_norm_b", "sink_b", "mem_norm_w", "w_mem_kv", "q_norm_c", "k_norm_c", "w_out")


def kernel(x, mem, norm_w, w_in, conv_w_a, a_log_fwd, a_log_bwd, dt_bias_fwd, dt_bias_bwd, o_norm_a, q_norm_b, k_norm_b, sink_b, mem_norm_w, w_mem_kv, q_norm_c, k_norm_c, w_out, loss_target, m_norm_w, m_w_in, m_conv_w_a, m_a_log_fwd, m_a_log_bwd, m_dt_bias_fwd, m_dt_bias_bwd, m_o_norm_a, m_q_norm_b, m_k_norm_b, m_sink_b, m_mem_norm_w, m_w_mem_kv, m_q_norm_c, m_k_norm_c, m_w_out, v_norm_w, v_w_in, v_conv_w_a, v_a_log_fwd, v_a_log_bwd, v_dt_bias_fwd, v_dt_bias_bwd, v_o_norm_a, v_q_norm_b, v_k_norm_b, v_sink_b, v_mem_norm_w, v_w_mem_kv, v_q_norm_c, v_k_norm_c, v_w_out):
    weights = dict(norm_w=norm_w, w_in=w_in, conv_w_a=conv_w_a, a_log_fwd=a_log_fwd, a_log_bwd=a_log_bwd,
                   dt_bias_fwd=dt_bias_fwd, dt_bias_bwd=dt_bias_bwd, o_norm_a=o_norm_a, q_norm_b=q_norm_b,
                   k_norm_b=k_norm_b, sink_b=sink_b, mem_norm_w=mem_norm_w, w_mem_kv=w_mem_kv, q_norm_c=q_norm_c,
                   k_norm_c=k_norm_c, w_out=w_out)
    mom1 = dict(norm_w=m_norm_w, w_in=m_w_in, conv_w_a=m_conv_w_a, a_log_fwd=m_a_log_fwd, a_log_bwd=m_a_log_bwd,
                dt_bias_fwd=m_dt_bias_fwd, dt_bias_bwd=m_dt_bias_bwd, o_norm_a=m_o_norm_a, q_norm_b=m_q_norm_b,
                k_norm_b=m_k_norm_b, sink_b=m_sink_b, mem_norm_w=m_mem_norm_w, w_mem_kv=m_w_mem_kv,
                q_norm_c=m_q_norm_c, k_norm_c=m_k_norm_c, w_out=m_w_out)
    mom2 = dict(norm_w=v_norm_w, w_in=v_w_in, conv_w_a=v_conv_w_a, a_log_fwd=v_a_log_fwd, a_log_bwd=v_a_log_bwd,
                dt_bias_fwd=v_dt_bias_fwd, dt_bias_bwd=v_dt_bias_bwd, o_norm_a=v_o_norm_a, q_norm_b=v_q_norm_b,
                k_norm_b=v_k_norm_b, sink_b=v_sink_b, mem_norm_w=v_mem_norm_w, w_mem_kv=v_w_mem_kv,
                q_norm_c=v_q_norm_c, k_norm_c=v_k_norm_c, w_out=v_w_out)
    chip = 2 * lax.axis_index("x") + lax.axis_index("y")

    w_in4, w_out4, w_kv4, conv4 = _all_gather_weights(w_in[0].astype(BF16), w_out[0].astype(BF16),
                                                      w_mem_kv[0].astype(BF16), conv_w_a[0])
    w_perm = _permute_blocks(w_in4)
    w_out_full = w_out4.reshape(D_MODEL, D_MODEL)
    w_kv_full = w_kv4.reshape(D_MODEL, 2 * C_HEADS * C_DIM)
    conv_full = jnp.transpose(conv4, (1, 0, 2)).reshape(CONV_K, 3 * A_WIDTH)
    pa = jnp.concatenate([_pad_row(a_log_fwd), _pad_row(a_log_bwd), _pad_row(dt_bias_fwd), _pad_row(dt_bias_bwd),
                          _pad_row(o_norm_a), jnp.zeros((3, LANE), F32)], axis=0)
    pb = jnp.concatenate([_pad_row(q_norm_b), _pad_row(k_norm_b), _pad_row(sink_b), jnp.zeros((5, LANE), F32)], axis=0)
    pc = jnp.concatenate([_pad_row(q_norm_c), _pad_row(k_norm_c), jnp.zeros((6, LANE), F32)], axis=0)

    r = _local_step(x[0], mem[0], loss_target[0], norm_w, w_perm, conv_full, pa, pb, pc, mem_norm_w, w_kv_full,
                    w_out_full)

    g_in4 = _unpermute_blocks(r["g_w_perm"])
    g_out4 = r["g_w_out"].reshape(N_CHIPS, D_MODEL // N_CHIPS, D_MODEL)
    g_kv4 = r["g_w_kv"].reshape(N_CHIPS, D_MODEL // N_CHIPS, 2 * C_HEADS * C_DIM)
    full = [g_in4, g_out4, g_kv4]
    core = lax.axis_index("c").astype(jnp.int32).reshape(1)
    got = _pair_exchange(full)
    pair = [_pair_sum(a, b, core, "grad_pair_sum_%d" % i) for i, (a, b) in enumerate(zip(full, got))]
    lands = _chip_exchange(pair)
    place = jnp.stack([chip, lax.axis_index("c")]).astype(jnp.int32)
    reduced = [_chip_sum(a, b, l, place, "grad_chip_sum_%d" % i) for i, (a, b, l) in enumerate(zip(full, got, lands))]
    g_w_in, g_w_out, g_w_kv = _pair_gather(reduced)

    d_pa, d_pb, d_pc = r["d_pa"], r["d_pb"], r["d_pc"]
    small_g = dict(norm_w=r["g_norm"], mem_norm_w=r["g_mem_norm"], o_norm_a=d_pa[4], q_norm_c=d_pc[0], k_norm_c=d_pc[1],
                   q_norm_b=d_pb[0, :B_DIM], k_norm_b=d_pb[1, :B_DIM], a_log_fwd=d_pa[0, :A_HEADS],
                   a_log_bwd=d_pa[1, :A_HEADS], dt_bias_fwd=d_pa[2, :A_HEADS], dt_bias_bwd=d_pa[3, :A_HEADS],
                   sink_b=d_pb[2, :B_HEADS])
    packed = _all_reduce_small(_pack_small(small_g, jnp.sum(r["loss_parts"][:, 0, 0]), r["g_conv"]))
    flat = packed.reshape(-1)
    loss = flat[SMALL_LOSS]
    conv_sum = flat[SMALL_CONV:].reshape(CONV_K, 3 * A_WIDTH)
    conv_cols = 3 * A_WIDTH // N_CHIPS
    g_conv = lax.dynamic_slice(conv_sum, (0, chip * conv_cols), (CONV_K, conv_cols))

    grads = _unpack_small(packed)
    grads.update(w_in=g_w_in, w_mem_kv=g_w_kv, w_out=g_w_out, conv_w_a=g_conv)
    delta, new_m, new_v = {}, {}, {}
    for n in ("w_in", "w_mem_kv", "w_out", "conv_w_a"):
        delta[n], new_m[n], new_v[n] = _adamw(weights[n][0], grads[n], mom1[n][0], mom2[n][0], "adamw_" + n)
    d_s, m_s, v_s = _adamw(_pack_small(weights), packed, _pack_small(mom1), _pack_small(mom2), "adamw_small")
    d_s, m_s, v_s = _unpack_small(d_s), _unpack_small(m_s), _unpack_small(v_s)
    for n in SMALL_NAMES:
        delta[n], new_m[n], new_v[n] = d_s[n], m_s[n], v_s[n]

    def shaped(tree):
        return [tree[n].reshape(weights[n].shape) for n in WEIGHT_ORDER]

    return (loss, r["g_x"].reshape(x.shape), *shaped(grads), *shaped(delta), *shaped(new_m), *shaped(new_v))
```

```python
import functools

import jax
import jax.numpy as jnp
from jax import lax
from jax.experimental import pallas as pl
from jax.experimental.pallas import tpu as pltpu

F32 = jnp.float32
BF16 = jnp.bfloat16
HI = lax.Precision.HIGHEST
MESH = pl.DeviceIdType.MESH

D_MODEL = 2048
A_WIDTH = 1024
A_HEADS = 8
A_DIM = 128
CONV_K = 5
CHUNK = 64
B_HEADS = 8
B_KV = 2
B_DIM = 64
WINDOW = 128
C_HEADS = 4
C_DIM = 128
MEM_LEN = 256
ROPE_THETA = 10000.0
EPS = 1e-6
IN_WIDTH = 6432
N_CHIPS = 4
W_IN_BLOCK = IN_WIDTH // N_CHIPS

LANE = 128
P_QA, P_KA, P_VA, P_ZA = 0, 1024, 2048, 3072
P_QB, P_ZB, P_QC, P_ZC = 4096, 4608, 5120, 5632
P_KB, P_VB, P_GT = 6144, 6272, 6400
P_WIDTH = 6656
O_GT, O_QB, O_KB, O_VB, O_ZB, O_QC, O_ZC = 4096, 4128, 4640, 4768, 4896, 5408, 5920

ADAM_LR, ADAM_B1, ADAM_B2, ADAM_EPS, ADAM_WD, ADAM_STEP = 0.001, 0.9, 0.999, 1e-08, 0.01, 10

VMEM_LIMIT = 56 * 1024 * 1024


def _params(sem=None):
    return pltpu.CompilerParams(dimension_semantics=sem, vmem_limit_bytes=VMEM_LIMIT)


def _dot(a, b, dims=(((1,), (0,)), ((), ())), precision=HI):
    return lax.dot_general(a, b, dims, precision=precision, preferred_element_type=F32)


def _dot_nt(a, b, precision=HI):
    return _dot(a, b, (((1,), (1,)), ((), ())), precision)


def _dot_tn(a, b, precision=HI):
    return _dot(a, b, (((0,), (0,)), ((), ())), precision)


_NN = (((1,), (0,)), ((), ()))
_NT = (((1,), (1,)), ((), ()))
_TN = (((0,), (0,)), ((), ()))


def _bdot(a, b, dims):
    return lax.dot_general(a.astype(BF16), b.astype(BF16), dims, preferred_element_type=F32)


@jax.custom_vjp
def _mm(a, b):
    return _bdot(a, b, _NN)


_mm.defvjp(lambda a, b: (_bdot(a, b, _NN), (a, b)),
           lambda res, ct: (_bdot(ct, res[1], _NT), _bdot(res[0], ct, _TN)))


@jax.custom_vjp
def _mm_nt(a, b):
    return _bdot(a, b, _NT)


_mm_nt.defvjp(lambda a, b: (_bdot(a, b, _NT), (a, b)),
              lambda res, ct: (_bdot(ct, res[1], _NN), _bdot(ct, res[0], _TN)))


@jax.custom_vjp
def _mm_tn(a, b):
    return _bdot(a, b, _TN)


_mm_tn.defvjp(lambda a, b: (_bdot(a, b, _TN), (a, b)),
              lambda res, ct: (_bdot(res[1], ct, _NT), _bdot(res[0], ct, _NN)))


def _rms(t, w):
    return t * lax.rsqrt(jnp.mean(t * t, axis=-1, keepdims=True) + EPS) * w


def _l2(t):
    return t * lax.rsqrt(jnp.sum(t * t, axis=-1, keepdims=True) + EPS)


def _silu(t):
    return t * jax.nn.sigmoid(t)


def _softplus(t):
    return jnp.maximum(t, 0.0) + jnp.log1p(jnp.exp(-jnp.abs(t)))


def _matmul(a, b, mode, out_dtype, name, tm=512, tn=512, tk=512):
    (m, k) = a.shape[::-1] if mode == "tn" else a.shape
    n = b.shape[0] if mode == "nt" else b.shape[1]
    tm, tn, tk = min(tm, m), min(tn, n), min(tk, k)
    assert m % tm == 0 and n % tn == 0 and k % tk == 0, (m, n, k, tm, tn, tk)
    if mode == "nn":
        a_spec = pl.BlockSpec((tm, tk), lambda i, j, kk: (i, kk))
        b_spec = pl.BlockSpec((tk, tn), lambda i, j, kk: (kk, j))
        dims = (((1,), (0,)), ((), ()))
    elif mode == "nt":
        a_spec = pl.BlockSpec((tm, tk), lambda i, j, kk: (i, kk))
        b_spec = pl.BlockSpec((tn, tk), lambda i, j, kk: (j, kk))
        dims = (((1,), (1,)), ((), ()))
    else:
        a_spec = pl.BlockSpec((tk, tm), lambda i, j, kk: (kk, i))
        b_spec = pl.BlockSpec((tk, tn), lambda i, j, kk: (kk, j))
        dims = (((0,), (0,)), ((), ()))
    nk = k // tk

    def body_one(a_ref, b_ref, o_ref):
        o_ref[...] = _bdot(a_ref[...], b_ref[...], dims).astype(out_dtype)

    def body_acc(a_ref, b_ref, o_ref, acc_ref):
        kk = pl.program_id(2)

        @pl.when(kk == 0)
        def _():
            acc_ref[...] = jnp.zeros_like(acc_ref)

        acc_ref[...] += _bdot(a_ref[...], b_ref[...], dims)

        @pl.when(kk == nk - 1)
        def _():
            o_ref[...] = acc_ref[...].astype(out_dtype)

    return pl.pallas_call(
        body_one if nk == 1 else body_acc, name=name, grid=(m // tm, n // tn, nk),
        in_specs=[a_spec, b_spec], out_specs=pl.BlockSpec((tm, tn), lambda i, j, kk: (i, j)),
        out_shape=jax.ShapeDtypeStruct((m, n), out_dtype),
        scratch_shapes=[] if nk == 1 else [pltpu.VMEM((tm, tn), F32)],
        compiler_params=_params(("parallel", "parallel", "arbitrary")),
    )(a, b)


def _rms_fwd(x, w, tr=256):
    s, d = x.shape

    def body(x_ref, w_ref, o_ref):
        o_ref[...] = _rms(x_ref[...], w_ref[...]).astype(BF16)

    return pl.pallas_call(
        body, name="rms_fwd", grid=(s // tr,),
        in_specs=[pl.BlockSpec((tr, d), lambda i: (i, 0)), pl.BlockSpec((1, d), lambda i: (0, 0))],
        out_specs=pl.BlockSpec((tr, d), lambda i: (i, 0)),
        out_shape=jax.ShapeDtypeStruct((s, d), BF16), compiler_params=_params(("parallel",)),
    )(x, w)


def _rms_bwd(x, w, d_hn, dy, tr=256):
    s, d = x.shape

    def body(x_ref, w_ref, g_ref, dy_ref, gx_ref, gw_ref):
        _, vjp = jax.vjp(_rms, x_ref[...], w_ref[...])
        dx, dw = vjp(g_ref[...])
        gx_ref[...] = dy_ref[...] + dx

        @pl.when(pl.program_id(0) == 0)
        def _():
            gw_ref[...] = jnp.zeros_like(gw_ref)

        gw_ref[...] += dw

    row = pl.BlockSpec((tr, d), lambda i: (i, 0))
    vec = pl.BlockSpec((1, d), lambda i: (0, 0))
    return pl.pallas_call(
        body, name="rms_bwd", grid=(s // tr,), in_specs=[row, vec, row, row], out_specs=[row, vec],
        out_shape=[jax.ShapeDtypeStruct((s, d), F32), jax.ShapeDtypeStruct((1, d), F32)],
        compiler_params=_params(("arbitrary",)),
    )(x, w, d_hn, dy)


def _loss_dy(x, mo, target, tr=256):
    s, d = x.shape
    nt = s // tr

    def body(x_ref, mo_ref, t_ref, dy_ref, dyb_ref, l_ref):
        err = x_ref[...] + mo_ref[...] - t_ref[...]
        dy = err * (1.0 / d)
        dy_ref[...] = dy
        dyb_ref[...] = dy.astype(BF16)
        l_ref[...] = jnp.full(l_ref.shape, 0.5 * jnp.sum(jnp.sum(err * err, axis=1, keepdims=True) * (1.0 / d)), F32)

    row = pl.BlockSpec((tr, d), lambda i: (i, 0))
    return pl.pallas_call(
        body, name="loss_dy", grid=(nt,), in_specs=[row, row, row],
        out_specs=[row, row, pl.BlockSpec((1, 8, LANE), lambda i: (i, 0, 0))],
        out_shape=[jax.ShapeDtypeStruct((s, d), F32), jax.ShapeDtypeStruct((s, d), BF16),
                   jax.ShapeDtypeStruct((nt, 8, LANE), F32)],
        compiler_params=_params(("parallel",)),
    )(x, mo, target)


def _shift_rows(t, s):
    if s == 0:
        return t
    n = t.shape[0]
    rolled = pltpu.roll(t, (-s) % n, axis=0)
    idx = lax.broadcasted_iota(jnp.int32, t.shape, 0) + s
    return jnp.where((idx >= 0) & (idx < n), rolled, 0.0)


def _conv_fwd(proj, conv_w):
    s = proj.shape[0]
    nblk = 3 * A_WIDTH // LANE

    def body(x_ref, w_ref, o_ref):
        x = x_ref[...]
        acc = jnp.zeros_like(x)
        for j in range(CONV_K):
            acc = acc + w_ref[j:j + 1, :] * _shift_rows(x, j - CONV_K // 2)
        o_ref[...] = acc

    return pl.pallas_call(
        body, name="conv_fwd", grid=(nblk,),
        in_specs=[pl.BlockSpec((s, LANE), lambda i: (0, i)), pl.BlockSpec((CONV_K, LANE), lambda i: (0, i))],
        out_specs=pl.BlockSpec((None, s, LANE), lambda i: (i // A_HEADS, 0, i % A_HEADS)),
        out_shape=jax.ShapeDtypeStruct((3, s, A_WIDTH), F32), compiler_params=_params(("parallel",)),
    )(proj, conv_w)


def _conv_bwd(proj, conv_w, d_c):
    s = proj.shape[0]
    nblk = 3 * A_WIDTH // LANE

    def body(x_ref, w_ref, g_ref, dx_ref, dw_ref):
        x, g = x_ref[...], g_ref[...]
        acc = jnp.zeros_like(x)
        for j in range(CONV_K):
            off = j - CONV_K // 2
            acc = acc + w_ref[j:j + 1, :] * _shift_rows(g, -off)
            dw_ref[j:j + 1, :] = jnp.sum(_shift_rows(x, off) * g, axis=0, keepdims=True)
        dx_ref[...] = acc

    col = pl.BlockSpec((s, LANE), lambda i: (0, i))
    wsp = pl.BlockSpec((CONV_K, LANE), lambda i: (0, i))
    dsp = pl.BlockSpec((None, s, LANE), lambda i: (i // A_HEADS, 0, i % A_HEADS))
    return pl.pallas_call(
        body, name="conv_bwd", grid=(nblk,), in_specs=[col, wsp, dsp], out_specs=[col, wsp],
        out_shape=[jax.ShapeDtypeStruct((s, 3 * A_WIDTH), F32), jax.ShapeDtypeStruct((CONV_K, 3 * A_WIDTH), F32)],
        compiler_params=_params(("parallel",)),
    )(proj, conv_w, d_c)


A_FWD_HEADS = 4
A_BWD_HEADS = 4


def _a_chain(st, cq, ck, cv, alpha, beta_raw, a_log, dt_b, incl, strict, last):
    c = CHUNK
    eye = (lax.broadcasted_iota(jnp.int32, (c, c), 0) == lax.broadcasted_iota(jnp.int32, (c, c), 1)).astype(F32)
    gb = -jnp.exp(a_log) * _softplus(alpha + dt_b)
    bb = jax.nn.sigmoid(beta_raw)
    q = _l2(_silu(cq)) * (A_DIM ** -0.5)
    k = _l2(_silu(ck))
    v = _silu(cv)

    gc = _dot(incl, jnp.broadcast_to(gb, (c, LANE)))
    tot = jnp.sum(gc * last, axis=0, keepdims=True)
    m1 = gc[:, :c]
    decay = incl * jnp.exp(incl * (m1 - m1.T))
    kb = k * bb
    vb = v * bb
    a = -(strict * decay * _mm_nt(kb, k))
    tinv = eye + a
    p = a
    for _ in range(5):
        p = _mm(p, p)
        tinv = tinv + _mm(tinv, p)
    eg = jnp.exp(gc)
    u = _mm(tinv, vb)
    w = _mm(tinv, kb * eg)
    qk = _mm_nt(q, k) * decay
    v_new = u - _mm(w, st)
    o = _mm(q * eg, st) + _mm(qk, v_new)
    st_new = st * jnp.exp(tot) + _mm_tn(k * jnp.exp(tot - gc), v_new)
    return st_new, o


def _a_step(sts, cq, ck, cv, gts, pa, h0):
    c = CHUNK
    lane = lax.broadcasted_iota(jnp.int32, (1, LANE), 1)
    ii = lax.broadcasted_iota(jnp.int32, (c, c), 0)
    jj = lax.broadcasted_iota(jnp.int32, (c, c), 1)
    row = lax.broadcasted_iota(jnp.int32, (c, 1), 0)

    def pick(t, col):
        return jnp.sum(jnp.where(lane == col, t, 0.0), axis=1, keepdims=True)

    alpha, beta_raw, a_log, dt_b, incl, strict, last = [], [], [], [], [], [], []
    for b in range(sts.shape[0]):
        h, rev = h0 + b // 2, b % 2
        alpha.append(pick(gts[b], h + 8 * rev))
        beta_raw.append(pick(gts[b], h + 16 + 8 * rev))
        a_log.append(pick(pa[rev:rev + 1, :], h))
        dt_b.append(pick(pa[2 + rev:3 + rev, :], h))
        incl.append(((ii <= jj) if rev else (ii >= jj)).astype(F32))
        strict.append(((ii < jj) if rev else (ii > jj)).astype(F32))
        last.append((row == (0 if rev else c - 1)).astype(F32))
    stack = lambda ts: jnp.concatenate([t[None] for t in ts], axis=0)
    return jax.vmap(_a_chain)(sts, cq, ck, cv, stack(alpha), stack(beta_raw), stack(a_log), stack(dt_b),
                              stack(incl), stack(strict), stack(last))


def _a_final(o, za, pa):
    outs = []
    for j in range(o.shape[1] // A_DIM):
        ln = slice(j * A_DIM, (j + 1) * A_DIM)
        outs.append(_rms(o[:, ln], pa[4:5, :]) * _silu(za[:, ln]))
    return jnp.concatenate(outs, axis=1)


def _a_tiles(n, nchunk, heads):
    tiles = []
    for b in range(2 * heads):
        i = (nchunk - 1 - n) if b % 2 else n
        tiles.append((i, pl.ds(pl.multiple_of(i * CHUNK, CHUNK), CHUNK), slice((b // 2) * A_DIM, (b // 2 + 1) * A_DIM)))
    return tiles


def _a_load(tiles, c_ref, gt_ref):
    cq, ck, cv = (jnp.stack([c_ref[r, sl, ln] for _, sl, ln in tiles], axis=0) for r in range(3))
    return cq, ck, cv, jnp.stack([gt_ref[sl, :] for _, sl, _ in tiles], axis=0)


def _loop_by_two(n, step, init):
    assert n % 2 == 0
    return lax.fori_loop(0, n // 2, lambda m, carry: step(2 * m + 1, step(2 * m, carry, 0), 1), init)


def _a_scan(h0, heads, nchunk, c_ref, gt_ref, pa, of_ref, ob_ref, s_ref):
    def step(n, sts, parity):
        tiles = _a_tiles(n, nchunk, heads)
        sts_new, o = _a_step(sts, *_a_load(tiles, c_ref, gt_ref), pa, h0)
        for b, (i, sl, ln) in enumerate(tiles):
            s_ref[b, i] = sts[b]
            (ob_ref if b % 2 else of_ref)[sl, ln] = o[b]
        return sts_new

    _loop_by_two(nchunk, step, jnp.zeros((2 * heads, A_DIM, A_DIM), F32))


def _a_specs(s, heads):
    wide = heads * A_DIM
    once = pl.Buffered(1)
    trio = pl.BlockSpec((3, s, wide), lambda g: (0, 0, g), pipeline_mode=once)
    gates = pl.BlockSpec((s, LANE), lambda g: (0, P_GT // LANE))
    small = pl.BlockSpec((8, LANE), lambda g: (0, 0))

    def cols(base):
        return pl.BlockSpec((s, wide), lambda g: (0, base // wide + g), pipeline_mode=once)

    state = pl.BlockSpec((2 * heads, s // CHUNK, A_DIM, A_DIM), lambda g: (g, 0, 0, 0), pipeline_mode=once)
    return wide, trio, gates, small, cols, state


def _delta_fwd(cqkv, proj, pa):
    s = cqkv.shape[1]
    nchunk = s // CHUNK
    heads = A_FWD_HEADS
    wide, trio, gates, small, cols, state = _a_specs(s, heads)

    def body(c_ref, gt_ref, za_ref, pa_ref, out_ref, o_ref, s_ref, ob_ref):
        h0 = pl.program_id(0) * heads
        pa_v = pa_ref[...]
        _a_scan(h0, heads, nchunk, c_ref, gt_ref, pa_v, o_ref, ob_ref, s_ref)
        o_ref[...] += ob_ref[...]
        out_ref[...] = _a_final(o_ref[...], za_ref[...], pa_v).astype(BF16)

    return pl.pallas_call(
        body, name="delta_fwd", grid=(A_HEADS // heads,),
        in_specs=[trio, gates, cols(P_ZA), small], out_specs=[cols(0), cols(0), state],
        out_shape=[jax.ShapeDtypeStruct((s, D_MODEL), BF16),
                   jax.ShapeDtypeStruct((s, A_WIDTH), F32),
                   jax.ShapeDtypeStruct((2 * A_HEADS, nchunk, A_DIM, A_DIM), F32)],
        scratch_shapes=[pltpu.VMEM((s, wide), F32)], compiler_params=_params(("parallel",)),
    )(cqkv, proj, proj, pa)


def _delta_out_bwd(o_sum, proj, pa, d_mixed, tr=256):
    s = o_sum.shape[0]

    def body(o_ref, za_ref, pa_ref, dm_ref, do_ref, dza_ref, dpa_ref):
        @pl.when(pl.program_id(0) == 0)
        def _():
            dpa_ref[...] = jnp.zeros_like(dpa_ref)

        _, vjp = jax.vjp(_a_final, o_ref[...], za_ref[...], pa_ref[...])
        d_o, d_za, dpa = vjp(dm_ref[...].astype(F32))
        do_ref[...] = d_o
        dza_ref[...] = d_za
        dpa_ref[...] += dpa

    def rows(col):
        return pl.BlockSpec((tr, A_WIDTH), lambda i: (i, col))

    small = pl.BlockSpec((8, LANE), lambda i: (0, 0))
    return pl.pallas_call(
        body, name="delta_out_bwd", grid=(s // tr,), in_specs=[rows(0), rows(P_ZA // A_WIDTH), small, rows(0)],
        out_specs=[rows(0), rows(0), small],
        out_shape=[jax.ShapeDtypeStruct((s, A_WIDTH), F32), jax.ShapeDtypeStruct((s, A_WIDTH), F32),
                   jax.ShapeDtypeStruct((8, LANE), F32)],
        compiler_params=_params(("arbitrary",)),
    )(o_sum, proj, pa, d_mixed)


def _delta_bwd(cqkv, proj, pa, d_o, states):
    s = cqkv.shape[1]
    nchunk = s // CHUNK
    heads = A_BWD_HEADS
    wide, trio, gates, small, cols, state = _a_specs(s, heads)

    def body(c_ref, gt_ref, pa_ref, do_ref, s_hbm, dc_ref, dgt_ref, dpa_ref, s_buf, s_sems):
        h0 = pl.program_id(0) * heads
        pa_v = pa_ref[...]

        @pl.when(h0 == 0)
        def _():
            dgt_ref[...] = jnp.zeros_like(dgt_ref)
            dpa_ref[...] = jnp.zeros_like(dpa_ref)

        dc_ref[...] = jnp.zeros_like(dc_ref)

        def state_copies(n, slot):
            return [pltpu.make_async_copy(s_hbm.at[2 * h0 + b, i], s_buf.at[slot, b], s_sems.at[slot, b])
                    for b, (i, _, _) in enumerate(_a_tiles(nchunk - 1 - n, nchunk, heads))]

        for cp in state_copies(0, 0):
            cp.start()

        def step(n, carry, parity):
            d_sts, dpa = carry
            tiles = _a_tiles(nchunk - 1 - n, nchunk, heads)
            for cp in state_copies(n, parity):
                cp.wait()

            @pl.when(n + 1 < nchunk)
            def _():
                for cp in state_copies(n + 1, 1 - parity):
                    cp.start()

            sts = s_buf[parity]
            d_o_t = jnp.stack([do_ref[sl, ln] for _, sl, ln in tiles], axis=0)
            _, vjp_c = jax.vjp(lambda *a: _a_step(*a, h0), sts, *_a_load(tiles, c_ref, gt_ref), pa_v)
            d_prev, dcq, dck, dcv, dgts, dpa_i = vjp_c((d_sts, d_o_t))
            for b, (_, sl, ln) in enumerate(tiles):
                for r, dc in enumerate((dcq, dck, dcv)):
                    dc_ref[r, sl, ln] += dc[b]
                dgt_ref[sl, :] += dgts[b]
            return d_prev, dpa + dpa_i

        init = (jnp.zeros((2 * heads, A_DIM, A_DIM), F32), jnp.zeros((8, LANE), F32))
        _, dpa_out = lax.fori_loop(0, nchunk, lambda n, carry: step(n, carry, n % 2), init)
        dpa_ref[...] += dpa_out

    fixed = pl.BlockSpec((s, LANE), lambda g: (0, 0))
    return pl.pallas_call(
        body, name="delta_bwd", grid=(A_HEADS // heads,),
        in_specs=[trio, gates, small, cols(0), pl.BlockSpec(memory_space=pl.ANY)], out_specs=[trio, fixed, small],
        out_shape=[jax.ShapeDtypeStruct((3, s, A_WIDTH), F32), jax.ShapeDtypeStruct((s, LANE), F32),
                   jax.ShapeDtypeStruct((8, LANE), F32)],
        scratch_shapes=[pltpu.VMEM((2, 2 * heads, A_DIM, A_DIM), F32), pltpu.SemaphoreType.DMA((2, 2 * heads))],
        compiler_params=_params(("arbitrary",)),
    )(cqkv, proj, pa, d_o, states)


def _rope_tables(s):
    inv = ROPE_THETA ** (-jnp.arange(0, B_DIM, 2, dtype=F32) / B_DIM)
    ang = jnp.arange(s, dtype=F32)[:, None] * inv[None, :]
    cos, sin = jnp.cos(ang), jnp.sin(ang)
    return jnp.concatenate([cos, cos], axis=1), jnp.concatenate([-sin, sin], axis=1)


def _b_block(q_t, z_t, k3, v3, cos_q, sin_q, cos_k, sin_k, pb, n, nb):
    w = WINDOW
    def swap(t):
        return jnp.concatenate([t[:, B_DIM // 2:], t[:, :B_DIM // 2]], axis=1)

    grp = B_HEADS // B_KV
    qi = lax.broadcasted_iota(jnp.int32, (grp * w, 3 * w), 0) & (w - 1)
    kj = lax.broadcasted_iota(jnp.int32, (grp * w, 3 * w), 1)
    kpos = kj + (n - 1) * w
    mask = (jnp.abs(kj - w - qi) <= w) & (kpos >= 0) & (kpos < nb * w)
    lane = lax.broadcasted_iota(jnp.int32, (1, LANE), 1)
    qn, kn = pb[0:1, :B_DIM], pb[1:2, :B_DIM]
    cos_g = jnp.concatenate([cos_q] * grp, axis=0)
    sin_g = jnp.concatenate([sin_q] * grp, axis=0)
    outs = []
    for hk in range(B_KV):
        k = _rms(k3[:, hk * B_DIM:(hk + 1) * B_DIM], kn)
        k = k * cos_k + swap(k) * sin_k
        v = v3[:, hk * B_DIM:(hk + 1) * B_DIM]
        heads = [hk * grp + g for g in range(grp)]
        q = _rms(jnp.concatenate([q_t[:, hq * B_DIM:(hq + 1) * B_DIM] for hq in heads], axis=0), qn)
        q = q * cos_g + swap(q) * sin_g
        sink = jnp.concatenate(
            [jnp.broadcast_to(jnp.sum(jnp.where(lane == hq, pb[2:3, :], 0.0), axis=1, keepdims=True), (w, 1))
             for hq in heads], axis=0)
        s = _mm_nt(q, k) * (B_DIM ** -0.5)
        s = jnp.where(mask, s, -jnp.inf)
        m = jnp.maximum(jnp.max(s, axis=1, keepdims=True), sink)
        p = jnp.exp(s - m)
        p = p / (jnp.sum(p, axis=1, keepdims=True) + jnp.exp(sink - m))
        o = _mm(p, v)
        outs += [o[g * w:(g + 1) * w, :] for g in range(grp)]
    return jnp.concatenate(outs, axis=1) * _silu(z_t)


def _b_specs(s):
    nb = s // WINDOW
    qsp = pl.BlockSpec((WINDOW, 512), lambda n: (n, P_QB // 512))
    zsp = pl.BlockSpec((WINDOW, 512), lambda n: (n, P_ZB // 512))

    def three(col, width):
        return [pl.BlockSpec((WINDOW, width), lambda n: (jnp.maximum(n - 1, 0), col)),
                pl.BlockSpec((WINDOW, width), lambda n: (n, col)),
                pl.BlockSpec((WINDOW, width), lambda n: (jnp.minimum(n + 1, nb - 1), col))]

    tab = pl.BlockSpec((WINDOW, B_DIM), lambda n: (n, 0))
    small = pl.BlockSpec((8, LANE), lambda n: (0, 0))
    specs = [qsp, zsp] + three(P_KB // LANE, LANE) + three(P_VB // LANE, LANE) + [tab, tab] + three(0, B_DIM) + three(0, B_DIM) + [small]
    return nb, specs


def _b_args(proj, cos2, sin2, pb):
    return (proj, proj, proj, proj, proj, proj, proj, proj, cos2, sin2, cos2, cos2, cos2, sin2, sin2, sin2, pb)


def _b_load(refs):
    (q_ref, z_ref, kp, kc, kx, vp, vc, vx, cq, sq, ckp, ckc, ckx, skp, skc, skx, pb_ref) = refs
    cat = lambda *r: jnp.concatenate([t[...] for t in r], axis=0)
    return (q_ref[...], z_ref[...], cat(kp, kc, kx), cat(vp, vc, vx), cq[...], sq[...], cat(ckp, ckc, ckx),
            cat(skp, skc, skx), pb_ref[...])


def _attn_b_fwd(proj, cos2, sin2, pb, mixed):
    s = proj.shape[0]
    nb, specs = _b_specs(s)

    def body(*refs):
        o_ref = refs[-1]
        args = _b_load(refs[:-2])
        o_ref[...] = _b_block(*args, pl.program_id(0), nb).astype(BF16)

    return pl.pallas_call(
        body, name="attn_b_fwd", grid=(nb,), in_specs=specs + [pl.BlockSpec(memory_space=pl.ANY)],
        out_specs=pl.BlockSpec((WINDOW, 512), lambda n: (n, A_WIDTH // 512)),
        out_shape=jax.ShapeDtypeStruct(mixed.shape, mixed.dtype), input_output_aliases={len(specs): 0},
        compiler_params=_params(("parallel",)),
    )(*_b_args(proj, cos2, sin2, pb), mixed)


def _attn_b_bwd(proj, cos2, sin2, pb, d_mixed):
    s = proj.shape[0]
    nb, specs = _b_specs(s)
    w = WINDOW

    def body(*refs):
        dm_ref, dq_ref, dz_ref, dk_ref, dv_ref, dpb_ref = refs[-6:]
        n = pl.program_id(0)
        q_t, z_t, k3, v3, cq, sq, ck, sk, pb_v = _b_load(refs[:-6])

        @pl.when(n == 0)
        def _():
            dk_ref[...] = jnp.zeros_like(dk_ref)
            dv_ref[...] = jnp.zeros_like(dv_ref)
            dpb_ref[...] = jnp.zeros_like(dpb_ref)

        def f(q_, z_, k_, v_, pb_):
            return _b_block(q_, z_, k_, v_, cq, sq, ck, sk, pb_, n, nb)

        _, vjp = jax.vjp(f, q_t, z_t, k3, v3, pb_v)
        dq, dz, dk3, dv3, dpb = vjp(dm_ref[...])
        dq_ref[...] = dq
        dz_ref[...] = dz
        dpb_ref[...] += dpb

        def add(j, cond):
            @pl.when(cond)
            def _():
                rows = pl.ds(pl.multiple_of((n - 1 + j) * w, w), w)
                dk_ref[rows, :] += dk3[j * w:(j + 1) * w, :]
                dv_ref[rows, :] += dv3[j * w:(j + 1) * w, :]

        add(0, n > 0)
        add(1, n >= 0)
        add(2, n < nb - 1)

    blk = pl.BlockSpec((w, 512), lambda n: (n, 0))
    whole = pl.BlockSpec((s, LANE), lambda n: (0, 0))
    small = pl.BlockSpec((8, LANE), lambda n: (0, 0))
    return pl.pallas_call(
        body, name="attn_b_bwd", grid=(nb,),
        in_specs=specs + [pl.BlockSpec((w, 512), lambda n: (n, 2))],
        out_specs=[blk, blk, whole, whole, small],
        out_shape=[jax.ShapeDtypeStruct((s, 512), F32), jax.ShapeDtypeStruct((s, 512), F32),
                   jax.ShapeDtypeStruct((s, LANE), F32), jax.ShapeDtypeStruct((s, LANE), F32),
                   jax.ShapeDtypeStruct((8, LANE), F32)],
        compiler_params=_params(("arbitrary",)),
    )(*_b_args(proj, cos2, sin2, pb), d_mixed)


def _mem_kv_fwd(mem, mem_norm_w, w_kv):
    def body(mem_ref, nw_ref, w_ref, kv_ref):
        mn = _rms(mem_ref[...], nw_ref[...]).astype(BF16)
        kv_ref[...] = jnp.dot(mn, w_ref[...], preferred_element_type=F32)

    return pl.pallas_call(
        body, name="mem_kv_fwd", out_shape=jax.ShapeDtypeStruct((MEM_LEN, 2 * C_HEADS * C_DIM), F32),
        compiler_params=_params(),
    )(mem, mem_norm_w, w_kv)


def _mem_kv_bwd(mem, mem_norm_w, w_kv, d_kv):
    def body(mem_ref, nw_ref, w_ref, g_ref, gw_ref, gn_ref):
        mn, vjp = jax.vjp(_rms, mem_ref[...], nw_ref[...])
        g = g_ref[...].astype(BF16)
        gw_ref[...] = lax.dot_general(mn.astype(BF16), g, (((0,), (0,)), ((), ())), preferred_element_type=F32)
        d_mn = lax.dot_general(g, w_ref[...], (((1,), (1,)), ((), ())), preferred_element_type=F32)
        gn_ref[...] = vjp(d_mn)[1]

    return pl.pallas_call(
        body, name="mem_kv_bwd",
        out_shape=[jax.ShapeDtypeStruct((D_MODEL, 2 * C_HEADS * C_DIM), F32), jax.ShapeDtypeStruct((1, D_MODEL), F32)],
        compiler_params=_params(),
    )(mem, mem_norm_w, w_kv, d_kv)


def _c_tile(q_t, z_t, kvm, pc):
    width = C_HEADS * C_DIM
    outs = []
    for h in range(C_HEADS):
        q = _rms(q_t[:, h * C_DIM:(h + 1) * C_DIM], pc[0:1, :])
        k = _rms(kvm[:, h * C_DIM:(h + 1) * C_DIM], pc[1:2, :])
        v = kvm[:, width + h * C_DIM:width + (h + 1) * C_DIM]
        s = _mm_nt(q, k) * (C_DIM ** -0.5)
        p = jnp.exp(s - jnp.max(s, axis=1, keepdims=True))
        p = p / jnp.sum(p, axis=1, keepdims=True)
        outs.append(_mm(p, v))
    return jnp.concatenate(outs, axis=1) * _silu(z_t)


def _attn_c_fwd(proj, kvm, pc, mixed, tq=256):
    s = proj.shape[0]

    def body(q_ref, z_ref, kv_ref, pc_ref, mixed_ref, o_ref):
        o_ref[...] = _c_tile(q_ref[...], z_ref[...], kv_ref[...], pc_ref[...]).astype(BF16)

    return pl.pallas_call(
        body, name="attn_c_fwd", grid=(s // tq,),
        in_specs=[pl.BlockSpec((tq, 512), lambda i: (i, P_QC // 512)), pl.BlockSpec((tq, 512), lambda i: (i, P_ZC // 512)),
                  pl.BlockSpec(kvm.shape, lambda i: (0, 0)), pl.BlockSpec((8, LANE), lambda i: (0, 0)),
                  pl.BlockSpec(memory_space=pl.ANY)],
        out_specs=pl.BlockSpec((tq, 512), lambda i: (i, (A_WIDTH + 512) // 512)),
        out_shape=jax.ShapeDtypeStruct(mixed.shape, mixed.dtype), input_output_aliases={4: 0},
        compiler_params=_params(("parallel",)),
    )(proj, proj, kvm, pc, mixed)


def _attn_c_bwd(proj, kvm, pc, d_mixed, tq=256):
    s = proj.shape[0]

    def body(q_ref, z_ref, kv_ref, pc_ref, dm_ref, dq_ref, dz_ref, dkv_ref, dpc_ref):
        @pl.when(pl.program_id(0) == 0)
        def _():
            dkv_ref[...] = jnp.zeros_like(dkv_ref)
            dpc_ref[...] = jnp.zeros_like(dpc_ref)

        _, vjp = jax.vjp(_c_tile, q_ref[...], z_ref[...], kv_ref[...], pc_ref[...])
        dq, dz, dkv, dpc = vjp(dm_ref[...])
        dq_ref[...] = dq
        dz_ref[...] = dz
        dkv_ref[...] += dkv
        dpc_ref[...] += dpc

    blk = pl.BlockSpec((tq, 512), lambda i: (i, 0))
    kvs = pl.BlockSpec(kvm.shape, lambda i: (0, 0))
    small = pl.BlockSpec((8, LANE), lambda i: (0, 0))
    return pl.pallas_call(
        body, name="attn_c_bwd", grid=(s // tq,),
        in_specs=[pl.BlockSpec((tq, 512), lambda i: (i, P_QC // 512)), pl.BlockSpec((tq, 512), lambda i: (i, P_ZC // 512)),
                  kvs, small, pl.BlockSpec((tq, 512), lambda i: (i, 3))],
        out_specs=[blk, blk, kvs, small],
        out_shape=[jax.ShapeDtypeStruct((s, 512), F32), jax.ShapeDtypeStruct((s, 512), F32),
                   jax.ShapeDtypeStruct(kvm.shape, F32), jax.ShapeDtypeStruct((8, LANE), F32)],
        compiler_params=_params(("arbitrary",)),
    )(proj, proj, kvm, pc, d_mixed)


def _pad_row(v, width=LANE):
    v = v.reshape(1, -1)
    return jnp.pad(v, ((0, 0), (0, width - v.shape[1])))


def _local_step(x, mem, target, norm_w, w_perm_t, conv_w, pa, pb, pc, mem_norm_w, w_kv, w_out):
    s = x.shape[0]
    cos2, sin2 = _rope_tables(s)
    hn = _rms_fwd(x, norm_w)
    wide = dict(tm=1024, tn=512, tk=2048)
    proj = _matmul(hn, w_perm_t, "nt", F32, "mm_proj", **wide)
    cqkv = _conv_fwd(proj, conv_w)
    mixed, o_sum, states = _delta_fwd(cqkv, proj, pa)
    mixed = _attn_b_fwd(proj, cos2, sin2, pb, mixed)
    kvm = _mem_kv_fwd(mem, mem_norm_w, w_kv)
    mixed = _attn_c_fwd(proj, kvm, pc, mixed)
    mo = _matmul(mixed, w_out, "nn", F32, "mm_out", **wide)
    dy, dyb, loss_parts = _loss_dy(x, mo, target)

    d_mixed = _matmul(dyb, w_out, "nt", F32, "mm_dmixed", **wide)
    g_w_out = _matmul(mixed, dyb, "tn", F32, "mm_gwout", **wide)
    d_qc, d_zc, d_kvm, d_pc = _attn_c_bwd(proj, kvm, pc, d_mixed)
    g_w_kv, g_mem_norm = _mem_kv_bwd(mem, mem_norm_w, w_kv, d_kvm)
    d_qb, d_zb, d_kb, d_vb, d_pb = _attn_b_bwd(proj, cos2, sin2, pb, d_mixed)
    d_o, d_za, d_pa_out = _delta_out_bwd(o_sum, proj, pa, d_mixed)
    d_c, d_gt, d_pa_scan = _delta_bwd(cqkv, proj, pa, d_o, states)
    d_pa = d_pa_out + d_pa_scan
    d_qkv, g_conv = _conv_bwd(proj, conv_w, d_c)
    d_proj = jnp.concatenate([d_qkv, d_za, d_qb, d_zb, d_qc, d_zc, d_kb, d_vb, d_gt,
                              jnp.zeros((s, P_WIDTH - P_GT - LANE), F32)], axis=1).astype(BF16)
    d_hn = _matmul(d_proj, w_perm_t, "nn", F32, "mm_dhn", tm=1024, tn=2048, tk=512)
    g_w_perm_t = _matmul(d_proj, hn, "tn", F32, "mm_gwin", tm=512, tn=1024, tk=2048)
    g_x, g_norm = _rms_bwd(x, norm_w, d_hn, dy)
    return dict(loss_parts=loss_parts, g_x=g_x, g_norm=g_norm, g_w_perm_t=g_w_perm_t, g_conv=g_conv, d_pa=d_pa,
                d_pb=d_pb, d_pc=d_pc, g_mem_norm=g_mem_norm, g_w_kv=g_w_kv, g_w_out=g_w_out)


_SEGMENTS = ((0, O_GT, 0), (O_GT, O_QB, P_GT), (O_QB, O_KB, P_QB), (O_KB, O_VB, P_KB), (O_VB, O_ZB, P_VB),
             (O_ZB, O_QC, P_ZB), (O_QC, O_ZC, P_QC), (O_ZC, IN_WIDTH, P_ZC))


def _permute_blocks(w4):
    parts = []
    for first, end, _ in sorted(_SEGMENTS, key=lambda seg: seg[2]):
        row = first
        while row < end:
            k = row // W_IN_BLOCK
            stop = min(end, (k + 1) * W_IN_BLOCK)
            parts.append(w4[k][row - k * W_IN_BLOCK:stop - k * W_IN_BLOCK, :])
            row = stop
    parts.append(jnp.zeros((P_WIDTH - IN_WIDTH, w4.shape[2]), w4.dtype))
    return jnp.concatenate(parts, axis=0)


def _unpermute_blocks(g):
    blocks = []
    for k in range(N_CHIPS):
        lo, hi = k * W_IN_BLOCK, (k + 1) * W_IN_BLOCK
        parts = [g[p + max(first, lo) - first:p + min(end, hi) - first, :]
                 for first, end, p in _SEGMENTS if max(first, lo) < min(end, hi)]
        blocks.append(jnp.concatenate(parts, axis=0))
    return jnp.stack(blocks, axis=0)


HBM = pl.BlockSpec(memory_space=pltpu.HBM)


def _place():
    x, y, c = lax.axis_index("x"), lax.axis_index("y"), lax.axis_index("c")
    chips = [(1 - x, y), (x, 1 - y), (1 - x, 1 - y)]
    return x, y, c, 2 * x + y, chips, [2 * cx + cy for cx, cy in chips]


PIECE_ROWS_CAP = 600


def _remote(src, dst, send_sems, recv_sems, k, to):
    return pltpu.make_async_remote_copy(src_ref=src, dst_ref=dst, send_sem=send_sems.at[k], recv_sem=recv_sems.at[k],
                                        device_id=to, device_id_type=MESH)


def _half_cols(ref, c):
    half = ref.shape[-1] // 2
    return pl.ds(pl.multiple_of(c * half, LANE), half)


def _all_gather_weights(w_in_b, w_out_b, w_kv_b, conv_b):
    bigs = (w_in_b, w_out_b, w_kv_b)
    n_big = len(bigs)

    def body(*refs):
        srcs, conv_src = refs[:n_big], refs[n_big]
        dsts, conv_dst = refs[n_big + 1:2 * n_big + 1], refs[2 * n_big + 1]
        send_sems, recv_sems, local_sems = refs[2 * n_big + 2:]
        x, y, c, me, chips, chip_ids = _place()
        sibling = (x, y, 1 - c)
        local = [pltpu.make_async_copy(src, dst.at[me], local_sems.at[a]) for a, (src, dst) in enumerate(zip(srcs, dsts))]
        local.append(pltpu.make_async_copy(conv_src, conv_dst.at[me], local_sems.at[n_big]))
        for cp in local:
            cp.start()
        sends = []
        for a, (src, dst) in enumerate(zip(srcs, dsts)):
            mine = _half_cols(src, c)
            for j, chip in enumerate(chips):
                sends.append(_remote(src.at[:, mine], dst.at[me, :, mine], send_sems, recv_sems, 6 * a + j, (*chip, c)))
        for j, chip in enumerate(chips):
            sends.append(_remote(conv_src, conv_dst.at[me], send_sems, recv_sems, 6 * n_big + j, (*chip, c)))
        for cp in sends:
            cp.start()
        passed = []
        for a, (src, dst) in enumerate(zip(srcs, dsts)):
            mine = _half_cols(src, c)
            for j, cid in enumerate(chip_ids):
                landed = dst.at[cid, :, mine]
                _remote(landed, landed, send_sems, recv_sems, 6 * a + j, sibling).wait_recv()
                cp = _remote(landed, landed, send_sems, recv_sems, 6 * a + 3 + j, sibling)
                cp.start()
                passed.append(cp)
        for a, (src, dst) in enumerate(zip(srcs, dsts)):
            other = _half_cols(src, 1 - c)
            for j, cid in enumerate(chip_ids):
                landed = dst.at[cid, :, other]
                _remote(landed, landed, send_sems, recv_sems, 6 * a + 3 + j, sibling).wait_recv()
        for j, cid in enumerate(chip_ids):
            _remote(conv_src, conv_dst.at[cid], send_sems, recv_sems, 6 * n_big + j, sibling).wait_recv()
        for cp in sends + passed:
            cp.wait_send()
        for cp in local:
            cp.wait()

    n_sem = 6 * n_big + 3
    return pl.pallas_call(
        body, name="all_gather_weights",
        out_shape=[jax.ShapeDtypeStruct((N_CHIPS,) + w.shape, w.dtype) for w in bigs + (conv_b,)],
        in_specs=[pl.BlockSpec(memory_space=pltpu.VMEM)] * (n_big + 1), out_specs=[HBM] * (n_big + 1),
        scratch_shapes=[pltpu.SemaphoreType.DMA((n_sem,)), pltpu.SemaphoreType.DMA((n_sem,)),
                        pltpu.SemaphoreType.DMA((n_big + 1,))],
        compiler_params=_params(),
    )(*bigs, conv_b)


def _pair_exchange(grads):
    n = len(grads)
    pieces = [_row_tile(g.shape[1]) for g in grads]

    def body(*refs):
        srcs, gots = refs[:n], refs[n:2 * n]
        stages = refs[2 * n:3 * n]
        send_sems, recv_sems, load_sems = refs[3 * n:]
        x, y, c, _, _, _ = _place()
        sibling = (x, y, 1 - c)
        for a in range(n):
            slabs, rows, _ = gots[a].shape
            piece = pieces[a]
            per_slab = rows // piece
            theirs = _half_cols(srcs[a], 1 - c)
            loads, sends = [], []
            for i in range(slabs * per_slab):
                k, r, slot = i // per_slab, i % per_slab, i % 2
                part = pl.ds(r * piece, piece)
                loads.append(pltpu.make_async_copy(srcs[a].at[k, part, theirs], stages[a].at[slot], load_sems.at[2 * a + slot]))
                sends.append(pltpu.make_async_remote_copy(
                    src_ref=stages[a].at[slot], dst_ref=gots[a].at[k, part, :],
                    send_sem=send_sems.at[2 * a + slot], recv_sem=recv_sems.at[a], device_id=sibling, device_id_type=MESH))
            loads[0].start()
            for i in range(len(loads)):
                loads[i].wait()
                sends[i].start()
                if i + 1 < len(loads):
                    if i >= 1:
                        sends[i - 1].wait_send()
                    loads[i + 1].start()
            for cp in sends[-2:]:
                cp.wait_send()
        for a in range(n):
            whole = srcs[a].at[:, :, _half_cols(srcs[a], c)]
            pltpu.make_async_remote_copy(src_ref=whole, dst_ref=gots[a], send_sem=send_sems.at[2 * a],
                                         recv_sem=recv_sems.at[a], device_id=sibling, device_id_type=MESH).wait_recv()

    halves = [jax.ShapeDtypeStruct((g.shape[0], g.shape[1], g.shape[2] // 2), g.dtype) for g in grads]
    return pl.pallas_call(
        body, name="grad_pair_exchange", out_shape=halves, in_specs=[HBM] * n, out_specs=[HBM] * n,
        scratch_shapes=[pltpu.VMEM((2, piece, g.shape[2] // 2), g.dtype) for piece, g in zip(pieces, grads)]
        + [pltpu.SemaphoreType.DMA((2 * n,)), pltpu.SemaphoreType.DMA((n,)), pltpu.SemaphoreType.DMA((2 * n,))],
        compiler_params=_params(),
    )(*grads)


def _chip_exchange(halves):
    n = len(halves)

    def body(*refs):
        srcs, lands = refs[:n], refs[n:2 * n]
        send_sems, recv_sems = refs[2 * n:]
        x, y, c, me, chips, chip_ids = _place()
        gives = []
        for a in range(n):
            for j, (chip, cid) in enumerate(zip(chips, chip_ids)):
                give = _remote(srcs[a].at[cid], lands[a].at[j], send_sems, recv_sems, 3 * a + j, (*chip, c))
                give.start()
                gives.append(give)
        for a in range(n):
            for j, cid in enumerate(chip_ids):
                _remote(srcs[a].at[cid], lands[a].at[j], send_sems, recv_sems, 3 * a + j, (x, y, c)).wait_recv()
        for give in gives:
            give.wait_send()

    return pl.pallas_call(
        body, name="grad_chip_exchange",
        out_shape=[jax.ShapeDtypeStruct((N_CHIPS - 1,) + h.shape[1:], h.dtype) for h in halves],
        in_specs=[HBM] * n, out_specs=[HBM] * n,
        scratch_shapes=[pltpu.SemaphoreType.DMA((3 * n,)), pltpu.SemaphoreType.DMA((3 * n,))],
    )(*halves)


def _pair_gather(halves):
    n = len(halves)

    def body(*refs):
        srcs, fulls = refs[:n], refs[n:2 * n]
        send_sems, recv_sems, local_sems = refs[2 * n:]
        x, y, c, _, _, _ = _place()
        copies = []
        for a in range(n):
            mine = _half_cols(fulls[a], c)
            keep = pltpu.make_async_copy(srcs[a], fulls[a].at[:, mine], local_sems.at[a])
            keep.start()
            give = _remote(srcs[a], fulls[a].at[:, mine], send_sems, recv_sems, a, (x, y, 1 - c))
            give.start()
            copies += [keep, give]
        for a in range(n):
            other = _half_cols(fulls[a], 1 - c)
            copies[2 * a].wait()
            copies[2 * a + 1].wait_send()
            _remote(srcs[a], fulls[a].at[:, other], send_sems, recv_sems, a, (x, y, 1 - c)).wait_recv()

    return pl.pallas_call(
        body, name="grad_pair_gather",
        out_shape=[jax.ShapeDtypeStruct((h.shape[0], 2 * h.shape[1]), h.dtype) for h in halves],
        in_specs=[pl.BlockSpec(memory_space=pltpu.VMEM)] * n, out_specs=[HBM] * n,
        scratch_shapes=[pltpu.SemaphoreType.DMA((n,)), pltpu.SemaphoreType.DMA((n,)), pltpu.SemaphoreType.DMA((n,))],
    )(*halves)


def _all_reduce_small(p):
    n_dev = 8

    def body(p_ref, o_ref, land, send_sems, recv_sems):
        x, y, c = lax.axis_index("x"), lax.axis_index("y"), lax.axis_index("c")
        me = 4 * x + 2 * y + c
        land[me] = p_ref[...]
        sends = []
        for k in range(1, n_dev):
            fx, fy, fc = (k >> 2) & 1, (k >> 1) & 1, k & 1
            to = (x ^ fx, y ^ fy, c ^ fc)
            cp = _remote(p_ref, land.at[me], send_sems, recv_sems, k - 1, to)
            cp.start()
            sends.append(cp)
        for k in range(1, n_dev):
            _remote(p_ref, land.at[me ^ k], send_sems, recv_sems, k - 1, (x, y, c)).wait_recv()
        total = land[0]
        for d in range(1, n_dev):
            total = total + land[d]
        o_ref[...] = total
        for cp in sends:
            cp.wait_send()

    vm = pl.BlockSpec(memory_space=pltpu.VMEM)
    return pl.pallas_call(
        body, name="all_reduce_small", out_shape=jax.ShapeDtypeStruct(p.shape, p.dtype), in_specs=[vm], out_specs=vm,
        scratch_shapes=[pltpu.VMEM((n_dev,) + p.shape, p.dtype), pltpu.SemaphoreType.DMA((n_dev - 1,)),
                        pltpu.SemaphoreType.DMA((n_dev - 1,))],
    )(p)


def _row_tile(rows):
    fits = [t for t in range(8, min(rows, PIECE_ROWS_CAP) + 1, 8) if rows % t == 0]
    return max(fits) if fits else rows


def _pair_sum(full, got, core, name):
    n, r, c = got.shape
    tr = _row_tile(r)

    def body(core_ref, a_ref, b_ref, o_ref):
        o_ref[...] = (a_ref[...] + b_ref[...]).astype(BF16)

    blk = pl.BlockSpec((None, tr, c), lambda i, j, core_ref: (i, j, 0))
    grid_spec = pltpu.PrefetchScalarGridSpec(
        num_scalar_prefetch=1, grid=(n, r // tr),
        in_specs=[pl.BlockSpec((None, tr, c), lambda i, j, core_ref: (i, j, core_ref[0])), blk], out_specs=blk)
    return pl.pallas_call(body, name=name, grid_spec=grid_spec, out_shape=jax.ShapeDtypeStruct(got.shape, BF16),
                          compiler_params=_params(("parallel", "parallel")))(core, full, got)


def _chip_sum(full, got, land, place, name):
    n, r, c = land.shape
    tr = _row_tile(r)

    def body(place_ref, a_ref, b_ref, l_ref, o_ref):
        total = a_ref[...] + b_ref[...]
        for j in range(n):
            total = total + l_ref[j].astype(F32)
        o_ref[...] = total

    grid_spec = pltpu.PrefetchScalarGridSpec(
        num_scalar_prefetch=1, grid=(r // tr,),
        in_specs=[pl.BlockSpec((None, tr, c), lambda i, p: (p[0], i, p[1])),
                  pl.BlockSpec((None, tr, c), lambda i, p: (p[0], i, 0)),
                  pl.BlockSpec((n, tr, c), lambda i, p: (0, i, 0))],
        out_specs=pl.BlockSpec((tr, c), lambda i, p: (i, 0)))
    return pl.pallas_call(body, name=name, grid_spec=grid_spec, out_shape=jax.ShapeDtypeStruct((r, c), F32),
                          compiler_params=_params(("parallel",)))(place, full, got, land)


def _adamw(w, g, m, v, name):
    r, c = w.shape
    tr = _row_tile(r)
    tc = 1024 if c % 1024 == 0 else c

    def body(w_ref, g_ref, m_ref, v_ref, d_ref, mo_ref, vo_ref):
        g_ = g_ref[...]
        m2 = ADAM_B1 * m_ref[...] + (1.0 - ADAM_B1) * g_
        v2 = ADAM_B2 * v_ref[...] + (1.0 - ADAM_B2) * jnp.square(g_)
        m_hat = m2 / (1.0 - ADAM_B1 ** ADAM_STEP)
        v_hat = v2 / (1.0 - ADAM_B2 ** ADAM_STEP)
        d_ref[...] = -ADAM_LR * (m_hat / (jnp.sqrt(v_hat) + ADAM_EPS) + ADAM_WD * w_ref[...])
        mo_ref[...] = m2
        vo_ref[...] = v2

    blk = pl.BlockSpec((tr, tc), lambda i, j: (i, j))
    return pl.pallas_call(body, name=name, grid=(r // tr, c // tc), in_specs=[blk] * 4, out_specs=[blk] * 3,
                          out_shape=[jax.ShapeDtypeStruct(w.shape, F32)] * 3,
                          compiler_params=_params(("parallel", "parallel")))(w, g, m, v)


SMALL_NAMES = ("norm_w", "mem_norm_w", "o_norm_a", "q_norm_c", "k_norm_c", "q_norm_b", "k_norm_b",
               "a_log_fwd", "a_log_bwd", "dt_bias_fwd", "dt_bias_bwd", "sink_b")
SMALL_SIZES = (2048, 2048, 128, 128, 128, 64, 64, 8, 8, 8, 8, 8)
SMALL_LOSS = sum(SMALL_SIZES)
SMALL_CONV = 5120
SMALL_TOTAL = SMALL_CONV + CONV_K * 3 * A_WIDTH
SMALL_ROWS = SMALL_TOTAL // LANE


def _pack_small(parts, extra=None, conv=None):
    vec = [parts[n].reshape(-1) for n in SMALL_NAMES]
    vec.append(jnp.zeros((1,), F32) if extra is None else extra.reshape(1))
    vec.append(jnp.zeros((SMALL_CONV - SMALL_LOSS - 1,), F32))
    vec.append(jnp.zeros((SMALL_TOTAL - SMALL_CONV,), F32) if conv is None else conv.reshape(-1))
    return jnp.concatenate(vec).reshape(SMALL_ROWS, LANE)


def _unpack_small(packed):
    flat = packed.reshape(-1)
    out, off = {}, 0
    for n, size in zip(SMALL_NAMES, SMALL_SIZES):
        out[n] = flat[off:off + size].reshape(1, size)
        off += size
    return out


WEIGHT_ORDER = ("norm_w", "w_in", "conv_w_a", "a_log_fwd", "a_log_bwd", "dt_bias_fwd", "dt_bias_bwd", "o_norm_a",
                "q_norm_b", "k_norm_b", "sink_b", "mem_norm_w", "w_mem_kv", "q_norm_c", "k_norm_c", "w_out")


def kernel(x, mem, norm_w, w_in, conv_w_a, a_log_fwd, a_log_bwd, dt_bias_fwd, dt_bias_bwd, o_norm_a, q_norm_b, k_norm_b, sink_b, mem_norm_w, w_mem_kv, q_norm_c, k_norm_c, w_out, loss_target, m_norm_w, m_w_in, m_conv_w_a, m_a_log_fwd, m_a_log_bwd, m_dt_bias_fwd, m_dt_bias_bwd, m_o_norm_a, m_q_norm_b, m_k_norm_b, m_sink_b, m_mem_norm_w, m_w_mem_kv, m_q_norm_c, m_k_norm_c, m_w_out, v_norm_w, v_w_in, v_conv_w_a, v_a_log_fwd, v_a_log_bwd, v_dt_bias_fwd, v_dt_bias_bwd, v_o_norm_a, v_q_norm_b, v_k_norm_b, v_sink_b, v_mem_norm_w, v_w_mem_kv, v_q_norm_c, v_k_norm_c, v_w_out):
    weights = dict(norm_w=norm_w, w_in=w_in, conv_w_a=conv_w_a, a_log_fwd=a_log_fwd, a_log_bwd=a_log_bwd,
                   dt_bias_fwd=dt_bias_fwd, dt_bias_bwd=dt_bias_bwd, o_norm_a=o_norm_a, q_norm_b=q_norm_b,
                   k_norm_b=k_norm_b, sink_b=sink_b, mem_norm_w=mem_norm_w, w_mem_kv=w_mem_kv, q_norm_c=q_norm_c,
                   k_norm_c=k_norm_c, w_out=w_out)
    mom1 = dict(norm_w=m_norm_w, w_in=m_w_in, conv_w_a=m_conv_w_a, a_log_fwd=m_a_log_fwd, a_log_bwd=m_a_log_bwd,
                dt_bias_fwd=m_dt_bias_fwd, dt_bias_bwd=m_dt_bias_bwd, o_norm_a=m_o_norm_a, q_norm_b=m_q_norm_b,
                k_norm_b=m_k_norm_b, sink_b=m_sink_b, mem_norm_w=m_mem_norm_w, w_mem_kv=m_w_mem_kv,
                q_norm_c=m_q_norm_c, k_norm_c=m_k_norm_c, w_out=m_w_out)
    mom2 = dict(norm_w=v_norm_w, w_in=v_w_in, conv_w_a=v_conv_w_a, a_log_fwd=v_a_log_fwd, a_log_bwd=v_a_log_bwd,
                dt_bias_fwd=v_dt_bias_fwd, dt_bias_bwd=v_dt_bias_bwd, o_norm_a=v_o_norm_a, q_norm_b=v_q_norm_b,
                k_norm_b=v_k_norm_b, sink_b=v_sink_b, mem_norm_w=v_mem_norm_w, w_mem_kv=v_w_mem_kv,
                q_norm_c=v_q_norm_c, k_norm_c=v_k_norm_c, w_out=v_w_out)
    chip = 2 * lax.axis_index("x") + lax.axis_index("y")

    w_in4, w_out4, w_kv4, conv4 = _all_gather_weights(jnp.transpose(w_in[0]).astype(BF16), w_out[0].astype(BF16),
                                                      w_mem_kv[0].astype(BF16), conv_w_a[0])
    w_perm_t = _permute_blocks(w_in4)
    w_out_full = w_out4.reshape(D_MODEL, D_MODEL)
    w_kv_full = w_kv4.reshape(D_MODEL, 2 * C_HEADS * C_DIM)
    conv_full = jnp.transpose(conv4, (1, 0, 2)).reshape(CONV_K, 3 * A_WIDTH)
    pa = jnp.concatenate([_pad_row(a_log_fwd), _pad_row(a_log_bwd), _pad_row(dt_bias_fwd), _pad_row(dt_bias_bwd),
                          _pad_row(o_norm_a), jnp.zeros((3, LANE), F32)], axis=0)
    pb = jnp.concatenate([_pad_row(q_norm_b), _pad_row(k_norm_b), _pad_row(sink_b), jnp.zeros((5, LANE), F32)], axis=0)
    pc = jnp.concatenate([_pad_row(q_norm_c), _pad_row(k_norm_c), jnp.zeros((6, LANE), F32)], axis=0)

    r = _local_step(x[0], mem[0], loss_target[0], norm_w, w_perm_t, conv_full, pa, pb, pc, mem_norm_w, w_kv_full,
                    w_out_full)

    g_in4 = _unpermute_blocks(r["g_w_perm_t"])
    g_out4 = r["g_w_out"].reshape(N_CHIPS, D_MODEL // N_CHIPS, D_MODEL)
    g_kv4 = r["g_w_kv"].reshape(N_CHIPS, D_MODEL // N_CHIPS, 2 * C_HEADS * C_DIM)
    full = [g_in4, g_out4, g_kv4]
    core = lax.axis_index("c").astype(jnp.int32).reshape(1)
    got = _pair_exchange(full)
    pair = [_pair_sum(a, b, core, "grad_pair_sum_%d" % i) for i, (a, b) in enumerate(zip(full, got))]
    lands = _chip_exchange(pair)
    place = jnp.stack([chip, lax.axis_index("c")]).astype(jnp.int32)
    reduced = [_chip_sum(a, b, l, place, "grad_chip_sum_%d" % i) for i, (a, b, l) in enumerate(zip(full, got, lands))]
    g_w_in_t, g_w_out, g_w_kv = _pair_gather(reduced)

    d_pa, d_pb, d_pc = r["d_pa"], r["d_pb"], r["d_pc"]
    small_g = dict(norm_w=r["g_norm"], mem_norm_w=r["g_mem_norm"], o_norm_a=d_pa[4], q_norm_c=d_pc[0], k_norm_c=d_pc[1],
                   q_norm_b=d_pb[0, :B_DIM], k_norm_b=d_pb[1, :B_DIM], a_log_fwd=d_pa[0, :A_HEADS],
                   a_log_bwd=d_pa[1, :A_HEADS], dt_bias_fwd=d_pa[2, :A_HEADS], dt_bias_bwd=d_pa[3, :A_HEADS],
                   sink_b=d_pb[2, :B_HEADS])
    packed = _all_reduce_small(_pack_small(small_g, jnp.sum(r["loss_parts"][:, 0, 0]), r["g_conv"]))
    flat = packed.reshape(-1)
    loss = flat[SMALL_LOSS]
    conv_sum = flat[SMALL_CONV:].reshape(CONV_K, 3 * A_WIDTH)
    conv_cols = 3 * A_WIDTH // N_CHIPS
    g_conv = lax.dynamic_slice(conv_sum, (0, chip * conv_cols), (CONV_K, conv_cols))

    grads = _unpack_small(packed)
    grads.update(w_in=jnp.transpose(g_w_in_t), w_mem_kv=g_w_kv, w_out=g_w_out, conv_w_a=g_conv)
    delta, new_m, new_v = {}, {}, {}
    for n in ("w_mem_kv", "w_out", "conv_w_a"):
        delta[n], new_m[n], new_v[n] = _adamw(weights[n][0], grads[n], mom1[n][0], mom2[n][0], "adamw_" + n)
    stepped = _adamw(jnp.transpose(w_in[0]), g_w_in_t, jnp.transpose(m_w_in[0]), jnp.transpose(v_w_in[0]), "adamw_w_in")
    delta["w_in"], new_m["w_in"], new_v["w_in"] = (jnp.transpose(t) for t in stepped)
    d_s, m_s, v_s = _adamw(_pack_small(weights), packed, _pack_small(mom1), _pack_small(mom2), "adamw_small")
    d_s, m_s, v_s = _unpack_small(d_s), _unpack_small(m_s), _unpack_small(v_s)
    for n in SMALL_NAMES:
        delta[n], new_m[n], new_v[n] = d_s[n], m_s[n], v_s[n]

    def shaped(tree):
        return [tree[n].reshape(weights[n].shape) for n in WEIGHT_ORDER]

    return (loss, r["g_x"].reshape(x.shape), *shaped(grads), *shaped(delta), *shaped(new_m), *shaped(new_v))
```

```python
import functools

import jax
import jax.numpy as jnp
from jax import lax
from jax.experimental import pallas as pl
from jax.experimental.pallas import tpu as pltpu

F32 = jnp.float32
BF16 = jnp.bfloat16
HI = lax.Precision.HIGHEST
MESH = pl.DeviceIdType.MESH

D_MODEL = 2048
A_WIDTH = 1024
A_HEADS = 8
A_DIM = 128
CONV_K = 5
CHUNK = 64
B_HEADS = 8
B_KV = 2
B_DIM = 64
WINDOW = 128
C_HEADS = 4
C_DIM = 128
MEM_LEN = 256
ROPE_THETA = 10000.0
EPS = 1e-6
IN_WIDTH = 6432
N_CHIPS = 4
W_IN_BLOCK = IN_WIDTH // N_CHIPS

LANE = 128
P_QA, P_KA, P_VA, P_ZA = 0, 1024, 2048, 3072
P_QB, P_ZB, P_QC, P_ZC = 4096, 4608, 5120, 5632
P_KB, P_VB, P_GT = 6144, 6272, 6400
P_WIDTH = 6656
O_GT, O_QB, O_KB, O_VB, O_ZB, O_QC, O_ZC = 4096, 4128, 4640, 4768, 4896, 5408, 5920

ADAM_LR, ADAM_B1, ADAM_B2, ADAM_EPS, ADAM_WD, ADAM_STEP = 0.001, 0.9, 0.999, 1e-08, 0.01, 10

VMEM_LIMIT = 56 * 1024 * 1024


def _params(sem=None):
    return pltpu.CompilerParams(dimension_semantics=sem, vmem_limit_bytes=VMEM_LIMIT)


def _dot(a, b, dims=(((1,), (0,)), ((), ())), precision=HI):
    return lax.dot_general(a, b, dims, precision=precision, preferred_element_type=F32)


def _dot_nt(a, b, precision=HI):
    return _dot(a, b, (((1,), (1,)), ((), ())), precision)


def _dot_tn(a, b, precision=HI):
    return _dot(a, b, (((0,), (0,)), ((), ())), precision)


_NN = (((1,), (0,)), ((), ()))
_NT = (((1,), (1,)), ((), ()))
_TN = (((0,), (0,)), ((), ()))


def _bdot(a, b, dims):
    return lax.dot_general(a.astype(BF16), b.astype(BF16), dims, preferred_element_type=F32)


@jax.custom_vjp
def _mm(a, b):
    return _bdot(a, b, _NN)


_mm.defvjp(lambda a, b: (_bdot(a, b, _NN), (a, b)),
           lambda res, ct: (_bdot(ct, res[1], _NT), _bdot(res[0], ct, _TN)))


@jax.custom_vjp
def _mm_nt(a, b):
    return _bdot(a, b, _NT)


_mm_nt.defvjp(lambda a, b: (_bdot(a, b, _NT), (a, b)),
              lambda res, ct: (_bdot(ct, res[1], _NN), _bdot(ct, res[0], _TN)))


@jax.custom_vjp
def _mm_tn(a, b):
    return _bdot(a, b, _TN)


_mm_tn.defvjp(lambda a, b: (_bdot(a, b, _TN), (a, b)),
              lambda res, ct: (_bdot(res[1], ct, _NT), _bdot(res[0], ct, _NN)))


def _rms(t, w):
    return t * lax.rsqrt(jnp.mean(t * t, axis=-1, keepdims=True) + EPS) * w


def _l2(t):
    return t * lax.rsqrt(jnp.sum(t * t, axis=-1, keepdims=True) + EPS)


def _silu(t):
    return t * jax.nn.sigmoid(t)


def _softplus(t):
    return jnp.maximum(t, 0.0) + jnp.log1p(jnp.exp(-jnp.abs(t)))


def _matmul(a, b, mode, out_dtype, name, tm=512, tn=512, tk=512):
    (m, k) = a.shape[::-1] if mode == "tn" else a.shape
    n = b.shape[0] if mode == "nt" else b.shape[1]
    tm, tn, tk = min(tm, m), min(tn, n), min(tk, k)
    assert m % tm == 0 and n % tn == 0 and k % tk == 0, (m, n, k, tm, tn, tk)
    if mode == "nn":
        a_spec = pl.BlockSpec((tm, tk), lambda i, j, kk: (i, kk))
        b_spec = pl.BlockSpec((tk, tn), lambda i, j, kk: (kk, j))
        dims = (((1,), (0,)), ((), ()))
    elif mode == "nt":
        a_spec = pl.BlockSpec((tm, tk), lambda i, j, kk: (i, kk))
        b_spec = pl.BlockSpec((tn, tk), lambda i, j, kk: (j, kk))
        dims = (((1,), (1,)), ((), ()))
    else:
        a_spec = pl.BlockSpec((tk, tm), lambda i, j, kk: (kk, i))
        b_spec = pl.BlockSpec((tk, tn), lambda i, j, kk: (kk, j))
        dims = (((0,), (0,)), ((), ()))
    nk = k // tk

    def body_one(a_ref, b_ref, o_ref):
        o_ref[...] = _bdot(a_ref[...], b_ref[...], dims).astype(out_dtype)

    def body_acc(a_ref, b_ref, o_ref, acc_ref):
        kk = pl.program_id(2)

        @pl.when(kk == 0)
        def _():
            acc_ref[...] = jnp.zeros_like(acc_ref)

        acc_ref[...] += _bdot(a_ref[...], b_ref[...], dims)

        @pl.when(kk == nk - 1)
        def _():
            o_ref[...] = acc_ref[...].astype(out_dtype)

    return pl.pallas_call(
        body_one if nk == 1 else body_acc, name=name, grid=(m // tm, n // tn, nk),
        in_specs=[a_spec, b_spec], out_specs=pl.BlockSpec((tm, tn), lambda i, j, kk: (i, j)),
        out_shape=jax.ShapeDtypeStruct((m, n), out_dtype),
        scratch_shapes=[] if nk == 1 else [pltpu.VMEM((tm, tn), F32)],
        compiler_params=_params(("parallel", "parallel", "arbitrary")),
    )(a, b)


def _rms_fwd(x, w, tr=256):
    s, d = x.shape

    def body(x_ref, w_ref, o_ref):
        o_ref[...] = _rms(x_ref[...], w_ref[...]).astype(BF16)

    return pl.pallas_call(
        body, name="rms_fwd", grid=(s // tr,),
        in_specs=[pl.BlockSpec((tr, d), lambda i: (i, 0)), pl.BlockSpec((1, d), lambda i: (0, 0))],
        out_specs=pl.BlockSpec((tr, d), lambda i: (i, 0)),
        out_shape=jax.ShapeDtypeStruct((s, d), BF16), compiler_params=_params(("parallel",)),
    )(x, w)


def _rms_bwd(x, w, d_hn, dy, tr=256):
    s, d = x.shape

    def body(x_ref, w_ref, g_ref, dy_ref, gx_ref, gw_ref):
        _, vjp = jax.vjp(_rms, x_ref[...], w_ref[...])
        dx, dw = vjp(g_ref[...])
        gx_ref[...] = dy_ref[...] + dx

        @pl.when(pl.program_id(0) == 0)
        def _():
            gw_ref[...] = jnp.zeros_like(gw_ref)

        gw_ref[...] += dw

    row = pl.BlockSpec((tr, d), lambda i: (i, 0))
    vec = pl.BlockSpec((1, d), lambda i: (0, 0))
    return pl.pallas_call(
        body, name="rms_bwd", grid=(s // tr,), in_specs=[row, vec, row, row], out_specs=[row, vec],
        out_shape=[jax.ShapeDtypeStruct((s, d), F32), jax.ShapeDtypeStruct((1, d), F32)],
        compiler_params=_params(("arbitrary",)),
    )(x, w, d_hn, dy)


def _loss_dy(x, mo, target, tr=256):
    s, d = x.shape
    nt = s // tr

    def body(x_ref, mo_ref, t_ref, dy_ref, dyb_ref, l_ref):
        err = x_ref[...] + mo_ref[...] - t_ref[...]
        dy = err * (1.0 / d)
        dy_ref[...] = dy
        dyb_ref[...] = dy.astype(BF16)
        l_ref[...] = jnp.full(l_ref.shape, 0.5 * jnp.sum(jnp.sum(err * err, axis=1, keepdims=True) * (1.0 / d)), F32)

    row = pl.BlockSpec((tr, d), lambda i: (i, 0))
    return pl.pallas_call(
        body, name="loss_dy", grid=(nt,), in_specs=[row, row, row],
        out_specs=[row, row, pl.BlockSpec((1, 8, LANE), lambda i: (i, 0, 0))],
        out_shape=[jax.ShapeDtypeStruct((s, d), F32), jax.ShapeDtypeStruct((s, d), BF16),
                   jax.ShapeDtypeStruct((nt, 8, LANE), F32)],
        compiler_params=_params(("parallel",)),
    )(x, mo, target)


def _shift_rows(t, s):
    if s == 0:
        return t
    n = t.shape[0]
    rolled = pltpu.roll(t, (-s) % n, axis=0)
    idx = lax.broadcasted_iota(jnp.int32, t.shape, 0) + s
    return jnp.where((idx >= 0) & (idx < n), rolled, 0.0)


def _conv_fwd(proj, conv_w):
    s = proj.shape[0]
    nblk = 3 * A_WIDTH // LANE

    def body(x_ref, w_ref, o_ref):
        x = x_ref[...]
        acc = jnp.zeros_like(x)
        for j in range(CONV_K):
            acc = acc + w_ref[j:j + 1, :] * _shift_rows(x, j - CONV_K // 2)
        o_ref[...] = acc

    return pl.pallas_call(
        body, name="conv_fwd", grid=(nblk,),
        in_specs=[pl.BlockSpec((s, LANE), lambda i: (0, i)), pl.BlockSpec((CONV_K, LANE), lambda i: (0, i))],
        out_specs=pl.BlockSpec((None, s, LANE), lambda i: (i // A_HEADS, 0, i % A_HEADS)),
        out_shape=jax.ShapeDtypeStruct((3, s, A_WIDTH), F32), compiler_params=_params(("parallel",)),
    )(proj, conv_w)


def _conv_bwd(proj, conv_w, d_c):
    s = proj.shape[0]
    nblk = 3 * A_WIDTH // LANE

    def body(x_ref, w_ref, g_ref, dx_ref, dw_ref):
        x, g = x_ref[...], g_ref[...]
        acc = jnp.zeros_like(x)
        for j in range(CONV_K):
            off = j - CONV_K // 2
            acc = acc + w_ref[j:j + 1, :] * _shift_rows(g, -off)
            dw_ref[j:j + 1, :] = jnp.sum(_shift_rows(x, off) * g, axis=0, keepdims=True)
        dx_ref[...] = acc

    col = pl.BlockSpec((s, LANE), lambda i: (0, i))
    wsp = pl.BlockSpec((CONV_K, LANE), lambda i: (0, i))
    dsp = pl.BlockSpec((None, s, LANE), lambda i: (i // A_HEADS, 0, i % A_HEADS))
    return pl.pallas_call(
        body, name="conv_bwd", grid=(nblk,), in_specs=[col, wsp, dsp], out_specs=[col, wsp],
        out_shape=[jax.ShapeDtypeStruct((s, 3 * A_WIDTH), F32), jax.ShapeDtypeStruct((CONV_K, 3 * A_WIDTH), F32)],
        compiler_params=_params(("parallel",)),
    )(proj, conv_w, d_c)


A_FWD_HEADS = 4
A_BWD_HEADS = 4


def _neumann_inverse(a):
    c = a.shape[-1]
    eye = (lax.broadcasted_iota(jnp.int32, (c, c), 0) == lax.broadcasted_iota(jnp.int32, (c, c), 1)).astype(F32)
    tinv = eye + a
    p = a
    for _ in range(5):
        p = _mm(p, p)
        tinv = tinv + _mm(tinv, p)
    return tinv


@jax.custom_vjp
def _unit_inverse(a):
    return _neumann_inverse(a)


def _unit_inverse_fwd(a):
    tinv = _neumann_inverse(a)
    return tinv, tinv


def _unit_inverse_bwd(tinv, ct):
    return (_bdot(_bdot(tinv, ct, _TN), tinv, _NT),)


_unit_inverse.defvjp(_unit_inverse_fwd, _unit_inverse_bwd)


def _a_chain(st, cq, ck, cv, alpha, beta_raw, a_log, dt_b, incl, strict, last):
    c = CHUNK
    gb = -jnp.exp(a_log) * _softplus(alpha + dt_b)
    bb = jax.nn.sigmoid(beta_raw)
    q = _l2(_silu(cq)) * (A_DIM ** -0.5)
    k = _l2(_silu(ck))
    v = _silu(cv)

    gc = _dot(incl, jnp.broadcast_to(gb, (c, LANE)))
    tot = jnp.sum(gc * last, axis=0, keepdims=True)
    m1 = gc[:, :c]
    decay = incl * jnp.exp(incl * (m1 - m1.T))
    kb = k * bb
    vb = v * bb
    a = -(strict * decay * _mm_nt(kb, k))
    tinv = _unit_inverse(a)
    eg = jnp.exp(gc)
    u = _mm(tinv, vb)
    w = _mm(tinv, kb * eg)
    qk = _mm_nt(q, k) * decay
    v_new = u - _mm(w, st)
    o = _mm(q * eg, st) + _mm(qk, v_new)
    st_new = st * jnp.exp(tot) + _mm_tn(k * jnp.exp(tot - gc), v_new)
    return st_new, o


def _a_step(sts, cq, ck, cv, gts, pa, h0):
    c = CHUNK
    lane = lax.broadcasted_iota(jnp.int32, (1, LANE), 1)
    ii = lax.broadcasted_iota(jnp.int32, (c, c), 0)
    jj = lax.broadcasted_iota(jnp.int32, (c, c), 1)
    row = lax.broadcasted_iota(jnp.int32, (c, 1), 0)

    def pick(t, col):
        return jnp.sum(jnp.where(lane == col, t, 0.0), axis=1, keepdims=True)

    alpha, beta_raw, a_log, dt_b, incl, strict, last = [], [], [], [], [], [], []
    for b in range(sts.shape[0]):
        h, rev = h0 + b // 2, b % 2
        alpha.append(pick(gts[b], h + 8 * rev))
        beta_raw.append(pick(gts[b], h + 16 + 8 * rev))
        a_log.append(pick(pa[rev:rev + 1, :], h))
        dt_b.append(pick(pa[2 + rev:3 + rev, :], h))
        incl.append(((ii <= jj) if rev else (ii >= jj)).astype(F32))
        strict.append(((ii < jj) if rev else (ii > jj)).astype(F32))
        last.append((row == (0 if rev else c - 1)).astype(F32))
    stack = lambda ts: jnp.concatenate([t[None] for t in ts], axis=0)
    return jax.vmap(_a_chain)(sts, cq, ck, cv, stack(alpha), stack(beta_raw), stack(a_log), stack(dt_b),
                              stack(incl), stack(strict), stack(last))


def _a_final(o, za, pa):
    outs = []
    for j in range(o.shape[1] // A_DIM):
        ln = slice(j * A_DIM, (j + 1) * A_DIM)
        outs.append(_rms(o[:, ln], pa[4:5, :]) * _silu(za[:, ln]))
    return jnp.concatenate(outs, axis=1)


def _a_tiles(n, nchunk, heads):
    tiles = []
    for b in range(2 * heads):
        i = (nchunk - 1 - n) if b % 2 else n
        tiles.append((i, pl.ds(pl.multiple_of(i * CHUNK, CHUNK), CHUNK), slice((b // 2) * A_DIM, (b // 2 + 1) * A_DIM)))
    return tiles


def _a_load(tiles, c_ref, gt_ref):
    cq, ck, cv = (jnp.stack([c_ref[r, sl, ln] for _, sl, ln in tiles], axis=0) for r in range(3))
    return cq, ck, cv, jnp.stack([gt_ref[sl, :] for _, sl, _ in tiles], axis=0)


def _loop_by_two(n, step, init):
    assert n % 2 == 0
    return lax.fori_loop(0, n // 2, lambda m, carry: step(2 * m + 1, step(2 * m, carry, 0), 1), init)


def _a_scan(h0, heads, nchunk, c_ref, gt_ref, pa, of_ref, ob_ref, s_ref):
    def step(n, sts, parity):
        tiles = _a_tiles(n, nchunk, heads)
        sts_new, o = _a_step(sts, *_a_load(tiles, c_ref, gt_ref), pa, h0)
        for b, (i, sl, ln) in enumerate(tiles):
            s_ref[b, i] = sts[b]
            (ob_ref if b % 2 else of_ref)[sl, ln] = o[b]
        return sts_new

    _loop_by_two(nchunk, step, jnp.zeros((2 * heads, A_DIM, A_DIM), F32))


def _a_specs(s, heads):
    wide = heads * A_DIM
    once = pl.Buffered(1)
    trio = pl.BlockSpec((3, s, wide), lambda g: (0, 0, g), pipeline_mode=once)
    gates = pl.BlockSpec((s, LANE), lambda g: (0, P_GT // LANE))
    small = pl.BlockSpec((8, LANE), lambda g: (0, 0))

    def cols(base):
        return pl.BlockSpec((s, wide), lambda g: (0, base // wide + g), pipeline_mode=once)

    state = pl.BlockSpec((2 * heads, s // CHUNK, A_DIM, A_DIM), lambda g: (g, 0, 0, 0), pipeline_mode=once)
    return wide, trio, gates, small, cols, state


def _delta_fwd(cqkv, proj, pa):
    s = cqkv.shape[1]
    nchunk = s // CHUNK
    heads = A_FWD_HEADS
    wide, trio, gates, small, cols, state = _a_specs(s, heads)

    def body(c_ref, gt_ref, za_ref, pa_ref, out_ref, o_ref, s_ref, ob_ref):
        h0 = pl.program_id(0) * heads
        pa_v = pa_ref[...]
        _a_scan(h0, heads, nchunk, c_ref, gt_ref, pa_v, o_ref, ob_ref, s_ref)
        o_ref[...] += ob_ref[...]
        out_ref[...] = _a_final(o_ref[...], za_ref[...], pa_v).astype(BF16)

    return pl.pallas_call(
        body, name="delta_fwd", grid=(A_HEADS // heads,),
        in_specs=[trio, gates, cols(P_ZA), small], out_specs=[cols(0), cols(0), state],
        out_shape=[jax.ShapeDtypeStruct((s, D_MODEL), BF16),
                   jax.ShapeDtypeStruct((s, A_WIDTH), F32),
                   jax.ShapeDtypeStruct((2 * A_HEADS, nchunk, A_DIM, A_DIM), F32)],
        scratch_shapes=[pltpu.VMEM((s, wide), F32)], compiler_params=_params(("parallel",)),
    )(cqkv, proj, proj, pa)


def _delta_out_bwd(o_sum, proj, pa, d_mixed, tr=256):
    s = o_sum.shape[0]

    def body(o_ref, za_ref, pa_ref, dm_ref, do_ref, dza_ref, dpa_ref):
        @pl.when(pl.program_id(0) == 0)
        def _():
            dpa_ref[...] = jnp.zeros_like(dpa_ref)

        _, vjp = jax.vjp(_a_final, o_ref[...], za_ref[...], pa_ref[...])
        d_o, d_za, dpa = vjp(dm_ref[...].astype(F32))
        do_ref[...] = d_o
        dza_ref[...] = d_za
        dpa_ref[...] += dpa

    def rows(col):
        return pl.BlockSpec((tr, A_WIDTH), lambda i: (i, col))

    small = pl.BlockSpec((8, LANE), lambda i: (0, 0))
    return pl.pallas_call(
        body, name="delta_out_bwd", grid=(s // tr,), in_specs=[rows(0), rows(P_ZA // A_WIDTH), small, rows(0)],
        out_specs=[rows(0), rows(0), small],
        out_shape=[jax.ShapeDtypeStruct((s, A_WIDTH), F32), jax.ShapeDtypeStruct((s, A_WIDTH), F32),
                   jax.ShapeDtypeStruct((8, LANE), F32)],
        compiler_params=_params(("arbitrary",)),
    )(o_sum, proj, pa, d_mixed)


def _delta_bwd(cqkv, proj, pa, d_o, states):
    s = cqkv.shape[1]
    nchunk = s // CHUNK
    heads = A_BWD_HEADS
    wide, trio, gates, small, cols, state = _a_specs(s, heads)

    def body(c_ref, gt_ref, pa_ref, do_ref, s_hbm, dc_ref, dgt_ref, dpa_ref, s_buf, s_sems):
        h0 = pl.program_id(0) * heads
        pa_v = pa_ref[...]

        @pl.when(h0 == 0)
        def _():
            dgt_ref[...] = jnp.zeros_like(dgt_ref)
            dpa_ref[...] = jnp.zeros_like(dpa_ref)

        dc_ref[...] = jnp.zeros_like(dc_ref)

        def state_copies(n, slot):
            return [pltpu.make_async_copy(s_hbm.at[2 * h0 + b, i], s_buf.at[slot, b], s_sems.at[slot, b])
                    for b, (i, _, _) in enumerate(_a_tiles(nchunk - 1 - n, nchunk, heads))]

        for cp in state_copies(0, 0):
            cp.start()

        def step(n, carry, parity):
            d_sts, dpa = carry
            tiles = _a_tiles(nchunk - 1 - n, nchunk, heads)
            for cp in state_copies(n, parity):
                cp.wait()

            @pl.when(n + 1 < nchunk)
            def _():
                for cp in state_copies(n + 1, 1 - parity):
                    cp.start()

            sts = s_buf[parity]
            d_o_t = jnp.stack([do_ref[sl, ln] for _, sl, ln in tiles], axis=0)
            _, vjp_c = jax.vjp(lambda *a: _a_step(*a, h0), sts, *_a_load(tiles, c_ref, gt_ref), pa_v)
            d_prev, dcq, dck, dcv, dgts, dpa_i = vjp_c((d_sts, d_o_t))
            for b, (_, sl, ln) in enumerate(tiles):
                for r, dc in enumerate((dcq, dck, dcv)):
                    dc_ref[r, sl, ln] += dc[b]
                dgt_ref[sl, :] += dgts[b]
            return d_prev, dpa + dpa_i

        init = (jnp.zeros((2 * heads, A_DIM, A_DIM), F32), jnp.zeros((8, LANE), F32))
        _, dpa_out = lax.fori_loop(0, nchunk, lambda n, carry: step(n, carry, n % 2), init)
        dpa_ref[...] += dpa_out

    fixed = pl.BlockSpec((s, LANE), lambda g: (0, 0))
    return pl.pallas_call(
        body, name="delta_bwd", grid=(A_HEADS // heads,),
        in_specs=[trio, gates, small, cols(0), pl.BlockSpec(memory_space=pl.ANY)], out_specs=[trio, fixed, small],
        out_shape=[jax.ShapeDtypeStruct((3, s, A_WIDTH), F32), jax.ShapeDtypeStruct((s, LANE), F32),
                   jax.ShapeDtypeStruct((8, LANE), F32)],
        scratch_shapes=[pltpu.VMEM((2, 2 * heads, A_DIM, A_DIM), F32), pltpu.SemaphoreType.DMA((2, 2 * heads))],
        compiler_params=_params(("arbitrary",)),
    )(cqkv, proj, pa, d_o, states)


def _rope_tables(s):
    inv = ROPE_THETA ** (-jnp.arange(0, B_DIM, 2, dtype=F32) / B_DIM)
    ang = jnp.arange(s, dtype=F32)[:, None] * inv[None, :]
    cos, sin = jnp.cos(ang), jnp.sin(ang)
    return jnp.concatenate([cos, cos], axis=1), jnp.concatenate([-sin, sin], axis=1)


def _b_block(q_t, z_t, k3, v3, cos_q, sin_q, cos_k, sin_k, pb, n, nb):
    w = WINDOW
    def swap(t):
        return jnp.concatenate([t[:, B_DIM // 2:], t[:, :B_DIM // 2]], axis=1)

    grp = B_HEADS // B_KV
    qi = lax.broadcasted_iota(jnp.int32, (grp * w, 3 * w), 0) & (w - 1)
    kj = lax.broadcasted_iota(jnp.int32, (grp * w, 3 * w), 1)
    kpos = kj + (n - 1) * w
    mask = (jnp.abs(kj - w - qi) <= w) & (kpos >= 0) & (kpos < nb * w)
    lane = lax.broadcasted_iota(jnp.int32, (1, LANE), 1)
    qn, kn = pb[0:1, :B_DIM], pb[1:2, :B_DIM]
    cos_g = jnp.concatenate([cos_q] * grp, axis=0)
    sin_g = jnp.concatenate([sin_q] * grp, axis=0)
    def group(q, k, v, sink):
        k = _rms(k, kn)
        k = k * cos_k + swap(k) * sin_k
        q = _rms(q, qn)
        q = q * cos_g + swap(q) * sin_g
        s = _mm_nt(q, k) * (B_DIM ** -0.5)
        s = jnp.where(mask, s, -jnp.inf)
        m = jnp.maximum(jnp.max(s, axis=1, keepdims=True), sink)
        p = jnp.exp(s - m)
        p = p / (jnp.sum(p, axis=1, keepdims=True) + jnp.exp(sink - m))
        return _mm(p, v)

    stack = lambda ts: jnp.concatenate([t[None] for t in ts], axis=0)
    qs, ks, vs, sinks = [], [], [], []
    for hk in range(B_KV):
        heads = [hk * grp + g for g in range(grp)]
        ks.append(k3[:, hk * B_DIM:(hk + 1) * B_DIM])
        vs.append(v3[:, hk * B_DIM:(hk + 1) * B_DIM])
        qs.append(jnp.concatenate([q_t[:, hq * B_DIM:(hq + 1) * B_DIM] for hq in heads], axis=0))
        sinks.append(jnp.concatenate(
            [jnp.broadcast_to(jnp.sum(jnp.where(lane == hq, pb[2:3, :], 0.0), axis=1, keepdims=True), (w, 1))
             for hq in heads], axis=0))
    o = jax.vmap(group)(stack(qs), stack(ks), stack(vs), stack(sinks))
    outs = [o[hk, g * w:(g + 1) * w, :] for hk in range(B_KV) for g in range(grp)]
    return jnp.concatenate(outs, axis=1) * _silu(z_t)


def _b_specs(s):
    nb = s // WINDOW
    qsp = pl.BlockSpec((WINDOW, 512), lambda n: (n, P_QB // 512))
    zsp = pl.BlockSpec((WINDOW, 512), lambda n: (n, P_ZB // 512))

    def three(col, width):
        return [pl.BlockSpec((WINDOW, width), lambda n: (jnp.maximum(n - 1, 0), col)),
                pl.BlockSpec((WINDOW, width), lambda n: (n, col)),
                pl.BlockSpec((WINDOW, width), lambda n: (jnp.minimum(n + 1, nb - 1), col))]

    tab = pl.BlockSpec((WINDOW, B_DIM), lambda n: (n, 0))
    small = pl.BlockSpec((8, LANE), lambda n: (0, 0))
    specs = [qsp, zsp] + three(P_KB // LANE, LANE) + three(P_VB // LANE, LANE) + [tab, tab] + three(0, B_DIM) + three(0, B_DIM) + [small]
    return nb, specs


def _b_args(proj, cos2, sin2, pb):
    return (proj, proj, proj, proj, proj, proj, proj, proj, cos2, sin2, cos2, cos2, cos2, sin2, sin2, sin2, pb)


def _b_load(refs):
    (q_ref, z_ref, kp, kc, kx, vp, vc, vx, cq, sq, ckp, ckc, ckx, skp, skc, skx, pb_ref) = refs
    cat = lambda *r: jnp.concatenate([t[...] for t in r], axis=0)
    return (q_ref[...], z_ref[...], cat(kp, kc, kx), cat(vp, vc, vx), cq[...], sq[...], cat(ckp, ckc, ckx),
            cat(skp, skc, skx), pb_ref[...])


def _attn_b_fwd(proj, cos2, sin2, pb, mixed):
    s = proj.shape[0]
    nb, specs = _b_specs(s)

    def body(*refs):
        o_ref = refs[-1]
        args = _b_load(refs[:-2])
        o_ref[...] = _b_block(*args, pl.program_id(0), nb).astype(BF16)

    return pl.pallas_call(
        body, name="attn_b_fwd", grid=(nb,), in_specs=specs + [pl.BlockSpec(memory_space=pl.ANY)],
        out_specs=pl.BlockSpec((WINDOW, 512), lambda n: (n, A_WIDTH // 512)),
        out_shape=jax.ShapeDtypeStruct(mixed.shape, mixed.dtype), input_output_aliases={len(specs): 0},
        compiler_params=_params(("parallel",)),
    )(*_b_args(proj, cos2, sin2, pb), mixed)


def _attn_b_bwd(proj, cos2, sin2, pb, d_mixed):
    s = proj.shape[0]
    nb, specs = _b_specs(s)
    w = WINDOW

    def body(*refs):
        dm_ref, dq_ref, dz_ref, dk_ref, dv_ref, dpb_ref = refs[-6:]
        n = pl.program_id(0)
        q_t, z_t, k3, v3, cq, sq, ck, sk, pb_v = _b_load(refs[:-6])

        @pl.when(n == 0)
        def _():
            dk_ref[...] = jnp.zeros_like(dk_ref)
            dv_ref[...] = jnp.zeros_like(dv_ref)
            dpb_ref[...] = jnp.zeros_like(dpb_ref)

        def f(q_, z_, k_, v_, pb_):
            return _b_block(q_, z_, k_, v_, cq, sq, ck, sk, pb_, n, nb)

        _, vjp = jax.vjp(f, q_t, z_t, k3, v3, pb_v)
        dq, dz, dk3, dv3, dpb = vjp(dm_ref[...])
        dq_ref[...] = dq
        dz_ref[...] = dz
        dpb_ref[...] += dpb

        def add(j, cond):
            @pl.when(cond)
            def _():
                rows = pl.ds(pl.multiple_of((n - 1 + j) * w, w), w)
                dk_ref[rows, :] += dk3[j * w:(j + 1) * w, :]
                dv_ref[rows, :] += dv3[j * w:(j + 1) * w, :]

        add(0, n > 0)
        add(1, n >= 0)
        add(2, n < nb - 1)

    blk = pl.BlockSpec((w, 512), lambda n: (n, 0))
    whole = pl.BlockSpec((s, LANE), lambda n: (0, 0))
    small = pl.BlockSpec((8, LANE), lambda n: (0, 0))
    return pl.pallas_call(
        body, name="attn_b_bwd", grid=(nb,),
        in_specs=specs + [pl.BlockSpec((w, 512), lambda n: (n, 2))],
        out_specs=[blk, blk, whole, whole, small],
        out_shape=[jax.ShapeDtypeStruct((s, 512), F32), jax.ShapeDtypeStruct((s, 512), F32),
                   jax.ShapeDtypeStruct((s, LANE), F32), jax.ShapeDtypeStruct((s, LANE), F32),
                   jax.ShapeDtypeStruct((8, LANE), F32)],
        compiler_params=_params(("arbitrary",)),
    )(*_b_args(proj, cos2, sin2, pb), d_mixed)


def _mem_kv_fwd(mem, mem_norm_w, w_kv):
    def body(mem_ref, nw_ref, w_ref, kv_ref):
        mn = _rms(mem_ref[...], nw_ref[...]).astype(BF16)
        kv_ref[...] = jnp.dot(mn, w_ref[...], preferred_element_type=F32)

    return pl.pallas_call(
        body, name="mem_kv_fwd", out_shape=jax.ShapeDtypeStruct((MEM_LEN, 2 * C_HEADS * C_DIM), F32),
        compiler_params=_params(),
    )(mem, mem_norm_w, w_kv)


def _mem_kv_bwd(mem, mem_norm_w, w_kv, d_kv):
    def body(mem_ref, nw_ref, w_ref, g_ref, gw_ref, gn_ref):
        mn, vjp = jax.vjp(_rms, mem_ref[...], nw_ref[...])
        g = g_ref[...].astype(BF16)
        gw_ref[...] = lax.dot_general(mn.astype(BF16), g, (((0,), (0,)), ((), ())), preferred_element_type=F32)
        d_mn = lax.dot_general(g, w_ref[...], (((1,), (1,)), ((), ())), preferred_element_type=F32)
        gn_ref[...] = vjp(d_mn)[1]

    return pl.pallas_call(
        body, name="mem_kv_bwd",
        out_shape=[jax.ShapeDtypeStruct((D_MODEL, 2 * C_HEADS * C_DIM), F32), jax.ShapeDtypeStruct((1, D_MODEL), F32)],
        compiler_params=_params(),
    )(mem, mem_norm_w, w_kv, d_kv)


def _c_tile(q_t, z_t, kvm, pc):
    width = C_HEADS * C_DIM
    outs = []
    for h in range(C_HEADS):
        q = _rms(q_t[:, h * C_DIM:(h + 1) * C_DIM], pc[0:1, :])
        k = _rms(kvm[:, h * C_DIM:(h + 1) * C_DIM], pc[1:2, :])
        v = kvm[:, width + h * C_DIM:width + (h + 1) * C_DIM]
        s = _mm_nt(q, k) * (C_DIM ** -0.5)
        p = jnp.exp(s - jnp.max(s, axis=1, keepdims=True))
        p = p / jnp.sum(p, axis=1, keepdims=True)
        outs.append(_mm(p, v))
    return jnp.concatenate(outs, axis=1) * _silu(z_t)


def _attn_c_fwd(proj, kvm, pc, mixed, tq=256):
    s = proj.shape[0]

    def body(q_ref, z_ref, kv_ref, pc_ref, mixed_ref, o_ref):
        o_ref[...] = _c_tile(q_ref[...], z_ref[...], kv_ref[...], pc_ref[...]).astype(BF16)

    return pl.pallas_call(
        body, name="attn_c_fwd", grid=(s // tq,),
        in_specs=[pl.BlockSpec((tq, 512), lambda i: (i, P_QC // 512)), pl.BlockSpec((tq, 512), lambda i: (i, P_ZC // 512)),
                  pl.BlockSpec(kvm.shape, lambda i: (0, 0)), pl.BlockSpec((8, LANE), lambda i: (0, 0)),
                  pl.BlockSpec(memory_space=pl.ANY)],
        out_specs=pl.BlockSpec((tq, 512), lambda i: (i, (A_WIDTH + 512) // 512)),
        out_shape=jax.ShapeDtypeStruct(mixed.shape, mixed.dtype), input_output_aliases={4: 0},
        compiler_params=_params(("parallel",)),
    )(proj, proj, kvm, pc, mixed)


def _attn_c_bwd(proj, kvm, pc, d_mixed, tq=256):
    s = proj.shape[0]

    def body(q_ref, z_ref, kv_ref, pc_ref, dm_ref, dq_ref, dz_ref, dkv_ref, dpc_ref):
        @pl.when(pl.program_id(0) == 0)
        def _():
            dkv_ref[...] = jnp.zeros_like(dkv_ref)
            dpc_ref[...] = jnp.zeros_like(dpc_ref)

        _, vjp = jax.vjp(_c_tile, q_ref[...], z_ref[...], kv_ref[...], pc_ref[...])
        dq, dz, dkv, dpc = vjp(dm_ref[...])
        dq_ref[...] = dq
        dz_ref[...] = dz
        dkv_ref[...] += dkv
        dpc_ref[...] += dpc

    blk = pl.BlockSpec((tq, 512), lambda i: (i, 0))
    kvs = pl.BlockSpec(kvm.shape, lambda i: (0, 0))
    small = pl.BlockSpec((8, LANE), lambda i: (0, 0))
    return pl.pallas_call(
        body, name="attn_c_bwd", grid=(s // tq,),
        in_specs=[pl.BlockSpec((tq, 512), lambda i: (i, P_QC // 512)), pl.BlockSpec((tq, 512), lambda i: (i, P_ZC // 512)),
                  kvs, small, pl.BlockSpec((tq, 512), lambda i: (i, 3))],
        out_specs=[blk, blk, kvs, small],
        out_shape=[jax.ShapeDtypeStruct((s, 512), F32), jax.ShapeDtypeStruct((s, 512), F32),
                   jax.ShapeDtypeStruct(kvm.shape, F32), jax.ShapeDtypeStruct((8, LANE), F32)],
        compiler_params=_params(("arbitrary",)),
    )(proj, proj, kvm, pc, d_mixed)


def _pad_row(v, width=LANE):
    v = v.reshape(1, -1)
    return jnp.pad(v, ((0, 0), (0, width - v.shape[1])))


def _local_step(x, mem, target, norm_w, w_perm_t, conv_w, pa, pb, pc, mem_norm_w, w_kv, w_out):
    s = x.shape[0]
    cos2, sin2 = _rope_tables(s)
    hn = _rms_fwd(x, norm_w)
    wide = dict(tm=1024, tn=512, tk=2048)
    proj = _matmul(hn, w_perm_t, "nt", F32, "mm_proj", **wide)
    cqkv = _conv_fwd(proj, conv_w)
    mixed, o_sum, states = _delta_fwd(cqkv, proj, pa)
    mixed = _attn_b_fwd(proj, cos2, sin2, pb, mixed)
    kvm = _mem_kv_fwd(mem, mem_norm_w, w_kv)
    mixed = _attn_c_fwd(proj, kvm, pc, mixed)
    mo = _matmul(mixed, w_out, "nn", F32, "mm_out", **wide)
    dy, dyb, loss_parts = _loss_dy(x, mo, target)

    d_mixed = _matmul(dyb, w_out, "nt", F32, "mm_dmixed", **wide)
    g_w_out = _matmul(mixed, dyb, "tn", F32, "mm_gwout", **wide)
    d_qc, d_zc, d_kvm, d_pc = _attn_c_bwd(proj, kvm, pc, d_mixed)
    g_w_kv, g_mem_norm = _mem_kv_bwd(mem, mem_norm_w, w_kv, d_kvm)
    d_qb, d_zb, d_kb, d_vb, d_pb = _attn_b_bwd(proj, cos2, sin2, pb, d_mixed)
    d_o, d_za, d_pa_out = _delta_out_bwd(o_sum, proj, pa, d_mixed)
    d_c, d_gt, d_pa_scan = _delta_bwd(cqkv, proj, pa, d_o, states)
    d_pa = d_pa_out + d_pa_scan
    d_qkv, g_conv = _conv_bwd(proj, conv_w, d_c)
    d_proj = jnp.concatenate([d_qkv, d_za, d_qb, d_zb, d_qc, d_zc, d_kb, d_vb, d_gt,
                              jnp.zeros((s, P_WIDTH - P_GT - LANE), F32)], axis=1).astype(BF16)
    d_hn = _matmul(d_proj, w_perm_t, "nn", F32, "mm_dhn", tm=1024, tn=2048, tk=512)
    g_w_perm_t = _matmul(d_proj, hn, "tn", F32, "mm_gwin", tm=512, tn=1024, tk=2048)
    g_x, g_norm = _rms_bwd(x, norm_w, d_hn, dy)
    return dict(loss_parts=loss_parts, g_x=g_x, g_norm=g_norm, g_w_perm_t=g_w_perm_t, g_conv=g_conv, d_pa=d_pa,
                d_pb=d_pb, d_pc=d_pc, g_mem_norm=g_mem_norm, g_w_kv=g_w_kv, g_w_out=g_w_out)


_SEGMENTS = ((0, O_GT, 0), (O_GT, O_QB, P_GT), (O_QB, O_KB, P_QB), (O_KB, O_VB, P_KB), (O_VB, O_ZB, P_VB),
             (O_ZB, O_QC, P_ZB), (O_QC, O_ZC, P_QC), (O_ZC, IN_WIDTH, P_ZC))


def _permute_blocks(w4):
    parts = []
    for first, end, _ in sorted(_SEGMENTS, key=lambda seg: seg[2]):
        row = first
        while row < end:
            k = row // W_IN_BLOCK
            stop = min(end, (k + 1) * W_IN_BLOCK)
            parts.append(w4[k][row - k * W_IN_BLOCK:stop - k * W_IN_BLOCK, :])
            row = stop
    parts.append(jnp.zeros((P_WIDTH - IN_WIDTH, w4.shape[2]), w4.dtype))
    return jnp.concatenate(parts, axis=0)


def _unpermute_blocks(g):
    blocks = []
    for k in range(N_CHIPS):
        lo, hi = k * W_IN_BLOCK, (k + 1) * W_IN_BLOCK
        parts = [g[p + max(first, lo) - first:p + min(end, hi) - first, :]
                 for first, end, p in _SEGMENTS if max(first, lo) < min(end, hi)]
        blocks.append(jnp.concatenate(parts, axis=0))
    return jnp.stack(blocks, axis=0)


HBM = pl.BlockSpec(memory_space=pltpu.HBM)


def _place():
    x, y, c = lax.axis_index("x"), lax.axis_index("y"), lax.axis_index("c")
    chips = [(1 - x, y), (x, 1 - y), (1 - x, 1 - y)]
    return x, y, c, 2 * x + y, chips, [2 * cx + cy for cx, cy in chips]


PIECE_ROWS_CAP = 600


def _remote(src, dst, send_sems, recv_sems, k, to):
    return pltpu.make_async_remote_copy(src_ref=src, dst_ref=dst, send_sem=send_sems.at[k], recv_sem=recv_sems.at[k],
                                        device_id=to, device_id_type=MESH)


def _half_cols(ref, c):
    half = ref.shape[-1] // 2
    return pl.ds(pl.multiple_of(c * half, LANE), half)


def _all_gather_weights(w_in_b, w_out_b, w_kv_b, conv_b):
    bigs = (w_in_b, w_out_b, w_kv_b)
    n_big = len(bigs)

    def body(*refs):
        srcs, conv_src = refs[:n_big], refs[n_big]
        dsts, conv_dst = refs[n_big + 1:2 * n_big + 1], refs[2 * n_big + 1]
        send_sems, recv_sems, local_sems = refs[2 * n_big + 2:]
        x, y, c, me, chips, chip_ids = _place()
        sibling = (x, y, 1 - c)
        local = [pltpu.make_async_copy(src, dst.at[me], local_sems.at[a]) for a, (src, dst) in enumerate(zip(srcs, dsts))]
        local.append(pltpu.make_async_copy(conv_src, conv_dst.at[me], local_sems.at[n_big]))
        for cp in local:
            cp.start()
        sends = []
        for a, (src, dst) in enumerate(zip(srcs, dsts)):
            mine = _half_cols(src, c)
            for j, chip in enumerate(chips):
                sends.append(_remote(src.at[:, mine], dst.at[me, :, mine], send_sems, recv_sems, 6 * a + j, (*chip, c)))
        for j, chip in enumerate(chips):
            sends.append(_remote(conv_src, conv_dst.at[me], send_sems, recv_sems, 6 * n_big + j, (*chip, c)))
        for cp in sends:
            cp.start()
        passed = []
        for a, (src, dst) in enumerate(zip(srcs, dsts)):
            mine = _half_cols(src, c)
            for j, cid in enumerate(chip_ids):
                landed = dst.at[cid, :, mine]
                _remote(landed, landed, send_sems, recv_sems, 6 * a + j, sibling).wait_recv()
                cp = _remote(landed, landed, send_sems, recv_sems, 6 * a + 3 + j, sibling)
                cp.start()
                passed.append(cp)
        for a, (src, dst) in enumerate(zip(srcs, dsts)):
            other = _half_cols(src, 1 - c)
            for j, cid in enumerate(chip_ids):
                landed = dst.at[cid, :, other]
                _remote(landed, landed, send_sems, recv_sems, 6 * a + 3 + j, sibling).wait_recv()
        for j, cid in enumerate(chip_ids):
            _remote(conv_src, conv_dst.at[cid], send_sems, recv_sems, 6 * n_big + j, sibling).wait_recv()
        for cp in sends + passed:
            cp.wait_send()
        for cp in local:
            cp.wait()

    n_sem = 6 * n_big + 3
    return pl.pallas_call(
        body, name="all_gather_weights",
        out_shape=[jax.ShapeDtypeStruct((N_CHIPS,) + w.shape, w.dtype) for w in bigs + (conv_b,)],
        in_specs=[pl.BlockSpec(memory_space=pltpu.VMEM)] * (n_big + 1), out_specs=[HBM] * (n_big + 1),
        scratch_shapes=[pltpu.SemaphoreType.DMA((n_sem,)), pltpu.SemaphoreType.DMA((n_sem,)),
                        pltpu.SemaphoreType.DMA((n_big + 1,))],
        compiler_params=_params(),
    )(*bigs, conv_b)


def _pair_exchange(grads):
    n = len(grads)
    pieces = [_row_tile(g.shape[1]) for g in grads]

    def body(*refs):
        srcs, gots = refs[:n], refs[n:2 * n]
        stages = refs[2 * n:3 * n]
        send_sems, recv_sems, load_sems = refs[3 * n:]
        x, y, c, _, _, _ = _place()
        sibling = (x, y, 1 - c)
        for a in range(n):
            slabs, rows, _ = gots[a].shape
            piece = pieces[a]
            per_slab = rows // piece
            theirs = _half_cols(srcs[a], 1 - c)
            loads, sends = [], []
            for i in range(slabs * per_slab):
                k, r, slot = i // per_slab, i % per_slab, i % 2
                part = pl.ds(r * piece, piece)
                loads.append(pltpu.make_async_copy(srcs[a].at[k, part, theirs], stages[a].at[slot], load_sems.at[2 * a + slot]))
                sends.append(pltpu.make_async_remote_copy(
                    src_ref=stages[a].at[slot], dst_ref=gots[a].at[k, part, :],
                    send_sem=send_sems.at[2 * a + slot], recv_sem=recv_sems.at[a], device_id=sibling, device_id_type=MESH))
            loads[0].start()
            for i in range(len(loads)):
                loads[i].wait()
                sends[i].start()
                if i + 1 < len(loads):
                    if i >= 1:
                        sends[i - 1].wait_send()
                    loads[i + 1].start()
            for cp in sends[-2:]:
                cp.wait_send()
        for a in range(n):
            whole = srcs[a].at[:, :, _half_cols(srcs[a], c)]
            pltpu.make_async_remote_copy(src_ref=whole, dst_ref=gots[a], send_sem=send_sems.at[2 * a],
                                         recv_sem=recv_sems.at[a], device_id=sibling, device_id_type=MESH).wait_recv()

    halves = [jax.ShapeDtypeStruct((g.shape[0], g.shape[1], g.shape[2] // 2), g.dtype) for g in grads]
    return pl.pallas_call(
        body, name="grad_pair_exchange", out_shape=halves, in_specs=[HBM] * n, out_specs=[HBM] * n,
        scratch_shapes=[pltpu.VMEM((2, piece, g.shape[2] // 2), g.dtype) for piece, g in zip(pieces, grads)]
        + [pltpu.SemaphoreType.DMA((2 * n,)), pltpu.SemaphoreType.DMA((n,)), pltpu.SemaphoreType.DMA((2 * n,))],
        compiler_params=_params(),
    )(*grads)


def _chip_exchange(halves):
    n = len(halves)

    def body(*refs):
        srcs, lands = refs[:n], refs[n:2 * n]
        send_sems, recv_sems = refs[2 * n:]
        x, y, c, me, chips, chip_ids = _place()
        gives = []
        for a in range(n):
            for j, (chip, cid) in enumerate(zip(chips, chip_ids)):
                give = _remote(srcs[a].at[cid], lands[a].at[j], send_sems, recv_sems, 3 * a + j, (*chip, c))
                give.start()
                gives.append(give)
        for a in range(n):
            for j, cid in enumerate(chip_ids):
                _remote(srcs[a].at[cid], lands[a].at[j], send_sems, recv_sems, 3 * a + j, (x, y, c)).wait_recv()
        for give in gives:
            give.wait_send()

    return pl.pallas_call(
        body, name="grad_chip_exchange",
        out_shape=[jax.ShapeDtypeStruct((N_CHIPS - 1,) + h.shape[1:], h.dtype) for h in halves],
        in_specs=[HBM] * n, out_specs=[HBM] * n,
        scratch_shapes=[pltpu.SemaphoreType.DMA((3 * n,)), pltpu.SemaphoreType.DMA((3 * n,))],
    )(*halves)


def _pair_gather(halves):
    n = len(halves)

    def body(*refs):
        srcs, fulls = refs[:n], refs[n:2 * n]
        send_sems, recv_sems, local_sems = refs[2 * n:]
        x, y, c, _, _, _ = _place()
        copies = []
        for a in range(n):
            mine = _half_cols(fulls[a], c)
            keep = pltpu.make_async_copy(srcs[a], fulls[a].at[:, mine], local_sems.at[a])
            keep.start()
            give = _remote(srcs[a], fulls[a].at[:, mine], send_sems, recv_sems, a, (x, y, 1 - c))
            give.start()
            copies += [keep, give]
        for a in range(n):
            other = _half_cols(fulls[a], 1 - c)
            copies[2 * a].wait()
            copies[2 * a + 1].wait_send()
            _remote(srcs[a], fulls[a].at[:, other], send_sems, recv_sems, a, (x, y, 1 - c)).wait_recv()

    return pl.pallas_call(
        body, name="grad_pair_gather",
        out_shape=[jax.ShapeDtypeStruct((h.shape[0], 2 * h.shape[1]), h.dtype) for h in halves],
        in_specs=[pl.BlockSpec(memory_space=pltpu.VMEM)] * n, out_specs=[HBM] * n,
        scratch_shapes=[pltpu.SemaphoreType.DMA((n,)), pltpu.SemaphoreType.DMA((n,)), pltpu.SemaphoreType.DMA((n,))],
    )(*halves)


def _all_reduce_small(p):
    n_dev = 8

    def body(p_ref, o_ref, land, send_sems, recv_sems):
        x, y, c = lax.axis_index("x"), lax.axis_index("y"), lax.axis_index("c")
        me = 4 * x + 2 * y + c
        land[me] = p_ref[...]
        sends = []
        for k in range(1, n_dev):
            fx, fy, fc = (k >> 2) & 1, (k >> 1) & 1, k & 1
            to = (x ^ fx, y ^ fy, c ^ fc)
            cp = _remote(p_ref, land.at[me], send_sems, recv_sems, k - 1, to)
            cp.start()
            sends.append(cp)
        for k in range(1, n_dev):
            _remote(p_ref, land.at[me ^ k], send_sems, recv_sems, k - 1, (x, y, c)).wait_recv()
        total = land[0]
        for d in range(1, n_dev):
            total = total + land[d]
        o_ref[...] = total
        for cp in sends:
            cp.wait_send()

    vm = pl.BlockSpec(memory_space=pltpu.VMEM)
    return pl.pallas_call(
        body, name="all_reduce_small", out_shape=jax.ShapeDtypeStruct(p.shape, p.dtype), in_specs=[vm], out_specs=vm,
        scratch_shapes=[pltpu.VMEM((n_dev,) + p.shape, p.dtype), pltpu.SemaphoreType.DMA((n_dev - 1,)),
                        pltpu.SemaphoreType.DMA((n_dev - 1,))],
    )(p)


def _row_tile(rows):
    fits = [t for t in range(8, min(rows, PIECE_ROWS_CAP) + 1, 8) if rows % t == 0]
    return max(fits) if fits else rows


def _pair_sum(full, got, core, name):
    n, r, c = got.shape
    tr = _row_tile(r)

    def body(core_ref, a_ref, b_ref, o_ref):
        o_ref[...] = (a_ref[...] + b_ref[...]).astype(BF16)

    blk = pl.BlockSpec((None, tr, c), lambda i, j, core_ref: (i, j, 0))
    grid_spec = pltpu.PrefetchScalarGridSpec(
        num_scalar_prefetch=1, grid=(n, r // tr),
        in_specs=[pl.BlockSpec((None, tr, c), lambda i, j, core_ref: (i, j, core_ref[0])), blk], out_specs=blk)
    return pl.pallas_call(body, name=name, grid_spec=grid_spec, out_shape=jax.ShapeDtypeStruct(got.shape, BF16),
                          compiler_params=_params(("parallel", "parallel")))(core, full, got)


def _chip_sum(full, got, land, place, name):
    n, r, c = land.shape
    tr = _row_tile(r)

    def body(place_ref, a_ref, b_ref, l_ref, o_ref):
        total = a_ref[...] + b_ref[...]
        for j in range(n):
            total = total + l_ref[j].astype(F32)
        o_ref[...] = total

    grid_spec = pltpu.PrefetchScalarGridSpec(
        num_scalar_prefetch=1, grid=(r // tr,),
        in_specs=[pl.BlockSpec((None, tr, c), lambda i, p: (p[0], i, p[1])),
                  pl.BlockSpec((None, tr, c), lambda i, p: (p[0], i, 0)),
                  pl.BlockSpec((n, tr, c), lambda i, p: (0, i, 0))],
        out_specs=pl.BlockSpec((tr, c), lambda i, p: (i, 0)))
    return pl.pallas_call(body, name=name, grid_spec=grid_spec, out_shape=jax.ShapeDtypeStruct((r, c), F32),
                          compiler_params=_params(("parallel",)))(place, full, got, land)


def _adamw(w, g, m, v, name):
    r, c = w.shape
    tr = _row_tile(r)
    tc = 1024 if c % 1024 == 0 else c

    def body(w_ref, g_ref, m_ref, v_ref, d_ref, mo_ref, vo_ref):
        g_ = g_ref[...]
        m2 = ADAM_B1 * m_ref[...] + (1.0 - ADAM_B1) * g_
        v2 = ADAM_B2 * v_ref[...] + (1.0 - ADAM_B2) * jnp.square(g_)
        m_hat = m2 / (1.0 - ADAM_B1 ** ADAM_STEP)
        v_hat = v2 / (1.0 - ADAM_B2 ** ADAM_STEP)
        d_ref[...] = -ADAM_LR * (m_hat / (jnp.sqrt(v_hat) + ADAM_EPS) + ADAM_WD * w_ref[...])
        mo_ref[...] = m2
        vo_ref[...] = v2

    blk = pl.BlockSpec((tr, tc), lambda i, j: (i, j))
    return pl.pallas_call(body, name=name, grid=(r // tr, c // tc), in_specs=[blk] * 4, out_specs=[blk] * 3,
                          out_shape=[jax.ShapeDtypeStruct(w.shape, F32)] * 3,
                          compiler_params=_params(("parallel", "parallel")))(w, g, m, v)


SMALL_NAMES = ("norm_w", "mem_norm_w", "o_norm_a", "q_norm_c", "k_norm_c", "q_norm_b", "k_norm_b",
               "a_log_fwd", "a_log_bwd", "dt_bias_fwd", "dt_bias_bwd", "sink_b")
SMALL_SIZES = (2048, 2048, 128, 128, 128, 64, 64, 8, 8, 8, 8, 8)
SMALL_LOSS = sum(SMALL_SIZES)
SMALL_CONV = 5120
SMALL_TOTAL = SMALL_CONV + CONV_K * 3 * A_WIDTH
SMALL_ROWS = SMALL_TOTAL // LANE


def _pack_small(parts, extra=None, conv=None):
    vec = [parts[n].reshape(-1) for n in SMALL_NAMES]
    vec.append(jnp.zeros((1,), F32) if extra is None else extra.reshape(1))
    vec.append(jnp.zeros((SMALL_CONV - SMALL_LOSS - 1,), F32))
    vec.append(jnp.zeros((SMALL_TOTAL - SMALL_CONV,), F32) if conv is None else conv.reshape(-1))
    return jnp.concatenate(vec).reshape(SMALL_ROWS, LANE)


def _unpack_small(packed):
    flat = packed.reshape(-1)
    out, off = {}, 0
    for n, size in zip(SMALL_NAMES, SMALL_SIZES):
        out[n] = flat[off:off + size].reshape(1, size)
        off += size
    return out


WEIGHT_ORDER = ("norm_w", "w_in", "conv_w_a", "a_log_fwd", "a_log_bwd", "dt_bias_fwd", "dt_bias_bwd", "o_norm_a",
                "q_norm_b", "k_norm_b", "sink_b", "mem_norm_w", "w_mem_kv", "q_norm_c", "k_norm_c", "w_out")


def kernel(x, mem, norm_w, w_in, conv_w_a, a_log_fwd, a_log_bwd, dt_bias_fwd, dt_bias_bwd, o_norm_a, q_norm_b, k_norm_b, sink_b, mem_norm_w, w_mem_kv, q_norm_c, k_norm_c, w_out, loss_target, m_norm_w, m_w_in, m_conv_w_a, m_a_log_fwd, m_a_log_bwd, m_dt_bias_fwd, m_dt_bias_bwd, m_o_norm_a, m_q_norm_b, m_k_norm_b, m_sink_b, m_mem_norm_w, m_w_mem_kv, m_q_norm_c, m_k_norm_c, m_w_out, v_norm_w, v_w_in, v_conv_w_a, v_a_log_fwd, v_a_log_bwd, v_dt_bias_fwd, v_dt_bias_bwd, v_o_norm_a, v_q_norm_b, v_k_norm_b, v_sink_b, v_mem_norm_w, v_w_mem_kv, v_q_norm_c, v_k_norm_c, v_w_out):
    weights = dict(norm_w=norm_w, w_in=w_in, conv_w_a=conv_w_a, a_log_fwd=a_log_fwd, a_log_bwd=a_log_bwd,
                   dt_bias_fwd=dt_bias_fwd, dt_bias_bwd=dt_bias_bwd, o_norm_a=o_norm_a, q_norm_b=q_norm_b,
                   k_norm_b=k_norm_b, sink_b=sink_b, mem_norm_w=mem_norm_w, w_mem_kv=w_mem_kv, q_norm_c=q_norm_c,
                   k_norm_c=k_norm_c, w_out=w_out)
    mom1 = dict(norm_w=m_norm_w, w_in=m_w_in, conv_w_a=m_conv_w_a, a_log_fwd=m_a_log_fwd, a_log_bwd=m_a_log_bwd,
                dt_bias_fwd=m_dt_bias_fwd, dt_bias_bwd=m_dt_bias_bwd, o_norm_a=m_o_norm_a, q_norm_b=m_q_norm_b,
                k_norm_b=m_k_norm_b, sink_b=m_sink_b, mem_norm_w=m_mem_norm_w, w_mem_kv=m_w_mem_kv,
                q_norm_c=m_q_norm_c, k_norm_c=m_k_norm_c, w_out=m_w_out)
    mom2 = dict(norm_w=v_norm_w, w_in=v_w_in, conv_w_a=v_conv_w_a, a_log_fwd=v_a_log_fwd, a_log_bwd=v_a_log_bwd,
                dt_bias_fwd=v_dt_bias_fwd, dt_bias_bwd=v_dt_bias_bwd, o_norm_a=v_o_norm_a, q_norm_b=v_q_norm_b,
                k_norm_b=v_k_norm_b, sink_b=v_sink_b, mem_norm_w=v_mem_norm_w, w_mem_kv=v_w_mem_kv,
                q_norm_c=v_q_norm_c, k_norm_c=v_k_norm_c, w_out=v_w_out)
    chip = 2 * lax.axis_index("x") + lax.axis_index("y")

    w_in4, w_out4, w_kv4, conv4 = _all_gather_weights(jnp.transpose(w_in[0]).astype(BF16), w_out[0].astype(BF16),
                                                      w_mem_kv[0].astype(BF16), conv_w_a[0])
    w_perm_t = _permute_blocks(w_in4)
    w_out_full = w_out4.reshape(D_MODEL, D_MODEL)
    w_kv_full = w_kv4.reshape(D_MODEL, 2 * C_HEADS * C_DIM)
    conv_full = jnp.transpose(conv4, (1, 0, 2)).reshape(CONV_K, 3 * A_WIDTH)
    pa = jnp.concatenate([_pad_row(a_log_fwd), _pad_row(a_log_bwd), _pad_row(dt_bias_fwd), _pad_row(dt_bias_bwd),
                          _pad_row(o_norm_a), jnp.zeros((3, LANE), F32)], axis=0)
    pb = jnp.concatenate([_pad_row(q_norm_b), _pad_row(k_norm_b), _pad_row(sink_b), jnp.zeros((5, LANE), F32)], axis=0)
    pc = jnp.concatenate([_pad_row(q_norm_c), _pad_row(k_norm_c), jnp.zeros((6, LANE), F32)], axis=0)

    r = _local_step(x[0], mem[0], loss_target[0], norm_w, w_perm_t, conv_full, pa, pb, pc, mem_norm_w, w_kv_full,
                    w_out_full)

    g_in4 = _unpermute_blocks(r["g_w_perm_t"])
    g_out4 = r["g_w_out"].reshape(N_CHIPS, D_MODEL // N_CHIPS, D_MODEL)
    g_kv4 = r["g_w_kv"].reshape(N_CHIPS, D_MODEL // N_CHIPS, 2 * C_HEADS * C_DIM)
    full = [g_in4, g_out4, g_kv4]
    core = lax.axis_index("c").astype(jnp.int32).reshape(1)
    got = _pair_exchange(full)
    pair = [_pair_sum(a, b, core, "grad_pair_sum_%d" % i) for i, (a, b) in enumerate(zip(full, got))]
    lands = _chip_exchange(pair)
    place = jnp.stack([chip, lax.axis_index("c")]).astype(jnp.int32)
    reduced = [_chip_sum(a, b, l, place, "grad_chip_sum_%d" % i) for i, (a, b, l) in enumerate(zip(full, got, lands))]
    g_w_in_t, g_w_out, g_w_kv = _pair_gather(reduced)

    d_pa, d_pb, d_pc = r["d_pa"], r["d_pb"], r["d_pc"]
    small_g = dict(norm_w=r["g_norm"], mem_norm_w=r["g_mem_norm"], o_norm_a=d_pa[4], q_norm_c=d_pc[0], k_norm_c=d_pc[1],
                   q_norm_b=d_pb[0, :B_DIM], k_norm_b=d_pb[1, :B_DIM], a_log_fwd=d_pa[0, :A_HEADS],
                   a_log_bwd=d_pa[1, :A_HEADS], dt_bias_fwd=d_pa[2, :A_HEADS], dt_bias_bwd=d_pa[3, :A_HEADS],
                   sink_b=d_pb[2, :B_HEADS])
    packed = _all_reduce_small(_pack_small(small_g, jnp.sum(r["loss_parts"][:, 0, 0]), r["g_conv"]))
    flat = packed.reshape(-1)
    loss = flat[SMALL_LOSS]
    conv_sum = flat[SMALL_CONV:].reshape(CONV_K, 3 * A_WIDTH)
    conv_cols = 3 * A_WIDTH // N_CHIPS
    g_conv = lax.dynamic_slice(conv_sum, (0, chip * conv_cols), (CONV_K, conv_cols))

    grads = _unpack_small(packed)
    grads.update(w_in=jnp.transpose(g_w_in_t), w_mem_kv=g_w_kv, w_out=g_w_out, conv_w_a=g_conv)
    delta, new_m, new_v = {}, {}, {}
    for n in ("w_mem_kv", "w_out", "conv_w_a"):
        delta[n], new_m[n], new_v[n] = _adamw(weights[n][0], grads[n], mom1[n][0], mom2[n][0], "adamw_" + n)
    stepped = _adamw(jnp.transpose(w_in[0]), g_w_in_t, jnp.transpose(m_w_in[0]), jnp.transpose(v_w_in[0]), "adamw_w_in")
    delta["w_in"], new_m["w_in"], new_v["w_in"] = (jnp.transpose(t) for t in stepped)
    d_s, m_s, v_s = _adamw(_pack_small(weights), packed, _pack_small(mom1), _pack_small(mom2), "adamw_small")
    d_s, m_s, v_s = _unpack_small(d_s), _unpack_small(m_s), _unpack_small(v_s)
    for n in SMALL_NAMES:
        delta[n], new_m[n], new_v[n] = d_s[n], m_s[n], v_s[n]

    def shaped(tree):
        return [tree[n].reshape(weights[n].shape) for n in WEIGHT_ORDER]

    return (loss, r["g_x"].reshape(x.shape), *shaped(grads), *shaped(delta), *shaped(new_m), *shaped(new_v))
```

```python
import functools

import jax
import jax.numpy as jnp
from jax import lax
from jax.experimental import pallas as pl
from jax.experimental.pallas import tpu as pltpu

F32 = jnp.float32
BF16 = jnp.bfloat16
HI = lax.Precision.HIGHEST
MESH = pl.DeviceIdType.MESH

D_MODEL = 2048
A_WIDTH = 1024
A_HEADS = 8
A_DIM = 128
CONV_K = 5
CHUNK = 64
B_HEADS = 8
B_KV = 2
B_DIM = 64
WINDOW = 128
C_HEADS = 4
C_DIM = 128
MEM_LEN = 256
ROPE_THETA = 10000.0
EPS = 1e-6
IN_WIDTH = 6432
N_CHIPS = 4
W_IN_BLOCK = IN_WIDTH // N_CHIPS

LANE = 128
P_QA, P_KA, P_VA, P_ZA = 0, 1024, 2048, 3072
P_QB, P_ZB, P_QC, P_ZC = 4096, 4608, 5120, 5632
P_KB, P_VB, P_GT = 6144, 6272, 6400
P_WIDTH = 6656
O_GT, O_QB, O_KB, O_VB, O_ZB, O_QC, O_ZC = 4096, 4128, 4640, 4768, 4896, 5408, 5920

ADAM_LR, ADAM_B1, ADAM_B2, ADAM_EPS, ADAM_WD, ADAM_STEP = 0.001, 0.9, 0.999, 1e-08, 0.01, 10

VMEM_LIMIT = 56 * 1024 * 1024


def _params(sem=None):
    return pltpu.CompilerParams(dimension_semantics=sem, vmem_limit_bytes=VMEM_LIMIT)


def _dot(a, b, dims=(((1,), (0,)), ((), ())), precision=HI):
    return lax.dot_general(a, b, dims, precision=precision, preferred_element_type=F32)


def _dot_nt(a, b, precision=HI):
    return _dot(a, b, (((1,), (1,)), ((), ())), precision)


def _dot_tn(a, b, precision=HI):
    return _dot(a, b, (((0,), (0,)), ((), ())), precision)


_NN = (((1,), (0,)), ((), ()))
_NT = (((1,), (1,)), ((), ()))
_TN = (((0,), (0,)), ((), ()))


def _bdot(a, b, dims):
    return lax.dot_general(a.astype(BF16), b.astype(BF16), dims, preferred_element_type=F32)


@jax.custom_vjp
def _mm(a, b):
    return _bdot(a, b, _NN)


_mm.defvjp(lambda a, b: (_bdot(a, b, _NN), (a, b)),
           lambda res, ct: (_bdot(ct, res[1], _NT), _bdot(res[0], ct, _TN)))


@jax.custom_vjp
def _mm_nt(a, b):
    return _bdot(a, b, _NT)


_mm_nt.defvjp(lambda a, b: (_bdot(a, b, _NT), (a, b)),
              lambda res, ct: (_bdot(ct, res[1], _NN), _bdot(ct, res[0], _TN)))


@jax.custom_vjp
def _mm_tn(a, b):
    return _bdot(a, b, _TN)


_mm_tn.defvjp(lambda a, b: (_bdot(a, b, _TN), (a, b)),
              lambda res, ct: (_bdot(res[1], ct, _NT), _bdot(res[0], ct, _NN)))


def _rms(t, w):
    return t * lax.rsqrt(jnp.mean(t * t, axis=-1, keepdims=True) + EPS) * w


def _l2(t):
    return t * lax.rsqrt(jnp.sum(t * t, axis=-1, keepdims=True) + EPS)


def _silu(t):
    return t * jax.nn.sigmoid(t)


def _softplus(t):
    return jnp.maximum(t, 0.0) + jnp.log1p(jnp.exp(-jnp.abs(t)))


def _matmul(a, b, mode, out_dtype, name, tm=512, tn=512, tk=512, ride=None):
    (m, k) = a.shape[::-1] if mode == "tn" else a.shape
    n = b.shape[0] if mode == "nt" else b.shape[1]
    tm, tn, tk = min(tm, m), min(tn, n), min(tk, k)
    assert m % tm == 0 and n % tn == 0 and k % tk == 0, (m, n, k, tm, tn, tk)
    if mode == "nn":
        a_spec = pl.BlockSpec((tm, tk), lambda i, j, kk: (i, kk))
        b_spec = pl.BlockSpec((tk, tn), lambda i, j, kk: (kk, j))
        dims = (((1,), (0,)), ((), ()))
    elif mode == "nt":
        a_spec = pl.BlockSpec((tm, tk), lambda i, j, kk: (i, kk))
        b_spec = pl.BlockSpec((tn, tk), lambda i, j, kk: (j, kk))
        dims = (((1,), (1,)), ((), ()))
    else:
        a_spec = pl.BlockSpec((tk, tm), lambda i, j, kk: (kk, i))
        b_spec = pl.BlockSpec((tk, tn), lambda i, j, kk: (kk, j))
        dims = (((0,), (0,)), ((), ()))
    nk = k // tk
    grid = (m // tm, n // tn, nk)
    n_in = len(ride.operands) if ride else 0
    n_out = len(ride.out_shapes) if ride else 0

    def body(*refs):
        a_ref, b_ref, o_ref = refs[0], refs[1], refs[2 + n_in]
        scratch = refs[3 + n_in + n_out:]
        step = (pl.program_id(0) * grid[1] + pl.program_id(1)) * nk + pl.program_id(2)
        riders = (refs[2:2 + n_in], refs[3 + n_in:3 + n_in + n_out], scratch[(0 if nk == 1 else 1):])
        if ride:
            pl.when(step == 0)(lambda: ride.start(*riders))
        if nk == 1:
            o_ref[...] = _bdot(a_ref[...], b_ref[...], dims).astype(out_dtype)
        else:
            acc_ref, kk = scratch[0], pl.program_id(2)

            @pl.when(kk == 0)
            def _():
                acc_ref[...] = jnp.zeros_like(acc_ref)

            acc_ref[...] += _bdot(a_ref[...], b_ref[...], dims)

            @pl.when(kk == nk - 1)
            def _():
                o_ref[...] = acc_ref[...].astype(out_dtype)
        if ride:
            pl.when(step == grid[0] * grid[1] * nk - 1)(lambda: ride.finish(*riders))

    out = pl.pallas_call(
        body, name=name, grid=grid,
        in_specs=[a_spec, b_spec] + [HBM] * n_in,
        out_specs=[pl.BlockSpec((tm, tn), lambda i, j, kk: (i, j))] + [HBM] * n_out,
        out_shape=[jax.ShapeDtypeStruct((m, n), out_dtype)] + (list(ride.out_shapes) if ride else []),
        scratch_shapes=([] if nk == 1 else [pltpu.VMEM((tm, tn), F32)]) + (list(ride.scratch_shapes) if ride else []),
        compiler_params=_params(("arbitrary",) * 3 if ride else ("parallel", "parallel", "arbitrary")),
    )(a, b, *(ride.operands if ride else []))
    return out if ride else out[0]


def _rms_fwd(x, w, tr=256):
    s, d = x.shape

    def body(x_ref, w_ref, o_ref):
        o_ref[...] = _rms(x_ref[...], w_ref[...]).astype(BF16)

    return pl.pallas_call(
        body, name="rms_fwd", grid=(s // tr,),
        in_specs=[pl.BlockSpec((tr, d), lambda i: (i, 0)), pl.BlockSpec((1, d), lambda i: (0, 0))],
        out_specs=pl.BlockSpec((tr, d), lambda i: (i, 0)),
        out_shape=jax.ShapeDtypeStruct((s, d), BF16), compiler_params=_params(("parallel",)),
    )(x, w)


def _rms_bwd(x, w, d_hn, dy, tr=256):
    s, d = x.shape

    def body(x_ref, w_ref, g_ref, dy_ref, gx_ref, gw_ref):
        _, vjp = jax.vjp(_rms, x_ref[...], w_ref[...])
        dx, dw = vjp(g_ref[...])
        gx_ref[...] = dy_ref[...] + dx

        @pl.when(pl.program_id(0) == 0)
        def _():
            gw_ref[...] = jnp.zeros_like(gw_ref)

        gw_ref[...] += dw

    row = pl.BlockSpec((tr, d), lambda i: (i, 0))
    vec = pl.BlockSpec((1, d), lambda i: (0, 0))
    return pl.pallas_call(
        body, name="rms_bwd", grid=(s // tr,), in_specs=[row, vec, row, row], out_specs=[row, vec],
        out_shape=[jax.ShapeDtypeStruct((s, d), F32), jax.ShapeDtypeStruct((1, d), F32)],
        compiler_params=_params(("arbitrary",)),
    )(x, w, d_hn, dy)


def _loss_dy(x, mo, target, tr=256):
    s, d = x.shape
    nt = s // tr

    def body(x_ref, mo_ref, t_ref, dy_ref, dyb_ref, l_ref):
        err = x_ref[...] + mo_ref[...] - t_ref[...]
        dy = err * (1.0 / d)
        dy_ref[...] = dy
        dyb_ref[...] = dy.astype(BF16)
        l_ref[...] = jnp.full(l_ref.shape, 0.5 * jnp.sum(jnp.sum(err * err, axis=1, keepdims=True) * (1.0 / d)), F32)

    row = pl.BlockSpec((tr, d), lambda i: (i, 0))
    return pl.pallas_call(
        body, name="loss_dy", grid=(nt,), in_specs=[row, row, row],
        out_specs=[row, row, pl.BlockSpec((1, 8, LANE), lambda i: (i, 0, 0))],
        out_shape=[jax.ShapeDtypeStruct((s, d), F32), jax.ShapeDtypeStruct((s, d), BF16),
                   jax.ShapeDtypeStruct((nt, 8, LANE), F32)],
        compiler_params=_params(("parallel",)),
    )(x, mo, target)


def _shift_rows(t, s):
    if s == 0:
        return t
    n = t.shape[0]
    rolled = pltpu.roll(t, (-s) % n, axis=0)
    idx = lax.broadcasted_iota(jnp.int32, t.shape, 0) + s
    return jnp.where((idx >= 0) & (idx < n), rolled, 0.0)


def _conv_fwd(proj, conv_w):
    s = proj.shape[0]
    nblk = 3 * A_WIDTH // LANE

    def body(x_ref, w_ref, o_ref):
        x = x_ref[...]
        acc = jnp.zeros_like(x)
        for j in range(CONV_K):
            acc = acc + w_ref[j:j + 1, :] * _shift_rows(x, j - CONV_K // 2)
        o_ref[...] = acc

    return pl.pallas_call(
        body, name="conv_fwd", grid=(nblk,),
        in_specs=[pl.BlockSpec((s, LANE), lambda i: (0, i)), pl.BlockSpec((CONV_K, LANE), lambda i: (0, i))],
        out_specs=pl.BlockSpec((None, s, LANE), lambda i: (i // A_HEADS, 0, i % A_HEADS)),
        out_shape=jax.ShapeDtypeStruct((3, s, A_WIDTH), F32), compiler_params=_params(("parallel",)),
    )(proj, conv_w)


def _conv_bwd(proj, conv_w, d_c):
    s = proj.shape[0]
    nblk = 3 * A_WIDTH // LANE

    def body(x_ref, w_ref, g_ref, dx_ref, dw_ref):
        x, g = x_ref[...], g_ref[...]
        acc = jnp.zeros_like(x)
        for j in range(CONV_K):
            off = j - CONV_K // 2
            acc = acc + w_ref[j:j + 1, :] * _shift_rows(g, -off)
            dw_ref[j:j + 1, :] = jnp.sum(_shift_rows(x, off) * g, axis=0, keepdims=True)
        dx_ref[...] = acc

    col = pl.BlockSpec((s, LANE), lambda i: (0, i))
    wsp = pl.BlockSpec((CONV_K, LANE), lambda i: (0, i))
    dsp = pl.BlockSpec((None, s, LANE), lambda i: (i // A_HEADS, 0, i % A_HEADS))
    return pl.pallas_call(
        body, name="conv_bwd", grid=(nblk,), in_specs=[col, wsp, dsp], out_specs=[col, wsp],
        out_shape=[jax.ShapeDtypeStruct((s, 3 * A_WIDTH), F32), jax.ShapeDtypeStruct((CONV_K, 3 * A_WIDTH), F32)],
        compiler_params=_params(("parallel",)),
    )(proj, conv_w, d_c)


A_FWD_HEADS = 4
A_BWD_HEADS = 4


def _neumann_inverse(a):
    c = a.shape[-1]
    eye = (lax.broadcasted_iota(jnp.int32, (c, c), 0) == lax.broadcasted_iota(jnp.int32, (c, c), 1)).astype(F32)
    tinv = eye + a
    p = a
    for _ in range(5):
        p = _mm(p, p)
        tinv = tinv + _mm(tinv, p)
    return tinv


@jax.custom_vjp
def _unit_inverse(a):
    return _neumann_inverse(a)


def _unit_inverse_fwd(a):
    tinv = _neumann_inverse(a)
    return tinv, tinv


def _unit_inverse_bwd(tinv, ct):
    return (_bdot(_bdot(tinv, ct, _TN), tinv, _NT),)


_unit_inverse.defvjp(_unit_inverse_fwd, _unit_inverse_bwd)


def _a_chain(st, cq, ck, cv, alpha, beta_raw, a_log, dt_b, incl, strict, last):
    c = CHUNK
    gb = -jnp.exp(a_log) * _softplus(alpha + dt_b)
    bb = jax.nn.sigmoid(beta_raw)
    q = _l2(_silu(cq)) * (A_DIM ** -0.5)
    k = _l2(_silu(ck))
    v = _silu(cv)

    gc = _dot(incl, jnp.broadcast_to(gb, (c, LANE)))
    tot = jnp.sum(gc * last, axis=0, keepdims=True)
    m1 = gc[:, :c]
    decay = incl * jnp.exp(incl * (m1 - m1.T))
    kb = k * bb
    vb = v * bb
    a = -(strict * decay * _mm_nt(kb, k))
    tinv = _unit_inverse(a)
    eg = jnp.exp(gc)
    u = _mm(tinv, vb)
    w = _mm(tinv, kb * eg)
    qk = _mm_nt(q, k) * decay
    v_new = u - _mm(w, st)
    o = _mm(q * eg, st) + _mm(qk, v_new)
    st_new = st * jnp.exp(tot) + _mm_tn(k * jnp.exp(tot - gc), v_new)
    return st_new, o


def _a_step(sts, cq, ck, cv, gts, pa, h0):
    c = CHUNK
    lane = lax.broadcasted_iota(jnp.int32, (1, LANE), 1)
    ii = lax.broadcasted_iota(jnp.int32, (c, c), 0)
    jj = lax.broadcasted_iota(jnp.int32, (c, c), 1)
    row = lax.broadcasted_iota(jnp.int32, (c, 1), 0)

    def pick(t, col):
        return jnp.sum(jnp.where(lane == col, t, 0.0), axis=1, keepdims=True)

    alpha, beta_raw, a_log, dt_b, incl, strict, last = [], [], [], [], [], [], []
    for b in range(sts.shape[0]):
        h, rev = h0 + b // 2, b % 2
        alpha.append(pick(gts[b], h + 8 * rev))
        beta_raw.append(pick(gts[b], h + 16 + 8 * rev))
        a_log.append(pick(pa[rev:rev + 1, :], h))
        dt_b.append(pick(pa[2 + rev:3 + rev, :], h))
        incl.append(((ii <= jj) if rev else (ii >= jj)).astype(F32))
        strict.append(((ii < jj) if rev else (ii > jj)).astype(F32))
        last.append((row == (0 if rev else c - 1)).astype(F32))
    stack = lambda ts: jnp.concatenate([t[None] for t in ts], axis=0)
    return jax.vmap(_a_chain)(sts, cq, ck, cv, stack(alpha), stack(beta_raw), stack(a_log), stack(dt_b),
                              stack(incl), stack(strict), stack(last))


def _a_final(o, za, pa):
    outs = []
    for j in range(o.shape[1] // A_DIM):
        ln = slice(j * A_DIM, (j + 1) * A_DIM)
        outs.append(_rms(o[:, ln], pa[4:5, :]) * _silu(za[:, ln]))
    return jnp.concatenate(outs, axis=1)


def _a_tiles(n, nchunk, heads):
    tiles = []
    for b in range(2 * heads):
        i = (nchunk - 1 - n) if b % 2 else n
        tiles.append((i, pl.ds(pl.multiple_of(i * CHUNK, CHUNK), CHUNK), slice((b // 2) * A_DIM, (b // 2 + 1) * A_DIM)))
    return tiles


def _a_load(tiles, c_ref, gt_ref):
    cq, ck, cv = (jnp.stack([c_ref[r, sl, ln] for _, sl, ln in tiles], axis=0) for r in range(3))
    return cq, ck, cv, jnp.stack([gt_ref[sl, :] for _, sl, _ in tiles], axis=0)


def _loop_by_two(n, step, init):
    assert n % 2 == 0
    return lax.fori_loop(0, n // 2, lambda m, carry: step(2 * m + 1, step(2 * m, carry, 0), 1), init)


def _a_scan(h0, heads, nchunk, c_ref, gt_ref, pa, of_ref, ob_ref, s_ref):
    def step(n, sts, parity):
        tiles = _a_tiles(n, nchunk, heads)
        sts_new, o = _a_step(sts, *_a_load(tiles, c_ref, gt_ref), pa, h0)
        for b, (i, sl, ln) in enumerate(tiles):
            s_ref[b, i] = sts[b]
            (ob_ref if b % 2 else of_ref)[sl, ln] = o[b]
        return sts_new

    _loop_by_two(nchunk, step, jnp.zeros((2 * heads, A_DIM, A_DIM), F32))


def _a_specs(s, heads):
    wide = heads * A_DIM
    once = pl.Buffered(1)
    trio = pl.BlockSpec((3, s, wide), lambda g: (0, 0, g), pipeline_mode=once)
    gates = pl.BlockSpec((s, LANE), lambda g: (0, P_GT // LANE))
    small = pl.BlockSpec((8, LANE), lambda g: (0, 0))

    def cols(base):
        return pl.BlockSpec((s, wide), lambda g: (0, base // wide + g), pipeline_mode=once)

    state = pl.BlockSpec((2 * heads, s // CHUNK, A_DIM, A_DIM), lambda g: (g, 0, 0, 0), pipeline_mode=once)
    return wide, trio, gates, small, cols, state


def _delta_fwd(cqkv, proj, pa):
    s = cqkv.shape[1]
    nchunk = s // CHUNK
    heads = A_FWD_HEADS
    wide, trio, gates, small, cols, state = _a_specs(s, heads)

    def body(c_ref, gt_ref, za_ref, pa_ref, out_ref, o_ref, s_ref, ob_ref):
        h0 = pl.program_id(0) * heads
        pa_v = pa_ref[...]
        _a_scan(h0, heads, nchunk, c_ref, gt_ref, pa_v, o_ref, ob_ref, s_ref)
        o_ref[...] += ob_ref[...]
        out_ref[...] = _a_final(o_ref[...], za_ref[...], pa_v).astype(BF16)

    return pl.pallas_call(
        body, name="delta_fwd", grid=(A_HEADS // heads,),
        in_specs=[trio, gates, cols(P_ZA), small], out_specs=[cols(0), cols(0), state],
        out_shape=[jax.ShapeDtypeStruct((s, D_MODEL), BF16),
                   jax.ShapeDtypeStruct((s, A_WIDTH), F32),
                   jax.ShapeDtypeStruct((2 * A_HEADS, nchunk, A_DIM, A_DIM), F32)],
        scratch_shapes=[pltpu.VMEM((s, wide), F32)], compiler_params=_params(("parallel",)),
    )(cqkv, proj, proj, pa)


def _delta_out_bwd(o_sum, proj, pa, d_mixed, tr=256):
    s = o_sum.shape[0]

    def body(o_ref, za_ref, pa_ref, dm_ref, do_ref, dza_ref, dpa_ref):
        @pl.when(pl.program_id(0) == 0)
        def _():
            dpa_ref[...] = jnp.zeros_like(dpa_ref)

        _, vjp = jax.vjp(_a_final, o_ref[...], za_ref[...], pa_ref[...])
        d_o, d_za, dpa = vjp(dm_ref[...].astype(F32))
        do_ref[...] = d_o
        dza_ref[...] = d_za
        dpa_ref[...] += dpa

    def rows(col):
        return pl.BlockSpec((tr, A_WIDTH), lambda i: (i, col))

    small = pl.BlockSpec((8, LANE), lambda i: (0, 0))
    return pl.pallas_call(
        body, name="delta_out_bwd", grid=(s // tr,), in_specs=[rows(0), rows(P_ZA // A_WIDTH), small, rows(0)],
        out_specs=[rows(0), rows(0), small],
        out_shape=[jax.ShapeDtypeStruct((s, A_WIDTH), F32), jax.ShapeDtypeStruct((s, A_WIDTH), F32),
                   jax.ShapeDtypeStruct((8, LANE), F32)],
        compiler_params=_params(("arbitrary",)),
    )(o_sum, proj, pa, d_mixed)


def _delta_bwd(cqkv, proj, pa, d_o, states):
    s = cqkv.shape[1]
    nchunk = s // CHUNK
    heads = A_BWD_HEADS
    wide, trio, gates, small, cols, state = _a_specs(s, heads)

    def body(c_ref, gt_ref, pa_ref, do_ref, s_hbm, dc_ref, dgt_ref, dpa_ref, s_buf, s_sems):
        h0 = pl.program_id(0) * heads
        pa_v = pa_ref[...]

        @pl.when(h0 == 0)
        def _():
            dgt_ref[...] = jnp.zeros_like(dgt_ref)
            dpa_ref[...] = jnp.zeros_like(dpa_ref)

        dc_ref[...] = jnp.zeros_like(dc_ref)

        def state_copies(n, slot):
            return [pltpu.make_async_copy(s_hbm.at[2 * h0 + b, i], s_buf.at[slot, b], s_sems.at[slot, b])
                    for b, (i, _, _) in enumerate(_a_tiles(nchunk - 1 - n, nchunk, heads))]

        for cp in state_copies(0, 0):
            cp.start()

        def step(n, carry, parity):
            d_sts, dpa = carry
            tiles = _a_tiles(nchunk - 1 - n, nchunk, heads)
            for cp in state_copies(n, parity):
                cp.wait()

            @pl.when(n + 1 < nchunk)
            def _():
                for cp in state_copies(n + 1, 1 - parity):
                    cp.start()

            sts = s_buf[parity]
            d_o_t = jnp.stack([do_ref[sl, ln] for _, sl, ln in tiles], axis=0)
            _, vjp_c = jax.vjp(lambda *a: _a_step(*a, h0), sts, *_a_load(tiles, c_ref, gt_ref), pa_v)
            d_prev, dcq, dck, dcv, dgts, dpa_i = vjp_c((d_sts, d_o_t))
            for b, (_, sl, ln) in enumerate(tiles):
                for r, dc in enumerate((dcq, dck, dcv)):
                    dc_ref[r, sl, ln] += dc[b]
                dgt_ref[sl, :] += dgts[b]
            return d_prev, dpa + dpa_i

        init = (jnp.zeros((2 * heads, A_DIM, A_DIM), F32), jnp.zeros((8, LANE), F32))
        _, dpa_out = lax.fori_loop(0, nchunk, lambda n, carry: step(n, carry, n % 2), init)
        dpa_ref[...] += dpa_out

    fixed = pl.BlockSpec((s, LANE), lambda g: (0, 0))
    return pl.pallas_call(
        body, name="delta_bwd", grid=(A_HEADS // heads,),
        in_specs=[trio, gates, small, cols(0), pl.BlockSpec(memory_space=pl.ANY)], out_specs=[trio, fixed, small],
        out_shape=[jax.ShapeDtypeStruct((3, s, A_WIDTH), F32), jax.ShapeDtypeStruct((s, LANE), F32),
                   jax.ShapeDtypeStruct((8, LANE), F32)],
        scratch_shapes=[pltpu.VMEM((2, 2 * heads, A_DIM, A_DIM), F32), pltpu.SemaphoreType.DMA((2, 2 * heads))],
        compiler_params=_params(("arbitrary",)),
    )(cqkv, proj, pa, d_o, states)


def _rope_tables(s):
    inv = ROPE_THETA ** (-jnp.arange(0, B_DIM, 2, dtype=F32) / B_DIM)
    ang = jnp.arange(s, dtype=F32)[:, None] * inv[None, :]
    cos, sin = jnp.cos(ang), jnp.sin(ang)
    return jnp.concatenate([cos, cos], axis=1), jnp.concatenate([-sin, sin], axis=1)


def _b_block(q_t, z_t, k3, v3, cos_q, sin_q, cos_k, sin_k, pb, n, nb):
    w = WINDOW
    def swap(t):
        return jnp.concatenate([t[:, B_DIM // 2:], t[:, :B_DIM // 2]], axis=1)

    grp = B_HEADS // B_KV
    qi = lax.broadcasted_iota(jnp.int32, (grp * w, 3 * w), 0) & (w - 1)
    kj = lax.broadcasted_iota(jnp.int32, (grp * w, 3 * w), 1)
    kpos = kj + (n - 1) * w
    mask = (jnp.abs(kj - w - qi) <= w) & (kpos >= 0) & (kpos < nb * w)
    lane = lax.broadcasted_iota(jnp.int32, (1, LANE), 1)
    qn, kn = pb[0:1, :B_DIM], pb[1:2, :B_DIM]
    cos_g = jnp.concatenate([cos_q] * grp, axis=0)
    sin_g = jnp.concatenate([sin_q] * grp, axis=0)
    def group(q, k, v, sink):
        k = _rms(k, kn)
        k = k * cos_k + swap(k) * sin_k
        q = _rms(q, qn)
        q = q * cos_g + swap(q) * sin_g
        s = _mm_nt(q, k) * (B_DIM ** -0.5)
        s = jnp.where(mask, s, -jnp.inf)
        m = jnp.maximum(jnp.max(s, axis=1, keepdims=True), sink)
        p = jnp.exp(s - m)
        p = p / (jnp.sum(p, axis=1, keepdims=True) + jnp.exp(sink - m))
        return _mm(p, v)

    stack = lambda ts: jnp.concatenate([t[None] for t in ts], axis=0)
    qs, ks, vs, sinks = [], [], [], []
    for hk in range(B_KV):
        heads = [hk * grp + g for g in range(grp)]
        ks.append(k3[:, hk * B_DIM:(hk + 1) * B_DIM])
        vs.append(v3[:, hk * B_DIM:(hk + 1) * B_DIM])
        qs.append(jnp.concatenate([q_t[:, hq * B_DIM:(hq + 1) * B_DIM] for hq in heads], axis=0))
        sinks.append(jnp.concatenate(
            [jnp.broadcast_to(jnp.sum(jnp.where(lane == hq, pb[2:3, :], 0.0), axis=1, keepdims=True), (w, 1))
             for hq in heads], axis=0))
    o = jax.vmap(group)(stack(qs), stack(ks), stack(vs), stack(sinks))
    outs = [o[hk, g * w:(g + 1) * w, :] for hk in range(B_KV) for g in range(grp)]
    return jnp.concatenate(outs, axis=1) * _silu(z_t)


def _b_specs(s):
    nb = s // WINDOW
    qsp = pl.BlockSpec((WINDOW, 512), lambda n: (n, P_QB // 512))
    zsp = pl.BlockSpec((WINDOW, 512), lambda n: (n, P_ZB // 512))

    def three(col, width):
        return [pl.BlockSpec((WINDOW, width), lambda n: (jnp.maximum(n - 1, 0), col)),
                pl.BlockSpec((WINDOW, width), lambda n: (n, col)),
                pl.BlockSpec((WINDOW, width), lambda n: (jnp.minimum(n + 1, nb - 1), col))]

    tab = pl.BlockSpec((WINDOW, B_DIM), lambda n: (n, 0))
    small = pl.BlockSpec((8, LANE), lambda n: (0, 0))
    specs = [qsp, zsp] + three(P_KB // LANE, LANE) + three(P_VB // LANE, LANE) + [tab, tab] + three(0, B_DIM) + three(0, B_DIM) + [small]
    return nb, specs


def _b_args(proj, cos2, sin2, pb):
    return (proj, proj, proj, proj, proj, proj, proj, proj, cos2, sin2, cos2, cos2, cos2, sin2, sin2, sin2, pb)


def _b_load(refs):
    (q_ref, z_ref, kp, kc, kx, vp, vc, vx, cq, sq, ckp, ckc, ckx, skp, skc, skx, pb_ref) = refs
    cat = lambda *r: jnp.concatenate([t[...] for t in r], axis=0)
    return (q_ref[...], z_ref[...], cat(kp, kc, kx), cat(vp, vc, vx), cq[...], sq[...], cat(ckp, ckc, ckx),
            cat(skp, skc, skx), pb_ref[...])


def _attn_b_fwd(proj, cos2, sin2, pb, mixed):
    s = proj.shape[0]
    nb, specs = _b_specs(s)

    def body(*refs):
        o_ref = refs[-1]
        args = _b_load(refs[:-2])
        o_ref[...] = _b_block(*args, pl.program_id(0), nb).astype(BF16)

    return pl.pallas_call(
        body, name="attn_b_fwd", grid=(nb,), in_specs=specs + [pl.BlockSpec(memory_space=pl.ANY)],
        out_specs=pl.BlockSpec((WINDOW, 512), lambda n: (n, A_WIDTH // 512)),
        out_shape=jax.ShapeDtypeStruct(mixed.shape, mixed.dtype), input_output_aliases={len(specs): 0},
        compiler_params=_params(("parallel",)),
    )(*_b_args(proj, cos2, sin2, pb), mixed)


def _attn_b_bwd(proj, cos2, sin2, pb, d_mixed):
    s = proj.shape[0]
    nb, specs = _b_specs(s)
    w = WINDOW

    def body(*refs):
        dm_ref, dq_ref, dz_ref, dk_ref, dv_ref, dpb_ref = refs[-6:]
        n = pl.program_id(0)
        q_t, z_t, k3, v3, cq, sq, ck, sk, pb_v = _b_load(refs[:-6])

        @pl.when(n == 0)
        def _():
            dk_ref[...] = jnp.zeros_like(dk_ref)
            dv_ref[...] = jnp.zeros_like(dv_ref)
            dpb_ref[...] = jnp.zeros_like(dpb_ref)

        def f(q_, z_, k_, v_, pb_):
            return _b_block(q_, z_, k_, v_, cq, sq, ck, sk, pb_, n, nb)

        _, vjp = jax.vjp(f, q_t, z_t, k3, v3, pb_v)
        dq, dz, dk3, dv3, dpb = vjp(dm_ref[...])
        dq_ref[...] = dq
        dz_ref[...] = dz
        dpb_ref[...] += dpb

        def add(j, cond):
            @pl.when(cond)
            def _():
                rows = pl.ds(pl.multiple_of((n - 1 + j) * w, w), w)
                dk_ref[rows, :] += dk3[j * w:(j + 1) * w, :]
                dv_ref[rows, :] += dv3[j * w:(j + 1) * w, :]

        add(0, n > 0)
        add(1, n >= 0)
        add(2, n < nb - 1)

    blk = pl.BlockSpec((w, 512), lambda n: (n, 0))
    whole = pl.BlockSpec((s, LANE), lambda n: (0, 0))
    small = pl.BlockSpec((8, LANE), lambda n: (0, 0))
    return pl.pallas_call(
        body, name="attn_b_bwd", grid=(nb,),
        in_specs=specs + [pl.BlockSpec((w, 512), lambda n: (n, 2))],
        out_specs=[blk, blk, whole, whole, small],
        out_shape=[jax.ShapeDtypeStruct((s, 512), F32), jax.ShapeDtypeStruct((s, 512), F32),
                   jax.ShapeDtypeStruct((s, LANE), F32), jax.ShapeDtypeStruct((s, LANE), F32),
                   jax.ShapeDtypeStruct((8, LANE), F32)],
        compiler_params=_params(("arbitrary",)),
    )(*_b_args(proj, cos2, sin2, pb), d_mixed)


def _mem_kv_fwd(mem, mem_norm_w, w_kv):
    def body(mem_ref, nw_ref, w_ref, kv_ref):
        mn = _rms(mem_ref[...], nw_ref[...]).astype(BF16)
        kv_ref[...] = jnp.dot(mn, w_ref[...], preferred_element_type=F32)

    return pl.pallas_call(
        body, name="mem_kv_fwd", out_shape=jax.ShapeDtypeStruct((MEM_LEN, 2 * C_HEADS * C_DIM), F32),
        compiler_params=_params(),
    )(mem, mem_norm_w, w_kv)


def _mem_kv_bwd(mem, mem_norm_w, w_kv, d_kv):
    def body(mem_ref, nw_ref, w_ref, g_ref, gw_ref, gn_ref):
        mn, vjp = jax.vjp(_rms, mem_ref[...], nw_ref[...])
        g = g_ref[...].astype(BF16)
        gw_ref[...] = lax.dot_general(mn.astype(BF16), g, (((0,), (0,)), ((), ())), preferred_element_type=F32)
        d_mn = lax.dot_general(g, w_ref[...], (((1,), (1,)), ((), ())), preferred_element_type=F32)
        gn_ref[...] = vjp(d_mn)[1]

    return pl.pallas_call(
        body, name="mem_kv_bwd",
        out_shape=[jax.ShapeDtypeStruct((D_MODEL, 2 * C_HEADS * C_DIM), F32), jax.ShapeDtypeStruct((1, D_MODEL), F32)],
        compiler_params=_params(),
    )(mem, mem_norm_w, w_kv, d_kv)


def _c_tile(q_t, z_t, kvm, pc):
    width = C_HEADS * C_DIM
    outs = []
    for h in range(C_HEADS):
        q = _rms(q_t[:, h * C_DIM:(h + 1) * C_DIM], pc[0:1, :])
        k = _rms(kvm[:, h * C_DIM:(h + 1) * C_DIM], pc[1:2, :])
        v = kvm[:, width + h * C_DIM:width + (h + 1) * C_DIM]
        s = _mm_nt(q, k) * (C_DIM ** -0.5)
        p = jnp.exp(s - jnp.max(s, axis=1, keepdims=True))
        p = p / jnp.sum(p, axis=1, keepdims=True)
        outs.append(_mm(p, v))
    return jnp.concatenate(outs, axis=1) * _silu(z_t)


def _attn_c_fwd(proj, kvm, pc, mixed, tq=256):
    s = proj.shape[0]

    def body(q_ref, z_ref, kv_ref, pc_ref, mixed_ref, o_ref):
        o_ref[...] = _c_tile(q_ref[...], z_ref[...], kv_ref[...], pc_ref[...]).astype(BF16)

    return pl.pallas_call(
        body, name="attn_c_fwd", grid=(s // tq,),
        in_specs=[pl.BlockSpec((tq, 512), lambda i: (i, P_QC // 512)), pl.BlockSpec((tq, 512), lambda i: (i, P_ZC // 512)),
                  pl.BlockSpec(kvm.shape, lambda i: (0, 0)), pl.BlockSpec((8, LANE), lambda i: (0, 0)),
                  pl.BlockSpec(memory_space=pl.ANY)],
        out_specs=pl.BlockSpec((tq, 512), lambda i: (i, (A_WIDTH + 512) // 512)),
        out_shape=jax.ShapeDtypeStruct(mixed.shape, mixed.dtype), input_output_aliases={4: 0},
        compiler_params=_params(("parallel",)),
    )(proj, proj, kvm, pc, mixed)


def _attn_c_bwd(proj, kvm, pc, d_mixed, tq=256):
    s = proj.shape[0]

    def body(q_ref, z_ref, kv_ref, pc_ref, dm_ref, dq_ref, dz_ref, dkv_ref, dpc_ref):
        @pl.when(pl.program_id(0) == 0)
        def _():
            dkv_ref[...] = jnp.zeros_like(dkv_ref)
            dpc_ref[...] = jnp.zeros_like(dpc_ref)

        _, vjp = jax.vjp(_c_tile, q_ref[...], z_ref[...], kv_ref[...], pc_ref[...])
        dq, dz, dkv, dpc = vjp(dm_ref[...])
        dq_ref[...] = dq
        dz_ref[...] = dz
        dkv_ref[...] += dkv
        dpc_ref[...] += dpc

    blk = pl.BlockSpec((tq, 512), lambda i: (i, 0))
    kvs = pl.BlockSpec(kvm.shape, lambda i: (0, 0))
    small = pl.BlockSpec((8, LANE), lambda i: (0, 0))
    return pl.pallas_call(
        body, name="attn_c_bwd", grid=(s // tq,),
        in_specs=[pl.BlockSpec((tq, 512), lambda i: (i, P_QC // 512)), pl.BlockSpec((tq, 512), lambda i: (i, P_ZC // 512)),
                  kvs, small, pl.BlockSpec((tq, 512), lambda i: (i, 3))],
        out_specs=[blk, blk, kvs, small],
        out_shape=[jax.ShapeDtypeStruct((s, 512), F32), jax.ShapeDtypeStruct((s, 512), F32),
                   jax.ShapeDtypeStruct(kvm.shape, F32), jax.ShapeDtypeStruct((8, LANE), F32)],
        compiler_params=_params(("arbitrary",)),
    )(proj, proj, kvm, pc, d_mixed)


def _pad_row(v, width=LANE):
    v = v.reshape(1, -1)
    return jnp.pad(v, ((0, 0), (0, width - v.shape[1])))


def _local_step(x, mem, target, norm_w, w_perm_t, conv_w, pa, pb, pc, mem_norm_w, w_kv, w_out, exchange=None):
    s = x.shape[0]
    cos2, sin2 = _rope_tables(s)
    hn = _rms_fwd(x, norm_w)
    wide = dict(tm=1024, tn=512, tk=2048)
    proj = _matmul(hn, w_perm_t, "nt", F32, "mm_proj", **wide)
    cqkv = _conv_fwd(proj, conv_w)
    mixed, o_sum, states = _delta_fwd(cqkv, proj, pa)
    mixed = _attn_b_fwd(proj, cos2, sin2, pb, mixed)
    kvm = _mem_kv_fwd(mem, mem_norm_w, w_kv)
    mixed = _attn_c_fwd(proj, kvm, pc, mixed)
    mo = _matmul(mixed, w_out, "nn", F32, "mm_out", **wide)
    dy, dyb, loss_parts = _loss_dy(x, mo, target)

    d_mixed = _matmul(dyb, w_out, "nt", F32, "mm_dmixed", **wide)
    g_w_out = _matmul(mixed, dyb, "tn", F32, "mm_gwout", **wide)
    d_qc, d_zc, d_kvm, d_pc = _attn_c_bwd(proj, kvm, pc, d_mixed)
    g_w_kv, g_mem_norm = _mem_kv_bwd(mem, mem_norm_w, w_kv, d_kvm)
    d_qb, d_zb, d_kb, d_vb, d_pb = _attn_b_bwd(proj, cos2, sin2, pb, d_mixed)
    d_o, d_za, d_pa_out = _delta_out_bwd(o_sum, proj, pa, d_mixed)
    d_c, d_gt, d_pa_scan = _delta_bwd(cqkv, proj, pa, d_o, states)
    d_pa = d_pa_out + d_pa_scan
    d_qkv, g_conv = _conv_bwd(proj, conv_w, d_c)
    d_proj = jnp.concatenate([d_qkv, d_za, d_qb, d_zb, d_qc, d_zc, d_kb, d_vb, d_gt,
                              jnp.zeros((s, P_WIDTH - P_GT - LANE), F32)], axis=1).astype(BF16)
    g_w_perm_t = _matmul(d_proj, hn, "tn", F32, "mm_gwin", tm=512, tn=1024, tk=2048)
    ride = exchange(g_w_perm_t, g_w_out, g_w_kv) if exchange else None
    d_hn = _matmul(d_proj, w_perm_t, "nn", F32, "mm_dhn", tm=1024, tn=2048, tk=512, ride=ride)
    d_hn, landed = (d_hn[0], d_hn[1:]) if ride else (d_hn, None)
    g_x, g_norm = _rms_bwd(x, norm_w, d_hn, dy)
    return dict(loss_parts=loss_parts, g_x=g_x, g_norm=g_norm, g_w_perm_t=g_w_perm_t, g_conv=g_conv, d_pa=d_pa,
                d_pb=d_pb, d_pc=d_pc, g_mem_norm=g_mem_norm, g_w_kv=g_w_kv, g_w_out=g_w_out, landed=landed)


_SEGMENTS = ((0, O_GT, 0), (O_GT, O_QB, P_GT), (O_QB, O_KB, P_QB), (O_KB, O_VB, P_KB), (O_VB, O_ZB, P_VB),
             (O_ZB, O_QC, P_ZB), (O_QC, O_ZC, P_QC), (O_ZC, IN_WIDTH, P_ZC))


def _permute_blocks(w4):
    parts = []
    for first, end, _ in sorted(_SEGMENTS, key=lambda seg: seg[2]):
        row = first
        while row < end:
            k = row // W_IN_BLOCK
            stop = min(end, (k + 1) * W_IN_BLOCK)
            parts.append(w4[k][row - k * W_IN_BLOCK:stop - k * W_IN_BLOCK, :])
            row = stop
    parts.append(jnp.zeros((P_WIDTH - IN_WIDTH, w4.shape[2]), w4.dtype))
    return jnp.concatenate(parts, axis=0)


def _unpermute_blocks(g):
    blocks = []
    for k in range(N_CHIPS):
        lo, hi = k * W_IN_BLOCK, (k + 1) * W_IN_BLOCK
        parts = [g[p + max(first, lo) - first:p + min(end, hi) - first, :]
                 for first, end, p in _SEGMENTS if max(first, lo) < min(end, hi)]
        blocks.append(jnp.concatenate(parts, axis=0))
    return jnp.stack(blocks, axis=0)


HBM = pl.BlockSpec(memory_space=pltpu.HBM)


def _place():
    x, y, c = lax.axis_index("x"), lax.axis_index("y"), lax.axis_index("c")
    chips = [(1 - x, y), (x, 1 - y), (1 - x, 1 - y)]
    return x, y, c, 2 * x + y, chips, [2 * cx + cy for cx, cy in chips]


PIECE_ROWS_CAP = 600


def _remote(src, dst, send_sems, recv_sems, k, to):
    return pltpu.make_async_remote_copy(src_ref=src, dst_ref=dst, send_sem=send_sems.at[k], recv_sem=recv_sems.at[k],
                                        device_id=to, device_id_type=MESH)


def _half_cols(ref, c):
    half = ref.shape[-1] // 2
    return pl.ds(pl.multiple_of(c * half, LANE), half)


def _all_gather_weights(w_in_b, w_out_b, w_kv_b, conv_b):
    bigs = (w_in_b, w_out_b, w_kv_b)
    n_big = len(bigs)

    def body(*refs):
        srcs, conv_src = refs[:n_big], refs[n_big]
        dsts, conv_dst = refs[n_big + 1:2 * n_big + 1], refs[2 * n_big + 1]
        send_sems, recv_sems, local_sems = refs[2 * n_big + 2:]
        x, y, c, me, chips, chip_ids = _place()
        sibling = (x, y, 1 - c)
        local = [pltpu.make_async_copy(src, dst.at[me], local_sems.at[a]) for a, (src, dst) in enumerate(zip(srcs, dsts))]
        local.append(pltpu.make_async_copy(conv_src, conv_dst.at[me], local_sems.at[n_big]))
        for cp in local:
            cp.start()
        sends = []
        for a, (src, dst) in enumerate(zip(srcs, dsts)):
            mine = _half_cols(src, c)
            for j, chip in enumerate(chips):
                sends.append(_remote(src.at[:, mine], dst.at[me, :, mine], send_sems, recv_sems, 6 * a + j, (*chip, c)))
        for j, chip in enumerate(chips):
            sends.append(_remote(conv_src, conv_dst.at[me], send_sems, recv_sems, 6 * n_big + j, (*chip, c)))
        for cp in sends:
            cp.start()
        passed = []
        for a, (src, dst) in enumerate(zip(srcs, dsts)):
            mine = _half_cols(src, c)
            for j, cid in enumerate(chip_ids):
                landed = dst.at[cid, :, mine]
                _remote(landed, landed, send_sems, recv_sems, 6 * a + j, sibling).wait_recv()
                cp = _remote(landed, landed, send_sems, recv_sems, 6 * a + 3 + j, sibling)
                cp.start()
                passed.append(cp)
        for a, (src, dst) in enumerate(zip(srcs, dsts)):
            other = _half_cols(src, 1 - c)
            for j, cid in enumerate(chip_ids):
                landed = dst.at[cid, :, other]
                _remote(landed, landed, send_sems, recv_sems, 6 * a + 3 + j, sibling).wait_recv()
        for j, cid in enumerate(chip_ids):
            _remote(conv_src, conv_dst.at[cid], send_sems, recv_sems, 6 * n_big + j, sibling).wait_recv()
        for cp in sends + passed:
            cp.wait_send()
        for cp in local:
            cp.wait()

    n_sem = 6 * n_big + 3
    return pl.pallas_call(
        body, name="all_gather_weights",
        out_shape=[jax.ShapeDtypeStruct((N_CHIPS,) + w.shape, w.dtype) for w in bigs + (conv_b,)],
        in_specs=[pl.BlockSpec(memory_space=pltpu.VMEM)] * (n_big + 1), out_specs=[HBM] * (n_big + 1),
        scratch_shapes=[pltpu.SemaphoreType.DMA((n_sem,)), pltpu.SemaphoreType.DMA((n_sem,)),
                        pltpu.SemaphoreType.DMA((n_big + 1,))],
        compiler_params=_params(),
    )(*bigs, conv_b)


def _pair_exchange(grads):
    n = len(grads)
    pieces = [_row_tile(g.shape[1]) for g in grads]

    def body(*refs):
        srcs, gots = refs[:n], refs[n:2 * n]
        stages = refs[2 * n:3 * n]
        send_sems, recv_sems, load_sems = refs[3 * n:]
        x, y, c, _, _, _ = _place()
        sibling = (x, y, 1 - c)
        for a in range(n):
            slabs, rows, _ = gots[a].shape
            piece = pieces[a]
            per_slab = rows // piece
            theirs = _half_cols(srcs[a], 1 - c)
            loads, sends = [], []
            for i in range(slabs * per_slab):
                k, r, slot = i // per_slab, i % per_slab, i % 2
                part = pl.ds(r * piece, piece)
                loads.append(pltpu.make_async_copy(srcs[a].at[k, part, theirs], stages[a].at[slot], load_sems.at[2 * a + slot]))
                sends.append(pltpu.make_async_remote_copy(
                    src_ref=stages[a].at[slot], dst_ref=gots[a].at[k, part, :],
                    send_sem=send_sems.at[2 * a + slot], recv_sem=recv_sems.at[a], device_id=sibling, device_id_type=MESH))
            loads[0].start()
            for i in range(len(loads)):
                loads[i].wait()
                sends[i].start()
                if i + 1 < len(loads):
                    if i >= 1:
                        sends[i - 1].wait_send()
                    loads[i + 1].start()
            for cp in sends[-2:]:
                cp.wait_send()
        for a in range(n):
            whole = srcs[a].at[:, :, _half_cols(srcs[a], c)]
            pltpu.make_async_remote_copy(src_ref=whole, dst_ref=gots[a], send_sem=send_sems.at[2 * a],
                                         recv_sem=recv_sems.at[a], device_id=sibling, device_id_type=MESH).wait_recv()

    halves = [jax.ShapeDtypeStruct((g.shape[0], g.shape[1], g.shape[2] // 2), g.dtype) for g in grads]
    return pl.pallas_call(
        body, name="grad_pair_exchange", out_shape=halves, in_specs=[HBM] * n, out_specs=[HBM] * n,
        scratch_shapes=[pltpu.VMEM((2, piece, g.shape[2] // 2), g.dtype) for piece, g in zip(pieces, grads)]
        + [pltpu.SemaphoreType.DMA((2 * n,)), pltpu.SemaphoreType.DMA((n,)), pltpu.SemaphoreType.DMA((2 * n,))],
        compiler_params=_params(),
    )(*grads)


class _ChipExchange:
    def __init__(self, halves):
        n = len(halves)
        self.operands = list(halves)
        self.out_shapes = [jax.ShapeDtypeStruct((N_CHIPS - 1,) + h.shape[1:], h.dtype) for h in halves]
        self.scratch_shapes = [pltpu.SemaphoreType.DMA((3 * n,)), pltpu.SemaphoreType.DMA((3 * n,))]

    @staticmethod
    def _copies(srcs, lands, sems):
        _, _, c, _, chips, chip_ids = _place()
        return [_remote(src.at[cid], land.at[j], sems[0], sems[1], 3 * a + j, (*chip, c))
                for a, (src, land) in enumerate(zip(srcs, lands)) for j, (chip, cid) in enumerate(zip(chips, chip_ids))]

    def start(self, srcs, lands, sems):
        for cp in self._copies(srcs, lands, sems):
            cp.start()

    def finish(self, srcs, lands, sems):
        copies = self._copies(srcs, lands, sems)
        for cp in copies:
            cp.wait_recv()
        for cp in copies:
            cp.wait_send()


def _pair_gather(halves):
    n = len(halves)

    def body(*refs):
        srcs, fulls = refs[:n], refs[n:2 * n]
        send_sems, recv_sems, local_sems = refs[2 * n:]
        x, y, c, _, _, _ = _place()
        copies = []
        for a in range(n):
            mine = _half_cols(fulls[a], c)
            keep = pltpu.make_async_copy(srcs[a], fulls[a].at[:, mine], local_sems.at[a])
            keep.start()
            give = _remote(srcs[a], fulls[a].at[:, mine], send_sems, recv_sems, a, (x, y, 1 - c))
            give.start()
            copies += [keep, give]
        for a in range(n):
            other = _half_cols(fulls[a], 1 - c)
            copies[2 * a].wait()
            copies[2 * a + 1].wait_send()
            _remote(srcs[a], fulls[a].at[:, other], send_sems, recv_sems, a, (x, y, 1 - c)).wait_recv()

    return pl.pallas_call(
        body, name="grad_pair_gather",
        out_shape=[jax.ShapeDtypeStruct((h.shape[0], 2 * h.shape[1]), h.dtype) for h in halves],
        in_specs=[pl.BlockSpec(memory_space=pltpu.VMEM)] * n, out_specs=[HBM] * n,
        scratch_shapes=[pltpu.SemaphoreType.DMA((n,)), pltpu.SemaphoreType.DMA((n,)), pltpu.SemaphoreType.DMA((n,))],
    )(*halves)


def _all_reduce_small(p):
    n_dev = 8

    def body(p_ref, o_ref, land, send_sems, recv_sems):
        x, y, c = lax.axis_index("x"), lax.axis_index("y"), lax.axis_index("c")
        me = 4 * x + 2 * y + c
        land[me] = p_ref[...]
        sends = []
        for k in range(1, n_dev):
            fx, fy, fc = (k >> 2) & 1, (k >> 1) & 1, k & 1
            to = (x ^ fx, y ^ fy, c ^ fc)
            cp = _remote(p_ref, land.at[me], send_sems, recv_sems, k - 1, to)
            cp.start()
            sends.append(cp)
        for k in range(1, n_dev):
            _remote(p_ref, land.at[me ^ k], send_sems, recv_sems, k - 1, (x, y, c)).wait_recv()
        total = land[0]
        for d in range(1, n_dev):
            total = total + land[d]
        o_ref[...] = total
        for cp in sends:
            cp.wait_send()

    vm = pl.BlockSpec(memory_space=pltpu.VMEM)
    return pl.pallas_call(
        body, name="all_reduce_small", out_shape=jax.ShapeDtypeStruct(p.shape, p.dtype), in_specs=[vm], out_specs=vm,
        scratch_shapes=[pltpu.VMEM((n_dev,) + p.shape, p.dtype), pltpu.SemaphoreType.DMA((n_dev - 1,)),
                        pltpu.SemaphoreType.DMA((n_dev - 1,))],
    )(p)


def _row_tile(rows):
    fits = [t for t in range(8, min(rows, PIECE_ROWS_CAP) + 1, 8) if rows % t == 0]
    return max(fits) if fits else rows


def _pair_sum(full, got, core, name):
    n, r, c = got.shape
    tr = _row_tile(r)

    def body(core_ref, a_ref, b_ref, o_ref):
        o_ref[...] = (a_ref[...] + b_ref[...]).astype(BF16)

    blk = pl.BlockSpec((None, tr, c), lambda i, j, core_ref: (i, j, 0))
    grid_spec = pltpu.PrefetchScalarGridSpec(
        num_scalar_prefetch=1, grid=(n, r // tr),
        in_specs=[pl.BlockSpec((None, tr, c), lambda i, j, core_ref: (i, j, core_ref[0])), blk], out_specs=blk)
    return pl.pallas_call(body, name=name, grid_spec=grid_spec, out_shape=jax.ShapeDtypeStruct(got.shape, BF16),
                          compiler_params=_params(("parallel", "parallel")))(core, full, got)


def _chip_sum(full, got, land, place, name):
    n, r, c = land.shape
    tr = _row_tile(r)

    def body(place_ref, a_ref, b_ref, l_ref, o_ref):
        total = a_ref[...] + b_ref[...]
        for j in range(n):
            total = total + l_ref[j].astype(F32)
        o_ref[...] = total

    grid_spec = pltpu.PrefetchScalarGridSpec(
        num_scalar_prefetch=1, grid=(r // tr,),
        in_specs=[pl.BlockSpec((None, tr, c), lambda i, p: (p[0], i, p[1])),
                  pl.BlockSpec((None, tr, c), lambda i, p: (p[0], i, 0)),
                  pl.BlockSpec((n, tr, c), lambda i, p: (0, i, 0))],
        out_specs=pl.BlockSpec((tr, c), lambda i, p: (i, 0)))
    return pl.pallas_call(body, name=name, grid_spec=grid_spec, out_shape=jax.ShapeDtypeStruct((r, c), F32),
                          compiler_params=_params(("parallel",)))(place, full, got, land)


def _adamw(w, g, m, v, name):
    r, c = w.shape
    tr = _row_tile(r)
    tc = 1024 if c % 1024 == 0 else c

    def body(w_ref, g_ref, m_ref, v_ref, d_ref, mo_ref, vo_ref):
        g_ = g_ref[...]
        m2 = ADAM_B1 * m_ref[...] + (1.0 - ADAM_B1) * g_
        v2 = ADAM_B2 * v_ref[...] + (1.0 - ADAM_B2) * jnp.square(g_)
        m_hat = m2 / (1.0 - ADAM_B1 ** ADAM_STEP)
        v_hat = v2 / (1.0 - ADAM_B2 ** ADAM_STEP)
        d_ref[...] = -ADAM_LR * (m_hat / (jnp.sqrt(v_hat) + ADAM_EPS) + ADAM_WD * w_ref[...])
        mo_ref[...] = m2
        vo_ref[...] = v2

    blk = pl.BlockSpec((tr, tc), lambda i, j: (i, j))
    return pl.pallas_call(body, name=name, grid=(r // tr, c // tc), in_specs=[blk] * 4, out_specs=[blk] * 3,
                          out_shape=[jax.ShapeDtypeStruct(w.shape, F32)] * 3,
                          compiler_params=_params(("parallel", "parallel")))(w, g, m, v)


SMALL_NAMES = ("norm_w", "mem_norm_w", "o_norm_a", "q_norm_c", "k_norm_c", "q_norm_b", "k_norm_b",
               "a_log_fwd", "a_log_bwd", "dt_bias_fwd", "dt_bias_bwd", "sink_b")
SMALL_SIZES = (2048, 2048, 128, 128, 128, 64, 64, 8, 8, 8, 8, 8)
SMALL_LOSS = sum(SMALL_SIZES)
SMALL_CONV = 5120
SMALL_TOTAL = SMALL_CONV + CONV_K * 3 * A_WIDTH
SMALL_ROWS = SMALL_TOTAL // LANE


def _pack_small(parts, extra=None, conv=None):
    vec = [parts[n].reshape(-1) for n in SMALL_NAMES]
    vec.append(jnp.zeros((1,), F32) if extra is None else extra.reshape(1))
    vec.append(jnp.zeros((SMALL_CONV - SMALL_LOSS - 1,), F32))
    vec.append(jnp.zeros((SMALL_TOTAL - SMALL_CONV,), F32) if conv is None else conv.reshape(-1))
    return jnp.concatenate(vec).reshape(SMALL_ROWS, LANE)


def _unpack_small(packed):
    flat = packed.reshape(-1)
    out, off = {}, 0
    for n, size in zip(SMALL_NAMES, SMALL_SIZES):
        out[n] = flat[off:off + size].reshape(1, size)
        off += size
    return out


WEIGHT_ORDER = ("norm_w", "w_in", "conv_w_a", "a_log_fwd", "a_log_bwd", "dt_bias_fwd", "dt_bias_bwd", "o_norm_a",
                "q_norm_b", "k_norm_b", "sink_b", "mem_norm_w", "w_mem_kv", "q_norm_c", "k_norm_c", "w_out")


def kernel(x, mem, norm_w, w_in, conv_w_a, a_log_fwd, a_log_bwd, dt_bias_fwd, dt_bias_bwd, o_norm_a, q_norm_b, k_norm_b, sink_b, mem_norm_w, w_mem_kv, q_norm_c, k_norm_c, w_out, loss_target, m_norm_w, m_w_in, m_conv_w_a, m_a_log_fwd, m_a_log_bwd, m_dt_bias_fwd, m_dt_bias_bwd, m_o_norm_a, m_q_norm_b, m_k_norm_b, m_sink_b, m_mem_norm_w, m_w_mem_kv, m_q_norm_c, m_k_norm_c, m_w_out, v_norm_w, v_w_in, v_conv_w_a, v_a_log_fwd, v_a_log_bwd, v_dt_bias_fwd, v_dt_bias_bwd, v_o_norm_a, v_q_norm_b, v_k_norm_b, v_sink_b, v_mem_norm_w, v_w_mem_kv, v_q_norm_c, v_k_norm_c, v_w_out):
    weights = dict(norm_w=norm_w, w_in=w_in, conv_w_a=conv_w_a, a_log_fwd=a_log_fwd, a_log_bwd=a_log_bwd,
                   dt_bias_fwd=dt_bias_fwd, dt_bias_bwd=dt_bias_bwd, o_norm_a=o_norm_a, q_norm_b=q_norm_b,
                   k_norm_b=k_norm_b, sink_b=sink_b, mem_norm_w=mem_norm_w, w_mem_kv=w_mem_kv, q_norm_c=q_norm_c,
                   k_norm_c=k_norm_c, w_out=w_out)
    mom1 = dict(norm_w=m_norm_w, w_in=m_w_in, conv_w_a=m_conv_w_a, a_log_fwd=m_a_log_fwd, a_log_bwd=m_a_log_bwd,
                dt_bias_fwd=m_dt_bias_fwd, dt_bias_bwd=m_dt_bias_bwd, o_norm_a=m_o_norm_a, q_norm_b=m_q_norm_b,
                k_norm_b=m_k_norm_b, sink_b=m_sink_b, mem_norm_w=m_mem_norm_w, w_mem_kv=m_w_mem_kv,
                q_norm_c=m_q_norm_c, k_norm_c=m_k_norm_c, w_out=m_w_out)
    mom2 = dict(norm_w=v_norm_w, w_in=v_w_in, conv_w_a=v_conv_w_a, a_log_fwd=v_a_log_fwd, a_log_bwd=v_a_log_bwd,
                dt_bias_fwd=v_dt_bias_fwd, dt_bias_bwd=v_dt_bias_bwd, o_norm_a=v_o_norm_a, q_norm_b=v_q_norm_b,
                k_norm_b=v_k_norm_b, sink_b=v_sink_b, mem_norm_w=v_mem_norm_w, w_mem_kv=v_w_mem_kv,
                q_norm_c=v_q_norm_c, k_norm_c=v_k_norm_c, w_out=v_w_out)
    chip = 2 * lax.axis_index("x") + lax.axis_index("y")

    w_in4, w_out4, w_kv4, conv4 = _all_gather_weights(jnp.transpose(w_in[0]).astype(BF16), w_out[0].astype(BF16),
                                                      w_mem_kv[0].astype(BF16), conv_w_a[0])
    w_perm_t = _permute_blocks(w_in4)
    w_out_full = w_out4.reshape(D_MODEL, D_MODEL)
    w_kv_full = w_kv4.reshape(D_MODEL, 2 * C_HEADS * C_DIM)
    conv_full = jnp.transpose(conv4, (1, 0, 2)).reshape(CONV_K, 3 * A_WIDTH)
    pa = jnp.concatenate([_pad_row(a_log_fwd), _pad_row(a_log_bwd), _pad_row(dt_bias_fwd), _pad_row(dt_bias_bwd),
                          _pad_row(o_norm_a), jnp.zeros((3, LANE), F32)], axis=0)
    pb = jnp.concatenate([_pad_row(q_norm_b), _pad_row(k_norm_b), _pad_row(sink_b), jnp.zeros((5, LANE), F32)], axis=0)
    pc = jnp.concatenate([_pad_row(q_norm_c), _pad_row(k_norm_c), jnp.zeros((6, LANE), F32)], axis=0)

    full, got = [], []

    def exchange(g_w_perm_t, g_w_out, g_w_kv):
        full.extend([_unpermute_blocks(g_w_perm_t), g_w_out.reshape(N_CHIPS, D_MODEL // N_CHIPS, D_MODEL),
                     g_w_kv.reshape(N_CHIPS, D_MODEL // N_CHIPS, 2 * C_HEADS * C_DIM)])
        core = lax.axis_index("c").astype(jnp.int32).reshape(1)
        got.extend(_pair_exchange(full))
        return _ChipExchange([_pair_sum(a, b, core, "grad_pair_sum_%d" % i) for i, (a, b) in enumerate(zip(full, got))])

    r = _local_step(x[0], mem[0], loss_target[0], norm_w, w_perm_t, conv_full, pa, pb, pc, mem_norm_w, w_kv_full,
                    w_out_full, exchange)
    place = jnp.stack([chip, lax.axis_index("c")]).astype(jnp.int32)
    reduced = [_chip_sum(a, b, l, place, "grad_chip_sum_%d" % i)
               for i, (a, b, l) in enumerate(zip(full, got, r["landed"]))]
    g_w_in_t, g_w_out, g_w_kv = _pair_gather(reduced)

    d_pa, d_pb, d_pc = r["d_pa"], r["d_pb"], r["d_pc"]
    small_g = dict(norm_w=r["g_norm"], mem_norm_w=r["g_mem_norm"], o_norm_a=d_pa[4], q_norm_c=d_pc[0], k_norm_c=d_pc[1],
                   q_norm_b=d_pb[0, :B_DIM], k_norm_b=d_pb[1, :B_DIM], a_log_fwd=d_pa[0, :A_HEADS],
                   a_log_bwd=d_pa[1, :A_HEADS], dt_bias_fwd=d_pa[2, :A_HEADS], dt_bias_bwd=d_pa[3, :A_HEADS],
                   sink_b=d_pb[2, :B_HEADS])
    packed = _all_reduce_small(_pack_small(small_g, jnp.sum(r["loss_parts"][:, 0, 0]), r["g_conv"]))
    flat = packed.reshape(-1)
    loss = flat[SMALL_LOSS]
    conv_sum = flat[SMALL_CONV:].reshape(CONV_K, 3 * A_WIDTH)
    conv_cols = 3 * A_WIDTH // N_CHIPS
    g_conv = lax.dynamic_slice(conv_sum, (0, chip * conv_cols), (CONV_K, conv_cols))

    grads = _unpack_small(packed)
    grads.update(w_in=jnp.transpose(g_w_in_t), w_mem_kv=g_w_kv, w_out=g_w_out, conv_w_a=g_conv)
    delta, new_m, new_v = {}, {}, {}
    for n in ("w_mem_kv", "w_out", "conv_w_a"):
        delta[n], new_m[n], new_v[n] = _adamw(weights[n][0], grads[n], mom1[n][0], mom2[n][0], "adamw_" + n)
    stepped = _adamw(jnp.transpose(w_in[0]), g_w_in_t, jnp.transpose(m_w_in[0]), jnp.transpose(v_w_in[0]), "adamw_w_in")
    delta["w_in"], new_m["w_in"], new_v["w_in"] = (jnp.transpose(t) for t in stepped)
    d_s, m_s, v_s = _adamw(_pack_small(weights), packed, _pack_small(mom1), _pack_small(mom2), "adamw_small")
    d_s, m_s, v_s = _unpack_small(d_s), _unpack_small(m_s), _unpack_small(v_s)
    for n in SMALL_NAMES:
        delta[n], new_m[n], new_v[n] = d_s[n], m_s[n], v_s[n]

    def shaped(tree):
        return [tree[n].reshape(weights[n].shape) for n in WEIGHT_ORDER]

    return (loss, r["g_x"].reshape(x.shape), *shaped(grads), *shaped(delta), *shaped(new_m), *shaped(new_v))
```

```python
import functools

import jax
import jax.numpy as jnp
from jax import lax
from jax.experimental import pallas as pl
from jax.experimental.pallas import tpu as pltpu

F32 = jnp.float32
BF16 = jnp.bfloat16
HI = lax.Precision.HIGHEST
MESH = pl.DeviceIdType.MESH

D_MODEL = 2048
A_WIDTH = 1024
A_HEADS = 8
A_DIM = 128
CONV_K = 5
CHUNK = 64
B_HEADS = 8
B_KV = 2
B_DIM = 64
WINDOW = 128
C_HEADS = 4
C_DIM = 128
MEM_LEN = 256
ROPE_THETA = 10000.0
EPS = 1e-6
IN_WIDTH = 6432
N_CHIPS = 4
W_IN_BLOCK = IN_WIDTH // N_CHIPS

LANE = 128
P_QA, P_KA, P_VA, P_ZA = 0, 1024, 2048, 3072
P_QB, P_ZB, P_QC, P_ZC = 4096, 4608, 5120, 5632
P_KB, P_VB, P_GT = 6144, 6272, 6400
P_WIDTH = 6656
O_GT, O_QB, O_KB, O_VB, O_ZB, O_QC, O_ZC = 4096, 4128, 4640, 4768, 4896, 5408, 5920

ADAM_LR, ADAM_B1, ADAM_B2, ADAM_EPS, ADAM_WD, ADAM_STEP = 0.001, 0.9, 0.999, 1e-08, 0.01, 10

VMEM_LIMIT = 56 * 1024 * 1024


def _params(sem=None):
    return pltpu.CompilerParams(dimension_semantics=sem, vmem_limit_bytes=VMEM_LIMIT)


def _dot(a, b, dims=(((1,), (0,)), ((), ())), precision=HI):
    return lax.dot_general(a, b, dims, precision=precision, preferred_element_type=F32)


def _dot_nt(a, b, precision=HI):
    return _dot(a, b, (((1,), (1,)), ((), ())), precision)


def _dot_tn(a, b, precision=HI):
    return _dot(a, b, (((0,), (0,)), ((), ())), precision)


_NN = (((1,), (0,)), ((), ()))
_NT = (((1,), (1,)), ((), ()))
_TN = (((0,), (0,)), ((), ()))


def _bdot(a, b, dims):
    return lax.dot_general(a.astype(BF16), b.astype(BF16), dims, preferred_element_type=F32)


@jax.custom_vjp
def _mm(a, b):
    return _bdot(a, b, _NN)


_mm.defvjp(lambda a, b: (_bdot(a, b, _NN), (a, b)),
           lambda res, ct: (_bdot(ct, res[1], _NT), _bdot(res[0], ct, _TN)))


@jax.custom_vjp
def _mm_nt(a, b):
    return _bdot(a, b, _NT)


_mm_nt.defvjp(lambda a, b: (_bdot(a, b, _NT), (a, b)),
              lambda res, ct: (_bdot(ct, res[1], _NN), _bdot(ct, res[0], _TN)))


@jax.custom_vjp
def _mm_tn(a, b):
    return _bdot(a, b, _TN)


_mm_tn.defvjp(lambda a, b: (_bdot(a, b, _TN), (a, b)),
              lambda res, ct: (_bdot(res[1], ct, _NT), _bdot(res[0], ct, _NN)))


def _rms(t, w):
    return t * lax.rsqrt(jnp.mean(t * t, axis=-1, keepdims=True) + EPS) * w


def _l2(t):
    return t * lax.rsqrt(jnp.sum(t * t, axis=-1, keepdims=True) + EPS)


def _silu(t):
    return t * jax.nn.sigmoid(t)


def _softplus(t):
    return jnp.maximum(t, 0.0) + jnp.log1p(jnp.exp(-jnp.abs(t)))


def _matmul(a, b, mode, out_dtype, name, tm=512, tn=512, tk=512, ride=None):
    (m, k) = a.shape[::-1] if mode == "tn" else a.shape
    n = b.shape[0] if mode == "nt" else b.shape[1]
    tm, tn, tk = min(tm, m), min(tn, n), min(tk, k)
    assert m % tm == 0 and n % tn == 0 and k % tk == 0, (m, n, k, tm, tn, tk)
    if mode == "nn":
        a_spec = pl.BlockSpec((tm, tk), lambda i, j, kk: (i, kk))
        b_spec = pl.BlockSpec((tk, tn), lambda i, j, kk: (kk, j))
        dims = (((1,), (0,)), ((), ()))
    elif mode == "nt":
        a_spec = pl.BlockSpec((tm, tk), lambda i, j, kk: (i, kk))
        b_spec = pl.BlockSpec((tn, tk), lambda i, j, kk: (j, kk))
        dims = (((1,), (1,)), ((), ()))
    else:
        a_spec = pl.BlockSpec((tk, tm), lambda i, j, kk: (kk, i))
        b_spec = pl.BlockSpec((tk, tn), lambda i, j, kk: (kk, j))
        dims = (((0,), (0,)), ((), ()))
    nk = k // tk
    grid = (m // tm, n // tn, nk)
    n_in = len(ride.operands) if ride else 0
    n_out = len(ride.out_shapes) if ride else 0

    def body(*refs):
        a_ref, b_ref, o_ref = refs[0], refs[1], refs[2 + n_in]
        scratch = refs[3 + n_in + n_out:]
        step = (pl.program_id(0) * grid[1] + pl.program_id(1)) * nk + pl.program_id(2)
        riders = (refs[2:2 + n_in], refs[3 + n_in:3 + n_in + n_out], scratch[(0 if nk == 1 else 1):])
        if ride:
            pl.when(step == 0)(lambda: ride.start(*riders))
        if nk == 1:
            o_ref[...] = _bdot(a_ref[...], b_ref[...], dims).astype(out_dtype)
        else:
            acc_ref, kk = scratch[0], pl.program_id(2)

            @pl.when(kk == 0)
            def _():
                acc_ref[...] = jnp.zeros_like(acc_ref)

            acc_ref[...] += _bdot(a_ref[...], b_ref[...], dims)

            @pl.when(kk == nk - 1)
            def _():
                o_ref[...] = acc_ref[...].astype(out_dtype)
        if ride:
            pl.when(step == grid[0] * grid[1] * nk - 1)(lambda: ride.finish(*riders))

    out = pl.pallas_call(
        body, name=name, grid=grid,
        in_specs=[a_spec, b_spec] + [HBM] * n_in,
        out_specs=[pl.BlockSpec((tm, tn), lambda i, j, kk: (i, j))] + [HBM] * n_out,
        out_shape=[jax.ShapeDtypeStruct((m, n), out_dtype)] + (list(ride.out_shapes) if ride else []),
        scratch_shapes=([] if nk == 1 else [pltpu.VMEM((tm, tn), F32)]) + (list(ride.scratch_shapes) if ride else []),
        compiler_params=_params(("arbitrary",) * 3 if ride else ("parallel", "parallel", "arbitrary")),
    )(a, b, *(ride.operands if ride else []))
    return out if ride else out[0]


def _rms_fwd(x, w, tr=256):
    s, d = x.shape

    def body(x_ref, w_ref, o_ref):
        o_ref[...] = _rms(x_ref[...], w_ref[...]).astype(BF16)

    return pl.pallas_call(
        body, name="rms_fwd", grid=(s // tr,),
        in_specs=[pl.BlockSpec((tr, d), lambda i: (i, 0)), pl.BlockSpec((1, d), lambda i: (0, 0))],
        out_specs=pl.BlockSpec((tr, d), lambda i: (i, 0)),
        out_shape=jax.ShapeDtypeStruct((s, d), BF16), compiler_params=_params(("parallel",)),
    )(x, w)


def _rms_bwd(x, w, d_hn, dy, tr=256):
    s, d = x.shape

    def body(x_ref, w_ref, g_ref, dy_ref, gx_ref, gw_ref):
        _, vjp = jax.vjp(_rms, x_ref[...], w_ref[...])
        dx, dw = vjp(g_ref[...])
        gx_ref[...] = dy_ref[...] + dx

        @pl.when(pl.program_id(0) == 0)
        def _():
            gw_ref[...] = jnp.zeros_like(gw_ref)

        gw_ref[...] += dw

    row = pl.BlockSpec((tr, d), lambda i: (i, 0))
    vec = pl.BlockSpec((1, d), lambda i: (0, 0))
    return pl.pallas_call(
        body, name="rms_bwd", grid=(s // tr,), in_specs=[row, vec, row, row], out_specs=[row, vec],
        out_shape=[jax.ShapeDtypeStruct((s, d), F32), jax.ShapeDtypeStruct((1, d), F32)],
        compiler_params=_params(("arbitrary",)),
    )(x, w, d_hn, dy)


def _loss_dy(x, mo, target, tr=256):
    s, d = x.shape
    nt = s // tr

    def body(x_ref, mo_ref, t_ref, dy_ref, dyb_ref, l_ref):
        err = x_ref[...] + mo_ref[...] - t_ref[...]
        dy = err * (1.0 / d)
        dy_ref[...] = dy
        dyb_ref[...] = dy.astype(BF16)
        l_ref[...] = jnp.full(l_ref.shape, 0.5 * jnp.sum(jnp.sum(err * err, axis=1, keepdims=True) * (1.0 / d)), F32)

    row = pl.BlockSpec((tr, d), lambda i: (i, 0))
    return pl.pallas_call(
        body, name="loss_dy", grid=(nt,), in_specs=[row, row, row],
        out_specs=[row, row, pl.BlockSpec((1, 8, LANE), lambda i: (i, 0, 0))],
        out_shape=[jax.ShapeDtypeStruct((s, d), F32), jax.ShapeDtypeStruct((s, d), BF16),
                   jax.ShapeDtypeStruct((nt, 8, LANE), F32)],
        compiler_params=_params(("parallel",)),
    )(x, mo, target)


def _shift_rows(t, s):
    if s == 0:
        return t
    n = t.shape[0]
    rolled = pltpu.roll(t, (-s) % n, axis=0)
    idx = lax.broadcasted_iota(jnp.int32, t.shape, 0) + s
    return jnp.where((idx >= 0) & (idx < n), rolled, 0.0)


def _conv_fwd(proj, conv_w):
    s = proj.shape[0]
    nblk = 3 * A_WIDTH // LANE

    def body(x_ref, w_ref, o_ref):
        x = x_ref[...]
        acc = jnp.zeros_like(x)
        for j in range(CONV_K):
            acc = acc + w_ref[j:j + 1, :] * _shift_rows(x, j - CONV_K // 2)
        o_ref[...] = acc

    return pl.pallas_call(
        body, name="conv_fwd", grid=(nblk,),
        in_specs=[pl.BlockSpec((s, LANE), lambda i: (0, i)), pl.BlockSpec((CONV_K, LANE), lambda i: (0, i))],
        out_specs=pl.BlockSpec((None, s, LANE), lambda i: (i // A_HEADS, 0, i % A_HEADS)),
        out_shape=jax.ShapeDtypeStruct((3, s, A_WIDTH), F32), compiler_params=_params(("parallel",)),
    )(proj, conv_w)


def _conv_bwd(proj, conv_w, d_c):
    s = proj.shape[0]
    nblk = 3 * A_WIDTH // LANE

    def body(x_ref, w_ref, g_ref, dx_ref, dw_ref):
        x, g = x_ref[...], g_ref[...]
        acc = jnp.zeros_like(x)
        for j in range(CONV_K):
            off = j - CONV_K // 2
            acc = acc + w_ref[j:j + 1, :] * _shift_rows(g, -off)
            dw_ref[j:j + 1, :] = jnp.sum(_shift_rows(x, off) * g, axis=0, keepdims=True)
        dx_ref[...] = acc

    col = pl.BlockSpec((s, LANE), lambda i: (0, i))
    wsp = pl.BlockSpec((CONV_K, LANE), lambda i: (0, i))
    dsp = pl.BlockSpec((None, s, LANE), lambda i: (i // A_HEADS, 0, i % A_HEADS))
    return pl.pallas_call(
        body, name="conv_bwd", grid=(nblk,), in_specs=[col, wsp, dsp], out_specs=[col, wsp],
        out_shape=[jax.ShapeDtypeStruct((s, 3 * A_WIDTH), F32), jax.ShapeDtypeStruct((CONV_K, 3 * A_WIDTH), F32)],
        compiler_params=_params(("parallel",)),
    )(proj, conv_w, d_c)


A_FWD_HEADS = 4
A_BWD_HEADS = 4


def _neumann_inverse(a):
    c = a.shape[-1]
    eye = (lax.broadcasted_iota(jnp.int32, (c, c), 0) == lax.broadcasted_iota(jnp.int32, (c, c), 1)).astype(F32)
    tinv = eye + a
    p = a
    for _ in range(5):
        p = _mm(p, p)
        tinv = tinv + _mm(tinv, p)
    return tinv


@jax.custom_vjp
def _unit_inverse(a):
    return _neumann_inverse(a)


def _unit_inverse_fwd(a):
    tinv = _neumann_inverse(a)
    return tinv, tinv


def _unit_inverse_bwd(tinv, ct):
    return (_bdot(_bdot(tinv, ct, _TN), tinv, _NT),)


_unit_inverse.defvjp(_unit_inverse_fwd, _unit_inverse_bwd)


def _a_chain(st, cq, ck, cv, alpha, beta_raw, a_log, dt_b, incl, strict, last):
    c = CHUNK
    gb = -jnp.exp(a_log) * _softplus(alpha + dt_b)
    bb = jax.nn.sigmoid(beta_raw)
    q = _l2(_silu(cq)) * (A_DIM ** -0.5)
    k = _l2(_silu(ck))
    v = _silu(cv)

    gc = _dot(incl, jnp.broadcast_to(gb, (c, LANE)))
    tot = jnp.sum(gc * last, axis=0, keepdims=True)
    m1 = gc[:, :c]
    decay = incl * jnp.exp(incl * (m1 - m1.T))
    kb = k * bb
    vb = v * bb
    a = -(strict * decay * _mm_nt(kb, k))
    tinv = _unit_inverse(a)
    eg = jnp.exp(gc)
    u = _mm(tinv, vb)
    w = _mm(tinv, kb * eg)
    qk = _mm_nt(q, k) * decay
    v_new = u - _mm(w, st)
    o = _mm(q * eg, st) + _mm(qk, v_new)
    st_new = st * jnp.exp(tot) + _mm_tn(k * jnp.exp(tot - gc), v_new)
    return st_new, o


def _a_step(sts, cq, ck, cv, gts, pa, h0):
    c = CHUNK
    lane = lax.broadcasted_iota(jnp.int32, (1, LANE), 1)
    ii = lax.broadcasted_iota(jnp.int32, (c, c), 0)
    jj = lax.broadcasted_iota(jnp.int32, (c, c), 1)
    row = lax.broadcasted_iota(jnp.int32, (c, 1), 0)

    def pick(t, col):
        return jnp.sum(jnp.where(lane == col, t, 0.0), axis=1, keepdims=True)

    alpha, beta_raw, a_log, dt_b, incl, strict, last = [], [], [], [], [], [], []
    for b in range(sts.shape[0]):
        h, rev = h0 + b // 2, b % 2
        alpha.append(pick(gts[b], h + 8 * rev))
        beta_raw.append(pick(gts[b], h + 16 + 8 * rev))
        a_log.append(pick(pa[rev:rev + 1, :], h))
        dt_b.append(pick(pa[2 + rev:3 + rev, :], h))
        incl.append(((ii <= jj) if rev else (ii >= jj)).astype(F32))
        strict.append(((ii < jj) if rev else (ii > jj)).astype(F32))
        last.append((row == (0 if rev else c - 1)).astype(F32))
    stack = lambda ts: jnp.concatenate([t[None] for t in ts], axis=0)
    return jax.vmap(_a_chain)(sts, cq, ck, cv, stack(alpha), stack(beta_raw), stack(a_log), stack(dt_b),
                              stack(incl), stack(strict), stack(last))


def _a_final(o, za, pa):
    outs = []
    for j in range(o.shape[1] // A_DIM):
        ln = slice(j * A_DIM, (j + 1) * A_DIM)
        outs.append(_rms(o[:, ln], pa[4:5, :]) * _silu(za[:, ln]))
    return jnp.concatenate(outs, axis=1)


def _a_tiles(n, nchunk, heads):
    tiles = []
    for b in range(2 * heads):
        i = (nchunk - 1 - n) if b % 2 else n
        tiles.append((i, pl.ds(pl.multiple_of(i * CHUNK, CHUNK), CHUNK), slice((b // 2) * A_DIM, (b // 2 + 1) * A_DIM)))
    return tiles


def _a_load(tiles, c_ref, gt_ref):
    cq, ck, cv = (jnp.stack([c_ref[r, sl, ln] for _, sl, ln in tiles], axis=0) for r in range(3))
    return cq, ck, cv, jnp.stack([gt_ref[sl, :] for _, sl, _ in tiles], axis=0)


def _loop_by_two(n, step, init):
    assert n % 2 == 0
    return lax.fori_loop(0, n // 2, lambda m, carry: step(2 * m + 1, step(2 * m, carry, 0), 1), init)


def _a_scan(h0, heads, nchunk, c_ref, gt_ref, pa, of_ref, ob_ref, s_ref):
    def step(n, sts, parity):
        tiles = _a_tiles(n, nchunk, heads)
        sts_new, o = _a_step(sts, *_a_load(tiles, c_ref, gt_ref), pa, h0)
        for b, (i, sl, ln) in enumerate(tiles):
            s_ref[b, i] = sts[b]
            (ob_ref if b % 2 else of_ref)[sl, ln] = o[b]
        return sts_new

    _loop_by_two(nchunk, step, jnp.zeros((2 * heads, A_DIM, A_DIM), F32))


def _a_specs(s, heads):
    wide = heads * A_DIM
    once = pl.Buffered(1)
    trio = pl.BlockSpec((3, s, wide), lambda g: (0, 0, g), pipeline_mode=once)
    gates = pl.BlockSpec((s, LANE), lambda g: (0, P_GT // LANE))
    small = pl.BlockSpec((8, LANE), lambda g: (0, 0))

    def cols(base):
        return pl.BlockSpec((s, wide), lambda g: (0, base // wide + g), pipeline_mode=once)

    state = pl.BlockSpec((2 * heads, s // CHUNK, A_DIM, A_DIM), lambda g: (g, 0, 0, 0), pipeline_mode=once)
    return wide, trio, gates, small, cols, state


def _delta_fwd(cqkv, proj, pa, ride=None):
    s = cqkv.shape[1]
    nchunk = s // CHUNK
    heads = A_FWD_HEADS
    steps = A_HEADS // heads
    wide, trio, gates, small, cols, state = _a_specs(s, heads)
    n_in = len(ride.operands) if ride else 0
    n_out = len(ride.out_shapes) if ride else 0

    def body(*refs):
        c_ref, gt_ref, za_ref, pa_ref = refs[:4]
        out_ref, o_ref, s_ref = refs[4 + n_in:7 + n_in]
        ob_ref = refs[7 + n_in + n_out]
        riders = (refs[4:4 + n_in], refs[7 + n_in:7 + n_in + n_out], refs[8 + n_in + n_out:])
        g = pl.program_id(0)
        if ride:
            pl.when(g == 0)(lambda: ride.start(*riders))
            pl.when(g == steps - 1)(lambda: ride.middle(*riders))
        h0 = g * heads
        pa_v = pa_ref[...]
        _a_scan(h0, heads, nchunk, c_ref, gt_ref, pa_v, o_ref, ob_ref, s_ref)
        o_ref[...] += ob_ref[...]
        out_ref[...] = _a_final(o_ref[...], za_ref[...], pa_v).astype(BF16)
        if ride:
            pl.when(g == steps - 1)(lambda: ride.finish(*riders))

    assert steps > 1
    return pl.pallas_call(
        body, name="delta_fwd", grid=(steps,),
        in_specs=[trio, gates, cols(P_ZA), small] + [HBM] * n_in, out_specs=[cols(0), cols(0), state] + [HBM] * n_out,
        out_shape=[jax.ShapeDtypeStruct((s, D_MODEL), BF16),
                   jax.ShapeDtypeStruct((s, A_WIDTH), F32),
                   jax.ShapeDtypeStruct((2 * A_HEADS, nchunk, A_DIM, A_DIM), F32)]
        + (list(ride.out_shapes) if ride else []),
        scratch_shapes=[pltpu.VMEM((s, wide), F32)] + (list(ride.scratch_shapes) if ride else []),
        compiler_params=_params(("arbitrary",)),
    )(cqkv, proj, proj, pa, *(ride.operands if ride else []))


def _delta_out_bwd(o_sum, proj, pa, d_mixed, tr=256):
    s = o_sum.shape[0]

    def body(o_ref, za_ref, pa_ref, dm_ref, do_ref, dza_ref, dpa_ref):
        @pl.when(pl.program_id(0) == 0)
        def _():
            dpa_ref[...] = jnp.zeros_like(dpa_ref)

        _, vjp = jax.vjp(_a_final, o_ref[...], za_ref[...], pa_ref[...])
        d_o, d_za, dpa = vjp(dm_ref[...].astype(F32))
        do_ref[...] = d_o
        dza_ref[...] = d_za
        dpa_ref[...] += dpa

    def rows(col):
        return pl.BlockSpec((tr, A_WIDTH), lambda i: (i, col))

    small = pl.BlockSpec((8, LANE), lambda i: (0, 0))
    return pl.pallas_call(
        body, name="delta_out_bwd", grid=(s // tr,), in_specs=[rows(0), rows(P_ZA // A_WIDTH), small, rows(0)],
        out_specs=[rows(0), rows(0), small],
        out_shape=[jax.ShapeDtypeStruct((s, A_WIDTH), F32), jax.ShapeDtypeStruct((s, A_WIDTH), F32),
                   jax.ShapeDtypeStruct((8, LANE), F32)],
        compiler_params=_params(("arbitrary",)),
    )(o_sum, proj, pa, d_mixed)


def _delta_bwd(cqkv, proj, pa, d_o, states):
    s = cqkv.shape[1]
    nchunk = s // CHUNK
    heads = A_BWD_HEADS
    wide, trio, gates, small, cols, state = _a_specs(s, heads)

    def body(c_ref, gt_ref, pa_ref, do_ref, s_hbm, dc_ref, dgt_ref, dpa_ref, s_buf, s_sems):
        h0 = pl.program_id(0) * heads
        pa_v = pa_ref[...]

        @pl.when(h0 == 0)
        def _():
            dgt_ref[...] = jnp.zeros_like(dgt_ref)
            dpa_ref[...] = jnp.zeros_like(dpa_ref)

        dc_ref[...] = jnp.zeros_like(dc_ref)

        def state_copies(n, slot):
            return [pltpu.make_async_copy(s_hbm.at[2 * h0 + b, i], s_buf.at[slot, b], s_sems.at[slot, b])
                    for b, (i, _, _) in enumerate(_a_tiles(nchunk - 1 - n, nchunk, heads))]

        for cp in state_copies(0, 0):
            cp.start()

        def step(n, carry, parity):
            d_sts, dpa = carry
            tiles = _a_tiles(nchunk - 1 - n, nchunk, heads)
            for cp in state_copies(n, parity):
                cp.wait()

            @pl.when(n + 1 < nchunk)
            def _():
                for cp in state_copies(n + 1, 1 - parity):
                    cp.start()

            sts = s_buf[parity]
            d_o_t = jnp.stack([do_ref[sl, ln] for _, sl, ln in tiles], axis=0)
            _, vjp_c = jax.vjp(lambda *a: _a_step(*a, h0), sts, *_a_load(tiles, c_ref, gt_ref), pa_v)
            d_prev, dcq, dck, dcv, dgts, dpa_i = vjp_c((d_sts, d_o_t))
            for b, (_, sl, ln) in enumerate(tiles):
                for r, dc in enumerate((dcq, dck, dcv)):
                    dc_ref[r, sl, ln] += dc[b]
                dgt_ref[sl, :] += dgts[b]
            return d_prev, dpa + dpa_i

        init = (jnp.zeros((2 * heads, A_DIM, A_DIM), F32), jnp.zeros((8, LANE), F32))
        _, dpa_out = lax.fori_loop(0, nchunk, lambda n, carry: step(n, carry, n % 2), init)
        dpa_ref[...] += dpa_out

    fixed = pl.BlockSpec((s, LANE), lambda g: (0, 0))
    return pl.pallas_call(
        body, name="delta_bwd", grid=(A_HEADS // heads,),
        in_specs=[trio, gates, small, cols(0), pl.BlockSpec(memory_space=pl.ANY)], out_specs=[trio, fixed, small],
        out_shape=[jax.ShapeDtypeStruct((3, s, A_WIDTH), F32), jax.ShapeDtypeStruct((s, LANE), F32),
                   jax.ShapeDtypeStruct((8, LANE), F32)],
        scratch_shapes=[pltpu.VMEM((2, 2 * heads, A_DIM, A_DIM), F32), pltpu.SemaphoreType.DMA((2, 2 * heads))],
        compiler_params=_params(("arbitrary",)),
    )(cqkv, proj, pa, d_o, states)


def _rope_tables(s):
    inv = ROPE_THETA ** (-jnp.arange(0, B_DIM, 2, dtype=F32) / B_DIM)
    ang = jnp.arange(s, dtype=F32)[:, None] * inv[None, :]
    cos, sin = jnp.cos(ang), jnp.sin(ang)
    return jnp.concatenate([cos, cos], axis=1), jnp.concatenate([-sin, sin], axis=1)


def _b_block(q_t, z_t, k3, v3, cos_q, sin_q, cos_k, sin_k, pb, n, nb):
    w = WINDOW
    def swap(t):
        return jnp.concatenate([t[:, B_DIM // 2:], t[:, :B_DIM // 2]], axis=1)

    grp = B_HEADS // B_KV
    qi = lax.broadcasted_iota(jnp.int32, (grp * w, 3 * w), 0) & (w - 1)
    kj = lax.broadcasted_iota(jnp.int32, (grp * w, 3 * w), 1)
    kpos = kj + (n - 1) * w
    mask = (jnp.abs(kj - w - qi) <= w) & (kpos >= 0) & (kpos < nb * w)
    lane = lax.broadcasted_iota(jnp.int32, (1, LANE), 1)
    qn, kn = pb[0:1, :B_DIM], pb[1:2, :B_DIM]
    cos_g = jnp.concatenate([cos_q] * grp, axis=0)
    sin_g = jnp.concatenate([sin_q] * grp, axis=0)
    def group(q, k, v, sink):
        k = _rms(k, kn)
        k = k * cos_k + swap(k) * sin_k
        q = _rms(q, qn)
        q = q * cos_g + swap(q) * sin_g
        s = _mm_nt(q, k) * (B_DIM ** -0.5)
        s = jnp.where(mask, s, -jnp.inf)
        m = jnp.maximum(jnp.max(s, axis=1, keepdims=True), sink)
        p = jnp.exp(s - m)
        p = p / (jnp.sum(p, axis=1, keepdims=True) + jnp.exp(sink - m))
        return _mm(p, v)

    stack = lambda ts: jnp.concatenate([t[None] for t in ts], axis=0)
    qs, ks, vs, sinks = [], [], [], []
    for hk in range(B_KV):
        heads = [hk * grp + g for g in range(grp)]
        ks.append(k3[:, hk * B_DIM:(hk + 1) * B_DIM])
        vs.append(v3[:, hk * B_DIM:(hk + 1) * B_DIM])
        qs.append(jnp.concatenate([q_t[:, hq * B_DIM:(hq + 1) * B_DIM] for hq in heads], axis=0))
        sinks.append(jnp.concatenate(
            [jnp.broadcast_to(jnp.sum(jnp.where(lane == hq, pb[2:3, :], 0.0), axis=1, keepdims=True), (w, 1))
             for hq in heads], axis=0))
    o = jax.vmap(group)(stack(qs), stack(ks), stack(vs), stack(sinks))
    outs = [o[hk, g * w:(g + 1) * w, :] for hk in range(B_KV) for g in range(grp)]
    return jnp.concatenate(outs, axis=1) * _silu(z_t)


def _b_specs(s):
    nb = s // WINDOW
    qsp = pl.BlockSpec((WINDOW, 512), lambda n: (n, P_QB // 512))
    zsp = pl.BlockSpec((WINDOW, 512), lambda n: (n, P_ZB // 512))

    def three(col, width):
        return [pl.BlockSpec((WINDOW, width), lambda n: (jnp.maximum(n - 1, 0), col)),
                pl.BlockSpec((WINDOW, width), lambda n: (n, col)),
                pl.BlockSpec((WINDOW, width), lambda n: (jnp.minimum(n + 1, nb - 1), col))]

    tab = pl.BlockSpec((WINDOW, B_DIM), lambda n: (n, 0))
    small = pl.BlockSpec((8, LANE), lambda n: (0, 0))
    specs = [qsp, zsp] + three(P_KB // LANE, LANE) + three(P_VB // LANE, LANE) + [tab, tab] + three(0, B_DIM) + three(0, B_DIM) + [small]
    return nb, specs


def _b_args(proj, cos2, sin2, pb):
    return (proj, proj, proj, proj, proj, proj, proj, proj, cos2, sin2, cos2, cos2, cos2, sin2, sin2, sin2, pb)


def _b_load(refs):
    (q_ref, z_ref, kp, kc, kx, vp, vc, vx, cq, sq, ckp, ckc, ckx, skp, skc, skx, pb_ref) = refs
    cat = lambda *r: jnp.concatenate([t[...] for t in r], axis=0)
    return (q_ref[...], z_ref[...], cat(kp, kc, kx), cat(vp, vc, vx), cq[...], sq[...], cat(ckp, ckc, ckx),
            cat(skp, skc, skx), pb_ref[...])


def _attn_b_fwd(proj, cos2, sin2, pb, mixed):
    s = proj.shape[0]
    nb, specs = _b_specs(s)

    def body(*refs):
        o_ref = refs[-1]
        args = _b_load(refs[:-2])
        o_ref[...] = _b_block(*args, pl.program_id(0), nb).astype(BF16)

    return pl.pallas_call(
        body, name="attn_b_fwd", grid=(nb,), in_specs=specs + [pl.BlockSpec(memory_space=pl.ANY)],
        out_specs=pl.BlockSpec((WINDOW, 512), lambda n: (n, A_WIDTH // 512)),
        out_shape=jax.ShapeDtypeStruct(mixed.shape, mixed.dtype), input_output_aliases={len(specs): 0},
        compiler_params=_params(("parallel",)),
    )(*_b_args(proj, cos2, sin2, pb), mixed)


def _attn_b_bwd(proj, cos2, sin2, pb, d_mixed):
    s = proj.shape[0]
    nb, specs = _b_specs(s)
    w = WINDOW

    def body(*refs):
        dm_ref, dq_ref, dz_ref, dk_ref, dv_ref, dpb_ref = refs[-6:]
        n = pl.program_id(0)
        q_t, z_t, k3, v3, cq, sq, ck, sk, pb_v = _b_load(refs[:-6])

        @pl.when(n == 0)
        def _():
            dk_ref[...] = jnp.zeros_like(dk_ref)
            dv_ref[...] = jnp.zeros_like(dv_ref)
            dpb_ref[...] = jnp.zeros_like(dpb_ref)

        def f(q_, z_, k_, v_, pb_):
            return _b_block(q_, z_, k_, v_, cq, sq, ck, sk, pb_, n, nb)

        _, vjp = jax.vjp(f, q_t, z_t, k3, v3, pb_v)
        dq, dz, dk3, dv3, dpb = vjp(dm_ref[...])
        dq_ref[...] = dq
        dz_ref[...] = dz
        dpb_ref[...] += dpb

        def add(j, cond):
            @pl.when(cond)
            def _():
                rows = pl.ds(pl.multiple_of((n - 1 + j) * w, w), w)
                dk_ref[rows, :] += dk3[j * w:(j + 1) * w, :]
                dv_ref[rows, :] += dv3[j * w:(j + 1) * w, :]

        add(0, n > 0)
        add(1, n >= 0)
        add(2, n < nb - 1)

    blk = pl.BlockSpec((w, 512), lambda n: (n, 0))
    whole = pl.BlockSpec((s, LANE), lambda n: (0, 0))
    small = pl.BlockSpec((8, LANE), lambda n: (0, 0))
    return pl.pallas_call(
        body, name="attn_b_bwd", grid=(nb,),
        in_specs=specs + [pl.BlockSpec((w, 512), lambda n: (n, 2))],
        out_specs=[blk, blk, whole, whole, small],
        out_shape=[jax.ShapeDtypeStruct((s, 512), F32), jax.ShapeDtypeStruct((s, 512), F32),
                   jax.ShapeDtypeStruct((s, LANE), F32), jax.ShapeDtypeStruct((s, LANE), F32),
                   jax.ShapeDtypeStruct((8, LANE), F32)],
        compiler_params=_params(("arbitrary",)),
    )(*_b_args(proj, cos2, sin2, pb), d_mixed)


def _mem_kv_fwd(mem, mem_norm_w, w_kv):
    def body(mem_ref, nw_ref, w_ref, kv_ref):
        mn = _rms(mem_ref[...], nw_ref[...]).astype(BF16)
        kv_ref[...] = jnp.dot(mn, w_ref[...], preferred_element_type=F32)

    return pl.pallas_call(
        body, name="mem_kv_fwd", out_shape=jax.ShapeDtypeStruct((MEM_LEN, 2 * C_HEADS * C_DIM), F32),
        compiler_params=_params(),
    )(mem, mem_norm_w, w_kv)


def _mem_kv_bwd(mem, mem_norm_w, w_kv, d_kv):
    def body(mem_ref, nw_ref, w_ref, g_ref, gw_ref, gn_ref):
        mn, vjp = jax.vjp(_rms, mem_ref[...], nw_ref[...])
        g = g_ref[...].astype(BF16)
        gw_ref[...] = lax.dot_general(mn.astype(BF16), g, (((0,), (0,)), ((), ())), preferred_element_type=F32)
        d_mn = lax.dot_general(g, w_ref[...], (((1,), (1,)), ((), ())), preferred_element_type=F32)
        gn_ref[...] = vjp(d_mn)[1]

    return pl.pallas_call(
        body, name="mem_kv_bwd",
        out_shape=[jax.ShapeDtypeStruct((D_MODEL, 2 * C_HEADS * C_DIM), F32), jax.ShapeDtypeStruct((1, D_MODEL), F32)],
        compiler_params=_params(),
    )(mem, mem_norm_w, w_kv, d_kv)


def _c_tile(q_t, z_t, kvm, pc):
    width = C_HEADS * C_DIM
    outs = []
    for h in range(C_HEADS):
        q = _rms(q_t[:, h * C_DIM:(h + 1) * C_DIM], pc[0:1, :])
        k = _rms(kvm[:, h * C_DIM:(h + 1) * C_DIM], pc[1:2, :])
        v = kvm[:, width + h * C_DIM:width + (h + 1) * C_DIM]
        s = _mm_nt(q, k) * (C_DIM ** -0.5)
        p = jnp.exp(s - jnp.max(s, axis=1, keepdims=True))
        p = p / jnp.sum(p, axis=1, keepdims=True)
        outs.append(_mm(p, v))
    return jnp.concatenate(outs, axis=1) * _silu(z_t)


def _attn_c_fwd(proj, kvm, pc, mixed, tq=256):
    s = proj.shape[0]

    def body(q_ref, z_ref, kv_ref, pc_ref, mixed_ref, o_ref):
        o_ref[...] = _c_tile(q_ref[...], z_ref[...], kv_ref[...], pc_ref[...]).astype(BF16)

    return pl.pallas_call(
        body, name="attn_c_fwd", grid=(s // tq,),
        in_specs=[pl.BlockSpec((tq, 512), lambda i: (i, P_QC // 512)), pl.BlockSpec((tq, 512), lambda i: (i, P_ZC // 512)),
                  pl.BlockSpec(kvm.shape, lambda i: (0, 0)), pl.BlockSpec((8, LANE), lambda i: (0, 0)),
                  pl.BlockSpec(memory_space=pl.ANY)],
        out_specs=pl.BlockSpec((tq, 512), lambda i: (i, (A_WIDTH + 512) // 512)),
        out_shape=jax.ShapeDtypeStruct(mixed.shape, mixed.dtype), input_output_aliases={4: 0},
        compiler_params=_params(("parallel",)),
    )(proj, proj, kvm, pc, mixed)


def _attn_c_bwd(proj, kvm, pc, d_mixed, tq=256):
    s = proj.shape[0]

    def body(q_ref, z_ref, kv_ref, pc_ref, dm_ref, dq_ref, dz_ref, dkv_ref, dpc_ref):
        @pl.when(pl.program_id(0) == 0)
        def _():
            dkv_ref[...] = jnp.zeros_like(dkv_ref)
            dpc_ref[...] = jnp.zeros_like(dpc_ref)

        _, vjp = jax.vjp(_c_tile, q_ref[...], z_ref[...], kv_ref[...], pc_ref[...])
        dq, dz, dkv, dpc = vjp(dm_ref[...])
        dq_ref[...] = dq
        dz_ref[...] = dz
        dkv_ref[...] += dkv
        dpc_ref[...] += dpc

    blk = pl.BlockSpec((tq, 512), lambda i: (i, 0))
    kvs = pl.BlockSpec(kvm.shape, lambda i: (0, 0))
    small = pl.BlockSpec((8, LANE), lambda i: (0, 0))
    return pl.pallas_call(
        body, name="attn_c_bwd", grid=(s // tq,),
        in_specs=[pl.BlockSpec((tq, 512), lambda i: (i, P_QC // 512)), pl.BlockSpec((tq, 512), lambda i: (i, P_ZC // 512)),
                  kvs, small, pl.BlockSpec((tq, 512), lambda i: (i, 3))],
        out_specs=[blk, blk, kvs, small],
        out_shape=[jax.ShapeDtypeStruct((s, 512), F32), jax.ShapeDtypeStruct((s, 512), F32),
                   jax.ShapeDtypeStruct(kvm.shape, F32), jax.ShapeDtypeStruct((8, LANE), F32)],
        compiler_params=_params(("arbitrary",)),
    )(proj, proj, kvm, pc, d_mixed)


def _pad_row(v, width=LANE):
    v = v.reshape(1, -1)
    return jnp.pad(v, ((0, 0), (0, width - v.shape[1])))


def _local_step(x, mem, target, norm_w, w_perm_t, conv_w, pa, pb, pc, mem_norm_w, w_kv, w_out, gather=None,
                exchange=None):
    s = x.shape[0]
    cos2, sin2 = _rope_tables(s)
    hn = _rms_fwd(x, norm_w)
    wide = dict(tm=1024, tn=512, tk=2048)
    proj = _matmul(hn, w_perm_t, "nt", F32, "mm_proj", **wide)
    cqkv = _conv_fwd(proj, conv_w)
    if gather is None:
        mixed, o_sum, states = _delta_fwd(cqkv, proj, pa)
    else:
        mixed, o_sum, states, *arrived = _delta_fwd(cqkv, proj, pa, gather[0])
        w_out, w_kv = gather[1](*arrived)
    mixed = _attn_b_fwd(proj, cos2, sin2, pb, mixed)
    kvm = _mem_kv_fwd(mem, mem_norm_w, w_kv)
    mixed = _attn_c_fwd(proj, kvm, pc, mixed)
    mo = _matmul(mixed, w_out, "nn", F32, "mm_out", **wide)
    dy, dyb, loss_parts = _loss_dy(x, mo, target)

    d_mixed = _matmul(dyb, w_out, "nt", F32, "mm_dmixed", **wide)
    g_w_out = _matmul(mixed, dyb, "tn", F32, "mm_gwout", **wide)
    d_qc, d_zc, d_kvm, d_pc = _attn_c_bwd(proj, kvm, pc, d_mixed)
    g_w_kv, g_mem_norm = _mem_kv_bwd(mem, mem_norm_w, w_kv, d_kvm)
    d_qb, d_zb, d_kb, d_vb, d_pb = _attn_b_bwd(proj, cos2, sin2, pb, d_mixed)
    d_o, d_za, d_pa_out = _delta_out_bwd(o_sum, proj, pa, d_mixed)
    d_c, d_gt, d_pa_scan = _delta_bwd(cqkv, proj, pa, d_o, states)
    d_pa = d_pa_out + d_pa_scan
    d_qkv, g_conv = _conv_bwd(proj, conv_w, d_c)
    d_proj = jnp.concatenate([d_qkv, d_za, d_qb, d_zb, d_qc, d_zc, d_kb, d_vb, d_gt,
                              jnp.zeros((s, P_WIDTH - P_GT - LANE), F32)], axis=1).astype(BF16)
    g_w_perm_t = _matmul(d_proj, hn, "tn", F32, "mm_gwin", tm=512, tn=1024, tk=2048)
    ride = exchange(g_w_perm_t, g_w_out, g_w_kv) if exchange else None
    d_hn = _matmul(d_proj, w_perm_t, "nn", F32, "mm_dhn", tm=1024, tn=2048, tk=512, ride=ride)
    d_hn, landed = (d_hn[0], d_hn[1:]) if ride else (d_hn, None)
    g_x, g_norm = _rms_bwd(x, norm_w, d_hn, dy)
    return dict(loss_parts=loss_parts, g_x=g_x, g_norm=g_norm, g_w_perm_t=g_w_perm_t, g_conv=g_conv, d_pa=d_pa,
                d_pb=d_pb, d_pc=d_pc, g_mem_norm=g_mem_norm, g_w_kv=g_w_kv, g_w_out=g_w_out, landed=landed)


_SEGMENTS = ((0, O_GT, 0), (O_GT, O_QB, P_GT), (O_QB, O_KB, P_QB), (O_KB, O_VB, P_KB), (O_VB, O_ZB, P_VB),
             (O_ZB, O_QC, P_ZB), (O_QC, O_ZC, P_QC), (O_ZC, IN_WIDTH, P_ZC))


def _permute_blocks(w4):
    parts = []
    for first, end, _ in sorted(_SEGMENTS, key=lambda seg: seg[2]):
        row = first
        while row < end:
            k = row // W_IN_BLOCK
            stop = min(end, (k + 1) * W_IN_BLOCK)
            parts.append(w4[k][row - k * W_IN_BLOCK:stop - k * W_IN_BLOCK, :])
            row = stop
    parts.append(jnp.zeros((P_WIDTH - IN_WIDTH, w4.shape[2]), w4.dtype))
    return jnp.concatenate(parts, axis=0)


def _unpermute_blocks(g):
    blocks = []
    for k in range(N_CHIPS):
        lo, hi = k * W_IN_BLOCK, (k + 1) * W_IN_BLOCK
        parts = [g[p + max(first, lo) - first:p + min(end, hi) - first, :]
                 for first, end, p in _SEGMENTS if max(first, lo) < min(end, hi)]
        blocks.append(jnp.concatenate(parts, axis=0))
    return jnp.stack(blocks, axis=0)


HBM = pl.BlockSpec(memory_space=pltpu.HBM)


def _place():
    x, y, c = lax.axis_index("x"), lax.axis_index("y"), lax.axis_index("c")
    chips = [(1 - x, y), (x, 1 - y), (1 - x, 1 - y)]
    return x, y, c, 2 * x + y, chips, [2 * cx + cy for cx, cy in chips]


PIECE_ROWS_CAP = 600


def _remote(src, dst, send_sems, recv_sems, k, to):
    return pltpu.make_async_remote_copy(src_ref=src, dst_ref=dst, send_sem=send_sems.at[k], recv_sem=recv_sems.at[k],
                                        device_id=to, device_id_type=MESH)


def _half_cols(ref, c):
    half = ref.shape[-1] // 2
    return pl.ds(pl.multiple_of(c * half, LANE), half)


class _PairedGather:
    def __init__(self, blocks):
        n = len(blocks)
        self.operands = list(blocks)
        self.out_shapes = [jax.ShapeDtypeStruct((N_CHIPS,) + b.shape, b.dtype) for b in blocks]
        self.scratch_shapes = [pltpu.SemaphoreType.DMA((6 * n,)), pltpu.SemaphoreType.DMA((6 * n,))]

    @staticmethod
    def _copies(srcs, dsts, sems):
        x, y, c, me, chips, chip_ids = _place()
        sends, landed, passes, passed = [], [], [], []
        for a, (src, dst) in enumerate(zip(srcs, dsts)):
            mine, other = _half_cols(src, c), _half_cols(src, 1 - c)
            for j, (chip, cid) in enumerate(zip(chips, chip_ids)):
                sends.append(_remote(src.at[:, mine], dst.at[me, :, mine], sems[0], sems[1], 6 * a + j, (*chip, c)))
                here = dst.at[cid, :, mine]
                landed.append(_remote(here, here, sems[0], sems[1], 6 * a + j, (x, y, 1 - c)))
                passes.append(_remote(here, here, sems[0], sems[1], 6 * a + 3 + j, (x, y, 1 - c)))
                there = dst.at[cid, :, other]
                passed.append(_remote(there, there, sems[0], sems[1], 6 * a + 3 + j, (x, y, 1 - c)))
        return sends, landed, passes, passed

    def start(self, srcs, dsts, sems):
        for cp in self._copies(srcs, dsts, sems)[0]:
            cp.start()

    def middle(self, srcs, dsts, sems):
        _, landed, passes, _ = self._copies(srcs, dsts, sems)
        for arrived, onward in zip(landed, passes):
            arrived.wait_recv()
            onward.start()

    def finish(self, srcs, dsts, sems):
        sends, _, passes, passed = self._copies(srcs, dsts, sems)
        for cp in passed:
            cp.wait_recv()
        for cp in sends + passes:
            cp.wait_send()


def _all_gather_weights(bigs, conv_b):
    bigs = tuple(bigs)
    n_big = len(bigs)

    def body(*refs):
        srcs, conv_src = refs[:n_big], refs[n_big]
        dsts, conv_dst = refs[n_big + 1:2 * n_big + 1], refs[2 * n_big + 1]
        send_sems, recv_sems, local_sems = refs[2 * n_big + 2:]
        x, y, c, me, chips, chip_ids = _place()
        sibling = (x, y, 1 - c)
        local = [pltpu.make_async_copy(src, dst.at[me], local_sems.at[a]) for a, (src, dst) in enumerate(zip(srcs, dsts))]
        local.append(pltpu.make_async_copy(conv_src, conv_dst.at[me], local_sems.at[n_big]))
        for cp in local:
            cp.start()
        sends = []
        for a, (src, dst) in enumerate(zip(srcs, dsts)):
            mine = _half_cols(src, c)
            for j, chip in enumerate(chips):
                sends.append(_remote(src.at[:, mine], dst.at[me, :, mine], send_sems, recv_sems, 6 * a + j, (*chip, c)))
        for j, chip in enumerate(chips):
            sends.append(_remote(conv_src, conv_dst.at[me], send_sems, recv_sems, 6 * n_big + j, (*chip, c)))
        for cp in sends:
            cp.start()
        passed = []
        for a, (src, dst) in enumerate(zip(srcs, dsts)):
            mine = _half_cols(src, c)
            for j, cid in enumerate(chip_ids):
                landed = dst.at[cid, :, mine]
                _remote(landed, landed, send_sems, recv_sems, 6 * a + j, sibling).wait_recv()
                cp = _remote(landed, landed, send_sems, recv_sems, 6 * a + 3 + j, sibling)
                cp.start()
                passed.append(cp)
        for a, (src, dst) in enumerate(zip(srcs, dsts)):
            other = _half_cols(src, 1 - c)
            for j, cid in enumerate(chip_ids):
                landed = dst.at[cid, :, other]
                _remote(landed, landed, send_sems, recv_sems, 6 * a + 3 + j, sibling).wait_recv()
        for j, cid in enumerate(chip_ids):
            _remote(conv_src, conv_dst.at[cid], send_sems, recv_sems, 6 * n_big + j, sibling).wait_recv()
        for cp in sends + passed:
            cp.wait_send()
        for cp in local:
            cp.wait()

    n_sem = 6 * n_big + 3
    return pl.pallas_call(
        body, name="all_gather_weights",
        out_shape=[jax.ShapeDtypeStruct((N_CHIPS,) + w.shape, w.dtype) for w in bigs + (conv_b,)],
        in_specs=[pl.BlockSpec(memory_space=pltpu.VMEM)] * (n_big + 1), out_specs=[HBM] * (n_big + 1),
        scratch_shapes=[pltpu.SemaphoreType.DMA((n_sem,)), pltpu.SemaphoreType.DMA((n_sem,)),
                        pltpu.SemaphoreType.DMA((n_big + 1,))],
        compiler_params=_params(),
    )(*bigs, conv_b)


def _pair_exchange(grads):
    n = len(grads)
    pieces = [_row_tile(g.shape[1]) for g in grads]

    def body(*refs):
        srcs, gots = refs[:n], refs[n:2 * n]
        stages = refs[2 * n:3 * n]
        send_sems, recv_sems, load_sems = refs[3 * n:]
        x, y, c, _, _, _ = _place()
        sibling = (x, y, 1 - c)
        for a in range(n):
            slabs, rows, _ = gots[a].shape
            piece = pieces[a]
            per_slab = rows // piece
            theirs = _half_cols(srcs[a], 1 - c)
            loads, sends = [], []
            for i in range(slabs * per_slab):
                k, r, slot = i // per_slab, i % per_slab, i % 2
                part = pl.ds(r * piece, piece)
                loads.append(pltpu.make_async_copy(srcs[a].at[k, part, theirs], stages[a].at[slot], load_sems.at[2 * a + slot]))
                sends.append(pltpu.make_async_remote_copy(
                    src_ref=stages[a].at[slot], dst_ref=gots[a].at[k, part, :],
                    send_sem=send_sems.at[2 * a + slot], recv_sem=recv_sems.at[a], device_id=sibling, device_id_type=MESH))
            loads[0].start()
            for i in range(len(loads)):
                loads[i].wait()
                sends[i].start()
                if i + 1 < len(loads):
                    if i >= 1:
                        sends[i - 1].wait_send()
                    loads[i + 1].start()
            for cp in sends[-2:]:
                cp.wait_send()
        for a in range(n):
            whole = srcs[a].at[:, :, _half_cols(srcs[a], c)]
            pltpu.make_async_remote_copy(src_ref=whole, dst_ref=gots[a], send_sem=send_sems.at[2 * a],
                                         recv_sem=recv_sems.at[a], device_id=sibling, device_id_type=MESH).wait_recv()

    halves = [jax.ShapeDtypeStruct((g.shape[0], g.shape[1], g.shape[2] // 2), g.dtype) for g in grads]
    return pl.pallas_call(
        body, name="grad_pair_exchange", out_shape=halves, in_specs=[HBM] * n, out_specs=[HBM] * n,
        scratch_shapes=[pltpu.VMEM((2, piece, g.shape[2] // 2), g.dtype) for piece, g in zip(pieces, grads)]
        + [pltpu.SemaphoreType.DMA((2 * n,)), pltpu.SemaphoreType.DMA((n,)), pltpu.SemaphoreType.DMA((2 * n,))],
        compiler_params=_params(),
    )(*grads)


class _ChipExchange:
    def __init__(self, halves):
        n = len(halves)
        self.operands = list(halves)
        self.out_shapes = [jax.ShapeDtypeStruct((N_CHIPS - 1,) + h.shape[1:], h.dtype) for h in halves]
        self.scratch_shapes = [pltpu.SemaphoreType.DMA((3 * n,)), pltpu.SemaphoreType.DMA((3 * n,))]

    @staticmethod
    def _copies(srcs, lands, sems):
        _, _, c, _, chips, chip_ids = _place()
        return [_remote(src.at[cid], land.at[j], sems[0], sems[1], 3 * a + j, (*chip, c))
                for a, (src, land) in enumerate(zip(srcs, lands)) for j, (chip, cid) in enumerate(zip(chips, chip_ids))]

    def start(self, srcs, lands, sems):
        for cp in self._copies(srcs, lands, sems):
            cp.start()

    def finish(self, srcs, lands, sems):
        copies = self._copies(srcs, lands, sems)
        for cp in copies:
            cp.wait_recv()
        for cp in copies:
            cp.wait_send()


def _pair_gather(halves):
    n = len(halves)

    def body(*refs):
        srcs, fulls = refs[:n], refs[n:2 * n]
        send_sems, recv_sems, local_sems = refs[2 * n:]
        x, y, c, _, _, _ = _place()
        copies = []
        for a in range(n):
            mine = _half_cols(fulls[a], c)
            keep = pltpu.make_async_copy(srcs[a], fulls[a].at[:, mine], local_sems.at[a])
            keep.start()
            give = _remote(srcs[a], fulls[a].at[:, mine], send_sems, recv_sems, a, (x, y, 1 - c))
            give.start()
            copies += [keep, give]
        for a in range(n):
            other = _half_cols(fulls[a], 1 - c)
            copies[2 * a].wait()
            copies[2 * a + 1].wait_send()
            _remote(srcs[a], fulls[a].at[:, other], send_sems, recv_sems, a, (x, y, 1 - c)).wait_recv()

    return pl.pallas_call(
        body, name="grad_pair_gather",
        out_shape=[jax.ShapeDtypeStruct((h.shape[0], 2 * h.shape[1]), h.dtype) for h in halves],
        in_specs=[pl.BlockSpec(memory_space=pltpu.VMEM)] * n, out_specs=[HBM] * n,
        scratch_shapes=[pltpu.SemaphoreType.DMA((n,)), pltpu.SemaphoreType.DMA((n,)), pltpu.SemaphoreType.DMA((n,))],
    )(*halves)


def _all_reduce_small(p):
    n_dev = 8

    def body(p_ref, o_ref, land, send_sems, recv_sems):
        x, y, c = lax.axis_index("x"), lax.axis_index("y"), lax.axis_index("c")
        me = 4 * x + 2 * y + c
        land[me] = p_ref[...]
        sends = []
        for k in range(1, n_dev):
            fx, fy, fc = (k >> 2) & 1, (k >> 1) & 1, k & 1
            to = (x ^ fx, y ^ fy, c ^ fc)
            cp = _remote(p_ref, land.at[me], send_sems, recv_sems, k - 1, to)
            cp.start()
            sends.append(cp)
        for k in range(1, n_dev):
            _remote(p_ref, land.at[me ^ k], send_sems, recv_sems, k - 1, (x, y, c)).wait_recv()
        total = land[0]
        for d in range(1, n_dev):
            total = total + land[d]
        o_ref[...] = total
        for cp in sends:
            cp.wait_send()

    vm = pl.BlockSpec(memory_space=pltpu.VMEM)
    return pl.pallas_call(
        body, name="all_reduce_small", out_shape=jax.ShapeDtypeStruct(p.shape, p.dtype), in_specs=[vm], out_specs=vm,
        scratch_shapes=[pltpu.VMEM((n_dev,) + p.shape, p.dtype), pltpu.SemaphoreType.DMA((n_dev - 1,)),
                        pltpu.SemaphoreType.DMA((n_dev - 1,))],
    )(p)


def _row_tile(rows):
    fits = [t for t in range(8, min(rows, PIECE_ROWS_CAP) + 1, 8) if rows % t == 0]
    return max(fits) if fits else rows


def _pair_sum(full, got, core, name):
    n, r, c = got.shape
    tr = _row_tile(r)

    def body(core_ref, a_ref, b_ref, o_ref):
        o_ref[...] = (a_ref[...] + b_ref[...]).astype(BF16)

    blk = pl.BlockSpec((None, tr, c), lambda i, j, core_ref: (i, j, 0))
    grid_spec = pltpu.PrefetchScalarGridSpec(
        num_scalar_prefetch=1, grid=(n, r // tr),
        in_specs=[pl.BlockSpec((None, tr, c), lambda i, j, core_ref: (i, j, core_ref[0])), blk], out_specs=blk)
    return pl.pallas_call(body, name=name, grid_spec=grid_spec, out_shape=jax.ShapeDtypeStruct(got.shape, BF16),
                          compiler_params=_params(("parallel", "parallel")))(core, full, got)


def _chip_sum(full, got, land, place, name):
    n, r, c = land.shape
    tr = _row_tile(r)

    def body(place_ref, a_ref, b_ref, l_ref, o_ref):
        total = a_ref[...] + b_ref[...]
        for j in range(n):
            total = total + l_ref[j].astype(F32)
        o_ref[...] = total

    grid_spec = pltpu.PrefetchScalarGridSpec(
        num_scalar_prefetch=1, grid=(r // tr,),
        in_specs=[pl.BlockSpec((None, tr, c), lambda i, p: (p[0], i, p[1])),
                  pl.BlockSpec((None, tr, c), lambda i, p: (p[0], i, 0)),
                  pl.BlockSpec((n, tr, c), lambda i, p: (0, i, 0))],
        out_specs=pl.BlockSpec((tr, c), lambda i, p: (i, 0)))
    return pl.pallas_call(body, name=name, grid_spec=grid_spec, out_shape=jax.ShapeDtypeStruct((r, c), F32),
                          compiler_params=_params(("parallel",)))(place, full, got, land)


def _adamw(w, g, m, v, name):
    r, c = w.shape
    tr = _row_tile(r)
    tc = 1024 if c % 1024 == 0 else c

    def body(w_ref, g_ref, m_ref, v_ref, d_ref, mo_ref, vo_ref):
        g_ = g_ref[...]
        m2 = ADAM_B1 * m_ref[...] + (1.0 - ADAM_B1) * g_
        v2 = ADAM_B2 * v_ref[...] + (1.0 - ADAM_B2) * jnp.square(g_)
        m_hat = m2 / (1.0 - ADAM_B1 ** ADAM_STEP)
        v_hat = v2 / (1.0 - ADAM_B2 ** ADAM_STEP)
        d_ref[...] = -ADAM_LR * (m_hat / (jnp.sqrt(v_hat) + ADAM_EPS) + ADAM_WD * w_ref[...])
        mo_ref[...] = m2
        vo_ref[...] = v2

    blk = pl.BlockSpec((tr, tc), lambda i, j: (i, j))
    return pl.pallas_call(body, name=name, grid=(r // tr, c // tc), in_specs=[blk] * 4, out_specs=[blk] * 3,
                          out_shape=[jax.ShapeDtypeStruct(w.shape, F32)] * 3,
                          compiler_params=_params(("parallel", "parallel")))(w, g, m, v)


SMALL_NAMES = ("norm_w", "mem_norm_w", "o_norm_a", "q_norm_c", "k_norm_c", "q_norm_b", "k_norm_b",
               "a_log_fwd", "a_log_bwd", "dt_bias_fwd", "dt_bias_bwd", "sink_b")
SMALL_SIZES = (2048, 2048, 128, 128, 128, 64, 64, 8, 8, 8, 8, 8)
SMALL_LOSS = sum(SMALL_SIZES)
SMALL_CONV = 5120
SMALL_TOTAL = SMALL_CONV + CONV_K * 3 * A_WIDTH
SMALL_ROWS = SMALL_TOTAL // LANE


def _pack_small(parts, extra=None, conv=None):
    vec = [parts[n].reshape(-1) for n in SMALL_NAMES]
    vec.append(jnp.zeros((1,), F32) if extra is None else extra.reshape(1))
    vec.append(jnp.zeros((SMALL_CONV - SMALL_LOSS - 1,), F32))
    vec.append(jnp.zeros((SMALL_TOTAL - SMALL_CONV,), F32) if conv is None else conv.reshape(-1))
    return jnp.concatenate(vec).reshape(SMALL_ROWS, LANE)


def _unpack_small(packed):
    flat = packed.reshape(-1)
    out, off = {}, 0
    for n, size in zip(SMALL_NAMES, SMALL_SIZES):
        out[n] = flat[off:off + size].reshape(1, size)
        off += size
    return out


WEIGHT_ORDER = ("norm_w", "w_in", "conv_w_a", "a_log_fwd", "a_log_bwd", "dt_bias_fwd", "dt_bias_bwd", "o_norm_a",
                "q_norm_b", "k_norm_b", "sink_b", "mem_norm_w", "w_mem_kv", "q_norm_c", "k_norm_c", "w_out")


def kernel(x, mem, norm_w, w_in, conv_w_a, a_log_fwd, a_log_bwd, dt_bias_fwd, dt_bias_bwd, o_norm_a, q_norm_b, k_norm_b, sink_b, mem_norm_w, w_mem_kv, q_norm_c, k_norm_c, w_out, loss_target, m_norm_w, m_w_in, m_conv_w_a, m_a_log_fwd, m_a_log_bwd, m_dt_bias_fwd, m_dt_bias_bwd, m_o_norm_a, m_q_norm_b, m_k_norm_b, m_sink_b, m_mem_norm_w, m_w_mem_kv, m_q_norm_c, m_k_norm_c, m_w_out, v_norm_w, v_w_in, v_conv_w_a, v_a_log_fwd, v_a_log_bwd, v_dt_bias_fwd, v_dt_bias_bwd, v_o_norm_a, v_q_norm_b, v_k_norm_b, v_sink_b, v_mem_norm_w, v_w_mem_kv, v_q_norm_c, v_k_norm_c, v_w_out):
    weights = dict(norm_w=norm_w, w_in=w_in, conv_w_a=conv_w_a, a_log_fwd=a_log_fwd, a_log_bwd=a_log_bwd,
                   dt_bias_fwd=dt_bias_fwd, dt_bias_bwd=dt_bias_bwd, o_norm_a=o_norm_a, q_norm_b=q_norm_b,
                   k_norm_b=k_norm_b, sink_b=sink_b, mem_norm_w=mem_norm_w, w_mem_kv=w_mem_kv, q_norm_c=q_norm_c,
                   k_norm_c=k_norm_c, w_out=w_out)
    mom1 = dict(norm_w=m_norm_w, w_in=m_w_in, conv_w_a=m_conv_w_a, a_log_fwd=m_a_log_fwd, a_log_bwd=m_a_log_bwd,
                dt_bias_fwd=m_dt_bias_fwd, dt_bias_bwd=m_dt_bias_bwd, o_norm_a=m_o_norm_a, q_norm_b=m_q_norm_b,
                k_norm_b=m_k_norm_b, sink_b=m_sink_b, mem_norm_w=m_mem_norm_w, w_mem_kv=m_w_mem_kv,
                q_norm_c=m_q_norm_c, k_norm_c=m_k_norm_c, w_out=m_w_out)
    mom2 = dict(norm_w=v_norm_w, w_in=v_w_in, conv_w_a=v_conv_w_a, a_log_fwd=v_a_log_fwd, a_log_bwd=v_a_log_bwd,
                dt_bias_fwd=v_dt_bias_fwd, dt_bias_bwd=v_dt_bias_bwd, o_norm_a=v_o_norm_a, q_norm_b=v_q_norm_b,
                k_norm_b=v_k_norm_b, sink_b=v_sink_b, mem_norm_w=v_mem_norm_w, w_mem_kv=v_w_mem_kv,
                q_norm_c=v_q_norm_c, k_norm_c=v_k_norm_c, w_out=v_w_out)
    chip = 2 * lax.axis_index("x") + lax.axis_index("y")

    w_in4, conv4 = _all_gather_weights([jnp.transpose(w_in[0]).astype(BF16)], conv_w_a[0])
    w_perm_t = _permute_blocks(w_in4)
    conv_full = jnp.transpose(conv4, (1, 0, 2)).reshape(CONV_K, 3 * A_WIDTH)
    own_out, own_kv = w_out[0].astype(BF16), w_mem_kv[0].astype(BF16)

    def assemble(w_out4, w_kv4):
        w_out4 = lax.dynamic_update_index_in_dim(w_out4, own_out, chip, 0)
        w_kv4 = lax.dynamic_update_index_in_dim(w_kv4, own_kv, chip, 0)
        return w_out4.reshape(D_MODEL, D_MODEL), w_kv4.reshape(D_MODEL, 2 * C_HEADS * C_DIM)

    gather = (_PairedGather([own_out, own_kv]), assemble)
    pa = jnp.concatenate([_pad_row(a_log_fwd), _pad_row(a_log_bwd), _pad_row(dt_bias_fwd), _pad_row(dt_bias_bwd),
                          _pad_row(o_norm_a), jnp.zeros((3, LANE), F32)], axis=0)
    pb = jnp.concatenate([_pad_row(q_norm_b), _pad_row(k_norm_b), _pad_row(sink_b), jnp.zeros((5, LANE), F32)], axis=0)
    pc = jnp.concatenate([_pad_row(q_norm_c), _pad_row(k_norm_c), jnp.zeros((6, LANE), F32)], axis=0)

    full, got = [], []

    def exchange(g_w_perm_t, g_w_out, g_w_kv):
        full.extend([_unpermute_blocks(g_w_perm_t), g_w_out.reshape(N_CHIPS, D_MODEL // N_CHIPS, D_MODEL),
                     g_w_kv.reshape(N_CHIPS, D_MODEL // N_CHIPS, 2 * C_HEADS * C_DIM)])
        core = lax.axis_index("c").astype(jnp.int32).reshape(1)
        got.extend(_pair_exchange(full))
        return _ChipExchange([_pair_sum(a, b, core, "grad_pair_sum_%d" % i) for i, (a, b) in enumerate(zip(full, got))])

    r = _local_step(x[0], mem[0], loss_target[0], norm_w, w_perm_t, conv_full, pa, pb, pc, mem_norm_w, None, None,
                    gather, exchange)
    place = jnp.stack([chip, lax.axis_index("c")]).astype(jnp.int32)
    reduced = [_chip_sum(a, b, l, place, "grad_chip_sum_%d" % i)
               for i, (a, b, l) in enumerate(zip(full, got, r["landed"]))]
    g_w_in_t, g_w_out, g_w_kv = _pair_gather(reduced)

    d_pa, d_pb, d_pc = r["d_pa"], r["d_pb"], r["d_pc"]
    small_g = dict(norm_w=r["g_norm"], mem_norm_w=r["g_mem_norm"], o_norm_a=d_pa[4], q_norm_c=d_pc[0], k_norm_c=d_pc[1],
                   q_norm_b=d_pb[0, :B_DIM], k_norm_b=d_pb[1, :B_DIM], a_log_fwd=d_pa[0, :A_HEADS],
                   a_log_bwd=d_pa[1, :A_HEADS], dt_bias_fwd=d_pa[2, :A_HEADS], dt_bias_bwd=d_pa[3, :A_HEADS],
                   sink_b=d_pb[2, :B_HEADS])
    packed = _all_reduce_small(_pack_small(small_g, jnp.sum(r["loss_parts"][:, 0, 0]), r["g_conv"]))
    flat = packed.reshape(-1)
    loss = flat[SMALL_LOSS]
    conv_sum = flat[SMALL_CONV:].reshape(CONV_K, 3 * A_WIDTH)
    conv_cols = 3 * A_WIDTH // N_CHIPS
    g_conv = lax.dynamic_slice(conv_sum, (0, chip * conv_cols), (CONV_K, conv_cols))

    grads = _unpack_small(packed)
    grads.update(w_in=jnp.transpose(g_w_in_t), w_mem_kv=g_w_kv, w_out=g_w_out, conv_w_a=g_conv)
    delta, new_m, new_v = {}, {}, {}
    for n in ("w_mem_kv", "w_out", "conv_w_a"):
        delta[n], new_m[n], new_v[n] = _adamw(weights[n][0], grads[n], mom1[n][0], mom2[n][0], "adamw_" + n)
    stepped = _adamw(jnp.transpose(w_in[0]), g_w_in_t, jnp.transpose(m_w_in[0]), jnp.transpose(v_w_in[0]), "adamw_w_in")
    delta["w_in"], new_m["w_in"], new_v["w_in"] = (jnp.transpose(t) for t in stepped)
    d_s, m_s, v_s = _adamw(_pack_small(weights), packed, _pack_small(mom1), _pack_small(mom2), "adamw_small")
    d_s, m_s, v_s = _unpack_small(d_s), _unpack_small(m_s), _unpack_small(v_s)
    for n in SMALL_NAMES:
        delta[n], new_m[n], new_v[n] = d_s[n], m_s[n], v_s[n]

    def shaped(tree):
        return [tree[n].reshape(weights[n].shape) for n in WEIGHT_ORDER]

    return (loss, r["g_x"].reshape(x.shape), *shaped(grads), *shaped(delta), *shaped(new_m), *shaped(new_v))
```

```python
import functools

import jax
import jax.numpy as jnp
from jax import lax
from jax.experimental import pallas as pl
from jax.experimental.pallas import tpu as pltpu

F32 = jnp.float32
BF16 = jnp.bfloat16
HI = lax.Precision.HIGHEST
MESH = pl.DeviceIdType.MESH

D_MODEL = 2048
A_WIDTH = 1024
A_HEADS = 8
A_DIM = 128
CONV_K = 5
CHUNK = 64
B_HEADS = 8
B_KV = 2
B_DIM = 64
WINDOW = 128
C_HEADS = 4
C_DIM = 128
MEM_LEN = 256
ROPE_THETA = 10000.0
EPS = 1e-6
IN_WIDTH = 6432
N_CHIPS = 4
W_IN_BLOCK = IN_WIDTH // N_CHIPS

LANE = 128
P_QA, P_KA, P_VA, P_ZA = 0, 1024, 2048, 3072
P_QB, P_ZB, P_QC, P_ZC = 4096, 4608, 5120, 5632
P_KB, P_VB, P_GT = 6144, 6272, 6400
P_WIDTH = 6656
O_GT, O_QB, O_KB, O_VB, O_ZB, O_QC, O_ZC = 4096, 4128, 4640, 4768, 4896, 5408, 5920

ADAM_LR, ADAM_B1, ADAM_B2, ADAM_EPS, ADAM_WD, ADAM_STEP = 0.001, 0.9, 0.999, 1e-08, 0.01, 10

VMEM_LIMIT = 56 * 1024 * 1024


def _params(sem=None):
    return pltpu.CompilerParams(dimension_semantics=sem, vmem_limit_bytes=VMEM_LIMIT)


def _dot(a, b, dims=(((1,), (0,)), ((), ())), precision=HI):
    return lax.dot_general(a, b, dims, precision=precision, preferred_element_type=F32)


def _dot_nt(a, b, precision=HI):
    return _dot(a, b, (((1,), (1,)), ((), ())), precision)


def _dot_tn(a, b, precision=HI):
    return _dot(a, b, (((0,), (0,)), ((), ())), precision)


_NN = (((1,), (0,)), ((), ()))
_NT = (((1,), (1,)), ((), ()))
_TN = (((0,), (0,)), ((), ()))


def _bdot(a, b, dims):
    return lax.dot_general(a.astype(BF16), b.astype(BF16), dims, preferred_element_type=F32)


@jax.custom_vjp
def _mm(a, b):
    return _bdot(a, b, _NN)


_mm.defvjp(lambda a, b: (_bdot(a, b, _NN), (a, b)),
           lambda res, ct: (_bdot(ct, res[1], _NT), _bdot(res[0], ct, _TN)))


@jax.custom_vjp
def _mm_nt(a, b):
    return _bdot(a, b, _NT)


_mm_nt.defvjp(lambda a, b: (_bdot(a, b, _NT), (a, b)),
              lambda res, ct: (_bdot(ct, res[1], _NN), _bdot(ct, res[0], _TN)))


@jax.custom_vjp
def _mm_tn(a, b):
    return _bdot(a, b, _TN)


_mm_tn.defvjp(lambda a, b: (_bdot(a, b, _TN), (a, b)),
              lambda res, ct: (_bdot(res[1], ct, _NT), _bdot(res[0], ct, _NN)))


def _rms(t, w):
    return t * lax.rsqrt(jnp.mean(t * t, axis=-1, keepdims=True) + EPS) * w


def _l2(t):
    return t * lax.rsqrt(jnp.sum(t * t, axis=-1, keepdims=True) + EPS)


def _silu(t):
    return t * jax.nn.sigmoid(t)


def _softplus(t):
    return jnp.maximum(t, 0.0) + jnp.log1p(jnp.exp(-jnp.abs(t)))


def _matmul(a, b, mode, out_dtype, name, tm=512, tn=512, tk=512, ride=None):
    (m, k) = a.shape[::-1] if mode == "tn" else a.shape
    n = b.shape[0] if mode == "nt" else b.shape[1]
    tm, tn, tk = min(tm, m), min(tn, n), min(tk, k)
    assert m % tm == 0 and n % tn == 0 and k % tk == 0, (m, n, k, tm, tn, tk)
    if mode == "nn":
        a_spec = pl.BlockSpec((tm, tk), lambda i, j, kk: (i, kk))
        b_spec = pl.BlockSpec((tk, tn), lambda i, j, kk: (kk, j))
        dims = (((1,), (0,)), ((), ()))
    elif mode == "nt":
        a_spec = pl.BlockSpec((tm, tk), lambda i, j, kk: (i, kk))
        b_spec = pl.BlockSpec((tn, tk), lambda i, j, kk: (j, kk))
        dims = (((1,), (1,)), ((), ()))
    else:
        a_spec = pl.BlockSpec((tk, tm), lambda i, j, kk: (kk, i))
        b_spec = pl.BlockSpec((tk, tn), lambda i, j, kk: (kk, j))
        dims = (((0,), (0,)), ((), ()))
    nk = k // tk
    grid = (m // tm, n // tn, nk)
    n_in = len(ride.operands) if ride else 0
    n_out = len(ride.out_shapes) if ride else 0

    def body(*refs):
        a_ref, b_ref, o_ref = refs[0], refs[1], refs[2 + n_in]
        scratch = refs[3 + n_in + n_out:]
        step = (pl.program_id(0) * grid[1] + pl.program_id(1)) * nk + pl.program_id(2)
        riders = (refs[2:2 + n_in], refs[3 + n_in:3 + n_in + n_out], scratch[(0 if nk == 1 else 1):])
        if ride:
            pl.when(step == 0)(lambda: ride.start(*riders))
        if nk == 1:
            o_ref[...] = _bdot(a_ref[...], b_ref[...], dims).astype(out_dtype)
        else:
            acc_ref, kk = scratch[0], pl.program_id(2)

            @pl.when(kk == 0)
            def _():
                acc_ref[...] = jnp.zeros_like(acc_ref)

            acc_ref[...] += _bdot(a_ref[...], b_ref[...], dims)

            @pl.when(kk == nk - 1)
            def _():
                o_ref[...] = acc_ref[...].astype(out_dtype)
        if ride:
            pl.when(step == grid[0] * grid[1] * nk - 1)(lambda: ride.finish(*riders))

    out = pl.pallas_call(
        body, name=name, grid=grid,
        in_specs=[a_spec, b_spec] + [HBM] * n_in,
        out_specs=[pl.BlockSpec((tm, tn), lambda i, j, kk: (i, j))] + [HBM] * n_out,
        out_shape=[jax.ShapeDtypeStruct((m, n), out_dtype)] + (list(ride.out_shapes) if ride else []),
        scratch_shapes=([] if nk == 1 else [pltpu.VMEM((tm, tn), F32)]) + (list(ride.scratch_shapes) if ride else []),
        compiler_params=_params(("arbitrary",) * 3 if ride else ("parallel", "parallel", "arbitrary")),
    )(a, b, *(ride.operands if ride else []))
    return out if ride else out[0]


def _rms_fwd(x, w, tr=256):
    s, d = x.shape

    def body(x_ref, w_ref, o_ref):
        o_ref[...] = _rms(x_ref[...], w_ref[...]).astype(BF16)

    return pl.pallas_call(
        body, name="rms_fwd", grid=(s // tr,),
        in_specs=[pl.BlockSpec((tr, d), lambda i: (i, 0)), pl.BlockSpec((1, d), lambda i: (0, 0))],
        out_specs=pl.BlockSpec((tr, d), lambda i: (i, 0)),
        out_shape=jax.ShapeDtypeStruct((s, d), BF16), compiler_params=_params(("parallel",)),
    )(x, w)


def _rms_bwd(x, w, d_hn, dy, tr=256):
    s, d = x.shape

    def body(x_ref, w_ref, g_ref, dy_ref, gx_ref, gw_ref):
        _, vjp = jax.vjp(_rms, x_ref[...], w_ref[...])
        dx, dw = vjp(g_ref[...])
        gx_ref[...] = dy_ref[...] + dx

        @pl.when(pl.program_id(0) == 0)
        def _():
            gw_ref[...] = jnp.zeros_like(gw_ref)

        gw_ref[...] += dw

    row = pl.BlockSpec((tr, d), lambda i: (i, 0))
    vec = pl.BlockSpec((1, d), lambda i: (0, 0))
    return pl.pallas_call(
        body, name="rms_bwd", grid=(s // tr,), in_specs=[row, vec, row, row], out_specs=[row, vec],
        out_shape=[jax.ShapeDtypeStruct((s, d), F32), jax.ShapeDtypeStruct((1, d), F32)],
        compiler_params=_params(("arbitrary",)),
    )(x, w, d_hn, dy)


def _loss_dy(x, mo, target, tr=256):
    s, d = x.shape
    nt = s // tr

    def body(x_ref, mo_ref, t_ref, dy_ref, dyb_ref, l_ref):
        err = x_ref[...] + mo_ref[...] - t_ref[...]
        dy = err * (1.0 / d)
        dy_ref[...] = dy
        dyb_ref[...] = dy.astype(BF16)
        l_ref[...] = jnp.full(l_ref.shape, 0.5 * jnp.sum(jnp.sum(err * err, axis=1, keepdims=True) * (1.0 / d)), F32)

    row = pl.BlockSpec((tr, d), lambda i: (i, 0))
    return pl.pallas_call(
        body, name="loss_dy", grid=(nt,), in_specs=[row, row, row],
        out_specs=[row, row, pl.BlockSpec((1, 8, LANE), lambda i: (i, 0, 0))],
        out_shape=[jax.ShapeDtypeStruct((s, d), F32), jax.ShapeDtypeStruct((s, d), BF16),
                   jax.ShapeDtypeStruct((nt, 8, LANE), F32)],
        compiler_params=_params(("parallel",)),
    )(x, mo, target)


def _shift_rows(t, s):
    if s == 0:
        return t
    n = t.shape[0]
    rolled = pltpu.roll(t, (-s) % n, axis=0)
    idx = lax.broadcasted_iota(jnp.int32, t.shape, 0) + s
    return jnp.where((idx >= 0) & (idx < n), rolled, 0.0)


def _conv_fwd(proj, conv_w):
    s = proj.shape[0]
    nblk = 3 * A_WIDTH // LANE

    def body(x_ref, w_ref, o_ref):
        x = x_ref[...]
        acc = jnp.zeros_like(x)
        for j in range(CONV_K):
            acc = acc + w_ref[j:j + 1, :] * _shift_rows(x, j - CONV_K // 2)
        o_ref[...] = acc

    return pl.pallas_call(
        body, name="conv_fwd", grid=(nblk,),
        in_specs=[pl.BlockSpec((s, LANE), lambda i: (0, i)), pl.BlockSpec((CONV_K, LANE), lambda i: (0, i))],
        out_specs=pl.BlockSpec((None, s, LANE), lambda i: (i // A_HEADS, 0, i % A_HEADS)),
        out_shape=jax.ShapeDtypeStruct((3, s, A_WIDTH), F32), compiler_params=_params(("parallel",)),
    )(proj, conv_w)


def _conv_bwd(proj, conv_w, d_c):
    s = proj.shape[0]
    nblk = 3 * A_WIDTH // LANE

    def body(x_ref, w_ref, g_ref, dx_ref, dw_ref):
        x, g = x_ref[...], g_ref[...]
        acc = jnp.zeros_like(x)
        for j in range(CONV_K):
            off = j - CONV_K // 2
            acc = acc + w_ref[j:j + 1, :] * _shift_rows(g, -off)
            dw_ref[j:j + 1, :] = jnp.sum(_shift_rows(x, off) * g, axis=0, keepdims=True)
        dx_ref[...] = acc

    col = pl.BlockSpec((s, LANE), lambda i: (0, i))
    wsp = pl.BlockSpec((CONV_K, LANE), lambda i: (0, i))
    dsp = pl.BlockSpec((None, s, LANE), lambda i: (i // A_HEADS, 0, i % A_HEADS))
    return pl.pallas_call(
        body, name="conv_bwd", grid=(nblk,), in_specs=[col, wsp, dsp], out_specs=[col, wsp],
        out_shape=[jax.ShapeDtypeStruct((s, 3 * A_WIDTH), F32), jax.ShapeDtypeStruct((CONV_K, 3 * A_WIDTH), F32)],
        compiler_params=_params(("parallel",)),
    )(proj, conv_w, d_c)


A_FWD_HEADS = 4
A_BWD_HEADS = 4


def _neumann_inverse(a):
    c = a.shape[-1]
    eye = (lax.broadcasted_iota(jnp.int32, (c, c), 0) == lax.broadcasted_iota(jnp.int32, (c, c), 1)).astype(F32)
    tinv = eye + a
    p = a
    for _ in range(5):
        p = _mm(p, p)
        tinv = tinv + _mm(tinv, p)
    return tinv


@jax.custom_vjp
def _unit_inverse(a):
    return _neumann_inverse(a)


def _unit_inverse_fwd(a):
    tinv = _neumann_inverse(a)
    return tinv, tinv


def _unit_inverse_bwd(tinv, ct):
    return (_bdot(_bdot(tinv, ct, _TN), tinv, _NT),)


_unit_inverse.defvjp(_unit_inverse_fwd, _unit_inverse_bwd)


def _a_chain(st, cq, ck, cv, alpha, beta_raw, a_log, dt_b, incl, strict, last):
    c = CHUNK
    gb = -jnp.exp(a_log) * _softplus(alpha + dt_b)
    bb = jax.nn.sigmoid(beta_raw)
    q = _l2(_silu(cq)) * (A_DIM ** -0.5)
    k = _l2(_silu(ck))
    v = _silu(cv)

    gc = _dot(incl, jnp.broadcast_to(gb, (c, LANE)))
    tot = jnp.sum(gc * last, axis=0, keepdims=True)
    m1 = gc[:, :c]
    decay = incl * jnp.exp(incl * (m1 - m1.T))
    kb = k * bb
    vb = v * bb
    a = -(strict * decay * _mm_nt(kb, k))
    tinv = _unit_inverse(a)
    eg = jnp.exp(gc)
    u = _mm(tinv, vb)
    w = _mm(tinv, kb * eg)
    qk = _mm_nt(q, k) * decay
    v_new = u - _mm(w, st)
    o = _mm(q * eg, st) + _mm(qk, v_new)
    st_new = st * jnp.exp(tot) + _mm_tn(k * jnp.exp(tot - gc), v_new)
    return st_new, o


def _a_step(sts, cq, ck, cv, gts, pa, h0):
    c = CHUNK
    lane = lax.broadcasted_iota(jnp.int32, (1, LANE), 1)
    ii = lax.broadcasted_iota(jnp.int32, (c, c), 0)
    jj = lax.broadcasted_iota(jnp.int32, (c, c), 1)
    row = lax.broadcasted_iota(jnp.int32, (c, 1), 0)

    def pick(t, col):
        return jnp.sum(jnp.where(lane == col, t, 0.0), axis=1, keepdims=True)

    alpha, beta_raw, a_log, dt_b, incl, strict, last = [], [], [], [], [], [], []
    for b in range(sts.shape[0]):
        h, rev = h0 + b // 2, b % 2
        alpha.append(pick(gts[b], h + 8 * rev))
        beta_raw.append(pick(gts[b], h + 16 + 8 * rev))
        a_log.append(pick(pa[rev:rev + 1, :], h))
        dt_b.append(pick(pa[2 + rev:3 + rev, :], h))
        incl.append(((ii <= jj) if rev else (ii >= jj)).astype(F32))
        strict.append(((ii < jj) if rev else (ii > jj)).astype(F32))
        last.append((row == (0 if rev else c - 1)).astype(F32))
    stack = lambda ts: jnp.concatenate([t[None] for t in ts], axis=0)
    return jax.vmap(_a_chain)(sts, cq, ck, cv, stack(alpha), stack(beta_raw), stack(a_log), stack(dt_b),
                              stack(incl), stack(strict), stack(last))


def _a_final(o, za, pa):
    outs = []
    for j in range(o.shape[1] // A_DIM):
        ln = slice(j * A_DIM, (j + 1) * A_DIM)
        outs.append(_rms(o[:, ln], pa[4:5, :]) * _silu(za[:, ln]))
    return jnp.concatenate(outs, axis=1)


def _a_tiles(n, nchunk, heads):
    tiles = []
    for b in range(2 * heads):
        i = (nchunk - 1 - n) if b % 2 else n
        tiles.append((i, pl.ds(pl.multiple_of(i * CHUNK, CHUNK), CHUNK), slice((b // 2) * A_DIM, (b // 2 + 1) * A_DIM)))
    return tiles


def _a_load(tiles, c_ref, gt_ref):
    cq, ck, cv = (jnp.stack([c_ref[r, sl, ln] for _, sl, ln in tiles], axis=0) for r in range(3))
    return cq, ck, cv, jnp.stack([gt_ref[sl, :] for _, sl, _ in tiles], axis=0)


def _loop_by_two(n, step, init):
    assert n % 2 == 0
    return lax.fori_loop(0, n // 2, lambda m, carry: step(2 * m + 1, step(2 * m, carry, 0), 1), init)


def _a_scan(h0, heads, nchunk, c_ref, gt_ref, pa, of_ref, ob_ref, s_ref):
    def step(n, sts, parity):
        tiles = _a_tiles(n, nchunk, heads)
        sts_new, o = _a_step(sts, *_a_load(tiles, c_ref, gt_ref), pa, h0)
        for b, (i, sl, ln) in enumerate(tiles):
            s_ref[b, i] = sts[b]
            (ob_ref if b % 2 else of_ref)[sl, ln] = o[b]
        return sts_new

    _loop_by_two(nchunk, step, jnp.zeros((2 * heads, A_DIM, A_DIM), F32))


def _a_specs(s, heads):
    wide = heads * A_DIM
    once = pl.Buffered(1)
    trio = pl.BlockSpec((3, s, wide), lambda g: (0, 0, g), pipeline_mode=once)
    gates = pl.BlockSpec((s, LANE), lambda g: (0, P_GT // LANE))
    small = pl.BlockSpec((8, LANE), lambda g: (0, 0))

    def cols(base):
        return pl.BlockSpec((s, wide), lambda g: (0, base // wide + g), pipeline_mode=once)

    state = pl.BlockSpec((2 * heads, s // CHUNK, A_DIM, A_DIM), lambda g: (g, 0, 0, 0), pipeline_mode=once)
    return wide, trio, gates, small, cols, state


def _delta_fwd(cqkv, proj, pa, ride=None):
    s = cqkv.shape[1]
    nchunk = s // CHUNK
    heads = A_FWD_HEADS
    steps = A_HEADS // heads
    wide, trio, gates, small, cols, state = _a_specs(s, heads)
    n_in = len(ride.operands) if ride else 0
    n_out = len(ride.out_shapes) if ride else 0

    def body(*refs):
        c_ref, gt_ref, za_ref, pa_ref = refs[:4]
        out_ref, o_ref, s_ref = refs[4 + n_in:7 + n_in]
        ob_ref = refs[7 + n_in + n_out]
        riders = (refs[4:4 + n_in], refs[7 + n_in:7 + n_in + n_out], refs[8 + n_in + n_out:])
        g = pl.program_id(0)
        if ride:
            pl.when(g == 0)(lambda: ride.start(*riders))
            pl.when(g == steps - 1)(lambda: ride.middle(*riders))
        h0 = g * heads
        pa_v = pa_ref[...]
        _a_scan(h0, heads, nchunk, c_ref, gt_ref, pa_v, o_ref, ob_ref, s_ref)
        o_ref[...] += ob_ref[...]
        out_ref[...] = _a_final(o_ref[...], za_ref[...], pa_v).astype(BF16)
        if ride:
            pl.when(g == steps - 1)(lambda: ride.finish(*riders))

    assert steps > 1
    return pl.pallas_call(
        body, name="delta_fwd", grid=(steps,),
        in_specs=[trio, gates, cols(P_ZA), small] + [HBM] * n_in, out_specs=[cols(0), cols(0), state] + [HBM] * n_out,
        out_shape=[jax.ShapeDtypeStruct((s, D_MODEL), BF16),
                   jax.ShapeDtypeStruct((s, A_WIDTH), F32),
                   jax.ShapeDtypeStruct((2 * A_HEADS, nchunk, A_DIM, A_DIM), F32)]
        + (list(ride.out_shapes) if ride else []),
        scratch_shapes=[pltpu.VMEM((s, wide), F32)] + (list(ride.scratch_shapes) if ride else []),
        compiler_params=_params(("arbitrary",)),
    )(cqkv, proj, proj, pa, *(ride.operands if ride else []))


def _delta_out_bwd(o_sum, proj, pa, d_mixed, tr=256):
    s = o_sum.shape[0]

    def body(o_ref, za_ref, pa_ref, dm_ref, do_ref, dza_ref, dpa_ref):
        @pl.when(pl.program_id(0) == 0)
        def _():
            dpa_ref[...] = jnp.zeros_like(dpa_ref)

        _, vjp = jax.vjp(_a_final, o_ref[...], za_ref[...], pa_ref[...])
        d_o, d_za, dpa = vjp(dm_ref[...].astype(F32))
        do_ref[...] = d_o
        dza_ref[...] = d_za
        dpa_ref[...] += dpa

    def rows(col):
        return pl.BlockSpec((tr, A_WIDTH), lambda i: (i, col))

    small = pl.BlockSpec((8, LANE), lambda i: (0, 0))
    return pl.pallas_call(
        body, name="delta_out_bwd", grid=(s // tr,), in_specs=[rows(0), rows(P_ZA // A_WIDTH), small, rows(0)],
        out_specs=[rows(0), rows(0), small],
        out_shape=[jax.ShapeDtypeStruct((s, A_WIDTH), F32), jax.ShapeDtypeStruct((s, A_WIDTH), F32),
                   jax.ShapeDtypeStruct((8, LANE), F32)],
        compiler_params=_params(("arbitrary",)),
    )(o_sum, proj, pa, d_mixed)


def _delta_bwd(cqkv, proj, pa, d_o, states, ride=None):
    s = cqkv.shape[1]
    nchunk = s // CHUNK
    heads = A_BWD_HEADS
    steps = A_HEADS // heads
    wide, trio, gates, small, cols, state = _a_specs(s, heads)
    n_in = len(ride.operands) if ride else 0
    n_out = len(ride.out_shapes) if ride else 0

    def body(*refs):
        c_ref, gt_ref, pa_ref, do_ref, s_hbm = refs[:5]
        dc_ref, dgt_ref, dpa_ref = refs[5 + n_in:8 + n_in]
        s_buf, s_sems = refs[8 + n_in + n_out:10 + n_in + n_out]
        riders = (refs[5:5 + n_in], refs[8 + n_in:8 + n_in + n_out], refs[10 + n_in + n_out:])
        if ride:
            pl.when(pl.program_id(0) == 0)(lambda: ride.start(*riders))
        h0 = pl.program_id(0) * heads
        pa_v = pa_ref[...]

        @pl.when(h0 == 0)
        def _():
            dgt_ref[...] = jnp.zeros_like(dgt_ref)
            dpa_ref[...] = jnp.zeros_like(dpa_ref)

        dc_ref[...] = jnp.zeros_like(dc_ref)

        def state_copies(n, slot):
            return [pltpu.make_async_copy(s_hbm.at[2 * h0 + b, i], s_buf.at[slot, b], s_sems.at[slot, b])
                    for b, (i, _, _) in enumerate(_a_tiles(nchunk - 1 - n, nchunk, heads))]

        for cp in state_copies(0, 0):
            cp.start()

        def step(n, carry, parity):
            d_sts, dpa = carry
            tiles = _a_tiles(nchunk - 1 - n, nchunk, heads)
            for cp in state_copies(n, parity):
                cp.wait()

            @pl.when(n + 1 < nchunk)
            def _():
                for cp in state_copies(n + 1, 1 - parity):
                    cp.start()

            sts = s_buf[parity]
            d_o_t = jnp.stack([do_ref[sl, ln] for _, sl, ln in tiles], axis=0)
            _, vjp_c = jax.vjp(lambda *a: _a_step(*a, h0), sts, *_a_load(tiles, c_ref, gt_ref), pa_v)
            d_prev, dcq, dck, dcv, dgts, dpa_i = vjp_c((d_sts, d_o_t))
            for b, (_, sl, ln) in enumerate(tiles):
                for r, dc in enumerate((dcq, dck, dcv)):
                    dc_ref[r, sl, ln] += dc[b]
                dgt_ref[sl, :] += dgts[b]
            return d_prev, dpa + dpa_i

        init = (jnp.zeros((2 * heads, A_DIM, A_DIM), F32), jnp.zeros((8, LANE), F32))
        _, dpa_out = lax.fori_loop(0, nchunk, lambda n, carry: step(n, carry, n % 2), init)
        dpa_ref[...] += dpa_out
        if ride:
            pl.when(pl.program_id(0) == steps - 1)(lambda: ride.finish(*riders))

    fixed = pl.BlockSpec((s, LANE), lambda g: (0, 0))
    return pl.pallas_call(
        body, name="delta_bwd", grid=(steps,),
        in_specs=[trio, gates, small, cols(0), pl.BlockSpec(memory_space=pl.ANY)] + [HBM] * n_in,
        out_specs=[trio, fixed, small] + [HBM] * n_out,
        out_shape=[jax.ShapeDtypeStruct((3, s, A_WIDTH), F32), jax.ShapeDtypeStruct((s, LANE), F32),
                   jax.ShapeDtypeStruct((8, LANE), F32)] + (list(ride.out_shapes) if ride else []),
        scratch_shapes=[pltpu.VMEM((2, 2 * heads, A_DIM, A_DIM), F32), pltpu.SemaphoreType.DMA((2, 2 * heads))]
        + (list(ride.scratch_shapes) if ride else []),
        compiler_params=_params(("arbitrary",)),
    )(cqkv, proj, pa, d_o, states, *(ride.operands if ride else []))


def _rope_tables(s):
    inv = ROPE_THETA ** (-jnp.arange(0, B_DIM, 2, dtype=F32) / B_DIM)
    ang = jnp.arange(s, dtype=F32)[:, None] * inv[None, :]
    cos, sin = jnp.cos(ang), jnp.sin(ang)
    return jnp.concatenate([cos, cos], axis=1), jnp.concatenate([-sin, sin], axis=1)


def _b_block(q_t, z_t, k3, v3, cos_q, sin_q, cos_k, sin_k, pb, n, nb):
    w = WINDOW
    def swap(t):
        return jnp.concatenate([t[:, B_DIM // 2:], t[:, :B_DIM // 2]], axis=1)

    grp = B_HEADS // B_KV
    qi = lax.broadcasted_iota(jnp.int32, (grp * w, 3 * w), 0) & (w - 1)
    kj = lax.broadcasted_iota(jnp.int32, (grp * w, 3 * w), 1)
    kpos = kj + (n - 1) * w
    mask = (jnp.abs(kj - w - qi) <= w) & (kpos >= 0) & (kpos < nb * w)
    lane = lax.broadcasted_iota(jnp.int32, (1, LANE), 1)
    qn, kn = pb[0:1, :B_DIM], pb[1:2, :B_DIM]
    cos_g = jnp.concatenate([cos_q] * grp, axis=0)
    sin_g = jnp.concatenate([sin_q] * grp, axis=0)
    def group(q, k, v, sink):
        k = _rms(k, kn)
        k = k * cos_k + swap(k) * sin_k
        q = _rms(q, qn)
        q = q * cos_g + swap(q) * sin_g
        s = _mm_nt(q, k) * (B_DIM ** -0.5)
        s = jnp.where(mask, s, -jnp.inf)
        m = jnp.maximum(jnp.max(s, axis=1, keepdims=True), sink)
        p = jnp.exp(s - m)
        p = p / (jnp.sum(p, axis=1, keepdims=True) + jnp.exp(sink - m))
        return _mm(p, v)

    stack = lambda ts: jnp.concatenate([t[None] for t in ts], axis=0)
    qs, ks, vs, sinks = [], [], [], []
    for hk in range(B_KV):
        heads = [hk * grp + g for g in range(grp)]
        ks.append(k3[:, hk * B_DIM:(hk + 1) * B_DIM])
        vs.append(v3[:, hk * B_DIM:(hk + 1) * B_DIM])
        qs.append(jnp.concatenate([q_t[:, hq * B_DIM:(hq + 1) * B_DIM] for hq in heads], axis=0))
        sinks.append(jnp.concatenate(
            [jnp.broadcast_to(jnp.sum(jnp.where(lane == hq, pb[2:3, :], 0.0), axis=1, keepdims=True), (w, 1))
             for hq in heads], axis=0))
    o = jax.vmap(group)(stack(qs), stack(ks), stack(vs), stack(sinks))
    outs = [o[hk, g * w:(g + 1) * w, :] for hk in range(B_KV) for g in range(grp)]
    return jnp.concatenate(outs, axis=1) * _silu(z_t)


def _b_specs(s):
    nb = s // WINDOW
    qsp = pl.BlockSpec((WINDOW, 512), lambda n: (n, P_QB // 512))
    zsp = pl.BlockSpec((WINDOW, 512), lambda n: (n, P_ZB // 512))

    def three(col, width):
        return [pl.BlockSpec((WINDOW, width), lambda n: (jnp.maximum(n - 1, 0), col)),
                pl.BlockSpec((WINDOW, width), lambda n: (n, col)),
                pl.BlockSpec((WINDOW, width), lambda n: (jnp.minimum(n + 1, nb - 1), col))]

    tab = pl.BlockSpec((WINDOW, B_DIM), lambda n: (n, 0))
    small = pl.BlockSpec((8, LANE), lambda n: (0, 0))
    specs = [qsp, zsp] + three(P_KB // LANE, LANE) + three(P_VB // LANE, LANE) + [tab, tab] + three(0, B_DIM) + three(0, B_DIM) + [small]
    return nb, specs


def _b_args(proj, cos2, sin2, pb):
    return (proj, proj, proj, proj, proj, proj, proj, proj, cos2, sin2, cos2, cos2, cos2, sin2, sin2, sin2, pb)


def _b_load(refs):
    (q_ref, z_ref, kp, kc, kx, vp, vc, vx, cq, sq, ckp, ckc, ckx, skp, skc, skx, pb_ref) = refs
    cat = lambda *r: jnp.concatenate([t[...] for t in r], axis=0)
    return (q_ref[...], z_ref[...], cat(kp, kc, kx), cat(vp, vc, vx), cq[...], sq[...], cat(ckp, ckc, ckx),
            cat(skp, skc, skx), pb_ref[...])


def _attn_b_fwd(proj, cos2, sin2, pb, mixed):
    s = proj.shape[0]
    nb, specs = _b_specs(s)

    def body(*refs):
        o_ref = refs[-1]
        args = _b_load(refs[:-2])
        o_ref[...] = _b_block(*args, pl.program_id(0), nb).astype(BF16)

    return pl.pallas_call(
        body, name="attn_b_fwd", grid=(nb,), in_specs=specs + [pl.BlockSpec(memory_space=pl.ANY)],
        out_specs=pl.BlockSpec((WINDOW, 512), lambda n: (n, A_WIDTH // 512)),
        out_shape=jax.ShapeDtypeStruct(mixed.shape, mixed.dtype), input_output_aliases={len(specs): 0},
        compiler_params=_params(("parallel",)),
    )(*_b_args(proj, cos2, sin2, pb), mixed)


def _attn_b_bwd(proj, cos2, sin2, pb, d_mixed):
    s = proj.shape[0]
    nb, specs = _b_specs(s)
    w = WINDOW

    def body(*refs):
        dm_ref, dq_ref, dz_ref, dk_ref, dv_ref, dpb_ref = refs[-6:]
        n = pl.program_id(0)
        q_t, z_t, k3, v3, cq, sq, ck, sk, pb_v = _b_load(refs[:-6])

        @pl.when(n == 0)
        def _():
            dk_ref[...] = jnp.zeros_like(dk_ref)
            dv_ref[...] = jnp.zeros_like(dv_ref)
            dpb_ref[...] = jnp.zeros_like(dpb_ref)

        def f(q_, z_, k_, v_, pb_):
            return _b_block(q_, z_, k_, v_, cq, sq, ck, sk, pb_, n, nb)

        _, vjp = jax.vjp(f, q_t, z_t, k3, v3, pb_v)
        dq, dz, dk3, dv3, dpb = vjp(dm_ref[...])
        dq_ref[...] = dq
        dz_ref[...] = dz
        dpb_ref[...] += dpb

        def add(j, cond):
            @pl.when(cond)
            def _():
                rows = pl.ds(pl.multiple_of((n - 1 + j) * w, w), w)
                dk_ref[rows, :] += dk3[j * w:(j + 1) * w, :]
                dv_ref[rows, :] += dv3[j * w:(j + 1) * w, :]

        add(0, n > 0)
        add(1, n >= 0)
        add(2, n < nb - 1)

    blk = pl.BlockSpec((w, 512), lambda n: (n, 0))
    whole = pl.BlockSpec((s, LANE), lambda n: (0, 0))
    small = pl.BlockSpec((8, LANE), lambda n: (0, 0))
    return pl.pallas_call(
        body, name="attn_b_bwd", grid=(nb,),
        in_specs=specs + [pl.BlockSpec((w, 512), lambda n: (n, 2))],
        out_specs=[blk, blk, whole, whole, small],
        out_shape=[jax.ShapeDtypeStruct((s, 512), F32), jax.ShapeDtypeStruct((s, 512), F32),
                   jax.ShapeDtypeStruct((s, LANE), F32), jax.ShapeDtypeStruct((s, LANE), F32),
                   jax.ShapeDtypeStruct((8, LANE), F32)],
        compiler_params=_params(("arbitrary",)),
    )(*_b_args(proj, cos2, sin2, pb), d_mixed)


def _mem_kv_fwd(mem, mem_norm_w, w_kv):
    def body(mem_ref, nw_ref, w_ref, kv_ref):
        mn = _rms(mem_ref[...], nw_ref[...]).astype(BF16)
        kv_ref[...] = jnp.dot(mn, w_ref[...], preferred_element_type=F32)

    return pl.pallas_call(
        body, name="mem_kv_fwd", out_shape=jax.ShapeDtypeStruct((MEM_LEN, 2 * C_HEADS * C_DIM), F32),
        compiler_params=_params(),
    )(mem, mem_norm_w, w_kv)


def _mem_kv_bwd(mem, mem_norm_w, w_kv, d_kv):
    def body(mem_ref, nw_ref, w_ref, g_ref, gw_ref, gn_ref):
        mn, vjp = jax.vjp(_rms, mem_ref[...], nw_ref[...])
        g = g_ref[...].astype(BF16)
        gw_ref[...] = lax.dot_general(mn.astype(BF16), g, (((0,), (0,)), ((), ())), preferred_element_type=F32)
        d_mn = lax.dot_general(g, w_ref[...], (((1,), (1,)), ((), ())), preferred_element_type=F32)
        gn_ref[...] = vjp(d_mn)[1]

    return pl.pallas_call(
        body, name="mem_kv_bwd",
        out_shape=[jax.ShapeDtypeStruct((D_MODEL, 2 * C_HEADS * C_DIM), F32), jax.ShapeDtypeStruct((1, D_MODEL), F32)],
        compiler_params=_params(),
    )(mem, mem_norm_w, w_kv, d_kv)


def _c_tile(q_t, z_t, kvm, pc):
    width = C_HEADS * C_DIM
    outs = []
    for h in range(C_HEADS):
        q = _rms(q_t[:, h * C_DIM:(h + 1) * C_DIM], pc[0:1, :])
        k = _rms(kvm[:, h * C_DIM:(h + 1) * C_DIM], pc[1:2, :])
        v = kvm[:, width + h * C_DIM:width + (h + 1) * C_DIM]
        s = _mm_nt(q, k) * (C_DIM ** -0.5)
        p = jnp.exp(s - jnp.max(s, axis=1, keepdims=True))
        p = p / jnp.sum(p, axis=1, keepdims=True)
        outs.append(_mm(p, v))
    return jnp.concatenate(outs, axis=1) * _silu(z_t)


def _attn_c_fwd(proj, kvm, pc, mixed, tq=256):
    s = proj.shape[0]

    def body(q_ref, z_ref, kv_ref, pc_ref, mixed_ref, o_ref):
        o_ref[...] = _c_tile(q_ref[...], z_ref[...], kv_ref[...], pc_ref[...]).astype(BF16)

    return pl.pallas_call(
        body, name="attn_c_fwd", grid=(s // tq,),
        in_specs=[pl.BlockSpec((tq, 512), lambda i: (i, P_QC // 512)), pl.BlockSpec((tq, 512), lambda i: (i, P_ZC // 512)),
                  pl.BlockSpec(kvm.shape, lambda i: (0, 0)), pl.BlockSpec((8, LANE), lambda i: (0, 0)),
                  pl.BlockSpec(memory_space=pl.ANY)],
        out_specs=pl.BlockSpec((tq, 512), lambda i: (i, (A_WIDTH + 512) // 512)),
        out_shape=jax.ShapeDtypeStruct(mixed.shape, mixed.dtype), input_output_aliases={4: 0},
        compiler_params=_params(("parallel",)),
    )(proj, proj, kvm, pc, mixed)


def _attn_c_bwd(proj, kvm, pc, d_mixed, tq=256):
    s = proj.shape[0]

    def body(q_ref, z_ref, kv_ref, pc_ref, dm_ref, dq_ref, dz_ref, dkv_ref, dpc_ref):
        @pl.when(pl.program_id(0) == 0)
        def _():
            dkv_ref[...] = jnp.zeros_like(dkv_ref)
            dpc_ref[...] = jnp.zeros_like(dpc_ref)

        _, vjp = jax.vjp(_c_tile, q_ref[...], z_ref[...], kv_ref[...], pc_ref[...])
        dq, dz, dkv, dpc = vjp(dm_ref[...])
        dq_ref[...] = dq
        dz_ref[...] = dz
        dkv_ref[...] += dkv
        dpc_ref[...] += dpc

    blk = pl.BlockSpec((tq, 512), lambda i: (i, 0))
    kvs = pl.BlockSpec(kvm.shape, lambda i: (0, 0))
    small = pl.BlockSpec((8, LANE), lambda i: (0, 0))
    return pl.pallas_call(
        body, name="attn_c_bwd", grid=(s // tq,),
        in_specs=[pl.BlockSpec((tq, 512), lambda i: (i, P_QC // 512)), pl.BlockSpec((tq, 512), lambda i: (i, P_ZC // 512)),
                  kvs, small, pl.BlockSpec((tq, 512), lambda i: (i, 3))],
        out_specs=[blk, blk, kvs, small],
        out_shape=[jax.ShapeDtypeStruct((s, 512), F32), jax.ShapeDtypeStruct((s, 512), F32),
                   jax.ShapeDtypeStruct(kvm.shape, F32), jax.ShapeDtypeStruct((8, LANE), F32)],
        compiler_params=_params(("arbitrary",)),
    )(proj, proj, kvm, pc, d_mixed)


def _pad_row(v, width=LANE):
    v = v.reshape(1, -1)
    return jnp.pad(v, ((0, 0), (0, width - v.shape[1])))


def _local_step(x, mem, target, norm_w, w_perm_t, conv_w, pa, pb, pc, mem_norm_w, w_kv, w_out, gather=None,
                exchange=None):
    s = x.shape[0]
    cos2, sin2 = _rope_tables(s)
    hn = _rms_fwd(x, norm_w)
    wide = dict(tm=1024, tn=512, tk=2048)
    proj = _matmul(hn, w_perm_t, "nt", F32, "mm_proj", **wide)
    cqkv = _conv_fwd(proj, conv_w)
    if gather is None:
        mixed, o_sum, states = _delta_fwd(cqkv, proj, pa)
    else:
        mixed, o_sum, states, *arrived = _delta_fwd(cqkv, proj, pa, gather[0])
        w_out, w_kv = gather[1](*arrived)
    mixed = _attn_b_fwd(proj, cos2, sin2, pb, mixed)
    kvm = _mem_kv_fwd(mem, mem_norm_w, w_kv)
    mixed = _attn_c_fwd(proj, kvm, pc, mixed)
    mo = _matmul(mixed, w_out, "nn", F32, "mm_out", **wide)
    dy, dyb, loss_parts = _loss_dy(x, mo, target)

    d_mixed = _matmul(dyb, w_out, "nt", F32, "mm_dmixed", **wide)
    g_w_out = _matmul(mixed, dyb, "tn", F32, "mm_gwout", **wide)
    d_qc, d_zc, d_kvm, d_pc = _attn_c_bwd(proj, kvm, pc, d_mixed)
    g_w_kv, g_mem_norm = _mem_kv_bwd(mem, mem_norm_w, w_kv, d_kvm)
    d_qb, d_zb, d_kb, d_vb, d_pb = _attn_b_bwd(proj, cos2, sin2, pb, d_mixed)
    d_o, d_za, d_pa_out = _delta_out_bwd(o_sum, proj, pa, d_mixed)
    early = exchange[0](g_w_out, g_w_kv) if exchange else None
    d_c, d_gt, d_pa_scan, *landed_early = _delta_bwd(cqkv, proj, pa, d_o, states, early)
    d_pa = d_pa_out + d_pa_scan
    d_qkv, g_conv = _conv_bwd(proj, conv_w, d_c)
    d_proj = jnp.concatenate([d_qkv, d_za, d_qb, d_zb, d_qc, d_zc, d_kb, d_vb, d_gt,
                              jnp.zeros((s, P_WIDTH - P_GT - LANE), F32)], axis=1).astype(BF16)
    g_w_perm_t = _matmul(d_proj, hn, "tn", F32, "mm_gwin", tm=512, tn=1024, tk=2048)
    late = exchange[1](g_w_perm_t) if exchange else None
    d_hn = _matmul(d_proj, w_perm_t, "nn", F32, "mm_dhn", tm=1024, tn=2048, tk=512, ride=late)
    d_hn, landed_late = (d_hn[0], list(d_hn[1:])) if late else (d_hn, [])
    g_x, g_norm = _rms_bwd(x, norm_w, d_hn, dy)
    return dict(loss_parts=loss_parts, g_x=g_x, g_norm=g_norm, g_w_perm_t=g_w_perm_t, g_conv=g_conv, d_pa=d_pa,
                d_pb=d_pb, d_pc=d_pc, g_mem_norm=g_mem_norm, g_w_kv=g_w_kv, g_w_out=g_w_out,
                landed=landed_late + landed_early)


_SEGMENTS = ((0, O_GT, 0), (O_GT, O_QB, P_GT), (O_QB, O_KB, P_QB), (O_KB, O_VB, P_KB), (O_VB, O_ZB, P_VB),
             (O_ZB, O_QC, P_ZB), (O_QC, O_ZC, P_QC), (O_ZC, IN_WIDTH, P_ZC))


def _permute_blocks(w4):
    parts = []
    for first, end, _ in sorted(_SEGMENTS, key=lambda seg: seg[2]):
        row = first
        while row < end:
            k = row // W_IN_BLOCK
            stop = min(end, (k + 1) * W_IN_BLOCK)
            parts.append(w4[k][row - k * W_IN_BLOCK:stop - k * W_IN_BLOCK, :])
            row = stop
    parts.append(jnp.zeros((P_WIDTH - IN_WIDTH, w4.shape[2]), w4.dtype))
    return jnp.concatenate(parts, axis=0)


def _unpermute_blocks(g):
    blocks = []
    for k in range(N_CHIPS):
        lo, hi = k * W_IN_BLOCK, (k + 1) * W_IN_BLOCK
        parts = [g[p + max(first, lo) - first:p + min(end, hi) - first, :]
                 for first, end, p in _SEGMENTS if max(first, lo) < min(end, hi)]
        blocks.append(jnp.concatenate(parts, axis=0))
    return jnp.stack(blocks, axis=0)


HBM = pl.BlockSpec(memory_space=pltpu.HBM)


def _place():
    x, y, c = lax.axis_index("x"), lax.axis_index("y"), lax.axis_index("c")
    chips = [(1 - x, y), (x, 1 - y), (1 - x, 1 - y)]
    return x, y, c, 2 * x + y, chips, [2 * cx + cy for cx, cy in chips]


PIECE_ROWS_CAP = 600


def _remote(src, dst, send_sems, recv_sems, k, to):
    return pltpu.make_async_remote_copy(src_ref=src, dst_ref=dst, send_sem=send_sems.at[k], recv_sem=recv_sems.at[k],
                                        device_id=to, device_id_type=MESH)


def _half_cols(ref, c):
    half = ref.shape[-1] // 2
    return pl.ds(pl.multiple_of(c * half, LANE), half)


class _PairedGather:
    def __init__(self, blocks):
        n = len(blocks)
        self.operands = list(blocks)
        self.out_shapes = [jax.ShapeDtypeStruct((N_CHIPS,) + b.shape, b.dtype) for b in blocks]
        self.scratch_shapes = [pltpu.SemaphoreType.DMA((6 * n,)), pltpu.SemaphoreType.DMA((6 * n,))]

    @staticmethod
    def _copies(srcs, dsts, sems):
        x, y, c, me, chips, chip_ids = _place()
        sends, landed, passes, passed = [], [], [], []
        for a, (src, dst) in enumerate(zip(srcs, dsts)):
            mine, other = _half_cols(src, c), _half_cols(src, 1 - c)
            for j, (chip, cid) in enumerate(zip(chips, chip_ids)):
                sends.append(_remote(src.at[:, mine], dst.at[me, :, mine], sems[0], sems[1], 6 * a + j, (*chip, c)))
                here = dst.at[cid, :, mine]
                landed.append(_remote(here, here, sems[0], sems[1], 6 * a + j, (x, y, 1 - c)))
                passes.append(_remote(here, here, sems[0], sems[1], 6 * a + 3 + j, (x, y, 1 - c)))
                there = dst.at[cid, :, other]
                passed.append(_remote(there, there, sems[0], sems[1], 6 * a + 3 + j, (x, y, 1 - c)))
        return sends, landed, passes, passed

    def start(self, srcs, dsts, sems):
        for cp in self._copies(srcs, dsts, sems)[0]:
            cp.start()

    def middle(self, srcs, dsts, sems):
        _, landed, passes, _ = self._copies(srcs, dsts, sems)
        for arrived, onward in zip(landed, passes):
            arrived.wait_recv()
            onward.start()

    def finish(self, srcs, dsts, sems):
        sends, _, passes, passed = self._copies(srcs, dsts, sems)
        for cp in passed:
            cp.wait_recv()
        for cp in sends + passes:
            cp.wait_send()


def _all_gather_weights(bigs, conv_b):
    bigs = tuple(bigs)
    n_big = len(bigs)

    def body(*refs):
        srcs, conv_src = refs[:n_big], refs[n_big]
        dsts, conv_dst = refs[n_big + 1:2 * n_big + 1], refs[2 * n_big + 1]
        send_sems, recv_sems, local_sems = refs[2 * n_big + 2:]
        x, y, c, me, chips, chip_ids = _place()
        sibling = (x, y, 1 - c)
        local = [pltpu.make_async_copy(src, dst.at[me], local_sems.at[a]) for a, (src, dst) in enumerate(zip(srcs, dsts))]
        local.append(pltpu.make_async_copy(conv_src, conv_dst.at[me], local_sems.at[n_big]))
        for cp in local:
            cp.start()
        sends = []
        for a, (src, dst) in enumerate(zip(srcs, dsts)):
            mine = _half_cols(src, c)
            for j, chip in enumerate(chips):
                sends.append(_remote(src.at[:, mine], dst.at[me, :, mine], send_sems, recv_sems, 6 * a + j, (*chip, c)))
        for j, chip in enumerate(chips):
            sends.append(_remote(conv_src, conv_dst.at[me], send_sems, recv_sems, 6 * n_big + j, (*chip, c)))
        for cp in sends:
            cp.start()
        passed = []
        for a, (src, dst) in enumerate(zip(srcs, dsts)):
            mine = _half_cols(src, c)
            for j, cid in enumerate(chip_ids):
                landed = dst.at[cid, :, mine]
                _remote(landed, landed, send_sems, recv_sems, 6 * a + j, sibling).wait_recv()
                cp = _remote(landed, landed, send_sems, recv_sems, 6 * a + 3 + j, sibling)
                cp.start()
                passed.append(cp)
        for a, (src, dst) in enumerate(zip(srcs, dsts)):
            other = _half_cols(src, 1 - c)
            for j, cid in enumerate(chip_ids):
                landed = dst.at[cid, :, other]
                _remote(landed, landed, send_sems, recv_sems, 6 * a + 3 + j, sibling).wait_recv()
        for j, cid in enumerate(chip_ids):
            _remote(conv_src, conv_dst.at[cid], send_sems, recv_sems, 6 * n_big + j, sibling).wait_recv()
        for cp in sends + passed:
            cp.wait_send()
        for cp in local:
            cp.wait()

    n_sem = 6 * n_big + 3
    return pl.pallas_call(
        body, name="all_gather_weights",
        out_shape=[jax.ShapeDtypeStruct((N_CHIPS,) + w.shape, w.dtype) for w in bigs + (conv_b,)],
        in_specs=[pl.BlockSpec(memory_space=pltpu.VMEM)] * (n_big + 1), out_specs=[HBM] * (n_big + 1),
        scratch_shapes=[pltpu.SemaphoreType.DMA((n_sem,)), pltpu.SemaphoreType.DMA((n_sem,)),
                        pltpu.SemaphoreType.DMA((n_big + 1,))],
        compiler_params=_params(),
    )(*bigs, conv_b)


def _pair_exchange(grads, name):
    n = len(grads)
    pieces = [_row_tile(g.shape[1]) for g in grads]

    def body(*refs):
        srcs, gots = refs[:n], refs[n:2 * n]
        stages = refs[2 * n:3 * n]
        send_sems, recv_sems, load_sems = refs[3 * n:]
        x, y, c, _, _, _ = _place()
        sibling = (x, y, 1 - c)
        for a in range(n):
            slabs, rows, _ = gots[a].shape
            piece = pieces[a]
            per_slab = rows // piece
            theirs = _half_cols(srcs[a], 1 - c)
            loads, sends = [], []
            for i in range(slabs * per_slab):
                k, r, slot = i // per_slab, i % per_slab, i % 2
                part = pl.ds(r * piece, piece)
                loads.append(pltpu.make_async_copy(srcs[a].at[k, part, theirs], stages[a].at[slot], load_sems.at[2 * a + slot]))
                sends.append(pltpu.make_async_remote_copy(
                    src_ref=stages[a].at[slot], dst_ref=gots[a].at[k, part, :],
                    send_sem=send_sems.at[2 * a + slot], recv_sem=recv_sems.at[a], device_id=sibling, device_id_type=MESH))
            loads[0].start()
            for i in range(len(loads)):
                loads[i].wait()
                sends[i].start()
                if i + 1 < len(loads):
                    if i >= 1:
                        sends[i - 1].wait_send()
                    loads[i + 1].start()
            for cp in sends[-2:]:
                cp.wait_send()
        for a in range(n):
            whole = srcs[a].at[:, :, _half_cols(srcs[a], c)]
            pltpu.make_async_remote_copy(src_ref=whole, dst_ref=gots[a], send_sem=send_sems.at[2 * a],
                                         recv_sem=recv_sems.at[a], device_id=sibling, device_id_type=MESH).wait_recv()

    halves = [jax.ShapeDtypeStruct((g.shape[0], g.shape[1], g.shape[2] // 2), g.dtype) for g in grads]
    return pl.pallas_call(
        body, name=name, out_shape=halves, in_specs=[HBM] * n, out_specs=[HBM] * n,
        scratch_shapes=[pltpu.VMEM((2, piece, g.shape[2] // 2), g.dtype) for piece, g in zip(pieces, grads)]
        + [pltpu.SemaphoreType.DMA((2 * n,)), pltpu.SemaphoreType.DMA((n,)), pltpu.SemaphoreType.DMA((2 * n,))],
        compiler_params=_params(),
    )(*grads)


class _ChipExchange:
    def __init__(self, halves):
        n = len(halves)
        self.operands = list(halves)
        self.out_shapes = [jax.ShapeDtypeStruct((N_CHIPS - 1,) + h.shape[1:], h.dtype) for h in halves]
        self.scratch_shapes = [pltpu.SemaphoreType.DMA((3 * n,)), pltpu.SemaphoreType.DMA((3 * n,))]

    @staticmethod
    def _copies(srcs, lands, sems):
        _, _, c, _, chips, chip_ids = _place()
        return [_remote(src.at[cid], land.at[j], sems[0], sems[1], 3 * a + j, (*chip, c))
                for a, (src, land) in enumerate(zip(srcs, lands)) for j, (chip, cid) in enumerate(zip(chips, chip_ids))]

    def start(self, srcs, lands, sems):
        for cp in self._copies(srcs, lands, sems):
            cp.start()

    def finish(self, srcs, lands, sems):
        copies = self._copies(srcs, lands, sems)
        for cp in copies:
            cp.wait_recv()
        for cp in copies:
            cp.wait_send()


def _pair_gather(halves):
    n = len(halves)

    def body(*refs):
        srcs, fulls = refs[:n], refs[n:2 * n]
        send_sems, recv_sems, local_sems = refs[2 * n:]
        x, y, c, _, _, _ = _place()
        copies = []
        for a in range(n):
            mine = _half_cols(fulls[a], c)
            keep = pltpu.make_async_copy(srcs[a], fulls[a].at[:, mine], local_sems.at[a])
            keep.start()
            give = _remote(srcs[a], fulls[a].at[:, mine], send_sems, recv_sems, a, (x, y, 1 - c))
            give.start()
            copies += [keep, give]
        for a in range(n):
            other = _half_cols(fulls[a], 1 - c)
            copies[2 * a].wait()
            copies[2 * a + 1].wait_send()
            _remote(srcs[a], fulls[a].at[:, other], send_sems, recv_sems, a, (x, y, 1 - c)).wait_recv()

    return pl.pallas_call(
        body, name="grad_pair_gather",
        out_shape=[jax.ShapeDtypeStruct((h.shape[0], 2 * h.shape[1]), h.dtype) for h in halves],
        in_specs=[pl.BlockSpec(memory_space=pltpu.VMEM)] * n, out_specs=[HBM] * n,
        scratch_shapes=[pltpu.SemaphoreType.DMA((n,)), pltpu.SemaphoreType.DMA((n,)), pltpu.SemaphoreType.DMA((n,))],
    )(*halves)


def _all_reduce_small(p):
    n_dev = 8

    def body(p_ref, o_ref, land, send_sems, recv_sems):
        x, y, c = lax.axis_index("x"), lax.axis_index("y"), lax.axis_index("c")
        me = 4 * x + 2 * y + c
        land[me] = p_ref[...]
        sends = []
        for k in range(1, n_dev):
            fx, fy, fc = (k >> 2) & 1, (k >> 1) & 1, k & 1
            to = (x ^ fx, y ^ fy, c ^ fc)
            cp = _remote(p_ref, land.at[me], send_sems, recv_sems, k - 1, to)
            cp.start()
            sends.append(cp)
        for k in range(1, n_dev):
            _remote(p_ref, land.at[me ^ k], send_sems, recv_sems, k - 1, (x, y, c)).wait_recv()
        total = land[0]
        for d in range(1, n_dev):
            total = total + land[d]
        o_ref[...] = total
        for cp in sends:
            cp.wait_send()

    vm = pl.BlockSpec(memory_space=pltpu.VMEM)
    return pl.pallas_call(
        body, name="all_reduce_small", out_shape=jax.ShapeDtypeStruct(p.shape, p.dtype), in_specs=[vm], out_specs=vm,
        scratch_shapes=[pltpu.VMEM((n_dev,) + p.shape, p.dtype), pltpu.SemaphoreType.DMA((n_dev - 1,)),
                        pltpu.SemaphoreType.DMA((n_dev - 1,))],
    )(p)


def _row_tile(rows):
    fits = [t for t in range(8, min(rows, PIECE_ROWS_CAP) + 1, 8) if rows % t == 0]
    return max(fits) if fits else rows


def _pair_sum(full, got, core, name):
    n, r, c = got.shape
    tr = _row_tile(r)

    def body(core_ref, a_ref, b_ref, o_ref):
        o_ref[...] = (a_ref[...] + b_ref[...]).astype(BF16)

    blk = pl.BlockSpec((None, tr, c), lambda i, j, core_ref: (i, j, 0))
    grid_spec = pltpu.PrefetchScalarGridSpec(
        num_scalar_prefetch=1, grid=(n, r // tr),
        in_specs=[pl.BlockSpec((None, tr, c), lambda i, j, core_ref: (i, j, core_ref[0])), blk], out_specs=blk)
    return pl.pallas_call(body, name=name, grid_spec=grid_spec, out_shape=jax.ShapeDtypeStruct(got.shape, BF16),
                          compiler_params=_params(("parallel", "parallel")))(core, full, got)


def _chip_sum(full, got, land, place, name):
    n, r, c = land.shape
    tr = _row_tile(r)

    def body(place_ref, a_ref, b_ref, l_ref, o_ref):
        total = a_ref[...] + b_ref[...]
        for j in range(n):
            total = total + l_ref[j].astype(F32)
        o_ref[...] = total

    grid_spec = pltpu.PrefetchScalarGridSpec(
        num_scalar_prefetch=1, grid=(r // tr,),
        in_specs=[pl.BlockSpec((None, tr, c), lambda i, p: (p[0], i, p[1])),
                  pl.BlockSpec((None, tr, c), lambda i, p: (p[0], i, 0)),
                  pl.BlockSpec((n, tr, c), lambda i, p: (0, i, 0))],
        out_specs=pl.BlockSpec((tr, c), lambda i, p: (i, 0)))
    return pl.pallas_call(body, name=name, grid_spec=grid_spec, out_shape=jax.ShapeDtypeStruct((r, c), F32),
                          compiler_params=_params(("parallel",)))(place, full, got, land)


def _adamw(w, g, m, v, name):
    r, c = w.shape
    tr = _row_tile(r)
    tc = 1024 if c % 1024 == 0 else c

    def body(w_ref, g_ref, m_ref, v_ref, d_ref, mo_ref, vo_ref):
        g_ = g_ref[...]
        m2 = ADAM_B1 * m_ref[...] + (1.0 - ADAM_B1) * g_
        v2 = ADAM_B2 * v_ref[...] + (1.0 - ADAM_B2) * jnp.square(g_)
        m_hat = m2 / (1.0 - ADAM_B1 ** ADAM_STEP)
        v_hat = v2 / (1.0 - ADAM_B2 ** ADAM_STEP)
        d_ref[...] = -ADAM_LR * (m_hat / (jnp.sqrt(v_hat) + ADAM_EPS) + ADAM_WD * w_ref[...])
        mo_ref[...] = m2
        vo_ref[...] = v2

    blk = pl.BlockSpec((tr, tc), lambda i, j: (i, j))
    return pl.pallas_call(body, name=name, grid=(r // tr, c // tc), in_specs=[blk] * 4, out_specs=[blk] * 3,
                          out_shape=[jax.ShapeDtypeStruct(w.shape, F32)] * 3,
                          compiler_params=_params(("parallel", "parallel")))(w, g, m, v)


SMALL_NAMES = ("norm_w", "mem_norm_w", "o_norm_a", "q_norm_c", "k_norm_c", "q_norm_b", "k_norm_b",
               "a_log_fwd", "a_log_bwd", "dt_bias_fwd", "dt_bias_bwd", "sink_b")
SMALL_SIZES = (2048, 2048, 128, 128, 128, 64, 64, 8, 8, 8, 8, 8)
SMALL_LOSS = sum(SMALL_SIZES)
SMALL_CONV = 5120
SMALL_TOTAL = SMALL_CONV + CONV_K * 3 * A_WIDTH
SMALL_ROWS = SMALL_TOTAL // LANE


def _pack_small(parts, extra=None, conv=None):
    vec = [parts[n].reshape(-1) for n in SMALL_NAMES]
    vec.append(jnp.zeros((1,), F32) if extra is None else extra.reshape(1))
    vec.append(jnp.zeros((SMALL_CONV - SMALL_LOSS - 1,), F32))
    vec.append(jnp.zeros((SMALL_TOTAL - SMALL_CONV,), F32) if conv is None else conv.reshape(-1))
    return jnp.concatenate(vec).reshape(SMALL_ROWS, LANE)


def _unpack_small(packed):
    flat = packed.reshape(-1)
    out, off = {}, 0
    for n, size in zip(SMALL_NAMES, SMALL_SIZES):
        out[n] = flat[off:off + size].reshape(1, size)
        off += size
    return out


WEIGHT_ORDER = ("norm_w", "w_in", "conv_w_a", "a_log_fwd", "a_log_bwd", "dt_bias_fwd", "dt_bias_bwd", "o_norm_a",
                "q_norm_b", "k_norm_b", "sink_b", "mem_norm_w", "w_mem_kv", "q_norm_c", "k_norm_c", "w_out")


def kernel(x, mem, norm_w, w_in, conv_w_a, a_log_fwd, a_log_bwd, dt_bias_fwd, dt_bias_bwd, o_norm_a, q_norm_b, k_norm_b, sink_b, mem_norm_w, w_mem_kv, q_norm_c, k_norm_c, w_out, loss_target, m_norm_w, m_w_in, m_conv_w_a, m_a_log_fwd, m_a_log_bwd, m_dt_bias_fwd, m_dt_bias_bwd, m_o_norm_a, m_q_norm_b, m_k_norm_b, m_sink_b, m_mem_norm_w, m_w_mem_kv, m_q_norm_c, m_k_norm_c, m_w_out, v_norm_w, v_w_in, v_conv_w_a, v_a_log_fwd, v_a_log_bwd, v_dt_bias_fwd, v_dt_bias_bwd, v_o_norm_a, v_q_norm_b, v_k_norm_b, v_sink_b, v_mem_norm_w, v_w_mem_kv, v_q_norm_c, v_k_norm_c, v_w_out):
    weights = dict(norm_w=norm_w, w_in=w_in, conv_w_a=conv_w_a, a_log_fwd=a_log_fwd, a_log_bwd=a_log_bwd,
                   dt_bias_fwd=dt_bias_fwd, dt_bias_bwd=dt_bias_bwd, o_norm_a=o_norm_a, q_norm_b=q_norm_b,
                   k_norm_b=k_norm_b, sink_b=sink_b, mem_norm_w=mem_norm_w, w_mem_kv=w_mem_kv, q_norm_c=q_norm_c,
                   k_norm_c=k_norm_c, w_out=w_out)
    mom1 = dict(norm_w=m_norm_w, w_in=m_w_in, conv_w_a=m_conv_w_a, a_log_fwd=m_a_log_fwd, a_log_bwd=m_a_log_bwd,
                dt_bias_fwd=m_dt_bias_fwd, dt_bias_bwd=m_dt_bias_bwd, o_norm_a=m_o_norm_a, q_norm_b=m_q_norm_b,
                k_norm_b=m_k_norm_b, sink_b=m_sink_b, mem_norm_w=m_mem_norm_w, w_mem_kv=m_w_mem_kv,
                q_norm_c=m_q_norm_c, k_norm_c=m_k_norm_c, w_out=m_w_out)
    mom2 = dict(norm_w=v_norm_w, w_in=v_w_in, conv_w_a=v_conv_w_a, a_log_fwd=v_a_log_fwd, a_log_bwd=v_a_log_bwd,
                dt_bias_fwd=v_dt_bias_fwd, dt_bias_bwd=v_dt_bias_bwd, o_norm_a=v_o_norm_a, q_norm_b=v_q_norm_b,
                k_norm_b=v_k_norm_b, sink_b=v_sink_b, mem_norm_w=v_mem_norm_w, w_mem_kv=v_w_mem_kv,
                q_norm_c=v_q_norm_c, k_norm_c=v_k_norm_c, w_out=v_w_out)
    chip = 2 * lax.axis_index("x") + lax.axis_index("y")

    w_in4, conv4 = _all_gather_weights([jnp.transpose(w_in[0]).astype(BF16)], conv_w_a[0])
    w_perm_t = _permute_blocks(w_in4)
    conv_full = jnp.transpose(conv4, (1, 0, 2)).reshape(CONV_K, 3 * A_WIDTH)
    own_out, own_kv = w_out[0].astype(BF16), w_mem_kv[0].astype(BF16)

    def assemble(w_out4, w_kv4):
        w_out4 = lax.dynamic_update_index_in_dim(w_out4, own_out, chip, 0)
        w_kv4 = lax.dynamic_update_index_in_dim(w_kv4, own_kv, chip, 0)
        return w_out4.reshape(D_MODEL, D_MODEL), w_kv4.reshape(D_MODEL, 2 * C_HEADS * C_DIM)

    gather = (_PairedGather([own_out, own_kv]), assemble)
    pa = jnp.concatenate([_pad_row(a_log_fwd), _pad_row(a_log_bwd), _pad_row(dt_bias_fwd), _pad_row(dt_bias_bwd),
                          _pad_row(o_norm_a), jnp.zeros((3, LANE), F32)], axis=0)
    pb = jnp.concatenate([_pad_row(q_norm_b), _pad_row(k_norm_b), _pad_row(sink_b), jnp.zeros((5, LANE), F32)], axis=0)
    pc = jnp.concatenate([_pad_row(q_norm_c), _pad_row(k_norm_c), jnp.zeros((6, LANE), F32)], axis=0)

    full, got = {}, {}
    core = lax.axis_index("c").astype(jnp.int32).reshape(1)

    def pair_round(tag, blocks):
        names = [tag + "_%d" % i for i in range(len(blocks))]
        full.update(zip(names, blocks))
        got.update(zip(names, _pair_exchange(blocks, "grad_pair_exchange_" + tag)))
        return _ChipExchange([_pair_sum(full[n], got[n], core, "grad_pair_sum_" + n) for n in names])

    def early(g_w_out, g_w_kv):
        return pair_round("early", [g_w_out.reshape(N_CHIPS, D_MODEL // N_CHIPS, D_MODEL),
                                    g_w_kv.reshape(N_CHIPS, D_MODEL // N_CHIPS, 2 * C_HEADS * C_DIM)])

    def late(g_w_perm_t):
        return pair_round("late", [_unpermute_blocks(g_w_perm_t)])

    r = _local_step(x[0], mem[0], loss_target[0], norm_w, w_perm_t, conv_full, pa, pb, pc, mem_norm_w, None, None,
                    gather, (early, late))
    place = jnp.stack([chip, lax.axis_index("c")]).astype(jnp.int32)
    reduced = [_chip_sum(full[n], got[n], l, place, "grad_chip_sum_" + n)
               for n, l in zip(("late_0", "early_0", "early_1"), r["landed"])]
    g_w_in_t, g_w_out, g_w_kv = _pair_gather(reduced)

    d_pa, d_pb, d_pc = r["d_pa"], r["d_pb"], r["d_pc"]
    small_g = dict(norm_w=r["g_norm"], mem_norm_w=r["g_mem_norm"], o_norm_a=d_pa[4], q_norm_c=d_pc[0], k_norm_c=d_pc[1],
                   q_norm_b=d_pb[0, :B_DIM], k_norm_b=d_pb[1, :B_DIM], a_log_fwd=d_pa[0, :A_HEADS],
                   a_log_bwd=d_pa[1, :A_HEADS], dt_bias_fwd=d_pa[2, :A_HEADS], dt_bias_bwd=d_pa[3, :A_HEADS],
                   sink_b=d_pb[2, :B_HEADS])
    packed = _all_reduce_small(_pack_small(small_g, jnp.sum(r["loss_parts"][:, 0, 0]), r["g_conv"]))
    flat = packed.reshape(-1)
    loss = flat[SMALL_LOSS]
    conv_sum = flat[SMALL_CONV:].reshape(CONV_K, 3 * A_WIDTH)
    conv_cols = 3 * A_WIDTH // N_CHIPS
    g_conv = lax.dynamic_slice(conv_sum, (0, chip * conv_cols), (CONV_K, conv_cols))

    grads = _unpack_small(packed)
    grads.update(w_in=jnp.transpose(g_w_in_t), w_mem_kv=g_w_kv, w_out=g_w_out, conv_w_a=g_conv)
    delta, new_m, new_v = {}, {}, {}
    for n in ("w_mem_kv", "w_out", "conv_w_a"):
        delta[n], new_m[n], new_v[n] = _adamw(weights[n][0], grads[n], mom1[n][0], mom2[n][0], "adamw_" + n)
    stepped = _adamw(jnp.transpose(w_in[0]), g_w_in_t, jnp.transpose(m_w_in[0]), jnp.transpose(v_w_in[0]), "adamw_w_in")
    delta["w_in"], new_m["w_in"], new_v["w_in"] = (jnp.transpose(t) for t in stepped)
    d_s, m_s, v_s = _adamw(_pack_small(weights), packed, _pack_small(mom1), _pack_small(mom2), "adamw_small")
    d_s, m_s, v_s = _unpack_small(d_s), _unpack_small(m_s), _unpack_small(v_s)
    for n in SMALL_NAMES:
        delta[n], new_m[n], new_v[n] = d_s[n], m_s[n], v_s[n]

    def shaped(tree):
        return [tree[n].reshape(weights[n].shape) for n in WEIGHT_ORDER]

    return (loss, r["g_x"].reshape(x.shape), *shaped(grads), *shaped(delta), *shaped(new_m), *shaped(new_v))
```

```python
import functools

import jax
import jax.numpy as jnp
from jax import lax
from jax.experimental import pallas as pl
from jax.experimental.pallas import tpu as pltpu

F32 = jnp.float32
BF16 = jnp.bfloat16
HI = lax.Precision.HIGHEST
MESH = pl.DeviceIdType.MESH

D_MODEL = 2048
A_WIDTH = 1024
A_HEADS = 8
A_DIM = 128
CONV_K = 5
CHUNK = 64
B_HEADS = 8
B_KV = 2
B_DIM = 64
WINDOW = 128
C_HEADS = 4
C_DIM = 128
MEM_LEN = 256
ROPE_THETA = 10000.0
EPS = 1e-6
IN_WIDTH = 6432
N_CHIPS = 4
W_IN_BLOCK = IN_WIDTH // N_CHIPS
W_IN_PAD = 1664

LANE = 128
P_QA, P_KA, P_VA, P_ZA = 0, 1024, 2048, 3072
P_QB, P_ZB, P_QC, P_ZC = 4096, 4608, 5120, 5632
P_KB, P_VB, P_GT = 6144, 6272, 6400
P_WIDTH = 6656
O_GT, O_QB, O_KB, O_VB, O_ZB, O_QC, O_ZC = 4096, 4128, 4640, 4768, 4896, 5408, 5920

ADAM_LR, ADAM_B1, ADAM_B2, ADAM_EPS, ADAM_WD, ADAM_STEP = 0.001, 0.9, 0.999, 1e-08, 0.01, 10

VMEM_LIMIT = 56 * 1024 * 1024


def _params(sem=None):
    return pltpu.CompilerParams(dimension_semantics=sem, vmem_limit_bytes=VMEM_LIMIT)


def _dot(a, b, dims=(((1,), (0,)), ((), ())), precision=HI):
    return lax.dot_general(a, b, dims, precision=precision, preferred_element_type=F32)


def _dot_nt(a, b, precision=HI):
    return _dot(a, b, (((1,), (1,)), ((), ())), precision)


def _dot_tn(a, b, precision=HI):
    return _dot(a, b, (((0,), (0,)), ((), ())), precision)


_NN = (((1,), (0,)), ((), ()))
_NT = (((1,), (1,)), ((), ()))
_TN = (((0,), (0,)), ((), ()))


def _bdot(a, b, dims):
    return lax.dot_general(a.astype(BF16), b.astype(BF16), dims, preferred_element_type=F32)


@jax.custom_vjp
def _mm(a, b):
    return _bdot(a, b, _NN)


_mm.defvjp(lambda a, b: (_bdot(a, b, _NN), (a, b)),
           lambda res, ct: (_bdot(ct, res[1], _NT), _bdot(res[0], ct, _TN)))


@jax.custom_vjp
def _mm_nt(a, b):
    return _bdot(a, b, _NT)


_mm_nt.defvjp(lambda a, b: (_bdot(a, b, _NT), (a, b)),
              lambda res, ct: (_bdot(ct, res[1], _NN), _bdot(ct, res[0], _TN)))


@jax.custom_vjp
def _mm_tn(a, b):
    return _bdot(a, b, _TN)


_mm_tn.defvjp(lambda a, b: (_bdot(a, b, _TN), (a, b)),
              lambda res, ct: (_bdot(res[1], ct, _NT), _bdot(res[0], ct, _NN)))


def _rms(t, w):
    return t * lax.rsqrt(jnp.mean(t * t, axis=-1, keepdims=True) + EPS) * w


def _l2(t):
    return t * lax.rsqrt(jnp.sum(t * t, axis=-1, keepdims=True) + EPS)


def _silu(t):
    return t * jax.nn.sigmoid(t)


def _softplus(t):
    return jnp.maximum(t, 0.0) + jnp.log1p(jnp.exp(-jnp.abs(t)))


def _matmul(a, b, mode, out_dtype, name, tm=512, tn=512, tk=512, ride=None):
    (m, k) = a.shape[::-1] if mode == "tn" else a.shape
    n = b.shape[0] if mode == "nt" else b.shape[1]
    tm, tn, tk = min(tm, m), min(tn, n), min(tk, k)
    assert m % tm == 0 and n % tn == 0 and k % tk == 0, (m, n, k, tm, tn, tk)
    if mode == "nn":
        a_spec = pl.BlockSpec((tm, tk), lambda i, j, kk: (i, kk))
        b_spec = pl.BlockSpec((tk, tn), lambda i, j, kk: (kk, j))
        dims = (((1,), (0,)), ((), ()))
    elif mode == "nt":
        a_spec = pl.BlockSpec((tm, tk), lambda i, j, kk: (i, kk))
        b_spec = pl.BlockSpec((tn, tk), lambda i, j, kk: (j, kk))
        dims = (((1,), (1,)), ((), ()))
    else:
        a_spec = pl.BlockSpec((tk, tm), lambda i, j, kk: (kk, i))
        b_spec = pl.BlockSpec((tk, tn), lambda i, j, kk: (kk, j))
        dims = (((0,), (0,)), ((), ()))
    nk = k // tk
    grid = (m // tm, n // tn, nk)
    n_in = len(ride.operands) if ride else 0
    n_out = len(ride.out_shapes) if ride else 0

    def body(*refs):
        a_ref, b_ref, o_ref = refs[0], refs[1], refs[2 + n_in]
        scratch = refs[3 + n_in + n_out:]
        step = (pl.program_id(0) * grid[1] + pl.program_id(1)) * nk + pl.program_id(2)
        riders = (refs[2:2 + n_in], refs[3 + n_in:3 + n_in + n_out], scratch[(0 if nk == 1 else 1):])
        if ride:
            pl.when(step == 0)(lambda: ride.start(*riders))
        if nk == 1:
            o_ref[...] = _bdot(a_ref[...], b_ref[...], dims).astype(out_dtype)
        else:
            acc_ref, kk = scratch[0], pl.program_id(2)

            @pl.when(kk == 0)
            def _():
                acc_ref[...] = jnp.zeros_like(acc_ref)

            acc_ref[...] += _bdot(a_ref[...], b_ref[...], dims)

            @pl.when(kk == nk - 1)
            def _():
                o_ref[...] = acc_ref[...].astype(out_dtype)
        if ride:
            pl.when(step == grid[0] * grid[1] * nk - 1)(lambda: ride.finish(*riders))

    out = pl.pallas_call(
        body, name=name, grid=grid,
        in_specs=[a_spec, b_spec] + [HBM] * n_in,
        out_specs=[pl.BlockSpec((tm, tn), lambda i, j, kk: (i, j))] + [HBM] * n_out,
        out_shape=[jax.ShapeDtypeStruct((m, n), out_dtype)] + (list(ride.out_shapes) if ride else []),
        scratch_shapes=([] if nk == 1 else [pltpu.VMEM((tm, tn), F32)]) + (list(ride.scratch_shapes) if ride else []),
        compiler_params=_params(("arbitrary",) * 3 if ride else ("parallel", "parallel", "arbitrary")),
    )(a, b, *(ride.operands if ride else []))
    return out if ride else out[0]


def _rms_fwd(x, w, tr=256):
    s, d = x.shape

    def body(x_ref, w_ref, o_ref):
        o_ref[...] = _rms(x_ref[...], w_ref[...]).astype(BF16)

    return pl.pallas_call(
        body, name="rms_fwd", grid=(s // tr,),
        in_specs=[pl.BlockSpec((tr, d), lambda i: (i, 0)), pl.BlockSpec((1, d), lambda i: (0, 0))],
        out_specs=pl.BlockSpec((tr, d), lambda i: (i, 0)),
        out_shape=jax.ShapeDtypeStruct((s, d), BF16), compiler_params=_params(("parallel",)),
    )(x, w)


def _rms_bwd(x, w, d_hn, dy, tr=256):
    s, d = x.shape

    def body(x_ref, w_ref, g_ref, dy_ref, gx_ref, gw_ref):
        _, vjp = jax.vjp(_rms, x_ref[...], w_ref[...])
        dx, dw = vjp(g_ref[...])
        gx_ref[...] = dy_ref[...] + dx

        @pl.when(pl.program_id(0) == 0)
        def _():
            gw_ref[...] = jnp.zeros_like(gw_ref)

        gw_ref[...] += dw

    row = pl.BlockSpec((tr, d), lambda i: (i, 0))
    vec = pl.BlockSpec((1, d), lambda i: (0, 0))
    return pl.pallas_call(
        body, name="rms_bwd", grid=(s // tr,), in_specs=[row, vec, row, row], out_specs=[row, vec],
        out_shape=[jax.ShapeDtypeStruct((s, d), F32), jax.ShapeDtypeStruct((1, d), F32)],
        compiler_params=_params(("arbitrary",)),
    )(x, w, d_hn, dy)


def _loss_dy(x, mo, target, tr=256):
    s, d = x.shape
    nt = s // tr

    def body(x_ref, mo_ref, t_ref, dy_ref, dyb_ref, l_ref):
        err = x_ref[...] + mo_ref[...] - t_ref[...]
        dy = err * (1.0 / d)
        dy_ref[...] = dy
        dyb_ref[...] = dy.astype(BF16)
        l_ref[...] = jnp.full(l_ref.shape, 0.5 * jnp.sum(jnp.sum(err * err, axis=1, keepdims=True) * (1.0 / d)), F32)

    row = pl.BlockSpec((tr, d), lambda i: (i, 0))
    return pl.pallas_call(
        body, name="loss_dy", grid=(nt,), in_specs=[row, row, row],
        out_specs=[row, row, pl.BlockSpec((1, 8, LANE), lambda i: (i, 0, 0))],
        out_shape=[jax.ShapeDtypeStruct((s, d), F32), jax.ShapeDtypeStruct((s, d), BF16),
                   jax.ShapeDtypeStruct((nt, 8, LANE), F32)],
        compiler_params=_params(("parallel",)),
    )(x, mo, target)


def _shift_rows(t, s):
    if s == 0:
        return t
    n = t.shape[0]
    rolled = pltpu.roll(t, (-s) % n, axis=0)
    idx = lax.broadcasted_iota(jnp.int32, t.shape, 0) + s
    return jnp.where((idx >= 0) & (idx < n), rolled, 0.0)


def _conv_fwd(proj, conv_w):
    s = proj.shape[0]
    nblk = 3 * A_WIDTH // LANE

    def body(x_ref, w_ref, o_ref):
        x = x_ref[...]
        acc = jnp.zeros_like(x)
        for j in range(CONV_K):
            acc = acc + w_ref[j:j + 1, :] * _shift_rows(x, j - CONV_K // 2)
        o_ref[...] = acc

    return pl.pallas_call(
        body, name="conv_fwd", grid=(nblk,),
        in_specs=[pl.BlockSpec((s, LANE), lambda i: (0, i)), pl.BlockSpec((CONV_K, LANE), lambda i: (0, i))],
        out_specs=pl.BlockSpec((None, s, LANE), lambda i: (i // A_HEADS, 0, i % A_HEADS)),
        out_shape=jax.ShapeDtypeStruct((3, s, A_WIDTH), F32), compiler_params=_params(("parallel",)),
    )(proj, conv_w)


def _conv_bwd(proj, conv_w, d_c):
    s = proj.shape[0]
    nblk = 3 * A_WIDTH // LANE

    def body(x_ref, w_ref, g_ref, dx_ref, dw_ref):
        x, g = x_ref[...], g_ref[...]
        acc = jnp.zeros_like(x)
        for j in range(CONV_K):
            off = j - CONV_K // 2
            acc = acc + w_ref[j:j + 1, :] * _shift_rows(g, -off)
            dw_ref[j:j + 1, :] = jnp.sum(_shift_rows(x, off) * g, axis=0, keepdims=True)
        dx_ref[...] = acc

    col = pl.BlockSpec((s, LANE), lambda i: (0, i))
    wsp = pl.BlockSpec((CONV_K, LANE), lambda i: (0, i))
    dsp = pl.BlockSpec((None, s, LANE), lambda i: (i // A_HEADS, 0, i % A_HEADS))
    return pl.pallas_call(
        body, name="conv_bwd", grid=(nblk,), in_specs=[col, wsp, dsp], out_specs=[col, wsp],
        out_shape=[jax.ShapeDtypeStruct((s, 3 * A_WIDTH), F32), jax.ShapeDtypeStruct((CONV_K, 3 * A_WIDTH), F32)],
        compiler_params=_params(("parallel",)),
    )(proj, conv_w, d_c)


A_FWD_HEADS = 4
A_BWD_HEADS = 4


def _neumann_inverse(a):
    c = a.shape[-1]
    eye = (lax.broadcasted_iota(jnp.int32, (c, c), 0) == lax.broadcasted_iota(jnp.int32, (c, c), 1)).astype(F32)
    tinv = eye + a
    p = a
    for _ in range(5):
        p = _mm(p, p)
        tinv = tinv + _mm(tinv, p)
    return tinv


@jax.custom_vjp
def _unit_inverse(a):
    return _neumann_inverse(a)


def _unit_inverse_fwd(a):
    tinv = _neumann_inverse(a)
    return tinv, tinv


def _unit_inverse_bwd(tinv, ct):
    return (_bdot(_bdot(tinv, ct, _TN), tinv, _NT),)


_unit_inverse.defvjp(_unit_inverse_fwd, _unit_inverse_bwd)


def _a_chain(st, cq, ck, cv, alpha, beta_raw, a_log, dt_b, incl, strict, last):
    c = CHUNK
    gb = -jnp.exp(a_log) * _softplus(alpha + dt_b)
    bb = jax.nn.sigmoid(beta_raw)
    q = _l2(_silu(cq)) * (A_DIM ** -0.5)
    k = _l2(_silu(ck))
    v = _silu(cv)

    gc = _dot(incl, jnp.broadcast_to(gb, (c, LANE)))
    tot = jnp.sum(gc * last, axis=0, keepdims=True)
    m1 = gc[:, :c]
    decay = incl * jnp.exp(incl * (m1 - m1.T))
    kb = k * bb
    vb = v * bb
    a = -(strict * decay * _mm_nt(kb, k))
    tinv = _unit_inverse(a)
    eg = jnp.exp(gc)
    u = _mm(tinv, vb)
    w = _mm(tinv, kb * eg)
    qk = _mm_nt(q, k) * decay
    v_new = u - _mm(w, st)
    o = _mm(q * eg, st) + _mm(qk, v_new)
    st_new = st * jnp.exp(tot) + _mm_tn(k * jnp.exp(tot - gc), v_new)
    return st_new, o


def _a_step(sts, cq, ck, cv, gts, pa, h0):
    c = CHUNK
    lane = lax.broadcasted_iota(jnp.int32, (1, LANE), 1)
    ii = lax.broadcasted_iota(jnp.int32, (c, c), 0)
    jj = lax.broadcasted_iota(jnp.int32, (c, c), 1)
    row = lax.broadcasted_iota(jnp.int32, (c, 1), 0)

    def pick(t, col):
        return jnp.sum(jnp.where(lane == col, t, 0.0), axis=1, keepdims=True)

    alpha, beta_raw, a_log, dt_b, incl, strict, last = [], [], [], [], [], [], []
    for b in range(sts.shape[0]):
        h, rev = h0 + b // 2, b % 2
        alpha.append(pick(gts[b], h + 8 * rev))
        beta_raw.append(pick(gts[b], h + 16 + 8 * rev))
        a_log.append(pick(pa[rev:rev + 1, :], h))
        dt_b.append(pick(pa[2 + rev:3 + rev, :], h))
        incl.append(((ii <= jj) if rev else (ii >= jj)).astype(F32))
        strict.append(((ii < jj) if rev else (ii > jj)).astype(F32))
        last.append((row == (0 if rev else c - 1)).astype(F32))
    stack = lambda ts: jnp.concatenate([t[None] for t in ts], axis=0)
    return jax.vmap(_a_chain)(sts, cq, ck, cv, stack(alpha), stack(beta_raw), stack(a_log), stack(dt_b),
                              stack(incl), stack(strict), stack(last))


def _a_final(o, za, pa):
    outs = []
    for j in range(o.shape[1] // A_DIM):
        ln = slice(j * A_DIM, (j + 1) * A_DIM)
        outs.append(_rms(o[:, ln], pa[4:5, :]) * _silu(za[:, ln]))
    return jnp.concatenate(outs, axis=1)


def _a_tiles(n, nchunk, heads):
    tiles = []
    for b in range(2 * heads):
        i = (nchunk - 1 - n) if b % 2 else n
        tiles.append((i, pl.ds(pl.multiple_of(i * CHUNK, CHUNK), CHUNK), slice((b // 2) * A_DIM, (b // 2 + 1) * A_DIM)))
    return tiles


def _a_load(tiles, c_ref, gt_ref):
    cq, ck, cv = (jnp.stack([c_ref[r, sl, ln] for _, sl, ln in tiles], axis=0) for r in range(3))
    return cq, ck, cv, jnp.stack([gt_ref[sl, :] for _, sl, _ in tiles], axis=0)


def _loop_by_two(n, step, init):
    assert n % 2 == 0
    return lax.fori_loop(0, n // 2, lambda m, carry: step(2 * m + 1, step(2 * m, carry, 0), 1), init)


def _a_scan(h0, heads, nchunk, c_ref, gt_ref, pa, of_ref, ob_ref, s_ref):
    def step(n, sts, parity):
        tiles = _a_tiles(n, nchunk, heads)
        sts_new, o = _a_step(sts, *_a_load(tiles, c_ref, gt_ref), pa, h0)
        for b, (i, sl, ln) in enumerate(tiles):
            s_ref[b, i] = sts[b]
            (ob_ref if b % 2 else of_ref)[sl, ln] = o[b]
        return sts_new

    _loop_by_two(nchunk, step, jnp.zeros((2 * heads, A_DIM, A_DIM), F32))


def _a_specs(s, heads):
    wide = heads * A_DIM
    once = pl.Buffered(1)
    trio = pl.BlockSpec((3, s, wide), lambda g: (0, 0, g), pipeline_mode=once)
    gates = pl.BlockSpec((s, LANE), lambda g: (0, P_GT // LANE))
    small = pl.BlockSpec((8, LANE), lambda g: (0, 0))

    def cols(base):
        return pl.BlockSpec((s, wide), lambda g: (0, base // wide + g), pipeline_mode=once)

    state = pl.BlockSpec((2 * heads, s // CHUNK, A_DIM, A_DIM), lambda g: (g, 0, 0, 0), pipeline_mode=once)
    return wide, trio, gates, small, cols, state


def _delta_fwd(cqkv, proj, pa, ride=None):
    s = cqkv.shape[1]
    nchunk = s // CHUNK
    heads = A_FWD_HEADS
    steps = A_HEADS // heads
    wide, trio, gates, small, cols, state = _a_specs(s, heads)
    n_in = len(ride.operands) if ride else 0
    n_out = len(ride.out_shapes) if ride else 0

    def body(*refs):
        c_ref, gt_ref, za_ref, pa_ref = refs[:4]
        out_ref, o_ref, s_ref = refs[4 + n_in:7 + n_in]
        ob_ref = refs[7 + n_in + n_out]
        riders = (refs[4:4 + n_in], refs[7 + n_in:7 + n_in + n_out], refs[8 + n_in + n_out:])
        g = pl.program_id(0)
        if ride:
            pl.when(g == 0)(lambda: ride.start(*riders))
            pl.when(g == steps - 1)(lambda: ride.middle(*riders))
        h0 = g * heads
        pa_v = pa_ref[...]
        _a_scan(h0, heads, nchunk, c_ref, gt_ref, pa_v, o_ref, ob_ref, s_ref)
        o_ref[...] += ob_ref[...]
        out_ref[...] = _a_final(o_ref[...], za_ref[...], pa_v).astype(BF16)
        if ride:
            pl.when(g == steps - 1)(lambda: ride.finish(*riders))

    assert steps > 1
    return pl.pallas_call(
        body, name="delta_fwd", grid=(steps,),
        in_specs=[trio, gates, cols(P_ZA), small] + [HBM] * n_in, out_specs=[cols(0), cols(0), state] + [HBM] * n_out,
        out_shape=[jax.ShapeDtypeStruct((s, D_MODEL), BF16),
                   jax.ShapeDtypeStruct((s, A_WIDTH), F32),
                   jax.ShapeDtypeStruct((2 * A_HEADS, nchunk, A_DIM, A_DIM), F32)]
        + (list(ride.out_shapes) if ride else []),
        scratch_shapes=[pltpu.VMEM((s, wide), F32)] + (list(ride.scratch_shapes) if ride else []),
        compiler_params=_params(("arbitrary",)),
    )(cqkv, proj, proj, pa, *(ride.operands if ride else []))


def _delta_out_bwd(o_sum, proj, pa, d_mixed, tr=256):
    s = o_sum.shape[0]

    def body(o_ref, za_ref, pa_ref, dm_ref, do_ref, dza_ref, dpa_ref):
        @pl.when(pl.program_id(0) == 0)
        def _():
            dpa_ref[...] = jnp.zeros_like(dpa_ref)

        _, vjp = jax.vjp(_a_final, o_ref[...], za_ref[...], pa_ref[...])
        d_o, d_za, dpa = vjp(dm_ref[...].astype(F32))
        do_ref[...] = d_o
        dza_ref[...] = d_za
        dpa_ref[...] += dpa

    def rows(col):
        return pl.BlockSpec((tr, A_WIDTH), lambda i: (i, col))

    small = pl.BlockSpec((8, LANE), lambda i: (0, 0))
    return pl.pallas_call(
        body, name="delta_out_bwd", grid=(s // tr,), in_specs=[rows(0), rows(P_ZA // A_WIDTH), small, rows(0)],
        out_specs=[rows(0), rows(0), small],
        out_shape=[jax.ShapeDtypeStruct((s, A_WIDTH), F32), jax.ShapeDtypeStruct((s, A_WIDTH), F32),
                   jax.ShapeDtypeStruct((8, LANE), F32)],
        compiler_params=_params(("arbitrary",)),
    )(o_sum, proj, pa, d_mixed)


def _delta_bwd(cqkv, proj, pa, d_o, states, ride=None):
    s = cqkv.shape[1]
    nchunk = s // CHUNK
    heads = A_BWD_HEADS
    steps = A_HEADS // heads
    wide, trio, gates, small, cols, state = _a_specs(s, heads)
    n_in = len(ride.operands) if ride else 0
    n_out = len(ride.out_shapes) if ride else 0

    def body(*refs):
        c_ref, gt_ref, pa_ref, do_ref, s_hbm = refs[:5]
        dc_ref, dgt_ref, dpa_ref = refs[5 + n_in:8 + n_in]
        s_buf, s_sems = refs[8 + n_in + n_out:10 + n_in + n_out]
        riders = (refs[5:5 + n_in], refs[8 + n_in:8 + n_in + n_out], refs[10 + n_in + n_out:])
        if ride:
            pl.when(pl.program_id(0) == 0)(lambda: ride.start(*riders))
        h0 = pl.program_id(0) * heads
        pa_v = pa_ref[...]

        @pl.when(h0 == 0)
        def _():
            dgt_ref[...] = jnp.zeros_like(dgt_ref)
            dpa_ref[...] = jnp.zeros_like(dpa_ref)

        dc_ref[...] = jnp.zeros_like(dc_ref)

        def state_copies(n, slot):
            return [pltpu.make_async_copy(s_hbm.at[2 * h0 + b, i], s_buf.at[slot, b], s_sems.at[slot, b])
                    for b, (i, _, _) in enumerate(_a_tiles(nchunk - 1 - n, nchunk, heads))]

        for cp in state_copies(0, 0):
            cp.start()

        def step(n, carry, parity):
            d_sts, dpa = carry
            tiles = _a_tiles(nchunk - 1 - n, nchunk, heads)
            for cp in state_copies(n, parity):
                cp.wait()

            @pl.when(n + 1 < nchunk)
            def _():
                for cp in state_copies(n + 1, 1 - parity):
                    cp.start()

            sts = s_buf[parity]
            d_o_t = jnp.stack([do_ref[sl, ln] for _, sl, ln in tiles], axis=0)
            _, vjp_c = jax.vjp(lambda *a: _a_step(*a, h0), sts, *_a_load(tiles, c_ref, gt_ref), pa_v)
            d_prev, dcq, dck, dcv, dgts, dpa_i = vjp_c((d_sts, d_o_t))
            for b, (_, sl, ln) in enumerate(tiles):
                for r, dc in enumerate((dcq, dck, dcv)):
                    dc_ref[r, sl, ln] += dc[b]
                dgt_ref[sl, :] += dgts[b]
            return d_prev, dpa + dpa_i

        init = (jnp.zeros((2 * heads, A_DIM, A_DIM), F32), jnp.zeros((8, LANE), F32))
        _, dpa_out = lax.fori_loop(0, nchunk, lambda n, carry: step(n, carry, n % 2), init)
        dpa_ref[...] += dpa_out
        if ride:
            pl.when(pl.program_id(0) == steps - 1)(lambda: ride.finish(*riders))

    fixed = pl.BlockSpec((s, LANE), lambda g: (0, 0))
    return pl.pallas_call(
        body, name="delta_bwd", grid=(steps,),
        in_specs=[trio, gates, small, cols(0), pl.BlockSpec(memory_space=pl.ANY)] + [HBM] * n_in,
        out_specs=[trio, fixed, small] + [HBM] * n_out,
        out_shape=[jax.ShapeDtypeStruct((3, s, A_WIDTH), F32), jax.ShapeDtypeStruct((s, LANE), F32),
                   jax.ShapeDtypeStruct((8, LANE), F32)] + (list(ride.out_shapes) if ride else []),
        scratch_shapes=[pltpu.VMEM((2, 2 * heads, A_DIM, A_DIM), F32), pltpu.SemaphoreType.DMA((2, 2 * heads))]
        + (list(ride.scratch_shapes) if ride else []),
        compiler_params=_params(("arbitrary",)),
    )(cqkv, proj, pa, d_o, states, *(ride.operands if ride else []))


def _rope_tables(s):
    inv = ROPE_THETA ** (-jnp.arange(0, B_DIM, 2, dtype=F32) / B_DIM)
    ang = jnp.arange(s, dtype=F32)[:, None] * inv[None, :]
    cos, sin = jnp.cos(ang), jnp.sin(ang)
    return jnp.concatenate([cos, cos], axis=1), jnp.concatenate([-sin, sin], axis=1)


def _b_block(q_t, z_t, k3, v3, cos_q, sin_q, cos_k, sin_k, pb, n, nb):
    w = WINDOW
    def swap(t):
        return jnp.concatenate([t[:, B_DIM // 2:], t[:, :B_DIM // 2]], axis=1)

    grp = B_HEADS // B_KV
    qi = lax.broadcasted_iota(jnp.int32, (grp * w, 3 * w), 0) & (w - 1)
    kj = lax.broadcasted_iota(jnp.int32, (grp * w, 3 * w), 1)
    kpos = kj + (n - 1) * w
    mask = (jnp.abs(kj - w - qi) <= w) & (kpos >= 0) & (kpos < nb * w)
    lane = lax.broadcasted_iota(jnp.int32, (1, LANE), 1)
    qn, kn = pb[0:1, :B_DIM], pb[1:2, :B_DIM]
    cos_g = jnp.concatenate([cos_q] * grp, axis=0)
    sin_g = jnp.concatenate([sin_q] * grp, axis=0)
    def group(q, k, v, sink):
        k = _rms(k, kn)
        k = k * cos_k + swap(k) * sin_k
        q = _rms(q, qn)
        q = q * cos_g + swap(q) * sin_g
        s = _mm_nt(q, k) * (B_DIM ** -0.5)
        s = jnp.where(mask, s, -jnp.inf)
        m = jnp.maximum(jnp.max(s, axis=1, keepdims=True), sink)
        p = jnp.exp(s - m)
        p = p / (jnp.sum(p, axis=1, keepdims=True) + jnp.exp(sink - m))
        return _mm(p, v)

    stack = lambda ts: jnp.concatenate([t[None] for t in ts], axis=0)
    qs, ks, vs, sinks = [], [], [], []
    for hk in range(B_KV):
        heads = [hk * grp + g for g in range(grp)]
        ks.append(k3[:, hk * B_DIM:(hk + 1) * B_DIM])
        vs.append(v3[:, hk * B_DIM:(hk + 1) * B_DIM])
        qs.append(jnp.concatenate([q_t[:, hq * B_DIM:(hq + 1) * B_DIM] for hq in heads], axis=0))
        sinks.append(jnp.concatenate(
            [jnp.broadcast_to(jnp.sum(jnp.where(lane == hq, pb[2:3, :], 0.0), axis=1, keepdims=True), (w, 1))
             for hq in heads], axis=0))
    o = jax.vmap(group)(stack(qs), stack(ks), stack(vs), stack(sinks))
    outs = [o[hk, g * w:(g + 1) * w, :] for hk in range(B_KV) for g in range(grp)]
    return jnp.concatenate(outs, axis=1) * _silu(z_t)


def _b_specs(s):
    nb = s // WINDOW
    qsp = pl.BlockSpec((WINDOW, 512), lambda n: (n, P_QB // 512))
    zsp = pl.BlockSpec((WINDOW, 512), lambda n: (n, P_ZB // 512))

    def three(col, width):
        return [pl.BlockSpec((WINDOW, width), lambda n: (jnp.maximum(n - 1, 0), col)),
                pl.BlockSpec((WINDOW, width), lambda n: (n, col)),
                pl.BlockSpec((WINDOW, width), lambda n: (jnp.minimum(n + 1, nb - 1), col))]

    tab = pl.BlockSpec((WINDOW, B_DIM), lambda n: (n, 0))
    small = pl.BlockSpec((8, LANE), lambda n: (0, 0))
    specs = [qsp, zsp] + three(P_KB // LANE, LANE) + three(P_VB // LANE, LANE) + [tab, tab] + three(0, B_DIM) + three(0, B_DIM) + [small]
    return nb, specs


def _b_args(proj, cos2, sin2, pb):
    return (proj, proj, proj, proj, proj, proj, proj, proj, cos2, sin2, cos2, cos2, cos2, sin2, sin2, sin2, pb)


def _b_load(refs):
    (q_ref, z_ref, kp, kc, kx, vp, vc, vx, cq, sq, ckp, ckc, ckx, skp, skc, skx, pb_ref) = refs
    cat = lambda *r: jnp.concatenate([t[...] for t in r], axis=0)
    return (q_ref[...], z_ref[...], cat(kp, kc, kx), cat(vp, vc, vx), cq[...], sq[...], cat(ckp, ckc, ckx),
            cat(skp, skc, skx), pb_ref[...])


def _attn_b_fwd(proj, cos2, sin2, pb, mixed):
    s = proj.shape[0]
    nb, specs = _b_specs(s)

    def body(*refs):
        o_ref = refs[-1]
        args = _b_load(refs[:-2])
        o_ref[...] = _b_block(*args, pl.program_id(0), nb).astype(BF16)

    return pl.pallas_call(
        body, name="attn_b_fwd", grid=(nb,), in_specs=specs + [pl.BlockSpec(memory_space=pl.ANY)],
        out_specs=pl.BlockSpec((WINDOW, 512), lambda n: (n, A_WIDTH // 512)),
        out_shape=jax.ShapeDtypeStruct(mixed.shape, mixed.dtype), input_output_aliases={len(specs): 0},
        compiler_params=_params(("parallel",)),
    )(*_b_args(proj, cos2, sin2, pb), mixed)


def _attn_b_bwd(proj, cos2, sin2, pb, d_mixed):
    s = proj.shape[0]
    nb, specs = _b_specs(s)
    w = WINDOW

    def body(*refs):
        dm_ref, dq_ref, dz_ref, dk_ref, dv_ref, dpb_ref = refs[-6:]
        n = pl.program_id(0)
        q_t, z_t, k3, v3, cq, sq, ck, sk, pb_v = _b_load(refs[:-6])

        @pl.when(n == 0)
        def _():
            dk_ref[...] = jnp.zeros_like(dk_ref)
            dv_ref[...] = jnp.zeros_like(dv_ref)
            dpb_ref[...] = jnp.zeros_like(dpb_ref)

        def f(q_, z_, k_, v_, pb_):
            return _b_block(q_, z_, k_, v_, cq, sq, ck, sk, pb_, n, nb)

        _, vjp = jax.vjp(f, q_t, z_t, k3, v3, pb_v)
        dq, dz, dk3, dv3, dpb = vjp(dm_ref[...])
        dq_ref[...] = dq
        dz_ref[...] = dz
        dpb_ref[...] += dpb

        def add(j, cond):
            @pl.when(cond)
            def _():
                rows = pl.ds(pl.multiple_of((n - 1 + j) * w, w), w)
                dk_ref[rows, :] += dk3[j * w:(j + 1) * w, :]
                dv_ref[rows, :] += dv3[j * w:(j + 1) * w, :]

        add(0, n > 0)
        add(1, n >= 0)
        add(2, n < nb - 1)

    blk = pl.BlockSpec((w, 512), lambda n: (n, 0))
    whole = pl.BlockSpec((s, LANE), lambda n: (0, 0))
    small = pl.BlockSpec((8, LANE), lambda n: (0, 0))
    return pl.pallas_call(
        body, name="attn_b_bwd", grid=(nb,),
        in_specs=specs + [pl.BlockSpec((w, 512), lambda n: (n, 2))],
        out_specs=[blk, blk, whole, whole, small],
        out_shape=[jax.ShapeDtypeStruct((s, 512), F32), jax.ShapeDtypeStruct((s, 512), F32),
                   jax.ShapeDtypeStruct((s, LANE), F32), jax.ShapeDtypeStruct((s, LANE), F32),
                   jax.ShapeDtypeStruct((8, LANE), F32)],
        compiler_params=_params(("arbitrary",)),
    )(*_b_args(proj, cos2, sin2, pb), d_mixed)


def _mem_kv_fwd(mem, mem_norm_w, w_kv):
    def body(mem_ref, nw_ref, w_ref, kv_ref):
        mn = _rms(mem_ref[...], nw_ref[...]).astype(BF16)
        kv_ref[...] = jnp.dot(mn, w_ref[...], preferred_element_type=F32)

    return pl.pallas_call(
        body, name="mem_kv_fwd", out_shape=jax.ShapeDtypeStruct((MEM_LEN, 2 * C_HEADS * C_DIM), F32),
        compiler_params=_params(),
    )(mem, mem_norm_w, w_kv)


def _mem_kv_bwd(mem, mem_norm_w, w_kv, d_kv):
    def body(mem_ref, nw_ref, w_ref, g_ref, gw_ref, gn_ref):
        mn, vjp = jax.vjp(_rms, mem_ref[...], nw_ref[...])
        g = g_ref[...].astype(BF16)
        gw_ref[...] = lax.dot_general(mn.astype(BF16), g, (((0,), (0,)), ((), ())), preferred_element_type=F32)
        d_mn = lax.dot_general(g, w_ref[...], (((1,), (1,)), ((), ())), preferred_element_type=F32)
        gn_ref[...] = vjp(d_mn)[1]

    return pl.pallas_call(
        body, name="mem_kv_bwd",
        out_shape=[jax.ShapeDtypeStruct((D_MODEL, 2 * C_HEADS * C_DIM), F32), jax.ShapeDtypeStruct((1, D_MODEL), F32)],
        compiler_params=_params(),
    )(mem, mem_norm_w, w_kv, d_kv)


def _c_tile(q_t, z_t, kvm, pc):
    width = C_HEADS * C_DIM
    outs = []
    for h in range(C_HEADS):
        q = _rms(q_t[:, h * C_DIM:(h + 1) * C_DIM], pc[0:1, :])
        k = _rms(kvm[:, h * C_DIM:(h + 1) * C_DIM], pc[1:2, :])
        v = kvm[:, width + h * C_DIM:width + (h + 1) * C_DIM]
        s = _mm_nt(q, k) * (C_DIM ** -0.5)
        p = jnp.exp(s - jnp.max(s, axis=1, keepdims=True))
        p = p / jnp.sum(p, axis=1, keepdims=True)
        outs.append(_mm(p, v))
    return jnp.concatenate(outs, axis=1) * _silu(z_t)


def _attn_c_fwd(proj, kvm, pc, mixed, tq=256):
    s = proj.shape[0]

    def body(q_ref, z_ref, kv_ref, pc_ref, mixed_ref, o_ref):
        o_ref[...] = _c_tile(q_ref[...], z_ref[...], kv_ref[...], pc_ref[...]).astype(BF16)

    return pl.pallas_call(
        body, name="attn_c_fwd", grid=(s // tq,),
        in_specs=[pl.BlockSpec((tq, 512), lambda i: (i, P_QC // 512)), pl.BlockSpec((tq, 512), lambda i: (i, P_ZC // 512)),
                  pl.BlockSpec(kvm.shape, lambda i: (0, 0)), pl.BlockSpec((8, LANE), lambda i: (0, 0)),
                  pl.BlockSpec(memory_space=pl.ANY)],
        out_specs=pl.BlockSpec((tq, 512), lambda i: (i, (A_WIDTH + 512) // 512)),
        out_shape=jax.ShapeDtypeStruct(mixed.shape, mixed.dtype), input_output_aliases={4: 0},
        compiler_params=_params(("parallel",)),
    )(proj, proj, kvm, pc, mixed)


def _attn_c_bwd(proj, kvm, pc, d_mixed, tq=256):
    s = proj.shape[0]

    def body(q_ref, z_ref, kv_ref, pc_ref, dm_ref, dq_ref, dz_ref, dkv_ref, dpc_ref):
        @pl.when(pl.program_id(0) == 0)
        def _():
            dkv_ref[...] = jnp.zeros_like(dkv_ref)
            dpc_ref[...] = jnp.zeros_like(dpc_ref)

        _, vjp = jax.vjp(_c_tile, q_ref[...], z_ref[...], kv_ref[...], pc_ref[...])
        dq, dz, dkv, dpc = vjp(dm_ref[...])
        dq_ref[...] = dq
        dz_ref[...] = dz
        dkv_ref[...] += dkv
        dpc_ref[...] += dpc

    blk = pl.BlockSpec((tq, 512), lambda i: (i, 0))
    kvs = pl.BlockSpec(kvm.shape, lambda i: (0, 0))
    small = pl.BlockSpec((8, LANE), lambda i: (0, 0))
    return pl.pallas_call(
        body, name="attn_c_bwd", grid=(s // tq,),
        in_specs=[pl.BlockSpec((tq, 512), lambda i: (i, P_QC // 512)), pl.BlockSpec((tq, 512), lambda i: (i, P_ZC // 512)),
                  kvs, small, pl.BlockSpec((tq, 512), lambda i: (i, 3))],
        out_specs=[blk, blk, kvs, small],
        out_shape=[jax.ShapeDtypeStruct((s, 512), F32), jax.ShapeDtypeStruct((s, 512), F32),
                   jax.ShapeDtypeStruct(kvm.shape, F32), jax.ShapeDtypeStruct((8, LANE), F32)],
        compiler_params=_params(("arbitrary",)),
    )(proj, proj, kvm, pc, d_mixed)


def _pad_row(v, width=LANE):
    v = v.reshape(1, -1)
    return jnp.pad(v, ((0, 0), (0, width - v.shape[1])))


def _local_step(x, mem, target, norm_w, w_perm_t, w_blocks_t, conv_w, pa, pb, pc, mem_norm_w, w_kv, w_out, gather=None,
                exchange=None):
    s = x.shape[0]
    cos2, sin2 = _rope_tables(s)
    hn = _rms_fwd(x, norm_w)
    wide = dict(tm=1024, tn=512, tk=2048)
    proj = _matmul(hn, w_perm_t, "nt", F32, "mm_proj", **wide)
    cqkv = _conv_fwd(proj, conv_w)
    if gather is None:
        mixed, o_sum, states = _delta_fwd(cqkv, proj, pa)
    else:
        mixed, o_sum, states, *arrived = _delta_fwd(cqkv, proj, pa, gather[0])
        w_out, w_kv = gather[1](*arrived)
    mixed = _attn_b_fwd(proj, cos2, sin2, pb, mixed)
    kvm = _mem_kv_fwd(mem, mem_norm_w, w_kv)
    mixed = _attn_c_fwd(proj, kvm, pc, mixed)
    mo = _matmul(mixed, w_out, "nn", F32, "mm_out", **wide)
    dy, dyb, loss_parts = _loss_dy(x, mo, target)

    d_mixed = _matmul(dyb, w_out, "nt", F32, "mm_dmixed", **wide)
    g_w_out = _matmul(mixed, dyb, "tn", F32, "mm_gwout", **wide)
    d_qc, d_zc, d_kvm, d_pc = _attn_c_bwd(proj, kvm, pc, d_mixed)
    g_w_kv, g_mem_norm = _mem_kv_bwd(mem, mem_norm_w, w_kv, d_kvm)
    d_qb, d_zb, d_kb, d_vb, d_pb = _attn_b_bwd(proj, cos2, sin2, pb, d_mixed)
    d_o, d_za, d_pa_out = _delta_out_bwd(o_sum, proj, pa, d_mixed)
    early = exchange[0](g_w_out, g_w_kv) if exchange else None
    d_c, d_gt, d_pa_scan, *landed_early = _delta_bwd(cqkv, proj, pa, d_o, states, early)
    d_pa = d_pa_out + d_pa_scan
    d_qkv, g_conv = _conv_bwd(proj, conv_w, d_c)
    d_proj = _cotangent_blocks(d_qkv, d_za, d_gt, d_qb, d_kb, d_vb, d_zb, d_qc, d_zc)
    g_w_blocks_t = _matmul(d_proj, hn, "tn", F32, "mm_gwin", tm=512, tn=2048, tk=2048)
    late = exchange[1](g_w_blocks_t) if exchange else None
    d_hn = _matmul(d_proj, w_blocks_t, "nn", F32, "mm_dhn", tm=1024, tn=2048, tk=512, ride=late)
    d_hn, landed_late = (d_hn[0], list(d_hn[1:])) if late else (d_hn, [])
    g_x, g_norm = _rms_bwd(x, norm_w, d_hn, dy)
    return dict(loss_parts=loss_parts, g_x=g_x, g_norm=g_norm, g_w_blocks_t=g_w_blocks_t, g_conv=g_conv, d_pa=d_pa,
                d_pb=d_pb, d_pc=d_pc, g_mem_norm=g_mem_norm, g_w_kv=g_w_kv, g_w_out=g_w_out,
                landed=landed_late + landed_early)


_SEGMENTS = ((0, O_GT, 0), (O_GT, O_QB, P_GT), (O_QB, O_KB, P_QB), (O_KB, O_VB, P_KB), (O_VB, O_ZB, P_VB),
             (O_ZB, O_QC, P_ZB), (O_QC, O_ZC, P_QC), (O_ZC, IN_WIDTH, P_ZC))


def _permute_blocks(w4):
    parts = []
    for first, end, _ in sorted(_SEGMENTS, key=lambda seg: seg[2]):
        row = first
        while row < end:
            k = row // W_IN_BLOCK
            stop = min(end, (k + 1) * W_IN_BLOCK)
            parts.append(w4[k][row - k * W_IN_BLOCK:stop - k * W_IN_BLOCK, :])
            row = stop
    parts.append(jnp.zeros((P_WIDTH - IN_WIDTH, w4.shape[2]), w4.dtype))
    return jnp.concatenate(parts, axis=0)


def _cotangent_blocks(d_qkv, d_za, d_gt, d_qb, d_kb, d_vb, d_zb, d_qc, d_zc):
    orig = jnp.concatenate([d_qkv, d_za, d_gt[:, :O_QB - O_GT], d_qb, d_kb, d_vb, d_zb, d_qc, d_zc], axis=1)
    pad = jnp.zeros((orig.shape[0], W_IN_PAD - W_IN_BLOCK), orig.dtype)
    parts = []
    for k in range(N_CHIPS):
        parts += [orig[:, k * W_IN_BLOCK:(k + 1) * W_IN_BLOCK], pad]
    return jnp.concatenate(parts, axis=1).astype(BF16)


HBM = pl.BlockSpec(memory_space=pltpu.HBM)


def _place():
    x, y, c = lax.axis_index("x"), lax.axis_index("y"), lax.axis_index("c")
    chips = [(1 - x, y), (x, 1 - y), (1 - x, 1 - y)]
    return x, y, c, 2 * x + y, chips, [2 * cx + cy for cx, cy in chips]


PIECE_ROWS_CAP = 600


def _remote(src, dst, send_sems, recv_sems, k, to):
    return pltpu.make_async_remote_copy(src_ref=src, dst_ref=dst, send_sem=send_sems.at[k], recv_sem=recv_sems.at[k],
                                        device_id=to, device_id_type=MESH)


def _half_cols(ref, c):
    half = ref.shape[-1] // 2
    return pl.ds(pl.multiple_of(c * half, LANE), half)


class _PairedGather:
    def __init__(self, blocks):
        n = len(blocks)
        self.operands = list(blocks)
        self.out_shapes = [jax.ShapeDtypeStruct((N_CHIPS,) + b.shape, b.dtype) for b in blocks]
        self.scratch_shapes = [pltpu.SemaphoreType.DMA((6 * n,)), pltpu.SemaphoreType.DMA((6 * n,))]

    @staticmethod
    def _copies(srcs, dsts, sems):
        x, y, c, me, chips, chip_ids = _place()
        sends, landed, passes, passed = [], [], [], []
        for a, (src, dst) in enumerate(zip(srcs, dsts)):
            mine, other = _half_cols(src, c), _half_cols(src, 1 - c)
            for j, (chip, cid) in enumerate(zip(chips, chip_ids)):
                sends.append(_remote(src.at[:, mine], dst.at[me, :, mine], sems[0], sems[1], 6 * a + j, (*chip, c)))
                here = dst.at[cid, :, mine]
                landed.append(_remote(here, here, sems[0], sems[1], 6 * a + j, (x, y, 1 - c)))
                passes.append(_remote(here, here, sems[0], sems[1], 6 * a + 3 + j, (x, y, 1 - c)))
                there = dst.at[cid, :, other]
                passed.append(_remote(there, there, sems[0], sems[1], 6 * a + 3 + j, (x, y, 1 - c)))
        return sends, landed, passes, passed

    def start(self, srcs, dsts, sems):
        for cp in self._copies(srcs, dsts, sems)[0]:
            cp.start()

    def middle(self, srcs, dsts, sems):
        _, landed, passes, _ = self._copies(srcs, dsts, sems)
        for arrived, onward in zip(landed, passes):
            arrived.wait_recv()
            onward.start()

    def finish(self, srcs, dsts, sems):
        sends, _, passes, passed = self._copies(srcs, dsts, sems)
        for cp in passed:
            cp.wait_recv()
        for cp in sends + passes:
            cp.wait_send()


def _all_gather_weights(bigs, conv_b):
    bigs = tuple(bigs)
    n_big = len(bigs)

    def body(*refs):
        srcs, conv_src = refs[:n_big], refs[n_big]
        dsts, conv_dst = refs[n_big + 1:2 * n_big + 1], refs[2 * n_big + 1]
        send_sems, recv_sems, local_sems = refs[2 * n_big + 2:]
        x, y, c, me, chips, chip_ids = _place()
        sibling = (x, y, 1 - c)
        local = [pltpu.make_async_copy(src, dst.at[me], local_sems.at[a]) for a, (src, dst) in enumerate(zip(srcs, dsts))]
        local.append(pltpu.make_async_copy(conv_src, conv_dst.at[me], local_sems.at[n_big]))
        for cp in local:
            cp.start()
        sends = []
        for a, (src, dst) in enumerate(zip(srcs, dsts)):
            mine = _half_cols(src, c)
            for j, chip in enumerate(chips):
                sends.append(_remote(src.at[:, mine], dst.at[me, :, mine], send_sems, recv_sems, 6 * a + j, (*chip, c)))
        for j, chip in enumerate(chips):
            sends.append(_remote(conv_src, conv_dst.at[me], send_sems, recv_sems, 6 * n_big + j, (*chip, c)))
        for cp in sends:
            cp.start()
        passed = []
        for a, (src, dst) in enumerate(zip(srcs, dsts)):
            mine = _half_cols(src, c)
            for j, cid in enumerate(chip_ids):
                landed = dst.at[cid, :, mine]
                _remote(landed, landed, send_sems, recv_sems, 6 * a + j, sibling).wait_recv()
                cp = _remote(landed, landed, send_sems, recv_sems, 6 * a + 3 + j, sibling)
                cp.start()
                passed.append(cp)
        for a, (src, dst) in enumerate(zip(srcs, dsts)):
            other = _half_cols(src, 1 - c)
            for j, cid in enumerate(chip_ids):
                landed = dst.at[cid, :, other]
                _remote(landed, landed, send_sems, recv_sems, 6 * a + 3 + j, sibling).wait_recv()
        for j, cid in enumerate(chip_ids):
            _remote(conv_src, conv_dst.at[cid], send_sems, recv_sems, 6 * n_big + j, sibling).wait_recv()
        for cp in sends + passed:
            cp.wait_send()
        for cp in local:
            cp.wait()

    n_sem = 6 * n_big + 3
    return pl.pallas_call(
        body, name="all_gather_weights",
        out_shape=[jax.ShapeDtypeStruct((N_CHIPS,) + w.shape, w.dtype) for w in bigs + (conv_b,)],
        in_specs=[pl.BlockSpec(memory_space=pltpu.VMEM)] * (n_big + 1), out_specs=[HBM] * (n_big + 1),
        scratch_shapes=[pltpu.SemaphoreType.DMA((n_sem,)), pltpu.SemaphoreType.DMA((n_sem,)),
                        pltpu.SemaphoreType.DMA((n_big + 1,))],
        compiler_params=_params(),
    )(*bigs, conv_b)


def _pair_exchange(grads, name):
    n = len(grads)
    pieces = [_row_tile(g.shape[1]) for g in grads]

    def body(*refs):
        srcs, gots = refs[:n], refs[n:2 * n]
        stages = refs[2 * n:3 * n]
        send_sems, recv_sems, load_sems = refs[3 * n:]
        x, y, c, _, _, _ = _place()
        sibling = (x, y, 1 - c)
        for a in range(n):
            slabs, rows, _ = gots[a].shape
            piece = pieces[a]
            per_slab = rows // piece
            theirs = _half_cols(srcs[a], 1 - c)
            loads, sends = [], []
            for i in range(slabs * per_slab):
                k, r, slot = i // per_slab, i % per_slab, i % 2
                part = pl.ds(r * piece, piece)
                loads.append(pltpu.make_async_copy(srcs[a].at[k, part, theirs], stages[a].at[slot], load_sems.at[2 * a + slot]))
                sends.append(pltpu.make_async_remote_copy(
                    src_ref=stages[a].at[slot], dst_ref=gots[a].at[k, part, :],
                    send_sem=send_sems.at[2 * a + slot], recv_sem=recv_sems.at[a], device_id=sibling, device_id_type=MESH))
            loads[0].start()
            for i in range(len(loads)):
                loads[i].wait()
                sends[i].start()
                if i + 1 < len(loads):
                    if i >= 1:
                        sends[i - 1].wait_send()
                    loads[i + 1].start()
            for cp in sends[-2:]:
                cp.wait_send()
        for a in range(n):
            whole = srcs[a].at[:, :, _half_cols(srcs[a], c)]
            pltpu.make_async_remote_copy(src_ref=whole, dst_ref=gots[a], send_sem=send_sems.at[2 * a],
                                         recv_sem=recv_sems.at[a], device_id=sibling, device_id_type=MESH).wait_recv()

    halves = [jax.ShapeDtypeStruct((g.shape[0], g.shape[1], g.shape[2] // 2), g.dtype) for g in grads]
    return pl.pallas_call(
        body, name=name, out_shape=halves, in_specs=[HBM] * n, out_specs=[HBM] * n,
        scratch_shapes=[pltpu.VMEM((2, piece, g.shape[2] // 2), g.dtype) for piece, g in zip(pieces, grads)]
        + [pltpu.SemaphoreType.DMA((2 * n,)), pltpu.SemaphoreType.DMA((n,)), pltpu.SemaphoreType.DMA((2 * n,))],
        compiler_params=_params(),
    )(*grads)


class _ChipExchange:
    def __init__(self, halves):
        n = len(halves)
        self.operands = list(halves)
        self.out_shapes = [jax.ShapeDtypeStruct((N_CHIPS - 1,) + h.shape[1:], h.dtype) for h in halves]
        self.scratch_shapes = [pltpu.SemaphoreType.DMA((3 * n,)), pltpu.SemaphoreType.DMA((3 * n,))]

    @staticmethod
    def _copies(srcs, lands, sems):
        _, _, c, _, chips, chip_ids = _place()
        return [_remote(src.at[cid], land.at[j], sems[0], sems[1], 3 * a + j, (*chip, c))
                for a, (src, land) in enumerate(zip(srcs, lands)) for j, (chip, cid) in enumerate(zip(chips, chip_ids))]

    def start(self, srcs, lands, sems):
        for cp in self._copies(srcs, lands, sems):
            cp.start()

    def finish(self, srcs, lands, sems):
        copies = self._copies(srcs, lands, sems)
        for cp in copies:
            cp.wait_recv()
        for cp in copies:
            cp.wait_send()


def _pair_gather(halves, rows):
    n = len(halves)

    def body(*refs):
        srcs, fulls = refs[:n], refs[n:2 * n]
        send_sems, recv_sems, local_sems = refs[2 * n:]
        x, y, c, _, _, _ = _place()
        copies = []
        for a in range(n):
            mine, src = _half_cols(fulls[a], c), srcs[a].at[pl.ds(0, rows[a]), :]
            keep = pltpu.make_async_copy(src, fulls[a].at[:, mine], local_sems.at[a])
            keep.start()
            give = _remote(src, fulls[a].at[:, mine], send_sems, recv_sems, a, (x, y, 1 - c))
            give.start()
            copies += [keep, give]
        for a in range(n):
            other, src = _half_cols(fulls[a], 1 - c), srcs[a].at[pl.ds(0, rows[a]), :]
            copies[2 * a].wait()
            copies[2 * a + 1].wait_send()
            _remote(src, fulls[a].at[:, other], send_sems, recv_sems, a, (x, y, 1 - c)).wait_recv()

    return pl.pallas_call(
        body, name="grad_pair_gather",
        out_shape=[jax.ShapeDtypeStruct((r, 2 * h.shape[1]), h.dtype) for r, h in zip(rows, halves)],
        in_specs=[pl.BlockSpec(memory_space=pltpu.VMEM)] * n, out_specs=[HBM] * n,
        scratch_shapes=[pltpu.SemaphoreType.DMA((n,)), pltpu.SemaphoreType.DMA((n,)), pltpu.SemaphoreType.DMA((n,))],
    )(*halves)


def _all_reduce_small(p):
    n_dev = 8

    def body(p_ref, o_ref, land, send_sems, recv_sems):
        x, y, c = lax.axis_index("x"), lax.axis_index("y"), lax.axis_index("c")
        me = 4 * x + 2 * y + c
        land[me] = p_ref[...]
        sends = []
        for k in range(1, n_dev):
            fx, fy, fc = (k >> 2) & 1, (k >> 1) & 1, k & 1
            to = (x ^ fx, y ^ fy, c ^ fc)
            cp = _remote(p_ref, land.at[me], send_sems, recv_sems, k - 1, to)
            cp.start()
            sends.append(cp)
        for k in range(1, n_dev):
            _remote(p_ref, land.at[me ^ k], send_sems, recv_sems, k - 1, (x, y, c)).wait_recv()
        total = land[0]
        for d in range(1, n_dev):
            total = total + land[d]
        o_ref[...] = total
        for cp in sends:
            cp.wait_send()

    vm = pl.BlockSpec(memory_space=pltpu.VMEM)
    return pl.pallas_call(
        body, name="all_reduce_small", out_shape=jax.ShapeDtypeStruct(p.shape, p.dtype), in_specs=[vm], out_specs=vm,
        scratch_shapes=[pltpu.VMEM((n_dev,) + p.shape, p.dtype), pltpu.SemaphoreType.DMA((n_dev - 1,)),
                        pltpu.SemaphoreType.DMA((n_dev - 1,))],
    )(p)


def _row_tile(rows):
    fits = [t for t in range(8, min(rows, PIECE_ROWS_CAP) + 1, 8) if rows % t == 0]
    return max(fits) if fits else rows


def _pair_sum(full, got, core, name):
    n, r, c = got.shape
    tr = _row_tile(r)

    def body(core_ref, a_ref, b_ref, o_ref):
        o_ref[...] = (a_ref[...] + b_ref[...]).astype(BF16)

    blk = pl.BlockSpec((None, tr, c), lambda i, j, core_ref: (i, j, 0))
    grid_spec = pltpu.PrefetchScalarGridSpec(
        num_scalar_prefetch=1, grid=(n, r // tr),
        in_specs=[pl.BlockSpec((None, tr, c), lambda i, j, core_ref: (i, j, core_ref[0])), blk], out_specs=blk)
    return pl.pallas_call(body, name=name, grid_spec=grid_spec, out_shape=jax.ShapeDtypeStruct(got.shape, BF16),
                          compiler_params=_params(("parallel", "parallel")))(core, full, got)


def _chip_sum(full, got, land, place, name):
    n, r, c = land.shape
    tr = _row_tile(r)

    def body(place_ref, a_ref, b_ref, l_ref, o_ref):
        total = a_ref[...] + b_ref[...]
        for j in range(n):
            total = total + l_ref[j].astype(F32)
        o_ref[...] = total

    grid_spec = pltpu.PrefetchScalarGridSpec(
        num_scalar_prefetch=1, grid=(r // tr,),
        in_specs=[pl.BlockSpec((None, tr, c), lambda i, p: (p[0], i, p[1])),
                  pl.BlockSpec((None, tr, c), lambda i, p: (p[0], i, 0)),
                  pl.BlockSpec((n, tr, c), lambda i, p: (0, i, 0))],
        out_specs=pl.BlockSpec((tr, c), lambda i, p: (i, 0)))
    return pl.pallas_call(body, name=name, grid_spec=grid_spec, out_shape=jax.ShapeDtypeStruct((r, c), F32),
                          compiler_params=_params(("parallel",)))(place, full, got, land)


def _adamw(w, g, m, v, name):
    r, c = w.shape
    tr = _row_tile(r)
    tc = 1024 if c % 1024 == 0 else c

    def body(w_ref, g_ref, m_ref, v_ref, d_ref, mo_ref, vo_ref):
        g_ = g_ref[...]
        m2 = ADAM_B1 * m_ref[...] + (1.0 - ADAM_B1) * g_
        v2 = ADAM_B2 * v_ref[...] + (1.0 - ADAM_B2) * jnp.square(g_)
        m_hat = m2 / (1.0 - ADAM_B1 ** ADAM_STEP)
        v_hat = v2 / (1.0 - ADAM_B2 ** ADAM_STEP)
        d_ref[...] = -ADAM_LR * (m_hat / (jnp.sqrt(v_hat) + ADAM_EPS) + ADAM_WD * w_ref[...])
        mo_ref[...] = m2
        vo_ref[...] = v2

    blk = pl.BlockSpec((tr, tc), lambda i, j: (i, j))
    return pl.pallas_call(body, name=name, grid=(r // tr, c // tc), in_specs=[blk] * 4, out_specs=[blk] * 3,
                          out_shape=[jax.ShapeDtypeStruct(w.shape, F32)] * 3,
                          compiler_params=_params(("parallel", "parallel")))(w, g, m, v)


SMALL_NAMES = ("norm_w", "mem_norm_w", "o_norm_a", "q_norm_c", "k_norm_c", "q_norm_b", "k_norm_b",
               "a_log_fwd", "a_log_bwd", "dt_bias_fwd", "dt_bias_bwd", "sink_b")
SMALL_SIZES = (2048, 2048, 128, 128, 128, 64, 64, 8, 8, 8, 8, 8)
SMALL_LOSS = sum(SMALL_SIZES)
SMALL_CONV = 5120
SMALL_TOTAL = SMALL_CONV + CONV_K * 3 * A_WIDTH
SMALL_ROWS = SMALL_TOTAL // LANE


def _pack_small(parts, extra=None, conv=None):
    vec = [parts[n].reshape(-1) for n in SMALL_NAMES]
    vec.append(jnp.zeros((1,), F32) if extra is None else extra.reshape(1))
    vec.append(jnp.zeros((SMALL_CONV - SMALL_LOSS - 1,), F32))
    vec.append(jnp.zeros((SMALL_TOTAL - SMALL_CONV,), F32) if conv is None else conv.reshape(-1))
    return jnp.concatenate(vec).reshape(SMALL_ROWS, LANE)


def _unpack_small(packed):
    flat = packed.reshape(-1)
    out, off = {}, 0
    for n, size in zip(SMALL_NAMES, SMALL_SIZES):
        out[n] = flat[off:off + size].reshape(1, size)
        off += size
    return out


WEIGHT_ORDER = ("norm_w", "w_in", "conv_w_a", "a_log_fwd", "a_log_bwd", "dt_bias_fwd", "dt_bias_bwd", "o_norm_a",
                "q_norm_b", "k_norm_b", "sink_b", "mem_norm_w", "w_mem_kv", "q_norm_c", "k_norm_c", "w_out")


def kernel(x, mem, norm_w, w_in, conv_w_a, a_log_fwd, a_log_bwd, dt_bias_fwd, dt_bias_bwd, o_norm_a, q_norm_b, k_norm_b, sink_b, mem_norm_w, w_mem_kv, q_norm_c, k_norm_c, w_out, loss_target, m_norm_w, m_w_in, m_conv_w_a, m_a_log_fwd, m_a_log_bwd, m_dt_bias_fwd, m_dt_bias_bwd, m_o_norm_a, m_q_norm_b, m_k_norm_b, m_sink_b, m_mem_norm_w, m_w_mem_kv, m_q_norm_c, m_k_norm_c, m_w_out, v_norm_w, v_w_in, v_conv_w_a, v_a_log_fwd, v_a_log_bwd, v_dt_bias_fwd, v_dt_bias_bwd, v_o_norm_a, v_q_norm_b, v_k_norm_b, v_sink_b, v_mem_norm_w, v_w_mem_kv, v_q_norm_c, v_k_norm_c, v_w_out):
    weights = dict(norm_w=norm_w, w_in=w_in, conv_w_a=conv_w_a, a_log_fwd=a_log_fwd, a_log_bwd=a_log_bwd,
                   dt_bias_fwd=dt_bias_fwd, dt_bias_bwd=dt_bias_bwd, o_norm_a=o_norm_a, q_norm_b=q_norm_b,
                   k_norm_b=k_norm_b, sink_b=sink_b, mem_norm_w=mem_norm_w, w_mem_kv=w_mem_kv, q_norm_c=q_norm_c,
                   k_norm_c=k_norm_c, w_out=w_out)
    mom1 = dict(norm_w=m_norm_w, w_in=m_w_in, conv_w_a=m_conv_w_a, a_log_fwd=m_a_log_fwd, a_log_bwd=m_a_log_bwd,
                dt_bias_fwd=m_dt_bias_fwd, dt_bias_bwd=m_dt_bias_bwd, o_norm_a=m_o_norm_a, q_norm_b=m_q_norm_b,
                k_norm_b=m_k_norm_b, sink_b=m_sink_b, mem_norm_w=m_mem_norm_w, w_mem_kv=m_w_mem_kv,
                q_norm_c=m_q_norm_c, k_norm_c=m_k_norm_c, w_out=m_w_out)
    mom2 = dict(norm_w=v_norm_w, w_in=v_w_in, conv_w_a=v_conv_w_a, a_log_fwd=v_a_log_fwd, a_log_bwd=v_a_log_bwd,
                dt_bias_fwd=v_dt_bias_fwd, dt_bias_bwd=v_dt_bias_bwd, o_norm_a=v_o_norm_a, q_norm_b=v_q_norm_b,
                k_norm_b=v_k_norm_b, sink_b=v_sink_b, mem_norm_w=v_mem_norm_w, w_mem_kv=v_w_mem_kv,
                q_norm_c=v_q_norm_c, k_norm_c=v_k_norm_c, w_out=v_w_out)
    chip = 2 * lax.axis_index("x") + lax.axis_index("y")

    w_in4, conv4 = _all_gather_weights([jnp.transpose(w_in[0]).astype(BF16)], conv_w_a[0])
    w_perm_t = _permute_blocks(w_in4)
    w_blocks_t = jnp.pad(w_in4, ((0, 0), (0, W_IN_PAD - W_IN_BLOCK), (0, 0))).reshape(N_CHIPS * W_IN_PAD, D_MODEL)
    conv_full = jnp.transpose(conv4, (1, 0, 2)).reshape(CONV_K, 3 * A_WIDTH)
    own_out, own_kv = w_out[0].astype(BF16), w_mem_kv[0].astype(BF16)

    def assemble(w_out4, w_kv4):
        w_out4 = lax.dynamic_update_index_in_dim(w_out4, own_out, chip, 0)
        w_kv4 = lax.dynamic_update_index_in_dim(w_kv4, own_kv, chip, 0)
        return w_out4.reshape(D_MODEL, D_MODEL), w_kv4.reshape(D_MODEL, 2 * C_HEADS * C_DIM)

    gather = (_PairedGather([own_out, own_kv]), assemble)
    pa = jnp.concatenate([_pad_row(a_log_fwd), _pad_row(a_log_bwd), _pad_row(dt_bias_fwd), _pad_row(dt_bias_bwd),
                          _pad_row(o_norm_a), jnp.zeros((3, LANE), F32)], axis=0)
    pb = jnp.concatenate([_pad_row(q_norm_b), _pad_row(k_norm_b), _pad_row(sink_b), jnp.zeros((5, LANE), F32)], axis=0)
    pc = jnp.concatenate([_pad_row(q_norm_c), _pad_row(k_norm_c), jnp.zeros((6, LANE), F32)], axis=0)

    full, got = {}, {}
    core = lax.axis_index("c").astype(jnp.int32).reshape(1)

    def pair_round(tag, blocks):
        names = [tag + "_%d" % i for i in range(len(blocks))]
        full.update(zip(names, blocks))
        got.update(zip(names, _pair_exchange(blocks, "grad_pair_exchange_" + tag)))
        return _ChipExchange([_pair_sum(full[n], got[n], core, "grad_pair_sum_" + n) for n in names])

    def early(g_w_out, g_w_kv):
        return pair_round("early", [g_w_out.reshape(N_CHIPS, D_MODEL // N_CHIPS, D_MODEL),
                                    g_w_kv.reshape(N_CHIPS, D_MODEL // N_CHIPS, 2 * C_HEADS * C_DIM)])

    def late(g_w_blocks_t):
        return pair_round("late", [g_w_blocks_t.reshape(N_CHIPS, W_IN_PAD, D_MODEL)])

    r = _local_step(x[0], mem[0], loss_target[0], norm_w, w_perm_t, w_blocks_t, conv_full, pa, pb, pc, mem_norm_w, None, None,
                    gather, (early, late))
    place = jnp.stack([chip, lax.axis_index("c")]).astype(jnp.int32)
    reduced = [_chip_sum(full[n], got[n], l, place, "grad_chip_sum_" + n)
               for n, l in zip(("late_0", "early_0", "early_1"), r["landed"])]
    g_w_in_t, g_w_out, g_w_kv = _pair_gather(reduced, [W_IN_BLOCK, D_MODEL // N_CHIPS, D_MODEL // N_CHIPS])

    d_pa, d_pb, d_pc = r["d_pa"], r["d_pb"], r["d_pc"]
    small_g = dict(norm_w=r["g_norm"], mem_norm_w=r["g_mem_norm"], o_norm_a=d_pa[4], q_norm_c=d_pc[0], k_norm_c=d_pc[1],
                   q_norm_b=d_pb[0, :B_DIM], k_norm_b=d_pb[1, :B_DIM], a_log_fwd=d_pa[0, :A_HEADS],
                   a_log_bwd=d_pa[1, :A_HEADS], dt_bias_fwd=d_pa[2, :A_HEADS], dt_bias_bwd=d_pa[3, :A_HEADS],
                   sink_b=d_pb[2, :B_HEADS])
    packed = _all_reduce_small(_pack_small(small_g, jnp.sum(r["loss_parts"][:, 0, 0]), r["g_conv"]))
    flat = packed.reshape(-1)
    loss = flat[SMALL_LOSS]
    conv_sum = flat[SMALL_CONV:].reshape(CONV_K, 3 * A_WIDTH)
    conv_cols = 3 * A_WIDTH // N_CHIPS
    g_conv = lax.dynamic_slice(conv_sum, (0, chip * conv_cols), (CONV_K, conv_cols))

    grads = _unpack_small(packed)
    grads.update(w_in=jnp.transpose(g_w_in_t), w_mem_kv=g_w_kv, w_out=g_w_out, conv_w_a=g_conv)
    delta, new_m, new_v = {}, {}, {}
    for n in ("w_mem_kv", "w_out", "conv_w_a"):
        delta[n], new_m[n], new_v[n] = _adamw(weights[n][0], grads[n], mom1[n][0], mom2[n][0], "adamw_" + n)
    stepped = _adamw(jnp.transpose(w_in[0]), g_w_in_t, jnp.transpose(m_w_in[0]), jnp.transpose(v_w_in[0]), "adamw_w_in")
    delta["w_in"], new_m["w_in"], new_v["w_in"] = (jnp.transpose(t) for t in stepped)
    d_s, m_s, v_s = _adamw(_pack_small(weights), packed, _pack_small(mom1), _pack_small(mom2), "adamw_small")
    d_s, m_s, v_s = _unpack_small(d_s), _unpack_small(m_s), _unpack_small(v_s)
    for n in SMALL_NAMES:
        delta[n], new_m[n], new_v[n] = d_s[n], m_s[n], v_s[n]

    def shaped(tree):
        return [tree[n].reshape(weights[n].shape) for n in WEIGHT_ORDER]

    return (loss, r["g_x"].reshape(x.shape), *shaped(grads), *shaped(delta), *shaped(new_m), *shaped(new_v))
```

```python
import functools

import jax
import jax.numpy as jnp
from jax import lax
from jax.experimental import pallas as pl
from jax.experimental.pallas import tpu as pltpu

F32 = jnp.float32
BF16 = jnp.bfloat16
HI = lax.Precision.HIGHEST
MESH = pl.DeviceIdType.MESH

D_MODEL = 2048
A_WIDTH = 1024
A_HEADS = 8
A_DIM = 128
CONV_K = 5
CHUNK = 64
B_HEADS = 8
B_KV = 2
B_DIM = 64
WINDOW = 128
C_HEADS = 4
C_DIM = 128
MEM_LEN = 256
ROPE_THETA = 10000.0
EPS = 1e-6
IN_WIDTH = 6432
N_CHIPS = 4
W_IN_BLOCK = IN_WIDTH // N_CHIPS
W_IN_PAD = 1664

LANE = 128
P_QA, P_KA, P_VA, P_ZA = 0, 1024, 2048, 3072
P_QB, P_ZB, P_QC, P_ZC = 4096, 4608, 5120, 5632
P_KB, P_VB, P_GT = 6144, 6272, 6400
P_WIDTH = 6656
O_GT, O_QB, O_KB, O_VB, O_ZB, O_QC, O_ZC = 4096, 4128, 4640, 4768, 4896, 5408, 5920

ADAM_LR, ADAM_B1, ADAM_B2, ADAM_EPS, ADAM_WD, ADAM_STEP = 0.001, 0.9, 0.999, 1e-08, 0.01, 10

VMEM_LIMIT = 56 * 1024 * 1024


def _params(sem=None):
    return pltpu.CompilerParams(dimension_semantics=sem, vmem_limit_bytes=VMEM_LIMIT)


def _dot(a, b, dims=(((1,), (0,)), ((), ())), precision=HI):
    return lax.dot_general(a, b, dims, precision=precision, preferred_element_type=F32)


def _dot_nt(a, b, precision=HI):
    return _dot(a, b, (((1,), (1,)), ((), ())), precision)


def _dot_tn(a, b, precision=HI):
    return _dot(a, b, (((0,), (0,)), ((), ())), precision)


_NN = (((1,), (0,)), ((), ()))
_NT = (((1,), (1,)), ((), ()))
_TN = (((0,), (0,)), ((), ()))


def _bdot(a, b, dims):
    return lax.dot_general(a.astype(BF16), b.astype(BF16), dims, preferred_element_type=F32)


@jax.custom_vjp
def _mm(a, b):
    return _bdot(a, b, _NN)


_mm.defvjp(lambda a, b: (_bdot(a, b, _NN), (a, b)),
           lambda res, ct: (_bdot(ct, res[1], _NT), _bdot(res[0], ct, _TN)))


@jax.custom_vjp
def _mm_nt(a, b):
    return _bdot(a, b, _NT)


_mm_nt.defvjp(lambda a, b: (_bdot(a, b, _NT), (a, b)),
              lambda res, ct: (_bdot(ct, res[1], _NN), _bdot(ct, res[0], _TN)))


@jax.custom_vjp
def _mm_tn(a, b):
    return _bdot(a, b, _TN)


_mm_tn.defvjp(lambda a, b: (_bdot(a, b, _TN), (a, b)),
              lambda res, ct: (_bdot(res[1], ct, _NT), _bdot(res[0], ct, _NN)))


def _rms(t, w):
    return t * lax.rsqrt(jnp.mean(t * t, axis=-1, keepdims=True) + EPS) * w


def _l2(t):
    return t * lax.rsqrt(jnp.sum(t * t, axis=-1, keepdims=True) + EPS)


def _silu(t):
    return t * jax.nn.sigmoid(t)


def _softplus(t):
    return jnp.maximum(t, 0.0) + jnp.log1p(jnp.exp(-jnp.abs(t)))


def _matmul(a, b, mode, out_dtype, name, tm=512, tn=512, tk=512, ride=None):
    (m, k) = a.shape[::-1] if mode == "tn" else a.shape
    n = b.shape[0] if mode == "nt" else b.shape[1]
    tm, tn, tk = min(tm, m), min(tn, n), min(tk, k)
    assert m % tm == 0 and n % tn == 0 and k % tk == 0, (m, n, k, tm, tn, tk)
    if mode == "nn":
        a_spec = pl.BlockSpec((tm, tk), lambda i, j, kk: (i, kk))
        b_spec = pl.BlockSpec((tk, tn), lambda i, j, kk: (kk, j))
        dims = (((1,), (0,)), ((), ()))
    elif mode == "nt":
        a_spec = pl.BlockSpec((tm, tk), lambda i, j, kk: (i, kk))
        b_spec = pl.BlockSpec((tn, tk), lambda i, j, kk: (j, kk))
        dims = (((1,), (1,)), ((), ()))
    else:
        a_spec = pl.BlockSpec((tk, tm), lambda i, j, kk: (kk, i))
        b_spec = pl.BlockSpec((tk, tn), lambda i, j, kk: (kk, j))
        dims = (((0,), (0,)), ((), ()))
    nk = k // tk
    grid = (m // tm, n // tn, nk)
    n_in = len(ride.operands) if ride else 0
    n_out = len(ride.out_shapes) if ride else 0

    def body(*refs):
        a_ref, b_ref, o_ref = refs[0], refs[1], refs[2 + n_in]
        scratch = refs[3 + n_in + n_out:]
        step = (pl.program_id(0) * grid[1] + pl.program_id(1)) * nk + pl.program_id(2)
        riders = (refs[2:2 + n_in], refs[3 + n_in:3 + n_in + n_out], scratch[(0 if nk == 1 else 1):])
        if ride:
            pl.when(step == 0)(lambda: ride.start(*riders))
        if nk == 1:
            o_ref[...] = _bdot(a_ref[...], b_ref[...], dims).astype(out_dtype)
        else:
            acc_ref, kk = scratch[0], pl.program_id(2)

            @pl.when(kk == 0)
            def _():
                acc_ref[...] = jnp.zeros_like(acc_ref)

            acc_ref[...] += _bdot(a_ref[...], b_ref[...], dims)

            @pl.when(kk == nk - 1)
            def _():
                o_ref[...] = acc_ref[...].astype(out_dtype)
        if ride:
            pl.when(step == grid[0] * grid[1] * nk - 1)(lambda: ride.finish(*riders))

    out = pl.pallas_call(
        body, name=name, grid=grid,
        in_specs=[a_spec, b_spec] + [HBM] * n_in,
        out_specs=[pl.BlockSpec((tm, tn), lambda i, j, kk: (i, j))] + [HBM] * n_out,
        out_shape=[jax.ShapeDtypeStruct((m, n), out_dtype)] + (list(ride.out_shapes) if ride else []),
        scratch_shapes=([] if nk == 1 else [pltpu.VMEM((tm, tn), F32)]) + (list(ride.scratch_shapes) if ride else []),
        compiler_params=_params(("arbitrary",) * 3 if ride else ("parallel", "parallel", "arbitrary")),
    )(a, b, *(ride.operands if ride else []))
    return out if ride else out[0]


def _rms_fwd(x, w, tr=256):
    s, d = x.shape

    def body(x_ref, w_ref, o_ref):
        o_ref[...] = _rms(x_ref[...], w_ref[...]).astype(BF16)

    return pl.pallas_call(
        body, name="rms_fwd", grid=(s // tr,),
        in_specs=[pl.BlockSpec((tr, d), lambda i: (i, 0)), pl.BlockSpec((1, d), lambda i: (0, 0))],
        out_specs=pl.BlockSpec((tr, d), lambda i: (i, 0)),
        out_shape=jax.ShapeDtypeStruct((s, d), BF16), compiler_params=_params(("parallel",)),
    )(x, w)


def _rms_bwd(x, w, d_hn, dy, tr=256):
    s, d = x.shape

    def body(x_ref, w_ref, g_ref, dy_ref, gx_ref, gw_ref):
        _, vjp = jax.vjp(_rms, x_ref[...], w_ref[...])
        dx, dw = vjp(g_ref[...])
        gx_ref[...] = dy_ref[...] + dx

        @pl.when(pl.program_id(0) == 0)
        def _():
            gw_ref[...] = jnp.zeros_like(gw_ref)

        gw_ref[...] += dw

    row = pl.BlockSpec((tr, d), lambda i: (i, 0))
    vec = pl.BlockSpec((1, d), lambda i: (0, 0))
    return pl.pallas_call(
        body, name="rms_bwd", grid=(s // tr,), in_specs=[row, vec, row, row], out_specs=[row, vec],
        out_shape=[jax.ShapeDtypeStruct((s, d), F32), jax.ShapeDtypeStruct((1, d), F32)],
        compiler_params=_params(("arbitrary",)),
    )(x, w, d_hn, dy)


def _loss_dy(x, mo, target, tr=256):
    s, d = x.shape
    nt = s // tr

    def body(x_ref, mo_ref, t_ref, dy_ref, dyb_ref, l_ref):
        err = x_ref[...] + mo_ref[...] - t_ref[...]
        dy = err * (1.0 / d)
        dy_ref[...] = dy
        dyb_ref[...] = dy.astype(BF16)
        l_ref[...] = jnp.full(l_ref.shape, 0.5 * jnp.sum(jnp.sum(err * err, axis=1, keepdims=True) * (1.0 / d)), F32)

    row = pl.BlockSpec((tr, d), lambda i: (i, 0))
    return pl.pallas_call(
        body, name="loss_dy", grid=(nt,), in_specs=[row, row, row],
        out_specs=[row, row, pl.BlockSpec((1, 8, LANE), lambda i: (i, 0, 0))],
        out_shape=[jax.ShapeDtypeStruct((s, d), F32), jax.ShapeDtypeStruct((s, d), BF16),
                   jax.ShapeDtypeStruct((nt, 8, LANE), F32)],
        compiler_params=_params(("parallel",)),
    )(x, mo, target)


def _shift_rows(t, s):
    if s == 0:
        return t
    n = t.shape[0]
    rolled = pltpu.roll(t, (-s) % n, axis=0)
    idx = lax.broadcasted_iota(jnp.int32, t.shape, 0) + s
    return jnp.where((idx >= 0) & (idx < n), rolled, 0.0)


def _conv_fwd(proj, conv_w):
    s = proj.shape[0]
    nblk = 3 * A_WIDTH // LANE

    def body(x_ref, w_ref, o_ref):
        x = x_ref[...]
        acc = jnp.zeros_like(x)
        for j in range(CONV_K):
            acc = acc + w_ref[j:j + 1, :] * _shift_rows(x, j - CONV_K // 2)
        o_ref[...] = acc

    return pl.pallas_call(
        body, name="conv_fwd", grid=(nblk,),
        in_specs=[pl.BlockSpec((s, LANE), lambda i: (0, i)), pl.BlockSpec((CONV_K, LANE), lambda i: (0, i))],
        out_specs=pl.BlockSpec((None, s, LANE), lambda i: (i // A_HEADS, 0, i % A_HEADS)),
        out_shape=jax.ShapeDtypeStruct((3, s, A_WIDTH), F32), compiler_params=_params(("parallel",)),
    )(proj, conv_w)


def _conv_bwd(proj, conv_w, d_c):
    s = proj.shape[0]
    nblk = 3 * A_WIDTH // LANE

    def body(x_ref, w_ref, g_ref, dx_ref, dw_ref):
        x, g = x_ref[...], g_ref[...]
        acc = jnp.zeros_like(x)
        for j in range(CONV_K):
            off = j - CONV_K // 2
            acc = acc + w_ref[j:j + 1, :] * _shift_rows(g, -off)
            dw_ref[j:j + 1, :] = jnp.sum(_shift_rows(x, off) * g, axis=0, keepdims=True)
        dx_ref[...] = acc.astype(BF16)

    col = pl.BlockSpec((s, LANE), lambda i: (0, i))
    wsp = pl.BlockSpec((CONV_K, LANE), lambda i: (0, i))
    dsp = pl.BlockSpec((None, s, LANE), lambda i: (i // A_HEADS, 0, i % A_HEADS))
    return pl.pallas_call(
        body, name="conv_bwd", grid=(nblk,), in_specs=[col, wsp, dsp], out_specs=[col, wsp],
        out_shape=[jax.ShapeDtypeStruct((s, 3 * A_WIDTH), BF16), jax.ShapeDtypeStruct((CONV_K, 3 * A_WIDTH), F32)],
        compiler_params=_params(("parallel",)),
    )(proj, conv_w, d_c)


A_FWD_HEADS = 4
A_BWD_HEADS = 4


def _neumann_inverse(a):
    c = a.shape[-1]
    eye = (lax.broadcasted_iota(jnp.int32, (c, c), 0) == lax.broadcasted_iota(jnp.int32, (c, c), 1)).astype(F32)
    tinv = eye + a
    p = a
    for _ in range(5):
        p = _mm(p, p)
        tinv = tinv + _mm(tinv, p)
    return tinv


@jax.custom_vjp
def _unit_inverse(a):
    return _neumann_inverse(a)


def _unit_inverse_fwd(a):
    tinv = _neumann_inverse(a)
    return tinv, tinv


def _unit_inverse_bwd(tinv, ct):
    return (_bdot(_bdot(tinv, ct, _TN), tinv, _NT),)


_unit_inverse.defvjp(_unit_inverse_fwd, _unit_inverse_bwd)


def _a_chain(st, cq, ck, cv, alpha, beta_raw, a_log, dt_b, incl, strict, last):
    c = CHUNK
    gb = -jnp.exp(a_log) * _softplus(alpha + dt_b)
    bb = jax.nn.sigmoid(beta_raw)
    q = _l2(_silu(cq)) * (A_DIM ** -0.5)
    k = _l2(_silu(ck))
    v = _silu(cv)

    gc = _dot(incl, jnp.broadcast_to(gb, (c, LANE)))
    tot = jnp.sum(gc * last, axis=0, keepdims=True)
    m1 = gc[:, :c]
    decay = incl * jnp.exp(incl * (m1 - m1.T))
    kb = k * bb
    vb = v * bb
    a = -(strict * decay * _mm_nt(kb, k))
    tinv = _unit_inverse(a)
    eg = jnp.exp(gc)
    u = _mm(tinv, vb)
    w = _mm(tinv, kb * eg)
    qk = _mm_nt(q, k) * decay
    v_new = u - _mm(w, st)
    o = _mm(q * eg, st) + _mm(qk, v_new)
    st_new = st * jnp.exp(tot) + _mm_tn(k * jnp.exp(tot - gc), v_new)
    return st_new, o


def _a_step(sts, cq, ck, cv, gts, pa, h0):
    c = CHUNK
    lane = lax.broadcasted_iota(jnp.int32, (1, LANE), 1)
    ii = lax.broadcasted_iota(jnp.int32, (c, c), 0)
    jj = lax.broadcasted_iota(jnp.int32, (c, c), 1)
    row = lax.broadcasted_iota(jnp.int32, (c, 1), 0)

    def pick(t, col):
        return jnp.sum(jnp.where(lane == col, t, 0.0), axis=1, keepdims=True)

    alpha, beta_raw, a_log, dt_b, incl, strict, last = [], [], [], [], [], [], []
    for b in range(sts.shape[0]):
        h, rev = h0 + b // 2, b % 2
        alpha.append(pick(gts[b], h + 8 * rev))
        beta_raw.append(pick(gts[b], h + 16 + 8 * rev))
        a_log.append(pick(pa[rev:rev + 1, :], h))
        dt_b.append(pick(pa[2 + rev:3 + rev, :], h))
        incl.append(((ii <= jj) if rev else (ii >= jj)).astype(F32))
        strict.append(((ii < jj) if rev else (ii > jj)).astype(F32))
        last.append((row == (0 if rev else c - 1)).astype(F32))
    stack = lambda ts: jnp.concatenate([t[None] for t in ts], axis=0)
    return jax.vmap(_a_chain)(sts, cq, ck, cv, stack(alpha), stack(beta_raw), stack(a_log), stack(dt_b),
                              stack(incl), stack(strict), stack(last))


def _a_final(o, za, pa):
    outs = []
    for j in range(o.shape[1] // A_DIM):
        ln = slice(j * A_DIM, (j + 1) * A_DIM)
        outs.append(_rms(o[:, ln], pa[4:5, :]) * _silu(za[:, ln]))
    return jnp.concatenate(outs, axis=1)


def _a_tiles(n, nchunk, heads):
    tiles = []
    for b in range(2 * heads):
        i = (nchunk - 1 - n) if b % 2 else n
        tiles.append((i, pl.ds(pl.multiple_of(i * CHUNK, CHUNK), CHUNK), slice((b // 2) * A_DIM, (b // 2 + 1) * A_DIM)))
    return tiles


def _a_load(tiles, c_ref, gt_ref):
    cq, ck, cv = (jnp.stack([c_ref[r, sl, ln] for _, sl, ln in tiles], axis=0) for r in range(3))
    return cq, ck, cv, jnp.stack([gt_ref[sl, :] for _, sl, _ in tiles], axis=0)


def _loop_by_two(n, step, init):
    assert n % 2 == 0
    return lax.fori_loop(0, n // 2, lambda m, carry: step(2 * m + 1, step(2 * m, carry, 0), 1), init)


def _a_scan(h0, heads, nchunk, c_ref, gt_ref, pa, of_ref, ob_ref, s_ref):
    def step(n, sts, parity):
        tiles = _a_tiles(n, nchunk, heads)
        sts_new, o = _a_step(sts, *_a_load(tiles, c_ref, gt_ref), pa, h0)
        for b, (i, sl, ln) in enumerate(tiles):
            s_ref[b, i] = sts[b]
            (ob_ref if b % 2 else of_ref)[sl, ln] = o[b]
        return sts_new

    _loop_by_two(nchunk, step, jnp.zeros((2 * heads, A_DIM, A_DIM), F32))


def _a_specs(s, heads):
    wide = heads * A_DIM
    once = pl.Buffered(1)
    trio = pl.BlockSpec((3, s, wide), lambda g: (0, 0, g), pipeline_mode=once)
    gates = pl.BlockSpec((s, LANE), lambda g: (0, P_GT // LANE))
    small = pl.BlockSpec((8, LANE), lambda g: (0, 0))

    def cols(base):
        return pl.BlockSpec((s, wide), lambda g: (0, base // wide + g), pipeline_mode=once)

    state = pl.BlockSpec((2 * heads, s // CHUNK, A_DIM, A_DIM), lambda g: (g, 0, 0, 0), pipeline_mode=once)
    return wide, trio, gates, small, cols, state


def _delta_fwd(cqkv, proj, pa, ride=None):
    s = cqkv.shape[1]
    nchunk = s // CHUNK
    heads = A_FWD_HEADS
    steps = A_HEADS // heads
    wide, trio, gates, small, cols, state = _a_specs(s, heads)
    n_in = len(ride.operands) if ride else 0
    n_out = len(ride.out_shapes) if ride else 0

    def body(*refs):
        c_ref, gt_ref, za_ref, pa_ref = refs[:4]
        out_ref, o_ref, s_ref = refs[4 + n_in:7 + n_in]
        ob_ref = refs[7 + n_in + n_out]
        riders = (refs[4:4 + n_in], refs[7 + n_in:7 + n_in + n_out], refs[8 + n_in + n_out:])
        g = pl.program_id(0)
        if ride:
            pl.when(g == 0)(lambda: ride.start(*riders))
            pl.when(g == steps - 1)(lambda: ride.middle(*riders))
        h0 = g * heads
        pa_v = pa_ref[...]
        _a_scan(h0, heads, nchunk, c_ref, gt_ref, pa_v, o_ref, ob_ref, s_ref)
        o_ref[...] += ob_ref[...]
        out_ref[...] = _a_final(o_ref[...], za_ref[...], pa_v).astype(BF16)
        if ride:
            pl.when(g == steps - 1)(lambda: ride.finish(*riders))

    assert steps > 1
    return pl.pallas_call(
        body, name="delta_fwd", grid=(steps,),
        in_specs=[trio, gates, cols(P_ZA), small] + [HBM] * n_in, out_specs=[cols(0), cols(0), state] + [HBM] * n_out,
        out_shape=[jax.ShapeDtypeStruct((s, D_MODEL), BF16),
                   jax.ShapeDtypeStruct((s, A_WIDTH), F32),
                   jax.ShapeDtypeStruct((2 * A_HEADS, nchunk, A_DIM, A_DIM), F32)]
        + (list(ride.out_shapes) if ride else []),
        scratch_shapes=[pltpu.VMEM((s, wide), F32)] + (list(ride.scratch_shapes) if ride else []),
        compiler_params=_params(("arbitrary",)),
    )(cqkv, proj, proj, pa, *(ride.operands if ride else []))


def _delta_out_bwd(o_sum, proj, pa, d_mixed, tr=256):
    s = o_sum.shape[0]

    def body(o_ref, za_ref, pa_ref, dm_ref, do_ref, dza_ref, dpa_ref):
        @pl.when(pl.program_id(0) == 0)
        def _():
            dpa_ref[...] = jnp.zeros_like(dpa_ref)

        _, vjp = jax.vjp(_a_final, o_ref[...], za_ref[...], pa_ref[...])
        d_o, d_za, dpa = vjp(dm_ref[...].astype(F32))
        do_ref[...] = d_o
        dza_ref[...] = d_za.astype(BF16)
        dpa_ref[...] += dpa

    def rows(col):
        return pl.BlockSpec((tr, A_WIDTH), lambda i: (i, col))

    small = pl.BlockSpec((8, LANE), lambda i: (0, 0))
    return pl.pallas_call(
        body, name="delta_out_bwd", grid=(s // tr,), in_specs=[rows(0), rows(P_ZA // A_WIDTH), small, rows(0)],
        out_specs=[rows(0), rows(0), small],
        out_shape=[jax.ShapeDtypeStruct((s, A_WIDTH), F32), jax.ShapeDtypeStruct((s, A_WIDTH), BF16),
                   jax.ShapeDtypeStruct((8, LANE), F32)],
        compiler_params=_params(("arbitrary",)),
    )(o_sum, proj, pa, d_mixed)


def _delta_bwd(cqkv, proj, pa, d_o, states, ride=None):
    s = cqkv.shape[1]
    nchunk = s // CHUNK
    heads = A_BWD_HEADS
    steps = A_HEADS // heads
    wide, trio, gates, small, cols, state = _a_specs(s, heads)
    n_in = len(ride.operands) if ride else 0
    n_out = len(ride.out_shapes) if ride else 0

    def body(*refs):
        c_ref, gt_ref, pa_ref, do_ref, s_hbm = refs[:5]
        dc_ref, dgt_ref, dpa_ref = refs[5 + n_in:8 + n_in]
        s_buf, s_sems = refs[8 + n_in + n_out:10 + n_in + n_out]
        riders = (refs[5:5 + n_in], refs[8 + n_in:8 + n_in + n_out], refs[10 + n_in + n_out:])
        if ride:
            pl.when(pl.program_id(0) == 0)(lambda: ride.start(*riders))
        h0 = pl.program_id(0) * heads
        pa_v = pa_ref[...]

        @pl.when(h0 == 0)
        def _():
            dgt_ref[...] = jnp.zeros_like(dgt_ref)
            dpa_ref[...] = jnp.zeros_like(dpa_ref)

        dc_ref[...] = jnp.zeros_like(dc_ref)

        def state_copies(n, slot):
            return [pltpu.make_async_copy(s_hbm.at[2 * h0 + b, i], s_buf.at[slot, b], s_sems.at[slot, b])
                    for b, (i, _, _) in enumerate(_a_tiles(nchunk - 1 - n, nchunk, heads))]

        for cp in state_copies(0, 0):
            cp.start()

        def step(n, carry, parity):
            d_sts, dpa = carry
            tiles = _a_tiles(nchunk - 1 - n, nchunk, heads)
            for cp in state_copies(n, parity):
                cp.wait()

            @pl.when(n + 1 < nchunk)
            def _():
                for cp in state_copies(n + 1, 1 - parity):
                    cp.start()

            sts = s_buf[parity]
            d_o_t = jnp.stack([do_ref[sl, ln] for _, sl, ln in tiles], axis=0)
            _, vjp_c = jax.vjp(lambda *a: _a_step(*a, h0), sts, *_a_load(tiles, c_ref, gt_ref), pa_v)
            d_prev, dcq, dck, dcv, dgts, dpa_i = vjp_c((d_sts, d_o_t))
            for b, (_, sl, ln) in enumerate(tiles):
                for r, dc in enumerate((dcq, dck, dcv)):
                    dc_ref[r, sl, ln] += dc[b]
                dgt_ref[sl, :] += dgts[b]
            return d_prev, dpa + dpa_i

        init = (jnp.zeros((2 * heads, A_DIM, A_DIM), F32), jnp.zeros((8, LANE), F32))
        _, dpa_out = lax.fori_loop(0, nchunk, lambda n, carry: step(n, carry, n % 2), init)
        dpa_ref[...] += dpa_out
        if ride:
            pl.when(pl.program_id(0) == steps - 1)(lambda: ride.finish(*riders))

    fixed = pl.BlockSpec((s, LANE), lambda g: (0, 0))
    return pl.pallas_call(
        body, name="delta_bwd", grid=(steps,),
        in_specs=[trio, gates, small, cols(0), pl.BlockSpec(memory_space=pl.ANY)] + [HBM] * n_in,
        out_specs=[trio, fixed, small] + [HBM] * n_out,
        out_shape=[jax.ShapeDtypeStruct((3, s, A_WIDTH), F32), jax.ShapeDtypeStruct((s, LANE), F32),
                   jax.ShapeDtypeStruct((8, LANE), F32)] + (list(ride.out_shapes) if ride else []),
        scratch_shapes=[pltpu.VMEM((2, 2 * heads, A_DIM, A_DIM), F32), pltpu.SemaphoreType.DMA((2, 2 * heads))]
        + (list(ride.scratch_shapes) if ride else []),
        compiler_params=_params(("arbitrary",)),
    )(cqkv, proj, pa, d_o, states, *(ride.operands if ride else []))


def _rope_tables(s):
    inv = ROPE_THETA ** (-jnp.arange(0, B_DIM, 2, dtype=F32) / B_DIM)
    ang = jnp.arange(s, dtype=F32)[:, None] * inv[None, :]
    cos, sin = jnp.cos(ang), jnp.sin(ang)
    return jnp.concatenate([cos, cos], axis=1), jnp.concatenate([-sin, sin], axis=1)


def _b_block(q_t, z_t, k3, v3, cos_q, sin_q, cos_k, sin_k, pb, n, nb):
    w = WINDOW
    def swap(t):
        return jnp.concatenate([t[:, B_DIM // 2:], t[:, :B_DIM // 2]], axis=1)

    grp = B_HEADS // B_KV
    qi = lax.broadcasted_iota(jnp.int32, (grp * w, 3 * w), 0) & (w - 1)
    kj = lax.broadcasted_iota(jnp.int32, (grp * w, 3 * w), 1)
    kpos = kj + (n - 1) * w
    mask = (jnp.abs(kj - w - qi) <= w) & (kpos >= 0) & (kpos < nb * w)
    lane = lax.broadcasted_iota(jnp.int32, (1, LANE), 1)
    qn, kn = pb[0:1, :B_DIM], pb[1:2, :B_DIM]
    cos_g = jnp.concatenate([cos_q] * grp, axis=0)
    sin_g = jnp.concatenate([sin_q] * grp, axis=0)
    def group(q, k, v, sink):
        k = _rms(k, kn)
        k = k * cos_k + swap(k) * sin_k
        q = _rms(q, qn)
        q = q * cos_g + swap(q) * sin_g
        s = _mm_nt(q, k) * (B_DIM ** -0.5)
        s = jnp.where(mask, s, -jnp.inf)
        m = jnp.maximum(jnp.max(s, axis=1, keepdims=True), sink)
        p = jnp.exp(s - m)
        p = p / (jnp.sum(p, axis=1, keepdims=True) + jnp.exp(sink - m))
        return _mm(p, v)

    stack = lambda ts: jnp.concatenate([t[None] for t in ts], axis=0)
    qs, ks, vs, sinks = [], [], [], []
    for hk in range(B_KV):
        heads = [hk * grp + g for g in range(grp)]
        ks.append(k3[:, hk * B_DIM:(hk + 1) * B_DIM])
        vs.append(v3[:, hk * B_DIM:(hk + 1) * B_DIM])
        qs.append(jnp.concatenate([q_t[:, hq * B_DIM:(hq + 1) * B_DIM] for hq in heads], axis=0))
        sinks.append(jnp.concatenate(
            [jnp.broadcast_to(jnp.sum(jnp.where(lane == hq, pb[2:3, :], 0.0), axis=1, keepdims=True), (w, 1))
             for hq in heads], axis=0))
    o = jax.vmap(group)(stack(qs), stack(ks), stack(vs), stack(sinks))
    outs = [o[hk, g * w:(g + 1) * w, :] for hk in range(B_KV) for g in range(grp)]
    return jnp.concatenate(outs, axis=1) * _silu(z_t)


def _b_specs(s):
    nb = s // WINDOW
    qsp = pl.BlockSpec((WINDOW, 512), lambda n: (n, P_QB // 512))
    zsp = pl.BlockSpec((WINDOW, 512), lambda n: (n, P_ZB // 512))

    def three(col, width):
        return [pl.BlockSpec((WINDOW, width), lambda n: (jnp.maximum(n - 1, 0), col)),
                pl.BlockSpec((WINDOW, width), lambda n: (n, col)),
                pl.BlockSpec((WINDOW, width), lambda n: (jnp.minimum(n + 1, nb - 1), col))]

    tab = pl.BlockSpec((WINDOW, B_DIM), lambda n: (n, 0))
    small = pl.BlockSpec((8, LANE), lambda n: (0, 0))
    specs = [qsp, zsp] + three(P_KB // LANE, LANE) + three(P_VB // LANE, LANE) + [tab, tab] + three(0, B_DIM) + three(0, B_DIM) + [small]
    return nb, specs


def _b_args(proj, cos2, sin2, pb):
    return (proj, proj, proj, proj, proj, proj, proj, proj, cos2, sin2, cos2, cos2, cos2, sin2, sin2, sin2, pb)


def _b_load(refs):
    (q_ref, z_ref, kp, kc, kx, vp, vc, vx, cq, sq, ckp, ckc, ckx, skp, skc, skx, pb_ref) = refs
    cat = lambda *r: jnp.concatenate([t[...] for t in r], axis=0)
    return (q_ref[...], z_ref[...], cat(kp, kc, kx), cat(vp, vc, vx), cq[...], sq[...], cat(ckp, ckc, ckx),
            cat(skp, skc, skx), pb_ref[...])


def _attn_b_fwd(proj, cos2, sin2, pb, mixed):
    s = proj.shape[0]
    nb, specs = _b_specs(s)

    def body(*refs):
        o_ref = refs[-1]
        args = _b_load(refs[:-2])
        o_ref[...] = _b_block(*args, pl.program_id(0), nb).astype(BF16)

    return pl.pallas_call(
        body, name="attn_b_fwd", grid=(nb,), in_specs=specs + [pl.BlockSpec(memory_space=pl.ANY)],
        out_specs=pl.BlockSpec((WINDOW, 512), lambda n: (n, A_WIDTH // 512)),
        out_shape=jax.ShapeDtypeStruct(mixed.shape, mixed.dtype), input_output_aliases={len(specs): 0},
        compiler_params=_params(("parallel",)),
    )(*_b_args(proj, cos2, sin2, pb), mixed)


def _attn_b_bwd(proj, cos2, sin2, pb, d_mixed):
    s = proj.shape[0]
    nb, specs = _b_specs(s)
    w = WINDOW

    def body(*refs):
        dm_ref, dq_ref, dz_ref, dk_ref, dv_ref, dpb_ref = refs[-6:]
        n = pl.program_id(0)
        q_t, z_t, k3, v3, cq, sq, ck, sk, pb_v = _b_load(refs[:-6])

        @pl.when(n == 0)
        def _():
            dk_ref[...] = jnp.zeros_like(dk_ref)
            dv_ref[...] = jnp.zeros_like(dv_ref)
            dpb_ref[...] = jnp.zeros_like(dpb_ref)

        def f(q_, z_, k_, v_, pb_):
            return _b_block(q_, z_, k_, v_, cq, sq, ck, sk, pb_, n, nb)

        _, vjp = jax.vjp(f, q_t, z_t, k3, v3, pb_v)
        dq, dz, dk3, dv3, dpb = vjp(dm_ref[...])
        dq_ref[...] = dq.astype(BF16)
        dz_ref[...] = dz.astype(BF16)
        dpb_ref[...] += dpb

        def add(j, cond):
            @pl.when(cond)
            def _():
                rows = pl.ds(pl.multiple_of((n - 1 + j) * w, w), w)
                dk_ref[rows, :] += dk3[j * w:(j + 1) * w, :]
                dv_ref[rows, :] += dv3[j * w:(j + 1) * w, :]

        add(0, n > 0)
        add(1, n >= 0)
        add(2, n < nb - 1)

    blk = pl.BlockSpec((w, 512), lambda n: (n, 0))
    whole = pl.BlockSpec((s, LANE), lambda n: (0, 0))
    small = pl.BlockSpec((8, LANE), lambda n: (0, 0))
    return pl.pallas_call(
        body, name="attn_b_bwd", grid=(nb,),
        in_specs=specs + [pl.BlockSpec((w, 512), lambda n: (n, 2))],
        out_specs=[blk, blk, whole, whole, small],
        out_shape=[jax.ShapeDtypeStruct((s, 512), BF16), jax.ShapeDtypeStruct((s, 512), BF16),
                   jax.ShapeDtypeStruct((s, LANE), F32), jax.ShapeDtypeStruct((s, LANE), F32),
                   jax.ShapeDtypeStruct((8, LANE), F32)],
        compiler_params=_params(("arbitrary",)),
    )(*_b_args(proj, cos2, sin2, pb), d_mixed)


def _mem_kv_fwd(mem, mem_norm_w, w_kv):
    def body(mem_ref, nw_ref, w_ref, kv_ref):
        mn = _rms(mem_ref[...], nw_ref[...]).astype(BF16)
        kv_ref[...] = jnp.dot(mn, w_ref[...], preferred_element_type=F32)

    return pl.pallas_call(
        body, name="mem_kv_fwd", out_shape=jax.ShapeDtypeStruct((MEM_LEN, 2 * C_HEADS * C_DIM), F32),
        compiler_params=_params(),
    )(mem, mem_norm_w, w_kv)


def _mem_kv_bwd(mem, mem_norm_w, w_kv, d_kv):
    def body(mem_ref, nw_ref, w_ref, g_ref, gw_ref, gn_ref):
        mn, vjp = jax.vjp(_rms, mem_ref[...], nw_ref[...])
        g = g_ref[...].astype(BF16)
        gw_ref[...] = lax.dot_general(mn.astype(BF16), g, (((0,), (0,)), ((), ())), preferred_element_type=F32)
        d_mn = lax.dot_general(g, w_ref[...], (((1,), (1,)), ((), ())), preferred_element_type=F32)
        gn_ref[...] = vjp(d_mn)[1]

    return pl.pallas_call(
        body, name="mem_kv_bwd",
        out_shape=[jax.ShapeDtypeStruct((D_MODEL, 2 * C_HEADS * C_DIM), F32), jax.ShapeDtypeStruct((1, D_MODEL), F32)],
        compiler_params=_params(),
    )(mem, mem_norm_w, w_kv, d_kv)


def _c_tile(q_t, z_t, kvm, pc):
    width = C_HEADS * C_DIM
    outs = []
    for h in range(C_HEADS):
        q = _rms(q_t[:, h * C_DIM:(h + 1) * C_DIM], pc[0:1, :])
        k = _rms(kvm[:, h * C_DIM:(h + 1) * C_DIM], pc[1:2, :])
        v = kvm[:, width + h * C_DIM:width + (h + 1) * C_DIM]
        s = _mm_nt(q, k) * (C_DIM ** -0.5)
        p = jnp.exp(s - jnp.max(s, axis=1, keepdims=True))
        p = p / jnp.sum(p, axis=1, keepdims=True)
        outs.append(_mm(p, v))
    return jnp.concatenate(outs, axis=1) * _silu(z_t)


def _attn_c_fwd(proj, kvm, pc, mixed, tq=256):
    s = proj.shape[0]

    def body(q_ref, z_ref, kv_ref, pc_ref, mixed_ref, o_ref):
        o_ref[...] = _c_tile(q_ref[...], z_ref[...], kv_ref[...], pc_ref[...]).astype(BF16)

    return pl.pallas_call(
        body, name="attn_c_fwd", grid=(s // tq,),
        in_specs=[pl.BlockSpec((tq, 512), lambda i: (i, P_QC // 512)), pl.BlockSpec((tq, 512), lambda i: (i, P_ZC // 512)),
                  pl.BlockSpec(kvm.shape, lambda i: (0, 0)), pl.BlockSpec((8, LANE), lambda i: (0, 0)),
                  pl.BlockSpec(memory_space=pl.ANY)],
        out_specs=pl.BlockSpec((tq, 512), lambda i: (i, (A_WIDTH + 512) // 512)),
        out_shape=jax.ShapeDtypeStruct(mixed.shape, mixed.dtype), input_output_aliases={4: 0},
        compiler_params=_params(("parallel",)),
    )(proj, proj, kvm, pc, mixed)


def _attn_c_bwd(proj, kvm, pc, d_mixed, tq=256):
    s = proj.shape[0]

    def body(q_ref, z_ref, kv_ref, pc_ref, dm_ref, dq_ref, dz_ref, dkv_ref, dpc_ref):
        @pl.when(pl.program_id(0) == 0)
        def _():
            dkv_ref[...] = jnp.zeros_like(dkv_ref)
            dpc_ref[...] = jnp.zeros_like(dpc_ref)

        _, vjp = jax.vjp(_c_tile, q_ref[...], z_ref[...], kv_ref[...], pc_ref[...])
        dq, dz, dkv, dpc = vjp(dm_ref[...])
        dq_ref[...] = dq.astype(BF16)
        dz_ref[...] = dz.astype(BF16)
        dkv_ref[...] += dkv
        dpc_ref[...] += dpc

    blk = pl.BlockSpec((tq, 512), lambda i: (i, 0))
    kvs = pl.BlockSpec(kvm.shape, lambda i: (0, 0))
    small = pl.BlockSpec((8, LANE), lambda i: (0, 0))
    return pl.pallas_call(
        body, name="attn_c_bwd", grid=(s // tq,),
        in_specs=[pl.BlockSpec((tq, 512), lambda i: (i, P_QC // 512)), pl.BlockSpec((tq, 512), lambda i: (i, P_ZC // 512)),
                  kvs, small, pl.BlockSpec((tq, 512), lambda i: (i, 3))],
        out_specs=[blk, blk, kvs, small],
        out_shape=[jax.ShapeDtypeStruct((s, 512), BF16), jax.ShapeDtypeStruct((s, 512), BF16),
                   jax.ShapeDtypeStruct(kvm.shape, F32), jax.ShapeDtypeStruct((8, LANE), F32)],
        compiler_params=_params(("arbitrary",)),
    )(proj, proj, kvm, pc, d_mixed)


def _pad_row(v, width=LANE):
    v = v.reshape(1, -1)
    return jnp.pad(v, ((0, 0), (0, width - v.shape[1])))


def _local_step(x, mem, target, norm_w, w_perm_t, w_blocks_t, conv_w, pa, pb, pc, mem_norm_w, w_kv, w_out, gather=None,
                exchange=None):
    s = x.shape[0]
    cos2, sin2 = _rope_tables(s)
    hn = _rms_fwd(x, norm_w)
    wide = dict(tm=1024, tn=512, tk=2048)
    proj = _matmul(hn, w_perm_t, "nt", F32, "mm_proj", **wide)
    cqkv = _conv_fwd(proj, conv_w)
    if gather is None:
        mixed, o_sum, states = _delta_fwd(cqkv, proj, pa)
    else:
        mixed, o_sum, states, *arrived = _delta_fwd(cqkv, proj, pa, gather[0])
        w_out, w_kv = gather[1](*arrived)
    mixed = _attn_b_fwd(proj, cos2, sin2, pb, mixed)
    kvm = _mem_kv_fwd(mem, mem_norm_w, w_kv)
    mixed = _attn_c_fwd(proj, kvm, pc, mixed)
    mo = _matmul(mixed, w_out, "nn", F32, "mm_out", **wide)
    dy, dyb, loss_parts = _loss_dy(x, mo, target)

    d_mixed = _matmul(dyb, w_out, "nt", F32, "mm_dmixed", **wide)
    g_w_out = _matmul(mixed, dyb, "tn", F32, "mm_gwout", **wide)
    d_qc, d_zc, d_kvm, d_pc = _attn_c_bwd(proj, kvm, pc, d_mixed)
    g_w_kv, g_mem_norm = _mem_kv_bwd(mem, mem_norm_w, w_kv, d_kvm)
    d_qb, d_zb, d_kb, d_vb, d_pb = _attn_b_bwd(proj, cos2, sin2, pb, d_mixed)
    d_o, d_za, d_pa_out = _delta_out_bwd(o_sum, proj, pa, d_mixed)
    early = exchange[0](g_w_out, g_w_kv) if exchange else None
    d_c, d_gt, d_pa_scan, *landed_early = _delta_bwd(cqkv, proj, pa, d_o, states, early)
    d_pa = d_pa_out + d_pa_scan
    d_qkv, g_conv = _conv_bwd(proj, conv_w, d_c)
    d_proj = _cotangent_blocks(d_qkv, d_za, d_gt, d_qb, d_kb, d_vb, d_zb, d_qc, d_zc)
    g_w_blocks_t = _matmul(d_proj, hn, "tn", F32, "mm_gwin", tm=512, tn=2048, tk=2048)
    late = exchange[1](g_w_blocks_t) if exchange else None
    d_hn = _matmul(d_proj, w_blocks_t, "nn", F32, "mm_dhn", tm=1024, tn=2048, tk=512, ride=late)
    d_hn, landed_late = (d_hn[0], list(d_hn[1:])) if late else (d_hn, [])
    g_x, g_norm = _rms_bwd(x, norm_w, d_hn, dy)
    return dict(loss_parts=loss_parts, g_x=g_x, g_norm=g_norm, g_w_blocks_t=g_w_blocks_t, g_conv=g_conv, d_pa=d_pa,
                d_pb=d_pb, d_pc=d_pc, g_mem_norm=g_mem_norm, g_w_kv=g_w_kv, g_w_out=g_w_out,
                landed=landed_late + landed_early)


_SEGMENTS = ((0, O_GT, 0), (O_GT, O_QB, P_GT), (O_QB, O_KB, P_QB), (O_KB, O_VB, P_KB), (O_VB, O_ZB, P_VB),
             (O_ZB, O_QC, P_ZB), (O_QC, O_ZC, P_QC), (O_ZC, IN_WIDTH, P_ZC))


def _permute_blocks(w4):
    parts = []
    for first, end, _ in sorted(_SEGMENTS, key=lambda seg: seg[2]):
        row = first
        while row < end:
            k = row // W_IN_BLOCK
            stop = min(end, (k + 1) * W_IN_BLOCK)
            parts.append(w4[k][row - k * W_IN_BLOCK:stop - k * W_IN_BLOCK, :])
            row = stop
    parts.append(jnp.zeros((P_WIDTH - IN_WIDTH, w4.shape[2]), w4.dtype))
    return jnp.concatenate(parts, axis=0)


def _cotangent_blocks(d_qkv, d_za, d_gt, d_qb, d_kb, d_vb, d_zb, d_qc, d_zc):
    pieces = [d_qkv, d_za, d_gt[:, :O_QB - O_GT], d_qb, d_kb, d_vb, d_zb, d_qc, d_zc]
    orig = jnp.concatenate([t.astype(BF16) for t in pieces], axis=1)
    pad = jnp.zeros((orig.shape[0], W_IN_PAD - W_IN_BLOCK), BF16)
    parts = []
    for k in range(N_CHIPS):
        parts += [orig[:, k * W_IN_BLOCK:(k + 1) * W_IN_BLOCK], pad]
    return jnp.concatenate(parts, axis=1)


HBM = pl.BlockSpec(memory_space=pltpu.HBM)


def _place():
    x, y, c = lax.axis_index("x"), lax.axis_index("y"), lax.axis_index("c")
    chips = [(1 - x, y), (x, 1 - y), (1 - x, 1 - y)]
    return x, y, c, 2 * x + y, chips, [2 * cx + cy for cx, cy in chips]


PIECE_ROWS_CAP = 600


def _remote(src, dst, send_sems, recv_sems, k, to):
    return pltpu.make_async_remote_copy(src_ref=src, dst_ref=dst, send_sem=send_sems.at[k], recv_sem=recv_sems.at[k],
                                        device_id=to, device_id_type=MESH)


def _half_cols(ref, c):
    half = ref.shape[-1] // 2
    return pl.ds(pl.multiple_of(c * half, LANE), half)


class _PairedGather:
    def __init__(self, blocks):
        n = len(blocks)
        self.operands = list(blocks)
        self.out_shapes = [jax.ShapeDtypeStruct((N_CHIPS,) + b.shape, b.dtype) for b in blocks]
        self.scratch_shapes = [pltpu.SemaphoreType.DMA((6 * n,)), pltpu.SemaphoreType.DMA((6 * n,))]

    @staticmethod
    def _copies(srcs, dsts, sems):
        x, y, c, me, chips, chip_ids = _place()
        sends, landed, passes, passed = [], [], [], []
        for a, (src, dst) in enumerate(zip(srcs, dsts)):
            mine, other = _half_cols(src, c), _half_cols(src, 1 - c)
            for j, (chip, cid) in enumerate(zip(chips, chip_ids)):
                sends.append(_remote(src.at[:, mine], dst.at[me, :, mine], sems[0], sems[1], 6 * a + j, (*chip, c)))
                here = dst.at[cid, :, mine]
                landed.append(_remote(here, here, sems[0], sems[1], 6 * a + j, (x, y, 1 - c)))
                passes.append(_remote(here, here, sems[0], sems[1], 6 * a + 3 + j, (x, y, 1 - c)))
                there = dst.at[cid, :, other]
                passed.append(_remote(there, there, sems[0], sems[1], 6 * a + 3 + j, (x, y, 1 - c)))
        return sends, landed, passes, passed

    def start(self, srcs, dsts, sems):
        for cp in self._copies(srcs, dsts, sems)[0]:
            cp.start()

    def middle(self, srcs, dsts, sems):
        _, landed, passes, _ = self._copies(srcs, dsts, sems)
        for arrived, onward in zip(landed, passes):
            arrived.wait_recv()
            onward.start()

    def finish(self, srcs, dsts, sems):
        sends, _, passes, passed = self._copies(srcs, dsts, sems)
        for cp in passed:
            cp.wait_recv()
        for cp in sends + passes:
            cp.wait_send()


def _all_gather_weights(bigs, conv_b):
    bigs = tuple(bigs)
    n_big = len(bigs)

    def body(*refs):
        srcs, conv_src = refs[:n_big], refs[n_big]
        dsts, conv_dst = refs[n_big + 1:2 * n_big + 1], refs[2 * n_big + 1]
        send_sems, recv_sems, local_sems = refs[2 * n_big + 2:]
        x, y, c, me, chips, chip_ids = _place()
        sibling = (x, y, 1 - c)
        local = [pltpu.make_async_copy(src, dst.at[me], local_sems.at[a]) for a, (src, dst) in enumerate(zip(srcs, dsts))]
        local.append(pltpu.make_async_copy(conv_src, conv_dst.at[me], local_sems.at[n_big]))
        for cp in local:
            cp.start()
        sends = []
        for a, (src, dst) in enumerate(zip(srcs, dsts)):
            mine = _half_cols(src, c)
            for j, chip in enumerate(chips):
                sends.append(_remote(src.at[:, mine], dst.at[me, :, mine], send_sems, recv_sems, 6 * a + j, (*chip, c)))
        for j, chip in enumerate(chips):
            sends.append(_remote(conv_src, conv_dst.at[me], send_sems, recv_sems, 6 * n_big + j, (*chip, c)))
        for cp in sends:
            cp.start()
        passed = []
        for a, (src, dst) in enumerate(zip(srcs, dsts)):
            mine = _half_cols(src, c)
            for j, cid in enumerate(chip_ids):
                landed = dst.at[cid, :, mine]
                _remote(landed, landed, send_sems, recv_sems, 6 * a + j, sibling).wait_recv()
                cp = _remote(landed, landed, send_sems, recv_sems, 6 * a + 3 + j, sibling)
                cp.start()
                passed.append(cp)
        for a, (src, dst) in enumerate(zip(srcs, dsts)):
            other = _half_cols(src, 1 - c)
            for j, cid in enumerate(chip_ids):
                landed = dst.at[cid, :, other]
                _remote(landed, landed, send_sems, recv_sems, 6 * a + 3 + j, sibling).wait_recv()
        for j, cid in enumerate(chip_ids):
            _remote(conv_src, conv_dst.at[cid], send_sems, recv_sems, 6 * n_big + j, sibling).wait_recv()
        for cp in sends + passed:
            cp.wait_send()
        for cp in local:
            cp.wait()

    n_sem = 6 * n_big + 3
    return pl.pallas_call(
        body, name="all_gather_weights",
        out_shape=[jax.ShapeDtypeStruct((N_CHIPS,) + w.shape, w.dtype) for w in bigs + (conv_b,)],
        in_specs=[pl.BlockSpec(memory_space=pltpu.VMEM)] * (n_big + 1), out_specs=[HBM] * (n_big + 1),
        scratch_shapes=[pltpu.SemaphoreType.DMA((n_sem,)), pltpu.SemaphoreType.DMA((n_sem,)),
                        pltpu.SemaphoreType.DMA((n_big + 1,))],
        compiler_params=_params(),
    )(*bigs, conv_b)


def _pair_exchange(grads, name):
    n = len(grads)
    pieces = [_row_tile(g.shape[1]) for g in grads]

    def body(*refs):
        srcs, gots = refs[:n], refs[n:2 * n]
        stages = refs[2 * n:3 * n]
        send_sems, recv_sems, load_sems = refs[3 * n:]
        x, y, c, _, _, _ = _place()
        sibling = (x, y, 1 - c)
        for a in range(n):
            slabs, rows, _ = gots[a].shape
            piece = pieces[a]
            per_slab = rows // piece
            theirs = _half_cols(srcs[a], 1 - c)
            loads, sends = [], []
            for i in range(slabs * per_slab):
                k, r, slot = i // per_slab, i % per_slab, i % 2
                part = pl.ds(r * piece, piece)
                loads.append(pltpu.make_async_copy(srcs[a].at[k, part, theirs], stages[a].at[slot], load_sems.at[2 * a + slot]))
                sends.append(pltpu.make_async_remote_copy(
                    src_ref=stages[a].at[slot], dst_ref=gots[a].at[k, part, :],
                    send_sem=send_sems.at[2 * a + slot], recv_sem=recv_sems.at[a], device_id=sibling, device_id_type=MESH))
            loads[0].start()
            for i in range(len(loads)):
                loads[i].wait()
                sends[i].start()
                if i + 1 < len(loads):
                    if i >= 1:
                        sends[i - 1].wait_send()
                    loads[i + 1].start()
            for cp in sends[-2:]:
                cp.wait_send()
        for a in range(n):
            whole = srcs[a].at[:, :, _half_cols(srcs[a], c)]
            pltpu.make_async_remote_copy(src_ref=whole, dst_ref=gots[a], send_sem=send_sems.at[2 * a],
                                         recv_sem=recv_sems.at[a], device_id=sibling, device_id_type=MESH).wait_recv()

    halves = [jax.ShapeDtypeStruct((g.shape[0], g.shape[1], g.shape[2] // 2), g.dtype) for g in grads]
    return pl.pallas_call(
        body, name=name, out_shape=halves, in_specs=[HBM] * n, out_specs=[HBM] * n,
        scratch_shapes=[pltpu.VMEM((2, piece, g.shape[2] // 2), g.dtype) for piece, g in zip(pieces, grads)]
        + [pltpu.SemaphoreType.DMA((2 * n,)), pltpu.SemaphoreType.DMA((n,)), pltpu.SemaphoreType.DMA((2 * n,))],
        compiler_params=_params(),
    )(*grads)


class _ChipExchange:
    def __init__(self, halves):
        n = len(halves)
        self.operands = list(halves)
        self.out_shapes = [jax.ShapeDtypeStruct((N_CHIPS - 1,) + h.shape[1:], h.dtype) for h in halves]
        self.scratch_shapes = [pltpu.SemaphoreType.DMA((3 * n,)), pltpu.SemaphoreType.DMA((3 * n,))]

    @staticmethod
    def _copies(srcs, lands, sems):
        _, _, c, _, chips, chip_ids = _place()
        return [_remote(src.at[cid], land.at[j], sems[0], sems[1], 3 * a + j, (*chip, c))
                for a, (src, land) in enumerate(zip(srcs, lands)) for j, (chip, cid) in enumerate(zip(chips, chip_ids))]

    def start(self, srcs, lands, sems):
        for cp in self._copies(srcs, lands, sems):
            cp.start()

    def finish(self, srcs, lands, sems):
        copies = self._copies(srcs, lands, sems)
        for cp in copies:
            cp.wait_recv()
        for cp in copies:
            cp.wait_send()


def _pair_gather(halves, rows):
    n = len(halves)

    def body(*refs):
        srcs, fulls = refs[:n], refs[n:2 * n]
        send_sems, recv_sems, local_sems = refs[2 * n:]
        x, y, c, _, _, _ = _place()
        copies = []
        for a in range(n):
            mine, src = _half_cols(fulls[a], c), srcs[a].at[pl.ds(0, rows[a]), :]
            keep = pltpu.make_async_copy(src, fulls[a].at[:, mine], local_sems.at[a])
            keep.start()
            give = _remote(src, fulls[a].at[:, mine], send_sems, recv_sems, a, (x, y, 1 - c))
            give.start()
            copies += [keep, give]
        for a in range(n):
            other, src = _half_cols(fulls[a], 1 - c), srcs[a].at[pl.ds(0, rows[a]), :]
            copies[2 * a].wait()
            copies[2 * a + 1].wait_send()
            _remote(src, fulls[a].at[:, other], send_sems, recv_sems, a, (x, y, 1 - c)).wait_recv()

    return pl.pallas_call(
        body, name="grad_pair_gather",
        out_shape=[jax.ShapeDtypeStruct((r, 2 * h.shape[1]), h.dtype) for r, h in zip(rows, halves)],
        in_specs=[pl.BlockSpec(memory_space=pltpu.VMEM)] * n, out_specs=[HBM] * n,
        scratch_shapes=[pltpu.SemaphoreType.DMA((n,)), pltpu.SemaphoreType.DMA((n,)), pltpu.SemaphoreType.DMA((n,))],
    )(*halves)


def _all_reduce_small(p):
    n_dev = 8

    def body(p_ref, o_ref, land, send_sems, recv_sems):
        x, y, c = lax.axis_index("x"), lax.axis_index("y"), lax.axis_index("c")
        me = 4 * x + 2 * y + c
        land[me] = p_ref[...]
        sends = []
        for k in range(1, n_dev):
            fx, fy, fc = (k >> 2) & 1, (k >> 1) & 1, k & 1
            to = (x ^ fx, y ^ fy, c ^ fc)
            cp = _remote(p_ref, land.at[me], send_sems, recv_sems, k - 1, to)
            cp.start()
            sends.append(cp)
        for k in range(1, n_dev):
            _remote(p_ref, land.at[me ^ k], send_sems, recv_sems, k - 1, (x, y, c)).wait_recv()
        total = land[0]
        for d in range(1, n_dev):
            total = total + land[d]
        o_ref[...] = total
        for cp in sends:
            cp.wait_send()

    vm = pl.BlockSpec(memory_space=pltpu.VMEM)
    return pl.pallas_call(
        body, name="all_reduce_small", out_shape=jax.ShapeDtypeStruct(p.shape, p.dtype), in_specs=[vm], out_specs=vm,
        scratch_shapes=[pltpu.VMEM((n_dev,) + p.shape, p.dtype), pltpu.SemaphoreType.DMA((n_dev - 1,)),
                        pltpu.SemaphoreType.DMA((n_dev - 1,))],
    )(p)


def _row_tile(rows):
    fits = [t for t in range(8, min(rows, PIECE_ROWS_CAP) + 1, 8) if rows % t == 0]
    return max(fits) if fits else rows


def _pair_sum(full, got, core, name):
    n, r, c = got.shape
    tr = _row_tile(r)

    def body(core_ref, a_ref, b_ref, o_ref):
        o_ref[...] = (a_ref[...] + b_ref[...]).astype(BF16)

    blk = pl.BlockSpec((None, tr, c), lambda i, j, core_ref: (i, j, 0))
    grid_spec = pltpu.PrefetchScalarGridSpec(
        num_scalar_prefetch=1, grid=(n, r // tr),
        in_specs=[pl.BlockSpec((None, tr, c), lambda i, j, core_ref: (i, j, core_ref[0])), blk], out_specs=blk)
    return pl.pallas_call(body, name=name, grid_spec=grid_spec, out_shape=jax.ShapeDtypeStruct(got.shape, BF16),
                          compiler_params=_params(("parallel", "parallel")))(core, full, got)


def _chip_sum(full, got, land, place, name):
    n, r, c = land.shape
    tr = _row_tile(r)

    def body(place_ref, a_ref, b_ref, l_ref, o_ref):
        total = a_ref[...] + b_ref[...]
        for j in range(n):
            total = total + l_ref[j].astype(F32)
        o_ref[...] = total

    grid_spec = pltpu.PrefetchScalarGridSpec(
        num_scalar_prefetch=1, grid=(r // tr,),
        in_specs=[pl.BlockSpec((None, tr, c), lambda i, p: (p[0], i, p[1])),
                  pl.BlockSpec((None, tr, c), lambda i, p: (p[0], i, 0)),
                  pl.BlockSpec((n, tr, c), lambda i, p: (0, i, 0))],
        out_specs=pl.BlockSpec((tr, c), lambda i, p: (i, 0)))
    return pl.pallas_call(body, name=name, grid_spec=grid_spec, out_shape=jax.ShapeDtypeStruct((r, c), F32),
                          compiler_params=_params(("parallel",)))(place, full, got, land)


def _adamw(w, g, m, v, name):
    r, c = w.shape
    tr = _row_tile(r)
    tc = 1024 if c % 1024 == 0 else c

    def body(w_ref, g_ref, m_ref, v_ref, d_ref, mo_ref, vo_ref):
        g_ = g_ref[...]
        m2 = ADAM_B1 * m_ref[...] + (1.0 - ADAM_B1) * g_
        v2 = ADAM_B2 * v_ref[...] + (1.0 - ADAM_B2) * jnp.square(g_)
        m_hat = m2 / (1.0 - ADAM_B1 ** ADAM_STEP)
        v_hat = v2 / (1.0 - ADAM_B2 ** ADAM_STEP)
        d_ref[...] = -ADAM_LR * (m_hat / (jnp.sqrt(v_hat) + ADAM_EPS) + ADAM_WD * w_ref[...])
        mo_ref[...] = m2
        vo_ref[...] = v2

    blk = pl.BlockSpec((tr, tc), lambda i, j: (i, j))
    return pl.pallas_call(body, name=name, grid=(r // tr, c // tc), in_specs=[blk] * 4, out_specs=[blk] * 3,
                          out_shape=[jax.ShapeDtypeStruct(w.shape, F32)] * 3,
                          compiler_params=_params(("parallel", "parallel")))(w, g, m, v)


SMALL_NAMES = ("norm_w", "mem_norm_w", "o_norm_a", "q_norm_c", "k_norm_c", "q_norm_b", "k_norm_b",
               "a_log_fwd", "a_log_bwd", "dt_bias_fwd", "dt_bias_bwd", "sink_b")
SMALL_SIZES = (2048, 2048, 128, 128, 128, 64, 64, 8, 8, 8, 8, 8)
SMALL_LOSS = sum(SMALL_SIZES)
SMALL_CONV = 5120
SMALL_TOTAL = SMALL_CONV + CONV_K * 3 * A_WIDTH
SMALL_ROWS = SMALL_TOTAL // LANE


def _pack_small(parts, extra=None, conv=None):
    vec = [parts[n].reshape(-1) for n in SMALL_NAMES]
    vec.append(jnp.zeros((1,), F32) if extra is None else extra.reshape(1))
    vec.append(jnp.zeros((SMALL_CONV - SMALL_LOSS - 1,), F32))
    vec.append(jnp.zeros((SMALL_TOTAL - SMALL_CONV,), F32) if conv is None else conv.reshape(-1))
    return jnp.concatenate(vec).reshape(SMALL_ROWS, LANE)


def _unpack_small(packed):
    flat = packed.reshape(-1)
    out, off = {}, 0
    for n, size in zip(SMALL_NAMES, SMALL_SIZES):
        out[n] = flat[off:off + size].reshape(1, size)
        off += size
    return out


WEIGHT_ORDER = ("norm_w", "w_in", "conv_w_a", "a_log_fwd", "a_log_bwd", "dt_bias_fwd", "dt_bias_bwd", "o_norm_a",
                "q_norm_b", "k_norm_b", "sink_b", "mem_norm_w", "w_mem_kv", "q_norm_c", "k_norm_c", "w_out")


def kernel(x, mem, norm_w, w_in, conv_w_a, a_log_fwd, a_log_bwd, dt_bias_fwd, dt_bias_bwd, o_norm_a, q_norm_b, k_norm_b, sink_b, mem_norm_w, w_mem_kv, q_norm_c, k_norm_c, w_out, loss_target, m_norm_w, m_w_in, m_conv_w_a, m_a_log_fwd, m_a_log_bwd, m_dt_bias_fwd, m_dt_bias_bwd, m_o_norm_a, m_q_norm_b, m_k_norm_b, m_sink_b, m_mem_norm_w, m_w_mem_kv, m_q_norm_c, m_k_norm_c, m_w_out, v_norm_w, v_w_in, v_conv_w_a, v_a_log_fwd, v_a_log_bwd, v_dt_bias_fwd, v_dt_bias_bwd, v_o_norm_a, v_q_norm_b, v_k_norm_b, v_sink_b, v_mem_norm_w, v_w_mem_kv, v_q_norm_c, v_k_norm_c, v_w_out):
    weights = dict(norm_w=norm_w, w_in=w_in, conv_w_a=conv_w_a, a_log_fwd=a_log_fwd, a_log_bwd=a_log_bwd,
                   dt_bias_fwd=dt_bias_fwd, dt_bias_bwd=dt_bias_bwd, o_norm_a=o_norm_a, q_norm_b=q_norm_b,
                   k_norm_b=k_norm_b, sink_b=sink_b, mem_norm_w=mem_norm_w, w_mem_kv=w_mem_kv, q_norm_c=q_norm_c,
                   k_norm_c=k_norm_c, w_out=w_out)
    mom1 = dict(norm_w=m_norm_w, w_in=m_w_in, conv_w_a=m_conv_w_a, a_log_fwd=m_a_log_fwd, a_log_bwd=m_a_log_bwd,
                dt_bias_fwd=m_dt_bias_fwd, dt_bias_bwd=m_dt_bias_bwd, o_norm_a=m_o_norm_a, q_norm_b=m_q_norm_b,
                k_norm_b=m_k_norm_b, sink_b=m_sink_b, mem_norm_w=m_mem_norm_w, w_mem_kv=m_w_mem_kv,
                q_norm_c=m_q_norm_c, k_norm_c=m_k_norm_c, w_out=m_w_out)
    mom2 = dict(norm_w=v_norm_w, w_in=v_w_in, conv_w_a=v_conv_w_a, a_log_fwd=v_a_log_fwd, a_log_bwd=v_a_log_bwd,
                dt_bias_fwd=v_dt_bias_fwd, dt_bias_bwd=v_dt_bias_bwd, o_norm_a=v_o_norm_a, q_norm_b=v_q_norm_b,
                k_norm_b=v_k_norm_b, sink_b=v_sink_b, mem_norm_w=v_mem_norm_w, w_mem_kv=v_w_mem_kv,
                q_norm_c=v_q_norm_c, k_norm_c=v_k_norm_c, w_out=v_w_out)
    chip = 2 * lax.axis_index("x") + lax.axis_index("y")

    own_in = jnp.pad(jnp.transpose(w_in[0]).astype(BF16), ((0, W_IN_PAD - W_IN_BLOCK), (0, 0)))
    w_in4, conv4 = _all_gather_weights([own_in], conv_w_a[0])
    w_perm_t = _permute_blocks(w_in4)
    w_blocks_t = w_in4.reshape(N_CHIPS * W_IN_PAD, D_MODEL)
    conv_full = jnp.transpose(conv4, (1, 0, 2)).reshape(CONV_K, 3 * A_WIDTH)
    own_out, own_kv = w_out[0].astype(BF16), w_mem_kv[0].astype(BF16)

    def assemble(w_out4, w_kv4):
        w_out4 = lax.dynamic_update_index_in_dim(w_out4, own_out, chip, 0)
        w_kv4 = lax.dynamic_update_index_in_dim(w_kv4, own_kv, chip, 0)
        return w_out4.reshape(D_MODEL, D_MODEL), w_kv4.reshape(D_MODEL, 2 * C_HEADS * C_DIM)

    gather = (_PairedGather([own_out, own_kv]), assemble)
    pa = jnp.concatenate([_pad_row(a_log_fwd), _pad_row(a_log_bwd), _pad_row(dt_bias_fwd), _pad_row(dt_bias_bwd),
                          _pad_row(o_norm_a), jnp.zeros((3, LANE), F32)], axis=0)
    pb = jnp.concatenate([_pad_row(q_norm_b), _pad_row(k_norm_b), _pad_row(sink_b), jnp.zeros((5, LANE), F32)], axis=0)
    pc = jnp.concatenate([_pad_row(q_norm_c), _pad_row(k_norm_c), jnp.zeros((6, LANE), F32)], axis=0)

    full, got = {}, {}
    core = lax.axis_index("c").astype(jnp.int32).reshape(1)

    def pair_round(tag, blocks):
        names = [tag + "_%d" % i for i in range(len(blocks))]
        full.update(zip(names, blocks))
        got.update(zip(names, _pair_exchange(blocks, "grad_pair_exchange_" + tag)))
        return _ChipExchange([_pair_sum(full[n], got[n], core, "grad_pair_sum_" + n) for n in names])

    def early(g_w_out, g_w_kv):
        return pair_round("early", [g_w_out.reshape(N_CHIPS, D_MODEL // N_CHIPS, D_MODEL),
                                    g_w_kv.reshape(N_CHIPS, D_MODEL // N_CHIPS, 2 * C_HEADS * C_DIM)])

    def late(g_w_blocks_t):
        return pair_round("late", [g_w_blocks_t.reshape(N_CHIPS, W_IN_PAD, D_MODEL)])

    r = _local_step(x[0], mem[0], loss_target[0], norm_w, w_perm_t, w_blocks_t, conv_full, pa, pb, pc, mem_norm_w, None, None,
                    gather, (early, late))
    place = jnp.stack([chip, lax.axis_index("c")]).astype(jnp.int32)
    reduced = [_chip_sum(full[n], got[n], l, place, "grad_chip_sum_" + n)
               for n, l in zip(("late_0", "early_0", "early_1"), r["landed"])]
    g_w_in_t, g_w_out, g_w_kv = _pair_gather(reduced, [W_IN_BLOCK, D_MODEL // N_CHIPS, D_MODEL // N_CHIPS])

    d_pa, d_pb, d_pc = r["d_pa"], r["d_pb"], r["d_pc"]
    small_g = dict(norm_w=r["g_norm"], mem_norm_w=r["g_mem_norm"], o_norm_a=d_pa[4], q_norm_c=d_pc[0], k_norm_c=d_pc[1],
                   q_norm_b=d_pb[0, :B_DIM], k_norm_b=d_pb[1, :B_DIM], a_log_fwd=d_pa[0, :A_HEADS],
                   a_log_bwd=d_pa[1, :A_HEADS], dt_bias_fwd=d_pa[2, :A_HEADS], dt_bias_bwd=d_pa[3, :A_HEADS],
                   sink_b=d_pb[2, :B_HEADS])
    packed = _all_reduce_small(_pack_small(small_g, jnp.sum(r["loss_parts"][:, 0, 0]), r["g_conv"]))
    flat = packed.reshape(-1)
    loss = flat[SMALL_LOSS]
    conv_sum = flat[SMALL_CONV:].reshape(CONV_K, 3 * A_WIDTH)
    conv_cols = 3 * A_WIDTH // N_CHIPS
    g_conv = lax.dynamic_slice(conv_sum, (0, chip * conv_cols), (CONV_K, conv_cols))

    grads = _unpack_small(packed)
    grads.update(w_in=jnp.transpose(g_w_in_t), w_mem_kv=g_w_kv, w_out=g_w_out, conv_w_a=g_conv)
    delta, new_m, new_v = {}, {}, {}
    for n in ("w_mem_kv", "w_out", "conv_w_a"):
        delta[n], new_m[n], new_v[n] = _adamw(weights[n][0], grads[n], mom1[n][0], mom2[n][0], "adamw_" + n)
    stepped = _adamw(jnp.transpose(w_in[0]), g_w_in_t, jnp.transpose(m_w_in[0]), jnp.transpose(v_w_in[0]), "adamw_w_in")
    delta["w_in"], new_m["w_in"], new_v["w_in"] = (jnp.transpose(t) for t in stepped)
    d_s, m_s, v_s = _adamw(_pack_small(weights), packed, _pack_small(mom1), _pack_small(mom2), "adamw_small")
    d_s, m_s, v_s = _unpack_small(d_s), _unpack_small(m_s), _unpack_small(v_s)
    for n in SMALL_NAMES:
        delta[n], new_m[n], new_v[n] = d_s[n], m_s[n], v_s[n]

    def shaped(tree):
        return [tree[n].reshape(weights[n].shape) for n in WEIGHT_ORDER]

    return (loss, r["g_x"].reshape(x.shape), *shaped(grads), *shaped(delta), *shaped(new_m), *shaped(new_v))
```

```python
import functools

import jax
import jax.numpy as jnp
from jax import lax
from jax.experimental import pallas as pl
from jax.experimental.pallas import tpu as pltpu

F32 = jnp.float32
BF16 = jnp.bfloat16
HI = lax.Precision.HIGHEST
MESH = pl.DeviceIdType.MESH

D_MODEL = 2048
A_WIDTH = 1024
A_HEADS = 8
A_DIM = 128
CONV_K = 5
CHUNK = 64
B_HEADS = 8
B_KV = 2
B_DIM = 64
WINDOW = 128
C_HEADS = 4
C_DIM = 128
MEM_LEN = 256
ROPE_THETA = 10000.0
EPS = 1e-6
IN_WIDTH = 6432
N_CHIPS = 4
W_IN_BLOCK = IN_WIDTH // N_CHIPS
W_IN_PAD = 1664

LANE = 128
P_QA, P_KA, P_VA, P_ZA = 0, 1024, 2048, 3072
P_QB, P_ZB, P_QC, P_ZC = 4096, 4608, 5120, 5632
P_KB, P_VB, P_GT = 6144, 6272, 6400
P_WIDTH = 6656
O_GT, O_QB, O_KB, O_VB, O_ZB, O_QC, O_ZC = 4096, 4128, 4640, 4768, 4896, 5408, 5920

ADAM_LR, ADAM_B1, ADAM_B2, ADAM_EPS, ADAM_WD, ADAM_STEP = 0.001, 0.9, 0.999, 1e-08, 0.01, 10

VMEM_LIMIT = 56 * 1024 * 1024


def _params(sem=None):
    return pltpu.CompilerParams(dimension_semantics=sem, vmem_limit_bytes=VMEM_LIMIT)


def _dot(a, b, dims=(((1,), (0,)), ((), ())), precision=HI):
    return lax.dot_general(a, b, dims, precision=precision, preferred_element_type=F32)


def _dot_nt(a, b, precision=HI):
    return _dot(a, b, (((1,), (1,)), ((), ())), precision)


def _dot_tn(a, b, precision=HI):
    return _dot(a, b, (((0,), (0,)), ((), ())), precision)


_NN = (((1,), (0,)), ((), ()))
_NT = (((1,), (1,)), ((), ()))
_TN = (((0,), (0,)), ((), ()))


def _bdot(a, b, dims):
    return lax.dot_general(a.astype(BF16), b.astype(BF16), dims, preferred_element_type=F32)


@jax.custom_vjp
def _mm(a, b):
    return _bdot(a, b, _NN)


_mm.defvjp(lambda a, b: (_bdot(a, b, _NN), (a, b)),
           lambda res, ct: (_bdot(ct, res[1], _NT), _bdot(res[0], ct, _TN)))


@jax.custom_vjp
def _mm_nt(a, b):
    return _bdot(a, b, _NT)


_mm_nt.defvjp(lambda a, b: (_bdot(a, b, _NT), (a, b)),
              lambda res, ct: (_bdot(ct, res[1], _NN), _bdot(ct, res[0], _TN)))


@jax.custom_vjp
def _mm_tn(a, b):
    return _bdot(a, b, _TN)


_mm_tn.defvjp(lambda a, b: (_bdot(a, b, _TN), (a, b)),
              lambda res, ct: (_bdot(res[1], ct, _NT), _bdot(res[0], ct, _NN)))


def _rms(t, w):
    return t * lax.rsqrt(jnp.mean(t * t, axis=-1, keepdims=True) + EPS) * w


def _l2(t):
    return t * lax.rsqrt(jnp.sum(t * t, axis=-1, keepdims=True) + EPS)


def _silu(t):
    return t * jax.nn.sigmoid(t)


def _softplus(t):
    return jnp.maximum(t, 0.0) + jnp.log1p(jnp.exp(-jnp.abs(t)))


def _matmul(a, b, mode, out_dtype, name, tm=512, tn=512, tk=512, ride=None):
    (m, k) = a.shape[::-1] if mode == "tn" else a.shape
    n = b.shape[0] if mode == "nt" else b.shape[1]
    tm, tn, tk = min(tm, m), min(tn, n), min(tk, k)
    assert m % tm == 0 and n % tn == 0 and k % tk == 0, (m, n, k, tm, tn, tk)
    if mode == "nn":
        a_spec = pl.BlockSpec((tm, tk), lambda i, j, kk: (i, kk))
        b_spec = pl.BlockSpec((tk, tn), lambda i, j, kk: (kk, j))
        dims = (((1,), (0,)), ((), ()))
    elif mode == "nt":
        a_spec = pl.BlockSpec((tm, tk), lambda i, j, kk: (i, kk))
        b_spec = pl.BlockSpec((tn, tk), lambda i, j, kk: (j, kk))
        dims = (((1,), (1,)), ((), ()))
    else:
        a_spec = pl.BlockSpec((tk, tm), lambda i, j, kk: (kk, i))
        b_spec = pl.BlockSpec((tk, tn), lambda i, j, kk: (kk, j))
        dims = (((0,), (0,)), ((), ()))
    nk = k // tk
    grid = (m // tm, n // tn, nk)
    n_in = len(ride.operands) if ride else 0
    n_out = len(ride.out_shapes) if ride else 0

    def body(*refs):
        a_ref, b_ref, o_ref = refs[0], refs[1], refs[2 + n_in]
        scratch = refs[3 + n_in + n_out:]
        step = (pl.program_id(0) * grid[1] + pl.program_id(1)) * nk + pl.program_id(2)
        riders = (refs[2:2 + n_in], refs[3 + n_in:3 + n_in + n_out], scratch[(0 if nk == 1 else 1):])
        if ride:
            pl.when(step == 0)(lambda: ride.start(*riders))
        if nk == 1:
            o_ref[...] = _bdot(a_ref[...], b_ref[...], dims).astype(out_dtype)
        else:
            acc_ref, kk = scratch[0], pl.program_id(2)

            @pl.when(kk == 0)
            def _():
                acc_ref[...] = jnp.zeros_like(acc_ref)

            acc_ref[...] += _bdot(a_ref[...], b_ref[...], dims)

            @pl.when(kk == nk - 1)
            def _():
                o_ref[...] = acc_ref[...].astype(out_dtype)
        if ride:
            pl.when(step == grid[0] * grid[1] * nk - 1)(lambda: ride.finish(*riders))

    out = pl.pallas_call(
        body, name=name, grid=grid,
        in_specs=[a_spec, b_spec] + [HBM] * n_in,
        out_specs=[pl.BlockSpec((tm, tn), lambda i, j, kk: (i, j))] + [HBM] * n_out,
        out_shape=[jax.ShapeDtypeStruct((m, n), out_dtype)] + (list(ride.out_shapes) if ride else []),
        scratch_shapes=([] if nk == 1 else [pltpu.VMEM((tm, tn), F32)]) + (list(ride.scratch_shapes) if ride else []),
        compiler_params=_params(("arbitrary",) * 3 if ride else ("parallel", "parallel", "arbitrary")),
    )(a, b, *(ride.operands if ride else []))
    return out if ride else out[0]


def _rms_fwd(x, w, tr=256):
    s, d = x.shape

    def body(x_ref, w_ref, o_ref):
        o_ref[...] = _rms(x_ref[...], w_ref[...]).astype(BF16)

    return pl.pallas_call(
        body, name="rms_fwd", grid=(s // tr,),
        in_specs=[pl.BlockSpec((tr, d), lambda i: (i, 0)), pl.BlockSpec((1, d), lambda i: (0, 0))],
        out_specs=pl.BlockSpec((tr, d), lambda i: (i, 0)),
        out_shape=jax.ShapeDtypeStruct((s, d), BF16), compiler_params=_params(("parallel",)),
    )(x, w)


def _rms_bwd(x, w, d_hn, dy, tr=256):
    s, d = x.shape

    def body(x_ref, w_ref, g_ref, dy_ref, gx_ref, gw_ref):
        _, vjp = jax.vjp(_rms, x_ref[...], w_ref[...])
        dx, dw = vjp(g_ref[...])
        gx_ref[...] = dy_ref[...] + dx

        @pl.when(pl.program_id(0) == 0)
        def _():
            gw_ref[...] = jnp.zeros_like(gw_ref)

        gw_ref[...] += dw

    row = pl.BlockSpec((tr, d), lambda i: (i, 0))
    vec = pl.BlockSpec((1, d), lambda i: (0, 0))
    return pl.pallas_call(
        body, name="rms_bwd", grid=(s // tr,), in_specs=[row, vec, row, row], out_specs=[row, vec],
        out_shape=[jax.ShapeDtypeStruct((s, d), F32), jax.ShapeDtypeStruct((1, d), F32)],
        compiler_params=_params(("arbitrary",)),
    )(x, w, d_hn, dy)


def _loss_dy(x, mo, target, tr=256):
    s, d = x.shape
    nt = s // tr

    def body(x_ref, mo_ref, t_ref, dy_ref, dyb_ref, l_ref):
        err = x_ref[...] + mo_ref[...] - t_ref[...]
        dy = err * (1.0 / d)
        dy_ref[...] = dy
        dyb_ref[...] = dy.astype(BF16)
        l_ref[...] = jnp.full(l_ref.shape, 0.5 * jnp.sum(jnp.sum(err * err, axis=1, keepdims=True) * (1.0 / d)), F32)

    row = pl.BlockSpec((tr, d), lambda i: (i, 0))
    return pl.pallas_call(
        body, name="loss_dy", grid=(nt,), in_specs=[row, row, row],
        out_specs=[row, row, pl.BlockSpec((1, 8, LANE), lambda i: (i, 0, 0))],
        out_shape=[jax.ShapeDtypeStruct((s, d), F32), jax.ShapeDtypeStruct((s, d), BF16),
                   jax.ShapeDtypeStruct((nt, 8, LANE), F32)],
        compiler_params=_params(("parallel",)),
    )(x, mo, target)


def _shift_rows(t, s):
    if s == 0:
        return t
    n = t.shape[0]
    rolled = pltpu.roll(t, (-s) % n, axis=0)
    idx = lax.broadcasted_iota(jnp.int32, t.shape, 0) + s
    return jnp.where((idx >= 0) & (idx < n), rolled, 0.0)


def _conv_fwd(proj, conv_w):
    s = proj.shape[0]
    nblk = 3 * A_WIDTH // LANE

    def body(x_ref, w_ref, o_ref):
        x = x_ref[...]
        acc = jnp.zeros_like(x)
        for j in range(CONV_K):
            acc = acc + w_ref[j:j + 1, :] * _shift_rows(x, j - CONV_K // 2)
        o_ref[...] = acc

    return pl.pallas_call(
        body, name="conv_fwd", grid=(nblk,),
        in_specs=[pl.BlockSpec((s, LANE), lambda i: (0, i)), pl.BlockSpec((CONV_K, LANE), lambda i: (0, i))],
        out_specs=pl.BlockSpec((None, s, LANE), lambda i: (i // A_HEADS, 0, i % A_HEADS)),
        out_shape=jax.ShapeDtypeStruct((3, s, A_WIDTH), F32), compiler_params=_params(("parallel",)),
    )(proj, conv_w)


def _conv_bwd(proj, conv_w, d_c):
    s = proj.shape[0]
    nblk = 3 * A_WIDTH // LANE

    def body(x_ref, w_ref, g_ref, dx_ref, dw_ref):
        x, g = x_ref[...], g_ref[...]
        acc = jnp.zeros_like(x)
        for j in range(CONV_K):
            off = j - CONV_K // 2
            acc = acc + w_ref[j:j + 1, :] * _shift_rows(g, -off)
            dw_ref[j:j + 1, :] = jnp.sum(_shift_rows(x, off) * g, axis=0, keepdims=True)
        dx_ref[...] = acc.astype(BF16)

    col = pl.BlockSpec((s, LANE), lambda i: (0, i))
    wsp = pl.BlockSpec((CONV_K, LANE), lambda i: (0, i))
    dsp = pl.BlockSpec((None, s, LANE), lambda i: (i // A_HEADS, 0, i % A_HEADS))
    return pl.pallas_call(
        body, name="conv_bwd", grid=(nblk,), in_specs=[col, wsp, dsp], out_specs=[col, wsp],
        out_shape=[jax.ShapeDtypeStruct((s, 3 * A_WIDTH), BF16), jax.ShapeDtypeStruct((CONV_K, 3 * A_WIDTH), F32)],
        compiler_params=_params(("parallel",)),
    )(proj, conv_w, d_c)


A_FWD_HEADS = 4
A_BWD_HEADS = 4


def _neumann_inverse(a):
    c = a.shape[-1]
    eye = (lax.broadcasted_iota(jnp.int32, (c, c), 0) == lax.broadcasted_iota(jnp.int32, (c, c), 1)).astype(F32)
    tinv = eye + a
    p = a
    for _ in range(5):
        p = _mm(p, p)
        tinv = tinv + _mm(tinv, p)
    return tinv


@jax.custom_vjp
def _unit_inverse(a):
    return _neumann_inverse(a)


def _unit_inverse_fwd(a):
    tinv = _neumann_inverse(a)
    return tinv, tinv


def _unit_inverse_bwd(tinv, ct):
    return (_bdot(_bdot(tinv, ct, _TN), tinv, _NT),)


_unit_inverse.defvjp(_unit_inverse_fwd, _unit_inverse_bwd)


@jax.custom_vjp
def _known_inverse(a, tinv):
    return tinv


_known_inverse.defvjp(lambda a, tinv: (tinv, tinv),
                      lambda tinv, ct: (_unit_inverse_bwd(tinv, ct)[0], jnp.zeros_like(tinv)))


def _a_chain(st, cq, ck, cv, alpha, beta_raw, a_log, dt_b, incl, strict, last, kept=None):
    c = CHUNK
    gb = -jnp.exp(a_log) * _softplus(alpha + dt_b)
    bb = jax.nn.sigmoid(beta_raw)
    q = _l2(_silu(cq)) * (A_DIM ** -0.5)
    k = _l2(_silu(ck))
    v = _silu(cv)

    gc = _dot(incl, jnp.broadcast_to(gb, (c, LANE)))
    tot = jnp.sum(gc * last, axis=0, keepdims=True)
    m1 = gc[:, :c]
    decay = incl * jnp.exp(incl * (m1 - m1.T))
    kb = k * bb
    vb = v * bb
    a = -(strict * decay * _mm_nt(kb, k))
    tinv = _unit_inverse(a) if kept is None else _known_inverse(a, kept)
    eg = jnp.exp(gc)
    u = _mm(tinv, vb)
    w = _mm(tinv, kb * eg)
    qk = _mm_nt(q, k) * decay
    v_new = u - _mm(w, st)
    o = _mm(q * eg, st) + _mm(qk, v_new)
    st_new = st * jnp.exp(tot) + _mm_tn(k * jnp.exp(tot - gc), v_new)
    return st_new, o, tinv


def _a_step(sts, cq, ck, cv, gts, pa, h0, kept=None):
    c = CHUNK
    lane = lax.broadcasted_iota(jnp.int32, (1, LANE), 1)
    ii = lax.broadcasted_iota(jnp.int32, (c, c), 0)
    jj = lax.broadcasted_iota(jnp.int32, (c, c), 1)
    row = lax.broadcasted_iota(jnp.int32, (c, 1), 0)

    def pick(t, col):
        return jnp.sum(jnp.where(lane == col, t, 0.0), axis=1, keepdims=True)

    alpha, beta_raw, a_log, dt_b, incl, strict, last = [], [], [], [], [], [], []
    for b in range(sts.shape[0]):
        h, rev = h0 + b // 2, b % 2
        alpha.append(pick(gts[b], h + 8 * rev))
        beta_raw.append(pick(gts[b], h + 16 + 8 * rev))
        a_log.append(pick(pa[rev:rev + 1, :], h))
        dt_b.append(pick(pa[2 + rev:3 + rev, :], h))
        incl.append(((ii <= jj) if rev else (ii >= jj)).astype(F32))
        strict.append(((ii < jj) if rev else (ii > jj)).astype(F32))
        last.append((row == (0 if rev else c - 1)).astype(F32))
    stack = lambda ts: jnp.concatenate([t[None] for t in ts], axis=0)
    return jax.vmap(_a_chain)(sts, cq, ck, cv, stack(alpha), stack(beta_raw), stack(a_log), stack(dt_b),
                              stack(incl), stack(strict), stack(last), kept)


def _a_final(o, za, pa):
    outs = []
    for j in range(o.shape[1] // A_DIM):
        ln = slice(j * A_DIM, (j + 1) * A_DIM)
        outs.append(_rms(o[:, ln], pa[4:5, :]) * _silu(za[:, ln]))
    return jnp.concatenate(outs, axis=1)


def _a_tiles(n, nchunk, heads):
    tiles = []
    for b in range(2 * heads):
        i = (nchunk - 1 - n) if b % 2 else n
        tiles.append((i, pl.ds(pl.multiple_of(i * CHUNK, CHUNK), CHUNK), slice((b // 2) * A_DIM, (b // 2 + 1) * A_DIM)))
    return tiles


def _a_load(tiles, c_ref, gt_ref):
    cq, ck, cv = (jnp.stack([c_ref[r, sl, ln] for _, sl, ln in tiles], axis=0) for r in range(3))
    return cq, ck, cv, jnp.stack([gt_ref[sl, :] for _, sl, _ in tiles], axis=0)


def _loop_by_two(n, step, init):
    assert n % 2 == 0
    return lax.fori_loop(0, n // 2, lambda m, carry: step(2 * m + 1, step(2 * m, carry, 0), 1), init)


def _a_scan(h0, heads, nchunk, c_ref, gt_ref, pa, of_ref, ob_ref, s_ref, t_ref):
    def step(n, sts, parity):
        tiles = _a_tiles(n, nchunk, heads)
        sts_new, o, tinv = _a_step(sts, *_a_load(tiles, c_ref, gt_ref), pa, h0)
        for b, (i, sl, ln) in enumerate(tiles):
            s_ref[b, i] = sts[b]
            t_ref[b, i] = tinv[b]
            (ob_ref if b % 2 else of_ref)[sl, ln] = o[b]
        return sts_new

    _loop_by_two(nchunk, step, jnp.zeros((2 * heads, A_DIM, A_DIM), F32))


def _a_specs(s, heads):
    wide = heads * A_DIM
    once = pl.Buffered(1)
    trio = pl.BlockSpec((3, s, wide), lambda g: (0, 0, g), pipeline_mode=once)
    gates = pl.BlockSpec((s, LANE), lambda g: (0, P_GT // LANE))
    small = pl.BlockSpec((8, LANE), lambda g: (0, 0))

    def cols(base):
        return pl.BlockSpec((s, wide), lambda g: (0, base // wide + g), pipeline_mode=once)

    state = pl.BlockSpec((2 * heads, s // CHUNK, A_DIM, A_DIM), lambda g: (g, 0, 0, 0), pipeline_mode=once)
    kept = pl.BlockSpec((2 * heads, s // CHUNK, CHUNK, CHUNK), lambda g: (g, 0, 0, 0), pipeline_mode=once)
    return wide, trio, gates, small, cols, state, kept


def _delta_fwd(cqkv, proj, pa, ride=None):
    s = cqkv.shape[1]
    nchunk = s // CHUNK
    heads = A_FWD_HEADS
    steps = A_HEADS // heads
    wide, trio, gates, small, cols, state, kept = _a_specs(s, heads)
    n_in = len(ride.operands) if ride else 0
    n_out = len(ride.out_shapes) if ride else 0

    def body(*refs):
        c_ref, gt_ref, za_ref, pa_ref = refs[:4]
        out_ref, o_ref, s_ref, t_ref = refs[4 + n_in:8 + n_in]
        ob_ref = refs[8 + n_in + n_out]
        riders = (refs[4:4 + n_in], refs[8 + n_in:8 + n_in + n_out], refs[9 + n_in + n_out:])
        g = pl.program_id(0)
        if ride:
            pl.when(g == 0)(lambda: ride.start(*riders))
            pl.when(g == steps - 1)(lambda: ride.middle(*riders))
        h0 = g * heads
        pa_v = pa_ref[...]
        _a_scan(h0, heads, nchunk, c_ref, gt_ref, pa_v, o_ref, ob_ref, s_ref, t_ref)
        o_ref[...] += ob_ref[...]
        out_ref[...] = _a_final(o_ref[...], za_ref[...], pa_v).astype(BF16)
        if ride:
            pl.when(g == steps - 1)(lambda: ride.finish(*riders))

    assert steps > 1
    return pl.pallas_call(
        body, name="delta_fwd", grid=(steps,),
        in_specs=[trio, gates, cols(P_ZA), small] + [HBM] * n_in,
        out_specs=[cols(0), cols(0), state, kept] + [HBM] * n_out,
        out_shape=[jax.ShapeDtypeStruct((s, D_MODEL), BF16),
                   jax.ShapeDtypeStruct((s, A_WIDTH), F32),
                   jax.ShapeDtypeStruct((2 * A_HEADS, nchunk, A_DIM, A_DIM), F32),
                   jax.ShapeDtypeStruct((2 * A_HEADS, nchunk, CHUNK, CHUNK), F32)]
        + (list(ride.out_shapes) if ride else []),
        scratch_shapes=[pltpu.VMEM((s, wide), F32)] + (list(ride.scratch_shapes) if ride else []),
        compiler_params=_params(("arbitrary",)),
    )(cqkv, proj, proj, pa, *(ride.operands if ride else []))


def _delta_out_bwd(o_sum, proj, pa, d_mixed, tr=256):
    s = o_sum.shape[0]

    def body(o_ref, za_ref, pa_ref, dm_ref, do_ref, dza_ref, dpa_ref):
        @pl.when(pl.program_id(0) == 0)
        def _():
            dpa_ref[...] = jnp.zeros_like(dpa_ref)

        _, vjp = jax.vjp(_a_final, o_ref[...], za_ref[...], pa_ref[...])
        d_o, d_za, dpa = vjp(dm_ref[...].astype(F32))
        do_ref[...] = d_o
        dza_ref[...] = d_za.astype(BF16)
        dpa_ref[...] += dpa

    def rows(col):
        return pl.BlockSpec((tr, A_WIDTH), lambda i: (i, col))

    small = pl.BlockSpec((8, LANE), lambda i: (0, 0))
    return pl.pallas_call(
        body, name="delta_out_bwd", grid=(s // tr,), in_specs=[rows(0), rows(P_ZA // A_WIDTH), small, rows(0)],
        out_specs=[rows(0), rows(0), small],
        out_shape=[jax.ShapeDtypeStruct((s, A_WIDTH), F32), jax.ShapeDtypeStruct((s, A_WIDTH), BF16),
                   jax.ShapeDtypeStruct((8, LANE), F32)],
        compiler_params=_params(("arbitrary",)),
    )(o_sum, proj, pa, d_mixed)


def _delta_bwd(cqkv, proj, pa, d_o, states, inverses, ride=None):
    s = cqkv.shape[1]
    nchunk = s // CHUNK
    heads = A_BWD_HEADS
    steps = A_HEADS // heads
    wide, trio, gates, small, cols, _, _ = _a_specs(s, heads)
    n_in = len(ride.operands) if ride else 0
    n_out = len(ride.out_shapes) if ride else 0

    def body(*refs):
        c_ref, gt_ref, pa_ref, do_ref, s_hbm, t_hbm = refs[:6]
        dc_ref, dgt_ref, dpa_ref = refs[6 + n_in:9 + n_in]
        s_buf, t_buf, s_sems = refs[9 + n_in + n_out:12 + n_in + n_out]
        riders = (refs[6:6 + n_in], refs[9 + n_in:9 + n_in + n_out], refs[12 + n_in + n_out:])
        if ride:
            pl.when(pl.program_id(0) == 0)(lambda: ride.start(*riders))
        h0 = pl.program_id(0) * heads
        pa_v = pa_ref[...]

        @pl.when(h0 == 0)
        def _():
            dgt_ref[...] = jnp.zeros_like(dgt_ref)
            dpa_ref[...] = jnp.zeros_like(dpa_ref)

        dc_ref[...] = jnp.zeros_like(dc_ref)

        def state_copies(n, slot):
            tiles = _a_tiles(nchunk - 1 - n, nchunk, heads)
            return ([pltpu.make_async_copy(s_hbm.at[2 * h0 + b, i], s_buf.at[slot, b], s_sems.at[0, slot, b])
                     for b, (i, _, _) in enumerate(tiles)]
                    + [pltpu.make_async_copy(t_hbm.at[2 * h0 + b, i], t_buf.at[slot, b], s_sems.at[1, slot, b])
                       for b, (i, _, _) in enumerate(tiles)])

        for cp in state_copies(0, 0):
            cp.start()

        def step(n, carry, parity):
            d_sts, dpa = carry
            tiles = _a_tiles(nchunk - 1 - n, nchunk, heads)
            for cp in state_copies(n, parity):
                cp.wait()

            @pl.when(n + 1 < nchunk)
            def _():
                for cp in state_copies(n + 1, 1 - parity):
                    cp.start()

            sts, kept = s_buf[parity], t_buf[parity]
            d_o_t = jnp.stack([do_ref[sl, ln] for _, sl, ln in tiles], axis=0)
            _, vjp_c = jax.vjp(lambda *a: _a_step(*a, h0, kept)[:2], sts, *_a_load(tiles, c_ref, gt_ref), pa_v)
            d_prev, dcq, dck, dcv, dgts, dpa_i = vjp_c((d_sts, d_o_t))
            for b, (_, sl, ln) in enumerate(tiles):
                for r, dc in enumerate((dcq, dck, dcv)):
                    dc_ref[r, sl, ln] += dc[b]
                dgt_ref[sl, :] += dgts[b]
            return d_prev, dpa + dpa_i

        init = (jnp.zeros((2 * heads, A_DIM, A_DIM), F32), jnp.zeros((8, LANE), F32))
        _, dpa_out = lax.fori_loop(0, nchunk, lambda n, carry: step(n, carry, n % 2), init)
        dpa_ref[...] += dpa_out
        if ride:
            pl.when(pl.program_id(0) == steps - 1)(lambda: ride.finish(*riders))

    fixed = pl.BlockSpec((s, LANE), lambda g: (0, 0))
    return pl.pallas_call(
        body, name="delta_bwd", grid=(steps,),
        in_specs=[trio, gates, small, cols(0), pl.BlockSpec(memory_space=pl.ANY), pl.BlockSpec(memory_space=pl.ANY)]
        + [HBM] * n_in,
        out_specs=[trio, fixed, small] + [HBM] * n_out,
        out_shape=[jax.ShapeDtypeStruct((3, s, A_WIDTH), F32), jax.ShapeDtypeStruct((s, LANE), F32),
                   jax.ShapeDtypeStruct((8, LANE), F32)] + (list(ride.out_shapes) if ride else []),
        scratch_shapes=[pltpu.VMEM((2, 2 * heads, A_DIM, A_DIM), F32), pltpu.VMEM((2, 2 * heads, CHUNK, CHUNK), F32),
                        pltpu.SemaphoreType.DMA((2, 2, 2 * heads))]
        + (list(ride.scratch_shapes) if ride else []),
        compiler_params=_params(("arbitrary",)),
    )(cqkv, proj, pa, d_o, states, inverses, *(ride.operands if ride else []))


def _rope_tables(s):
    inv = ROPE_THETA ** (-jnp.arange(0, B_DIM, 2, dtype=F32) / B_DIM)
    ang = jnp.arange(s, dtype=F32)[:, None] * inv[None, :]
    cos, sin = jnp.cos(ang), jnp.sin(ang)
    return jnp.concatenate([cos, cos], axis=1), jnp.concatenate([-sin, sin], axis=1)


def _b_block(q_t, z_t, k3, v3, cos_q, sin_q, cos_k, sin_k, pb, n, nb):
    w = WINDOW
    def swap(t):
        return jnp.concatenate([t[:, B_DIM // 2:], t[:, :B_DIM // 2]], axis=1)

    grp = B_HEADS // B_KV
    qi = lax.broadcasted_iota(jnp.int32, (grp * w, 3 * w), 0) & (w - 1)
    kj = lax.broadcasted_iota(jnp.int32, (grp * w, 3 * w), 1)
    kpos = kj + (n - 1) * w
    mask = (jnp.abs(kj - w - qi) <= w) & (kpos >= 0) & (kpos < nb * w)
    lane = lax.broadcasted_iota(jnp.int32, (1, LANE), 1)
    qn, kn = pb[0:1, :B_DIM], pb[1:2, :B_DIM]
    cos_g = jnp.concatenate([cos_q] * grp, axis=0)
    sin_g = jnp.concatenate([sin_q] * grp, axis=0)
    def group(q, k, v, sink):
        k = _rms(k, kn)
        k = k * cos_k + swap(k) * sin_k
        q = _rms(q, qn)
        q = q * cos_g + swap(q) * sin_g
        s = _mm_nt(q, k) * (B_DIM ** -0.5)
        s = jnp.where(mask, s, -jnp.inf)
        m = jnp.maximum(jnp.max(s, axis=1, keepdims=True), sink)
        p = jnp.exp(s - m)
        p = p / (jnp.sum(p, axis=1, keepdims=True) + jnp.exp(sink - m))
        return _mm(p, v)

    stack = lambda ts: jnp.concatenate([t[None] for t in ts], axis=0)
    qs, ks, vs, sinks = [], [], [], []
    for hk in range(B_KV):
        heads = [hk * grp + g for g in range(grp)]
        ks.append(k3[:, hk * B_DIM:(hk + 1) * B_DIM])
        vs.append(v3[:, hk * B_DIM:(hk + 1) * B_DIM])
        qs.append(jnp.concatenate([q_t[:, hq * B_DIM:(hq + 1) * B_DIM] for hq in heads], axis=0))
        sinks.append(jnp.concatenate(
            [jnp.broadcast_to(jnp.sum(jnp.where(lane == hq, pb[2:3, :], 0.0), axis=1, keepdims=True), (w, 1))
             for hq in heads], axis=0))
    o = jax.vmap(group)(stack(qs), stack(ks), stack(vs), stack(sinks))
    outs = [o[hk, g * w:(g + 1) * w, :] for hk in range(B_KV) for g in range(grp)]
    return jnp.concatenate(outs, axis=1) * _silu(z_t)


def _b_specs(s):
    nb = s // WINDOW
    qsp = pl.BlockSpec((WINDOW, 512), lambda n: (n, P_QB // 512))
    zsp = pl.BlockSpec((WINDOW, 512), lambda n: (n, P_ZB // 512))

    def three(col, width):
        return [pl.BlockSpec((WINDOW, width), lambda n: (jnp.maximum(n - 1, 0), col)),
                pl.BlockSpec((WINDOW, width), lambda n: (n, col)),
                pl.BlockSpec((WINDOW, width), lambda n: (jnp.minimum(n + 1, nb - 1), col))]

    tab = pl.BlockSpec((WINDOW, B_DIM), lambda n: (n, 0))
    small = pl.BlockSpec((8, LANE), lambda n: (0, 0))
    specs = [qsp, zsp] + three(P_KB // LANE, LANE) + three(P_VB // LANE, LANE) + [tab, tab] + three(0, B_DIM) + three(0, B_DIM) + [small]
    return nb, specs


def _b_args(proj, cos2, sin2, pb):
    return (proj, proj, proj, proj, proj, proj, proj, proj, cos2, sin2, cos2, cos2, cos2, sin2, sin2, sin2, pb)


def _b_load(refs):
    (q_ref, z_ref, kp, kc, kx, vp, vc, vx, cq, sq, ckp, ckc, ckx, skp, skc, skx, pb_ref) = refs
    cat = lambda *r: jnp.concatenate([t[...] for t in r], axis=0)
    return (q_ref[...], z_ref[...], cat(kp, kc, kx), cat(vp, vc, vx), cq[...], sq[...], cat(ckp, ckc, ckx),
            cat(skp, skc, skx), pb_ref[...])


def _attn_b_fwd(proj, cos2, sin2, pb, mixed):
    s = proj.shape[0]
    nb, specs = _b_specs(s)

    def body(*refs):
        o_ref = refs[-1]
        args = _b_load(refs[:-2])
        o_ref[...] = _b_block(*args, pl.program_id(0), nb).astype(BF16)

    return pl.pallas_call(
        body, name="attn_b_fwd", grid=(nb,), in_specs=specs + [pl.BlockSpec(memory_space=pl.ANY)],
        out_specs=pl.BlockSpec((WINDOW, 512), lambda n: (n, A_WIDTH // 512)),
        out_shape=jax.ShapeDtypeStruct(mixed.shape, mixed.dtype), input_output_aliases={len(specs): 0},
        compiler_params=_params(("parallel",)),
    )(*_b_args(proj, cos2, sin2, pb), mixed)


def _attn_b_bwd(proj, cos2, sin2, pb, d_mixed):
    s = proj.shape[0]
    nb, specs = _b_specs(s)
    w = WINDOW

    def body(*refs):
        dm_ref, dq_ref, dz_ref, dk_ref, dv_ref, dpb_ref = refs[-6:]
        n = pl.program_id(0)
        q_t, z_t, k3, v3, cq, sq, ck, sk, pb_v = _b_load(refs[:-6])

        @pl.when(n == 0)
        def _():
            dk_ref[...] = jnp.zeros_like(dk_ref)
            dv_ref[...] = jnp.zeros_like(dv_ref)
            dpb_ref[...] = jnp.zeros_like(dpb_ref)

        def f(q_, z_, k_, v_, pb_):
            return _b_block(q_, z_, k_, v_, cq, sq, ck, sk, pb_, n, nb)

        _, vjp = jax.vjp(f, q_t, z_t, k3, v3, pb_v)
        dq, dz, dk3, dv3, dpb = vjp(dm_ref[...])
        dq_ref[...] = dq.astype(BF16)
        dz_ref[...] = dz.astype(BF16)
        dpb_ref[...] += dpb

        def add(j, cond):
            @pl.when(cond)
            def _():
                rows = pl.ds(pl.multiple_of((n - 1 + j) * w, w), w)
                dk_ref[rows, :] += dk3[j * w:(j + 1) * w, :]
                dv_ref[rows, :] += dv3[j * w:(j + 1) * w, :]

        add(0, n > 0)
        add(1, n >= 0)
        add(2, n < nb - 1)

    blk = pl.BlockSpec((w, 512), lambda n: (n, 0))
    whole = pl.BlockSpec((s, LANE), lambda n: (0, 0))
    small = pl.BlockSpec((8, LANE), lambda n: (0, 0))
    return pl.pallas_call(
        body, name="attn_b_bwd", grid=(nb,),
        in_specs=specs + [pl.BlockSpec((w, 512), lambda n: (n, 2))],
        out_specs=[blk, blk, whole, whole, small],
        out_shape=[jax.ShapeDtypeStruct((s, 512), BF16), jax.ShapeDtypeStruct((s, 512), BF16),
                   jax.ShapeDtypeStruct((s, LANE), F32), jax.ShapeDtypeStruct((s, LANE), F32),
                   jax.ShapeDtypeStruct((8, LANE), F32)],
        compiler_params=_params(("arbitrary",)),
    )(*_b_args(proj, cos2, sin2, pb), d_mixed)


def _mem_kv_fwd(mem, mem_norm_w, w_kv):
    def body(mem_ref, nw_ref, w_ref, kv_ref):
        mn = _rms(mem_ref[...], nw_ref[...]).astype(BF16)
        kv_ref[...] = jnp.dot(mn, w_ref[...], preferred_element_type=F32)

    return pl.pallas_call(
        body, name="mem_kv_fwd", out_shape=jax.ShapeDtypeStruct((MEM_LEN, 2 * C_HEADS * C_DIM), F32),
        compiler_params=_params(),
    )(mem, mem_norm_w, w_kv)


def _mem_kv_bwd(mem, mem_norm_w, w_kv, d_kv):
    def body(mem_ref, nw_ref, w_ref, g_ref, gw_ref, gn_ref):
        mn, vjp = jax.vjp(_rms, mem_ref[...], nw_ref[...])
        g = g_ref[...].astype(BF16)
        gw_ref[...] = lax.dot_general(mn.astype(BF16), g, (((0,), (0,)), ((), ())), preferred_element_type=F32)
        d_mn = lax.dot_general(g, w_ref[...], (((1,), (1,)), ((), ())), preferred_element_type=F32)
        gn_ref[...] = vjp(d_mn)[1]

    return pl.pallas_call(
        body, name="mem_kv_bwd",
        out_shape=[jax.ShapeDtypeStruct((D_MODEL, 2 * C_HEADS * C_DIM), F32), jax.ShapeDtypeStruct((1, D_MODEL), F32)],
        compiler_params=_params(),
    )(mem, mem_norm_w, w_kv, d_kv)


def _c_tile(q_t, z_t, kvm, pc):
    width = C_HEADS * C_DIM
    outs = []
    for h in range(C_HEADS):
        q = _rms(q_t[:, h * C_DIM:(h + 1) * C_DIM], pc[0:1, :])
        k = _rms(kvm[:, h * C_DIM:(h + 1) * C_DIM], pc[1:2, :])
        v = kvm[:, width + h * C_DIM:width + (h + 1) * C_DIM]
        s = _mm_nt(q, k) * (C_DIM ** -0.5)
        p = jnp.exp(s - jnp.max(s, axis=1, keepdims=True))
        p = p / jnp.sum(p, axis=1, keepdims=True)
        outs.append(_mm(p, v))
    return jnp.concatenate(outs, axis=1) * _silu(z_t)


def _attn_c_fwd(proj, kvm, pc, mixed, tq=256):
    s = proj.shape[0]

    def body(q_ref, z_ref, kv_ref, pc_ref, mixed_ref, o_ref):
        o_ref[...] = _c_tile(q_ref[...], z_ref[...], kv_ref[...], pc_ref[...]).astype(BF16)

    return pl.pallas_call(
        body, name="attn_c_fwd", grid=(s // tq,),
        in_specs=[pl.BlockSpec((tq, 512), lambda i: (i, P_QC // 512)), pl.BlockSpec((tq, 512), lambda i: (i, P_ZC // 512)),
                  pl.BlockSpec(kvm.shape, lambda i: (0, 0)), pl.BlockSpec((8, LANE), lambda i: (0, 0)),
                  pl.BlockSpec(memory_space=pl.ANY)],
        out_specs=pl.BlockSpec((tq, 512), lambda i: (i, (A_WIDTH + 512) // 512)),
        out_shape=jax.ShapeDtypeStruct(mixed.shape, mixed.dtype), input_output_aliases={4: 0},
        compiler_params=_params(("parallel",)),
    )(proj, proj, kvm, pc, mixed)


def _attn_c_bwd(proj, kvm, pc, d_mixed, tq=256):
    s = proj.shape[0]

    def body(q_ref, z_ref, kv_ref, pc_ref, dm_ref, dq_ref, dz_ref, dkv_ref, dpc_ref):
        @pl.when(pl.program_id(0) == 0)
        def _():
            dkv_ref[...] = jnp.zeros_like(dkv_ref)
            dpc_ref[...] = jnp.zeros_like(dpc_ref)

        _, vjp = jax.vjp(_c_tile, q_ref[...], z_ref[...], kv_ref[...], pc_ref[...])
        dq, dz, dkv, dpc = vjp(dm_ref[...])
        dq_ref[...] = dq.astype(BF16)
        dz_ref[...] = dz.astype(BF16)
        dkv_ref[...] += dkv
        dpc_ref[...] += dpc

    blk = pl.BlockSpec((tq, 512), lambda i: (i, 0))
    kvs = pl.BlockSpec(kvm.shape, lambda i: (0, 0))
    small = pl.BlockSpec((8, LANE), lambda i: (0, 0))
    return pl.pallas_call(
        body, name="attn_c_bwd", grid=(s // tq,),
        in_specs=[pl.BlockSpec((tq, 512), lambda i: (i, P_QC // 512)), pl.BlockSpec((tq, 512), lambda i: (i, P_ZC // 512)),
                  kvs, small, pl.BlockSpec((tq, 512), lambda i: (i, 3))],
        out_specs=[blk, blk, kvs, small],
        out_shape=[jax.ShapeDtypeStruct((s, 512), BF16), jax.ShapeDtypeStruct((s, 512), BF16),
                   jax.ShapeDtypeStruct(kvm.shape, F32), jax.ShapeDtypeStruct((8, LANE), F32)],
        compiler_params=_params(("arbitrary",)),
    )(proj, proj, kvm, pc, d_mixed)


def _pad_row(v, width=LANE):
    v = v.reshape(1, -1)
    return jnp.pad(v, ((0, 0), (0, width - v.shape[1])))


def _local_step(x, mem, target, norm_w, w_perm_t, w_blocks_t, conv_w, pa, pb, pc, mem_norm_w, w_kv, w_out, gather=None,
                exchange=None):
    s = x.shape[0]
    cos2, sin2 = _rope_tables(s)
    hn = _rms_fwd(x, norm_w)
    wide = dict(tm=1024, tn=512, tk=2048)
    proj = _matmul(hn, w_perm_t, "nt", F32, "mm_proj", **wide)
    cqkv = _conv_fwd(proj, conv_w)
    if gather is None:
        mixed, o_sum, states, inverses = _delta_fwd(cqkv, proj, pa)
    else:
        mixed, o_sum, states, inverses, *arrived = _delta_fwd(cqkv, proj, pa, gather[0])
        w_out, w_kv = gather[1](*arrived)
    mixed = _attn_b_fwd(proj, cos2, sin2, pb, mixed)
    kvm = _mem_kv_fwd(mem, mem_norm_w, w_kv)
    mixed = _attn_c_fwd(proj, kvm, pc, mixed)
    mo = _matmul(mixed, w_out, "nn", F32, "mm_out", **wide)
    dy, dyb, loss_parts = _loss_dy(x, mo, target)

    d_mixed = _matmul(dyb, w_out, "nt", F32, "mm_dmixed", **wide)
    g_w_out = _matmul(mixed, dyb, "tn", F32, "mm_gwout", **wide)
    d_qc, d_zc, d_kvm, d_pc = _attn_c_bwd(proj, kvm, pc, d_mixed)
    g_w_kv, g_mem_norm = _mem_kv_bwd(mem, mem_norm_w, w_kv, d_kvm)
    d_qb, d_zb, d_kb, d_vb, d_pb = _attn_b_bwd(proj, cos2, sin2, pb, d_mixed)
    d_o, d_za, d_pa_out = _delta_out_bwd(o_sum, proj, pa, d_mixed)
    early = exchange[0](g_w_out, g_w_kv) if exchange else None
    d_c, d_gt, d_pa_scan, *landed_early = _delta_bwd(cqkv, proj, pa, d_o, states, inverses, early)
    d_pa = d_pa_out + d_pa_scan
    d_qkv, g_conv = _conv_bwd(proj, conv_w, d_c)
    d_proj = _cotangent_blocks(d_qkv, d_za, d_gt, d_qb, d_kb, d_vb, d_zb, d_qc, d_zc)
    g_w_blocks_t = _matmul(d_proj, hn, "tn", F32, "mm_gwin", tm=512, tn=2048, tk=2048)
    late = exchange[1](g_w_blocks_t) if exchange else None
    d_hn = _matmul(d_proj, w_blocks_t, "nn", F32, "mm_dhn", tm=1024, tn=2048, tk=512, ride=late)
    d_hn, landed_late = (d_hn[0], list(d_hn[1:])) if late else (d_hn, [])
    g_x, g_norm = _rms_bwd(x, norm_w, d_hn, dy)
    return dict(loss_parts=loss_parts, g_x=g_x, g_norm=g_norm, g_w_blocks_t=g_w_blocks_t, g_conv=g_conv, d_pa=d_pa,
                d_pb=d_pb, d_pc=d_pc, g_mem_norm=g_mem_norm, g_w_kv=g_w_kv, g_w_out=g_w_out,
                landed=landed_late + landed_early)


_SEGMENTS = ((0, O_GT, 0), (O_GT, O_QB, P_GT), (O_QB, O_KB, P_QB), (O_KB, O_VB, P_KB), (O_VB, O_ZB, P_VB),
             (O_ZB, O_QC, P_ZB), (O_QC, O_ZC, P_QC), (O_ZC, IN_WIDTH, P_ZC))


def _permute_blocks(w4):
    parts = []
    for first, end, _ in sorted(_SEGMENTS, key=lambda seg: seg[2]):
        row = first
        while row < end:
            k = row // W_IN_BLOCK
            stop = min(end, (k + 1) * W_IN_BLOCK)
            parts.append(w4[k][row - k * W_IN_BLOCK:stop - k * W_IN_BLOCK, :])
            row = stop
    parts.append(jnp.zeros((P_WIDTH - IN_WIDTH, w4.shape[2]), w4.dtype))
    return jnp.concatenate(parts, axis=0)


def _cotangent_blocks(d_qkv, d_za, d_gt, d_qb, d_kb, d_vb, d_zb, d_qc, d_zc):
    pieces = [d_qkv, d_za, d_gt[:, :O_QB - O_GT], d_qb, d_kb, d_vb, d_zb, d_qc, d_zc]
    orig = jnp.concatenate([t.astype(BF16) for t in pieces], axis=1)
    pad = jnp.zeros((orig.shape[0], W_IN_PAD - W_IN_BLOCK), BF16)
    parts = []
    for k in range(N_CHIPS):
        parts += [orig[:, k * W_IN_BLOCK:(k + 1) * W_IN_BLOCK], pad]
    return jnp.concatenate(parts, axis=1)


HBM = pl.BlockSpec(memory_space=pltpu.HBM)


def _place():
    x, y, c = lax.axis_index("x"), lax.axis_index("y"), lax.axis_index("c")
    chips = [(1 - x, y), (x, 1 - y), (1 - x, 1 - y)]
    return x, y, c, 2 * x + y, chips, [2 * cx + cy for cx, cy in chips]


PIECE_ROWS_CAP = 600


def _remote(src, dst, send_sems, recv_sems, k, to):
    return pltpu.make_async_remote_copy(src_ref=src, dst_ref=dst, send_sem=send_sems.at[k], recv_sem=recv_sems.at[k],
                                        device_id=to, device_id_type=MESH)


def _half_cols(ref, c):
    half = ref.shape[-1] // 2
    return pl.ds(pl.multiple_of(c * half, LANE), half)


class _PairedGather:
    def __init__(self, blocks):
        n = len(blocks)
        self.operands = list(blocks)
        self.out_shapes = [jax.ShapeDtypeStruct((N_CHIPS,) + b.shape, b.dtype) for b in blocks]
        self.scratch_shapes = [pltpu.SemaphoreType.DMA((6 * n,)), pltpu.SemaphoreType.DMA((6 * n,))]

    @staticmethod
    def _copies(srcs, dsts, sems):
        x, y, c, me, chips, chip_ids = _place()
        sends, landed, passes, passed = [], [], [], []
        for a, (src, dst) in enumerate(zip(srcs, dsts)):
            mine, other = _half_cols(src, c), _half_cols(src, 1 - c)
            for j, (chip, cid) in enumerate(zip(chips, chip_ids)):
                sends.append(_remote(src.at[:, mine], dst.at[me, :, mine], sems[0], sems[1], 6 * a + j, (*chip, c)))
                here = dst.at[cid, :, mine]
                landed.append(_remote(here, here, sems[0], sems[1], 6 * a + j, (x, y, 1 - c)))
                passes.append(_remote(here, here, sems[0], sems[1], 6 * a + 3 + j, (x, y, 1 - c)))
                there = dst.at[cid, :, other]
                passed.append(_remote(there, there, sems[0], sems[1], 6 * a + 3 + j, (x, y, 1 - c)))
        return sends, landed, passes, passed

    def start(self, srcs, dsts, sems):
        for cp in self._copies(srcs, dsts, sems)[0]:
            cp.start()

    def middle(self, srcs, dsts, sems):
        _, landed, passes, _ = self._copies(srcs, dsts, sems)
        for arrived, onward in zip(landed, passes):
            arrived.wait_recv()
            onward.start()

    def finish(self, srcs, dsts, sems):
        sends, _, passes, passed = self._copies(srcs, dsts, sems)
        for cp in passed:
            cp.wait_recv()
        for cp in sends + passes:
            cp.wait_send()


def _all_gather_weights(bigs, conv_b):
    bigs = tuple(bigs)
    n_big = len(bigs)

    def body(*refs):
        srcs, conv_src = refs[:n_big], refs[n_big]
        dsts, conv_dst = refs[n_big + 1:2 * n_big + 1], refs[2 * n_big + 1]
        send_sems, recv_sems, local_sems = refs[2 * n_big + 2:]
        x, y, c, me, chips, chip_ids = _place()
        sibling = (x, y, 1 - c)
        local = [pltpu.make_async_copy(src, dst.at[me], local_sems.at[a]) for a, (src, dst) in enumerate(zip(srcs, dsts))]
        local.append(pltpu.make_async_copy(conv_src, conv_dst.at[me], local_sems.at[n_big]))
        for cp in local:
            cp.start()
        sends = []
        for a, (src, dst) in enumerate(zip(srcs, dsts)):
            mine = _half_cols(src, c)
            for j, chip in enumerate(chips):
                sends.append(_remote(src.at[:, mine], dst.at[me, :, mine], send_sems, recv_sems, 6 * a + j, (*chip, c)))
        for j, chip in enumerate(chips):
            sends.append(_remote(conv_src, conv_dst.at[me], send_sems, recv_sems, 6 * n_big + j, (*chip, c)))
        for cp in sends:
            cp.start()
        passed = []
        for a, (src, dst) in enumerate(zip(srcs, dsts)):
            mine = _half_cols(src, c)
            for j, cid in enumerate(chip_ids):
                landed = dst.at[cid, :, mine]
                _remote(landed, landed, send_sems, recv_sems, 6 * a + j, sibling).wait_recv()
                cp = _remote(landed, landed, send_sems, recv_sems, 6 * a + 3 + j, sibling)
                cp.start()
                passed.append(cp)
        for a, (src, dst) in enumerate(zip(srcs, dsts)):
            other = _half_cols(src, 1 - c)
            for j, cid in enumerate(chip_ids):
                landed = dst.at[cid, :, other]
                _remote(landed, landed, send_sems, recv_sems, 6 * a + 3 + j, sibling).wait_recv()
        for j, cid in enumerate(chip_ids):
            _remote(conv_src, conv_dst.at[cid], send_sems, recv_sems, 6 * n_big + j, sibling).wait_recv()
        for cp in sends + passed:
            cp.wait_send()
        for cp in local:
            cp.wait()

    n_sem = 6 * n_big + 3
    return pl.pallas_call(
        body, name="all_gather_weights",
        out_shape=[jax.ShapeDtypeStruct((N_CHIPS,) + w.shape, w.dtype) for w in bigs + (conv_b,)],
        in_specs=[pl.BlockSpec(memory_space=pltpu.VMEM)] * (n_big + 1), out_specs=[HBM] * (n_big + 1),
        scratch_shapes=[pltpu.SemaphoreType.DMA((n_sem,)), pltpu.SemaphoreType.DMA((n_sem,)),
                        pltpu.SemaphoreType.DMA((n_big + 1,))],
        compiler_params=_params(),
    )(*bigs, conv_b)


def _pair_exchange(grads, name):
    n = len(grads)
    pieces = [_row_tile(g.shape[1]) for g in grads]

    def body(*refs):
        srcs, gots = refs[:n], refs[n:2 * n]
        stages = refs[2 * n:3 * n]
        send_sems, recv_sems, load_sems = refs[3 * n:]
        x, y, c, _, _, _ = _place()
        sibling = (x, y, 1 - c)
        for a in range(n):
            slabs, rows, _ = gots[a].shape
            piece = pieces[a]
            per_slab = rows // piece
            theirs = _half_cols(srcs[a], 1 - c)
            loads, sends = [], []
            for i in range(slabs * per_slab):
                k, r, slot = i // per_slab, i % per_slab, i % 2
                part = pl.ds(r * piece, piece)
                loads.append(pltpu.make_async_copy(srcs[a].at[k, part, theirs], stages[a].at[slot], load_sems.at[2 * a + slot]))
                sends.append(pltpu.make_async_remote_copy(
                    src_ref=stages[a].at[slot], dst_ref=gots[a].at[k, part, :],
                    send_sem=send_sems.at[2 * a + slot], recv_sem=recv_sems.at[a], device_id=sibling, device_id_type=MESH))
            loads[0].start()
            for i in range(len(loads)):
                loads[i].wait()
                sends[i].start()
                if i + 1 < len(loads):
                    if i >= 1:
                        sends[i - 1].wait_send()
                    loads[i + 1].start()
            for cp in sends[-2:]:
                cp.wait_send()
        for a in range(n):
            whole = srcs[a].at[:, :, _half_cols(srcs[a], c)]
            pltpu.make_async_remote_copy(src_ref=whole, dst_ref=gots[a], send_sem=send_sems.at[2 * a],
                                         recv_sem=recv_sems.at[a], device_id=sibling, device_id_type=MESH).wait_recv()

    halves = [jax.ShapeDtypeStruct((g.shape[0], g.shape[1], g.shape[2] // 2), g.dtype) for g in grads]
    return pl.pallas_call(
        body, name=name, out_shape=halves, in_specs=[HBM] * n, out_specs=[HBM] * n,
        scratch_shapes=[pltpu.VMEM((2, piece, g.shape[2] // 2), g.dtype) for piece, g in zip(pieces, grads)]
        + [pltpu.SemaphoreType.DMA((2 * n,)), pltpu.SemaphoreType.DMA((n,)), pltpu.SemaphoreType.DMA((2 * n,))],
        compiler_params=_params(),
    )(*grads)


class _ChipExchange:
    def __init__(self, halves):
        n = len(halves)
        self.operands = list(halves)
        self.out_shapes = [jax.ShapeDtypeStruct((N_CHIPS - 1,) + h.shape[1:], h.dtype) for h in halves]
        self.scratch_shapes = [pltpu.SemaphoreType.DMA((3 * n,)), pltpu.SemaphoreType.DMA((3 * n,))]

    @staticmethod
    def _copies(srcs, lands, sems):
        _, _, c, _, chips, chip_ids = _place()
        return [_remote(src.at[cid], land.at[j], sems[0], sems[1], 3 * a + j, (*chip, c))
                for a, (src, land) in enumerate(zip(srcs, lands)) for j, (chip, cid) in enumerate(zip(chips, chip_ids))]

    def start(self, srcs, lands, sems):
        for cp in self._copies(srcs, lands, sems):
            cp.start()

    def finish(self, srcs, lands, sems):
        copies = self._copies(srcs, lands, sems)
        for cp in copies:
            cp.wait_recv()
        for cp in copies:
            cp.wait_send()


def _pair_gather(halves, rows):
    n = len(halves)

    def body(*refs):
        srcs, fulls = refs[:n], refs[n:2 * n]
        send_sems, recv_sems, local_sems = refs[2 * n:]
        x, y, c, _, _, _ = _place()
        copies = []
        for a in range(n):
            mine, src = _half_cols(fulls[a], c), srcs[a].at[pl.ds(0, rows[a]), :]
            keep = pltpu.make_async_copy(src, fulls[a].at[:, mine], local_sems.at[a])
            keep.start()
            give = _remote(src, fulls[a].at[:, mine], send_sems, recv_sems, a, (x, y, 1 - c))
            give.start()
            copies += [keep, give]
        for a in range(n):
            other, src = _half_cols(fulls[a], 1 - c), srcs[a].at[pl.ds(0, rows[a]), :]
            copies[2 * a].wait()
            copies[2 * a + 1].wait_send()
            _remote(src, fulls[a].at[:, other], send_sems, recv_sems, a, (x, y, 1 - c)).wait_recv()

    return pl.pallas_call(
        body, name="grad_pair_gather",
        out_shape=[jax.ShapeDtypeStruct((r, 2 * h.shape[1]), h.dtype) for r, h in zip(rows, halves)],
        in_specs=[pl.BlockSpec(memory_space=pltpu.VMEM)] * n, out_specs=[HBM] * n,
        scratch_shapes=[pltpu.SemaphoreType.DMA((n,)), pltpu.SemaphoreType.DMA((n,)), pltpu.SemaphoreType.DMA((n,))],
    )(*halves)


def _all_reduce_small(p):
    n_dev = 8

    def body(p_ref, o_ref, land, send_sems, recv_sems):
        x, y, c = lax.axis_index("x"), lax.axis_index("y"), lax.axis_index("c")
        me = 4 * x + 2 * y + c
        land[me] = p_ref[...]
        sends = []
        for k in range(1, n_dev):
            fx, fy, fc = (k >> 2) & 1, (k >> 1) & 1, k & 1
            to = (x ^ fx, y ^ fy, c ^ fc)
            cp = _remote(p_ref, land.at[me], send_sems, recv_sems, k - 1, to)
            cp.start()
            sends.append(cp)
        for k in range(1, n_dev):
            _remote(p_ref, land.at[me ^ k], send_sems, recv_sems, k - 1, (x, y, c)).wait_recv()
        total = land[0]
        for d in range(1, n_dev):
            total = total + land[d]
        o_ref[...] = total
        for cp in sends:
            cp.wait_send()

    vm = pl.BlockSpec(memory_space=pltpu.VMEM)
    return pl.pallas_call(
        body, name="all_reduce_small", out_shape=jax.ShapeDtypeStruct(p.shape, p.dtype), in_specs=[vm], out_specs=vm,
        scratch_shapes=[pltpu.VMEM((n_dev,) + p.shape, p.dtype), pltpu.SemaphoreType.DMA((n_dev - 1,)),
                        pltpu.SemaphoreType.DMA((n_dev - 1,))],
    )(p)


def _row_tile(rows):
    fits = [t for t in range(8, min(rows, PIECE_ROWS_CAP) + 1, 8) if rows % t == 0]
    return max(fits) if fits else rows


def _pair_sum(full, got, core, name):
    n, r, c = got.shape
    tr = _row_tile(r)

    def body(core_ref, a_ref, b_ref, o_ref):
        o_ref[...] = (a_ref[...] + b_ref[...]).astype(BF16)

    blk = pl.BlockSpec((None, tr, c), lambda i, j, core_ref: (i, j, 0))
    grid_spec = pltpu.PrefetchScalarGridSpec(
        num_scalar_prefetch=1, grid=(n, r // tr),
        in_specs=[pl.BlockSpec((None, tr, c), lambda i, j, core_ref: (i, j, core_ref[0])), blk], out_specs=blk)
    return pl.pallas_call(body, name=name, grid_spec=grid_spec, out_shape=jax.ShapeDtypeStruct(got.shape, BF16),
                          compiler_params=_params(("parallel", "parallel")))(core, full, got)


def _chip_sum(full, got, land, place, name):
    n, r, c = land.shape
    tr = _row_tile(r)

    def body(place_ref, a_ref, b_ref, l_ref, o_ref):
        total = a_ref[...] + b_ref[...]
        for j in range(n):
            total = total + l_ref[j].astype(F32)
        o_ref[...] = total

    grid_spec = pltpu.PrefetchScalarGridSpec(
        num_scalar_prefetch=1, grid=(r // tr,),
        in_specs=[pl.BlockSpec((None, tr, c), lambda i, p: (p[0], i, p[1])),
                  pl.BlockSpec((None, tr, c), lambda i, p: (p[0], i, 0)),
                  pl.BlockSpec((n, tr, c), lambda i, p: (0, i, 0))],
        out_specs=pl.BlockSpec((tr, c), lambda i, p: (i, 0)))
    return pl.pallas_call(body, name=name, grid_spec=grid_spec, out_shape=jax.ShapeDtypeStruct((r, c), F32),
                          compiler_params=_params(("parallel",)))(place, full, got, land)


def _adamw(w, g, m, v, name):
    r, c = w.shape
    tr = _row_tile(r)
    tc = 1024 if c % 1024 == 0 else c

    def body(w_ref, g_ref, m_ref, v_ref, d_ref, mo_ref, vo_ref):
        g_ = g_ref[...]
        m2 = ADAM_B1 * m_ref[...] + (1.0 - ADAM_B1) * g_
        v2 = ADAM_B2 * v_ref[...] + (1.0 - ADAM_B2) * jnp.square(g_)
        m_hat = m2 / (1.0 - ADAM_B1 ** ADAM_STEP)
        v_hat = v2 / (1.0 - ADAM_B2 ** ADAM_STEP)
        d_ref[...] = -ADAM_LR * (m_hat / (jnp.sqrt(v_hat) + ADAM_EPS) + ADAM_WD * w_ref[...])
        mo_ref[...] = m2
        vo_ref[...] = v2

    blk = pl.BlockSpec((tr, tc), lambda i, j: (i, j))
    return pl.pallas_call(body, name=name, grid=(r // tr, c // tc), in_specs=[blk] * 4, out_specs=[blk] * 3,
                          out_shape=[jax.ShapeDtypeStruct(w.shape, F32)] * 3,
                          compiler_params=_params(("parallel", "parallel")))(w, g, m, v)


SMALL_NAMES = ("norm_w", "mem_norm_w", "o_norm_a", "q_norm_c", "k_norm_c", "q_norm_b", "k_norm_b",
               "a_log_fwd", "a_log_bwd", "dt_bias_fwd", "dt_bias_bwd", "sink_b")
SMALL_SIZES = (2048, 2048, 128, 128, 128, 64, 64, 8, 8, 8, 8, 8)
SMALL_LOSS = sum(SMALL_SIZES)
SMALL_CONV = 5120
SMALL_TOTAL = SMALL_CONV + CONV_K * 3 * A_WIDTH
SMALL_ROWS = SMALL_TOTAL // LANE


def _pack_small(parts, extra=None, conv=None):
    vec = [parts[n].reshape(-1) for n in SMALL_NAMES]
    vec.append(jnp.zeros((1,), F32) if extra is None else extra.reshape(1))
    vec.append(jnp.zeros((SMALL_CONV - SMALL_LOSS - 1,), F32))
    vec.append(jnp.zeros((SMALL_TOTAL - SMALL_CONV,), F32) if conv is None else conv.reshape(-1))
    return jnp.concatenate(vec).reshape(SMALL_ROWS, LANE)


def _unpack_small(packed):
    flat = packed.reshape(-1)
    out, off = {}, 0
    for n, size in zip(SMALL_NAMES, SMALL_SIZES):
        out[n] = flat[off:off + size].reshape(1, size)
        off += size
    return out


WEIGHT_ORDER = ("norm_w", "w_in", "conv_w_a", "a_log_fwd", "a_log_bwd", "dt_bias_fwd", "dt_bias_bwd", "o_norm_a",
                "q_norm_b", "k_norm_b", "sink_b", "mem_norm_w", "w_mem_kv", "q_norm_c", "k_norm_c", "w_out")


def kernel(x, mem, norm_w, w_in, conv_w_a, a_log_fwd, a_log_bwd, dt_bias_fwd, dt_bias_bwd, o_norm_a, q_norm_b, k_norm_b, sink_b, mem_norm_w, w_mem_kv, q_norm_c, k_norm_c, w_out, loss_target, m_norm_w, m_w_in, m_conv_w_a, m_a_log_fwd, m_a_log_bwd, m_dt_bias_fwd, m_dt_bias_bwd, m_o_norm_a, m_q_norm_b, m_k_norm_b, m_sink_b, m_mem_norm_w, m_w_mem_kv, m_q_norm_c, m_k_norm_c, m_w_out, v_norm_w, v_w_in, v_conv_w_a, v_a_log_fwd, v_a_log_bwd, v_dt_bias_fwd, v_dt_bias_bwd, v_o_norm_a, v_q_norm_b, v_k_norm_b, v_sink_b, v_mem_norm_w, v_w_mem_kv, v_q_norm_c, v_k_norm_c, v_w_out):
    weights = dict(norm_w=norm_w, w_in=w_in, conv_w_a=conv_w_a, a_log_fwd=a_log_fwd, a_log_bwd=a_log_bwd,
                   dt_bias_fwd=dt_bias_fwd, dt_bias_bwd=dt_bias_bwd, o_norm_a=o_norm_a, q_norm_b=q_norm_b,
                   k_norm_b=k_norm_b, sink_b=sink_b, mem_norm_w=mem_norm_w, w_mem_kv=w_mem_kv, q_norm_c=q_norm_c,
                   k_norm_c=k_norm_c, w_out=w_out)
    mom1 = dict(norm_w=m_norm_w, w_in=m_w_in, conv_w_a=m_conv_w_a, a_log_fwd=m_a_log_fwd, a_log_bwd=m_a_log_bwd,
                dt_bias_fwd=m_dt_bias_fwd, dt_bias_bwd=m_dt_bias_bwd, o_norm_a=m_o_norm_a, q_norm_b=m_q_norm_b,
                k_norm_b=m_k_norm_b, sink_b=m_sink_b, mem_norm_w=m_mem_norm_w, w_mem_kv=m_w_mem_kv,
                q_norm_c=m_q_norm_c, k_norm_c=m_k_norm_c, w_out=m_w_out)
    mom2 = dict(norm_w=v_norm_w, w_in=v_w_in, conv_w_a=v_conv_w_a, a_log_fwd=v_a_log_fwd, a_log_bwd=v_a_log_bwd,
                dt_bias_fwd=v_dt_bias_fwd, dt_bias_bwd=v_dt_bias_bwd, o_norm_a=v_o_norm_a, q_norm_b=v_q_norm_b,
                k_norm_b=v_k_norm_b, sink_b=v_sink_b, mem_norm_w=v_mem_norm_w, w_mem_kv=v_w_mem_kv,
                q_norm_c=v_q_norm_c, k_norm_c=v_k_norm_c, w_out=v_w_out)
    chip = 2 * lax.axis_index("x") + lax.axis_index("y")

    own_in = jnp.pad(jnp.transpose(w_in[0]).astype(BF16), ((0, W_IN_PAD - W_IN_BLOCK), (0, 0)))
    w_in4, conv4 = _all_gather_weights([own_in], conv_w_a[0])
    w_perm_t = _permute_blocks(w_in4)
    w_blocks_t = w_in4.reshape(N_CHIPS * W_IN_PAD, D_MODEL)
    conv_full = jnp.transpose(conv4, (1, 0, 2)).reshape(CONV_K, 3 * A_WIDTH)
    own_out, own_kv = w_out[0].astype(BF16), w_mem_kv[0].astype(BF16)

    def assemble(w_out4, w_kv4):
        w_out4 = lax.dynamic_update_index_in_dim(w_out4, own_out, chip, 0)
        w_kv4 = lax.dynamic_update_index_in_dim(w_kv4, own_kv, chip, 0)
        return w_out4.reshape(D_MODEL, D_MODEL), w_kv4.reshape(D_MODEL, 2 * C_HEADS * C_DIM)

    gather = (_PairedGather([own_out, own_kv]), assemble)
    pa = jnp.concatenate([_pad_row(a_log_fwd), _pad_row(a_log_bwd), _pad_row(dt_bias_fwd), _pad_row(dt_bias_bwd),
                          _pad_row(o_norm_a), jnp.zeros((3, LANE), F32)], axis=0)
    pb = jnp.concatenate([_pad_row(q_norm_b), _pad_row(k_norm_b), _pad_row(sink_b), jnp.zeros((5, LANE), F32)], axis=0)
    pc = jnp.concatenate([_pad_row(q_norm_c), _pad_row(k_norm_c), jnp.zeros((6, LANE), F32)], axis=0)

    full, got = {}, {}
    core = lax.axis_index("c").astype(jnp.int32).reshape(1)

    def pair_round(tag, blocks):
        names = [tag + "_%d" % i for i in range(len(blocks))]
        full.update(zip(names, blocks))
        got.update(zip(names, _pair_exchange(blocks, "grad_pair_exchange_" + tag)))
        return _ChipExchange([_pair_sum(full[n], got[n], core, "grad_pair_sum_" + n) for n in names])

    def early(g_w_out, g_w_kv):
        return pair_round("early", [g_w_out.reshape(N_CHIPS, D_MODEL // N_CHIPS, D_MODEL),
                                    g_w_kv.reshape(N_CHIPS, D_MODEL // N_CHIPS, 2 * C_HEADS * C_DIM)])

    def late(g_w_blocks_t):
        return pair_round("late", [g_w_blocks_t.reshape(N_CHIPS, W_IN_PAD, D_MODEL)])

    r = _local_step(x[0], mem[0], loss_target[0], norm_w, w_perm_t, w_blocks_t, conv_full, pa, pb, pc, mem_norm_w, None, None,
                    gather, (early, late))
    place = jnp.stack([chip, lax.axis_index("c")]).astype(jnp.int32)
    reduced = [_chip_sum(full[n], got[n], l, place, "grad_chip_sum_" + n)
               for n, l in zip(("late_0", "early_0", "early_1"), r["landed"])]
    g_w_in_t, g_w_out, g_w_kv = _pair_gather(reduced, [W_IN_BLOCK, D_MODEL // N_CHIPS, D_MODEL // N_CHIPS])

    d_pa, d_pb, d_pc = r["d_pa"], r["d_pb"], r["d_pc"]
    small_g = dict(norm_w=r["g_norm"], mem_norm_w=r["g_mem_norm"], o_norm_a=d_pa[4], q_norm_c=d_pc[0], k_norm_c=d_pc[1],
                   q_norm_b=d_pb[0, :B_DIM], k_norm_b=d_pb[1, :B_DIM], a_log_fwd=d_pa[0, :A_HEADS],
                   a_log_bwd=d_pa[1, :A_HEADS], dt_bias_fwd=d_pa[2, :A_HEADS], dt_bias_bwd=d_pa[3, :A_HEADS],
                   sink_b=d_pb[2, :B_HEADS])
    packed = _all_reduce_small(_pack_small(small_g, jnp.sum(r["loss_parts"][:, 0, 0]), r["g_conv"]))
    flat = packed.reshape(-1)
    loss = flat[SMALL_LOSS]
    conv_sum = flat[SMALL_CONV:].reshape(CONV_K, 3 * A_WIDTH)
    conv_cols = 3 * A_WIDTH // N_CHIPS
    g_conv = lax.dynamic_slice(conv_sum, (0, chip * conv_cols), (CONV_K, conv_cols))

    grads = _unpack_small(packed)
    grads.update(w_in=jnp.transpose(g_w_in_t), w_mem_kv=g_w_kv, w_out=g_w_out, conv_w_a=g_conv)
    delta, new_m, new_v = {}, {}, {}
    for n in ("w_mem_kv", "w_out", "conv_w_a"):
        delta[n], new_m[n], new_v[n] = _adamw(weights[n][0], grads[n], mom1[n][0], mom2[n][0], "adamw_" + n)
    stepped = _adamw(jnp.transpose(w_in[0]), g_w_in_t, jnp.transpose(m_w_in[0]), jnp.transpose(v_w_in[0]), "adamw_w_in")
    delta["w_in"], new_m["w_in"], new_v["w_in"] = (jnp.transpose(t) for t in stepped)
    d_s, m_s, v_s = _adamw(_pack_small(weights), packed, _pack_small(mom1), _pack_small(mom2), "adamw_small")
    d_s, m_s, v_s = _unpack_small(d_s), _unpack_small(m_s), _unpack_small(v_s)
    for n in SMALL_NAMES:
        delta[n], new_m[n], new_v[n] = d_s[n], m_s[n], v_s[n]

    def shaped(tree):
        return [tree[n].reshape(weights[n].shape) for n in WEIGHT_ORDER]

    return (loss, r["g_x"].reshape(x.shape), *shaped(grads), *shaped(delta), *shaped(new_m), *shaped(new_v))
```

```python
import functools

import jax
import jax.numpy as jnp
from jax import lax
from jax.experimental import pallas as pl
from jax.experimental.pallas import tpu as pltpu

F32 = jnp.float32
BF16 = jnp.bfloat16
HI = lax.Precision.HIGHEST
MESH = pl.DeviceIdType.MESH

D_MODEL = 2048
A_WIDTH = 1024
A_HEADS = 8
A_DIM = 128
CONV_K = 5
CHUNK = 64
B_HEADS = 8
B_KV = 2
B_DIM = 64
WINDOW = 128
C_HEADS = 4
C_DIM = 128
MEM_LEN = 256
ROPE_THETA = 10000.0
EPS = 1e-6
IN_WIDTH = 6432
N_CHIPS = 4
W_IN_BLOCK = IN_WIDTH // N_CHIPS
W_IN_PAD = 1664

LANE = 128
P_QA, P_KA, P_VA, P_ZA = 0, 1024, 2048, 3072
P_QB, P_ZB, P_QC, P_ZC = 4096, 4608, 5120, 5632
P_KB, P_VB, P_GT = 6144, 6272, 6400
P_WIDTH = 6656
O_GT, O_QB, O_KB, O_VB, O_ZB, O_QC, O_ZC = 4096, 4128, 4640, 4768, 4896, 5408, 5920

ADAM_LR, ADAM_B1, ADAM_B2, ADAM_EPS, ADAM_WD, ADAM_STEP = 0.001, 0.9, 0.999, 1e-08, 0.01, 10

VMEM_LIMIT = 56 * 1024 * 1024


def _params(sem=None):
    return pltpu.CompilerParams(dimension_semantics=sem, vmem_limit_bytes=VMEM_LIMIT)


def _dot(a, b, dims=(((1,), (0,)), ((), ())), precision=HI):
    return lax.dot_general(a, b, dims, precision=precision, preferred_element_type=F32)


def _dot_nt(a, b, precision=HI):
    return _dot(a, b, (((1,), (1,)), ((), ())), precision)


def _dot_tn(a, b, precision=HI):
    return _dot(a, b, (((0,), (0,)), ((), ())), precision)


_NN = (((1,), (0,)), ((), ()))
_NT = (((1,), (1,)), ((), ()))
_TN = (((0,), (0,)), ((), ()))


def _bdot(a, b, dims):
    return lax.dot_general(a.astype(BF16), b.astype(BF16), dims, preferred_element_type=F32)


@jax.custom_vjp
def _mm(a, b):
    return _bdot(a, b, _NN)


_mm.defvjp(lambda a, b: (_bdot(a, b, _NN), (a, b)),
           lambda res, ct: (_bdot(ct, res[1], _NT), _bdot(res[0], ct, _TN)))


@jax.custom_vjp
def _mm_nt(a, b):
    return _bdot(a, b, _NT)


_mm_nt.defvjp(lambda a, b: (_bdot(a, b, _NT), (a, b)),
              lambda res, ct: (_bdot(ct, res[1], _NN), _bdot(ct, res[0], _TN)))


@jax.custom_vjp
def _mm_tn(a, b):
    return _bdot(a, b, _TN)


_mm_tn.defvjp(lambda a, b: (_bdot(a, b, _TN), (a, b)),
              lambda res, ct: (_bdot(res[1], ct, _NT), _bdot(res[0], ct, _NN)))


def _rms(t, w):
    return t * lax.rsqrt(jnp.mean(t * t, axis=-1, keepdims=True) + EPS) * w


def _l2(t):
    return t * lax.rsqrt(jnp.sum(t * t, axis=-1, keepdims=True) + EPS)


def _silu(t):
    return t * jax.nn.sigmoid(t)


def _softplus(t):
    return jnp.maximum(t, 0.0) + jnp.log1p(jnp.exp(-jnp.abs(t)))


def _matmul(a, b, mode, out_dtype, name, tm=512, tn=512, tk=512, ride=None):
    (m, k) = a.shape[::-1] if mode == "tn" else a.shape
    n = b.shape[0] if mode == "nt" else b.shape[1]
    tm, tn, tk = min(tm, m), min(tn, n), min(tk, k)
    assert m % tm == 0 and n % tn == 0 and k % tk == 0, (m, n, k, tm, tn, tk)
    if mode == "nn":
        a_spec = pl.BlockSpec((tm, tk), lambda i, j, kk: (i, kk))
        b_spec = pl.BlockSpec((tk, tn), lambda i, j, kk: (kk, j))
        dims = (((1,), (0,)), ((), ()))
    elif mode == "nt":
        a_spec = pl.BlockSpec((tm, tk), lambda i, j, kk: (i, kk))
        b_spec = pl.BlockSpec((tn, tk), lambda i, j, kk: (j, kk))
        dims = (((1,), (1,)), ((), ()))
    else:
        a_spec = pl.BlockSpec((tk, tm), lambda i, j, kk: (kk, i))
        b_spec = pl.BlockSpec((tk, tn), lambda i, j, kk: (kk, j))
        dims = (((0,), (0,)), ((), ()))
    nk = k // tk
    grid = (m // tm, n // tn, nk)
    n_in = len(ride.operands) if ride else 0
    n_out = len(ride.out_shapes) if ride else 0

    def body(*refs):
        a_ref, b_ref, o_ref = refs[0], refs[1], refs[2 + n_in]
        scratch = refs[3 + n_in + n_out:]
        step = (pl.program_id(0) * grid[1] + pl.program_id(1)) * nk + pl.program_id(2)
        riders = (refs[2:2 + n_in], refs[3 + n_in:3 + n_in + n_out], scratch[(0 if nk == 1 else 1):])
        if ride:
            pl.when(step == 0)(lambda: ride.start(*riders))
        if nk == 1:
            o_ref[...] = _bdot(a_ref[...], b_ref[...], dims).astype(out_dtype)
        else:
            acc_ref, kk = scratch[0], pl.program_id(2)

            @pl.when(kk == 0)
            def _():
                acc_ref[...] = jnp.zeros_like(acc_ref)

            acc_ref[...] += _bdot(a_ref[...], b_ref[...], dims)

            @pl.when(kk == nk - 1)
            def _():
                o_ref[...] = acc_ref[...].astype(out_dtype)
        if ride:
            pl.when(step == grid[0] * grid[1] * nk - 1)(lambda: ride.finish(*riders))

    out = pl.pallas_call(
        body, name=name, grid=grid,
        in_specs=[a_spec, b_spec] + [HBM] * n_in,
        out_specs=[pl.BlockSpec((tm, tn), lambda i, j, kk: (i, j))] + [HBM] * n_out,
        out_shape=[jax.ShapeDtypeStruct((m, n), out_dtype)] + (list(ride.out_shapes) if ride else []),
        scratch_shapes=([] if nk == 1 else [pltpu.VMEM((tm, tn), F32)]) + (list(ride.scratch_shapes) if ride else []),
        compiler_params=_params(("arbitrary",) * 3 if ride else ("parallel", "parallel", "arbitrary")),
    )(a, b, *(ride.operands if ride else []))
    return out if ride else out[0]


def _rms_fwd(x, w, tr=256):
    s, d = x.shape

    def body(x_ref, w_ref, o_ref):
        o_ref[...] = _rms(x_ref[...], w_ref[...]).astype(BF16)

    return pl.pallas_call(
        body, name="rms_fwd", grid=(s // tr,),
        in_specs=[pl.BlockSpec((tr, d), lambda i: (i, 0)), pl.BlockSpec((1, d), lambda i: (0, 0))],
        out_specs=pl.BlockSpec((tr, d), lambda i: (i, 0)),
        out_shape=jax.ShapeDtypeStruct((s, d), BF16), compiler_params=_params(("parallel",)),
    )(x, w)


def _rms_bwd(x, w, d_hn, dy, tr=256):
    s, d = x.shape

    def body(x_ref, w_ref, g_ref, dy_ref, gx_ref, gw_ref):
        _, vjp = jax.vjp(_rms, x_ref[...], w_ref[...])
        dx, dw = vjp(g_ref[...])
        gx_ref[...] = dy_ref[...] + dx

        @pl.when(pl.program_id(0) == 0)
        def _():
            gw_ref[...] = jnp.zeros_like(gw_ref)

        gw_ref[...] += dw

    row = pl.BlockSpec((tr, d), lambda i: (i, 0))
    vec = pl.BlockSpec((1, d), lambda i: (0, 0))
    return pl.pallas_call(
        body, name="rms_bwd", grid=(s // tr,), in_specs=[row, vec, row, row], out_specs=[row, vec],
        out_shape=[jax.ShapeDtypeStruct((s, d), F32), jax.ShapeDtypeStruct((1, d), F32)],
        compiler_params=_params(("arbitrary",)),
    )(x, w, d_hn, dy)


def _loss_dy(x, mo, target, tr=256):
    s, d = x.shape
    nt = s // tr

    def body(x_ref, mo_ref, t_ref, dy_ref, dyb_ref, l_ref):
        err = x_ref[...] + mo_ref[...] - t_ref[...]
        dy = err * (1.0 / d)
        dy_ref[...] = dy
        dyb_ref[...] = dy.astype(BF16)
        l_ref[...] = jnp.full(l_ref.shape, 0.5 * jnp.sum(jnp.sum(err * err, axis=1, keepdims=True) * (1.0 / d)), F32)

    row = pl.BlockSpec((tr, d), lambda i: (i, 0))
    return pl.pallas_call(
        body, name="loss_dy", grid=(nt,), in_specs=[row, row, row],
        out_specs=[row, row, pl.BlockSpec((1, 8, LANE), lambda i: (i, 0, 0))],
        out_shape=[jax.ShapeDtypeStruct((s, d), F32), jax.ShapeDtypeStruct((s, d), BF16),
                   jax.ShapeDtypeStruct((nt, 8, LANE), F32)],
        compiler_params=_params(("parallel",)),
    )(x, mo, target)


def _shift_rows(t, s):
    if s == 0:
        return t
    n = t.shape[0]
    rolled = pltpu.roll(t, (-s) % n, axis=0)
    idx = lax.broadcasted_iota(jnp.int32, t.shape, 0) + s
    return jnp.where((idx >= 0) & (idx < n), rolled, 0.0)


def _conv_fwd(proj, conv_w):
    s = proj.shape[0]
    nblk = 3 * A_WIDTH // LANE

    def body(x_ref, w_ref, o_ref):
        x = x_ref[...]
        acc = jnp.zeros_like(x)
        for j in range(CONV_K):
            acc = acc + w_ref[j:j + 1, :] * _shift_rows(x, j - CONV_K // 2)
        o_ref[...] = acc

    return pl.pallas_call(
        body, name="conv_fwd", grid=(nblk,),
        in_specs=[pl.BlockSpec((s, LANE), lambda i: (0, i)), pl.BlockSpec((CONV_K, LANE), lambda i: (0, i))],
        out_specs=pl.BlockSpec((None, s, LANE), lambda i: (i // A_HEADS, 0, i % A_HEADS)),
        out_shape=jax.ShapeDtypeStruct((3, s, A_WIDTH), F32), compiler_params=_params(("parallel",)),
    )(proj, conv_w)


def _conv_bwd(proj, conv_w, d_c):
    s = proj.shape[0]
    nblk = 3 * A_WIDTH // LANE

    def body(x_ref, w_ref, g_ref, dx_ref, dw_ref):
        x, g = x_ref[...], g_ref[...]
        acc = jnp.zeros_like(x)
        for j in range(CONV_K):
            off = j - CONV_K // 2
            acc = acc + w_ref[j:j + 1, :] * _shift_rows(g, -off)
            dw_ref[j:j + 1, :] = jnp.sum(_shift_rows(x, off) * g, axis=0, keepdims=True)
        dx_ref[...] = acc.astype(BF16)

    col = pl.BlockSpec((s, LANE), lambda i: (0, i))
    wsp = pl.BlockSpec((CONV_K, LANE), lambda i: (0, i))
    dsp = pl.BlockSpec((None, s, LANE), lambda i: (i // A_HEADS, 0, i % A_HEADS))
    return pl.pallas_call(
        body, name="conv_bwd", grid=(nblk,), in_specs=[col, wsp, dsp], out_specs=[col, wsp],
        out_shape=[jax.ShapeDtypeStruct((s, 3 * A_WIDTH), BF16), jax.ShapeDtypeStruct((CONV_K, 3 * A_WIDTH), F32)],
        compiler_params=_params(("parallel",)),
    )(proj, conv_w, d_c)


A_FWD_HEADS = 4
A_BWD_HEADS = 4


def _neumann_inverse(a):
    c = a.shape[-1]
    eye = (lax.broadcasted_iota(jnp.int32, (c, c), 0) == lax.broadcasted_iota(jnp.int32, (c, c), 1)).astype(F32)
    tinv = eye + a
    p = a
    for _ in range(5):
        p = _mm(p, p)
        tinv = tinv + _mm(tinv, p)
    return tinv


@jax.custom_vjp
def _unit_inverse(a):
    return _neumann_inverse(a)


def _unit_inverse_fwd(a):
    tinv = _neumann_inverse(a)
    return tinv, tinv


def _unit_inverse_bwd(tinv, ct):
    return (_bdot(_bdot(tinv, ct, _TN), tinv, _NT),)


_unit_inverse.defvjp(_unit_inverse_fwd, _unit_inverse_bwd)


@jax.custom_vjp
def _known_inverse(a, tinv):
    return tinv


_known_inverse.defvjp(lambda a, tinv: (tinv, tinv),
                      lambda tinv, ct: (_unit_inverse_bwd(tinv, ct)[0], jnp.zeros_like(tinv)))


def _a_chain(st, cq, ck, cv, alpha, beta_raw, a_log, dt_b, incl, strict, last, kept=None):
    c = CHUNK
    gb = -jnp.exp(a_log) * _softplus(alpha + dt_b)
    bb = jax.nn.sigmoid(beta_raw)
    q = _l2(_silu(cq)) * (A_DIM ** -0.5)
    k = _l2(_silu(ck))
    v = _silu(cv)

    gc = _dot(incl, jnp.broadcast_to(gb, (c, LANE)))
    tot = jnp.sum(gc * last, axis=0, keepdims=True)
    m1 = gc[:, :c]
    decay = incl * jnp.exp(incl * (m1 - m1.T))
    kb = k * bb
    vb = v * bb
    a = -(strict * decay * _mm_nt(kb, k))
    tinv = _unit_inverse(a) if kept is None else _known_inverse(a, kept)
    eg = jnp.exp(gc)
    u = _mm(tinv, vb)
    w = _mm(tinv, kb * eg)
    qk = _mm_nt(q, k) * decay
    v_new = u - _mm(w, st)
    o = _mm(q * eg, st) + _mm(qk, v_new)
    st_new = st * jnp.exp(tot) + _mm_tn(k * jnp.exp(tot - gc), v_new)
    return st_new, o, tinv


def _a_step(sts, cq, ck, cv, gts, pa, h0, kept=None):
    c = CHUNK
    lane = lax.broadcasted_iota(jnp.int32, (1, LANE), 1)
    ii = lax.broadcasted_iota(jnp.int32, (c, c), 0)
    jj = lax.broadcasted_iota(jnp.int32, (c, c), 1)
    row = lax.broadcasted_iota(jnp.int32, (c, 1), 0)

    def pick(t, col):
        return jnp.sum(jnp.where(lane == col, t, 0.0), axis=1, keepdims=True)

    alpha, beta_raw, a_log, dt_b, incl, strict, last = [], [], [], [], [], [], []
    for b in range(sts.shape[0]):
        h, rev = h0 + b // 2, b % 2
        alpha.append(pick(gts[b], h + 8 * rev))
        beta_raw.append(pick(gts[b], h + 16 + 8 * rev))
        a_log.append(pick(pa[rev:rev + 1, :], h))
        dt_b.append(pick(pa[2 + rev:3 + rev, :], h))
        incl.append(((ii <= jj) if rev else (ii >= jj)).astype(F32))
        strict.append(((ii < jj) if rev else (ii > jj)).astype(F32))
        last.append((row == (0 if rev else c - 1)).astype(F32))
    stack = lambda ts: jnp.concatenate([t[None] for t in ts], axis=0)
    return jax.vmap(_a_chain)(sts, cq, ck, cv, stack(alpha), stack(beta_raw), stack(a_log), stack(dt_b),
                              stack(incl), stack(strict), stack(last), kept)


def _a_final(o, za, pa):
    outs = []
    for j in range(o.shape[1] // A_DIM):
        ln = slice(j * A_DIM, (j + 1) * A_DIM)
        outs.append(_rms(o[:, ln], pa[4:5, :]) * _silu(za[:, ln]))
    return jnp.concatenate(outs, axis=1)


def _a_tiles(n, nchunk, heads):
    tiles = []
    for b in range(2 * heads):
        i = (nchunk - 1 - n) if b % 2 else n
        tiles.append((i, pl.ds(pl.multiple_of(i * CHUNK, CHUNK), CHUNK), slice((b // 2) * A_DIM, (b // 2 + 1) * A_DIM)))
    return tiles


def _a_load(tiles, c_ref, gt_ref):
    cq, ck, cv = (jnp.stack([c_ref[r, sl, ln] for _, sl, ln in tiles], axis=0) for r in range(3))
    return cq, ck, cv, jnp.stack([gt_ref[sl, :] for _, sl, _ in tiles], axis=0)


def _loop_by_two(n, step, init):
    assert n % 2 == 0
    return lax.fori_loop(0, n // 2, lambda m, carry: step(2 * m + 1, step(2 * m, carry, 0), 1), init)


def _a_scan(h0, heads, nchunk, c_ref, gt_ref, pa, of_ref, ob_ref, s_ref, t_ref):
    def step(n, sts, parity):
        tiles = _a_tiles(n, nchunk, heads)
        sts_new, o, tinv = _a_step(sts, *_a_load(tiles, c_ref, gt_ref), pa, h0)
        for b, (i, sl, ln) in enumerate(tiles):
            s_ref[b, i] = sts[b]
            t_ref[b, i] = tinv[b]
            (ob_ref if b % 2 else of_ref)[sl, ln] = o[b]
        return sts_new

    _loop_by_two(nchunk, step, jnp.zeros((2 * heads, A_DIM, A_DIM), F32))


def _a_specs(s, heads):
    wide = heads * A_DIM
    once = pl.Buffered(1)
    trio = pl.BlockSpec((3, s, wide), lambda g: (0, 0, g), pipeline_mode=once)
    gates = pl.BlockSpec((s, LANE), lambda g: (0, P_GT // LANE))
    small = pl.BlockSpec((8, LANE), lambda g: (0, 0))

    def cols(base):
        return pl.BlockSpec((s, wide), lambda g: (0, base // wide + g), pipeline_mode=once)

    state = pl.BlockSpec((2 * heads, s // CHUNK, A_DIM, A_DIM), lambda g: (g, 0, 0, 0), pipeline_mode=once)
    kept = pl.BlockSpec((2 * heads, s // CHUNK, CHUNK, CHUNK), lambda g: (g, 0, 0, 0), pipeline_mode=once)
    return wide, trio, gates, small, cols, state, kept


def _delta_fwd(cqkv, proj, pa, ride=None):
    s = cqkv.shape[1]
    nchunk = s // CHUNK
    heads = A_FWD_HEADS
    steps = A_HEADS // heads
    wide, trio, gates, small, cols, state, kept = _a_specs(s, heads)
    n_in = len(ride.operands) if ride else 0
    n_out = len(ride.out_shapes) if ride else 0

    def body(*refs):
        c_ref, gt_ref, za_ref, pa_ref = refs[:4]
        out_ref, o_ref, s_ref, t_ref = refs[4 + n_in:8 + n_in]
        ob_ref = refs[8 + n_in + n_out]
        riders = (refs[4:4 + n_in], refs[8 + n_in:8 + n_in + n_out], refs[9 + n_in + n_out:])
        g = pl.program_id(0)
        if ride:
            pl.when(g == 0)(lambda: ride.start(*riders))
            pl.when(g == steps - 1)(lambda: ride.middle(*riders))
        h0 = g * heads
        pa_v = pa_ref[...]
        _a_scan(h0, heads, nchunk, c_ref, gt_ref, pa_v, o_ref, ob_ref, s_ref, t_ref)
        o_ref[...] += ob_ref[...]
        out_ref[...] = _a_final(o_ref[...], za_ref[...], pa_v).astype(BF16)
        if ride:
            pl.when(g == steps - 1)(lambda: ride.finish(*riders))

    assert steps > 1
    return pl.pallas_call(
        body, name="delta_fwd", grid=(steps,),
        in_specs=[trio, gates, cols(P_ZA), small] + [HBM] * n_in,
        out_specs=[cols(0), cols(0), state, kept] + [HBM] * n_out,
        out_shape=[jax.ShapeDtypeStruct((s, D_MODEL), BF16),
                   jax.ShapeDtypeStruct((s, A_WIDTH), F32),
                   jax.ShapeDtypeStruct((2 * A_HEADS, nchunk, A_DIM, A_DIM), F32),
                   jax.ShapeDtypeStruct((2 * A_HEADS, nchunk, CHUNK, CHUNK), F32)]
        + (list(ride.out_shapes) if ride else []),
        scratch_shapes=[pltpu.VMEM((s, wide), F32)] + (list(ride.scratch_shapes) if ride else []),
        compiler_params=_params(("arbitrary",)),
    )(cqkv, proj, proj, pa, *(ride.operands if ride else []))


def _delta_out_bwd(o_sum, proj, pa, d_mixed, tr=256):
    s = o_sum.shape[0]

    def body(o_ref, za_ref, pa_ref, dm_ref, do_ref, dza_ref, dpa_ref):
        @pl.when(pl.program_id(0) == 0)
        def _():
            dpa_ref[...] = jnp.zeros_like(dpa_ref)

        _, vjp = jax.vjp(_a_final, o_ref[...], za_ref[...], pa_ref[...])
        d_o, d_za, dpa = vjp(dm_ref[...].astype(F32))
        do_ref[...] = d_o
        dza_ref[...] = d_za.astype(BF16)
        dpa_ref[...] += dpa

    def rows(col):
        return pl.BlockSpec((tr, A_WIDTH), lambda i: (i, col))

    small = pl.BlockSpec((8, LANE), lambda i: (0, 0))
    return pl.pallas_call(
        body, name="delta_out_bwd", grid=(s // tr,), in_specs=[rows(0), rows(P_ZA // A_WIDTH), small, rows(0)],
        out_specs=[rows(0), rows(0), small],
        out_shape=[jax.ShapeDtypeStruct((s, A_WIDTH), F32), jax.ShapeDtypeStruct((s, A_WIDTH), BF16),
                   jax.ShapeDtypeStruct((8, LANE), F32)],
        compiler_params=_params(("arbitrary",)),
    )(o_sum, proj, pa, d_mixed)


def _delta_bwd(cqkv, proj, pa, d_o, states, inverses, ride=None):
    s = cqkv.shape[1]
    nchunk = s // CHUNK
    heads = A_BWD_HEADS
    steps = A_HEADS // heads
    wide, trio, gates, small, cols, _, _ = _a_specs(s, heads)
    n_in = len(ride.operands) if ride else 0
    n_out = len(ride.out_shapes) if ride else 0

    def body(*refs):
        c_ref, gt_ref, pa_ref, do_ref, s_hbm, t_hbm = refs[:6]
        dc_ref, dgt_ref, dpa_ref = refs[6 + n_in:9 + n_in]
        s_buf, t_buf, s_sems = refs[9 + n_in + n_out:12 + n_in + n_out]
        riders = (refs[6:6 + n_in], refs[9 + n_in:9 + n_in + n_out], refs[12 + n_in + n_out:])
        if ride:
            pl.when(pl.program_id(0) == 0)(lambda: ride.start(*riders))
        h0 = pl.program_id(0) * heads
        pa_v = pa_ref[...]

        @pl.when(h0 == 0)
        def _():
            dgt_ref[...] = jnp.zeros_like(dgt_ref)
            dpa_ref[...] = jnp.zeros_like(dpa_ref)

        dc_ref[...] = jnp.zeros_like(dc_ref)

        def state_copies(n, slot):
            tiles = _a_tiles(nchunk - 1 - n, nchunk, heads)
            return ([pltpu.make_async_copy(s_hbm.at[2 * h0 + b, i], s_buf.at[slot, b], s_sems.at[0, slot, b])
                     for b, (i, _, _) in enumerate(tiles)]
                    + [pltpu.make_async_copy(t_hbm.at[2 * h0 + b, i], t_buf.at[slot, b], s_sems.at[1, slot, b])
                       for b, (i, _, _) in enumerate(tiles)])

        for cp in state_copies(0, 0):
            cp.start()

        def step(n, carry, parity):
            d_sts, dpa = carry
            tiles = _a_tiles(nchunk - 1 - n, nchunk, heads)
            for cp in state_copies(n, parity):
                cp.wait()

            @pl.when(n + 1 < nchunk)
            def _():
                for cp in state_copies(n + 1, 1 - parity):
                    cp.start()

            sts, kept = s_buf[parity], t_buf[parity]
            d_o_t = jnp.stack([do_ref[sl, ln] for _, sl, ln in tiles], axis=0)
            _, vjp_c = jax.vjp(lambda *a: _a_step(*a, h0, kept)[:2], sts, *_a_load(tiles, c_ref, gt_ref), pa_v)
            d_prev, dcq, dck, dcv, dgts, dpa_i = vjp_c((d_sts, d_o_t))
            for b, (_, sl, ln) in enumerate(tiles):
                for r, dc in enumerate((dcq, dck, dcv)):
                    dc_ref[r, sl, ln] += dc[b]
                dgt_ref[sl, :] += dgts[b]
            return d_prev, dpa + dpa_i

        init = (jnp.zeros((2 * heads, A_DIM, A_DIM), F32), jnp.zeros((8, LANE), F32))
        _, dpa_out = lax.fori_loop(0, nchunk, lambda n, carry: step(n, carry, n % 2), init)
        dpa_ref[...] += dpa_out
        if ride:
            pl.when(pl.program_id(0) == steps - 1)(lambda: ride.finish(*riders))

    fixed = pl.BlockSpec((s, LANE), lambda g: (0, 0))
    return pl.pallas_call(
        body, name="delta_bwd", grid=(steps,),
        in_specs=[trio, gates, small, cols(0), pl.BlockSpec(memory_space=pl.ANY), pl.BlockSpec(memory_space=pl.ANY)]
        + [HBM] * n_in,
        out_specs=[trio, fixed, small] + [HBM] * n_out,
        out_shape=[jax.ShapeDtypeStruct((3, s, A_WIDTH), F32), jax.ShapeDtypeStruct((s, LANE), F32),
                   jax.ShapeDtypeStruct((8, LANE), F32)] + (list(ride.out_shapes) if ride else []),
        scratch_shapes=[pltpu.VMEM((2, 2 * heads, A_DIM, A_DIM), F32), pltpu.VMEM((2, 2 * heads, CHUNK, CHUNK), F32),
                        pltpu.SemaphoreType.DMA((2, 2, 2 * heads))]
        + (list(ride.scratch_shapes) if ride else []),
        compiler_params=_params(("arbitrary",)),
    )(cqkv, proj, pa, d_o, states, inverses, *(ride.operands if ride else []))


def _rope_tables(s):
    inv = ROPE_THETA ** (-jnp.arange(0, B_DIM, 2, dtype=F32) / B_DIM)
    ang = jnp.arange(s, dtype=F32)[:, None] * inv[None, :]
    cos, sin = jnp.cos(ang), jnp.sin(ang)
    return jnp.concatenate([cos, cos], axis=1), jnp.concatenate([-sin, sin], axis=1)


def _b_block(q_t, z_t, k3, v3, cos_q, sin_q, cos_k, sin_k, pb, n, nb):
    w = WINDOW
    def swap(t):
        return jnp.concatenate([t[:, B_DIM // 2:], t[:, :B_DIM // 2]], axis=1)

    grp = B_HEADS // B_KV
    qi = lax.broadcasted_iota(jnp.int32, (grp * w, 3 * w), 0) & (w - 1)
    kj = lax.broadcasted_iota(jnp.int32, (grp * w, 3 * w), 1)
    kpos = kj + (n - 1) * w
    mask = (jnp.abs(kj - w - qi) <= w) & (kpos >= 0) & (kpos < nb * w)
    lane = lax.broadcasted_iota(jnp.int32, (1, LANE), 1)
    qn, kn = pb[0:1, :B_DIM], pb[1:2, :B_DIM]
    cos_g = jnp.concatenate([cos_q] * grp, axis=0)
    sin_g = jnp.concatenate([sin_q] * grp, axis=0)
    def group(q, k, v, sink):
        k = _rms(k, kn)
        k = k * cos_k + swap(k) * sin_k
        q = _rms(q, qn)
        q = q * cos_g + swap(q) * sin_g
        s = _mm_nt(q, k) * (B_DIM ** -0.5)
        s = jnp.where(mask, s, -jnp.inf)
        m = jnp.maximum(jnp.max(s, axis=1, keepdims=True), sink)
        p = jnp.exp(s - m)
        p = p / (jnp.sum(p, axis=1, keepdims=True) + jnp.exp(sink - m))
        return _mm(p, v)

    stack = lambda ts: jnp.concatenate([t[None] for t in ts], axis=0)
    qs, ks, vs, sinks = [], [], [], []
    for hk in range(B_KV):
        heads = [hk * grp + g for g in range(grp)]
        ks.append(k3[:, hk * B_DIM:(hk + 1) * B_DIM])
        vs.append(v3[:, hk * B_DIM:(hk + 1) * B_DIM])
        qs.append(jnp.concatenate([q_t[:, hq * B_DIM:(hq + 1) * B_DIM] for hq in heads], axis=0))
        sinks.append(jnp.concatenate(
            [jnp.broadcast_to(jnp.sum(jnp.where(lane == hq, pb[2:3, :], 0.0), axis=1, keepdims=True), (w, 1))
             for hq in heads], axis=0))
    o = jax.vmap(group)(stack(qs), stack(ks), stack(vs), stack(sinks))
    outs = [o[hk, g * w:(g + 1) * w, :] for hk in range(B_KV) for g in range(grp)]
    return jnp.concatenate(outs, axis=1) * _silu(z_t)


def _b_specs(s):
    nb = s // WINDOW
    qsp = pl.BlockSpec((WINDOW, 512), lambda n: (n, P_QB // 512))
    zsp = pl.BlockSpec((WINDOW, 512), lambda n: (n, P_ZB // 512))

    def three(col, width):
        return [pl.BlockSpec((WINDOW, width), lambda n: (jnp.maximum(n - 1, 0), col)),
                pl.BlockSpec((WINDOW, width), lambda n: (n, col)),
                pl.BlockSpec((WINDOW, width), lambda n: (jnp.minimum(n + 1, nb - 1), col))]

    tab = pl.BlockSpec((WINDOW, B_DIM), lambda n: (n, 0))
    small = pl.BlockSpec((8, LANE), lambda n: (0, 0))
    specs = [qsp, zsp] + three(P_KB // LANE, LANE) + three(P_VB // LANE, LANE) + [tab, tab] + three(0, B_DIM) + three(0, B_DIM) + [small]
    return nb, specs


def _b_args(proj, cos2, sin2, pb):
    return (proj, proj, proj, proj, proj, proj, proj, proj, cos2, sin2, cos2, cos2, cos2, sin2, sin2, sin2, pb)


def _b_load(refs):
    (q_ref, z_ref, kp, kc, kx, vp, vc, vx, cq, sq, ckp, ckc, ckx, skp, skc, skx, pb_ref) = refs
    cat = lambda *r: jnp.concatenate([t[...] for t in r], axis=0)
    return (q_ref[...], z_ref[...], cat(kp, kc, kx), cat(vp, vc, vx), cq[...], sq[...], cat(ckp, ckc, ckx),
            cat(skp, skc, skx), pb_ref[...])


def _attn_b_fwd(proj, cos2, sin2, pb, mixed):
    s = proj.shape[0]
    nb, specs = _b_specs(s)

    def body(*refs):
        o_ref = refs[-1]
        args = _b_load(refs[:-2])
        o_ref[...] = _b_block(*args, pl.program_id(0), nb).astype(BF16)

    return pl.pallas_call(
        body, name="attn_b_fwd", grid=(nb,), in_specs=specs + [pl.BlockSpec(memory_space=pl.ANY)],
        out_specs=pl.BlockSpec((WINDOW, 512), lambda n: (n, A_WIDTH // 512)),
        out_shape=jax.ShapeDtypeStruct(mixed.shape, mixed.dtype), input_output_aliases={len(specs): 0},
        compiler_params=_params(("parallel",)),
    )(*_b_args(proj, cos2, sin2, pb), mixed)


def _attn_b_bwd(proj, cos2, sin2, pb, d_mixed):
    s = proj.shape[0]
    nb, specs = _b_specs(s)
    w = WINDOW

    def body(*refs):
        dm_ref, dq_ref, dz_ref, dk_ref, dv_ref, dpb_ref = refs[-6:]
        n = pl.program_id(0)
        q_t, z_t, k3, v3, cq, sq, ck, sk, pb_v = _b_load(refs[:-6])

        @pl.when(n == 0)
        def _():
            dk_ref[...] = jnp.zeros_like(dk_ref)
            dv_ref[...] = jnp.zeros_like(dv_ref)
            dpb_ref[...] = jnp.zeros_like(dpb_ref)

        def f(q_, z_, k_, v_, pb_):
            return _b_block(q_, z_, k_, v_, cq, sq, ck, sk, pb_, n, nb)

        _, vjp = jax.vjp(f, q_t, z_t, k3, v3, pb_v)
        dq, dz, dk3, dv3, dpb = vjp(dm_ref[...])
        dq_ref[...] = dq.astype(BF16)
        dz_ref[...] = dz.astype(BF16)
        dpb_ref[...] += dpb

        def add(j, cond):
            @pl.when(cond)
            def _():
                rows = pl.ds(pl.multiple_of((n - 1 + j) * w, w), w)
                dk_ref[rows, :] += dk3[j * w:(j + 1) * w, :]
                dv_ref[rows, :] += dv3[j * w:(j + 1) * w, :]

        add(0, n > 0)
        add(1, n >= 0)
        add(2, n < nb - 1)

    blk = pl.BlockSpec((w, 512), lambda n: (n, 0))
    whole = pl.BlockSpec((s, LANE), lambda n: (0, 0))
    small = pl.BlockSpec((8, LANE), lambda n: (0, 0))
    return pl.pallas_call(
        body, name="attn_b_bwd", grid=(nb,),
        in_specs=specs + [pl.BlockSpec((w, 512), lambda n: (n, 2))],
        out_specs=[blk, blk, whole, whole, small],
        out_shape=[jax.ShapeDtypeStruct((s, 512), BF16), jax.ShapeDtypeStruct((s, 512), BF16),
                   jax.ShapeDtypeStruct((s, LANE), F32), jax.ShapeDtypeStruct((s, LANE), F32),
                   jax.ShapeDtypeStruct((8, LANE), F32)],
        compiler_params=_params(("arbitrary",)),
    )(*_b_args(proj, cos2, sin2, pb), d_mixed)


def _mem_kv_fwd(mem, mem_norm_w, w_kv):
    def body(mem_ref, nw_ref, w_ref, kv_ref):
        mn = _rms(mem_ref[...], nw_ref[...]).astype(BF16)
        kv_ref[...] = jnp.dot(mn, w_ref[...], preferred_element_type=F32)

    return pl.pallas_call(
        body, name="mem_kv_fwd", out_shape=jax.ShapeDtypeStruct((MEM_LEN, 2 * C_HEADS * C_DIM), F32),
        compiler_params=_params(),
    )(mem, mem_norm_w, w_kv)


def _mem_kv_bwd(mem, mem_norm_w, w_kv, d_kv):
    def body(mem_ref, nw_ref, w_ref, g_ref, gw_ref, gn_ref):
        mn, vjp = jax.vjp(_rms, mem_ref[...], nw_ref[...])
        g = g_ref[...].astype(BF16)
        gw_ref[...] = lax.dot_general(mn.astype(BF16), g, (((0,), (0,)), ((), ())), preferred_element_type=F32)
        d_mn = lax.dot_general(g, w_ref[...], (((1,), (1,)), ((), ())), preferred_element_type=F32)
        gn_ref[...] = vjp(d_mn)[1]

    return pl.pallas_call(
        body, name="mem_kv_bwd",
        out_shape=[jax.ShapeDtypeStruct((D_MODEL, 2 * C_HEADS * C_DIM), F32), jax.ShapeDtypeStruct((1, D_MODEL), F32)],
        compiler_params=_params(),
    )(mem, mem_norm_w, w_kv, d_kv)


def _c_tile(q_t, z_t, kvm, pc):
    width = C_HEADS * C_DIM
    outs = []
    for h in range(C_HEADS):
        q = _rms(q_t[:, h * C_DIM:(h + 1) * C_DIM], pc[0:1, :])
        k = _rms(kvm[:, h * C_DIM:(h + 1) * C_DIM], pc[1:2, :])
        v = kvm[:, width + h * C_DIM:width + (h + 1) * C_DIM]
        s = _mm_nt(q, k) * (C_DIM ** -0.5)
        p = jnp.exp(s - jnp.max(s, axis=1, keepdims=True))
        p = p / jnp.sum(p, axis=1, keepdims=True)
        outs.append(_mm(p, v))
    return jnp.concatenate(outs, axis=1) * _silu(z_t)


def _attn_c_fwd(proj, kvm, pc, mixed, tq=256):
    s = proj.shape[0]

    def body(q_ref, z_ref, kv_ref, pc_ref, mixed_ref, o_ref):
        o_ref[...] = _c_tile(q_ref[...], z_ref[...], kv_ref[...], pc_ref[...]).astype(BF16)

    return pl.pallas_call(
        body, name="attn_c_fwd", grid=(s // tq,),
        in_specs=[pl.BlockSpec((tq, 512), lambda i: (i, P_QC // 512)), pl.BlockSpec((tq, 512), lambda i: (i, P_ZC // 512)),
                  pl.BlockSpec(kvm.shape, lambda i: (0, 0)), pl.BlockSpec((8, LANE), lambda i: (0, 0)),
                  pl.BlockSpec(memory_space=pl.ANY)],
        out_specs=pl.BlockSpec((tq, 512), lambda i: (i, (A_WIDTH + 512) // 512)),
        out_shape=jax.ShapeDtypeStruct(mixed.shape, mixed.dtype), input_output_aliases={4: 0},
        compiler_params=_params(("parallel",)),
    )(proj, proj, kvm, pc, mixed)


def _attn_c_bwd(proj, kvm, pc, d_mixed, tq=256):
    s = proj.shape[0]

    def body(q_ref, z_ref, kv_ref, pc_ref, dm_ref, dq_ref, dz_ref, dkv_ref, dpc_ref):
        @pl.when(pl.program_id(0) == 0)
        def _():
            dkv_ref[...] = jnp.zeros_like(dkv_ref)
            dpc_ref[...] = jnp.zeros_like(dpc_ref)

        _, vjp = jax.vjp(_c_tile, q_ref[...], z_ref[...], kv_ref[...], pc_ref[...])
        dq, dz, dkv, dpc = vjp(dm_ref[...])
        dq_ref[...] = dq.astype(BF16)
        dz_ref[...] = dz.astype(BF16)
        dkv_ref[...] += dkv
        dpc_ref[...] += dpc

    blk = pl.BlockSpec((tq, 512), lambda i: (i, 0))
    kvs = pl.BlockSpec(kvm.shape, lambda i: (0, 0))
    small = pl.BlockSpec((8, LANE), lambda i: (0, 0))
    return pl.pallas_call(
        body, name="attn_c_bwd", grid=(s // tq,),
        in_specs=[pl.BlockSpec((tq, 512), lambda i: (i, P_QC // 512)), pl.BlockSpec((tq, 512), lambda i: (i, P_ZC // 512)),
                  kvs, small, pl.BlockSpec((tq, 512), lambda i: (i, 3))],
        out_specs=[blk, blk, kvs, small],
        out_shape=[jax.ShapeDtypeStruct((s, 512), BF16), jax.ShapeDtypeStruct((s, 512), BF16),
                   jax.ShapeDtypeStruct(kvm.shape, F32), jax.ShapeDtypeStruct((8, LANE), F32)],
        compiler_params=_params(("arbitrary",)),
    )(proj, proj, kvm, pc, d_mixed)


def _pad_row(v, width=LANE):
    v = v.reshape(1, -1)
    return jnp.pad(v, ((0, 0), (0, width - v.shape[1])))


def _local_step(x, mem, target, norm_w, w_perm_t, w_blocks_t, conv_w, pa, pb, pc, mem_norm_w, w_kv, w_out, gather=None,
                exchange=None):
    s = x.shape[0]
    cos2, sin2 = _rope_tables(s)
    hn = _rms_fwd(x, norm_w)
    wide = dict(tm=1024, tn=512, tk=2048)
    proj = _matmul(hn, w_perm_t, "nt", F32, "mm_proj", **wide)
    cqkv = _conv_fwd(proj, conv_w)
    if gather is None:
        mixed, o_sum, states, inverses = _delta_fwd(cqkv, proj, pa)
    else:
        mixed, o_sum, states, inverses, *arrived = _delta_fwd(cqkv, proj, pa, gather[0])
        w_out, w_kv = gather[1](*arrived)
    mixed = _attn_b_fwd(proj, cos2, sin2, pb, mixed)
    kvm = _mem_kv_fwd(mem, mem_norm_w, w_kv)
    mixed = _attn_c_fwd(proj, kvm, pc, mixed)
    mo = _matmul(mixed, w_out, "nn", F32, "mm_out", **wide)
    dy, dyb, loss_parts = _loss_dy(x, mo, target)

    d_mixed = _matmul(dyb, w_out, "nt", F32, "mm_dmixed", **wide)
    g_w_out = _matmul(mixed, dyb, "tn", F32, "mm_gwout", **wide)
    d_qc, d_zc, d_kvm, d_pc = _attn_c_bwd(proj, kvm, pc, d_mixed)
    g_w_kv, g_mem_norm = _mem_kv_bwd(mem, mem_norm_w, w_kv, d_kvm)
    d_qb, d_zb, d_kb, d_vb, d_pb = _attn_b_bwd(proj, cos2, sin2, pb, d_mixed)
    d_o, d_za, d_pa_out = _delta_out_bwd(o_sum, proj, pa, d_mixed)
    early = exchange[0](g_w_out, g_w_kv) if exchange else None
    d_c, d_gt, d_pa_scan, *landed_early = _delta_bwd(cqkv, proj, pa, d_o, states, inverses, early)
    d_pa = d_pa_out + d_pa_scan
    d_qkv, g_conv = _conv_bwd(proj, conv_w, d_c)
    d_proj = _cotangent_blocks(d_qkv, d_za, d_gt, d_qb, d_kb, d_vb, d_zb, d_qc, d_zc)
    g_w_blocks_t = _matmul(d_proj, hn, "tn", F32, "mm_gwin", tm=512, tn=2048, tk=2048)
    late = exchange[1](g_w_blocks_t) if exchange else None
    d_hn = _matmul(d_proj, w_blocks_t, "nn", F32, "mm_dhn", tm=1024, tn=2048, tk=512, ride=late)
    d_hn, landed_late = (d_hn[0], list(d_hn[1:])) if late else (d_hn, [])
    g_x, g_norm = _rms_bwd(x, norm_w, d_hn, dy)
    return dict(loss_parts=loss_parts, g_x=g_x, g_norm=g_norm, g_w_blocks_t=g_w_blocks_t, g_conv=g_conv, d_pa=d_pa,
                d_pb=d_pb, d_pc=d_pc, g_mem_norm=g_mem_norm, g_w_kv=g_w_kv, g_w_out=g_w_out,
                landed=landed_late + landed_early)


_SEGMENTS = ((0, O_GT, 0), (O_GT, O_QB, P_GT), (O_QB, O_KB, P_QB), (O_KB, O_VB, P_KB), (O_VB, O_ZB, P_VB),
             (O_ZB, O_QC, P_ZB), (O_QC, O_ZC, P_QC), (O_ZC, IN_WIDTH, P_ZC))


def _permute_blocks(w4):
    parts = []
    for first, end, _ in sorted(_SEGMENTS, key=lambda seg: seg[2]):
        row = first
        while row < end:
            k = row // W_IN_BLOCK
            stop = min(end, (k + 1) * W_IN_BLOCK)
            parts.append(w4[k][row - k * W_IN_BLOCK:stop - k * W_IN_BLOCK, :])
            row = stop
    parts.append(jnp.zeros((P_WIDTH - IN_WIDTH, w4.shape[2]), w4.dtype))
    return jnp.concatenate(parts, axis=0)


def _cotangent_blocks(d_qkv, d_za, d_gt, d_qb, d_kb, d_vb, d_zb, d_qc, d_zc):
    s = d_qkv.shape[0]
    tr = min(256, s)
    pieces = (d_qkv, d_za, d_gt, d_qb, d_kb, d_vb, d_zb, d_qc, d_zc)

    def body(*refs):
        o_ref = refs[-1]
        tiles = [r[...].astype(BF16) for r in refs[:-1]]
        tiles[2] = tiles[2][:, :O_QB - O_GT]
        orig = jnp.concatenate(tiles, axis=1)
        pad = jnp.zeros((tr, W_IN_PAD - W_IN_BLOCK), BF16)
        parts = []
        for k in range(N_CHIPS):
            parts += [orig[:, k * W_IN_BLOCK:(k + 1) * W_IN_BLOCK], pad]
        o_ref[...] = jnp.concatenate(parts, axis=1)

    return pl.pallas_call(
        body, name="cotangent_blocks", grid=(s // tr,),
        in_specs=[pl.BlockSpec((tr, p.shape[1]), lambda i: (i, 0)) for p in pieces],
        out_specs=pl.BlockSpec((tr, N_CHIPS * W_IN_PAD), lambda i: (i, 0)),
        out_shape=jax.ShapeDtypeStruct((s, N_CHIPS * W_IN_PAD), BF16), compiler_params=_params(("parallel",)),
    )(*pieces)


HBM = pl.BlockSpec(memory_space=pltpu.HBM)


def _place():
    x, y, c = lax.axis_index("x"), lax.axis_index("y"), lax.axis_index("c")
    chips = [(1 - x, y), (x, 1 - y), (1 - x, 1 - y)]
    return x, y, c, 2 * x + y, chips, [2 * cx + cy for cx, cy in chips]


PIECE_ROWS_CAP = 600


def _remote(src, dst, send_sems, recv_sems, k, to):
    return pltpu.make_async_remote_copy(src_ref=src, dst_ref=dst, send_sem=send_sems.at[k], recv_sem=recv_sems.at[k],
                                        device_id=to, device_id_type=MESH)


def _half_cols(ref, c):
    half = ref.shape[-1] // 2
    return pl.ds(pl.multiple_of(c * half, LANE), half)


class _PairedGather:
    def __init__(self, blocks):
        n = len(blocks)
        self.operands = list(blocks)
        self.out_shapes = [jax.ShapeDtypeStruct((N_CHIPS,) + b.shape, b.dtype) for b in blocks]
        self.scratch_shapes = [pltpu.SemaphoreType.DMA((6 * n,)), pltpu.SemaphoreType.DMA((6 * n,))]

    @staticmethod
    def _copies(srcs, dsts, sems):
        x, y, c, me, chips, chip_ids = _place()
        sends, landed, passes, passed = [], [], [], []
        for a, (src, dst) in enumerate(zip(srcs, dsts)):
            mine, other = _half_cols(src, c), _half_cols(src, 1 - c)
            for j, (chip, cid) in enumerate(zip(chips, chip_ids)):
                sends.append(_remote(src.at[:, mine], dst.at[me, :, mine], sems[0], sems[1], 6 * a + j, (*chip, c)))
                here = dst.at[cid, :, mine]
                landed.append(_remote(here, here, sems[0], sems[1], 6 * a + j, (x, y, 1 - c)))
                passes.append(_remote(here, here, sems[0], sems[1], 6 * a + 3 + j, (x, y, 1 - c)))
                there = dst.at[cid, :, other]
                passed.append(_remote(there, there, sems[0], sems[1], 6 * a + 3 + j, (x, y, 1 - c)))
        return sends, landed, passes, passed

    def start(self, srcs, dsts, sems):
        for cp in self._copies(srcs, dsts, sems)[0]:
            cp.start()

    def middle(self, srcs, dsts, sems):
        _, landed, passes, _ = self._copies(srcs, dsts, sems)
        for arrived, onward in zip(landed, passes):
            arrived.wait_recv()
            onward.start()

    def finish(self, srcs, dsts, sems):
        sends, _, passes, passed = self._copies(srcs, dsts, sems)
        for cp in passed:
            cp.wait_recv()
        for cp in sends + passes:
            cp.wait_send()


def _all_gather_weights(bigs, conv_b):
    bigs = tuple(bigs)
    n_big = len(bigs)

    def body(*refs):
        srcs, conv_src = refs[:n_big], refs[n_big]
        dsts, conv_dst = refs[n_big + 1:2 * n_big + 1], refs[2 * n_big + 1]
        send_sems, recv_sems, local_sems = refs[2 * n_big + 2:]
        x, y, c, me, chips, chip_ids = _place()
        sibling = (x, y, 1 - c)
        local = [pltpu.make_async_copy(src, dst.at[me], local_sems.at[a]) for a, (src, dst) in enumerate(zip(srcs, dsts))]
        local.append(pltpu.make_async_copy(conv_src, conv_dst.at[me], local_sems.at[n_big]))
        for cp in local:
            cp.start()
        sends = []
        for a, (src, dst) in enumerate(zip(srcs, dsts)):
            mine = _half_cols(src, c)
            for j, chip in enumerate(chips):
                sends.append(_remote(src.at[:, mine], dst.at[me, :, mine], send_sems, recv_sems, 6 * a + j, (*chip, c)))
        for j, chip in enumerate(chips):
            sends.append(_remote(conv_src, conv_dst.at[me], send_sems, recv_sems, 6 * n_big + j, (*chip, c)))
        for cp in sends:
            cp.start()
        passed = []
        for a, (src, dst) in enumerate(zip(srcs, dsts)):
            mine = _half_cols(src, c)
            for j, cid in enumerate(chip_ids):
                landed = dst.at[cid, :, mine]
                _remote(landed, landed, send_sems, recv_sems, 6 * a + j, sibling).wait_recv()
                cp = _remote(landed, landed, send_sems, recv_sems, 6 * a + 3 + j, sibling)
                cp.start()
                passed.append(cp)
        for a, (src, dst) in enumerate(zip(srcs, dsts)):
            other = _half_cols(src, 1 - c)
            for j, cid in enumerate(chip_ids):
                landed = dst.at[cid, :, other]
                _remote(landed, landed, send_sems, recv_sems, 6 * a + 3 + j, sibling).wait_recv()
        for j, cid in enumerate(chip_ids):
            _remote(conv_src, conv_dst.at[cid], send_sems, recv_sems, 6 * n_big + j, sibling).wait_recv()
        for cp in sends + passed:
            cp.wait_send()
        for cp in local:
            cp.wait()

    n_sem = 6 * n_big + 3
    return pl.pallas_call(
        body, name="all_gather_weights",
        out_shape=[jax.ShapeDtypeStruct((N_CHIPS,) + w.shape, w.dtype) for w in bigs + (conv_b,)],
        in_specs=[pl.BlockSpec(memory_space=pltpu.VMEM)] * (n_big + 1), out_specs=[HBM] * (n_big + 1),
        scratch_shapes=[pltpu.SemaphoreType.DMA((n_sem,)), pltpu.SemaphoreType.DMA((n_sem,)),
                        pltpu.SemaphoreType.DMA((n_big + 1,))],
        compiler_params=_params(),
    )(*bigs, conv_b)


def _pair_exchange(grads, name):
    n = len(grads)
    pieces = [_row_tile(g.shape[1]) for g in grads]

    def body(*refs):
        srcs, gots = refs[:n], refs[n:2 * n]
        stages = refs[2 * n:3 * n]
        send_sems, recv_sems, load_sems = refs[3 * n:]
        x, y, c, _, _, _ = _place()
        sibling = (x, y, 1 - c)
        for a in range(n):
            slabs, rows, _ = gots[a].shape
            piece = pieces[a]
            per_slab = rows // piece
            theirs = _half_cols(srcs[a], 1 - c)
            loads, sends = [], []
            for i in range(slabs * per_slab):
                k, r, slot = i // per_slab, i % per_slab, i % 2
                part = pl.ds(r * piece, piece)
                loads.append(pltpu.make_async_copy(srcs[a].at[k, part, theirs], stages[a].at[slot], load_sems.at[2 * a + slot]))
                sends.append(pltpu.make_async_remote_copy(
                    src_ref=stages[a].at[slot], dst_ref=gots[a].at[k, part, :],
                    send_sem=send_sems.at[2 * a + slot], recv_sem=recv_sems.at[a], device_id=sibling, device_id_type=MESH))
            loads[0].start()
            for i in range(len(loads)):
                loads[i].wait()
                sends[i].start()
                if i + 1 < len(loads):
                    if i >= 1:
                        sends[i - 1].wait_send()
                    loads[i + 1].start()
            for cp in sends[-2:]:
                cp.wait_send()
        for a in range(n):
            whole = srcs[a].at[:, :, _half_cols(srcs[a], c)]
            pltpu.make_async_remote_copy(src_ref=whole, dst_ref=gots[a], send_sem=send_sems.at[2 * a],
                                         recv_sem=recv_sems.at[a], device_id=sibling, device_id_type=MESH).wait_recv()

    halves = [jax.ShapeDtypeStruct((g.shape[0], g.shape[1], g.shape[2] // 2), g.dtype) for g in grads]
    return pl.pallas_call(
        body, name=name, out_shape=halves, in_specs=[HBM] * n, out_specs=[HBM] * n,
        scratch_shapes=[pltpu.VMEM((2, piece, g.shape[2] // 2), g.dtype) for piece, g in zip(pieces, grads)]
        + [pltpu.SemaphoreType.DMA((2 * n,)), pltpu.SemaphoreType.DMA((n,)), pltpu.SemaphoreType.DMA((2 * n,))],
        compiler_params=_params(),
    )(*grads)


class _ChipExchange:
    def __init__(self, halves):
        n = len(halves)
        self.operands = list(halves)
        self.out_shapes = [jax.ShapeDtypeStruct((N_CHIPS - 1,) + h.shape[1:], h.dtype) for h in halves]
        self.scratch_shapes = [pltpu.SemaphoreType.DMA((3 * n,)), pltpu.SemaphoreType.DMA((3 * n,))]

    @staticmethod
    def _copies(srcs, lands, sems):
        _, _, c, _, chips, chip_ids = _place()
        return [_remote(src.at[cid], land.at[j], sems[0], sems[1], 3 * a + j, (*chip, c))
                for a, (src, land) in enumerate(zip(srcs, lands)) for j, (chip, cid) in enumerate(zip(chips, chip_ids))]

    def start(self, srcs, lands, sems):
        for cp in self._copies(srcs, lands, sems):
            cp.start()

    def finish(self, srcs, lands, sems):
        copies = self._copies(srcs, lands, sems)
        for cp in copies:
            cp.wait_recv()
        for cp in copies:
            cp.wait_send()


def _pair_gather(halves, rows):
    n = len(halves)

    def body(*refs):
        srcs, fulls = refs[:n], refs[n:2 * n]
        send_sems, recv_sems, local_sems = refs[2 * n:]
        x, y, c, _, _, _ = _place()
        copies = []
        for a in range(n):
            mine, src = _half_cols(fulls[a], c), srcs[a].at[pl.ds(0, rows[a]), :]
            keep = pltpu.make_async_copy(src, fulls[a].at[:, mine], local_sems.at[a])
            keep.start()
            give = _remote(src, fulls[a].at[:, mine], send_sems, recv_sems, a, (x, y, 1 - c))
            give.start()
            copies += [keep, give]
        for a in range(n):
            other, src = _half_cols(fulls[a], 1 - c), srcs[a].at[pl.ds(0, rows[a]), :]
            copies[2 * a].wait()
            copies[2 * a + 1].wait_send()
            _remote(src, fulls[a].at[:, other], send_sems, recv_sems, a, (x, y, 1 - c)).wait_recv()

    return pl.pallas_call(
        body, name="grad_pair_gather",
        out_shape=[jax.ShapeDtypeStruct((r, 2 * h.shape[1]), h.dtype) for r, h in zip(rows, halves)],
        in_specs=[pl.BlockSpec(memory_space=pltpu.VMEM)] * n, out_specs=[HBM] * n,
        scratch_shapes=[pltpu.SemaphoreType.DMA((n,)), pltpu.SemaphoreType.DMA((n,)), pltpu.SemaphoreType.DMA((n,))],
    )(*halves)


def _all_reduce_small(p):
    n_dev = 8

    def body(p_ref, o_ref, land, send_sems, recv_sems):
        x, y, c = lax.axis_index("x"), lax.axis_index("y"), lax.axis_index("c")
        me = 4 * x + 2 * y + c
        land[me] = p_ref[...]
        sends = []
        for k in range(1, n_dev):
            fx, fy, fc = (k >> 2) & 1, (k >> 1) & 1, k & 1
            to = (x ^ fx, y ^ fy, c ^ fc)
            cp = _remote(p_ref, land.at[me], send_sems, recv_sems, k - 1, to)
            cp.start()
            sends.append(cp)
        for k in range(1, n_dev):
            _remote(p_ref, land.at[me ^ k], send_sems, recv_sems, k - 1, (x, y, c)).wait_recv()
        total = land[0]
        for d in range(1, n_dev):
            total = total + land[d]
        o_ref[...] = total
        for cp in sends:
            cp.wait_send()

    vm = pl.BlockSpec(memory_space=pltpu.VMEM)
    return pl.pallas_call(
        body, name="all_reduce_small", out_shape=jax.ShapeDtypeStruct(p.shape, p.dtype), in_specs=[vm], out_specs=vm,
        scratch_shapes=[pltpu.VMEM((n_dev,) + p.shape, p.dtype), pltpu.SemaphoreType.DMA((n_dev - 1,)),
                        pltpu.SemaphoreType.DMA((n_dev - 1,))],
    )(p)


def _row_tile(rows):
    fits = [t for t in range(8, min(rows, PIECE_ROWS_CAP) + 1, 8) if rows % t == 0]
    return max(fits) if fits else rows


def _pair_sum(full, got, core, name):
    n, r, c = got.shape
    tr = _row_tile(r)

    def body(core_ref, a_ref, b_ref, o_ref):
        o_ref[...] = (a_ref[...] + b_ref[...]).astype(BF16)

    blk = pl.BlockSpec((None, tr, c), lambda i, j, core_ref: (i, j, 0))
    grid_spec = pltpu.PrefetchScalarGridSpec(
        num_scalar_prefetch=1, grid=(n, r // tr),
        in_specs=[pl.BlockSpec((None, tr, c), lambda i, j, core_ref: (i, j, core_ref[0])), blk], out_specs=blk)
    return pl.pallas_call(body, name=name, grid_spec=grid_spec, out_shape=jax.ShapeDtypeStruct(got.shape, BF16),
                          compiler_params=_params(("parallel", "parallel")))(core, full, got)


def _chip_sum(full, got, land, place, name):
    n, r, c = land.shape
    tr = _row_tile(r)

    def body(place_ref, a_ref, b_ref, l_ref, o_ref):
        total = a_ref[...] + b_ref[...]
        for j in range(n):
            total = total + l_ref[j].astype(F32)
        o_ref[...] = total

    grid_spec = pltpu.PrefetchScalarGridSpec(
        num_scalar_prefetch=1, grid=(r // tr,),
        in_specs=[pl.BlockSpec((None, tr, c), lambda i, p: (p[0], i, p[1])),
                  pl.BlockSpec((None, tr, c), lambda i, p: (p[0], i, 0)),
                  pl.BlockSpec((n, tr, c), lambda i, p: (0, i, 0))],
        out_specs=pl.BlockSpec((tr, c), lambda i, p: (i, 0)))
    return pl.pallas_call(body, name=name, grid_spec=grid_spec, out_shape=jax.ShapeDtypeStruct((r, c), F32),
                          compiler_params=_params(("parallel",)))(place, full, got, land)


def _adamw(w, g, m, v, name):
    r, c = w.shape
    tr = _row_tile(r)
    tc = 1024 if c % 1024 == 0 else c

    def body(w_ref, g_ref, m_ref, v_ref, d_ref, mo_ref, vo_ref):
        g_ = g_ref[...]
        m2 = ADAM_B1 * m_ref[...] + (1.0 - ADAM_B1) * g_
        v2 = ADAM_B2 * v_ref[...] + (1.0 - ADAM_B2) * jnp.square(g_)
        m_hat = m2 / (1.0 - ADAM_B1 ** ADAM_STEP)
        v_hat = v2 / (1.0 - ADAM_B2 ** ADAM_STEP)
        d_ref[...] = -ADAM_LR * (m_hat / (jnp.sqrt(v_hat) + ADAM_EPS) + ADAM_WD * w_ref[...])
        mo_ref[...] = m2
        vo_ref[...] = v2

    blk = pl.BlockSpec((tr, tc), lambda i, j: (i, j))
    return pl.pallas_call(body, name=name, grid=(r // tr, c // tc), in_specs=[blk] * 4, out_specs=[blk] * 3,
                          out_shape=[jax.ShapeDtypeStruct(w.shape, F32)] * 3,
                          compiler_params=_params(("parallel", "parallel")))(w, g, m, v)


SMALL_NAMES = ("norm_w", "mem_norm_w", "o_norm_a", "q_norm_c", "k_norm_c", "q_norm_b", "k_norm_b",
               "a_log_fwd", "a_log_bwd", "dt_bias_fwd", "dt_bias_bwd", "sink_b")
SMALL_SIZES = (2048, 2048, 128, 128, 128, 64, 64, 8, 8, 8, 8, 8)
SMALL_LOSS = sum(SMALL_SIZES)
SMALL_CONV = 5120
SMALL_TOTAL = SMALL_CONV + CONV_K * 3 * A_WIDTH
SMALL_ROWS = SMALL_TOTAL // LANE


def _pack_small(parts, extra=None, conv=None):
    vec = [parts[n].reshape(-1) for n in SMALL_NAMES]
    vec.append(jnp.zeros((1,), F32) if extra is None else extra.reshape(1))
    vec.append(jnp.zeros((SMALL_CONV - SMALL_LOSS - 1,), F32))
    vec.append(jnp.zeros((SMALL_TOTAL - SMALL_CONV,), F32) if conv is None else conv.reshape(-1))
    return jnp.concatenate(vec).reshape(SMALL_ROWS, LANE)


def _unpack_small(packed):
    flat = packed.reshape(-1)
    out, off = {}, 0
    for n, size in zip(SMALL_NAMES, SMALL_SIZES):
        out[n] = flat[off:off + size].reshape(1, size)
        off += size
    return out


WEIGHT_ORDER = ("norm_w", "w_in", "conv_w_a", "a_log_fwd", "a_log_bwd", "dt_bias_fwd", "dt_bias_bwd", "o_norm_a",
                "q_norm_b", "k_norm_b", "sink_b", "mem_norm_w", "w_mem_kv", "q_norm_c", "k_norm_c", "w_out")


def kernel(x, mem, norm_w, w_in, conv_w_a, a_log_fwd, a_log_bwd, dt_bias_fwd, dt_bias_bwd, o_norm_a, q_norm_b, k_norm_b, sink_b, mem_norm_w, w_mem_kv, q_norm_c, k_norm_c, w_out, loss_target, m_norm_w, m_w_in, m_conv_w_a, m_a_log_fwd, m_a_log_bwd, m_dt_bias_fwd, m_dt_bias_bwd, m_o_norm_a, m_q_norm_b, m_k_norm_b, m_sink_b, m_mem_norm_w, m_w_mem_kv, m_q_norm_c, m_k_norm_c, m_w_out, v_norm_w, v_w_in, v_conv_w_a, v_a_log_fwd, v_a_log_bwd, v_dt_bias_fwd, v_dt_bias_bwd, v_o_norm_a, v_q_norm_b, v_k_norm_b, v_sink_b, v_mem_norm_w, v_w_mem_kv, v_q_norm_c, v_k_norm_c, v_w_out):
    weights = dict(norm_w=norm_w, w_in=w_in, conv_w_a=conv_w_a, a_log_fwd=a_log_fwd, a_log_bwd=a_log_bwd,
                   dt_bias_fwd=dt_bias_fwd, dt_bias_bwd=dt_bias_bwd, o_norm_a=o_norm_a, q_norm_b=q_norm_b,
                   k_norm_b=k_norm_b, sink_b=sink_b, mem_norm_w=mem_norm_w, w_mem_kv=w_mem_kv, q_norm_c=q_norm_c,
                   k_norm_c=k_norm_c, w_out=w_out)
    mom1 = dict(norm_w=m_norm_w, w_in=m_w_in, conv_w_a=m_conv_w_a, a_log_fwd=m_a_log_fwd, a_log_bwd=m_a_log_bwd,
                dt_bias_fwd=m_dt_bias_fwd, dt_bias_bwd=m_dt_bias_bwd, o_norm_a=m_o_norm_a, q_norm_b=m_q_norm_b,
                k_norm_b=m_k_norm_b, sink_b=m_sink_b, mem_norm_w=m_mem_norm_w, w_mem_kv=m_w_mem_kv,
                q_norm_c=m_q_norm_c, k_norm_c=m_k_norm_c, w_out=m_w_out)
    mom2 = dict(norm_w=v_norm_w, w_in=v_w_in, conv_w_a=v_conv_w_a, a_log_fwd=v_a_log_fwd, a_log_bwd=v_a_log_bwd,
                dt_bias_fwd=v_dt_bias_fwd, dt_bias_bwd=v_dt_bias_bwd, o_norm_a=v_o_norm_a, q_norm_b=v_q_norm_b,
                k_norm_b=v_k_norm_b, sink_b=v_sink_b, mem_norm_w=v_mem_norm_w, w_mem_kv=v_w_mem_kv,
                q_norm_c=v_q_norm_c, k_norm_c=v_k_norm_c, w_out=v_w_out)
    chip = 2 * lax.axis_index("x") + lax.axis_index("y")

    own_in = jnp.pad(jnp.transpose(w_in[0]).astype(BF16), ((0, W_IN_PAD - W_IN_BLOCK), (0, 0)))
    w_in4, conv4 = _all_gather_weights([own_in], conv_w_a[0])
    w_perm_t = _permute_blocks(w_in4)
    w_blocks_t = w_in4.reshape(N_CHIPS * W_IN_PAD, D_MODEL)
    conv_full = jnp.transpose(conv4, (1, 0, 2)).reshape(CONV_K, 3 * A_WIDTH)
    own_out, own_kv = w_out[0].astype(BF16), w_mem_kv[0].astype(BF16)

    def assemble(w_out4, w_kv4):
        w_out4 = lax.dynamic_update_index_in_dim(w_out4, own_out, chip, 0)
        w_kv4 = lax.dynamic_update_index_in_dim(w_kv4, own_kv, chip, 0)
        return w_out4.reshape(D_MODEL, D_MODEL), w_kv4.reshape(D_MODEL, 2 * C_HEADS * C_DIM)

    gather = (_PairedGather([own_out, own_kv]), assemble)
    pa = jnp.concatenate([_pad_row(a_log_fwd), _pad_row(a_log_bwd), _pad_row(dt_bias_fwd), _pad_row(dt_bias_bwd),
                          _pad_row(o_norm_a), jnp.zeros((3, LANE), F32)], axis=0)
    pb = jnp.concatenate([_pad_row(q_norm_b), _pad_row(k_norm_b), _pad_row(sink_b), jnp.zeros((5, LANE), F32)], axis=0)
    pc = jnp.concatenate([_pad_row(q_norm_c), _pad_row(k_norm_c), jnp.zeros((6, LANE), F32)], axis=0)

    full, got = {}, {}
    core = lax.axis_index("c").astype(jnp.int32).reshape(1)

    def pair_round(tag, blocks):
        names = [tag + "_%d" % i for i in range(len(blocks))]
        full.update(zip(names, blocks))
        got.update(zip(names, _pair_exchange(blocks, "grad_pair_exchange_" + tag)))
        return _ChipExchange([_pair_sum(full[n], got[n], core, "grad_pair_sum_" + n) for n in names])

    def early(g_w_out, g_w_kv):
        return pair_round("early", [g_w_out.reshape(N_CHIPS, D_MODEL // N_CHIPS, D_MODEL),
                                    g_w_kv.reshape(N_CHIPS, D_MODEL // N_CHIPS, 2 * C_HEADS * C_DIM)])

    def late(g_w_blocks_t):
        return pair_round("late", [g_w_blocks_t.reshape(N_CHIPS, W_IN_PAD, D_MODEL)])

    r = _local_step(x[0], mem[0], loss_target[0], norm_w, w_perm_t, w_blocks_t, conv_full, pa, pb, pc, mem_norm_w, None, None,
                    gather, (early, late))
    place = jnp.stack([chip, lax.axis_index("c")]).astype(jnp.int32)
    reduced = [_chip_sum(full[n], got[n], l, place, "grad_chip_sum_" + n)
               for n, l in zip(("late_0", "early_0", "early_1"), r["landed"])]
    g_w_in_t, g_w_out, g_w_kv = _pair_gather(reduced, [W_IN_BLOCK, D_MODEL // N_CHIPS, D_MODEL // N_CHIPS])

    d_pa, d_pb, d_pc = r["d_pa"], r["d_pb"], r["d_pc"]
    small_g = dict(norm_w=r["g_norm"], mem_norm_w=r["g_mem_norm"], o_norm_a=d_pa[4], q_norm_c=d_pc[0], k_norm_c=d_pc[1],
                   q_norm_b=d_pb[0, :B_DIM], k_norm_b=d_pb[1, :B_DIM], a_log_fwd=d_pa[0, :A_HEADS],
                   a_log_bwd=d_pa[1, :A_HEADS], dt_bias_fwd=d_pa[2, :A_HEADS], dt_bias_bwd=d_pa[3, :A_HEADS],
                   sink_b=d_pb[2, :B_HEADS])
    packed = _all_reduce_small(_pack_small(small_g, jnp.sum(r["loss_parts"][:, 0, 0]), r["g_conv"]))
    flat = packed.reshape(-1)
    loss = flat[SMALL_LOSS]
    conv_sum = flat[SMALL_CONV:].reshape(CONV_K, 3 * A_WIDTH)
    conv_cols = 3 * A_WIDTH // N_CHIPS
    g_conv = lax.dynamic_slice(conv_sum, (0, chip * conv_cols), (CONV_K, conv_cols))

    grads = _unpack_small(packed)
    grads.update(w_in=jnp.transpose(g_w_in_t), w_mem_kv=g_w_kv, w_out=g_w_out, conv_w_a=g_conv)
    delta, new_m, new_v = {}, {}, {}
    for n in ("w_mem_kv", "w_out", "conv_w_a"):
        delta[n], new_m[n], new_v[n] = _adamw(weights[n][0], grads[n], mom1[n][0], mom2[n][0], "adamw_" + n)
    stepped = _adamw(jnp.transpose(w_in[0]), g_w_in_t, jnp.transpose(m_w_in[0]), jnp.transpose(v_w_in[0]), "adamw_w_in")
    delta["w_in"], new_m["w_in"], new_v["w_in"] = (jnp.transpose(t) for t in stepped)
    d_s, m_s, v_s = _adamw(_pack_small(weights), packed, _pack_small(mom1), _pack_small(mom2), "adamw_small")
    d_s, m_s, v_s = _unpack_small(d_s), _unpack_small(m_s), _unpack_small(v_s)
    for n in SMALL_NAMES:
        delta[n], new_m[n], new_v[n] = d_s[n], m_s[n], v_s[n]

    def shaped(tree):
        return [tree[n].reshape(weights[n].shape) for n in WEIGHT_ORDER]

    return (loss, r["g_x"].reshape(x.shape), *shaped(grads), *shaped(delta), *shaped(new_m), *shaped(new_v))
```

```python
import functools

import jax
import jax.numpy as jnp
from jax import lax
from jax.experimental import pallas as pl
from jax.experimental.pallas import tpu as pltpu

F32 = jnp.float32
BF16 = jnp.bfloat16
HI = lax.Precision.HIGHEST
MESH = pl.DeviceIdType.MESH

D_MODEL = 2048
A_WIDTH = 1024
A_HEADS = 8
A_DIM = 128
CONV_K = 5
CHUNK = 64
B_HEADS = 8
B_KV = 2
B_DIM = 64
WINDOW = 128
C_HEADS = 4
C_DIM = 128
MEM_LEN = 256
ROPE_THETA = 10000.0
EPS = 1e-6
IN_WIDTH = 6432
N_CHIPS = 4
W_IN_BLOCK = IN_WIDTH // N_CHIPS
W_IN_PAD = 1664

LANE = 128
P_QA, P_KA, P_VA, P_ZA = 0, 1024, 2048, 3072
P_QB, P_ZB, P_QC, P_ZC = 4096, 4608, 5120, 5632
P_KB, P_VB, P_GT = 6144, 6272, 6400
P_WIDTH = 6656
O_GT, O_QB, O_KB, O_VB, O_ZB, O_QC, O_ZC = 4096, 4128, 4640, 4768, 4896, 5408, 5920

ADAM_LR, ADAM_B1, ADAM_B2, ADAM_EPS, ADAM_WD, ADAM_STEP = 0.001, 0.9, 0.999, 1e-08, 0.01, 10

VMEM_LIMIT = 56 * 1024 * 1024


def _params(sem=None):
    return pltpu.CompilerParams(dimension_semantics=sem, vmem_limit_bytes=VMEM_LIMIT)


def _dot(a, b, dims=(((1,), (0,)), ((), ())), precision=HI):
    return lax.dot_general(a, b, dims, precision=precision, preferred_element_type=F32)


def _dot_nt(a, b, precision=HI):
    return _dot(a, b, (((1,), (1,)), ((), ())), precision)


def _dot_tn(a, b, precision=HI):
    return _dot(a, b, (((0,), (0,)), ((), ())), precision)


_NN = (((1,), (0,)), ((), ()))
_NT = (((1,), (1,)), ((), ()))
_TN = (((0,), (0,)), ((), ()))


def _bdot(a, b, dims):
    return lax.dot_general(a.astype(BF16), b.astype(BF16), dims, preferred_element_type=F32)


@jax.custom_vjp
def _mm(a, b):
    return _bdot(a, b, _NN)


_mm.defvjp(lambda a, b: (_bdot(a, b, _NN), (a, b)),
           lambda res, ct: (_bdot(ct, res[1], _NT), _bdot(res[0], ct, _TN)))


@jax.custom_vjp
def _mm_nt(a, b):
    return _bdot(a, b, _NT)


_mm_nt.defvjp(lambda a, b: (_bdot(a, b, _NT), (a, b)),
              lambda res, ct: (_bdot(ct, res[1], _NN), _bdot(ct, res[0], _TN)))


@jax.custom_vjp
def _mm_tn(a, b):
    return _bdot(a, b, _TN)


_mm_tn.defvjp(lambda a, b: (_bdot(a, b, _TN), (a, b)),
              lambda res, ct: (_bdot(res[1], ct, _NT), _bdot(res[0], ct, _NN)))


def _rms(t, w):
    return t * lax.rsqrt(jnp.mean(t * t, axis=-1, keepdims=True) + EPS) * w


def _l2(t):
    return t * lax.rsqrt(jnp.sum(t * t, axis=-1, keepdims=True) + EPS)


def _silu(t):
    return t * jax.nn.sigmoid(t)


def _softplus(t):
    return jnp.maximum(t, 0.0) + jnp.log1p(jnp.exp(-jnp.abs(t)))


def _matmul(a, b, mode, out_dtype, name, tm=512, tn=512, tk=512, ride=None):
    (m, k) = a.shape[::-1] if mode == "tn" else a.shape
    n = b.shape[0] if mode == "nt" else b.shape[1]
    tm, tn, tk = min(tm, m), min(tn, n), min(tk, k)
    assert m % tm == 0 and n % tn == 0 and k % tk == 0, (m, n, k, tm, tn, tk)
    if mode == "nn":
        a_spec = pl.BlockSpec((tm, tk), lambda i, j, kk: (i, kk))
        b_spec = pl.BlockSpec((tk, tn), lambda i, j, kk: (kk, j))
        dims = (((1,), (0,)), ((), ()))
    elif mode == "nt":
        a_spec = pl.BlockSpec((tm, tk), lambda i, j, kk: (i, kk))
        b_spec = pl.BlockSpec((tn, tk), lambda i, j, kk: (j, kk))
        dims = (((1,), (1,)), ((), ()))
    else:
        a_spec = pl.BlockSpec((tk, tm), lambda i, j, kk: (kk, i))
        b_spec = pl.BlockSpec((tk, tn), lambda i, j, kk: (kk, j))
        dims = (((0,), (0,)), ((), ()))
    nk = k // tk
    grid = (m // tm, n // tn, nk)
    n_in = len(ride.operands) if ride else 0
    n_out = len(ride.out_shapes) if ride else 0

    def body(*refs):
        a_ref, b_ref, o_ref = refs[0], refs[1], refs[2 + n_in]
        scratch = refs[3 + n_in + n_out:]
        step = (pl.program_id(0) * grid[1] + pl.program_id(1)) * nk + pl.program_id(2)
        riders = (refs[2:2 + n_in], refs[3 + n_in:3 + n_in + n_out], scratch[(0 if nk == 1 else 1):])
        if ride:
            pl.when(step == 0)(lambda: ride.start(*riders))
        if nk == 1:
            o_ref[...] = _bdot(a_ref[...], b_ref[...], dims).astype(out_dtype)
        else:
            acc_ref, kk = scratch[0], pl.program_id(2)

            @pl.when(kk == 0)
            def _():
                acc_ref[...] = jnp.zeros_like(acc_ref)

            acc_ref[...] += _bdot(a_ref[...], b_ref[...], dims)

            @pl.when(kk == nk - 1)
            def _():
                o_ref[...] = acc_ref[...].astype(out_dtype)
        if ride:
            pl.when(step == grid[0] * grid[1] * nk - 1)(lambda: ride.finish(*riders))

    out = pl.pallas_call(
        body, name=name, grid=grid,
        in_specs=[a_spec, b_spec] + [HBM] * n_in,
        out_specs=[pl.BlockSpec((tm, tn), lambda i, j, kk: (i, j))] + [HBM] * n_out,
        out_shape=[jax.ShapeDtypeStruct((m, n), out_dtype)] + (list(ride.out_shapes) if ride else []),
        scratch_shapes=([] if nk == 1 else [pltpu.VMEM((tm, tn), F32)]) + (list(ride.scratch_shapes) if ride else []),
        compiler_params=_params(("arbitrary",) * 3 if ride else ("parallel", "parallel", "arbitrary")),
    )(a, b, *(ride.operands if ride else []))
    return out if ride else out[0]


def _rms_fwd(x, w, tr=256):
    s, d = x.shape

    def body(x_ref, w_ref, o_ref):
        o_ref[...] = _rms(x_ref[...], w_ref[...]).astype(BF16)

    return pl.pallas_call(
        body, name="rms_fwd", grid=(s // tr,),
        in_specs=[pl.BlockSpec((tr, d), lambda i: (i, 0)), pl.BlockSpec((1, d), lambda i: (0, 0))],
        out_specs=pl.BlockSpec((tr, d), lambda i: (i, 0)),
        out_shape=jax.ShapeDtypeStruct((s, d), BF16), compiler_params=_params(("parallel",)),
    )(x, w)


def _input_grad(d_proj, w_t, x, w, dy, ride=None, tm=512, tk=512):
    s, k = d_proj.shape
    d = w_t.shape[1]
    tm = min(tm, s)
    nk = k // tk
    grid = (s // tm, nk)
    n_in = len(ride.operands) if ride else 0
    n_out = len(ride.out_shapes) if ride else 0

    def body(*refs):
        a_ref, b_ref, x_ref, w_ref, dy_ref = refs[:5]
        gx_ref, gw_ref = refs[5 + n_in:7 + n_in]
        acc_ref = refs[7 + n_in + n_out]
        riders = (refs[5:5 + n_in], refs[7 + n_in:7 + n_in + n_out], refs[8 + n_in + n_out:])
        kk = pl.program_id(1)
        step = pl.program_id(0) * nk + kk
        if ride:
            pl.when(step == 0)(lambda: ride.start(*riders))

        @pl.when(step == 0)
        def _():
            gw_ref[...] = jnp.zeros_like(gw_ref)

        @pl.when(kk == 0)
        def _():
            acc_ref[...] = jnp.zeros_like(acc_ref)

        acc_ref[...] += _bdot(a_ref[...], b_ref[...], _NN)

        @pl.when(kk == nk - 1)
        def _():
            _, vjp = jax.vjp(_rms, x_ref[...], w_ref[...])
            dx, dw = vjp(acc_ref[...])
            gx_ref[...] = dy_ref[...] + dx
            gw_ref[...] += dw

        if ride:
            pl.when(step == grid[0] * nk - 1)(lambda: ride.finish(*riders))

    row = pl.BlockSpec((tm, d), lambda i, kk: (i, 0))
    vec = pl.BlockSpec((1, d), lambda i, kk: (0, 0))
    return pl.pallas_call(
        body, name="input_grad", grid=grid,
        in_specs=[pl.BlockSpec((tm, tk), lambda i, kk: (i, kk)), pl.BlockSpec((tk, d), lambda i, kk: (kk, 0)), row, vec, row]
        + [HBM] * n_in,
        out_specs=[row, vec] + [HBM] * n_out,
        out_shape=[jax.ShapeDtypeStruct((s, d), F32), jax.ShapeDtypeStruct((1, d), F32)]
        + (list(ride.out_shapes) if ride else []),
        scratch_shapes=[pltpu.VMEM((tm, d), F32)] + (list(ride.scratch_shapes) if ride else []),
        compiler_params=_params(("arbitrary", "arbitrary")),
    )(d_proj, w_t, x, w, dy, *(ride.operands if ride else []))


def _loss_dy(x, mo, target, tr=256):
    s, d = x.shape
    nt = s // tr

    def body(x_ref, mo_ref, t_ref, dy_ref, dyb_ref, l_ref):
        err = x_ref[...] + mo_ref[...] - t_ref[...]
        dy = err * (1.0 / d)
        dy_ref[...] = dy
        dyb_ref[...] = dy.astype(BF16)
        l_ref[...] = jnp.full(l_ref.shape, 0.5 * jnp.sum(jnp.sum(err * err, axis=1, keepdims=True) * (1.0 / d)), F32)

    row = pl.BlockSpec((tr, d), lambda i: (i, 0))
    return pl.pallas_call(
        body, name="loss_dy", grid=(nt,), in_specs=[row, row, row],
        out_specs=[row, row, pl.BlockSpec((1, 8, LANE), lambda i: (i, 0, 0))],
        out_shape=[jax.ShapeDtypeStruct((s, d), F32), jax.ShapeDtypeStruct((s, d), BF16),
                   jax.ShapeDtypeStruct((nt, 8, LANE), F32)],
        compiler_params=_params(("parallel",)),
    )(x, mo, target)


def _shift_rows(t, s):
    if s == 0:
        return t
    n = t.shape[0]
    rolled = pltpu.roll(t, (-s) % n, axis=0)
    idx = lax.broadcasted_iota(jnp.int32, t.shape, 0) + s
    return jnp.where((idx >= 0) & (idx < n), rolled, 0.0)


def _conv_fwd(proj, conv_w):
    s = proj.shape[0]
    nblk = 3 * A_WIDTH // LANE

    def body(x_ref, w_ref, o_ref):
        x = x_ref[...]
        acc = jnp.zeros_like(x)
        for j in range(CONV_K):
            acc = acc + w_ref[j:j + 1, :] * _shift_rows(x, j - CONV_K // 2)
        o_ref[...] = acc

    return pl.pallas_call(
        body, name="conv_fwd", grid=(nblk,),
        in_specs=[pl.BlockSpec((s, LANE), lambda i: (0, i)), pl.BlockSpec((CONV_K, LANE), lambda i: (0, i))],
        out_specs=pl.BlockSpec((None, s, LANE), lambda i: (i // A_HEADS, 0, i % A_HEADS)),
        out_shape=jax.ShapeDtypeStruct((3, s, A_WIDTH), F32), compiler_params=_params(("parallel",)),
    )(proj, conv_w)


def _conv_bwd(proj, conv_w, d_c):
    s = proj.shape[0]
    nblk = 3 * A_WIDTH // LANE

    def body(x_ref, w_ref, g_ref, dx_ref, dw_ref):
        x, g = x_ref[...], g_ref[...]
        acc = jnp.zeros_like(x)
        for j in range(CONV_K):
            off = j - CONV_K // 2
            acc = acc + w_ref[j:j + 1, :] * _shift_rows(g, -off)
            dw_ref[j:j + 1, :] = jnp.sum(_shift_rows(x, off) * g, axis=0, keepdims=True)
        dx_ref[...] = acc.astype(BF16)

    col = pl.BlockSpec((s, LANE), lambda i: (0, i))
    wsp = pl.BlockSpec((CONV_K, LANE), lambda i: (0, i))
    dsp = pl.BlockSpec((None, s, LANE), lambda i: (i // A_HEADS, 0, i % A_HEADS))
    return pl.pallas_call(
        body, name="conv_bwd", grid=(nblk,), in_specs=[col, wsp, dsp], out_specs=[col, wsp],
        out_shape=[jax.ShapeDtypeStruct((s, 3 * A_WIDTH), BF16), jax.ShapeDtypeStruct((CONV_K, 3 * A_WIDTH), F32)],
        compiler_params=_params(("parallel",)),
    )(proj, conv_w, d_c)


A_FWD_HEADS = 4
A_BWD_HEADS = 4


def _neumann_inverse(a):
    c = a.shape[-1]
    eye = (lax.broadcasted_iota(jnp.int32, (c, c), 0) == lax.broadcasted_iota(jnp.int32, (c, c), 1)).astype(F32)
    tinv = eye + a
    p = a
    for _ in range(5):
        p = _mm(p, p)
        tinv = tinv + _mm(tinv, p)
    return tinv


@jax.custom_vjp
def _unit_inverse(a):
    return _neumann_inverse(a)


def _unit_inverse_fwd(a):
    tinv = _neumann_inverse(a)
    return tinv, tinv


def _unit_inverse_bwd(tinv, ct):
    return (_bdot(_bdot(tinv, ct, _TN), tinv, _NT),)


_unit_inverse.defvjp(_unit_inverse_fwd, _unit_inverse_bwd)


@jax.custom_vjp
def _known_inverse(a, tinv):
    return tinv


_known_inverse.defvjp(lambda a, tinv: (tinv, tinv),
                      lambda tinv, ct: (_unit_inverse_bwd(tinv, ct)[0], jnp.zeros_like(tinv)))


def _a_chain(st, cq, ck, cv, alpha, beta_raw, a_log, dt_b, incl, strict, last, kept=None):
    c = CHUNK
    gb = -jnp.exp(a_log) * _softplus(alpha + dt_b)
    bb = jax.nn.sigmoid(beta_raw)
    q = _l2(_silu(cq)) * (A_DIM ** -0.5)
    k = _l2(_silu(ck))
    v = _silu(cv)

    gc = _dot(incl, jnp.broadcast_to(gb, (c, LANE)))
    tot = jnp.sum(gc * last, axis=0, keepdims=True)
    m1 = gc[:, :c]
    decay = incl * jnp.exp(incl * (m1 - m1.T))
    kb = k * bb
    vb = v * bb
    a = -(strict * decay * _mm_nt(kb, k))
    tinv = _unit_inverse(a) if kept is None else _known_inverse(a, kept)
    eg = jnp.exp(gc)
    u = _mm(tinv, vb)
    w = _mm(tinv, kb * eg)
    qk = _mm_nt(q, k) * decay
    v_new = u - _mm(w, st)
    o = _mm(q * eg, st) + _mm(qk, v_new)
    st_new = st * jnp.exp(tot) + _mm_tn(k * jnp.exp(tot - gc), v_new)
    return st_new, o, tinv


def _a_step(sts, cq, ck, cv, gts, pa, h0, kept=None):
    c = CHUNK
    lane = lax.broadcasted_iota(jnp.int32, (1, LANE), 1)
    ii = lax.broadcasted_iota(jnp.int32, (c, c), 0)
    jj = lax.broadcasted_iota(jnp.int32, (c, c), 1)
    row = lax.broadcasted_iota(jnp.int32, (c, 1), 0)

    def pick(t, col):
        return jnp.sum(jnp.where(lane == col, t, 0.0), axis=1, keepdims=True)

    alpha, beta_raw, a_log, dt_b, incl, strict, last = [], [], [], [], [], [], []
    for b in range(sts.shape[0]):
        h, rev = h0 + b // 2, b % 2
        alpha.append(pick(gts[b], h + 8 * rev))
        beta_raw.append(pick(gts[b], h + 16 + 8 * rev))
        a_log.append(pick(pa[rev:rev + 1, :], h))
        dt_b.append(pick(pa[2 + rev:3 + rev, :], h))
        incl.append(((ii <= jj) if rev else (ii >= jj)).astype(F32))
        strict.append(((ii < jj) if rev else (ii > jj)).astype(F32))
        last.append((row == (0 if rev else c - 1)).astype(F32))
    stack = lambda ts: jnp.concatenate([t[None] for t in ts], axis=0)
    return jax.vmap(_a_chain)(sts, cq, ck, cv, stack(alpha), stack(beta_raw), stack(a_log), stack(dt_b),
                              stack(incl), stack(strict), stack(last), kept)


def _a_final(o, za, pa):
    outs = []
    for j in range(o.shape[1] // A_DIM):
        ln = slice(j * A_DIM, (j + 1) * A_DIM)
        outs.append(_rms(o[:, ln], pa[4:5, :]) * _silu(za[:, ln]))
    return jnp.concatenate(outs, axis=1)


def _a_tiles(n, nchunk, heads):
    tiles = []
    for b in range(2 * heads):
        i = (nchunk - 1 - n) if b % 2 else n
        tiles.append((i, pl.ds(pl.multiple_of(i * CHUNK, CHUNK), CHUNK), slice((b // 2) * A_DIM, (b // 2 + 1) * A_DIM)))
    return tiles


def _a_load(tiles, c_ref, gt_ref):
    cq, ck, cv = (jnp.stack([c_ref[r, sl, ln] for _, sl, ln in tiles], axis=0) for r in range(3))
    return cq, ck, cv, jnp.stack([gt_ref[sl, :] for _, sl, _ in tiles], axis=0)


def _loop_by_two(n, step, init):
    assert n % 2 == 0
    return lax.fori_loop(0, n // 2, lambda m, carry: step(2 * m + 1, step(2 * m, carry, 0), 1), init)


def _a_scan(h0, heads, nchunk, c_ref, gt_ref, pa, of_ref, ob_ref, s_ref, t_ref):
    def step(n, sts, parity):
        tiles = _a_tiles(n, nchunk, heads)
        sts_new, o, tinv = _a_step(sts, *_a_load(tiles, c_ref, gt_ref), pa, h0)
        for b, (i, sl, ln) in enumerate(tiles):
            s_ref[b, i] = sts[b]
            t_ref[b, i] = tinv[b]
            (ob_ref if b % 2 else of_ref)[sl, ln] = o[b]
        return sts_new

    _loop_by_two(nchunk, step, jnp.zeros((2 * heads, A_DIM, A_DIM), F32))


def _a_specs(s, heads):
    wide = heads * A_DIM
    once = pl.Buffered(1)
    trio = pl.BlockSpec((3, s, wide), lambda g: (0, 0, g), pipeline_mode=once)
    gates = pl.BlockSpec((s, LANE), lambda g: (0, P_GT // LANE))
    small = pl.BlockSpec((8, LANE), lambda g: (0, 0))

    def cols(base):
        return pl.BlockSpec((s, wide), lambda g: (0, base // wide + g), pipeline_mode=once)

    state = pl.BlockSpec((2 * heads, s // CHUNK, A_DIM, A_DIM), lambda g: (g, 0, 0, 0), pipeline_mode=once)
    kept = pl.BlockSpec((2 * heads, s // CHUNK, CHUNK, CHUNK), lambda g: (g, 0, 0, 0), pipeline_mode=once)
    return wide, trio, gates, small, cols, state, kept


def _delta_fwd(cqkv, proj, pa, ride=None):
    s = cqkv.shape[1]
    nchunk = s // CHUNK
    heads = A_FWD_HEADS
    steps = A_HEADS // heads
    wide, trio, gates, small, cols, state, kept = _a_specs(s, heads)
    n_in = len(ride.operands) if ride else 0
    n_out = len(ride.out_shapes) if ride else 0

    def body(*refs):
        c_ref, gt_ref, za_ref, pa_ref = refs[:4]
        out_ref, o_ref, s_ref, t_ref = refs[4 + n_in:8 + n_in]
        ob_ref = refs[8 + n_in + n_out]
        riders = (refs[4:4 + n_in], refs[8 + n_in:8 + n_in + n_out], refs[9 + n_in + n_out:])
        g = pl.program_id(0)
        if ride:
            pl.when(g == 0)(lambda: ride.start(*riders))
            pl.when(g == steps - 1)(lambda: ride.middle(*riders))
        h0 = g * heads
        pa_v = pa_ref[...]
        _a_scan(h0, heads, nchunk, c_ref, gt_ref, pa_v, o_ref, ob_ref, s_ref, t_ref)
        o_ref[...] += ob_ref[...]
        out_ref[...] = _a_final(o_ref[...], za_ref[...], pa_v).astype(BF16)
        if ride:
            pl.when(g == steps - 1)(lambda: ride.finish(*riders))

    assert steps > 1
    return pl.pallas_call(
        body, name="delta_fwd", grid=(steps,),
        in_specs=[trio, gates, cols(P_ZA), small] + [HBM] * n_in,
        out_specs=[cols(0), cols(0), state, kept] + [HBM] * n_out,
        out_shape=[jax.ShapeDtypeStruct((s, D_MODEL), BF16),
                   jax.ShapeDtypeStruct((s, A_WIDTH), F32),
                   jax.ShapeDtypeStruct((2 * A_HEADS, nchunk, A_DIM, A_DIM), F32),
                   jax.ShapeDtypeStruct((2 * A_HEADS, nchunk, CHUNK, CHUNK), F32)]
        + (list(ride.out_shapes) if ride else []),
        scratch_shapes=[pltpu.VMEM((s, wide), F32)] + (list(ride.scratch_shapes) if ride else []),
        compiler_params=_params(("arbitrary",)),
    )(cqkv, proj, proj, pa, *(ride.operands if ride else []))


def _delta_out_bwd(o_sum, proj, pa, d_mixed, tr=256):
    s = o_sum.shape[0]

    def body(o_ref, za_ref, pa_ref, dm_ref, do_ref, dza_ref, dpa_ref):
        @pl.when(pl.program_id(0) == 0)
        def _():
            dpa_ref[...] = jnp.zeros_like(dpa_ref)

        _, vjp = jax.vjp(_a_final, o_ref[...], za_ref[...], pa_ref[...])
        d_o, d_za, dpa = vjp(dm_ref[...].astype(F32))
        do_ref[...] = d_o
        dza_ref[...] = d_za.astype(BF16)
        dpa_ref[...] += dpa

    def rows(col):
        return pl.BlockSpec((tr, A_WIDTH), lambda i: (i, col))

    small = pl.BlockSpec((8, LANE), lambda i: (0, 0))
    return pl.pallas_call(
        body, name="delta_out_bwd", grid=(s // tr,), in_specs=[rows(0), rows(P_ZA // A_WIDTH), small, rows(0)],
        out_specs=[rows(0), rows(0), small],
        out_shape=[jax.ShapeDtypeStruct((s, A_WIDTH), F32), jax.ShapeDtypeStruct((s, A_WIDTH), BF16),
                   jax.ShapeDtypeStruct((8, LANE), F32)],
        compiler_params=_params(("arbitrary",)),
    )(o_sum, proj, pa, d_mixed)


def _delta_bwd(cqkv, proj, pa, d_o, states, inverses, ride=None):
    s = cqkv.shape[1]
    nchunk = s // CHUNK
    heads = A_BWD_HEADS
    steps = A_HEADS // heads
    wide, trio, gates, small, cols, _, _ = _a_specs(s, heads)
    n_in = len(ride.operands) if ride else 0
    n_out = len(ride.out_shapes) if ride else 0

    def body(*refs):
        c_ref, gt_ref, pa_ref, do_ref, s_hbm, t_hbm = refs[:6]
        dc_ref, dgt_ref, dpa_ref = refs[6 + n_in:9 + n_in]
        s_buf, t_buf, s_sems = refs[9 + n_in + n_out:12 + n_in + n_out]
        riders = (refs[6:6 + n_in], refs[9 + n_in:9 + n_in + n_out], refs[12 + n_in + n_out:])
        if ride:
            pl.when(pl.program_id(0) == 0)(lambda: ride.start(*riders))
        h0 = pl.program_id(0) * heads
        pa_v = pa_ref[...]

        @pl.when(h0 == 0)
        def _():
            dgt_ref[...] = jnp.zeros_like(dgt_ref)
            dpa_ref[...] = jnp.zeros_like(dpa_ref)

        dc_ref[...] = jnp.zeros_like(dc_ref)

        def state_copies(n, slot):
            tiles = _a_tiles(nchunk - 1 - n, nchunk, heads)
            return ([pltpu.make_async_copy(s_hbm.at[2 * h0 + b, i], s_buf.at[slot, b], s_sems.at[0, slot, b])
                     for b, (i, _, _) in enumerate(tiles)]
                    + [pltpu.make_async_copy(t_hbm.at[2 * h0 + b, i], t_buf.at[slot, b], s_sems.at[1, slot, b])
                       for b, (i, _, _) in enumerate(tiles)])

        for cp in state_copies(0, 0):
            cp.start()

        def step(n, carry, parity):
            d_sts, dpa = carry
            tiles = _a_tiles(nchunk - 1 - n, nchunk, heads)
            for cp in state_copies(n, parity):
                cp.wait()

            @pl.when(n + 1 < nchunk)
            def _():
                for cp in state_copies(n + 1, 1 - parity):
                    cp.start()

            sts, kept = s_buf[parity], t_buf[parity]
            d_o_t = jnp.stack([do_ref[sl, ln] for _, sl, ln in tiles], axis=0)
            _, vjp_c = jax.vjp(lambda *a: _a_step(*a, h0, kept)[:2], sts, *_a_load(tiles, c_ref, gt_ref), pa_v)
            d_prev, dcq, dck, dcv, dgts, dpa_i = vjp_c((d_sts, d_o_t))
            for b, (_, sl, ln) in enumerate(tiles):
                for r, dc in enumerate((dcq, dck, dcv)):
                    dc_ref[r, sl, ln] += dc[b]
                dgt_ref[sl, :] += dgts[b]
            return d_prev, dpa + dpa_i

        init = (jnp.zeros((2 * heads, A_DIM, A_DIM), F32), jnp.zeros((8, LANE), F32))
        _, dpa_out = lax.fori_loop(0, nchunk, lambda n, carry: step(n, carry, n % 2), init)
        dpa_ref[...] += dpa_out
        if ride:
            pl.when(pl.program_id(0) == steps - 1)(lambda: ride.finish(*riders))

    fixed = pl.BlockSpec((s, LANE), lambda g: (0, 0))
    return pl.pallas_call(
        body, name="delta_bwd", grid=(steps,),
        in_specs=[trio, gates, small, cols(0), pl.BlockSpec(memory_space=pl.ANY), pl.BlockSpec(memory_space=pl.ANY)]
        + [HBM] * n_in,
        out_specs=[trio, fixed, small] + [HBM] * n_out,
        out_shape=[jax.ShapeDtypeStruct((3, s, A_WIDTH), F32), jax.ShapeDtypeStruct((s, LANE), F32),
                   jax.ShapeDtypeStruct((8, LANE), F32)] + (list(ride.out_shapes) if ride else []),
        scratch_shapes=[pltpu.VMEM((2, 2 * heads, A_DIM, A_DIM), F32), pltpu.VMEM((2, 2 * heads, CHUNK, CHUNK), F32),
                        pltpu.SemaphoreType.DMA((2, 2, 2 * heads))]
        + (list(ride.scratch_shapes) if ride else []),
        compiler_params=_params(("arbitrary",)),
    )(cqkv, proj, pa, d_o, states, inverses, *(ride.operands if ride else []))


def _rope_tables(s):
    inv = ROPE_THETA ** (-jnp.arange(0, B_DIM, 2, dtype=F32) / B_DIM)
    ang = jnp.arange(s, dtype=F32)[:, None] * inv[None, :]
    cos, sin = jnp.cos(ang), jnp.sin(ang)
    return jnp.concatenate([cos, cos], axis=1), jnp.concatenate([-sin, sin], axis=1)


def _b_block(q_t, z_t, k3, v3, cos_q, sin_q, cos_k, sin_k, pb, n, nb):
    w = WINDOW
    def swap(t):
        return jnp.concatenate([t[:, B_DIM // 2:], t[:, :B_DIM // 2]], axis=1)

    grp = B_HEADS // B_KV
    qi = lax.broadcasted_iota(jnp.int32, (grp * w, 3 * w), 0) & (w - 1)
    kj = lax.broadcasted_iota(jnp.int32, (grp * w, 3 * w), 1)
    kpos = kj + (n - 1) * w
    mask = (jnp.abs(kj - w - qi) <= w) & (kpos >= 0) & (kpos < nb * w)
    lane = lax.broadcasted_iota(jnp.int32, (1, LANE), 1)
    qn, kn = pb[0:1, :B_DIM], pb[1:2, :B_DIM]
    cos_g = jnp.concatenate([cos_q] * grp, axis=0)
    sin_g = jnp.concatenate([sin_q] * grp, axis=0)
    def group(q, k, v, sink):
        k = _rms(k, kn)
        k = k * cos_k + swap(k) * sin_k
        q = _rms(q, qn)
        q = q * cos_g + swap(q) * sin_g
        s = _mm_nt(q, k) * (B_DIM ** -0.5)
        s = jnp.where(mask, s, -jnp.inf)
        m = jnp.maximum(jnp.max(s, axis=1, keepdims=True), sink)
        p = jnp.exp(s - m)
        p = p / (jnp.sum(p, axis=1, keepdims=True) + jnp.exp(sink - m))
        return _mm(p, v)

    stack = lambda ts: jnp.concatenate([t[None] for t in ts], axis=0)
    qs, ks, vs, sinks = [], [], [], []
    for hk in range(B_KV):
        heads = [hk * grp + g for g in range(grp)]
        ks.append(k3[:, hk * B_DIM:(hk + 1) * B_DIM])
        vs.append(v3[:, hk * B_DIM:(hk + 1) * B_DIM])
        qs.append(jnp.concatenate([q_t[:, hq * B_DIM:(hq + 1) * B_DIM] for hq in heads], axis=0))
        sinks.append(jnp.concatenate(
            [jnp.broadcast_to(jnp.sum(jnp.where(lane == hq, pb[2:3, :], 0.0), axis=1, keepdims=True), (w, 1))
             for hq in heads], axis=0))
    o = jax.vmap(group)(stack(qs), stack(ks), stack(vs), stack(sinks))
    outs = [o[hk, g * w:(g + 1) * w, :] for hk in range(B_KV) for g in range(grp)]
    return jnp.concatenate(outs, axis=1) * _silu(z_t)


def _b_specs(s):
    nb = s // WINDOW
    qsp = pl.BlockSpec((WINDOW, 512), lambda n: (n, P_QB // 512))
    zsp = pl.BlockSpec((WINDOW, 512), lambda n: (n, P_ZB // 512))

    def three(col, width):
        return [pl.BlockSpec((WINDOW, width), lambda n: (jnp.maximum(n - 1, 0), col)),
                pl.BlockSpec((WINDOW, width), lambda n: (n, col)),
                pl.BlockSpec((WINDOW, width), lambda n: (jnp.minimum(n + 1, nb - 1), col))]

    tab = pl.BlockSpec((WINDOW, B_DIM), lambda n: (n, 0))
    small = pl.BlockSpec((8, LANE), lambda n: (0, 0))
    specs = [qsp, zsp] + three(P_KB // LANE, LANE) + three(P_VB // LANE, LANE) + [tab, tab] + three(0, B_DIM) + three(0, B_DIM) + [small]
    return nb, specs


def _b_args(proj, cos2, sin2, pb):
    return (proj, proj, proj, proj, proj, proj, proj, proj, cos2, sin2, cos2, cos2, cos2, sin2, sin2, sin2, pb)


def _b_load(refs):
    (q_ref, z_ref, kp, kc, kx, vp, vc, vx, cq, sq, ckp, ckc, ckx, skp, skc, skx, pb_ref) = refs
    cat = lambda *r: jnp.concatenate([t[...] for t in r], axis=0)
    return (q_ref[...], z_ref[...], cat(kp, kc, kx), cat(vp, vc, vx), cq[...], sq[...], cat(ckp, ckc, ckx),
            cat(skp, skc, skx), pb_ref[...])


def _attn_b_fwd(proj, cos2, sin2, pb, mixed):
    s = proj.shape[0]
    nb, specs = _b_specs(s)

    def body(*refs):
        o_ref = refs[-1]
        args = _b_load(refs[:-2])
        o_ref[...] = _b_block(*args, pl.program_id(0), nb).astype(BF16)

    return pl.pallas_call(
        body, name="attn_b_fwd", grid=(nb,), in_specs=specs + [pl.BlockSpec(memory_space=pl.ANY)],
        out_specs=pl.BlockSpec((WINDOW, 512), lambda n: (n, A_WIDTH // 512)),
        out_shape=jax.ShapeDtypeStruct(mixed.shape, mixed.dtype), input_output_aliases={len(specs): 0},
        compiler_params=_params(("parallel",)),
    )(*_b_args(proj, cos2, sin2, pb), mixed)


def _attn_b_bwd(proj, cos2, sin2, pb, d_mixed):
    s = proj.shape[0]
    nb, specs = _b_specs(s)
    w = WINDOW

    def body(*refs):
        dm_ref, dq_ref, dz_ref, dk_ref, dv_ref, dpb_ref = refs[-6:]
        n = pl.program_id(0)
        q_t, z_t, k3, v3, cq, sq, ck, sk, pb_v = _b_load(refs[:-6])

        @pl.when(n == 0)
        def _():
            dk_ref[...] = jnp.zeros_like(dk_ref)
            dv_ref[...] = jnp.zeros_like(dv_ref)
            dpb_ref[...] = jnp.zeros_like(dpb_ref)

        def f(q_, z_, k_, v_, pb_):
            return _b_block(q_, z_, k_, v_, cq, sq, ck, sk, pb_, n, nb)

        _, vjp = jax.vjp(f, q_t, z_t, k3, v3, pb_v)
        dq, dz, dk3, dv3, dpb = vjp(dm_ref[...])
        dq_ref[...] = dq.astype(BF16)
        dz_ref[...] = dz.astype(BF16)
        dpb_ref[...] += dpb

        def add(j, cond):
            @pl.when(cond)
            def _():
                rows = pl.ds(pl.multiple_of((n - 1 + j) * w, w), w)
                dk_ref[rows, :] += dk3[j * w:(j + 1) * w, :]
                dv_ref[rows, :] += dv3[j * w:(j + 1) * w, :]

        add(0, n > 0)
        add(1, n >= 0)
        add(2, n < nb - 1)

    blk = pl.BlockSpec((w, 512), lambda n: (n, 0))
    whole = pl.BlockSpec((s, LANE), lambda n: (0, 0))
    small = pl.BlockSpec((8, LANE), lambda n: (0, 0))
    return pl.pallas_call(
        body, name="attn_b_bwd", grid=(nb,),
        in_specs=specs + [pl.BlockSpec((w, 512), lambda n: (n, 2))],
        out_specs=[blk, blk, whole, whole, small],
        out_shape=[jax.ShapeDtypeStruct((s, 512), BF16), jax.ShapeDtypeStruct((s, 512), BF16),
                   jax.ShapeDtypeStruct((s, LANE), F32), jax.ShapeDtypeStruct((s, LANE), F32),
                   jax.ShapeDtypeStruct((8, LANE), F32)],
        compiler_params=_params(("arbitrary",)),
    )(*_b_args(proj, cos2, sin2, pb), d_mixed)


def _mem_kv_fwd(mem, mem_norm_w, w_kv):
    def body(mem_ref, nw_ref, w_ref, kv_ref):
        mn = _rms(mem_ref[...], nw_ref[...]).astype(BF16)
        kv_ref[...] = jnp.dot(mn, w_ref[...], preferred_element_type=F32)

    return pl.pallas_call(
        body, name="mem_kv_fwd", out_shape=jax.ShapeDtypeStruct((MEM_LEN, 2 * C_HEADS * C_DIM), F32),
        compiler_params=_params(),
    )(mem, mem_norm_w, w_kv)


def _mem_kv_bwd(mem, mem_norm_w, w_kv, d_kv):
    def body(mem_ref, nw_ref, w_ref, g_ref, gw_ref, gn_ref):
        mn, vjp = jax.vjp(_rms, mem_ref[...], nw_ref[...])
        g = g_ref[...].astype(BF16)
        gw_ref[...] = lax.dot_general(mn.astype(BF16), g, (((0,), (0,)), ((), ())), preferred_element_type=F32)
        d_mn = lax.dot_general(g, w_ref[...], (((1,), (1,)), ((), ())), preferred_element_type=F32)
        gn_ref[...] = vjp(d_mn)[1]

    return pl.pallas_call(
        body, name="mem_kv_bwd",
        out_shape=[jax.ShapeDtypeStruct((D_MODEL, 2 * C_HEADS * C_DIM), F32), jax.ShapeDtypeStruct((1, D_MODEL), F32)],
        compiler_params=_params(),
    )(mem, mem_norm_w, w_kv, d_kv)


def _c_tile(q_t, z_t, kvm, pc):
    width = C_HEADS * C_DIM
    outs = []
    for h in range(C_HEADS):
        q = _rms(q_t[:, h * C_DIM:(h + 1) * C_DIM], pc[0:1, :])
        k = _rms(kvm[:, h * C_DIM:(h + 1) * C_DIM], pc[1:2, :])
        v = kvm[:, width + h * C_DIM:width + (h + 1) * C_DIM]
        s = _mm_nt(q, k) * (C_DIM ** -0.5)
        p = jnp.exp(s - jnp.max(s, axis=1, keepdims=True))
        p = p / jnp.sum(p, axis=1, keepdims=True)
        outs.append(_mm(p, v))
    return jnp.concatenate(outs, axis=1) * _silu(z_t)


def _attn_c_fwd(proj, kvm, pc, mixed, tq=256):
    s = proj.shape[0]

    def body(q_ref, z_ref, kv_ref, pc_ref, mixed_ref, o_ref):
        o_ref[...] = _c_tile(q_ref[...], z_ref[...], kv_ref[...], pc_ref[...]).astype(BF16)

    return pl.pallas_call(
        body, name="attn_c_fwd", grid=(s // tq,),
        in_specs=[pl.BlockSpec((tq, 512), lambda i: (i, P_QC // 512)), pl.BlockSpec((tq, 512), lambda i: (i, P_ZC // 512)),
                  pl.BlockSpec(kvm.shape, lambda i: (0, 0)), pl.BlockSpec((8, LANE), lambda i: (0, 0)),
                  pl.BlockSpec(memory_space=pl.ANY)],
        out_specs=pl.BlockSpec((tq, 512), lambda i: (i, (A_WIDTH + 512) // 512)),
        out_shape=jax.ShapeDtypeStruct(mixed.shape, mixed.dtype), input_output_aliases={4: 0},
        compiler_params=_params(("parallel",)),
    )(proj, proj, kvm, pc, mixed)


def _attn_c_bwd(proj, kvm, pc, d_mixed, tq=256):
    s = proj.shape[0]

    def body(q_ref, z_ref, kv_ref, pc_ref, dm_ref, dq_ref, dz_ref, dkv_ref, dpc_ref):
        @pl.when(pl.program_id(0) == 0)
        def _():
            dkv_ref[...] = jnp.zeros_like(dkv_ref)
            dpc_ref[...] = jnp.zeros_like(dpc_ref)

        _, vjp = jax.vjp(_c_tile, q_ref[...], z_ref[...], kv_ref[...], pc_ref[...])
        dq, dz, dkv, dpc = vjp(dm_ref[...])
        dq_ref[...] = dq.astype(BF16)
        dz_ref[...] = dz.astype(BF16)
        dkv_ref[...] += dkv
        dpc_ref[...] += dpc

    blk = pl.BlockSpec((tq, 512), lambda i: (i, 0))
    kvs = pl.BlockSpec(kvm.shape, lambda i: (0, 0))
    small = pl.BlockSpec((8, LANE), lambda i: (0, 0))
    return pl.pallas_call(
        body, name="attn_c_bwd", grid=(s // tq,),
        in_specs=[pl.BlockSpec((tq, 512), lambda i: (i, P_QC // 512)), pl.BlockSpec((tq, 512), lambda i: (i, P_ZC // 512)),
                  kvs, small, pl.BlockSpec((tq, 512), lambda i: (i, 3))],
        out_specs=[blk, blk, kvs, small],
        out_shape=[jax.ShapeDtypeStruct((s, 512), BF16), jax.ShapeDtypeStruct((s, 512), BF16),
                   jax.ShapeDtypeStruct(kvm.shape, F32), jax.ShapeDtypeStruct((8, LANE), F32)],
        compiler_params=_params(("arbitrary",)),
    )(proj, proj, kvm, pc, d_mixed)


def _pad_row(v, width=LANE):
    v = v.reshape(1, -1)
    return jnp.pad(v, ((0, 0), (0, width - v.shape[1])))


def _local_step(x, mem, target, norm_w, w_perm_t, w_blocks_t, conv_w, pa, pb, pc, mem_norm_w, w_kv, w_out, gather=None,
                exchange=None):
    s = x.shape[0]
    cos2, sin2 = _rope_tables(s)
    hn = _rms_fwd(x, norm_w)
    wide = dict(tm=1024, tn=512, tk=2048)
    proj = _matmul(hn, w_perm_t, "nt", F32, "mm_proj", **wide)
    cqkv = _conv_fwd(proj, conv_w)
    if gather is None:
        mixed, o_sum, states, inverses = _delta_fwd(cqkv, proj, pa)
    else:
        mixed, o_sum, states, inverses, *arrived = _delta_fwd(cqkv, proj, pa, gather[0])
        w_out, w_kv = gather[1](*arrived)
    mixed = _attn_b_fwd(proj, cos2, sin2, pb, mixed)
    kvm = _mem_kv_fwd(mem, mem_norm_w, w_kv)
    mixed = _attn_c_fwd(proj, kvm, pc, mixed)
    mo = _matmul(mixed, w_out, "nn", F32, "mm_out", **wide)
    dy, dyb, loss_parts = _loss_dy(x, mo, target)

    d_mixed = _matmul(dyb, w_out, "nt", F32, "mm_dmixed", **wide)
    g_w_out = _matmul(mixed, dyb, "tn", F32, "mm_gwout", **wide)
    d_qc, d_zc, d_kvm, d_pc = _attn_c_bwd(proj, kvm, pc, d_mixed)
    g_w_kv, g_mem_norm = _mem_kv_bwd(mem, mem_norm_w, w_kv, d_kvm)
    d_qb, d_zb, d_kb, d_vb, d_pb = _attn_b_bwd(proj, cos2, sin2, pb, d_mixed)
    d_o, d_za, d_pa_out = _delta_out_bwd(o_sum, proj, pa, d_mixed)
    early = exchange[0](g_w_out, g_w_kv) if exchange else None
    d_c, d_gt, d_pa_scan, *landed_early = _delta_bwd(cqkv, proj, pa, d_o, states, inverses, early)
    d_pa = d_pa_out + d_pa_scan
    d_qkv, g_conv = _conv_bwd(proj, conv_w, d_c)
    d_proj = _cotangent_blocks(d_qkv, d_za, d_gt, d_qb, d_kb, d_vb, d_zb, d_qc, d_zc)
    g_w_blocks_t = _matmul(d_proj, hn, "tn", F32, "mm_gwin", tm=512, tn=2048, tk=2048)
    late = exchange[1](g_w_blocks_t) if exchange else None
    g_x, g_norm, *landed_late = _input_grad(d_proj, w_blocks_t, x, norm_w, dy, late)
    return dict(loss_parts=loss_parts, g_x=g_x, g_norm=g_norm, g_w_blocks_t=g_w_blocks_t, g_conv=g_conv, d_pa=d_pa,
                d_pb=d_pb, d_pc=d_pc, g_mem_norm=g_mem_norm, g_w_kv=g_w_kv, g_w_out=g_w_out,
                landed=landed_late + landed_early)


_SEGMENTS = ((0, O_GT, 0), (O_GT, O_QB, P_GT), (O_QB, O_KB, P_QB), (O_KB, O_VB, P_KB), (O_VB, O_ZB, P_VB),
             (O_ZB, O_QC, P_ZB), (O_QC, O_ZC, P_QC), (O_ZC, IN_WIDTH, P_ZC))


def _permute_blocks(w4):
    parts = []
    for first, end, _ in sorted(_SEGMENTS, key=lambda seg: seg[2]):
        row = first
        while row < end:
            k = row // W_IN_BLOCK
            stop = min(end, (k + 1) * W_IN_BLOCK)
            parts.append(w4[k][row - k * W_IN_BLOCK:stop - k * W_IN_BLOCK, :])
            row = stop
    parts.append(jnp.zeros((P_WIDTH - IN_WIDTH, w4.shape[2]), w4.dtype))
    return jnp.concatenate(parts, axis=0)


def _cotangent_blocks(d_qkv, d_za, d_gt, d_qb, d_kb, d_vb, d_zb, d_qc, d_zc):
    s = d_qkv.shape[0]
    tr = min(256, s)
    pieces = (d_qkv, d_za, d_gt, d_qb, d_kb, d_vb, d_zb, d_qc, d_zc)

    def body(*refs):
        o_ref = refs[-1]
        tiles = [r[...].astype(BF16) for r in refs[:-1]]
        tiles[2] = tiles[2][:, :O_QB - O_GT]
        orig = jnp.concatenate(tiles, axis=1)
        pad = jnp.zeros((tr, W_IN_PAD - W_IN_BLOCK), BF16)
        parts = []
        for k in range(N_CHIPS):
            parts += [orig[:, k * W_IN_BLOCK:(k + 1) * W_IN_BLOCK], pad]
        o_ref[...] = jnp.concatenate(parts, axis=1)

    return pl.pallas_call(
        body, name="cotangent_blocks", grid=(s // tr,),
        in_specs=[pl.BlockSpec((tr, p.shape[1]), lambda i: (i, 0)) for p in pieces],
        out_specs=pl.BlockSpec((tr, N_CHIPS * W_IN_PAD), lambda i: (i, 0)),
        out_shape=jax.ShapeDtypeStruct((s, N_CHIPS * W_IN_PAD), BF16), compiler_params=_params(("parallel",)),
    )(*pieces)


HBM = pl.BlockSpec(memory_space=pltpu.HBM)


def _place():
    x, y, c = lax.axis_index("x"), lax.axis_index("y"), lax.axis_index("c")
    chips = [(1 - x, y), (x, 1 - y), (1 - x, 1 - y)]
    return x, y, c, 2 * x + y, chips, [2 * cx + cy for cx, cy in chips]


PIECE_ROWS_CAP = 600


def _remote(src, dst, send_sems, recv_sems, k, to):
    return pltpu.make_async_remote_copy(src_ref=src, dst_ref=dst, send_sem=send_sems.at[k], recv_sem=recv_sems.at[k],
                                        device_id=to, device_id_type=MESH)


def _half_cols(ref, c):
    half = ref.shape[-1] // 2
    return pl.ds(pl.multiple_of(c * half, LANE), half)


class _PairedGather:
    def __init__(self, blocks):
        n = len(blocks)
        self.operands = list(blocks)
        self.out_shapes = [jax.ShapeDtypeStruct((N_CHIPS,) + b.shape, b.dtype) for b in blocks]
        self.scratch_shapes = [pltpu.SemaphoreType.DMA((6 * n,)), pltpu.SemaphoreType.DMA((6 * n,))]

    @staticmethod
    def _copies(srcs, dsts, sems):
        x, y, c, me, chips, chip_ids = _place()
        sends, landed, passes, passed = [], [], [], []
        for a, (src, dst) in enumerate(zip(srcs, dsts)):
            mine, other = _half_cols(src, c), _half_cols(src, 1 - c)
            for j, (chip, cid) in enumerate(zip(chips, chip_ids)):
                sends.append(_remote(src.at[:, mine], dst.at[me, :, mine], sems[0], sems[1], 6 * a + j, (*chip, c)))
                here = dst.at[cid, :, mine]
                landed.append(_remote(here, here, sems[0], sems[1], 6 * a + j, (x, y, 1 - c)))
                passes.append(_remote(here, here, sems[0], sems[1], 6 * a + 3 + j, (x, y, 1 - c)))
                there = dst.at[cid, :, other]
                passed.append(_remote(there, there, sems[0], sems[1], 6 * a + 3 + j, (x, y, 1 - c)))
        return sends, landed, passes, passed

    def start(self, srcs, dsts, sems):
        for cp in self._copies(srcs, dsts, sems)[0]:
            cp.start()

    def middle(self, srcs, dsts, sems):
        _, landed, passes, _ = self._copies(srcs, dsts, sems)
        for arrived, onward in zip(landed, passes):
            arrived.wait_recv()
            onward.start()

    def finish(self, srcs, dsts, sems):
        sends, _, passes, passed = self._copies(srcs, dsts, sems)
        for cp in passed:
            cp.wait_recv()
        for cp in sends + passes:
            cp.wait_send()


def _all_gather_weights(bigs, conv_b):
    bigs = tuple(bigs)
    n_big = len(bigs)

    def body(*refs):
        srcs, conv_src = refs[:n_big], refs[n_big]
        dsts, conv_dst = refs[n_big + 1:2 * n_big + 1], refs[2 * n_big + 1]
        send_sems, recv_sems, local_sems = refs[2 * n_big + 2:]
        x, y, c, me, chips, chip_ids = _place()
        sibling = (x, y, 1 - c)
        local = [pltpu.make_async_copy(src, dst.at[me], local_sems.at[a]) for a, (src, dst) in enumerate(zip(srcs, dsts))]
        local.append(pltpu.make_async_copy(conv_src, conv_dst.at[me], local_sems.at[n_big]))
        for cp in local:
            cp.start()
        sends = []
        for a, (src, dst) in enumerate(zip(srcs, dsts)):
            mine = _half_cols(src, c)
            for j, chip in enumerate(chips):
                sends.append(_remote(src.at[:, mine], dst.at[me, :, mine], send_sems, recv_sems, 6 * a + j, (*chip, c)))
        for j, chip in enumerate(chips):
            sends.append(_remote(conv_src, conv_dst.at[me], send_sems, recv_sems, 6 * n_big + j, (*chip, c)))
        for cp in sends:
            cp.start()
        passed = []
        for a, (src, dst) in enumerate(zip(srcs, dsts)):
            mine = _half_cols(src, c)
            for j, cid in enumerate(chip_ids):
                landed = dst.at[cid, :, mine]
                _remote(landed, landed, send_sems, recv_sems, 6 * a + j, sibling).wait_recv()
                cp = _remote(landed, landed, send_sems, recv_sems, 6 * a + 3 + j, sibling)
                cp.start()
                passed.append(cp)
        for a, (src, dst) in enumerate(zip(srcs, dsts)):
            other = _half_cols(src, 1 - c)
            for j, cid in enumerate(chip_ids):
                landed = dst.at[cid, :, other]
                _remote(landed, landed, send_sems, recv_sems, 6 * a + 3 + j, sibling).wait_recv()
        for j, cid in enumerate(chip_ids):
            _remote(conv_src, conv_dst.at[cid], send_sems, recv_sems, 6 * n_big + j, sibling).wait_recv()
        for cp in sends + passed:
            cp.wait_send()
        for cp in local:
            cp.wait()

    n_sem = 6 * n_big + 3
    return pl.pallas_call(
        body, name="all_gather_weights",
        out_shape=[jax.ShapeDtypeStruct((N_CHIPS,) + w.shape, w.dtype) for w in bigs + (conv_b,)],
        in_specs=[pl.BlockSpec(memory_space=pltpu.VMEM)] * (n_big + 1), out_specs=[HBM] * (n_big + 1),
        scratch_shapes=[pltpu.SemaphoreType.DMA((n_sem,)), pltpu.SemaphoreType.DMA((n_sem,)),
                        pltpu.SemaphoreType.DMA((n_big + 1,))],
        compiler_params=_params(),
    )(*bigs, conv_b)


def _pair_exchange(grads, name):
    n = len(grads)
    pieces = [_row_tile(g.shape[1]) for g in grads]

    def body(*refs):
        srcs, gots = refs[:n], refs[n:2 * n]
        stages = refs[2 * n:3 * n]
        send_sems, recv_sems, load_sems = refs[3 * n:]
        x, y, c, _, _, _ = _place()
        sibling = (x, y, 1 - c)
        for a in range(n):
            slabs, rows, _ = gots[a].shape
            piece = pieces[a]
            per_slab = rows // piece
            theirs = _half_cols(srcs[a], 1 - c)
            loads, sends = [], []
            for i in range(slabs * per_slab):
                k, r, slot = i // per_slab, i % per_slab, i % 2
                part = pl.ds(r * piece, piece)
                loads.append(pltpu.make_async_copy(srcs[a].at[k, part, theirs], stages[a].at[slot], load_sems.at[2 * a + slot]))
                sends.append(pltpu.make_async_remote_copy(
                    src_ref=stages[a].at[slot], dst_ref=gots[a].at[k, part, :],
                    send_sem=send_sems.at[2 * a + slot], recv_sem=recv_sems.at[a], device_id=sibling, device_id_type=MESH))
            loads[0].start()
            for i in range(len(loads)):
                loads[i].wait()
                sends[i].start()
                if i + 1 < len(loads):
                    if i >= 1:
                        sends[i - 1].wait_send()
                    loads[i + 1].start()
            for cp in sends[-2:]:
                cp.wait_send()
        for a in range(n):
            whole = srcs[a].at[:, :, _half_cols(srcs[a], c)]
            pltpu.make_async_remote_copy(src_ref=whole, dst_ref=gots[a], send_sem=send_sems.at[2 * a],
                                         recv_sem=recv_sems.at[a], device_id=sibling, device_id_type=MESH).wait_recv()

    halves = [jax.ShapeDtypeStruct((g.shape[0], g.shape[1], g.shape[2] // 2), g.dtype) for g in grads]
    return pl.pallas_call(
        body, name=name, out_shape=halves, in_specs=[HBM] * n, out_specs=[HBM] * n,
        scratch_shapes=[pltpu.VMEM((2, piece, g.shape[2] // 2), g.dtype) for piece, g in zip(pieces, grads)]
        + [pltpu.SemaphoreType.DMA((2 * n,)), pltpu.SemaphoreType.DMA((n,)), pltpu.SemaphoreType.DMA((2 * n,))],
        compiler_params=_params(),
    )(*grads)


class _ChipExchange:
    def __init__(self, halves):
        n = len(halves)
        self.operands = list(halves)
        self.out_shapes = [jax.ShapeDtypeStruct((N_CHIPS - 1,) + h.shape[1:], h.dtype) for h in halves]
        self.scratch_shapes = [pltpu.SemaphoreType.DMA((3 * n,)), pltpu.SemaphoreType.DMA((3 * n,))]

    @staticmethod
    def _copies(srcs, lands, sems):
        _, _, c, _, chips, chip_ids = _place()
        return [_remote(src.at[cid], land.at[j], sems[0], sems[1], 3 * a + j, (*chip, c))
                for a, (src, land) in enumerate(zip(srcs, lands)) for j, (chip, cid) in enumerate(zip(chips, chip_ids))]

    def start(self, srcs, lands, sems):
        for cp in self._copies(srcs, lands, sems):
            cp.start()

    def finish(self, srcs, lands, sems):
        copies = self._copies(srcs, lands, sems)
        for cp in copies:
            cp.wait_recv()
        for cp in copies:
            cp.wait_send()


def _pair_gather(halves, rows):
    n = len(halves)

    def body(*refs):
        srcs, fulls = refs[:n], refs[n:2 * n]
        send_sems, recv_sems, local_sems = refs[2 * n:]
        x, y, c, _, _, _ = _place()
        copies = []
        for a in range(n):
            mine, src = _half_cols(fulls[a], c), srcs[a].at[pl.ds(0, rows[a]), :]
            keep = pltpu.make_async_copy(src, fulls[a].at[:, mine], local_sems.at[a])
            keep.start()
            give = _remote(src, fulls[a].at[:, mine], send_sems, recv_sems, a, (x, y, 1 - c))
            give.start()
            copies += [keep, give]
        for a in range(n):
            other, src = _half_cols(fulls[a], 1 - c), srcs[a].at[pl.ds(0, rows[a]), :]
            copies[2 * a].wait()
            copies[2 * a + 1].wait_send()
            _remote(src, fulls[a].at[:, other], send_sems, recv_sems, a, (x, y, 1 - c)).wait_recv()

    return pl.pallas_call(
        body, name="grad_pair_gather",
        out_shape=[jax.ShapeDtypeStruct((r, 2 * h.shape[1]), h.dtype) for r, h in zip(rows, halves)],
        in_specs=[pl.BlockSpec(memory_space=pltpu.VMEM)] * n, out_specs=[HBM] * n,
        scratch_shapes=[pltpu.SemaphoreType.DMA((n,)), pltpu.SemaphoreType.DMA((n,)), pltpu.SemaphoreType.DMA((n,))],
    )(*halves)


def _all_reduce_small(p):
    n_dev = 8

    def body(p_ref, o_ref, land, send_sems, recv_sems):
        x, y, c = lax.axis_index("x"), lax.axis_index("y"), lax.axis_index("c")
        me = 4 * x + 2 * y + c
        land[me] = p_ref[...]
        sends = []
        for k in range(1, n_dev):
            fx, fy, fc = (k >> 2) & 1, (k >> 1) & 1, k & 1
            to = (x ^ fx, y ^ fy, c ^ fc)
            cp = _remote(p_ref, land.at[me], send_sems, recv_sems, k - 1, to)
            cp.start()
            sends.append(cp)
        for k in range(1, n_dev):
            _remote(p_ref, land.at[me ^ k], send_sems, recv_sems, k - 1, (x, y, c)).wait_recv()
        total = land[0]
        for d in range(1, n_dev):
            total = total + land[d]
        o_ref[...] = total
        for cp in sends:
            cp.wait_send()

    vm = pl.BlockSpec(memory_space=pltpu.VMEM)
    return pl.pallas_call(
        body, name="all_reduce_small", out_shape=jax.ShapeDtypeStruct(p.shape, p.dtype), in_specs=[vm], out_specs=vm,
        scratch_shapes=[pltpu.VMEM((n_dev,) + p.shape, p.dtype), pltpu.SemaphoreType.DMA((n_dev - 1,)),
                        pltpu.SemaphoreType.DMA((n_dev - 1,))],
    )(p)


def _row_tile(rows):
    fits = [t for t in range(8, min(rows, PIECE_ROWS_CAP) + 1, 8) if rows % t == 0]
    return max(fits) if fits else rows


def _pair_sum(full, got, core, name):
    n, r, c = got.shape
    tr = _row_tile(r)

    def body(core_ref, a_ref, b_ref, o_ref):
        o_ref[...] = (a_ref[...] + b_ref[...]).astype(BF16)

    blk = pl.BlockSpec((None, tr, c), lambda i, j, core_ref: (i, j, 0))
    grid_spec = pltpu.PrefetchScalarGridSpec(
        num_scalar_prefetch=1, grid=(n, r // tr),
        in_specs=[pl.BlockSpec((None, tr, c), lambda i, j, core_ref: (i, j, core_ref[0])), blk], out_specs=blk)
    return pl.pallas_call(body, name=name, grid_spec=grid_spec, out_shape=jax.ShapeDtypeStruct(got.shape, BF16),
                          compiler_params=_params(("parallel", "parallel")))(core, full, got)


def _chip_sum(full, got, land, place, name):
    n, r, c = land.shape
    tr = _row_tile(r)

    def body(place_ref, a_ref, b_ref, l_ref, o_ref):
        total = a_ref[...] + b_ref[...]
        for j in range(n):
            total = total + l_ref[j].astype(F32)
        o_ref[...] = total

    grid_spec = pltpu.PrefetchScalarGridSpec(
        num_scalar_prefetch=1, grid=(r // tr,),
        in_specs=[pl.BlockSpec((None, tr, c), lambda i, p: (p[0], i, p[1])),
                  pl.BlockSpec((None, tr, c), lambda i, p: (p[0], i, 0)),
                  pl.BlockSpec((n, tr, c), lambda i, p: (0, i, 0))],
        out_specs=pl.BlockSpec((tr, c), lambda i, p: (i, 0)))
    return pl.pallas_call(body, name=name, grid_spec=grid_spec, out_shape=jax.ShapeDtypeStruct((r, c), F32),
                          compiler_params=_params(("parallel",)))(place, full, got, land)


def _adamw(w, g, m, v, name):
    r, c = w.shape
    tr = _row_tile(r)
    tc = 1024 if c % 1024 == 0 else c

    def body(w_ref, g_ref, m_ref, v_ref, d_ref, mo_ref, vo_ref):
        g_ = g_ref[...]
        m2 = ADAM_B1 * m_ref[...] + (1.0 - ADAM_B1) * g_
        v2 = ADAM_B2 * v_ref[...] + (1.0 - ADAM_B2) * jnp.square(g_)
        m_hat = m2 / (1.0 - ADAM_B1 ** ADAM_STEP)
        v_hat = v2 / (1.0 - ADAM_B2 ** ADAM_STEP)
        d_ref[...] = -ADAM_LR * (m_hat / (jnp.sqrt(v_hat) + ADAM_EPS) + ADAM_WD * w_ref[...])
        mo_ref[...] = m2
        vo_ref[...] = v2

    blk = pl.BlockSpec((tr, tc), lambda i, j: (i, j))
    return pl.pallas_call(body, name=name, grid=(r // tr, c // tc), in_specs=[blk] * 4, out_specs=[blk] * 3,
                          out_shape=[jax.ShapeDtypeStruct(w.shape, F32)] * 3,
                          compiler_params=_params(("parallel", "parallel")))(w, g, m, v)


SMALL_NAMES = ("norm_w", "mem_norm_w", "o_norm_a", "q_norm_c", "k_norm_c", "q_norm_b", "k_norm_b",
               "a_log_fwd", "a_log_bwd", "dt_bias_fwd", "dt_bias_bwd", "sink_b")
SMALL_SIZES = (2048, 2048, 128, 128, 128, 64, 64, 8, 8, 8, 8, 8)
SMALL_LOSS = sum(SMALL_SIZES)
SMALL_CONV = 5120
SMALL_TOTAL = SMALL_CONV + CONV_K * 3 * A_WIDTH
SMALL_ROWS = SMALL_TOTAL // LANE


def _pack_small(parts, extra=None, conv=None):
    vec = [parts[n].reshape(-1) for n in SMALL_NAMES]
    vec.append(jnp.zeros((1,), F32) if extra is None else extra.reshape(1))
    vec.append(jnp.zeros((SMALL_CONV - SMALL_LOSS - 1,), F32))
    vec.append(jnp.zeros((SMALL_TOTAL - SMALL_CONV,), F32) if conv is None else conv.reshape(-1))
    return jnp.concatenate(vec).reshape(SMALL_ROWS, LANE)


def _unpack_small(packed):
    flat = packed.reshape(-1)
    out, off = {}, 0
    for n, size in zip(SMALL_NAMES, SMALL_SIZES):
        out[n] = flat[off:off + size].reshape(1, size)
        off += size
    return out


WEIGHT_ORDER = ("norm_w", "w_in", "conv_w_a", "a_log_fwd", "a_log_bwd", "dt_bias_fwd", "dt_bias_bwd", "o_norm_a",
                "q_norm_b", "k_norm_b", "sink_b", "mem_norm_w", "w_mem_kv", "q_norm_c", "k_norm_c", "w_out")


def kernel(x, mem, norm_w, w_in, conv_w_a, a_log_fwd, a_log_bwd, dt_bias_fwd, dt_bias_bwd, o_norm_a, q_norm_b, k_norm_b, sink_b, mem_norm_w, w_mem_kv, q_norm_c, k_norm_c, w_out, loss_target, m_norm_w, m_w_in, m_conv_w_a, m_a_log_fwd, m_a_log_bwd, m_dt_bias_fwd, m_dt_bias_bwd, m_o_norm_a, m_q_norm_b, m_k_norm_b, m_sink_b, m_mem_norm_w, m_w_mem_kv, m_q_norm_c, m_k_norm_c, m_w_out, v_norm_w, v_w_in, v_conv_w_a, v_a_log_fwd, v_a_log_bwd, v_dt_bias_fwd, v_dt_bias_bwd, v_o_norm_a, v_q_norm_b, v_k_norm_b, v_sink_b, v_mem_norm_w, v_w_mem_kv, v_q_norm_c, v_k_norm_c, v_w_out):
    weights = dict(norm_w=norm_w, w_in=w_in, conv_w_a=conv_w_a, a_log_fwd=a_log_fwd, a_log_bwd=a_log_bwd,
                   dt_bias_fwd=dt_bias_fwd, dt_bias_bwd=dt_bias_bwd, o_norm_a=o_norm_a, q_norm_b=q_norm_b,
                   k_norm_b=k_norm_b, sink_b=sink_b, mem_norm_w=mem_norm_w, w_mem_kv=w_mem_kv, q_norm_c=q_norm_c,
                   k_norm_c=k_norm_c, w_out=w_out)
    mom1 = dict(norm_w=m_norm_w, w_in=m_w_in, conv_w_a=m_conv_w_a, a_log_fwd=m_a_log_fwd, a_log_bwd=m_a_log_bwd,
                dt_bias_fwd=m_dt_bias_fwd, dt_bias_bwd=m_dt_bias_bwd, o_norm_a=m_o_norm_a, q_norm_b=m_q_norm_b,
                k_norm_b=m_k_norm_b, sink_b=m_sink_b, mem_norm_w=m_mem_norm_w, w_mem_kv=m_w_mem_kv,
                q_norm_c=m_q_norm_c, k_norm_c=m_k_norm_c, w_out=m_w_out)
    mom2 = dict(norm_w=v_norm_w, w_in=v_w_in, conv_w_a=v_conv_w_a, a_log_fwd=v_a_log_fwd, a_log_bwd=v_a_log_bwd,
                dt_bias_fwd=v_dt_bias_fwd, dt_bias_bwd=v_dt_bias_bwd, o_norm_a=v_o_norm_a, q_norm_b=v_q_norm_b,
                k_norm_b=v_k_norm_b, sink_b=v_sink_b, mem_norm_w=v_mem_norm_w, w_mem_kv=v_w_mem_kv,
                q_norm_c=v_q_norm_c, k_norm_c=v_k_norm_c, w_out=v_w_out)
    chip = 2 * lax.axis_index("x") + lax.axis_index("y")

    own_in = jnp.pad(jnp.transpose(w_in[0]).astype(BF16), ((0, W_IN_PAD - W_IN_BLOCK), (0, 0)))
    w_in4, conv4 = _all_gather_weights([own_in], conv_w_a[0])
    w_perm_t = _permute_blocks(w_in4)
    w_blocks_t = w_in4.reshape(N_CHIPS * W_IN_PAD, D_MODEL)
    conv_full = jnp.transpose(conv4, (1, 0, 2)).reshape(CONV_K, 3 * A_WIDTH)
    own_out, own_kv = w_out[0].astype(BF16), w_mem_kv[0].astype(BF16)

    def assemble(w_out4, w_kv4):
        w_out4 = lax.dynamic_update_index_in_dim(w_out4, own_out, chip, 0)
        w_kv4 = lax.dynamic_update_index_in_dim(w_kv4, own_kv, chip, 0)
        return w_out4.reshape(D_MODEL, D_MODEL), w_kv4.reshape(D_MODEL, 2 * C_HEADS * C_DIM)

    gather = (_PairedGather([own_out, own_kv]), assemble)
    pa = jnp.concatenate([_pad_row(a_log_fwd), _pad_row(a_log_bwd), _pad_row(dt_bias_fwd), _pad_row(dt_bias_bwd),
                          _pad_row(o_norm_a), jnp.zeros((3, LANE), F32)], axis=0)
    pb = jnp.concatenate([_pad_row(q_norm_b), _pad_row(k_norm_b), _pad_row(sink_b), jnp.zeros((5, LANE), F32)], axis=0)
    pc = jnp.concatenate([_pad_row(q_norm_c), _pad_row(k_norm_c), jnp.zeros((6, LANE), F32)], axis=0)

    full, got = {}, {}
    core = lax.axis_index("c").astype(jnp.int32).reshape(1)

    def pair_round(tag, blocks):
        names = [tag + "_%d" % i for i in range(len(blocks))]
        full.update(zip(names, blocks))
        got.update(zip(names, _pair_exchange(blocks, "grad_pair_exchange_" + tag)))
        return _ChipExchange([_pair_sum(full[n], got[n], core, "grad_pair_sum_" + n) for n in names])

    def early(g_w_out, g_w_kv):
        return pair_round("early", [g_w_out.reshape(N_CHIPS, D_MODEL // N_CHIPS, D_MODEL),
                                    g_w_kv.reshape(N_CHIPS, D_MODEL // N_CHIPS, 2 * C_HEADS * C_DIM)])

    def late(g_w_blocks_t):
        return pair_round("late", [g_w_blocks_t.reshape(N_CHIPS, W_IN_PAD, D_MODEL)])

    r = _local_step(x[0], mem[0], loss_target[0], norm_w, w_perm_t, w_blocks_t, conv_full, pa, pb, pc, mem_norm_w, None, None,
                    gather, (early, late))
    place = jnp.stack([chip, lax.axis_index("c")]).astype(jnp.int32)
    reduced = [_chip_sum(full[n], got[n], l, place, "grad_chip_sum_" + n)
               for n, l in zip(("late_0", "early_0", "early_1"), r["landed"])]
    g_w_in_t, g_w_out, g_w_kv = _pair_gather(reduced, [W_IN_BLOCK, D_MODEL // N_CHIPS, D_MODEL // N_CHIPS])

    d_pa, d_pb, d_pc = r["d_pa"], r["d_pb"], r["d_pc"]
    small_g = dict(norm_w=r["g_norm"], mem_norm_w=r["g_mem_norm"], o_norm_a=d_pa[4], q_norm_c=d_pc[0], k_norm_c=d_pc[1],
                   q_norm_b=d_pb[0, :B_DIM], k_norm_b=d_pb[1, :B_DIM], a_log_fwd=d_pa[0, :A_HEADS],
                   a_log_bwd=d_pa[1, :A_HEADS], dt_bias_fwd=d_pa[2, :A_HEADS], dt_bias_bwd=d_pa[3, :A_HEADS],
                   sink_b=d_pb[2, :B_HEADS])
    packed = _all_reduce_small(_pack_small(small_g, jnp.sum(r["loss_parts"][:, 0, 0]), r["g_conv"]))
    flat = packed.reshape(-1)
    loss = flat[SMALL_LOSS]
    conv_sum = flat[SMALL_CONV:].reshape(CONV_K, 3 * A_WIDTH)
    conv_cols = 3 * A_WIDTH // N_CHIPS
    g_conv = lax.dynamic_slice(conv_sum, (0, chip * conv_cols), (CONV_K, conv_cols))

    grads = _unpack_small(packed)
    grads.update(w_in=jnp.transpose(g_w_in_t), w_mem_kv=g_w_kv, w_out=g_w_out, conv_w_a=g_conv)
    delta, new_m, new_v = {}, {}, {}
    for n in ("w_mem_kv", "w_out", "conv_w_a"):
        delta[n], new_m[n], new_v[n] = _adamw(weights[n][0], grads[n], mom1[n][0], mom2[n][0], "adamw_" + n)
    stepped = _adamw(jnp.transpose(w_in[0]), g_w_in_t, jnp.transpose(m_w_in[0]), jnp.transpose(v_w_in[0]), "adamw_w_in")
    delta["w_in"], new_m["w_in"], new_v["w_in"] = (jnp.transpose(t) for t in stepped)
    d_s, m_s, v_s = _adamw(_pack_small(weights), packed, _pack_small(mom1), _pack_small(mom2), "adamw_small")
    d_s, m_s, v_s = _unpack_small(d_s), _unpack_small(m_s), _unpack_small(v_s)
    for n in SMALL_NAMES:
        delta[n], new_m[n], new_v[n] = d_s[n], m_s[n], v_s[n]

    def shaped(tree):
        return [tree[n].reshape(weights[n].shape) for n in WEIGHT_ORDER]

    return (loss, r["g_x"].reshape(x.shape), *shaped(grads), *shaped(delta), *shaped(new_m), *shaped(new_v))
```

```python
import functools

import jax
import jax.numpy as jnp
from jax import lax
from jax.experimental import pallas as pl
from jax.experimental.pallas import tpu as pltpu

F32 = jnp.float32
BF16 = jnp.bfloat16
HI = lax.Precision.HIGHEST
MESH = pl.DeviceIdType.MESH

D_MODEL = 2048
A_WIDTH = 1024
A_HEADS = 8
A_DIM = 128
CONV_K = 5
CHUNK = 64
B_HEADS = 8
B_KV = 2
B_DIM = 64
WINDOW = 128
C_HEADS = 4
C_DIM = 128
MEM_LEN = 256
ROPE_THETA = 10000.0
EPS = 1e-6
IN_WIDTH = 6432
N_CHIPS = 4
W_IN_BLOCK = IN_WIDTH // N_CHIPS
W_IN_PAD = 1664

LANE = 128
P_QA, P_KA, P_VA, P_ZA = 0, 1024, 2048, 3072
P_QB, P_ZB, P_QC, P_ZC = 4096, 4608, 5120, 5632
P_KB, P_VB, P_GT = 6144, 6272, 6400
P_WIDTH = 6656
O_GT, O_QB, O_KB, O_VB, O_ZB, O_QC, O_ZC = 4096, 4128, 4640, 4768, 4896, 5408, 5920

ADAM_LR, ADAM_B1, ADAM_B2, ADAM_EPS, ADAM_WD, ADAM_STEP = 0.001, 0.9, 0.999, 1e-08, 0.01, 10

VMEM_LIMIT = 56 * 1024 * 1024


def _params(sem=None):
    return pltpu.CompilerParams(dimension_semantics=sem, vmem_limit_bytes=VMEM_LIMIT)


def _dot(a, b, dims=(((1,), (0,)), ((), ())), precision=HI):
    return lax.dot_general(a, b, dims, precision=precision, preferred_element_type=F32)


def _dot_nt(a, b, precision=HI):
    return _dot(a, b, (((1,), (1,)), ((), ())), precision)


def _dot_tn(a, b, precision=HI):
    return _dot(a, b, (((0,), (0,)), ((), ())), precision)


_NN = (((1,), (0,)), ((), ()))
_NT = (((1,), (1,)), ((), ()))
_TN = (((0,), (0,)), ((), ()))


def _bdot(a, b, dims):
    return lax.dot_general(a.astype(BF16), b.astype(BF16), dims, preferred_element_type=F32)


@jax.custom_vjp
def _mm(a, b):
    return _bdot(a, b, _NN)


_mm.defvjp(lambda a, b: (_bdot(a, b, _NN), (a, b)),
           lambda res, ct: (_bdot(ct, res[1], _NT), _bdot(res[0], ct, _TN)))


@jax.custom_vjp
def _mm_nt(a, b):
    return _bdot(a, b, _NT)


_mm_nt.defvjp(lambda a, b: (_bdot(a, b, _NT), (a, b)),
              lambda res, ct: (_bdot(ct, res[1], _NN), _bdot(ct, res[0], _TN)))


@jax.custom_vjp
def _mm_tn(a, b):
    return _bdot(a, b, _TN)


_mm_tn.defvjp(lambda a, b: (_bdot(a, b, _TN), (a, b)),
              lambda res, ct: (_bdot(res[1], ct, _NT), _bdot(res[0], ct, _NN)))


def _rms(t, w):
    return t * lax.rsqrt(jnp.mean(t * t, axis=-1, keepdims=True) + EPS) * w


def _l2(t):
    return t * lax.rsqrt(jnp.sum(t * t, axis=-1, keepdims=True) + EPS)


def _silu(t):
    return t * jax.nn.sigmoid(t)


def _softplus(t):
    return jnp.maximum(t, 0.0) + jnp.log1p(jnp.exp(-jnp.abs(t)))


def _matmul(a, b, mode, out_dtype, name, tm=512, tn=512, tk=512, ride=None):
    (m, k) = a.shape[::-1] if mode == "tn" else a.shape
    n = b.shape[0] if mode == "nt" else b.shape[1]
    tm, tn, tk = min(tm, m), min(tn, n), min(tk, k)
    assert m % tm == 0 and n % tn == 0 and k % tk == 0, (m, n, k, tm, tn, tk)
    if mode == "nn":
        a_spec = pl.BlockSpec((tm, tk), lambda i, j, kk: (i, kk))
        b_spec = pl.BlockSpec((tk, tn), lambda i, j, kk: (kk, j))
        dims = (((1,), (0,)), ((), ()))
    elif mode == "nt":
        a_spec = pl.BlockSpec((tm, tk), lambda i, j, kk: (i, kk))
        b_spec = pl.BlockSpec((tn, tk), lambda i, j, kk: (j, kk))
        dims = (((1,), (1,)), ((), ()))
    else:
        a_spec = pl.BlockSpec((tk, tm), lambda i, j, kk: (kk, i))
        b_spec = pl.BlockSpec((tk, tn), lambda i, j, kk: (kk, j))
        dims = (((0,), (0,)), ((), ()))
    nk = k // tk
    grid = (m // tm, n // tn, nk)
    n_in = len(ride.operands) if ride else 0
    n_out = len(ride.out_shapes) if ride else 0

    def body(*refs):
        a_ref, b_ref, o_ref = refs[0], refs[1], refs[2 + n_in]
        scratch = refs[3 + n_in + n_out:]
        step = (pl.program_id(0) * grid[1] + pl.program_id(1)) * nk + pl.program_id(2)
        riders = (refs[2:2 + n_in], refs[3 + n_in:3 + n_in + n_out], scratch[(0 if nk == 1 else 1):])
        if ride:
            pl.when(step == 0)(lambda: ride.start(*riders))
        if nk == 1:
            o_ref[...] = _bdot(a_ref[...], b_ref[...], dims).astype(out_dtype)
        else:
            acc_ref, kk = scratch[0], pl.program_id(2)

            @pl.when(kk == 0)
            def _():
                acc_ref[...] = jnp.zeros_like(acc_ref)

            acc_ref[...] += _bdot(a_ref[...], b_ref[...], dims)

            @pl.when(kk == nk - 1)
            def _():
                o_ref[...] = acc_ref[...].astype(out_dtype)
        if ride:
            pl.when(step == grid[0] * grid[1] * nk - 1)(lambda: ride.finish(*riders))

    out = pl.pallas_call(
        body, name=name, grid=grid,
        in_specs=[a_spec, b_spec] + [HBM] * n_in,
        out_specs=[pl.BlockSpec((tm, tn), lambda i, j, kk: (i, j))] + [HBM] * n_out,
        out_shape=[jax.ShapeDtypeStruct((m, n), out_dtype)] + (list(ride.out_shapes) if ride else []),
        scratch_shapes=([] if nk == 1 else [pltpu.VMEM((tm, tn), F32)]) + (list(ride.scratch_shapes) if ride else []),
        compiler_params=_params(("arbitrary",) * 3 if ride else ("parallel", "parallel", "arbitrary")),
    )(a, b, *(ride.operands if ride else []))
    return out if ride else out[0]


def _rms_fwd(x, w, tr=256):
    s, d = x.shape

    def body(x_ref, w_ref, o_ref):
        o_ref[...] = _rms(x_ref[...], w_ref[...]).astype(BF16)

    return pl.pallas_call(
        body, name="rms_fwd", grid=(s // tr,),
        in_specs=[pl.BlockSpec((tr, d), lambda i: (i, 0)), pl.BlockSpec((1, d), lambda i: (0, 0))],
        out_specs=pl.BlockSpec((tr, d), lambda i: (i, 0)),
        out_shape=jax.ShapeDtypeStruct((s, d), BF16), compiler_params=_params(("parallel",)),
    )(x, w)


def _input_grad(d_proj, w_t, x, w, dy, ride=None, tm=512, tk=512):
    s, k = d_proj.shape
    d = w_t.shape[1]
    tm = min(tm, s)
    nk = k // tk
    grid = (s // tm, nk)
    n_in = len(ride.operands) if ride else 0
    n_out = len(ride.out_shapes) if ride else 0

    def body(*refs):
        a_ref, b_ref, x_ref, w_ref, dy_ref = refs[:5]
        gx_ref, gw_ref = refs[5 + n_in:7 + n_in]
        acc_ref = refs[7 + n_in + n_out]
        riders = (refs[5:5 + n_in], refs[7 + n_in:7 + n_in + n_out], refs[8 + n_in + n_out:])
        kk = pl.program_id(1)
        step = pl.program_id(0) * nk + kk
        if ride:
            pl.when(step == 0)(lambda: ride.start(*riders))

        @pl.when(step == 0)
        def _():
            gw_ref[...] = jnp.zeros_like(gw_ref)

        @pl.when(kk == 0)
        def _():
            acc_ref[...] = jnp.zeros_like(acc_ref)

        acc_ref[...] += _bdot(a_ref[...], b_ref[...], _NN)

        @pl.when(kk == nk - 1)
        def _():
            _, vjp = jax.vjp(_rms, x_ref[...], w_ref[...])
            dx, dw = vjp(acc_ref[...])
            gx_ref[...] = dy_ref[...] + dx
            gw_ref[...] += dw

        if ride:
            pl.when(step == grid[0] * nk - 1)(lambda: ride.finish(*riders))

    row = pl.BlockSpec((tm, d), lambda i, kk: (i, 0))
    vec = pl.BlockSpec((1, d), lambda i, kk: (0, 0))
    return pl.pallas_call(
        body, name="input_grad", grid=grid,
        in_specs=[pl.BlockSpec((tm, tk), lambda i, kk: (i, kk)), pl.BlockSpec((tk, d), lambda i, kk: (kk, 0)), row, vec, row]
        + [HBM] * n_in,
        out_specs=[row, vec] + [HBM] * n_out,
        out_shape=[jax.ShapeDtypeStruct((s, d), F32), jax.ShapeDtypeStruct((1, d), F32)]
        + (list(ride.out_shapes) if ride else []),
        scratch_shapes=[pltpu.VMEM((tm, d), F32)] + (list(ride.scratch_shapes) if ride else []),
        compiler_params=_params(("arbitrary", "arbitrary")),
    )(d_proj, w_t, x, w, dy, *(ride.operands if ride else []))


def _loss_dy(x, mo, target, tr=256):
    s, d = x.shape
    nt = s // tr

    def body(x_ref, mo_ref, t_ref, dy_ref, dyb_ref, l_ref):
        err = x_ref[...] + mo_ref[...] - t_ref[...]
        dy = err * (1.0 / d)
        dy_ref[...] = dy
        dyb_ref[...] = dy.astype(BF16)
        l_ref[...] = jnp.full(l_ref.shape, 0.5 * jnp.sum(jnp.sum(err * err, axis=1, keepdims=True) * (1.0 / d)), F32)

    row = pl.BlockSpec((tr, d), lambda i: (i, 0))
    return pl.pallas_call(
        body, name="loss_dy", grid=(nt,), in_specs=[row, row, row],
        out_specs=[row, row, pl.BlockSpec((1, 8, LANE), lambda i: (i, 0, 0))],
        out_shape=[jax.ShapeDtypeStruct((s, d), F32), jax.ShapeDtypeStruct((s, d), BF16),
                   jax.ShapeDtypeStruct((nt, 8, LANE), F32)],
        compiler_params=_params(("parallel",)),
    )(x, mo, target)


def _shift_rows(t, s):
    if s == 0:
        return t
    n = t.shape[0]
    rolled = pltpu.roll(t, (-s) % n, axis=0)
    idx = lax.broadcasted_iota(jnp.int32, t.shape, 0) + s
    return jnp.where((idx >= 0) & (idx < n), rolled, 0.0)


CONV_COLS = 512
CONV_SPLIT = A_WIDTH // CONV_COLS


def _conv_fwd(proj, conv_w):
    s = proj.shape[0]
    nblk = 3 * A_WIDTH // CONV_COLS

    def body(x_ref, w_ref, o_ref):
        x = x_ref[...]
        acc = jnp.zeros_like(x)
        for j in range(CONV_K):
            acc = acc + w_ref[j:j + 1, :] * _shift_rows(x, j - CONV_K // 2)
        o_ref[...] = acc

    return pl.pallas_call(
        body, name="conv_fwd", grid=(nblk,),
        in_specs=[pl.BlockSpec((s, CONV_COLS), lambda i: (0, i)), pl.BlockSpec((CONV_K, CONV_COLS), lambda i: (0, i))],
        out_specs=pl.BlockSpec((None, s, CONV_COLS), lambda i: (i // CONV_SPLIT, 0, i % CONV_SPLIT)),
        out_shape=jax.ShapeDtypeStruct((3, s, A_WIDTH), F32), compiler_params=_params(("parallel",)),
    )(proj, conv_w)


def _conv_bwd(proj, conv_w, d_c):
    s = proj.shape[0]
    nblk = 3 * A_WIDTH // CONV_COLS

    def body(x_ref, w_ref, g_ref, dx_ref, dw_ref):
        x, g = x_ref[...], g_ref[...]
        acc = jnp.zeros_like(x)
        for j in range(CONV_K):
            off = j - CONV_K // 2
            acc = acc + w_ref[j:j + 1, :] * _shift_rows(g, -off)
            dw_ref[j:j + 1, :] = jnp.sum(_shift_rows(x, off) * g, axis=0, keepdims=True)
        dx_ref[...] = acc.astype(BF16)

    col = pl.BlockSpec((s, CONV_COLS), lambda i: (0, i))
    wsp = pl.BlockSpec((CONV_K, CONV_COLS), lambda i: (0, i))
    dsp = pl.BlockSpec((None, s, CONV_COLS), lambda i: (i // CONV_SPLIT, 0, i % CONV_SPLIT))
    return pl.pallas_call(
        body, name="conv_bwd", grid=(nblk,), in_specs=[col, wsp, dsp], out_specs=[col, wsp],
        out_shape=[jax.ShapeDtypeStruct((s, 3 * A_WIDTH), BF16), jax.ShapeDtypeStruct((CONV_K, 3 * A_WIDTH), F32)],
        compiler_params=_params(("parallel",)),
    )(proj, conv_w, d_c)


A_FWD_HEADS = 4
A_BWD_HEADS = 4


def _neumann_inverse(a):
    c = a.shape[-1]
    eye = (lax.broadcasted_iota(jnp.int32, (c, c), 0) == lax.broadcasted_iota(jnp.int32, (c, c), 1)).astype(F32)
    tinv = eye + a
    p = a
    for _ in range(5):
        p = _mm(p, p)
        tinv = tinv + _mm(tinv, p)
    return tinv


@jax.custom_vjp
def _unit_inverse(a):
    return _neumann_inverse(a)


def _unit_inverse_fwd(a):
    tinv = _neumann_inverse(a)
    return tinv, tinv


def _unit_inverse_bwd(tinv, ct):
    return (_bdot(_bdot(tinv, ct, _TN), tinv, _NT),)


_unit_inverse.defvjp(_unit_inverse_fwd, _unit_inverse_bwd)


@jax.custom_vjp
def _known_inverse(a, tinv):
    return tinv


_known_inverse.defvjp(lambda a, tinv: (tinv, tinv),
                      lambda tinv, ct: (_unit_inverse_bwd(tinv, ct)[0], jnp.zeros_like(tinv)))


def _a_chain(st, cq, ck, cv, alpha, beta_raw, a_log, dt_b, incl, strict, last, kept=None):
    c = CHUNK
    gb = -jnp.exp(a_log) * _softplus(alpha + dt_b)
    bb = jax.nn.sigmoid(beta_raw)
    q = _l2(_silu(cq)) * (A_DIM ** -0.5)
    k = _l2(_silu(ck))
    v = _silu(cv)

    gc = _dot(incl, jnp.broadcast_to(gb, (c, LANE)))
    tot = jnp.sum(gc * last, axis=0, keepdims=True)
    m1 = gc[:, :c]
    decay = incl * jnp.exp(incl * (m1 - m1.T))
    kb = k * bb
    vb = v * bb
    a = -(strict * decay * _mm_nt(kb, k))
    tinv = _unit_inverse(a) if kept is None else _known_inverse(a, kept)
    eg = jnp.exp(gc)
    u = _mm(tinv, vb)
    w = _mm(tinv, kb * eg)
    qk = _mm_nt(q, k) * decay
    v_new = u - _mm(w, st)
    o = _mm(q * eg, st) + _mm(qk, v_new)
    st_new = st * jnp.exp(tot) + _mm_tn(k * jnp.exp(tot - gc), v_new)
    return st_new, o, tinv


def _a_step(sts, cq, ck, cv, gts, pa, h0, kept=None):
    c = CHUNK
    lane = lax.broadcasted_iota(jnp.int32, (1, LANE), 1)
    ii = lax.broadcasted_iota(jnp.int32, (c, c), 0)
    jj = lax.broadcasted_iota(jnp.int32, (c, c), 1)
    row = lax.broadcasted_iota(jnp.int32, (c, 1), 0)

    def pick(t, col):
        return jnp.sum(jnp.where(lane == col, t, 0.0), axis=1, keepdims=True)

    alpha, beta_raw, a_log, dt_b, incl, strict, last = [], [], [], [], [], [], []
    for b in range(sts.shape[0]):
        h, rev = h0 + b // 2, b % 2
        alpha.append(pick(gts[b], h + 8 * rev))
        beta_raw.append(pick(gts[b], h + 16 + 8 * rev))
        a_log.append(pick(pa[rev:rev + 1, :], h))
        dt_b.append(pick(pa[2 + rev:3 + rev, :], h))
        incl.append(((ii <= jj) if rev else (ii >= jj)).astype(F32))
        strict.append(((ii < jj) if rev else (ii > jj)).astype(F32))
        last.append((row == (0 if rev else c - 1)).astype(F32))
    stack = lambda ts: jnp.concatenate([t[None] for t in ts], axis=0)
    return jax.vmap(_a_chain)(sts, cq, ck, cv, stack(alpha), stack(beta_raw), stack(a_log), stack(dt_b),
                              stack(incl), stack(strict), stack(last), kept)


def _a_final(o, za, pa):
    outs = []
    for j in range(o.shape[1] // A_DIM):
        ln = slice(j * A_DIM, (j + 1) * A_DIM)
        outs.append(_rms(o[:, ln], pa[4:5, :]) * _silu(za[:, ln]))
    return jnp.concatenate(outs, axis=1)


def _a_tiles(n, nchunk, heads):
    tiles = []
    for b in range(2 * heads):
        i = (nchunk - 1 - n) if b % 2 else n
        tiles.append((i, pl.ds(pl.multiple_of(i * CHUNK, CHUNK), CHUNK), slice((b // 2) * A_DIM, (b // 2 + 1) * A_DIM)))
    return tiles


def _a_load(tiles, c_ref, gt_ref):
    cq, ck, cv = (jnp.stack([c_ref[r, sl, ln] for _, sl, ln in tiles], axis=0) for r in range(3))
    return cq, ck, cv, jnp.stack([gt_ref[sl, :] for _, sl, _ in tiles], axis=0)


def _loop_by_two(n, step, init):
    assert n % 2 == 0
    return lax.fori_loop(0, n // 2, lambda m, carry: step(2 * m + 1, step(2 * m, carry, 0), 1), init)


def _a_scan(h0, heads, nchunk, c_ref, gt_ref, pa, of_ref, ob_ref, s_ref, t_ref):
    def step(n, sts, parity):
        tiles = _a_tiles(n, nchunk, heads)
        sts_new, o, tinv = _a_step(sts, *_a_load(tiles, c_ref, gt_ref), pa, h0)
        for b, (i, sl, ln) in enumerate(tiles):
            s_ref[b, i] = sts[b]
            t_ref[b, i] = tinv[b]
            (ob_ref if b % 2 else of_ref)[sl, ln] = o[b]
        return sts_new

    _loop_by_two(nchunk, step, jnp.zeros((2 * heads, A_DIM, A_DIM), F32))


def _a_specs(s, heads):
    wide = heads * A_DIM
    once = pl.Buffered(1)
    trio = pl.BlockSpec((3, s, wide), lambda g: (0, 0, g), pipeline_mode=once)
    gates = pl.BlockSpec((s, LANE), lambda g: (0, P_GT // LANE))
    small = pl.BlockSpec((8, LANE), lambda g: (0, 0))

    def cols(base):
        return pl.BlockSpec((s, wide), lambda g: (0, base // wide + g), pipeline_mode=once)

    state = pl.BlockSpec((2 * heads, s // CHUNK, A_DIM, A_DIM), lambda g: (g, 0, 0, 0), pipeline_mode=once)
    kept = pl.BlockSpec((2 * heads, s // CHUNK, CHUNK, CHUNK), lambda g: (g, 0, 0, 0), pipeline_mode=once)
    return wide, trio, gates, small, cols, state, kept


def _delta_fwd(cqkv, proj, pa, ride=None):
    s = cqkv.shape[1]
    nchunk = s // CHUNK
    heads = A_FWD_HEADS
    steps = A_HEADS // heads
    wide, trio, gates, small, cols, state, kept = _a_specs(s, heads)
    n_in = len(ride.operands) if ride else 0
    n_out = len(ride.out_shapes) if ride else 0

    def body(*refs):
        c_ref, gt_ref, za_ref, pa_ref = refs[:4]
        out_ref, o_ref, s_ref, t_ref = refs[4 + n_in:8 + n_in]
        ob_ref = refs[8 + n_in + n_out]
        riders = (refs[4:4 + n_in], refs[8 + n_in:8 + n_in + n_out], refs[9 + n_in + n_out:])
        g = pl.program_id(0)
        if ride:
            pl.when(g == 0)(lambda: ride.start(*riders))
            pl.when(g == steps - 1)(lambda: ride.middle(*riders))
        h0 = g * heads
        pa_v = pa_ref[...]
        _a_scan(h0, heads, nchunk, c_ref, gt_ref, pa_v, o_ref, ob_ref, s_ref, t_ref)
        o_ref[...] += ob_ref[...]
        out_ref[...] = _a_final(o_ref[...], za_ref[...], pa_v).astype(BF16)
        if ride:
            pl.when(g == steps - 1)(lambda: ride.finish(*riders))

    assert steps > 1
    return pl.pallas_call(
        body, name="delta_fwd", grid=(steps,),
        in_specs=[trio, gates, cols(P_ZA), small] + [HBM] * n_in,
        out_specs=[cols(0), cols(0), state, kept] + [HBM] * n_out,
        out_shape=[jax.ShapeDtypeStruct((s, D_MODEL), BF16),
                   jax.ShapeDtypeStruct((s, A_WIDTH), F32),
                   jax.ShapeDtypeStruct((2 * A_HEADS, nchunk, A_DIM, A_DIM), F32),
                   jax.ShapeDtypeStruct((2 * A_HEADS, nchunk, CHUNK, CHUNK), F32)]
        + (list(ride.out_shapes) if ride else []),
        scratch_shapes=[pltpu.VMEM((s, wide), F32)] + (list(ride.scratch_shapes) if ride else []),
        compiler_params=_params(("arbitrary",)),
    )(cqkv, proj, proj, pa, *(ride.operands if ride else []))


def _delta_out_bwd(o_sum, proj, pa, d_mixed, tr=256):
    s = o_sum.shape[0]

    def body(o_ref, za_ref, pa_ref, dm_ref, do_ref, dza_ref, dpa_ref):
        @pl.when(pl.program_id(0) == 0)
        def _():
            dpa_ref[...] = jnp.zeros_like(dpa_ref)

        _, vjp = jax.vjp(_a_final, o_ref[...], za_ref[...], pa_ref[...])
        d_o, d_za, dpa = vjp(dm_ref[...].astype(F32))
        do_ref[...] = d_o
        dza_ref[...] = d_za.astype(BF16)
        dpa_ref[...] += dpa

    def rows(col):
        return pl.BlockSpec((tr, A_WIDTH), lambda i: (i, col))

    small = pl.BlockSpec((8, LANE), lambda i: (0, 0))
    return pl.pallas_call(
        body, name="delta_out_bwd", grid=(s // tr,), in_specs=[rows(0), rows(P_ZA // A_WIDTH), small, rows(0)],
        out_specs=[rows(0), rows(0), small],
        out_shape=[jax.ShapeDtypeStruct((s, A_WIDTH), F32), jax.ShapeDtypeStruct((s, A_WIDTH), BF16),
                   jax.ShapeDtypeStruct((8, LANE), F32)],
        compiler_params=_params(("arbitrary",)),
    )(o_sum, proj, pa, d_mixed)


def _delta_bwd(cqkv, proj, pa, d_o, states, inverses, ride=None):
    s = cqkv.shape[1]
    nchunk = s // CHUNK
    heads = A_BWD_HEADS
    steps = A_HEADS // heads
    wide, trio, gates, small, cols, _, _ = _a_specs(s, heads)
    n_in = len(ride.operands) if ride else 0
    n_out = len(ride.out_shapes) if ride else 0

    def body(*refs):
        c_ref, gt_ref, pa_ref, do_ref, s_hbm, t_hbm = refs[:6]
        dc_ref, dgt_ref, dpa_ref = refs[6 + n_in:9 + n_in]
        s_buf, t_buf, s_sems = refs[9 + n_in + n_out:12 + n_in + n_out]
        riders = (refs[6:6 + n_in], refs[9 + n_in:9 + n_in + n_out], refs[12 + n_in + n_out:])
        if ride:
            pl.when(pl.program_id(0) == 0)(lambda: ride.start(*riders))
        h0 = pl.program_id(0) * heads
        pa_v = pa_ref[...]

        @pl.when(h0 == 0)
        def _():
            dgt_ref[...] = jnp.zeros_like(dgt_ref)
            dpa_ref[...] = jnp.zeros_like(dpa_ref)

        dc_ref[...] = jnp.zeros_like(dc_ref)

        def state_copies(n, slot):
            tiles = _a_tiles(nchunk - 1 - n, nchunk, heads)
            return ([pltpu.make_async_copy(s_hbm.at[2 * h0 + b, i], s_buf.at[slot, b], s_sems.at[0, slot, b])
                     for b, (i, _, _) in enumerate(tiles)]
                    + [pltpu.make_async_copy(t_hbm.at[2 * h0 + b, i], t_buf.at[slot, b], s_sems.at[1, slot, b])
                       for b, (i, _, _) in enumerate(tiles)])

        for cp in state_copies(0, 0):
            cp.start()

        def step(n, carry, parity):
            d_sts, dpa = carry
            tiles = _a_tiles(nchunk - 1 - n, nchunk, heads)
            for cp in state_copies(n, parity):
                cp.wait()

            @pl.when(n + 1 < nchunk)
            def _():
                for cp in state_copies(n + 1, 1 - parity):
                    cp.start()

            sts, kept = s_buf[parity], t_buf[parity]
            d_o_t = jnp.stack([do_ref[sl, ln] for _, sl, ln in tiles], axis=0)
            _, vjp_c = jax.vjp(lambda *a: _a_step(*a, h0, kept)[:2], sts, *_a_load(tiles, c_ref, gt_ref), pa_v)
            d_prev, dcq, dck, dcv, dgts, dpa_i = vjp_c((d_sts, d_o_t))
            for b, (_, sl, ln) in enumerate(tiles):
                for r, dc in enumerate((dcq, dck, dcv)):
                    dc_ref[r, sl, ln] += dc[b]
                dgt_ref[sl, :] += dgts[b]
            return d_prev, dpa + dpa_i

        init = (jnp.zeros((2 * heads, A_DIM, A_DIM), F32), jnp.zeros((8, LANE), F32))
        _, dpa_out = lax.fori_loop(0, nchunk, lambda n, carry: step(n, carry, n % 2), init)
        dpa_ref[...] += dpa_out
        if ride:
            pl.when(pl.program_id(0) == steps - 1)(lambda: ride.finish(*riders))

    fixed = pl.BlockSpec((s, LANE), lambda g: (0, 0))
    return pl.pallas_call(
        body, name="delta_bwd", grid=(steps,),
        in_specs=[trio, gates, small, cols(0), pl.BlockSpec(memory_space=pl.ANY), pl.BlockSpec(memory_space=pl.ANY)]
        + [HBM] * n_in,
        out_specs=[trio, fixed, small] + [HBM] * n_out,
        out_shape=[jax.ShapeDtypeStruct((3, s, A_WIDTH), F32), jax.ShapeDtypeStruct((s, LANE), F32),
                   jax.ShapeDtypeStruct((8, LANE), F32)] + (list(ride.out_shapes) if ride else []),
        scratch_shapes=[pltpu.VMEM((2, 2 * heads, A_DIM, A_DIM), F32), pltpu.VMEM((2, 2 * heads, CHUNK, CHUNK), F32),
                        pltpu.SemaphoreType.DMA((2, 2, 2 * heads))]
        + (list(ride.scratch_shapes) if ride else []),
        compiler_params=_params(("arbitrary",)),
    )(cqkv, proj, pa, d_o, states, inverses, *(ride.operands if ride else []))


def _rope_tables(s):
    inv = ROPE_THETA ** (-jnp.arange(0, B_DIM, 2, dtype=F32) / B_DIM)
    ang = jnp.arange(s, dtype=F32)[:, None] * inv[None, :]
    cos, sin = jnp.cos(ang), jnp.sin(ang)
    return jnp.concatenate([cos, cos], axis=1), jnp.concatenate([-sin, sin], axis=1)


def _b_block(q_t, z_t, k3, v3, cos_q, sin_q, cos_k, sin_k, pb, n, nb):
    w = WINDOW
    def swap(t):
        return jnp.concatenate([t[:, B_DIM // 2:], t[:, :B_DIM // 2]], axis=1)

    grp = B_HEADS // B_KV
    qi = lax.broadcasted_iota(jnp.int32, (grp * w, 3 * w), 0) & (w - 1)
    kj = lax.broadcasted_iota(jnp.int32, (grp * w, 3 * w), 1)
    kpos = kj + (n - 1) * w
    mask = (jnp.abs(kj - w - qi) <= w) & (kpos >= 0) & (kpos < nb * w)
    lane = lax.broadcasted_iota(jnp.int32, (1, LANE), 1)
    qn, kn = pb[0:1, :B_DIM], pb[1:2, :B_DIM]
    cos_g = jnp.concatenate([cos_q] * grp, axis=0)
    sin_g = jnp.concatenate([sin_q] * grp, axis=0)
    def group(q, k, v, sink):
        k = _rms(k, kn)
        k = k * cos_k + swap(k) * sin_k
        q = _rms(q, qn)
        q = q * cos_g + swap(q) * sin_g
        s = _mm_nt(q, k) * (B_DIM ** -0.5)
        s = jnp.where(mask, s, -jnp.inf)
        m = jnp.maximum(jnp.max(s, axis=1, keepdims=True), sink)
        p = jnp.exp(s - m)
        p = p / (jnp.sum(p, axis=1, keepdims=True) + jnp.exp(sink - m))
        return _mm(p, v)

    stack = lambda ts: jnp.concatenate([t[None] for t in ts], axis=0)
    qs, ks, vs, sinks = [], [], [], []
    for hk in range(B_KV):
        heads = [hk * grp + g for g in range(grp)]
        ks.append(k3[:, hk * B_DIM:(hk + 1) * B_DIM])
        vs.append(v3[:, hk * B_DIM:(hk + 1) * B_DIM])
        qs.append(jnp.concatenate([q_t[:, hq * B_DIM:(hq + 1) * B_DIM] for hq in heads], axis=0))
        sinks.append(jnp.concatenate(
            [jnp.broadcast_to(jnp.sum(jnp.where(lane == hq, pb[2:3, :], 0.0), axis=1, keepdims=True), (w, 1))
             for hq in heads], axis=0))
    o = jax.vmap(group)(stack(qs), stack(ks), stack(vs), stack(sinks))
    outs = [o[hk, g * w:(g + 1) * w, :] for hk in range(B_KV) for g in range(grp)]
    return jnp.concatenate(outs, axis=1) * _silu(z_t)


def _b_specs(s):
    nb = s // WINDOW
    qsp = pl.BlockSpec((WINDOW, 512), lambda n: (n, P_QB // 512))
    zsp = pl.BlockSpec((WINDOW, 512), lambda n: (n, P_ZB // 512))

    def three(col, width):
        return [pl.BlockSpec((WINDOW, width), lambda n: (jnp.maximum(n - 1, 0), col)),
                pl.BlockSpec((WINDOW, width), lambda n: (n, col)),
                pl.BlockSpec((WINDOW, width), lambda n: (jnp.minimum(n + 1, nb - 1), col))]

    tab = pl.BlockSpec((WINDOW, B_DIM), lambda n: (n, 0))
    small = pl.BlockSpec((8, LANE), lambda n: (0, 0))
    specs = [qsp, zsp] + three(P_KB // LANE, LANE) + three(P_VB // LANE, LANE) + [tab, tab] + three(0, B_DIM) + three(0, B_DIM) + [small]
    return nb, specs


def _b_args(proj, cos2, sin2, pb):
    return (proj, proj, proj, proj, proj, proj, proj, proj, cos2, sin2, cos2, cos2, cos2, sin2, sin2, sin2, pb)


def _b_load(refs):
    (q_ref, z_ref, kp, kc, kx, vp, vc, vx, cq, sq, ckp, ckc, ckx, skp, skc, skx, pb_ref) = refs
    cat = lambda *r: jnp.concatenate([t[...] for t in r], axis=0)
    return (q_ref[...], z_ref[...], cat(kp, kc, kx), cat(vp, vc, vx), cq[...], sq[...], cat(ckp, ckc, ckx),
            cat(skp, skc, skx), pb_ref[...])


def _attn_b_fwd(proj, cos2, sin2, pb, mixed):
    s = proj.shape[0]
    nb, specs = _b_specs(s)

    def body(*refs):
        o_ref = refs[-1]
        args = _b_load(refs[:-2])
        o_ref[...] = _b_block(*args, pl.program_id(0), nb).astype(BF16)

    return pl.pallas_call(
        body, name="attn_b_fwd", grid=(nb,), in_specs=specs + [pl.BlockSpec(memory_space=pl.ANY)],
        out_specs=pl.BlockSpec((WINDOW, 512), lambda n: (n, A_WIDTH // 512)),
        out_shape=jax.ShapeDtypeStruct(mixed.shape, mixed.dtype), input_output_aliases={len(specs): 0},
        compiler_params=_params(("parallel",)),
    )(*_b_args(proj, cos2, sin2, pb), mixed)


def _attn_b_bwd(proj, cos2, sin2, pb, d_mixed):
    s = proj.shape[0]
    nb, specs = _b_specs(s)
    w = WINDOW

    def body(*refs):
        dm_ref, dq_ref, dz_ref, dk_ref, dv_ref, dpb_ref = refs[-6:]
        n = pl.program_id(0)
        q_t, z_t, k3, v3, cq, sq, ck, sk, pb_v = _b_load(refs[:-6])

        @pl.when(n == 0)
        def _():
            dk_ref[...] = jnp.zeros_like(dk_ref)
            dv_ref[...] = jnp.zeros_like(dv_ref)
            dpb_ref[...] = jnp.zeros_like(dpb_ref)

        def f(q_, z_, k_, v_, pb_):
            return _b_block(q_, z_, k_, v_, cq, sq, ck, sk, pb_, n, nb)

        _, vjp = jax.vjp(f, q_t, z_t, k3, v3, pb_v)
        dq, dz, dk3, dv3, dpb = vjp(dm_ref[...])
        dq_ref[...] = dq.astype(BF16)
        dz_ref[...] = dz.astype(BF16)
        dpb_ref[...] += dpb

        def add(j, cond):
            @pl.when(cond)
            def _():
                rows = pl.ds(pl.multiple_of((n - 1 + j) * w, w), w)
                dk_ref[rows, :] += dk3[j * w:(j + 1) * w, :]
                dv_ref[rows, :] += dv3[j * w:(j + 1) * w, :]

        add(0, n > 0)
        add(1, n >= 0)
        add(2, n < nb - 1)

    blk = pl.BlockSpec((w, 512), lambda n: (n, 0))
    whole = pl.BlockSpec((s, LANE), lambda n: (0, 0))
    small = pl.BlockSpec((8, LANE), lambda n: (0, 0))
    return pl.pallas_call(
        body, name="attn_b_bwd", grid=(nb,),
        in_specs=specs + [pl.BlockSpec((w, 512), lambda n: (n, 2))],
        out_specs=[blk, blk, whole, whole, small],
        out_shape=[jax.ShapeDtypeStruct((s, 512), BF16), jax.ShapeDtypeStruct((s, 512), BF16),
                   jax.ShapeDtypeStruct((s, LANE), F32), jax.ShapeDtypeStruct((s, LANE), F32),
                   jax.ShapeDtypeStruct((8, LANE), F32)],
        compiler_params=_params(("arbitrary",)),
    )(*_b_args(proj, cos2, sin2, pb), d_mixed)


def _mem_kv_fwd(mem, mem_norm_w, w_kv):
    def body(mem_ref, nw_ref, w_ref, kv_ref):
        mn = _rms(mem_ref[...], nw_ref[...]).astype(BF16)
        kv_ref[...] = jnp.dot(mn, w_ref[...], preferred_element_type=F32)

    return pl.pallas_call(
        body, name="mem_kv_fwd", out_shape=jax.ShapeDtypeStruct((MEM_LEN, 2 * C_HEADS * C_DIM), F32),
        compiler_params=_params(),
    )(mem, mem_norm_w, w_kv)


def _mem_kv_bwd(mem, mem_norm_w, w_kv, d_kv):
    def body(mem_ref, nw_ref, w_ref, g_ref, gw_ref, gn_ref):
        mn, vjp = jax.vjp(_rms, mem_ref[...], nw_ref[...])
        g = g_ref[...].astype(BF16)
        gw_ref[...] = lax.dot_general(mn.astype(BF16), g, (((0,), (0,)), ((), ())), preferred_element_type=F32)
        d_mn = lax.dot_general(g, w_ref[...], (((1,), (1,)), ((), ())), preferred_element_type=F32)
        gn_ref[...] = vjp(d_mn)[1]

    return pl.pallas_call(
        body, name="mem_kv_bwd",
        out_shape=[jax.ShapeDtypeStruct((D_MODEL, 2 * C_HEADS * C_DIM), F32), jax.ShapeDtypeStruct((1, D_MODEL), F32)],
        compiler_params=_params(),
    )(mem, mem_norm_w, w_kv, d_kv)


def _c_tile(q_t, z_t, kvm, pc):
    width = C_HEADS * C_DIM
    outs = []
    for h in range(C_HEADS):
        q = _rms(q_t[:, h * C_DIM:(h + 1) * C_DIM], pc[0:1, :])
        k = _rms(kvm[:, h * C_DIM:(h + 1) * C_DIM], pc[1:2, :])
        v = kvm[:, width + h * C_DIM:width + (h + 1) * C_DIM]
        s = _mm_nt(q, k) * (C_DIM ** -0.5)
        p = jnp.exp(s - jnp.max(s, axis=1, keepdims=True))
        p = p / jnp.sum(p, axis=1, keepdims=True)
        outs.append(_mm(p, v))
    return jnp.concatenate(outs, axis=1) * _silu(z_t)


def _attn_c_fwd(proj, kvm, pc, mixed, tq=256):
    s = proj.shape[0]

    def body(q_ref, z_ref, kv_ref, pc_ref, mixed_ref, o_ref):
        o_ref[...] = _c_tile(q_ref[...], z_ref[...], kv_ref[...], pc_ref[...]).astype(BF16)

    return pl.pallas_call(
        body, name="attn_c_fwd", grid=(s // tq,),
        in_specs=[pl.BlockSpec((tq, 512), lambda i: (i, P_QC // 512)), pl.BlockSpec((tq, 512), lambda i: (i, P_ZC // 512)),
                  pl.BlockSpec(kvm.shape, lambda i: (0, 0)), pl.BlockSpec((8, LANE), lambda i: (0, 0)),
                  pl.BlockSpec(memory_space=pl.ANY)],
        out_specs=pl.BlockSpec((tq, 512), lambda i: (i, (A_WIDTH + 512) // 512)),
        out_shape=jax.ShapeDtypeStruct(mixed.shape, mixed.dtype), input_output_aliases={4: 0},
        compiler_params=_params(("parallel",)),
    )(proj, proj, kvm, pc, mixed)


def _attn_c_bwd(proj, kvm, pc, d_mixed, tq=256):
    s = proj.shape[0]

    def body(q_ref, z_ref, kv_ref, pc_ref, dm_ref, dq_ref, dz_ref, dkv_ref, dpc_ref):
        @pl.when(pl.program_id(0) == 0)
        def _():
            dkv_ref[...] = jnp.zeros_like(dkv_ref)
            dpc_ref[...] = jnp.zeros_like(dpc_ref)

        _, vjp = jax.vjp(_c_tile, q_ref[...], z_ref[...], kv_ref[...], pc_ref[...])
        dq, dz, dkv, dpc = vjp(dm_ref[...])
        dq_ref[...] = dq.astype(BF16)
        dz_ref[...] = dz.astype(BF16)
        dkv_ref[...] += dkv
        dpc_ref[...] += dpc

    blk = pl.BlockSpec((tq, 512), lambda i: (i, 0))
    kvs = pl.BlockSpec(kvm.shape, lambda i: (0, 0))
    small = pl.BlockSpec((8, LANE), lambda i: (0, 0))
    return pl.pallas_call(
        body, name="attn_c_bwd", grid=(s // tq,),
        in_specs=[pl.BlockSpec((tq, 512), lambda i: (i, P_QC // 512)), pl.BlockSpec((tq, 512), lambda i: (i, P_ZC // 512)),
                  kvs, small, pl.BlockSpec((tq, 512), lambda i: (i, 3))],
        out_specs=[blk, blk, kvs, small],
        out_shape=[jax.ShapeDtypeStruct((s, 512), BF16), jax.ShapeDtypeStruct((s, 512), BF16),
                   jax.ShapeDtypeStruct(kvm.shape, F32), jax.ShapeDtypeStruct((8, LANE), F32)],
        compiler_params=_params(("arbitrary",)),
    )(proj, proj, kvm, pc, d_mixed)


def _pad_row(v, width=LANE):
    v = v.reshape(1, -1)
    return jnp.pad(v, ((0, 0), (0, width - v.shape[1])))


def _local_step(x, mem, target, norm_w, w_perm_t, w_blocks_t, conv_w, pa, pb, pc, mem_norm_w, w_kv, w_out, gather=None,
                exchange=None):
    s = x.shape[0]
    cos2, sin2 = _rope_tables(s)
    hn = _rms_fwd(x, norm_w)
    wide = dict(tm=1024, tn=512, tk=2048)
    proj = _matmul(hn, w_perm_t, "nt", F32, "mm_proj", **wide)
    cqkv = _conv_fwd(proj, conv_w)
    if gather is None:
        mixed, o_sum, states, inverses = _delta_fwd(cqkv, proj, pa)
    else:
        mixed, o_sum, states, inverses, *arrived = _delta_fwd(cqkv, proj, pa, gather[0])
        w_out, w_kv = gather[1](*arrived)
    mixed = _attn_b_fwd(proj, cos2, sin2, pb, mixed)
    kvm = _mem_kv_fwd(mem, mem_norm_w, w_kv)
    mixed = _attn_c_fwd(proj, kvm, pc, mixed)
    mo = _matmul(mixed, w_out, "nn", F32, "mm_out", **wide)
    dy, dyb, loss_parts = _loss_dy(x, mo, target)

    d_mixed = _matmul(dyb, w_out, "nt", F32, "mm_dmixed", **wide)
    g_w_out = _matmul(mixed, dyb, "tn", F32, "mm_gwout", **wide)
    d_qc, d_zc, d_kvm, d_pc = _attn_c_bwd(proj, kvm, pc, d_mixed)
    g_w_kv, g_mem_norm = _mem_kv_bwd(mem, mem_norm_w, w_kv, d_kvm)
    d_qb, d_zb, d_kb, d_vb, d_pb = _attn_b_bwd(proj, cos2, sin2, pb, d_mixed)
    d_o, d_za, d_pa_out = _delta_out_bwd(o_sum, proj, pa, d_mixed)
    early = exchange[0](g_w_out, g_w_kv) if exchange else None
    d_c, d_gt, d_pa_scan, *landed_early = _delta_bwd(cqkv, proj, pa, d_o, states, inverses, early)
    d_pa = d_pa_out + d_pa_scan
    d_qkv, g_conv = _conv_bwd(proj, conv_w, d_c)
    d_proj = _cotangent_blocks(d_qkv, d_za, d_gt, d_qb, d_kb, d_vb, d_zb, d_qc, d_zc)
    g_w_blocks_t = _matmul(d_proj, hn, "tn", F32, "mm_gwin", tm=512, tn=2048, tk=2048)
    late = exchange[1](g_w_blocks_t) if exchange else None
    g_x, g_norm, *landed_late = _input_grad(d_proj, w_blocks_t, x, norm_w, dy, late)
    return dict(loss_parts=loss_parts, g_x=g_x, g_norm=g_norm, g_w_blocks_t=g_w_blocks_t, g_conv=g_conv, d_pa=d_pa,
                d_pb=d_pb, d_pc=d_pc, g_mem_norm=g_mem_norm, g_w_kv=g_w_kv, g_w_out=g_w_out,
                landed=landed_late + landed_early)


_SEGMENTS = ((0, O_GT, 0), (O_GT, O_QB, P_GT), (O_QB, O_KB, P_QB), (O_KB, O_VB, P_KB), (O_VB, O_ZB, P_VB),
             (O_ZB, O_QC, P_ZB), (O_QC, O_ZC, P_QC), (O_ZC, IN_WIDTH, P_ZC))


def _permute_blocks(w4):
    parts = []
    for first, end, _ in sorted(_SEGMENTS, key=lambda seg: seg[2]):
        row = first
        while row < end:
            k = row // W_IN_BLOCK
            stop = min(end, (k + 1) * W_IN_BLOCK)
            parts.append(w4[k][row - k * W_IN_BLOCK:stop - k * W_IN_BLOCK, :])
            row = stop
    parts.append(jnp.zeros((P_WIDTH - IN_WIDTH, w4.shape[2]), w4.dtype))
    return jnp.concatenate(parts, axis=0)


def _cotangent_blocks(d_qkv, d_za, d_gt, d_qb, d_kb, d_vb, d_zb, d_qc, d_zc):
    s = d_qkv.shape[0]
    tr = min(256, s)
    pieces = (d_qkv, d_za, d_gt, d_qb, d_kb, d_vb, d_zb, d_qc, d_zc)

    def body(*refs):
        o_ref = refs[-1]
        tiles = [r[...].astype(BF16) for r in refs[:-1]]
        tiles[2] = tiles[2][:, :O_QB - O_GT]
        orig = jnp.concatenate(tiles, axis=1)
        pad = jnp.zeros((tr, W_IN_PAD - W_IN_BLOCK), BF16)
        parts = []
        for k in range(N_CHIPS):
            parts += [orig[:, k * W_IN_BLOCK:(k + 1) * W_IN_BLOCK], pad]
        o_ref[...] = jnp.concatenate(parts, axis=1)

    return pl.pallas_call(
        body, name="cotangent_blocks", grid=(s // tr,),
        in_specs=[pl.BlockSpec((tr, p.shape[1]), lambda i: (i, 0)) for p in pieces],
        out_specs=pl.BlockSpec((tr, N_CHIPS * W_IN_PAD), lambda i: (i, 0)),
        out_shape=jax.ShapeDtypeStruct((s, N_CHIPS * W_IN_PAD), BF16), compiler_params=_params(("parallel",)),
    )(*pieces)


HBM = pl.BlockSpec(memory_space=pltpu.HBM)


def _place():
    x, y, c = lax.axis_index("x"), lax.axis_index("y"), lax.axis_index("c")
    chips = [(1 - x, y), (x, 1 - y), (1 - x, 1 - y)]
    return x, y, c, 2 * x + y, chips, [2 * cx + cy for cx, cy in chips]


PIECE_ROWS_CAP = 600


def _remote(src, dst, send_sems, recv_sems, k, to):
    return pltpu.make_async_remote_copy(src_ref=src, dst_ref=dst, send_sem=send_sems.at[k], recv_sem=recv_sems.at[k],
                                        device_id=to, device_id_type=MESH)


def _half_cols(ref, c):
    half = ref.shape[-1] // 2
    return pl.ds(pl.multiple_of(c * half, LANE), half)


class _PairedGather:
    def __init__(self, blocks):
        n = len(blocks)
        self.operands = list(blocks)
        self.out_shapes = [jax.ShapeDtypeStruct((N_CHIPS,) + b.shape, b.dtype) for b in blocks]
        self.scratch_shapes = [pltpu.SemaphoreType.DMA((6 * n,)), pltpu.SemaphoreType.DMA((6 * n,))]

    @staticmethod
    def _copies(srcs, dsts, sems):
        x, y, c, me, chips, chip_ids = _place()
        sends, landed, passes, passed = [], [], [], []
        for a, (src, dst) in enumerate(zip(srcs, dsts)):
            mine, other = _half_cols(src, c), _half_cols(src, 1 - c)
            for j, (chip, cid) in enumerate(zip(chips, chip_ids)):
                sends.append(_remote(src.at[:, mine], dst.at[me, :, mine], sems[0], sems[1], 6 * a + j, (*chip, c)))
                here = dst.at[cid, :, mine]
                landed.append(_remote(here, here, sems[0], sems[1], 6 * a + j, (x, y, 1 - c)))
                passes.append(_remote(here, here, sems[0], sems[1], 6 * a + 3 + j, (x, y, 1 - c)))
                there = dst.at[cid, :, other]
                passed.append(_remote(there, there, sems[0], sems[1], 6 * a + 3 + j, (x, y, 1 - c)))
        return sends, landed, passes, passed

    def start(self, srcs, dsts, sems):
        for cp in self._copies(srcs, dsts, sems)[0]:
            cp.start()

    def middle(self, srcs, dsts, sems):
        _, landed, passes, _ = self._copies(srcs, dsts, sems)
        for arrived, onward in zip(landed, passes):
            arrived.wait_recv()
            onward.start()

    def finish(self, srcs, dsts, sems):
        sends, _, passes, passed = self._copies(srcs, dsts, sems)
        for cp in passed:
            cp.wait_recv()
        for cp in sends + passes:
            cp.wait_send()


def _all_gather_weights(bigs, conv_b):
    bigs = tuple(bigs)
    n_big = len(bigs)

    def body(*refs):
        srcs, conv_src = refs[:n_big], refs[n_big]
        dsts, conv_dst = refs[n_big + 1:2 * n_big + 1], refs[2 * n_big + 1]
        send_sems, recv_sems, local_sems = refs[2 * n_big + 2:]
        x, y, c, me, chips, chip_ids = _place()
        sibling = (x, y, 1 - c)
        local = [pltpu.make_async_copy(src, dst.at[me], local_sems.at[a]) for a, (src, dst) in enumerate(zip(srcs, dsts))]
        local.append(pltpu.make_async_copy(conv_src, conv_dst.at[me], local_sems.at[n_big]))
        for cp in local:
            cp.start()
        sends = []
        for a, (src, dst) in enumerate(zip(srcs, dsts)):
            mine = _half_cols(src, c)
            for j, chip in enumerate(chips):
                sends.append(_remote(src.at[:, mine], dst.at[me, :, mine], send_sems, recv_sems, 6 * a + j, (*chip, c)))
        for j, chip in enumerate(chips):
            sends.append(_remote(conv_src, conv_dst.at[me], send_sems, recv_sems, 6 * n_big + j, (*chip, c)))
        for cp in sends:
            cp.start()
        passed = []
        for a, (src, dst) in enumerate(zip(srcs, dsts)):
            mine = _half_cols(src, c)
            for j, cid in enumerate(chip_ids):
                landed = dst.at[cid, :, mine]
                _remote(landed, landed, send_sems, recv_sems, 6 * a + j, sibling).wait_recv()
                cp = _remote(landed, landed, send_sems, recv_sems, 6 * a + 3 + j, sibling)
                cp.start()
                passed.append(cp)
        for a, (src, dst) in enumerate(zip(srcs, dsts)):
            other = _half_cols(src, 1 - c)
            for j, cid in enumerate(chip_ids):
                landed = dst.at[cid, :, other]
                _remote(landed, landed, send_sems, recv_sems, 6 * a + 3 + j, sibling).wait_recv()
        for j, cid in enumerate(chip_ids):
            _remote(conv_src, conv_dst.at[cid], send_sems, recv_sems, 6 * n_big + j, sibling).wait_recv()
        for cp in sends + passed:
            cp.wait_send()
        for cp in local:
            cp.wait()

    n_sem = 6 * n_big + 3
    return pl.pallas_call(
        body, name="all_gather_weights",
        out_shape=[jax.ShapeDtypeStruct((N_CHIPS,) + w.shape, w.dtype) for w in bigs + (conv_b,)],
        in_specs=[pl.BlockSpec(memory_space=pltpu.VMEM)] * (n_big + 1), out_specs=[HBM] * (n_big + 1),
        scratch_shapes=[pltpu.SemaphoreType.DMA((n_sem,)), pltpu.SemaphoreType.DMA((n_sem,)),
                        pltpu.SemaphoreType.DMA((n_big + 1,))],
        compiler_params=_params(),
    )(*bigs, conv_b)


def _pair_exchange(grads, name):
    n = len(grads)
    pieces = [_row_tile(g.shape[1]) for g in grads]

    def body(*refs):
        srcs, gots = refs[:n], refs[n:2 * n]
        stages = refs[2 * n:3 * n]
        send_sems, recv_sems, load_sems = refs[3 * n:]
        x, y, c, _, _, _ = _place()
        sibling = (x, y, 1 - c)
        for a in range(n):
            slabs, rows, _ = gots[a].shape
            piece = pieces[a]
            per_slab = rows // piece
            theirs = _half_cols(srcs[a], 1 - c)
            loads, sends = [], []
            for i in range(slabs * per_slab):
                k, r, slot = i // per_slab, i % per_slab, i % 2
                part = pl.ds(r * piece, piece)
                loads.append(pltpu.make_async_copy(srcs[a].at[k, part, theirs], stages[a].at[slot], load_sems.at[2 * a + slot]))
                sends.append(pltpu.make_async_remote_copy(
                    src_ref=stages[a].at[slot], dst_ref=gots[a].at[k, part, :],
                    send_sem=send_sems.at[2 * a + slot], recv_sem=recv_sems.at[a], device_id=sibling, device_id_type=MESH))
            loads[0].start()
            for i in range(len(loads)):
                loads[i].wait()
                sends[i].start()
                if i + 1 < len(loads):
                    if i >= 1:
                        sends[i - 1].wait_send()
                    loads[i + 1].start()
            for cp in sends[-2:]:
                cp.wait_send()
        for a in range(n):
            whole = srcs[a].at[:, :, _half_cols(srcs[a], c)]
            pltpu.make_async_remote_copy(src_ref=whole, dst_ref=gots[a], send_sem=send_sems.at[2 * a],
                                         recv_sem=recv_sems.at[a], device_id=sibling, device_id_type=MESH).wait_recv()

    halves = [jax.ShapeDtypeStruct((g.shape[0], g.shape[1], g.shape[2] // 2), g.dtype) for g in grads]
    return pl.pallas_call(
        body, name=name, out_shape=halves, in_specs=[HBM] * n, out_specs=[HBM] * n,
        scratch_shapes=[pltpu.VMEM((2, piece, g.shape[2] // 2), g.dtype) for piece, g in zip(pieces, grads)]
        + [pltpu.SemaphoreType.DMA((2 * n,)), pltpu.SemaphoreType.DMA((n,)), pltpu.SemaphoreType.DMA((2 * n,))],
        compiler_params=_params(),
    )(*grads)


class _ChipExchange:
    def __init__(self, halves):
        n = len(halves)
        self.operands = list(halves)
        self.out_shapes = [jax.ShapeDtypeStruct((N_CHIPS - 1,) + h.shape[1:], h.dtype) for h in halves]
        self.scratch_shapes = [pltpu.SemaphoreType.DMA((3 * n,)), pltpu.SemaphoreType.DMA((3 * n,))]

    @staticmethod
    def _copies(srcs, lands, sems):
        _, _, c, _, chips, chip_ids = _place()
        return [_remote(src.at[cid], land.at[j], sems[0], sems[1], 3 * a + j, (*chip, c))
                for a, (src, land) in enumerate(zip(srcs, lands)) for j, (chip, cid) in enumerate(zip(chips, chip_ids))]

    def start(self, srcs, lands, sems):
        for cp in self._copies(srcs, lands, sems):
            cp.start()

    def finish(self, srcs, lands, sems):
        copies = self._copies(srcs, lands, sems)
        for cp in copies:
            cp.wait_recv()
        for cp in copies:
            cp.wait_send()


def _pair_gather(halves, rows):
    n = len(halves)

    def body(*refs):
        srcs, fulls = refs[:n], refs[n:2 * n]
        send_sems, recv_sems, local_sems = refs[2 * n:]
        x, y, c, _, _, _ = _place()
        copies = []
        for a in range(n):
            mine, src = _half_cols(fulls[a], c), srcs[a].at[pl.ds(0, rows[a]), :]
            keep = pltpu.make_async_copy(src, fulls[a].at[:, mine], local_sems.at[a])
            keep.start()
            give = _remote(src, fulls[a].at[:, mine], send_sems, recv_sems, a, (x, y, 1 - c))
            give.start()
            copies += [keep, give]
        for a in range(n):
            other, src = _half_cols(fulls[a], 1 - c), srcs[a].at[pl.ds(0, rows[a]), :]
            copies[2 * a].wait()
            copies[2 * a + 1].wait_send()
            _remote(src, fulls[a].at[:, other], send_sems, recv_sems, a, (x, y, 1 - c)).wait_recv()

    return pl.pallas_call(
        body, name="grad_pair_gather",
        out_shape=[jax.ShapeDtypeStruct((r, 2 * h.shape[1]), h.dtype) for r, h in zip(rows, halves)],
        in_specs=[pl.BlockSpec(memory_space=pltpu.VMEM)] * n, out_specs=[HBM] * n,
        scratch_shapes=[pltpu.SemaphoreType.DMA((n,)), pltpu.SemaphoreType.DMA((n,)), pltpu.SemaphoreType.DMA((n,))],
    )(*halves)


def _all_reduce_small(p):
    n_dev = 8

    def body(p_ref, o_ref, land, send_sems, recv_sems):
        x, y, c = lax.axis_index("x"), lax.axis_index("y"), lax.axis_index("c")
        me = 4 * x + 2 * y + c
        land[me] = p_ref[...]
        sends = []
        for k in range(1, n_dev):
            fx, fy, fc = (k >> 2) & 1, (k >> 1) & 1, k & 1
            to = (x ^ fx, y ^ fy, c ^ fc)
            cp = _remote(p_ref, land.at[me], send_sems, recv_sems, k - 1, to)
            cp.start()
            sends.append(cp)
        for k in range(1, n_dev):
            _remote(p_ref, land.at[me ^ k], send_sems, recv_sems, k - 1, (x, y, c)).wait_recv()
        total = land[0]
        for d in range(1, n_dev):
            total = total + land[d]
        o_ref[...] = total
        for cp in sends:
            cp.wait_send()

    vm = pl.BlockSpec(memory_space=pltpu.VMEM)
    return pl.pallas_call(
        body, name="all_reduce_small", out_shape=jax.ShapeDtypeStruct(p.shape, p.dtype), in_specs=[vm], out_specs=vm,
        scratch_shapes=[pltpu.VMEM((n_dev,) + p.shape, p.dtype), pltpu.SemaphoreType.DMA((n_dev - 1,)),
                        pltpu.SemaphoreType.DMA((n_dev - 1,))],
    )(p)


def _row_tile(rows):
    fits = [t for t in range(8, min(rows, PIECE_ROWS_CAP) + 1, 8) if rows % t == 0]
    return max(fits) if fits else rows


def _pair_sum(full, got, core, name):
    n, r, c = got.shape
    tr = _row_tile(r)

    def body(core_ref, a_ref, b_ref, o_ref):
        o_ref[...] = (a_ref[...] + b_ref[...]).astype(BF16)

    blk = pl.BlockSpec((None, tr, c), lambda i, j, core_ref: (i, j, 0))
    grid_spec = pltpu.PrefetchScalarGridSpec(
        num_scalar_prefetch=1, grid=(n, r // tr),
        in_specs=[pl.BlockSpec((None, tr, c), lambda i, j, core_ref: (i, j, core_ref[0])), blk], out_specs=blk)
    return pl.pallas_call(body, name=name, grid_spec=grid_spec, out_shape=jax.ShapeDtypeStruct(got.shape, BF16),
                          compiler_params=_params(("parallel", "parallel")))(core, full, got)


def _chip_sum(full, got, land, place, name):
    n, r, c = land.shape
    tr = _row_tile(r)

    def body(place_ref, a_ref, b_ref, l_ref, o_ref):
        total = a_ref[...] + b_ref[...]
        for j in range(n):
            total = total + l_ref[j].astype(F32)
        o_ref[...] = total

    grid_spec = pltpu.PrefetchScalarGridSpec(
        num_scalar_prefetch=1, grid=(r // tr,),
        in_specs=[pl.BlockSpec((None, tr, c), lambda i, p: (p[0], i, p[1])),
                  pl.BlockSpec((None, tr, c), lambda i, p: (p[0], i, 0)),
                  pl.BlockSpec((n, tr, c), lambda i, p: (0, i, 0))],
        out_specs=pl.BlockSpec((tr, c), lambda i, p: (i, 0)))
    return pl.pallas_call(body, name=name, grid_spec=grid_spec, out_shape=jax.ShapeDtypeStruct((r, c), F32),
                          compiler_params=_params(("parallel",)))(place, full, got, land)


def _adamw(w, g, m, v, name, echo=False):
    r, c = w.shape
    tr = _row_tile(r)
    tc = 1024 if c % 1024 == 0 else c

    def body(w_ref, g_ref, m_ref, v_ref, d_ref, mo_ref, vo_ref, *g_out):
        g_ = g_ref[...]
        for o in g_out:
            o[...] = g_
        m2 = ADAM_B1 * m_ref[...] + (1.0 - ADAM_B1) * g_
        v2 = ADAM_B2 * v_ref[...] + (1.0 - ADAM_B2) * jnp.square(g_)
        m_hat = m2 / (1.0 - ADAM_B1 ** ADAM_STEP)
        v_hat = v2 / (1.0 - ADAM_B2 ** ADAM_STEP)
        d_ref[...] = -ADAM_LR * (m_hat / (jnp.sqrt(v_hat) + ADAM_EPS) + ADAM_WD * w_ref[...])
        mo_ref[...] = m2
        vo_ref[...] = v2

    blk = pl.BlockSpec((tr, tc), lambda i, j: (i, j))
    n_out = 4 if echo else 3
    return pl.pallas_call(body, name=name, grid=(r // tr, c // tc), in_specs=[blk] * 4, out_specs=[blk] * n_out,
                          out_shape=[jax.ShapeDtypeStruct(w.shape, F32)] * n_out,
                          compiler_params=_params(("parallel", "parallel")))(w, g, m, v)


SMALL_NAMES = ("norm_w", "mem_norm_w", "o_norm_a", "q_norm_c", "k_norm_c", "q_norm_b", "k_norm_b",
               "a_log_fwd", "a_log_bwd", "dt_bias_fwd", "dt_bias_bwd", "sink_b")
SMALL_SIZES = (2048, 2048, 128, 128, 128, 64, 64, 8, 8, 8, 8, 8)
SMALL_LOSS = sum(SMALL_SIZES)
SMALL_CONV = 5120
SMALL_TOTAL = SMALL_CONV + CONV_K * 3 * A_WIDTH
SMALL_ROWS = SMALL_TOTAL // LANE


def _pack_small(parts, extra=None, conv=None):
    vec = [parts[n].reshape(-1) for n in SMALL_NAMES]
    vec.append(jnp.zeros((1,), F32) if extra is None else extra.reshape(1))
    vec.append(jnp.zeros((SMALL_CONV - SMALL_LOSS - 1,), F32))
    vec.append(jnp.zeros((SMALL_TOTAL - SMALL_CONV,), F32) if conv is None else conv.reshape(-1))
    return jnp.concatenate(vec).reshape(SMALL_ROWS, LANE)


def _unpack_small(packed):
    flat = packed.reshape(-1)
    out, off = {}, 0
    for n, size in zip(SMALL_NAMES, SMALL_SIZES):
        out[n] = flat[off:off + size].reshape(1, size)
        off += size
    return out


WEIGHT_ORDER = ("norm_w", "w_in", "conv_w_a", "a_log_fwd", "a_log_bwd", "dt_bias_fwd", "dt_bias_bwd", "o_norm_a",
                "q_norm_b", "k_norm_b", "sink_b", "mem_norm_w", "w_mem_kv", "q_norm_c", "k_norm_c", "w_out")


def kernel(x, mem, norm_w, w_in, conv_w_a, a_log_fwd, a_log_bwd, dt_bias_fwd, dt_bias_bwd, o_norm_a, q_norm_b, k_norm_b, sink_b, mem_norm_w, w_mem_kv, q_norm_c, k_norm_c, w_out, loss_target, m_norm_w, m_w_in, m_conv_w_a, m_a_log_fwd, m_a_log_bwd, m_dt_bias_fwd, m_dt_bias_bwd, m_o_norm_a, m_q_norm_b, m_k_norm_b, m_sink_b, m_mem_norm_w, m_w_mem_kv, m_q_norm_c, m_k_norm_c, m_w_out, v_norm_w, v_w_in, v_conv_w_a, v_a_log_fwd, v_a_log_bwd, v_dt_bias_fwd, v_dt_bias_bwd, v_o_norm_a, v_q_norm_b, v_k_norm_b, v_sink_b, v_mem_norm_w, v_w_mem_kv, v_q_norm_c, v_k_norm_c, v_w_out):
    weights = dict(norm_w=norm_w, w_in=w_in, conv_w_a=conv_w_a, a_log_fwd=a_log_fwd, a_log_bwd=a_log_bwd,
                   dt_bias_fwd=dt_bias_fwd, dt_bias_bwd=dt_bias_bwd, o_norm_a=o_norm_a, q_norm_b=q_norm_b,
                   k_norm_b=k_norm_b, sink_b=sink_b, mem_norm_w=mem_norm_w, w_mem_kv=w_mem_kv, q_norm_c=q_norm_c,
                   k_norm_c=k_norm_c, w_out=w_out)
    mom1 = dict(norm_w=m_norm_w, w_in=m_w_in, conv_w_a=m_conv_w_a, a_log_fwd=m_a_log_fwd, a_log_bwd=m_a_log_bwd,
                dt_bias_fwd=m_dt_bias_fwd, dt_bias_bwd=m_dt_bias_bwd, o_norm_a=m_o_norm_a, q_norm_b=m_q_norm_b,
                k_norm_b=m_k_norm_b, sink_b=m_sink_b, mem_norm_w=m_mem_norm_w, w_mem_kv=m_w_mem_kv,
                q_norm_c=m_q_norm_c, k_norm_c=m_k_norm_c, w_out=m_w_out)
    mom2 = dict(norm_w=v_norm_w, w_in=v_w_in, conv_w_a=v_conv_w_a, a_log_fwd=v_a_log_fwd, a_log_bwd=v_a_log_bwd,
                dt_bias_fwd=v_dt_bias_fwd, dt_bias_bwd=v_dt_bias_bwd, o_norm_a=v_o_norm_a, q_norm_b=v_q_norm_b,
                k_norm_b=v_k_norm_b, sink_b=v_sink_b, mem_norm_w=v_mem_norm_w, w_mem_kv=v_w_mem_kv,
                q_norm_c=v_q_norm_c, k_norm_c=v_k_norm_c, w_out=v_w_out)
    chip = 2 * lax.axis_index("x") + lax.axis_index("y")

    own_in = jnp.pad(jnp.transpose(w_in[0]).astype(BF16), ((0, W_IN_PAD - W_IN_BLOCK), (0, 0)))
    w_in4, conv4 = _all_gather_weights([own_in], conv_w_a[0])
    w_perm_t = _permute_blocks(w_in4)
    w_blocks_t = w_in4.reshape(N_CHIPS * W_IN_PAD, D_MODEL)
    conv_full = jnp.transpose(conv4, (1, 0, 2)).reshape(CONV_K, 3 * A_WIDTH)
    own_out, own_kv = w_out[0].astype(BF16), w_mem_kv[0].astype(BF16)

    def assemble(w_out4, w_kv4):
        w_out4 = lax.dynamic_update_index_in_dim(w_out4, own_out, chip, 0)
        w_kv4 = lax.dynamic_update_index_in_dim(w_kv4, own_kv, chip, 0)
        return w_out4.reshape(D_MODEL, D_MODEL), w_kv4.reshape(D_MODEL, 2 * C_HEADS * C_DIM)

    gather = (_PairedGather([own_out, own_kv]), assemble)
    pa = jnp.concatenate([_pad_row(a_log_fwd), _pad_row(a_log_bwd), _pad_row(dt_bias_fwd), _pad_row(dt_bias_bwd),
                          _pad_row(o_norm_a), jnp.zeros((3, LANE), F32)], axis=0)
    pb = jnp.concatenate([_pad_row(q_norm_b), _pad_row(k_norm_b), _pad_row(sink_b), jnp.zeros((5, LANE), F32)], axis=0)
    pc = jnp.concatenate([_pad_row(q_norm_c), _pad_row(k_norm_c), jnp.zeros((6, LANE), F32)], axis=0)

    full, got = {}, {}
    core = lax.axis_index("c").astype(jnp.int32).reshape(1)

    def pair_round(tag, blocks):
        names = [tag + "_%d" % i for i in range(len(blocks))]
        full.update(zip(names, blocks))
        got.update(zip(names, _pair_exchange(blocks, "grad_pair_exchange_" + tag)))
        return _ChipExchange([_pair_sum(full[n], got[n], core, "grad_pair_sum_" + n) for n in names])

    def early(g_w_out, g_w_kv):
        return pair_round("early", [g_w_out.reshape(N_CHIPS, D_MODEL // N_CHIPS, D_MODEL),
                                    g_w_kv.reshape(N_CHIPS, D_MODEL // N_CHIPS, 2 * C_HEADS * C_DIM)])

    def late(g_w_blocks_t):
        return pair_round("late", [g_w_blocks_t.reshape(N_CHIPS, W_IN_PAD, D_MODEL)])

    r = _local_step(x[0], mem[0], loss_target[0], norm_w, w_perm_t, w_blocks_t, conv_full, pa, pb, pc, mem_norm_w, None, None,
                    gather, (early, late))
    place = jnp.stack([chip, lax.axis_index("c")]).astype(jnp.int32)
    reduced = [_chip_sum(full[n], got[n], l, place, "grad_chip_sum_" + n)
               for n, l in zip(("late_0", "early_0", "early_1"), r["landed"])]
    g_w_in_t, g_w_out, g_w_kv = _pair_gather(reduced, [W_IN_BLOCK, D_MODEL // N_CHIPS, D_MODEL // N_CHIPS])

    d_pa, d_pb, d_pc = r["d_pa"], r["d_pb"], r["d_pc"]
    small_g = dict(norm_w=r["g_norm"], mem_norm_w=r["g_mem_norm"], o_norm_a=d_pa[4], q_norm_c=d_pc[0], k_norm_c=d_pc[1],
                   q_norm_b=d_pb[0, :B_DIM], k_norm_b=d_pb[1, :B_DIM], a_log_fwd=d_pa[0, :A_HEADS],
                   a_log_bwd=d_pa[1, :A_HEADS], dt_bias_fwd=d_pa[2, :A_HEADS], dt_bias_bwd=d_pa[3, :A_HEADS],
                   sink_b=d_pb[2, :B_HEADS])
    packed = _all_reduce_small(_pack_small(small_g, jnp.sum(r["loss_parts"][:, 0, 0]), r["g_conv"]))
    flat = packed.reshape(-1)
    loss = flat[SMALL_LOSS]
    conv_sum = flat[SMALL_CONV:].reshape(CONV_K, 3 * A_WIDTH)
    conv_cols = 3 * A_WIDTH // N_CHIPS
    g_conv = lax.dynamic_slice(conv_sum, (0, chip * conv_cols), (CONV_K, conv_cols))

    grads = _unpack_small(packed)
    grads["conv_w_a"] = g_conv
    delta, new_m, new_v = {}, {}, {}
    delta["conv_w_a"], new_m["conv_w_a"], new_v["conv_w_a"] = _adamw(conv_w_a[0], g_conv, m_conv_w_a[0], v_conv_w_a[0],
                                                                     "adamw_conv_w_a")
    for n, g in (("w_mem_kv", g_w_kv), ("w_out", g_w_out)):
        delta[n], new_m[n], new_v[n], grads[n] = _adamw(weights[n][0], g, mom1[n][0], mom2[n][0], "adamw_" + n, echo=True)
    stepped = _adamw(jnp.transpose(w_in[0]), g_w_in_t, jnp.transpose(m_w_in[0]), jnp.transpose(v_w_in[0]), "adamw_w_in",
                     echo=True)
    delta["w_in"], new_m["w_in"], new_v["w_in"], grads["w_in"] = (jnp.transpose(t) for t in stepped)
    d_s, m_s, v_s = _adamw(_pack_small(weights), packed, _pack_small(mom1), _pack_small(mom2), "adamw_small")
    d_s, m_s, v_s = _unpack_small(d_s), _unpack_small(m_s), _unpack_small(v_s)
    for n in SMALL_NAMES:
        delta[n], new_m[n], new_v[n] = d_s[n], m_s[n], v_s[n]

    def shaped(tree):
        return [tree[n].reshape(weights[n].shape) for n in WEIGHT_ORDER]

    return (loss, r["g_x"].reshape(x.shape), *shaped(grads), *shaped(delta), *shaped(new_m), *shaped(new_v))
```

```python
import jax
import jax.numpy as jnp
from jax import lax
from jax.experimental import pallas as pl
from jax.experimental.pallas import tpu as pltpu

F32 = jnp.float32
BF16 = jnp.bfloat16
HI = lax.Precision.HIGHEST
MESH = pl.DeviceIdType.MESH

D_MODEL = 2048
A_WIDTH = 1024
A_HEADS = 8
A_DIM = 128
CONV_K = 5
CHUNK = 64
B_HEADS = 8
B_KV = 2
B_DIM = 64
WINDOW = 128
C_HEADS = 4
C_DIM = 128
MEM_LEN = 256
ROPE_THETA = 10000.0
EPS = 1e-6
IN_WIDTH = 6432
N_CHIPS = 4
W_IN_BLOCK = IN_WIDTH // N_CHIPS
W_IN_PAD = 1664

LANE = 128
P_QA, P_KA, P_VA, P_ZA = 0, 1024, 2048, 3072
P_QB, P_ZB, P_QC, P_ZC = 4096, 4608, 5120, 5632
P_KB, P_VB, P_GT = 6144, 6272, 6400
P_WIDTH = 6656
O_GT, O_QB, O_KB, O_VB, O_ZB, O_QC, O_ZC = 4096, 4128, 4640, 4768, 4896, 5408, 5920

ADAM_LR, ADAM_B1, ADAM_B2, ADAM_EPS, ADAM_WD, ADAM_STEP = 0.001, 0.9, 0.999, 1e-08, 0.01, 10

VMEM_LIMIT = 56 * 1024 * 1024


def _params(sem=None):
    return pltpu.CompilerParams(dimension_semantics=sem, vmem_limit_bytes=VMEM_LIMIT)


def _dot(a, b, dims=(((1,), (0,)), ((), ())), precision=HI):
    return lax.dot_general(a, b, dims, precision=precision, preferred_element_type=F32)


_NN = (((1,), (0,)), ((), ()))
_NT = (((1,), (1,)), ((), ()))
_TN = (((0,), (0,)), ((), ()))


def _bdot(a, b, dims):
    return lax.dot_general(a.astype(BF16), b.astype(BF16), dims, preferred_element_type=F32)


@jax.custom_vjp
def _mm(a, b):
    return _bdot(a, b, _NN)


_mm.defvjp(lambda a, b: (_bdot(a, b, _NN), (a, b)),
           lambda res, ct: (_bdot(ct, res[1], _NT), _bdot(res[0], ct, _TN)))


@jax.custom_vjp
def _mm_nt(a, b):
    return _bdot(a, b, _NT)


_mm_nt.defvjp(lambda a, b: (_bdot(a, b, _NT), (a, b)),
              lambda res, ct: (_bdot(ct, res[1], _NN), _bdot(ct, res[0], _TN)))


@jax.custom_vjp
def _mm_tn(a, b):
    return _bdot(a, b, _TN)


_mm_tn.defvjp(lambda a, b: (_bdot(a, b, _TN), (a, b)),
              lambda res, ct: (_bdot(res[1], ct, _NT), _bdot(res[0], ct, _NN)))


def _rms(t, w):
    return t * lax.rsqrt(jnp.mean(t * t, axis=-1, keepdims=True) + EPS) * w


def _l2(t):
    return t * lax.rsqrt(jnp.sum(t * t, axis=-1, keepdims=True) + EPS)


def _silu(t):
    return t * jax.nn.sigmoid(t)


def _softplus(t):
    return jnp.maximum(t, 0.0) + jnp.log1p(jnp.exp(-jnp.abs(t)))


def _matmul(a, b, mode, out_dtype, name, tm=512, tn=512, tk=512, ride=None):
    (m, k) = a.shape[::-1] if mode == "tn" else a.shape
    n = b.shape[0] if mode == "nt" else b.shape[1]
    tm, tn, tk = min(tm, m), min(tn, n), min(tk, k)
    assert m % tm == 0 and n % tn == 0 and k % tk == 0, (m, n, k, tm, tn, tk)
    if mode == "nn":
        a_spec = pl.BlockSpec((tm, tk), lambda i, j, kk: (i, kk))
        b_spec = pl.BlockSpec((tk, tn), lambda i, j, kk: (kk, j))
        dims = (((1,), (0,)), ((), ()))
    elif mode == "nt":
        a_spec = pl.BlockSpec((tm, tk), lambda i, j, kk: (i, kk))
        b_spec = pl.BlockSpec((tn, tk), lambda i, j, kk: (j, kk))
        dims = (((1,), (1,)), ((), ()))
    else:
        a_spec = pl.BlockSpec((tk, tm), lambda i, j, kk: (kk, i))
        b_spec = pl.BlockSpec((tk, tn), lambda i, j, kk: (kk, j))
        dims = (((0,), (0,)), ((), ()))
    nk = k // tk
    grid = (m // tm, n // tn, nk)
    n_in = len(ride.operands) if ride else 0
    n_out = len(ride.out_shapes) if ride else 0

    def body(*refs):
        a_ref, b_ref, o_ref = refs[0], refs[1], refs[2 + n_in]
        scratch = refs[3 + n_in + n_out:]
        step = (pl.program_id(0) * grid[1] + pl.program_id(1)) * nk + pl.program_id(2)
        riders = (refs[2:2 + n_in], refs[3 + n_in:3 + n_in + n_out], scratch[(0 if nk == 1 else 1):])
        if ride:
            pl.when(step == 0)(lambda: ride.start(*riders))
        if nk == 1:
            o_ref[...] = _bdot(a_ref[...], b_ref[...], dims).astype(out_dtype)
        else:
            acc_ref, kk = scratch[0], pl.program_id(2)

            @pl.when(kk == 0)
            def _():
                acc_ref[...] = jnp.zeros_like(acc_ref)

            acc_ref[...] += _bdot(a_ref[...], b_ref[...], dims)

            @pl.when(kk == nk - 1)
            def _():
                o_ref[...] = acc_ref[...].astype(out_dtype)
        if ride:
            pl.when(step == grid[0] * grid[1] * nk - 1)(lambda: ride.finish(*riders))

    out = pl.pallas_call(
        body, name=name, grid=grid,
        in_specs=[a_spec, b_spec] + [HBM] * n_in,
        out_specs=[pl.BlockSpec((tm, tn), lambda i, j, kk: (i, j))] + [HBM] * n_out,
        out_shape=[jax.ShapeDtypeStruct((m, n), out_dtype)] + (list(ride.out_shapes) if ride else []),
        scratch_shapes=([] if nk == 1 else [pltpu.VMEM((tm, tn), F32)]) + (list(ride.scratch_shapes) if ride else []),
        compiler_params=_params(("arbitrary",) * 3 if ride else ("parallel", "parallel", "arbitrary")),
    )(a, b, *(ride.operands if ride else []))
    return out if ride else out[0]


def _rms_fwd(x, w, tr=256):
    s, d = x.shape

    def body(x_ref, w_ref, o_ref):
        o_ref[...] = _rms(x_ref[...], w_ref[...]).astype(BF16)

    return pl.pallas_call(
        body, name="rms_fwd", grid=(s // tr,),
        in_specs=[pl.BlockSpec((tr, d), lambda i: (i, 0)), pl.BlockSpec((1, d), lambda i: (0, 0))],
        out_specs=pl.BlockSpec((tr, d), lambda i: (i, 0)),
        out_shape=jax.ShapeDtypeStruct((s, d), BF16), compiler_params=_params(("parallel",)),
    )(x, w)


def _input_grad(d_proj, w_t, x, w, dy, ride=None, tm=512, tk=512):
    s, k = d_proj.shape
    d = w_t.shape[1]
    tm = min(tm, s)
    nk = k // tk
    grid = (s // tm, nk)
    n_in = len(ride.operands) if ride else 0
    n_out = len(ride.out_shapes) if ride else 0

    def body(*refs):
        a_ref, b_ref, x_ref, w_ref, dy_ref = refs[:5]
        gx_ref, gw_ref = refs[5 + n_in:7 + n_in]
        acc_ref = refs[7 + n_in + n_out]
        riders = (refs[5:5 + n_in], refs[7 + n_in:7 + n_in + n_out], refs[8 + n_in + n_out:])
        kk = pl.program_id(1)
        step = pl.program_id(0) * nk + kk
        if ride:
            pl.when(step == 0)(lambda: ride.start(*riders))

        @pl.when(step == 0)
        def _():
            gw_ref[...] = jnp.zeros_like(gw_ref)

        @pl.when(kk == 0)
        def _():
            acc_ref[...] = jnp.zeros_like(acc_ref)

        acc_ref[...] += _bdot(a_ref[...], b_ref[...], _NN)

        @pl.when(kk == nk - 1)
        def _():
            _, vjp = jax.vjp(_rms, x_ref[...], w_ref[...])
            dx, dw = vjp(acc_ref[...])
            gx_ref[...] = dy_ref[...] + dx
            gw_ref[...] += dw

        if ride:
            pl.when(step == grid[0] * nk - 1)(lambda: ride.finish(*riders))

    row = pl.BlockSpec((tm, d), lambda i, kk: (i, 0))
    vec = pl.BlockSpec((1, d), lambda i, kk: (0, 0))
    return pl.pallas_call(
        body, name="input_grad", grid=grid,
        in_specs=[pl.BlockSpec((tm, tk), lambda i, kk: (i, kk)), pl.BlockSpec((tk, d), lambda i, kk: (kk, 0)), row, vec, row]
        + [HBM] * n_in,
        out_specs=[row, vec] + [HBM] * n_out,
        out_shape=[jax.ShapeDtypeStruct((s, d), F32), jax.ShapeDtypeStruct((1, d), F32)]
        + (list(ride.out_shapes) if ride else []),
        scratch_shapes=[pltpu.VMEM((tm, d), F32)] + (list(ride.scratch_shapes) if ride else []),
        compiler_params=_params(("arbitrary", "arbitrary")),
    )(d_proj, w_t, x, w, dy, *(ride.operands if ride else []))


def _loss_dy(x, mo, target, tr=256):
    s, d = x.shape
    nt = s // tr

    def body(x_ref, mo_ref, t_ref, dy_ref, dyb_ref, l_ref):
        err = x_ref[...] + mo_ref[...] - t_ref[...]
        dy = err * (1.0 / d)
        dy_ref[...] = dy
        dyb_ref[...] = dy.astype(BF16)
        l_ref[...] = jnp.full(l_ref.shape, 0.5 * jnp.sum(jnp.sum(err * err, axis=1, keepdims=True) * (1.0 / d)), F32)

    row = pl.BlockSpec((tr, d), lambda i: (i, 0))
    return pl.pallas_call(
        body, name="loss_dy", grid=(nt,), in_specs=[row, row, row],
        out_specs=[row, row, pl.BlockSpec((1, 8, LANE), lambda i: (i, 0, 0))],
        out_shape=[jax.ShapeDtypeStruct((s, d), F32), jax.ShapeDtypeStruct((s, d), BF16),
                   jax.ShapeDtypeStruct((nt, 8, LANE), F32)],
        compiler_params=_params(("parallel",)),
    )(x, mo, target)


def _shift_rows(t, s):
    if s == 0:
        return t
    n = t.shape[0]
    rolled = pltpu.roll(t, (-s) % n, axis=0)
    idx = lax.broadcasted_iota(jnp.int32, t.shape, 0) + s
    return jnp.where((idx >= 0) & (idx < n), rolled, 0.0)


CONV_FWD_COLS = 512
CONV_BWD_COLS = 128


def _conv_fwd(proj, conv_w):
    s = proj.shape[0]
    cols, split = CONV_FWD_COLS, A_WIDTH // CONV_FWD_COLS
    nblk = 3 * A_WIDTH // cols

    def body(x_ref, w_ref, o_ref):
        x = x_ref[...]
        acc = jnp.zeros_like(x)
        for j in range(CONV_K):
            acc = acc + w_ref[j:j + 1, :] * _shift_rows(x, j - CONV_K // 2)
        o_ref[...] = acc

    return pl.pallas_call(
        body, name="conv_fwd", grid=(nblk,),
        in_specs=[pl.BlockSpec((s, cols), lambda i: (0, i)), pl.BlockSpec((CONV_K, cols), lambda i: (0, i))],
        out_specs=pl.BlockSpec((None, s, cols), lambda i: (i // split, 0, i % split)),
        out_shape=jax.ShapeDtypeStruct((3, s, A_WIDTH), F32), compiler_params=_params(("parallel",)),
    )(proj, conv_w)


def _conv_bwd(proj, conv_w, d_c):
    s = proj.shape[0]
    cols, split = CONV_BWD_COLS, A_WIDTH // CONV_BWD_COLS
    nblk = 3 * A_WIDTH // cols

    def body(x_ref, w_ref, g_ref, dx_ref, dw_ref):
        x, g = x_ref[...], g_ref[...]
        acc = jnp.zeros_like(x)
        for j in range(CONV_K):
            off = j - CONV_K // 2
            acc = acc + w_ref[j:j + 1, :] * _shift_rows(g, -off)
            dw_ref[j:j + 1, :] = jnp.sum(_shift_rows(x, off) * g, axis=0, keepdims=True)
        dx_ref[...] = acc.astype(BF16)

    col = pl.BlockSpec((s, cols), lambda i: (0, i))
    wsp = pl.BlockSpec((CONV_K, cols), lambda i: (0, i))
    dsp = pl.BlockSpec((None, s, cols), lambda i: (i // split, 0, i % split))
    return pl.pallas_call(
        body, name="conv_bwd", grid=(nblk,), in_specs=[col, wsp, dsp], out_specs=[col, wsp],
        out_shape=[jax.ShapeDtypeStruct((s, 3 * A_WIDTH), BF16), jax.ShapeDtypeStruct((CONV_K, 3 * A_WIDTH), F32)],
        compiler_params=_params(("parallel",)),
    )(proj, conv_w, d_c)


A_FWD_HEADS = 4
A_BWD_HEADS = 4


def _neumann_inverse(a):
    c = a.shape[-1]
    eye = (lax.broadcasted_iota(jnp.int32, (c, c), 0) == lax.broadcasted_iota(jnp.int32, (c, c), 1)).astype(F32)
    tinv = eye + a
    p = a
    for _ in range(5):
        p = _mm(p, p)
        tinv = tinv + _mm(tinv, p)
    return tinv


@jax.custom_vjp
def _unit_inverse(a):
    return _neumann_inverse(a)


def _unit_inverse_fwd(a):
    tinv = _neumann_inverse(a)
    return tinv, tinv


def _unit_inverse_bwd(tinv, ct):
    return (_bdot(_bdot(tinv, ct, _TN), tinv, _NT),)


_unit_inverse.defvjp(_unit_inverse_fwd, _unit_inverse_bwd)


@jax.custom_vjp
def _known_inverse(a, tinv):
    return tinv


_known_inverse.defvjp(lambda a, tinv: (tinv, tinv),
                      lambda tinv, ct: (_unit_inverse_bwd(tinv, ct)[0], jnp.zeros_like(tinv)))


def _a_chain(st, cq, ck, cv, alpha, beta_raw, a_log, dt_b, incl, strict, last, kept=None):
    c = CHUNK
    gb = -jnp.exp(a_log) * _softplus(alpha + dt_b)
    bb = jax.nn.sigmoid(beta_raw)
    q = _l2(_silu(cq)) * (A_DIM ** -0.5)
    k = _l2(_silu(ck))
    v = _silu(cv)

    gc = _dot(incl, jnp.broadcast_to(gb, (c, LANE)))
    tot = jnp.sum(gc * last, axis=0, keepdims=True)
    m1 = gc[:, :c]
    decay = incl * jnp.exp(incl * (m1 - m1.T))
    kb = k * bb
    vb = v * bb
    a = -(strict * decay * _mm_nt(kb, k))
    tinv = _unit_inverse(a) if kept is None else _known_inverse(a, kept)
    eg = jnp.exp(gc)
    u = _mm(tinv, vb)
    w = _mm(tinv, kb * eg)
    qk = _mm_nt(q, k) * decay
    v_new = u - _mm(w, st)
    o = _mm(q * eg, st) + _mm(qk, v_new)
    st_new = st * jnp.exp(tot) + _mm_tn(k * jnp.exp(tot - gc), v_new)
    return st_new, o, tinv


def _a_step(sts, cq, ck, cv, gts, pa, h0, kept=None):
    c = CHUNK
    lane = lax.broadcasted_iota(jnp.int32, (1, LANE), 1)
    ii = lax.broadcasted_iota(jnp.int32, (c, c), 0)
    jj = lax.broadcasted_iota(jnp.int32, (c, c), 1)
    row = lax.broadcasted_iota(jnp.int32, (c, 1), 0)

    def pick(t, col):
        return jnp.sum(jnp.where(lane == col, t, 0.0), axis=1, keepdims=True)

    alpha, beta_raw, a_log, dt_b, incl, strict, last = [], [], [], [], [], [], []
    for b in range(sts.shape[0]):
        h, rev = h0 + b // 2, b % 2
        alpha.append(pick(gts[b], h + 8 * rev))
        beta_raw.append(pick(gts[b], h + 16 + 8 * rev))
        a_log.append(pick(pa[rev:rev + 1, :], h))
        dt_b.append(pick(pa[2 + rev:3 + rev, :], h))
        incl.append(((ii <= jj) if rev else (ii >= jj)).astype(F32))
        strict.append(((ii < jj) if rev else (ii > jj)).astype(F32))
        last.append((row == (0 if rev else c - 1)).astype(F32))
    stack = lambda ts: jnp.concatenate([t[None] for t in ts], axis=0)
    return jax.vmap(_a_chain)(sts, cq, ck, cv, stack(alpha), stack(beta_raw), stack(a_log), stack(dt_b),
                              stack(incl), stack(strict), stack(last), kept)


def _a_final(o, za, pa):
    outs = []
    for j in range(o.shape[1] // A_DIM):
        ln = slice(j * A_DIM, (j + 1) * A_DIM)
        outs.append(_rms(o[:, ln], pa[4:5, :]) * _silu(za[:, ln]))
    return jnp.concatenate(outs, axis=1)


def _a_tiles(n, nchunk, heads):
    tiles = []
    for b in range(2 * heads):
        i = (nchunk - 1 - n) if b % 2 else n
        tiles.append((i, pl.ds(pl.multiple_of(i * CHUNK, CHUNK), CHUNK), slice((b // 2) * A_DIM, (b // 2 + 1) * A_DIM)))
    return tiles


def _a_load(tiles, c_ref, gt_ref):
    cq, ck, cv = (jnp.stack([c_ref[r, sl, ln] for _, sl, ln in tiles], axis=0) for r in range(3))
    return cq, ck, cv, jnp.stack([gt_ref[sl, :] for _, sl, _ in tiles], axis=0)


def _loop_by_two(n, step, init):
    assert n % 2 == 0
    return lax.fori_loop(0, n // 2, lambda m, carry: step(2 * m + 1, step(2 * m, carry, 0), 1), init)


def _a_scan(h0, heads, nchunk, c_ref, gt_ref, pa, of_ref, ob_ref, s_ref, t_ref):
    def step(n, sts, parity):
        tiles = _a_tiles(n, nchunk, heads)
        sts_new, o, tinv = _a_step(sts, *_a_load(tiles, c_ref, gt_ref), pa, h0)
        for b, (i, sl, ln) in enumerate(tiles):
            s_ref[b, i] = sts[b]
            t_ref[b, i] = tinv[b]
            (ob_ref if b % 2 else of_ref)[sl, ln] = o[b]
        return sts_new

    _loop_by_two(nchunk, step, jnp.zeros((2 * heads, A_DIM, A_DIM), F32))


def _a_specs(s, heads):
    wide = heads * A_DIM
    once = pl.Buffered(1)
    trio = pl.BlockSpec((3, s, wide), lambda g: (0, 0, g), pipeline_mode=once)
    gates = pl.BlockSpec((s, LANE), lambda g: (0, P_GT // LANE))
    small = pl.BlockSpec((8, LANE), lambda g: (0, 0))

    def cols(base):
        return pl.BlockSpec((s, wide), lambda g: (0, base // wide + g), pipeline_mode=once)

    state = pl.BlockSpec((2 * heads, s // CHUNK, A_DIM, A_DIM), lambda g: (g, 0, 0, 0), pipeline_mode=once)
    kept = pl.BlockSpec((2 * heads, s // CHUNK, CHUNK, CHUNK), lambda g: (g, 0, 0, 0), pipeline_mode=once)
    return wide, trio, gates, small, cols, state, kept


def _delta_fwd(cqkv, proj, pa, ride=None):
    s = cqkv.shape[1]
    nchunk = s // CHUNK
    heads = A_FWD_HEADS
    steps = A_HEADS // heads
    wide, trio, gates, small, cols, state, kept = _a_specs(s, heads)
    n_in = len(ride.operands) if ride else 0
    n_out = len(ride.out_shapes) if ride else 0

    def body(*refs):
        c_ref, gt_ref, za_ref, pa_ref = refs[:4]
        out_ref, o_ref, s_ref, t_ref = refs[4 + n_in:8 + n_in]
        ob_ref = refs[8 + n_in + n_out]
        riders = (refs[4:4 + n_in], refs[8 + n_in:8 + n_in + n_out], refs[9 + n_in + n_out:])
        g = pl.program_id(0)
        if ride:
            pl.when(g == 0)(lambda: ride.start(*riders))
            pl.when(g == steps - 1)(lambda: ride.middle(*riders))
        h0 = g * heads
        pa_v = pa_ref[...]
        _a_scan(h0, heads, nchunk, c_ref, gt_ref, pa_v, o_ref, ob_ref, s_ref, t_ref)
        o_ref[...] += ob_ref[...]
        out_ref[...] = _a_final(o_ref[...], za_ref[...], pa_v).astype(BF16)
        if ride:
            pl.when(g == steps - 1)(lambda: ride.finish(*riders))

    assert steps > 1
    return pl.pallas_call(
        body, name="delta_fwd", grid=(steps,),
        in_specs=[trio, gates, cols(P_ZA), small] + [HBM] * n_in,
        out_specs=[cols(0), cols(0), state, kept] + [HBM] * n_out,
        out_shape=[jax.ShapeDtypeStruct((s, D_MODEL), BF16),
                   jax.ShapeDtypeStruct((s, A_WIDTH), F32),
                   jax.ShapeDtypeStruct((2 * A_HEADS, nchunk, A_DIM, A_DIM), F32),
                   jax.ShapeDtypeStruct((2 * A_HEADS, nchunk, CHUNK, CHUNK), F32)]
        + (list(ride.out_shapes) if ride else []),
        scratch_shapes=[pltpu.VMEM((s, wide), F32)] + (list(ride.scratch_shapes) if ride else []),
        compiler_params=_params(("arbitrary",)),
    )(cqkv, proj, proj, pa, *(ride.operands if ride else []))


def _delta_out_bwd(o_sum, proj, pa, d_mixed, tr=256):
    s = o_sum.shape[0]

    def body(o_ref, za_ref, pa_ref, dm_ref, do_ref, dza_ref, dpa_ref):
        @pl.when(pl.program_id(0) == 0)
        def _():
            dpa_ref[...] = jnp.zeros_like(dpa_ref)

        _, vjp = jax.vjp(_a_final, o_ref[...], za_ref[...], pa_ref[...])
        d_o, d_za, dpa = vjp(dm_ref[...].astype(F32))
        do_ref[...] = d_o
        dza_ref[...] = d_za.astype(BF16)
        dpa_ref[...] += dpa

    def rows(col):
        return pl.BlockSpec((tr, A_WIDTH), lambda i: (i, col))

    small = pl.BlockSpec((8, LANE), lambda i: (0, 0))
    return pl.pallas_call(
        body, name="delta_out_bwd", grid=(s // tr,), in_specs=[rows(0), rows(P_ZA // A_WIDTH), small, rows(0)],
        out_specs=[rows(0), rows(0), small],
        out_shape=[jax.ShapeDtypeStruct((s, A_WIDTH), F32), jax.ShapeDtypeStruct((s, A_WIDTH), BF16),
                   jax.ShapeDtypeStruct((8, LANE), F32)],
        compiler_params=_params(("arbitrary",)),
    )(o_sum, proj, pa, d_mixed)


def _delta_bwd(cqkv, proj, pa, d_o, states, inverses, ride=None):
    s = cqkv.shape[1]
    nchunk = s // CHUNK
    heads = A_BWD_HEADS
    steps = A_HEADS // heads
    wide, trio, gates, small, cols, _, _ = _a_specs(s, heads)
    n_in = len(ride.operands) if ride else 0
    n_out = len(ride.out_shapes) if ride else 0

    def body(*refs):
        c_ref, gt_ref, pa_ref, do_ref, s_hbm, t_hbm = refs[:6]
        dc_ref, dgt_ref, dpa_ref = refs[6 + n_in:9 + n_in]
        s_buf, t_buf, s_sems = refs[9 + n_in + n_out:12 + n_in + n_out]
        riders = (refs[6:6 + n_in], refs[9 + n_in:9 + n_in + n_out], refs[12 + n_in + n_out:])
        if ride:
            pl.when(pl.program_id(0) == 0)(lambda: ride.start(*riders))
        h0 = pl.program_id(0) * heads
        pa_v = pa_ref[...]

        @pl.when(h0 == 0)
        def _():
            dgt_ref[...] = jnp.zeros_like(dgt_ref)
            dpa_ref[...] = jnp.zeros_like(dpa_ref)

        dc_ref[...] = jnp.zeros_like(dc_ref)

        def state_copies(n, slot):
            tiles = _a_tiles(nchunk - 1 - n, nchunk, heads)
            return ([pltpu.make_async_copy(s_hbm.at[2 * h0 + b, i], s_buf.at[slot, b], s_sems.at[0, slot, b])
                     for b, (i, _, _) in enumerate(tiles)]
                    + [pltpu.make_async_copy(t_hbm.at[2 * h0 + b, i], t_buf.at[slot, b], s_sems.at[1, slot, b])
                       for b, (i, _, _) in enumerate(tiles)])

        for cp in state_copies(0, 0):
            cp.start()

        def step(n, carry, parity):
            d_sts, dpa = carry
            tiles = _a_tiles(nchunk - 1 - n, nchunk, heads)
            for cp in state_copies(n, parity):
                cp.wait()

            @pl.when(n + 1 < nchunk)
            def _():
                for cp in state_copies(n + 1, 1 - parity):
                    cp.start()

            sts, kept = s_buf[parity], t_buf[parity]
            d_o_t = jnp.stack([do_ref[sl, ln] for _, sl, ln in tiles], axis=0)
            _, vjp_c = jax.vjp(lambda *a: _a_step(*a, h0, kept)[:2], sts, *_a_load(tiles, c_ref, gt_ref), pa_v)
            d_prev, dcq, dck, dcv, dgts, dpa_i = vjp_c((d_sts, d_o_t))
            for b, (_, sl, ln) in enumerate(tiles):
                for r, dc in enumerate((dcq, dck, dcv)):
                    dc_ref[r, sl, ln] += dc[b]
                dgt_ref[sl, :] += dgts[b]
            return d_prev, dpa + dpa_i

        init = (jnp.zeros((2 * heads, A_DIM, A_DIM), F32), jnp.zeros((8, LANE), F32))
        _, dpa_out = lax.fori_loop(0, nchunk, lambda n, carry: step(n, carry, n % 2), init)
        dpa_ref[...] += dpa_out
        if ride:
            pl.when(pl.program_id(0) == steps - 1)(lambda: ride.finish(*riders))

    fixed = pl.BlockSpec((s, LANE), lambda g: (0, 0))
    return pl.pallas_call(
        body, name="delta_bwd", grid=(steps,),
        in_specs=[trio, gates, small, cols(0), pl.BlockSpec(memory_space=pl.ANY), pl.BlockSpec(memory_space=pl.ANY)]
        + [HBM] * n_in,
        out_specs=[trio, fixed, small] + [HBM] * n_out,
        out_shape=[jax.ShapeDtypeStruct((3, s, A_WIDTH), F32), jax.ShapeDtypeStruct((s, LANE), F32),
                   jax.ShapeDtypeStruct((8, LANE), F32)] + (list(ride.out_shapes) if ride else []),
        scratch_shapes=[pltpu.VMEM((2, 2 * heads, A_DIM, A_DIM), F32), pltpu.VMEM((2, 2 * heads, CHUNK, CHUNK), F32),
                        pltpu.SemaphoreType.DMA((2, 2, 2 * heads))]
        + (list(ride.scratch_shapes) if ride else []),
        compiler_params=_params(("arbitrary",)),
    )(cqkv, proj, pa, d_o, states, inverses, *(ride.operands if ride else []))


def _rope_tables(s):
    inv = ROPE_THETA ** (-jnp.arange(0, B_DIM, 2, dtype=F32) / B_DIM)
    ang = jnp.arange(s, dtype=F32)[:, None] * inv[None, :]
    cos, sin = jnp.cos(ang), jnp.sin(ang)
    return jnp.concatenate([cos, cos], axis=1), jnp.concatenate([-sin, sin], axis=1)


def _b_block(q_t, z_t, k3, v3, cos_q, sin_q, cos_k, sin_k, pb, n, nb):
    w = WINDOW
    def swap(t):
        return jnp.concatenate([t[:, B_DIM // 2:], t[:, :B_DIM // 2]], axis=1)

    grp = B_HEADS // B_KV
    qi = lax.broadcasted_iota(jnp.int32, (grp * w, 3 * w), 0) & (w - 1)
    kj = lax.broadcasted_iota(jnp.int32, (grp * w, 3 * w), 1)
    kpos = kj + (n - 1) * w
    mask = (jnp.abs(kj - w - qi) <= w) & (kpos >= 0) & (kpos < nb * w)
    lane = lax.broadcasted_iota(jnp.int32, (1, LANE), 1)
    qn, kn = pb[0:1, :B_DIM], pb[1:2, :B_DIM]
    cos_g = jnp.concatenate([cos_q] * grp, axis=0)
    sin_g = jnp.concatenate([sin_q] * grp, axis=0)
    def group(q, k, v, sink):
        k = _rms(k, kn)
        k = k * cos_k + swap(k) * sin_k
        q = _rms(q, qn)
        q = q * cos_g + swap(q) * sin_g
        s = _mm_nt(q, k) * (B_DIM ** -0.5)
        s = jnp.where(mask, s, -jnp.inf)
        m = jnp.maximum(jnp.max(s, axis=1, keepdims=True), sink)
        p = jnp.exp(s - m)
        p = p / (jnp.sum(p, axis=1, keepdims=True) + jnp.exp(sink - m))
        return _mm(p, v)

    stack = lambda ts: jnp.concatenate([t[None] for t in ts], axis=0)
    qs, ks, vs, sinks = [], [], [], []
    for hk in range(B_KV):
        heads = [hk * grp + g for g in range(grp)]
        ks.append(k3[:, hk * B_DIM:(hk + 1) * B_DIM])
        vs.append(v3[:, hk * B_DIM:(hk + 1) * B_DIM])
        qs.append(jnp.concatenate([q_t[:, hq * B_DIM:(hq + 1) * B_DIM] for hq in heads], axis=0))
        sinks.append(jnp.concatenate(
            [jnp.broadcast_to(jnp.sum(jnp.where(lane == hq, pb[2:3, :], 0.0), axis=1, keepdims=True), (w, 1))
             for hq in heads], axis=0))
    o = jax.vmap(group)(stack(qs), stack(ks), stack(vs), stack(sinks))
    outs = [o[hk, g * w:(g + 1) * w, :] for hk in range(B_KV) for g in range(grp)]
    return jnp.concatenate(outs, axis=1) * _silu(z_t)


def _b_specs(s):
    nb = s // WINDOW
    qsp = pl.BlockSpec((WINDOW, 512), lambda n: (n, P_QB // 512))
    zsp = pl.BlockSpec((WINDOW, 512), lambda n: (n, P_ZB // 512))

    def three(col, width):
        return [pl.BlockSpec((WINDOW, width), lambda n: (jnp.maximum(n - 1, 0), col)),
                pl.BlockSpec((WINDOW, width), lambda n: (n, col)),
                pl.BlockSpec((WINDOW, width), lambda n: (jnp.minimum(n + 1, nb - 1), col))]

    tab = pl.BlockSpec((WINDOW, B_DIM), lambda n: (n, 0))
    small = pl.BlockSpec((8, LANE), lambda n: (0, 0))
    specs = [qsp, zsp] + three(P_KB // LANE, LANE) + three(P_VB // LANE, LANE) + [tab, tab] + three(0, B_DIM) + three(0, B_DIM) + [small]
    return nb, specs


def _b_args(proj, cos2, sin2, pb):
    return (proj, proj, proj, proj, proj, proj, proj, proj, cos2, sin2, cos2, cos2, cos2, sin2, sin2, sin2, pb)


def _b_load(refs):
    (q_ref, z_ref, kp, kc, kx, vp, vc, vx, cq, sq, ckp, ckc, ckx, skp, skc, skx, pb_ref) = refs
    cat = lambda *r: jnp.concatenate([t[...] for t in r], axis=0)
    return (q_ref[...], z_ref[...], cat(kp, kc, kx), cat(vp, vc, vx), cq[...], sq[...], cat(ckp, ckc, ckx),
            cat(skp, skc, skx), pb_ref[...])


def _attn_b_fwd(proj, cos2, sin2, pb, mixed):
    s = proj.shape[0]
    nb, specs = _b_specs(s)

    def body(*refs):
        o_ref = refs[-1]
        args = _b_load(refs[:-2])
        o_ref[...] = _b_block(*args, pl.program_id(0), nb).astype(BF16)

    return pl.pallas_call(
        body, name="attn_b_fwd", grid=(nb,), in_specs=specs + [pl.BlockSpec(memory_space=pl.ANY)],
        out_specs=pl.BlockSpec((WINDOW, 512), lambda n: (n, A_WIDTH // 512)),
        out_shape=jax.ShapeDtypeStruct(mixed.shape, mixed.dtype), input_output_aliases={len(specs): 0},
        compiler_params=_params(("parallel",)),
    )(*_b_args(proj, cos2, sin2, pb), mixed)


def _attn_b_bwd(proj, cos2, sin2, pb, d_mixed):
    s = proj.shape[0]
    nb, specs = _b_specs(s)
    w = WINDOW

    def body(*refs):
        dm_ref, dq_ref, dz_ref, dk_ref, dv_ref, dpb_ref = refs[-6:]
        n = pl.program_id(0)
        q_t, z_t, k3, v3, cq, sq, ck, sk, pb_v = _b_load(refs[:-6])

        @pl.when(n == 0)
        def _():
            dk_ref[...] = jnp.zeros_like(dk_ref)
            dv_ref[...] = jnp.zeros_like(dv_ref)
            dpb_ref[...] = jnp.zeros_like(dpb_ref)

        def f(q_, z_, k_, v_, pb_):
            return _b_block(q_, z_, k_, v_, cq, sq, ck, sk, pb_, n, nb)

        _, vjp = jax.vjp(f, q_t, z_t, k3, v3, pb_v)
        dq, dz, dk3, dv3, dpb = vjp(dm_ref[...])
        dq_ref[...] = dq.astype(BF16)
        dz_ref[...] = dz.astype(BF16)
        dpb_ref[...] += dpb

        def add(j, cond):
            @pl.when(cond)
            def _():
                rows = pl.ds(pl.multiple_of((n - 1 + j) * w, w), w)
                dk_ref[rows, :] += dk3[j * w:(j + 1) * w, :]
                dv_ref[rows, :] += dv3[j * w:(j + 1) * w, :]

        add(0, n > 0)
        add(1, n >= 0)
        add(2, n < nb - 1)

    blk = pl.BlockSpec((w, 512), lambda n: (n, 0))
    whole = pl.BlockSpec((s, LANE), lambda n: (0, 0))
    small = pl.BlockSpec((8, LANE), lambda n: (0, 0))
    return pl.pallas_call(
        body, name="attn_b_bwd", grid=(nb,),
        in_specs=specs + [pl.BlockSpec((w, 512), lambda n: (n, 2))],
        out_specs=[blk, blk, whole, whole, small],
        out_shape=[jax.ShapeDtypeStruct((s, 512), BF16), jax.ShapeDtypeStruct((s, 512), BF16),
                   jax.ShapeDtypeStruct((s, LANE), F32), jax.ShapeDtypeStruct((s, LANE), F32),
                   jax.ShapeDtypeStruct((8, LANE), F32)],
        compiler_params=_params(("arbitrary",)),
    )(*_b_args(proj, cos2, sin2, pb), d_mixed)


def _mem_kv_fwd(mem, mem_norm_w, w_kv):
    def body(mem_ref, nw_ref, w_ref, kv_ref):
        mn = _rms(mem_ref[...], nw_ref[...]).astype(BF16)
        kv_ref[...] = jnp.dot(mn, w_ref[...], preferred_element_type=F32)

    return pl.pallas_call(
        body, name="mem_kv_fwd", out_shape=jax.ShapeDtypeStruct((MEM_LEN, 2 * C_HEADS * C_DIM), F32),
        compiler_params=_params(),
    )(mem, mem_norm_w, w_kv)


def _mem_kv_bwd(mem, mem_norm_w, w_kv, d_kv):
    def body(mem_ref, nw_ref, w_ref, g_ref, gw_ref, gn_ref):
        mn, vjp = jax.vjp(_rms, mem_ref[...], nw_ref[...])
        g = g_ref[...].astype(BF16)
        gw_ref[...] = lax.dot_general(mn.astype(BF16), g, (((0,), (0,)), ((), ())), preferred_element_type=F32)
        d_mn = lax.dot_general(g, w_ref[...], (((1,), (1,)), ((), ())), preferred_element_type=F32)
        gn_ref[...] = vjp(d_mn)[1]

    return pl.pallas_call(
        body, name="mem_kv_bwd",
        out_shape=[jax.ShapeDtypeStruct((D_MODEL, 2 * C_HEADS * C_DIM), F32), jax.ShapeDtypeStruct((1, D_MODEL), F32)],
        compiler_params=_params(),
    )(mem, mem_norm_w, w_kv, d_kv)


def _c_tile(q_t, z_t, kvm, pc):
    width = C_HEADS * C_DIM
    outs = []
    for h in range(C_HEADS):
        q = _rms(q_t[:, h * C_DIM:(h + 1) * C_DIM], pc[0:1, :])
        k = _rms(kvm[:, h * C_DIM:(h + 1) * C_DIM], pc[1:2, :])
        v = kvm[:, width + h * C_DIM:width + (h + 1) * C_DIM]
        s = _mm_nt(q, k) * (C_DIM ** -0.5)
        p = jnp.exp(s - jnp.max(s, axis=1, keepdims=True))
        p = p / jnp.sum(p, axis=1, keepdims=True)
        outs.append(_mm(p, v))
    return jnp.concatenate(outs, axis=1) * _silu(z_t)


def _attn_c_fwd(proj, kvm, pc, mixed, tq=256):
    s = proj.shape[0]

    def body(q_ref, z_ref, kv_ref, pc_ref, mixed_ref, o_ref):
        o_ref[...] = _c_tile(q_ref[...], z_ref[...], kv_ref[...], pc_ref[...]).astype(BF16)

    return pl.pallas_call(
        body, name="attn_c_fwd", grid=(s // tq,),
        in_specs=[pl.BlockSpec((tq, 512), lambda i: (i, P_QC // 512)), pl.BlockSpec((tq, 512), lambda i: (i, P_ZC // 512)),
                  pl.BlockSpec(kvm.shape, lambda i: (0, 0)), pl.BlockSpec((8, LANE), lambda i: (0, 0)),
                  pl.BlockSpec(memory_space=pl.ANY)],
        out_specs=pl.BlockSpec((tq, 512), lambda i: (i, (A_WIDTH + 512) // 512)),
        out_shape=jax.ShapeDtypeStruct(mixed.shape, mixed.dtype), input_output_aliases={4: 0},
        compiler_params=_params(("parallel",)),
    )(proj, proj, kvm, pc, mixed)


def _attn_c_bwd(proj, kvm, pc, d_mixed, tq=256):
    s = proj.shape[0]

    def body(q_ref, z_ref, kv_ref, pc_ref, dm_ref, dq_ref, dz_ref, dkv_ref, dpc_ref):
        @pl.when(pl.program_id(0) == 0)
        def _():
            dkv_ref[...] = jnp.zeros_like(dkv_ref)
            dpc_ref[...] = jnp.zeros_like(dpc_ref)

        _, vjp = jax.vjp(_c_tile, q_ref[...], z_ref[...], kv_ref[...], pc_ref[...])
        dq, dz, dkv, dpc = vjp(dm_ref[...])
        dq_ref[...] = dq.astype(BF16)
        dz_ref[...] = dz.astype(BF16)
        dkv_ref[...] += dkv
        dpc_ref[...] += dpc

    blk = pl.BlockSpec((tq, 512), lambda i: (i, 0))
    kvs = pl.BlockSpec(kvm.shape, lambda i: (0, 0))
    small = pl.BlockSpec((8, LANE), lambda i: (0, 0))
    return pl.pallas_call(
        body, name="attn_c_bwd", grid=(s // tq,),
        in_specs=[pl.BlockSpec((tq, 512), lambda i: (i, P_QC // 512)), pl.BlockSpec((tq, 512), lambda i: (i, P_ZC // 512)),
                  kvs, small, pl.BlockSpec((tq, 512), lambda i: (i, 3))],
        out_specs=[blk, blk, kvs, small],
        out_shape=[jax.ShapeDtypeStruct((s, 512), BF16), jax.ShapeDtypeStruct((s, 512), BF16),
                   jax.ShapeDtypeStruct(kvm.shape, F32), jax.ShapeDtypeStruct((8, LANE), F32)],
        compiler_params=_params(("arbitrary",)),
    )(proj, proj, kvm, pc, d_mixed)


def _pad_row(v, width=LANE):
    v = v.reshape(1, -1)
    return jnp.pad(v, ((0, 0), (0, width - v.shape[1])))


def _local_step(x, mem, target, norm_w, w_perm_t, w_blocks_t, conv_w, pa, pb, pc, mem_norm_w, w_kv, w_out, gather=None,
                exchange=None):
    s = x.shape[0]
    cos2, sin2 = _rope_tables(s)
    hn = _rms_fwd(x, norm_w)
    wide = dict(tm=1024, tn=512, tk=2048)
    proj = _matmul(hn, w_perm_t, "nt", F32, "mm_proj", **wide)
    cqkv = _conv_fwd(proj, conv_w)
    if gather is None:
        mixed, o_sum, states, inverses = _delta_fwd(cqkv, proj, pa)
    else:
        mixed, o_sum, states, inverses, *arrived = _delta_fwd(cqkv, proj, pa, gather[0])
        w_out, w_kv = gather[1](*arrived)
    mixed = _attn_b_fwd(proj, cos2, sin2, pb, mixed)
    kvm = _mem_kv_fwd(mem, mem_norm_w, w_kv)
    mixed = _attn_c_fwd(proj, kvm, pc, mixed)
    mo = _matmul(mixed, w_out, "nn", F32, "mm_out", **wide)
    dy, dyb, loss_parts = _loss_dy(x, mo, target)

    d_mixed = _matmul(dyb, w_out, "nt", F32, "mm_dmixed", **wide)
    g_w_out = _matmul(mixed, dyb, "tn", F32, "mm_gwout", **wide)
    d_qc, d_zc, d_kvm, d_pc = _attn_c_bwd(proj, kvm, pc, d_mixed)
    g_w_kv, g_mem_norm = _mem_kv_bwd(mem, mem_norm_w, w_kv, d_kvm)
    d_qb, d_zb, d_kb, d_vb, d_pb = _attn_b_bwd(proj, cos2, sin2, pb, d_mixed)
    d_o, d_za, d_pa_out = _delta_out_bwd(o_sum, proj, pa, d_mixed)
    early = exchange[0](g_w_out, g_w_kv) if exchange else None
    d_c, d_gt, d_pa_scan, *landed_early = _delta_bwd(cqkv, proj, pa, d_o, states, inverses, early)
    d_pa = d_pa_out + d_pa_scan
    d_qkv, g_conv = _conv_bwd(proj, conv_w, d_c)
    d_proj = _cotangent_blocks(d_qkv, d_za, d_gt, d_qb, d_kb, d_vb, d_zb, d_qc, d_zc)
    g_w_blocks_t = _matmul(d_proj, hn, "tn", F32, "mm_gwin", tm=512, tn=2048, tk=2048)
    late = exchange[1](g_w_blocks_t) if exchange else None
    g_x, g_norm, *landed_late = _input_grad(d_proj, w_blocks_t, x, norm_w, dy, late)
    return dict(loss_parts=loss_parts, g_x=g_x, g_norm=g_norm, g_w_blocks_t=g_w_blocks_t, g_conv=g_conv, d_pa=d_pa,
                d_pb=d_pb, d_pc=d_pc, g_mem_norm=g_mem_norm, g_w_kv=g_w_kv, g_w_out=g_w_out,
                landed=landed_late + landed_early)


_SEGMENTS = ((0, O_GT, 0), (O_GT, O_QB, P_GT), (O_QB, O_KB, P_QB), (O_KB, O_VB, P_KB), (O_VB, O_ZB, P_VB),
             (O_ZB, O_QC, P_ZB), (O_QC, O_ZC, P_QC), (O_ZC, IN_WIDTH, P_ZC))


def _permute_blocks(w4):
    parts = []
    for first, end, _ in sorted(_SEGMENTS, key=lambda seg: seg[2]):
        row = first
        while row < end:
            k = row // W_IN_BLOCK
            stop = min(end, (k + 1) * W_IN_BLOCK)
            parts.append(w4[k][row - k * W_IN_BLOCK:stop - k * W_IN_BLOCK, :])
            row = stop
    parts.append(jnp.zeros((P_WIDTH - IN_WIDTH, w4.shape[2]), w4.dtype))
    return jnp.concatenate(parts, axis=0)


def _cotangent_blocks(d_qkv, d_za, d_gt, d_qb, d_kb, d_vb, d_zb, d_qc, d_zc):
    s = d_qkv.shape[0]
    tr = min(256, s)
    pieces = (d_qkv, d_za, d_gt, d_qb, d_kb, d_vb, d_zb, d_qc, d_zc)

    def body(*refs):
        o_ref = refs[-1]
        tiles = [r[...].astype(BF16) for r in refs[:-1]]
        tiles[2] = tiles[2][:, :O_QB - O_GT]
        orig = jnp.concatenate(tiles, axis=1)
        pad = jnp.zeros((tr, W_IN_PAD - W_IN_BLOCK), BF16)
        parts = []
        for k in range(N_CHIPS):
            parts += [orig[:, k * W_IN_BLOCK:(k + 1) * W_IN_BLOCK], pad]
        o_ref[...] = jnp.concatenate(parts, axis=1)

    return pl.pallas_call(
        body, name="cotangent_blocks", grid=(s // tr,),
        in_specs=[pl.BlockSpec((tr, p.shape[1]), lambda i: (i, 0)) for p in pieces],
        out_specs=pl.BlockSpec((tr, N_CHIPS * W_IN_PAD), lambda i: (i, 0)),
        out_shape=jax.ShapeDtypeStruct((s, N_CHIPS * W_IN_PAD), BF16), compiler_params=_params(("parallel",)),
    )(*pieces)


HBM = pl.BlockSpec(memory_space=pltpu.HBM)


def _place():
    x, y, c = lax.axis_index("x"), lax.axis_index("y"), lax.axis_index("c")
    chips = [(1 - x, y), (x, 1 - y), (1 - x, 1 - y)]
    return x, y, c, 2 * x + y, chips, [2 * cx + cy for cx, cy in chips]


PIECE_ROWS_CAP = 600


def _remote(src, dst, send_sems, recv_sems, k, to):
    return pltpu.make_async_remote_copy(src_ref=src, dst_ref=dst, send_sem=send_sems.at[k], recv_sem=recv_sems.at[k],
                                        device_id=to, device_id_type=MESH)


def _half_cols(ref, c):
    half = ref.shape[-1] // 2
    return pl.ds(pl.multiple_of(c * half, LANE), half)


class _PairedGather:
    def __init__(self, blocks):
        n = len(blocks)
        self.operands = list(blocks)
        self.out_shapes = [jax.ShapeDtypeStruct((N_CHIPS,) + b.shape, b.dtype) for b in blocks]
        self.scratch_shapes = [pltpu.SemaphoreType.DMA((6 * n,)), pltpu.SemaphoreType.DMA((6 * n,))]

    @staticmethod
    def _copies(srcs, dsts, sems):
        x, y, c, me, chips, chip_ids = _place()
        sends, landed, passes, passed = [], [], [], []
        for a, (src, dst) in enumerate(zip(srcs, dsts)):
            mine, other = _half_cols(src, c), _half_cols(src, 1 - c)
            for j, (chip, cid) in enumerate(zip(chips, chip_ids)):
                sends.append(_remote(src.at[:, mine], dst.at[me, :, mine], sems[0], sems[1], 6 * a + j, (*chip, c)))
                here = dst.at[cid, :, mine]
                landed.append(_remote(here, here, sems[0], sems[1], 6 * a + j, (x, y, 1 - c)))
                passes.append(_remote(here, here, sems[0], sems[1], 6 * a + 3 + j, (x, y, 1 - c)))
                there = dst.at[cid, :, other]
                passed.append(_remote(there, there, sems[0], sems[1], 6 * a + 3 + j, (x, y, 1 - c)))
        return sends, landed, passes, passed

    def start(self, srcs, dsts, sems):
        for cp in self._copies(srcs, dsts, sems)[0]:
            cp.start()

    def middle(self, srcs, dsts, sems):
        _, landed, passes, _ = self._copies(srcs, dsts, sems)
        for arrived, onward in zip(landed, passes):
            arrived.wait_recv()
            onward.start()

    def finish(self, srcs, dsts, sems):
        sends, _, passes, passed = self._copies(srcs, dsts, sems)
        for cp in passed:
            cp.wait_recv()
        for cp in sends + passes:
            cp.wait_send()


def _all_gather_weights(bigs, conv_b):
    bigs = tuple(bigs)
    n_big = len(bigs)

    def body(*refs):
        srcs, conv_src = refs[:n_big], refs[n_big]
        dsts, conv_dst = refs[n_big + 1:2 * n_big + 1], refs[2 * n_big + 1]
        send_sems, recv_sems, local_sems = refs[2 * n_big + 2:]
        x, y, c, me, chips, chip_ids = _place()
        sibling = (x, y, 1 - c)
        local = [pltpu.make_async_copy(src, dst.at[me], local_sems.at[a]) for a, (src, dst) in enumerate(zip(srcs, dsts))]
        local.append(pltpu.make_async_copy(conv_src, conv_dst.at[me], local_sems.at[n_big]))
        for cp in local:
            cp.start()
        sends = []
        for a, (src, dst) in enumerate(zip(srcs, dsts)):
            mine = _half_cols(src, c)
            for j, chip in enumerate(chips):
                sends.append(_remote(src.at[:, mine], dst.at[me, :, mine], send_sems, recv_sems, 6 * a + j, (*chip, c)))
        for j, chip in enumerate(chips):
            sends.append(_remote(conv_src, conv_dst.at[me], send_sems, recv_sems, 6 * n_big + j, (*chip, c)))
        for cp in sends:
            cp.start()
        passed = []
        for a, (src, dst) in enumerate(zip(srcs, dsts)):
            mine = _half_cols(src, c)
            for j, cid in enumerate(chip_ids):
                landed = dst.at[cid, :, mine]
                _remote(landed, landed, send_sems, recv_sems, 6 * a + j, sibling).wait_recv()
                cp = _remote(landed, landed, send_sems, recv_sems, 6 * a + 3 + j, sibling)
                cp.start()
                passed.append(cp)
        for a, (src, dst) in enumerate(zip(srcs, dsts)):
            other = _half_cols(src, 1 - c)
            for j, cid in enumerate(chip_ids):
                landed = dst.at[cid, :, other]
                _remote(landed, landed, send_sems, recv_sems, 6 * a + 3 + j, sibling).wait_recv()
        for j, cid in enumerate(chip_ids):
            _remote(conv_src, conv_dst.at[cid], send_sems, recv_sems, 6 * n_big + j, sibling).wait_recv()
        for cp in sends + passed:
            cp.wait_send()
        for cp in local:
            cp.wait()

    n_sem = 6 * n_big + 3
    return pl.pallas_call(
        body, name="all_gather_weights",
        out_shape=[jax.ShapeDtypeStruct((N_CHIPS,) + w.shape, w.dtype) for w in bigs + (conv_b,)],
        in_specs=[pl.BlockSpec(memory_space=pltpu.VMEM)] * (n_big + 1), out_specs=[HBM] * (n_big + 1),
        scratch_shapes=[pltpu.SemaphoreType.DMA((n_sem,)), pltpu.SemaphoreType.DMA((n_sem,)),
                        pltpu.SemaphoreType.DMA((n_big + 1,))],
        compiler_params=_params(),
    )(*bigs, conv_b)


def _pair_exchange(grads, name):
    n = len(grads)
    pieces = [_row_tile(g.shape[1]) for g in grads]

    def body(*refs):
        srcs, gots = refs[:n], refs[n:2 * n]
        stages = refs[2 * n:3 * n]
        send_sems, recv_sems, load_sems = refs[3 * n:]
        x, y, c, _, _, _ = _place()
        sibling = (x, y, 1 - c)
        for a in range(n):
            slabs, rows, _ = gots[a].shape
            piece = pieces[a]
            per_slab = rows // piece
            theirs = _half_cols(srcs[a], 1 - c)
            loads, sends = [], []
            for i in range(slabs * per_slab):
                k, r, slot = i // per_slab, i % per_slab, i % 2
                part = pl.ds(r * piece, piece)
                loads.append(pltpu.make_async_copy(srcs[a].at[k, part, theirs], stages[a].at[slot], load_sems.at[2 * a + slot]))
                sends.append(pltpu.make_async_remote_copy(
                    src_ref=stages[a].at[slot], dst_ref=gots[a].at[k, part, :],
                    send_sem=send_sems.at[2 * a + slot], recv_sem=recv_sems.at[a], device_id=sibling, device_id_type=MESH))
            loads[0].start()
            for i in range(len(loads)):
                loads[i].wait()
                sends[i].start()
                if i + 1 < len(loads):
                    if i >= 1:
                        sends[i - 1].wait_send()
                    loads[i + 1].start()
            for cp in sends[-2:]:
                cp.wait_send()
        for a in range(n):
            whole = srcs[a].at[:, :, _half_cols(srcs[a], c)]
            pltpu.make_async_remote_copy(src_ref=whole, dst_ref=gots[a], send_sem=send_sems.at[2 * a],
                                         recv_sem=recv_sems.at[a], device_id=sibling, device_id_type=MESH).wait_recv()

    halves = [jax.ShapeDtypeStruct((g.shape[0], g.shape[1], g.shape[2] // 2), g.dtype) for g in grads]
    return pl.pallas_call(
        body, name=name, out_shape=halves, in_specs=[HBM] * n, out_specs=[HBM] * n,
        scratch_shapes=[pltpu.VMEM((2, piece, g.shape[2] // 2), g.dtype) for piece, g in zip(pieces, grads)]
        + [pltpu.SemaphoreType.DMA((2 * n,)), pltpu.SemaphoreType.DMA((n,)), pltpu.SemaphoreType.DMA((2 * n,))],
        compiler_params=_params(),
    )(*grads)


class _ChipExchange:
    def __init__(self, halves):
        n = len(halves)
        self.operands = list(halves)
        self.out_shapes = [jax.ShapeDtypeStruct((N_CHIPS - 1,) + h.shape[1:], h.dtype) for h in halves]
        self.scratch_shapes = [pltpu.SemaphoreType.DMA((3 * n,)), pltpu.SemaphoreType.DMA((3 * n,))]

    @staticmethod
    def _copies(srcs, lands, sems):
        _, _, c, _, chips, chip_ids = _place()
        return [_remote(src.at[cid], land.at[j], sems[0], sems[1], 3 * a + j, (*chip, c))
                for a, (src, land) in enumerate(zip(srcs, lands)) for j, (chip, cid) in enumerate(zip(chips, chip_ids))]

    def start(self, srcs, lands, sems):
        for cp in self._copies(srcs, lands, sems):
            cp.start()

    def finish(self, srcs, lands, sems):
        copies = self._copies(srcs, lands, sems)
        for cp in copies:
            cp.wait_recv()
        for cp in copies:
            cp.wait_send()


def _pair_gather(halves, rows):
    n = len(halves)

    def body(*refs):
        srcs, fulls = refs[:n], refs[n:2 * n]
        send_sems, recv_sems, local_sems = refs[2 * n:]
        x, y, c, _, _, _ = _place()
        copies = []
        for a in range(n):
            mine, src = _half_cols(fulls[a], c), srcs[a].at[pl.ds(0, rows[a]), :]
            keep = pltpu.make_async_copy(src, fulls[a].at[:, mine], local_sems.at[a])
            keep.start()
            give = _remote(src, fulls[a].at[:, mine], send_sems, recv_sems, a, (x, y, 1 - c))
            give.start()
            copies += [keep, give]
        for a in range(n):
            other, src = _half_cols(fulls[a], 1 - c), srcs[a].at[pl.ds(0, rows[a]), :]
            copies[2 * a].wait()
            copies[2 * a + 1].wait_send()
            _remote(src, fulls[a].at[:, other], send_sems, recv_sems, a, (x, y, 1 - c)).wait_recv()

    return pl.pallas_call(
        body, name="grad_pair_gather",
        out_shape=[jax.ShapeDtypeStruct((r, 2 * h.shape[1]), h.dtype) for r, h in zip(rows, halves)],
        in_specs=[pl.BlockSpec(memory_space=pltpu.VMEM)] * n, out_specs=[HBM] * n,
        scratch_shapes=[pltpu.SemaphoreType.DMA((n,)), pltpu.SemaphoreType.DMA((n,)), pltpu.SemaphoreType.DMA((n,))],
    )(*halves)


def _all_reduce_small(p):
    n_dev = 8

    def body(p_ref, o_ref, land, send_sems, recv_sems):
        x, y, c = lax.axis_index("x"), lax.axis_index("y"), lax.axis_index("c")
        me = 4 * x + 2 * y + c
        land[me] = p_ref[...]
        sends = []
        for k in range(1, n_dev):
            fx, fy, fc = (k >> 2) & 1, (k >> 1) & 1, k & 1
            to = (x ^ fx, y ^ fy, c ^ fc)
            cp = _remote(p_ref, land.at[me], send_sems, recv_sems, k - 1, to)
            cp.start()
            sends.append(cp)
        for k in range(1, n_dev):
            _remote(p_ref, land.at[me ^ k], send_sems, recv_sems, k - 1, (x, y, c)).wait_recv()
        total = land[0]
        for d in range(1, n_dev):
            total = total + land[d]
        o_ref[...] = total
        for cp in sends:
            cp.wait_send()

    vm = pl.BlockSpec(memory_space=pltpu.VMEM)
    return pl.pallas_call(
        body, name="all_reduce_small", out_shape=jax.ShapeDtypeStruct(p.shape, p.dtype), in_specs=[vm], out_specs=vm,
        scratch_shapes=[pltpu.VMEM((n_dev,) + p.shape, p.dtype), pltpu.SemaphoreType.DMA((n_dev - 1,)),
                        pltpu.SemaphoreType.DMA((n_dev - 1,))],
    )(p)


def _row_tile(rows):
    fits = [t for t in range(8, min(rows, PIECE_ROWS_CAP) + 1, 8) if rows % t == 0]
    return max(fits) if fits else rows


def _pair_sum(full, got, core, name):
    n, r, c = got.shape
    tr = _row_tile(r)

    def body(core_ref, a_ref, b_ref, o_ref):
        o_ref[...] = (a_ref[...] + b_ref[...]).astype(BF16)

    blk = pl.BlockSpec((None, tr, c), lambda i, j, core_ref: (i, j, 0))
    grid_spec = pltpu.PrefetchScalarGridSpec(
        num_scalar_prefetch=1, grid=(n, r // tr),
        in_specs=[pl.BlockSpec((None, tr, c), lambda i, j, core_ref: (i, j, core_ref[0])), blk], out_specs=blk)
    return pl.pallas_call(body, name=name, grid_spec=grid_spec, out_shape=jax.ShapeDtypeStruct(got.shape, BF16),
                          compiler_params=_params(("parallel", "parallel")))(core, full, got)


def _chip_sum(full, got, land, place, name):
    n, r, c = land.shape
    tr = _row_tile(r)

    def body(place_ref, a_ref, b_ref, l_ref, o_ref):
        total = a_ref[...] + b_ref[...]
        for j in range(n):
            total = total + l_ref[j].astype(F32)
        o_ref[...] = total

    grid_spec = pltpu.PrefetchScalarGridSpec(
        num_scalar_prefetch=1, grid=(r // tr,),
        in_specs=[pl.BlockSpec((None, tr, c), lambda i, p: (p[0], i, p[1])),
                  pl.BlockSpec((None, tr, c), lambda i, p: (p[0], i, 0)),
                  pl.BlockSpec((n, tr, c), lambda i, p: (0, i, 0))],
        out_specs=pl.BlockSpec((tr, c), lambda i, p: (i, 0)))
    return pl.pallas_call(body, name=name, grid_spec=grid_spec, out_shape=jax.ShapeDtypeStruct((r, c), F32),
                          compiler_params=_params(("parallel",)))(place, full, got, land)


def _adamw(w, g, m, v, name, echo=False):
    r, c = w.shape
    tr = _row_tile(r)
    tc = 1024 if c % 1024 == 0 else c

    def body(w_ref, g_ref, m_ref, v_ref, d_ref, mo_ref, vo_ref, *g_out):
        g_ = g_ref[...]
        for o in g_out:
            o[...] = g_
        m2 = ADAM_B1 * m_ref[...] + (1.0 - ADAM_B1) * g_
        v2 = ADAM_B2 * v_ref[...] + (1.0 - ADAM_B2) * jnp.square(g_)
        m_hat = m2 / (1.0 - ADAM_B1 ** ADAM_STEP)
        v_hat = v2 / (1.0 - ADAM_B2 ** ADAM_STEP)
        d_ref[...] = -ADAM_LR * (m_hat / (jnp.sqrt(v_hat) + ADAM_EPS) + ADAM_WD * w_ref[...])
        mo_ref[...] = m2
        vo_ref[...] = v2

    blk = pl.BlockSpec((tr, tc), lambda i, j: (i, j))
    n_out = 4 if echo else 3
    return pl.pallas_call(body, name=name, grid=(r // tr, c // tc), in_specs=[blk] * 4, out_specs=[blk] * n_out,
                          out_shape=[jax.ShapeDtypeStruct(w.shape, F32)] * n_out,
                          compiler_params=_params(("parallel", "parallel")))(w, g, m, v)


SMALL_NAMES = ("norm_w", "mem_norm_w", "o_norm_a", "q_norm_c", "k_norm_c", "q_norm_b", "k_norm_b",
               "a_log_fwd", "a_log_bwd", "dt_bias_fwd", "dt_bias_bwd", "sink_b")
SMALL_SIZES = (2048, 2048, 128, 128, 128, 64, 64, 8, 8, 8, 8, 8)
SMALL_LOSS = sum(SMALL_SIZES)
SMALL_CONV = 5120
SMALL_TOTAL = SMALL_CONV + CONV_K * 3 * A_WIDTH
SMALL_ROWS = SMALL_TOTAL // LANE


def _pack_small(parts, extra=None, conv=None):
    vec = [parts[n].reshape(-1) for n in SMALL_NAMES]
    vec.append(jnp.zeros((1,), F32) if extra is None else extra.reshape(1))
    vec.append(jnp.zeros((SMALL_CONV - SMALL_LOSS - 1,), F32))
    vec.append(jnp.zeros((SMALL_TOTAL - SMALL_CONV,), F32) if conv is None else conv.reshape(-1))
    return jnp.concatenate(vec).reshape(SMALL_ROWS, LANE)


def _unpack_small(packed):
    flat = packed.reshape(-1)
    out, off = {}, 0
    for n, size in zip(SMALL_NAMES, SMALL_SIZES):
        out[n] = flat[off:off + size].reshape(1, size)
        off += size
    return out


WEIGHT_ORDER = ("norm_w", "w_in", "conv_w_a", "a_log_fwd", "a_log_bwd", "dt_bias_fwd", "dt_bias_bwd", "o_norm_a",
                "q_norm_b", "k_norm_b", "sink_b", "mem_norm_w", "w_mem_kv", "q_norm_c", "k_norm_c", "w_out")


def kernel(x, mem, norm_w, w_in, conv_w_a, a_log_fwd, a_log_bwd, dt_bias_fwd, dt_bias_bwd, o_norm_a, q_norm_b, k_norm_b, sink_b, mem_norm_w, w_mem_kv, q_norm_c, k_norm_c, w_out, loss_target, m_norm_w, m_w_in, m_conv_w_a, m_a_log_fwd, m_a_log_bwd, m_dt_bias_fwd, m_dt_bias_bwd, m_o_norm_a, m_q_norm_b, m_k_norm_b, m_sink_b, m_mem_norm_w, m_w_mem_kv, m_q_norm_c, m_k_norm_c, m_w_out, v_norm_w, v_w_in, v_conv_w_a, v_a_log_fwd, v_a_log_bwd, v_dt_bias_fwd, v_dt_bias_bwd, v_o_norm_a, v_q_norm_b, v_k_norm_b, v_sink_b, v_mem_norm_w, v_w_mem_kv, v_q_norm_c, v_k_norm_c, v_w_out):
    weights = dict(norm_w=norm_w, w_in=w_in, conv_w_a=conv_w_a, a_log_fwd=a_log_fwd, a_log_bwd=a_log_bwd,
                   dt_bias_fwd=dt_bias_fwd, dt_bias_bwd=dt_bias_bwd, o_norm_a=o_norm_a, q_norm_b=q_norm_b,
                   k_norm_b=k_norm_b, sink_b=sink_b, mem_norm_w=mem_norm_w, w_mem_kv=w_mem_kv, q_norm_c=q_norm_c,
                   k_norm_c=k_norm_c, w_out=w_out)
    mom1 = dict(norm_w=m_norm_w, w_in=m_w_in, conv_w_a=m_conv_w_a, a_log_fwd=m_a_log_fwd, a_log_bwd=m_a_log_bwd,
                dt_bias_fwd=m_dt_bias_fwd, dt_bias_bwd=m_dt_bias_bwd, o_norm_a=m_o_norm_a, q_norm_b=m_q_norm_b,
                k_norm_b=m_k_norm_b, sink_b=m_sink_b, mem_norm_w=m_mem_norm_w, w_mem_kv=m_w_mem_kv,
                q_norm_c=m_q_norm_c, k_norm_c=m_k_norm_c, w_out=m_w_out)
    mom2 = dict(norm_w=v_norm_w, w_in=v_w_in, conv_w_a=v_conv_w_a, a_log_fwd=v_a_log_fwd, a_log_bwd=v_a_log_bwd,
                dt_bias_fwd=v_dt_bias_fwd, dt_bias_bwd=v_dt_bias_bwd, o_norm_a=v_o_norm_a, q_norm_b=v_q_norm_b,
                k_norm_b=v_k_norm_b, sink_b=v_sink_b, mem_norm_w=v_mem_norm_w, w_mem_kv=v_w_mem_kv,
                q_norm_c=v_q_norm_c, k_norm_c=v_k_norm_c, w_out=v_w_out)
    chip = 2 * lax.axis_index("x") + lax.axis_index("y")

    own_in = jnp.pad(jnp.transpose(w_in[0]).astype(BF16), ((0, W_IN_PAD - W_IN_BLOCK), (0, 0)))
    w_in4, conv4 = _all_gather_weights([own_in], conv_w_a[0])
    w_perm_t = _permute_blocks(w_in4)
    w_blocks_t = w_in4.reshape(N_CHIPS * W_IN_PAD, D_MODEL)
    conv_full = jnp.transpose(conv4, (1, 0, 2)).reshape(CONV_K, 3 * A_WIDTH)
    own_out, own_kv = w_out[0].astype(BF16), w_mem_kv[0].astype(BF16)

    def assemble(w_out4, w_kv4):
        w_out4 = lax.dynamic_update_index_in_dim(w_out4, own_out, chip, 0)
        w_kv4 = lax.dynamic_update_index_in_dim(w_kv4, own_kv, chip, 0)
        return w_out4.reshape(D_MODEL, D_MODEL), w_kv4.reshape(D_MODEL, 2 * C_HEADS * C_DIM)

    gather = (_PairedGather([own_out, own_kv]), assemble)
    pa = jnp.concatenate([_pad_row(a_log_fwd), _pad_row(a_log_bwd), _pad_row(dt_bias_fwd), _pad_row(dt_bias_bwd),
                          _pad_row(o_norm_a), jnp.zeros((3, LANE), F32)], axis=0)
    pb = jnp.concatenate([_pad_row(q_norm_b), _pad_row(k_norm_b), _pad_row(sink_b), jnp.zeros((5, LANE), F32)], axis=0)
    pc = jnp.concatenate([_pad_row(q_norm_c), _pad_row(k_norm_c), jnp.zeros((6, LANE), F32)], axis=0)

    full, got = {}, {}
    core = lax.axis_index("c").astype(jnp.int32).reshape(1)

    def pair_round(tag, blocks):
        names = [tag + "_%d" % i for i in range(len(blocks))]
        full.update(zip(names, blocks))
        got.update(zip(names, _pair_exchange(blocks, "grad_pair_exchange_" + tag)))
        return _ChipExchange([_pair_sum(full[n], got[n], core, "grad_pair_sum_" + n) for n in names])

    def early(g_w_out, g_w_kv):
        return pair_round("early", [g_w_out.reshape(N_CHIPS, D_MODEL // N_CHIPS, D_MODEL),
                                    g_w_kv.reshape(N_CHIPS, D_MODEL // N_CHIPS, 2 * C_HEADS * C_DIM)])

    def late(g_w_blocks_t):
        return pair_round("late", [g_w_blocks_t.reshape(N_CHIPS, W_IN_PAD, D_MODEL)])

    r = _local_step(x[0], mem[0], loss_target[0], norm_w, w_perm_t, w_blocks_t, conv_full, pa, pb, pc, mem_norm_w, None, None,
                    gather, (early, late))
    place = jnp.stack([chip, lax.axis_index("c")]).astype(jnp.int32)
    reduced = [_chip_sum(full[n], got[n], l, place, "grad_chip_sum_" + n)
               for n, l in zip(("late_0", "early_0", "early_1"), r["landed"])]
    g_w_in_t, g_w_out, g_w_kv = _pair_gather(reduced, [W_IN_BLOCK, D_MODEL // N_CHIPS, D_MODEL // N_CHIPS])

    d_pa, d_pb, d_pc = r["d_pa"], r["d_pb"], r["d_pc"]
    small_g = dict(norm_w=r["g_norm"], mem_norm_w=r["g_mem_norm"], o_norm_a=d_pa[4], q_norm_c=d_pc[0], k_norm_c=d_pc[1],
                   q_norm_b=d_pb[0, :B_DIM], k_norm_b=d_pb[1, :B_DIM], a_log_fwd=d_pa[0, :A_HEADS],
                   a_log_bwd=d_pa[1, :A_HEADS], dt_bias_fwd=d_pa[2, :A_HEADS], dt_bias_bwd=d_pa[3, :A_HEADS],
                   sink_b=d_pb[2, :B_HEADS])
    packed = _all_reduce_small(_pack_small(small_g, jnp.sum(r["loss_parts"][:, 0, 0]), r["g_conv"]))
    flat = packed.reshape(-1)
    loss = flat[SMALL_LOSS]
    conv_sum = flat[SMALL_CONV:].reshape(CONV_K, 3 * A_WIDTH)
    conv_cols = 3 * A_WIDTH // N_CHIPS
    g_conv = lax.dynamic_slice(conv_sum, (0, chip * conv_cols), (CONV_K, conv_cols))

    grads = _unpack_small(packed)
    grads["conv_w_a"] = g_conv
    delta, new_m, new_v = {}, {}, {}
    delta["conv_w_a"], new_m["conv_w_a"], new_v["conv_w_a"] = _adamw(conv_w_a[0], g_conv, m_conv_w_a[0], v_conv_w_a[0],
                                                                     "adamw_conv_w_a")
    for n, g in (("w_mem_kv", g_w_kv), ("w_out", g_w_out)):
        delta[n], new_m[n], new_v[n], grads[n] = _adamw(weights[n][0], g, mom1[n][0], mom2[n][0], "adamw_" + n, echo=True)
    stepped = _adamw(jnp.transpose(w_in[0]), g_w_in_t, jnp.transpose(m_w_in[0]), jnp.transpose(v_w_in[0]), "adamw_w_in",
                     echo=True)
    delta["w_in"], new_m["w_in"], new_v["w_in"], grads["w_in"] = (jnp.transpose(t) for t in stepped)
    d_s, m_s, v_s = _adamw(_pack_small(weights), packed, _pack_small(mom1), _pack_small(mom2), "adamw_small")
    d_s, m_s, v_s = _unpack_small(d_s), _unpack_small(m_s), _unpack_small(v_s)
    for n in SMALL_NAMES:
        delta[n], new_m[n], new_v[n] = d_s[n], m_s[n], v_s[n]

    def shaped(tree):
        return [tree[n].reshape(weights[n].shape) for n in WEIGHT_ORDER]

    return (loss, r["g_x"].reshape(x.shape), *shaped(grads), *shaped(delta), *shaped(new_m), *shaped(new_v))
```

```python
import jax
import jax.numpy as jnp
from jax import lax
from jax.experimental import pallas as pl
from jax.experimental.pallas import tpu as pltpu

F32 = jnp.float32
BF16 = jnp.bfloat16
HI = lax.Precision.HIGHEST
MESH = pl.DeviceIdType.MESH

D_MODEL = 2048
A_WIDTH = 1024
A_HEADS = 8
A_DIM = 128
CONV_K = 5
CHUNK = 64
B_HEADS = 8
B_KV = 2
B_DIM = 64
WINDOW = 128
C_HEADS = 4
C_DIM = 128
MEM_LEN = 256
ROPE_THETA = 10000.0
EPS = 1e-6
IN_WIDTH = 6432
N_CHIPS = 4
W_IN_BLOCK = IN_WIDTH // N_CHIPS
W_IN_PAD = 1664

LANE = 128
P_QA, P_KA, P_VA, P_ZA = 0, 1024, 2048, 3072
P_QB, P_ZB, P_QC, P_ZC = 4096, 4608, 5120, 5632
P_KB, P_VB, P_GT = 6144, 6272, 6400
P_WIDTH = 6656
O_GT, O_QB, O_KB, O_VB, O_ZB, O_QC, O_ZC = 4096, 4128, 4640, 4768, 4896, 5408, 5920

ADAM_LR, ADAM_B1, ADAM_B2, ADAM_EPS, ADAM_WD, ADAM_STEP = 0.001, 0.9, 0.999, 1e-08, 0.01, 10

VMEM_LIMIT = 56 * 1024 * 1024


def _params(sem=None):
    return pltpu.CompilerParams(dimension_semantics=sem, vmem_limit_bytes=VMEM_LIMIT)


def _dot(a, b, dims=(((1,), (0,)), ((), ())), precision=HI):
    return lax.dot_general(a, b, dims, precision=precision, preferred_element_type=F32)


_NN = (((1,), (0,)), ((), ()))
_NT = (((1,), (1,)), ((), ()))
_TN = (((0,), (0,)), ((), ()))


def _bdot(a, b, dims):
    return lax.dot_general(a.astype(BF16), b.astype(BF16), dims, preferred_element_type=F32)


@jax.custom_vjp
def _mm(a, b):
    return _bdot(a, b, _NN)


_mm.defvjp(lambda a, b: (_bdot(a, b, _NN), (a, b)),
           lambda res, ct: (_bdot(ct, res[1], _NT), _bdot(res[0], ct, _TN)))


@jax.custom_vjp
def _mm_nt(a, b):
    return _bdot(a, b, _NT)


_mm_nt.defvjp(lambda a, b: (_bdot(a, b, _NT), (a, b)),
              lambda res, ct: (_bdot(ct, res[1], _NN), _bdot(ct, res[0], _TN)))


@jax.custom_vjp
def _mm_tn(a, b):
    return _bdot(a, b, _TN)


_mm_tn.defvjp(lambda a, b: (_bdot(a, b, _TN), (a, b)),
              lambda res, ct: (_bdot(res[1], ct, _NT), _bdot(res[0], ct, _NN)))


def _rms(t, w):
    return t * lax.rsqrt(jnp.mean(t * t, axis=-1, keepdims=True) + EPS) * w


def _l2(t):
    return t * lax.rsqrt(jnp.sum(t * t, axis=-1, keepdims=True) + EPS)


def _silu(t):
    return t * jax.nn.sigmoid(t)


def _softplus(t):
    return jnp.maximum(t, 0.0) + jnp.log1p(jnp.exp(-jnp.abs(t)))


def _matmul(a, b, mode, out_dtype, name, tm=512, tn=512, tk=512, ride=None):
    (m, k) = a.shape[::-1] if mode == "tn" else a.shape
    n = b.shape[0] if mode == "nt" else b.shape[1]
    tm, tn, tk = min(tm, m), min(tn, n), min(tk, k)
    assert m % tm == 0 and n % tn == 0 and k % tk == 0, (m, n, k, tm, tn, tk)
    if mode == "nn":
        a_spec = pl.BlockSpec((tm, tk), lambda i, j, kk: (i, kk))
        b_spec = pl.BlockSpec((tk, tn), lambda i, j, kk: (kk, j))
        dims = (((1,), (0,)), ((), ()))
    elif mode == "nt":
        a_spec = pl.BlockSpec((tm, tk), lambda i, j, kk: (i, kk))
        b_spec = pl.BlockSpec((tn, tk), lambda i, j, kk: (j, kk))
        dims = (((1,), (1,)), ((), ()))
    else:
        a_spec = pl.BlockSpec((tk, tm), lambda i, j, kk: (kk, i))
        b_spec = pl.BlockSpec((tk, tn), lambda i, j, kk: (kk, j))
        dims = (((0,), (0,)), ((), ()))
    nk = k // tk
    grid = (m // tm, n // tn, nk)
    n_in = len(ride.operands) if ride else 0
    n_out = len(ride.out_shapes) if ride else 0

    def body(*refs):
        a_ref, b_ref, o_ref = refs[0], refs[1], refs[2 + n_in]
        scratch = refs[3 + n_in + n_out:]
        step = (pl.program_id(0) * grid[1] + pl.program_id(1)) * nk + pl.program_id(2)
        riders = (refs[2:2 + n_in], refs[3 + n_in:3 + n_in + n_out], scratch[(0 if nk == 1 else 1):])
        if ride:
            pl.when(step == 0)(lambda: ride.start(*riders))
        if nk == 1:
            o_ref[...] = _bdot(a_ref[...], b_ref[...], dims).astype(out_dtype)
        else:
            acc_ref, kk = scratch[0], pl.program_id(2)

            @pl.when(kk == 0)
            def _():
                acc_ref[...] = jnp.zeros_like(acc_ref)

            acc_ref[...] += _bdot(a_ref[...], b_ref[...], dims)

            @pl.when(kk == nk - 1)
            def _():
                o_ref[...] = acc_ref[...].astype(out_dtype)
        if ride:
            pl.when(step == grid[0] * grid[1] * nk - 1)(lambda: ride.finish(*riders))

    out = pl.pallas_call(
        body, name=name, grid=grid,
        in_specs=[a_spec, b_spec] + [HBM] * n_in,
        out_specs=[pl.BlockSpec((tm, tn), lambda i, j, kk: (i, j))] + [HBM] * n_out,
        out_shape=[jax.ShapeDtypeStruct((m, n), out_dtype)] + (list(ride.out_shapes) if ride else []),
        scratch_shapes=([] if nk == 1 else [pltpu.VMEM((tm, tn), F32)]) + (list(ride.scratch_shapes) if ride else []),
        compiler_params=_params(("arbitrary",) * 3 if ride else ("parallel", "parallel", "arbitrary")),
    )(a, b, *(ride.operands if ride else []))
    return out if ride else out[0]


def _rms_fwd(x, w, tr=256):
    s, d = x.shape

    def body(x_ref, w_ref, o_ref):
        o_ref[...] = _rms(x_ref[...], w_ref[...]).astype(BF16)

    return pl.pallas_call(
        body, name="rms_fwd", grid=(s // tr,),
        in_specs=[pl.BlockSpec((tr, d), lambda i: (i, 0)), pl.BlockSpec((1, d), lambda i: (0, 0))],
        out_specs=pl.BlockSpec((tr, d), lambda i: (i, 0)),
        out_shape=jax.ShapeDtypeStruct((s, d), BF16), compiler_params=_params(("parallel",)),
    )(x, w)


def _input_grad(d_proj, w_t, x, w, dy, ride=None, tm=512, tk=512):
    s, k = d_proj.shape
    d = w_t.shape[1]
    tm = min(tm, s)
    nk = k // tk
    grid = (s // tm, nk)
    n_in = len(ride.operands) if ride else 0
    n_out = len(ride.out_shapes) if ride else 0

    def body(*refs):
        a_ref, b_ref, x_ref, w_ref, dy_ref = refs[:5]
        gx_ref, gw_ref = refs[5 + n_in:7 + n_in]
        acc_ref = refs[7 + n_in + n_out]
        riders = (refs[5:5 + n_in], refs[7 + n_in:7 + n_in + n_out], refs[8 + n_in + n_out:])
        kk = pl.program_id(1)
        step = pl.program_id(0) * nk + kk
        if ride:
            pl.when(step == 0)(lambda: ride.start(*riders))

        @pl.when(step == 0)
        def _():
            gw_ref[...] = jnp.zeros_like(gw_ref)

        @pl.when(kk == 0)
        def _():
            acc_ref[...] = jnp.zeros_like(acc_ref)

        acc_ref[...] += _bdot(a_ref[...], b_ref[...], _NN)

        @pl.when(kk == nk - 1)
        def _():
            _, vjp = jax.vjp(_rms, x_ref[...], w_ref[...])
            dx, dw = vjp(acc_ref[...])
            gx_ref[...] = dy_ref[...] + dx
            gw_ref[...] += dw

        if ride:
            pl.when(step == grid[0] * nk - 1)(lambda: ride.finish(*riders))

    row = pl.BlockSpec((tm, d), lambda i, kk: (i, 0))
    vec = pl.BlockSpec((1, d), lambda i, kk: (0, 0))
    return pl.pallas_call(
        body, name="input_grad", grid=grid,
        in_specs=[pl.BlockSpec((tm, tk), lambda i, kk: (i, kk)), pl.BlockSpec((tk, d), lambda i, kk: (kk, 0)), row, vec, row]
        + [HBM] * n_in,
        out_specs=[row, vec] + [HBM] * n_out,
        out_shape=[jax.ShapeDtypeStruct((s, d), F32), jax.ShapeDtypeStruct((1, d), F32)]
        + (list(ride.out_shapes) if ride else []),
        scratch_shapes=[pltpu.VMEM((tm, d), F32)] + (list(ride.scratch_shapes) if ride else []),
        compiler_params=_params(("arbitrary", "arbitrary")),
    )(d_proj, w_t, x, w, dy, *(ride.operands if ride else []))


def _out_loss(mixed, w_out, x, target, tm=1024, tn=512):
    s, d = x.shape
    tm = min(tm, s)
    ni, nj = s // tm, d // tn

    def body(m_ref, w_ref, x_ref, t_ref, dy_ref, dyb_ref, l_ref):
        err = x_ref[...] + _bdot(m_ref[...], w_ref[...], _NN) - t_ref[...]
        dy = err * (1.0 / d)
        dy_ref[...] = dy
        dyb_ref[...] = dy.astype(BF16)
        l_ref[...] = jnp.full(l_ref.shape, 0.5 * jnp.sum(jnp.sum(err * err, axis=1, keepdims=True) * (1.0 / d)), F32)

    tile = pl.BlockSpec((tm, tn), lambda i, j: (i, j))
    return pl.pallas_call(
        body, name="out_loss", grid=(ni, nj),
        in_specs=[pl.BlockSpec((tm, mixed.shape[1]), lambda i, j: (i, 0)),
                  pl.BlockSpec((mixed.shape[1], tn), lambda i, j: (0, j)), tile, tile],
        out_specs=[tile, tile, pl.BlockSpec((1, 8, LANE), lambda i, j: (i * nj + j, 0, 0))],
        out_shape=[jax.ShapeDtypeStruct((s, d), F32), jax.ShapeDtypeStruct((s, d), BF16),
                   jax.ShapeDtypeStruct((ni * nj, 8, LANE), F32)],
        compiler_params=_params(("parallel", "parallel")),
    )(mixed, w_out, x, target)


def _shift_rows(t, s):
    if s == 0:
        return t
    n = t.shape[0]
    rolled = pltpu.roll(t, (-s) % n, axis=0)
    idx = lax.broadcasted_iota(jnp.int32, t.shape, 0) + s
    return jnp.where((idx >= 0) & (idx < n), rolled, 0.0)


CONV_FWD_COLS = 512
CONV_BWD_COLS = 128


def _conv_fwd(proj, conv_w):
    s = proj.shape[0]
    cols, split = CONV_FWD_COLS, A_WIDTH // CONV_FWD_COLS
    nblk = 3 * A_WIDTH // cols

    def body(x_ref, w_ref, o_ref):
        x = x_ref[...]
        acc = jnp.zeros_like(x)
        for j in range(CONV_K):
            acc = acc + w_ref[j:j + 1, :] * _shift_rows(x, j - CONV_K // 2)
        o_ref[...] = acc

    return pl.pallas_call(
        body, name="conv_fwd", grid=(nblk,),
        in_specs=[pl.BlockSpec((s, cols), lambda i: (0, i)), pl.BlockSpec((CONV_K, cols), lambda i: (0, i))],
        out_specs=pl.BlockSpec((None, s, cols), lambda i: (i // split, 0, i % split)),
        out_shape=jax.ShapeDtypeStruct((3, s, A_WIDTH), F32), compiler_params=_params(("parallel",)),
    )(proj, conv_w)


def _conv_bwd(proj, conv_w, d_c):
    s = proj.shape[0]
    cols, split = CONV_BWD_COLS, A_WIDTH // CONV_BWD_COLS
    nblk = 3 * A_WIDTH // cols

    def body(x_ref, w_ref, g_ref, dx_ref, dw_ref):
        x, g = x_ref[...], g_ref[...]
        acc = jnp.zeros_like(x)
        for j in range(CONV_K):
            off = j - CONV_K // 2
            acc = acc + w_ref[j:j + 1, :] * _shift_rows(g, -off)
            dw_ref[j:j + 1, :] = jnp.sum(_shift_rows(x, off) * g, axis=0, keepdims=True)
        dx_ref[...] = acc.astype(BF16)

    col = pl.BlockSpec((s, cols), lambda i: (0, i))
    wsp = pl.BlockSpec((CONV_K, cols), lambda i: (0, i))
    dsp = pl.BlockSpec((None, s, cols), lambda i: (i // split, 0, i % split))
    return pl.pallas_call(
        body, name="conv_bwd", grid=(nblk,), in_specs=[col, wsp, dsp], out_specs=[col, wsp],
        out_shape=[jax.ShapeDtypeStruct((s, 3 * A_WIDTH), BF16), jax.ShapeDtypeStruct((CONV_K, 3 * A_WIDTH), F32)],
        compiler_params=_params(("parallel",)),
    )(proj, conv_w, d_c)


A_FWD_HEADS = 4
A_BWD_HEADS = 4


def _neumann_inverse(a):
    c = a.shape[-1]
    eye = (lax.broadcasted_iota(jnp.int32, (c, c), 0) == lax.broadcasted_iota(jnp.int32, (c, c), 1)).astype(F32)
    tinv = eye + a
    p = a
    for _ in range(5):
        p = _mm(p, p)
        tinv = tinv + _mm(tinv, p)
    return tinv


@jax.custom_vjp
def _unit_inverse(a):
    return _neumann_inverse(a)


def _unit_inverse_fwd(a):
    tinv = _neumann_inverse(a)
    return tinv, tinv


def _unit_inverse_bwd(tinv, ct):
    return (_bdot(_bdot(tinv, ct, _TN), tinv, _NT),)


_unit_inverse.defvjp(_unit_inverse_fwd, _unit_inverse_bwd)


@jax.custom_vjp
def _known_inverse(a, tinv):
    return tinv


_known_inverse.defvjp(lambda a, tinv: (tinv, tinv),
                      lambda tinv, ct: (_unit_inverse_bwd(tinv, ct)[0], jnp.zeros_like(tinv)))


def _a_chain(st, cq, ck, cv, alpha, beta_raw, a_log, dt_b, incl, strict, last, kept=None):
    c = CHUNK
    gb = -jnp.exp(a_log) * _softplus(alpha + dt_b)
    bb = jax.nn.sigmoid(beta_raw)
    q = _l2(_silu(cq)) * (A_DIM ** -0.5)
    k = _l2(_silu(ck))
    v = _silu(cv)

    gc = _dot(incl, jnp.broadcast_to(gb, (c, LANE)))
    tot = jnp.sum(gc * last, axis=0, keepdims=True)
    m1 = gc[:, :c]
    decay = incl * jnp.exp(incl * (m1 - m1.T))
    kb = k * bb
    vb = v * bb
    a = -(strict * decay * _mm_nt(kb, k))
    tinv = _unit_inverse(a) if kept is None else _known_inverse(a, kept)
    eg = jnp.exp(gc)
    u = _mm(tinv, vb)
    w = _mm(tinv, kb * eg)
    qk = _mm_nt(q, k) * decay
    v_new = u - _mm(w, st)
    o = _mm(q * eg, st) + _mm(qk, v_new)
    st_new = st * jnp.exp(tot) + _mm_tn(k * jnp.exp(tot - gc), v_new)
    return st_new, o, tinv


def _a_step(sts, cq, ck, cv, gts, pa, h0, kept=None):
    c = CHUNK
    lane = lax.broadcasted_iota(jnp.int32, (1, LANE), 1)
    ii = lax.broadcasted_iota(jnp.int32, (c, c), 0)
    jj = lax.broadcasted_iota(jnp.int32, (c, c), 1)
    row = lax.broadcasted_iota(jnp.int32, (c, 1), 0)

    def pick(t, col):
        return jnp.sum(jnp.where(lane == col, t, 0.0), axis=1, keepdims=True)

    alpha, beta_raw, a_log, dt_b, incl, strict, last = [], [], [], [], [], [], []
    for b in range(sts.shape[0]):
        h, rev = h0 + b // 2, b % 2
        alpha.append(pick(gts[b], h + 8 * rev))
        beta_raw.append(pick(gts[b], h + 16 + 8 * rev))
        a_log.append(pick(pa[rev:rev + 1, :], h))
        dt_b.append(pick(pa[2 + rev:3 + rev, :], h))
        incl.append(((ii <= jj) if rev else (ii >= jj)).astype(F32))
        strict.append(((ii < jj) if rev else (ii > jj)).astype(F32))
        last.append((row == (0 if rev else c - 1)).astype(F32))
    stack = lambda ts: jnp.concatenate([t[None] for t in ts], axis=0)
    return jax.vmap(_a_chain)(sts, cq, ck, cv, stack(alpha), stack(beta_raw), stack(a_log), stack(dt_b),
                              stack(incl), stack(strict), stack(last), kept)


def _a_final(o, za, pa):
    outs = []
    for j in range(o.shape[1] // A_DIM):
        ln = slice(j * A_DIM, (j + 1) * A_DIM)
        outs.append(_rms(o[:, ln], pa[4:5, :]) * _silu(za[:, ln]))
    return jnp.concatenate(outs, axis=1)


def _a_tiles(n, nchunk, heads):
    tiles = []
    for b in range(2 * heads):
        i = (nchunk - 1 - n) if b % 2 else n
        tiles.append((i, pl.ds(pl.multiple_of(i * CHUNK, CHUNK), CHUNK), slice((b // 2) * A_DIM, (b // 2 + 1) * A_DIM)))
    return tiles


def _a_load(tiles, c_ref, gt_ref):
    cq, ck, cv = (jnp.stack([c_ref[r, sl, ln] for _, sl, ln in tiles], axis=0) for r in range(3))
    return cq, ck, cv, jnp.stack([gt_ref[sl, :] for _, sl, _ in tiles], axis=0)


def _loop_by_two(n, step, init):
    assert n % 2 == 0
    return lax.fori_loop(0, n // 2, lambda m, carry: step(2 * m + 1, step(2 * m, carry, 0), 1), init)


def _a_scan(h0, heads, nchunk, c_ref, gt_ref, pa, of_ref, ob_ref, s_ref, t_ref):
    def step(n, sts, parity):
        tiles = _a_tiles(n, nchunk, heads)
        sts_new, o, tinv = _a_step(sts, *_a_load(tiles, c_ref, gt_ref), pa, h0)
        for b, (i, sl, ln) in enumerate(tiles):
            s_ref[b, i] = sts[b]
            t_ref[b, i] = tinv[b]
            (ob_ref if b % 2 else of_ref)[sl, ln] = o[b]
        return sts_new

    _loop_by_two(nchunk, step, jnp.zeros((2 * heads, A_DIM, A_DIM), F32))


def _a_specs(s, heads):
    wide = heads * A_DIM
    once = pl.Buffered(1)
    trio = pl.BlockSpec((3, s, wide), lambda g: (0, 0, g), pipeline_mode=once)
    gates = pl.BlockSpec((s, LANE), lambda g: (0, P_GT // LANE))
    small = pl.BlockSpec((8, LANE), lambda g: (0, 0))

    def cols(base):
        return pl.BlockSpec((s, wide), lambda g: (0, base // wide + g), pipeline_mode=once)

    state = pl.BlockSpec((2 * heads, s // CHUNK, A_DIM, A_DIM), lambda g: (g, 0, 0, 0), pipeline_mode=once)
    kept = pl.BlockSpec((2 * heads, s // CHUNK, CHUNK, CHUNK), lambda g: (g, 0, 0, 0), pipeline_mode=once)
    return wide, trio, gates, small, cols, state, kept


def _delta_fwd(cqkv, proj, pa, ride=None):
    s = cqkv.shape[1]
    nchunk = s // CHUNK
    heads = A_FWD_HEADS
    steps = A_HEADS // heads
    wide, trio, gates, small, cols, state, kept = _a_specs(s, heads)
    n_in = len(ride.operands) if ride else 0
    n_out = len(ride.out_shapes) if ride else 0

    def body(*refs):
        c_ref, gt_ref, za_ref, pa_ref = refs[:4]
        out_ref, o_ref, s_ref, t_ref = refs[4 + n_in:8 + n_in]
        ob_ref = refs[8 + n_in + n_out]
        riders = (refs[4:4 + n_in], refs[8 + n_in:8 + n_in + n_out], refs[9 + n_in + n_out:])
        g = pl.program_id(0)
        if ride:
            pl.when(g == 0)(lambda: ride.start(*riders))
            pl.when(g == steps - 1)(lambda: ride.middle(*riders))
        h0 = g * heads
        pa_v = pa_ref[...]
        _a_scan(h0, heads, nchunk, c_ref, gt_ref, pa_v, o_ref, ob_ref, s_ref, t_ref)
        o_ref[...] += ob_ref[...]
        out_ref[...] = _a_final(o_ref[...], za_ref[...], pa_v).astype(BF16)
        if ride:
            pl.when(g == steps - 1)(lambda: ride.finish(*riders))

    assert steps > 1
    return pl.pallas_call(
        body, name="delta_fwd", grid=(steps,),
        in_specs=[trio, gates, cols(P_ZA), small] + [HBM] * n_in,
        out_specs=[cols(0), cols(0), state, kept] + [HBM] * n_out,
        out_shape=[jax.ShapeDtypeStruct((s, D_MODEL), BF16),
                   jax.ShapeDtypeStruct((s, A_WIDTH), F32),
                   jax.ShapeDtypeStruct((2 * A_HEADS, nchunk, A_DIM, A_DIM), F32),
                   jax.ShapeDtypeStruct((2 * A_HEADS, nchunk, CHUNK, CHUNK), F32)]
        + (list(ride.out_shapes) if ride else []),
        scratch_shapes=[pltpu.VMEM((s, wide), F32)] + (list(ride.scratch_shapes) if ride else []),
        compiler_params=_params(("arbitrary",)),
    )(cqkv, proj, proj, pa, *(ride.operands if ride else []))


def _delta_out_bwd(o_sum, proj, pa, d_mixed, tr=256):
    s = o_sum.shape[0]

    def body(o_ref, za_ref, pa_ref, dm_ref, do_ref, dza_ref, dpa_ref):
        @pl.when(pl.program_id(0) == 0)
        def _():
            dpa_ref[...] = jnp.zeros_like(dpa_ref)

        _, vjp = jax.vjp(_a_final, o_ref[...], za_ref[...], pa_ref[...])
        d_o, d_za, dpa = vjp(dm_ref[...].astype(F32))
        do_ref[...] = d_o
        dza_ref[...] = d_za.astype(BF16)
        dpa_ref[...] += dpa

    def rows(col):
        return pl.BlockSpec((tr, A_WIDTH), lambda i: (i, col))

    small = pl.BlockSpec((8, LANE), lambda i: (0, 0))
    return pl.pallas_call(
        body, name="delta_out_bwd", grid=(s // tr,), in_specs=[rows(0), rows(P_ZA // A_WIDTH), small, rows(0)],
        out_specs=[rows(0), rows(0), small],
        out_shape=[jax.ShapeDtypeStruct((s, A_WIDTH), F32), jax.ShapeDtypeStruct((s, A_WIDTH), BF16),
                   jax.ShapeDtypeStruct((8, LANE), F32)],
        compiler_params=_params(("arbitrary",)),
    )(o_sum, proj, pa, d_mixed)


def _delta_bwd(cqkv, proj, pa, d_o, states, inverses, ride=None):
    s = cqkv.shape[1]
    nchunk = s // CHUNK
    heads = A_BWD_HEADS
    steps = A_HEADS // heads
    wide, trio, gates, small, cols, _, _ = _a_specs(s, heads)
    n_in = len(ride.operands) if ride else 0
    n_out = len(ride.out_shapes) if ride else 0

    def body(*refs):
        c_ref, gt_ref, pa_ref, do_ref, s_hbm, t_hbm = refs[:6]
        dc_ref, dgt_ref, dpa_ref = refs[6 + n_in:9 + n_in]
        s_buf, t_buf, s_sems = refs[9 + n_in + n_out:12 + n_in + n_out]
        riders = (refs[6:6 + n_in], refs[9 + n_in:9 + n_in + n_out], refs[12 + n_in + n_out:])
        if ride:
            pl.when(pl.program_id(0) == 0)(lambda: ride.start(*riders))
        h0 = pl.program_id(0) * heads
        pa_v = pa_ref[...]

        @pl.when(h0 == 0)
        def _():
            dgt_ref[...] = jnp.zeros_like(dgt_ref)
            dpa_ref[...] = jnp.zeros_like(dpa_ref)

        dc_ref[...] = jnp.zeros_like(dc_ref)

        def state_copies(n, slot):
            tiles = _a_tiles(nchunk - 1 - n, nchunk, heads)
            return ([pltpu.make_async_copy(s_hbm.at[2 * h0 + b, i], s_buf.at[slot, b], s_sems.at[0, slot, b])
                     for b, (i, _, _) in enumerate(tiles)]
                    + [pltpu.make_async_copy(t_hbm.at[2 * h0 + b, i], t_buf.at[slot, b], s_sems.at[1, slot, b])
                       for b, (i, _, _) in enumerate(tiles)])

        for cp in state_copies(0, 0):
            cp.start()

        def step(n, carry, parity):
            d_sts, dpa = carry
            tiles = _a_tiles(nchunk - 1 - n, nchunk, heads)
            for cp in state_copies(n, parity):
                cp.wait()

            @pl.when(n + 1 < nchunk)
            def _():
                for cp in state_copies(n + 1, 1 - parity):
                    cp.start()

            sts, kept = s_buf[parity], t_buf[parity]
            d_o_t = jnp.stack([do_ref[sl, ln] for _, sl, ln in tiles], axis=0)
            _, vjp_c = jax.vjp(lambda *a: _a_step(*a, h0, kept)[:2], sts, *_a_load(tiles, c_ref, gt_ref), pa_v)
            d_prev, dcq, dck, dcv, dgts, dpa_i = vjp_c((d_sts, d_o_t))
            for b, (_, sl, ln) in enumerate(tiles):
                for r, dc in enumerate((dcq, dck, dcv)):
                    dc_ref[r, sl, ln] += dc[b]
                dgt_ref[sl, :] += dgts[b]
            return d_prev, dpa + dpa_i

        init = (jnp.zeros((2 * heads, A_DIM, A_DIM), F32), jnp.zeros((8, LANE), F32))
        _, dpa_out = lax.fori_loop(0, nchunk, lambda n, carry: step(n, carry, n % 2), init)
        dpa_ref[...] += dpa_out
        if ride:
            pl.when(pl.program_id(0) == steps - 1)(lambda: ride.finish(*riders))

    fixed = pl.BlockSpec((s, LANE), lambda g: (0, 0))
    return pl.pallas_call(
        body, name="delta_bwd", grid=(steps,),
        in_specs=[trio, gates, small, cols(0), pl.BlockSpec(memory_space=pl.ANY), pl.BlockSpec(memory_space=pl.ANY)]
        + [HBM] * n_in,
        out_specs=[trio, fixed, small] + [HBM] * n_out,
        out_shape=[jax.ShapeDtypeStruct((3, s, A_WIDTH), F32), jax.ShapeDtypeStruct((s, LANE), F32),
                   jax.ShapeDtypeStruct((8, LANE), F32)] + (list(ride.out_shapes) if ride else []),
        scratch_shapes=[pltpu.VMEM((2, 2 * heads, A_DIM, A_DIM), F32), pltpu.VMEM((2, 2 * heads, CHUNK, CHUNK), F32),
                        pltpu.SemaphoreType.DMA((2, 2, 2 * heads))]
        + (list(ride.scratch_shapes) if ride else []),
        compiler_params=_params(("arbitrary",)),
    )(cqkv, proj, pa, d_o, states, inverses, *(ride.operands if ride else []))


def _rope_tables(s):
    inv = ROPE_THETA ** (-jnp.arange(0, B_DIM, 2, dtype=F32) / B_DIM)
    ang = jnp.arange(s, dtype=F32)[:, None] * inv[None, :]
    cos, sin = jnp.cos(ang), jnp.sin(ang)
    return jnp.concatenate([cos, cos], axis=1), jnp.concatenate([-sin, sin], axis=1)


def _b_block(q_t, z_t, k3, v3, cos_q, sin_q, cos_k, sin_k, pb, n, nb):
    w = WINDOW
    def swap(t):
        return jnp.concatenate([t[:, B_DIM // 2:], t[:, :B_DIM // 2]], axis=1)

    grp = B_HEADS // B_KV
    qi = lax.broadcasted_iota(jnp.int32, (grp * w, 3 * w), 0) & (w - 1)
    kj = lax.broadcasted_iota(jnp.int32, (grp * w, 3 * w), 1)
    kpos = kj + (n - 1) * w
    mask = (jnp.abs(kj - w - qi) <= w) & (kpos >= 0) & (kpos < nb * w)
    lane = lax.broadcasted_iota(jnp.int32, (1, LANE), 1)
    qn, kn = pb[0:1, :B_DIM], pb[1:2, :B_DIM]
    cos_g = jnp.concatenate([cos_q] * grp, axis=0)
    sin_g = jnp.concatenate([sin_q] * grp, axis=0)
    def group(q, k, v, sink):
        k = _rms(k, kn)
        k = k * cos_k + swap(k) * sin_k
        q = _rms(q, qn)
        q = q * cos_g + swap(q) * sin_g
        s = _mm_nt(q, k) * (B_DIM ** -0.5)
        s = jnp.where(mask, s, -jnp.inf)
        m = jnp.maximum(jnp.max(s, axis=1, keepdims=True), sink)
        p = jnp.exp(s - m)
        p = p / (jnp.sum(p, axis=1, keepdims=True) + jnp.exp(sink - m))
        return _mm(p, v)

    stack = lambda ts: jnp.concatenate([t[None] for t in ts], axis=0)
    qs, ks, vs, sinks = [], [], [], []
    for hk in range(B_KV):
        heads = [hk * grp + g for g in range(grp)]
        ks.append(k3[:, hk * B_DIM:(hk + 1) * B_DIM])
        vs.append(v3[:, hk * B_DIM:(hk + 1) * B_DIM])
        qs.append(jnp.concatenate([q_t[:, hq * B_DIM:(hq + 1) * B_DIM] for hq in heads], axis=0))
        sinks.append(jnp.concatenate(
            [jnp.broadcast_to(jnp.sum(jnp.where(lane == hq, pb[2:3, :], 0.0), axis=1, keepdims=True), (w, 1))
             for hq in heads], axis=0))
    o = jax.vmap(group)(stack(qs), stack(ks), stack(vs), stack(sinks))
    outs = [o[hk, g * w:(g + 1) * w, :] for hk in range(B_KV) for g in range(grp)]
    return jnp.concatenate(outs, axis=1) * _silu(z_t)


def _b_specs(s):
    nb = s // WINDOW
    qsp = pl.BlockSpec((WINDOW, 512), lambda n: (n, P_QB // 512))
    zsp = pl.BlockSpec((WINDOW, 512), lambda n: (n, P_ZB // 512))

    def three(col, width):
        return [pl.BlockSpec((WINDOW, width), lambda n: (jnp.maximum(n - 1, 0), col)),
                pl.BlockSpec((WINDOW, width), lambda n: (n, col)),
                pl.BlockSpec((WINDOW, width), lambda n: (jnp.minimum(n + 1, nb - 1), col))]

    tab = pl.BlockSpec((WINDOW, B_DIM), lambda n: (n, 0))
    small = pl.BlockSpec((8, LANE), lambda n: (0, 0))
    specs = [qsp, zsp] + three(P_KB // LANE, LANE) + three(P_VB // LANE, LANE) + [tab, tab] + three(0, B_DIM) + three(0, B_DIM) + [small]
    return nb, specs


def _b_args(proj, cos2, sin2, pb):
    return (proj, proj, proj, proj, proj, proj, proj, proj, cos2, sin2, cos2, cos2, cos2, sin2, sin2, sin2, pb)


def _b_load(refs):
    (q_ref, z_ref, kp, kc, kx, vp, vc, vx, cq, sq, ckp, ckc, ckx, skp, skc, skx, pb_ref) = refs
    cat = lambda *r: jnp.concatenate([t[...] for t in r], axis=0)
    return (q_ref[...], z_ref[...], cat(kp, kc, kx), cat(vp, vc, vx), cq[...], sq[...], cat(ckp, ckc, ckx),
            cat(skp, skc, skx), pb_ref[...])


def _attn_b_fwd(proj, cos2, sin2, pb, mixed):
    s = proj.shape[0]
    nb, specs = _b_specs(s)

    def body(*refs):
        o_ref = refs[-1]
        args = _b_load(refs[:-2])
        o_ref[...] = _b_block(*args, pl.program_id(0), nb).astype(BF16)

    return pl.pallas_call(
        body, name="attn_b_fwd", grid=(nb,), in_specs=specs + [pl.BlockSpec(memory_space=pl.ANY)],
        out_specs=pl.BlockSpec((WINDOW, 512), lambda n: (n, A_WIDTH // 512)),
        out_shape=jax.ShapeDtypeStruct(mixed.shape, mixed.dtype), input_output_aliases={len(specs): 0},
        compiler_params=_params(("parallel",)),
    )(*_b_args(proj, cos2, sin2, pb), mixed)


def _attn_b_bwd(proj, cos2, sin2, pb, d_mixed):
    s = proj.shape[0]
    nb, specs = _b_specs(s)
    w = WINDOW

    def body(*refs):
        dm_ref, dq_ref, dz_ref, dk_ref, dv_ref, dpb_ref = refs[-6:]
        n = pl.program_id(0)
        q_t, z_t, k3, v3, cq, sq, ck, sk, pb_v = _b_load(refs[:-6])

        @pl.when(n == 0)
        def _():
            dk_ref[...] = jnp.zeros_like(dk_ref)
            dv_ref[...] = jnp.zeros_like(dv_ref)
            dpb_ref[...] = jnp.zeros_like(dpb_ref)

        def f(q_, z_, k_, v_, pb_):
            return _b_block(q_, z_, k_, v_, cq, sq, ck, sk, pb_, n, nb)

        _, vjp = jax.vjp(f, q_t, z_t, k3, v3, pb_v)
        dq, dz, dk3, dv3, dpb = vjp(dm_ref[...])
        dq_ref[...] = dq.astype(BF16)
        dz_ref[...] = dz.astype(BF16)
        dpb_ref[...] += dpb

        def add(j, cond):
            @pl.when(cond)
            def _():
                rows = pl.ds(pl.multiple_of((n - 1 + j) * w, w), w)
                dk_ref[rows, :] += dk3[j * w:(j + 1) * w, :]
                dv_ref[rows, :] += dv3[j * w:(j + 1) * w, :]

        add(0, n > 0)
        add(1, n >= 0)
        add(2, n < nb - 1)

    blk = pl.BlockSpec((w, 512), lambda n: (n, 0))
    whole = pl.BlockSpec((s, LANE), lambda n: (0, 0))
    small = pl.BlockSpec((8, LANE), lambda n: (0, 0))
    return pl.pallas_call(
        body, name="attn_b_bwd", grid=(nb,),
        in_specs=specs + [pl.BlockSpec((w, 512), lambda n: (n, 2))],
        out_specs=[blk, blk, whole, whole, small],
        out_shape=[jax.ShapeDtypeStruct((s, 512), BF16), jax.ShapeDtypeStruct((s, 512), BF16),
                   jax.ShapeDtypeStruct((s, LANE), F32), jax.ShapeDtypeStruct((s, LANE), F32),
                   jax.ShapeDtypeStruct((8, LANE), F32)],
        compiler_params=_params(("arbitrary",)),
    )(*_b_args(proj, cos2, sin2, pb), d_mixed)


def _mem_kv_fwd(mem, mem_norm_w, w_kv):
    def body(mem_ref, nw_ref, w_ref, kv_ref):
        mn = _rms(mem_ref[...], nw_ref[...]).astype(BF16)
        kv_ref[...] = jnp.dot(mn, w_ref[...], preferred_element_type=F32)

    return pl.pallas_call(
        body, name="mem_kv_fwd", out_shape=jax.ShapeDtypeStruct((MEM_LEN, 2 * C_HEADS * C_DIM), F32),
        compiler_params=_params(),
    )(mem, mem_norm_w, w_kv)


def _mem_kv_bwd(mem, mem_norm_w, w_kv, d_kv):
    def body(mem_ref, nw_ref, w_ref, g_ref, gw_ref, gn_ref):
        mn, vjp = jax.vjp(_rms, mem_ref[...], nw_ref[...])
        g = g_ref[...].astype(BF16)
        gw_ref[...] = lax.dot_general(mn.astype(BF16), g, (((0,), (0,)), ((), ())), preferred_element_type=F32)
        d_mn = lax.dot_general(g, w_ref[...], (((1,), (1,)), ((), ())), preferred_element_type=F32)
        gn_ref[...] = vjp(d_mn)[1]

    return pl.pallas_call(
        body, name="mem_kv_bwd",
        out_shape=[jax.ShapeDtypeStruct((D_MODEL, 2 * C_HEADS * C_DIM), F32), jax.ShapeDtypeStruct((1, D_MODEL), F32)],
        compiler_params=_params(),
    )(mem, mem_norm_w, w_kv, d_kv)


def _c_tile(q_t, z_t, kvm, pc):
    width = C_HEADS * C_DIM
    outs = []
    for h in range(C_HEADS):
        q = _rms(q_t[:, h * C_DIM:(h + 1) * C_DIM], pc[0:1, :])
        k = _rms(kvm[:, h * C_DIM:(h + 1) * C_DIM], pc[1:2, :])
        v = kvm[:, width + h * C_DIM:width + (h + 1) * C_DIM]
        s = _mm_nt(q, k) * (C_DIM ** -0.5)
        p = jnp.exp(s - jnp.max(s, axis=1, keepdims=True))
        p = p / jnp.sum(p, axis=1, keepdims=True)
        outs.append(_mm(p, v))
    return jnp.concatenate(outs, axis=1) * _silu(z_t)


def _attn_c_fwd(proj, kvm, pc, mixed, tq=256):
    s = proj.shape[0]

    def body(q_ref, z_ref, kv_ref, pc_ref, mixed_ref, o_ref):
        o_ref[...] = _c_tile(q_ref[...], z_ref[...], kv_ref[...], pc_ref[...]).astype(BF16)

    return pl.pallas_call(
        body, name="attn_c_fwd", grid=(s // tq,),
        in_specs=[pl.BlockSpec((tq, 512), lambda i: (i, P_QC // 512)), pl.BlockSpec((tq, 512), lambda i: (i, P_ZC // 512)),
                  pl.BlockSpec(kvm.shape, lambda i: (0, 0)), pl.BlockSpec((8, LANE), lambda i: (0, 0)),
                  pl.BlockSpec(memory_space=pl.ANY)],
        out_specs=pl.BlockSpec((tq, 512), lambda i: (i, (A_WIDTH + 512) // 512)),
        out_shape=jax.ShapeDtypeStruct(mixed.shape, mixed.dtype), input_output_aliases={4: 0},
        compiler_params=_params(("parallel",)),
    )(proj, proj, kvm, pc, mixed)


def _attn_c_bwd(proj, kvm, pc, d_mixed, tq=256):
    s = proj.shape[0]

    def body(q_ref, z_ref, kv_ref, pc_ref, dm_ref, dq_ref, dz_ref, dkv_ref, dpc_ref):
        @pl.when(pl.program_id(0) == 0)
        def _():
            dkv_ref[...] = jnp.zeros_like(dkv_ref)
            dpc_ref[...] = jnp.zeros_like(dpc_ref)

        _, vjp = jax.vjp(_c_tile, q_ref[...], z_ref[...], kv_ref[...], pc_ref[...])
        dq, dz, dkv, dpc = vjp(dm_ref[...])
        dq_ref[...] = dq.astype(BF16)
        dz_ref[...] = dz.astype(BF16)
        dkv_ref[...] += dkv
        dpc_ref[...] += dpc

    blk = pl.BlockSpec((tq, 512), lambda i: (i, 0))
    kvs = pl.BlockSpec(kvm.shape, lambda i: (0, 0))
    small = pl.BlockSpec((8, LANE), lambda i: (0, 0))
    return pl.pallas_call(
        body, name="attn_c_bwd", grid=(s // tq,),
        in_specs=[pl.BlockSpec((tq, 512), lambda i: (i, P_QC // 512)), pl.BlockSpec((tq, 512), lambda i: (i, P_ZC // 512)),
                  kvs, small, pl.BlockSpec((tq, 512), lambda i: (i, 3))],
        out_specs=[blk, blk, kvs, small],
        out_shape=[jax.ShapeDtypeStruct((s, 512), BF16), jax.ShapeDtypeStruct((s, 512), BF16),
                   jax.ShapeDtypeStruct(kvm.shape, F32), jax.ShapeDtypeStruct((8, LANE), F32)],
        compiler_params=_params(("arbitrary",)),
    )(proj, proj, kvm, pc, d_mixed)


def _pad_row(v, width=LANE):
    v = v.reshape(1, -1)
    return jnp.pad(v, ((0, 0), (0, width - v.shape[1])))


def _local_step(x, mem, target, norm_w, w_perm_t, w_blocks_t, conv_w, pa, pb, pc, mem_norm_w, w_kv, w_out, gather=None,
                exchange=None):
    s = x.shape[0]
    cos2, sin2 = _rope_tables(s)
    hn = _rms_fwd(x, norm_w)
    wide = dict(tm=1024, tn=512, tk=2048)
    proj = _matmul(hn, w_perm_t, "nt", F32, "mm_proj", **wide)
    cqkv = _conv_fwd(proj, conv_w)
    if gather is None:
        mixed, o_sum, states, inverses = _delta_fwd(cqkv, proj, pa)
    else:
        mixed, o_sum, states, inverses, *arrived = _delta_fwd(cqkv, proj, pa, gather[0])
        w_out, w_kv = gather[1](*arrived)
    mixed = _attn_b_fwd(proj, cos2, sin2, pb, mixed)
    kvm = _mem_kv_fwd(mem, mem_norm_w, w_kv)
    mixed = _attn_c_fwd(proj, kvm, pc, mixed)
    dy, dyb, loss_parts = _out_loss(mixed, w_out, x, target)

    d_mixed = _matmul(dyb, w_out, "nt", F32, "mm_dmixed", **wide)
    g_w_out = _matmul(mixed, dyb, "tn", F32, "mm_gwout", **wide)
    d_qc, d_zc, d_kvm, d_pc = _attn_c_bwd(proj, kvm, pc, d_mixed)
    g_w_kv, g_mem_norm = _mem_kv_bwd(mem, mem_norm_w, w_kv, d_kvm)
    d_qb, d_zb, d_kb, d_vb, d_pb = _attn_b_bwd(proj, cos2, sin2, pb, d_mixed)
    d_o, d_za, d_pa_out = _delta_out_bwd(o_sum, proj, pa, d_mixed)
    early = exchange[0](g_w_out, g_w_kv) if exchange else None
    d_c, d_gt, d_pa_scan, *landed_early = _delta_bwd(cqkv, proj, pa, d_o, states, inverses, early)
    d_pa = d_pa_out + d_pa_scan
    d_qkv, g_conv = _conv_bwd(proj, conv_w, d_c)
    d_proj = _cotangent_blocks(d_qkv, d_za, d_gt, d_qb, d_kb, d_vb, d_zb, d_qc, d_zc)
    g_w_blocks_t = _matmul(d_proj, hn, "tn", F32, "mm_gwin", tm=512, tn=2048, tk=2048)
    late = exchange[1](g_w_blocks_t) if exchange else None
    g_x, g_norm, *landed_late = _input_grad(d_proj, w_blocks_t, x, norm_w, dy, late)
    return dict(loss_parts=loss_parts, g_x=g_x, g_norm=g_norm, g_w_blocks_t=g_w_blocks_t, g_conv=g_conv, d_pa=d_pa,
                d_pb=d_pb, d_pc=d_pc, g_mem_norm=g_mem_norm, g_w_kv=g_w_kv, g_w_out=g_w_out,
                landed=landed_late + landed_early)


_SEGMENTS = ((0, O_GT, 0), (O_GT, O_QB, P_GT), (O_QB, O_KB, P_QB), (O_KB, O_VB, P_KB), (O_VB, O_ZB, P_VB),
             (O_ZB, O_QC, P_ZB), (O_QC, O_ZC, P_QC), (O_ZC, IN_WIDTH, P_ZC))


def _permute_blocks(w4):
    parts = []
    for first, end, _ in sorted(_SEGMENTS, key=lambda seg: seg[2]):
        row = first
        while row < end:
            k = row // W_IN_BLOCK
            stop = min(end, (k + 1) * W_IN_BLOCK)
            parts.append(w4[k][row - k * W_IN_BLOCK:stop - k * W_IN_BLOCK, :])
            row = stop
    parts.append(jnp.zeros((P_WIDTH - IN_WIDTH, w4.shape[2]), w4.dtype))
    return jnp.concatenate(parts, axis=0)


def _cotangent_blocks(d_qkv, d_za, d_gt, d_qb, d_kb, d_vb, d_zb, d_qc, d_zc):
    s = d_qkv.shape[0]
    tr = min(256, s)
    pieces = (d_qkv, d_za, d_gt, d_qb, d_kb, d_vb, d_zb, d_qc, d_zc)

    def body(*refs):
        o_ref = refs[-1]
        tiles = [r[...].astype(BF16) for r in refs[:-1]]
        tiles[2] = tiles[2][:, :O_QB - O_GT]
        orig = jnp.concatenate(tiles, axis=1)
        pad = jnp.zeros((tr, W_IN_PAD - W_IN_BLOCK), BF16)
        parts = []
        for k in range(N_CHIPS):
            parts += [orig[:, k * W_IN_BLOCK:(k + 1) * W_IN_BLOCK], pad]
        o_ref[...] = jnp.concatenate(parts, axis=1)

    return pl.pallas_call(
        body, name="cotangent_blocks", grid=(s // tr,),
        in_specs=[pl.BlockSpec((tr, p.shape[1]), lambda i: (i, 0)) for p in pieces],
        out_specs=pl.BlockSpec((tr, N_CHIPS * W_IN_PAD), lambda i: (i, 0)),
        out_shape=jax.ShapeDtypeStruct((s, N_CHIPS * W_IN_PAD), BF16), compiler_params=_params(("parallel",)),
    )(*pieces)


HBM = pl.BlockSpec(memory_space=pltpu.HBM)


def _place():
    x, y, c = lax.axis_index("x"), lax.axis_index("y"), lax.axis_index("c")
    chips = [(1 - x, y), (x, 1 - y), (1 - x, 1 - y)]
    return x, y, c, 2 * x + y, chips, [2 * cx + cy for cx, cy in chips]


PIECE_ROWS_CAP = 600


def _remote(src, dst, send_sems, recv_sems, k, to):
    return pltpu.make_async_remote_copy(src_ref=src, dst_ref=dst, send_sem=send_sems.at[k], recv_sem=recv_sems.at[k],
                                        device_id=to, device_id_type=MESH)


def _half_cols(ref, c):
    half = ref.shape[-1] // 2
    return pl.ds(pl.multiple_of(c * half, LANE), half)


class _PairedGather:
    def __init__(self, blocks):
        n = len(blocks)
        self.operands = list(blocks)
        self.out_shapes = [jax.ShapeDtypeStruct((N_CHIPS,) + b.shape, b.dtype) for b in blocks]
        self.scratch_shapes = [pltpu.SemaphoreType.DMA((6 * n,)), pltpu.SemaphoreType.DMA((6 * n,))]

    @staticmethod
    def _copies(srcs, dsts, sems):
        x, y, c, me, chips, chip_ids = _place()
        sends, landed, passes, passed = [], [], [], []
        for a, (src, dst) in enumerate(zip(srcs, dsts)):
            mine, other = _half_cols(src, c), _half_cols(src, 1 - c)
            for j, (chip, cid) in enumerate(zip(chips, chip_ids)):
                sends.append(_remote(src.at[:, mine], dst.at[me, :, mine], sems[0], sems[1], 6 * a + j, (*chip, c)))
                here = dst.at[cid, :, mine]
                landed.append(_remote(here, here, sems[0], sems[1], 6 * a + j, (x, y, 1 - c)))
                passes.append(_remote(here, here, sems[0], sems[1], 6 * a + 3 + j, (x, y, 1 - c)))
                there = dst.at[cid, :, other]
                passed.append(_remote(there, there, sems[0], sems[1], 6 * a + 3 + j, (x, y, 1 - c)))
        return sends, landed, passes, passed

    def start(self, srcs, dsts, sems):
        for cp in self._copies(srcs, dsts, sems)[0]:
            cp.start()

    def middle(self, srcs, dsts, sems):
        _, landed, passes, _ = self._copies(srcs, dsts, sems)
        for arrived, onward in zip(landed, passes):
            arrived.wait_recv()
            onward.start()

    def finish(self, srcs, dsts, sems):
        sends, _, passes, passed = self._copies(srcs, dsts, sems)
        for cp in passed:
            cp.wait_recv()
        for cp in sends + passes:
            cp.wait_send()


def _all_gather_weights(bigs, conv_b):
    bigs = tuple(bigs)
    n_big = len(bigs)

    def body(*refs):
        srcs, conv_src = refs[:n_big], refs[n_big]
        dsts, conv_dst = refs[n_big + 1:2 * n_big + 1], refs[2 * n_big + 1]
        send_sems, recv_sems, local_sems = refs[2 * n_big + 2:]
        x, y, c, me, chips, chip_ids = _place()
        sibling = (x, y, 1 - c)
        local = [pltpu.make_async_copy(src, dst.at[me], local_sems.at[a]) for a, (src, dst) in enumerate(zip(srcs, dsts))]
        local.append(pltpu.make_async_copy(conv_src, conv_dst.at[me], local_sems.at[n_big]))
        for cp in local:
            cp.start()
        sends = []
        for a, (src, dst) in enumerate(zip(srcs, dsts)):
            mine = _half_cols(src, c)
            for j, chip in enumerate(chips):
                sends.append(_remote(src.at[:, mine], dst.at[me, :, mine], send_sems, recv_sems, 6 * a + j, (*chip, c)))
        for j, chip in enumerate(chips):
            sends.append(_remote(conv_src, conv_dst.at[me], send_sems, recv_sems, 6 * n_big + j, (*chip, c)))
        for cp in sends:
            cp.start()
        passed = []
        for a, (src, dst) in enumerate(zip(srcs, dsts)):
            mine = _half_cols(src, c)
            for j, cid in enumerate(chip_ids):
                landed = dst.at[cid, :, mine]
                _remote(landed, landed, send_sems, recv_sems, 6 * a + j, sibling).wait_recv()
                cp = _remote(landed, landed, send_sems, recv_sems, 6 * a + 3 + j, sibling)
                cp.start()
                passed.append(cp)
        for a, (src, dst) in enumerate(zip(srcs, dsts)):
            other = _half_cols(src, 1 - c)
            for j, cid in enumerate(chip_ids):
                landed = dst.at[cid, :, other]
                _remote(landed, landed, send_sems, recv_sems, 6 * a + 3 + j, sibling).wait_recv()
        for j, cid in enumerate(chip_ids):
            _remote(conv_src, conv_dst.at[cid], send_sems, recv_sems, 6 * n_big + j, sibling).wait_recv()
        for cp in sends + passed:
            cp.wait_send()
        for cp in local:
            cp.wait()

    n_sem = 6 * n_big + 3
    return pl.pallas_call(
        body, name="all_gather_weights",
        out_shape=[jax.ShapeDtypeStruct((N_CHIPS,) + w.shape, w.dtype) for w in bigs + (conv_b,)],
        in_specs=[pl.BlockSpec(memory_space=pltpu.VMEM)] * (n_big + 1), out_specs=[HBM] * (n_big + 1),
        scratch_shapes=[pltpu.SemaphoreType.DMA((n_sem,)), pltpu.SemaphoreType.DMA((n_sem,)),
                        pltpu.SemaphoreType.DMA((n_big + 1,))],
        compiler_params=_params(),
    )(*bigs, conv_b)


def _pair_exchange(grads, name):
    n = len(grads)
    pieces = [_row_tile(g.shape[1]) for g in grads]

    def body(*refs):
        srcs, gots = refs[:n], refs[n:2 * n]
        stages = refs[2 * n:3 * n]
        send_sems, recv_sems, load_sems = refs[3 * n:]
        x, y, c, _, _, _ = _place()
        sibling = (x, y, 1 - c)
        for a in range(n):
            slabs, rows, _ = gots[a].shape
            piece = pieces[a]
            per_slab = rows // piece
            theirs = _half_cols(srcs[a], 1 - c)
            loads, sends = [], []
            for i in range(slabs * per_slab):
                k, r, slot = i // per_slab, i % per_slab, i % 2
                part = pl.ds(r * piece, piece)
                loads.append(pltpu.make_async_copy(srcs[a].at[k, part, theirs], stages[a].at[slot], load_sems.at[2 * a + slot]))
                sends.append(pltpu.make_async_remote_copy(
                    src_ref=stages[a].at[slot], dst_ref=gots[a].at[k, part, :],
                    send_sem=send_sems.at[2 * a + slot], recv_sem=recv_sems.at[a], device_id=sibling, device_id_type=MESH))
            loads[0].start()
            for i in range(len(loads)):
                loads[i].wait()
                sends[i].start()
                if i + 1 < len(loads):
                    if i >= 1:
                        sends[i - 1].wait_send()
                    loads[i + 1].start()
            for cp in sends[-2:]:
                cp.wait_send()
        for a in range(n):
            whole = srcs[a].at[:, :, _half_cols(srcs[a], c)]
            pltpu.make_async_remote_copy(src_ref=whole, dst_ref=gots[a], send_sem=send_sems.at[2 * a],
                                         recv_sem=recv_sems.at[a], device_id=sibling, device_id_type=MESH).wait_recv()

    halves = [jax.ShapeDtypeStruct((g.shape[0], g.shape[1], g.shape[2] // 2), g.dtype) for g in grads]
    return pl.pallas_call(
        body, name=name, out_shape=halves, in_specs=[HBM] * n, out_specs=[HBM] * n,
        scratch_shapes=[pltpu.VMEM((2, piece, g.shape[2] // 2), g.dtype) for piece, g in zip(pieces, grads)]
        + [pltpu.SemaphoreType.DMA((2 * n,)), pltpu.SemaphoreType.DMA((n,)), pltpu.SemaphoreType.DMA((2 * n,))],
        compiler_params=_params(),
    )(*grads)


class _ChipExchange:
    def __init__(self, halves):
        n = len(halves)
        self.operands = list(halves)
        self.out_shapes = [jax.ShapeDtypeStruct((N_CHIPS - 1,) + h.shape[1:], h.dtype) for h in halves]
        self.scratch_shapes = [pltpu.SemaphoreType.DMA((3 * n,)), pltpu.SemaphoreType.DMA((3 * n,))]

    @staticmethod
    def _copies(srcs, lands, sems):
        _, _, c, _, chips, chip_ids = _place()
        return [_remote(src.at[cid], land.at[j], sems[0], sems[1], 3 * a + j, (*chip, c))
                for a, (src, land) in enumerate(zip(srcs, lands)) for j, (chip, cid) in enumerate(zip(chips, chip_ids))]

    def start(self, srcs, lands, sems):
        for cp in self._copies(srcs, lands, sems):
            cp.start()

    def finish(self, srcs, lands, sems):
        copies = self._copies(srcs, lands, sems)
        for cp in copies:
            cp.wait_recv()
        for cp in copies:
            cp.wait_send()


def _pair_gather(halves, rows):
    n = len(halves)

    def body(*refs):
        srcs, fulls = refs[:n], refs[n:2 * n]
        send_sems, recv_sems, local_sems = refs[2 * n:]
        x, y, c, _, _, _ = _place()
        copies = []
        for a in range(n):
            mine, src = _half_cols(fulls[a], c), srcs[a].at[pl.ds(0, rows[a]), :]
            keep = pltpu.make_async_copy(src, fulls[a].at[:, mine], local_sems.at[a])
            keep.start()
            give = _remote(src, fulls[a].at[:, mine], send_sems, recv_sems, a, (x, y, 1 - c))
            give.start()
            copies += [keep, give]
        for a in range(n):
            other, src = _half_cols(fulls[a], 1 - c), srcs[a].at[pl.ds(0, rows[a]), :]
            copies[2 * a].wait()
            copies[2 * a + 1].wait_send()
            _remote(src, fulls[a].at[:, other], send_sems, recv_sems, a, (x, y, 1 - c)).wait_recv()

    return pl.pallas_call(
        body, name="grad_pair_gather",
        out_shape=[jax.ShapeDtypeStruct((r, 2 * h.shape[1]), h.dtype) for r, h in zip(rows, halves)],
        in_specs=[pl.BlockSpec(memory_space=pltpu.VMEM)] * n, out_specs=[HBM] * n,
        scratch_shapes=[pltpu.SemaphoreType.DMA((n,)), pltpu.SemaphoreType.DMA((n,)), pltpu.SemaphoreType.DMA((n,))],
    )(*halves)


def _all_reduce_small(p):
    n_dev = 8

    def body(p_ref, o_ref, land, send_sems, recv_sems):
        x, y, c = lax.axis_index("x"), lax.axis_index("y"), lax.axis_index("c")
        me = 4 * x + 2 * y + c
        land[me] = p_ref[...]
        sends = []
        for k in range(1, n_dev):
            fx, fy, fc = (k >> 2) & 1, (k >> 1) & 1, k & 1
            to = (x ^ fx, y ^ fy, c ^ fc)
            cp = _remote(p_ref, land.at[me], send_sems, recv_sems, k - 1, to)
            cp.start()
            sends.append(cp)
        for k in range(1, n_dev):
            _remote(p_ref, land.at[me ^ k], send_sems, recv_sems, k - 1, (x, y, c)).wait_recv()
        total = land[0]
        for d in range(1, n_dev):
            total = total + land[d]
        o_ref[...] = total
        for cp in sends:
            cp.wait_send()

    vm = pl.BlockSpec(memory_space=pltpu.VMEM)
    return pl.pallas_call(
        body, name="all_reduce_small", out_shape=jax.ShapeDtypeStruct(p.shape, p.dtype), in_specs=[vm], out_specs=vm,
        scratch_shapes=[pltpu.VMEM((n_dev,) + p.shape, p.dtype), pltpu.SemaphoreType.DMA((n_dev - 1,)),
                        pltpu.SemaphoreType.DMA((n_dev - 1,))],
    )(p)


def _row_tile(rows):
    fits = [t for t in range(8, min(rows, PIECE_ROWS_CAP) + 1, 8) if rows % t == 0]
    return max(fits) if fits else rows


def _pair_sum(full, got, core, name):
    n, r, c = got.shape
    tr = _row_tile(r)

    def body(core_ref, a_ref, b_ref, o_ref):
        o_ref[...] = (a_ref[...] + b_ref[...]).astype(BF16)

    blk = pl.BlockSpec((None, tr, c), lambda i, j, core_ref: (i, j, 0))
    grid_spec = pltpu.PrefetchScalarGridSpec(
        num_scalar_prefetch=1, grid=(n, r // tr),
        in_specs=[pl.BlockSpec((None, tr, c), lambda i, j, core_ref: (i, j, core_ref[0])), blk], out_specs=blk)
    return pl.pallas_call(body, name=name, grid_spec=grid_spec, out_shape=jax.ShapeDtypeStruct(got.shape, BF16),
                          compiler_params=_params(("parallel", "parallel")))(core, full, got)


def _chip_sum(full, got, land, place, name):
    n, r, c = land.shape
    tr = _row_tile(r)

    def body(place_ref, a_ref, b_ref, l_ref, o_ref):
        total = a_ref[...] + b_ref[...]
        for j in range(n):
            total = total + l_ref[j].astype(F32)
        o_ref[...] = total

    grid_spec = pltpu.PrefetchScalarGridSpec(
        num_scalar_prefetch=1, grid=(r // tr,),
        in_specs=[pl.BlockSpec((None, tr, c), lambda i, p: (p[0], i, p[1])),
                  pl.BlockSpec((None, tr, c), lambda i, p: (p[0], i, 0)),
                  pl.BlockSpec((n, tr, c), lambda i, p: (0, i, 0))],
        out_specs=pl.BlockSpec((tr, c), lambda i, p: (i, 0)))
    return pl.pallas_call(body, name=name, grid_spec=grid_spec, out_shape=jax.ShapeDtypeStruct((r, c), F32),
                          compiler_params=_params(("parallel",)))(place, full, got, land)


def _adamw(w, g, m, v, name, echo=False):
    r, c = w.shape
    tr = _row_tile(r)
    tc = 1024 if c % 1024 == 0 else c

    def body(w_ref, g_ref, m_ref, v_ref, d_ref, mo_ref, vo_ref, *g_out):
        g_ = g_ref[...]
        for o in g_out:
            o[...] = g_
        m2 = ADAM_B1 * m_ref[...] + (1.0 - ADAM_B1) * g_
        v2 = ADAM_B2 * v_ref[...] + (1.0 - ADAM_B2) * jnp.square(g_)
        m_hat = m2 / (1.0 - ADAM_B1 ** ADAM_STEP)
        v_hat = v2 / (1.0 - ADAM_B2 ** ADAM_STEP)
        d_ref[...] = -ADAM_LR * (m_hat / (jnp.sqrt(v_hat) + ADAM_EPS) + ADAM_WD * w_ref[...])
        mo_ref[...] = m2
        vo_ref[...] = v2

    blk = pl.BlockSpec((tr, tc), lambda i, j: (i, j))
    n_out = 4 if echo else 3
    return pl.pallas_call(body, name=name, grid=(r // tr, c // tc), in_specs=[blk] * 4, out_specs=[blk] * n_out,
                          out_shape=[jax.ShapeDtypeStruct(w.shape, F32)] * n_out,
                          compiler_params=_params(("parallel", "parallel")))(w, g, m, v)


SMALL_NAMES = ("norm_w", "mem_norm_w", "o_norm_a", "q_norm_c", "k_norm_c", "q_norm_b", "k_norm_b",
               "a_log_fwd", "a_log_bwd", "dt_bias_fwd", "dt_bias_bwd", "sink_b")
SMALL_SIZES = (2048, 2048, 128, 128, 128, 64, 64, 8, 8, 8, 8, 8)
SMALL_LOSS = sum(SMALL_SIZES)
SMALL_CONV = 5120
SMALL_TOTAL = SMALL_CONV + CONV_K * 3 * A_WIDTH
SMALL_ROWS = SMALL_TOTAL // LANE


def _pack_small(parts, extra=None, conv=None):
    vec = [parts[n].reshape(-1) for n in SMALL_NAMES]
    vec.append(jnp.zeros((1,), F32) if extra is None else extra.reshape(1))
    vec.append(jnp.zeros((SMALL_CONV - SMALL_LOSS - 1,), F32))
    vec.append(jnp.zeros((SMALL_TOTAL - SMALL_CONV,), F32) if conv is None else conv.reshape(-1))
    return jnp.concatenate(vec).reshape(SMALL_ROWS, LANE)


def _unpack_small(packed):
    flat = packed.reshape(-1)
    out, off = {}, 0
    for n, size in zip(SMALL_NAMES, SMALL_SIZES):
        out[n] = flat[off:off + size].reshape(1, size)
        off += size
    return out


WEIGHT_ORDER = ("norm_w", "w_in", "conv_w_a", "a_log_fwd", "a_log_bwd", "dt_bias_fwd", "dt_bias_bwd", "o_norm_a",
                "q_norm_b", "k_norm_b", "sink_b", "mem_norm_w", "w_mem_kv", "q_norm_c", "k_norm_c", "w_out")


def kernel(x, mem, norm_w, w_in, conv_w_a, a_log_fwd, a_log_bwd, dt_bias_fwd, dt_bias_bwd, o_norm_a, q_norm_b, k_norm_b, sink_b, mem_norm_w, w_mem_kv, q_norm_c, k_norm_c, w_out, loss_target, m_norm_w, m_w_in, m_conv_w_a, m_a_log_fwd, m_a_log_bwd, m_dt_bias_fwd, m_dt_bias_bwd, m_o_norm_a, m_q_norm_b, m_k_norm_b, m_sink_b, m_mem_norm_w, m_w_mem_kv, m_q_norm_c, m_k_norm_c, m_w_out, v_norm_w, v_w_in, v_conv_w_a, v_a_log_fwd, v_a_log_bwd, v_dt_bias_fwd, v_dt_bias_bwd, v_o_norm_a, v_q_norm_b, v_k_norm_b, v_sink_b, v_mem_norm_w, v_w_mem_kv, v_q_norm_c, v_k_norm_c, v_w_out):
    weights = dict(norm_w=norm_w, w_in=w_in, conv_w_a=conv_w_a, a_log_fwd=a_log_fwd, a_log_bwd=a_log_bwd,
                   dt_bias_fwd=dt_bias_fwd, dt_bias_bwd=dt_bias_bwd, o_norm_a=o_norm_a, q_norm_b=q_norm_b,
                   k_norm_b=k_norm_b, sink_b=sink_b, mem_norm_w=mem_norm_w, w_mem_kv=w_mem_kv, q_norm_c=q_norm_c,
                   k_norm_c=k_norm_c, w_out=w_out)
    mom1 = dict(norm_w=m_norm_w, w_in=m_w_in, conv_w_a=m_conv_w_a, a_log_fwd=m_a_log_fwd, a_log_bwd=m_a_log_bwd,
                dt_bias_fwd=m_dt_bias_fwd, dt_bias_bwd=m_dt_bias_bwd, o_norm_a=m_o_norm_a, q_norm_b=m_q_norm_b,
                k_norm_b=m_k_norm_b, sink_b=m_sink_b, mem_norm_w=m_mem_norm_w, w_mem_kv=m_w_mem_kv,
                q_norm_c=m_q_norm_c, k_norm_c=m_k_norm_c, w_out=m_w_out)
    mom2 = dict(norm_w=v_norm_w, w_in=v_w_in, conv_w_a=v_conv_w_a, a_log_fwd=v_a_log_fwd, a_log_bwd=v_a_log_bwd,
                dt_bias_fwd=v_dt_bias_fwd, dt_bias_bwd=v_dt_bias_bwd, o_norm_a=v_o_norm_a, q_norm_b=v_q_norm_b,
                k_norm_b=v_k_norm_b, sink_b=v_sink_b, mem_norm_w=v_mem_norm_w, w_mem_kv=v_w_mem_kv,
                q_norm_c=v_q_norm_c, k_norm_c=v_k_norm_c, w_out=v_w_out)
    chip = 2 * lax.axis_index("x") + lax.axis_index("y")

    own_in = jnp.pad(jnp.transpose(w_in[0]).astype(BF16), ((0, W_IN_PAD - W_IN_BLOCK), (0, 0)))
    w_in4, conv4 = _all_gather_weights([own_in], conv_w_a[0])
    w_perm_t = _permute_blocks(w_in4)
    w_blocks_t = w_in4.reshape(N_CHIPS * W_IN_PAD, D_MODEL)
    conv_full = jnp.transpose(conv4, (1, 0, 2)).reshape(CONV_K, 3 * A_WIDTH)
    own_out, own_kv = w_out[0].astype(BF16), w_mem_kv[0].astype(BF16)

    def assemble(w_out4, w_kv4):
        w_out4 = lax.dynamic_update_index_in_dim(w_out4, own_out, chip, 0)
        w_kv4 = lax.dynamic_update_index_in_dim(w_kv4, own_kv, chip, 0)
        return w_out4.reshape(D_MODEL, D_MODEL), w_kv4.reshape(D_MODEL, 2 * C_HEADS * C_DIM)

    gather = (_PairedGather([own_out, own_kv]), assemble)
    pa = jnp.concatenate([_pad_row(a_log_fwd), _pad_row(a_log_bwd), _pad_row(dt_bias_fwd), _pad_row(dt_bias_bwd),
                          _pad_row(o_norm_a), jnp.zeros((3, LANE), F32)], axis=0)
    pb = jnp.concatenate([_pad_row(q_norm_b), _pad_row(k_norm_b), _pad_row(sink_b), jnp.zeros((5, LANE), F32)], axis=0)
    pc = jnp.concatenate([_pad_row(q_norm_c), _pad_row(k_norm_c), jnp.zeros((6, LANE), F32)], axis=0)

    full, got = {}, {}
    core = lax.axis_index("c").astype(jnp.int32).reshape(1)

    def pair_round(tag, blocks):
        names = [tag + "_%d" % i for i in range(len(blocks))]
        full.update(zip(names, blocks))
        got.update(zip(names, _pair_exchange(blocks, "grad_pair_exchange_" + tag)))
        return _ChipExchange([_pair_sum(full[n], got[n], core, "grad_pair_sum_" + n) for n in names])

    def early(g_w_out, g_w_kv):
        return pair_round("early", [g_w_out.reshape(N_CHIPS, D_MODEL // N_CHIPS, D_MODEL),
                                    g_w_kv.reshape(N_CHIPS, D_MODEL // N_CHIPS, 2 * C_HEADS * C_DIM)])

    def late(g_w_blocks_t):
        return pair_round("late", [g_w_blocks_t.reshape(N_CHIPS, W_IN_PAD, D_MODEL)])

    r = _local_step(x[0], mem[0], loss_target[0], norm_w, w_perm_t, w_blocks_t, conv_full, pa, pb, pc, mem_norm_w, None, None,
                    gather, (early, late))
    place = jnp.stack([chip, lax.axis_index("c")]).astype(jnp.int32)
    reduced = [_chip_sum(full[n], got[n], l, place, "grad_chip_sum_" + n)
               for n, l in zip(("late_0", "early_0", "early_1"), r["landed"])]
    g_w_in_t, g_w_out, g_w_kv = _pair_gather(reduced, [W_IN_BLOCK, D_MODEL // N_CHIPS, D_MODEL // N_CHIPS])

    d_pa, d_pb, d_pc = r["d_pa"], r["d_pb"], r["d_pc"]
    small_g = dict(norm_w=r["g_norm"], mem_norm_w=r["g_mem_norm"], o_norm_a=d_pa[4], q_norm_c=d_pc[0], k_norm_c=d_pc[1],
                   q_norm_b=d_pb[0, :B_DIM], k_norm_b=d_pb[1, :B_DIM], a_log_fwd=d_pa[0, :A_HEADS],
                   a_log_bwd=d_pa[1, :A_HEADS], dt_bias_fwd=d_pa[2, :A_HEADS], dt_bias_bwd=d_pa[3, :A_HEADS],
                   sink_b=d_pb[2, :B_HEADS])
    packed = _all_reduce_small(_pack_small(small_g, jnp.sum(r["loss_parts"][:, 0, 0]), r["g_conv"]))
    flat = packed.reshape(-1)
    loss = flat[SMALL_LOSS]
    conv_sum = flat[SMALL_CONV:].reshape(CONV_K, 3 * A_WIDTH)
    conv_cols = 3 * A_WIDTH // N_CHIPS
    g_conv = lax.dynamic_slice(conv_sum, (0, chip * conv_cols), (CONV_K, conv_cols))

    grads = _unpack_small(packed)
    grads["conv_w_a"] = g_conv
    delta, new_m, new_v = {}, {}, {}
    delta["conv_w_a"], new_m["conv_w_a"], new_v["conv_w_a"] = _adamw(conv_w_a[0], g_conv, m_conv_w_a[0], v_conv_w_a[0],
                                                                     "adamw_conv_w_a")
    for n, g in (("w_mem_kv", g_w_kv), ("w_out", g_w_out)):
        delta[n], new_m[n], new_v[n], grads[n] = _adamw(weights[n][0], g, mom1[n][0], mom2[n][0], "adamw_" + n, echo=True)
    stepped = _adamw(jnp.transpose(w_in[0]), g_w_in_t, jnp.transpose(m_w_in[0]), jnp.transpose(v_w_in[0]), "adamw_w_in",
                     echo=True)
    delta["w_in"], new_m["w_in"], new_v["w_in"], grads["w_in"] = (jnp.transpose(t) for t in stepped)
    d_s, m_s, v_s = _adamw(_pack_small(weights), packed, _pack_small(mom1), _pack_small(mom2), "adamw_small")
    d_s, m_s, v_s = _unpack_small(d_s), _unpack_small(m_s), _unpack_small(v_s)
    for n in SMALL_NAMES:
        delta[n], new_m[n], new_v[n] = d_s[n], m_s[n], v_s[n]

    def shaped(tree):
        return [tree[n].reshape(weights[n].shape) for n in WEIGHT_ORDER]

    return (loss, r["g_x"].reshape(x.shape), *shaped(grads), *shaped(delta), *shaped(new_m), *shaped(new_v))
```

```python
import jax
import jax.numpy as jnp
from jax import lax
from jax.experimental import pallas as pl
from jax.experimental.pallas import tpu as pltpu

F32 = jnp.float32
BF16 = jnp.bfloat16
HI = lax.Precision.HIGHEST
MESH = pl.DeviceIdType.MESH

D_MODEL = 2048
A_WIDTH = 1024
A_HEADS = 8
A_DIM = 128
CONV_K = 5
CHUNK = 64
B_HEADS = 8
B_KV = 2
B_DIM = 64
WINDOW = 128
C_HEADS = 4
C_DIM = 128
MEM_LEN = 256
ROPE_THETA = 10000.0
EPS = 1e-6
IN_WIDTH = 6432
N_CHIPS = 4
W_IN_BLOCK = IN_WIDTH // N_CHIPS
W_IN_PAD = 1664

LANE = 128
P_QA, P_KA, P_VA, P_ZA = 0, 1024, 2048, 3072
P_QB, P_ZB, P_QC, P_ZC = 4096, 4608, 5120, 5632
P_KB, P_VB, P_GT = 6144, 6272, 6400
P_WIDTH = 6656
O_GT, O_QB, O_KB, O_VB, O_ZB, O_QC, O_ZC = 4096, 4128, 4640, 4768, 4896, 5408, 5920

ADAM_LR, ADAM_B1, ADAM_B2, ADAM_EPS, ADAM_WD, ADAM_STEP = 0.001, 0.9, 0.999, 1e-08, 0.01, 10

VMEM_LIMIT = 56 * 1024 * 1024


def _params(sem=None):
    return pltpu.CompilerParams(dimension_semantics=sem, vmem_limit_bytes=VMEM_LIMIT)


def _dot(a, b, dims=(((1,), (0,)), ((), ())), precision=HI):
    return lax.dot_general(a, b, dims, precision=precision, preferred_element_type=F32)


_NN = (((1,), (0,)), ((), ()))
_NT = (((1,), (1,)), ((), ()))
_TN = (((0,), (0,)), ((), ()))


def _bdot(a, b, dims):
    return lax.dot_general(a.astype(BF16), b.astype(BF16), dims, preferred_element_type=F32)


@jax.custom_vjp
def _mm(a, b):
    return _bdot(a, b, _NN)


_mm.defvjp(lambda a, b: (_bdot(a, b, _NN), (a, b)),
           lambda res, ct: (_bdot(ct, res[1], _NT), _bdot(res[0], ct, _TN)))


@jax.custom_vjp
def _mm_nt(a, b):
    return _bdot(a, b, _NT)


_mm_nt.defvjp(lambda a, b: (_bdot(a, b, _NT), (a, b)),
              lambda res, ct: (_bdot(ct, res[1], _NN), _bdot(ct, res[0], _TN)))


@jax.custom_vjp
def _mm_tn(a, b):
    return _bdot(a, b, _TN)


_mm_tn.defvjp(lambda a, b: (_bdot(a, b, _TN), (a, b)),
              lambda res, ct: (_bdot(res[1], ct, _NT), _bdot(res[0], ct, _NN)))


def _rms(t, w):
    return t * lax.rsqrt(jnp.mean(t * t, axis=-1, keepdims=True) + EPS) * w


def _l2(t):
    return t * lax.rsqrt(jnp.sum(t * t, axis=-1, keepdims=True) + EPS)


def _silu(t):
    return t * jax.nn.sigmoid(t)


def _softplus(t):
    return jnp.maximum(t, 0.0) + jnp.log1p(jnp.exp(-jnp.abs(t)))


def _matmul(a, b, mode, out_dtype, name, tm=512, tn=512, tk=512, ride=None):
    (m, k) = a.shape[::-1] if mode == "tn" else a.shape
    n = b.shape[0] if mode == "nt" else b.shape[1]
    tm, tn, tk = min(tm, m), min(tn, n), min(tk, k)
    assert m % tm == 0 and n % tn == 0 and k % tk == 0, (m, n, k, tm, tn, tk)
    if mode == "nn":
        a_spec = pl.BlockSpec((tm, tk), lambda i, j, kk: (i, kk))
        b_spec = pl.BlockSpec((tk, tn), lambda i, j, kk: (kk, j))
        dims = (((1,), (0,)), ((), ()))
    elif mode == "nt":
        a_spec = pl.BlockSpec((tm, tk), lambda i, j, kk: (i, kk))
        b_spec = pl.BlockSpec((tn, tk), lambda i, j, kk: (j, kk))
        dims = (((1,), (1,)), ((), ()))
    else:
        a_spec = pl.BlockSpec((tk, tm), lambda i, j, kk: (kk, i))
        b_spec = pl.BlockSpec((tk, tn), lambda i, j, kk: (kk, j))
        dims = (((0,), (0,)), ((), ()))
    nk = k // tk
    grid = (m // tm, n // tn, nk)
    n_in = len(ride.operands) if ride else 0
    n_out = len(ride.out_shapes) if ride else 0

    def body(*refs):
        a_ref, b_ref, o_ref = refs[0], refs[1], refs[2 + n_in]
        scratch = refs[3 + n_in + n_out:]
        step = (pl.program_id(0) * grid[1] + pl.program_id(1)) * nk + pl.program_id(2)
        riders = (refs[2:2 + n_in], refs[3 + n_in:3 + n_in + n_out], scratch[(0 if nk == 1 else 1):])
        if ride:
            pl.when(step == 0)(lambda: ride.start(*riders))
        if nk == 1:
            o_ref[...] = _bdot(a_ref[...], b_ref[...], dims).astype(out_dtype)
        else:
            acc_ref, kk = scratch[0], pl.program_id(2)

            @pl.when(kk == 0)
            def _():
                acc_ref[...] = jnp.zeros_like(acc_ref)

            acc_ref[...] += _bdot(a_ref[...], b_ref[...], dims)

            @pl.when(kk == nk - 1)
            def _():
                o_ref[...] = acc_ref[...].astype(out_dtype)
        if ride:
            pl.when(step == grid[0] * grid[1] * nk - 1)(lambda: ride.finish(*riders))

    out = pl.pallas_call(
        body, name=name, grid=grid,
        in_specs=[a_spec, b_spec] + [HBM] * n_in,
        out_specs=[pl.BlockSpec((tm, tn), lambda i, j, kk: (i, j))] + [HBM] * n_out,
        out_shape=[jax.ShapeDtypeStruct((m, n), out_dtype)] + (list(ride.out_shapes) if ride else []),
        scratch_shapes=([] if nk == 1 else [pltpu.VMEM((tm, tn), F32)]) + (list(ride.scratch_shapes) if ride else []),
        compiler_params=_params(("arbitrary",) * 3 if ride else ("parallel", "parallel", "arbitrary")),
    )(a, b, *(ride.operands if ride else []))
    return out if ride else out[0]


def _rms_fwd(x, w, tr=256):
    s, d = x.shape

    def body(x_ref, w_ref, o_ref):
        o_ref[...] = _rms(x_ref[...], w_ref[...]).astype(BF16)

    return pl.pallas_call(
        body, name="rms_fwd", grid=(s // tr,),
        in_specs=[pl.BlockSpec((tr, d), lambda i: (i, 0)), pl.BlockSpec((1, d), lambda i: (0, 0))],
        out_specs=pl.BlockSpec((tr, d), lambda i: (i, 0)),
        out_shape=jax.ShapeDtypeStruct((s, d), BF16), compiler_params=_params(("parallel",)),
    )(x, w)


def _input_grad(d_proj, w_t, x, w, dy, ride=None, tm=512, tk=512):
    s, k = d_proj.shape
    d = w_t.shape[1]
    tm = min(tm, s)
    nk = k // tk
    grid = (s // tm, nk)
    n_in = len(ride.operands) if ride else 0
    n_out = len(ride.out_shapes) if ride else 0

    def body(*refs):
        a_ref, b_ref, x_ref, w_ref, dy_ref = refs[:5]
        gx_ref, gw_ref = refs[5 + n_in:7 + n_in]
        acc_ref = refs[7 + n_in + n_out]
        riders = (refs[5:5 + n_in], refs[7 + n_in:7 + n_in + n_out], refs[8 + n_in + n_out:])
        kk = pl.program_id(1)
        step = pl.program_id(0) * nk + kk
        if ride:
            pl.when(step == 0)(lambda: ride.start(*riders))

        @pl.when(step == 0)
        def _():
            gw_ref[...] = jnp.zeros_like(gw_ref)

        @pl.when(kk == 0)
        def _():
            acc_ref[...] = jnp.zeros_like(acc_ref)

        acc_ref[...] += _bdot(a_ref[...], b_ref[...], _NN)

        @pl.when(kk == nk - 1)
        def _():
            _, vjp = jax.vjp(_rms, x_ref[...], w_ref[...])
            dx, dw = vjp(acc_ref[...])
            gx_ref[...] = dy_ref[...] + dx
            gw_ref[...] += dw

        if ride:
            pl.when(step == grid[0] * nk - 1)(lambda: ride.finish(*riders))

    row = pl.BlockSpec((tm, d), lambda i, kk: (i, 0))
    vec = pl.BlockSpec((1, d), lambda i, kk: (0, 0))
    return pl.pallas_call(
        body, name="input_grad", grid=grid,
        in_specs=[pl.BlockSpec((tm, tk), lambda i, kk: (i, kk)), pl.BlockSpec((tk, d), lambda i, kk: (kk, 0)), row, vec, row]
        + [HBM] * n_in,
        out_specs=[row, vec] + [HBM] * n_out,
        out_shape=[jax.ShapeDtypeStruct((s, d), F32), jax.ShapeDtypeStruct((1, d), F32)]
        + (list(ride.out_shapes) if ride else []),
        scratch_shapes=[pltpu.VMEM((tm, d), F32)] + (list(ride.scratch_shapes) if ride else []),
        compiler_params=_params(("arbitrary", "arbitrary")),
    )(d_proj, w_t, x, w, dy, *(ride.operands if ride else []))


def _out_loss(mixed, w_out, x, target, tm=1024, tn=512):
    s, d = x.shape
    tm = min(tm, s)
    ni, nj = s // tm, d // tn

    def body(m_ref, w_ref, x_ref, t_ref, dy_ref, dyb_ref, l_ref):
        err = x_ref[...] + _bdot(m_ref[...], w_ref[...], _NN) - t_ref[...]
        dy = err * (1.0 / d)
        dy_ref[...] = dy
        dyb_ref[...] = dy.astype(BF16)
        l_ref[...] = jnp.full(l_ref.shape, 0.5 * jnp.sum(jnp.sum(err * err, axis=1, keepdims=True) * (1.0 / d)), F32)

    tile = pl.BlockSpec((tm, tn), lambda i, j: (i, j))
    return pl.pallas_call(
        body, name="out_loss", grid=(ni, nj),
        in_specs=[pl.BlockSpec((tm, mixed.shape[1]), lambda i, j: (i, 0)),
                  pl.BlockSpec((mixed.shape[1], tn), lambda i, j: (0, j)), tile, tile],
        out_specs=[tile, tile, pl.BlockSpec((1, 8, LANE), lambda i, j: (i * nj + j, 0, 0))],
        out_shape=[jax.ShapeDtypeStruct((s, d), F32), jax.ShapeDtypeStruct((s, d), BF16),
                   jax.ShapeDtypeStruct((ni * nj, 8, LANE), F32)],
        compiler_params=_params(("parallel", "parallel")),
    )(mixed, w_out, x, target)


def _shift_rows(t, s):
    if s == 0:
        return t
    n = t.shape[0]
    rolled = pltpu.roll(t, (-s) % n, axis=0)
    idx = lax.broadcasted_iota(jnp.int32, t.shape, 0) + s
    return jnp.where((idx >= 0) & (idx < n), rolled, 0.0)


CONV_FWD_COLS = 512
CONV_BWD_COLS = 128


def _conv_fwd(proj, conv_w):
    s = proj.shape[0]
    cols, split = CONV_FWD_COLS, A_WIDTH // CONV_FWD_COLS
    nblk = 3 * A_WIDTH // cols

    def body(x_ref, w_ref, o_ref):
        x = x_ref[...]
        acc = jnp.zeros_like(x)
        for j in range(CONV_K):
            acc = acc + w_ref[j:j + 1, :] * _shift_rows(x, j - CONV_K // 2)
        o_ref[...] = acc

    return pl.pallas_call(
        body, name="conv_fwd", grid=(nblk,),
        in_specs=[pl.BlockSpec((s, cols), lambda i: (0, i)), pl.BlockSpec((CONV_K, cols), lambda i: (0, i))],
        out_specs=pl.BlockSpec((None, s, cols), lambda i: (i // split, 0, i % split)),
        out_shape=jax.ShapeDtypeStruct((3, s, A_WIDTH), F32), compiler_params=_params(("parallel",)),
    )(proj, conv_w)


def _conv_bwd(proj, conv_w, d_c):
    s = proj.shape[0]
    cols, split = CONV_BWD_COLS, A_WIDTH // CONV_BWD_COLS
    nblk = 3 * A_WIDTH // cols

    def body(x_ref, w_ref, g_ref, dx_ref, dw_ref):
        x, g = x_ref[...], g_ref[...]
        acc = jnp.zeros_like(x)
        for j in range(CONV_K):
            off = j - CONV_K // 2
            acc = acc + w_ref[j:j + 1, :] * _shift_rows(g, -off)
            dw_ref[j:j + 1, :] = jnp.sum(_shift_rows(x, off) * g, axis=0, keepdims=True)
        dx_ref[...] = acc.astype(BF16)

    col = pl.BlockSpec((s, cols), lambda i: (0, i))
    wsp = pl.BlockSpec((CONV_K, cols), lambda i: (0, i))
    dsp = pl.BlockSpec((None, s, cols), lambda i: (i // split, 0, i % split))
    return pl.pallas_call(
        body, name="conv_bwd", grid=(nblk,), in_specs=[col, wsp, dsp], out_specs=[col, wsp],
        out_shape=[jax.ShapeDtypeStruct((s, 3 * A_WIDTH), BF16), jax.ShapeDtypeStruct((CONV_K, 3 * A_WIDTH), F32)],
        compiler_params=_params(("parallel",)),
    )(proj, conv_w, d_c)


A_FWD_HEADS = 4
A_BWD_HEADS = 4


def _neumann_inverse(a):
    c = a.shape[-1]
    eye = (lax.broadcasted_iota(jnp.int32, (c, c), 0) == lax.broadcasted_iota(jnp.int32, (c, c), 1)).astype(F32)
    tinv = eye + a
    p = a
    for _ in range(5):
        p = _mm(p, p)
        tinv = tinv + _mm(tinv, p)
    return tinv


@jax.custom_vjp
def _unit_inverse(a):
    return _neumann_inverse(a)


def _unit_inverse_fwd(a):
    tinv = _neumann_inverse(a)
    return tinv, tinv


def _unit_inverse_bwd(tinv, ct):
    return (_bdot(_bdot(tinv, ct, _TN), tinv, _NT),)


_unit_inverse.defvjp(_unit_inverse_fwd, _unit_inverse_bwd)


@jax.custom_vjp
def _known_inverse(a, tinv):
    return tinv


_known_inverse.defvjp(lambda a, tinv: (tinv, tinv),
                      lambda tinv, ct: (_unit_inverse_bwd(tinv, ct)[0], jnp.zeros_like(tinv)))


def _a_chain(st, cq, ck, cv, alpha, beta_raw, a_log, dt_b, incl, strict, last, kept=None):
    c = CHUNK
    gb = -jnp.exp(a_log) * _softplus(alpha + dt_b)
    bb = jax.nn.sigmoid(beta_raw)
    q = _l2(_silu(cq)) * (A_DIM ** -0.5)
    k = _l2(_silu(ck))
    v = _silu(cv)

    gc = _dot(incl, jnp.broadcast_to(gb, (c, LANE)))
    tot = jnp.sum(gc * last, axis=0, keepdims=True)
    m1 = gc[:, :c]
    decay = incl * jnp.exp(incl * (m1 - m1.T))
    kb = k * bb
    vb = v * bb
    a = -(strict * decay * _mm_nt(kb, k))
    tinv = _unit_inverse(a) if kept is None else _known_inverse(a, kept)
    eg = jnp.exp(gc)
    u = _mm(tinv, vb)
    w = _mm(tinv, kb * eg)
    qk = _mm_nt(q, k) * decay
    v_new = u - _mm(w, st)
    o = _mm(q * eg, st) + _mm(qk, v_new)
    st_new = st * jnp.exp(tot) + _mm_tn(k * jnp.exp(tot - gc), v_new)
    return st_new, o, tinv


def _a_step(sts, cq, ck, cv, gts, pa, h0, kept=None):
    c = CHUNK
    lane = lax.broadcasted_iota(jnp.int32, (1, LANE), 1)
    ii = lax.broadcasted_iota(jnp.int32, (c, c), 0)
    jj = lax.broadcasted_iota(jnp.int32, (c, c), 1)
    row = lax.broadcasted_iota(jnp.int32, (c, 1), 0)

    def pick(t, col):
        return jnp.sum(jnp.where(lane == col, t, 0.0), axis=1, keepdims=True)

    alpha, beta_raw, a_log, dt_b, incl, strict, last = [], [], [], [], [], [], []
    for b in range(sts.shape[0]):
        h, rev = h0 + b // 2, b % 2
        alpha.append(pick(gts[b], h + 8 * rev))
        beta_raw.append(pick(gts[b], h + 16 + 8 * rev))
        a_log.append(pick(pa[rev:rev + 1, :], h))
        dt_b.append(pick(pa[2 + rev:3 + rev, :], h))
        incl.append(((ii <= jj) if rev else (ii >= jj)).astype(F32))
        strict.append(((ii < jj) if rev else (ii > jj)).astype(F32))
        last.append((row == (0 if rev else c - 1)).astype(F32))
    stack = lambda ts: jnp.concatenate([t[None] for t in ts], axis=0)
    return jax.vmap(_a_chain)(sts, cq, ck, cv, stack(alpha), stack(beta_raw), stack(a_log), stack(dt_b),
                              stack(incl), stack(strict), stack(last), kept)


def _a_final(o, za, pa):
    outs = []
    for j in range(o.shape[1] // A_DIM):
        ln = slice(j * A_DIM, (j + 1) * A_DIM)
        outs.append(_rms(o[:, ln], pa[4:5, :]) * _silu(za[:, ln]))
    return jnp.concatenate(outs, axis=1)


def _a_tiles(n, nchunk, heads):
    tiles = []
    for b in range(2 * heads):
        i = (nchunk - 1 - n) if b % 2 else n
        tiles.append((i, pl.ds(pl.multiple_of(i * CHUNK, CHUNK), CHUNK), slice((b // 2) * A_DIM, (b // 2 + 1) * A_DIM)))
    return tiles


def _a_load(tiles, c_ref, gt_ref):
    cq, ck, cv = (jnp.stack([c_ref[r, sl, ln] for _, sl, ln in tiles], axis=0) for r in range(3))
    return cq, ck, cv, jnp.stack([gt_ref[sl, :] for _, sl, _ in tiles], axis=0)


def _loop_by_two(n, step, init):
    assert n % 2 == 0
    return lax.fori_loop(0, n // 2, lambda m, carry: step(2 * m + 1, step(2 * m, carry, 0), 1), init)


def _a_scan(h0, heads, nchunk, c_ref, gt_ref, pa, of_ref, ob_ref, s_ref, t_ref):
    def step(n, sts, parity):
        tiles = _a_tiles(n, nchunk, heads)
        sts_new, o, tinv = _a_step(sts, *_a_load(tiles, c_ref, gt_ref), pa, h0)
        for b, (i, sl, ln) in enumerate(tiles):
            s_ref[b, i] = sts[b]
            t_ref[b, i] = tinv[b]
            (ob_ref if b % 2 else of_ref)[sl, ln] = o[b]
        return sts_new

    _loop_by_two(nchunk, step, jnp.zeros((2 * heads, A_DIM, A_DIM), F32))


def _a_specs(s, heads):
    wide = heads * A_DIM
    once = pl.Buffered(1)
    trio = pl.BlockSpec((3, s, wide), lambda g: (0, 0, g), pipeline_mode=once)
    gates = pl.BlockSpec((s, LANE), lambda g: (0, P_GT // LANE))
    small = pl.BlockSpec((8, LANE), lambda g: (0, 0))

    def cols(base):
        return pl.BlockSpec((s, wide), lambda g: (0, base // wide + g), pipeline_mode=once)

    state = pl.BlockSpec((2 * heads, s // CHUNK, A_DIM, A_DIM), lambda g: (g, 0, 0, 0), pipeline_mode=once)
    kept = pl.BlockSpec((2 * heads, s // CHUNK, CHUNK, CHUNK), lambda g: (g, 0, 0, 0), pipeline_mode=once)
    return wide, trio, gates, small, cols, state, kept


def _delta_fwd(cqkv, proj, pa, ride=None):
    s = cqkv.shape[1]
    nchunk = s // CHUNK
    heads = A_FWD_HEADS
    steps = A_HEADS // heads
    wide, trio, gates, small, cols, state, kept = _a_specs(s, heads)
    n_in = len(ride.operands) if ride else 0
    n_out = len(ride.out_shapes) if ride else 0

    def body(*refs):
        c_ref, gt_ref, za_ref, pa_ref = refs[:4]
        out_ref, o_ref, s_ref, t_ref = refs[4 + n_in:8 + n_in]
        ob_ref = refs[8 + n_in + n_out]
        riders = (refs[4:4 + n_in], refs[8 + n_in:8 + n_in + n_out], refs[9 + n_in + n_out:])
        g = pl.program_id(0)
        if ride:
            pl.when(g == 0)(lambda: ride.start(*riders))
            pl.when(g == steps - 1)(lambda: ride.middle(*riders))
        h0 = g * heads
        pa_v = pa_ref[...]
        _a_scan(h0, heads, nchunk, c_ref, gt_ref, pa_v, o_ref, ob_ref, s_ref, t_ref)
        o_ref[...] += ob_ref[...]
        out_ref[...] = _a_final(o_ref[...], za_ref[...], pa_v).astype(BF16)
        if ride:
            pl.when(g == steps - 1)(lambda: ride.finish(*riders))

    assert steps > 1
    return pl.pallas_call(
        body, name="delta_fwd", grid=(steps,),
        in_specs=[trio, gates, cols(P_ZA), small] + [HBM] * n_in,
        out_specs=[cols(0), cols(0), state, kept] + [HBM] * n_out,
        out_shape=[jax.ShapeDtypeStruct((s, D_MODEL), BF16),
                   jax.ShapeDtypeStruct((s, A_WIDTH), F32),
                   jax.ShapeDtypeStruct((2 * A_HEADS, nchunk, A_DIM, A_DIM), F32),
                   jax.ShapeDtypeStruct((2 * A_HEADS, nchunk, CHUNK, CHUNK), F32)]
        + (list(ride.out_shapes) if ride else []),
        scratch_shapes=[pltpu.VMEM((s, wide), F32)] + (list(ride.scratch_shapes) if ride else []),
        compiler_params=_params(("arbitrary",)),
    )(cqkv, proj, proj, pa, *(ride.operands if ride else []))


def _delta_out_bwd(o_sum, proj, pa, d_mixed, tr=256):
    s = o_sum.shape[0]

    def body(o_ref, za_ref, pa_ref, dm_ref, do_ref, dza_ref, dpa_ref):
        @pl.when(pl.program_id(0) == 0)
        def _():
            dpa_ref[...] = jnp.zeros_like(dpa_ref)

        _, vjp = jax.vjp(_a_final, o_ref[...], za_ref[...], pa_ref[...])
        d_o, d_za, dpa = vjp(dm_ref[...].astype(F32))
        do_ref[...] = d_o
        dza_ref[...] = d_za.astype(BF16)
        dpa_ref[...] += dpa

    def rows(col):
        return pl.BlockSpec((tr, A_WIDTH), lambda i: (i, col))

    small = pl.BlockSpec((8, LANE), lambda i: (0, 0))
    return pl.pallas_call(
        body, name="delta_out_bwd", grid=(s // tr,), in_specs=[rows(0), rows(P_ZA // A_WIDTH), small, rows(0)],
        out_specs=[rows(0), rows(0), small],
        out_shape=[jax.ShapeDtypeStruct((s, A_WIDTH), F32), jax.ShapeDtypeStruct((s, A_WIDTH), BF16),
                   jax.ShapeDtypeStruct((8, LANE), F32)],
        compiler_params=_params(("arbitrary",)),
    )(o_sum, proj, pa, d_mixed)


def _delta_bwd(cqkv, proj, pa, d_o, states, inverses, ride=None):
    s = cqkv.shape[1]
    nchunk = s // CHUNK
    heads = A_BWD_HEADS
    steps = A_HEADS // heads
    wide, trio, gates, small, cols, _, _ = _a_specs(s, heads)
    n_in = len(ride.operands) if ride else 0
    n_out = len(ride.out_shapes) if ride else 0

    def body(*refs):
        c_ref, gt_ref, pa_ref, do_ref, s_hbm, t_hbm = refs[:6]
        dc_ref, dgt_ref, dpa_ref = refs[6 + n_in:9 + n_in]
        s_buf, t_buf, s_sems = refs[9 + n_in + n_out:12 + n_in + n_out]
        riders = (refs[6:6 + n_in], refs[9 + n_in:9 + n_in + n_out], refs[12 + n_in + n_out:])
        if ride:
            pl.when(pl.program_id(0) == 0)(lambda: ride.start(*riders))
        h0 = pl.program_id(0) * heads
        pa_v = pa_ref[...]

        @pl.when(h0 == 0)
        def _():
            dgt_ref[...] = jnp.zeros_like(dgt_ref)
            dpa_ref[...] = jnp.zeros_like(dpa_ref)

        dc_ref[...] = jnp.zeros_like(dc_ref)

        def state_copies(n, slot):
            tiles = _a_tiles(nchunk - 1 - n, nchunk, heads)
            return ([pltpu.make_async_copy(s_hbm.at[2 * h0 + b, i], s_buf.at[slot, b], s_sems.at[0, slot, b])
                     for b, (i, _, _) in enumerate(tiles)]
                    + [pltpu.make_async_copy(t_hbm.at[2 * h0 + b, i], t_buf.at[slot, b], s_sems.at[1, slot, b])
                       for b, (i, _, _) in enumerate(tiles)])

        for cp in state_copies(0, 0):
            cp.start()

        def step(n, carry, parity):
            d_sts, dpa = carry
            tiles = _a_tiles(nchunk - 1 - n, nchunk, heads)
            for cp in state_copies(n, parity):
                cp.wait()

            @pl.when(n + 1 < nchunk)
            def _():
                for cp in state_copies(n + 1, 1 - parity):
                    cp.start()

            sts, kept = s_buf[parity], t_buf[parity]
            d_o_t = jnp.stack([do_ref[sl, ln] for _, sl, ln in tiles], axis=0)
            _, vjp_c = jax.vjp(lambda *a: _a_step(*a, h0, kept)[:2], sts, *_a_load(tiles, c_ref, gt_ref), pa_v)
            d_prev, dcq, dck, dcv, dgts, dpa_i = vjp_c((d_sts, d_o_t))
            for b, (_, sl, ln) in enumerate(tiles):
                for r, dc in enumerate((dcq, dck, dcv)):
                    dc_ref[r, sl, ln] += dc[b]
                dgt_ref[sl, :] += dgts[b]
            return d_prev, dpa + dpa_i

        init = (jnp.zeros((2 * heads, A_DIM, A_DIM), F32), jnp.zeros((8, LANE), F32))
        _, dpa_out = lax.fori_loop(0, nchunk, lambda n, carry: step(n, carry, n % 2), init)
        dpa_ref[...] += dpa_out
        if ride:
            pl.when(pl.program_id(0) == steps - 1)(lambda: ride.finish(*riders))

    fixed = pl.BlockSpec((s, LANE), lambda g: (0, 0))
    return pl.pallas_call(
        body, name="delta_bwd", grid=(steps,),
        in_specs=[trio, gates, small, cols(0), pl.BlockSpec(memory_space=pl.ANY), pl.BlockSpec(memory_space=pl.ANY)]
        + [HBM] * n_in,
        out_specs=[trio, fixed, small] + [HBM] * n_out,
        out_shape=[jax.ShapeDtypeStruct((3, s, A_WIDTH), F32), jax.ShapeDtypeStruct((s, LANE), F32),
                   jax.ShapeDtypeStruct((8, LANE), F32)] + (list(ride.out_shapes) if ride else []),
        scratch_shapes=[pltpu.VMEM((2, 2 * heads, A_DIM, A_DIM), F32), pltpu.VMEM((2, 2 * heads, CHUNK, CHUNK), F32),
                        pltpu.SemaphoreType.DMA((2, 2, 2 * heads))]
        + (list(ride.scratch_shapes) if ride else []),
        compiler_params=_params(("arbitrary",)),
    )(cqkv, proj, pa, d_o, states, inverses, *(ride.operands if ride else []))


def _rope_tables(s):
    inv = ROPE_THETA ** (-jnp.arange(0, B_DIM, 2, dtype=F32) / B_DIM)
    ang = jnp.arange(s, dtype=F32)[:, None] * inv[None, :]
    cos, sin = jnp.cos(ang), jnp.sin(ang)
    return jnp.concatenate([cos, cos], axis=1), jnp.concatenate([-sin, sin], axis=1)


def _b_block(q_t, z_t, k3, v3, cos_q, sin_q, cos_k, sin_k, pb, n, nb):
    w = WINDOW
    def swap(t):
        return jnp.concatenate([t[:, B_DIM // 2:], t[:, :B_DIM // 2]], axis=1)

    grp = B_HEADS // B_KV
    qi = lax.broadcasted_iota(jnp.int32, (grp * w, 3 * w), 0) & (w - 1)
    kj = lax.broadcasted_iota(jnp.int32, (grp * w, 3 * w), 1)
    kpos = kj + (n - 1) * w
    mask = (jnp.abs(kj - w - qi) <= w) & (kpos >= 0) & (kpos < nb * w)
    lane = lax.broadcasted_iota(jnp.int32, (1, LANE), 1)
    qn, kn = pb[0:1, :B_DIM], pb[1:2, :B_DIM]
    cos_g = jnp.concatenate([cos_q] * grp, axis=0)
    sin_g = jnp.concatenate([sin_q] * grp, axis=0)
    def group(q, k, v, sink):
        k = _rms(k, kn)
        k = k * cos_k + swap(k) * sin_k
        q = _rms(q, qn)
        q = q * cos_g + swap(q) * sin_g
        s = _mm_nt(q, k) * (B_DIM ** -0.5)
        s = jnp.where(mask, s, -jnp.inf)
        m = jnp.maximum(jnp.max(s, axis=1, keepdims=True), sink)
        p = jnp.exp(s - m)
        p = p / (jnp.sum(p, axis=1, keepdims=True) + jnp.exp(sink - m))
        return _mm(p, v)

    stack = lambda ts: jnp.concatenate([t[None] for t in ts], axis=0)
    qs, ks, vs, sinks = [], [], [], []
    for hk in range(B_KV):
        heads = [hk * grp + g for g in range(grp)]
        ks.append(k3[:, hk * B_DIM:(hk + 1) * B_DIM])
        vs.append(v3[:, hk * B_DIM:(hk + 1) * B_DIM])
        qs.append(jnp.concatenate([q_t[:, hq * B_DIM:(hq + 1) * B_DIM] for hq in heads], axis=0))
        sinks.append(jnp.concatenate(
            [jnp.broadcast_to(jnp.sum(jnp.where(lane == hq, pb[2:3, :], 0.0), axis=1, keepdims=True), (w, 1))
             for hq in heads], axis=0))
    o = jax.vmap(group)(stack(qs), stack(ks), stack(vs), stack(sinks))
    outs = [o[hk, g * w:(g + 1) * w, :] for hk in range(B_KV) for g in range(grp)]
    return jnp.concatenate(outs, axis=1) * _silu(z_t)


def _b_specs(s):
    nb = s // WINDOW
    qsp = pl.BlockSpec((WINDOW, 512), lambda n: (n, P_QB // 512))
    zsp = pl.BlockSpec((WINDOW, 512), lambda n: (n, P_ZB // 512))

    def three(col, width):
        return [pl.BlockSpec((WINDOW, width), lambda n: (jnp.maximum(n - 1, 0), col)),
                pl.BlockSpec((WINDOW, width), lambda n: (n, col)),
                pl.BlockSpec((WINDOW, width), lambda n: (jnp.minimum(n + 1, nb - 1), col))]

    tab = pl.BlockSpec((WINDOW, B_DIM), lambda n: (n, 0))
    small = pl.BlockSpec((8, LANE), lambda n: (0, 0))
    specs = [qsp, zsp] + three(P_KB // LANE, LANE) + three(P_VB // LANE, LANE) + [tab, tab] + three(0, B_DIM) + three(0, B_DIM) + [small]
    return nb, specs


def _b_args(proj, cos2, sin2, pb):
    return (proj, proj, proj, proj, proj, proj, proj, proj, cos2, sin2, cos2, cos2, cos2, sin2, sin2, sin2, pb)


def _b_load(refs):
    (q_ref, z_ref, kp, kc, kx, vp, vc, vx, cq, sq, ckp, ckc, ckx, skp, skc, skx, pb_ref) = refs
    cat = lambda *r: jnp.concatenate([t[...] for t in r], axis=0)
    return (q_ref[...], z_ref[...], cat(kp, kc, kx), cat(vp, vc, vx), cq[...], sq[...], cat(ckp, ckc, ckx),
            cat(skp, skc, skx), pb_ref[...])


def _attn_b_fwd(proj, cos2, sin2, pb, mixed):
    s = proj.shape[0]
    nb, specs = _b_specs(s)

    def body(*refs):
        o_ref = refs[-1]
        args = _b_load(refs[:-2])
        o_ref[...] = _b_block(*args, pl.program_id(0), nb).astype(BF16)

    return pl.pallas_call(
        body, name="attn_b_fwd", grid=(nb,), in_specs=specs + [pl.BlockSpec(memory_space=pl.ANY)],
        out_specs=pl.BlockSpec((WINDOW, 512), lambda n: (n, A_WIDTH // 512)),
        out_shape=jax.ShapeDtypeStruct(mixed.shape, mixed.dtype), input_output_aliases={len(specs): 0},
        compiler_params=_params(("parallel",)),
    )(*_b_args(proj, cos2, sin2, pb), mixed)


def _attn_b_bwd(proj, cos2, sin2, pb, d_mixed):
    s = proj.shape[0]
    nb, specs = _b_specs(s)
    w = WINDOW

    def body(*refs):
        dm_ref, dq_ref, dz_ref, dk_ref, dv_ref, dpb_ref = refs[-6:]
        n = pl.program_id(0)
        q_t, z_t, k3, v3, cq, sq, ck, sk, pb_v = _b_load(refs[:-6])

        @pl.when(n == 0)
        def _():
            dk_ref[...] = jnp.zeros_like(dk_ref)
            dv_ref[...] = jnp.zeros_like(dv_ref)
            dpb_ref[...] = jnp.zeros_like(dpb_ref)

        def f(q_, z_, k_, v_, pb_):
            return _b_block(q_, z_, k_, v_, cq, sq, ck, sk, pb_, n, nb)

        _, vjp = jax.vjp(f, q_t, z_t, k3, v3, pb_v)
        dq, dz, dk3, dv3, dpb = vjp(dm_ref[...])
        dq_ref[...] = dq.astype(BF16)
        dz_ref[...] = dz.astype(BF16)
        dpb_ref[...] += dpb

        def add(j, cond):
            @pl.when(cond)
            def _():
                rows = pl.ds(pl.multiple_of((n - 1 + j) * w, w), w)
                dk_ref[rows, :] += dk3[j * w:(j + 1) * w, :]
                dv_ref[rows, :] += dv3[j * w:(j + 1) * w, :]

        add(0, n > 0)
        add(1, n >= 0)
        add(2, n < nb - 1)

    blk = pl.BlockSpec((w, 512), lambda n: (n, 0))
    whole = pl.BlockSpec((s, LANE), lambda n: (0, 0))
    small = pl.BlockSpec((8, LANE), lambda n: (0, 0))
    return pl.pallas_call(
        body, name="attn_b_bwd", grid=(nb,),
        in_specs=specs + [pl.BlockSpec((w, 512), lambda n: (n, 2))],
        out_specs=[blk, blk, whole, whole, small],
        out_shape=[jax.ShapeDtypeStruct((s, 512), BF16), jax.ShapeDtypeStruct((s, 512), BF16),
                   jax.ShapeDtypeStruct((s, LANE), F32), jax.ShapeDtypeStruct((s, LANE), F32),
                   jax.ShapeDtypeStruct((8, LANE), F32)],
        compiler_params=_params(("arbitrary",)),
    )(*_b_args(proj, cos2, sin2, pb), d_mixed)


def _mem_kv_fwd(mem, mem_norm_w, w_kv):
    def body(mem_ref, nw_ref, w_ref, kv_ref):
        mn = _rms(mem_ref[...], nw_ref[...]).astype(BF16)
        kv_ref[...] = jnp.dot(mn, w_ref[...], preferred_element_type=F32)

    return pl.pallas_call(
        body, name="mem_kv_fwd", out_shape=jax.ShapeDtypeStruct((MEM_LEN, 2 * C_HEADS * C_DIM), F32),
        compiler_params=_params(),
    )(mem, mem_norm_w, w_kv)


def _mem_kv_bwd(mem, mem_norm_w, w_kv, d_kv):
    def body(mem_ref, nw_ref, w_ref, g_ref, gw_ref, gn_ref):
        mn, vjp = jax.vjp(_rms, mem_ref[...], nw_ref[...])
        g = g_ref[...].astype(BF16)
        gw_ref[...] = lax.dot_general(mn.astype(BF16), g, (((0,), (0,)), ((), ())), preferred_element_type=F32)
        d_mn = lax.dot_general(g, w_ref[...], (((1,), (1,)), ((), ())), preferred_element_type=F32)
        gn_ref[...] = vjp(d_mn)[1]

    return pl.pallas_call(
        body, name="mem_kv_bwd",
        out_shape=[jax.ShapeDtypeStruct((D_MODEL, 2 * C_HEADS * C_DIM), F32), jax.ShapeDtypeStruct((1, D_MODEL), F32)],
        compiler_params=_params(),
    )(mem, mem_norm_w, w_kv, d_kv)


def _c_tile(q_t, z_t, kvm, pc):
    width = C_HEADS * C_DIM
    outs = []
    for h in range(C_HEADS):
        q = _rms(q_t[:, h * C_DIM:(h + 1) * C_DIM], pc[0:1, :])
        k = _rms(kvm[:, h * C_DIM:(h + 1) * C_DIM], pc[1:2, :])
        v = kvm[:, width + h * C_DIM:width + (h + 1) * C_DIM]
        s = _mm_nt(q, k) * (C_DIM ** -0.5)
        p = jnp.exp(s - jnp.max(s, axis=1, keepdims=True))
        p = p / jnp.sum(p, axis=1, keepdims=True)
        outs.append(_mm(p, v))
    return jnp.concatenate(outs, axis=1) * _silu(z_t)


def _attn_c_fwd(proj, kvm, pc, mixed, tq=256):
    s = proj.shape[0]

    def body(q_ref, z_ref, kv_ref, pc_ref, mixed_ref, o_ref):
        o_ref[...] = _c_tile(q_ref[...], z_ref[...], kv_ref[...], pc_ref[...]).astype(BF16)

    return pl.pallas_call(
        body, name="attn_c_fwd", grid=(s // tq,),
        in_specs=[pl.BlockSpec((tq, 512), lambda i: (i, P_QC // 512)), pl.BlockSpec((tq, 512), lambda i: (i, P_ZC // 512)),
                  pl.BlockSpec(kvm.shape, lambda i: (0, 0)), pl.BlockSpec((8, LANE), lambda i: (0, 0)),
                  pl.BlockSpec(memory_space=pl.ANY)],
        out_specs=pl.BlockSpec((tq, 512), lambda i: (i, (A_WIDTH + 512) // 512)),
        out_shape=jax.ShapeDtypeStruct(mixed.shape, mixed.dtype), input_output_aliases={4: 0},
        compiler_params=_params(("parallel",)),
    )(proj, proj, kvm, pc, mixed)


def _attn_c_bwd(proj, kvm, pc, d_mixed, tq=256):
    s = proj.shape[0]

    def body(q_ref, z_ref, kv_ref, pc_ref, dm_ref, dq_ref, dz_ref, dkv_ref, dpc_ref):
        @pl.when(pl.program_id(0) == 0)
        def _():
            dkv_ref[...] = jnp.zeros_like(dkv_ref)
            dpc_ref[...] = jnp.zeros_like(dpc_ref)

        _, vjp = jax.vjp(_c_tile, q_ref[...], z_ref[...], kv_ref[...], pc_ref[...])
        dq, dz, dkv, dpc = vjp(dm_ref[...])
        dq_ref[...] = dq.astype(BF16)
        dz_ref[...] = dz.astype(BF16)
        dkv_ref[...] += dkv
        dpc_ref[...] += dpc

    blk = pl.BlockSpec((tq, 512), lambda i: (i, 0))
    kvs = pl.BlockSpec(kvm.shape, lambda i: (0, 0))
    small = pl.BlockSpec((8, LANE), lambda i: (0, 0))
    return pl.pallas_call(
        body, name="attn_c_bwd", grid=(s // tq,),
        in_specs=[pl.BlockSpec((tq, 512), lambda i: (i, P_QC // 512)), pl.BlockSpec((tq, 512), lambda i: (i, P_ZC // 512)),
                  kvs, small, pl.BlockSpec((tq, 512), lambda i: (i, 3))],
        out_specs=[blk, blk, kvs, small],
        out_shape=[jax.ShapeDtypeStruct((s, 512), BF16), jax.ShapeDtypeStruct((s, 512), BF16),
                   jax.ShapeDtypeStruct(kvm.shape, F32), jax.ShapeDtypeStruct((8, LANE), F32)],
        compiler_params=_params(("arbitrary",)),
    )(proj, proj, kvm, pc, d_mixed)


def _pad_row(v, width=LANE):
    v = v.reshape(1, -1)
    return jnp.pad(v, ((0, 0), (0, width - v.shape[1])))


def _local_step(x, mem, target, norm_w, w_perm_t, w_blocks_t, conv_w, pa, pb, pc, mem_norm_w, w_kv, w_out, gather=None,
                exchange=None):
    s = x.shape[0]
    cos2, sin2 = _rope_tables(s)
    hn = _rms_fwd(x, norm_w)
    wide = dict(tm=1024, tn=512, tk=2048)
    proj = _matmul(hn, w_perm_t, "nt", F32, "mm_proj", **wide)
    cqkv = _conv_fwd(proj, conv_w)
    if gather is None:
        mixed, o_sum, states, inverses = _delta_fwd(cqkv, proj, pa)
    else:
        mixed, o_sum, states, inverses, *arrived = _delta_fwd(cqkv, proj, pa, gather[0])
        w_out, w_kv = gather[1](*arrived)
    mixed = _attn_b_fwd(proj, cos2, sin2, pb, mixed)
    kvm = _mem_kv_fwd(mem, mem_norm_w, w_kv)
    mixed = _attn_c_fwd(proj, kvm, pc, mixed)
    dy, dyb, loss_parts = _out_loss(mixed, w_out, x, target)

    d_mixed = _matmul(dyb, w_out, "nt", F32, "mm_dmixed", **wide)
    g_w_out = _matmul(mixed, dyb, "tn", F32, "mm_gwout", **wide)
    d_qc, d_zc, d_kvm, d_pc = _attn_c_bwd(proj, kvm, pc, d_mixed)
    g_w_kv, g_mem_norm = _mem_kv_bwd(mem, mem_norm_w, w_kv, d_kvm)
    d_qb, d_zb, d_kb, d_vb, d_pb = _attn_b_bwd(proj, cos2, sin2, pb, d_mixed)
    d_o, d_za, d_pa_out = _delta_out_bwd(o_sum, proj, pa, d_mixed)
    early = exchange[0](g_w_out, g_w_kv) if exchange else None
    d_c, d_gt, d_pa_scan, *landed_early = _delta_bwd(cqkv, proj, pa, d_o, states, inverses, early)
    d_pa = d_pa_out + d_pa_scan
    d_qkv, g_conv = _conv_bwd(proj, conv_w, d_c)
    d_proj = _cotangent_blocks(d_qkv, d_za, d_gt, d_qb, d_kb, d_vb, d_zb, d_qc, d_zc)
    g_w_blocks_t = _matmul(d_proj, hn, "tn", F32, "mm_gwin", tm=512, tn=2048, tk=2048)
    late = exchange[1](g_w_blocks_t) if exchange else None
    g_x, g_norm, *landed_late = _input_grad(d_proj, w_blocks_t, x, norm_w, dy, late)
    return dict(loss_parts=loss_parts, g_x=g_x, g_norm=g_norm, g_w_blocks_t=g_w_blocks_t, g_conv=g_conv, d_pa=d_pa,
                d_pb=d_pb, d_pc=d_pc, g_mem_norm=g_mem_norm, g_w_kv=g_w_kv, g_w_out=g_w_out,
                landed=landed_late + landed_early)


_SEGMENTS = ((0, O_GT, 0), (O_GT, O_QB, P_GT), (O_QB, O_KB, P_QB), (O_KB, O_VB, P_KB), (O_VB, O_ZB, P_VB),
             (O_ZB, O_QC, P_ZB), (O_QC, O_ZC, P_QC), (O_ZC, IN_WIDTH, P_ZC))


def _permute_blocks(w4):
    parts = []
    for first, end, _ in sorted(_SEGMENTS, key=lambda seg: seg[2]):
        row = first
        while row < end:
            k = row // W_IN_BLOCK
            stop = min(end, (k + 1) * W_IN_BLOCK)
            parts.append(w4[k][row - k * W_IN_BLOCK:stop - k * W_IN_BLOCK, :])
            row = stop
    parts.append(jnp.zeros((P_WIDTH - IN_WIDTH, w4.shape[2]), w4.dtype))
    return jnp.concatenate(parts, axis=0)


def _cotangent_blocks(d_qkv, d_za, d_gt, d_qb, d_kb, d_vb, d_zb, d_qc, d_zc):
    s = d_qkv.shape[0]
    tr = min(256, s)
    pieces = (d_qkv, d_za, d_gt, d_qb, d_kb, d_vb, d_zb, d_qc, d_zc)

    def body(*refs):
        o_ref = refs[-1]
        tiles = [r[...].astype(BF16) for r in refs[:-1]]
        tiles[2] = tiles[2][:, :O_QB - O_GT]
        orig = jnp.concatenate(tiles, axis=1)
        pad = jnp.zeros((tr, W_IN_PAD - W_IN_BLOCK), BF16)
        parts = []
        for k in range(N_CHIPS):
            parts += [orig[:, k * W_IN_BLOCK:(k + 1) * W_IN_BLOCK], pad]
        o_ref[...] = jnp.concatenate(parts, axis=1)

    return pl.pallas_call(
        body, name="cotangent_blocks", grid=(s // tr,),
        in_specs=[pl.BlockSpec((tr, p.shape[1]), lambda i: (i, 0)) for p in pieces],
        out_specs=pl.BlockSpec((tr, N_CHIPS * W_IN_PAD), lambda i: (i, 0)),
        out_shape=jax.ShapeDtypeStruct((s, N_CHIPS * W_IN_PAD), BF16), compiler_params=_params(("parallel",)),
    )(*pieces)


HBM = pl.BlockSpec(memory_space=pltpu.HBM)


def _place():
    x, y, c = lax.axis_index("x"), lax.axis_index("y"), lax.axis_index("c")
    chips = [(1 - x, y), (x, 1 - y), (1 - x, 1 - y)]
    return x, y, c, 2 * x + y, chips, [2 * cx + cy for cx, cy in chips]


PIECE_ROWS_CAP = 600


def _remote(src, dst, send_sems, recv_sems, k, to):
    return pltpu.make_async_remote_copy(src_ref=src, dst_ref=dst, send_sem=send_sems.at[k], recv_sem=recv_sems.at[k],
                                        device_id=to, device_id_type=MESH)


def _half_cols(ref, c):
    half = ref.shape[-1] // 2
    return pl.ds(pl.multiple_of(c * half, LANE), half)


class _PairedGather:
    def __init__(self, blocks):
        n = len(blocks)
        self.operands = list(blocks)
        self.out_shapes = [jax.ShapeDtypeStruct((N_CHIPS,) + b.shape, b.dtype) for b in blocks]
        self.scratch_shapes = [pltpu.SemaphoreType.DMA((6 * n,)), pltpu.SemaphoreType.DMA((6 * n,))]

    @staticmethod
    def _copies(srcs, dsts, sems):
        x, y, c, me, chips, chip_ids = _place()
        sends, landed, passes, passed = [], [], [], []
        for a, (src, dst) in enumerate(zip(srcs, dsts)):
            mine, other = _half_cols(src, c), _half_cols(src, 1 - c)
            for j, (chip, cid) in enumerate(zip(chips, chip_ids)):
                sends.append(_remote(src.at[:, mine], dst.at[me, :, mine], sems[0], sems[1], 6 * a + j, (*chip, c)))
                here = dst.at[cid, :, mine]
                landed.append(_remote(here, here, sems[0], sems[1], 6 * a + j, (x, y, 1 - c)))
                passes.append(_remote(here, here, sems[0], sems[1], 6 * a + 3 + j, (x, y, 1 - c)))
                there = dst.at[cid, :, other]
                passed.append(_remote(there, there, sems[0], sems[1], 6 * a + 3 + j, (x, y, 1 - c)))
        return sends, landed, passes, passed

    def start(self, srcs, dsts, sems):
        for cp in self._copies(srcs, dsts, sems)[0]:
            cp.start()

    def middle(self, srcs, dsts, sems):
        _, landed, passes, _ = self._copies(srcs, dsts, sems)
        for arrived, onward in zip(landed, passes):
            arrived.wait_recv()
            onward.start()

    def finish(self, srcs, dsts, sems):
        sends, _, passes, passed = self._copies(srcs, dsts, sems)
        for cp in passed:
            cp.wait_recv()
        for cp in sends + passes:
            cp.wait_send()


def _all_gather_weights(bigs, conv_b):
    bigs = tuple(bigs)
    n_big = len(bigs)

    def body(*refs):
        srcs, conv_src = refs[:n_big], refs[n_big]
        dsts, conv_dst = refs[n_big + 1:2 * n_big + 1], refs[2 * n_big + 1]
        send_sems, recv_sems, local_sems = refs[2 * n_big + 2:]
        x, y, c, me, chips, chip_ids = _place()
        sibling = (x, y, 1 - c)
        local = [pltpu.make_async_copy(src, dst.at[me], local_sems.at[a]) for a, (src, dst) in enumerate(zip(srcs, dsts))]
        local.append(pltpu.make_async_copy(conv_src, conv_dst.at[me], local_sems.at[n_big]))
        for cp in local:
            cp.start()
        sends = []
        for a, (src, dst) in enumerate(zip(srcs, dsts)):
            mine = _half_cols(src, c)
            for j, chip in enumerate(chips):
                sends.append(_remote(src.at[:, mine], dst.at[me, :, mine], send_sems, recv_sems, 6 * a + j, (*chip, c)))
        for j, chip in enumerate(chips):
            sends.append(_remote(conv_src, conv_dst.at[me], send_sems, recv_sems, 6 * n_big + j, (*chip, c)))
        for cp in sends:
            cp.start()
        passed = []
        for a, (src, dst) in enumerate(zip(srcs, dsts)):
            mine = _half_cols(src, c)
            for j, cid in enumerate(chip_ids):
                landed = dst.at[cid, :, mine]
                _remote(landed, landed, send_sems, recv_sems, 6 * a + j, sibling).wait_recv()
                cp = _remote(landed, landed, send_sems, recv_sems, 6 * a + 3 + j, sibling)
                cp.start()
                passed.append(cp)
        for a, (src, dst) in enumerate(zip(srcs, dsts)):
            other = _half_cols(src, 1 - c)
            for j, cid in enumerate(chip_ids):
                landed = dst.at[cid, :, other]
                _remote(landed, landed, send_sems, recv_sems, 6 * a + 3 + j, sibling).wait_recv()
        for j, cid in enumerate(chip_ids):
            _remote(conv_src, conv_dst.at[cid], send_sems, recv_sems, 6 * n_big + j, sibling).wait_recv()
        for cp in sends + passed:
            cp.wait_send()
        for cp in local:
            cp.wait()

    n_sem = 6 * n_big + 3
    return pl.pallas_call(
        body, name="all_gather_weights",
        out_shape=[jax.ShapeDtypeStruct((N_CHIPS,) + w.shape, w.dtype) for w in bigs + (conv_b,)],
        in_specs=[pl.BlockSpec(memory_space=pltpu.VMEM)] * (n_big + 1), out_specs=[HBM] * (n_big + 1),
        scratch_shapes=[pltpu.SemaphoreType.DMA((n_sem,)), pltpu.SemaphoreType.DMA((n_sem,)),
                        pltpu.SemaphoreType.DMA((n_big + 1,))],
        compiler_params=_params(),
    )(*bigs, conv_b)


def _pair_exchange(grads, name):
    n = len(grads)
    pieces = [_row_tile(g.shape[1]) for g in grads]

    def body(*refs):
        srcs, gots = refs[:n], refs[n:2 * n]
        stages, narrow = refs[2 * n:3 * n], refs[3 * n:4 * n]
        send_sems, recv_sems, load_sems = refs[4 * n:]
        x, y, c, _, _, _ = _place()
        sibling = (x, y, 1 - c)
        for a in range(n):
            slabs, rows, _ = gots[a].shape
            piece = pieces[a]
            per_slab = rows // piece
            theirs = _half_cols(srcs[a], 1 - c)
            loads, sends = [], []
            for i in range(slabs * per_slab):
                k, r, slot = i // per_slab, i % per_slab, i % 2
                part = pl.ds(r * piece, piece)
                loads.append(pltpu.make_async_copy(srcs[a].at[k, part, theirs], stages[a].at[slot], load_sems.at[2 * a + slot]))
                sends.append(pltpu.make_async_remote_copy(
                    src_ref=narrow[a].at[slot], dst_ref=gots[a].at[k, part, :],
                    send_sem=send_sems.at[2 * a + slot], recv_sem=recv_sems.at[a], device_id=sibling, device_id_type=MESH))
            loads[0].start()
            for i in range(len(loads)):
                loads[i].wait()
                narrow[a][i % 2] = stages[a][i % 2].astype(BF16)
                sends[i].start()
                if i + 1 < len(loads):
                    if i >= 1:
                        sends[i - 1].wait_send()
                    loads[i + 1].start()
            for cp in sends[-2:]:
                cp.wait_send()
        for a in range(n):
            pltpu.make_async_remote_copy(src_ref=gots[a], dst_ref=gots[a], send_sem=send_sems.at[2 * a],
                                         recv_sem=recv_sems.at[a], device_id=sibling, device_id_type=MESH).wait_recv()

    halves = [jax.ShapeDtypeStruct((g.shape[0], g.shape[1], g.shape[2] // 2), BF16) for g in grads]
    return pl.pallas_call(
        body, name=name, out_shape=halves, in_specs=[HBM] * n, out_specs=[HBM] * n,
        scratch_shapes=[pltpu.VMEM((2, piece, g.shape[2] // 2), dt) for dt in (F32, BF16) for piece, g in zip(pieces, grads)]
        + [pltpu.SemaphoreType.DMA((2 * n,)), pltpu.SemaphoreType.DMA((n,)), pltpu.SemaphoreType.DMA((2 * n,))],
        compiler_params=_params(),
    )(*grads)


class _ChipExchange:
    def __init__(self, halves):
        n = len(halves)
        self.operands = list(halves)
        self.out_shapes = [jax.ShapeDtypeStruct((N_CHIPS - 1,) + h.shape[1:], h.dtype) for h in halves]
        self.scratch_shapes = [pltpu.SemaphoreType.DMA((3 * n,)), pltpu.SemaphoreType.DMA((3 * n,))]

    @staticmethod
    def _copies(srcs, lands, sems):
        _, _, c, _, chips, chip_ids = _place()
        return [_remote(src.at[cid], land.at[j], sems[0], sems[1], 3 * a + j, (*chip, c))
                for a, (src, land) in enumerate(zip(srcs, lands)) for j, (chip, cid) in enumerate(zip(chips, chip_ids))]

    def start(self, srcs, lands, sems):
        for cp in self._copies(srcs, lands, sems):
            cp.start()

    def finish(self, srcs, lands, sems):
        copies = self._copies(srcs, lands, sems)
        for cp in copies:
            cp.wait_recv()
        for cp in copies:
            cp.wait_send()


def _pair_gather(halves, rows):
    n = len(halves)

    def body(*refs):
        srcs, fulls = refs[:n], refs[n:2 * n]
        send_sems, recv_sems, local_sems = refs[2 * n:]
        x, y, c, _, _, _ = _place()
        copies = []
        for a in range(n):
            mine, src = _half_cols(fulls[a], c), srcs[a].at[pl.ds(0, rows[a]), :]
            keep = pltpu.make_async_copy(src, fulls[a].at[:, mine], local_sems.at[a])
            keep.start()
            give = _remote(src, fulls[a].at[:, mine], send_sems, recv_sems, a, (x, y, 1 - c))
            give.start()
            copies += [keep, give]
        for a in range(n):
            other, src = _half_cols(fulls[a], 1 - c), srcs[a].at[pl.ds(0, rows[a]), :]
            copies[2 * a].wait()
            copies[2 * a + 1].wait_send()
            _remote(src, fulls[a].at[:, other], send_sems, recv_sems, a, (x, y, 1 - c)).wait_recv()

    return pl.pallas_call(
        body, name="grad_pair_gather",
        out_shape=[jax.ShapeDtypeStruct((r, 2 * h.shape[1]), h.dtype) for r, h in zip(rows, halves)],
        in_specs=[pl.BlockSpec(memory_space=pltpu.VMEM)] * n, out_specs=[HBM] * n,
        scratch_shapes=[pltpu.SemaphoreType.DMA((n,)), pltpu.SemaphoreType.DMA((n,)), pltpu.SemaphoreType.DMA((n,))],
    )(*halves)


def _all_reduce_small(p):
    n_dev = 8

    def body(p_ref, o_ref, land, send_sems, recv_sems):
        x, y, c = lax.axis_index("x"), lax.axis_index("y"), lax.axis_index("c")
        me = 4 * x + 2 * y + c
        land[me] = p_ref[...]
        sends = []
        for k in range(1, n_dev):
            fx, fy, fc = (k >> 2) & 1, (k >> 1) & 1, k & 1
            to = (x ^ fx, y ^ fy, c ^ fc)
            cp = _remote(p_ref, land.at[me], send_sems, recv_sems, k - 1, to)
            cp.start()
            sends.append(cp)
        for k in range(1, n_dev):
            _remote(p_ref, land.at[me ^ k], send_sems, recv_sems, k - 1, (x, y, c)).wait_recv()
        total = land[0]
        for d in range(1, n_dev):
            total = total + land[d]
        o_ref[...] = total
        for cp in sends:
            cp.wait_send()

    vm = pl.BlockSpec(memory_space=pltpu.VMEM)
    return pl.pallas_call(
        body, name="all_reduce_small", out_shape=jax.ShapeDtypeStruct(p.shape, p.dtype), in_specs=[vm], out_specs=vm,
        scratch_shapes=[pltpu.VMEM((n_dev,) + p.shape, p.dtype), pltpu.SemaphoreType.DMA((n_dev - 1,)),
                        pltpu.SemaphoreType.DMA((n_dev - 1,))],
    )(p)


def _row_tile(rows):
    fits = [t for t in range(8, min(rows, PIECE_ROWS_CAP) + 1, 8) if rows % t == 0]
    return max(fits) if fits else rows


def _pair_sum(full, got, core, name):
    n, r, c = got.shape
    tr = _row_tile(r)

    def body(core_ref, a_ref, b_ref, o_ref):
        o_ref[...] = (a_ref[...] + b_ref[...].astype(F32)).astype(BF16)

    blk = pl.BlockSpec((None, tr, c), lambda i, j, core_ref: (i, j, 0))
    grid_spec = pltpu.PrefetchScalarGridSpec(
        num_scalar_prefetch=1, grid=(n, r // tr),
        in_specs=[pl.BlockSpec((None, tr, c), lambda i, j, core_ref: (i, j, core_ref[0])), blk], out_specs=blk)
    return pl.pallas_call(body, name=name, grid_spec=grid_spec, out_shape=jax.ShapeDtypeStruct(got.shape, BF16),
                          compiler_params=_params(("parallel", "parallel")))(core, full, got)


def _chip_sum(full, got, land, place, name):
    n, r, c = land.shape
    tr = _row_tile(r)

    def body(place_ref, a_ref, b_ref, l_ref, o_ref):
        total = a_ref[...] + b_ref[...].astype(F32)
        for j in range(n):
            total = total + l_ref[j].astype(F32)
        o_ref[...] = total

    grid_spec = pltpu.PrefetchScalarGridSpec(
        num_scalar_prefetch=1, grid=(r // tr,),
        in_specs=[pl.BlockSpec((None, tr, c), lambda i, p: (p[0], i, p[1])),
                  pl.BlockSpec((None, tr, c), lambda i, p: (p[0], i, 0)),
                  pl.BlockSpec((n, tr, c), lambda i, p: (0, i, 0))],
        out_specs=pl.BlockSpec((tr, c), lambda i, p: (i, 0)))
    return pl.pallas_call(body, name=name, grid_spec=grid_spec, out_shape=jax.ShapeDtypeStruct((r, c), F32),
                          compiler_params=_params(("parallel",)))(place, full, got, land)


def _adamw(w, g, m, v, name, echo=False):
    r, c = w.shape
    tr = _row_tile(r)
    tc = 1024 if c % 1024 == 0 else c

    def body(w_ref, g_ref, m_ref, v_ref, d_ref, mo_ref, vo_ref, *g_out):
        g_ = g_ref[...]
        for o in g_out:
            o[...] = g_
        m2 = ADAM_B1 * m_ref[...] + (1.0 - ADAM_B1) * g_
        v2 = ADAM_B2 * v_ref[...] + (1.0 - ADAM_B2) * jnp.square(g_)
        m_hat = m2 / (1.0 - ADAM_B1 ** ADAM_STEP)
        v_hat = v2 / (1.0 - ADAM_B2 ** ADAM_STEP)
        d_ref[...] = -ADAM_LR * (m_hat / (jnp.sqrt(v_hat) + ADAM_EPS) + ADAM_WD * w_ref[...])
        mo_ref[...] = m2
        vo_ref[...] = v2

    blk = pl.BlockSpec((tr, tc), lambda i, j: (i, j))
    n_out = 4 if echo else 3
    return pl.pallas_call(body, name=name, grid=(r // tr, c // tc), in_specs=[blk] * 4, out_specs=[blk] * n_out,
                          out_shape=[jax.ShapeDtypeStruct(w.shape, F32)] * n_out,
                          compiler_params=_params(("parallel", "parallel")))(w, g, m, v)


SMALL_NAMES = ("norm_w", "mem_norm_w", "o_norm_a", "q_norm_c", "k_norm_c", "q_norm_b", "k_norm_b",
               "a_log_fwd", "a_log_bwd", "dt_bias_fwd", "dt_bias_bwd", "sink_b")
SMALL_SIZES = (2048, 2048, 128, 128, 128, 64, 64, 8, 8, 8, 8, 8)
SMALL_LOSS = sum(SMALL_SIZES)
SMALL_CONV = 5120
SMALL_TOTAL = SMALL_CONV + CONV_K * 3 * A_WIDTH
SMALL_ROWS = SMALL_TOTAL // LANE


def _pack_small(parts, extra=None, conv=None):
    vec = [parts[n].reshape(-1) for n in SMALL_NAMES]
    vec.append(jnp.zeros((1,), F32) if extra is None else extra.reshape(1))
    vec.append(jnp.zeros((SMALL_CONV - SMALL_LOSS - 1,), F32))
    vec.append(jnp.zeros((SMALL_TOTAL - SMALL_CONV,), F32) if conv is None else conv.reshape(-1))
    return jnp.concatenate(vec).reshape(SMALL_ROWS, LANE)


def _unpack_small(packed):
    flat = packed.reshape(-1)
    out, off = {}, 0
    for n, size in zip(SMALL_NAMES, SMALL_SIZES):
        out[n] = flat[off:off + size].reshape(1, size)
        off += size
    return out


WEIGHT_ORDER = ("norm_w", "w_in", "conv_w_a", "a_log_fwd", "a_log_bwd", "dt_bias_fwd", "dt_bias_bwd", "o_norm_a",
                "q_norm_b", "k_norm_b", "sink_b", "mem_norm_w", "w_mem_kv", "q_norm_c", "k_norm_c", "w_out")


def kernel(x, mem, norm_w, w_in, conv_w_a, a_log_fwd, a_log_bwd, dt_bias_fwd, dt_bias_bwd, o_norm_a, q_norm_b, k_norm_b, sink_b, mem_norm_w, w_mem_kv, q_norm_c, k_norm_c, w_out, loss_target, m_norm_w, m_w_in, m_conv_w_a, m_a_log_fwd, m_a_log_bwd, m_dt_bias_fwd, m_dt_bias_bwd, m_o_norm_a, m_q_norm_b, m_k_norm_b, m_sink_b, m_mem_norm_w, m_w_mem_kv, m_q_norm_c, m_k_norm_c, m_w_out, v_norm_w, v_w_in, v_conv_w_a, v_a_log_fwd, v_a_log_bwd, v_dt_bias_fwd, v_dt_bias_bwd, v_o_norm_a, v_q_norm_b, v_k_norm_b, v_sink_b, v_mem_norm_w, v_w_mem_kv, v_q_norm_c, v_k_norm_c, v_w_out):
    weights = dict(norm_w=norm_w, w_in=w_in, conv_w_a=conv_w_a, a_log_fwd=a_log_fwd, a_log_bwd=a_log_bwd,
                   dt_bias_fwd=dt_bias_fwd, dt_bias_bwd=dt_bias_bwd, o_norm_a=o_norm_a, q_norm_b=q_norm_b,
                   k_norm_b=k_norm_b, sink_b=sink_b, mem_norm_w=mem_norm_w, w_mem_kv=w_mem_kv, q_norm_c=q_norm_c,
                   k_norm_c=k_norm_c, w_out=w_out)
    mom1 = dict(norm_w=m_norm_w, w_in=m_w_in, conv_w_a=m_conv_w_a, a_log_fwd=m_a_log_fwd, a_log_bwd=m_a_log_bwd,
                dt_bias_fwd=m_dt_bias_fwd, dt_bias_bwd=m_dt_bias_bwd, o_norm_a=m_o_norm_a, q_norm_b=m_q_norm_b,
                k_norm_b=m_k_norm_b, sink_b=m_sink_b, mem_norm_w=m_mem_norm_w, w_mem_kv=m_w_mem_kv,
                q_norm_c=m_q_norm_c, k_norm_c=m_k_norm_c, w_out=m_w_out)
    mom2 = dict(norm_w=v_norm_w, w_in=v_w_in, conv_w_a=v_conv_w_a, a_log_fwd=v_a_log_fwd, a_log_bwd=v_a_log_bwd,
                dt_bias_fwd=v_dt_bias_fwd, dt_bias_bwd=v_dt_bias_bwd, o_norm_a=v_o_norm_a, q_norm_b=v_q_norm_b,
                k_norm_b=v_k_norm_b, sink_b=v_sink_b, mem_norm_w=v_mem_norm_w, w_mem_kv=v_w_mem_kv,
                q_norm_c=v_q_norm_c, k_norm_c=v_k_norm_c, w_out=v_w_out)
    chip = 2 * lax.axis_index("x") + lax.axis_index("y")

    own_in = jnp.pad(jnp.transpose(w_in[0]).astype(BF16), ((0, W_IN_PAD - W_IN_BLOCK), (0, 0)))
    w_in4, conv4 = _all_gather_weights([own_in], conv_w_a[0])
    w_perm_t = _permute_blocks(w_in4)
    w_blocks_t = w_in4.reshape(N_CHIPS * W_IN_PAD, D_MODEL)
    conv_full = jnp.transpose(conv4, (1, 0, 2)).reshape(CONV_K, 3 * A_WIDTH)
    own_out, own_kv = w_out[0].astype(BF16), w_mem_kv[0].astype(BF16)

    def assemble(w_out4, w_kv4):
        w_out4 = lax.dynamic_update_index_in_dim(w_out4, own_out, chip, 0)
        w_kv4 = lax.dynamic_update_index_in_dim(w_kv4, own_kv, chip, 0)
        return w_out4.reshape(D_MODEL, D_MODEL), w_kv4.reshape(D_MODEL, 2 * C_HEADS * C_DIM)

    gather = (_PairedGather([own_out, own_kv]), assemble)
    pa = jnp.concatenate([_pad_row(a_log_fwd), _pad_row(a_log_bwd), _pad_row(dt_bias_fwd), _pad_row(dt_bias_bwd),
                          _pad_row(o_norm_a), jnp.zeros((3, LANE), F32)], axis=0)
    pb = jnp.concatenate([_pad_row(q_norm_b), _pad_row(k_norm_b), _pad_row(sink_b), jnp.zeros((5, LANE), F32)], axis=0)
    pc = jnp.concatenate([_pad_row(q_norm_c), _pad_row(k_norm_c), jnp.zeros((6, LANE), F32)], axis=0)

    full, got = {}, {}
    core = lax.axis_index("c").astype(jnp.int32).reshape(1)

    def pair_round(tag, blocks):
        names = [tag + "_%d" % i for i in range(len(blocks))]
        full.update(zip(names, blocks))
        got.update(zip(names, _pair_exchange(blocks, "grad_pair_exchange_" + tag)))
        return _ChipExchange([_pair_sum(full[n], got[n], core, "grad_pair_sum_" + n) for n in names])

    def early(g_w_out, g_w_kv):
        return pair_round("early", [g_w_out.reshape(N_CHIPS, D_MODEL // N_CHIPS, D_MODEL),
                                    g_w_kv.reshape(N_CHIPS, D_MODEL // N_CHIPS, 2 * C_HEADS * C_DIM)])

    def late(g_w_blocks_t):
        return pair_round("late", [g_w_blocks_t.reshape(N_CHIPS, W_IN_PAD, D_MODEL)])

    r = _local_step(x[0], mem[0], loss_target[0], norm_w, w_perm_t, w_blocks_t, conv_full, pa, pb, pc, mem_norm_w, None, None,
                    gather, (early, late))
    place = jnp.stack([chip, lax.axis_index("c")]).astype(jnp.int32)
    reduced = [_chip_sum(full[n], got[n], l, place, "grad_chip_sum_" + n)
               for n, l in zip(("late_0", "early_0", "early_1"), r["landed"])]
    g_w_in_t, g_w_out, g_w_kv = _pair_gather(reduced, [W_IN_BLOCK, D_MODEL // N_CHIPS, D_MODEL // N_CHIPS])

    d_pa, d_pb, d_pc = r["d_pa"], r["d_pb"], r["d_pc"]
    small_g = dict(norm_w=r["g_norm"], mem_norm_w=r["g_mem_norm"], o_norm_a=d_pa[4], q_norm_c=d_pc[0], k_norm_c=d_pc[1],
                   q_norm_b=d_pb[0, :B_DIM], k_norm_b=d_pb[1, :B_DIM], a_log_fwd=d_pa[0, :A_HEADS],
                   a_log_bwd=d_pa[1, :A_HEADS], dt_bias_fwd=d_pa[2, :A_HEADS], dt_bias_bwd=d_pa[3, :A_HEADS],
                   sink_b=d_pb[2, :B_HEADS])
    packed = _all_reduce_small(_pack_small(small_g, jnp.sum(r["loss_parts"][:, 0, 0]), r["g_conv"]))
    flat = packed.reshape(-1)
    loss = flat[SMALL_LOSS]
    conv_sum = flat[SMALL_CONV:].reshape(CONV_K, 3 * A_WIDTH)
    conv_cols = 3 * A_WIDTH // N_CHIPS
    g_conv = lax.dynamic_slice(conv_sum, (0, chip * conv_cols), (CONV_K, conv_cols))

    grads = _unpack_small(packed)
    grads["conv_w_a"] = g_conv
    delta, new_m, new_v = {}, {}, {}
    delta["conv_w_a"], new_m["conv_w_a"], new_v["conv_w_a"] = _adamw(conv_w_a[0], g_conv, m_conv_w_a[0], v_conv_w_a[0],
                                                                     "adamw_conv_w_a")
    for n, g in (("w_mem_kv", g_w_kv), ("w_out", g_w_out)):
        delta[n], new_m[n], new_v[n], grads[n] = _adamw(weights[n][0], g, mom1[n][0], mom2[n][0], "adamw_" + n, echo=True)
    stepped = _adamw(jnp.transpose(w_in[0]), g_w_in_t, jnp.transpose(m_w_in[0]), jnp.transpose(v_w_in[0]), "adamw_w_in",
                     echo=True)
    delta["w_in"], new_m["w_in"], new_v["w_in"], grads["w_in"] = (jnp.transpose(t) for t in stepped)
    d_s, m_s, v_s = _adamw(_pack_small(weights), packed, _pack_small(mom1), _pack_small(mom2), "adamw_small")
    d_s, m_s, v_s = _unpack_small(d_s), _unpack_small(m_s), _unpack_small(v_s)
    for n in SMALL_NAMES:
        delta[n], new_m[n], new_v[n] = d_s[n], m_s[n], v_s[n]

    def shaped(tree):
        return [tree[n].reshape(weights[n].shape) for n in WEIGHT_ORDER]

    return (loss, r["g_x"].reshape(x.shape), *shaped(grads), *shaped(delta), *shaped(new_m), *shaped(new_v))
```

```python
import jax
import jax.numpy as jnp
from jax import lax
from jax.experimental import pallas as pl
from jax.experimental.pallas import tpu as pltpu

F32 = jnp.float32
BF16 = jnp.bfloat16
HI = lax.Precision.HIGHEST
MESH = pl.DeviceIdType.MESH

D_MODEL = 2048
A_WIDTH = 1024
A_HEADS = 8
A_DIM = 128
CONV_K = 5
CHUNK = 64
B_HEADS = 8
B_KV = 2
B_DIM = 64
WINDOW = 128
C_HEADS = 4
C_DIM = 128
MEM_LEN = 256
ROPE_THETA = 10000.0
EPS = 1e-6
IN_WIDTH = 6432
N_CHIPS = 4
W_IN_BLOCK = IN_WIDTH // N_CHIPS
W_IN_PAD = 1664

LANE = 128
P_QA, P_KA, P_VA, P_ZA = 0, 1024, 2048, 3072
P_QB, P_ZB, P_QC, P_ZC = 4096, 4608, 5120, 5632
P_KB, P_VB, P_GT = 6144, 6272, 6400
P_WIDTH = 6656
O_GT, O_QB, O_KB, O_VB, O_ZB, O_QC, O_ZC = 4096, 4128, 4640, 4768, 4896, 5408, 5920

ADAM_LR, ADAM_B1, ADAM_B2, ADAM_EPS, ADAM_WD, ADAM_STEP = 0.001, 0.9, 0.999, 1e-08, 0.01, 10

VMEM_LIMIT = 56 * 1024 * 1024


def _params(sem=None):
    return pltpu.CompilerParams(dimension_semantics=sem, vmem_limit_bytes=VMEM_LIMIT)


def _dot(a, b, dims=(((1,), (0,)), ((), ())), precision=HI):
    return lax.dot_general(a, b, dims, precision=precision, preferred_element_type=F32)


_NN = (((1,), (0,)), ((), ()))
_NT = (((1,), (1,)), ((), ()))
_TN = (((0,), (0,)), ((), ()))


def _bdot(a, b, dims):
    return lax.dot_general(a.astype(BF16), b.astype(BF16), dims, preferred_element_type=F32)


@jax.custom_vjp
def _mm(a, b):
    return _bdot(a, b, _NN)


_mm.defvjp(lambda a, b: (_bdot(a, b, _NN), (a, b)),
           lambda res, ct: (_bdot(ct, res[1], _NT), _bdot(res[0], ct, _TN)))


@jax.custom_vjp
def _mm_nt(a, b):
    return _bdot(a, b, _NT)


_mm_nt.defvjp(lambda a, b: (_bdot(a, b, _NT), (a, b)),
              lambda res, ct: (_bdot(ct, res[1], _NN), _bdot(ct, res[0], _TN)))


@jax.custom_vjp
def _mm_tn(a, b):
    return _bdot(a, b, _TN)


_mm_tn.defvjp(lambda a, b: (_bdot(a, b, _TN), (a, b)),
              lambda res, ct: (_bdot(res[1], ct, _NT), _bdot(res[0], ct, _NN)))


def _rms(t, w):
    return t * lax.rsqrt(jnp.mean(t * t, axis=-1, keepdims=True) + EPS) * w


def _l2(t):
    return t * lax.rsqrt(jnp.sum(t * t, axis=-1, keepdims=True) + EPS)


def _silu(t):
    return t * jax.nn.sigmoid(t)


def _softplus(t):
    return jnp.maximum(t, 0.0) + jnp.log1p(jnp.exp(-jnp.abs(t)))


def _matmul(a, b, mode, out_dtype, name, tm=512, tn=512, tk=512, ride=None):
    (m, k) = a.shape[::-1] if mode == "tn" else a.shape
    n = b.shape[0] if mode == "nt" else b.shape[1]
    tm, tn, tk = min(tm, m), min(tn, n), min(tk, k)
    assert m % tm == 0 and n % tn == 0 and k % tk == 0, (m, n, k, tm, tn, tk)
    if mode == "nn":
        a_spec = pl.BlockSpec((tm, tk), lambda i, j, kk: (i, kk))
        b_spec = pl.BlockSpec((tk, tn), lambda i, j, kk: (kk, j))
        dims = (((1,), (0,)), ((), ()))
    elif mode == "nt":
        a_spec = pl.BlockSpec((tm, tk), lambda i, j, kk: (i, kk))
        b_spec = pl.BlockSpec((tn, tk), lambda i, j, kk: (j, kk))
        dims = (((1,), (1,)), ((), ()))
    else:
        a_spec = pl.BlockSpec((tk, tm), lambda i, j, kk: (kk, i))
        b_spec = pl.BlockSpec((tk, tn), lambda i, j, kk: (kk, j))
        dims = (((0,), (0,)), ((), ()))
    nk = k // tk
    grid = (m // tm, n // tn, nk)
    n_in = len(ride.operands) if ride else 0
    n_out = len(ride.out_shapes) if ride else 0

    def body(*refs):
        a_ref, b_ref, o_ref = refs[0], refs[1], refs[2 + n_in]
        scratch = refs[3 + n_in + n_out:]
        step = (pl.program_id(0) * grid[1] + pl.program_id(1)) * nk + pl.program_id(2)
        riders = (refs[2:2 + n_in], refs[3 + n_in:3 + n_in + n_out], scratch[(0 if nk == 1 else 1):])
        if ride:
            pl.when(step == 0)(lambda: ride.start(*riders))
        if nk == 1:
            o_ref[...] = _bdot(a_ref[...], b_ref[...], dims).astype(out_dtype)
        else:
            acc_ref, kk = scratch[0], pl.program_id(2)

            @pl.when(kk == 0)
            def _():
                acc_ref[...] = jnp.zeros_like(acc_ref)

            acc_ref[...] += _bdot(a_ref[...], b_ref[...], dims)

            @pl.when(kk == nk - 1)
            def _():
                o_ref[...] = acc_ref[...].astype(out_dtype)
        if ride:
            pl.when(step == grid[0] * grid[1] * nk - 1)(lambda: ride.finish(*riders))

    out = pl.pallas_call(
        body, name=name, grid=grid,
        in_specs=[a_spec, b_spec] + [HBM] * n_in,
        out_specs=[pl.BlockSpec((tm, tn), lambda i, j, kk: (i, j))] + [HBM] * n_out,
        out_shape=[jax.ShapeDtypeStruct((m, n), out_dtype)] + (list(ride.out_shapes) if ride else []),
        scratch_shapes=([] if nk == 1 else [pltpu.VMEM((tm, tn), F32)]) + (list(ride.scratch_shapes) if ride else []),
        compiler_params=_params(("arbitrary",) * 3 if ride else ("parallel", "parallel", "arbitrary")),
    )(a, b, *(ride.operands if ride else []))
    return out if ride else out[0]


def _rms_fwd(x, w, tr=256):
    s, d = x.shape

    def body(x_ref, w_ref, o_ref):
        o_ref[...] = _rms(x_ref[...], w_ref[...]).astype(BF16)

    return pl.pallas_call(
        body, name="rms_fwd", grid=(s // tr,),
        in_specs=[pl.BlockSpec((tr, d), lambda i: (i, 0)), pl.BlockSpec((1, d), lambda i: (0, 0))],
        out_specs=pl.BlockSpec((tr, d), lambda i: (i, 0)),
        out_shape=jax.ShapeDtypeStruct((s, d), BF16), compiler_params=_params(("parallel",)),
    )(x, w)


def _input_grad(d_proj, w_t, x, w, dy, ride=None, tm=512, tk=512):
    s, k = d_proj.shape
    d = w_t.shape[1]
    tm = min(tm, s)
    nk = k // tk
    grid = (s // tm, nk)
    n_in = len(ride.operands) if ride else 0
    n_out = len(ride.out_shapes) if ride else 0

    def body(*refs):
        a_ref, b_ref, x_ref, w_ref, dy_ref = refs[:5]
        gx_ref, gw_ref = refs[5 + n_in:7 + n_in]
        acc_ref = refs[7 + n_in + n_out]
        riders = (refs[5:5 + n_in], refs[7 + n_in:7 + n_in + n_out], refs[8 + n_in + n_out:])
        kk = pl.program_id(1)
        step = pl.program_id(0) * nk + kk
        if ride:
            pl.when(step == 0)(lambda: ride.start(*riders))

        @pl.when(step == 0)
        def _():
            gw_ref[...] = jnp.zeros_like(gw_ref)

        @pl.when(kk == 0)
        def _():
            acc_ref[...] = jnp.zeros_like(acc_ref)

        acc_ref[...] += _bdot(a_ref[...], b_ref[...], _NN)

        @pl.when(kk == nk - 1)
        def _():
            _, vjp = jax.vjp(_rms, x_ref[...], w_ref[...])
            dx, dw = vjp(acc_ref[...])
            gx_ref[...] = dy_ref[...] + dx
            gw_ref[...] += dw

        if ride:
            pl.when(step == grid[0] * nk - 1)(lambda: ride.finish(*riders))

    row = pl.BlockSpec((tm, d), lambda i, kk: (i, 0))
    vec = pl.BlockSpec((1, d), lambda i, kk: (0, 0))
    return pl.pallas_call(
        body, name="input_grad", grid=grid,
        in_specs=[pl.BlockSpec((tm, tk), lambda i, kk: (i, kk)), pl.BlockSpec((tk, d), lambda i, kk: (kk, 0)), row, vec, row]
        + [HBM] * n_in,
        out_specs=[row, vec] + [HBM] * n_out,
        out_shape=[jax.ShapeDtypeStruct((s, d), F32), jax.ShapeDtypeStruct((1, d), F32)]
        + (list(ride.out_shapes) if ride else []),
        scratch_shapes=[pltpu.VMEM((tm, d), F32)] + (list(ride.scratch_shapes) if ride else []),
        compiler_params=_params(("arbitrary", "arbitrary")),
    )(d_proj, w_t, x, w, dy, *(ride.operands if ride else []))


def _out_loss(mixed, w_out, x, target, tm=1024, tn=512):
    s, d = x.shape
    tm = min(tm, s)
    ni, nj = s // tm, d // tn

    def body(m_ref, w_ref, x_ref, t_ref, dy_ref, dyb_ref, l_ref):
        err = x_ref[...] + _bdot(m_ref[...], w_ref[...], _NN) - t_ref[...]
        dy = err * (1.0 / d)
        dy_ref[...] = dy
        dyb_ref[...] = dy.astype(BF16)
        l_ref[...] = jnp.full(l_ref.shape, 0.5 * jnp.sum(jnp.sum(err * err, axis=1, keepdims=True) * (1.0 / d)), F32)

    tile = pl.BlockSpec((tm, tn), lambda i, j: (i, j))
    return pl.pallas_call(
        body, name="out_loss", grid=(ni, nj),
        in_specs=[pl.BlockSpec((tm, mixed.shape[1]), lambda i, j: (i, 0)),
                  pl.BlockSpec((mixed.shape[1], tn), lambda i, j: (0, j)), tile, tile],
        out_specs=[tile, tile, pl.BlockSpec((1, 8, LANE), lambda i, j: (i * nj + j, 0, 0))],
        out_shape=[jax.ShapeDtypeStruct((s, d), F32), jax.ShapeDtypeStruct((s, d), BF16),
                   jax.ShapeDtypeStruct((ni * nj, 8, LANE), F32)],
        compiler_params=_params(("parallel", "parallel")),
    )(mixed, w_out, x, target)


def _shift_rows(t, s):
    if s == 0:
        return t
    n = t.shape[0]
    rolled = pltpu.roll(t, (-s) % n, axis=0)
    idx = lax.broadcasted_iota(jnp.int32, t.shape, 0) + s
    return jnp.where((idx >= 0) & (idx < n), rolled, 0.0)


CONV_FWD_COLS = 512
CONV_BWD_COLS = 128


def _conv_fwd(proj, conv_w):
    s = proj.shape[0]
    cols, split = CONV_FWD_COLS, A_WIDTH // CONV_FWD_COLS
    nblk = 3 * A_WIDTH // cols

    def body(x_ref, w_ref, o_ref):
        x = x_ref[...]
        acc = jnp.zeros_like(x)
        for j in range(CONV_K):
            acc = acc + w_ref[j:j + 1, :] * _shift_rows(x, j - CONV_K // 2)
        o_ref[...] = acc

    return pl.pallas_call(
        body, name="conv_fwd", grid=(nblk,),
        in_specs=[pl.BlockSpec((s, cols), lambda i: (0, i)), pl.BlockSpec((CONV_K, cols), lambda i: (0, i))],
        out_specs=pl.BlockSpec((None, s, cols), lambda i: (i // split, 0, i % split)),
        out_shape=jax.ShapeDtypeStruct((3, s, A_WIDTH), F32), compiler_params=_params(("parallel",)),
    )(proj, conv_w)


def _conv_bwd(proj, conv_w, d_c):
    s = proj.shape[0]
    cols, split = CONV_BWD_COLS, A_WIDTH // CONV_BWD_COLS
    nblk = 3 * A_WIDTH // cols

    def body(x_ref, w_ref, g_ref, dx_ref, dw_ref):
        x, g = x_ref[...], g_ref[...]
        acc = jnp.zeros_like(x)
        for j in range(CONV_K):
            off = j - CONV_K // 2
            acc = acc + w_ref[j:j + 1, :] * _shift_rows(g, -off)
            dw_ref[j:j + 1, :] = jnp.sum(_shift_rows(x, off) * g, axis=0, keepdims=True)
        dx_ref[...] = acc.astype(BF16)

    col = pl.BlockSpec((s, cols), lambda i: (0, i))
    wsp = pl.BlockSpec((CONV_K, cols), lambda i: (0, i))
    dsp = pl.BlockSpec((None, s, cols), lambda i: (i // split, 0, i % split))
    return pl.pallas_call(
        body, name="conv_bwd", grid=(nblk,), in_specs=[col, wsp, dsp], out_specs=[col, wsp],
        out_shape=[jax.ShapeDtypeStruct((s, 3 * A_WIDTH), BF16), jax.ShapeDtypeStruct((CONV_K, 3 * A_WIDTH), F32)],
        compiler_params=_params(("parallel",)),
    )(proj, conv_w, d_c)


A_FWD_HEADS = 4
A_BWD_HEADS = 4


def _neumann_inverse(a):
    c = a.shape[-1]
    eye = (lax.broadcasted_iota(jnp.int32, (c, c), 0) == lax.broadcasted_iota(jnp.int32, (c, c), 1)).astype(F32)
    tinv = eye + a
    p = a
    for _ in range(5):
        p = _mm(p, p)
        tinv = tinv + _mm(tinv, p)
    return tinv


@jax.custom_vjp
def _unit_inverse(a):
    return _neumann_inverse(a)


def _unit_inverse_fwd(a):
    tinv = _neumann_inverse(a)
    return tinv, tinv


def _unit_inverse_bwd(tinv, ct):
    return (_bdot(_bdot(tinv, ct, _TN), tinv, _NT),)


_unit_inverse.defvjp(_unit_inverse_fwd, _unit_inverse_bwd)


@jax.custom_vjp
def _known_inverse(a, tinv):
    return tinv


_known_inverse.defvjp(lambda a, tinv: (tinv, tinv),
                      lambda tinv, ct: (_unit_inverse_bwd(tinv, ct)[0], jnp.zeros_like(tinv)))


def _a_chain(st, cq, ck, cv, alpha, beta_raw, a_log, dt_b, incl, strict, last, kept=None):
    c = CHUNK
    gb = -jnp.exp(a_log) * _softplus(alpha + dt_b)
    bb = jax.nn.sigmoid(beta_raw)
    q = _l2(_silu(cq)) * (A_DIM ** -0.5)
    k = _l2(_silu(ck))
    v = _silu(cv)

    gc = _dot(incl, jnp.broadcast_to(gb, (c, LANE)))
    tot = jnp.sum(gc * last, axis=0, keepdims=True)
    m1 = gc[:, :c]
    decay = incl * jnp.exp(incl * (m1 - m1.T))
    kb = k * bb
    vb = v * bb
    a = -(strict * decay * _mm_nt(kb, k))
    tinv = _unit_inverse(a) if kept is None else _known_inverse(a, kept)
    eg = jnp.exp(gc)
    u = _mm(tinv, vb)
    w = _mm(tinv, kb * eg)
    qk = _mm_nt(q, k) * decay
    v_new = u - _mm(w, st)
    o = _mm(q * eg, st) + _mm(qk, v_new)
    st_new = st * jnp.exp(tot) + _mm_tn(k * jnp.exp(tot - gc), v_new)
    return st_new, o, tinv


def _a_step(sts, cq, ck, cv, gts, pa, h0, kept=None):
    c = CHUNK
    lane = lax.broadcasted_iota(jnp.int32, (1, LANE), 1)
    ii = lax.broadcasted_iota(jnp.int32, (c, c), 0)
    jj = lax.broadcasted_iota(jnp.int32, (c, c), 1)
    row = lax.broadcasted_iota(jnp.int32, (c, 1), 0)

    def pick(t, col):
        return jnp.sum(jnp.where(lane == col, t, 0.0), axis=1, keepdims=True)

    alpha, beta_raw, a_log, dt_b, incl, strict, last = [], [], [], [], [], [], []
    for b in range(sts.shape[0]):
        h, rev = h0 + b // 2, b % 2
        alpha.append(pick(gts[b], h + 8 * rev))
        beta_raw.append(pick(gts[b], h + 16 + 8 * rev))
        a_log.append(pick(pa[rev:rev + 1, :], h))
        dt_b.append(pick(pa[2 + rev:3 + rev, :], h))
        incl.append(((ii <= jj) if rev else (ii >= jj)).astype(F32))
        strict.append(((ii < jj) if rev else (ii > jj)).astype(F32))
        last.append((row == (0 if rev else c - 1)).astype(F32))
    stack = lambda ts: jnp.concatenate([t[None] for t in ts], axis=0)
    return jax.vmap(_a_chain)(sts, cq, ck, cv, stack(alpha), stack(beta_raw), stack(a_log), stack(dt_b),
                              stack(incl), stack(strict), stack(last), kept)


def _a_final(o, za, pa):
    outs = []
    for j in range(o.shape[1] // A_DIM):
        ln = slice(j * A_DIM, (j + 1) * A_DIM)
        outs.append(_rms(o[:, ln], pa[4:5, :]) * _silu(za[:, ln]))
    return jnp.concatenate(outs, axis=1)


def _a_tiles(n, nchunk, heads):
    tiles = []
    for b in range(2 * heads):
        i = (nchunk - 1 - n) if b % 2 else n
        tiles.append((i, pl.ds(pl.multiple_of(i * CHUNK, CHUNK), CHUNK), slice((b // 2) * A_DIM, (b // 2 + 1) * A_DIM)))
    return tiles


def _a_load(tiles, c_ref, gt_ref):
    cq, ck, cv = (jnp.stack([c_ref[r, sl, ln] for _, sl, ln in tiles], axis=0) for r in range(3))
    return cq, ck, cv, jnp.stack([gt_ref[sl, :] for _, sl, _ in tiles], axis=0)


def _loop_by_two(n, step, init):
    assert n % 2 == 0
    return lax.fori_loop(0, n // 2, lambda m, carry: step(2 * m + 1, step(2 * m, carry, 0), 1), init)


def _a_scan(h0, heads, nchunk, c_ref, gt_ref, pa, of_ref, ob_ref, s_ref, t_ref):
    def step(n, sts, parity):
        tiles = _a_tiles(n, nchunk, heads)
        sts_new, o, tinv = _a_step(sts, *_a_load(tiles, c_ref, gt_ref), pa, h0)
        for b, (i, sl, ln) in enumerate(tiles):
            s_ref[b, i] = sts[b]
            t_ref[b, i] = tinv[b]
            (ob_ref if b % 2 else of_ref)[sl, ln] = o[b]
        return sts_new

    _loop_by_two(nchunk, step, jnp.zeros((2 * heads, A_DIM, A_DIM), F32))


def _a_specs(s, heads):
    wide = heads * A_DIM
    once = pl.Buffered(1)
    trio = pl.BlockSpec((3, s, wide), lambda g: (0, 0, g), pipeline_mode=once)
    gates = pl.BlockSpec((s, LANE), lambda g: (0, P_GT // LANE))
    small = pl.BlockSpec((8, LANE), lambda g: (0, 0))

    def cols(base):
        return pl.BlockSpec((s, wide), lambda g: (0, base // wide + g), pipeline_mode=once)

    state = pl.BlockSpec((2 * heads, s // CHUNK, A_DIM, A_DIM), lambda g: (g, 0, 0, 0), pipeline_mode=once)
    kept = pl.BlockSpec((2 * heads, s // CHUNK, CHUNK, CHUNK), lambda g: (g, 0, 0, 0), pipeline_mode=once)
    return wide, trio, gates, small, cols, state, kept


def _delta_fwd(cqkv, proj, pa, ride=None):
    s = cqkv.shape[1]
    nchunk = s // CHUNK
    heads = A_FWD_HEADS
    steps = A_HEADS // heads
    wide, trio, gates, small, cols, state, kept = _a_specs(s, heads)
    n_in = len(ride.operands) if ride else 0
    n_out = len(ride.out_shapes) if ride else 0

    def body(*refs):
        c_ref, gt_ref, za_ref, pa_ref = refs[:4]
        out_ref, o_ref, s_ref, t_ref = refs[4 + n_in:8 + n_in]
        ob_ref = refs[8 + n_in + n_out]
        riders = (refs[4:4 + n_in], refs[8 + n_in:8 + n_in + n_out], refs[9 + n_in + n_out:])
        g = pl.program_id(0)
        if ride:
            pl.when(g == 0)(lambda: ride.start(*riders))
            pl.when(g == steps - 1)(lambda: ride.middle(*riders))
        h0 = g * heads
        pa_v = pa_ref[...]
        _a_scan(h0, heads, nchunk, c_ref, gt_ref, pa_v, o_ref, ob_ref, s_ref, t_ref)
        o_ref[...] += ob_ref[...]
        out_ref[...] = _a_final(o_ref[...], za_ref[...], pa_v).astype(BF16)
        if ride:
            pl.when(g == steps - 1)(lambda: ride.finish(*riders))

    assert steps > 1
    return pl.pallas_call(
        body, name="delta_fwd", grid=(steps,),
        in_specs=[trio, gates, cols(P_ZA), small] + [HBM] * n_in,
        out_specs=[cols(0), cols(0), state, kept] + [HBM] * n_out,
        out_shape=[jax.ShapeDtypeStruct((s, D_MODEL), BF16),
                   jax.ShapeDtypeStruct((s, A_WIDTH), F32),
                   jax.ShapeDtypeStruct((2 * A_HEADS, nchunk, A_DIM, A_DIM), F32),
                   jax.ShapeDtypeStruct((2 * A_HEADS, nchunk, CHUNK, CHUNK), F32)]
        + (list(ride.out_shapes) if ride else []),
        scratch_shapes=[pltpu.VMEM((s, wide), F32)] + (list(ride.scratch_shapes) if ride else []),
        compiler_params=_params(("arbitrary",)),
    )(cqkv, proj, proj, pa, *(ride.operands if ride else []))


def _delta_out_bwd(o_sum, proj, pa, d_mixed, tr=256):
    s = o_sum.shape[0]

    def body(o_ref, za_ref, pa_ref, dm_ref, do_ref, dza_ref, dpa_ref):
        @pl.when(pl.program_id(0) == 0)
        def _():
            dpa_ref[...] = jnp.zeros_like(dpa_ref)

        _, vjp = jax.vjp(_a_final, o_ref[...], za_ref[...], pa_ref[...])
        d_o, d_za, dpa = vjp(dm_ref[...].astype(F32))
        do_ref[...] = d_o
        dza_ref[...] = d_za.astype(BF16)
        dpa_ref[...] += dpa

    def rows(col):
        return pl.BlockSpec((tr, A_WIDTH), lambda i: (i, col))

    small = pl.BlockSpec((8, LANE), lambda i: (0, 0))
    return pl.pallas_call(
        body, name="delta_out_bwd", grid=(s // tr,), in_specs=[rows(0), rows(P_ZA // A_WIDTH), small, rows(0)],
        out_specs=[rows(0), rows(0), small],
        out_shape=[jax.ShapeDtypeStruct((s, A_WIDTH), F32), jax.ShapeDtypeStruct((s, A_WIDTH), BF16),
                   jax.ShapeDtypeStruct((8, LANE), F32)],
        compiler_params=_params(("arbitrary",)),
    )(o_sum, proj, pa, d_mixed)


def _delta_bwd(cqkv, proj, pa, d_o, states, inverses, ride=None):
    s = cqkv.shape[1]
    nchunk = s // CHUNK
    heads = A_BWD_HEADS
    steps = A_HEADS // heads
    wide, trio, gates, small, cols, _, _ = _a_specs(s, heads)
    n_in = len(ride.operands) if ride else 0
    n_out = len(ride.out_shapes) if ride else 0

    def body(*refs):
        c_ref, gt_ref, pa_ref, do_ref, s_hbm, t_hbm = refs[:6]
        dc_ref, dgt_ref, dpa_ref = refs[6 + n_in:9 + n_in]
        s_buf, t_buf, s_sems = refs[9 + n_in + n_out:12 + n_in + n_out]
        riders = (refs[6:6 + n_in], refs[9 + n_in:9 + n_in + n_out], refs[12 + n_in + n_out:])
        if ride:
            pl.when(pl.program_id(0) == 0)(lambda: ride.start(*riders))
        h0 = pl.program_id(0) * heads
        pa_v = pa_ref[...]

        @pl.when(h0 == 0)
        def _():
            dgt_ref[...] = jnp.zeros_like(dgt_ref)
            dpa_ref[...] = jnp.zeros_like(dpa_ref)

        dc_ref[...] = jnp.zeros_like(dc_ref)

        def state_copies(n, slot):
            tiles = _a_tiles(nchunk - 1 - n, nchunk, heads)
            return ([pltpu.make_async_copy(s_hbm.at[2 * h0 + b, i], s_buf.at[slot, b], s_sems.at[0, slot, b])
                     for b, (i, _, _) in enumerate(tiles)]
                    + [pltpu.make_async_copy(t_hbm.at[2 * h0 + b, i], t_buf.at[slot, b], s_sems.at[1, slot, b])
                       for b, (i, _, _) in enumerate(tiles)])

        for cp in state_copies(0, 0):
            cp.start()

        def step(n, carry, parity):
            d_sts, dpa = carry
            tiles = _a_tiles(nchunk - 1 - n, nchunk, heads)
            for cp in state_copies(n, parity):
                cp.wait()

            @pl.when(n + 1 < nchunk)
            def _():
                for cp in state_copies(n + 1, 1 - parity):
                    cp.start()

            sts, kept = s_buf[parity], t_buf[parity]
            d_o_t = jnp.stack([do_ref[sl, ln] for _, sl, ln in tiles], axis=0)
            _, vjp_c = jax.vjp(lambda *a: _a_step(*a, h0, kept)[:2], sts, *_a_load(tiles, c_ref, gt_ref), pa_v)
            d_prev, dcq, dck, dcv, dgts, dpa_i = vjp_c((d_sts, d_o_t))
            for b, (_, sl, ln) in enumerate(tiles):
                for r, dc in enumerate((dcq, dck, dcv)):
                    dc_ref[r, sl, ln] += dc[b]
                dgt_ref[sl, :] += dgts[b]
            return d_prev, dpa + dpa_i

        init = (jnp.zeros((2 * heads, A_DIM, A_DIM), F32), jnp.zeros((8, LANE), F32))
        _, dpa_out = lax.fori_loop(0, nchunk, lambda n, carry: step(n, carry, n % 2), init)
        dpa_ref[...] += dpa_out
        if ride:
            pl.when(pl.program_id(0) == steps - 1)(lambda: ride.finish(*riders))

    fixed = pl.BlockSpec((s, LANE), lambda g: (0, 0))
    return pl.pallas_call(
        body, name="delta_bwd", grid=(steps,),
        in_specs=[trio, gates, small, cols(0), pl.BlockSpec(memory_space=pl.ANY), pl.BlockSpec(memory_space=pl.ANY)]
        + [HBM] * n_in,
        out_specs=[trio, fixed, small] + [HBM] * n_out,
        out_shape=[jax.ShapeDtypeStruct((3, s, A_WIDTH), F32), jax.ShapeDtypeStruct((s, LANE), F32),
                   jax.ShapeDtypeStruct((8, LANE), F32)] + (list(ride.out_shapes) if ride else []),
        scratch_shapes=[pltpu.VMEM((2, 2 * heads, A_DIM, A_DIM), F32), pltpu.VMEM((2, 2 * heads, CHUNK, CHUNK), F32),
                        pltpu.SemaphoreType.DMA((2, 2, 2 * heads))]
        + (list(ride.scratch_shapes) if ride else []),
        compiler_params=_params(("arbitrary",)),
    )(cqkv, proj, pa, d_o, states, inverses, *(ride.operands if ride else []))


def _rope_tables(s):
    inv = ROPE_THETA ** (-jnp.arange(0, B_DIM, 2, dtype=F32) / B_DIM)
    ang = jnp.arange(s, dtype=F32)[:, None] * inv[None, :]
    cos, sin = jnp.cos(ang), jnp.sin(ang)
    return jnp.concatenate([cos, cos], axis=1), jnp.concatenate([-sin, sin], axis=1)


def _b_block(q_t, z_t, k3, v3, cos_q, sin_q, cos_k, sin_k, pb, n, nb):
    w = WINDOW
    def swap(t):
        return jnp.concatenate([t[:, B_DIM // 2:], t[:, :B_DIM // 2]], axis=1)

    grp = B_HEADS // B_KV
    qi = lax.broadcasted_iota(jnp.int32, (grp * w, 3 * w), 0) & (w - 1)
    kj = lax.broadcasted_iota(jnp.int32, (grp * w, 3 * w), 1)
    kpos = kj + (n - 1) * w
    mask = (jnp.abs(kj - w - qi) <= w) & (kpos >= 0) & (kpos < nb * w)
    lane = lax.broadcasted_iota(jnp.int32, (1, LANE), 1)
    qn, kn = pb[0:1, :B_DIM], pb[1:2, :B_DIM]
    cos_g = jnp.concatenate([cos_q] * grp, axis=0)
    sin_g = jnp.concatenate([sin_q] * grp, axis=0)
    def group(q, k, v, sink):
        k = _rms(k, kn)
        k = k * cos_k + swap(k) * sin_k
        q = _rms(q, qn)
        q = q * cos_g + swap(q) * sin_g
        s = _mm_nt(q, k) * (B_DIM ** -0.5)
        s = jnp.where(mask, s, -jnp.inf)
        m = jnp.maximum(jnp.max(s, axis=1, keepdims=True), sink)
        p = jnp.exp(s - m)
        p = p / (jnp.sum(p, axis=1, keepdims=True) + jnp.exp(sink - m))
        return _mm(p, v)

    stack = lambda ts: jnp.concatenate([t[None] for t in ts], axis=0)
    qs, ks, vs, sinks = [], [], [], []
    for hk in range(B_KV):
        heads = [hk * grp + g for g in range(grp)]
        ks.append(k3[:, hk * B_DIM:(hk + 1) * B_DIM])
        vs.append(v3[:, hk * B_DIM:(hk + 1) * B_DIM])
        qs.append(jnp.concatenate([q_t[:, hq * B_DIM:(hq + 1) * B_DIM] for hq in heads], axis=0))
        sinks.append(jnp.concatenate(
            [jnp.broadcast_to(jnp.sum(jnp.where(lane == hq, pb[2:3, :], 0.0), axis=1, keepdims=True), (w, 1))
             for hq in heads], axis=0))
    o = jax.vmap(group)(stack(qs), stack(ks), stack(vs), stack(sinks))
    outs = [o[hk, g * w:(g + 1) * w, :] for hk in range(B_KV) for g in range(grp)]
    return jnp.concatenate(outs, axis=1) * _silu(z_t)


def _b_specs(s):
    nb = s // WINDOW
    qsp = pl.BlockSpec((WINDOW, 512), lambda n: (n, P_QB // 512))
    zsp = pl.BlockSpec((WINDOW, 512), lambda n: (n, P_ZB // 512))

    def three(col, width):
        return [pl.BlockSpec((WINDOW, width), lambda n: (jnp.maximum(n - 1, 0), col)),
                pl.BlockSpec((WINDOW, width), lambda n: (n, col)),
                pl.BlockSpec((WINDOW, width), lambda n: (jnp.minimum(n + 1, nb - 1), col))]

    tab = pl.BlockSpec((WINDOW, B_DIM), lambda n: (n, 0))
    small = pl.BlockSpec((8, LANE), lambda n: (0, 0))
    specs = [qsp, zsp] + three(P_KB // LANE, LANE) + three(P_VB // LANE, LANE) + [tab, tab] + three(0, B_DIM) + three(0, B_DIM) + [small]
    return nb, specs


def _b_args(proj, cos2, sin2, pb):
    return (proj, proj, proj, proj, proj, proj, proj, proj, cos2, sin2, cos2, cos2, cos2, sin2, sin2, sin2, pb)


def _b_load(refs):
    (q_ref, z_ref, kp, kc, kx, vp, vc, vx, cq, sq, ckp, ckc, ckx, skp, skc, skx, pb_ref) = refs
    cat = lambda *r: jnp.concatenate([t[...] for t in r], axis=0)
    return (q_ref[...], z_ref[...], cat(kp, kc, kx), cat(vp, vc, vx), cq[...], sq[...], cat(ckp, ckc, ckx),
            cat(skp, skc, skx), pb_ref[...])


def _attn_b_fwd(proj, cos2, sin2, pb, mixed):
    s = proj.shape[0]
    nb, specs = _b_specs(s)

    def body(*refs):
        o_ref = refs[-1]
        args = _b_load(refs[:-2])
        o_ref[...] = _b_block(*args, pl.program_id(0), nb).astype(BF16)

    return pl.pallas_call(
        body, name="attn_b_fwd", grid=(nb,), in_specs=specs + [pl.BlockSpec(memory_space=pl.ANY)],
        out_specs=pl.BlockSpec((WINDOW, 512), lambda n: (n, A_WIDTH // 512)),
        out_shape=jax.ShapeDtypeStruct(mixed.shape, mixed.dtype), input_output_aliases={len(specs): 0},
        compiler_params=_params(("parallel",)),
    )(*_b_args(proj, cos2, sin2, pb), mixed)


def _attn_b_bwd(proj, cos2, sin2, pb, d_mixed):
    s = proj.shape[0]
    nb, specs = _b_specs(s)
    w = WINDOW

    def body(*refs):
        dm_ref, dq_ref, dz_ref, dk_ref, dv_ref, dpb_ref = refs[-6:]
        n = pl.program_id(0)
        q_t, z_t, k3, v3, cq, sq, ck, sk, pb_v = _b_load(refs[:-6])

        @pl.when(n == 0)
        def _():
            dk_ref[...] = jnp.zeros_like(dk_ref)
            dv_ref[...] = jnp.zeros_like(dv_ref)
            dpb_ref[...] = jnp.zeros_like(dpb_ref)

        def f(q_, z_, k_, v_, pb_):
            return _b_block(q_, z_, k_, v_, cq, sq, ck, sk, pb_, n, nb)

        _, vjp = jax.vjp(f, q_t, z_t, k3, v3, pb_v)
        dq, dz, dk3, dv3, dpb = vjp(dm_ref[...])
        dq_ref[...] = dq.astype(BF16)
        dz_ref[...] = dz.astype(BF16)
        dpb_ref[...] += dpb

        def add(j, cond):
            @pl.when(cond)
            def _():
                rows = pl.ds(pl.multiple_of((n - 1 + j) * w, w), w)
                dk_ref[rows, :] += dk3[j * w:(j + 1) * w, :]
                dv_ref[rows, :] += dv3[j * w:(j + 1) * w, :]

        add(0, n > 0)
        add(1, n >= 0)
        add(2, n < nb - 1)

    blk = pl.BlockSpec((w, 512), lambda n: (n, 0))
    whole = pl.BlockSpec((s, LANE), lambda n: (0, 0))
    small = pl.BlockSpec((8, LANE), lambda n: (0, 0))
    return pl.pallas_call(
        body, name="attn_b_bwd", grid=(nb,),
        in_specs=specs + [pl.BlockSpec((w, 512), lambda n: (n, 2))],
        out_specs=[blk, blk, whole, whole, small],
        out_shape=[jax.ShapeDtypeStruct((s, 512), BF16), jax.ShapeDtypeStruct((s, 512), BF16),
                   jax.ShapeDtypeStruct((s, LANE), F32), jax.ShapeDtypeStruct((s, LANE), F32),
                   jax.ShapeDtypeStruct((8, LANE), F32)],
        compiler_params=_params(("arbitrary",)),
    )(*_b_args(proj, cos2, sin2, pb), d_mixed)


def _mem_kv_fwd(mem, mem_norm_w, w_kv):
    def body(mem_ref, nw_ref, w_ref, kv_ref):
        mn = _rms(mem_ref[...], nw_ref[...]).astype(BF16)
        kv_ref[...] = jnp.dot(mn, w_ref[...], preferred_element_type=F32)

    return pl.pallas_call(
        body, name="mem_kv_fwd", out_shape=jax.ShapeDtypeStruct((MEM_LEN, 2 * C_HEADS * C_DIM), F32),
        compiler_params=_params(),
    )(mem, mem_norm_w, w_kv)


def _mem_kv_bwd(mem, mem_norm_w, w_kv, d_kv):
    def body(mem_ref, nw_ref, w_ref, g_ref, gw_ref, gn_ref):
        mn, vjp = jax.vjp(_rms, mem_ref[...], nw_ref[...])
        g = g_ref[...].astype(BF16)
        gw_ref[...] = lax.dot_general(mn.astype(BF16), g, (((0,), (0,)), ((), ())), preferred_element_type=F32)
        d_mn = lax.dot_general(g, w_ref[...], (((1,), (1,)), ((), ())), preferred_element_type=F32)
        gn_ref[...] = vjp(d_mn)[1]

    return pl.pallas_call(
        body, name="mem_kv_bwd",
        out_shape=[jax.ShapeDtypeStruct((D_MODEL, 2 * C_HEADS * C_DIM), F32), jax.ShapeDtypeStruct((1, D_MODEL), F32)],
        compiler_params=_params(),
    )(mem, mem_norm_w, w_kv, d_kv)


def _c_tile(q_t, z_t, kvm, pc):
    width = C_HEADS * C_DIM

    def head(q, k, v):
        q = _rms(q, pc[0:1, :])
        k = _rms(k, pc[1:2, :])
        s = _mm_nt(q, k) * (C_DIM ** -0.5)
        p = jnp.exp(s - jnp.max(s, axis=1, keepdims=True))
        p = p / jnp.sum(p, axis=1, keepdims=True)
        return _mm(p, v)

    def stack(t, first):
        return jnp.concatenate([t[None, :, first + h * C_DIM:first + (h + 1) * C_DIM] for h in range(C_HEADS)], axis=0)

    o = jax.vmap(head)(stack(q_t, 0), stack(kvm, 0), stack(kvm, width))
    return jnp.concatenate([o[h] for h in range(C_HEADS)], axis=1) * _silu(z_t)


def _attn_c_fwd(proj, kvm, pc, mixed, tq=256):
    s = proj.shape[0]

    def body(q_ref, z_ref, kv_ref, pc_ref, mixed_ref, o_ref):
        o_ref[...] = _c_tile(q_ref[...], z_ref[...], kv_ref[...], pc_ref[...]).astype(BF16)

    return pl.pallas_call(
        body, name="attn_c_fwd", grid=(s // tq,),
        in_specs=[pl.BlockSpec((tq, 512), lambda i: (i, P_QC // 512)), pl.BlockSpec((tq, 512), lambda i: (i, P_ZC // 512)),
                  pl.BlockSpec(kvm.shape, lambda i: (0, 0)), pl.BlockSpec((8, LANE), lambda i: (0, 0)),
                  pl.BlockSpec(memory_space=pl.ANY)],
        out_specs=pl.BlockSpec((tq, 512), lambda i: (i, (A_WIDTH + 512) // 512)),
        out_shape=jax.ShapeDtypeStruct(mixed.shape, mixed.dtype), input_output_aliases={4: 0},
        compiler_params=_params(("parallel",)),
    )(proj, proj, kvm, pc, mixed)


def _attn_c_bwd(proj, kvm, pc, d_mixed, tq=256):
    s = proj.shape[0]

    def body(q_ref, z_ref, kv_ref, pc_ref, dm_ref, dq_ref, dz_ref, dkv_ref, dpc_ref):
        @pl.when(pl.program_id(0) == 0)
        def _():
            dkv_ref[...] = jnp.zeros_like(dkv_ref)
            dpc_ref[...] = jnp.zeros_like(dpc_ref)

        _, vjp = jax.vjp(_c_tile, q_ref[...], z_ref[...], kv_ref[...], pc_ref[...])
        dq, dz, dkv, dpc = vjp(dm_ref[...])
        dq_ref[...] = dq.astype(BF16)
        dz_ref[...] = dz.astype(BF16)
        dkv_ref[...] += dkv
        dpc_ref[...] += dpc

    blk = pl.BlockSpec((tq, 512), lambda i: (i, 0))
    kvs = pl.BlockSpec(kvm.shape, lambda i: (0, 0))
    small = pl.BlockSpec((8, LANE), lambda i: (0, 0))
    return pl.pallas_call(
        body, name="attn_c_bwd", grid=(s // tq,),
        in_specs=[pl.BlockSpec((tq, 512), lambda i: (i, P_QC // 512)), pl.BlockSpec((tq, 512), lambda i: (i, P_ZC // 512)),
                  kvs, small, pl.BlockSpec((tq, 512), lambda i: (i, 3))],
        out_specs=[blk, blk, kvs, small],
        out_shape=[jax.ShapeDtypeStruct((s, 512), BF16), jax.ShapeDtypeStruct((s, 512), BF16),
                   jax.ShapeDtypeStruct(kvm.shape, F32), jax.ShapeDtypeStruct((8, LANE), F32)],
        compiler_params=_params(("arbitrary",)),
    )(proj, proj, kvm, pc, d_mixed)


def _pad_row(v, width=LANE):
    v = v.reshape(1, -1)
    return jnp.pad(v, ((0, 0), (0, width - v.shape[1])))


def _local_step(x, mem, target, norm_w, w_perm_t, w_blocks_t, conv_w, pa, pb, pc, mem_norm_w, w_kv, w_out, gather=None,
                exchange=None):
    s = x.shape[0]
    cos2, sin2 = _rope_tables(s)
    hn = _rms_fwd(x, norm_w)
    wide = dict(tm=1024, tn=512, tk=2048)
    proj = _matmul(hn, w_perm_t, "nt", F32, "mm_proj", **wide)
    cqkv = _conv_fwd(proj, conv_w)
    if gather is None:
        mixed, o_sum, states, inverses = _delta_fwd(cqkv, proj, pa)
    else:
        mixed, o_sum, states, inverses, *arrived = _delta_fwd(cqkv, proj, pa, gather[0])
        w_out, w_kv = gather[1](*arrived)
    mixed = _attn_b_fwd(proj, cos2, sin2, pb, mixed)
    kvm = _mem_kv_fwd(mem, mem_norm_w, w_kv)
    mixed = _attn_c_fwd(proj, kvm, pc, mixed)
    dy, dyb, loss_parts = _out_loss(mixed, w_out, x, target)

    d_mixed = _matmul(dyb, w_out, "nt", F32, "mm_dmixed", **wide)
    g_w_out = _matmul(mixed, dyb, "tn", F32, "mm_gwout", **wide)
    d_qc, d_zc, d_kvm, d_pc = _attn_c_bwd(proj, kvm, pc, d_mixed)
    g_w_kv, g_mem_norm = _mem_kv_bwd(mem, mem_norm_w, w_kv, d_kvm)
    d_qb, d_zb, d_kb, d_vb, d_pb = _attn_b_bwd(proj, cos2, sin2, pb, d_mixed)
    d_o, d_za, d_pa_out = _delta_out_bwd(o_sum, proj, pa, d_mixed)
    early = exchange[0](g_w_out, g_w_kv) if exchange else None
    d_c, d_gt, d_pa_scan, *landed_early = _delta_bwd(cqkv, proj, pa, d_o, states, inverses, early)
    d_pa = d_pa_out + d_pa_scan
    d_qkv, g_conv = _conv_bwd(proj, conv_w, d_c)
    d_proj = _cotangent_blocks(d_qkv, d_za, d_gt, d_qb, d_kb, d_vb, d_zb, d_qc, d_zc)
    g_w_blocks_t = _matmul(d_proj, hn, "tn", F32, "mm_gwin", tm=512, tn=2048, tk=2048)
    late = exchange[1](g_w_blocks_t) if exchange else None
    g_x, g_norm, *landed_late = _input_grad(d_proj, w_blocks_t, x, norm_w, dy, late)
    return dict(loss_parts=loss_parts, g_x=g_x, g_norm=g_norm, g_w_blocks_t=g_w_blocks_t, g_conv=g_conv, d_pa=d_pa,
                d_pb=d_pb, d_pc=d_pc, g_mem_norm=g_mem_norm, g_w_kv=g_w_kv, g_w_out=g_w_out,
                landed=landed_late + landed_early)


_SEGMENTS = ((0, O_GT, 0), (O_GT, O_QB, P_GT), (O_QB, O_KB, P_QB), (O_KB, O_VB, P_KB), (O_VB, O_ZB, P_VB),
             (O_ZB, O_QC, P_ZB), (O_QC, O_ZC, P_QC), (O_ZC, IN_WIDTH, P_ZC))


def _permute_blocks(w4):
    parts = []
    for first, end, _ in sorted(_SEGMENTS, key=lambda seg: seg[2]):
        row = first
        while row < end:
            k = row // W_IN_BLOCK
            stop = min(end, (k + 1) * W_IN_BLOCK)
            parts.append(w4[k][row - k * W_IN_BLOCK:stop - k * W_IN_BLOCK, :])
            row = stop
    parts.append(jnp.zeros((P_WIDTH - IN_WIDTH, w4.shape[2]), w4.dtype))
    return jnp.concatenate(parts, axis=0)


def _cotangent_blocks(d_qkv, d_za, d_gt, d_qb, d_kb, d_vb, d_zb, d_qc, d_zc):
    s = d_qkv.shape[0]
    tr = min(256, s)
    pieces = (d_qkv, d_za, d_gt, d_qb, d_kb, d_vb, d_zb, d_qc, d_zc)

    def body(*refs):
        o_ref = refs[-1]
        tiles = [r[...].astype(BF16) for r in refs[:-1]]
        tiles[2] = tiles[2][:, :O_QB - O_GT]
        orig = jnp.concatenate(tiles, axis=1)
        pad = jnp.zeros((tr, W_IN_PAD - W_IN_BLOCK), BF16)
        parts = []
        for k in range(N_CHIPS):
            parts += [orig[:, k * W_IN_BLOCK:(k + 1) * W_IN_BLOCK], pad]
        o_ref[...] = jnp.concatenate(parts, axis=1)

    return pl.pallas_call(
        body, name="cotangent_blocks", grid=(s // tr,),
        in_specs=[pl.BlockSpec((tr, p.shape[1]), lambda i: (i, 0)) for p in pieces],
        out_specs=pl.BlockSpec((tr, N_CHIPS * W_IN_PAD), lambda i: (i, 0)),
        out_shape=jax.ShapeDtypeStruct((s, N_CHIPS * W_IN_PAD), BF16), compiler_params=_params(("parallel",)),
    )(*pieces)


HBM = pl.BlockSpec(memory_space=pltpu.HBM)


def _place():
    x, y, c = lax.axis_index("x"), lax.axis_index("y"), lax.axis_index("c")
    chips = [(1 - x, y), (x, 1 - y), (1 - x, 1 - y)]
    return x, y, c, 2 * x + y, chips, [2 * cx + cy for cx, cy in chips]


PIECE_ROWS_CAP = 600


def _remote(src, dst, send_sems, recv_sems, k, to):
    return pltpu.make_async_remote_copy(src_ref=src, dst_ref=dst, send_sem=send_sems.at[k], recv_sem=recv_sems.at[k],
                                        device_id=to, device_id_type=MESH)


def _half_cols(ref, c):
    half = ref.shape[-1] // 2
    return pl.ds(pl.multiple_of(c * half, LANE), half)


class _PairedGather:
    def __init__(self, blocks):
        n = len(blocks)
        self.operands = list(blocks)
        self.out_shapes = [jax.ShapeDtypeStruct((N_CHIPS,) + b.shape, b.dtype) for b in blocks]
        self.scratch_shapes = [pltpu.SemaphoreType.DMA((6 * n,)), pltpu.SemaphoreType.DMA((6 * n,))]

    @staticmethod
    def _copies(srcs, dsts, sems):
        x, y, c, me, chips, chip_ids = _place()
        sends, landed, passes, passed = [], [], [], []
        for a, (src, dst) in enumerate(zip(srcs, dsts)):
            mine, other = _half_cols(src, c), _half_cols(src, 1 - c)
            for j, (chip, cid) in enumerate(zip(chips, chip_ids)):
                sends.append(_remote(src.at[:, mine], dst.at[me, :, mine], sems[0], sems[1], 6 * a + j, (*chip, c)))
                here = dst.at[cid, :, mine]
                landed.append(_remote(here, here, sems[0], sems[1], 6 * a + j, (x, y, 1 - c)))
                passes.append(_remote(here, here, sems[0], sems[1], 6 * a + 3 + j, (x, y, 1 - c)))
                there = dst.at[cid, :, other]
                passed.append(_remote(there, there, sems[0], sems[1], 6 * a + 3 + j, (x, y, 1 - c)))
        return sends, landed, passes, passed

    def start(self, srcs, dsts, sems):
        for cp in self._copies(srcs, dsts, sems)[0]:
            cp.start()

    def middle(self, srcs, dsts, sems):
        _, landed, passes, _ = self._copies(srcs, dsts, sems)
        for arrived, onward in zip(landed, passes):
            arrived.wait_recv()
            onward.start()

    def finish(self, srcs, dsts, sems):
        sends, _, passes, passed = self._copies(srcs, dsts, sems)
        for cp in passed:
            cp.wait_recv()
        for cp in sends + passes:
            cp.wait_send()


def _all_gather_weights(bigs, conv_b):
    bigs = tuple(bigs)
    n_big = len(bigs)

    def body(*refs):
        srcs, conv_src = refs[:n_big], refs[n_big]
        dsts, conv_dst = refs[n_big + 1:2 * n_big + 1], refs[2 * n_big + 1]
        send_sems, recv_sems, local_sems = refs[2 * n_big + 2:]
        x, y, c, me, chips, chip_ids = _place()
        sibling = (x, y, 1 - c)
        local = [pltpu.make_async_copy(src, dst.at[me], local_sems.at[a]) for a, (src, dst) in enumerate(zip(srcs, dsts))]
        local.append(pltpu.make_async_copy(conv_src, conv_dst.at[me], local_sems.at[n_big]))
        for cp in local:
            cp.start()
        sends = []
        for a, (src, dst) in enumerate(zip(srcs, dsts)):
            mine = _half_cols(src, c)
            for j, chip in enumerate(chips):
                sends.append(_remote(src.at[:, mine], dst.at[me, :, mine], send_sems, recv_sems, 6 * a + j, (*chip, c)))
        for j, chip in enumerate(chips):
            sends.append(_remote(conv_src, conv_dst.at[me], send_sems, recv_sems, 6 * n_big + j, (*chip, c)))
        for cp in sends:
            cp.start()
        passed = []
        for a, (src, dst) in enumerate(zip(srcs, dsts)):
            mine = _half_cols(src, c)
            for j, cid in enumerate(chip_ids):
                landed = dst.at[cid, :, mine]
                _remote(landed, landed, send_sems, recv_sems, 6 * a + j, sibling).wait_recv()
                cp = _remote(landed, landed, send_sems, recv_sems, 6 * a + 3 + j, sibling)
                cp.start()
                passed.append(cp)
        for a, (src, dst) in enumerate(zip(srcs, dsts)):
            other = _half_cols(src, 1 - c)
            for j, cid in enumerate(chip_ids):
                landed = dst.at[cid, :, other]
                _remote(landed, landed, send_sems, recv_sems, 6 * a + 3 + j, sibling).wait_recv()
        for j, cid in enumerate(chip_ids):
            _remote(conv_src, conv_dst.at[cid], send_sems, recv_sems, 6 * n_big + j, sibling).wait_recv()
        for cp in sends + passed:
            cp.wait_send()
        for cp in local:
            cp.wait()

    n_sem = 6 * n_big + 3
    return pl.pallas_call(
        body, name="all_gather_weights",
        out_shape=[jax.ShapeDtypeStruct((N_CHIPS,) + w.shape, w.dtype) for w in bigs + (conv_b,)],
        in_specs=[pl.BlockSpec(memory_space=pltpu.VMEM)] * (n_big + 1), out_specs=[HBM] * (n_big + 1),
        scratch_shapes=[pltpu.SemaphoreType.DMA((n_sem,)), pltpu.SemaphoreType.DMA((n_sem,)),
                        pltpu.SemaphoreType.DMA((n_big + 1,))],
        compiler_params=_params(),
    )(*bigs, conv_b)


def _pair_exchange(grads, name):
    n = len(grads)
    pieces = [_row_tile(g.shape[1]) for g in grads]

    def body(*refs):
        srcs, gots = refs[:n], refs[n:2 * n]
        stages, narrow = refs[2 * n:3 * n], refs[3 * n:4 * n]
        send_sems, recv_sems, load_sems = refs[4 * n:]
        x, y, c, _, _, _ = _place()
        sibling = (x, y, 1 - c)
        for a in range(n):
            slabs, rows, _ = gots[a].shape
            piece = pieces[a]
            per_slab = rows // piece
            theirs = _half_cols(srcs[a], 1 - c)
            loads, sends = [], []
            for i in range(slabs * per_slab):
                k, r, slot = i // per_slab, i % per_slab, i % 2
                part = pl.ds(r * piece, piece)
                loads.append(pltpu.make_async_copy(srcs[a].at[k, part, theirs], stages[a].at[slot], load_sems.at[2 * a + slot]))
                sends.append(pltpu.make_async_remote_copy(
                    src_ref=narrow[a].at[slot], dst_ref=gots[a].at[k, part, :],
                    send_sem=send_sems.at[2 * a + slot], recv_sem=recv_sems.at[a], device_id=sibling, device_id_type=MESH))
            loads[0].start()
            for i in range(len(loads)):
                loads[i].wait()
                narrow[a][i % 2] = stages[a][i % 2].astype(BF16)
                sends[i].start()
                if i + 1 < len(loads):
                    if i >= 1:
                        sends[i - 1].wait_send()
                    loads[i + 1].start()
            for cp in sends[-2:]:
                cp.wait_send()
        for a in range(n):
            pltpu.make_async_remote_copy(src_ref=gots[a], dst_ref=gots[a], send_sem=send_sems.at[2 * a],
                                         recv_sem=recv_sems.at[a], device_id=sibling, device_id_type=MESH).wait_recv()

    halves = [jax.ShapeDtypeStruct((g.shape[0], g.shape[1], g.shape[2] // 2), BF16) for g in grads]
    return pl.pallas_call(
        body, name=name, out_shape=halves, in_specs=[HBM] * n, out_specs=[HBM] * n,
        scratch_shapes=[pltpu.VMEM((2, piece, g.shape[2] // 2), dt) for dt in (F32, BF16) for piece, g in zip(pieces, grads)]
        + [pltpu.SemaphoreType.DMA((2 * n,)), pltpu.SemaphoreType.DMA((n,)), pltpu.SemaphoreType.DMA((2 * n,))],
        compiler_params=_params(),
    )(*grads)


class _ChipExchange:
    def __init__(self, halves):
        n = len(halves)
        self.operands = list(halves)
        self.out_shapes = [jax.ShapeDtypeStruct((N_CHIPS - 1,) + h.shape[1:], h.dtype) for h in halves]
        self.scratch_shapes = [pltpu.SemaphoreType.DMA((3 * n,)), pltpu.SemaphoreType.DMA((3 * n,))]

    @staticmethod
    def _copies(srcs, lands, sems):
        _, _, c, _, chips, chip_ids = _place()
        return [_remote(src.at[cid], land.at[j], sems[0], sems[1], 3 * a + j, (*chip, c))
                for a, (src, land) in enumerate(zip(srcs, lands)) for j, (chip, cid) in enumerate(zip(chips, chip_ids))]

    def start(self, srcs, lands, sems):
        for cp in self._copies(srcs, lands, sems):
            cp.start()

    def finish(self, srcs, lands, sems):
        copies = self._copies(srcs, lands, sems)
        for cp in copies:
            cp.wait_recv()
        for cp in copies:
            cp.wait_send()


def _pair_gather(halves, rows):
    n = len(halves)

    def body(*refs):
        srcs, fulls = refs[:n], refs[n:2 * n]
        send_sems, recv_sems, local_sems = refs[2 * n:]
        x, y, c, _, _, _ = _place()
        copies = []
        for a in range(n):
            mine, src = _half_cols(fulls[a], c), srcs[a].at[pl.ds(0, rows[a]), :]
            keep = pltpu.make_async_copy(src, fulls[a].at[:, mine], local_sems.at[a])
            keep.start()
            give = _remote(src, fulls[a].at[:, mine], send_sems, recv_sems, a, (x, y, 1 - c))
            give.start()
            copies += [keep, give]
        for a in range(n):
            other, src = _half_cols(fulls[a], 1 - c), srcs[a].at[pl.ds(0, rows[a]), :]
            copies[2 * a].wait()
            copies[2 * a + 1].wait_send()
            _remote(src, fulls[a].at[:, other], send_sems, recv_sems, a, (x, y, 1 - c)).wait_recv()

    return pl.pallas_call(
        body, name="grad_pair_gather",
        out_shape=[jax.ShapeDtypeStruct((r, 2 * h.shape[1]), h.dtype) for r, h in zip(rows, halves)],
        in_specs=[pl.BlockSpec(memory_space=pltpu.VMEM)] * n, out_specs=[HBM] * n,
        scratch_shapes=[pltpu.SemaphoreType.DMA((n,)), pltpu.SemaphoreType.DMA((n,)), pltpu.SemaphoreType.DMA((n,))],
    )(*halves)


def _all_reduce_small(p):
    n_dev = 8

    def body(p_ref, o_ref, land, send_sems, recv_sems):
        x, y, c = lax.axis_index("x"), lax.axis_index("y"), lax.axis_index("c")
        me = 4 * x + 2 * y + c
        land[me] = p_ref[...]
        sends = []
        for k in range(1, n_dev):
            fx, fy, fc = (k >> 2) & 1, (k >> 1) & 1, k & 1
            to = (x ^ fx, y ^ fy, c ^ fc)
            cp = _remote(p_ref, land.at[me], send_sems, recv_sems, k - 1, to)
            cp.start()
            sends.append(cp)
        for k in range(1, n_dev):
            _remote(p_ref, land.at[me ^ k], send_sems, recv_sems, k - 1, (x, y, c)).wait_recv()
        total = land[0]
        for d in range(1, n_dev):
            total = total + land[d]
        o_ref[...] = total
        for cp in sends:
            cp.wait_send()

    vm = pl.BlockSpec(memory_space=pltpu.VMEM)
    return pl.pallas_call(
        body, name="all_reduce_small", out_shape=jax.ShapeDtypeStruct(p.shape, p.dtype), in_specs=[vm], out_specs=vm,
        scratch_shapes=[pltpu.VMEM((n_dev,) + p.shape, p.dtype), pltpu.SemaphoreType.DMA((n_dev - 1,)),
                        pltpu.SemaphoreType.DMA((n_dev - 1,))],
    )(p)


def _row_tile(rows):
    fits = [t for t in range(8, min(rows, PIECE_ROWS_CAP) + 1, 8) if rows % t == 0]
    return max(fits) if fits else rows


def _pair_sum(full, got, core, name):
    n, r, c = got.shape
    tr = _row_tile(r)

    def body(core_ref, a_ref, b_ref, o_ref):
        o_ref[...] = (a_ref[...] + b_ref[...].astype(F32)).astype(BF16)

    blk = pl.BlockSpec((None, tr, c), lambda i, j, core_ref: (i, j, 0))
    grid_spec = pltpu.PrefetchScalarGridSpec(
        num_scalar_prefetch=1, grid=(n, r // tr),
        in_specs=[pl.BlockSpec((None, tr, c), lambda i, j, core_ref: (i, j, core_ref[0])), blk], out_specs=blk)
    return pl.pallas_call(body, name=name, grid_spec=grid_spec, out_shape=jax.ShapeDtypeStruct(got.shape, BF16),
                          compiler_params=_params(("parallel", "parallel")))(core, full, got)


def _chip_sum(full, got, land, place, name):
    n, r, c = land.shape
    tr = _row_tile(r)

    def body(place_ref, a_ref, b_ref, l_ref, o_ref):
        total = a_ref[...] + b_ref[...].astype(F32)
        for j in range(n):
            total = total + l_ref[j].astype(F32)
        o_ref[...] = total

    grid_spec = pltpu.PrefetchScalarGridSpec(
        num_scalar_prefetch=1, grid=(r // tr,),
        in_specs=[pl.BlockSpec((None, tr, c), lambda i, p: (p[0], i, p[1])),
                  pl.BlockSpec((None, tr, c), lambda i, p: (p[0], i, 0)),
                  pl.BlockSpec((n, tr, c), lambda i, p: (0, i, 0))],
        out_specs=pl.BlockSpec((tr, c), lambda i, p: (i, 0)))
    return pl.pallas_call(body, name=name, grid_spec=grid_spec, out_shape=jax.ShapeDtypeStruct((r, c), F32),
                          compiler_params=_params(("parallel",)))(place, full, got, land)


def _adamw(w, g, m, v, name, echo=False):
    r, c = w.shape
    tr = _row_tile(r)
    tc = 1024 if c % 1024 == 0 else c

    def body(w_ref, g_ref, m_ref, v_ref, d_ref, mo_ref, vo_ref, *g_out):
        g_ = g_ref[...]
        for o in g_out:
            o[...] = g_
        m2 = ADAM_B1 * m_ref[...] + (1.0 - ADAM_B1) * g_
        v2 = ADAM_B2 * v_ref[...] + (1.0 - ADAM_B2) * jnp.square(g_)
        m_hat = m2 / (1.0 - ADAM_B1 ** ADAM_STEP)
        v_hat = v2 / (1.0 - ADAM_B2 ** ADAM_STEP)
        d_ref[...] = -ADAM_LR * (m_hat / (jnp.sqrt(v_hat) + ADAM_EPS) + ADAM_WD * w_ref[...])
        mo_ref[...] = m2
        vo_ref[...] = v2

    blk = pl.BlockSpec((tr, tc), lambda i, j: (i, j))
    n_out = 4 if echo else 3
    return pl.pallas_call(body, name=name, grid=(r // tr, c // tc), in_specs=[blk] * 4, out_specs=[blk] * n_out,
                          out_shape=[jax.ShapeDtypeStruct(w.shape, F32)] * n_out,
                          compiler_params=_params(("parallel", "parallel")))(w, g, m, v)


SMALL_NAMES = ("norm_w", "mem_norm_w", "o_norm_a", "q_norm_c", "k_norm_c", "q_norm_b", "k_norm_b",
               "a_log_fwd", "a_log_bwd", "dt_bias_fwd", "dt_bias_bwd", "sink_b")
SMALL_SIZES = (2048, 2048, 128, 128, 128, 64, 64, 8, 8, 8, 8, 8)
SMALL_LOSS = sum(SMALL_SIZES)
SMALL_CONV = 5120
SMALL_TOTAL = SMALL_CONV + CONV_K * 3 * A_WIDTH
SMALL_ROWS = SMALL_TOTAL // LANE


def _pack_small(parts, extra=None, conv=None):
    vec = [parts[n].reshape(-1) for n in SMALL_NAMES]
    vec.append(jnp.zeros((1,), F32) if extra is None else extra.reshape(1))
    vec.append(jnp.zeros((SMALL_CONV - SMALL_LOSS - 1,), F32))
    vec.append(jnp.zeros((SMALL_TOTAL - SMALL_CONV,), F32) if conv is None else conv.reshape(-1))
    return jnp.concatenate(vec).reshape(SMALL_ROWS, LANE)


def _unpack_small(packed):
    flat = packed.reshape(-1)
    out, off = {}, 0
    for n, size in zip(SMALL_NAMES, SMALL_SIZES):
        out[n] = flat[off:off + size].reshape(1, size)
        off += size
    return out


WEIGHT_ORDER = ("norm_w", "w_in", "conv_w_a", "a_log_fwd", "a_log_bwd", "dt_bias_fwd", "dt_bias_bwd", "o_norm_a",
                "q_norm_b", "k_norm_b", "sink_b", "mem_norm_w", "w_mem_kv", "q_norm_c", "k_norm_c", "w_out")


def kernel(x, mem, norm_w, w_in, conv_w_a, a_log_fwd, a_log_bwd, dt_bias_fwd, dt_bias_bwd, o_norm_a, q_norm_b, k_norm_b, sink_b, mem_norm_w, w_mem_kv, q_norm_c, k_norm_c, w_out, loss_target, m_norm_w, m_w_in, m_conv_w_a, m_a_log_fwd, m_a_log_bwd, m_dt_bias_fwd, m_dt_bias_bwd, m_o_norm_a, m_q_norm_b, m_k_norm_b, m_sink_b, m_mem_norm_w, m_w_mem_kv, m_q_norm_c, m_k_norm_c, m_w_out, v_norm_w, v_w_in, v_conv_w_a, v_a_log_fwd, v_a_log_bwd, v_dt_bias_fwd, v_dt_bias_bwd, v_o_norm_a, v_q_norm_b, v_k_norm_b, v_sink_b, v_mem_norm_w, v_w_mem_kv, v_q_norm_c, v_k_norm_c, v_w_out):
    weights = dict(norm_w=norm_w, w_in=w_in, conv_w_a=conv_w_a, a_log_fwd=a_log_fwd, a_log_bwd=a_log_bwd,
                   dt_bias_fwd=dt_bias_fwd, dt_bias_bwd=dt_bias_bwd, o_norm_a=o_norm_a, q_norm_b=q_norm_b,
                   k_norm_b=k_norm_b, sink_b=sink_b, mem_norm_w=mem_norm_w, w_mem_kv=w_mem_kv, q_norm_c=q_norm_c,
                   k_norm_c=k_norm_c, w_out=w_out)
    mom1 = dict(norm_w=m_norm_w, w_in=m_w_in, conv_w_a=m_conv_w_a, a_log_fwd=m_a_log_fwd, a_log_bwd=m_a_log_bwd,
                dt_bias_fwd=m_dt_bias_fwd, dt_bias_bwd=m_dt_bias_bwd, o_norm_a=m_o_norm_a, q_norm_b=m_q_norm_b,
                k_norm_b=m_k_norm_b, sink_b=m_sink_b, mem_norm_w=m_mem_norm_w, w_mem_kv=m_w_mem_kv,
                q_norm_c=m_q_norm_c, k_norm_c=m_k_norm_c, w_out=m_w_out)
    mom2 = dict(norm_w=v_norm_w, w_in=v_w_in, conv_w_a=v_conv_w_a, a_log_fwd=v_a_log_fwd, a_log_bwd=v_a_log_bwd,
                dt_bias_fwd=v_dt_bias_fwd, dt_bias_bwd=v_dt_bias_bwd, o_norm_a=v_o_norm_a, q_norm_b=v_q_norm_b,
                k_norm_b=v_k_norm_b, sink_b=v_sink_b, mem_norm_w=v_mem_norm_w, w_mem_kv=v_w_mem_kv,
                q_norm_c=v_q_norm_c, k_norm_c=v_k_norm_c, w_out=v_w_out)
    chip = 2 * lax.axis_index("x") + lax.axis_index("y")

    own_in = jnp.pad(jnp.transpose(w_in[0]).astype(BF16), ((0, W_IN_PAD - W_IN_BLOCK), (0, 0)))
    w_in4, conv4 = _all_gather_weights([own_in], conv_w_a[0])
    w_perm_t = _permute_blocks(w_in4)
    w_blocks_t = w_in4.reshape(N_CHIPS * W_IN_PAD, D_MODEL)
    conv_full = jnp.transpose(conv4, (1, 0, 2)).reshape(CONV_K, 3 * A_WIDTH)
    own_out, own_kv = w_out[0].astype(BF16), w_mem_kv[0].astype(BF16)

    def assemble(w_out4, w_kv4):
        w_out4 = lax.dynamic_update_index_in_dim(w_out4, own_out, chip, 0)
        w_kv4 = lax.dynamic_update_index_in_dim(w_kv4, own_kv, chip, 0)
        return w_out4.reshape(D_MODEL, D_MODEL), w_kv4.reshape(D_MODEL, 2 * C_HEADS * C_DIM)

    gather = (_PairedGather([own_out, own_kv]), assemble)
    pa = jnp.concatenate([_pad_row(a_log_fwd), _pad_row(a_log_bwd), _pad_row(dt_bias_fwd), _pad_row(dt_bias_bwd),
                          _pad_row(o_norm_a), jnp.zeros((3, LANE), F32)], axis=0)
    pb = jnp.concatenate([_pad_row(q_norm_b), _pad_row(k_norm_b), _pad_row(sink_b), jnp.zeros((5, LANE), F32)], axis=0)
    pc = jnp.concatenate([_pad_row(q_norm_c), _pad_row(k_norm_c), jnp.zeros((6, LANE), F32)], axis=0)

    full, got = {}, {}
    core = lax.axis_index("c").astype(jnp.int32).reshape(1)

    def pair_round(tag, blocks):
        names = [tag + "_%d" % i for i in range(len(blocks))]
        full.update(zip(names, blocks))
        got.update(zip(names, _pair_exchange(blocks, "grad_pair_exchange_" + tag)))
        return _ChipExchange([_pair_sum(full[n], got[n], core, "grad_pair_sum_" + n) for n in names])

    def early(g_w_out, g_w_kv):
        return pair_round("early", [g_w_out.reshape(N_CHIPS, D_MODEL // N_CHIPS, D_MODEL),
                                    g_w_kv.reshape(N_CHIPS, D_MODEL // N_CHIPS, 2 * C_HEADS * C_DIM)])

    def late(g_w_blocks_t):
        return pair_round("late", [g_w_blocks_t.reshape(N_CHIPS, W_IN_PAD, D_MODEL)])

    r = _local_step(x[0], mem[0], loss_target[0], norm_w, w_perm_t, w_blocks_t, conv_full, pa, pb, pc, mem_norm_w, None, None,
                    gather, (early, late))
    place = jnp.stack([chip, lax.axis_index("c")]).astype(jnp.int32)
    reduced = [_chip_sum(full[n], got[n], l, place, "grad_chip_sum_" + n)
               for n, l in zip(("late_0", "early_0", "early_1"), r["landed"])]
    g_w_in_t, g_w_out, g_w_kv = _pair_gather(reduced, [W_IN_BLOCK, D_MODEL // N_CHIPS, D_MODEL // N_CHIPS])

    d_pa, d_pb, d_pc = r["d_pa"], r["d_pb"], r["d_pc"]
    small_g = dict(norm_w=r["g_norm"], mem_norm_w=r["g_mem_norm"], o_norm_a=d_pa[4], q_norm_c=d_pc[0], k_norm_c=d_pc[1],
                   q_norm_b=d_pb[0, :B_DIM], k_norm_b=d_pb[1, :B_DIM], a_log_fwd=d_pa[0, :A_HEADS],
                   a_log_bwd=d_pa[1, :A_HEADS], dt_bias_fwd=d_pa[2, :A_HEADS], dt_bias_bwd=d_pa[3, :A_HEADS],
                   sink_b=d_pb[2, :B_HEADS])
    packed = _all_reduce_small(_pack_small(small_g, jnp.sum(r["loss_parts"][:, 0, 0]), r["g_conv"]))
    flat = packed.reshape(-1)
    loss = flat[SMALL_LOSS]
    conv_sum = flat[SMALL_CONV:].reshape(CONV_K, 3 * A_WIDTH)
    conv_cols = 3 * A_WIDTH // N_CHIPS
    g_conv = lax.dynamic_slice(conv_sum, (0, chip * conv_cols), (CONV_K, conv_cols))

    grads = _unpack_small(packed)
    grads["conv_w_a"] = g_conv
    delta, new_m, new_v = {}, {}, {}
    delta["conv_w_a"], new_m["conv_w_a"], new_v["conv_w_a"] = _adamw(conv_w_a[0], g_conv, m_conv_w_a[0], v_conv_w_a[0],
                                                                     "adamw_conv_w_a")
    for n, g in (("w_mem_kv", g_w_kv), ("w_out", g_w_out)):
        delta[n], new_m[n], new_v[n], grads[n] = _adamw(weights[n][0], g, mom1[n][0], mom2[n][0], "adamw_" + n, echo=True)
    stepped = _adamw(jnp.transpose(w_in[0]), g_w_in_t, jnp.transpose(m_w_in[0]), jnp.transpose(v_w_in[0]), "adamw_w_in",
                     echo=True)
    delta["w_in"], new_m["w_in"], new_v["w_in"], grads["w_in"] = (jnp.transpose(t) for t in stepped)
    d_s, m_s, v_s = _adamw(_pack_small(weights), packed, _pack_small(mom1), _pack_small(mom2), "adamw_small")
    d_s, m_s, v_s = _unpack_small(d_s), _unpack_small(m_s), _unpack_small(v_s)
    for n in SMALL_NAMES:
        delta[n], new_m[n], new_v[n] = d_s[n], m_s[n], v_s[n]

    def shaped(tree):
        return [tree[n].reshape(weights[n].shape) for n in WEIGHT_ORDER]

    return (loss, r["g_x"].reshape(x.shape), *shaped(grads), *shaped(delta), *shaped(new_m), *shaped(new_v))
```

```python
import jax
import jax.numpy as jnp
from jax import lax
from jax.experimental import pallas as pl
from jax.experimental.pallas import tpu as pltpu

F32 = jnp.float32
BF16 = jnp.bfloat16
HI = lax.Precision.HIGHEST
MESH = pl.DeviceIdType.MESH

D_MODEL = 2048
A_WIDTH = 1024
A_HEADS = 8
A_DIM = 128
CONV_K = 5
CHUNK = 64
B_HEADS = 8
B_KV = 2
B_DIM = 64
WINDOW = 128
C_HEADS = 4
C_DIM = 128
MEM_LEN = 256
ROPE_THETA = 10000.0
EPS = 1e-6
IN_WIDTH = 6432
N_CHIPS = 4
W_IN_BLOCK = IN_WIDTH // N_CHIPS
W_IN_PAD = 1664

LANE = 128
P_QA, P_KA, P_VA, P_ZA = 0, 1024, 2048, 3072
P_QB, P_ZB, P_QC, P_ZC = 4096, 4608, 5120, 5632
P_KB, P_VB, P_GT = 6144, 6272, 6400
P_WIDTH = 6656
O_GT, O_QB, O_KB, O_VB, O_ZB, O_QC, O_ZC = 4096, 4128, 4640, 4768, 4896, 5408, 5920

ADAM_LR, ADAM_B1, ADAM_B2, ADAM_EPS, ADAM_WD, ADAM_STEP = 0.001, 0.9, 0.999, 1e-08, 0.01, 10

VMEM_LIMIT = 56 * 1024 * 1024


def _params(sem=None):
    return pltpu.CompilerParams(dimension_semantics=sem, vmem_limit_bytes=VMEM_LIMIT)


def _dot(a, b, dims=(((1,), (0,)), ((), ())), precision=HI):
    return lax.dot_general(a, b, dims, precision=precision, preferred_element_type=F32)


_NN = (((1,), (0,)), ((), ()))
_NT = (((1,), (1,)), ((), ()))
_TN = (((0,), (0,)), ((), ()))


def _bdot(a, b, dims):
    return lax.dot_general(a.astype(BF16), b.astype(BF16), dims, preferred_element_type=F32)


@jax.custom_vjp
def _mm(a, b):
    return _bdot(a, b, _NN)


_mm.defvjp(lambda a, b: (_bdot(a, b, _NN), (a, b)),
           lambda res, ct: (_bdot(ct, res[1], _NT), _bdot(res[0], ct, _TN)))


@jax.custom_vjp
def _mm_nt(a, b):
    return _bdot(a, b, _NT)


_mm_nt.defvjp(lambda a, b: (_bdot(a, b, _NT), (a, b)),
              lambda res, ct: (_bdot(ct, res[1], _NN), _bdot(ct, res[0], _TN)))


@jax.custom_vjp
def _mm_tn(a, b):
    return _bdot(a, b, _TN)


_mm_tn.defvjp(lambda a, b: (_bdot(a, b, _TN), (a, b)),
              lambda res, ct: (_bdot(res[1], ct, _NT), _bdot(res[0], ct, _NN)))


def _rms(t, w):
    return t * lax.rsqrt(jnp.mean(t * t, axis=-1, keepdims=True) + EPS) * w


def _l2(t):
    return t * lax.rsqrt(jnp.sum(t * t, axis=-1, keepdims=True) + EPS)


def _silu(t):
    return t * jax.nn.sigmoid(t)


def _softplus(t):
    return jnp.maximum(t, 0.0) + jnp.log1p(jnp.exp(-jnp.abs(t)))


def _matmul(a, b, mode, out_dtype, name, tm=512, tn=512, tk=512):
    (m, k) = a.shape[::-1] if mode == "tn" else a.shape
    n = b.shape[0] if mode == "nt" else b.shape[1]
    tm, tn, tk = min(tm, m), min(tn, n), min(tk, k)
    assert m % tm == 0 and n % tn == 0 and k % tk == 0, (m, n, k, tm, tn, tk)
    if mode == "nn":
        a_spec = pl.BlockSpec((tm, tk), lambda i, j, kk: (i, kk))
        b_spec = pl.BlockSpec((tk, tn), lambda i, j, kk: (kk, j))
        dims = (((1,), (0,)), ((), ()))
    elif mode == "nt":
        a_spec = pl.BlockSpec((tm, tk), lambda i, j, kk: (i, kk))
        b_spec = pl.BlockSpec((tn, tk), lambda i, j, kk: (j, kk))
        dims = (((1,), (1,)), ((), ()))
    else:
        a_spec = pl.BlockSpec((tk, tm), lambda i, j, kk: (kk, i))
        b_spec = pl.BlockSpec((tk, tn), lambda i, j, kk: (kk, j))
        dims = (((0,), (0,)), ((), ()))
    nk = k // tk

    def body(a_ref, b_ref, o_ref, *scratch):
        if nk == 1:
            o_ref[...] = _bdot(a_ref[...], b_ref[...], dims).astype(out_dtype)
        else:
            acc_ref, kk = scratch[0], pl.program_id(2)

            @pl.when(kk == 0)
            def _():
                acc_ref[...] = jnp.zeros_like(acc_ref)

            acc_ref[...] += _bdot(a_ref[...], b_ref[...], dims)

            @pl.when(kk == nk - 1)
            def _():
                o_ref[...] = acc_ref[...].astype(out_dtype)

    return pl.pallas_call(
        body, name=name, grid=(m // tm, n // tn, nk),
        in_specs=[a_spec, b_spec], out_specs=pl.BlockSpec((tm, tn), lambda i, j, kk: (i, j)),
        out_shape=jax.ShapeDtypeStruct((m, n), out_dtype),
        scratch_shapes=[] if nk == 1 else [pltpu.VMEM((tm, tn), F32)],
        compiler_params=_params(("parallel", "parallel", "arbitrary")),
    )(a, b)


def _rms_fwd(x, w, tr=256):
    s, d = x.shape

    def body(x_ref, w_ref, o_ref):
        o_ref[...] = _rms(x_ref[...], w_ref[...]).astype(BF16)

    return pl.pallas_call(
        body, name="rms_fwd", grid=(s // tr,),
        in_specs=[pl.BlockSpec((tr, d), lambda i: (i, 0)), pl.BlockSpec((1, d), lambda i: (0, 0))],
        out_specs=pl.BlockSpec((tr, d), lambda i: (i, 0)),
        out_shape=jax.ShapeDtypeStruct((s, d), BF16), compiler_params=_params(("parallel",)),
    )(x, w)


def _input_grad(d_proj, w_t, x, w, dy, ride=None, tm=512, tk=512):
    s, k = d_proj.shape
    d = w_t.shape[1]
    tm = min(tm, s)
    nk = k // tk
    grid = (s // tm, nk)
    n_in = len(ride.operands) if ride else 0
    n_out = len(ride.out_shapes) if ride else 0

    def body(*refs):
        a_ref, b_ref, x_ref, w_ref, dy_ref = refs[:5]
        gx_ref, gw_ref = refs[5 + n_in:7 + n_in]
        acc_ref = refs[7 + n_in + n_out]
        riders = (refs[5:5 + n_in], refs[7 + n_in:7 + n_in + n_out], refs[8 + n_in + n_out:])
        kk = pl.program_id(1)
        step = pl.program_id(0) * nk + kk
        if ride:
            pl.when(step == 0)(lambda: ride.start(*riders))

        @pl.when(step == 0)
        def _():
            gw_ref[...] = jnp.zeros_like(gw_ref)

        @pl.when(kk == 0)
        def _():
            acc_ref[...] = jnp.zeros_like(acc_ref)

        acc_ref[...] += _bdot(a_ref[...], b_ref[...], _NN)

        @pl.when(kk == nk - 1)
        def _():
            _, vjp = jax.vjp(_rms, x_ref[...], w_ref[...])
            dx, dw = vjp(acc_ref[...])
            gx_ref[...] = dy_ref[...] + dx
            gw_ref[...] += dw

        if ride:
            pl.when(step == grid[0] * nk - 1)(lambda: ride.finish(*riders))

    row = pl.BlockSpec((tm, d), lambda i, kk: (i, 0))
    vec = pl.BlockSpec((1, d), lambda i, kk: (0, 0))
    return pl.pallas_call(
        body, name="input_grad", grid=grid,
        in_specs=[pl.BlockSpec((tm, tk), lambda i, kk: (i, kk)), pl.BlockSpec((tk, d), lambda i, kk: (kk, 0)), row, vec, row]
        + [HBM] * n_in,
        out_specs=[row, vec] + [HBM] * n_out,
        out_shape=[jax.ShapeDtypeStruct((s, d), F32), jax.ShapeDtypeStruct((1, d), F32)]
        + (list(ride.out_shapes) if ride else []),
        scratch_shapes=[pltpu.VMEM((tm, d), F32)] + (list(ride.scratch_shapes) if ride else []),
        compiler_params=_params(("arbitrary", "arbitrary")),
    )(d_proj, w_t, x, w, dy, *(ride.operands if ride else []))


def _out_loss(mixed, w_out, x, target, tm=1024, tn=512):
    s, d = x.shape
    tm = min(tm, s)
    ni, nj = s // tm, d // tn

    def body(m_ref, w_ref, x_ref, t_ref, dy_ref, dyb_ref, l_ref):
        err = x_ref[...] + _bdot(m_ref[...], w_ref[...], _NN) - t_ref[...]
        dy = err * (1.0 / d)
        dy_ref[...] = dy
        dyb_ref[...] = dy.astype(BF16)
        l_ref[...] = jnp.full(l_ref.shape, 0.5 * jnp.sum(jnp.sum(err * err, axis=1, keepdims=True) * (1.0 / d)), F32)

    tile = pl.BlockSpec((tm, tn), lambda i, j: (i, j))
    return pl.pallas_call(
        body, name="out_loss", grid=(ni, nj),
        in_specs=[pl.BlockSpec((tm, mixed.shape[1]), lambda i, j: (i, 0)),
                  pl.BlockSpec((mixed.shape[1], tn), lambda i, j: (0, j)), tile, tile],
        out_specs=[tile, tile, pl.BlockSpec((1, 8, LANE), lambda i, j: (i * nj + j, 0, 0))],
        out_shape=[jax.ShapeDtypeStruct((s, d), F32), jax.ShapeDtypeStruct((s, d), BF16),
                   jax.ShapeDtypeStruct((ni * nj, 8, LANE), F32)],
        compiler_params=_params(("parallel", "parallel")),
    )(mixed, w_out, x, target)


def _shift_rows(t, s):
    if s == 0:
        return t
    n = t.shape[0]
    rolled = pltpu.roll(t, (-s) % n, axis=0)
    idx = lax.broadcasted_iota(jnp.int32, t.shape, 0) + s
    return jnp.where((idx >= 0) & (idx < n), rolled, 0.0)


CONV_FWD_COLS = 512
CONV_BWD_COLS = 128


def _conv_fwd(proj, conv_w):
    s = proj.shape[0]
    cols, split = CONV_FWD_COLS, A_WIDTH // CONV_FWD_COLS
    nblk = 3 * A_WIDTH // cols

    def body(x_ref, w_ref, o_ref):
        x = x_ref[...]
        acc = jnp.zeros_like(x)
        for j in range(CONV_K):
            acc = acc + w_ref[j:j + 1, :] * _shift_rows(x, j - CONV_K // 2)
        o_ref[...] = acc

    return pl.pallas_call(
        body, name="conv_fwd", grid=(nblk,),
        in_specs=[pl.BlockSpec((s, cols), lambda i: (0, i)), pl.BlockSpec((CONV_K, cols), lambda i: (0, i))],
        out_specs=pl.BlockSpec((None, s, cols), lambda i: (i // split, 0, i % split)),
        out_shape=jax.ShapeDtypeStruct((3, s, A_WIDTH), F32), compiler_params=_params(("parallel",)),
    )(proj, conv_w)


def _conv_bwd(proj, conv_w, d_c):
    s = proj.shape[0]
    cols, split = CONV_BWD_COLS, A_WIDTH // CONV_BWD_COLS
    nblk = 3 * A_WIDTH // cols

    def body(x_ref, w_ref, g_ref, dx_ref, dw_ref):
        x, g = x_ref[...], g_ref[...]
        acc = jnp.zeros_like(x)
        for j in range(CONV_K):
            off = j - CONV_K // 2
            acc = acc + w_ref[j:j + 1, :] * _shift_rows(g, -off)
            dw_ref[j:j + 1, :] = jnp.sum(_shift_rows(x, off) * g, axis=0, keepdims=True)
        dx_ref[...] = acc.astype(BF16)

    col = pl.BlockSpec((s, cols), lambda i: (0, i))
    wsp = pl.BlockSpec((CONV_K, cols), lambda i: (0, i))
    dsp = pl.BlockSpec((None, s, cols), lambda i: (i // split, 0, i % split))
    return pl.pallas_call(
        body, name="conv_bwd", grid=(nblk,), in_specs=[col, wsp, dsp], out_specs=[col, wsp],
        out_shape=[jax.ShapeDtypeStruct((s, 3 * A_WIDTH), BF16), jax.ShapeDtypeStruct((CONV_K, 3 * A_WIDTH), F32)],
        compiler_params=_params(("parallel",)),
    )(proj, conv_w, d_c)


A_FWD_HEADS = 4
A_BWD_HEADS = 4


def _neumann_inverse(a):
    c = a.shape[-1]
    eye = (lax.broadcasted_iota(jnp.int32, (c, c), 0) == lax.broadcasted_iota(jnp.int32, (c, c), 1)).astype(F32)
    tinv = eye + a
    p = a
    for _ in range(5):
        p = _mm(p, p)
        tinv = tinv + _mm(tinv, p)
    return tinv


@jax.custom_vjp
def _unit_inverse(a):
    return _neumann_inverse(a)


def _unit_inverse_fwd(a):
    tinv = _neumann_inverse(a)
    return tinv, tinv


def _unit_inverse_bwd(tinv, ct):
    return (_bdot(_bdot(tinv, ct, _TN), tinv, _NT),)


_unit_inverse.defvjp(_unit_inverse_fwd, _unit_inverse_bwd)


@jax.custom_vjp
def _known_inverse(a, tinv):
    return tinv


_known_inverse.defvjp(lambda a, tinv: (tinv, tinv),
                      lambda tinv, ct: (_unit_inverse_bwd(tinv, ct)[0], jnp.zeros_like(tinv)))


def _a_chain(st, cq, ck, cv, alpha, beta_raw, a_log, dt_b, incl, strict, last, kept=None):
    c = CHUNK
    gb = -jnp.exp(a_log) * _softplus(alpha + dt_b)
    bb = jax.nn.sigmoid(beta_raw)
    q = _l2(_silu(cq)) * (A_DIM ** -0.5)
    k = _l2(_silu(ck))
    v = _silu(cv)

    gc = _dot(incl, jnp.broadcast_to(gb, (c, LANE)))
    tot = jnp.sum(gc * last, axis=0, keepdims=True)
    m1 = gc[:, :c]
    decay = incl * jnp.exp(incl * (m1 - m1.T))
    kb = k * bb
    vb = v * bb
    a = -(strict * decay * _mm_nt(kb, k))
    tinv = _unit_inverse(a) if kept is None else _known_inverse(a, kept)
    eg = jnp.exp(gc)
    u = _mm(tinv, vb)
    w = _mm(tinv, kb * eg)
    qk = _mm_nt(q, k) * decay
    v_new = u - _mm(w, st)
    o = _mm(q * eg, st) + _mm(qk, v_new)
    st_new = st * jnp.exp(tot) + _mm_tn(k * jnp.exp(tot - gc), v_new)
    return st_new, o, tinv


def _a_step(sts, cq, ck, cv, gts, pa, h0, kept=None):
    c = CHUNK
    lane = lax.broadcasted_iota(jnp.int32, (1, LANE), 1)
    ii = lax.broadcasted_iota(jnp.int32, (c, c), 0)
    jj = lax.broadcasted_iota(jnp.int32, (c, c), 1)
    row = lax.broadcasted_iota(jnp.int32, (c, 1), 0)

    def pick(t, col):
        return jnp.sum(jnp.where(lane == col, t, 0.0), axis=1, keepdims=True)

    alpha, beta_raw, a_log, dt_b, incl, strict, last = [], [], [], [], [], [], []
    for b in range(sts.shape[0]):
        h, rev = h0 + b // 2, b % 2
        alpha.append(pick(gts[b], h + 8 * rev))
        beta_raw.append(pick(gts[b], h + 16 + 8 * rev))
        a_log.append(pick(pa[rev:rev + 1, :], h))
        dt_b.append(pick(pa[2 + rev:3 + rev, :], h))
        incl.append(((ii <= jj) if rev else (ii >= jj)).astype(F32))
        strict.append(((ii < jj) if rev else (ii > jj)).astype(F32))
        last.append((row == (0 if rev else c - 1)).astype(F32))
    stack = lambda ts: jnp.concatenate([t[None] for t in ts], axis=0)
    return jax.vmap(_a_chain)(sts, cq, ck, cv, stack(alpha), stack(beta_raw), stack(a_log), stack(dt_b),
                              stack(incl), stack(strict), stack(last), kept)


def _a_final(o, za, pa):
    outs = []
    for j in range(o.shape[1] // A_DIM):
        ln = slice(j * A_DIM, (j + 1) * A_DIM)
        outs.append(_rms(o[:, ln], pa[4:5, :]) * _silu(za[:, ln]))
    return jnp.concatenate(outs, axis=1)


def _a_tiles(n, nchunk, heads):
    tiles = []
    for b in range(2 * heads):
        i = (nchunk - 1 - n) if b % 2 else n
        tiles.append((i, pl.ds(pl.multiple_of(i * CHUNK, CHUNK), CHUNK), slice((b // 2) * A_DIM, (b // 2 + 1) * A_DIM)))
    return tiles


def _a_load(tiles, c_ref, gt_ref):
    cq, ck, cv = (jnp.stack([c_ref[r, sl, ln] for _, sl, ln in tiles], axis=0) for r in range(3))
    return cq, ck, cv, jnp.stack([gt_ref[sl, :] for _, sl, _ in tiles], axis=0)


SCAN_STEPS_PER_TRIP = 4


def _loop_unrolled(n, step, init):
    per = SCAN_STEPS_PER_TRIP
    assert n % per == 0

    def trip(m, carry):
        for j in range(per):
            carry = step(per * m + j, carry, j % 2)
        return carry

    return lax.fori_loop(0, n // per, trip, init)


def _a_scan(h0, heads, nchunk, c_ref, gt_ref, pa, of_ref, ob_ref, s_ref, t_ref):
    def step(n, sts, parity):
        tiles = _a_tiles(n, nchunk, heads)
        sts_new, o, tinv = _a_step(sts, *_a_load(tiles, c_ref, gt_ref), pa, h0)
        for b, (i, sl, ln) in enumerate(tiles):
            s_ref[b, i] = sts[b]
            t_ref[b, i] = tinv[b]
            (ob_ref if b % 2 else of_ref)[sl, ln] = o[b]
        return sts_new

    _loop_unrolled(nchunk, step, jnp.zeros((2 * heads, A_DIM, A_DIM), F32))


def _a_specs(s, heads):
    wide = heads * A_DIM
    once = pl.Buffered(1)
    trio = pl.BlockSpec((3, s, wide), lambda g: (0, 0, g), pipeline_mode=once)
    gates = pl.BlockSpec((s, LANE), lambda g: (0, P_GT // LANE))
    small = pl.BlockSpec((8, LANE), lambda g: (0, 0))

    def cols(base):
        return pl.BlockSpec((s, wide), lambda g: (0, base // wide + g), pipeline_mode=once)

    state = pl.BlockSpec((2 * heads, s // CHUNK, A_DIM, A_DIM), lambda g: (g, 0, 0, 0), pipeline_mode=once)
    kept = pl.BlockSpec((2 * heads, s // CHUNK, CHUNK, CHUNK), lambda g: (g, 0, 0, 0), pipeline_mode=once)
    return wide, trio, gates, small, cols, state, kept


def _delta_fwd(cqkv, proj, pa, ride=None):
    s = cqkv.shape[1]
    nchunk = s // CHUNK
    heads = A_FWD_HEADS
    steps = A_HEADS // heads
    wide, trio, gates, small, cols, state, kept = _a_specs(s, heads)
    n_in = len(ride.operands) if ride else 0
    n_out = len(ride.out_shapes) if ride else 0

    def body(*refs):
        c_ref, gt_ref, za_ref, pa_ref = refs[:4]
        out_ref, o_ref, s_ref, t_ref = refs[4 + n_in:8 + n_in]
        ob_ref = refs[8 + n_in + n_out]
        riders = (refs[4:4 + n_in], refs[8 + n_in:8 + n_in + n_out], refs[9 + n_in + n_out:])
        g = pl.program_id(0)
        if ride:
            pl.when(g == 0)(lambda: ride.start(*riders))
            pl.when(g == steps - 1)(lambda: ride.middle(*riders))
        h0 = g * heads
        pa_v = pa_ref[...]
        _a_scan(h0, heads, nchunk, c_ref, gt_ref, pa_v, o_ref, ob_ref, s_ref, t_ref)
        o_ref[...] += ob_ref[...]
        out_ref[...] = _a_final(o_ref[...], za_ref[...], pa_v).astype(BF16)
        if ride:
            pl.when(g == steps - 1)(lambda: ride.finish(*riders))

    assert steps > 1
    return pl.pallas_call(
        body, name="delta_fwd", grid=(steps,),
        in_specs=[trio, gates, cols(P_ZA), small] + [HBM] * n_in,
        out_specs=[cols(0), cols(0), state, kept] + [HBM] * n_out,
        out_shape=[jax.ShapeDtypeStruct((s, D_MODEL), BF16),
                   jax.ShapeDtypeStruct((s, A_WIDTH), F32),
                   jax.ShapeDtypeStruct((2 * A_HEADS, nchunk, A_DIM, A_DIM), F32),
                   jax.ShapeDtypeStruct((2 * A_HEADS, nchunk, CHUNK, CHUNK), F32)]
        + (list(ride.out_shapes) if ride else []),
        scratch_shapes=[pltpu.VMEM((s, wide), F32)] + (list(ride.scratch_shapes) if ride else []),
        compiler_params=_params(("arbitrary",)),
    )(cqkv, proj, proj, pa, *(ride.operands if ride else []))


def _delta_out_bwd(o_sum, proj, pa, d_mixed, tr=256):
    s = o_sum.shape[0]

    def body(o_ref, za_ref, pa_ref, dm_ref, do_ref, dza_ref, dpa_ref):
        @pl.when(pl.program_id(0) == 0)
        def _():
            dpa_ref[...] = jnp.zeros_like(dpa_ref)

        _, vjp = jax.vjp(_a_final, o_ref[...], za_ref[...], pa_ref[...])
        d_o, d_za, dpa = vjp(dm_ref[...].astype(F32))
        do_ref[...] = d_o
        dza_ref[...] = d_za.astype(BF16)
        dpa_ref[...] += dpa

    def rows(col):
        return pl.BlockSpec((tr, A_WIDTH), lambda i: (i, col))

    small = pl.BlockSpec((8, LANE), lambda i: (0, 0))
    return pl.pallas_call(
        body, name="delta_out_bwd", grid=(s // tr,), in_specs=[rows(0), rows(P_ZA // A_WIDTH), small, rows(0)],
        out_specs=[rows(0), rows(0), small],
        out_shape=[jax.ShapeDtypeStruct((s, A_WIDTH), F32), jax.ShapeDtypeStruct((s, A_WIDTH), BF16),
                   jax.ShapeDtypeStruct((8, LANE), F32)],
        compiler_params=_params(("arbitrary",)),
    )(o_sum, proj, pa, d_mixed)


def _delta_bwd(cqkv, proj, pa, d_o, states, inverses, ride=None):
    s = cqkv.shape[1]
    nchunk = s // CHUNK
    heads = A_BWD_HEADS
    steps = A_HEADS // heads
    wide, trio, gates, small, cols, _, _ = _a_specs(s, heads)
    n_in = len(ride.operands) if ride else 0
    n_out = len(ride.out_shapes) if ride else 0

    def body(*refs):
        c_ref, gt_ref, pa_ref, do_ref, s_hbm, t_hbm = refs[:6]
        dc_ref, dgt_ref, dpa_ref = refs[6 + n_in:9 + n_in]
        s_buf, t_buf, s_sems = refs[9 + n_in + n_out:12 + n_in + n_out]
        riders = (refs[6:6 + n_in], refs[9 + n_in:9 + n_in + n_out], refs[12 + n_in + n_out:])
        if ride:
            pl.when(pl.program_id(0) == 0)(lambda: ride.start(*riders))
        h0 = pl.program_id(0) * heads
        pa_v = pa_ref[...]

        @pl.when(h0 == 0)
        def _():
            dgt_ref[...] = jnp.zeros_like(dgt_ref)
            dpa_ref[...] = jnp.zeros_like(dpa_ref)

        dc_ref[...] = jnp.zeros_like(dc_ref)

        def state_copies(n, slot):
            tiles = _a_tiles(nchunk - 1 - n, nchunk, heads)
            return ([pltpu.make_async_copy(s_hbm.at[2 * h0 + b, i], s_buf.at[slot, b], s_sems.at[0, slot, b])
                     for b, (i, _, _) in enumerate(tiles)]
                    + [pltpu.make_async_copy(t_hbm.at[2 * h0 + b, i], t_buf.at[slot, b], s_sems.at[1, slot, b])
                       for b, (i, _, _) in enumerate(tiles)])

        for cp in state_copies(0, 0):
            cp.start()

        def step(n, carry, parity):
            d_sts, dpa = carry
            tiles = _a_tiles(nchunk - 1 - n, nchunk, heads)
            for cp in state_copies(n, parity):
                cp.wait()

            @pl.when(n + 1 < nchunk)
            def _():
                for cp in state_copies(n + 1, 1 - parity):
                    cp.start()

            sts, kept = s_buf[parity], t_buf[parity]
            d_o_t = jnp.stack([do_ref[sl, ln] for _, sl, ln in tiles], axis=0)
            _, vjp_c = jax.vjp(lambda *a: _a_step(*a, h0, kept)[:2], sts, *_a_load(tiles, c_ref, gt_ref), pa_v)
            d_prev, dcq, dck, dcv, dgts, dpa_i = vjp_c((d_sts, d_o_t))
            for b, (_, sl, ln) in enumerate(tiles):
                for r, dc in enumerate((dcq, dck, dcv)):
                    dc_ref[r, sl, ln] += dc[b]
                dgt_ref[sl, :] += dgts[b]
            return d_prev, dpa + dpa_i

        init = (jnp.zeros((2 * heads, A_DIM, A_DIM), F32), jnp.zeros((8, LANE), F32))
        _, dpa_out = lax.fori_loop(0, nchunk, lambda n, carry: step(n, carry, n % 2), init)
        dpa_ref[...] += dpa_out
        if ride:
            pl.when(pl.program_id(0) == steps - 1)(lambda: ride.finish(*riders))

    fixed = pl.BlockSpec((s, LANE), lambda g: (0, 0))
    return pl.pallas_call(
        body, name="delta_bwd", grid=(steps,),
        in_specs=[trio, gates, small, cols(0), pl.BlockSpec(memory_space=pl.ANY), pl.BlockSpec(memory_space=pl.ANY)]
        + [HBM] * n_in,
        out_specs=[trio, fixed, small] + [HBM] * n_out,
        out_shape=[jax.ShapeDtypeStruct((3, s, A_WIDTH), F32), jax.ShapeDtypeStruct((s, LANE), F32),
                   jax.ShapeDtypeStruct((8, LANE), F32)] + (list(ride.out_shapes) if ride else []),
        scratch_shapes=[pltpu.VMEM((2, 2 * heads, A_DIM, A_DIM), F32), pltpu.VMEM((2, 2 * heads, CHUNK, CHUNK), F32),
                        pltpu.SemaphoreType.DMA((2, 2, 2 * heads))]
        + (list(ride.scratch_shapes) if ride else []),
        compiler_params=_params(("arbitrary",)),
    )(cqkv, proj, pa, d_o, states, inverses, *(ride.operands if ride else []))


def _rope_tables(s):
    inv = ROPE_THETA ** (-jnp.arange(0, B_DIM, 2, dtype=F32) / B_DIM)
    ang = jnp.arange(s, dtype=F32)[:, None] * inv[None, :]
    cos, sin = jnp.cos(ang), jnp.sin(ang)
    return jnp.concatenate([cos, cos], axis=1), jnp.concatenate([-sin, sin], axis=1)


def _b_block(q_t, z_t, k3, v3, cos_q, sin_q, cos_k, sin_k, pb, n, nb):
    w = WINDOW
    def swap(t):
        return jnp.concatenate([t[:, B_DIM // 2:], t[:, :B_DIM // 2]], axis=1)

    grp = B_HEADS // B_KV
    qi = lax.broadcasted_iota(jnp.int32, (grp * w, 3 * w), 0) & (w - 1)
    kj = lax.broadcasted_iota(jnp.int32, (grp * w, 3 * w), 1)
    kpos = kj + (n - 1) * w
    mask = (jnp.abs(kj - w - qi) <= w) & (kpos >= 0) & (kpos < nb * w)
    lane = lax.broadcasted_iota(jnp.int32, (1, LANE), 1)
    qn, kn = pb[0:1, :B_DIM], pb[1:2, :B_DIM]
    cos_g = jnp.concatenate([cos_q] * grp, axis=0)
    sin_g = jnp.concatenate([sin_q] * grp, axis=0)
    def group(q, k, v, sink):
        k = _rms(k, kn)
        k = k * cos_k + swap(k) * sin_k
        q = _rms(q, qn)
        q = q * cos_g + swap(q) * sin_g
        s = _mm_nt(q, k) * (B_DIM ** -0.5)
        s = jnp.where(mask, s, -jnp.inf)
        m = jnp.maximum(jnp.max(s, axis=1, keepdims=True), sink)
        p = jnp.exp(s - m)
        p = p / (jnp.sum(p, axis=1, keepdims=True) + jnp.exp(sink - m))
        return _mm(p, v)

    stack = lambda ts: jnp.concatenate([t[None] for t in ts], axis=0)
    qs, ks, vs, sinks = [], [], [], []
    for hk in range(B_KV):
        heads = [hk * grp + g for g in range(grp)]
        ks.append(k3[:, hk * B_DIM:(hk + 1) * B_DIM])
        vs.append(v3[:, hk * B_DIM:(hk + 1) * B_DIM])
        qs.append(jnp.concatenate([q_t[:, hq * B_DIM:(hq + 1) * B_DIM] for hq in heads], axis=0))
        sinks.append(jnp.concatenate(
            [jnp.broadcast_to(jnp.sum(jnp.where(lane == hq, pb[2:3, :], 0.0), axis=1, keepdims=True), (w, 1))
             for hq in heads], axis=0))
    o = jax.vmap(group)(stack(qs), stack(ks), stack(vs), stack(sinks))
    outs = [o[hk, g * w:(g + 1) * w, :] for hk in range(B_KV) for g in range(grp)]
    return jnp.concatenate(outs, axis=1) * _silu(z_t)


def _b_specs(s):
    nb = s // WINDOW
    qsp = pl.BlockSpec((WINDOW, 512), lambda n: (n, P_QB // 512))
    zsp = pl.BlockSpec((WINDOW, 512), lambda n: (n, P_ZB // 512))

    def three(col, width):
        return [pl.BlockSpec((WINDOW, width), lambda n: (jnp.maximum(n - 1, 0), col)),
                pl.BlockSpec((WINDOW, width), lambda n: (n, col)),
                pl.BlockSpec((WINDOW, width), lambda n: (jnp.minimum(n + 1, nb - 1), col))]

    tab = pl.BlockSpec((WINDOW, B_DIM), lambda n: (n, 0))
    small = pl.BlockSpec((8, LANE), lambda n: (0, 0))
    specs = [qsp, zsp] + three(P_KB // LANE, LANE) + three(P_VB // LANE, LANE) + [tab, tab] + three(0, B_DIM) + three(0, B_DIM) + [small]
    return nb, specs


def _b_args(proj, cos2, sin2, pb):
    return (proj, proj, proj, proj, proj, proj, proj, proj, cos2, sin2, cos2, cos2, cos2, sin2, sin2, sin2, pb)


def _b_load(refs):
    (q_ref, z_ref, kp, kc, kx, vp, vc, vx, cq, sq, ckp, ckc, ckx, skp, skc, skx, pb_ref) = refs
    cat = lambda *r: jnp.concatenate([t[...] for t in r], axis=0)
    return (q_ref[...], z_ref[...], cat(kp, kc, kx), cat(vp, vc, vx), cq[...], sq[...], cat(ckp, ckc, ckx),
            cat(skp, skc, skx), pb_ref[...])


def _attn_b_fwd(proj, cos2, sin2, pb, mixed):
    s = proj.shape[0]
    nb, specs = _b_specs(s)

    def body(*refs):
        o_ref = refs[-1]
        args = _b_load(refs[:-2])
        o_ref[...] = _b_block(*args, pl.program_id(0), nb).astype(BF16)

    return pl.pallas_call(
        body, name="attn_b_fwd", grid=(nb,), in_specs=specs + [pl.BlockSpec(memory_space=pl.ANY)],
        out_specs=pl.BlockSpec((WINDOW, 512), lambda n: (n, A_WIDTH // 512)),
        out_shape=jax.ShapeDtypeStruct(mixed.shape, mixed.dtype), input_output_aliases={len(specs): 0},
        compiler_params=_params(("parallel",)),
    )(*_b_args(proj, cos2, sin2, pb), mixed)


def _attn_b_bwd(proj, cos2, sin2, pb, d_mixed):
    s = proj.shape[0]
    nb, specs = _b_specs(s)
    w = WINDOW

    def body(*refs):
        dm_ref, dq_ref, dz_ref, dk_ref, dv_ref, dpb_ref = refs[-6:]
        n = pl.program_id(0)
        q_t, z_t, k3, v3, cq, sq, ck, sk, pb_v = _b_load(refs[:-6])

        @pl.when(n == 0)
        def _():
            dk_ref[...] = jnp.zeros_like(dk_ref)
            dv_ref[...] = jnp.zeros_like(dv_ref)
            dpb_ref[...] = jnp.zeros_like(dpb_ref)

        def f(q_, z_, k_, v_, pb_):
            return _b_block(q_, z_, k_, v_, cq, sq, ck, sk, pb_, n, nb)

        _, vjp = jax.vjp(f, q_t, z_t, k3, v3, pb_v)
        dq, dz, dk3, dv3, dpb = vjp(dm_ref[...])
        dq_ref[...] = dq.astype(BF16)
        dz_ref[...] = dz.astype(BF16)
        dpb_ref[...] += dpb

        def add(j, cond):
            @pl.when(cond)
            def _():
                rows = pl.ds(pl.multiple_of((n - 1 + j) * w, w), w)
                dk_ref[rows, :] += dk3[j * w:(j + 1) * w, :]
                dv_ref[rows, :] += dv3[j * w:(j + 1) * w, :]

        add(0, n > 0)
        add(1, n >= 0)
        add(2, n < nb - 1)

    blk = pl.BlockSpec((w, 512), lambda n: (n, 0))
    whole = pl.BlockSpec((s, LANE), lambda n: (0, 0))
    small = pl.BlockSpec((8, LANE), lambda n: (0, 0))
    return pl.pallas_call(
        body, name="attn_b_bwd", grid=(nb,),
        in_specs=specs + [pl.BlockSpec((w, 512), lambda n: (n, 2))],
        out_specs=[blk, blk, whole, whole, small],
        out_shape=[jax.ShapeDtypeStruct((s, 512), BF16), jax.ShapeDtypeStruct((s, 512), BF16),
                   jax.ShapeDtypeStruct((s, LANE), F32), jax.ShapeDtypeStruct((s, LANE), F32),
                   jax.ShapeDtypeStruct((8, LANE), F32)],
        compiler_params=_params(("arbitrary",)),
    )(*_b_args(proj, cos2, sin2, pb), d_mixed)


def _mem_kv_fwd(mem, mem_norm_w, w_kv):
    def body(mem_ref, nw_ref, w_ref, kv_ref):
        mn = _rms(mem_ref[...], nw_ref[...]).astype(BF16)
        kv_ref[...] = jnp.dot(mn, w_ref[...], preferred_element_type=F32)

    return pl.pallas_call(
        body, name="mem_kv_fwd", out_shape=jax.ShapeDtypeStruct((MEM_LEN, 2 * C_HEADS * C_DIM), F32),
        compiler_params=_params(),
    )(mem, mem_norm_w, w_kv)


def _mem_kv_bwd(mem, mem_norm_w, w_kv, d_kv):
    def body(mem_ref, nw_ref, w_ref, g_ref, gw_ref, gn_ref):
        mn, vjp = jax.vjp(_rms, mem_ref[...], nw_ref[...])
        g = g_ref[...].astype(BF16)
        gw_ref[...] = lax.dot_general(mn.astype(BF16), g, (((0,), (0,)), ((), ())), preferred_element_type=F32)
        d_mn = lax.dot_general(g, w_ref[...], (((1,), (1,)), ((), ())), preferred_element_type=F32)
        gn_ref[...] = vjp(d_mn)[1]

    return pl.pallas_call(
        body, name="mem_kv_bwd",
        out_shape=[jax.ShapeDtypeStruct((D_MODEL, 2 * C_HEADS * C_DIM), F32), jax.ShapeDtypeStruct((1, D_MODEL), F32)],
        compiler_params=_params(),
    )(mem, mem_norm_w, w_kv, d_kv)


def _c_tile(q_t, z_t, kvm, pc):
    width = C_HEADS * C_DIM

    def head(q, k, v):
        q = _rms(q, pc[0:1, :])
        k = _rms(k, pc[1:2, :])
        s = _mm_nt(q, k) * (C_DIM ** -0.5)
        p = jnp.exp(s - jnp.max(s, axis=1, keepdims=True))
        p = p / jnp.sum(p, axis=1, keepdims=True)
        return _mm(p, v)

    def stack(t, first):
        return jnp.concatenate([t[None, :, first + h * C_DIM:first + (h + 1) * C_DIM] for h in range(C_HEADS)], axis=0)

    o = jax.vmap(head)(stack(q_t, 0), stack(kvm, 0), stack(kvm, width))
    return jnp.concatenate([o[h] for h in range(C_HEADS)], axis=1) * _silu(z_t)


def _attn_c_fwd(proj, kvm, pc, mixed, tq=256):
    s = proj.shape[0]

    def body(q_ref, z_ref, kv_ref, pc_ref, mixed_ref, o_ref):
        o_ref[...] = _c_tile(q_ref[...], z_ref[...], kv_ref[...], pc_ref[...]).astype(BF16)

    return pl.pallas_call(
        body, name="attn_c_fwd", grid=(s // tq,),
        in_specs=[pl.BlockSpec((tq, 512), lambda i: (i, P_QC // 512)), pl.BlockSpec((tq, 512), lambda i: (i, P_ZC // 512)),
                  pl.BlockSpec(kvm.shape, lambda i: (0, 0)), pl.BlockSpec((8, LANE), lambda i: (0, 0)),
                  pl.BlockSpec(memory_space=pl.ANY)],
        out_specs=pl.BlockSpec((tq, 512), lambda i: (i, (A_WIDTH + 512) // 512)),
        out_shape=jax.ShapeDtypeStruct(mixed.shape, mixed.dtype), input_output_aliases={4: 0},
        compiler_params=_params(("parallel",)),
    )(proj, proj, kvm, pc, mixed)


def _attn_c_bwd(proj, kvm, pc, d_mixed, tq=256):
    s = proj.shape[0]

    def body(q_ref, z_ref, kv_ref, pc_ref, dm_ref, dq_ref, dz_ref, dkv_ref, dpc_ref):
        @pl.when(pl.program_id(0) == 0)
        def _():
            dkv_ref[...] = jnp.zeros_like(dkv_ref)
            dpc_ref[...] = jnp.zeros_like(dpc_ref)

        _, vjp = jax.vjp(_c_tile, q_ref[...], z_ref[...], kv_ref[...], pc_ref[...])
        dq, dz, dkv, dpc = vjp(dm_ref[...])
        dq_ref[...] = dq.astype(BF16)
        dz_ref[...] = dz.astype(BF16)
        dkv_ref[...] += dkv
        dpc_ref[...] += dpc

    blk = pl.BlockSpec((tq, 512), lambda i: (i, 0))
    kvs = pl.BlockSpec(kvm.shape, lambda i: (0, 0))
    small = pl.BlockSpec((8, LANE), lambda i: (0, 0))
    return pl.pallas_call(
        body, name="attn_c_bwd", grid=(s // tq,),
        in_specs=[pl.BlockSpec((tq, 512), lambda i: (i, P_QC // 512)), pl.BlockSpec((tq, 512), lambda i: (i, P_ZC // 512)),
                  kvs, small, pl.BlockSpec((tq, 512), lambda i: (i, 3))],
        out_specs=[blk, blk, kvs, small],
        out_shape=[jax.ShapeDtypeStruct((s, 512), BF16), jax.ShapeDtypeStruct((s, 512), BF16),
                   jax.ShapeDtypeStruct(kvm.shape, F32), jax.ShapeDtypeStruct((8, LANE), F32)],
        compiler_params=_params(("arbitrary",)),
    )(proj, proj, kvm, pc, d_mixed)


def _pad_row(v, width=LANE):
    v = v.reshape(1, -1)
    return jnp.pad(v, ((0, 0), (0, width - v.shape[1])))


def _local_step(x, mem, target, norm_w, w_perm_t, w_blocks_t, conv_w, pa, pb, pc, mem_norm_w, w_kv, w_out, gather=None,
                exchange=None):
    s = x.shape[0]
    cos2, sin2 = _rope_tables(s)
    hn = _rms_fwd(x, norm_w)
    wide = dict(tm=1024, tn=512, tk=2048)
    proj = _matmul(hn, w_perm_t, "nt", F32, "mm_proj", **wide)
    cqkv = _conv_fwd(proj, conv_w)
    if gather is None:
        mixed, o_sum, states, inverses = _delta_fwd(cqkv, proj, pa)
    else:
        mixed, o_sum, states, inverses, *arrived = _delta_fwd(cqkv, proj, pa, gather[0])
        w_out, w_kv = gather[1](*arrived)
    mixed = _attn_b_fwd(proj, cos2, sin2, pb, mixed)
    kvm = _mem_kv_fwd(mem, mem_norm_w, w_kv)
    mixed = _attn_c_fwd(proj, kvm, pc, mixed)
    dy, dyb, loss_parts = _out_loss(mixed, w_out, x, target)

    d_mixed = _matmul(dyb, w_out, "nt", F32, "mm_dmixed", **wide)
    g_w_out = _matmul(mixed, dyb, "tn", F32, "mm_gwout", **wide)
    d_qc, d_zc, d_kvm, d_pc = _attn_c_bwd(proj, kvm, pc, d_mixed)
    g_w_kv, g_mem_norm = _mem_kv_bwd(mem, mem_norm_w, w_kv, d_kvm)
    d_qb, d_zb, d_kb, d_vb, d_pb = _attn_b_bwd(proj, cos2, sin2, pb, d_mixed)
    d_o, d_za, d_pa_out = _delta_out_bwd(o_sum, proj, pa, d_mixed)
    early = exchange[0](g_w_out, g_w_kv) if exchange else None
    d_c, d_gt, d_pa_scan, *landed_early = _delta_bwd(cqkv, proj, pa, d_o, states, inverses, early)
    d_pa = d_pa_out + d_pa_scan
    d_qkv, g_conv = _conv_bwd(proj, conv_w, d_c)
    d_proj = _cotangent_blocks(d_qkv, d_za, d_gt, d_qb, d_kb, d_vb, d_zb, d_qc, d_zc)
    g_w_blocks_t = _matmul(d_proj, hn, "tn", F32, "mm_gwin", tm=512, tn=2048, tk=2048)
    late = exchange[1](g_w_blocks_t) if exchange else None
    g_x, g_norm, *landed_late = _input_grad(d_proj, w_blocks_t, x, norm_w, dy, late)
    return dict(loss_parts=loss_parts, g_x=g_x, g_norm=g_norm, g_w_blocks_t=g_w_blocks_t, g_conv=g_conv, d_pa=d_pa,
                d_pb=d_pb, d_pc=d_pc, g_mem_norm=g_mem_norm, g_w_kv=g_w_kv, g_w_out=g_w_out,
                landed=landed_late + landed_early)


_SEGMENTS = ((0, O_GT, 0), (O_GT, O_QB, P_GT), (O_QB, O_KB, P_QB), (O_KB, O_VB, P_KB), (O_VB, O_ZB, P_VB),
             (O_ZB, O_QC, P_ZB), (O_QC, O_ZC, P_QC), (O_ZC, IN_WIDTH, P_ZC))


def _permute_blocks(w4):
    parts = []
    for first, end, _ in sorted(_SEGMENTS, key=lambda seg: seg[2]):
        row = first
        while row < end:
            k = row // W_IN_BLOCK
            stop = min(end, (k + 1) * W_IN_BLOCK)
            parts.append(w4[k][row - k * W_IN_BLOCK:stop - k * W_IN_BLOCK, :])
            row = stop
    parts.append(jnp.zeros((P_WIDTH - IN_WIDTH, w4.shape[2]), w4.dtype))
    return jnp.concatenate(parts, axis=0)


def _cotangent_blocks(d_qkv, d_za, d_gt, d_qb, d_kb, d_vb, d_zb, d_qc, d_zc):
    s = d_qkv.shape[0]
    tr = min(256, s)
    pieces = (d_qkv, d_za, d_gt, d_qb, d_kb, d_vb, d_zb, d_qc, d_zc)

    def body(*refs):
        o_ref = refs[-1]
        tiles = [r[...].astype(BF16) for r in refs[:-1]]
        tiles[2] = tiles[2][:, :O_QB - O_GT]
        orig = jnp.concatenate(tiles, axis=1)
        pad = jnp.zeros((tr, W_IN_PAD - W_IN_BLOCK), BF16)
        parts = []
        for k in range(N_CHIPS):
            parts += [orig[:, k * W_IN_BLOCK:(k + 1) * W_IN_BLOCK], pad]
        o_ref[...] = jnp.concatenate(parts, axis=1)

    return pl.pallas_call(
        body, name="cotangent_blocks", grid=(s // tr,),
        in_specs=[pl.BlockSpec((tr, p.shape[1]), lambda i: (i, 0)) for p in pieces],
        out_specs=pl.BlockSpec((tr, N_CHIPS * W_IN_PAD), lambda i: (i, 0)),
        out_shape=jax.ShapeDtypeStruct((s, N_CHIPS * W_IN_PAD), BF16), compiler_params=_params(("parallel",)),
    )(*pieces)


HBM = pl.BlockSpec(memory_space=pltpu.HBM)


def _place():
    x, y, c = lax.axis_index("x"), lax.axis_index("y"), lax.axis_index("c")
    chips = [(1 - x, y), (x, 1 - y), (1 - x, 1 - y)]
    return x, y, c, 2 * x + y, chips, [2 * cx + cy for cx, cy in chips]


PIECE_ROWS_CAP = 600


def _remote(src, dst, send_sems, recv_sems, k, to):
    return pltpu.make_async_remote_copy(src_ref=src, dst_ref=dst, send_sem=send_sems.at[k], recv_sem=recv_sems.at[k],
                                        device_id=to, device_id_type=MESH)


def _half_cols(ref, c):
    half = ref.shape[-1] // 2
    return pl.ds(pl.multiple_of(c * half, LANE), half)


class _PairedGather:
    def __init__(self, blocks):
        n = len(blocks)
        self.operands = list(blocks)
        self.out_shapes = [jax.ShapeDtypeStruct((N_CHIPS,) + b.shape, b.dtype) for b in blocks]
        self.scratch_shapes = [pltpu.SemaphoreType.DMA((6 * n,)), pltpu.SemaphoreType.DMA((6 * n,))]

    @staticmethod
    def _copies(srcs, dsts, sems):
        x, y, c, me, chips, chip_ids = _place()
        sends, landed, passes, passed = [], [], [], []
        for a, (src, dst) in enumerate(zip(srcs, dsts)):
            mine, other = _half_cols(src, c), _half_cols(src, 1 - c)
            for j, (chip, cid) in enumerate(zip(chips, chip_ids)):
                sends.append(_remote(src.at[:, mine], dst.at[me, :, mine], sems[0], sems[1], 6 * a + j, (*chip, c)))
                here = dst.at[cid, :, mine]
                landed.append(_remote(here, here, sems[0], sems[1], 6 * a + j, (x, y, 1 - c)))
                passes.append(_remote(here, here, sems[0], sems[1], 6 * a + 3 + j, (x, y, 1 - c)))
                there = dst.at[cid, :, other]
                passed.append(_remote(there, there, sems[0], sems[1], 6 * a + 3 + j, (x, y, 1 - c)))
        return sends, landed, passes, passed

    def start(self, srcs, dsts, sems):
        for cp in self._copies(srcs, dsts, sems)[0]:
            cp.start()

    def middle(self, srcs, dsts, sems):
        _, landed, passes, _ = self._copies(srcs, dsts, sems)
        for arrived, onward in zip(landed, passes):
            arrived.wait_recv()
            onward.start()

    def finish(self, srcs, dsts, sems):
        sends, _, passes, passed = self._copies(srcs, dsts, sems)
        for cp in passed:
            cp.wait_recv()
        for cp in sends + passes:
            cp.wait_send()


def _all_gather_weights(bigs, conv_b):
    bigs = tuple(bigs)
    n_big = len(bigs)

    def body(*refs):
        srcs, conv_src = refs[:n_big], refs[n_big]
        dsts, conv_dst = refs[n_big + 1:2 * n_big + 1], refs[2 * n_big + 1]
        send_sems, recv_sems, local_sems = refs[2 * n_big + 2:]
        x, y, c, me, chips, chip_ids = _place()
        sibling = (x, y, 1 - c)
        local = [pltpu.make_async_copy(src, dst.at[me], local_sems.at[a]) for a, (src, dst) in enumerate(zip(srcs, dsts))]
        local.append(pltpu.make_async_copy(conv_src, conv_dst.at[me], local_sems.at[n_big]))
        for cp in local:
            cp.start()
        sends = []
        for a, (src, dst) in enumerate(zip(srcs, dsts)):
            mine = _half_cols(src, c)
            for j, chip in enumerate(chips):
                sends.append(_remote(src.at[:, mine], dst.at[me, :, mine], send_sems, recv_sems, 6 * a + j, (*chip, c)))
        for j, chip in enumerate(chips):
            sends.append(_remote(conv_src, conv_dst.at[me], send_sems, recv_sems, 6 * n_big + j, (*chip, c)))
        for cp in sends:
            cp.start()
        passed = []
        for a, (src, dst) in enumerate(zip(srcs, dsts)):
            mine = _half_cols(src, c)
            for j, cid in enumerate(chip_ids):
                landed = dst.at[cid, :, mine]
                _remote(landed, landed, send_sems, recv_sems, 6 * a + j, sibling).wait_recv()
                cp = _remote(landed, landed, send_sems, recv_sems, 6 * a + 3 + j, sibling)
                cp.start()
                passed.append(cp)
        for a, (src, dst) in enumerate(zip(srcs, dsts)):
            other = _half_cols(src, 1 - c)
            for j, cid in enumerate(chip_ids):
                landed = dst.at[cid, :, other]
                _remote(landed, landed, send_sems, recv_sems, 6 * a + 3 + j, sibling).wait_recv()
        for j, cid in enumerate(chip_ids):
            _remote(conv_src, conv_dst.at[cid], send_sems, recv_sems, 6 * n_big + j, sibling).wait_recv()
        for cp in sends + passed:
            cp.wait_send()
        for cp in local:
            cp.wait()

    n_sem = 6 * n_big + 3
    return pl.pallas_call(
        body, name="all_gather_weights",
        out_shape=[jax.ShapeDtypeStruct((N_CHIPS,) + w.shape, w.dtype) for w in bigs + (conv_b,)],
        in_specs=[pl.BlockSpec(memory_space=pltpu.VMEM)] * (n_big + 1), out_specs=[HBM] * (n_big + 1),
        scratch_shapes=[pltpu.SemaphoreType.DMA((n_sem,)), pltpu.SemaphoreType.DMA((n_sem,)),
                        pltpu.SemaphoreType.DMA((n_big + 1,))],
        compiler_params=_params(),
    )(*bigs, conv_b)


def _pair_exchange(grads, name):
    n = len(grads)
    pieces = [_row_tile(g.shape[1]) for g in grads]

    def body(*refs):
        srcs, gots = refs[:n], refs[n:2 * n]
        stages, narrow = refs[2 * n:3 * n], refs[3 * n:4 * n]
        send_sems, recv_sems, load_sems = refs[4 * n:]
        x, y, c, _, _, _ = _place()
        sibling = (x, y, 1 - c)
        for a in range(n):
            slabs, rows, _ = gots[a].shape
            piece = pieces[a]
            per_slab = rows // piece
            theirs = _half_cols(srcs[a], 1 - c)
            loads, sends = [], []
            for i in range(slabs * per_slab):
                k, r, slot = i // per_slab, i % per_slab, i % 2
                part = pl.ds(r * piece, piece)
                loads.append(pltpu.make_async_copy(srcs[a].at[k, part, theirs], stages[a].at[slot], load_sems.at[2 * a + slot]))
                sends.append(pltpu.make_async_remote_copy(
                    src_ref=narrow[a].at[slot], dst_ref=gots[a].at[k, part, :],
                    send_sem=send_sems.at[2 * a + slot], recv_sem=recv_sems.at[a], device_id=sibling, device_id_type=MESH))
            loads[0].start()
            for i in range(len(loads)):
                loads[i].wait()
                narrow[a][i % 2] = stages[a][i % 2].astype(BF16)
                sends[i].start()
                if i + 1 < len(loads):
                    if i >= 1:
                        sends[i - 1].wait_send()
                    loads[i + 1].start()
            for cp in sends[-2:]:
                cp.wait_send()
        for a in range(n):
            pltpu.make_async_remote_copy(src_ref=gots[a], dst_ref=gots[a], send_sem=send_sems.at[2 * a],
                                         recv_sem=recv_sems.at[a], device_id=sibling, device_id_type=MESH).wait_recv()

    halves = [jax.ShapeDtypeStruct((g.shape[0], g.shape[1], g.shape[2] // 2), BF16) for g in grads]
    return pl.pallas_call(
        body, name=name, out_shape=halves, in_specs=[HBM] * n, out_specs=[HBM] * n,
        scratch_shapes=[pltpu.VMEM((2, piece, g.shape[2] // 2), dt) for dt in (F32, BF16) for piece, g in zip(pieces, grads)]
        + [pltpu.SemaphoreType.DMA((2 * n,)), pltpu.SemaphoreType.DMA((n,)), pltpu.SemaphoreType.DMA((2 * n,))],
        compiler_params=_params(),
    )(*grads)


class _ChipExchange:
    def __init__(self, halves):
        n = len(halves)
        self.operands = list(halves)
        self.out_shapes = [jax.ShapeDtypeStruct((N_CHIPS - 1,) + h.shape[1:], h.dtype) for h in halves]
        self.scratch_shapes = [pltpu.SemaphoreType.DMA((3 * n,)), pltpu.SemaphoreType.DMA((3 * n,))]

    @staticmethod
    def _copies(srcs, lands, sems):
        _, _, c, _, chips, chip_ids = _place()
        return [_remote(src.at[cid], land.at[j], sems[0], sems[1], 3 * a + j, (*chip, c))
                for a, (src, land) in enumerate(zip(srcs, lands)) for j, (chip, cid) in enumerate(zip(chips, chip_ids))]

    def start(self, srcs, lands, sems):
        for cp in self._copies(srcs, lands, sems):
            cp.start()

    def finish(self, srcs, lands, sems):
        copies = self._copies(srcs, lands, sems)
        for cp in copies:
            cp.wait_recv()
        for cp in copies:
            cp.wait_send()


def _pair_gather(halves, rows):
    n = len(halves)

    def body(*refs):
        srcs, fulls = refs[:n], refs[n:2 * n]
        send_sems, recv_sems, local_sems = refs[2 * n:]
        x, y, c, _, _, _ = _place()
        copies = []
        for a in range(n):
            mine, src = _half_cols(fulls[a], c), srcs[a].at[pl.ds(0, rows[a]), :]
            keep = pltpu.make_async_copy(src, fulls[a].at[:, mine], local_sems.at[a])
            keep.start()
            give = _remote(src, fulls[a].at[:, mine], send_sems, recv_sems, a, (x, y, 1 - c))
            give.start()
            copies += [keep, give]
        for a in range(n):
            other, src = _half_cols(fulls[a], 1 - c), srcs[a].at[pl.ds(0, rows[a]), :]
            copies[2 * a].wait()
            copies[2 * a + 1].wait_send()
            _remote(src, fulls[a].at[:, other], send_sems, recv_sems, a, (x, y, 1 - c)).wait_recv()

    return pl.pallas_call(
        body, name="grad_pair_gather",
        out_shape=[jax.ShapeDtypeStruct((r, 2 * h.shape[1]), h.dtype) for r, h in zip(rows, halves)],
        in_specs=[pl.BlockSpec(memory_space=pltpu.VMEM)] * n, out_specs=[HBM] * n,
        scratch_shapes=[pltpu.SemaphoreType.DMA((n,)), pltpu.SemaphoreType.DMA((n,)), pltpu.SemaphoreType.DMA((n,))],
    )(*halves)


def _all_reduce_small(p):
    n_dev = 8

    def body(p_ref, o_ref, land, send_sems, recv_sems):
        x, y, c = lax.axis_index("x"), lax.axis_index("y"), lax.axis_index("c")
        me = 4 * x + 2 * y + c
        land[me] = p_ref[...]
        sends = []
        for k in range(1, n_dev):
            fx, fy, fc = (k >> 2) & 1, (k >> 1) & 1, k & 1
            to = (x ^ fx, y ^ fy, c ^ fc)
            cp = _remote(p_ref, land.at[me], send_sems, recv_sems, k - 1, to)
            cp.start()
            sends.append(cp)
        for k in range(1, n_dev):
            _remote(p_ref, land.at[me ^ k], send_sems, recv_sems, k - 1, (x, y, c)).wait_recv()
        total = land[0]
        for d in range(1, n_dev):
            total = total + land[d]
        o_ref[...] = total
        for cp in sends:
            cp.wait_send()

    vm = pl.BlockSpec(memory_space=pltpu.VMEM)
    return pl.pallas_call(
        body, name="all_reduce_small", out_shape=jax.ShapeDtypeStruct(p.shape, p.dtype), in_specs=[vm], out_specs=vm,
        scratch_shapes=[pltpu.VMEM((n_dev,) + p.shape, p.dtype), pltpu.SemaphoreType.DMA((n_dev - 1,)),
                        pltpu.SemaphoreType.DMA((n_dev - 1,))],
    )(p)


def _row_tile(rows):
    fits = [t for t in range(8, min(rows, PIECE_ROWS_CAP) + 1, 8) if rows % t == 0]
    return max(fits) if fits else rows


def _pair_sum(full, got, core, name):
    n, r, c = got.shape
    tr = _row_tile(r)

    def body(core_ref, a_ref, b_ref, o_ref):
        o_ref[...] = (a_ref[...] + b_ref[...].astype(F32)).astype(BF16)

    blk = pl.BlockSpec((None, tr, c), lambda i, j, core_ref: (i, j, 0))
    grid_spec = pltpu.PrefetchScalarGridSpec(
        num_scalar_prefetch=1, grid=(n, r // tr),
        in_specs=[pl.BlockSpec((None, tr, c), lambda i, j, core_ref: (i, j, core_ref[0])), blk], out_specs=blk)
    return pl.pallas_call(body, name=name, grid_spec=grid_spec, out_shape=jax.ShapeDtypeStruct(got.shape, BF16),
                          compiler_params=_params(("parallel", "parallel")))(core, full, got)


def _chip_sum(full, got, land, place, name):
    n, r, c = land.shape
    tr = _row_tile(r)

    def body(place_ref, a_ref, b_ref, l_ref, o_ref):
        total = a_ref[...] + b_ref[...].astype(F32)
        for j in range(n):
            total = total + l_ref[j].astype(F32)
        o_ref[...] = total

    grid_spec = pltpu.PrefetchScalarGridSpec(
        num_scalar_prefetch=1, grid=(r // tr,),
        in_specs=[pl.BlockSpec((None, tr, c), lambda i, p: (p[0], i, p[1])),
                  pl.BlockSpec((None, tr, c), lambda i, p: (p[0], i, 0)),
                  pl.BlockSpec((n, tr, c), lambda i, p: (0, i, 0))],
        out_specs=pl.BlockSpec((tr, c), lambda i, p: (i, 0)))
    return pl.pallas_call(body, name=name, grid_spec=grid_spec, out_shape=jax.ShapeDtypeStruct((r, c), F32),
                          compiler_params=_params(("parallel",)))(place, full, got, land)


def _adamw(w, g, m, v, name, echo=False):
    r, c = w.shape
    tr = _row_tile(r)
    tc = 1024 if c % 1024 == 0 else c

    def body(w_ref, g_ref, m_ref, v_ref, d_ref, mo_ref, vo_ref, *g_out):
        g_ = g_ref[...]
        for o in g_out:
            o[...] = g_
        m2 = ADAM_B1 * m_ref[...] + (1.0 - ADAM_B1) * g_
        v2 = ADAM_B2 * v_ref[...] + (1.0 - ADAM_B2) * jnp.square(g_)
        m_hat = m2 / (1.0 - ADAM_B1 ** ADAM_STEP)
        v_hat = v2 / (1.0 - ADAM_B2 ** ADAM_STEP)
        d_ref[...] = -ADAM_LR * (m_hat / (jnp.sqrt(v_hat) + ADAM_EPS) + ADAM_WD * w_ref[...])
        mo_ref[...] = m2
        vo_ref[...] = v2

    blk = pl.BlockSpec((tr, tc), lambda i, j: (i, j))
    n_out = 4 if echo else 3
    return pl.pallas_call(body, name=name, grid=(r // tr, c // tc), in_specs=[blk] * 4, out_specs=[blk] * n_out,
                          out_shape=[jax.ShapeDtypeStruct(w.shape, F32)] * n_out,
                          compiler_params=_params(("parallel", "parallel")))(w, g, m, v)


SMALL_NAMES = ("norm_w", "mem_norm_w", "o_norm_a", "q_norm_c", "k_norm_c", "q_norm_b", "k_norm_b",
               "a_log_fwd", "a_log_bwd", "dt_bias_fwd", "dt_bias_bwd", "sink_b")
SMALL_SIZES = (2048, 2048, 128, 128, 128, 64, 64, 8, 8, 8, 8, 8)
SMALL_LOSS = sum(SMALL_SIZES)
SMALL_CONV = 5120
SMALL_TOTAL = SMALL_CONV + CONV_K * 3 * A_WIDTH
SMALL_ROWS = SMALL_TOTAL // LANE


def _pack_small(parts, extra=None, conv=None):
    vec = [parts[n].reshape(-1) for n in SMALL_NAMES]
    vec.append(jnp.zeros((1,), F32) if extra is None else extra.reshape(1))
    vec.append(jnp.zeros((SMALL_CONV - SMALL_LOSS - 1,), F32))
    vec.append(jnp.zeros((SMALL_TOTAL - SMALL_CONV,), F32) if conv is None else conv.reshape(-1))
    return jnp.concatenate(vec).reshape(SMALL_ROWS, LANE)


def _unpack_small(packed):
    flat = packed.reshape(-1)
    out, off = {}, 0
    for n, size in zip(SMALL_NAMES, SMALL_SIZES):
        out[n] = flat[off:off + size].reshape(1, size)
        off += size
    return out


WEIGHT_ORDER = ("norm_w", "w_in", "conv_w_a", "a_log_fwd", "a_log_bwd", "dt_bias_fwd", "dt_bias_bwd", "o_norm_a",
                "q_norm_b", "k_norm_b", "sink_b", "mem_norm_w", "w_mem_kv", "q_norm_c", "k_norm_c", "w_out")


def kernel(x, mem, norm_w, w_in, conv_w_a, a_log_fwd, a_log_bwd, dt_bias_fwd, dt_bias_bwd, o_norm_a, q_norm_b, k_norm_b, sink_b, mem_norm_w, w_mem_kv, q_norm_c, k_norm_c, w_out, loss_target, m_norm_w, m_w_in, m_conv_w_a, m_a_log_fwd, m_a_log_bwd, m_dt_bias_fwd, m_dt_bias_bwd, m_o_norm_a, m_q_norm_b, m_k_norm_b, m_sink_b, m_mem_norm_w, m_w_mem_kv, m_q_norm_c, m_k_norm_c, m_w_out, v_norm_w, v_w_in, v_conv_w_a, v_a_log_fwd, v_a_log_bwd, v_dt_bias_fwd, v_dt_bias_bwd, v_o_norm_a, v_q_norm_b, v_k_norm_b, v_sink_b, v_mem_norm_w, v_w_mem_kv, v_q_norm_c, v_k_norm_c, v_w_out):
    weights = dict(norm_w=norm_w, w_in=w_in, conv_w_a=conv_w_a, a_log_fwd=a_log_fwd, a_log_bwd=a_log_bwd,
                   dt_bias_fwd=dt_bias_fwd, dt_bias_bwd=dt_bias_bwd, o_norm_a=o_norm_a, q_norm_b=q_norm_b,
                   k_norm_b=k_norm_b, sink_b=sink_b, mem_norm_w=mem_norm_w, w_mem_kv=w_mem_kv, q_norm_c=q_norm_c,
                   k_norm_c=k_norm_c, w_out=w_out)
    mom1 = dict(norm_w=m_norm_w, w_in=m_w_in, conv_w_a=m_conv_w_a, a_log_fwd=m_a_log_fwd, a_log_bwd=m_a_log_bwd,
                dt_bias_fwd=m_dt_bias_fwd, dt_bias_bwd=m_dt_bias_bwd, o_norm_a=m_o_norm_a, q_norm_b=m_q_norm_b,
                k_norm_b=m_k_norm_b, sink_b=m_sink_b, mem_norm_w=m_mem_norm_w, w_mem_kv=m_w_mem_kv,
                q_norm_c=m_q_norm_c, k_norm_c=m_k_norm_c, w_out=m_w_out)
    mom2 = dict(norm_w=v_norm_w, w_in=v_w_in, conv_w_a=v_conv_w_a, a_log_fwd=v_a_log_fwd, a_log_bwd=v_a_log_bwd,
                dt_bias_fwd=v_dt_bias_fwd, dt_bias_bwd=v_dt_bias_bwd, o_norm_a=v_o_norm_a, q_norm_b=v_q_norm_b,
                k_norm_b=v_k_norm_b, sink_b=v_sink_b, mem_norm_w=v_mem_norm_w, w_mem_kv=v_w_mem_kv,
                q_norm_c=v_q_norm_c, k_norm_c=v_k_norm_c, w_out=v_w_out)
    chip = 2 * lax.axis_index("x") + lax.axis_index("y")

    own_in = jnp.pad(jnp.transpose(w_in[0]).astype(BF16), ((0, W_IN_PAD - W_IN_BLOCK), (0, 0)))
    w_in4, conv4 = _all_gather_weights([own_in], conv_w_a[0])
    w_perm_t = _permute_blocks(w_in4)
    w_blocks_t = w_in4.reshape(N_CHIPS * W_IN_PAD, D_MODEL)
    conv_full = jnp.transpose(conv4, (1, 0, 2)).reshape(CONV_K, 3 * A_WIDTH)
    own_out, own_kv = w_out[0].astype(BF16), w_mem_kv[0].astype(BF16)

    def assemble(w_out4, w_kv4):
        w_out4 = lax.dynamic_update_index_in_dim(w_out4, own_out, chip, 0)
        w_kv4 = lax.dynamic_update_index_in_dim(w_kv4, own_kv, chip, 0)
        return w_out4.reshape(D_MODEL, D_MODEL), w_kv4.reshape(D_MODEL, 2 * C_HEADS * C_DIM)

    gather = (_PairedGather([own_out, own_kv]), assemble)
    pa = jnp.concatenate([_pad_row(a_log_fwd), _pad_row(a_log_bwd), _pad_row(dt_bias_fwd), _pad_row(dt_bias_bwd),
                          _pad_row(o_norm_a), jnp.zeros((3, LANE), F32)], axis=0)
    pb = jnp.concatenate([_pad_row(q_norm_b), _pad_row(k_norm_b), _pad_row(sink_b), jnp.zeros((5, LANE), F32)], axis=0)
    pc = jnp.concatenate([_pad_row(q_norm_c), _pad_row(k_norm_c), jnp.zeros((6, LANE), F32)], axis=0)

    full, got = {}, {}
    core = lax.axis_index("c").astype(jnp.int32).reshape(1)

    def pair_round(tag, blocks):
        names = [tag + "_%d" % i for i in range(len(blocks))]
        full.update(zip(names, blocks))
        got.update(zip(names, _pair_exchange(blocks, "grad_pair_exchange_" + tag)))
        return _ChipExchange([_pair_sum(full[n], got[n], core, "grad_pair_sum_" + n) for n in names])

    def early(g_w_out, g_w_kv):
        return pair_round("early", [g_w_out.reshape(N_CHIPS, D_MODEL // N_CHIPS, D_MODEL),
                                    g_w_kv.reshape(N_CHIPS, D_MODEL // N_CHIPS, 2 * C_HEADS * C_DIM)])

    def late(g_w_blocks_t):
        return pair_round("late", [g_w_blocks_t.reshape(N_CHIPS, W_IN_PAD, D_MODEL)])

    r = _local_step(x[0], mem[0], loss_target[0], norm_w, w_perm_t, w_blocks_t, conv_full, pa, pb, pc, mem_norm_w, None, None,
                    gather, (early, late))
    place = jnp.stack([chip, lax.axis_index("c")]).astype(jnp.int32)
    reduced = [_chip_sum(full[n], got[n], l, place, "grad_chip_sum_" + n)
               for n, l in zip(("late_0", "early_0", "early_1"), r["landed"])]
    g_w_in_t, g_w_out, g_w_kv = _pair_gather(reduced, [W_IN_BLOCK, D_MODEL // N_CHIPS, D_MODEL // N_CHIPS])

    d_pa, d_pb, d_pc = r["d_pa"], r["d_pb"], r["d_pc"]
    small_g = dict(norm_w=r["g_norm"], mem_norm_w=r["g_mem_norm"], o_norm_a=d_pa[4], q_norm_c=d_pc[0], k_norm_c=d_pc[1],
                   q_norm_b=d_pb[0, :B_DIM], k_norm_b=d_pb[1, :B_DIM], a_log_fwd=d_pa[0, :A_HEADS],
                   a_log_bwd=d_pa[1, :A_HEADS], dt_bias_fwd=d_pa[2, :A_HEADS], dt_bias_bwd=d_pa[3, :A_HEADS],
                   sink_b=d_pb[2, :B_HEADS])
    packed = _all_reduce_small(_pack_small(small_g, jnp.sum(r["loss_parts"][:, 0, 0]), r["g_conv"]))
    flat = packed.reshape(-1)
    loss = flat[SMALL_LOSS]
    conv_sum = flat[SMALL_CONV:].reshape(CONV_K, 3 * A_WIDTH)
    conv_cols = 3 * A_WIDTH // N_CHIPS
    g_conv = lax.dynamic_slice(conv_sum, (0, chip * conv_cols), (CONV_K, conv_cols))

    grads = _unpack_small(packed)
    grads["conv_w_a"] = g_conv
    delta, new_m, new_v = {}, {}, {}
    delta["conv_w_a"], new_m["conv_w_a"], new_v["conv_w_a"] = _adamw(conv_w_a[0], g_conv, m_conv_w_a[0], v_conv_w_a[0],
                                                                     "adamw_conv_w_a")
    for n, g in (("w_mem_kv", g_w_kv), ("w_out", g_w_out)):
        delta[n], new_m[n], new_v[n], grads[n] = _adamw(weights[n][0], g, mom1[n][0], mom2[n][0], "adamw_" + n, echo=True)
    stepped = _adamw(jnp.transpose(w_in[0]), g_w_in_t, jnp.transpose(m_w_in[0]), jnp.transpose(v_w_in[0]), "adamw_w_in",
                     echo=True)
    delta["w_in"], new_m["w_in"], new_v["w_in"], grads["w_in"] = (jnp.transpose(t) for t in stepped)
    d_s, m_s, v_s = _adamw(_pack_small(weights), packed, _pack_small(mom1), _pack_small(mom2), "adamw_small")
    d_s, m_s, v_s = _unpack_small(d_s), _unpack_small(m_s), _unpack_small(v_s)
    for n in SMALL_NAMES:
        delta[n], new_m[n], new_v[n] = d_s[n], m_s[n], v_s[n]

    def shaped(tree):
        return [tree[n].reshape(weights[n].shape) for n in WEIGHT_ORDER]

    return (loss, r["g_x"].reshape(x.shape), *shaped(grads), *shaped(delta), *shaped(new_m), *shaped(new_v))
```

```python
import jax
import jax.numpy as jnp
from jax import lax
from jax.experimental import pallas as pl
from jax.experimental.pallas import tpu as pltpu

F32 = jnp.float32
BF16 = jnp.bfloat16
HI = lax.Precision.HIGHEST
MESH = pl.DeviceIdType.MESH

D_MODEL = 2048
A_WIDTH = 1024
A_HEADS = 8
A_DIM = 128
CONV_K = 5
CHUNK = 64
B_HEADS = 8
B_KV = 2
B_DIM = 64
WINDOW = 128
C_HEADS = 4
C_DIM = 128
MEM_LEN = 256
ROPE_THETA = 10000.0
EPS = 1e-6
IN_WIDTH = 6432
N_CHIPS = 4
W_IN_BLOCK = IN_WIDTH // N_CHIPS
W_IN_PAD = 1664

LANE = 128
P_QA, P_KA, P_VA, P_ZA = 0, 1024, 2048, 3072
P_QB, P_ZB, P_QC, P_ZC = 4096, 4608, 5120, 5632
P_KB, P_VB, P_GT = 6144, 6272, 6400
P_WIDTH = 6656
O_GT, O_QB, O_KB, O_VB, O_ZB, O_QC, O_ZC = 4096, 4128, 4640, 4768, 4896, 5408, 5920

ADAM_LR, ADAM_B1, ADAM_B2, ADAM_EPS, ADAM_WD, ADAM_STEP = 0.001, 0.9, 0.999, 1e-08, 0.01, 10

VMEM_LIMIT = 56 * 1024 * 1024


def _params(sem=None):
    return pltpu.CompilerParams(dimension_semantics=sem, vmem_limit_bytes=VMEM_LIMIT)


def _dot(a, b, dims=(((1,), (0,)), ((), ())), precision=HI):
    return lax.dot_general(a, b, dims, precision=precision, preferred_element_type=F32)


_NN = (((1,), (0,)), ((), ()))
_NT = (((1,), (1,)), ((), ()))
_TN = (((0,), (0,)), ((), ()))


def _bdot(a, b, dims):
    return lax.dot_general(a.astype(BF16), b.astype(BF16), dims, preferred_element_type=F32)


@jax.custom_vjp
def _mm(a, b):
    return _bdot(a, b, _NN)


_mm.defvjp(lambda a, b: (_bdot(a, b, _NN), (a, b)),
           lambda res, ct: (_bdot(ct, res[1], _NT), _bdot(res[0], ct, _TN)))


@jax.custom_vjp
def _mm_nt(a, b):
    return _bdot(a, b, _NT)


_mm_nt.defvjp(lambda a, b: (_bdot(a, b, _NT), (a, b)),
              lambda res, ct: (_bdot(ct, res[1], _NN), _bdot(ct, res[0], _TN)))


@jax.custom_vjp
def _mm_tn(a, b):
    return _bdot(a, b, _TN)


_mm_tn.defvjp(lambda a, b: (_bdot(a, b, _TN), (a, b)),
              lambda res, ct: (_bdot(res[1], ct, _NT), _bdot(res[0], ct, _NN)))


def _rms(t, w):
    return t * lax.rsqrt(jnp.mean(t * t, axis=-1, keepdims=True) + EPS) * w


def _l2(t):
    return t * lax.rsqrt(jnp.sum(t * t, axis=-1, keepdims=True) + EPS)


def _silu(t):
    return t * jax.nn.sigmoid(t)


def _softplus(t):
    return jnp.maximum(t, 0.0) + jnp.log1p(jnp.exp(-jnp.abs(t)))


def _matmul(a, b, mode, out_dtype, name, tm=512, tn=512, tk=512):
    (m, k) = a.shape[::-1] if mode == "tn" else a.shape
    n = b.shape[0] if mode == "nt" else b.shape[1]
    tm, tn, tk = min(tm, m), min(tn, n), min(tk, k)
    assert m % tm == 0 and n % tn == 0 and k % tk == 0, (m, n, k, tm, tn, tk)
    if mode == "nn":
        a_spec = pl.BlockSpec((tm, tk), lambda i, j, kk: (i, kk))
        b_spec = pl.BlockSpec((tk, tn), lambda i, j, kk: (kk, j))
        dims = (((1,), (0,)), ((), ()))
    elif mode == "nt":
        a_spec = pl.BlockSpec((tm, tk), lambda i, j, kk: (i, kk))
        b_spec = pl.BlockSpec((tn, tk), lambda i, j, kk: (j, kk))
        dims = (((1,), (1,)), ((), ()))
    else:
        a_spec = pl.BlockSpec((tk, tm), lambda i, j, kk: (kk, i))
        b_spec = pl.BlockSpec((tk, tn), lambda i, j, kk: (kk, j))
        dims = (((0,), (0,)), ((), ()))
    nk = k // tk

    def body(a_ref, b_ref, o_ref, *scratch):
        if nk == 1:
            o_ref[...] = _bdot(a_ref[...], b_ref[...], dims).astype(out_dtype)
        else:
            acc_ref, kk = scratch[0], pl.program_id(2)

            @pl.when(kk == 0)
            def _():
                acc_ref[...] = jnp.zeros_like(acc_ref)

            acc_ref[...] += _bdot(a_ref[...], b_ref[...], dims)

            @pl.when(kk == nk - 1)
            def _():
                o_ref[...] = acc_ref[...].astype(out_dtype)

    return pl.pallas_call(
        body, name=name, grid=(m // tm, n // tn, nk),
        in_specs=[a_spec, b_spec], out_specs=pl.BlockSpec((tm, tn), lambda i, j, kk: (i, j)),
        out_shape=jax.ShapeDtypeStruct((m, n), out_dtype),
        scratch_shapes=[] if nk == 1 else [pltpu.VMEM((tm, tn), F32)],
        compiler_params=_params(("parallel", "parallel", "arbitrary")),
    )(a, b)


def _rms_fwd(x, w, tr=256):
    s, d = x.shape

    def body(x_ref, w_ref, o_ref):
        o_ref[...] = _rms(x_ref[...], w_ref[...]).astype(BF16)

    return pl.pallas_call(
        body, name="rms_fwd", grid=(s // tr,),
        in_specs=[pl.BlockSpec((tr, d), lambda i: (i, 0)), pl.BlockSpec((1, d), lambda i: (0, 0))],
        out_specs=pl.BlockSpec((tr, d), lambda i: (i, 0)),
        out_shape=jax.ShapeDtypeStruct((s, d), BF16), compiler_params=_params(("parallel",)),
    )(x, w)


def _input_grad(d_proj, w_t, x, w, dy, ride=None, tm=512, tk=512):
    s, k = d_proj.shape
    d = w_t.shape[1]
    tm = min(tm, s)
    nk = k // tk
    grid = (s // tm, nk)
    n_in = len(ride.operands) if ride else 0
    n_out = len(ride.out_shapes) if ride else 0

    def body(*refs):
        a_ref, b_ref, x_ref, w_ref, dy_ref = refs[:5]
        gx_ref, gw_ref = refs[5 + n_in:7 + n_in]
        acc_ref = refs[7 + n_in + n_out]
        riders = (refs[5:5 + n_in], refs[7 + n_in:7 + n_in + n_out], refs[8 + n_in + n_out:])
        kk = pl.program_id(1)
        step = pl.program_id(0) * nk + kk
        if ride:
            pl.when(step == 0)(lambda: ride.start(*riders))

        @pl.when(step == 0)
        def _():
            gw_ref[...] = jnp.zeros_like(gw_ref)

        @pl.when(kk == 0)
        def _():
            acc_ref[...] = jnp.zeros_like(acc_ref)

        acc_ref[...] += _bdot(a_ref[...], b_ref[...], _NN)

        @pl.when(kk == nk - 1)
        def _():
            _, vjp = jax.vjp(_rms, x_ref[...], w_ref[...])
            dx, dw = vjp(acc_ref[...])
            gx_ref[...] = dy_ref[...] + dx
            gw_ref[...] += dw

        if ride:
            pl.when(step == grid[0] * nk - 1)(lambda: ride.finish(*riders))

    row = pl.BlockSpec((tm, d), lambda i, kk: (i, 0))
    vec = pl.BlockSpec((1, d), lambda i, kk: (0, 0))
    return pl.pallas_call(
        body, name="input_grad", grid=grid,
        in_specs=[pl.BlockSpec((tm, tk), lambda i, kk: (i, kk)), pl.BlockSpec((tk, d), lambda i, kk: (kk, 0)), row, vec, row]
        + [HBM] * n_in,
        out_specs=[row, vec] + [HBM] * n_out,
        out_shape=[jax.ShapeDtypeStruct((s, d), F32), jax.ShapeDtypeStruct((1, d), F32)]
        + (list(ride.out_shapes) if ride else []),
        scratch_shapes=[pltpu.VMEM((tm, d), F32)] + (list(ride.scratch_shapes) if ride else []),
        compiler_params=_params(("arbitrary", "arbitrary")),
    )(d_proj, w_t, x, w, dy, *(ride.operands if ride else []))


def _out_loss(mixed, w_out, x, target, tm=1024, tn=512):
    s, d = x.shape
    tm = min(tm, s)
    ni, nj = s // tm, d // tn

    def body(m_ref, w_ref, x_ref, t_ref, dy_ref, dyb_ref, l_ref):
        err = x_ref[...] + _bdot(m_ref[...], w_ref[...], _NN) - t_ref[...]
        dy = err * (1.0 / d)
        dy_ref[...] = dy
        dyb_ref[...] = dy.astype(BF16)
        l_ref[...] = jnp.full(l_ref.shape, 0.5 * jnp.sum(jnp.sum(err * err, axis=1, keepdims=True) * (1.0 / d)), F32)

    tile = pl.BlockSpec((tm, tn), lambda i, j: (i, j))
    return pl.pallas_call(
        body, name="out_loss", grid=(ni, nj),
        in_specs=[pl.BlockSpec((tm, mixed.shape[1]), lambda i, j: (i, 0)),
                  pl.BlockSpec((mixed.shape[1], tn), lambda i, j: (0, j)), tile, tile],
        out_specs=[tile, tile, pl.BlockSpec((1, 8, LANE), lambda i, j: (i * nj + j, 0, 0))],
        out_shape=[jax.ShapeDtypeStruct((s, d), F32), jax.ShapeDtypeStruct((s, d), BF16),
                   jax.ShapeDtypeStruct((ni * nj, 8, LANE), F32)],
        compiler_params=_params(("parallel", "parallel")),
    )(mixed, w_out, x, target)


def _shift_rows(t, s):
    if s == 0:
        return t
    n = t.shape[0]
    rolled = pltpu.roll(t, (-s) % n, axis=0)
    idx = lax.broadcasted_iota(jnp.int32, t.shape, 0) + s
    return jnp.where((idx >= 0) & (idx < n), rolled, 0.0)


CONV_FWD_COLS = 512
CONV_BWD_COLS = 128


def _conv_fwd(proj, conv_w):
    s = proj.shape[0]
    cols, split = CONV_FWD_COLS, A_WIDTH // CONV_FWD_COLS
    nblk = 3 * A_WIDTH // cols

    def body(x_ref, w_ref, o_ref):
        x = x_ref[...]
        acc = jnp.zeros_like(x)
        for j in range(CONV_K):
            acc = acc + w_ref[j:j + 1, :] * _shift_rows(x, j - CONV_K // 2)
        o_ref[...] = acc

    return pl.pallas_call(
        body, name="conv_fwd", grid=(nblk,),
        in_specs=[pl.BlockSpec((s, cols), lambda i: (0, i)), pl.BlockSpec((CONV_K, cols), lambda i: (0, i))],
        out_specs=pl.BlockSpec((None, s, cols), lambda i: (i // split, 0, i % split)),
        out_shape=jax.ShapeDtypeStruct((3, s, A_WIDTH), F32), compiler_params=_params(("parallel",)),
    )(proj, conv_w)


def _conv_bwd(proj, conv_w, d_c):
    s = proj.shape[0]
    cols, split = CONV_BWD_COLS, A_WIDTH // CONV_BWD_COLS
    nblk = 3 * A_WIDTH // cols

    def body(x_ref, w_ref, g_ref, dx_ref, dw_ref):
        x, g = x_ref[...], g_ref[...]
        acc = jnp.zeros_like(x)
        for j in range(CONV_K):
            off = j - CONV_K // 2
            acc = acc + w_ref[j:j + 1, :] * _shift_rows(g, -off)
            dw_ref[j:j + 1, :] = jnp.sum(_shift_rows(x, off) * g, axis=0, keepdims=True)
        dx_ref[...] = acc.astype(BF16)

    col = pl.BlockSpec((s, cols), lambda i: (0, i))
    wsp = pl.BlockSpec((CONV_K, cols), lambda i: (0, i))
    dsp = pl.BlockSpec((None, s, cols), lambda i: (i // split, 0, i % split))
    return pl.pallas_call(
        body, name="conv_bwd", grid=(nblk,), in_specs=[col, wsp, dsp], out_specs=[col, wsp],
        out_shape=[jax.ShapeDtypeStruct((s, 3 * A_WIDTH), BF16), jax.ShapeDtypeStruct((CONV_K, 3 * A_WIDTH), F32)],
        compiler_params=_params(("parallel",)),
    )(proj, conv_w, d_c)


A_FWD_HEADS = 4
A_BWD_HEADS = 4


def _neumann_inverse(a):
    c = a.shape[-1]
    eye = (lax.broadcasted_iota(jnp.int32, (c, c), 0) == lax.broadcasted_iota(jnp.int32, (c, c), 1)).astype(F32)
    tinv = eye + a
    p = a
    for _ in range(5):
        p = _mm(p, p)
        tinv = tinv + _mm(tinv, p)
    return tinv


@jax.custom_vjp
def _unit_inverse(a):
    return _neumann_inverse(a)


def _unit_inverse_fwd(a):
    tinv = _neumann_inverse(a)
    return tinv, tinv


def _unit_inverse_bwd(tinv, ct):
    return (_bdot(_bdot(tinv, ct, _TN), tinv, _NT),)


_unit_inverse.defvjp(_unit_inverse_fwd, _unit_inverse_bwd)


@jax.custom_vjp
def _known_inverse(a, tinv):
    return tinv


_known_inverse.defvjp(lambda a, tinv: (tinv, tinv),
                      lambda tinv, ct: (_unit_inverse_bwd(tinv, ct)[0], jnp.zeros_like(tinv)))


def _a_chain(st, cq, ck, cv, alpha, beta_raw, a_log, dt_b, incl, strict, last, kept=None):
    c = CHUNK
    gb = -jnp.exp(a_log) * _softplus(alpha + dt_b)
    bb = jax.nn.sigmoid(beta_raw)
    q = _l2(_silu(cq)) * (A_DIM ** -0.5)
    k = _l2(_silu(ck))
    v = _silu(cv)

    gc = _dot(incl, jnp.broadcast_to(gb, (c, LANE)))
    tot = jnp.sum(gc * last, axis=0, keepdims=True)
    m1 = gc[:, :c]
    decay = incl * jnp.exp(incl * (m1 - m1.T))
    kb = k * bb
    vb = v * bb
    a = -(strict * decay * _mm_nt(kb, k))
    tinv = _unit_inverse(a) if kept is None else _known_inverse(a, kept)
    eg = jnp.exp(gc)
    u = _mm(tinv, vb)
    w = _mm(tinv, kb * eg)
    qk = _mm_nt(q, k) * decay
    v_new = u - _mm(w, st)
    o = _mm(q * eg, st) + _mm(qk, v_new)
    st_new = st * jnp.exp(tot) + _mm_tn(k * jnp.exp(tot - gc), v_new)
    return st_new, o, tinv


def _a_step(sts, cq, ck, cv, gts, pa, h0, kept=None):
    c = CHUNK
    lane = lax.broadcasted_iota(jnp.int32, (1, LANE), 1)
    ii = lax.broadcasted_iota(jnp.int32, (c, c), 0)
    jj = lax.broadcasted_iota(jnp.int32, (c, c), 1)
    row = lax.broadcasted_iota(jnp.int32, (c, 1), 0)

    def pick(t, col):
        return jnp.sum(jnp.where(lane == col, t, 0.0), axis=1, keepdims=True)

    alpha, beta_raw, a_log, dt_b, incl, strict, last = [], [], [], [], [], [], []
    for b in range(sts.shape[0]):
        h, rev = h0 + b // 2, b % 2
        alpha.append(pick(gts[b], h + 8 * rev))
        beta_raw.append(pick(gts[b], h + 16 + 8 * rev))
        a_log.append(pick(pa[rev:rev + 1, :], h))
        dt_b.append(pick(pa[2 + rev:3 + rev, :], h))
        incl.append(((ii <= jj) if rev else (ii >= jj)).astype(F32))
        strict.append(((ii < jj) if rev else (ii > jj)).astype(F32))
        last.append((row == (0 if rev else c - 1)).astype(F32))
    stack = lambda ts: jnp.concatenate([t[None] for t in ts], axis=0)
    return jax.vmap(_a_chain)(sts, cq, ck, cv, stack(alpha), stack(beta_raw), stack(a_log), stack(dt_b),
                              stack(incl), stack(strict), stack(last), kept)


def _a_final(o, za, pa):
    outs = []
    for j in range(o.shape[1] // A_DIM):
        ln = slice(j * A_DIM, (j + 1) * A_DIM)
        outs.append(_rms(o[:, ln], pa[4:5, :]) * _silu(za[:, ln]))
    return jnp.concatenate(outs, axis=1)


def _a_tiles(n, nchunk, heads):
    tiles = []
    for b in range(2 * heads):
        i = (nchunk - 1 - n) if b % 2 else n
        tiles.append((i, pl.ds(pl.multiple_of(i * CHUNK, CHUNK), CHUNK), slice((b // 2) * A_DIM, (b // 2 + 1) * A_DIM)))
    return tiles


def _a_load(tiles, c_ref, gt_ref):
    cq, ck, cv = (jnp.stack([c_ref[r, sl, ln] for _, sl, ln in tiles], axis=0) for r in range(3))
    return cq, ck, cv, jnp.stack([gt_ref[sl, :] for _, sl, _ in tiles], axis=0)


SCAN_STEPS_PER_TRIP = 4


def _loop_unrolled(n, step, init):
    per = SCAN_STEPS_PER_TRIP
    assert n % per == 0

    def trip(m, carry):
        for j in range(per):
            carry = step(per * m + j, carry, j % 2)
        return carry

    return lax.fori_loop(0, n // per, trip, init)


def _a_scan(h0, heads, nchunk, c_ref, gt_ref, pa, of_ref, ob_ref, s_ref, t_ref):
    def step(n, sts, parity):
        tiles = _a_tiles(n, nchunk, heads)
        sts_new, o, tinv = _a_step(sts, *_a_load(tiles, c_ref, gt_ref), pa, h0)
        for b, (i, sl, ln) in enumerate(tiles):
            s_ref[b, i] = sts[b]
            t_ref[b, i] = tinv[b]
            (ob_ref if b % 2 else of_ref)[sl, ln] = o[b]
        return sts_new

    _loop_unrolled(nchunk, step, jnp.zeros((2 * heads, A_DIM, A_DIM), F32))


def _a_specs(s, heads):
    wide = heads * A_DIM
    once = pl.Buffered(1)
    trio = pl.BlockSpec((3, s, wide), lambda g: (0, 0, g), pipeline_mode=once)
    gates = pl.BlockSpec((s, LANE), lambda g: (0, P_GT // LANE))
    small = pl.BlockSpec((8, LANE), lambda g: (0, 0))

    def cols(base):
        return pl.BlockSpec((s, wide), lambda g: (0, base // wide + g), pipeline_mode=once)

    state = pl.BlockSpec((2 * heads, s // CHUNK, A_DIM, A_DIM), lambda g: (g, 0, 0, 0), pipeline_mode=once)
    kept = pl.BlockSpec((2 * heads, s // CHUNK, CHUNK, CHUNK), lambda g: (g, 0, 0, 0), pipeline_mode=once)
    return wide, trio, gates, small, cols, state, kept


def _delta_fwd(cqkv, proj, pa, ride=None):
    s = cqkv.shape[1]
    nchunk = s // CHUNK
    heads = A_FWD_HEADS
    steps = A_HEADS // heads
    wide, trio, gates, small, cols, state, kept = _a_specs(s, heads)
    n_in = len(ride.operands) if ride else 0
    n_out = len(ride.out_shapes) if ride else 0

    def body(*refs):
        c_ref, gt_ref, za_ref, pa_ref = refs[:4]
        out_ref, o_ref, s_ref, t_ref = refs[4 + n_in:8 + n_in]
        ob_ref = refs[8 + n_in + n_out]
        riders = (refs[4:4 + n_in], refs[8 + n_in:8 + n_in + n_out], refs[9 + n_in + n_out:])
        g = pl.program_id(0)
        if ride:
            pl.when(g == 0)(lambda: ride.start(*riders))
            pl.when(g == steps - 1)(lambda: ride.middle(*riders))
        h0 = g * heads
        pa_v = pa_ref[...]
        _a_scan(h0, heads, nchunk, c_ref, gt_ref, pa_v, o_ref, ob_ref, s_ref, t_ref)
        o_ref[...] += ob_ref[...]
        out_ref[...] = _a_final(o_ref[...], za_ref[...], pa_v).astype(BF16)
        if ride:
            pl.when(g == steps - 1)(lambda: ride.finish(*riders))

    assert steps > 1
    return pl.pallas_call(
        body, name="delta_fwd", grid=(steps,),
        in_specs=[trio, gates, cols(P_ZA), small] + [HBM] * n_in,
        out_specs=[cols(0), cols(0), state, kept] + [HBM] * n_out,
        out_shape=[jax.ShapeDtypeStruct((s, D_MODEL), BF16),
                   jax.ShapeDtypeStruct((s, A_WIDTH), F32),
                   jax.ShapeDtypeStruct((2 * A_HEADS, nchunk, A_DIM, A_DIM), F32),
                   jax.ShapeDtypeStruct((2 * A_HEADS, nchunk, CHUNK, CHUNK), F32)]
        + (list(ride.out_shapes) if ride else []),
        scratch_shapes=[pltpu.VMEM((s, wide), F32)] + (list(ride.scratch_shapes) if ride else []),
        compiler_params=_params(("arbitrary",)),
    )(cqkv, proj, proj, pa, *(ride.operands if ride else []))


def _delta_out_bwd(o_sum, proj, pa, d_mixed, tr=256):
    s = o_sum.shape[0]

    def body(o_ref, za_ref, pa_ref, dm_ref, do_ref, dza_ref, dpa_ref):
        @pl.when(pl.program_id(0) == 0)
        def _():
            dpa_ref[...] = jnp.zeros_like(dpa_ref)

        _, vjp = jax.vjp(_a_final, o_ref[...], za_ref[...], pa_ref[...])
        d_o, d_za, dpa = vjp(dm_ref[...].astype(F32))
        do_ref[...] = d_o
        dza_ref[...] = d_za.astype(BF16)
        dpa_ref[...] += dpa

    def rows(col):
        return pl.BlockSpec((tr, A_WIDTH), lambda i: (i, col))

    small = pl.BlockSpec((8, LANE), lambda i: (0, 0))
    return pl.pallas_call(
        body, name="delta_out_bwd", grid=(s // tr,), in_specs=[rows(0), rows(P_ZA // A_WIDTH), small, rows(0)],
        out_specs=[rows(0), rows(0), small],
        out_shape=[jax.ShapeDtypeStruct((s, A_WIDTH), F32), jax.ShapeDtypeStruct((s, A_WIDTH), BF16),
                   jax.ShapeDtypeStruct((8, LANE), F32)],
        compiler_params=_params(("arbitrary",)),
    )(o_sum, proj, pa, d_mixed)


def _delta_bwd(cqkv, proj, pa, d_o, states, inverses, ride=None):
    s = cqkv.shape[1]
    nchunk = s // CHUNK
    heads = A_BWD_HEADS
    steps = A_HEADS // heads
    wide, trio, gates, small, cols, _, _ = _a_specs(s, heads)
    n_in = len(ride.operands) if ride else 0
    n_out = len(ride.out_shapes) if ride else 0

    def body(*refs):
        c_ref, gt_ref, pa_ref, do_ref, s_hbm, t_hbm = refs[:6]
        dc_ref, dgt_ref, dpa_ref = refs[6 + n_in:9 + n_in]
        s_buf, t_buf, s_sems = refs[9 + n_in + n_out:12 + n_in + n_out]
        riders = (refs[6:6 + n_in], refs[9 + n_in:9 + n_in + n_out], refs[12 + n_in + n_out:])
        if ride:
            pl.when(pl.program_id(0) == 0)(lambda: ride.start(*riders))
        h0 = pl.program_id(0) * heads
        pa_v = pa_ref[...]

        @pl.when(h0 == 0)
        def _():
            dgt_ref[...] = jnp.zeros_like(dgt_ref)
            dpa_ref[...] = jnp.zeros_like(dpa_ref)

        dc_ref[...] = jnp.zeros_like(dc_ref)

        def state_copies(n, slot):
            tiles = _a_tiles(nchunk - 1 - n, nchunk, heads)
            return ([pltpu.make_async_copy(s_hbm.at[2 * h0 + b, i], s_buf.at[slot, b], s_sems.at[0, slot, b])
                     for b, (i, _, _) in enumerate(tiles)]
                    + [pltpu.make_async_copy(t_hbm.at[2 * h0 + b, i], t_buf.at[slot, b], s_sems.at[1, slot, b])
                       for b, (i, _, _) in enumerate(tiles)])

        for cp in state_copies(0, 0):
            cp.start()

        def step(n, carry, parity):
            d_sts, dpa = carry
            tiles = _a_tiles(nchunk - 1 - n, nchunk, heads)
            for cp in state_copies(n, parity):
                cp.wait()

            @pl.when(n + 1 < nchunk)
            def _():
                for cp in state_copies(n + 1, 1 - parity):
                    cp.start()

            sts, kept = s_buf[parity], t_buf[parity]
            d_o_t = jnp.stack([do_ref[sl, ln] for _, sl, ln in tiles], axis=0)
            _, vjp_c = jax.vjp(lambda *a: _a_step(*a, h0, kept)[:2], sts, *_a_load(tiles, c_ref, gt_ref), pa_v)
            d_prev, dcq, dck, dcv, dgts, dpa_i = vjp_c((d_sts, d_o_t))
            for b, (_, sl, ln) in enumerate(tiles):
                for r, dc in enumerate((dcq, dck, dcv)):
                    dc_ref[r, sl, ln] += dc[b]
                dgt_ref[sl, :] += dgts[b]
            return d_prev, dpa + dpa_i

        init = (jnp.zeros((2 * heads, A_DIM, A_DIM), F32), jnp.zeros((8, LANE), F32))
        _, dpa_out = lax.fori_loop(0, nchunk, lambda n, carry: step(n, carry, n % 2), init)
        dpa_ref[...] += dpa_out
        if ride:
            pl.when(pl.program_id(0) == steps - 1)(lambda: ride.finish(*riders))

    fixed = pl.BlockSpec((s, LANE), lambda g: (0, 0))
    return pl.pallas_call(
        body, name="delta_bwd", grid=(steps,),
        in_specs=[trio, gates, small, cols(0), pl.BlockSpec(memory_space=pl.ANY), pl.BlockSpec(memory_space=pl.ANY)]
        + [HBM] * n_in,
        out_specs=[trio, fixed, small] + [HBM] * n_out,
        out_shape=[jax.ShapeDtypeStruct((3, s, A_WIDTH), F32), jax.ShapeDtypeStruct((s, LANE), F32),
                   jax.ShapeDtypeStruct((8, LANE), F32)] + (list(ride.out_shapes) if ride else []),
        scratch_shapes=[pltpu.VMEM((2, 2 * heads, A_DIM, A_DIM), F32), pltpu.VMEM((2, 2 * heads, CHUNK, CHUNK), F32),
                        pltpu.SemaphoreType.DMA((2, 2, 2 * heads))]
        + (list(ride.scratch_shapes) if ride else []),
        compiler_params=_params(("arbitrary",)),
    )(cqkv, proj, pa, d_o, states, inverses, *(ride.operands if ride else []))


def _rope_tables(s):
    inv = ROPE_THETA ** (-jnp.arange(0, B_DIM, 2, dtype=F32) / B_DIM)
    ang = jnp.arange(s, dtype=F32)[:, None] * inv[None, :]
    cos, sin = jnp.cos(ang), jnp.sin(ang)
    return jnp.concatenate([cos, cos], axis=1), jnp.concatenate([-sin, sin], axis=1)


def _b_block(q_t, z_t, k3, v3, cos_q, sin_q, cos_k, sin_k, pb, n, nb):
    w = WINDOW
    def swap(t):
        return jnp.concatenate([t[:, B_DIM // 2:], t[:, :B_DIM // 2]], axis=1)

    grp = B_HEADS // B_KV
    qi = lax.broadcasted_iota(jnp.int32, (grp * w, 3 * w), 0) & (w - 1)
    kj = lax.broadcasted_iota(jnp.int32, (grp * w, 3 * w), 1)
    kpos = kj + (n - 1) * w
    mask = (jnp.abs(kj - w - qi) <= w) & (kpos >= 0) & (kpos < nb * w)
    lane = lax.broadcasted_iota(jnp.int32, (1, LANE), 1)
    qn, kn = pb[0:1, :B_DIM], pb[1:2, :B_DIM]
    cos_g = jnp.concatenate([cos_q] * grp, axis=0)
    sin_g = jnp.concatenate([sin_q] * grp, axis=0)
    def group(q, k, v, sink):
        k = _rms(k, kn)
        k = k * cos_k + swap(k) * sin_k
        q = _rms(q, qn)
        q = q * cos_g + swap(q) * sin_g
        s = _mm_nt(q, k) * (B_DIM ** -0.5)
        s = jnp.where(mask, s, -jnp.inf)
        m = jnp.maximum(jnp.max(s, axis=1, keepdims=True), sink)
        p = jnp.exp(s - m)
        p = p / (jnp.sum(p, axis=1, keepdims=True) + jnp.exp(sink - m))
        return _mm(p, v)

    stack = lambda ts: jnp.concatenate([t[None] for t in ts], axis=0)
    qs, ks, vs, sinks = [], [], [], []
    for hk in range(B_KV):
        heads = [hk * grp + g for g in range(grp)]
        ks.append(k3[:, hk * B_DIM:(hk + 1) * B_DIM])
        vs.append(v3[:, hk * B_DIM:(hk + 1) * B_DIM])
        qs.append(jnp.concatenate([q_t[:, hq * B_DIM:(hq + 1) * B_DIM] for hq in heads], axis=0))
        sinks.append(jnp.concatenate(
            [jnp.broadcast_to(jnp.sum(jnp.where(lane == hq, pb[2:3, :], 0.0), axis=1, keepdims=True), (w, 1))
             for hq in heads], axis=0))
    o = jax.vmap(group)(stack(qs), stack(ks), stack(vs), stack(sinks))
    outs = [o[hk, g * w:(g + 1) * w, :] for hk in range(B_KV) for g in range(grp)]
    return jnp.concatenate(outs, axis=1) * _silu(z_t)


def _b_specs(s):
    nb = s // WINDOW
    qsp = pl.BlockSpec((WINDOW, 512), lambda n: (n, P_QB // 512))
    zsp = pl.BlockSpec((WINDOW, 512), lambda n: (n, P_ZB // 512))

    def three(col, width):
        return [pl.BlockSpec((WINDOW, width), lambda n: (jnp.maximum(n - 1, 0), col)),
                pl.BlockSpec((WINDOW, width), lambda n: (n, col)),
                pl.BlockSpec((WINDOW, width), lambda n: (jnp.minimum(n + 1, nb - 1), col))]

    tab = pl.BlockSpec((WINDOW, B_DIM), lambda n: (n, 0))
    small = pl.BlockSpec((8, LANE), lambda n: (0, 0))
    specs = [qsp, zsp] + three(P_KB // LANE, LANE) + three(P_VB // LANE, LANE) + [tab, tab] + three(0, B_DIM) + three(0, B_DIM) + [small]
    return nb, specs


def _b_args(proj, cos2, sin2, pb):
    return (proj, proj, proj, proj, proj, proj, proj, proj, cos2, sin2, cos2, cos2, cos2, sin2, sin2, sin2, pb)


def _b_load(refs):
    (q_ref, z_ref, kp, kc, kx, vp, vc, vx, cq, sq, ckp, ckc, ckx, skp, skc, skx, pb_ref) = refs
    cat = lambda *r: jnp.concatenate([t[...] for t in r], axis=0)
    return (q_ref[...], z_ref[...], cat(kp, kc, kx), cat(vp, vc, vx), cq[...], sq[...], cat(ckp, ckc, ckx),
            cat(skp, skc, skx), pb_ref[...])


def _attn_b_fwd(proj, cos2, sin2, pb, mixed):
    s = proj.shape[0]
    nb, specs = _b_specs(s)

    def body(*refs):
        o_ref = refs[-1]
        args = _b_load(refs[:-2])
        o_ref[...] = _b_block(*args, pl.program_id(0), nb).astype(BF16)

    return pl.pallas_call(
        body, name="attn_b_fwd", grid=(nb,), in_specs=specs + [pl.BlockSpec(memory_space=pl.ANY)],
        out_specs=pl.BlockSpec((WINDOW, 512), lambda n: (n, A_WIDTH // 512)),
        out_shape=jax.ShapeDtypeStruct(mixed.shape, mixed.dtype), input_output_aliases={len(specs): 0},
        compiler_params=_params(("parallel",)),
    )(*_b_args(proj, cos2, sin2, pb), mixed)


def _attn_b_bwd(proj, cos2, sin2, pb, d_mixed):
    s = proj.shape[0]
    nb, specs = _b_specs(s)
    w = WINDOW

    def body(*refs):
        dm_ref, dq_ref, dz_ref, dk_ref, dv_ref, dpb_ref = refs[-6:]
        n = pl.program_id(0)
        q_t, z_t, k3, v3, cq, sq, ck, sk, pb_v = _b_load(refs[:-6])

        @pl.when(n == 0)
        def _():
            dk_ref[...] = jnp.zeros_like(dk_ref)
            dv_ref[...] = jnp.zeros_like(dv_ref)
            dpb_ref[...] = jnp.zeros_like(dpb_ref)

        def f(q_, z_, k_, v_, pb_):
            return _b_block(q_, z_, k_, v_, cq, sq, ck, sk, pb_, n, nb)

        _, vjp = jax.vjp(f, q_t, z_t, k3, v3, pb_v)
        dq, dz, dk3, dv3, dpb = vjp(dm_ref[...].astype(F32))
        dq_ref[...] = dq.astype(BF16)
        dz_ref[...] = dz.astype(BF16)
        dpb_ref[...] += dpb

        def add(j, cond):
            @pl.when(cond)
            def _():
                rows = pl.ds(pl.multiple_of((n - 1 + j) * w, w), w)
                dk_ref[rows, :] += dk3[j * w:(j + 1) * w, :]
                dv_ref[rows, :] += dv3[j * w:(j + 1) * w, :]

        add(0, n > 0)
        add(1, n >= 0)
        add(2, n < nb - 1)

    blk = pl.BlockSpec((w, 512), lambda n: (n, 0))
    whole = pl.BlockSpec((s, LANE), lambda n: (0, 0))
    small = pl.BlockSpec((8, LANE), lambda n: (0, 0))
    return pl.pallas_call(
        body, name="attn_b_bwd", grid=(nb,),
        in_specs=specs + [pl.BlockSpec((w, 512), lambda n: (n, 2))],
        out_specs=[blk, blk, whole, whole, small],
        out_shape=[jax.ShapeDtypeStruct((s, 512), BF16), jax.ShapeDtypeStruct((s, 512), BF16),
                   jax.ShapeDtypeStruct((s, LANE), F32), jax.ShapeDtypeStruct((s, LANE), F32),
                   jax.ShapeDtypeStruct((8, LANE), F32)],
        compiler_params=_params(("arbitrary",)),
    )(*_b_args(proj, cos2, sin2, pb), d_mixed)


def _mem_kv_fwd(mem, mem_norm_w, w_kv):
    def body(mem_ref, nw_ref, w_ref, kv_ref):
        mn = _rms(mem_ref[...], nw_ref[...]).astype(BF16)
        kv_ref[...] = jnp.dot(mn, w_ref[...], preferred_element_type=F32)

    return pl.pallas_call(
        body, name="mem_kv_fwd", out_shape=jax.ShapeDtypeStruct((MEM_LEN, 2 * C_HEADS * C_DIM), F32),
        compiler_params=_params(),
    )(mem, mem_norm_w, w_kv)


def _mem_kv_bwd(mem, mem_norm_w, w_kv, d_kv):
    def body(mem_ref, nw_ref, w_ref, g_ref, gw_ref, gn_ref):
        mn, vjp = jax.vjp(_rms, mem_ref[...], nw_ref[...])
        g = g_ref[...].astype(BF16)
        gw_ref[...] = lax.dot_general(mn.astype(BF16), g, (((0,), (0,)), ((), ())), preferred_element_type=F32)
        d_mn = lax.dot_general(g, w_ref[...], (((1,), (1,)), ((), ())), preferred_element_type=F32)
        gn_ref[...] = vjp(d_mn)[1]

    return pl.pallas_call(
        body, name="mem_kv_bwd",
        out_shape=[jax.ShapeDtypeStruct((D_MODEL, 2 * C_HEADS * C_DIM), F32), jax.ShapeDtypeStruct((1, D_MODEL), F32)],
        compiler_params=_params(),
    )(mem, mem_norm_w, w_kv, d_kv)


def _c_tile(q_t, z_t, kvm, pc):
    width = C_HEADS * C_DIM

    def head(q, k, v):
        q = _rms(q, pc[0:1, :])
        k = _rms(k, pc[1:2, :])
        s = _mm_nt(q, k) * (C_DIM ** -0.5)
        p = jnp.exp(s - jnp.max(s, axis=1, keepdims=True))
        p = p / jnp.sum(p, axis=1, keepdims=True)
        return _mm(p, v)

    def stack(t, first):
        return jnp.concatenate([t[None, :, first + h * C_DIM:first + (h + 1) * C_DIM] for h in range(C_HEADS)], axis=0)

    o = jax.vmap(head)(stack(q_t, 0), stack(kvm, 0), stack(kvm, width))
    return jnp.concatenate([o[h] for h in range(C_HEADS)], axis=1) * _silu(z_t)


def _attn_c_fwd(proj, kvm, pc, mixed, tq=256):
    s = proj.shape[0]

    def body(q_ref, z_ref, kv_ref, pc_ref, mixed_ref, o_ref):
        o_ref[...] = _c_tile(q_ref[...], z_ref[...], kv_ref[...], pc_ref[...]).astype(BF16)

    return pl.pallas_call(
        body, name="attn_c_fwd", grid=(s // tq,),
        in_specs=[pl.BlockSpec((tq, 512), lambda i: (i, P_QC // 512)), pl.BlockSpec((tq, 512), lambda i: (i, P_ZC // 512)),
                  pl.BlockSpec(kvm.shape, lambda i: (0, 0)), pl.BlockSpec((8, LANE), lambda i: (0, 0)),
                  pl.BlockSpec(memory_space=pl.ANY)],
        out_specs=pl.BlockSpec((tq, 512), lambda i: (i, (A_WIDTH + 512) // 512)),
        out_shape=jax.ShapeDtypeStruct(mixed.shape, mixed.dtype), input_output_aliases={4: 0},
        compiler_params=_params(("parallel",)),
    )(proj, proj, kvm, pc, mixed)


def _attn_c_bwd(proj, kvm, pc, d_mixed, tq=256):
    s = proj.shape[0]

    def body(q_ref, z_ref, kv_ref, pc_ref, dm_ref, dq_ref, dz_ref, dkv_ref, dpc_ref):
        @pl.when(pl.program_id(0) == 0)
        def _():
            dkv_ref[...] = jnp.zeros_like(dkv_ref)
            dpc_ref[...] = jnp.zeros_like(dpc_ref)

        _, vjp = jax.vjp(_c_tile, q_ref[...], z_ref[...], kv_ref[...], pc_ref[...])
        dq, dz, dkv, dpc = vjp(dm_ref[...].astype(F32))
        dq_ref[...] = dq.astype(BF16)
        dz_ref[...] = dz.astype(BF16)
        dkv_ref[...] += dkv
        dpc_ref[...] += dpc

    blk = pl.BlockSpec((tq, 512), lambda i: (i, 0))
    kvs = pl.BlockSpec(kvm.shape, lambda i: (0, 0))
    small = pl.BlockSpec((8, LANE), lambda i: (0, 0))
    return pl.pallas_call(
        body, name="attn_c_bwd", grid=(s // tq,),
        in_specs=[pl.BlockSpec((tq, 512), lambda i: (i, P_QC // 512)), pl.BlockSpec((tq, 512), lambda i: (i, P_ZC // 512)),
                  kvs, small, pl.BlockSpec((tq, 512), lambda i: (i, 3))],
        out_specs=[blk, blk, kvs, small],
        out_shape=[jax.ShapeDtypeStruct((s, 512), BF16), jax.ShapeDtypeStruct((s, 512), BF16),
                   jax.ShapeDtypeStruct(kvm.shape, F32), jax.ShapeDtypeStruct((8, LANE), F32)],
        compiler_params=_params(("arbitrary",)),
    )(proj, proj, kvm, pc, d_mixed)


def _pad_row(v, width=LANE):
    v = v.reshape(1, -1)
    return jnp.pad(v, ((0, 0), (0, width - v.shape[1])))


def _local_step(x, mem, target, norm_w, w_perm_t, w_blocks_t, conv_w, pa, pb, pc, mem_norm_w, w_kv, w_out, gather=None,
                exchange=None):
    s = x.shape[0]
    cos2, sin2 = _rope_tables(s)
    hn = _rms_fwd(x, norm_w)
    wide = dict(tm=1024, tn=512, tk=2048)
    proj = _matmul(hn, w_perm_t, "nt", F32, "mm_proj", **wide)
    cqkv = _conv_fwd(proj, conv_w)
    if gather is None:
        mixed, o_sum, states, inverses = _delta_fwd(cqkv, proj, pa)
    else:
        mixed, o_sum, states, inverses, *arrived = _delta_fwd(cqkv, proj, pa, gather[0])
        w_out, w_kv = gather[1](*arrived)
    mixed = _attn_b_fwd(proj, cos2, sin2, pb, mixed)
    kvm = _mem_kv_fwd(mem, mem_norm_w, w_kv)
    mixed = _attn_c_fwd(proj, kvm, pc, mixed)
    dy, dyb, loss_parts = _out_loss(mixed, w_out, x, target)

    d_mixed = _matmul(dyb, w_out, "nt", BF16, "mm_dmixed", **wide)
    g_w_out = _matmul(mixed, dyb, "tn", F32, "mm_gwout", **wide)
    d_qc, d_zc, d_kvm, d_pc = _attn_c_bwd(proj, kvm, pc, d_mixed)
    g_w_kv, g_mem_norm = _mem_kv_bwd(mem, mem_norm_w, w_kv, d_kvm)
    d_qb, d_zb, d_kb, d_vb, d_pb = _attn_b_bwd(proj, cos2, sin2, pb, d_mixed)
    d_o, d_za, d_pa_out = _delta_out_bwd(o_sum, proj, pa, d_mixed)
    early = exchange[0](g_w_out, g_w_kv) if exchange else None
    d_c, d_gt, d_pa_scan, *landed_early = _delta_bwd(cqkv, proj, pa, d_o, states, inverses, early)
    d_pa = d_pa_out + d_pa_scan
    d_qkv, g_conv = _conv_bwd(proj, conv_w, d_c)
    d_proj = _cotangent_blocks(d_qkv, d_za, d_gt, d_qb, d_kb, d_vb, d_zb, d_qc, d_zc)
    g_w_blocks_t = _matmul(d_proj, hn, "tn", F32, "mm_gwin", tm=512, tn=2048, tk=2048)
    late = exchange[1](g_w_blocks_t) if exchange else None
    g_x, g_norm, *landed_late = _input_grad(d_proj, w_blocks_t, x, norm_w, dy, late)
    return dict(loss_parts=loss_parts, g_x=g_x, g_norm=g_norm, g_w_blocks_t=g_w_blocks_t, g_conv=g_conv, d_pa=d_pa,
                d_pb=d_pb, d_pc=d_pc, g_mem_norm=g_mem_norm, g_w_kv=g_w_kv, g_w_out=g_w_out,
                landed=landed_late + landed_early)


_SEGMENTS = ((0, O_GT, 0), (O_GT, O_QB, P_GT), (O_QB, O_KB, P_QB), (O_KB, O_VB, P_KB), (O_VB, O_ZB, P_VB),
             (O_ZB, O_QC, P_ZB), (O_QC, O_ZC, P_QC), (O_ZC, IN_WIDTH, P_ZC))


def _permute_blocks(w4):
    parts = []
    for first, end, _ in sorted(_SEGMENTS, key=lambda seg: seg[2]):
        row = first
        while row < end:
            k = row // W_IN_BLOCK
            stop = min(end, (k + 1) * W_IN_BLOCK)
            parts.append(w4[k][row - k * W_IN_BLOCK:stop - k * W_IN_BLOCK, :])
            row = stop
    parts.append(jnp.zeros((P_WIDTH - IN_WIDTH, w4.shape[2]), w4.dtype))
    return jnp.concatenate(parts, axis=0)


def _cotangent_blocks(d_qkv, d_za, d_gt, d_qb, d_kb, d_vb, d_zb, d_qc, d_zc):
    s = d_qkv.shape[0]
    tr = min(256, s)
    pieces = (d_qkv, d_za, d_gt, d_qb, d_kb, d_vb, d_zb, d_qc, d_zc)

    def body(*refs):
        o_ref = refs[-1]
        tiles = [r[...].astype(BF16) for r in refs[:-1]]
        tiles[2] = tiles[2][:, :O_QB - O_GT]
        orig = jnp.concatenate(tiles, axis=1)
        pad = jnp.zeros((tr, W_IN_PAD - W_IN_BLOCK), BF16)
        parts = []
        for k in range(N_CHIPS):
            parts += [orig[:, k * W_IN_BLOCK:(k + 1) * W_IN_BLOCK], pad]
        o_ref[...] = jnp.concatenate(parts, axis=1)

    return pl.pallas_call(
        body, name="cotangent_blocks", grid=(s // tr,),
        in_specs=[pl.BlockSpec((tr, p.shape[1]), lambda i: (i, 0)) for p in pieces],
        out_specs=pl.BlockSpec((tr, N_CHIPS * W_IN_PAD), lambda i: (i, 0)),
        out_shape=jax.ShapeDtypeStruct((s, N_CHIPS * W_IN_PAD), BF16), compiler_params=_params(("parallel",)),
    )(*pieces)


HBM = pl.BlockSpec(memory_space=pltpu.HBM)


def _place():
    x, y, c = lax.axis_index("x"), lax.axis_index("y"), lax.axis_index("c")
    chips = [(1 - x, y), (x, 1 - y), (1 - x, 1 - y)]
    return x, y, c, 2 * x + y, chips, [2 * cx + cy for cx, cy in chips]


PIECE_ROWS_CAP = 600


def _remote(src, dst, send_sems, recv_sems, k, to):
    return pltpu.make_async_remote_copy(src_ref=src, dst_ref=dst, send_sem=send_sems.at[k], recv_sem=recv_sems.at[k],
                                        device_id=to, device_id_type=MESH)


def _half_cols(ref, c):
    half = ref.shape[-1] // 2
    return pl.ds(pl.multiple_of(c * half, LANE), half)


class _PairedGather:
    def __init__(self, blocks):
        n = len(blocks)
        self.operands = list(blocks)
        self.out_shapes = [jax.ShapeDtypeStruct((N_CHIPS,) + b.shape, b.dtype) for b in blocks]
        self.scratch_shapes = [pltpu.SemaphoreType.DMA((6 * n,)), pltpu.SemaphoreType.DMA((6 * n,))]

    @staticmethod
    def _copies(srcs, dsts, sems):
        x, y, c, me, chips, chip_ids = _place()
        sends, landed, passes, passed = [], [], [], []
        for a, (src, dst) in enumerate(zip(srcs, dsts)):
            mine, other = _half_cols(src, c), _half_cols(src, 1 - c)
            for j, (chip, cid) in enumerate(zip(chips, chip_ids)):
                sends.append(_remote(src.at[:, mine], dst.at[me, :, mine], sems[0], sems[1], 6 * a + j, (*chip, c)))
                here = dst.at[cid, :, mine]
                landed.append(_remote(here, here, sems[0], sems[1], 6 * a + j, (x, y, 1 - c)))
                passes.append(_remote(here, here, sems[0], sems[1], 6 * a + 3 + j, (x, y, 1 - c)))
                there = dst.at[cid, :, other]
                passed.append(_remote(there, there, sems[0], sems[1], 6 * a + 3 + j, (x, y, 1 - c)))
        return sends, landed, passes, passed

    def start(self, srcs, dsts, sems):
        for cp in self._copies(srcs, dsts, sems)[0]:
            cp.start()

    def middle(self, srcs, dsts, sems):
        _, landed, passes, _ = self._copies(srcs, dsts, sems)
        for arrived, onward in zip(landed, passes):
            arrived.wait_recv()
            onward.start()

    def finish(self, srcs, dsts, sems):
        sends, _, passes, passed = self._copies(srcs, dsts, sems)
        for cp in passed:
            cp.wait_recv()
        for cp in sends + passes:
            cp.wait_send()


def _all_gather_weights(bigs, conv_b):
    bigs = tuple(bigs)
    n_big = len(bigs)

    def body(*refs):
        srcs, conv_src = refs[:n_big], refs[n_big]
        dsts, conv_dst = refs[n_big + 1:2 * n_big + 1], refs[2 * n_big + 1]
        send_sems, recv_sems, local_sems = refs[2 * n_big + 2:]
        x, y, c, me, chips, chip_ids = _place()
        sibling = (x, y, 1 - c)
        local = [pltpu.make_async_copy(src, dst.at[me], local_sems.at[a]) for a, (src, dst) in enumerate(zip(srcs, dsts))]
        local.append(pltpu.make_async_copy(conv_src, conv_dst.at[me], local_sems.at[n_big]))
        for cp in local:
            cp.start()
        sends = []
        for a, (src, dst) in enumerate(zip(srcs, dsts)):
            mine = _half_cols(src, c)
            for j, chip in enumerate(chips):
                sends.append(_remote(src.at[:, mine], dst.at[me, :, mine], send_sems, recv_sems, 6 * a + j, (*chip, c)))
        for j, chip in enumerate(chips):
            sends.append(_remote(conv_src, conv_dst.at[me], send_sems, recv_sems, 6 * n_big + j, (*chip, c)))
        for cp in sends:
            cp.start()
        passed = []
        for a, (src, dst) in enumerate(zip(srcs, dsts)):
            mine = _half_cols(src, c)
            for j, cid in enumerate(chip_ids):
                landed = dst.at[cid, :, mine]
                _remote(landed, landed, send_sems, recv_sems, 6 * a + j, sibling).wait_recv()
                cp = _remote(landed, landed, send_sems, recv_sems, 6 * a + 3 + j, sibling)
                cp.start()
                passed.append(cp)
        for a, (src, dst) in enumerate(zip(srcs, dsts)):
            other = _half_cols(src, 1 - c)
            for j, cid in enumerate(chip_ids):
                landed = dst.at[cid, :, other]
                _remote(landed, landed, send_sems, recv_sems, 6 * a + 3 + j, sibling).wait_recv()
        for j, cid in enumerate(chip_ids):
            _remote(conv_src, conv_dst.at[cid], send_sems, recv_sems, 6 * n_big + j, sibling).wait_recv()
        for cp in sends + passed:
            cp.wait_send()
        for cp in local:
            cp.wait()

    n_sem = 6 * n_big + 3
    return pl.pallas_call(
        body, name="all_gather_weights",
        out_shape=[jax.ShapeDtypeStruct((N_CHIPS,) + w.shape, w.dtype) for w in bigs + (conv_b,)],
        in_specs=[pl.BlockSpec(memory_space=pltpu.VMEM)] * (n_big + 1), out_specs=[HBM] * (n_big + 1),
        scratch_shapes=[pltpu.SemaphoreType.DMA((n_sem,)), pltpu.SemaphoreType.DMA((n_sem,)),
                        pltpu.SemaphoreType.DMA((n_big + 1,))],
        compiler_params=_params(),
    )(*bigs, conv_b)


def _pair_exchange(grads, name):
    n = len(grads)
    pieces = [_row_tile(g.shape[1]) for g in grads]

    def body(*refs):
        srcs, gots = refs[:n], refs[n:2 * n]
        stages, narrow = refs[2 * n:3 * n], refs[3 * n:4 * n]
        send_sems, recv_sems, load_sems = refs[4 * n:]
        x, y, c, _, _, _ = _place()
        sibling = (x, y, 1 - c)
        for a in range(n):
            slabs, rows, _ = gots[a].shape
            piece = pieces[a]
            per_slab = rows // piece
            theirs = _half_cols(srcs[a], 1 - c)
            loads, sends = [], []
            for i in range(slabs * per_slab):
                k, r, slot = i // per_slab, i % per_slab, i % 2
                part = pl.ds(r * piece, piece)
                loads.append(pltpu.make_async_copy(srcs[a].at[k, part, theirs], stages[a].at[slot], load_sems.at[2 * a + slot]))
                sends.append(pltpu.make_async_remote_copy(
                    src_ref=narrow[a].at[slot], dst_ref=gots[a].at[k, part, :],
                    send_sem=send_sems.at[2 * a + slot], recv_sem=recv_sems.at[a], device_id=sibling, device_id_type=MESH))
            loads[0].start()
            for i in range(len(loads)):
                loads[i].wait()
                narrow[a][i % 2] = stages[a][i % 2].astype(BF16)
                sends[i].start()
                if i + 1 < len(loads):
                    if i >= 1:
                        sends[i - 1].wait_send()
                    loads[i + 1].start()
            for cp in sends[-2:]:
                cp.wait_send()
        for a in range(n):
            pltpu.make_async_remote_copy(src_ref=gots[a], dst_ref=gots[a], send_sem=send_sems.at[2 * a],
                                         recv_sem=recv_sems.at[a], device_id=sibling, device_id_type=MESH).wait_recv()

    halves = [jax.ShapeDtypeStruct((g.shape[0], g.shape[1], g.shape[2] // 2), BF16) for g in grads]
    return pl.pallas_call(
        body, name=name, out_shape=halves, in_specs=[HBM] * n, out_specs=[HBM] * n,
        scratch_shapes=[pltpu.VMEM((2, piece, g.shape[2] // 2), dt) for dt in (F32, BF16) for piece, g in zip(pieces, grads)]
        + [pltpu.SemaphoreType.DMA((2 * n,)), pltpu.SemaphoreType.DMA((n,)), pltpu.SemaphoreType.DMA((2 * n,))],
        compiler_params=_params(),
    )(*grads)


class _ChipExchange:
    def __init__(self, halves):
        n = len(halves)
        self.operands = list(halves)
        self.out_shapes = [jax.ShapeDtypeStruct((N_CHIPS - 1,) + h.shape[1:], h.dtype) for h in halves]
        self.scratch_shapes = [pltpu.SemaphoreType.DMA((3 * n,)), pltpu.SemaphoreType.DMA((3 * n,))]

    @staticmethod
    def _copies(srcs, lands, sems):
        _, _, c, _, chips, chip_ids = _place()
        return [_remote(src.at[cid], land.at[j], sems[0], sems[1], 3 * a + j, (*chip, c))
                for a, (src, land) in enumerate(zip(srcs, lands)) for j, (chip, cid) in enumerate(zip(chips, chip_ids))]

    def start(self, srcs, lands, sems):
        for cp in self._copies(srcs, lands, sems):
            cp.start()

    def finish(self, srcs, lands, sems):
        copies = self._copies(srcs, lands, sems)
        for cp in copies:
            cp.wait_recv()
        for cp in copies:
            cp.wait_send()


def _pair_gather(halves, rows):
    n = len(halves)

    def body(*refs):
        srcs, fulls = refs[:n], refs[n:2 * n]
        send_sems, recv_sems, local_sems = refs[2 * n:]
        x, y, c, _, _, _ = _place()
        copies = []
        for a in range(n):
            mine, src = _half_cols(fulls[a], c), srcs[a].at[pl.ds(0, rows[a]), :]
            keep = pltpu.make_async_copy(src, fulls[a].at[:, mine], local_sems.at[a])
            keep.start()
            give = _remote(src, fulls[a].at[:, mine], send_sems, recv_sems, a, (x, y, 1 - c))
            give.start()
            copies += [keep, give]
        for a in range(n):
            other, src = _half_cols(fulls[a], 1 - c), srcs[a].at[pl.ds(0, rows[a]), :]
            copies[2 * a].wait()
            copies[2 * a + 1].wait_send()
            _remote(src, fulls[a].at[:, other], send_sems, recv_sems, a, (x, y, 1 - c)).wait_recv()

    return pl.pallas_call(
        body, name="grad_pair_gather",
        out_shape=[jax.ShapeDtypeStruct((r, 2 * h.shape[1]), h.dtype) for r, h in zip(rows, halves)],
        in_specs=[pl.BlockSpec(memory_space=pltpu.VMEM)] * n, out_specs=[HBM] * n,
        scratch_shapes=[pltpu.SemaphoreType.DMA((n,)), pltpu.SemaphoreType.DMA((n,)), pltpu.SemaphoreType.DMA((n,))],
    )(*halves)


def _all_reduce_small(p):
    n_dev = 8

    def body(p_ref, o_ref, land, send_sems, recv_sems):
        x, y, c = lax.axis_index("x"), lax.axis_index("y"), lax.axis_index("c")
        me = 4 * x + 2 * y + c
        land[me] = p_ref[...]
        sends = []
        for k in range(1, n_dev):
            fx, fy, fc = (k >> 2) & 1, (k >> 1) & 1, k & 1
            to = (x ^ fx, y ^ fy, c ^ fc)
            cp = _remote(p_ref, land.at[me], send_sems, recv_sems, k - 1, to)
            cp.start()
            sends.append(cp)
        for k in range(1, n_dev):
            _remote(p_ref, land.at[me ^ k], send_sems, recv_sems, k - 1, (x, y, c)).wait_recv()
        total = land[0]
        for d in range(1, n_dev):
            total = total + land[d]
        o_ref[...] = total
        for cp in sends:
            cp.wait_send()

    vm = pl.BlockSpec(memory_space=pltpu.VMEM)
    return pl.pallas_call(
        body, name="all_reduce_small", out_shape=jax.ShapeDtypeStruct(p.shape, p.dtype), in_specs=[vm], out_specs=vm,
        scratch_shapes=[pltpu.VMEM((n_dev,) + p.shape, p.dtype), pltpu.SemaphoreType.DMA((n_dev - 1,)),
                        pltpu.SemaphoreType.DMA((n_dev - 1,))],
    )(p)


def _row_tile(rows):
    fits = [t for t in range(8, min(rows, PIECE_ROWS_CAP) + 1, 8) if rows % t == 0]
    return max(fits) if fits else rows


def _pair_sum(full, got, core, name):
    n, r, c = got.shape
    tr = _row_tile(r)

    def body(core_ref, a_ref, b_ref, o_ref):
        o_ref[...] = (a_ref[...] + b_ref[...].astype(F32)).astype(BF16)

    blk = pl.BlockSpec((None, tr, c), lambda i, j, core_ref: (i, j, 0))
    grid_spec = pltpu.PrefetchScalarGridSpec(
        num_scalar_prefetch=1, grid=(n, r // tr),
        in_specs=[pl.BlockSpec((None, tr, c), lambda i, j, core_ref: (i, j, core_ref[0])), blk], out_specs=blk)
    return pl.pallas_call(body, name=name, grid_spec=grid_spec, out_shape=jax.ShapeDtypeStruct(got.shape, BF16),
                          compiler_params=_params(("parallel", "parallel")))(core, full, got)


def _chip_sum(full, got, land, place, name):
    n, r, c = land.shape
    tr = _row_tile(r)

    def body(place_ref, a_ref, b_ref, l_ref, o_ref):
        total = a_ref[...] + b_ref[...].astype(F32)
        for j in range(n):
            total = total + l_ref[j].astype(F32)
        o_ref[...] = total

    grid_spec = pltpu.PrefetchScalarGridSpec(
        num_scalar_prefetch=1, grid=(r // tr,),
        in_specs=[pl.BlockSpec((None, tr, c), lambda i, p: (p[0], i, p[1])),
                  pl.BlockSpec((None, tr, c), lambda i, p: (p[0], i, 0)),
                  pl.BlockSpec((n, tr, c), lambda i, p: (0, i, 0))],
        out_specs=pl.BlockSpec((tr, c), lambda i, p: (i, 0)))
    return pl.pallas_call(body, name=name, grid_spec=grid_spec, out_shape=jax.ShapeDtypeStruct((r, c), F32),
                          compiler_params=_params(("parallel",)))(place, full, got, land)


def _adamw(w, g, m, v, name, echo=False):
    r, c = w.shape
    tr = _row_tile(r)
    tc = 1024 if c % 1024 == 0 else c

    def body(w_ref, g_ref, m_ref, v_ref, d_ref, mo_ref, vo_ref, *g_out):
        g_ = g_ref[...]
        for o in g_out:
            o[...] = g_
        m2 = ADAM_B1 * m_ref[...] + (1.0 - ADAM_B1) * g_
        v2 = ADAM_B2 * v_ref[...] + (1.0 - ADAM_B2) * jnp.square(g_)
        m_hat = m2 / (1.0 - ADAM_B1 ** ADAM_STEP)
        v_hat = v2 / (1.0 - ADAM_B2 ** ADAM_STEP)
        d_ref[...] = -ADAM_LR * (m_hat / (jnp.sqrt(v_hat) + ADAM_EPS) + ADAM_WD * w_ref[...])
        mo_ref[...] = m2
        vo_ref[...] = v2

    blk = pl.BlockSpec((tr, tc), lambda i, j: (i, j))
    n_out = 4 if echo else 3
    return pl.pallas_call(body, name=name, grid=(r // tr, c // tc), in_specs=[blk] * 4, out_specs=[blk] * n_out,
                          out_shape=[jax.ShapeDtypeStruct(w.shape, F32)] * n_out,
                          compiler_params=_params(("parallel", "parallel")))(w, g, m, v)


SMALL_NAMES = ("norm_w", "mem_norm_w", "o_norm_a", "q_norm_c", "k_norm_c", "q_norm_b", "k_norm_b",
               "a_log_fwd", "a_log_bwd", "dt_bias_fwd", "dt_bias_bwd", "sink_b")
SMALL_SIZES = (2048, 2048, 128, 128, 128, 64, 64, 8, 8, 8, 8, 8)
SMALL_LOSS = sum(SMALL_SIZES)
SMALL_CONV = 5120
SMALL_TOTAL = SMALL_CONV + CONV_K * 3 * A_WIDTH
SMALL_ROWS = SMALL_TOTAL // LANE


def _pack_small(parts, extra=None, conv=None):
    vec = [parts[n].reshape(-1) for n in SMALL_NAMES]
    vec.append(jnp.zeros((1,), F32) if extra is None else extra.reshape(1))
    vec.append(jnp.zeros((SMALL_CONV - SMALL_LOSS - 1,), F32))
    vec.append(jnp.zeros((SMALL_TOTAL - SMALL_CONV,), F32) if conv is None else conv.reshape(-1))
    return jnp.concatenate(vec).reshape(SMALL_ROWS, LANE)


def _unpack_small(packed):
    flat = packed.reshape(-1)
    out, off = {}, 0
    for n, size in zip(SMALL_NAMES, SMALL_SIZES):
        out[n] = flat[off:off + size].reshape(1, size)
        off += size
    return out


WEIGHT_ORDER = ("norm_w", "w_in", "conv_w_a", "a_log_fwd", "a_log_bwd", "dt_bias_fwd", "dt_bias_bwd", "o_norm_a",
                "q_norm_b", "k_norm_b", "sink_b", "mem_norm_w", "w_mem_kv", "q_norm_c", "k_norm_c", "w_out")


def kernel(x, mem, norm_w, w_in, conv_w_a, a_log_fwd, a_log_bwd, dt_bias_fwd, dt_bias_bwd, o_norm_a, q_norm_b, k_norm_b, sink_b, mem_norm_w, w_mem_kv, q_norm_c, k_norm_c, w_out, loss_target, m_norm_w, m_w_in, m_conv_w_a, m_a_log_fwd, m_a_log_bwd, m_dt_bias_fwd, m_dt_bias_bwd, m_o_norm_a, m_q_norm_b, m_k_norm_b, m_sink_b, m_mem_norm_w, m_w_mem_kv, m_q_norm_c, m_k_norm_c, m_w_out, v_norm_w, v_w_in, v_conv_w_a, v_a_log_fwd, v_a_log_bwd, v_dt_bias_fwd, v_dt_bias_bwd, v_o_norm_a, v_q_norm_b, v_k_norm_b, v_sink_b, v_mem_norm_w, v_w_mem_kv, v_q_norm_c, v_k_norm_c, v_w_out):
    weights = dict(norm_w=norm_w, w_in=w_in, conv_w_a=conv_w_a, a_log_fwd=a_log_fwd, a_log_bwd=a_log_bwd,
                   dt_bias_fwd=dt_bias_fwd, dt_bias_bwd=dt_bias_bwd, o_norm_a=o_norm_a, q_norm_b=q_norm_b,
                   k_norm_b=k_norm_b, sink_b=sink_b, mem_norm_w=mem_norm_w, w_mem_kv=w_mem_kv, q_norm_c=q_norm_c,
                   k_norm_c=k_norm_c, w_out=w_out)
    mom1 = dict(norm_w=m_norm_w, w_in=m_w_in, conv_w_a=m_conv_w_a, a_log_fwd=m_a_log_fwd, a_log_bwd=m_a_log_bwd,
                dt_bias_fwd=m_dt_bias_fwd, dt_bias_bwd=m_dt_bias_bwd, o_norm_a=m_o_norm_a, q_norm_b=m_q_norm_b,
                k_norm_b=m_k_norm_b, sink_b=m_sink_b, mem_norm_w=m_mem_norm_w, w_mem_kv=m_w_mem_kv,
                q_norm_c=m_q_norm_c, k_norm_c=m_k_norm_c, w_out=m_w_out)
    mom2 = dict(norm_w=v_norm_w, w_in=v_w_in, conv_w_a=v_conv_w_a, a_log_fwd=v_a_log_fwd, a_log_bwd=v_a_log_bwd,
                dt_bias_fwd=v_dt_bias_fwd, dt_bias_bwd=v_dt_bias_bwd, o_norm_a=v_o_norm_a, q_norm_b=v_q_norm_b,
                k_norm_b=v_k_norm_b, sink_b=v_sink_b, mem_norm_w=v_mem_norm_w, w_mem_kv=v_w_mem_kv,
                q_norm_c=v_q_norm_c, k_norm_c=v_k_norm_c, w_out=v_w_out)
    chip = 2 * lax.axis_index("x") + lax.axis_index("y")

    own_in = jnp.pad(jnp.transpose(w_in[0]).astype(BF16), ((0, W_IN_PAD - W_IN_BLOCK), (0, 0)))
    w_in4, conv4 = _all_gather_weights([own_in], conv_w_a[0])
    w_perm_t = _permute_blocks(w_in4)
    w_blocks_t = w_in4.reshape(N_CHIPS * W_IN_PAD, D_MODEL)
    conv_full = jnp.transpose(conv4, (1, 0, 2)).reshape(CONV_K, 3 * A_WIDTH)
    own_out, own_kv = w_out[0].astype(BF16), w_mem_kv[0].astype(BF16)

    def assemble(w_out4, w_kv4):
        w_out4 = lax.dynamic_update_index_in_dim(w_out4, own_out, chip, 0)
        w_kv4 = lax.dynamic_update_index_in_dim(w_kv4, own_kv, chip, 0)
        return w_out4.reshape(D_MODEL, D_MODEL), w_kv4.reshape(D_MODEL, 2 * C_HEADS * C_DIM)

    gather = (_PairedGather([own_out, own_kv]), assemble)
    pa = jnp.concatenate([_pad_row(a_log_fwd), _pad_row(a_log_bwd), _pad_row(dt_bias_fwd), _pad_row(dt_bias_bwd),
                          _pad_row(o_norm_a), jnp.zeros((3, LANE), F32)], axis=0)
    pb = jnp.concatenate([_pad_row(q_norm_b), _pad_row(k_norm_b), _pad_row(sink_b), jnp.zeros((5, LANE), F32)], axis=0)
    pc = jnp.concatenate([_pad_row(q_norm_c), _pad_row(k_norm_c), jnp.zeros((6, LANE), F32)], axis=0)

    full, got = {}, {}
    core = lax.axis_index("c").astype(jnp.int32).reshape(1)

    def pair_round(tag, blocks):
        names = [tag + "_%d" % i for i in range(len(blocks))]
        full.update(zip(names, blocks))
        got.update(zip(names, _pair_exchange(blocks, "grad_pair_exchange_" + tag)))
        return _ChipExchange([_pair_sum(full[n], got[n], core, "grad_pair_sum_" + n) for n in names])

    def early(g_w_out, g_w_kv):
        return pair_round("early", [g_w_out.reshape(N_CHIPS, D_MODEL // N_CHIPS, D_MODEL),
                                    g_w_kv.reshape(N_CHIPS, D_MODEL // N_CHIPS, 2 * C_HEADS * C_DIM)])

    def late(g_w_blocks_t):
        return pair_round("late", [g_w_blocks_t.reshape(N_CHIPS, W_IN_PAD, D_MODEL)])

    r = _local_step(x[0], mem[0], loss_target[0], norm_w, w_perm_t, w_blocks_t, conv_full, pa, pb, pc, mem_norm_w, None, None,
                    gather, (early, late))
    place = jnp.stack([chip, lax.axis_index("c")]).astype(jnp.int32)
    reduced = [_chip_sum(full[n], got[n], l, place, "grad_chip_sum_" + n)
               for n, l in zip(("late_0", "early_0", "early_1"), r["landed"])]
    g_w_in_t, g_w_out, g_w_kv = _pair_gather(reduced, [W_IN_BLOCK, D_MODEL // N_CHIPS, D_MODEL // N_CHIPS])

    d_pa, d_pb, d_pc = r["d_pa"], r["d_pb"], r["d_pc"]
    small_g = dict(norm_w=r["g_norm"], mem_norm_w=r["g_mem_norm"], o_norm_a=d_pa[4], q_norm_c=d_pc[0], k_norm_c=d_pc[1],
                   q_norm_b=d_pb[0, :B_DIM], k_norm_b=d_pb[1, :B_DIM], a_log_fwd=d_pa[0, :A_HEADS],
                   a_log_bwd=d_pa[1, :A_HEADS], dt_bias_fwd=d_pa[2, :A_HEADS], dt_bias_bwd=d_pa[3, :A_HEADS],
                   sink_b=d_pb[2, :B_HEADS])
    packed = _all_reduce_small(_pack_small(small_g, jnp.sum(r["loss_parts"][:, 0, 0]), r["g_conv"]))
    flat = packed.reshape(-1)
    loss = flat[SMALL_LOSS]
    conv_sum = flat[SMALL_CONV:].reshape(CONV_K, 3 * A_WIDTH)
    conv_cols = 3 * A_WIDTH // N_CHIPS
    g_conv = lax.dynamic_slice(conv_sum, (0, chip * conv_cols), (CONV_K, conv_cols))

    grads = _unpack_small(packed)
    grads["conv_w_a"] = g_conv
    delta, new_m, new_v = {}, {}, {}
    delta["conv_w_a"], new_m["conv_w_a"], new_v["conv_w_a"] = _adamw(conv_w_a[0], g_conv, m_conv_w_a[0], v_conv_w_a[0],
                                                                     "adamw_conv_w_a")
    for n, g in (("w_mem_kv", g_w_kv), ("w_out", g_w_out)):
        delta[n], new_m[n], new_v[n], grads[n] = _adamw(weights[n][0], g, mom1[n][0], mom2[n][0], "adamw_" + n, echo=True)
    stepped = _adamw(jnp.transpose(w_in[0]), g_w_in_t, jnp.transpose(m_w_in[0]), jnp.transpose(v_w_in[0]), "adamw_w_in",
                     echo=True)
    delta["w_in"], new_m["w_in"], new_v["w_in"], grads["w_in"] = (jnp.transpose(t) for t in stepped)
    d_s, m_s, v_s = _adamw(_pack_small(weights), packed, _pack_small(mom1), _pack_small(mom2), "adamw_small")
    d_s, m_s, v_s = _unpack_small(d_s), _unpack_small(m_s), _unpack_small(v_s)
    for n in SMALL_NAMES:
        delta[n], new_m[n], new_v[n] = d_s[n], m_s[n], v_s[n]

    def shaped(tree):
        return [tree[n].reshape(weights[n].shape) for n in WEIGHT_ORDER]

    return (loss, r["g_x"].reshape(x.shape), *shaped(grads), *shaped(delta), *shaped(new_m), *shaped(new_v))
```

```python
import jax
import jax.numpy as jnp
from jax import lax
from jax.experimental import pallas as pl
from jax.experimental.pallas import tpu as pltpu

F32 = jnp.float32
BF16 = jnp.bfloat16
HI = lax.Precision.HIGHEST
MESH = pl.DeviceIdType.MESH

D_MODEL = 2048
A_WIDTH = 1024
A_HEADS = 8
A_DIM = 128
CONV_K = 5
CHUNK = 64
B_HEADS = 8
B_KV = 2
B_DIM = 64
WINDOW = 128
C_HEADS = 4
C_DIM = 128
MEM_LEN = 256
ROPE_THETA = 10000.0
EPS = 1e-6
IN_WIDTH = 6432
N_CHIPS = 4
W_IN_BLOCK = IN_WIDTH // N_CHIPS
W_IN_PAD = 1664

LANE = 128
P_QA, P_KA, P_VA, P_ZA = 0, 1024, 2048, 3072
P_QB, P_ZB, P_QC, P_ZC = 4096, 4608, 5120, 5632
P_KB, P_VB, P_GT = 6144, 6272, 6400
P_WIDTH = 6656
O_GT, O_QB, O_KB, O_VB, O_ZB, O_QC, O_ZC = 4096, 4128, 4640, 4768, 4896, 5408, 5920

ADAM_LR, ADAM_B1, ADAM_B2, ADAM_EPS, ADAM_WD, ADAM_STEP = 0.001, 0.9, 0.999, 1e-08, 0.01, 10

VMEM_LIMIT = 56 * 1024 * 1024


def _params(sem=None):
    return pltpu.CompilerParams(dimension_semantics=sem, vmem_limit_bytes=VMEM_LIMIT)


def _dot(a, b, dims=(((1,), (0,)), ((), ())), precision=HI):
    return lax.dot_general(a, b, dims, precision=precision, preferred_element_type=F32)


_NN = (((1,), (0,)), ((), ()))
_NT = (((1,), (1,)), ((), ()))
_TN = (((0,), (0,)), ((), ()))


def _bdot(a, b, dims):
    return lax.dot_general(a.astype(BF16), b.astype(BF16), dims, preferred_element_type=F32)


@jax.custom_vjp
def _mm(a, b):
    return _bdot(a, b, _NN)


_mm.defvjp(lambda a, b: (_bdot(a, b, _NN), (a, b)),
           lambda res, ct: (_bdot(ct, res[1], _NT), _bdot(res[0], ct, _TN)))


@jax.custom_vjp
def _mm_nt(a, b):
    return _bdot(a, b, _NT)


_mm_nt.defvjp(lambda a, b: (_bdot(a, b, _NT), (a, b)),
              lambda res, ct: (_bdot(ct, res[1], _NN), _bdot(ct, res[0], _TN)))


@jax.custom_vjp
def _mm_tn(a, b):
    return _bdot(a, b, _TN)


_mm_tn.defvjp(lambda a, b: (_bdot(a, b, _TN), (a, b)),
              lambda res, ct: (_bdot(res[1], ct, _NT), _bdot(res[0], ct, _NN)))


def _rms(t, w):
    return t * lax.rsqrt(jnp.mean(t * t, axis=-1, keepdims=True) + EPS) * w


def _l2(t):
    return t * lax.rsqrt(jnp.sum(t * t, axis=-1, keepdims=True) + EPS)


def _silu(t):
    return t * jax.nn.sigmoid(t)


def _softplus(t):
    return jnp.maximum(t, 0.0) + jnp.log1p(jnp.exp(-jnp.abs(t)))


def _matmul(a, b, mode, out_dtype, name, tm=512, tn=512, tk=512):
    (m, k) = a.shape[::-1] if mode == "tn" else a.shape
    n = b.shape[0] if mode == "nt" else b.shape[1]
    tm, tn, tk = min(tm, m), min(tn, n), min(tk, k)
    assert m % tm == 0 and n % tn == 0 and k % tk == 0, (m, n, k, tm, tn, tk)
    if mode == "nn":
        a_spec = pl.BlockSpec((tm, tk), lambda i, j, kk: (i, kk))
        b_spec = pl.BlockSpec((tk, tn), lambda i, j, kk: (kk, j))
        dims = (((1,), (0,)), ((), ()))
    elif mode == "nt":
        a_spec = pl.BlockSpec((tm, tk), lambda i, j, kk: (i, kk))
        b_spec = pl.BlockSpec((tn, tk), lambda i, j, kk: (j, kk))
        dims = (((1,), (1,)), ((), ()))
    else:
        a_spec = pl.BlockSpec((tk, tm), lambda i, j, kk: (kk, i))
        b_spec = pl.BlockSpec((tk, tn), lambda i, j, kk: (kk, j))
        dims = (((0,), (0,)), ((), ()))
    nk = k // tk

    def body(a_ref, b_ref, o_ref, *scratch):
        if nk == 1:
            o_ref[...] = _bdot(a_ref[...], b_ref[...], dims).astype(out_dtype)
        else:
            acc_ref, kk = scratch[0], pl.program_id(2)

            @pl.when(kk == 0)
            def _():
                acc_ref[...] = jnp.zeros_like(acc_ref)

            acc_ref[...] += _bdot(a_ref[...], b_ref[...], dims)

            @pl.when(kk == nk - 1)
            def _():
                o_ref[...] = acc_ref[...].astype(out_dtype)

    return pl.pallas_call(
        body, name=name, grid=(m // tm, n // tn, nk),
        in_specs=[a_spec, b_spec], out_specs=pl.BlockSpec((tm, tn), lambda i, j, kk: (i, j)),
        out_shape=jax.ShapeDtypeStruct((m, n), out_dtype),
        scratch_shapes=[] if nk == 1 else [pltpu.VMEM((tm, tn), F32)],
        compiler_params=_params(("parallel", "parallel", "arbitrary")),
    )(a, b)


def _rms_fwd(x, w, tr=256):
    s, d = x.shape

    def body(x_ref, w_ref, o_ref):
        o_ref[...] = _rms(x_ref[...], w_ref[...]).astype(BF16)

    return pl.pallas_call(
        body, name="rms_fwd", grid=(s // tr,),
        in_specs=[pl.BlockSpec((tr, d), lambda i: (i, 0)), pl.BlockSpec((1, d), lambda i: (0, 0))],
        out_specs=pl.BlockSpec((tr, d), lambda i: (i, 0)),
        out_shape=jax.ShapeDtypeStruct((s, d), BF16), compiler_params=_params(("parallel",)),
    )(x, w)


def _input_grad(d_proj, w_t, x, w, dy, ride=None, tm=512, tk=512):
    s, k = d_proj.shape
    d = w_t.shape[1]
    tm = min(tm, s)
    nk = k // tk
    grid = (s // tm, nk)
    n_in = len(ride.operands) if ride else 0
    n_out = len(ride.out_shapes) if ride else 0

    def body(*refs):
        a_ref, b_ref, x_ref, w_ref, dy_ref = refs[:5]
        gx_ref, gw_ref = refs[5 + n_in:7 + n_in]
        acc_ref = refs[7 + n_in + n_out]
        riders = (refs[5:5 + n_in], refs[7 + n_in:7 + n_in + n_out], refs[8 + n_in + n_out:])
        kk = pl.program_id(1)
        step = pl.program_id(0) * nk + kk
        if ride:
            pl.when(step == 0)(lambda: ride.start(*riders))

        @pl.when(step == 0)
        def _():
            gw_ref[...] = jnp.zeros_like(gw_ref)

        @pl.when(kk == 0)
        def _():
            acc_ref[...] = jnp.zeros_like(acc_ref)

        acc_ref[...] += _bdot(a_ref[...], b_ref[...], _NN)

        @pl.when(kk == nk - 1)
        def _():
            _, vjp = jax.vjp(_rms, x_ref[...], w_ref[...])
            dx, dw = vjp(acc_ref[...])
            gx_ref[...] = dy_ref[...] + dx
            gw_ref[...] += dw

        if ride:
            pl.when(step == grid[0] * nk - 1)(lambda: ride.finish(*riders))

    row = pl.BlockSpec((tm, d), lambda i, kk: (i, 0))
    vec = pl.BlockSpec((1, d), lambda i, kk: (0, 0))
    return pl.pallas_call(
        body, name="input_grad", grid=grid,
        in_specs=[pl.BlockSpec((tm, tk), lambda i, kk: (i, kk)), pl.BlockSpec((tk, d), lambda i, kk: (kk, 0)), row, vec, row]
        + [HBM] * n_in,
        out_specs=[row, vec] + [HBM] * n_out,
        out_shape=[jax.ShapeDtypeStruct((s, d), F32), jax.ShapeDtypeStruct((1, d), F32)]
        + (list(ride.out_shapes) if ride else []),
        scratch_shapes=[pltpu.VMEM((tm, d), F32)] + (list(ride.scratch_shapes) if ride else []),
        compiler_params=_params(("arbitrary", "arbitrary")),
    )(d_proj, w_t, x, w, dy, *(ride.operands if ride else []))


def _out_loss(mixed, w_out, x, target, tm=1024, tn=512):
    s, d = x.shape
    tm = min(tm, s)
    ni, nj = s // tm, d // tn

    def body(m_ref, w_ref, x_ref, t_ref, dy_ref, dyb_ref, l_ref):
        err = x_ref[...] + _bdot(m_ref[...], w_ref[...], _NN) - t_ref[...]
        dy = err * (1.0 / d)
        dy_ref[...] = dy
        dyb_ref[...] = dy.astype(BF16)
        l_ref[...] = jnp.full(l_ref.shape, 0.5 * jnp.sum(jnp.sum(err * err, axis=1, keepdims=True) * (1.0 / d)), F32)

    tile = pl.BlockSpec((tm, tn), lambda i, j: (i, j))
    return pl.pallas_call(
        body, name="out_loss", grid=(ni, nj),
        in_specs=[pl.BlockSpec((tm, mixed.shape[1]), lambda i, j: (i, 0)),
                  pl.BlockSpec((mixed.shape[1], tn), lambda i, j: (0, j)), tile, tile],
        out_specs=[tile, tile, pl.BlockSpec((1, 8, LANE), lambda i, j: (i * nj + j, 0, 0))],
        out_shape=[jax.ShapeDtypeStruct((s, d), F32), jax.ShapeDtypeStruct((s, d), BF16),
                   jax.ShapeDtypeStruct((ni * nj, 8, LANE), F32)],
        compiler_params=_params(("parallel", "parallel")),
    )(mixed, w_out, x, target)


def _shift_rows(t, s):
    if s == 0:
        return t
    n = t.shape[0]
    rolled = pltpu.roll(t, (-s) % n, axis=0)
    idx = lax.broadcasted_iota(jnp.int32, t.shape, 0) + s
    return jnp.where((idx >= 0) & (idx < n), rolled, 0.0)


CONV_FWD_COLS = 512
CONV_BWD_COLS = 128


def _conv_fwd(proj, conv_w):
    s = proj.shape[0]
    cols, split = CONV_FWD_COLS, A_WIDTH // CONV_FWD_COLS
    nblk = 3 * A_WIDTH // cols

    def body(x_ref, w_ref, o_ref):
        x = x_ref[...]
        acc = jnp.zeros_like(x)
        for j in range(CONV_K):
            acc = acc + w_ref[j:j + 1, :] * _shift_rows(x, j - CONV_K // 2)
        o_ref[...] = acc

    return pl.pallas_call(
        body, name="conv_fwd", grid=(nblk,),
        in_specs=[pl.BlockSpec((s, cols), lambda i: (0, i)), pl.BlockSpec((CONV_K, cols), lambda i: (0, i))],
        out_specs=pl.BlockSpec((None, s, cols), lambda i: (i // split, 0, i % split)),
        out_shape=jax.ShapeDtypeStruct((3, s, A_WIDTH), F32), compiler_params=_params(("parallel",)),
    )(proj, conv_w)


def _conv_bwd(proj, conv_w, d_c):
    s = proj.shape[0]
    cols, split = CONV_BWD_COLS, A_WIDTH // CONV_BWD_COLS
    nblk = 3 * A_WIDTH // cols

    def body(x_ref, w_ref, g_ref, dx_ref, dw_ref):
        x, g = x_ref[...], g_ref[...]
        acc = jnp.zeros_like(x)
        for j in range(CONV_K):
            off = j - CONV_K // 2
            acc = acc + w_ref[j:j + 1, :] * _shift_rows(g, -off)
            dw_ref[j:j + 1, :] = jnp.sum(_shift_rows(x, off) * g, axis=0, keepdims=True)
        dx_ref[...] = acc.astype(BF16)

    col = pl.BlockSpec((s, cols), lambda i: (0, i))
    wsp = pl.BlockSpec((CONV_K, cols), lambda i: (0, i))
    dsp = pl.BlockSpec((None, s, cols), lambda i: (i // split, 0, i % split))
    return pl.pallas_call(
        body, name="conv_bwd", grid=(nblk,), in_specs=[col, wsp, dsp], out_specs=[col, wsp],
        out_shape=[jax.ShapeDtypeStruct((s, 3 * A_WIDTH), BF16), jax.ShapeDtypeStruct((CONV_K, 3 * A_WIDTH), F32)],
        compiler_params=_params(("parallel",)),
    )(proj, conv_w, d_c)


A_FWD_HEADS = 4
A_BWD_HEADS = 4


def _neumann_inverse(a):
    c = a.shape[-1]
    eye = (lax.broadcasted_iota(jnp.int32, (c, c), 0) == lax.broadcasted_iota(jnp.int32, (c, c), 1)).astype(F32)
    tinv = eye + a
    p = a
    for _ in range(5):
        p = _mm(p, p)
        tinv = tinv + _mm(tinv, p)
    return tinv


@jax.custom_vjp
def _unit_inverse(a):
    return _neumann_inverse(a)


def _unit_inverse_fwd(a):
    tinv = _neumann_inverse(a)
    return tinv, tinv


def _unit_inverse_bwd(tinv, ct):
    return (_bdot(_bdot(tinv, ct, _TN), tinv, _NT),)


_unit_inverse.defvjp(_unit_inverse_fwd, _unit_inverse_bwd)


@jax.custom_vjp
def _known_inverse(a, tinv):
    return tinv


_known_inverse.defvjp(lambda a, tinv: (tinv, tinv),
                      lambda tinv, ct: (_unit_inverse_bwd(tinv, ct)[0], jnp.zeros_like(tinv)))


def _a_chain(st, cq, ck, cv, alpha, beta_raw, a_log, dt_b, incl, strict, last, kept=None):
    c = CHUNK
    gb = -jnp.exp(a_log) * _softplus(alpha + dt_b)
    bb = jax.nn.sigmoid(beta_raw)
    q = _l2(_silu(cq)) * (A_DIM ** -0.5)
    k = _l2(_silu(ck))
    v = _silu(cv)

    gc = _dot(incl, jnp.broadcast_to(gb, (c, LANE)))
    tot = jnp.sum(gc * last, axis=0, keepdims=True)
    m1 = gc[:, :c]
    decay = incl * jnp.exp(incl * (m1 - m1.T))
    kb = k * bb
    vb = v * bb
    a = -(strict * decay * _mm_nt(kb, k))
    tinv = _unit_inverse(a) if kept is None else _known_inverse(a, kept)
    eg = jnp.exp(gc)
    u = _mm(tinv, vb)
    w = _mm(tinv, kb * eg)
    qk = _mm_nt(q, k) * decay
    v_new = u - _mm(w, st)
    o = _mm(q * eg, st) + _mm(qk, v_new)
    st_new = st * jnp.exp(tot) + _mm_tn(k * jnp.exp(tot - gc), v_new)
    return st_new, o, tinv


def _a_step(sts, cq, ck, cv, gts, pa, h0, kept=None):
    c = CHUNK
    lane = lax.broadcasted_iota(jnp.int32, (1, LANE), 1)
    ii = lax.broadcasted_iota(jnp.int32, (c, c), 0)
    jj = lax.broadcasted_iota(jnp.int32, (c, c), 1)
    row = lax.broadcasted_iota(jnp.int32, (c, 1), 0)

    def pick(t, col):
        return jnp.sum(jnp.where(lane == col, t, 0.0), axis=1, keepdims=True)

    alpha, beta_raw, a_log, dt_b, incl, strict, last = [], [], [], [], [], [], []
    for b in range(sts.shape[0]):
        h, rev = h0 + b // 2, b % 2
        alpha.append(pick(gts[b], h + 8 * rev))
        beta_raw.append(pick(gts[b], h + 16 + 8 * rev))
        a_log.append(pick(pa[rev:rev + 1, :], h))
        dt_b.append(pick(pa[2 + rev:3 + rev, :], h))
        incl.append(((ii <= jj) if rev else (ii >= jj)).astype(F32))
        strict.append(((ii < jj) if rev else (ii > jj)).astype(F32))
        last.append((row == (0 if rev else c - 1)).astype(F32))
    stack = lambda ts: jnp.concatenate([t[None] for t in ts], axis=0)
    return jax.vmap(_a_chain)(sts, cq, ck, cv, stack(alpha), stack(beta_raw), stack(a_log), stack(dt_b),
                              stack(incl), stack(strict), stack(last), kept)


def _a_final(o, za, pa):
    outs = []
    for j in range(o.shape[1] // A_DIM):
        ln = slice(j * A_DIM, (j + 1) * A_DIM)
        outs.append(_rms(o[:, ln], pa[4:5, :]) * _silu(za[:, ln]))
    return jnp.concatenate(outs, axis=1)


def _a_tiles(n, nchunk, heads):
    tiles = []
    for b in range(2 * heads):
        i = (nchunk - 1 - n) if b % 2 else n
        tiles.append((i, pl.ds(pl.multiple_of(i * CHUNK, CHUNK), CHUNK), slice((b // 2) * A_DIM, (b // 2 + 1) * A_DIM)))
    return tiles


def _a_load(tiles, c_ref, gt_ref):
    cq, ck, cv = (jnp.stack([c_ref[r, sl, ln] for _, sl, ln in tiles], axis=0) for r in range(3))
    return cq, ck, cv, jnp.stack([gt_ref[sl, :] for _, sl, _ in tiles], axis=0)


SCAN_STEPS_PER_TRIP = 4


def _loop_unrolled(n, step, init):
    per = SCAN_STEPS_PER_TRIP
    assert n % per == 0

    def trip(m, carry):
        for j in range(per):
            carry = step(per * m + j, carry, j % 2)
        return carry

    return lax.fori_loop(0, n // per, trip, init)


def _a_scan(h0, heads, nchunk, c_ref, gt_ref, pa, of_ref, ob_ref, s_ref, t_ref):
    def step(n, sts, parity):
        tiles = _a_tiles(n, nchunk, heads)
        sts_new, o, tinv = _a_step(sts, *_a_load(tiles, c_ref, gt_ref), pa, h0)
        for b, (i, sl, ln) in enumerate(tiles):
            s_ref[b, i] = sts[b]
            t_ref[b, i] = tinv[b]
            (ob_ref if b % 2 else of_ref)[sl, ln] = o[b]
        return sts_new

    _loop_unrolled(nchunk, step, jnp.zeros((2 * heads, A_DIM, A_DIM), F32))


def _a_specs(s, heads):
    wide = heads * A_DIM
    once = pl.Buffered(1)
    trio = pl.BlockSpec((3, s, wide), lambda g: (0, 0, g), pipeline_mode=once)
    gates = pl.BlockSpec((s, LANE), lambda g: (0, P_GT // LANE))
    small = pl.BlockSpec((8, LANE), lambda g: (0, 0))

    def cols(base):
        return pl.BlockSpec((s, wide), lambda g: (0, base // wide + g), pipeline_mode=once)

    state = pl.BlockSpec((2 * heads, s // CHUNK, A_DIM, A_DIM), lambda g: (g, 0, 0, 0), pipeline_mode=once)
    kept = pl.BlockSpec((2 * heads, s // CHUNK, CHUNK, CHUNK), lambda g: (g, 0, 0, 0), pipeline_mode=once)
    return wide, trio, gates, small, cols, state, kept


def _delta_fwd(cqkv, proj, pa, ride=None):
    s = cqkv.shape[1]
    nchunk = s // CHUNK
    heads = A_FWD_HEADS
    steps = A_HEADS // heads
    wide, trio, gates, small, cols, state, kept = _a_specs(s, heads)
    n_in = len(ride.operands) if ride else 0
    n_out = len(ride.out_shapes) if ride else 0

    def body(*refs):
        c_ref, gt_ref, za_ref, pa_ref = refs[:4]
        out_ref, o_ref, s_ref, t_ref = refs[4 + n_in:8 + n_in]
        ob_ref = refs[8 + n_in + n_out]
        riders = (refs[4:4 + n_in], refs[8 + n_in:8 + n_in + n_out], refs[9 + n_in + n_out:])
        g = pl.program_id(0)
        if ride:
            pl.when(g == 0)(lambda: ride.start(*riders))
            pl.when(g == steps - 1)(lambda: ride.middle(*riders))
        h0 = g * heads
        pa_v = pa_ref[...]
        _a_scan(h0, heads, nchunk, c_ref, gt_ref, pa_v, o_ref, ob_ref, s_ref, t_ref)
        o_ref[...] += ob_ref[...]
        out_ref[...] = _a_final(o_ref[...], za_ref[...], pa_v).astype(BF16)
        if ride:
            pl.when(g == steps - 1)(lambda: ride.finish(*riders))

    assert steps > 1
    return pl.pallas_call(
        body, name="delta_fwd", grid=(steps,),
        in_specs=[trio, gates, cols(P_ZA), small] + [HBM] * n_in,
        out_specs=[cols(0), cols(0), state, kept] + [HBM] * n_out,
        out_shape=[jax.ShapeDtypeStruct((s, D_MODEL), BF16),
                   jax.ShapeDtypeStruct((s, A_WIDTH), F32),
                   jax.ShapeDtypeStruct((2 * A_HEADS, nchunk, A_DIM, A_DIM), F32),
                   jax.ShapeDtypeStruct((2 * A_HEADS, nchunk, CHUNK, CHUNK), F32)]
        + (list(ride.out_shapes) if ride else []),
        scratch_shapes=[pltpu.VMEM((s, wide), F32)] + (list(ride.scratch_shapes) if ride else []),
        compiler_params=_params(("arbitrary",)),
    )(cqkv, proj, proj, pa, *(ride.operands if ride else []))


def _delta_out_bwd(o_sum, proj, pa, d_mixed, tr=256):
    s = o_sum.shape[0]

    def body(o_ref, za_ref, pa_ref, dm_ref, do_ref, dza_ref, dpa_ref):
        @pl.when(pl.program_id(0) == 0)
        def _():
            dpa_ref[...] = jnp.zeros_like(dpa_ref)

        _, vjp = jax.vjp(_a_final, o_ref[...], za_ref[...], pa_ref[...])
        d_o, d_za, dpa = vjp(dm_ref[...].astype(F32))
        do_ref[...] = d_o
        dza_ref[...] = d_za.astype(BF16)
        dpa_ref[...] += dpa

    def rows(col):
        return pl.BlockSpec((tr, A_WIDTH), lambda i: (i, col))

    small = pl.BlockSpec((8, LANE), lambda i: (0, 0))
    return pl.pallas_call(
        body, name="delta_out_bwd", grid=(s // tr,), in_specs=[rows(0), rows(P_ZA // A_WIDTH), small, rows(0)],
        out_specs=[rows(0), rows(0), small],
        out_shape=[jax.ShapeDtypeStruct((s, A_WIDTH), F32), jax.ShapeDtypeStruct((s, A_WIDTH), BF16),
                   jax.ShapeDtypeStruct((8, LANE), F32)],
        compiler_params=_params(("arbitrary",)),
    )(o_sum, proj, pa, d_mixed)


def _delta_bwd(cqkv, proj, pa, d_o, states, inverses, ride=None):
    s = cqkv.shape[1]
    nchunk = s // CHUNK
    heads = A_BWD_HEADS
    steps = A_HEADS // heads
    wide, trio, gates, small, cols, _, _ = _a_specs(s, heads)
    n_in = len(ride.operands) if ride else 0
    n_out = len(ride.out_shapes) if ride else 0

    def body(*refs):
        c_ref, gt_ref, pa_ref, do_ref, s_hbm, t_hbm = refs[:6]
        dc_ref, dgt_ref, dpa_ref = refs[6 + n_in:9 + n_in]
        s_buf, t_buf, s_sems = refs[9 + n_in + n_out:12 + n_in + n_out]
        riders = (refs[6:6 + n_in], refs[9 + n_in:9 + n_in + n_out], refs[12 + n_in + n_out:])
        if ride:
            pl.when(pl.program_id(0) == 0)(lambda: ride.start(*riders))
        h0 = pl.program_id(0) * heads
        pa_v = pa_ref[...]

        @pl.when(h0 == 0)
        def _():
            dgt_ref[...] = jnp.zeros_like(dgt_ref)
            dpa_ref[...] = jnp.zeros_like(dpa_ref)

        dc_ref[...] = jnp.zeros_like(dc_ref)

        def state_copies(n, slot):
            tiles = _a_tiles(nchunk - 1 - n, nchunk, heads)
            return ([pltpu.make_async_copy(s_hbm.at[2 * h0 + b, i], s_buf.at[slot, b], s_sems.at[0, slot, b])
                     for b, (i, _, _) in enumerate(tiles)]
                    + [pltpu.make_async_copy(t_hbm.at[2 * h0 + b, i], t_buf.at[slot, b], s_sems.at[1, slot, b])
                       for b, (i, _, _) in enumerate(tiles)])

        for cp in state_copies(0, 0):
            cp.start()

        def step(n, carry, parity):
            d_sts, dpa = carry
            tiles = _a_tiles(nchunk - 1 - n, nchunk, heads)
            for cp in state_copies(n, parity):
                cp.wait()

            @pl.when(n + 1 < nchunk)
            def _():
                for cp in state_copies(n + 1, 1 - parity):
                    cp.start()

            sts, kept = s_buf[parity], t_buf[parity]
            d_o_t = jnp.stack([do_ref[sl, ln] for _, sl, ln in tiles], axis=0)
            _, vjp_c = jax.vjp(lambda *a: _a_step(*a, h0, kept)[:2], sts, *_a_load(tiles, c_ref, gt_ref), pa_v)
            d_prev, dcq, dck, dcv, dgts, dpa_i = vjp_c((d_sts, d_o_t))
            for b, (_, sl, ln) in enumerate(tiles):
                for r, dc in enumerate((dcq, dck, dcv)):
                    dc_ref[r, sl, ln] += dc[b]
                dgt_ref[sl, :] += dgts[b]
            return d_prev, dpa + dpa_i

        init = (jnp.zeros((2 * heads, A_DIM, A_DIM), F32), jnp.zeros((8, LANE), F32))
        _, dpa_out = lax.fori_loop(0, nchunk, lambda n, carry: step(n, carry, n % 2), init)
        dpa_ref[...] += dpa_out
        if ride:
            pl.when(pl.program_id(0) == steps - 1)(lambda: ride.finish(*riders))

    fixed = pl.BlockSpec((s, LANE), lambda g: (0, 0))
    return pl.pallas_call(
        body, name="delta_bwd", grid=(steps,),
        in_specs=[trio, gates, small, cols(0), pl.BlockSpec(memory_space=pl.ANY), pl.BlockSpec(memory_space=pl.ANY)]
        + [HBM] * n_in,
        out_specs=[trio, fixed, small] + [HBM] * n_out,
        out_shape=[jax.ShapeDtypeStruct((3, s, A_WIDTH), F32), jax.ShapeDtypeStruct((s, LANE), F32),
                   jax.ShapeDtypeStruct((8, LANE), F32)] + (list(ride.out_shapes) if ride else []),
        scratch_shapes=[pltpu.VMEM((2, 2 * heads, A_DIM, A_DIM), F32), pltpu.VMEM((2, 2 * heads, CHUNK, CHUNK), F32),
                        pltpu.SemaphoreType.DMA((2, 2, 2 * heads))]
        + (list(ride.scratch_shapes) if ride else []),
        compiler_params=_params(("arbitrary",)),
    )(cqkv, proj, pa, d_o, states, inverses, *(ride.operands if ride else []))


def _rope_tables(s):
    inv = ROPE_THETA ** (-jnp.arange(0, B_DIM, 2, dtype=F32) / B_DIM)
    ang = jnp.arange(s, dtype=F32)[:, None] * inv[None, :]
    cos, sin = jnp.cos(ang), jnp.sin(ang)
    return jnp.concatenate([cos, cos], axis=1), jnp.concatenate([-sin, sin], axis=1)


def _b_block(q_t, z_t, k3, v3, cos_q, sin_q, cos_k, sin_k, pb, n, nb):
    w = WINDOW
    def swap(t):
        return jnp.concatenate([t[:, B_DIM // 2:], t[:, :B_DIM // 2]], axis=1)

    grp = B_HEADS // B_KV
    qi = lax.broadcasted_iota(jnp.int32, (grp * w, 3 * w), 0) & (w - 1)
    kj = lax.broadcasted_iota(jnp.int32, (grp * w, 3 * w), 1)
    kpos = kj + (n - 1) * w
    mask = (jnp.abs(kj - w - qi) <= w) & (kpos >= 0) & (kpos < nb * w)
    lane = lax.broadcasted_iota(jnp.int32, (1, LANE), 1)
    qn, kn = pb[0:1, :B_DIM], pb[1:2, :B_DIM]
    cos_g = jnp.concatenate([cos_q] * grp, axis=0)
    sin_g = jnp.concatenate([sin_q] * grp, axis=0)
    def group(q, k, v, sink):
        k = _rms(k, kn)
        k = k * cos_k + swap(k) * sin_k
        q = _rms(q, qn)
        q = q * cos_g + swap(q) * sin_g
        s = _mm_nt(q, k) * (B_DIM ** -0.5)
        s = jnp.where(mask, s, -jnp.inf)
        m = jnp.maximum(jnp.max(s, axis=1, keepdims=True), sink)
        p = jnp.exp(s - m)
        p = p / (jnp.sum(p, axis=1, keepdims=True) + jnp.exp(sink - m))
        return _mm(p, v)

    stack = lambda ts: jnp.concatenate([t[None] for t in ts], axis=0)
    qs, ks, vs, sinks = [], [], [], []
    for hk in range(B_KV):
        heads = [hk * grp + g for g in range(grp)]
        ks.append(k3[:, hk * B_DIM:(hk + 1) * B_DIM])
        vs.append(v3[:, hk * B_DIM:(hk + 1) * B_DIM])
        qs.append(jnp.concatenate([q_t[:, hq * B_DIM:(hq + 1) * B_DIM] for hq in heads], axis=0))
        sinks.append(jnp.concatenate(
            [jnp.broadcast_to(jnp.sum(jnp.where(lane == hq, pb[2:3, :], 0.0), axis=1, keepdims=True), (w, 1))
             for hq in heads], axis=0))
    o = jax.vmap(group)(stack(qs), stack(ks), stack(vs), stack(sinks))
    outs = [o[hk, g * w:(g + 1) * w, :] for hk in range(B_KV) for g in range(grp)]
    return jnp.concatenate(outs, axis=1) * _silu(z_t)


def _b_specs(s):
    nb = s // WINDOW
    qsp = pl.BlockSpec((WINDOW, 512), lambda n: (n, P_QB // 512))
    zsp = pl.BlockSpec((WINDOW, 512), lambda n: (n, P_ZB // 512))

    def three(col, width):
        return [pl.BlockSpec((WINDOW, width), lambda n: (jnp.maximum(n - 1, 0), col)),
                pl.BlockSpec((WINDOW, width), lambda n: (n, col)),
                pl.BlockSpec((WINDOW, width), lambda n: (jnp.minimum(n + 1, nb - 1), col))]

    tab = pl.BlockSpec((WINDOW, B_DIM), lambda n: (n, 0))
    small = pl.BlockSpec((8, LANE), lambda n: (0, 0))
    specs = [qsp, zsp] + three(P_KB // LANE, LANE) + three(P_VB // LANE, LANE) + [tab, tab] + three(0, B_DIM) + three(0, B_DIM) + [small]
    return nb, specs


def _b_args(proj, cos2, sin2, pb):
    return (proj, proj, proj, proj, proj, proj, proj, proj, cos2, sin2, cos2, cos2, cos2, sin2, sin2, sin2, pb)


def _b_load(refs):
    (q_ref, z_ref, kp, kc, kx, vp, vc, vx, cq, sq, ckp, ckc, ckx, skp, skc, skx, pb_ref) = refs
    cat = lambda *r: jnp.concatenate([t[...] for t in r], axis=0)
    return (q_ref[...], z_ref[...], cat(kp, kc, kx), cat(vp, vc, vx), cq[...], sq[...], cat(ckp, ckc, ckx),
            cat(skp, skc, skx), pb_ref[...])


def _attn_b_fwd(proj, cos2, sin2, pb, mixed):
    s = proj.shape[0]
    nb, specs = _b_specs(s)

    def body(*refs):
        o_ref = refs[-1]
        args = _b_load(refs[:-2])
        o_ref[...] = _b_block(*args, pl.program_id(0), nb).astype(BF16)

    return pl.pallas_call(
        body, name="attn_b_fwd", grid=(nb,), in_specs=specs + [pl.BlockSpec(memory_space=pl.ANY)],
        out_specs=pl.BlockSpec((WINDOW, 512), lambda n: (n, A_WIDTH // 512)),
        out_shape=jax.ShapeDtypeStruct(mixed.shape, mixed.dtype), input_output_aliases={len(specs): 0},
        compiler_params=_params(("parallel",)),
    )(*_b_args(proj, cos2, sin2, pb), mixed)


def _attn_b_bwd(proj, cos2, sin2, pb, d_mixed):
    s = proj.shape[0]
    nb, specs = _b_specs(s)
    w = WINDOW

    def body(*refs):
        dm_ref, dq_ref, dz_ref, dk_ref, dv_ref, dpb_ref = refs[-6:]
        n = pl.program_id(0)
        q_t, z_t, k3, v3, cq, sq, ck, sk, pb_v = _b_load(refs[:-6])

        @pl.when(n == 0)
        def _():
            dk_ref[...] = jnp.zeros_like(dk_ref)
            dv_ref[...] = jnp.zeros_like(dv_ref)
            dpb_ref[...] = jnp.zeros_like(dpb_ref)

        def f(q_, z_, k_, v_, pb_):
            return _b_block(q_, z_, k_, v_, cq, sq, ck, sk, pb_, n, nb)

        _, vjp = jax.vjp(f, q_t, z_t, k3, v3, pb_v)
        dq, dz, dk3, dv3, dpb = vjp(dm_ref[...])
        dq_ref[...] = dq.astype(BF16)
        dz_ref[...] = dz.astype(BF16)
        dpb_ref[...] += dpb

        def add(j, cond):
            @pl.when(cond)
            def _():
                rows = pl.ds(pl.multiple_of((n - 1 + j) * w, w), w)
                dk_ref[rows, :] += dk3[j * w:(j + 1) * w, :]
                dv_ref[rows, :] += dv3[j * w:(j + 1) * w, :]

        add(0, n > 0)
        add(1, n >= 0)
        add(2, n < nb - 1)

    blk = pl.BlockSpec((w, 512), lambda n: (n, 0))
    whole = pl.BlockSpec((s, LANE), lambda n: (0, 0))
    small = pl.BlockSpec((8, LANE), lambda n: (0, 0))
    return pl.pallas_call(
        body, name="attn_b_bwd", grid=(nb,),
        in_specs=specs + [pl.BlockSpec((w, 512), lambda n: (n, 2))],
        out_specs=[blk, blk, whole, whole, small],
        out_shape=[jax.ShapeDtypeStruct((s, 512), BF16), jax.ShapeDtypeStruct((s, 512), BF16),
                   jax.ShapeDtypeStruct((s, LANE), F32), jax.ShapeDtypeStruct((s, LANE), F32),
                   jax.ShapeDtypeStruct((8, LANE), F32)],
        compiler_params=_params(("arbitrary",)),
    )(*_b_args(proj, cos2, sin2, pb), d_mixed)


def _mem_kv_fwd(mem, mem_norm_w, w_kv):
    def body(mem_ref, nw_ref, w_ref, kv_ref):
        mn = _rms(mem_ref[...], nw_ref[...]).astype(BF16)
        kv_ref[...] = jnp.dot(mn, w_ref[...], preferred_element_type=F32)

    return pl.pallas_call(
        body, name="mem_kv_fwd", out_shape=jax.ShapeDtypeStruct((MEM_LEN, 2 * C_HEADS * C_DIM), F32),
        compiler_params=_params(),
    )(mem, mem_norm_w, w_kv)


def _mem_kv_bwd(mem, mem_norm_w, w_kv, d_kv):
    def body(mem_ref, nw_ref, w_ref, g_ref, gw_ref, gn_ref):
        mn, vjp = jax.vjp(_rms, mem_ref[...], nw_ref[...])
        g = g_ref[...].astype(BF16)
        gw_ref[...] = lax.dot_general(mn.astype(BF16), g, (((0,), (0,)), ((), ())), preferred_element_type=F32)
        d_mn = lax.dot_general(g, w_ref[...], (((1,), (1,)), ((), ())), preferred_element_type=F32)
        gn_ref[...] = vjp(d_mn)[1]

    return pl.pallas_call(
        body, name="mem_kv_bwd",
        out_shape=[jax.ShapeDtypeStruct((D_MODEL, 2 * C_HEADS * C_DIM), F32), jax.ShapeDtypeStruct((1, D_MODEL), F32)],
        compiler_params=_params(),
    )(mem, mem_norm_w, w_kv, d_kv)


def _c_tile(q_t, z_t, kvm, pc):
    width = C_HEADS * C_DIM

    def head(q, k, v):
        q = _rms(q, pc[0:1, :])
        k = _rms(k, pc[1:2, :])
        s = _mm_nt(q, k) * (C_DIM ** -0.5)
        p = jnp.exp(s - jnp.max(s, axis=1, keepdims=True))
        p = p / jnp.sum(p, axis=1, keepdims=True)
        return _mm(p, v)

    def stack(t, first):
        return jnp.concatenate([t[None, :, first + h * C_DIM:first + (h + 1) * C_DIM] for h in range(C_HEADS)], axis=0)

    o = jax.vmap(head)(stack(q_t, 0), stack(kvm, 0), stack(kvm, width))
    return jnp.concatenate([o[h] for h in range(C_HEADS)], axis=1) * _silu(z_t)


def _attn_c_fwd(proj, kvm, pc, mixed, tq=256):
    s = proj.shape[0]

    def body(q_ref, z_ref, kv_ref, pc_ref, mixed_ref, o_ref):
        o_ref[...] = _c_tile(q_ref[...], z_ref[...], kv_ref[...], pc_ref[...]).astype(BF16)

    return pl.pallas_call(
        body, name="attn_c_fwd", grid=(s // tq,),
        in_specs=[pl.BlockSpec((tq, 512), lambda i: (i, P_QC // 512)), pl.BlockSpec((tq, 512), lambda i: (i, P_ZC // 512)),
                  pl.BlockSpec(kvm.shape, lambda i: (0, 0)), pl.BlockSpec((8, LANE), lambda i: (0, 0)),
                  pl.BlockSpec(memory_space=pl.ANY)],
        out_specs=pl.BlockSpec((tq, 512), lambda i: (i, (A_WIDTH + 512) // 512)),
        out_shape=jax.ShapeDtypeStruct(mixed.shape, mixed.dtype), input_output_aliases={4: 0},
        compiler_params=_params(("parallel",)),
    )(proj, proj, kvm, pc, mixed)


def _attn_c_bwd(proj, kvm, pc, d_mixed, tq=256):
    s = proj.shape[0]

    def body(q_ref, z_ref, kv_ref, pc_ref, dm_ref, dq_ref, dz_ref, dkv_ref, dpc_ref):
        @pl.when(pl.program_id(0) == 0)
        def _():
            dkv_ref[...] = jnp.zeros_like(dkv_ref)
            dpc_ref[...] = jnp.zeros_like(dpc_ref)

        _, vjp = jax.vjp(_c_tile, q_ref[...], z_ref[...], kv_ref[...], pc_ref[...])
        dq, dz, dkv, dpc = vjp(dm_ref[...])
        dq_ref[...] = dq.astype(BF16)
        dz_ref[...] = dz.astype(BF16)
        dkv_ref[...] += dkv
        dpc_ref[...] += dpc

    blk = pl.BlockSpec((tq, 512), lambda i: (i, 0))
    kvs = pl.BlockSpec(kvm.shape, lambda i: (0, 0))
    small = pl.BlockSpec((8, LANE), lambda i: (0, 0))
    return pl.pallas_call(
        body, name="attn_c_bwd", grid=(s // tq,),
        in_specs=[pl.BlockSpec((tq, 512), lambda i: (i, P_QC // 512)), pl.BlockSpec((tq, 512), lambda i: (i, P_ZC // 512)),
                  kvs, small, pl.BlockSpec((tq, 512), lambda i: (i, 3))],
        out_specs=[blk, blk, kvs, small],
        out_shape=[jax.ShapeDtypeStruct((s, 512), BF16), jax.ShapeDtypeStruct((s, 512), BF16),
                   jax.ShapeDtypeStruct(kvm.shape, F32), jax.ShapeDtypeStruct((8, LANE), F32)],
        compiler_params=_params(("arbitrary",)),
    )(proj, proj, kvm, pc, d_mixed)


def _pad_row(v, width=LANE):
    v = v.reshape(1, -1)
    return jnp.pad(v, ((0, 0), (0, width - v.shape[1])))


def _local_step(x, mem, target, norm_w, w_perm_t, w_blocks_t, conv_w, pa, pb, pc, mem_norm_w, w_kv, w_out, gather=None,
                exchange=None):
    s = x.shape[0]
    cos2, sin2 = _rope_tables(s)
    hn = _rms_fwd(x, norm_w)
    wide = dict(tm=1024, tn=512, tk=2048)
    proj = _matmul(hn, w_perm_t, "nt", F32, "mm_proj", **wide)
    cqkv = _conv_fwd(proj, conv_w)
    if gather is None:
        mixed, o_sum, states, inverses = _delta_fwd(cqkv, proj, pa)
    else:
        mixed, o_sum, states, inverses, *arrived = _delta_fwd(cqkv, proj, pa, gather[0])
        w_out, w_kv = gather[1](*arrived)
    mixed = _attn_b_fwd(proj, cos2, sin2, pb, mixed)
    kvm = _mem_kv_fwd(mem, mem_norm_w, w_kv)
    mixed = _attn_c_fwd(proj, kvm, pc, mixed)
    dy, dyb, loss_parts = _out_loss(mixed, w_out, x, target)

    d_mixed = _matmul(dyb, w_out, "nt", F32, "mm_dmixed", **wide)
    g_w_out = _matmul(mixed, dyb, "tn", F32, "mm_gwout", **wide)
    d_qc, d_zc, d_kvm, d_pc = _attn_c_bwd(proj, kvm, pc, d_mixed)
    g_w_kv, g_mem_norm = _mem_kv_bwd(mem, mem_norm_w, w_kv, d_kvm)
    d_qb, d_zb, d_kb, d_vb, d_pb = _attn_b_bwd(proj, cos2, sin2, pb, d_mixed)
    d_o, d_za, d_pa_out = _delta_out_bwd(o_sum, proj, pa, d_mixed)
    early = exchange[0](g_w_out, g_w_kv) if exchange else None
    d_c, d_gt, d_pa_scan, *landed_early = _delta_bwd(cqkv, proj, pa, d_o, states, inverses, early)
    d_pa = d_pa_out + d_pa_scan
    d_qkv, g_conv = _conv_bwd(proj, conv_w, d_c)
    d_proj = _cotangent_blocks(d_qkv, d_za, d_gt, d_qb, d_kb, d_vb, d_zb, d_qc, d_zc)
    g_w_blocks_t = _matmul(d_proj, hn, "tn", F32, "mm_gwin", tm=512, tn=2048, tk=2048)
    late = exchange[1](g_w_blocks_t) if exchange else None
    g_x, g_norm, *landed_late = _input_grad(d_proj, w_blocks_t, x, norm_w, dy, late)
    return dict(loss_parts=loss_parts, g_x=g_x, g_norm=g_norm, g_w_blocks_t=g_w_blocks_t, g_conv=g_conv, d_pa=d_pa,
                d_pb=d_pb, d_pc=d_pc, g_mem_norm=g_mem_norm, g_w_kv=g_w_kv, g_w_out=g_w_out,
                landed=landed_late + landed_early)


_SEGMENTS = ((0, O_GT, 0), (O_GT, O_QB, P_GT), (O_QB, O_KB, P_QB), (O_KB, O_VB, P_KB), (O_VB, O_ZB, P_VB),
             (O_ZB, O_QC, P_ZB), (O_QC, O_ZC, P_QC), (O_ZC, IN_WIDTH, P_ZC))


def _permute_blocks(w4):
    parts = []
    for first, end, _ in sorted(_SEGMENTS, key=lambda seg: seg[2]):
        row = first
        while row < end:
            k = row // W_IN_BLOCK
            stop = min(end, (k + 1) * W_IN_BLOCK)
            parts.append(w4[k][row - k * W_IN_BLOCK:stop - k * W_IN_BLOCK, :])
            row = stop
    parts.append(jnp.zeros((P_WIDTH - IN_WIDTH, w4.shape[2]), w4.dtype))
    return jnp.concatenate(parts, axis=0)


def _cotangent_blocks(d_qkv, d_za, d_gt, d_qb, d_kb, d_vb, d_zb, d_qc, d_zc):
    s = d_qkv.shape[0]
    tr = min(256, s)
    pieces = (d_qkv, d_za, d_gt, d_qb, d_kb, d_vb, d_zb, d_qc, d_zc)

    def body(*refs):
        o_ref = refs[-1]
        tiles = [r[...].astype(BF16) for r in refs[:-1]]
        tiles[2] = tiles[2][:, :O_QB - O_GT]
        orig = jnp.concatenate(tiles, axis=1)
        pad = jnp.zeros((tr, W_IN_PAD - W_IN_BLOCK), BF16)
        parts = []
        for k in range(N_CHIPS):
            parts += [orig[:, k * W_IN_BLOCK:(k + 1) * W_IN_BLOCK], pad]
        o_ref[...] = jnp.concatenate(parts, axis=1)

    return pl.pallas_call(
        body, name="cotangent_blocks", grid=(s // tr,),
        in_specs=[pl.BlockSpec((tr, p.shape[1]), lambda i: (i, 0)) for p in pieces],
        out_specs=pl.BlockSpec((tr, N_CHIPS * W_IN_PAD), lambda i: (i, 0)),
        out_shape=jax.ShapeDtypeStruct((s, N_CHIPS * W_IN_PAD), BF16), compiler_params=_params(("parallel",)),
    )(*pieces)


HBM = pl.BlockSpec(memory_space=pltpu.HBM)


def _place():
    x, y, c = lax.axis_index("x"), lax.axis_index("y"), lax.axis_index("c")
    chips = [(1 - x, y), (x, 1 - y), (1 - x, 1 - y)]
    return x, y, c, 2 * x + y, chips, [2 * cx + cy for cx, cy in chips]


PIECE_ROWS_CAP = 600


def _remote(src, dst, send_sems, recv_sems, k, to):
    return pltpu.make_async_remote_copy(src_ref=src, dst_ref=dst, send_sem=send_sems.at[k], recv_sem=recv_sems.at[k],
                                        device_id=to, device_id_type=MESH)


def _half_cols(ref, c):
    half = ref.shape[-1] // 2
    return pl.ds(pl.multiple_of(c * half, LANE), half)


class _PairedGather:
    def __init__(self, blocks):
        n = len(blocks)
        self.operands = list(blocks)
        self.out_shapes = [jax.ShapeDtypeStruct((N_CHIPS,) + b.shape, b.dtype) for b in blocks]
        self.scratch_shapes = [pltpu.SemaphoreType.DMA((6 * n,)), pltpu.SemaphoreType.DMA((6 * n,))]

    @staticmethod
    def _copies(srcs, dsts, sems):
        x, y, c, me, chips, chip_ids = _place()
        sends, landed, passes, passed = [], [], [], []
        for a, (src, dst) in enumerate(zip(srcs, dsts)):
            mine, other = _half_cols(src, c), _half_cols(src, 1 - c)
            for j, (chip, cid) in enumerate(zip(chips, chip_ids)):
                sends.append(_remote(src.at[:, mine], dst.at[me, :, mine], sems[0], sems[1], 6 * a + j, (*chip, c)))
                here = dst.at[cid, :, mine]
                landed.append(_remote(here, here, sems[0], sems[1], 6 * a + j, (x, y, 1 - c)))
                passes.append(_remote(here, here, sems[0], sems[1], 6 * a + 3 + j, (x, y, 1 - c)))
                there = dst.at[cid, :, other]
                passed.append(_remote(there, there, sems[0], sems[1], 6 * a + 3 + j, (x, y, 1 - c)))
        return sends, landed, passes, passed

    def start(self, srcs, dsts, sems):
        for cp in self._copies(srcs, dsts, sems)[0]:
            cp.start()

    def middle(self, srcs, dsts, sems):
        _, landed, passes, _ = self._copies(srcs, dsts, sems)
        for arrived, onward in zip(landed, passes):
            arrived.wait_recv()
            onward.start()

    def finish(self, srcs, dsts, sems):
        sends, _, passes, passed = self._copies(srcs, dsts, sems)
        for cp in passed:
            cp.wait_recv()
        for cp in sends + passes:
            cp.wait_send()


def _all_gather_weights(bigs, conv_b):
    bigs = tuple(bigs)
    n_big = len(bigs)

    def body(*refs):
        srcs, conv_src = refs[:n_big], refs[n_big]
        dsts, conv_dst = refs[n_big + 1:2 * n_big + 1], refs[2 * n_big + 1]
        send_sems, recv_sems, local_sems = refs[2 * n_big + 2:]
        x, y, c, me, chips, chip_ids = _place()
        sibling = (x, y, 1 - c)
        local = [pltpu.make_async_copy(src, dst.at[me], local_sems.at[a]) for a, (src, dst) in enumerate(zip(srcs, dsts))]
        local.append(pltpu.make_async_copy(conv_src, conv_dst.at[me], local_sems.at[n_big]))
        for cp in local:
            cp.start()
        sends = []
        for a, (src, dst) in enumerate(zip(srcs, dsts)):
            mine = _half_cols(src, c)
            for j, chip in enumerate(chips):
                sends.append(_remote(src.at[:, mine], dst.at[me, :, mine], send_sems, recv_sems, 6 * a + j, (*chip, c)))
        for j, chip in enumerate(chips):
            sends.append(_remote(conv_src, conv_dst.at[me], send_sems, recv_sems, 6 * n_big + j, (*chip, c)))
        for cp in sends:
            cp.start()
        passed = []
        for a, (src, dst) in enumerate(zip(srcs, dsts)):
            mine = _half_cols(src, c)
            for j, cid in enumerate(chip_ids):
                landed = dst.at[cid, :, mine]
                _remote(landed, landed, send_sems, recv_sems, 6 * a + j, sibling).wait_recv()
                cp = _remote(landed, landed, send_sems, recv_sems, 6 * a + 3 + j, sibling)
                cp.start()
                passed.append(cp)
        for a, (src, dst) in enumerate(zip(srcs, dsts)):
            other = _half_cols(src, 1 - c)
            for j, cid in enumerate(chip_ids):
                landed = dst.at[cid, :, other]
                _remote(landed, landed, send_sems, recv_sems, 6 * a + 3 + j, sibling).wait_recv()
        for j, cid in enumerate(chip_ids):
            _remote(conv_src, conv_dst.at[cid], send_sems, recv_sems, 6 * n_big + j, sibling).wait_recv()
        for cp in sends + passed:
            cp.wait_send()
        for cp in local:
            cp.wait()

    n_sem = 6 * n_big + 3
    return pl.pallas_call(
        body, name="all_gather_weights",
        out_shape=[jax.ShapeDtypeStruct((N_CHIPS,) + w.shape, w.dtype) for w in bigs + (conv_b,)],
        in_specs=[pl.BlockSpec(memory_space=pltpu.VMEM)] * (n_big + 1), out_specs=[HBM] * (n_big + 1),
        scratch_shapes=[pltpu.SemaphoreType.DMA((n_sem,)), pltpu.SemaphoreType.DMA((n_sem,)),
                        pltpu.SemaphoreType.DMA((n_big + 1,))],
        compiler_params=_params(),
    )(*bigs, conv_b)


def _pair_exchange(grads, name):
    n = len(grads)
    pieces = [_row_tile(g.shape[1]) for g in grads]

    def body(*refs):
        srcs, gots = refs[:n], refs[n:2 * n]
        stages, narrow = refs[2 * n:3 * n], refs[3 * n:4 * n]
        send_sems, recv_sems, load_sems = refs[4 * n:]
        x, y, c, _, _, _ = _place()
        sibling = (x, y, 1 - c)
        for a in range(n):
            slabs, rows, _ = gots[a].shape
            piece = pieces[a]
            per_slab = rows // piece
            theirs = _half_cols(srcs[a], 1 - c)
            loads, sends = [], []
            for i in range(slabs * per_slab):
                k, r, slot = i // per_slab, i % per_slab, i % 2
                part = pl.ds(r * piece, piece)
                loads.append(pltpu.make_async_copy(srcs[a].at[k, part, theirs], stages[a].at[slot], load_sems.at[2 * a + slot]))
                sends.append(pltpu.make_async_remote_copy(
                    src_ref=narrow[a].at[slot], dst_ref=gots[a].at[k, part, :],
                    send_sem=send_sems.at[2 * a + slot], recv_sem=recv_sems.at[a], device_id=sibling, device_id_type=MESH))
            loads[0].start()
            for i in range(len(loads)):
                loads[i].wait()
                narrow[a][i % 2] = stages[a][i % 2].astype(BF16)
                sends[i].start()
                if i + 1 < len(loads):
                    if i >= 1:
                        sends[i - 1].wait_send()
                    loads[i + 1].start()
            for cp in sends[-2:]:
                cp.wait_send()
        for a in range(n):
            pltpu.make_async_remote_copy(src_ref=gots[a], dst_ref=gots[a], send_sem=send_sems.at[2 * a],
                                         recv_sem=recv_sems.at[a], device_id=sibling, device_id_type=MESH).wait_recv()

    halves = [jax.ShapeDtypeStruct((g.shape[0], g.shape[1], g.shape[2] // 2), BF16) for g in grads]
    return pl.pallas_call(
        body, name=name, out_shape=halves, in_specs=[HBM] * n, out_specs=[HBM] * n,
        scratch_shapes=[pltpu.VMEM((2, piece, g.shape[2] // 2), dt) for dt in (F32, BF16) for piece, g in zip(pieces, grads)]
        + [pltpu.SemaphoreType.DMA((2 * n,)), pltpu.SemaphoreType.DMA((n,)), pltpu.SemaphoreType.DMA((2 * n,))],
        compiler_params=_params(),
    )(*grads)


class _ChipExchange:
    def __init__(self, halves):
        n = len(halves)
        self.operands = list(halves)
        self.out_shapes = [jax.ShapeDtypeStruct((N_CHIPS - 1,) + h.shape[1:], h.dtype) for h in halves]
        self.scratch_shapes = [pltpu.SemaphoreType.DMA((3 * n,)), pltpu.SemaphoreType.DMA((3 * n,))]

    @staticmethod
    def _copies(srcs, lands, sems):
        _, _, c, _, chips, chip_ids = _place()
        return [_remote(src.at[cid], land.at[j], sems[0], sems[1], 3 * a + j, (*chip, c))
                for a, (src, land) in enumerate(zip(srcs, lands)) for j, (chip, cid) in enumerate(zip(chips, chip_ids))]

    def start(self, srcs, lands, sems):
        for cp in self._copies(srcs, lands, sems):
            cp.start()

    def finish(self, srcs, lands, sems):
        copies = self._copies(srcs, lands, sems)
        for cp in copies:
            cp.wait_recv()
        for cp in copies:
            cp.wait_send()


def _pair_gather(halves, rows):
    n = len(halves)

    def body(*refs):
        srcs, fulls = refs[:n], refs[n:2 * n]
        send_sems, recv_sems, local_sems = refs[2 * n:]
        x, y, c, _, _, _ = _place()
        copies = []
        for a in range(n):
            mine, src = _half_cols(fulls[a], c), srcs[a].at[pl.ds(0, rows[a]), :]
            keep = pltpu.make_async_copy(src, fulls[a].at[:, mine], local_sems.at[a])
            keep.start()
            give = _remote(src, fulls[a].at[:, mine], send_sems, recv_sems, a, (x, y, 1 - c))
            give.start()
            copies += [keep, give]
        for a in range(n):
            other, src = _half_cols(fulls[a], 1 - c), srcs[a].at[pl.ds(0, rows[a]), :]
            copies[2 * a].wait()
            copies[2 * a + 1].wait_send()
            _remote(src, fulls[a].at[:, other], send_sems, recv_sems, a, (x, y, 1 - c)).wait_recv()

    return pl.pallas_call(
        body, name="grad_pair_gather",
        out_shape=[jax.ShapeDtypeStruct((r, 2 * h.shape[1]), h.dtype) for r, h in zip(rows, halves)],
        in_specs=[pl.BlockSpec(memory_space=pltpu.VMEM)] * n, out_specs=[HBM] * n,
        scratch_shapes=[pltpu.SemaphoreType.DMA((n,)), pltpu.SemaphoreType.DMA((n,)), pltpu.SemaphoreType.DMA((n,))],
    )(*halves)


def _all_reduce_small(p):
    n_dev = 8

    def body(p_ref, o_ref, land, send_sems, recv_sems):
        x, y, c = lax.axis_index("x"), lax.axis_index("y"), lax.axis_index("c")
        me = 4 * x + 2 * y + c
        land[me] = p_ref[...]
        sends = []
        for k in range(1, n_dev):
            fx, fy, fc = (k >> 2) & 1, (k >> 1) & 1, k & 1
            to = (x ^ fx, y ^ fy, c ^ fc)
            cp = _remote(p_ref, land.at[me], send_sems, recv_sems, k - 1, to)
            cp.start()
            sends.append(cp)
        for k in range(1, n_dev):
            _remote(p_ref, land.at[me ^ k], send_sems, recv_sems, k - 1, (x, y, c)).wait_recv()
        total = land[0]
        for d in range(1, n_dev):
            total = total + land[d]
        o_ref[...] = total
        for cp in sends:
            cp.wait_send()

    vm = pl.BlockSpec(memory_space=pltpu.VMEM)
    return pl.pallas_call(
        body, name="all_reduce_small", out_shape=jax.ShapeDtypeStruct(p.shape, p.dtype), in_specs=[vm], out_specs=vm,
        scratch_shapes=[pltpu.VMEM((n_dev,) + p.shape, p.dtype), pltpu.SemaphoreType.DMA((n_dev - 1,)),
                        pltpu.SemaphoreType.DMA((n_dev - 1,))],
    )(p)


def _finish_reduction(halves, rows, p):
    n, n_dev = len(halves), 8

    def body(*refs):
        srcs, p_ref = refs[:n], refs[n]
        fulls, o_ref = refs[n + 1:2 * n + 1], refs[2 * n + 1]
        land, send_sems, recv_sems, local_sems = refs[2 * n + 2:]
        x, y, c = lax.axis_index("x"), lax.axis_index("y"), lax.axis_index("c")
        me = 4 * x + 2 * y + c
        copies = []
        for a in range(n):
            mine, src = _half_cols(fulls[a], c), srcs[a].at[pl.ds(0, rows[a]), :]
            keep = pltpu.make_async_copy(src, fulls[a].at[:, mine], local_sems.at[a])
            keep.start()
            give = _remote(src, fulls[a].at[:, mine], send_sems, recv_sems, a, (x, y, 1 - c))
            give.start()
            copies += [keep, give]
        land[me] = p_ref[...]
        sends = []
        for k in range(1, n_dev):
            to = (x ^ ((k >> 2) & 1), y ^ ((k >> 1) & 1), c ^ (k & 1))
            cp = _remote(p_ref, land.at[me], send_sems, recv_sems, n + k - 1, to)
            cp.start()
            sends.append(cp)
        for k in range(1, n_dev):
            _remote(p_ref, land.at[me ^ k], send_sems, recv_sems, n + k - 1, (x, y, c)).wait_recv()
        total = land[0]
        for d in range(1, n_dev):
            total = total + land[d]
        o_ref[...] = total
        for a in range(n):
            other, src = _half_cols(fulls[a], 1 - c), srcs[a].at[pl.ds(0, rows[a]), :]
            copies[2 * a].wait()
            copies[2 * a + 1].wait_send()
            _remote(src, fulls[a].at[:, other], send_sems, recv_sems, a, (x, y, 1 - c)).wait_recv()
        for cp in sends:
            cp.wait_send()

    vm = pl.BlockSpec(memory_space=pltpu.VMEM)
    n_sem = n + n_dev - 1
    out = pl.pallas_call(
        body, name="finish_reduction",
        out_shape=[jax.ShapeDtypeStruct((r, 2 * h.shape[1]), h.dtype) for r, h in zip(rows, halves)]
        + [jax.ShapeDtypeStruct(p.shape, p.dtype)],
        in_specs=[vm] * (n + 1), out_specs=[HBM] * n + [vm],
        scratch_shapes=[pltpu.VMEM((n_dev,) + p.shape, p.dtype), pltpu.SemaphoreType.DMA((n_sem,)),
                        pltpu.SemaphoreType.DMA((n_sem,)), pltpu.SemaphoreType.DMA((n,))],
        compiler_params=_params(),
    )(*halves, p)
    return out[:n], out[n]


def _row_tile(rows):
    fits = [t for t in range(8, min(rows, PIECE_ROWS_CAP) + 1, 8) if rows % t == 0]
    return max(fits) if fits else rows


def _pair_sum(full, got, core, name):
    n, r, c = got.shape
    tr = _row_tile(r)

    def body(core_ref, a_ref, b_ref, o_ref):
        o_ref[...] = (a_ref[...] + b_ref[...].astype(F32)).astype(BF16)

    blk = pl.BlockSpec((None, tr, c), lambda i, j, core_ref: (i, j, 0))
    grid_spec = pltpu.PrefetchScalarGridSpec(
        num_scalar_prefetch=1, grid=(n, r // tr),
        in_specs=[pl.BlockSpec((None, tr, c), lambda i, j, core_ref: (i, j, core_ref[0])), blk], out_specs=blk)
    return pl.pallas_call(body, name=name, grid_spec=grid_spec, out_shape=jax.ShapeDtypeStruct(got.shape, BF16),
                          compiler_params=_params(("parallel", "parallel")))(core, full, got)


def _chip_sum(full, got, land, place, name):
    n, r, c = land.shape
    tr = _row_tile(r)

    def body(place_ref, a_ref, b_ref, l_ref, o_ref):
        total = a_ref[...] + b_ref[...].astype(F32)
        for j in range(n):
            total = total + l_ref[j].astype(F32)
        o_ref[...] = total

    grid_spec = pltpu.PrefetchScalarGridSpec(
        num_scalar_prefetch=1, grid=(r // tr,),
        in_specs=[pl.BlockSpec((None, tr, c), lambda i, p: (p[0], i, p[1])),
                  pl.BlockSpec((None, tr, c), lambda i, p: (p[0], i, 0)),
                  pl.BlockSpec((n, tr, c), lambda i, p: (0, i, 0))],
        out_specs=pl.BlockSpec((tr, c), lambda i, p: (i, 0)))
    return pl.pallas_call(body, name=name, grid_spec=grid_spec, out_shape=jax.ShapeDtypeStruct((r, c), F32),
                          compiler_params=_params(("parallel",)))(place, full, got, land)


def _adamw(w, g, m, v, name, echo=False):
    r, c = w.shape
    tr = _row_tile(r)
    tc = 1024 if c % 1024 == 0 else c

    def body(w_ref, g_ref, m_ref, v_ref, d_ref, mo_ref, vo_ref, *g_out):
        g_ = g_ref[...]
        for o in g_out:
            o[...] = g_
        m2 = ADAM_B1 * m_ref[...] + (1.0 - ADAM_B1) * g_
        v2 = ADAM_B2 * v_ref[...] + (1.0 - ADAM_B2) * jnp.square(g_)
        m_hat = m2 / (1.0 - ADAM_B1 ** ADAM_STEP)
        v_hat = v2 / (1.0 - ADAM_B2 ** ADAM_STEP)
        d_ref[...] = -ADAM_LR * (m_hat / (jnp.sqrt(v_hat) + ADAM_EPS) + ADAM_WD * w_ref[...])
        mo_ref[...] = m2
        vo_ref[...] = v2

    blk = pl.BlockSpec((tr, tc), lambda i, j: (i, j))
    n_out = 4 if echo else 3
    return pl.pallas_call(body, name=name, grid=(r // tr, c // tc), in_specs=[blk] * 4, out_specs=[blk] * n_out,
                          out_shape=[jax.ShapeDtypeStruct(w.shape, F32)] * n_out,
                          compiler_params=_params(("parallel", "parallel")))(w, g, m, v)


SMALL_NAMES = ("norm_w", "mem_norm_w", "o_norm_a", "q_norm_c", "k_norm_c", "q_norm_b", "k_norm_b",
               "a_log_fwd", "a_log_bwd", "dt_bias_fwd", "dt_bias_bwd", "sink_b")
SMALL_SIZES = (2048, 2048, 128, 128, 128, 64, 64, 8, 8, 8, 8, 8)
SMALL_LOSS = sum(SMALL_SIZES)
SMALL_CONV = 5120
SMALL_TOTAL = SMALL_CONV + CONV_K * 3 * A_WIDTH
SMALL_ROWS = SMALL_TOTAL // LANE


def _pack_small(parts, extra=None, conv=None):
    vec = [parts[n].reshape(-1) for n in SMALL_NAMES]
    vec.append(jnp.zeros((1,), F32) if extra is None else extra.reshape(1))
    vec.append(jnp.zeros((SMALL_CONV - SMALL_LOSS - 1,), F32))
    vec.append(jnp.zeros((SMALL_TOTAL - SMALL_CONV,), F32) if conv is None else conv.reshape(-1))
    return jnp.concatenate(vec).reshape(SMALL_ROWS, LANE)


def _unpack_small(packed):
    flat = packed.reshape(-1)
    out, off = {}, 0
    for n, size in zip(SMALL_NAMES, SMALL_SIZES):
        out[n] = flat[off:off + size].reshape(1, size)
        off += size
    return out


WEIGHT_ORDER = ("norm_w", "w_in", "conv_w_a", "a_log_fwd", "a_log_bwd", "dt_bias_fwd", "dt_bias_bwd", "o_norm_a",
                "q_norm_b", "k_norm_b", "sink_b", "mem_norm_w", "w_mem_kv", "q_norm_c", "k_norm_c", "w_out")


def kernel(x, mem, norm_w, w_in, conv_w_a, a_log_fwd, a_log_bwd, dt_bias_fwd, dt_bias_bwd, o_norm_a, q_norm_b, k_norm_b, sink_b, mem_norm_w, w_mem_kv, q_norm_c, k_norm_c, w_out, loss_target, m_norm_w, m_w_in, m_conv_w_a, m_a_log_fwd, m_a_log_bwd, m_dt_bias_fwd, m_dt_bias_bwd, m_o_norm_a, m_q_norm_b, m_k_norm_b, m_sink_b, m_mem_norm_w, m_w_mem_kv, m_q_norm_c, m_k_norm_c, m_w_out, v_norm_w, v_w_in, v_conv_w_a, v_a_log_fwd, v_a_log_bwd, v_dt_bias_fwd, v_dt_bias_bwd, v_o_norm_a, v_q_norm_b, v_k_norm_b, v_sink_b, v_mem_norm_w, v_w_mem_kv, v_q_norm_c, v_k_norm_c, v_w_out):
    weights = dict(norm_w=norm_w, w_in=w_in, conv_w_a=conv_w_a, a_log_fwd=a_log_fwd, a_log_bwd=a_log_bwd,
                   dt_bias_fwd=dt_bias_fwd, dt_bias_bwd=dt_bias_bwd, o_norm_a=o_norm_a, q_norm_b=q_norm_b,
                   k_norm_b=k_norm_b, sink_b=sink_b, mem_norm_w=mem_norm_w, w_mem_kv=w_mem_kv, q_norm_c=q_norm_c,
                   k_norm_c=k_norm_c, w_out=w_out)
    mom1 = dict(norm_w=m_norm_w, w_in=m_w_in, conv_w_a=m_conv_w_a, a_log_fwd=m_a_log_fwd, a_log_bwd=m_a_log_bwd,
                dt_bias_fwd=m_dt_bias_fwd, dt_bias_bwd=m_dt_bias_bwd, o_norm_a=m_o_norm_a, q_norm_b=m_q_norm_b,
                k_norm_b=m_k_norm_b, sink_b=m_sink_b, mem_norm_w=m_mem_norm_w, w_mem_kv=m_w_mem_kv,
                q_norm_c=m_q_norm_c, k_norm_c=m_k_norm_c, w_out=m_w_out)
    mom2 = dict(norm_w=v_norm_w, w_in=v_w_in, conv_w_a=v_conv_w_a, a_log_fwd=v_a_log_fwd, a_log_bwd=v_a_log_bwd,
                dt_bias_fwd=v_dt_bias_fwd, dt_bias_bwd=v_dt_bias_bwd, o_norm_a=v_o_norm_a, q_norm_b=v_q_norm_b,
                k_norm_b=v_k_norm_b, sink_b=v_sink_b, mem_norm_w=v_mem_norm_w, w_mem_kv=v_w_mem_kv,
                q_norm_c=v_q_norm_c, k_norm_c=v_k_norm_c, w_out=v_w_out)
    chip = 2 * lax.axis_index("x") + lax.axis_index("y")

    own_in = jnp.pad(jnp.transpose(w_in[0]).astype(BF16), ((0, W_IN_PAD - W_IN_BLOCK), (0, 0)))
    w_in4, conv4 = _all_gather_weights([own_in], conv_w_a[0])
    w_perm_t = _permute_blocks(w_in4)
    w_blocks_t = w_in4.reshape(N_CHIPS * W_IN_PAD, D_MODEL)
    conv_full = jnp.transpose(conv4, (1, 0, 2)).reshape(CONV_K, 3 * A_WIDTH)
    own_out, own_kv = w_out[0].astype(BF16), w_mem_kv[0].astype(BF16)

    def assemble(w_out4, w_kv4):
        w_out4 = lax.dynamic_update_index_in_dim(w_out4, own_out, chip, 0)
        w_kv4 = lax.dynamic_update_index_in_dim(w_kv4, own_kv, chip, 0)
        return w_out4.reshape(D_MODEL, D_MODEL), w_kv4.reshape(D_MODEL, 2 * C_HEADS * C_DIM)

    gather = (_PairedGather([own_out, own_kv]), assemble)
    pa = jnp.concatenate([_pad_row(a_log_fwd), _pad_row(a_log_bwd), _pad_row(dt_bias_fwd), _pad_row(dt_bias_bwd),
                          _pad_row(o_norm_a), jnp.zeros((3, LANE), F32)], axis=0)
    pb = jnp.concatenate([_pad_row(q_norm_b), _pad_row(k_norm_b), _pad_row(sink_b), jnp.zeros((5, LANE), F32)], axis=0)
    pc = jnp.concatenate([_pad_row(q_norm_c), _pad_row(k_norm_c), jnp.zeros((6, LANE), F32)], axis=0)

    full, got = {}, {}
    core = lax.axis_index("c").astype(jnp.int32).reshape(1)

    def pair_round(tag, blocks):
        names = [tag + "_%d" % i for i in range(len(blocks))]
        full.update(zip(names, blocks))
        got.update(zip(names, _pair_exchange(blocks, "grad_pair_exchange_" + tag)))
        return _ChipExchange([_pair_sum(full[n], got[n], core, "grad_pair_sum_" + n) for n in names])

    def early(g_w_out, g_w_kv):
        return pair_round("early", [g_w_out.reshape(N_CHIPS, D_MODEL // N_CHIPS, D_MODEL),
                                    g_w_kv.reshape(N_CHIPS, D_MODEL // N_CHIPS, 2 * C_HEADS * C_DIM)])

    def late(g_w_blocks_t):
        return pair_round("late", [g_w_blocks_t.reshape(N_CHIPS, W_IN_PAD, D_MODEL)])

    r = _local_step(x[0], mem[0], loss_target[0], norm_w, w_perm_t, w_blocks_t, conv_full, pa, pb, pc, mem_norm_w, None, None,
                    gather, (early, late))
    place = jnp.stack([chip, lax.axis_index("c")]).astype(jnp.int32)
    reduced = [_chip_sum(full[n], got[n], l, place, "grad_chip_sum_" + n)
               for n, l in zip(("late_0", "early_0", "early_1"), r["landed"])]
    d_pa, d_pb, d_pc = r["d_pa"], r["d_pb"], r["d_pc"]
    small_g = dict(norm_w=r["g_norm"], mem_norm_w=r["g_mem_norm"], o_norm_a=d_pa[4], q_norm_c=d_pc[0], k_norm_c=d_pc[1],
                   q_norm_b=d_pb[0, :B_DIM], k_norm_b=d_pb[1, :B_DIM], a_log_fwd=d_pa[0, :A_HEADS],
                   a_log_bwd=d_pa[1, :A_HEADS], dt_bias_fwd=d_pa[2, :A_HEADS], dt_bias_bwd=d_pa[3, :A_HEADS],
                   sink_b=d_pb[2, :B_HEADS])
    (g_w_in_t, g_w_out, g_w_kv), packed = _finish_reduction(
        reduced, [W_IN_BLOCK, D_MODEL // N_CHIPS, D_MODEL // N_CHIPS],
        _pack_small(small_g, jnp.sum(r["loss_parts"][:, 0, 0]), r["g_conv"]))
    flat = packed.reshape(-1)
    loss = flat[SMALL_LOSS]
    conv_sum = flat[SMALL_CONV:].reshape(CONV_K, 3 * A_WIDTH)
    conv_cols = 3 * A_WIDTH // N_CHIPS
    g_conv = lax.dynamic_slice(conv_sum, (0, chip * conv_cols), (CONV_K, conv_cols))

    grads = _unpack_small(packed)
    grads["conv_w_a"] = g_conv
    delta, new_m, new_v = {}, {}, {}
    delta["conv_w_a"], new_m["conv_w_a"], new_v["conv_w_a"] = _adamw(conv_w_a[0], g_conv, m_conv_w_a[0], v_conv_w_a[0],
                                                                     "adamw_conv_w_a")
    for n, g in (("w_mem_kv", g_w_kv), ("w_out", g_w_out)):
        delta[n], new_m[n], new_v[n], grads[n] = _adamw(weights[n][0], g, mom1[n][0], mom2[n][0], "adamw_" + n, echo=True)
    stepped = _adamw(jnp.transpose(w_in[0]), g_w_in_t, jnp.transpose(m_w_in[0]), jnp.transpose(v_w_in[0]), "adamw_w_in",
                     echo=True)
    delta["w_in"], new_m["w_in"], new_v["w_in"], grads["w_in"] = (jnp.transpose(t) for t in stepped)
    d_s, m_s, v_s = _adamw(_pack_small(weights), packed, _pack_small(mom1), _pack_small(mom2), "adamw_small")
    d_s, m_s, v_s = _unpack_small(d_s), _unpack_small(m_s), _unpack_small(v_s)
    for n in SMALL_NAMES:
        delta[n], new_m[n], new_v[n] = d_s[n], m_s[n], v_s[n]

    def shaped(tree):
        return [tree[n].reshape(weights[n].shape) for n in WEIGHT_ORDER]

    return (loss, r["g_x"].reshape(x.shape), *shaped(grads), *shaped(delta), *shaped(new_m), *shaped(new_v))
```
